```python
import jax
import jax.numpy as jnp
from jax import lax
import numpy as np

D_MODEL = 1024
BATCH = 16
SEQ = 2048
DEPTH = 2

D_MIX = D_MODEL
N_GROUPS = 4
GROUP_WIDTH = D_MIX // N_GROUPS
A_HEADS = 4
A_DK = GROUP_WIDTH // A_HEADS
A_DV = GROUP_WIDTH // A_HEADS
A_CHUNK = 16
LB_FLOOR = 1e-30
B_HEADS = 4
B_Q_LORA = 256
B_KV_LORA = 128
B_NOPE = 64
B_ROPE = 32
B_V = GROUP_WIDTH // B_HEADS
ROPE_THETA = 10000.0
C_HEADS = 4
C_HEAD_DIM = GROUP_WIDTH // C_HEADS
FOX_GATE_BIAS = 3.0
D_GROUPS = 4
D_GROUP_DIM = GROUP_WIDTH // D_GROUPS
D_CHUNK = 128
Q_BLOCK = 128
D_FF = 2816
N_MOD = 9
ALPHA = (2 * DEPTH) ** 0.25
BETA = (8 * DEPTH) ** -0.25
LN_EPS = 1e-5
RMS_EPS = 1e-6
MIX_SPLIT_SIZES = (GROUP_WIDTH, GROUP_WIDTH, GROUP_WIDTH, GROUP_WIDTH,
                   B_Q_LORA, B_KV_LORA, B_ROPE,
                   GROUP_WIDTH, GROUP_WIDTH, GROUP_WIDTH, C_HEADS,
                   GROUP_WIDTH, GROUP_WIDTH)
MIX_IN_COLS = sum(MIX_SPLIT_SIZES)

kernel_name = "hybrid_hgrn2_mla_fox_gmlp_deepnorm_block"


def layer_norm(x, g, b):
    xf = x.astype(jnp.float32)
    mu = jnp.mean(xf, axis=-1, keepdims=True)
    var = jnp.mean(jnp.square(xf - mu), axis=-1, keepdims=True)
    return ((xf - mu) * lax.rsqrt(var + LN_EPS)).astype(x.dtype) * g + b


def rms_norm(x, g):
    xf = x.astype(jnp.float32)
    return (xf * lax.rsqrt(jnp.mean(xf * xf, axis=-1, keepdims=True) + RMS_EPS)).astype(x.dtype) * g


def swiglu_ffn(h, w_in, w_out):
    gate, up = jnp.split(h @ w_in, 2, axis=-1)
    return (jax.nn.silu(gate) * up) @ w_out


def rope(x, pos):
    half = x.shape[-1] // 2
    inv_freq = ROPE_THETA ** (-jnp.arange(half, dtype=jnp.float32) / half)
    ang = pos.astype(jnp.float32)[:, None] * inv_freq[None, :]
    cos = jnp.cos(ang)[None, :, None, :].astype(x.dtype)
    sin = jnp.sin(ang)[None, :, None, :].astype(x.dtype)
    x1, x2 = x[..., :half], x[..., half:]
    return jnp.concatenate([x1 * cos - x2 * sin, x1 * sin + x2 * cos], axis=-1)


def causal_softmax_attention(q, k, v, cum_log_f=None):
    b, s, h, dk = q.shape
    dv = v.shape[-1]
    n_blocks = s // Q_BLOCK
    scale = dk ** -0.5
    k_pos = jnp.arange(s)
    cum_t = None if cum_log_f is None else jnp.swapaxes(cum_log_f, 1, 2)

    def one_block(i):
        start = i * Q_BLOCK
        q_i = lax.dynamic_slice_in_dim(q, start, Q_BLOCK, axis=1)
        logits = jnp.einsum('bqhd,bkhd->bhqk', q_i, k,
                            preferred_element_type=jnp.float32) * scale
        if cum_t is not None:
            f_i = lax.dynamic_slice_in_dim(cum_t, start, Q_BLOCK, axis=2)
            logits = logits + (f_i[..., :, None] - cum_t[..., None, :])
        q_pos = start + jnp.arange(Q_BLOCK)
        logits = jnp.where(k_pos[None, :] <= q_pos[:, None], logits, -jnp.inf)
        p = jax.nn.softmax(logits, axis=-1).astype(v.dtype)
        return jnp.einsum('bhqk,bkhd->bqhd', p, v)

    out = lax.map(one_block, jnp.arange(n_blocks))
    return jnp.moveaxis(out, 0, 1).reshape(b, s, h * dv)


def hgrn2_mixer(q, f_logit, inp, g_out, lb, norm_g):
    b, s, _ = q.shape
    dt = q.dtype
    n_chunks = s // A_CHUNK
    lbf = lb.astype(jnp.float32)
    log_f = jnp.logaddexp(jnp.log(jnp.maximum(lbf, LB_FLOOR)),
                          jnp.log1p(-lbf) + jax.nn.log_sigmoid(f_logit.astype(jnp.float32)))
    k = -jnp.expm1(log_f)
    qf = jax.nn.silu(q.astype(jnp.float32))
    shp_k = (b, n_chunks, A_CHUNK, A_HEADS, A_DK)
    qf, k, log_f = qf.reshape(shp_k), k.reshape(shp_k), log_f.reshape(shp_k)
    v = inp.astype(jnp.float32).reshape(b, n_chunks, A_CHUNK, A_HEADS, A_DV)
    G = jnp.cumsum(log_f, axis=2)
    G_last = G[:, :, -1:]
    causal = jnp.tril(jnp.ones((A_CHUNK, A_CHUNK), dtype=bool))[None, None, :, :, None, None]
    rel = jnp.where(causal, G[:, :, :, None] - G[:, :, None, :], -jnp.inf)
    decay = jnp.exp(rel)
    scores = jnp.einsum('bctha,bcsha,bctsha->bchts', qf, k, decay)
    o_intra = jnp.einsum('bchts,bcshv->bcthv', scores, v)
    q_dec = qf * jnp.exp(G)
    k_to_end = k * jnp.exp(G_last - G)
    chunk_kv = jnp.einsum('bcsha,bcshv->cbhav', k_to_end, v)
    chunk_decay = jnp.transpose(jnp.exp(G_last[:, :, 0]), (1, 0, 2, 3))

    def step(state, xs):
        dec, kv = xs
        return dec[..., None] * state + kv, state

    state0 = jnp.zeros((b, A_HEADS, A_DK, A_DV), jnp.float32)
    _, state_in = lax.scan(step, state0, (chunk_decay, chunk_kv))
    o_inter = jnp.einsum('bctha,cbhav->bcthv', q_dec, state_in)
    o = (o_intra + o_inter).reshape(b, s, A_HEADS, A_DV)
    o = rms_norm(o, norm_g.astype(jnp.float32).reshape(A_HEADS, A_DV)).reshape(b, s, GROUP_WIDTH)
    return (o * jax.nn.silu(g_out.astype(jnp.float32))).astype(dt)


def mla_mixer(c_q, c_kv, k_rope, q_norm_g, kv_norm_g, w_uq, w_ukv, pos):
    b, s, _ = c_q.shape
    q = (rms_norm(c_q, q_norm_g) @ w_uq).reshape(b, s, B_HEADS, B_NOPE + B_ROPE)
    q_nope, q_rope = q[..., :B_NOPE], q[..., B_NOPE:]
    kv = (rms_norm(c_kv, kv_norm_g) @ w_ukv).reshape(b, s, B_HEADS, B_NOPE + B_V)
    k_nope, v = kv[..., :B_NOPE], kv[..., B_NOPE:]
    k_r = rope(k_rope[:, :, None, :], pos)
    q = jnp.concatenate([q_nope, rope(q_rope, pos)], axis=-1)
    k = jnp.concatenate([k_nope, jnp.broadcast_to(k_r, (b, s, B_HEADS, B_ROPE))], axis=-1)
    return causal_softmax_attention(q, k, v)


def fox_mixer(q, k, v, f_logit, b_f):
    b, s, _ = q.shape
    shp = (b, s, C_HEADS, C_HEAD_DIM)
    log_f = jax.nn.log_sigmoid(f_logit.astype(jnp.float32) + b_f.astype(jnp.float32))
    cum = jnp.cumsum(log_f, axis=1)
    return causal_softmax_attention(q.reshape(shp), k.reshape(shp), v.reshape(shp), cum)


def gmlp_mixer(u, v, ln_g, ln_b, w_s, b_s):
    b, s, _ = u.shape
    n_chunks = s // D_CHUNK
    u = jax.nn.gelu(u)
    v = layer_norm(jax.nn.gelu(v), ln_g, ln_b).reshape(b, n_chunks, D_CHUNK, D_GROUPS, D_GROUP_DIM)
    causal = jnp.tril(jnp.ones((D_CHUNK, D_CHUNK), dtype=bool))
    w = jnp.where(causal, w_s, 0.0)
    mixed = jnp.einsum('gts,bcsgd->bctgd', w, v) + jnp.swapaxes(b_s, 0, 1)[:, :, None]
    return u * mixed.reshape(b, s, GROUP_WIDTH)


def hybrid_token_mixer(h, w_in, w_out, lb, hgrn_norm_g, mla_q_norm_g, mla_kv_norm_g,
                       mla_w_uq, mla_w_ukv, fox_b_f, gmlp_ln_g, gmlp_ln_b, gmlp_w_s, gmlp_b_s):
    split_idx = [int(i) for i in np.cumsum(MIX_SPLIT_SIZES)[:-1]]
    proj = h @ w_in
    (a_q, a_f, a_i, a_g, b_cq, b_ckv, b_kr,
     c_q, c_k, c_v, c_f, d_u, d_v) = jnp.split(proj, split_idx, axis=-1)
    pos = jnp.arange(h.shape[1])
    o_a = hgrn2_mixer(a_q, a_f, a_i, a_g, lb, hgrn_norm_g)
    o_b = mla_mixer(b_cq, b_ckv, b_kr, mla_q_norm_g, mla_kv_norm_g, mla_w_uq, mla_w_ukv, pos)
    o_c = fox_mixer(c_q, c_k, c_v, c_f, fox_b_f)
    o_d = gmlp_mixer(d_u, d_v, gmlp_ln_g, gmlp_ln_b, gmlp_w_s, gmlp_b_s)
    return jnp.concatenate([o_a, o_b.astype(h.dtype), o_c.astype(h.dtype), o_d], axis=-1) @ w_out


def _fwd_setup_inputs(seed: int = 0) -> dict:
    key = jax.random.key(seed)
    ks = jax.random.split(key, 23)
    L = DEPTH

    def nrm(k, shape, scale):
        return scale * jax.random.normal(k, shape, jnp.float32)

    return {
        'x': nrm(ks[0], (BATCH, SEQ, D_MODEL), 1.0),
        'c': nrm(ks[1], (BATCH, D_MODEL), 1.0),
        'ada_w': nrm(ks[2], (L, D_MODEL, N_MOD * D_MODEL), 0.1 * D_MODEL ** -0.5),
        'ada_b': nrm(ks[3], (L, N_MOD * D_MODEL), 0.01),
        'ln_g': 1.0 + nrm(ks[4], (L, 3, D_MODEL), 0.02),
        'ln_b': nrm(ks[5], (L, 3, D_MODEL), 0.02),
        'ffn1_w_in': nrm(ks[6], (L, D_MODEL, 2 * D_FF), D_MODEL ** -0.5),
        'ffn1_w_out': nrm(ks[7], (L, D_FF, D_MODEL), BETA * D_FF ** -0.5),
        'ffn2_w_in': nrm(ks[8], (L, D_MODEL, 2 * D_FF), D_MODEL ** -0.5),
        'ffn2_w_out': nrm(ks[9], (L, D_FF, D_MODEL), BETA * D_FF ** -0.5),
        'mix_w_in': nrm(ks[10], (L, D_MODEL, MIX_IN_COLS), D_MODEL ** -0.5),
        'mix_w_out': nrm(ks[11], (L, D_MIX, D_MODEL), BETA * D_MIX ** -0.5),
        'hgrn_lb_logits': nrm(ks[12], (L, GROUP_WIDTH), 0.5),
        'hgrn_norm_g': 1.0 + nrm(ks[13], (L, GROUP_WIDTH), 0.02),
        'mla_q_norm_g': 1.0 + nrm(ks[14], (L, B_Q_LORA), 0.02),
        'mla_kv_norm_g': 1.0 + nrm(ks[15], (L, B_KV_LORA), 0.02),
        'mla_w_uq': nrm(ks[16], (L, B_Q_LORA, B_HEADS * (B_NOPE + B_ROPE)), B_Q_LORA ** -0.5),
        'mla_w_ukv': nrm(ks[17], (L, B_KV_LORA, B_HEADS * (B_NOPE + B_V)), B_KV_LORA ** -0.5),
        'fox_b_f': FOX_GATE_BIAS + nrm(ks[18], (L, C_HEADS), 0.5),
        'gmlp_ln_g': 1.0 + nrm(ks[19], (L, GROUP_WIDTH), 0.02),
        'gmlp_ln_b': nrm(ks[20], (L, GROUP_WIDTH), 0.02),
        'gmlp_w_s': nrm(ks[21], (L, D_GROUPS, D_CHUNK, D_CHUNK), 0.5 * D_CHUNK ** -0.5),
        'gmlp_b_s': 1.0 + nrm(ks[22], (L, D_GROUPS, D_CHUNK), 0.02),
    }


def _fwd_reference(x, c, ada_w, ada_b, ln_g, ln_b, ffn1_w_in, ffn1_w_out, ffn2_w_in, ffn2_w_out,
              mix_w_in, mix_w_out, hgrn_lb_logits, hgrn_norm_g, mla_q_norm_g, mla_kv_norm_g,
              mla_w_uq, mla_w_ukv, fox_b_f, gmlp_ln_g, gmlp_ln_b, gmlp_w_s, gmlp_b_s):
    lb_sm = jax.nn.softmax(hgrn_lb_logits.astype(jnp.float32), axis=0)
    lb_all = jnp.cumsum(lb_sm, axis=0) - lb_sm[0]
    c_act = jax.nn.silu(c)
    for l in range(DEPTH):
        mod = (c_act @ ada_w[l] + ada_b[l])[:, None, :]
        sh1, sc1, g1, sh2, sc2, g2, sh3, sc3, g3 = jnp.split(mod, N_MOD, axis=-1)
        h = x * (1.0 + sc1) + sh1
        x = layer_norm(ALPHA * x + 0.5 * (1.0 + g1) * swiglu_ffn(h, ffn1_w_in[l], ffn1_w_out[l]),
                       ln_g[l, 0], ln_b[l, 0])
        h = x * (1.0 + sc2) + sh2
        mixed = hybrid_token_mixer(h, mix_w_in[l], mix_w_out[l], lb_all[l], hgrn_norm_g[l],
                                   mla_q_norm_g[l], mla_kv_norm_g[l], mla_w_uq[l], mla_w_ukv[l],
                                   fox_b_f[l], gmlp_ln_g[l], gmlp_ln_b[l], gmlp_w_s[l], gmlp_b_s[l])
        x = layer_norm(ALPHA * x + (1.0 + g2) * mixed, ln_g[l, 1], ln_b[l, 1])
        h = x * (1.0 + sc3) + sh3
        x = layer_norm(ALPHA * x + 0.5 * (1.0 + g3) * swiglu_ffn(h, ffn2_w_in[l], ffn2_w_out[l]),
                       ln_g[l, 2], ln_b[l, 2])
    return x


import jax as _jax
import jax.numpy as _jnp

TWIN_FORMAT = 'train_step'
FWD_PARAMS = ['x', 'c', 'ada_w', 'ada_b', 'ln_g', 'ln_b', 'ffn1_w_in', 'ffn1_w_out', 'ffn2_w_in', 'ffn2_w_out', 'mix_w_in', 'mix_w_out', 'hgrn_lb_logits', 'hgrn_norm_g', 'mla_q_norm_g', 'mla_kv_norm_g', 'mla_w_uq', 'mla_w_ukv', 'fox_b_f', 'gmlp_ln_g', 'gmlp_ln_b', 'gmlp_w_s', 'gmlp_b_s']
TWIN_WEIGHTS = ['ada_w', 'ada_b', 'ln_g', 'ln_b', 'ffn1_w_in', 'ffn1_w_out', 'ffn2_w_in', 'ffn2_w_out', 'mix_w_in', 'mix_w_out', 'hgrn_lb_logits', 'hgrn_norm_g', 'mla_q_norm_g', 'mla_kv_norm_g', 'mla_w_uq', 'mla_w_ukv', 'fox_b_f', 'gmlp_ln_g', 'gmlp_ln_b', 'gmlp_w_s', 'gmlp_b_s']
TWIN_DIFF_INPUT = 'x'
TWIN_INPUTS = ['x', 'c', 'ada_w', 'ada_b', 'ln_g', 'ln_b', 'ffn1_w_in', 'ffn1_w_out', 'ffn2_w_in', 'ffn2_w_out', 'mix_w_in', 'mix_w_out', 'hgrn_lb_logits', 'hgrn_norm_g', 'mla_q_norm_g', 'mla_kv_norm_g', 'mla_w_uq', 'mla_w_ukv', 'fox_b_f', 'gmlp_ln_g', 'gmlp_ln_b', 'gmlp_w_s', 'gmlp_b_s', 'loss_target', 'm_ada_w', 'm_ada_b', 'm_ln_g', 'm_ln_b', 'm_ffn1_w_in', 'm_ffn1_w_out', 'm_ffn2_w_in', 'm_ffn2_w_out', 'm_mix_w_in', 'm_mix_w_out', 'm_hgrn_lb_logits', 'm_hgrn_norm_g', 'm_mla_q_norm_g', 'm_mla_kv_norm_g', 'm_mla_w_uq', 'm_mla_w_ukv', 'm_fox_b_f', 'm_gmlp_ln_g', 'm_gmlp_ln_b', 'm_gmlp_w_s', 'm_gmlp_b_s', 'v_ada_w', 'v_ada_b', 'v_ln_g', 'v_ln_b', 'v_ffn1_w_in', 'v_ffn1_w_out', 'v_ffn2_w_in', 'v_ffn2_w_out', 'v_mix_w_in', 'v_mix_w_out', 'v_hgrn_lb_logits', 'v_hgrn_norm_g', 'v_mla_q_norm_g', 'v_mla_kv_norm_g', 'v_mla_w_uq', 'v_mla_w_ukv', 'v_fox_b_f', 'v_gmlp_ln_g', 'v_gmlp_ln_b', 'v_gmlp_w_s', 'v_gmlp_b_s']
TWIN_OUTPUTS = ['loss', 'grad_x', 'grad_ada_w', 'grad_ada_b', 'grad_ln_g', 'grad_ln_b', 'grad_ffn1_w_in', 'grad_ffn1_w_out', 'grad_ffn2_w_in', 'grad_ffn2_w_out', 'grad_mix_w_in', 'grad_mix_w_out', 'grad_hgrn_lb_logits', 'grad_hgrn_norm_g', 'grad_mla_q_norm_g', 'grad_mla_kv_norm_g', 'grad_mla_w_uq', 'grad_mla_w_ukv', 'grad_fox_b_f', 'grad_gmlp_ln_g', 'grad_gmlp_ln_b', 'grad_gmlp_w_s', 'grad_gmlp_b_s', 'delta_ada_w', 'delta_ada_b', 'delta_ln_g', 'delta_ln_b', 'delta_ffn1_w_in', 'delta_ffn1_w_out', 'delta_ffn2_w_in', 'delta_ffn2_w_out', 'delta_mix_w_in', 'delta_mix_w_out', 'delta_hgrn_lb_logits', 'delta_hgrn_norm_g', 'delta_mla_q_norm_g', 'delta_mla_kv_norm_g', 'delta_mla_w_uq', 'delta_mla_w_ukv', 'delta_fox_b_f', 'delta_gmlp_ln_g', 'delta_gmlp_ln_b', 'delta_gmlp_w_s', 'delta_gmlp_b_s', 'new_m_ada_w', 'new_m_ada_b', 'new_m_ln_g', 'new_m_ln_b', 'new_m_ffn1_w_in', 'new_m_ffn1_w_out', 'new_m_ffn2_w_in', 'new_m_ffn2_w_out', 'new_m_mix_w_in', 'new_m_mix_w_out', 'new_m_hgrn_lb_logits', 'new_m_hgrn_norm_g', 'new_m_mla_q_norm_g', 'new_m_mla_kv_norm_g', 'new_m_mla_w_uq', 'new_m_mla_w_ukv', 'new_m_fox_b_f', 'new_m_gmlp_ln_g', 'new_m_gmlp_ln_b', 'new_m_gmlp_w_s', 'new_m_gmlp_b_s', 'new_v_ada_w', 'new_v_ada_b', 'new_v_ln_g', 'new_v_ln_b', 'new_v_ffn1_w_in', 'new_v_ffn1_w_out', 'new_v_ffn2_w_in', 'new_v_ffn2_w_out', 'new_v_mix_w_in', 'new_v_mix_w_out', 'new_v_hgrn_lb_logits', 'new_v_hgrn_norm_g', 'new_v_mla_q_norm_g', 'new_v_mla_kv_norm_g', 'new_v_mla_w_uq', 'new_v_mla_w_ukv', 'new_v_fox_b_f', 'new_v_gmlp_ln_g', 'new_v_gmlp_ln_b', 'new_v_gmlp_w_s', 'new_v_gmlp_b_s']
TWIN_LEAF_KINDS = {'loss': 'loss', 'grad_x': 'grad_x', 'grad_ada_w': 'grad_w', 'grad_ada_b': 'grad_w', 'grad_ln_g': 'grad_w', 'grad_ln_b': 'grad_w', 'grad_ffn1_w_in': 'grad_w', 'grad_ffn1_w_out': 'grad_w', 'grad_ffn2_w_in': 'grad_w', 'grad_ffn2_w_out': 'grad_w', 'grad_mix_w_in': 'grad_w', 'grad_mix_w_out': 'grad_w', 'grad_hgrn_lb_logits': 'grad_w', 'grad_hgrn_norm_g': 'grad_w', 'grad_mla_q_norm_g': 'grad_w', 'grad_mla_kv_norm_g': 'grad_w', 'grad_mla_w_uq': 'grad_w', 'grad_mla_w_ukv': 'grad_w', 'grad_fox_b_f': 'grad_w', 'grad_gmlp_ln_g': 'grad_w', 'grad_gmlp_ln_b': 'grad_w', 'grad_gmlp_w_s': 'grad_w', 'grad_gmlp_b_s': 'grad_w', 'delta_ada_w': 'delta_w', 'delta_ada_b': 'delta_w', 'delta_ln_g': 'delta_w', 'delta_ln_b': 'delta_w', 'delta_ffn1_w_in': 'delta_w', 'delta_ffn1_w_out': 'delta_w', 'delta_ffn2_w_in': 'delta_w', 'delta_ffn2_w_out': 'delta_w', 'delta_mix_w_in': 'delta_w', 'delta_mix_w_out': 'delta_w', 'delta_hgrn_lb_logits': 'delta_w', 'delta_hgrn_norm_g': 'delta_w', 'delta_mla_q_norm_g': 'delta_w', 'delta_mla_kv_norm_g': 'delta_w', 'delta_mla_w_uq': 'delta_w', 'delta_mla_w_ukv': 'delta_w', 'delta_fox_b_f': 'delta_w', 'delta_gmlp_ln_g': 'delta_w', 'delta_gmlp_ln_b': 'delta_w', 'delta_gmlp_w_s': 'delta_w', 'delta_gmlp_b_s': 'delta_w', 'new_m_ada_w': 'new_m', 'new_m_ada_b': 'new_m', 'new_m_ln_g': 'new_m', 'new_m_ln_b': 'new_m', 'new_m_ffn1_w_in': 'new_m', 'new_m_ffn1_w_out': 'new_m', 'new_m_ffn2_w_in': 'new_m', 'new_m_ffn2_w_out': 'new_m', 'new_m_mix_w_in': 'new_m', 'new_m_mix_w_out': 'new_m', 'new_m_hgrn_lb_logits': 'new_m', 'new_m_hgrn_norm_g': 'new_m', 'new_m_mla_q_norm_g': 'new_m', 'new_m_mla_kv_norm_g': 'new_m', 'new_m_mla_w_uq': 'new_m', 'new_m_mla_w_ukv': 'new_m', 'new_m_fox_b_f': 'new_m', 'new_m_gmlp_ln_g': 'new_m', 'new_m_gmlp_ln_b': 'new_m', 'new_m_gmlp_w_s': 'new_m', 'new_m_gmlp_b_s': 'new_m', 'new_v_ada_w': 'new_v', 'new_v_ada_b': 'new_v', 'new_v_ln_g': 'new_v', 'new_v_ln_b': 'new_v', 'new_v_ffn1_w_in': 'new_v', 'new_v_ffn1_w_out': 'new_v', 'new_v_ffn2_w_in': 'new_v', 'new_v_ffn2_w_out': 'new_v', 'new_v_mix_w_in': 'new_v', 'new_v_mix_w_out': 'new_v', 'new_v_hgrn_lb_logits': 'new_v', 'new_v_hgrn_norm_g': 'new_v', 'new_v_mla_q_norm_g': 'new_v', 'new_v_mla_kv_norm_g': 'new_v', 'new_v_mla_w_uq': 'new_v', 'new_v_mla_w_ukv': 'new_v', 'new_v_fox_b_f': 'new_v', 'new_v_gmlp_ln_g': 'new_v', 'new_v_gmlp_ln_b': 'new_v', 'new_v_gmlp_w_s': 'new_v', 'new_v_gmlp_b_s': 'new_v'}


def _forward(args):
    return _fwd_reference(*[args[k] for k in FWD_PARAMS])


def _output_shape():
    out = _jax.eval_shape(lambda: _forward(_fwd_setup_inputs(0)))
    return out.shape, out.dtype

N_MICROBATCH = 1
ADAM_LR = 0.001
ADAM_B1 = 0.9
ADAM_B2 = 0.999
ADAM_EPS = 1e-08
ADAM_WD = 0.01
ADAM_STEP = 10
PER_EXAMPLE_BATCH_AXIS = {'x': 0, 'c': 0, 'loss_target': 0}
SHARED_INPUTS = []
_WEIGHT_DTYPES = {'ada_w': _jnp.float32, 'ada_b': _jnp.float32, 'ln_g': _jnp.float32, 'ln_b': _jnp.float32, 'ffn1_w_in': _jnp.float32, 'ffn1_w_out': _jnp.float32, 'ffn2_w_in': _jnp.float32, 'ffn2_w_out': _jnp.float32, 'mix_w_in': _jnp.float32, 'mix_w_out': _jnp.float32, 'hgrn_lb_logits': _jnp.float32, 'hgrn_norm_g': _jnp.float32, 'mla_q_norm_g': _jnp.float32, 'mla_kv_norm_g': _jnp.float32, 'mla_w_uq': _jnp.float32, 'mla_w_ukv': _jnp.float32, 'fox_b_f': _jnp.float32, 'gmlp_ln_g': _jnp.float32, 'gmlp_ln_b': _jnp.float32, 'gmlp_w_s': _jnp.float32, 'gmlp_b_s': _jnp.float32}
MOMENT_SCALE = {'ada_w': 2.439095e-02, 'ada_b': 4.750921e-02, 'ln_g': 1.312517e+01, 'ln_b': 1.074798e+00, 'ffn1_w_in': 1.180309e-02, 'ffn1_w_out': 3.845825e-02, 'ffn2_w_in': 1.161439e-02, 'ffn2_w_out': 3.790634e-02, 'mix_w_in': 2.543735e-02, 'mix_w_out': 7.018983e-02, 'hgrn_lb_logits': 5.816925e-03, 'hgrn_norm_g': 3.695170e-02, 'mla_q_norm_g': 1.149177e-02, 'mla_kv_norm_g': 2.807453e-02, 'mla_w_uq': 9.575770e-03, 'mla_w_ukv': 1.304867e-02, 'fox_b_f': 9.250279e-02, 'gmlp_ln_g': 1.519501e-02, 'gmlp_ln_b': 1.474052e-02, 'gmlp_w_s': 2.068949e-02, 'gmlp_b_s': 2.979234e-02}


def _to_microbatches(a, axis):
    t = _jnp.moveaxis(a, axis, 0)
    t = t.reshape((N_MICROBATCH, t.shape[0] // N_MICROBATCH) + t.shape[1:])
    return _jnp.moveaxis(t, 1, axis + 1)


def setup_inputs(seed: int = 0) -> dict:
    inp = _fwd_setup_inputs(seed)
    key = _jax.random.fold_in(_jax.random.key(seed), 7919)
    shape, _ = _output_shape()
    out = dict(inp)
    out["loss_target"] = _jax.random.normal(_jax.random.fold_in(key, 0), shape, _jnp.float32)
    for i, name in enumerate(TWIN_WEIGHTS):
        w = inp[name].astype(_jnp.float32)
        if MOMENT_SCALE is None:
            s = _jnp.sqrt(_jnp.mean(_jnp.square(w)) + 1e-30)
        else:
            s = MOMENT_SCALE[name]
        km, kv = _jax.random.split(_jax.random.fold_in(key, i + 1))
        out[name] = w
        out["m_" + name] = s * _jax.random.normal(km, w.shape, _jnp.float32)
        out["v_" + name] = (s * s) * _jax.random.uniform(kv, w.shape, _jnp.float32, 0.5, 1.5)
    if N_MICROBATCH > 1:
        for name, axis in PER_EXAMPLE_BATCH_AXIS.items():
            out[name] = _to_microbatches(out[name], axis)
    return {'x': out['x'], 'c': out['c'], 'ada_w': out['ada_w'], 'ada_b': out['ada_b'], 'ln_g': out['ln_g'], 'ln_b': out['ln_b'], 'ffn1_w_in': out['ffn1_w_in'], 'ffn1_w_out': out['ffn1_w_out'], 'ffn2_w_in': out['ffn2_w_in'], 'ffn2_w_out': out['ffn2_w_out'], 'mix_w_in': out['mix_w_in'], 'mix_w_out': out['mix_w_out'], 'hgrn_lb_logits': out['hgrn_lb_logits'], 'hgrn_norm_g': out['hgrn_norm_g'], 'mla_q_norm_g': out['mla_q_norm_g'], 'mla_kv_norm_g': out['mla_kv_norm_g'], 'mla_w_uq': out['mla_w_uq'], 'mla_w_ukv': out['mla_w_ukv'], 'fox_b_f': out['fox_b_f'], 'gmlp_ln_g': out['gmlp_ln_g'], 'gmlp_ln_b': out['gmlp_ln_b'], 'gmlp_w_s': out['gmlp_w_s'], 'gmlp_b_s': out['gmlp_b_s'], 'loss_target': out['loss_target'], 'm_ada_w': out['m_ada_w'], 'm_ada_b': out['m_ada_b'], 'm_ln_g': out['m_ln_g'], 'm_ln_b': out['m_ln_b'], 'm_ffn1_w_in': out['m_ffn1_w_in'], 'm_ffn1_w_out': out['m_ffn1_w_out'], 'm_ffn2_w_in': out['m_ffn2_w_in'], 'm_ffn2_w_out': out['m_ffn2_w_out'], 'm_mix_w_in': out['m_mix_w_in'], 'm_mix_w_out': out['m_mix_w_out'], 'm_hgrn_lb_logits': out['m_hgrn_lb_logits'], 'm_hgrn_norm_g': out['m_hgrn_norm_g'], 'm_mla_q_norm_g': out['m_mla_q_norm_g'], 'm_mla_kv_norm_g': out['m_mla_kv_norm_g'], 'm_mla_w_uq': out['m_mla_w_uq'], 'm_mla_w_ukv': out['m_mla_w_ukv'], 'm_fox_b_f': out['m_fox_b_f'], 'm_gmlp_ln_g': out['m_gmlp_ln_g'], 'm_gmlp_ln_b': out['m_gmlp_ln_b'], 'm_gmlp_w_s': out['m_gmlp_w_s'], 'm_gmlp_b_s': out['m_gmlp_b_s'], 'v_ada_w': out['v_ada_w'], 'v_ada_b': out['v_ada_b'], 'v_ln_g': out['v_ln_g'], 'v_ln_b': out['v_ln_b'], 'v_ffn1_w_in': out['v_ffn1_w_in'], 'v_ffn1_w_out': out['v_ffn1_w_out'], 'v_ffn2_w_in': out['v_ffn2_w_in'], 'v_ffn2_w_out': out['v_ffn2_w_out'], 'v_mix_w_in': out['v_mix_w_in'], 'v_mix_w_out': out['v_mix_w_out'], 'v_hgrn_lb_logits': out['v_hgrn_lb_logits'], 'v_hgrn_norm_g': out['v_hgrn_norm_g'], 'v_mla_q_norm_g': out['v_mla_q_norm_g'], 'v_mla_kv_norm_g': out['v_mla_kv_norm_g'], 'v_mla_w_uq': out['v_mla_w_uq'], 'v_mla_w_ukv': out['v_mla_w_ukv'], 'v_fox_b_f': out['v_fox_b_f'], 'v_gmlp_ln_g': out['v_gmlp_ln_g'], 'v_gmlp_ln_b': out['v_gmlp_ln_b'], 'v_gmlp_w_s': out['v_gmlp_w_s'], 'v_gmlp_b_s': out['v_gmlp_b_s']}


def _loss(weights, diff, rest, loss_target):
    with _jax.named_scope("forward"):
        args = {**rest, TWIN_DIFF_INPUT: diff, **{k: w.astype(_WEIGHT_DTYPES[k]) for k, w in weights.items()}}
        y = _forward(args)
    with _jax.named_scope("loss_head"):
        err = _jnp.square(y.astype(_jnp.float32) - loss_target)
        return 0.5 * _jnp.sum(_jnp.mean(err, axis=-1)) if err.ndim else 0.5 * err


def _adamw(w, g, m, v):
    m = ADAM_B1 * m + (1.0 - ADAM_B1) * g
    v = ADAM_B2 * v + (1.0 - ADAM_B2) * _jnp.square(g)
    m_hat = m / (1.0 - ADAM_B1 ** ADAM_STEP)
    v_hat = v / (1.0 - ADAM_B2 ** ADAM_STEP)
    delta = -ADAM_LR * (m_hat / (_jnp.sqrt(v_hat) + ADAM_EPS) + ADAM_WD * w)
    return delta, m, v


def reference(x, c, ada_w, ada_b, ln_g, ln_b, ffn1_w_in, ffn1_w_out, ffn2_w_in, ffn2_w_out, mix_w_in, mix_w_out, hgrn_lb_logits, hgrn_norm_g, mla_q_norm_g, mla_kv_norm_g, mla_w_uq, mla_w_ukv, fox_b_f, gmlp_ln_g, gmlp_ln_b, gmlp_w_s, gmlp_b_s, loss_target, m_ada_w, m_ada_b, m_ln_g, m_ln_b, m_ffn1_w_in, m_ffn1_w_out, m_ffn2_w_in, m_ffn2_w_out, m_mix_w_in, m_mix_w_out, m_hgrn_lb_logits, m_hgrn_norm_g, m_mla_q_norm_g, m_mla_kv_norm_g, m_mla_w_uq, m_mla_w_ukv, m_fox_b_f, m_gmlp_ln_g, m_gmlp_ln_b, m_gmlp_w_s, m_gmlp_b_s, v_ada_w, v_ada_b, v_ln_g, v_ln_b, v_ffn1_w_in, v_ffn1_w_out, v_ffn2_w_in, v_ffn2_w_out, v_mix_w_in, v_mix_w_out, v_hgrn_lb_logits, v_hgrn_norm_g, v_mla_q_norm_g, v_mla_kv_norm_g, v_mla_w_uq, v_mla_w_ukv, v_fox_b_f, v_gmlp_ln_g, v_gmlp_ln_b, v_gmlp_w_s, v_gmlp_b_s):
    given = dict(x=x, c=c, ada_w=ada_w, ada_b=ada_b, ln_g=ln_g, ln_b=ln_b, ffn1_w_in=ffn1_w_in, ffn1_w_out=ffn1_w_out, ffn2_w_in=ffn2_w_in, ffn2_w_out=ffn2_w_out, mix_w_in=mix_w_in, mix_w_out=mix_w_out, hgrn_lb_logits=hgrn_lb_logits, hgrn_norm_g=hgrn_norm_g, mla_q_norm_g=mla_q_norm_g, mla_kv_norm_g=mla_kv_norm_g, mla_w_uq=mla_w_uq, mla_w_ukv=mla_w_ukv, fox_b_f=fox_b_f, gmlp_ln_g=gmlp_ln_g, gmlp_ln_b=gmlp_ln_b, gmlp_w_s=gmlp_w_s, gmlp_b_s=gmlp_b_s, loss_target=loss_target, m_ada_w=m_ada_w, m_ada_b=m_ada_b, m_ln_g=m_ln_g, m_ln_b=m_ln_b, m_ffn1_w_in=m_ffn1_w_in, m_ffn1_w_out=m_ffn1_w_out, m_ffn2_w_in=m_ffn2_w_in, m_ffn2_w_out=m_ffn2_w_out, m_mix_w_in=m_mix_w_in, m_mix_w_out=m_mix_w_out, m_hgrn_lb_logits=m_hgrn_lb_logits, m_hgrn_norm_g=m_hgrn_norm_g, m_mla_q_norm_g=m_mla_q_norm_g, m_mla_kv_norm_g=m_mla_kv_norm_g, m_mla_w_uq=m_mla_w_uq, m_mla_w_ukv=m_mla_w_ukv, m_fox_b_f=m_fox_b_f, m_gmlp_ln_g=m_gmlp_ln_g, m_gmlp_ln_b=m_gmlp_ln_b, m_gmlp_w_s=m_gmlp_w_s, m_gmlp_b_s=m_gmlp_b_s, v_ada_w=v_ada_w, v_ada_b=v_ada_b, v_ln_g=v_ln_g, v_ln_b=v_ln_b, v_ffn1_w_in=v_ffn1_w_in, v_ffn1_w_out=v_ffn1_w_out, v_ffn2_w_in=v_ffn2_w_in, v_ffn2_w_out=v_ffn2_w_out, v_mix_w_in=v_mix_w_in, v_mix_w_out=v_mix_w_out, v_hgrn_lb_logits=v_hgrn_lb_logits, v_hgrn_norm_g=v_hgrn_norm_g, v_mla_q_norm_g=v_mla_q_norm_g, v_mla_kv_norm_g=v_mla_kv_norm_g, v_mla_w_uq=v_mla_w_uq, v_mla_w_ukv=v_mla_w_ukv, v_fox_b_f=v_fox_b_f, v_gmlp_ln_g=v_gmlp_ln_g, v_gmlp_ln_b=v_gmlp_ln_b, v_gmlp_w_s=v_gmlp_w_s, v_gmlp_b_s=v_gmlp_b_s)
    weights = {n: given[n] for n in TWIN_WEIGHTS}
    shared = {n: given[n] for n in SHARED_INPUTS}
    per_example = {n: given[n] for n in ['x', 'c']}
    grad_fn = _jax.value_and_grad(_loss, argnums=(0, 1))

    def one_microbatch(ex, loss_target):
        ex = dict(ex)
        diff = ex.pop(TWIN_DIFF_INPUT)
        return grad_fn(weights, diff, {**shared, **ex}, loss_target)

    if N_MICROBATCH == 1:
        loss, (grad_w, grad_x) = one_microbatch(per_example, given["loss_target"])
    else:
        def body(carry, xs):
            loss_sum, grad_sum = carry
            l_k, (gw_k, gx_k) = one_microbatch(xs[0], xs[1])
            with _jax.named_scope("update"):
                return (loss_sum + l_k, _jax.tree.map(_jnp.add, grad_sum, gw_k)), gx_k

        init = (_jnp.zeros((), _jnp.float32), _jax.tree.map(_jnp.zeros_like, weights))
        (loss, grad_w), grad_x = _jax.lax.scan(body, init, (per_example, given["loss_target"]))
    with _jax.named_scope("update"):
        delta_w, new_m, new_v = {}, {}, {}
        for n in TWIN_WEIGHTS:
            delta_w[n], new_m[n], new_v[n] = _adamw(weights[n], grad_w[n], given["m_" + n], given["v_" + n])
    return (loss, grad_x, *[grad_w[n] for n in TWIN_WEIGHTS], *[delta_w[n] for n in TWIN_WEIGHTS],
            *[new_m[n] for n in TWIN_WEIGHTS], *[new_v[n] for n in TWIN_WEIGHTS])
```

```python
import functools

import numpy as np
import jax
import jax.numpy as jnp
from jax import lax
from jax.experimental import pallas as pl
from jax.experimental.pallas import tpu as pltpu

F32 = jnp.float32
BF16 = jnp.bfloat16
HI = lax.Precision.HIGHEST

D_MODEL = 1024
DEPTH = 2
GROUP_WIDTH = 256
N_HEADS = 4
HEAD_DIM = 64
A_CHUNK = 16
LB_FLOOR = 1e-30
B_NOPE = 64
B_ROPE = 32
ROPE_THETA = 10000.0
D_CHUNK = 128
D_FF = 2816
N_MOD = 9
ALPHA = (2 * DEPTH) ** 0.25
LN_EPS = 1e-5
RMS_EPS = 1e-6
ADAM_LR = 0.001
ADAM_B1 = 0.9
ADAM_B2 = 0.999
ADAM_EPS = 1e-08
ADAM_WD = 0.01
ADAM_STEP = 10

N_DEV = 8
LANES = 128
PACK_W = 3712
MO_W = 1536
VMEM_LIMIT = 56 * 1024 * 1024
NEG = -1e30

MIX_ORIG_W = 2724
O_BCQ, O_BCKV, O_BKR, O_CQ, O_CK, O_CV, O_CF, O_DU, O_DV = 1024, 1280, 1408, 1440, 1696, 1952, 2208, 2212, 2468
P_B, P_KR, P_CQ, P_CKV, P_D, P_CF = 1024, 1408, 1536, 2048, 3072, 3584


def _cparams(sem):
    return pltpu.CompilerParams(dimension_semantics=sem, vmem_limit_bytes=VMEM_LIMIT)


def _mix_in_src():
    src = -np.ones(PACK_W, np.int64)
    src[0:P_KR] = np.arange(0, O_BKR)
    src[P_KR + 64:P_KR + 80] = O_BKR + np.arange(16)
    src[P_KR + 96:P_KR + 112] = O_BKR + 16 + np.arange(16)
    for h in range(N_HEADS):
        src[P_CQ + 128 * h:P_CQ + 128 * h + 64] = O_CQ + 64 * h + np.arange(64)
        src[P_CKV + 256 * h:P_CKV + 256 * h + 64] = O_CK + 64 * h + np.arange(64)
        src[P_CKV + 256 * h + 128:P_CKV + 256 * h + 192] = O_CV + 64 * h + np.arange(64)
    src[P_D:P_D + 512] = O_DU + np.arange(512)
    src[P_CF:P_CF + 4] = O_CF + np.arange(4)
    return src


def _uq_src():
    src = -np.ones(512, np.int64)
    for h in range(N_HEADS):
        src[128 * h:128 * h + 64] = 96 * h + np.arange(64)
        src[128 * h + 64:128 * h + 80] = 96 * h + 64 + np.arange(16)
        src[128 * h + 96:128 * h + 112] = 96 * h + 80 + np.arange(16)
    return src


def _ukv_src():
    src = -np.ones(1024, np.int64)
    for h in range(N_HEADS):
        src[256 * h:256 * h + 64] = 128 * h + np.arange(64)
        src[256 * h + 128:256 * h + 192] = 128 * h + 64 + np.arange(64)
    return src


def _mo_src():
    src = -np.ones(MO_W, np.int64)
    src[0:256] = np.arange(256)
    for g in range(2):
        for h in range(N_HEADS):
            src[256 + 512 * g + 128 * h:256 + 512 * g + 128 * h + 64] = 256 + 256 * g + 64 * h + np.arange(64)
    src[1280:1536] = 768 + np.arange(256)
    return src


def _pack_cols(w, src):
    valid = jnp.asarray(src >= 0)
    return jnp.where(valid, jnp.take(w, jnp.asarray(np.maximum(src, 0)), axis=-1), jnp.zeros((), w.dtype))


def _unpack_cols(wp, src, n):
    dst = np.zeros(n, np.int64)
    dst[src[src >= 0]] = np.nonzero(src >= 0)[0]
    return jnp.take(wp, jnp.asarray(dst), axis=-1)


def _rope_tables(seq):
    half = B_ROPE // 2
    inv_freq = ROPE_THETA ** (-jnp.arange(half, dtype=F32) / half)
    ang = jnp.arange(seq).astype(F32)[:, None] * inv_freq[None, :]
    cos, sin = jnp.cos(ang), jnp.sin(ang)
    z16 = jnp.zeros((seq, 16), F32)
    c = jnp.concatenate([jnp.ones((seq, 64), F32), cos, z16, cos, z16], axis=1)
    s1 = jnp.concatenate([jnp.zeros((seq, 64), F32), -sin, z16, z16, z16], axis=1)
    s2 = jnp.concatenate([jnp.zeros((seq, 64), F32), z16, z16, sin, z16], axis=1)
    return c, s1, s2


def _matmul(a, b, *, mode, group_out, out_dtype, tm, tk, name):
    ga, gb = a.shape[0], b.shape[0]
    g_n = max(ga, gb)
    if mode == "tn":
        k_dim, m_dim = a.shape[1:]
    else:
        m_dim, k_dim = a.shape[1:]
    n_dim = b.shape[1] if mode == "nt" else b.shape[2]
    assert m_dim % tm == 0 and k_dim % tk == 0
    kt = k_dim // tk
    n_red = kt if group_out else g_n * kt
    g_out = g_n if group_out else 1

    def split(g, r):
        return (g, r) if group_out else (r // kt, r % kt)

    def a_map(g, i, r):
        gg, kk = split(g, r)
        gg = gg if ga > 1 else 0
        return (gg, kk, i) if mode == "tn" else (gg, i, kk)

    def b_map(g, i, r):
        gg, kk = split(g, r)
        gg = gg if gb > 1 else 0
        return (gg, 0, kk) if mode == "nt" else (gg, kk, 0)

    a_blk = (None, tk, tm) if mode == "tn" else (None, tm, tk)
    b_blk = (None, n_dim, tk) if mode == "nt" else (None, tk, n_dim)
    dn = {"nn": (((1,), (0,)), ((), ())), "nt": (((1,), (1,)), ((), ())), "tn": (((0,), (0,)), ((), ()))}[mode]

    def body(a_ref, b_ref, o_ref, *scratch):
        part = lax.dot_general(a_ref[...].astype(BF16), b_ref[...].astype(BF16), dn, preferred_element_type=F32)
        if n_red == 1:
            o_ref[...] = part.astype(o_ref.dtype)
            return
        acc_ref, = scratch
        r = pl.program_id(2)

        @pl.when(r == 0)
        def _():
            acc_ref[...] = part

        @pl.when(r > 0)
        def _():
            acc_ref[...] += part

        @pl.when(r == n_red - 1)
        def _():
            o_ref[...] = acc_ref[...].astype(o_ref.dtype)

    return pl.pallas_call(
        body, name=name, grid=(g_out, m_dim // tm, n_red),
        in_specs=[pl.BlockSpec(a_blk, a_map), pl.BlockSpec(b_blk, b_map)],
        out_specs=pl.BlockSpec((None, tm, n_dim), lambda g, i, r: (g, i, 0)),
        out_shape=jax.ShapeDtypeStruct((g_out, m_dim, n_dim), out_dtype),
        scratch_shapes=[] if n_red == 1 else [pltpu.VMEM((tm, n_dim), F32)],
        compiler_params=_cparams(("parallel", "parallel", "arbitrary")),
    )(a, b)


def _row_spec(ts, d):
    return pl.BlockSpec((None, ts, d), lambda b, s: (b, s, 0))


def _mod_spec(d):
    return pl.BlockSpec((None, N_MOD, d), lambda b, s: (b, 0, 0))


def _vec_spec(d):
    return pl.BlockSpec((1, d), lambda b, s: (0, 0))


def _bvec_spec(d):
    return pl.BlockSpec((None, 1, d), lambda b, s: (b, 0, 0))


def _modulate(x, mod, sh_row, sc_row, name, ts=512):
    bsz, seq, d = x.shape

    def body(x_ref, mod_ref, o_ref):
        sh = mod_ref[sh_row:sh_row + 1, :]
        sc = mod_ref[sc_row:sc_row + 1, :]
        o_ref[...] = (x_ref[...] * (1.0 + sc) + sh).astype(o_ref.dtype)

    return pl.pallas_call(
        body, name=name, grid=(bsz, seq // ts),
        in_specs=[_row_spec(ts, d), _mod_spec(d)], out_specs=_row_spec(ts, d),
        out_shape=jax.ShapeDtypeStruct((bsz, seq, d), BF16),
        compiler_params=_cparams(("parallel", "parallel")),
    )(x, mod)


def _modulate_bwd(dh, x, mod, dx_res, sc_row, name, ts=512):
    bsz, seq, d = x.shape

    def body(dh_ref, x_ref, mod_ref, dxr_ref, dx_ref, dsh_ref, dsc_ref):
        s = pl.program_id(1)
        sc = mod_ref[sc_row:sc_row + 1, :]
        dh_v = dh_ref[...]
        dx_ref[...] = dxr_ref[...] + dh_v * (1.0 + sc)
        psh = jnp.sum(dh_v, axis=0, keepdims=True)
        psc = jnp.sum(dh_v * x_ref[...], axis=0, keepdims=True)

        @pl.when(s == 0)
        def _():
            dsh_ref[...] = psh
            dsc_ref[...] = psc

        @pl.when(s > 0)
        def _():
            dsh_ref[...] += psh
            dsc_ref[...] += psc

    return pl.pallas_call(
        body, name=name, grid=(bsz, seq // ts),
        in_specs=[_row_spec(ts, d), _row_spec(ts, d), _mod_spec(d), _row_spec(ts, d)],
        out_specs=[_row_spec(ts, d), _bvec_spec(d), _bvec_spec(d)],
        out_shape=[jax.ShapeDtypeStruct((bsz, seq, d), F32), jax.ShapeDtypeStruct((bsz, 1, d), F32),
                   jax.ShapeDtypeStruct((bsz, 1, d), F32)],
        compiler_params=_cparams(("parallel", "arbitrary")),
    )(dh, x, mod, dx_res)


def _res_ln_fn(x, f, g, lng, lnb, cmul):
    r = ALPHA * x + (cmul * (1.0 + g)) * f
    mu = jnp.mean(r, axis=-1, keepdims=True)
    rc = r - mu
    var = jnp.mean(rc * rc, axis=-1, keepdims=True)
    return rc * lax.rsqrt(var + LN_EPS) * lng + lnb


def _res_ln(x, f, mod, lng, lnb, g_row, cmul, name, ts=512):
    bsz, seq, d = x.shape

    def body(x_ref, f_ref, mod_ref, lng_ref, lnb_ref, o_ref):
        g = mod_ref[g_row:g_row + 1, :]
        o_ref[...] = _res_ln_fn(x_ref[...], f_ref[...], g, lng_ref[...], lnb_ref[...], cmul)

    return pl.pallas_call(
        body, name=name, grid=(bsz, seq // ts),
        in_specs=[_row_spec(ts, d), _row_spec(ts, d), _mod_spec(d), _vec_spec(d), _vec_spec(d)],
        out_specs=_row_spec(ts, d), out_shape=jax.ShapeDtypeStruct((bsz, seq, d), F32),
        compiler_params=_cparams(("parallel", "parallel")),
    )(x, f, mod, lng, lnb)


def _res_ln_bwd(dy, x, f, mod, lng, lnb, g_row, cmul, name, ts=256):
    bsz, seq, d = x.shape

    def body(dy_ref, x_ref, f_ref, mod_ref, lng_ref, lnb_ref, dx_ref, df_ref, dg_ref, dlg_ref, dlb_ref):
        b, s = pl.program_id(0), pl.program_id(1)
        g = mod_ref[g_row:g_row + 1, :]
        _, vjp = jax.vjp(functools.partial(_res_ln_fn, cmul=cmul), x_ref[...], f_ref[...], g, lng_ref[...],
                         lnb_ref[...])
        dx, df, dg, dlg, dlb = vjp(dy_ref[...])
        dx_ref[...] = dx
        df_ref[...] = df.astype(df_ref.dtype)

        @pl.when(s == 0)
        def _():
            dg_ref[...] = dg

        @pl.when(s > 0)
        def _():
            dg_ref[...] += dg

        first = jnp.logical_and(b == 0, s == 0)

        @pl.when(first)
        def _():
            dlg_ref[...] = dlg
            dlb_ref[...] = dlb

        @pl.when(jnp.logical_not(first))
        def _():
            dlg_ref[...] += dlg
            dlb_ref[...] += dlb

    return pl.pallas_call(
        body, name=name, grid=(bsz, seq // ts),
        in_specs=[_row_spec(ts, d), _row_spec(ts, d), _row_spec(ts, d), _mod_spec(d), _vec_spec(d), _vec_spec(d)],
        out_specs=[_row_spec(ts, d), _row_spec(ts, d), _bvec_spec(d), _vec_spec(d), _vec_spec(d)],
        out_shape=[jax.ShapeDtypeStruct((bsz, seq, d), F32), jax.ShapeDtypeStruct((bsz, seq, d), BF16),
                   jax.ShapeDtypeStruct((bsz, 1, d), F32), jax.ShapeDtypeStruct((1, d), F32),
                   jax.ShapeDtypeStruct((1, d), F32)],
        compiler_params=_cparams(("arbitrary", "arbitrary")),
    )(dy, x, f, mod, lng, lnb)


def _loss_head(y, target, name, ts=512):
    bsz, seq, d = y.shape
    n_s = seq // ts

    def body(y_ref, t_ref, dy_ref, loss_ref, acc_ref):
        b, s = pl.program_id(0), pl.program_id(1)
        err = y_ref[...] - t_ref[...]
        dy_ref[...] = err * (1.0 / d)
        part = jnp.sum(err * err, axis=0, keepdims=True)
        first = jnp.logical_and(b == 0, s == 0)

        @pl.when(first)
        def _():
            acc_ref[...] = part

        @pl.when(jnp.logical_not(first))
        def _():
            acc_ref[...] += part

        @pl.when(jnp.logical_and(b == bsz - 1, s == n_s - 1))
        def _():
            loss_ref[...] = jnp.sum(acc_ref[...], axis=1, keepdims=True) * (0.5 / d)

    return pl.pallas_call(
        body, name=name, grid=(bsz, n_s),
        in_specs=[_row_spec(ts, d), _row_spec(ts, d)],
        out_specs=[_row_spec(ts, d), pl.BlockSpec((1, 1), lambda b, s: (0, 0))],
        out_shape=[jax.ShapeDtypeStruct((bsz, seq, d), F32), jax.ShapeDtypeStruct((1, 1), F32)],
        scratch_shapes=[pltpu.VMEM((1, d), F32)],
        compiler_params=_cparams(("arbitrary", "arbitrary")),
    )(y, target)


def _swiglu(z, name, tm=512):
    n_sh, t, w = z.shape
    half = n_sh // 2

    def body(g_ref, u_ref, o_ref):
        g = g_ref[...]
        o_ref[...] = (g * jax.nn.sigmoid(g) * u_ref[...]).astype(o_ref.dtype)

    return pl.pallas_call(
        body, name=name, grid=(half, t // tm),
        in_specs=[pl.BlockSpec((None, tm, w), lambda g, i: (g, i, 0)),
                  pl.BlockSpec((None, tm, w), lambda g, i: (g + half, i, 0))],
        out_specs=pl.BlockSpec((None, tm, w), lambda g, i: (g, i, 0)),
        out_shape=jax.ShapeDtypeStruct((half, t, w), BF16),
        compiler_params=_cparams(("parallel", "parallel")),
    )(z, z)


def _swiglu_bwd(da, z, name, tm=512):
    n_sh, t, w = z.shape
    half = n_sh // 2

    def body(da_ref, g_ref, u_ref, o_ref):
        j = pl.program_id(0)
        g = g_ref[...]
        sig = jax.nn.sigmoid(g)

        @pl.when(j < half)
        def _():
            o_ref[...] = (da_ref[...] * u_ref[...] * (sig * (1.0 + g * (1.0 - sig)))).astype(o_ref.dtype)

        @pl.when(j >= half)
        def _():
            o_ref[...] = (da_ref[...] * (g * sig)).astype(o_ref.dtype)

    return pl.pallas_call(
        body, name=name, grid=(n_sh, t // tm),
        in_specs=[pl.BlockSpec((None, tm, w), lambda j, i: (j % half, i, 0)),
                  pl.BlockSpec((None, tm, w), lambda j, i: (j % half, i, 0)),
                  pl.BlockSpec((None, tm, w), lambda j, i: (j % half + half, i, 0))],
        out_specs=pl.BlockSpec((None, tm, w), lambda j, i: (j, i, 0)),
        out_shape=jax.ShapeDtypeStruct((n_sh, t, w), BF16),
        compiler_params=_cparams(("parallel", "parallel")),
    )(da, z, z)


def _log_sigmoid(x):
    return jnp.minimum(x, 0.0) - jnp.log(1.0 + jnp.exp(-jnp.abs(x)))


def _hgrn_consts():
    r = lax.broadcasted_iota(jnp.int32, (GROUP_WIDTH, GROUP_WIDTH), 0)
    c = lax.broadcasted_iota(jnp.int32, (GROUP_WIDTH, GROUP_WIDTH), 1)
    bd = (r // HEAD_DIM == c // HEAD_DIM).astype(F32)
    r16 = lax.broadcasted_iota(jnp.int32, (A_CHUNK, A_CHUNK), 0)
    c16 = lax.broadcasted_iota(jnp.int32, (A_CHUNK, A_CHUNK), 1)
    tril = (r16 >= c16).astype(F32)
    rows = lax.broadcasted_iota(jnp.int32, (A_CHUNK, GROUP_WIDTH), 0)
    return bd, tril, rows


def _hgrn_lb(logits8, layer):
    rows = lax.broadcasted_iota(jnp.int32, logits8.shape, 0)
    valid = rows < DEPTH
    mx = jnp.max(jnp.where(valid, logits8, NEG), axis=0, keepdims=True)
    e = jnp.where(valid, jnp.exp(logits8 - mx), 0.0)
    sm = e / jnp.sum(e, axis=0, keepdims=True)
    pick = jnp.logical_and(rows >= 1, rows <= layer)
    return jnp.sum(jnp.where(pick, sm, 0.0), axis=0, keepdims=True)


def _hgrn_chunk(aq, af, ai, ag, logits8, norm_g, st, *, layer, consts):
    bd, tril, rows = consts
    lb = _hgrn_lb(logits8, layer)
    la = jnp.log(jnp.maximum(lb, LB_FLOOR))
    b2 = jnp.log(1.0 - lb) + _log_sigmoid(af)
    log_f = jnp.maximum(la, b2) + jnp.log(1.0 + jnp.exp(-jnp.abs(la - b2)))
    k = 1.0 - jnp.exp(log_f)
    qf = aq * jax.nn.sigmoid(aq)
    g_cum = jnp.dot(tril, log_f, precision=HI, preferred_element_type=F32)

    def row(v, s):
        return jnp.sum(jnp.where(rows == s, v, 0.0), axis=0, keepdims=True)

    parts = []
    v_rows = []
    for s in range(A_CHUNK):
        rel = jnp.where(rows >= s, g_cum - row(g_cum, s), NEG)
        parts.append(qf * (row(k, s) * jnp.exp(rel)))
        v_rows.append(row(ai, s))
    a_all = jnp.dot(jnp.concatenate(parts, axis=0), bd, precision=HI, preferred_element_type=F32)
    o = jnp.zeros_like(aq)
    for s in range(A_CHUNK):
        o = o + a_all[s * A_CHUNK:(s + 1) * A_CHUNK, :] * v_rows[s]
    q_dec = qf * jnp.exp(g_cum)
    o = o + lax.dot_general(q_dec, st, (((1,), (1,)), ((), ())), precision=HI, preferred_element_type=F32)
    g_last = row(g_cum, A_CHUNK - 1)
    k_end = k * jnp.exp(g_last - g_cum)
    kv = lax.dot_general(ai, k_end, (((0,), (0,)), ((), ())), precision=HI, preferred_element_type=F32)
    st_new = st * jnp.exp(g_last) + kv * bd
    ms = jnp.dot(o * o, bd, precision=HI, preferred_element_type=F32) * (1.0 / HEAD_DIM)
    o = o * lax.rsqrt(ms + RMS_EPS) * norm_g
    return o * (ag * jax.nn.sigmoid(ag)), st_new


def _hgrn_fwd(proj, logits8, norm_g, layer, name, ts=128):
    bsz, seq, _ = proj.shape
    n_ch = ts // A_CHUNK

    def body(p_ref, lg_ref, ng_ref, o_ref, st_ref, st_scr):
        @pl.when(pl.program_id(1) == 0)
        def _():
            st_scr[...] = jnp.zeros_like(st_scr)

        consts = _hgrn_consts()
        logits_v, ng_v = lg_ref[...], ng_ref[...]

        def chunk(ci, carry):
            r = pl.multiple_of(ci * A_CHUNK, A_CHUNK)
            st = st_scr[...]
            st_ref[ci] = st
            o, st_new = _hgrn_chunk(
                p_ref[pl.ds(r, A_CHUNK), 0:256], p_ref[pl.ds(r, A_CHUNK), 256:512],
                p_ref[pl.ds(r, A_CHUNK), 512:768], p_ref[pl.ds(r, A_CHUNK), 768:1024],
                logits_v, ng_v, st, layer=layer, consts=consts)
            o_ref[pl.ds(r, A_CHUNK), :] = o.astype(o_ref.dtype)
            st_scr[...] = st_new
            return carry

        lax.fori_loop(0, n_ch, chunk, 0)

    return pl.pallas_call(
        body, name=name, grid=(bsz, seq // ts),
        in_specs=[pl.BlockSpec((None, ts, 1024), lambda b, s: (b, s, 0)),
                  pl.BlockSpec((8, GROUP_WIDTH), lambda b, s: (0, 0)),
                  pl.BlockSpec((1, GROUP_WIDTH), lambda b, s: (0, 0))],
        out_specs=[pl.BlockSpec((None, ts, GROUP_WIDTH), lambda b, s: (b, s, 0)),
                   pl.BlockSpec((None, n_ch, GROUP_WIDTH, GROUP_WIDTH), lambda b, s: (b, s, 0, 0))],
        out_shape=[jax.ShapeDtypeStruct((bsz, seq, MO_W), BF16),
                   jax.ShapeDtypeStruct((bsz, seq // A_CHUNK, GROUP_WIDTH, GROUP_WIDTH), F32)],
        scratch_shapes=[pltpu.VMEM((GROUP_WIDTH, GROUP_WIDTH), F32)],
        compiler_params=_cparams(("parallel", "arbitrary")),
    )(proj, logits8, norm_g)


def _hgrn_bwd(dmo, proj, states, logits8, norm_g, layer, name, ts=128):
    bsz, seq, _ = proj.shape
    n_ch = ts // A_CHUNK
    n_s = seq // ts

    def body(do_ref, p_ref, st_ref, lg_ref, ng_ref, dp_ref, dlg_ref, dng_ref, dst_scr):
        b, s = pl.program_id(0), pl.program_id(1)

        @pl.when(s == 0)
        def _():
            dst_scr[...] = jnp.zeros_like(dst_scr)

        @pl.when(jnp.logical_and(b == 0, s == 0))
        def _():
            dlg_ref[...] = jnp.zeros_like(dlg_ref)
            dng_ref[...] = jnp.zeros_like(dng_ref)

        consts = _hgrn_consts()
        logits_v, ng_v = lg_ref[...], ng_ref[...]
        fn = functools.partial(_hgrn_chunk, layer=layer, consts=consts)

        def chunk(t, carry):
            ci = n_ch - 1 - t
            r = pl.multiple_of(ci * A_CHUNK, A_CHUNK)
            _, vjp = jax.vjp(
                fn, p_ref[pl.ds(r, A_CHUNK), 0:256], p_ref[pl.ds(r, A_CHUNK), 256:512],
                p_ref[pl.ds(r, A_CHUNK), 512:768], p_ref[pl.ds(r, A_CHUNK), 768:1024],
                logits_v, ng_v, st_ref[ci])
            daq, daf, dai, dag, dlg, dng, dst = vjp((do_ref[pl.ds(r, A_CHUNK), :], dst_scr[...]))
            dp_ref[pl.ds(r, A_CHUNK), 0:256] = daq.astype(dp_ref.dtype)
            dp_ref[pl.ds(r, A_CHUNK), 256:512] = daf.astype(dp_ref.dtype)
            dp_ref[pl.ds(r, A_CHUNK), 512:768] = dai.astype(dp_ref.dtype)
            dp_ref[pl.ds(r, A_CHUNK), 768:1024] = dag.astype(dp_ref.dtype)
            dlg_ref[...] += dlg
            dng_ref[...] += dng
            dst_scr[...] = dst
            return carry

        lax.fori_loop(0, n_ch, chunk, 0)

    rev = lambda b, s: (b, n_s - 1 - s, 0)
    return pl.pallas_call(
        body, name=name, grid=(bsz, n_s),
        in_specs=[pl.BlockSpec((None, ts, GROUP_WIDTH), rev),
                  pl.BlockSpec((None, ts, 1024), rev),
                  pl.BlockSpec((None, n_ch, GROUP_WIDTH, GROUP_WIDTH), lambda b, s: (b, n_s - 1 - s, 0, 0)),
                  pl.BlockSpec((8, GROUP_WIDTH), lambda b, s: (0, 0)),
                  pl.BlockSpec((1, GROUP_WIDTH), lambda b, s: (0, 0))],
        out_specs=[pl.BlockSpec((None, ts, 1024), rev),
                   pl.BlockSpec((8, GROUP_WIDTH), lambda b, s: (0, 0)),
                   pl.BlockSpec((1, GROUP_WIDTH), lambda b, s: (0, 0))],
        out_shape=[jax.ShapeDtypeStruct((bsz, seq, PACK_W), BF16),
                   jax.ShapeDtypeStruct((8, GROUP_WIDTH), F32), jax.ShapeDtypeStruct((1, GROUP_WIDTH), F32)],
        scratch_shapes=[pltpu.VMEM((GROUP_WIDTH, GROUP_WIDTH), F32)],
        compiler_params=_cparams(("arbitrary", "arbitrary")),
    )(dmo, proj, states, logits8, norm_g)


def _rms_fn(x, g):
    return x * lax.rsqrt(jnp.mean(x * x, axis=-1, keepdims=True) + RMS_EPS) * g


def _tile4(t):
    return jnp.concatenate([t, t, t, t], axis=1)


def _rope(x, c, s1, s2):
    w = x.shape[-1]
    return x * c + pltpu.roll(x, 32, axis=1) * s2 + pltpu.roll(x, w - 32, axis=1) * s1


def _rope_t(dy, c, s1, s2):
    w = dy.shape[-1]
    return dy * c + pltpu.roll(dy * s2, w - 32, axis=1) + pltpu.roll(dy * s1, 32, axis=1)


def _mla_pre(proj, qg, kvg, wq, wkv, tabs, name, ts=256):
    bsz, seq, _ = proj.shape

    def body(p_ref, qg_ref, kvg_ref, wq_ref, wkv_ref, c_ref, s1_ref, s2_ref, q_ref, kv_ref):
        nq = _rms_fn(p_ref[:, 0:256], qg_ref[...])
        nkv = _rms_fn(p_ref[:, 256:384], kvg_ref[...])
        c, s1, s2 = c_ref[...], s1_ref[...], s2_ref[...]
        qp = jnp.dot(nq.astype(BF16), wq_ref[...], preferred_element_type=F32)
        q_ref[...] = _rope(qp, _tile4(c), _tile4(s1), _tile4(s2)).astype(q_ref.dtype)
        kv = jnp.dot(nkv.astype(BF16), wkv_ref[...], preferred_element_type=F32)
        krr = _rope(p_ref[:, 384:512], c, s1, s2)
        zero = jnp.zeros_like(krr)
        kv_ref[...] = (kv + jnp.concatenate([krr, zero] * N_HEADS, axis=1)).astype(kv_ref.dtype)

    tab_spec = pl.BlockSpec((ts, LANES), lambda b, s: (s, 0))
    return pl.pallas_call(
        body, name=name, grid=(bsz, seq // ts),
        in_specs=[pl.BlockSpec((None, ts, 512), lambda b, s: (b, s, P_B // 512)),
                  _vec_spec(256), _vec_spec(128),
                  pl.BlockSpec((256, 512), lambda b, s: (0, 0)), pl.BlockSpec((128, 1024), lambda b, s: (0, 0)),
                  tab_spec, tab_spec, tab_spec],
        out_specs=[_row_spec(ts, 512), _row_spec(ts, 1024)],
        out_shape=[jax.ShapeDtypeStruct((bsz, seq, 512), BF16), jax.ShapeDtypeStruct((bsz, seq, 1024), BF16)],
        compiler_params=_cparams(("parallel", "parallel")),
    )(proj, qg, kvg, wq, wkv, *tabs)


def _mla_pre_bwd(dq, dkv, dproj, proj, qg, kvg, wq, wkv, tabs, name, ts=256):
    bsz, seq, _ = proj.shape

    def body(dq_ref, dkv_ref, dp_any, p_ref, qg_ref, kvg_ref, wq_ref, wkv_ref, c_ref, s1_ref, s2_ref,
             dp_ref, dqg_ref, dkvg_ref, dwq_ref, dwkv_ref):
        del dp_any
        first = jnp.logical_and(pl.program_id(0) == 0, pl.program_id(1) == 0)

        @pl.when(first)
        def _():
            dqg_ref[...] = jnp.zeros_like(dqg_ref)
            dkvg_ref[...] = jnp.zeros_like(dkvg_ref)
            dwq_ref[...] = jnp.zeros_like(dwq_ref)
            dwkv_ref[...] = jnp.zeros_like(dwkv_ref)

        c, s1, s2 = c_ref[...], s1_ref[...], s2_ref[...]
        nq, vjp_q = jax.vjp(_rms_fn, p_ref[:, 0:256], qg_ref[...])
        nkv, vjp_kv = jax.vjp(_rms_fn, p_ref[:, 256:384], kvg_ref[...])
        dqp = _rope_t(dq_ref[...], _tile4(c), _tile4(s1), _tile4(s2)).astype(BF16)
        dkv_v = dkv_ref[...]
        dkv_b = dkv_v.astype(BF16)
        tn = (((0,), (0,)), ((), ()))
        nt = (((1,), (1,)), ((), ()))
        dwq_ref[...] += lax.dot_general(nq.astype(BF16), dqp, tn, preferred_element_type=F32)
        dwkv_ref[...] += lax.dot_general(nkv.astype(BF16), dkv_b, tn, preferred_element_type=F32)
        dcq, dqg = vjp_q(lax.dot_general(dqp, wq_ref[...], nt, preferred_element_type=F32))
        dckv, dkvg = vjp_kv(lax.dot_general(dkv_b, wkv_ref[...], nt, preferred_element_type=F32))
        dqg_ref[...] += dqg
        dkvg_ref[...] += dkvg
        dk_sum = dkv_v[:, 0:128] + dkv_v[:, 256:384] + dkv_v[:, 512:640] + dkv_v[:, 768:896]
        lane = lax.broadcasted_iota(jnp.int32, dk_sum.shape, 1)
        dkr = jnp.where(lane >= 64, _rope_t(dk_sum, c, s1, s2), 0.0)
        dp_ref[:, 0:256] = dcq.astype(dp_ref.dtype)
        dp_ref[:, 256:384] = dckv.astype(dp_ref.dtype)
        dp_ref[:, 384:512] = dkr.astype(dp_ref.dtype)

    tab_spec = pl.BlockSpec((ts, LANES), lambda b, s: (s, 0))
    const = lambda shape: pl.BlockSpec(shape, lambda b, s: (0, 0))
    return pl.pallas_call(
        body, name=name, grid=(bsz, seq // ts),
        in_specs=[_row_spec(ts, 512), _row_spec(ts, 1024), pl.BlockSpec(memory_space=pl.ANY),
                  pl.BlockSpec((None, ts, 512), lambda b, s: (b, s, P_B // 512)),
                  _vec_spec(256), _vec_spec(128), const((256, 512)), const((128, 1024)),
                  tab_spec, tab_spec, tab_spec],
        out_specs=[pl.BlockSpec((None, ts, 512), lambda b, s: (b, s, P_B // 512)),
                   _vec_spec(256), _vec_spec(128), const((256, 512)), const((128, 1024))],
        out_shape=[jax.ShapeDtypeStruct(dproj.shape, dproj.dtype), jax.ShapeDtypeStruct((1, 256), F32),
                   jax.ShapeDtypeStruct((1, 128), F32), jax.ShapeDtypeStruct((256, 512), F32),
                   jax.ShapeDtypeStruct((128, 1024), F32)],
        input_output_aliases={2: 0},
        compiler_params=_cparams(("arbitrary", "arbitrary")),
    )(dq, dkv, dproj, proj, qg, kvg, wq, wkv, *tabs)


def _fox_gate(proj, bf, name):
    bsz, seq, _ = proj.shape
    n_blk = seq // LANES

    def body(x_ref, bf_ref, f_ref):
        r_i = lax.broadcasted_iota(jnp.int32, (LANES, LANES), 0)
        c_i = lax.broadcasted_iota(jnp.int32, (LANES, LANES), 1)
        tril = (r_i >= c_i).astype(F32)
        bias = bf_ref[...]

        def blk(i, carry):
            r = pl.multiple_of(i * LANES, LANES)
            lf = _log_sigmoid(x_ref[pl.ds(r, LANES), :] + bias)
            f_ref[pl.ds(r, LANES), :] = jnp.dot(tril, lf, precision=HI, preferred_element_type=F32) + carry
            return carry + jnp.sum(lf, axis=0, keepdims=True)

        lax.fori_loop(0, n_blk, blk, jnp.zeros((1, LANES), F32))

    return pl.pallas_call(
        body, name=name, grid=(bsz,),
        in_specs=[pl.BlockSpec((None, seq, LANES), lambda b: (b, 0, P_CF // LANES)),
                  pl.BlockSpec((1, LANES), lambda b: (0, 0))],
        out_specs=pl.BlockSpec((None, seq, LANES), lambda b: (b, 0, 0)),
        out_shape=jax.ShapeDtypeStruct((bsz, seq, LANES), F32),
        compiler_params=_cparams(("parallel",)),
    )(proj, bf)


def _fox_gate_bwd(dcum, dproj, proj, bf, name):
    bsz, seq, _ = proj.shape
    n_blk = seq // LANES

    def body(dc_ref, dp_any, x_ref, bf_ref, dp_ref, dbf_ref):
        del dp_any

        @pl.when(pl.program_id(0) == 0)
        def _():
            dbf_ref[...] = jnp.zeros_like(dbf_ref)

        r_i = lax.broadcasted_iota(jnp.int32, (LANES, LANES), 0)
        c_i = lax.broadcasted_iota(jnp.int32, (LANES, LANES), 1)
        triu = (r_i <= c_i).astype(F32)
        bias = bf_ref[...]

        def blk(t, carry):
            tail, dbf = carry
            r = pl.multiple_of((n_blk - 1 - t) * LANES, LANES)
            dc = dc_ref[pl.ds(r, LANES), :]
            dlf = jnp.dot(triu, dc, precision=HI, preferred_element_type=F32) + tail
            dx = dlf * (1.0 - jax.nn.sigmoid(x_ref[pl.ds(r, LANES), :] + bias))
            dp_ref[pl.ds(r, LANES), :] = dx.astype(dp_ref.dtype)
            return tail + jnp.sum(dc, axis=0, keepdims=True), dbf + jnp.sum(dx, axis=0, keepdims=True)

        z = jnp.zeros((1, LANES), F32)
        _, dbf = lax.fori_loop(0, n_blk, blk, (z, z))
        dbf_ref[...] += dbf

    return pl.pallas_call(
        body, name=name, grid=(bsz,),
        in_specs=[pl.BlockSpec((None, seq, LANES), lambda b: (b, 0, 0)), pl.BlockSpec(memory_space=pl.ANY),
                  pl.BlockSpec((None, seq, LANES), lambda b: (b, 0, P_CF // LANES)),
                  pl.BlockSpec((1, LANES), lambda b: (0, 0))],
        out_specs=[pl.BlockSpec((None, seq, LANES), lambda b: (b, 0, P_CF // LANES)),
                   pl.BlockSpec((1, LANES), lambda b: (0, 0))],
        out_shape=[jax.ShapeDtypeStruct(dproj.shape, dproj.dtype), jax.ShapeDtypeStruct((1, LANES), F32)],
        input_output_aliases={1: 0},
        compiler_params=_cparams(("arbitrary",)),
    )(dcum, dproj, proj, bf)


def _gate_terms(fc_ref, fr_ref, h, tq, tk):
    lane = lax.broadcasted_iota(jnp.int32, (tq, LANES), 1)
    fcol = jnp.sum(jnp.where(lane == h, fc_ref[...], 0.0), axis=1, keepdims=True)
    sub = lax.broadcasted_iota(jnp.int32, (8, tk), 0)
    frow = jnp.sum(jnp.where(sub == h, fr_ref[...], 0.0), axis=0, keepdims=True)
    return fcol - frow


def _scores(q_ref, k_ref, gate_refs, scale, h, diag, tq, tk):
    s = lax.dot_general(q_ref[...].astype(BF16), k_ref[...].astype(BF16), (((1,), (1,)), ((), ())),
                        preferred_element_type=F32) * scale
    if gate_refs is not None:
        s = s + _gate_terms(gate_refs[0], gate_refs[1], h, tq, tk)
    r_i = lax.broadcasted_iota(jnp.int32, (tq, tk), 0)
    c_i = lax.broadcasted_iota(jnp.int32, (tq, tk), 1)
    return jnp.where(jnp.logical_or(jnp.logical_not(diag), c_i <= r_i), s, NEG)


def _attn_fwd(qa, q0, kva, kv0, mo, o0, gates, scale, name, tq=256):
    bsz, seq, _ = qa.shape
    n_q = seq // tq
    gated = gates is not None

    def body(*refs):
        q_ref, k_ref, v_ref = refs[:3]
        gate_refs = refs[3:5] if gated else None
        o_ref, lse_ref, m_s, l_s, acc_s = refs[-5:]
        h, i, j = pl.program_id(1), pl.program_id(2), pl.program_id(3)

        @pl.when(j == 0)
        def _():
            m_s[...] = jnp.full_like(m_s, NEG)
            l_s[...] = jnp.zeros_like(l_s)
            acc_s[...] = jnp.zeros_like(acc_s)

        @pl.when(j <= i)
        def _():
            s = _scores(q_ref, k_ref, gate_refs, scale, h, j == i, tq, tq)
            m_prev = m_s[...]
            m_new = jnp.maximum(m_prev, jnp.max(s, axis=1, keepdims=True))
            alpha = jnp.exp(m_prev - m_new)
            p = jnp.exp(s - m_new)
            l_s[...] = alpha * l_s[...] + jnp.sum(p, axis=1, keepdims=True)
            acc_s[...] = alpha * acc_s[...] + jnp.dot(p.astype(BF16), v_ref[...].astype(BF16),
                                                      preferred_element_type=F32)
            m_s[...] = m_new

        @pl.when(j == i)
        def _():
            o_ref[...] = (acc_s[...] / l_s[...]).astype(o_ref.dtype)
            lse_ref[...] = m_s[...] + jnp.log(l_s[...])

    blk = (None, tq, LANES)
    in_specs = [pl.BlockSpec(blk, lambda b, h, i, j: (b, i, q0 + h)),
                pl.BlockSpec(blk, lambda b, h, i, j: (b, jnp.minimum(j, i), kv0 + 2 * h)),
                pl.BlockSpec(blk, lambda b, h, i, j: (b, jnp.minimum(j, i), kv0 + 2 * h + 1))]
    args = [qa, kva, kva]
    if gated:
        in_specs += [pl.BlockSpec(blk, lambda b, h, i, j: (b, i, 0)),
                     pl.BlockSpec((None, 8, tq), lambda b, h, i, j: (b, 0, jnp.minimum(j, i)))]
        args += list(gates)
    in_specs.append(pl.BlockSpec(memory_space=pl.ANY))
    args.append(mo)
    return pl.pallas_call(
        body, name=name, grid=(bsz, N_HEADS, n_q, n_q), in_specs=in_specs,
        out_specs=[pl.BlockSpec(blk, lambda b, h, i, j: (b, i, o0 + h)),
                   pl.BlockSpec((None, None, tq, 1), lambda b, h, i, j: (b, h, i, 0))],
        out_shape=[jax.ShapeDtypeStruct(mo.shape, mo.dtype), jax.ShapeDtypeStruct((bsz, N_HEADS, seq, 1), F32)],
        scratch_shapes=[pltpu.VMEM((tq, 1), F32), pltpu.VMEM((tq, 1), F32), pltpu.VMEM((tq, LANES), F32)],
        input_output_aliases={len(args) - 1: 0},
        compiler_params=_cparams(("parallel", "parallel", "parallel", "arbitrary")),
    )(*args)


def _attn_bwd_q(qa, q0, kva, kv0, mo, dmo, o0, lse, gates, scale, out, out0, name, tq=256):
    bsz, seq, _ = qa.shape
    n_q = seq // tq
    gated = gates is not None
    aliased = not isinstance(out, jax.ShapeDtypeStruct)

    def body(*refs):
        q_ref, k_ref, v_ref, o_ref, do_ref, lse_ref = refs[:6]
        gate_refs = refs[6:8] if gated else None
        dq_ref, delta_ref, dfq_ref, acc_s, dl_s, df_s = refs[-6:]
        h, i, j = pl.program_id(1), pl.program_id(2), pl.program_id(3)

        @pl.when(j == 0)
        def _():
            acc_s[...] = jnp.zeros_like(acc_s)
            df_s[...] = jnp.zeros_like(df_s)
            dl_s[...] = jnp.sum(do_ref[...] * o_ref[...].astype(F32), axis=1, keepdims=True)

        @pl.when(j <= i)
        def _():
            s = _scores(q_ref, k_ref, gate_refs, scale, h, j == i, tq, tq)
            p = jnp.exp(s - lse_ref[...])
            dp = lax.dot_general(do_ref[...].astype(BF16), v_ref[...].astype(BF16), (((1,), (1,)), ((), ())),
                                 preferred_element_type=F32)
            ds = p * (dp - dl_s[...])
            acc_s[...] += jnp.dot(ds.astype(BF16), k_ref[...].astype(BF16), preferred_element_type=F32)
            df_s[...] += jnp.sum(ds, axis=1, keepdims=True)

        @pl.when(j == i)
        def _():
            dq_ref[...] = (acc_s[...] * scale).astype(dq_ref.dtype)
            delta_ref[...] = dl_s[...]
            dfq_ref[...] = df_s[...]

    blk = (None, tq, LANES)
    col = pl.BlockSpec((None, None, tq, 1), lambda b, h, i, j: (b, h, i, 0))
    in_specs = [pl.BlockSpec(blk, lambda b, h, i, j: (b, i, q0 + h)),
                pl.BlockSpec(blk, lambda b, h, i, j: (b, jnp.minimum(j, i), kv0 + 2 * h)),
                pl.BlockSpec(blk, lambda b, h, i, j: (b, jnp.minimum(j, i), kv0 + 2 * h + 1)),
                pl.BlockSpec(blk, lambda b, h, i, j: (b, i, o0 + h)),
                pl.BlockSpec(blk, lambda b, h, i, j: (b, i, o0 + h)), col]
    args = [qa, kva, kva, mo, dmo, lse]
    if gated:
        in_specs += [pl.BlockSpec(blk, lambda b, h, i, j: (b, i, 0)),
                     pl.BlockSpec((None, 8, tq), lambda b, h, i, j: (b, 0, jnp.minimum(j, i)))]
        args += list(gates)
    aliases = {}
    if aliased:
        in_specs.append(pl.BlockSpec(memory_space=pl.ANY))
        args.append(out)
        aliases = {len(args) - 1: 0}
    vec = jax.ShapeDtypeStruct((bsz, N_HEADS, seq, 1), F32)
    return pl.pallas_call(
        body, name=name, grid=(bsz, N_HEADS, n_q, n_q), in_specs=in_specs,
        out_specs=[pl.BlockSpec(blk, lambda b, h, i, j: (b, i, out0 + h)), col, col],
        out_shape=[jax.ShapeDtypeStruct(out.shape, out.dtype), vec, vec],
        scratch_shapes=[pltpu.VMEM((tq, LANES), F32), pltpu.VMEM((tq, 1), F32), pltpu.VMEM((tq, 1), F32)],
        input_output_aliases=aliases,
        compiler_params=_cparams(("parallel", "parallel", "parallel", "arbitrary")),
    )(*args)


def _attn_bwd_kv(qa, q0, kva, kv0, dmo, o0, lse, delta, gates, scale, out, out0, name, tq=256):
    bsz, seq, _ = qa.shape
    n_q = seq // tq
    gated = gates is not None
    aliased = not isinstance(out, jax.ShapeDtypeStruct)

    def body(*refs):
        q_ref, k_ref, v_ref, do_ref, lse_ref, dl_ref = refs[:6]
        gate_refs = refs[6:8] if gated else None
        dkv_ref, dfk_ref, dk_s, dv_s, df_s = refs[-5:]
        h, j, i = pl.program_id(1), pl.program_id(2), pl.program_id(3)

        @pl.when(i == 0)
        def _():
            dk_s[...] = jnp.zeros_like(dk_s)
            dv_s[...] = jnp.zeros_like(dv_s)
            df_s[...] = jnp.zeros_like(df_s)

        @pl.when(i >= j)
        def _():
            s = _scores(q_ref, k_ref, gate_refs, scale, h, j == i, tq, tq)
            p = jnp.exp(s - lse_ref[...])
            do_b = do_ref[...].astype(BF16)
            dp = lax.dot_general(do_b, v_ref[...].astype(BF16), (((1,), (1,)), ((), ())),
                                 preferred_element_type=F32)
            ds = p * (dp - dl_ref[...])
            tn = (((0,), (0,)), ((), ()))
            dv_s[...] += lax.dot_general(p.astype(BF16), do_b, tn, preferred_element_type=F32)
            dk_s[...] += lax.dot_general(ds.astype(BF16), q_ref[...].astype(BF16), tn, preferred_element_type=F32)
            df_s[...] -= jnp.sum(ds, axis=0, keepdims=True)

        @pl.when(i == n_q - 1)
        def _():
            dkv_ref[:, 0:LANES] = (dk_s[...] * scale).astype(dkv_ref.dtype)
            dkv_ref[:, LANES:2 * LANES] = dv_s[...].astype(dkv_ref.dtype)
            dfk_ref[...] = df_s[...]

    blk = (None, tq, LANES)
    col = pl.BlockSpec((None, None, tq, 1), lambda b, h, j, i: (b, h, jnp.maximum(i, j), 0))
    in_specs = [pl.BlockSpec(blk, lambda b, h, j, i: (b, jnp.maximum(i, j), q0 + h)),
                pl.BlockSpec(blk, lambda b, h, j, i: (b, j, kv0 + 2 * h)),
                pl.BlockSpec(blk, lambda b, h, j, i: (b, j, kv0 + 2 * h + 1)),
                pl.BlockSpec(blk, lambda b, h, j, i: (b, jnp.maximum(i, j), o0 + h)), col, col]
    args = [qa, kva, kva, dmo, lse, delta]
    if gated:
        in_specs += [pl.BlockSpec(blk, lambda b, h, j, i: (b, jnp.maximum(i, j), 0)),
                     pl.BlockSpec((None, 8, tq), lambda b, h, j, i: (b, 0, j))]
        args += list(gates)
    aliases = {}
    if aliased:
        in_specs.append(pl.BlockSpec(memory_space=pl.ANY))
        args.append(out)
        aliases = {len(args) - 1: 0}
    return pl.pallas_call(
        body, name=name, grid=(bsz, N_HEADS, n_q, n_q), in_specs=in_specs,
        out_specs=[pl.BlockSpec((None, tq, 2 * LANES), lambda b, h, j, i: (b, j, out0 + h)),
                   pl.BlockSpec((None, None, 1, tq), lambda b, h, j, i: (b, h, 0, j))],
        out_shape=[jax.ShapeDtypeStruct(out.shape, out.dtype), jax.ShapeDtypeStruct((bsz, N_HEADS, 1, seq), F32)],
        scratch_shapes=[pltpu.VMEM((tq, LANES), F32), pltpu.VMEM((tq, LANES), F32), pltpu.VMEM((1, tq), F32)],
        input_output_aliases=aliases,
        compiler_params=_cparams(("parallel", "parallel", "parallel", "arbitrary")),
    )(*args)


def _gmlp_fn(uv, lng, lnb, ws, bst):
    u = jax.nn.gelu(uv[:, 0:GROUP_WIDTH])
    gv = jax.nn.gelu(uv[:, GROUP_WIDTH:2 * GROUP_WIDTH])
    mu = jnp.mean(gv, axis=-1, keepdims=True)
    vc = gv - mu
    var = jnp.mean(vc * vc, axis=-1, keepdims=True)
    vln = vc * lax.rsqrt(var + LN_EPS) * lng + lnb
    r_i = lax.broadcasted_iota(jnp.int32, (D_CHUNK, D_CHUNK), 0)
    c_i = lax.broadcasted_iota(jnp.int32, (D_CHUNK, D_CHUNK), 1)
    lane_g = lax.broadcasted_iota(jnp.int32, (D_CHUNK, GROUP_WIDTH), 1) // HEAD_DIM
    e_r = lax.broadcasted_iota(jnp.int32, (LANES, GROUP_WIDTH), 0)
    e_c = lax.broadcasted_iota(jnp.int32, (LANES, GROUP_WIDTH), 1)
    expand = (e_r == e_c // HEAD_DIM).astype(F32)
    mixed = jnp.dot(bst, expand, precision=HI, preferred_element_type=F32)
    for g in range(4):
        w = jnp.where(r_i >= c_i, ws[g], 0.0)
        mixed = mixed + jnp.where(lane_g == g, jnp.dot(w, vln, precision=HI, preferred_element_type=F32), 0.0)
    return u * mixed


def _gmlp_fwd(proj, mo, lng, lnb, ws, bst, name):
    bsz, seq, _ = proj.shape

    def body(p_ref, mo_any, lng_ref, lnb_ref, ws_ref, bst_ref, o_ref):
        del mo_any
        o_ref[...] = _gmlp_fn(p_ref[...], lng_ref[...], lnb_ref[...], ws_ref[...], bst_ref[...]).astype(o_ref.dtype)

    return pl.pallas_call(
        body, name=name, grid=(bsz, seq // D_CHUNK),
        in_specs=[pl.BlockSpec((None, D_CHUNK, 512), lambda b, s: (b, s, P_D // 512)),
                  pl.BlockSpec(memory_space=pl.ANY), _vec_spec(256), _vec_spec(256),
                  pl.BlockSpec((4, D_CHUNK, D_CHUNK), lambda b, s: (0, 0, 0)),
                  pl.BlockSpec((D_CHUNK, LANES), lambda b, s: (0, 0))],
        out_specs=pl.BlockSpec((None, D_CHUNK, GROUP_WIDTH), lambda b, s: (b, s, 1280 // GROUP_WIDTH)),
        out_shape=jax.ShapeDtypeStruct(mo.shape, mo.dtype),
        input_output_aliases={1: 0},
        compiler_params=_cparams(("parallel", "parallel")),
    )(proj, mo, lng, lnb, ws, bst)


def _gmlp_bwd(dmo, dproj, proj, lng, lnb, ws, bst, name):
    bsz, seq, _ = proj.shape

    def body(do_ref, dp_any, p_ref, lng_ref, lnb_ref, ws_ref, bst_ref, dp_ref, dlg_ref, dlb_ref, dws_ref, dbst_ref):
        del dp_any
        first = jnp.logical_and(pl.program_id(0) == 0, pl.program_id(1) == 0)

        @pl.when(first)
        def _():
            dlg_ref[...] = jnp.zeros_like(dlg_ref)
            dlb_ref[...] = jnp.zeros_like(dlb_ref)
            dws_ref[...] = jnp.zeros_like(dws_ref)
            dbst_ref[...] = jnp.zeros_like(dbst_ref)

        _, vjp = jax.vjp(_gmlp_fn, p_ref[...], lng_ref[...], lnb_ref[...], ws_ref[...], bst_ref[...])
        duv, dlg, dlb, dws, dbst = vjp(do_ref[...])
        dp_ref[...] = duv.astype(dp_ref.dtype)
        dlg_ref[...] += dlg
        dlb_ref[...] += dlb
        dws_ref[...] += dws
        dbst_ref[...] += dbst

    const2 = lambda shape: pl.BlockSpec(shape, lambda b, s: (0,) * len(shape))
    return pl.pallas_call(
        body, name=name, grid=(bsz, seq // D_CHUNK),
        in_specs=[pl.BlockSpec((None, D_CHUNK, GROUP_WIDTH), lambda b, s: (b, s, 1280 // GROUP_WIDTH)),
                  pl.BlockSpec(memory_space=pl.ANY),
                  pl.BlockSpec((None, D_CHUNK, 512), lambda b, s: (b, s, P_D // 512)),
                  _vec_spec(256), _vec_spec(256), const2((4, D_CHUNK, D_CHUNK)), const2((D_CHUNK, LANES))],
        out_specs=[pl.BlockSpec((None, D_CHUNK, 512), lambda b, s: (b, s, P_D // 512)),
                   _vec_spec(256), _vec_spec(256), const2((4, D_CHUNK, D_CHUNK)), const2((D_CHUNK, LANES))],
        out_shape=[jax.ShapeDtypeStruct(dproj.shape, dproj.dtype), jax.ShapeDtypeStruct((1, 256), F32),
                   jax.ShapeDtypeStruct((1, 256), F32), jax.ShapeDtypeStruct((4, D_CHUNK, D_CHUNK), F32),
                   jax.ShapeDtypeStruct((D_CHUNK, LANES), F32)],
        input_output_aliases={1: 0},
        compiler_params=_cparams(("arbitrary", "arbitrary")),
    )(dmo, dproj, proj, lng, lnb, ws, bst)


def _ada_fwd(c_all, ada_w, name):
    n_b = c_all.shape[0]
    depth, d, cols = ada_w.shape

    def body(c_ref, w_ref, o_ref):
        cv = c_ref[...]
        act = (cv * jax.nn.sigmoid(cv)).astype(BF16)
        o_ref[...] = jnp.dot(act, w_ref[...].astype(BF16), preferred_element_type=F32)

    return pl.pallas_call(
        body, name=name, grid=(depth,),
        in_specs=[pl.BlockSpec((n_b, d), lambda l: (0, 0)), pl.BlockSpec((None, d, cols), lambda l: (l, 0, 0))],
        out_specs=pl.BlockSpec((None, n_b, cols), lambda l: (l, 0, 0)),
        out_shape=jax.ShapeDtypeStruct((depth, n_b, cols), F32),
        compiler_params=_cparams(("parallel",)),
    )(c_all, ada_w)


def _ada_bwd(c_all, dmod_cols, dmod_full, name):
    n_b, d = c_all.shape
    depth, _, cols = dmod_cols.shape
    full = dmod_full.shape[-1]

    def body(c_ref, dm_ref, df_ref, gw_ref, gb_ref):
        cv = c_ref[...]
        act = (cv * jax.nn.sigmoid(cv)).astype(BF16)
        gw_ref[...] = lax.dot_general(act, dm_ref[...].astype(BF16), (((0,), (0,)), ((), ())),
                                      preferred_element_type=F32)
        gb_ref[...] = jnp.sum(df_ref[...], axis=0, keepdims=True)

    return pl.pallas_call(
        body, name=name, grid=(depth,),
        in_specs=[pl.BlockSpec((n_b, d), lambda l: (0, 0)), pl.BlockSpec((None, n_b, cols), lambda l: (l, 0, 0)),
                  pl.BlockSpec((None, n_b, full), lambda l: (l, 0, 0))],
        out_specs=[pl.BlockSpec((None, d, cols), lambda l: (l, 0, 0)),
                   pl.BlockSpec((None, 1, full), lambda l: (l, 0, 0))],
        out_shape=[jax.ShapeDtypeStruct((depth, d, cols), F32), jax.ShapeDtypeStruct((depth, 1, full), F32)],
        compiler_params=_cparams(("parallel",)),
    )(c_all, dmod_cols, dmod_full)


def _adamw(gparts, w, m, v, name):
    n_p, rows, cols = gparts.shape
    tr = rows
    for cand in (512, 256, 128, 64, 32, 16):
        if rows % cand == 0 and rows > cand:
            tr = cand
            break

    def body(g_ref, w_ref, m_ref, v_ref, go_ref, do_ref, mo_ref, vo_ref):
        g = g_ref[0].astype(F32)
        for p in range(1, n_p):
            g = g + g_ref[p].astype(F32)
        m_new = ADAM_B1 * m_ref[...] + (1.0 - ADAM_B1) * g
        v_new = ADAM_B2 * v_ref[...] + (1.0 - ADAM_B2) * (g * g)
        m_hat = m_new / (1.0 - ADAM_B1 ** ADAM_STEP)
        v_hat = v_new / (1.0 - ADAM_B2 ** ADAM_STEP)
        go_ref[...] = g
        do_ref[...] = -ADAM_LR * (m_hat / (jnp.sqrt(v_hat) + ADAM_EPS) + ADAM_WD * w_ref[...])
        mo_ref[...] = m_new
        vo_ref[...] = v_new

    spec = pl.BlockSpec((tr, cols), lambda i: (i, 0))
    shp = jax.ShapeDtypeStruct((rows, cols), F32)
    return pl.pallas_call(
        body, name=name, grid=(rows // tr,),
        in_specs=[pl.BlockSpec((n_p, tr, cols), lambda i: (0, i, 0)), spec, spec, spec],
        out_specs=[spec, spec, spec, spec], out_shape=[shp, shp, shp, shp],
        compiler_params=_cparams(("parallel",)),
    )(gparts, w, m, v)


def _sum_parts(parts, name):
    n_p, rows, cols = parts.shape
    tr = 256 if rows % 256 == 0 else rows

    def body(p_ref, o_ref):
        acc = p_ref[0]
        for p in range(1, n_p):
            acc = acc + p_ref[p]
        o_ref[...] = acc

    return pl.pallas_call(
        body, name=name, grid=(rows // tr,),
        in_specs=[pl.BlockSpec((n_p, tr, cols), lambda i: (0, i, 0))],
        out_specs=pl.BlockSpec((tr, cols), lambda i: (i, 0)),
        out_shape=jax.ShapeDtypeStruct((rows, cols), F32),
        compiler_params=_cparams(("parallel",)),
    )(parts)


def _exchange(ins, out_shapes, plan, name):
    n_in, n_out, n_cp = len(ins), len(out_shapes), len(plan)
    flips = [(fx, fy, fc) for fx in (0, 1) for fy in (0, 1) for fc in (0, 1)][1:]

    def body(*refs):
        in_refs, out_refs = refs[:n_in], refs[n_in:n_in + n_out]
        send_sems, recv_sems, loc_sems = refs[n_in + n_out:]
        x, y, c = lax.axis_index("x"), lax.axis_index("y"), lax.axis_index("c")
        me = 4 * x + 2 * y + c
        peers = []
        for fx, fy, fc in flips:
            px, py, pc = (1 - x if fx else x), (1 - y if fy else y), (1 - c if fc else c)
            peers.append(((px, py, pc), 4 * px + 2 * py + pc))

        def sel(ref, idx):
            return ref.at[idx] if idx else ref

        def remote(n, k, src_dev_slot, dst_for):
            i, in_sel, o, out_sel = plan[n]
            dev, idx = peers[k]
            return pltpu.make_async_remote_copy(
                src_ref=sel(in_refs[i], in_sel(dst_for)), dst_ref=sel(out_refs[o], out_sel(src_dev_slot)),
                send_sem=send_sems.at[n, k], recv_sem=recv_sems.at[n, k],
                device_id=dev, device_id_type=pl.DeviceIdType.MESH)

        local = []
        for n, (i, in_sel, o, out_sel) in enumerate(plan):
            cp = pltpu.make_async_copy(sel(in_refs[i], in_sel(me)), sel(out_refs[o], out_sel(me)), loc_sems.at[n])
            cp.start()
            local.append(cp)
        sends = []
        for k in range(len(flips)):
            for n in range(n_cp):
                cp = remote(n, k, me, peers[k][1])
                cp.start()
                sends.append(cp)
        for k in range(len(flips)):
            for n in range(n_cp):
                remote(n, k, peers[k][1], me).wait_recv()
        for cp in sends:
            cp.wait_send()
        for cp in local:
            cp.wait()

    any_spec = pl.BlockSpec(memory_space=pl.ANY)
    return pl.pallas_call(
        body, name=name,
        in_specs=[any_spec] * n_in, out_specs=[any_spec] * n_out, out_shape=list(out_shapes),
        scratch_shapes=[pltpu.SemaphoreType.DMA((n_cp, N_DEV - 1)), pltpu.SemaphoreType.DMA((n_cp, N_DEV - 1)),
                        pltpu.SemaphoreType.DMA((n_cp,))],
    )(*ins)


def _all_gather(arrs, name):
    shapes = [jax.ShapeDtypeStruct((N_DEV,) + a.shape, a.dtype) for a in arrs]
    plan = [(i, (lambda p: ()), i, (lambda s: (s,))) for i in range(len(arrs))]
    return _exchange(list(arrs), shapes, plan, name)


def _reduce_scatter_push(groups, name):
    ins, shapes, plan = [], [], []
    for w, layers in enumerate(groups):
        shapes.append(jax.ShapeDtypeStruct((N_DEV, len(layers)) + layers[0].shape[1:], layers[0].dtype))
        for l, arr in enumerate(layers):
            plan.append((len(ins), (lambda p: (p,)), w, (lambda s, l=l: (s, l))))
            ins.append(arr)
    return _exchange(ins, shapes, plan, name)


def _ffn_fwd(x, mod, w_in, w_out, lng, lnb, rows, tag):
    bsz, seq, d = x.shape
    t = bsz * seq
    h = _modulate(x, mod, rows[0], rows[1], f"modulate_{tag}")
    z = _matmul(h.reshape(1, t, d), w_in, mode="nn", group_out=True, out_dtype=F32, tm=512, tk=d,
                name=f"ffn_in_{tag}")
    a = _swiglu(z, f"swiglu_{tag}")
    f = _matmul(a, w_out, mode="nn", group_out=False, out_dtype=F32, tm=512, tk=a.shape[2],
                name=f"ffn_out_{tag}").reshape(bsz, seq, d)
    y = _res_ln(x, f, mod, lng, lnb, rows[2], 0.5, f"res_ln_{tag}")
    return y, (x, h, z, a, f)


def _ffn_bwd(dy, saved, mod, w_in, w_out, lng, lnb, rows, tag):
    x, h, z, a, f = saved
    bsz, seq, d = x.shape
    t = bsz * seq
    dx_res, df, dgate, dlg, dlb = _res_ln_bwd(dy, x, f, mod, lng, lnb, rows[2], 0.5, f"res_ln_bwd_{tag}")
    df2 = df.reshape(1, t, d)
    da = _matmul(df2, w_out, mode="nt", group_out=True, out_dtype=F32, tm=512, tk=d, name=f"ffn_out_dx_{tag}")
    dw_out = _matmul(a, df2, mode="tn", group_out=True, out_dtype=BF16, tm=a.shape[2], tk=512,
                     name=f"ffn_out_dw_{tag}")
    dz = _swiglu_bwd(da, z, f"swiglu_bwd_{tag}")
    dh = _matmul(dz, w_in, mode="nt", group_out=False, out_dtype=F32, tm=512, tk=dz.shape[2],
                 name=f"ffn_in_dx_{tag}").reshape(bsz, seq, d)
    dw_in = _matmul(h.reshape(1, t, d), dz, mode="tn", group_out=True, out_dtype=BF16, tm=512, tk=512,
                    name=f"ffn_in_dw_{tag}")
    dx, dsh, dsc = _modulate_bwd(dh, x, mod, dx_res, rows[1], f"modulate_bwd_{tag}")
    return dx, (dsh, dsc, dgate), dw_in, dw_out, dlg, dlb


def _mixer_fwd(x, mod, wts, small, lng, lnb, layer, tabs):
    bsz, seq, d = x.shape
    t = bsz * seq
    h = _modulate(x, mod, 3, 4, "modulate_mix")
    proj = _matmul(h.reshape(1, t, d), wts["mix_in"][None], mode="nn", group_out=True, out_dtype=F32, tm=256, tk=d,
                   name="mix_in").reshape(bsz, seq, PACK_W)
    mo, states = _hgrn_fwd(proj, small["lb_logits8"], small["hgrn_norm_g"], layer, f"hgrn_fwd_l{layer}")
    q, kv = _mla_pre(proj, small["q_norm_g"], small["kv_norm_g"], wts["uq"], wts["ukv"], tabs, "mla_pre")
    mla_scale = float((B_NOPE + B_ROPE) ** -0.5)
    mo, lse_b = _attn_fwd(q, 0, kv, 0, mo, 2, None, mla_scale, "mla_attn_fwd")
    fg = _fox_gate(proj, small["fox_b_f"], "fox_gate")
    gates = (fg, jnp.swapaxes(fg[:, :, 0:8], 1, 2))
    fox_scale = float(HEAD_DIM ** -0.5)
    mo, lse_c = _attn_fwd(proj, P_CQ // LANES, proj, P_CKV // LANES, mo, 6, gates, fox_scale, "fox_attn_fwd")
    mo = _gmlp_fwd(proj, mo, small["gmlp_ln_g"], small["gmlp_ln_b"], small["gmlp_w_s"], small["gmlp_bst"],
                   "gmlp_fwd")
    mixed = _matmul(mo.reshape(1, t, MO_W), wts["mix_out"][None], mode="nn", group_out=True, out_dtype=F32,
                    tm=512, tk=MO_W, name="mix_out").reshape(bsz, seq, d)
    y = _res_ln(x, mixed, mod, lng, lnb, 5, 1.0, "res_ln_mix")
    return y, (x, h, proj, mo, states, q, kv, lse_b, gates, lse_c, mixed)


def _mixer_bwd(dy, saved, mod, wts, small, lng, lnb, layer, tabs):
    x, h, proj, mo, states, q, kv, lse_b, gates, lse_c, mixed = saved
    bsz, seq, d = x.shape
    t = bsz * seq
    dx_res, dmixed, dgate, dlg, dlb = _res_ln_bwd(dy, x, mixed, mod, lng, lnb, 5, 1.0, "res_ln_bwd_mix")
    dm2 = dmixed.reshape(1, t, d)
    dmo = _matmul(dm2, wts["mix_out"][None], mode="nt", group_out=True, out_dtype=F32, tm=512, tk=d,
                  name="mix_out_dx").reshape(bsz, seq, MO_W)
    dw_out = _matmul(mo.reshape(1, t, MO_W), dm2, mode="tn", group_out=True, out_dtype=F32, tm=512, tk=512,
                     name="mix_out_dw")[0]
    g = {}
    dproj, g["lb_logits8"], g["hgrn_norm_g"] = _hgrn_bwd(dmo, proj, states, small["lb_logits8"],
                                                         small["hgrn_norm_g"], layer, f"hgrn_bwd_l{layer}")
    mla_scale = float((B_NOPE + B_ROPE) ** -0.5)
    dq, delta_b, _ = _attn_bwd_q(q, 0, kv, 0, mo, dmo, 2, lse_b, None, mla_scale,
                                 jax.ShapeDtypeStruct((bsz, seq, 512), F32), 0, "mla_attn_bwd_q")
    dkv, _ = _attn_bwd_kv(q, 0, kv, 0, dmo, 2, lse_b, delta_b, None, mla_scale,
                          jax.ShapeDtypeStruct((bsz, seq, 1024), F32), 0, "mla_attn_bwd_kv")
    dproj, g["q_norm_g"], g["kv_norm_g"], g["uq"], g["ukv"] = _mla_pre_bwd(
        dq, dkv, dproj, proj, small["q_norm_g"], small["kv_norm_g"], wts["uq"], wts["ukv"], tabs, "mla_pre_bwd")
    fox_scale = float(HEAD_DIM ** -0.5)
    dproj, delta_c, dfq = _attn_bwd_q(proj, P_CQ // LANES, proj, P_CKV // LANES, mo, dmo, 6, lse_c, gates,
                                      fox_scale, dproj, P_CQ // LANES, "fox_attn_bwd_q")
    dproj, dfk = _attn_bwd_kv(proj, P_CQ // LANES, proj, P_CKV // LANES, dmo, 6, lse_c, delta_c, gates, fox_scale,
                              dproj, P_CKV // (2 * LANES), "fox_attn_bwd_kv")
    dcum = jnp.swapaxes(dfq[..., 0], 1, 2) + jnp.swapaxes(dfk[:, :, 0, :], 1, 2)
    dcum = jnp.pad(dcum, ((0, 0), (0, 0), (0, LANES - N_HEADS)))
    dproj, g["fox_b_f"] = _fox_gate_bwd(dcum, dproj, proj, small["fox_b_f"], "fox_gate_bwd")
    dproj, g["gmlp_ln_g"], g["gmlp_ln_b"], g["gmlp_w_s"], g["gmlp_bst"] = _gmlp_bwd(
        dmo, dproj, proj, small["gmlp_ln_g"], small["gmlp_ln_b"], small["gmlp_w_s"], small["gmlp_bst"], "gmlp_bwd")
    dp2 = dproj.reshape(1, t, PACK_W)
    dh = _matmul(dp2, wts["mix_in"][None], mode="nt", group_out=True, out_dtype=F32, tm=256, tk=PACK_W,
                 name="mix_in_dx").reshape(bsz, seq, d)
    dw_in = _matmul(h.reshape(1, t, d), dp2, mode="tn", group_out=True, out_dtype=BF16, tm=256, tk=512,
                    name="mix_in_dw")[0]
    dx, dsh, dsc = _modulate_bwd(dh, x, mod, dx_res, 4, "modulate_bwd_mix")
    return dx, (dsh, dsc, dgate), dw_in, dw_out, g, dlg, dlb


def _small_views(p, layer):
    return {
        "lb_logits8": jnp.pad(p["hgrn_lb_logits"], ((0, 8 - DEPTH), (0, 0))),
        "hgrn_norm_g": p["hgrn_norm_g"][layer][None],
        "q_norm_g": p["mla_q_norm_g"][layer][None],
        "kv_norm_g": p["mla_kv_norm_g"][layer][None],
        "fox_b_f": jnp.pad(p["fox_b_f"][layer][None], ((0, 0), (0, LANES - N_HEADS))),
        "gmlp_ln_g": p["gmlp_ln_g"][layer][None],
        "gmlp_ln_b": p["gmlp_ln_b"][layer][None],
        "gmlp_w_s": p["gmlp_w_s"][layer],
        "gmlp_bst": jnp.pad(p["gmlp_b_s"][layer].T, ((0, 0), (0, LANES - N_HEADS))),
    }


def _local_step(x, mod, target, full, p):
    bsz, seq, d = x.shape
    tabs = _rope_tables(seq)
    saved = []
    for l in range(DEPTH):
        w, sm = full[l], _small_views(p, l)
        lng, lnb = p["ln_g"][l], p["ln_b"][l]
        x, s1 = _ffn_fwd(x, mod[l], w["ffn1_in"], w["ffn1_out"], lng[0:1], lnb[0:1], (0, 1, 2), "ffn1")
        x, s2 = _mixer_fwd(x, mod[l], w, sm, lng[1:2], lnb[1:2], l, tabs)
        x, s3 = _ffn_fwd(x, mod[l], w["ffn2_in"], w["ffn2_out"], lng[2:3], lnb[2:3], (6, 7, 8), "ffn2")
        saved.append((s1, s2, s3))
    dx, loss = _loss_head(x, target, "loss_head")
    big, small, dmods = [None] * DEPTH, [None] * DEPTH, [None] * DEPTH
    for l in reversed(range(DEPTH)):
        w, sm = full[l], _small_views(p, l)
        lng, lnb = p["ln_g"][l], p["ln_b"][l]
        s1, s2, s3 = saved[l]
        dx, dm3, dwi2, dwo2, dlg2, dlb2 = _ffn_bwd(dx, s3, mod[l], w["ffn2_in"], w["ffn2_out"], lng[2:3], lnb[2:3],
                                                   (6, 7, 8), "ffn2")
        dx, dm2, dwmi, dwmo, g, dlg1, dlb1 = _mixer_bwd(dx, s2, mod[l], w, sm, lng[1:2], lnb[1:2], l, tabs)
        dx, dm1, dwi1, dwo1, dlg0, dlb0 = _ffn_bwd(dx, s1, mod[l], w["ffn1_in"], w["ffn1_out"], lng[0:1], lnb[0:1],
                                                   (0, 1, 2), "ffn1")
        dmods[l] = jnp.concatenate(list(dm1) + list(dm2) + list(dm3), axis=1)
        big[l] = {"ffn1_in": dwi1, "ffn1_out": dwo1, "ffn2_in": dwi2, "ffn2_out": dwo2, "mix_in": dwmi,
                  "mix_out": dwmo}
        g["ln_g"] = jnp.concatenate([dlg0, dlg1, dlg2], axis=0)
        g["ln_b"] = jnp.concatenate([dlb0, dlb1, dlb2], axis=0)
        small[l] = g
    return loss, dx, jnp.stack(dmods), big, small


_BIG = ("ffn1_in", "ffn1_out", "ffn2_in", "ffn2_out", "mix_in", "mix_out")


def _small_grad_list(small, loss):
    def both(fn):
        return jnp.stack([fn(small[l]) for l in range(DEPTH)])

    uq_src, ukv_src = _uq_src(), _ukv_src()
    return [
        ("loss", loss.reshape(1)),
        ("ln_g", both(lambda g: g["ln_g"])), ("ln_b", both(lambda g: g["ln_b"])),
        ("hgrn_lb_logits", small[0]["lb_logits8"][:DEPTH] + small[1]["lb_logits8"][:DEPTH]),
        ("hgrn_norm_g", both(lambda g: g["hgrn_norm_g"][0])),
        ("mla_q_norm_g", both(lambda g: g["q_norm_g"][0])),
        ("mla_kv_norm_g", both(lambda g: g["kv_norm_g"][0])),
        ("mla_w_uq", both(lambda g: _unpack_cols(g["uq"], uq_src, 384))),
        ("mla_w_ukv", both(lambda g: _unpack_cols(g["ukv"], ukv_src, 512))),
        ("fox_b_f", both(lambda g: g["fox_b_f"][0, :N_HEADS])),
        ("gmlp_ln_g", both(lambda g: g["gmlp_ln_g"][0])), ("gmlp_ln_b", both(lambda g: g["gmlp_ln_b"][0])),
        ("gmlp_w_s", both(lambda g: g["gmlp_w_s"])),
        ("gmlp_b_s", both(lambda g: g["gmlp_bst"][:, :N_HEADS].T)),
    ]


_PACK_COLS = 512


def _pack_small(items):
    flat = jnp.concatenate([a.reshape(-1).astype(F32) for _, a in items])
    n = flat.shape[0]
    tile = 8 * _PACK_COLS
    flat = jnp.pad(flat, (0, (-n) % tile))
    return flat.reshape(-1, _PACK_COLS)


def _unpack_small(buf, items):
    flat = buf.reshape(-1)
    out, off = {}, 0
    for name, a in items:
        out[name] = flat[off:off + a.size].reshape(a.shape)
        off += a.size
    return out


def _as2d(a):
    return a.reshape(-1, a.shape[-1])


def kernel(x, c, ada_w, ada_b, ln_g, ln_b, ffn1_w_in, ffn1_w_out, ffn2_w_in, ffn2_w_out, mix_w_in, mix_w_out, hgrn_lb_logits, hgrn_norm_g, mla_q_norm_g, mla_kv_norm_g, mla_w_uq, mla_w_ukv, fox_b_f, gmlp_ln_g, gmlp_ln_b, gmlp_w_s, gmlp_b_s, loss_target, m_ada_w, m_ada_b, m_ln_g, m_ln_b, m_ffn1_w_in, m_ffn1_w_out, m_ffn2_w_in, m_ffn2_w_out, m_mix_w_in, m_mix_w_out, m_hgrn_lb_logits, m_hgrn_norm_g, m_mla_q_norm_g, m_mla_kv_norm_g, m_mla_w_uq, m_mla_w_ukv, m_fox_b_f, m_gmlp_ln_g, m_gmlp_ln_b, m_gmlp_w_s, m_gmlp_b_s, v_ada_w, v_ada_b, v_ln_g, v_ln_b, v_ffn1_w_in, v_ffn1_w_out, v_ffn2_w_in, v_ffn2_w_out, v_mix_w_in, v_mix_w_out, v_hgrn_lb_logits, v_hgrn_norm_g, v_mla_q_norm_g, v_mla_kv_norm_g, v_mla_w_uq, v_mla_w_ukv, v_fox_b_f, v_gmlp_ln_g, v_gmlp_ln_b, v_gmlp_w_s, v_gmlp_b_s):
    names = ["ada_w", "ada_b", "ln_g", "ln_b", "ffn1_w_in", "ffn1_w_out", "ffn2_w_in", "ffn2_w_out", "mix_w_in",
             "mix_w_out", "hgrn_lb_logits", "hgrn_norm_g", "mla_q_norm_g", "mla_kv_norm_g", "mla_w_uq", "mla_w_ukv",
             "fox_b_f", "gmlp_ln_g", "gmlp_ln_b", "gmlp_w_s", "gmlp_b_s"]
    w = dict(zip(names, [ada_w, ada_b, ln_g, ln_b, ffn1_w_in, ffn1_w_out, ffn2_w_in, ffn2_w_out, mix_w_in, mix_w_out,
                         hgrn_lb_logits, hgrn_norm_g, mla_q_norm_g, mla_kv_norm_g, mla_w_uq, mla_w_ukv, fox_b_f,
                         gmlp_ln_g, gmlp_ln_b, gmlp_w_s, gmlp_b_s]))
    m = dict(zip(names, [m_ada_w, m_ada_b, m_ln_g, m_ln_b, m_ffn1_w_in, m_ffn1_w_out, m_ffn2_w_in, m_ffn2_w_out,
                         m_mix_w_in, m_mix_w_out, m_hgrn_lb_logits, m_hgrn_norm_g, m_mla_q_norm_g, m_mla_kv_norm_g,
                         m_mla_w_uq, m_mla_w_ukv, m_fox_b_f, m_gmlp_ln_g, m_gmlp_ln_b, m_gmlp_w_s, m_gmlp_b_s]))
    v = dict(zip(names, [v_ada_w, v_ada_b, v_ln_g, v_ln_b, v_ffn1_w_in, v_ffn1_w_out, v_ffn2_w_in, v_ffn2_w_out,
                         v_mix_w_in, v_mix_w_out, v_hgrn_lb_logits, v_hgrn_norm_g, v_mla_q_norm_g, v_mla_kv_norm_g,
                         v_mla_w_uq, v_mla_w_ukv, v_fox_b_f, v_gmlp_ln_g, v_gmlp_ln_b, v_gmlp_w_s, v_gmlp_b_s]))
    bsz, seq, d = x.shape
    me = 4 * lax.axis_index("x") + 2 * lax.axis_index("y") + lax.axis_index("c")
    mix_src, uq_src, ukv_src, mo_src = _mix_in_src(), _uq_src(), _ukv_src(), _mo_src()

    shard_names = ["ffn1_w_in", "ffn1_w_out", "ffn2_w_in", "ffn2_w_out", "mix_w_in", "mix_w_out", "mla_w_uq",
                   "mla_w_ukv"]
    shards = []
    for l in range(DEPTH):
        for n in shard_names:
            a = w[n][l]
            if n == "mix_w_in":
                a = _pack_cols(a, mix_src)
            shards.append(a.astype(BF16))
    gathered = _all_gather(shards + [c, ln_g, ln_b], "gather_weights")
    c_all = gathered[-3].reshape(N_DEV * bsz, d)
    ln_g_full = jnp.moveaxis(gathered[-2], 0, 2).reshape(DEPTH, 3, d)
    ln_b_full = jnp.moveaxis(gathered[-1], 0, 2).reshape(DEPTH, 3, d)

    full = []
    for l in range(DEPTH):
        gw = dict(zip(shard_names, gathered[l * len(shard_names):(l + 1) * len(shard_names)]))
        uq = jnp.moveaxis(gw["mla_w_uq"], 0, 1).reshape(256, 384)
        ukv = jnp.moveaxis(gw["mla_w_ukv"], 0, 1).reshape(128, 512)
        full.append({
            "ffn1_in": gw["ffn1_w_in"], "ffn1_out": gw["ffn1_w_out"].reshape(4, 704, d),
            "ffn2_in": gw["ffn2_w_in"], "ffn2_out": gw["ffn2_w_out"].reshape(4, 704, d),
            "mix_in": gw["mix_w_in"].reshape(d, PACK_W),
            "mix_out": _pack_cols(gw["mix_w_out"].reshape(d, d).T, mo_src).T,
            "uq": _pack_cols(uq, uq_src), "ukv": _pack_cols(ukv, ukv_src),
        })

    mod_cols = _ada_fwd(c_all, ada_w, "ada_fwd")
    mod_all, = _all_gather([mod_cols], "gather_mod")
    mod_mine = lax.dynamic_slice_in_dim(mod_all, me * bsz, bsz, axis=2)
    mod = jnp.moveaxis(mod_mine, 0, 2).reshape(DEPTH, bsz, N_MOD * d) + ada_b[:, None, :]
    mod = mod.reshape(DEPTH, bsz, N_MOD, d)

    p = dict(w)
    p["ln_g"], p["ln_b"] = ln_g_full, ln_b_full
    loss, grad_x, dmod, big, small = _local_step(x, mod, loss_target, full, p)

    dmod_all, = _all_gather([dmod.reshape(DEPTH, bsz, N_MOD * d)], "gather_dmod")
    dmod_full = jnp.moveaxis(dmod_all, 0, 1).reshape(DEPTH, N_DEV * bsz, N_MOD * d)
    cols = ada_w.shape[2]
    dmod_cols = lax.dynamic_slice_in_dim(dmod_full, me * cols, cols, axis=2)
    g_ada_w, g_ada_b = _ada_bwd(c_all, dmod_cols, dmod_full, "ada_bwd")

    def chunks(name, arr):
        if name in ("ffn1_in", "ffn2_in"):
            return arr
        if name in ("ffn1_out", "ffn2_out"):
            return arr.reshape(N_DEV, arr.shape[1] // 2, d)
        if name == "mix_in":
            return arr.reshape(N_DEV, d // N_DEV, PACK_W)
        return _unpack_cols(arr.T, mo_src, d).T.astype(BF16).reshape(N_DEV, d // N_DEV, d)

    groups = [[chunks(n, big[l][n]) for l in range(DEPTH)] for n in _BIG]
    recv = dict(zip(_BIG, _reduce_scatter_push(groups, "scatter_grads")))
    recv["mix_in"] = _unpack_cols(recv["mix_in"], mix_src, MIX_ORIG_W)

    items = _small_grad_list(small, loss)
    parts, = _all_gather([_pack_small(items)], "gather_small")
    sg = _unpack_small(_sum_parts(parts, "sum_small"), items)

    out = {}

    def update(name, gparts):
        shape = w[name].shape
        res = _adamw(gparts, _as2d(w[name]), _as2d(m[name]), _as2d(v[name]), f"adamw_{name}")
        out[name] = tuple(r.reshape(shape) for r in res)

    big_of = {"ffn1_w_in": "ffn1_in", "ffn1_w_out": "ffn1_out", "ffn2_w_in": "ffn2_in", "ffn2_w_out": "ffn2_out",
              "mix_w_in": "mix_in", "mix_w_out": "mix_out"}
    for name, key in big_of.items():
        r = recv[key]
        update(name, r.reshape(N_DEV, -1, r.shape[-1]))
    update("ada_w", _as2d(g_ada_w)[None])
    update("ada_b", g_ada_b.reshape(1, DEPTH, N_MOD * d))
    for name in ("ln_g", "ln_b"):
        g_loc = lax.dynamic_slice_in_dim(sg[name], me * (d // N_DEV), d // N_DEV, axis=2)
        update(name, _as2d(g_loc)[None])
    for name, width in (("mla_w_uq", 48), ("mla_w_ukv", 64)):
        g_loc = lax.dynamic_slice_in_dim(sg[name], me * width, width, axis=2)
        update(name, _as2d(g_loc)[None])
    for name in ("hgrn_lb_logits", "hgrn_norm_g", "mla_q_norm_g", "mla_kv_norm_g", "fox_b_f", "gmlp_ln_g",
                 "gmlp_ln_b", "gmlp_w_s", "gmlp_b_s"):
        update(name, _as2d(sg[name])[None])

    return (sg["loss"][0], grad_x, *[out[n][0] for n in names], *[out[n][1] for n in names],
            *[out[n][2] for n in names], *[out[n][3] for n in names])
```

```python
import functools

import numpy as np
import jax
import jax.numpy as jnp
from jax import lax
from jax.experimental import pallas as pl
from jax.experimental.pallas import tpu as pltpu

F32 = jnp.float32
BF16 = jnp.bfloat16
HI = lax.Precision.HIGHEST

D_MODEL = 1024
DEPTH = 2
GROUP_WIDTH = 256
N_HEADS = 4
HEAD_DIM = 64
A_CHUNK = 16
LB_FLOOR = 1e-30
B_NOPE = 64
B_ROPE = 32
ROPE_THETA = 10000.0
D_CHUNK = 128
D_FF = 2816
N_MOD = 9
ALPHA = (2 * DEPTH) ** 0.25
LN_EPS = 1e-5
RMS_EPS = 1e-6
ADAM_LR = 0.001
ADAM_B1 = 0.9
ADAM_B2 = 0.999
ADAM_EPS = 1e-08
ADAM_WD = 0.01
ADAM_STEP = 10

N_DEV = 8
LANES = 128
PACK_W = 3712
MO_W = 1536
VMEM_LIMIT = 56 * 1024 * 1024
NEG = -1e30
ATTN_TILE = 512

MIX_ORIG_W = 2724
O_BCQ, O_BCKV, O_BKR, O_CQ, O_CK, O_CV, O_CF, O_DU, O_DV = 1024, 1280, 1408, 1440, 1696, 1952, 2208, 2212, 2468
P_B, P_KR, P_CQ, P_CKV, P_D, P_CF = 1024, 1408, 1536, 2048, 3072, 3584


_DN = {"nn": (((1,), (0,)), ((), ())), "nt": (((1,), (1,)), ((), ())), "tn": (((0,), (0,)), ((), ()))}


def _raw_bdot(a, b, mode):
    return lax.dot_general(a.astype(BF16), b.astype(BF16), _DN[mode], preferred_element_type=F32)


@functools.partial(jax.custom_vjp, nondiff_argnums=(2,))
def _bdot(a, b, mode):
    return _raw_bdot(a, b, mode)


def _bdot_fwd(a, b, mode):
    return _raw_bdot(a, b, mode), (a, b)


def _bdot_bwd(mode, res, g):
    a, b = res
    if mode == "nn":
        return _raw_bdot(g, b, "nt"), _raw_bdot(a, g, "tn")
    if mode == "nt":
        return _raw_bdot(g, b, "nn"), _raw_bdot(g, a, "tn")
    return _raw_bdot(b, g, "nt"), _raw_bdot(a, g, "nn")


_bdot.defvjp(_bdot_fwd, _bdot_bwd)


def _cparams(sem):
    return pltpu.CompilerParams(dimension_semantics=sem, vmem_limit_bytes=VMEM_LIMIT)


def _mix_in_src():
    src = -np.ones(PACK_W, np.int64)
    src[0:P_KR] = np.arange(0, O_BKR)
    src[P_KR + 64:P_KR + 80] = O_BKR + np.arange(16)
    src[P_KR + 96:P_KR + 112] = O_BKR + 16 + np.arange(16)
    for h in range(N_HEADS):
        src[P_CQ + 128 * h:P_CQ + 128 * h + 64] = O_CQ + 64 * h + np.arange(64)
        src[P_CKV + 256 * h:P_CKV + 256 * h + 64] = O_CK + 64 * h + np.arange(64)
        src[P_CKV + 256 * h + 128:P_CKV + 256 * h + 192] = O_CV + 64 * h + np.arange(64)
    src[P_D:P_D + 512] = O_DU + np.arange(512)
    src[P_CF:P_CF + 4] = O_CF + np.arange(4)
    return src


def _uq_src():
    src = -np.ones(512, np.int64)
    for h in range(N_HEADS):
        src[128 * h:128 * h + 64] = 96 * h + np.arange(64)
        src[128 * h + 64:128 * h + 80] = 96 * h + 64 + np.arange(16)
        src[128 * h + 96:128 * h + 112] = 96 * h + 80 + np.arange(16)
    return src


def _ukv_src():
    src = -np.ones(1024, np.int64)
    for h in range(N_HEADS):
        src[256 * h:256 * h + 64] = 128 * h + np.arange(64)
        src[256 * h + 128:256 * h + 192] = 128 * h + 64 + np.arange(64)
    return src


def _mo_src():
    src = -np.ones(MO_W, np.int64)
    src[0:256] = np.arange(256)
    for g in range(2):
        for h in range(N_HEADS):
            src[256 + 512 * g + 128 * h:256 + 512 * g + 128 * h + 64] = 256 + 256 * g + 64 * h + np.arange(64)
    src[1280:1536] = 768 + np.arange(256)
    return src


def _pack_cols(w, src):
    valid = jnp.asarray(src >= 0)
    return jnp.where(valid, jnp.take(w, jnp.asarray(np.maximum(src, 0)), axis=-1), jnp.zeros((), w.dtype))


def _unpack_cols(wp, src, n):
    dst = np.zeros(n, np.int64)
    dst[src[src >= 0]] = np.nonzero(src >= 0)[0]
    return jnp.take(wp, jnp.asarray(dst), axis=-1)


def _rope_tables(seq):
    half = B_ROPE // 2
    inv_freq = ROPE_THETA ** (-jnp.arange(half, dtype=F32) / half)
    ang = jnp.arange(seq).astype(F32)[:, None] * inv_freq[None, :]
    cos, sin = jnp.cos(ang), jnp.sin(ang)
    z16 = jnp.zeros((seq, 16), F32)
    c = jnp.concatenate([jnp.ones((seq, 64), F32), cos, z16, cos, z16], axis=1)
    s1 = jnp.concatenate([jnp.zeros((seq, 64), F32), -sin, z16, z16, z16], axis=1)
    s2 = jnp.concatenate([jnp.zeros((seq, 64), F32), z16, z16, sin, z16], axis=1)
    return c, s1, s2


def _matmul(a, b, *, mode, group_out, out_dtype, tm, tk, name):
    ga, gb = a.shape[0], b.shape[0]
    g_n = max(ga, gb)
    if mode == "tn":
        k_dim, m_dim = a.shape[1:]
    else:
        m_dim, k_dim = a.shape[1:]
    n_dim = b.shape[1] if mode == "nt" else b.shape[2]
    assert m_dim % tm == 0 and k_dim % tk == 0
    kt = k_dim // tk
    n_red = kt if group_out else g_n * kt
    g_out = g_n if group_out else 1

    def split(g, r):
        return (g, r) if group_out else (r // kt, r % kt)

    def a_map(g, i, r):
        gg, kk = split(g, r)
        gg = gg if ga > 1 else 0
        return (gg, kk, i) if mode == "tn" else (gg, i, kk)

    def b_map(g, i, r):
        gg, kk = split(g, r)
        gg = gg if gb > 1 else 0
        return (gg, 0, kk) if mode == "nt" else (gg, kk, 0)

    a_blk = (None, tk, tm) if mode == "tn" else (None, tm, tk)
    b_blk = (None, n_dim, tk) if mode == "nt" else (None, tk, n_dim)
    dn = _DN[mode]

    def body(a_ref, b_ref, o_ref, *scratch):
        part = lax.dot_general(a_ref[...].astype(BF16), b_ref[...].astype(BF16), dn, preferred_element_type=F32)
        if n_red == 1:
            o_ref[...] = part.astype(o_ref.dtype)
            return
        acc_ref, = scratch
        r = pl.program_id(2)

        @pl.when(r == 0)
        def _():
            acc_ref[...] = part

        @pl.when(r > 0)
        def _():
            acc_ref[...] += part

        @pl.when(r == n_red - 1)
        def _():
            o_ref[...] = acc_ref[...].astype(o_ref.dtype)

    return pl.pallas_call(
        body, name=name, grid=(g_out, m_dim // tm, n_red),
        in_specs=[pl.BlockSpec(a_blk, a_map), pl.BlockSpec(b_blk, b_map)],
        out_specs=pl.BlockSpec((None, tm, n_dim), lambda g, i, r: (g, i, 0)),
        out_shape=jax.ShapeDtypeStruct((g_out, m_dim, n_dim), out_dtype),
        scratch_shapes=[] if n_red == 1 else [pltpu.VMEM((tm, n_dim), F32)],
        compiler_params=_cparams(("parallel", "parallel", "arbitrary")),
    )(a, b)


def _row_spec(ts, d):
    return pl.BlockSpec((None, ts, d), lambda b, s: (b, s, 0))


def _mod_spec(d):
    return pl.BlockSpec((None, N_MOD, d), lambda b, s: (b, 0, 0))


def _vec_spec(d):
    return pl.BlockSpec((1, d), lambda b, s: (0, 0))


def _bvec_spec(d):
    return pl.BlockSpec((None, 1, d), lambda b, s: (b, 0, 0))


def _modulate(x, mod, sh_row, sc_row, name, ts=512):
    bsz, seq, d = x.shape

    def body(x_ref, mod_ref, o_ref):
        sh = mod_ref[sh_row:sh_row + 1, :]
        sc = mod_ref[sc_row:sc_row + 1, :]
        o_ref[...] = (x_ref[...] * (1.0 + sc) + sh).astype(o_ref.dtype)

    return pl.pallas_call(
        body, name=name, grid=(bsz, seq // ts),
        in_specs=[_row_spec(ts, d), _mod_spec(d)], out_specs=_row_spec(ts, d),
        out_shape=jax.ShapeDtypeStruct((bsz, seq, d), BF16),
        compiler_params=_cparams(("parallel", "parallel")),
    )(x, mod)


def _modulate_bwd(dh, x, mod, dx_res, sc_row, name, ts=512):
    bsz, seq, d = x.shape

    def body(dh_ref, x_ref, mod_ref, dxr_ref, dx_ref, dsh_ref, dsc_ref):
        s = pl.program_id(1)
        sc = mod_ref[sc_row:sc_row + 1, :]
        dh_v = dh_ref[...]
        dx_ref[...] = dxr_ref[...] + dh_v * (1.0 + sc)
        psh = jnp.sum(dh_v, axis=0, keepdims=True)
        psc = jnp.sum(dh_v * x_ref[...], axis=0, keepdims=True)

        @pl.when(s == 0)
        def _():
            dsh_ref[...] = psh
            dsc_ref[...] = psc

        @pl.when(s > 0)
        def _():
            dsh_ref[...] += psh
            dsc_ref[...] += psc

    return pl.pallas_call(
        body, name=name, grid=(bsz, seq // ts),
        in_specs=[_row_spec(ts, d), _row_spec(ts, d), _mod_spec(d), _row_spec(ts, d)],
        out_specs=[_row_spec(ts, d), _bvec_spec(d), _bvec_spec(d)],
        out_shape=[jax.ShapeDtypeStruct((bsz, seq, d), F32), jax.ShapeDtypeStruct((bsz, 1, d), F32),
                   jax.ShapeDtypeStruct((bsz, 1, d), F32)],
        compiler_params=_cparams(("parallel", "arbitrary")),
    )(dh, x, mod, dx_res)


def _res_ln_fn(x, f, g, lng, lnb, cmul):
    r = ALPHA * x + (cmul * (1.0 + g)) * f
    mu = jnp.mean(r, axis=-1, keepdims=True)
    rc = r - mu
    var = jnp.mean(rc * rc, axis=-1, keepdims=True)
    return rc * lax.rsqrt(var + LN_EPS) * lng + lnb


def _res_ln(x, f, mod, lng, lnb, g_row, cmul, name, ts=512):
    bsz, seq, d = x.shape

    def body(x_ref, f_ref, mod_ref, lng_ref, lnb_ref, o_ref):
        g = mod_ref[g_row:g_row + 1, :]
        o_ref[...] = _res_ln_fn(x_ref[...], f_ref[...], g, lng_ref[...], lnb_ref[...], cmul)

    return pl.pallas_call(
        body, name=name, grid=(bsz, seq // ts),
        in_specs=[_row_spec(ts, d), _row_spec(ts, d), _mod_spec(d), _vec_spec(d), _vec_spec(d)],
        out_specs=_row_spec(ts, d), out_shape=jax.ShapeDtypeStruct((bsz, seq, d), F32),
        compiler_params=_cparams(("parallel", "parallel")),
    )(x, f, mod, lng, lnb)


def _res_ln_bwd(dy, x, f, mod, lng, lnb, g_row, cmul, name, ts=256):
    bsz, seq, d = x.shape

    def body(dy_ref, x_ref, f_ref, mod_ref, lng_ref, lnb_ref, dx_ref, df_ref, dg_ref, dlg_ref, dlb_ref):
        b, s = pl.program_id(0), pl.program_id(1)
        g = mod_ref[g_row:g_row + 1, :]
        _, vjp = jax.vjp(functools.partial(_res_ln_fn, cmul=cmul), x_ref[...], f_ref[...], g, lng_ref[...],
                         lnb_ref[...])
        dx, df, dg, dlg, dlb = vjp(dy_ref[...])
        dx_ref[...] = dx
        df_ref[...] = df.astype(df_ref.dtype)

        @pl.when(s == 0)
        def _():
            dg_ref[...] = dg

        @pl.when(s > 0)
        def _():
            dg_ref[...] += dg

        first = jnp.logical_and(b == 0, s == 0)

        @pl.when(first)
        def _():
            dlg_ref[...] = dlg
            dlb_ref[...] = dlb

        @pl.when(jnp.logical_not(first))
        def _():
            dlg_ref[...] += dlg
            dlb_ref[...] += dlb

    return pl.pallas_call(
        body, name=name, grid=(bsz, seq // ts),
        in_specs=[_row_spec(ts, d), _row_spec(ts, d), _row_spec(ts, d), _mod_spec(d), _vec_spec(d), _vec_spec(d)],
        out_specs=[_row_spec(ts, d), _row_spec(ts, d), _bvec_spec(d), _vec_spec(d), _vec_spec(d)],
        out_shape=[jax.ShapeDtypeStruct((bsz, seq, d), F32), jax.ShapeDtypeStruct((bsz, seq, d), BF16),
                   jax.ShapeDtypeStruct((bsz, 1, d), F32), jax.ShapeDtypeStruct((1, d), F32),
                   jax.ShapeDtypeStruct((1, d), F32)],
        compiler_params=_cparams(("arbitrary", "arbitrary")),
    )(dy, x, f, mod, lng, lnb)


def _loss_head(y, target, name, ts=512):
    bsz, seq, d = y.shape
    n_s = seq // ts

    def body(y_ref, t_ref, dy_ref, loss_ref, acc_ref):
        b, s = pl.program_id(0), pl.program_id(1)
        err = y_ref[...] - t_ref[...]
        dy_ref[...] = err * (1.0 / d)
        part = jnp.sum(err * err, axis=0, keepdims=True)
        first = jnp.logical_and(b == 0, s == 0)

        @pl.when(first)
        def _():
            acc_ref[...] = part

        @pl.when(jnp.logical_not(first))
        def _():
            acc_ref[...] += part

        @pl.when(jnp.logical_and(b == bsz - 1, s == n_s - 1))
        def _():
            loss_ref[...] = jnp.sum(acc_ref[...], axis=1, keepdims=True) * (0.5 / d)

    return pl.pallas_call(
        body, name=name, grid=(bsz, n_s),
        in_specs=[_row_spec(ts, d), _row_spec(ts, d)],
        out_specs=[_row_spec(ts, d), pl.BlockSpec((1, 1), lambda b, s: (0, 0))],
        out_shape=[jax.ShapeDtypeStruct((bsz, seq, d), F32), jax.ShapeDtypeStruct((1, 1), F32)],
        scratch_shapes=[pltpu.VMEM((1, d), F32)],
        compiler_params=_cparams(("arbitrary", "arbitrary")),
    )(y, target)


def _ffn_in_swiglu(h, w_in, name, tm=512):
    t, d = h.shape
    n_sh, _, w = w_in.shape
    half = n_sh // 2

    def body(h_ref, w_ref, z_ref, a_ref):
        hv = h_ref[...]
        g = jnp.dot(hv, w_ref[0], preferred_element_type=F32)
        u = jnp.dot(hv, w_ref[1], preferred_element_type=F32)
        z_ref[0] = g.astype(z_ref.dtype)
        z_ref[1] = u.astype(z_ref.dtype)
        a_ref[...] = (g * jax.nn.sigmoid(g) * u).astype(a_ref.dtype)

    return pl.pallas_call(
        body, name=name, grid=(half, t // tm),
        in_specs=[pl.BlockSpec((tm, d), lambda g, i: (i, 0)),
                  pl.BlockSpec((2, None, d, w), lambda g, i: (0, g, 0, 0))],
        out_specs=[pl.BlockSpec((2, None, tm, w), lambda g, i: (0, g, i, 0)),
                   pl.BlockSpec((None, tm, w), lambda g, i: (g, i, 0))],
        out_shape=[jax.ShapeDtypeStruct((2, half, t, w), BF16), jax.ShapeDtypeStruct((half, t, w), BF16)],
        compiler_params=_cparams(("parallel", "parallel")),
    )(h, w_in.reshape(2, half, d, w))


def _ffn_out_dx_swiglu(df, w_out, z, name, tm=512):
    t, d = df.shape
    half, w, _ = w_out.shape

    def body(df_ref, w_ref, z_ref, dz_ref):
        da = lax.dot_general(df_ref[...], w_ref[...], _DN["nt"], preferred_element_type=F32)
        g = z_ref[0].astype(F32)
        u = z_ref[1].astype(F32)
        sig = jax.nn.sigmoid(g)
        dz_ref[0] = (da * u * (sig * (1.0 + g * (1.0 - sig)))).astype(dz_ref.dtype)
        dz_ref[1] = (da * (g * sig)).astype(dz_ref.dtype)

    zspec = pl.BlockSpec((2, None, tm, w), lambda g, i: (0, g, i, 0))
    return pl.pallas_call(
        body, name=name, grid=(half, t // tm),
        in_specs=[pl.BlockSpec((tm, d), lambda g, i: (i, 0)), pl.BlockSpec((None, w, d), lambda g, i: (g, 0, 0)),
                  zspec],
        out_specs=zspec, out_shape=jax.ShapeDtypeStruct(z.shape, BF16),
        compiler_params=_cparams(("parallel", "parallel")),
    )(df, w_out, z)


def _log_sigmoid(x):
    return jnp.minimum(x, 0.0) - jnp.log(1.0 + jnp.exp(-jnp.abs(x)))


def _hgrn_consts():
    r = lax.broadcasted_iota(jnp.int32, (GROUP_WIDTH, GROUP_WIDTH), 0)
    c = lax.broadcasted_iota(jnp.int32, (GROUP_WIDTH, GROUP_WIDTH), 1)
    bd = (r // HEAD_DIM == c // HEAD_DIM).astype(F32)
    r16 = lax.broadcasted_iota(jnp.int32, (A_CHUNK, A_CHUNK), 0)
    c16 = lax.broadcasted_iota(jnp.int32, (A_CHUNK, A_CHUNK), 1)
    tril = (r16 >= c16).astype(F32)
    rows = lax.broadcasted_iota(jnp.int32, (A_CHUNK, GROUP_WIDTH), 0)
    return bd, tril, rows


def _hgrn_lb(logits8, layer):
    rows = lax.broadcasted_iota(jnp.int32, logits8.shape, 0)
    valid = rows < DEPTH
    mx = jnp.max(jnp.where(valid, logits8, NEG), axis=0, keepdims=True)
    e = jnp.where(valid, jnp.exp(logits8 - mx), 0.0)
    sm = e / jnp.sum(e, axis=0, keepdims=True)
    pick = jnp.logical_and(rows >= 1, rows <= layer)
    return jnp.sum(jnp.where(pick, sm, 0.0), axis=0, keepdims=True)


def _hgrn_chunk(aq, af, ai, ag, logits8, norm_g, st, *, layer, consts):
    bd, tril, rows = consts
    lb = _hgrn_lb(logits8, layer)
    la = jnp.log(jnp.maximum(lb, LB_FLOOR))
    b2 = jnp.log(1.0 - lb) + _log_sigmoid(af)
    log_f = jnp.maximum(la, b2) + jnp.log(1.0 + jnp.exp(-jnp.abs(la - b2)))
    k = 1.0 - jnp.exp(log_f)
    qf = aq * jax.nn.sigmoid(aq)
    g_cum = jnp.dot(tril, log_f, precision=HI, preferred_element_type=F32)

    def row(v, s):
        return jnp.sum(jnp.where(rows == s, v, 0.0), axis=0, keepdims=True)

    parts = []
    v_rows = []
    for s in range(A_CHUNK):
        rel = jnp.where(rows >= s, g_cum - row(g_cum, s), NEG)
        parts.append(qf * (row(k, s) * jnp.exp(rel)))
        v_rows.append(row(ai, s))
    a_all = _bdot(jnp.concatenate(parts, axis=0), bd, "nn")
    o = jnp.zeros_like(aq)
    for s in range(A_CHUNK):
        o = o + a_all[s * A_CHUNK:(s + 1) * A_CHUNK, :] * v_rows[s]
    q_dec = qf * jnp.exp(g_cum)
    o = o + _bdot(q_dec, st, "nt")
    g_last = row(g_cum, A_CHUNK - 1)
    k_end = k * jnp.exp(g_last - g_cum)
    kv = _bdot(ai, k_end, "tn")
    st_new = st * jnp.exp(g_last) + kv * bd
    ms = _bdot(o * o, bd, "nn") * (1.0 / HEAD_DIM)
    o = o * lax.rsqrt(ms + RMS_EPS) * norm_g
    return o * (ag * jax.nn.sigmoid(ag)), st_new


def _hgrn_fwd(proj, logits8, norm_g, layer, name, ts=128):
    bsz, seq, _ = proj.shape
    n_ch = ts // A_CHUNK

    def body(p_ref, lg_ref, ng_ref, o_ref, st_ref, st_scr):
        @pl.when(pl.program_id(1) == 0)
        def _():
            st_scr[...] = jnp.zeros_like(st_scr)

        consts = _hgrn_consts()
        logits_v, ng_v = lg_ref[...], ng_ref[...]

        def chunk(ci, carry):
            r = pl.multiple_of(ci * A_CHUNK, A_CHUNK)
            st = st_scr[...]
            st_ref[ci] = st
            o, st_new = _hgrn_chunk(
                p_ref[pl.ds(r, A_CHUNK), 0:256], p_ref[pl.ds(r, A_CHUNK), 256:512],
                p_ref[pl.ds(r, A_CHUNK), 512:768], p_ref[pl.ds(r, A_CHUNK), 768:1024],
                logits_v, ng_v, st, layer=layer, consts=consts)
            o_ref[pl.ds(r, A_CHUNK), :] = o.astype(o_ref.dtype)
            st_scr[...] = st_new
            return carry

        lax.fori_loop(0, n_ch, chunk, 0)

    return pl.pallas_call(
        body, name=name, grid=(bsz, seq // ts),
        in_specs=[pl.BlockSpec((None, ts, 1024), lambda b, s: (b, s, 0)),
                  pl.BlockSpec((8, GROUP_WIDTH), lambda b, s: (0, 0)),
                  pl.BlockSpec((1, GROUP_WIDTH), lambda b, s: (0, 0))],
        out_specs=[pl.BlockSpec((None, ts, GROUP_WIDTH), lambda b, s: (b, s, 0)),
                   pl.BlockSpec((None, n_ch, GROUP_WIDTH, GROUP_WIDTH), lambda b, s: (b, s, 0, 0))],
        out_shape=[jax.ShapeDtypeStruct((bsz, seq, MO_W), BF16),
                   jax.ShapeDtypeStruct((bsz, seq // A_CHUNK, GROUP_WIDTH, GROUP_WIDTH), F32)],
        scratch_shapes=[pltpu.VMEM((GROUP_WIDTH, GROUP_WIDTH), F32)],
        compiler_params=_cparams(("parallel", "arbitrary")),
    )(proj, logits8, norm_g)


def _hgrn_bwd(dmo, proj, states, logits8, norm_g, layer, name, ts=128):
    bsz, seq, _ = proj.shape
    n_ch = ts // A_CHUNK
    n_s = seq // ts

    def body(do_ref, p_ref, st_ref, lg_ref, ng_ref, dp_ref, dlg_ref, dng_ref, dst_scr):
        b, s = pl.program_id(0), pl.program_id(1)

        @pl.when(s == 0)
        def _():
            dst_scr[...] = jnp.zeros_like(dst_scr)

        @pl.when(jnp.logical_and(b == 0, s == 0))
        def _():
            dlg_ref[...] = jnp.zeros_like(dlg_ref)
            dng_ref[...] = jnp.zeros_like(dng_ref)

        consts = _hgrn_consts()
        logits_v, ng_v = lg_ref[...], ng_ref[...]
        fn = functools.partial(_hgrn_chunk, layer=layer, consts=consts)

        def chunk(t, carry):
            ci = n_ch - 1 - t
            r = pl.multiple_of(ci * A_CHUNK, A_CHUNK)
            _, vjp = jax.vjp(
                fn, p_ref[pl.ds(r, A_CHUNK), 0:256], p_ref[pl.ds(r, A_CHUNK), 256:512],
                p_ref[pl.ds(r, A_CHUNK), 512:768], p_ref[pl.ds(r, A_CHUNK), 768:1024],
                logits_v, ng_v, st_ref[ci])
            daq, daf, dai, dag, dlg, dng, dst = vjp((do_ref[pl.ds(r, A_CHUNK), :], dst_scr[...]))
            dp_ref[pl.ds(r, A_CHUNK), 0:256] = daq.astype(dp_ref.dtype)
            dp_ref[pl.ds(r, A_CHUNK), 256:512] = daf.astype(dp_ref.dtype)
            dp_ref[pl.ds(r, A_CHUNK), 512:768] = dai.astype(dp_ref.dtype)
            dp_ref[pl.ds(r, A_CHUNK), 768:1024] = dag.astype(dp_ref.dtype)
            dlg_ref[...] += dlg
            dng_ref[...] += dng
            dst_scr[...] = dst
            return carry

        lax.fori_loop(0, n_ch, chunk, 0)

    rev = lambda b, s: (b, n_s - 1 - s, 0)
    return pl.pallas_call(
        body, name=name, grid=(bsz, n_s),
        in_specs=[pl.BlockSpec((None, ts, GROUP_WIDTH), rev),
                  pl.BlockSpec((None, ts, 1024), rev),
                  pl.BlockSpec((None, n_ch, GROUP_WIDTH, GROUP_WIDTH), lambda b, s: (b, n_s - 1 - s, 0, 0)),
                  pl.BlockSpec((8, GROUP_WIDTH), lambda b, s: (0, 0)),
                  pl.BlockSpec((1, GROUP_WIDTH), lambda b, s: (0, 0))],
        out_specs=[pl.BlockSpec((None, ts, 1024), rev),
                   pl.BlockSpec((8, GROUP_WIDTH), lambda b, s: (0, 0)),
                   pl.BlockSpec((1, GROUP_WIDTH), lambda b, s: (0, 0))],
        out_shape=[jax.ShapeDtypeStruct((bsz, seq, PACK_W), BF16),
                   jax.ShapeDtypeStruct((8, GROUP_WIDTH), F32), jax.ShapeDtypeStruct((1, GROUP_WIDTH), F32)],
        scratch_shapes=[pltpu.VMEM((GROUP_WIDTH, GROUP_WIDTH), F32)],
        compiler_params=_cparams(("arbitrary", "arbitrary")),
    )(dmo, proj, states, logits8, norm_g)


def _rms_fn(x, g):
    return x * lax.rsqrt(jnp.mean(x * x, axis=-1, keepdims=True) + RMS_EPS) * g


def _tile4(t):
    return jnp.concatenate([t, t, t, t], axis=1)


def _rope(x, c, s1, s2):
    w = x.shape[-1]
    return x * c + pltpu.roll(x, 32, axis=1) * s2 + pltpu.roll(x, w - 32, axis=1) * s1


def _rope_t(dy, c, s1, s2):
    w = dy.shape[-1]
    return dy * c + pltpu.roll(dy * s2, w - 32, axis=1) + pltpu.roll(dy * s1, 32, axis=1)


def _mla_pre(proj, qg, kvg, wq, wkv, tabs, name, ts=256):
    bsz, seq, _ = proj.shape

    def body(p_ref, qg_ref, kvg_ref, wq_ref, wkv_ref, c_ref, s1_ref, s2_ref, q_ref, kv_ref):
        nq = _rms_fn(p_ref[:, 0:256], qg_ref[...])
        nkv = _rms_fn(p_ref[:, 256:384], kvg_ref[...])
        c, s1, s2 = c_ref[...], s1_ref[...], s2_ref[...]
        qp = jnp.dot(nq.astype(BF16), wq_ref[...], preferred_element_type=F32)
        q_ref[...] = _rope(qp, _tile4(c), _tile4(s1), _tile4(s2)).astype(q_ref.dtype)
        kv = jnp.dot(nkv.astype(BF16), wkv_ref[...], preferred_element_type=F32)
        krr = _rope(p_ref[:, 384:512], c, s1, s2)
        zero = jnp.zeros_like(krr)
        kv_ref[...] = (kv + jnp.concatenate([krr, zero] * N_HEADS, axis=1)).astype(kv_ref.dtype)

    tab_spec = pl.BlockSpec((ts, LANES), lambda b, s: (s, 0))
    return pl.pallas_call(
        body, name=name, grid=(bsz, seq // ts),
        in_specs=[pl.BlockSpec((None, ts, 512), lambda b, s: (b, s, P_B // 512)),
                  _vec_spec(256), _vec_spec(128),
                  pl.BlockSpec((256, 512), lambda b, s: (0, 0)), pl.BlockSpec((128, 1024), lambda b, s: (0, 0)),
                  tab_spec, tab_spec, tab_spec],
        out_specs=[_row_spec(ts, 512), _row_spec(ts, 1024)],
        out_shape=[jax.ShapeDtypeStruct((bsz, seq, 512), BF16), jax.ShapeDtypeStruct((bsz, seq, 1024), BF16)],
        compiler_params=_cparams(("parallel", "parallel")),
    )(proj, qg, kvg, wq, wkv, *tabs)


def _mla_pre_bwd(dq, dkv, dproj, proj, qg, kvg, wq, wkv, tabs, name, ts=256):
    bsz, seq, _ = proj.shape

    def body(dq_ref, dkv_ref, dp_any, p_ref, qg_ref, kvg_ref, wq_ref, wkv_ref, c_ref, s1_ref, s2_ref,
             dp_ref, dqg_ref, dkvg_ref, dwq_ref, dwkv_ref):
        del dp_any
        first = jnp.logical_and(pl.program_id(0) == 0, pl.program_id(1) == 0)

        @pl.when(first)
        def _():
            dqg_ref[...] = jnp.zeros_like(dqg_ref)
            dkvg_ref[...] = jnp.zeros_like(dkvg_ref)
            dwq_ref[...] = jnp.zeros_like(dwq_ref)
            dwkv_ref[...] = jnp.zeros_like(dwkv_ref)

        c, s1, s2 = c_ref[...], s1_ref[...], s2_ref[...]
        nq, vjp_q = jax.vjp(_rms_fn, p_ref[:, 0:256], qg_ref[...])
        nkv, vjp_kv = jax.vjp(_rms_fn, p_ref[:, 256:384], kvg_ref[...])
        dqp = _rope_t(dq_ref[...], _tile4(c), _tile4(s1), _tile4(s2)).astype(BF16)
        dkv_v = dkv_ref[...]
        dkv_b = dkv_v.astype(BF16)
        tn = (((0,), (0,)), ((), ()))
        nt = (((1,), (1,)), ((), ()))
        dwq_ref[...] += lax.dot_general(nq.astype(BF16), dqp, tn, preferred_element_type=F32)
        dwkv_ref[...] += lax.dot_general(nkv.astype(BF16), dkv_b, tn, preferred_element_type=F32)
        dcq, dqg = vjp_q(lax.dot_general(dqp, wq_ref[...], nt, preferred_element_type=F32))
        dckv, dkvg = vjp_kv(lax.dot_general(dkv_b, wkv_ref[...], nt, preferred_element_type=F32))
        dqg_ref[...] += dqg
        dkvg_ref[...] += dkvg
        dk_sum = dkv_v[:, 0:128] + dkv_v[:, 256:384] + dkv_v[:, 512:640] + dkv_v[:, 768:896]
        lane = lax.broadcasted_iota(jnp.int32, dk_sum.shape, 1)
        dkr = jnp.where(lane >= 64, _rope_t(dk_sum, c, s1, s2), 0.0)
        dp_ref[:, 0:256] = dcq.astype(dp_ref.dtype)
        dp_ref[:, 256:384] = dckv.astype(dp_ref.dtype)
        dp_ref[:, 384:512] = dkr.astype(dp_ref.dtype)

    tab_spec = pl.BlockSpec((ts, LANES), lambda b, s: (s, 0))
    const = lambda shape: pl.BlockSpec(shape, lambda b, s: (0, 0))
    return pl.pallas_call(
        body, name=name, grid=(bsz, seq // ts),
        in_specs=[_row_spec(ts, 512), _row_spec(ts, 1024), pl.BlockSpec(memory_space=pl.ANY),
                  pl.BlockSpec((None, ts, 512), lambda b, s: (b, s, P_B // 512)),
                  _vec_spec(256), _vec_spec(128), const((256, 512)), const((128, 1024)),
                  tab_spec, tab_spec, tab_spec],
        out_specs=[pl.BlockSpec((None, ts, 512), lambda b, s: (b, s, P_B // 512)),
                   _vec_spec(256), _vec_spec(128), const((256, 512)), const((128, 1024))],
        out_shape=[jax.ShapeDtypeStruct(dproj.shape, dproj.dtype), jax.ShapeDtypeStruct((1, 256), F32),
                   jax.ShapeDtypeStruct((1, 128), F32), jax.ShapeDtypeStruct((256, 512), F32),
                   jax.ShapeDtypeStruct((128, 1024), F32)],
        input_output_aliases={2: 0},
        compiler_params=_cparams(("arbitrary", "arbitrary")),
    )(dq, dkv, dproj, proj, qg, kvg, wq, wkv, *tabs)


def _fox_gate(proj, bf, name):
    bsz, seq, _ = proj.shape
    n_blk = seq // LANES

    def body(x_ref, bf_ref, f_ref):
        r_i = lax.broadcasted_iota(jnp.int32, (LANES, LANES), 0)
        c_i = lax.broadcasted_iota(jnp.int32, (LANES, LANES), 1)
        tril = (r_i >= c_i).astype(F32)
        bias = bf_ref[...]

        def blk(i, carry):
            r = pl.multiple_of(i * LANES, LANES)
            lf = _log_sigmoid(x_ref[pl.ds(r, LANES), :] + bias)
            f_ref[pl.ds(r, LANES), :] = jnp.dot(tril, lf, precision=HI, preferred_element_type=F32) + carry
            return carry + jnp.sum(lf, axis=0, keepdims=True)

        lax.fori_loop(0, n_blk, blk, jnp.zeros((1, LANES), F32))

    return pl.pallas_call(
        body, name=name, grid=(bsz,),
        in_specs=[pl.BlockSpec((None, seq, LANES), lambda b: (b, 0, P_CF // LANES)),
                  pl.BlockSpec((1, LANES), lambda b: (0, 0))],
        out_specs=pl.BlockSpec((None, seq, LANES), lambda b: (b, 0, 0)),
        out_shape=jax.ShapeDtypeStruct((bsz, seq, LANES), F32),
        compiler_params=_cparams(("parallel",)),
    )(proj, bf)


def _fox_gate_bwd(dcum, dproj, proj, bf, name):
    bsz, seq, _ = proj.shape
    n_blk = seq // LANES

    def body(dc_ref, dp_any, x_ref, bf_ref, dp_ref, dbf_ref):
        del dp_any

        @pl.when(pl.program_id(0) == 0)
        def _():
            dbf_ref[...] = jnp.zeros_like(dbf_ref)

        r_i = lax.broadcasted_iota(jnp.int32, (LANES, LANES), 0)
        c_i = lax.broadcasted_iota(jnp.int32, (LANES, LANES), 1)
        triu = (r_i <= c_i).astype(F32)
        bias = bf_ref[...]

        def blk(t, carry):
            tail, dbf = carry
            r = pl.multiple_of((n_blk - 1 - t) * LANES, LANES)
            dc = dc_ref[pl.ds(r, LANES), :]
            dlf = jnp.dot(triu, dc, precision=HI, preferred_element_type=F32) + tail
            dx = dlf * (1.0 - jax.nn.sigmoid(x_ref[pl.ds(r, LANES), :] + bias))
            dp_ref[pl.ds(r, LANES), :] = dx.astype(dp_ref.dtype)
            return tail + jnp.sum(dc, axis=0, keepdims=True), dbf + jnp.sum(dx, axis=0, keepdims=True)

        z = jnp.zeros((1, LANES), F32)
        _, dbf = lax.fori_loop(0, n_blk, blk, (z, z))
        dbf_ref[...] += dbf

    return pl.pallas_call(
        body, name=name, grid=(bsz,),
        in_specs=[pl.BlockSpec((None, seq, LANES), lambda b: (b, 0, 0)), pl.BlockSpec(memory_space=pl.ANY),
                  pl.BlockSpec((None, seq, LANES), lambda b: (b, 0, P_CF // LANES)),
                  pl.BlockSpec((1, LANES), lambda b: (0, 0))],
        out_specs=[pl.BlockSpec((None, seq, LANES), lambda b: (b, 0, P_CF // LANES)),
                   pl.BlockSpec((1, LANES), lambda b: (0, 0))],
        out_shape=[jax.ShapeDtypeStruct(dproj.shape, dproj.dtype), jax.ShapeDtypeStruct((1, LANES), F32)],
        input_output_aliases={1: 0},
        compiler_params=_cparams(("arbitrary",)),
    )(dcum, dproj, proj, bf)


def _gate_terms(fc_ref, fr_ref, h, tq, tk):
    lane = lax.broadcasted_iota(jnp.int32, (tq, LANES), 1)
    fcol = jnp.sum(jnp.where(lane == h, fc_ref[...], 0.0), axis=1, keepdims=True)
    sub = lax.broadcasted_iota(jnp.int32, (8, tk), 0)
    frow = jnp.sum(jnp.where(sub == h, fr_ref[...], 0.0), axis=0, keepdims=True)
    return fcol - frow


def _scores(q_ref, k_ref, gate_refs, scale, h, diag, tq, tk):
    s = lax.dot_general(q_ref[...].astype(BF16), k_ref[...].astype(BF16), (((1,), (1,)), ((), ())),
                        preferred_element_type=F32) * scale
    if gate_refs is not None:
        s = s + _gate_terms(gate_refs[0], gate_refs[1], h, tq, tk)
    r_i = lax.broadcasted_iota(jnp.int32, (tq, tk), 0)
    c_i = lax.broadcasted_iota(jnp.int32, (tq, tk), 1)
    return jnp.where(jnp.logical_or(jnp.logical_not(diag), c_i <= r_i), s, NEG)


def _attn_fwd(qa, q0, kva, kv0, mo, o0, gates, scale, name, tq=None):
    bsz, seq, _ = qa.shape
    tq = ATTN_TILE if tq is None else tq
    n_q = seq // tq
    gated = gates is not None

    def body(*refs):
        q_ref, k_ref, v_ref = refs[:3]
        gate_refs = refs[3:5] if gated else None
        o_ref, lse_ref, m_s, l_s, acc_s = refs[-5:]
        h, i, j = pl.program_id(1), pl.program_id(2), pl.program_id(3)

        @pl.when(j == 0)
        def _():
            m_s[...] = jnp.full_like(m_s, NEG)
            l_s[...] = jnp.zeros_like(l_s)
            acc_s[...] = jnp.zeros_like(acc_s)

        @pl.when(j <= i)
        def _():
            s = _scores(q_ref, k_ref, gate_refs, scale, h, j == i, tq, tq)
            m_prev = m_s[...]
            m_new = jnp.maximum(m_prev, jnp.max(s, axis=1, keepdims=True))
            alpha = jnp.exp(m_prev - m_new)
            p = jnp.exp(s - m_new)
            l_s[...] = alpha * l_s[...] + jnp.sum(p, axis=1, keepdims=True)
            acc_s[...] = alpha * acc_s[...] + jnp.dot(p.astype(BF16), v_ref[...].astype(BF16),
                                                      preferred_element_type=F32)
            m_s[...] = m_new

        @pl.when(j == i)
        def _():
            o_ref[...] = (acc_s[...] / l_s[...]).astype(o_ref.dtype)
            lse_ref[...] = m_s[...] + jnp.log(l_s[...])

    blk = (None, tq, LANES)
    in_specs = [pl.BlockSpec(blk, lambda b, h, i, j: (b, i, q0 + h)),
                pl.BlockSpec(blk, lambda b, h, i, j: (b, jnp.minimum(j, i), kv0 + 2 * h)),
                pl.BlockSpec(blk, lambda b, h, i, j: (b, jnp.minimum(j, i), kv0 + 2 * h + 1))]
    args = [qa, kva, kva]
    if gated:
        in_specs += [pl.BlockSpec(blk, lambda b, h, i, j: (b, i, 0)),
                     pl.BlockSpec((None, 8, tq), lambda b, h, i, j: (b, 0, jnp.minimum(j, i)))]
        args += list(gates)
    in_specs.append(pl.BlockSpec(memory_space=pl.ANY))
    args.append(mo)
    return pl.pallas_call(
        body, name=name, grid=(bsz, N_HEADS, n_q, n_q), in_specs=in_specs,
        out_specs=[pl.BlockSpec(blk, lambda b, h, i, j: (b, i, o0 + h)),
                   pl.BlockSpec((None, None, tq, 1), lambda b, h, i, j: (b, h, i, 0))],
        out_shape=[jax.ShapeDtypeStruct(mo.shape, mo.dtype), jax.ShapeDtypeStruct((bsz, N_HEADS, seq, 1), F32)],
        scratch_shapes=[pltpu.VMEM((tq, 1), F32), pltpu.VMEM((tq, 1), F32), pltpu.VMEM((tq, LANES), F32)],
        input_output_aliases={len(args) - 1: 0},
        compiler_params=_cparams(("parallel", "parallel", "parallel", "arbitrary")),
    )(*args)


def _attn_bwd_q(qa, q0, kva, kv0, mo, dmo, o0, lse, gates, scale, out, out0, name, tq=None):
    bsz, seq, _ = qa.shape
    tq = ATTN_TILE if tq is None else tq
    n_q = seq // tq
    gated = gates is not None
    aliased = not isinstance(out, jax.ShapeDtypeStruct)

    def body(*refs):
        q_ref, k_ref, v_ref, o_ref, do_ref, lse_ref = refs[:6]
        gate_refs = refs[6:8] if gated else None
        dq_ref, delta_ref, dfq_ref, acc_s, dl_s, df_s = refs[-6:]
        h, i, j = pl.program_id(1), pl.program_id(2), pl.program_id(3)

        @pl.when(j == 0)
        def _():
            acc_s[...] = jnp.zeros_like(acc_s)
            df_s[...] = jnp.zeros_like(df_s)
            dl_s[...] = jnp.sum(do_ref[...] * o_ref[...].astype(F32), axis=1, keepdims=True)

        @pl.when(j <= i)
        def _():
            s = _scores(q_ref, k_ref, gate_refs, scale, h, j == i, tq, tq)
            p = jnp.exp(s - lse_ref[...])
            dp = lax.dot_general(do_ref[...].astype(BF16), v_ref[...].astype(BF16), (((1,), (1,)), ((), ())),
                                 preferred_element_type=F32)
            ds = p * (dp - dl_s[...])
            acc_s[...] += jnp.dot(ds.astype(BF16), k_ref[...].astype(BF16), preferred_element_type=F32)
            df_s[...] += jnp.sum(ds, axis=1, keepdims=True)

        @pl.when(j == i)
        def _():
            dq_ref[...] = (acc_s[...] * scale).astype(dq_ref.dtype)
            delta_ref[...] = dl_s[...]
            dfq_ref[...] = df_s[...]

    blk = (None, tq, LANES)
    col = pl.BlockSpec((None, None, tq, 1), lambda b, h, i, j: (b, h, i, 0))
    in_specs = [pl.BlockSpec(blk, lambda b, h, i, j: (b, i, q0 + h)),
                pl.BlockSpec(blk, lambda b, h, i, j: (b, jnp.minimum(j, i), kv0 + 2 * h)),
                pl.BlockSpec(blk, lambda b, h, i, j: (b, jnp.minimum(j, i), kv0 + 2 * h + 1)),
                pl.BlockSpec(blk, lambda b, h, i, j: (b, i, o0 + h)),
                pl.BlockSpec(blk, lambda b, h, i, j: (b, i, o0 + h)), col]
    args = [qa, kva, kva, mo, dmo, lse]
    if gated:
        in_specs += [pl.BlockSpec(blk, lambda b, h, i, j: (b, i, 0)),
                     pl.BlockSpec((None, 8, tq), lambda b, h, i, j: (b, 0, jnp.minimum(j, i)))]
        args += list(gates)
    aliases = {}
    if aliased:
        in_specs.append(pl.BlockSpec(memory_space=pl.ANY))
        args.append(out)
        aliases = {len(args) - 1: 0}
    vec = jax.ShapeDtypeStruct((bsz, N_HEADS, seq, 1), F32)
    return pl.pallas_call(
        body, name=name, grid=(bsz, N_HEADS, n_q, n_q), in_specs=in_specs,
        out_specs=[pl.BlockSpec(blk, lambda b, h, i, j: (b, i, out0 + h)), col, col],
        out_shape=[jax.ShapeDtypeStruct(out.shape, out.dtype), vec, vec],
        scratch_shapes=[pltpu.VMEM((tq, LANES), F32), pltpu.VMEM((tq, 1), F32), pltpu.VMEM((tq, 1), F32)],
        input_output_aliases=aliases,
        compiler_params=_cparams(("parallel", "parallel", "parallel", "arbitrary")),
    )(*args)


def _attn_bwd_kv(qa, q0, kva, kv0, dmo, o0, lse, delta, gates, scale, out, out0, name, tq=None):
    bsz, seq, _ = qa.shape
    tq = ATTN_TILE if tq is None else tq
    n_q = seq // tq
    gated = gates is not None
    aliased = not isinstance(out, jax.ShapeDtypeStruct)

    def body(*refs):
        q_ref, k_ref, v_ref, do_ref, lse_ref, dl_ref = refs[:6]
        gate_refs = refs[6:8] if gated else None
        dkv_ref, dfk_ref, dk_s, dv_s, df_s = refs[-5:]
        h, j, i = pl.program_id(1), pl.program_id(2), pl.program_id(3)

        @pl.when(i == 0)
        def _():
            dk_s[...] = jnp.zeros_like(dk_s)
            dv_s[...] = jnp.zeros_like(dv_s)
            df_s[...] = jnp.zeros_like(df_s)

        @pl.when(i >= j)
        def _():
            s = _scores(q_ref, k_ref, gate_refs, scale, h, j == i, tq, tq)
            p = jnp.exp(s - lse_ref[...])
            do_b = do_ref[...].astype(BF16)
            dp = lax.dot_general(do_b, v_ref[...].astype(BF16), (((1,), (1,)), ((), ())),
                                 preferred_element_type=F32)
            ds = p * (dp - dl_ref[...])
            tn = (((0,), (0,)), ((), ()))
            dv_s[...] += lax.dot_general(p.astype(BF16), do_b, tn, preferred_element_type=F32)
            dk_s[...] += lax.dot_general(ds.astype(BF16), q_ref[...].astype(BF16), tn, preferred_element_type=F32)
            df_s[...] -= jnp.sum(ds, axis=0, keepdims=True)

        @pl.when(i == n_q - 1)
        def _():
            dkv_ref[:, 0:LANES] = (dk_s[...] * scale).astype(dkv_ref.dtype)
            dkv_ref[:, LANES:2 * LANES] = dv_s[...].astype(dkv_ref.dtype)
            dfk_ref[...] = df_s[...]

    blk = (None, tq, LANES)
    col = pl.BlockSpec((None, None, tq, 1), lambda b, h, j, i: (b, h, jnp.maximum(i, j), 0))
    in_specs = [pl.BlockSpec(blk, lambda b, h, j, i: (b, jnp.maximum(i, j), q0 + h)),
                pl.BlockSpec(blk, lambda b, h, j, i: (b, j, kv0 + 2 * h)),
                pl.BlockSpec(blk, lambda b, h, j, i: (b, j, kv0 + 2 * h + 1)),
                pl.BlockSpec(blk, lambda b, h, j, i: (b, jnp.maximum(i, j), o0 + h)), col, col]
    args = [qa, kva, kva, dmo, lse, delta]
    if gated:
        in_specs += [pl.BlockSpec(blk, lambda b, h, j, i: (b, jnp.maximum(i, j), 0)),
                     pl.BlockSpec((None, 8, tq), lambda b, h, j, i: (b, 0, j))]
        args += list(gates)
    aliases = {}
    if aliased:
        in_specs.append(pl.BlockSpec(memory_space=pl.ANY))
        args.append(out)
        aliases = {len(args) - 1: 0}
    return pl.pallas_call(
        body, name=name, grid=(bsz, N_HEADS, n_q, n_q), in_specs=in_specs,
        out_specs=[pl.BlockSpec((None, tq, 2 * LANES), lambda b, h, j, i: (b, j, out0 + h)),
                   pl.BlockSpec((None, None, 1, tq), lambda b, h, j, i: (b, h, 0, j))],
        out_shape=[jax.ShapeDtypeStruct(out.shape, out.dtype), jax.ShapeDtypeStruct((bsz, N_HEADS, 1, seq), F32)],
        scratch_shapes=[pltpu.VMEM((tq, LANES), F32), pltpu.VMEM((tq, LANES), F32), pltpu.VMEM((1, tq), F32)],
        input_output_aliases=aliases,
        compiler_params=_cparams(("parallel", "parallel", "parallel", "arbitrary")),
    )(*args)


def _gmlp_fn(uv, lng, lnb, ws, bst):
    u = jax.nn.gelu(uv[:, 0:GROUP_WIDTH])
    gv = jax.nn.gelu(uv[:, GROUP_WIDTH:2 * GROUP_WIDTH])
    mu = jnp.mean(gv, axis=-1, keepdims=True)
    vc = gv - mu
    var = jnp.mean(vc * vc, axis=-1, keepdims=True)
    vln = vc * lax.rsqrt(var + LN_EPS) * lng + lnb
    r_i = lax.broadcasted_iota(jnp.int32, (D_CHUNK, D_CHUNK), 0)
    c_i = lax.broadcasted_iota(jnp.int32, (D_CHUNK, D_CHUNK), 1)
    lane_g = lax.broadcasted_iota(jnp.int32, (D_CHUNK, GROUP_WIDTH), 1) // HEAD_DIM
    e_r = lax.broadcasted_iota(jnp.int32, (LANES, GROUP_WIDTH), 0)
    e_c = lax.broadcasted_iota(jnp.int32, (LANES, GROUP_WIDTH), 1)
    expand = (e_r == e_c // HEAD_DIM).astype(F32)
    mixed = jnp.dot(bst, expand, precision=HI, preferred_element_type=F32)
    for g in range(4):
        w = jnp.where(r_i >= c_i, ws[g], 0.0)
        mixed = mixed + jnp.where(lane_g == g, _bdot(w, vln, "nn"), 0.0)
    return u * mixed


def _gmlp_fwd(proj, mo, lng, lnb, ws, bst, name):
    bsz, seq, _ = proj.shape

    def body(p_ref, mo_any, lng_ref, lnb_ref, ws_ref, bst_ref, o_ref):
        del mo_any
        o_ref[...] = _gmlp_fn(p_ref[...], lng_ref[...], lnb_ref[...], ws_ref[...], bst_ref[...]).astype(o_ref.dtype)

    return pl.pallas_call(
        body, name=name, grid=(bsz, seq // D_CHUNK),
        in_specs=[pl.BlockSpec((None, D_CHUNK, 512), lambda b, s: (b, s, P_D // 512)),
                  pl.BlockSpec(memory_space=pl.ANY), _vec_spec(256), _vec_spec(256),
                  pl.BlockSpec((4, D_CHUNK, D_CHUNK), lambda b, s: (0, 0, 0)),
                  pl.BlockSpec((D_CHUNK, LANES), lambda b, s: (0, 0))],
        out_specs=pl.BlockSpec((None, D_CHUNK, GROUP_WIDTH), lambda b, s: (b, s, 1280 // GROUP_WIDTH)),
        out_shape=jax.ShapeDtypeStruct(mo.shape, mo.dtype),
        input_output_aliases={1: 0},
        compiler_params=_cparams(("parallel", "parallel")),
    )(proj, mo, lng, lnb, ws, bst)


def _gmlp_bwd(dmo, dproj, proj, lng, lnb, ws, bst, name):
    bsz, seq, _ = proj.shape

    def body(do_ref, dp_any, p_ref, lng_ref, lnb_ref, ws_ref, bst_ref, dp_ref, dlg_ref, dlb_ref, dws_ref, dbst_ref):
        del dp_any
        first = jnp.logical_and(pl.program_id(0) == 0, pl.program_id(1) == 0)

        @pl.when(first)
        def _():
            dlg_ref[...] = jnp.zeros_like(dlg_ref)
            dlb_ref[...] = jnp.zeros_like(dlb_ref)
            dws_ref[...] = jnp.zeros_like(dws_ref)
            dbst_ref[...] = jnp.zeros_like(dbst_ref)

        _, vjp = jax.vjp(_gmlp_fn, p_ref[...], lng_ref[...], lnb_ref[...], ws_ref[...], bst_ref[...])
        duv, dlg, dlb, dws, dbst = vjp(do_ref[...])
        dp_ref[...] = duv.astype(dp_ref.dtype)
        dlg_ref[...] += dlg
        dlb_ref[...] += dlb
        dws_ref[...] += dws
        dbst_ref[...] += dbst

    const2 = lambda shape: pl.BlockSpec(shape, lambda b, s: (0,) * len(shape))
    return pl.pallas_call(
        body, name=name, grid=(bsz, seq // D_CHUNK),
        in_specs=[pl.BlockSpec((None, D_CHUNK, GROUP_WIDTH), lambda b, s: (b, s, 1280 // GROUP_WIDTH)),
                  pl.BlockSpec(memory_space=pl.ANY),
                  pl.BlockSpec((None, D_CHUNK, 512), lambda b, s: (b, s, P_D // 512)),
                  _vec_spec(256), _vec_spec(256), const2((4, D_CHUNK, D_CHUNK)), const2((D_CHUNK, LANES))],
        out_specs=[pl.BlockSpec((None, D_CHUNK, 512), lambda b, s: (b, s, P_D // 512)),
                   _vec_spec(256), _vec_spec(256), const2((4, D_CHUNK, D_CHUNK)), const2((D_CHUNK, LANES))],
        out_shape=[jax.ShapeDtypeStruct(dproj.shape, dproj.dtype), jax.ShapeDtypeStruct((1, 256), F32),
                   jax.ShapeDtypeStruct((1, 256), F32), jax.ShapeDtypeStruct((4, D_CHUNK, D_CHUNK), F32),
                   jax.ShapeDtypeStruct((D_CHUNK, LANES), F32)],
        input_output_aliases={1: 0},
        compiler_params=_cparams(("arbitrary", "arbitrary")),
    )(dmo, dproj, proj, lng, lnb, ws, bst)


def _ada_fwd(c_all, ada_w, name):
    n_b = c_all.shape[0]
    depth, d, cols = ada_w.shape

    def body(c_ref, w_ref, o_ref):
        cv = c_ref[...]
        act = (cv * jax.nn.sigmoid(cv)).astype(BF16)
        o_ref[...] = jnp.dot(act, w_ref[...].astype(BF16), preferred_element_type=F32)

    return pl.pallas_call(
        body, name=name, grid=(depth,),
        in_specs=[pl.BlockSpec((n_b, d), lambda l: (0, 0)), pl.BlockSpec((None, d, cols), lambda l: (l, 0, 0))],
        out_specs=pl.BlockSpec((None, n_b, cols), lambda l: (l, 0, 0)),
        out_shape=jax.ShapeDtypeStruct((depth, n_b, cols), F32),
        compiler_params=_cparams(("parallel",)),
    )(c_all, ada_w)


def _ada_bwd(c_all, dmod_cols, dmod_full, name):
    n_b, d = c_all.shape
    depth, _, cols = dmod_cols.shape
    full = dmod_full.shape[-1]

    def body(c_ref, dm_ref, df_ref, gw_ref, gb_ref):
        cv = c_ref[...]
        act = (cv * jax.nn.sigmoid(cv)).astype(BF16)
        gw_ref[...] = lax.dot_general(act, dm_ref[...].astype(BF16), (((0,), (0,)), ((), ())),
                                      preferred_element_type=F32)
        gb_ref[...] = jnp.sum(df_ref[...], axis=0, keepdims=True)

    return pl.pallas_call(
        body, name=name, grid=(depth,),
        in_specs=[pl.BlockSpec((n_b, d), lambda l: (0, 0)), pl.BlockSpec((None, n_b, cols), lambda l: (l, 0, 0)),
                  pl.BlockSpec((None, n_b, full), lambda l: (l, 0, 0))],
        out_specs=[pl.BlockSpec((None, d, cols), lambda l: (l, 0, 0)),
                   pl.BlockSpec((None, 1, full), lambda l: (l, 0, 0))],
        out_shape=[jax.ShapeDtypeStruct((depth, d, cols), F32), jax.ShapeDtypeStruct((depth, 1, full), F32)],
        compiler_params=_cparams(("parallel",)),
    )(c_all, dmod_cols, dmod_full)


def _adamw(gparts, w, m, v, name):
    n_p, rows, cols = gparts.shape
    tr = rows
    for cand in (512, 256, 128, 64, 32, 16):
        if rows % cand == 0 and rows > cand:
            tr = cand
            break

    def body(g_ref, w_ref, m_ref, v_ref, go_ref, do_ref, mo_ref, vo_ref):
        g = g_ref[0].astype(F32)
        for p in range(1, n_p):
            g = g + g_ref[p].astype(F32)
        m_new = ADAM_B1 * m_ref[...] + (1.0 - ADAM_B1) * g
        v_new = ADAM_B2 * v_ref[...] + (1.0 - ADAM_B2) * (g * g)
        m_hat = m_new / (1.0 - ADAM_B1 ** ADAM_STEP)
        v_hat = v_new / (1.0 - ADAM_B2 ** ADAM_STEP)
        go_ref[...] = g
        do_ref[...] = -ADAM_LR * (m_hat / (jnp.sqrt(v_hat) + ADAM_EPS) + ADAM_WD * w_ref[...])
        mo_ref[...] = m_new
        vo_ref[...] = v_new

    spec = pl.BlockSpec((tr, cols), lambda i: (i, 0))
    shp = jax.ShapeDtypeStruct((rows, cols), F32)
    return pl.pallas_call(
        body, name=name, grid=(rows // tr,),
        in_specs=[pl.BlockSpec((n_p, tr, cols), lambda i: (0, i, 0)), spec, spec, spec],
        out_specs=[spec, spec, spec, spec], out_shape=[shp, shp, shp, shp],
        compiler_params=_cparams(("parallel",)),
    )(gparts, w, m, v)


def _sum_parts(parts, name):
    n_p, rows, cols = parts.shape
    tr = 256 if rows % 256 == 0 else rows

    def body(p_ref, o_ref):
        acc = p_ref[0]
        for p in range(1, n_p):
            acc = acc + p_ref[p]
        o_ref[...] = acc

    return pl.pallas_call(
        body, name=name, grid=(rows // tr,),
        in_specs=[pl.BlockSpec((n_p, tr, cols), lambda i: (0, i, 0))],
        out_specs=pl.BlockSpec((tr, cols), lambda i: (i, 0)),
        out_shape=jax.ShapeDtypeStruct((rows, cols), F32),
        compiler_params=_cparams(("parallel",)),
    )(parts)


def _exchange(ins, out_shapes, plan, name):
    n_in, n_out, n_cp = len(ins), len(out_shapes), len(plan)
    flips = [(fx, fy, fc) for fx in (0, 1) for fy in (0, 1) for fc in (0, 1)][1:]

    def body(*refs):
        in_refs, out_refs = refs[:n_in], refs[n_in:n_in + n_out]
        send_sems, recv_sems, loc_sems = refs[n_in + n_out:]
        x, y, c = lax.axis_index("x"), lax.axis_index("y"), lax.axis_index("c")
        me = 4 * x + 2 * y + c
        peers = []
        for fx, fy, fc in flips:
            px, py, pc = (1 - x if fx else x), (1 - y if fy else y), (1 - c if fc else c)
            peers.append(((px, py, pc), 4 * px + 2 * py + pc))

        def sel(ref, idx):
            return ref.at[idx] if idx else ref

        def remote(n, k, src_dev_slot, dst_for):
            i, in_sel, o, out_sel = plan[n]
            dev, idx = peers[k]
            return pltpu.make_async_remote_copy(
                src_ref=sel(in_refs[i], in_sel(dst_for)), dst_ref=sel(out_refs[o], out_sel(src_dev_slot)),
                send_sem=send_sems.at[n, k], recv_sem=recv_sems.at[n, k],
                device_id=dev, device_id_type=pl.DeviceIdType.MESH)

        local = []
        for n, (i, in_sel, o, out_sel) in enumerate(plan):
            cp = pltpu.make_async_copy(sel(in_refs[i], in_sel(me)), sel(out_refs[o], out_sel(me)), loc_sems.at[n])
            cp.start()
            local.append(cp)
        sends = []
        for k in range(len(flips)):
            for n in range(n_cp):
                cp = remote(n, k, me, peers[k][1])
                cp.start()
                sends.append(cp)
        for k in range(len(flips)):
            for n in range(n_cp):
                remote(n, k, peers[k][1], me).wait_recv()
        for cp in sends:
            cp.wait_send()
        for cp in local:
            cp.wait()

    any_spec = pl.BlockSpec(memory_space=pl.ANY)
    return pl.pallas_call(
        body, name=name,
        in_specs=[any_spec] * n_in, out_specs=[any_spec] * n_out, out_shape=list(out_shapes),
        scratch_shapes=[pltpu.SemaphoreType.DMA((n_cp, N_DEV - 1)), pltpu.SemaphoreType.DMA((n_cp, N_DEV - 1)),
                        pltpu.SemaphoreType.DMA((n_cp,))],
    )(*ins)


def _all_gather(arrs, name):
    n = len(arrs)

    def body(*refs):
        in_refs, out_refs = refs[:n], refs[n:2 * n]
        send_sems, recv_sems, loc_sems = refs[2 * n:]
        x, y, c = lax.axis_index("x"), lax.axis_index("y"), lax.axis_index("c")
        me, sibling = (x, y, c), (x, y, 1 - c)
        chips = [(1 - x, y), (x, 1 - y), (1 - x, 1 - y)]

        def copy(a, k, block, to, src=None):
            slot = out_refs[a].at[4 * block[0] + 2 * block[1] + block[2]]
            return pltpu.make_async_remote_copy(
                src_ref=slot if src is None else src, dst_ref=slot, send_sem=send_sems.at[a, k],
                recv_sem=recv_sems.at[a, k], device_id=to, device_id_type=pl.DeviceIdType.MESH)

        mine = [pltpu.make_async_copy(in_refs[a], out_refs[a].at[4 * x + 2 * y + c], loc_sems.at[a])
                for a in range(n)]
        for cp in mine:
            cp.start()
        first = []
        for a in range(n):
            first.append(copy(a, 0, me, sibling, src=in_refs[a]))
            first += [copy(a, 1 + j, me, (*chip, c), src=in_refs[a]) for j, chip in enumerate(chips)]
        for cp in first:
            cp.start()
        passed = []
        for j, chip in enumerate(chips):
            for a in range(n):
                copy(a, 1 + j, (*chip, c), me).wait_recv()
                cp = copy(a, 4 + j, (*chip, c), sibling)
                cp.start()
                passed.append(cp)
        for a in range(n):
            copy(a, 0, sibling, me).wait_recv()
        for j, chip in enumerate(chips):
            for a in range(n):
                copy(a, 4 + j, (*chip, 1 - c), me).wait_recv()
        for cp in first + passed:
            cp.wait_send()
        for cp in mine:
            cp.wait()

    any_spec = pl.BlockSpec(memory_space=pl.ANY)
    return pl.pallas_call(
        body, name=name, in_specs=[any_spec] * n, out_specs=[any_spec] * n,
        out_shape=[jax.ShapeDtypeStruct((N_DEV,) + a.shape, a.dtype) for a in arrs],
        scratch_shapes=[pltpu.SemaphoreType.DMA((n, N_DEV - 1)), pltpu.SemaphoreType.DMA((n, N_DEV - 1)),
                        pltpu.SemaphoreType.DMA((n,))],
    )(*arrs)


def _reduce_scatter_push(groups, name):
    ins, shapes, plan = [], [], []
    for w, layers in enumerate(groups):
        shapes.append(jax.ShapeDtypeStruct((N_DEV, len(layers)) + layers[0].shape[1:], layers[0].dtype))
        for l, arr in enumerate(layers):
            plan.append((len(ins), (lambda p: (p,)), w, (lambda s, l=l: (s, l))))
            ins.append(arr)
    return _exchange(ins, shapes, plan, name)


def _ffn_fwd(x, mod, w_in, w_out, lng, lnb, rows, tag):
    bsz, seq, d = x.shape
    t = bsz * seq
    h = _modulate(x, mod, rows[0], rows[1], f"modulate_{tag}")
    z, a = _ffn_in_swiglu(h.reshape(t, d), w_in, f"ffn_in_{tag}")
    f = _matmul(a, w_out, mode="nn", group_out=False, out_dtype=F32, tm=512, tk=a.shape[2],
                name=f"ffn_out_{tag}").reshape(bsz, seq, d)
    y = _res_ln(x, f, mod, lng, lnb, rows[2], 0.5, f"res_ln_{tag}")
    return y, (x, h, z, a, f)


def _ffn_bwd(dy, saved, mod, w_in, w_out, lng, lnb, rows, tag):
    x, h, z, a, f = saved
    bsz, seq, d = x.shape
    t = bsz * seq
    dx_res, df, dgate, dlg, dlb = _res_ln_bwd(dy, x, f, mod, lng, lnb, rows[2], 0.5, f"res_ln_bwd_{tag}")
    df2 = df.reshape(1, t, d)
    dw_out = _matmul(a, df2, mode="tn", group_out=True, out_dtype=BF16, tm=a.shape[2], tk=512,
                     name=f"ffn_out_dw_{tag}")
    dz = _ffn_out_dx_swiglu(df.reshape(t, d), w_out, z, f"ffn_out_dx_{tag}").reshape(N_DEV, t, -1)
    dh = _matmul(dz, w_in, mode="nt", group_out=False, out_dtype=F32, tm=512, tk=dz.shape[2],
                 name=f"ffn_in_dx_{tag}").reshape(bsz, seq, d)
    dw_in = _matmul(h.reshape(1, t, d), dz, mode="tn", group_out=True, out_dtype=BF16, tm=512, tk=512,
                    name=f"ffn_in_dw_{tag}")
    dx, dsh, dsc = _modulate_bwd(dh, x, mod, dx_res, rows[1], f"modulate_bwd_{tag}")
    return dx, (dsh, dsc, dgate), dw_in, dw_out, dlg, dlb


def _mixer_fwd(x, mod, wts, small, lng, lnb, layer, tabs):
    bsz, seq, d = x.shape
    t = bsz * seq
    h = _modulate(x, mod, 3, 4, "modulate_mix")
    proj = _matmul(h.reshape(1, t, d), wts["mix_in"][None], mode="nn", group_out=True, out_dtype=F32, tm=256, tk=d,
                   name="mix_in").reshape(bsz, seq, PACK_W)
    mo, states = _hgrn_fwd(proj, small["lb_logits8"], small["hgrn_norm_g"], layer, f"hgrn_fwd_l{layer}")
    q, kv = _mla_pre(proj, small["q_norm_g"], small["kv_norm_g"], wts["uq"], wts["ukv"], tabs, "mla_pre")
    mla_scale = float((B_NOPE + B_ROPE) ** -0.5)
    mo, lse_b = _attn_fwd(q, 0, kv, 0, mo, 2, None, mla_scale, "mla_attn_fwd")
    fg = _fox_gate(proj, small["fox_b_f"], "fox_gate")
    gates = (fg, jnp.swapaxes(fg[:, :, 0:8], 1, 2))
    fox_scale = float(HEAD_DIM ** -0.5)
    mo, lse_c = _attn_fwd(proj, P_CQ // LANES, proj, P_CKV // LANES, mo, 6, gates, fox_scale, "fox_attn_fwd")
    mo = _gmlp_fwd(proj, mo, small["gmlp_ln_g"], small["gmlp_ln_b"], small["gmlp_w_s"], small["gmlp_bst"],
                   "gmlp_fwd")
    mixed = _matmul(mo.reshape(1, t, MO_W), wts["mix_out"][None], mode="nn", group_out=True, out_dtype=F32,
                    tm=512, tk=MO_W, name="mix_out").reshape(bsz, seq, d)
    y = _res_ln(x, mixed, mod, lng, lnb, 5, 1.0, "res_ln_mix")
    return y, (x, h, proj, mo, states, q, kv, lse_b, gates, lse_c, mixed)


def _mixer_bwd(dy, saved, mod, wts, small, lng, lnb, layer, tabs):
    x, h, proj, mo, states, q, kv, lse_b, gates, lse_c, mixed = saved
    bsz, seq, d = x.shape
    t = bsz * seq
    dx_res, dmixed, dgate, dlg, dlb = _res_ln_bwd(dy, x, mixed, mod, lng, lnb, 5, 1.0, "res_ln_bwd_mix")
    dm2 = dmixed.reshape(1, t, d)
    dmo = _matmul(dm2, wts["mix_out"][None], mode="nt", group_out=True, out_dtype=F32, tm=512, tk=d,
                  name="mix_out_dx").reshape(bsz, seq, MO_W)
    dw_out = _matmul(mo.reshape(1, t, MO_W), dm2, mode="tn", group_out=True, out_dtype=F32, tm=512, tk=512,
                     name="mix_out_dw")[0]
    g = {}
    dproj, g["lb_logits8"], g["hgrn_norm_g"] = _hgrn_bwd(dmo, proj, states, small["lb_logits8"],
                                                         small["hgrn_norm_g"], layer, f"hgrn_bwd_l{layer}")
    mla_scale = float((B_NOPE + B_ROPE) ** -0.5)
    dq, delta_b, _ = _attn_bwd_q(q, 0, kv, 0, mo, dmo, 2, lse_b, None, mla_scale,
                                 jax.ShapeDtypeStruct((bsz, seq, 512), F32), 0, "mla_attn_bwd_q")
    dkv, _ = _attn_bwd_kv(q, 0, kv, 0, dmo, 2, lse_b, delta_b, None, mla_scale,
                          jax.ShapeDtypeStruct((bsz, seq, 1024), F32), 0, "mla_attn_bwd_kv")
    dproj, g["q_norm_g"], g["kv_norm_g"], g["uq"], g["ukv"] = _mla_pre_bwd(
        dq, dkv, dproj, proj, small["q_norm_g"], small["kv_norm_g"], wts["uq"], wts["ukv"], tabs, "mla_pre_bwd")
    fox_scale = float(HEAD_DIM ** -0.5)
    dproj, delta_c, dfq = _attn_bwd_q(proj, P_CQ // LANES, proj, P_CKV // LANES, mo, dmo, 6, lse_c, gates,
                                      fox_scale, dproj, P_CQ // LANES, "fox_attn_bwd_q")
    dproj, dfk = _attn_bwd_kv(proj, P_CQ // LANES, proj, P_CKV // LANES, dmo, 6, lse_c, delta_c, gates, fox_scale,
                              dproj, P_CKV // (2 * LANES), "fox_attn_bwd_kv")
    dcum = jnp.swapaxes(dfq[..., 0], 1, 2) + jnp.swapaxes(dfk[:, :, 0, :], 1, 2)
    dcum = jnp.pad(dcum, ((0, 0), (0, 0), (0, LANES - N_HEADS)))
    dproj, g["fox_b_f"] = _fox_gate_bwd(dcum, dproj, proj, small["fox_b_f"], "fox_gate_bwd")
    dproj, g["gmlp_ln_g"], g["gmlp_ln_b"], g["gmlp_w_s"], g["gmlp_bst"] = _gmlp_bwd(
        dmo, dproj, proj, small["gmlp_ln_g"], small["gmlp_ln_b"], small["gmlp_w_s"], small["gmlp_bst"], "gmlp_bwd")
    dp2 = dproj.reshape(1, t, PACK_W)
    dh = _matmul(dp2, wts["mix_in"][None], mode="nt", group_out=True, out_dtype=F32, tm=256, tk=PACK_W,
                 name="mix_in_dx").reshape(bsz, seq, d)
    dw_in = _matmul(h.reshape(1, t, d), dp2, mode="tn", group_out=True, out_dtype=BF16, tm=256, tk=512,
                    name="mix_in_dw")[0]
    dx, dsh, dsc = _modulate_bwd(dh, x, mod, dx_res, 4, "modulate_bwd_mix")
    return dx, (dsh, dsc, dgate), dw_in, dw_out, g, dlg, dlb


def _small_views(p, layer):
    return {
        "lb_logits8": jnp.pad(p["hgrn_lb_logits"], ((0, 8 - DEPTH), (0, 0))),
        "hgrn_norm_g": p["hgrn_norm_g"][layer][None],
        "q_norm_g": p["mla_q_norm_g"][layer][None],
        "kv_norm_g": p["mla_kv_norm_g"][layer][None],
        "fox_b_f": jnp.pad(p["fox_b_f"][layer][None], ((0, 0), (0, LANES - N_HEADS))),
        "gmlp_ln_g": p["gmlp_ln_g"][layer][None],
        "gmlp_ln_b": p["gmlp_ln_b"][layer][None],
        "gmlp_w_s": p["gmlp_w_s"][layer],
        "gmlp_bst": jnp.pad(p["gmlp_b_s"][layer].T, ((0, 0), (0, LANES - N_HEADS))),
    }


def _local_step(x, mod, target, full, p):
    bsz, seq, d = x.shape
    tabs = _rope_tables(seq)
    saved = []
    for l in range(DEPTH):
        w, sm = full[l], _small_views(p, l)
        lng, lnb = p["ln_g"][l], p["ln_b"][l]
        x, s1 = _ffn_fwd(x, mod[l], w["ffn1_in"], w["ffn1_out"], lng[0:1], lnb[0:1], (0, 1, 2), "ffn1")
        x, s2 = _mixer_fwd(x, mod[l], w, sm, lng[1:2], lnb[1:2], l, tabs)
        x, s3 = _ffn_fwd(x, mod[l], w["ffn2_in"], w["ffn2_out"], lng[2:3], lnb[2:3], (6, 7, 8), "ffn2")
        saved.append((s1, s2, s3))
    dx, loss = _loss_head(x, target, "loss_head")
    big, small, dmods = [None] * DEPTH, [None] * DEPTH, [None] * DEPTH
    for l in reversed(range(DEPTH)):
        w, sm = full[l], _small_views(p, l)
        lng, lnb = p["ln_g"][l], p["ln_b"][l]
        s1, s2, s3 = saved[l]
        dx, dm3, dwi2, dwo2, dlg2, dlb2 = _ffn_bwd(dx, s3, mod[l], w["ffn2_in"], w["ffn2_out"], lng[2:3], lnb[2:3],
                                                   (6, 7, 8), "ffn2")
        dx, dm2, dwmi, dwmo, g, dlg1, dlb1 = _mixer_bwd(dx, s2, mod[l], w, sm, lng[1:2], lnb[1:2], l, tabs)
        dx, dm1, dwi1, dwo1, dlg0, dlb0 = _ffn_bwd(dx, s1, mod[l], w["ffn1_in"], w["ffn1_out"], lng[0:1], lnb[0:1],
                                                   (0, 1, 2), "ffn1")
        dmods[l] = jnp.concatenate(list(dm1) + list(dm2) + list(dm3), axis=1)
        big[l] = {"ffn1_in": dwi1, "ffn1_out": dwo1, "ffn2_in": dwi2, "ffn2_out": dwo2, "mix_in": dwmi,
                  "mix_out": dwmo}
        g["ln_g"] = jnp.concatenate([dlg0, dlg1, dlg2], axis=0)
        g["ln_b"] = jnp.concatenate([dlb0, dlb1, dlb2], axis=0)
        small[l] = g
    return loss, dx, jnp.stack(dmods), big, small


_BIG = ("ffn1_in", "ffn1_out", "ffn2_in", "ffn2_out", "mix_in", "mix_out")


def _small_grad_list(small, loss):
    def both(fn):
        return jnp.stack([fn(small[l]) for l in range(DEPTH)])

    uq_src, ukv_src = _uq_src(), _ukv_src()
    return [
        ("loss", loss.reshape(1)),
        ("ln_g", both(lambda g: g["ln_g"])), ("ln_b", both(lambda g: g["ln_b"])),
        ("hgrn_lb_logits", small[0]["lb_logits8"][:DEPTH] + small[1]["lb_logits8"][:DEPTH]),
        ("hgrn_norm_g", both(lambda g: g["hgrn_norm_g"][0])),
        ("mla_q_norm_g", both(lambda g: g["q_norm_g"][0])),
        ("mla_kv_norm_g", both(lambda g: g["kv_norm_g"][0])),
        ("mla_w_uq", both(lambda g: _unpack_cols(g["uq"], uq_src, 384))),
        ("mla_w_ukv", both(lambda g: _unpack_cols(g["ukv"], ukv_src, 512))),
        ("fox_b_f", both(lambda g: g["fox_b_f"][0, :N_HEADS])),
        ("gmlp_ln_g", both(lambda g: g["gmlp_ln_g"][0])), ("gmlp_ln_b", both(lambda g: g["gmlp_ln_b"][0])),
        ("gmlp_w_s", both(lambda g: g["gmlp_w_s"])),
        ("gmlp_b_s", both(lambda g: g["gmlp_bst"][:, :N_HEADS].T)),
    ]


_PACK_COLS = 512


def _pack_small(items):
    flat = jnp.concatenate([a.reshape(-1).astype(F32) for _, a in items])
    n = flat.shape[0]
    tile = 8 * _PACK_COLS
    flat = jnp.pad(flat, (0, (-n) % tile))
    return flat.reshape(-1, _PACK_COLS)


def _unpack_small(buf, items):
    flat = buf.reshape(-1)
    out, off = {}, 0
    for name, a in items:
        out[name] = flat[off:off + a.size].reshape(a.shape)
        off += a.size
    return out


def _as2d(a):
    return a.reshape(-1, a.shape[-1])


def kernel(x, c, ada_w, ada_b, ln_g, ln_b, ffn1_w_in, ffn1_w_out, ffn2_w_in, ffn2_w_out, mix_w_in, mix_w_out, hgrn_lb_logits, hgrn_norm_g, mla_q_norm_g, mla_kv_norm_g, mla_w_uq, mla_w_ukv, fox_b_f, gmlp_ln_g, gmlp_ln_b, gmlp_w_s, gmlp_b_s, loss_target, m_ada_w, m_ada_b, m_ln_g, m_ln_b, m_ffn1_w_in, m_ffn1_w_out, m_ffn2_w_in, m_ffn2_w_out, m_mix_w_in, m_mix_w_out, m_hgrn_lb_logits, m_hgrn_norm_g, m_mla_q_norm_g, m_mla_kv_norm_g, m_mla_w_uq, m_mla_w_ukv, m_fox_b_f, m_gmlp_ln_g, m_gmlp_ln_b, m_gmlp_w_s, m_gmlp_b_s, v_ada_w, v_ada_b, v_ln_g, v_ln_b, v_ffn1_w_in, v_ffn1_w_out, v_ffn2_w_in, v_ffn2_w_out, v_mix_w_in, v_mix_w_out, v_hgrn_lb_logits, v_hgrn_norm_g, v_mla_q_norm_g, v_mla_kv_norm_g, v_mla_w_uq, v_mla_w_ukv, v_fox_b_f, v_gmlp_ln_g, v_gmlp_ln_b, v_gmlp_w_s, v_gmlp_b_s):
    names = ["ada_w", "ada_b", "ln_g", "ln_b", "ffn1_w_in", "ffn1_w_out", "ffn2_w_in", "ffn2_w_out", "mix_w_in",
             "mix_w_out", "hgrn_lb_logits", "hgrn_norm_g", "mla_q_norm_g", "mla_kv_norm_g", "mla_w_uq", "mla_w_ukv",
             "fox_b_f", "gmlp_ln_g", "gmlp_ln_b", "gmlp_w_s", "gmlp_b_s"]
    w = dict(zip(names, [ada_w, ada_b, ln_g, ln_b, ffn1_w_in, ffn1_w_out, ffn2_w_in, ffn2_w_out, mix_w_in, mix_w_out,
                         hgrn_lb_logits, hgrn_norm_g, mla_q_norm_g, mla_kv_norm_g, mla_w_uq, mla_w_ukv, fox_b_f,
                         gmlp_ln_g, gmlp_ln_b, gmlp_w_s, gmlp_b_s]))
    m = dict(zip(names, [m_ada_w, m_ada_b, m_ln_g, m_ln_b, m_ffn1_w_in, m_ffn1_w_out, m_ffn2_w_in, m_ffn2_w_out,
                         m_mix_w_in, m_mix_w_out, m_hgrn_lb_logits, m_hgrn_norm_g, m_mla_q_norm_g, m_mla_kv_norm_g,
                         m_mla_w_uq, m_mla_w_ukv, m_fox_b_f, m_gmlp_ln_g, m_gmlp_ln_b, m_gmlp_w_s, m_gmlp_b_s]))
    v = dict(zip(names, [v_ada_w, v_ada_b, v_ln_g, v_ln_b, v_ffn1_w_in, v_ffn1_w_out, v_ffn2_w_in, v_ffn2_w_out,
                         v_mix_w_in, v_mix_w_out, v_hgrn_lb_logits, v_hgrn_norm_g, v_mla_q_norm_g, v_mla_kv_norm_g,
                         v_mla_w_uq, v_mla_w_ukv, v_fox_b_f, v_gmlp_ln_g, v_gmlp_ln_b, v_gmlp_w_s, v_gmlp_b_s]))
    bsz, seq, d = x.shape
    me = 4 * lax.axis_index("x") + 2 * lax.axis_index("y") + lax.axis_index("c")
    mix_src, uq_src, ukv_src, mo_src = _mix_in_src(), _uq_src(), _ukv_src(), _mo_src()

    shard_names = ["ffn1_w_in", "ffn1_w_out", "ffn2_w_in", "ffn2_w_out", "mix_w_in", "mix_w_out", "mla_w_uq",
                   "mla_w_ukv"]
    shards = []
    for l in range(DEPTH):
        for n in shard_names:
            a = w[n][l]
            if n == "mix_w_in":
                a = _pack_cols(a, mix_src)
            shards.append(a.astype(BF16))
    gathered = _all_gather(shards + [c, ln_g, ln_b], "gather_weights")
    c_all = gathered[-3].reshape(N_DEV * bsz, d)
    ln_g_full = jnp.moveaxis(gathered[-2], 0, 2).reshape(DEPTH, 3, d)
    ln_b_full = jnp.moveaxis(gathered[-1], 0, 2).reshape(DEPTH, 3, d)

    full = []
    for l in range(DEPTH):
        gw = dict(zip(shard_names, gathered[l * len(shard_names):(l + 1) * len(shard_names)]))
        uq = jnp.moveaxis(gw["mla_w_uq"], 0, 1).reshape(256, 384)
        ukv = jnp.moveaxis(gw["mla_w_ukv"], 0, 1).reshape(128, 512)
        full.append({
            "ffn1_in": gw["ffn1_w_in"], "ffn1_out": gw["ffn1_w_out"].reshape(4, 704, d),
            "ffn2_in": gw["ffn2_w_in"], "ffn2_out": gw["ffn2_w_out"].reshape(4, 704, d),
            "mix_in": gw["mix_w_in"].reshape(d, PACK_W),
            "mix_out": _pack_cols(gw["mix_w_out"].reshape(d, d).T, mo_src).T,
            "uq": _pack_cols(uq, uq_src), "ukv": _pack_cols(ukv, ukv_src),
        })

    mod_cols = _ada_fwd(c_all, ada_w, "ada_fwd")
    mod_all, = _all_gather([mod_cols], "gather_mod")
    mod_mine = lax.dynamic_slice_in_dim(mod_all, me * bsz, bsz, axis=2)
    mod = jnp.moveaxis(mod_mine, 0, 2).reshape(DEPTH, bsz, N_MOD * d) + ada_b[:, None, :]
    mod = mod.reshape(DEPTH, bsz, N_MOD, d)

    p = dict(w)
    p["ln_g"], p["ln_b"] = ln_g_full, ln_b_full
    loss, grad_x, dmod, big, small = _local_step(x, mod, loss_target, full, p)

    dmod_all, = _all_gather([dmod.reshape(DEPTH, bsz, N_MOD * d)], "gather_dmod")
    dmod_full = jnp.moveaxis(dmod_all, 0, 1).reshape(DEPTH, N_DEV * bsz, N_MOD * d)
    cols = ada_w.shape[2]
    dmod_cols = lax.dynamic_slice_in_dim(dmod_full, me * cols, cols, axis=2)
    g_ada_w, g_ada_b = _ada_bwd(c_all, dmod_cols, dmod_full, "ada_bwd")

    def chunks(name, arr):
        if name in ("ffn1_in", "ffn2_in"):
            return arr
        if name in ("ffn1_out", "ffn2_out"):
            return arr.reshape(N_DEV, arr.shape[1] // 2, d)
        if name == "mix_in":
            return arr.reshape(N_DEV, d // N_DEV, PACK_W)
        return _unpack_cols(arr.T, mo_src, d).T.astype(BF16).reshape(N_DEV, d // N_DEV, d)

    groups = [[chunks(n, big[l][n]) for l in range(DEPTH)] for n in _BIG]
    recv = dict(zip(_BIG, _reduce_scatter_push(groups, "scatter_grads")))
    recv["mix_in"] = _unpack_cols(recv["mix_in"], mix_src, MIX_ORIG_W)

    items = _small_grad_list(small, loss)
    parts, = _all_gather([_pack_small(items)], "gather_small")
    sg = _unpack_small(_sum_parts(parts, "sum_small"), items)

    out = {}

    def update(name, gparts):
        shape = w[name].shape
        res = _adamw(gparts, _as2d(w[name]), _as2d(m[name]), _as2d(v[name]), f"adamw_{name}")
        out[name] = tuple(r.reshape(shape) for r in res)

    big_of = {"ffn1_w_in": "ffn1_in", "ffn1_w_out": "ffn1_out", "ffn2_w_in": "ffn2_in", "ffn2_w_out": "ffn2_out",
              "mix_w_in": "mix_in", "mix_w_out": "mix_out"}
    for name, key in big_of.items():
        r = recv[key]
        update(name, r.reshape(N_DEV, -1, r.shape[-1]))
    update("ada_w", _as2d(g_ada_w)[None])
    update("ada_b", g_ada_b.reshape(1, DEPTH, N_MOD * d))
    for name in ("ln_g", "ln_b"):
        g_loc = lax.dynamic_slice_in_dim(sg[name], me * (d // N_DEV), d // N_DEV, axis=2)
        update(name, _as2d(g_loc)[None])
    for name, width in (("mla_w_uq", 48), ("mla_w_ukv", 64)):
        g_loc = lax.dynamic_slice_in_dim(sg[name], me * width, width, axis=2)
        update(name, _as2d(g_loc)[None])
    for name in ("hgrn_lb_logits", "hgrn_norm_g", "mla_q_norm_g", "mla_kv_norm_g", "fox_b_f", "gmlp_ln_g",
                 "gmlp_ln_b", "gmlp_w_s", "gmlp_b_s"):
        update(name, _as2d(sg[name])[None])

    return (sg["loss"][0], grad_x, *[out[n][0] for n in names], *[out[n][1] for n in names],
            *[out[n][2] for n in names], *[out[n][3] for n in names])
```

```python
import functools

import numpy as np
import jax
import jax.numpy as jnp
from jax import lax
from jax.experimental import pallas as pl
from jax.experimental.pallas import tpu as pltpu

F32 = jnp.float32
BF16 = jnp.bfloat16
HI = lax.Precision.HIGHEST

D_MODEL = 1024
DEPTH = 2
GROUP_WIDTH = 256
N_HEADS = 4
HEAD_DIM = 64
A_CHUNK = 16
LB_FLOOR = 1e-30
B_NOPE = 64
B_ROPE = 32
ROPE_THETA = 10000.0
D_CHUNK = 128
D_FF = 2816
N_MOD = 9
ALPHA = (2 * DEPTH) ** 0.25
LN_EPS = 1e-5
RMS_EPS = 1e-6
ADAM_LR = 0.001
ADAM_B1 = 0.9
ADAM_B2 = 0.999
ADAM_EPS = 1e-08
ADAM_WD = 0.01
ADAM_STEP = 10

N_DEV = 8
LANES = 128
PACK_W = 3712
MO_W = 1536
VMEM_LIMIT = 56 * 1024 * 1024
NEG = -1e30
ATTN_TILE = 512

MIX_ORIG_W = 2724
O_BCQ, O_BCKV, O_BKR, O_CQ, O_CK, O_CV, O_CF, O_DU, O_DV = 1024, 1280, 1408, 1440, 1696, 1952, 2208, 2212, 2468
P_B, P_KR, P_CQ, P_CKV, P_D, P_CF = 1024, 1408, 1536, 2048, 3072, 3584


_DN = {"nn": (((1,), (0,)), ((), ())), "nt": (((1,), (1,)), ((), ())), "tn": (((0,), (0,)), ((), ()))}


def _raw_bdot(a, b, mode):
    return lax.dot_general(a.astype(BF16), b.astype(BF16), _DN[mode], preferred_element_type=F32)


@functools.partial(jax.custom_vjp, nondiff_argnums=(2,))
def _bdot(a, b, mode):
    return _raw_bdot(a, b, mode)


def _bdot_fwd(a, b, mode):
    return _raw_bdot(a, b, mode), (a, b)


def _bdot_bwd(mode, res, g):
    a, b = res
    if mode == "nn":
        return _raw_bdot(g, b, "nt"), _raw_bdot(a, g, "tn")
    if mode == "nt":
        return _raw_bdot(g, b, "nn"), _raw_bdot(g, a, "tn")
    return _raw_bdot(b, g, "nt"), _raw_bdot(a, g, "nn")


_bdot.defvjp(_bdot_fwd, _bdot_bwd)


def _cparams(sem):
    return pltpu.CompilerParams(dimension_semantics=sem, vmem_limit_bytes=VMEM_LIMIT)


def _mix_in_src():
    src = -np.ones(PACK_W, np.int64)
    src[0:P_KR] = np.arange(0, O_BKR)
    src[P_KR + 64:P_KR + 80] = O_BKR + np.arange(16)
    src[P_KR + 96:P_KR + 112] = O_BKR + 16 + np.arange(16)
    for h in range(N_HEADS):
        src[P_CQ + 128 * h:P_CQ + 128 * h + 64] = O_CQ + 64 * h + np.arange(64)
        src[P_CKV + 256 * h:P_CKV + 256 * h + 64] = O_CK + 64 * h + np.arange(64)
        src[P_CKV + 256 * h + 128:P_CKV + 256 * h + 192] = O_CV + 64 * h + np.arange(64)
    src[P_D:P_D + 512] = O_DU + np.arange(512)
    src[P_CF:P_CF + 4] = O_CF + np.arange(4)
    return src


def _uq_src():
    src = -np.ones(512, np.int64)
    for h in range(N_HEADS):
        src[128 * h:128 * h + 64] = 96 * h + np.arange(64)
        src[128 * h + 64:128 * h + 80] = 96 * h + 64 + np.arange(16)
        src[128 * h + 96:128 * h + 112] = 96 * h + 80 + np.arange(16)
    return src


def _ukv_src():
    src = -np.ones(1024, np.int64)
    for h in range(N_HEADS):
        src[256 * h:256 * h + 64] = 128 * h + np.arange(64)
        src[256 * h + 128:256 * h + 192] = 128 * h + 64 + np.arange(64)
    return src


def _mo_src():
    src = -np.ones(MO_W, np.int64)
    src[0:256] = np.arange(256)
    for g in range(2):
        for h in range(N_HEADS):
            src[256 + 512 * g + 128 * h:256 + 512 * g + 128 * h + 64] = 256 + 256 * g + 64 * h + np.arange(64)
    src[1280:1536] = 768 + np.arange(256)
    return src


def _runs(idx):
    runs, i = [], 0
    while i < len(idx):
        j = i + 1
        while j < len(idx) and ((idx[i] < 0 and idx[j] < 0) or (idx[i] >= 0 and idx[j] == idx[i] + j - i)):
            j += 1
        runs.append((int(idx[i]), j - i))
        i = j
    return runs


def _take_runs(w, idx):
    parts = [jnp.zeros(w.shape[:-1] + (n,), w.dtype) if s < 0 else lax.slice_in_dim(w, s, s + n, axis=w.ndim - 1)
             for s, n in _runs(idx)]
    return jnp.concatenate(parts, axis=-1)


def _pack_cols(w, src):
    return _take_runs(w, src)


def _unpack_cols(wp, src, n):
    dst = np.zeros(n, np.int64)
    dst[src[src >= 0]] = np.nonzero(src >= 0)[0]
    return _take_runs(wp, dst)


def _rope_tables(seq):
    half = B_ROPE // 2
    inv_freq = ROPE_THETA ** (-jnp.arange(half, dtype=F32) / half)
    ang = jnp.arange(seq).astype(F32)[:, None] * inv_freq[None, :]
    cos, sin = jnp.cos(ang), jnp.sin(ang)
    z16 = jnp.zeros((seq, 16), F32)
    c = jnp.concatenate([jnp.ones((seq, 64), F32), cos, z16, cos, z16], axis=1)
    s1 = jnp.concatenate([jnp.zeros((seq, 64), F32), -sin, z16, z16, z16], axis=1)
    s2 = jnp.concatenate([jnp.zeros((seq, 64), F32), z16, z16, sin, z16], axis=1)
    return c, s1, s2


def _matmul(a, b, *, mode, group_out, out_dtype, tm, tk, name):
    ga, gb = a.shape[0], b.shape[0]
    g_n = max(ga, gb)
    if mode == "tn":
        k_dim, m_dim = a.shape[1:]
    else:
        m_dim, k_dim = a.shape[1:]
    n_dim = b.shape[1] if mode == "nt" else b.shape[2]
    assert m_dim % tm == 0 and k_dim % tk == 0
    kt = k_dim // tk
    n_red = kt if group_out else g_n * kt
    g_out = g_n if group_out else 1

    def split(g, r):
        return (g, r) if group_out else (r // kt, r % kt)

    def a_map(g, i, r):
        gg, kk = split(g, r)
        gg = gg if ga > 1 else 0
        return (gg, kk, i) if mode == "tn" else (gg, i, kk)

    def b_map(g, i, r):
        gg, kk = split(g, r)
        gg = gg if gb > 1 else 0
        return (gg, 0, kk) if mode == "nt" else (gg, kk, 0)

    a_blk = (None, tk, tm) if mode == "tn" else (None, tm, tk)
    b_blk = (None, n_dim, tk) if mode == "nt" else (None, tk, n_dim)
    dn = _DN[mode]

    def body(a_ref, b_ref, o_ref, *scratch):
        part = lax.dot_general(a_ref[...].astype(BF16), b_ref[...].astype(BF16), dn, preferred_element_type=F32)
        if n_red == 1:
            o_ref[...] = part.astype(o_ref.dtype)
            return
        acc_ref, = scratch
        r = pl.program_id(2)

        @pl.when(r == 0)
        def _():
            acc_ref[...] = part

        @pl.when(r > 0)
        def _():
            acc_ref[...] += part

        @pl.when(r == n_red - 1)
        def _():
            o_ref[...] = acc_ref[...].astype(o_ref.dtype)

    return pl.pallas_call(
        body, name=name, grid=(g_out, m_dim // tm, n_red),
        in_specs=[pl.BlockSpec(a_blk, a_map), pl.BlockSpec(b_blk, b_map)],
        out_specs=pl.BlockSpec((None, tm, n_dim), lambda g, i, r: (g, i, 0)),
        out_shape=jax.ShapeDtypeStruct((g_out, m_dim, n_dim), out_dtype),
        scratch_shapes=[] if n_red == 1 else [pltpu.VMEM((tm, n_dim), F32)],
        compiler_params=_cparams(("parallel", "parallel", "arbitrary")),
    )(a, b)


def _row_spec(ts, d):
    return pl.BlockSpec((None, ts, d), lambda b, s: (b, s, 0))


def _mod_spec(d):
    return pl.BlockSpec((None, N_MOD, d), lambda b, s: (b, 0, 0))


def _vec_spec(d):
    return pl.BlockSpec((1, d), lambda b, s: (0, 0))


def _bvec_spec(d):
    return pl.BlockSpec((None, 1, d), lambda b, s: (b, 0, 0))


def _modulate(x, mod, sh_row, sc_row, name, ts=512):
    bsz, seq, d = x.shape

    def body(x_ref, mod_ref, o_ref):
        sh = mod_ref[sh_row:sh_row + 1, :]
        sc = mod_ref[sc_row:sc_row + 1, :]
        o_ref[...] = (x_ref[...] * (1.0 + sc) + sh).astype(o_ref.dtype)

    return pl.pallas_call(
        body, name=name, grid=(bsz, seq // ts),
        in_specs=[_row_spec(ts, d), _mod_spec(d)], out_specs=_row_spec(ts, d),
        out_shape=jax.ShapeDtypeStruct((bsz, seq, d), BF16),
        compiler_params=_cparams(("parallel", "parallel")),
    )(x, mod)


def _modulate_bwd(dh, x, mod, dx_res, sc_row, name, ts=512):
    bsz, seq, d = x.shape

    def body(dh_ref, x_ref, mod_ref, dxr_ref, dx_ref, dsh_ref, dsc_ref):
        s = pl.program_id(1)
        sc = mod_ref[sc_row:sc_row + 1, :]
        dh_v = dh_ref[...]
        dx_ref[...] = dxr_ref[...] + dh_v * (1.0 + sc)
        psh = jnp.sum(dh_v, axis=0, keepdims=True)
        psc = jnp.sum(dh_v * x_ref[...], axis=0, keepdims=True)

        @pl.when(s == 0)
        def _():
            dsh_ref[...] = psh
            dsc_ref[...] = psc

        @pl.when(s > 0)
        def _():
            dsh_ref[...] += psh
            dsc_ref[...] += psc

    return pl.pallas_call(
        body, name=name, grid=(bsz, seq // ts),
        in_specs=[_row_spec(ts, d), _row_spec(ts, d), _mod_spec(d), _row_spec(ts, d)],
        out_specs=[_row_spec(ts, d), _bvec_spec(d), _bvec_spec(d)],
        out_shape=[jax.ShapeDtypeStruct((bsz, seq, d), F32), jax.ShapeDtypeStruct((bsz, 1, d), F32),
                   jax.ShapeDtypeStruct((bsz, 1, d), F32)],
        compiler_params=_cparams(("parallel", "arbitrary")),
    )(dh, x, mod, dx_res)


def _res_ln_fn(x, f, g, lng, lnb, cmul):
    r = ALPHA * x + (cmul * (1.0 + g)) * f
    mu = jnp.mean(r, axis=-1, keepdims=True)
    rc = r - mu
    var = jnp.mean(rc * rc, axis=-1, keepdims=True)
    return rc * lax.rsqrt(var + LN_EPS) * lng + lnb


def _res_ln(x, f, mod, lng, lnb, g_row, cmul, name, ts=512):
    bsz, seq, d = x.shape

    def body(x_ref, f_ref, mod_ref, lng_ref, lnb_ref, o_ref):
        g = mod_ref[g_row:g_row + 1, :]
        o_ref[...] = _res_ln_fn(x_ref[...], f_ref[...], g, lng_ref[...], lnb_ref[...], cmul)

    return pl.pallas_call(
        body, name=name, grid=(bsz, seq // ts),
        in_specs=[_row_spec(ts, d), _row_spec(ts, d), _mod_spec(d), _vec_spec(d), _vec_spec(d)],
        out_specs=_row_spec(ts, d), out_shape=jax.ShapeDtypeStruct((bsz, seq, d), F32),
        compiler_params=_cparams(("parallel", "parallel")),
    )(x, f, mod, lng, lnb)


def _res_ln_bwd(dy, x, f, mod, lng, lnb, g_row, cmul, name, ts=256):
    bsz, seq, d = x.shape

    def body(dy_ref, x_ref, f_ref, mod_ref, lng_ref, lnb_ref, dx_ref, df_ref, dg_ref, dlg_ref, dlb_ref):
        b, s = pl.program_id(0), pl.program_id(1)
        g = mod_ref[g_row:g_row + 1, :]
        _, vjp = jax.vjp(functools.partial(_res_ln_fn, cmul=cmul), x_ref[...], f_ref[...], g, lng_ref[...],
                         lnb_ref[...])
        dx, df, dg, dlg, dlb = vjp(dy_ref[...])
        dx_ref[...] = dx
        df_ref[...] = df.astype(df_ref.dtype)

        @pl.when(s == 0)
        def _():
            dg_ref[...] = dg

        @pl.when(s > 0)
        def _():
            dg_ref[...] += dg

        first = jnp.logical_and(b == 0, s == 0)

        @pl.when(first)
        def _():
            dlg_ref[...] = dlg
            dlb_ref[...] = dlb

        @pl.when(jnp.logical_not(first))
        def _():
            dlg_ref[...] += dlg
            dlb_ref[...] += dlb

    return pl.pallas_call(
        body, name=name, grid=(bsz, seq // ts),
        in_specs=[_row_spec(ts, d), _row_spec(ts, d), _row_spec(ts, d), _mod_spec(d), _vec_spec(d), _vec_spec(d)],
        out_specs=[_row_spec(ts, d), _row_spec(ts, d), _bvec_spec(d), _vec_spec(d), _vec_spec(d)],
        out_shape=[jax.ShapeDtypeStruct((bsz, seq, d), F32), jax.ShapeDtypeStruct((bsz, seq, d), BF16),
                   jax.ShapeDtypeStruct((bsz, 1, d), F32), jax.ShapeDtypeStruct((1, d), F32),
                   jax.ShapeDtypeStruct((1, d), F32)],
        compiler_params=_cparams(("arbitrary", "arbitrary")),
    )(dy, x, f, mod, lng, lnb)


def _loss_head(y, target, name, ts=512):
    bsz, seq, d = y.shape
    n_s = seq // ts

    def body(y_ref, t_ref, dy_ref, loss_ref, acc_ref):
        b, s = pl.program_id(0), pl.program_id(1)
        err = y_ref[...] - t_ref[...]
        dy_ref[...] = err * (1.0 / d)
        part = jnp.sum(err * err, axis=0, keepdims=True)
        first = jnp.logical_and(b == 0, s == 0)

        @pl.when(first)
        def _():
            acc_ref[...] = part

        @pl.when(jnp.logical_not(first))
        def _():
            acc_ref[...] += part

        @pl.when(jnp.logical_and(b == bsz - 1, s == n_s - 1))
        def _():
            loss_ref[...] = jnp.sum(acc_ref[...], axis=1, keepdims=True) * (0.5 / d)

    return pl.pallas_call(
        body, name=name, grid=(bsz, n_s),
        in_specs=[_row_spec(ts, d), _row_spec(ts, d)],
        out_specs=[_row_spec(ts, d), pl.BlockSpec((1, 1), lambda b, s: (0, 0))],
        out_shape=[jax.ShapeDtypeStruct((bsz, seq, d), F32), jax.ShapeDtypeStruct((1, 1), F32)],
        scratch_shapes=[pltpu.VMEM((1, d), F32)],
        compiler_params=_cparams(("arbitrary", "arbitrary")),
    )(y, target)


def _ffn_in_swiglu(h, w_in, name, tm=1024):
    t, d = h.shape
    n_sh, _, w = w_in.shape
    half = n_sh // 2

    def body(h_ref, w_ref, z_ref, a_ref):
        hv = h_ref[...]
        g = jnp.dot(hv, w_ref[0], preferred_element_type=F32)
        u = jnp.dot(hv, w_ref[1], preferred_element_type=F32)
        z_ref[0] = g.astype(z_ref.dtype)
        z_ref[1] = u.astype(z_ref.dtype)
        a_ref[...] = (g * jax.nn.sigmoid(g) * u).astype(a_ref.dtype)

    return pl.pallas_call(
        body, name=name, grid=(half, t // tm),
        in_specs=[pl.BlockSpec((tm, d), lambda g, i: (i, 0)),
                  pl.BlockSpec((2, None, d, w), lambda g, i: (0, g, 0, 0))],
        out_specs=[pl.BlockSpec((2, None, tm, w), lambda g, i: (0, g, i, 0)),
                   pl.BlockSpec((None, tm, w), lambda g, i: (g, i, 0))],
        out_shape=[jax.ShapeDtypeStruct((2, half, t, w), BF16), jax.ShapeDtypeStruct((half, t, w), BF16)],
        compiler_params=_cparams(("parallel", "parallel")),
    )(h, w_in.reshape(2, half, d, w))


def _ffn_out_dx_swiglu(df, w_out, z, name, tm=1024):
    t, d = df.shape
    half, w, _ = w_out.shape

    def body(df_ref, w_ref, z_ref, dz_ref):
        da = lax.dot_general(df_ref[...], w_ref[...], _DN["nt"], preferred_element_type=F32)
        g = z_ref[0].astype(F32)
        u = z_ref[1].astype(F32)
        sig = jax.nn.sigmoid(g)
        dz_ref[0] = (da * u * (sig * (1.0 + g * (1.0 - sig)))).astype(dz_ref.dtype)
        dz_ref[1] = (da * (g * sig)).astype(dz_ref.dtype)

    zspec = pl.BlockSpec((2, None, tm, w), lambda g, i: (0, g, i, 0))
    return pl.pallas_call(
        body, name=name, grid=(half, t // tm),
        in_specs=[pl.BlockSpec((tm, d), lambda g, i: (i, 0)), pl.BlockSpec((None, w, d), lambda g, i: (g, 0, 0)),
                  zspec],
        out_specs=zspec, out_shape=jax.ShapeDtypeStruct(z.shape, BF16),
        compiler_params=_cparams(("parallel", "parallel")),
    )(df, w_out, z)


def _log_sigmoid(x):
    return jnp.minimum(x, 0.0) - jnp.log(1.0 + jnp.exp(-jnp.abs(x)))


def _hgrn_consts():
    r = lax.broadcasted_iota(jnp.int32, (GROUP_WIDTH, GROUP_WIDTH), 0)
    c = lax.broadcasted_iota(jnp.int32, (GROUP_WIDTH, GROUP_WIDTH), 1)
    bd = (r // HEAD_DIM == c // HEAD_DIM).astype(F32)
    r16 = lax.broadcasted_iota(jnp.int32, (A_CHUNK, A_CHUNK), 0)
    c16 = lax.broadcasted_iota(jnp.int32, (A_CHUNK, A_CHUNK), 1)
    tril = (r16 >= c16).astype(F32)
    rows = lax.broadcasted_iota(jnp.int32, (A_CHUNK, GROUP_WIDTH), 0)
    return bd, tril, rows


def _hgrn_lb(logits8, layer):
    rows = lax.broadcasted_iota(jnp.int32, logits8.shape, 0)
    valid = rows < DEPTH
    mx = jnp.max(jnp.where(valid, logits8, NEG), axis=0, keepdims=True)
    e = jnp.where(valid, jnp.exp(logits8 - mx), 0.0)
    sm = e / jnp.sum(e, axis=0, keepdims=True)
    pick = jnp.logical_and(rows >= 1, rows <= layer)
    return jnp.sum(jnp.where(pick, sm, 0.0), axis=0, keepdims=True)


def _hgrn_chunk(aq, af, ai, ag, logits8, norm_g, st, *, layer, consts):
    bd, tril, rows = consts
    lb = _hgrn_lb(logits8, layer)
    la = jnp.log(jnp.maximum(lb, LB_FLOOR))
    b2 = jnp.log(1.0 - lb) + _log_sigmoid(af)
    log_f = jnp.maximum(la, b2) + jnp.log(1.0 + jnp.exp(-jnp.abs(la - b2)))
    k = 1.0 - jnp.exp(log_f)
    qf = aq * jax.nn.sigmoid(aq)
    g_cum = jnp.dot(tril, log_f, precision=HI, preferred_element_type=F32)

    def row(v, s):
        return jnp.sum(jnp.where(rows == s, v, 0.0), axis=0, keepdims=True)

    parts = []
    v_rows = []
    for s in range(A_CHUNK):
        rel = jnp.where(rows >= s, g_cum - row(g_cum, s), NEG)
        parts.append(qf * (row(k, s) * jnp.exp(rel)))
        v_rows.append(row(ai, s))
    a_all = _bdot(jnp.concatenate(parts, axis=0), bd, "nn")
    o = jnp.zeros_like(aq)
    for s in range(A_CHUNK):
        o = o + a_all[s * A_CHUNK:(s + 1) * A_CHUNK, :] * v_rows[s]
    q_dec = qf * jnp.exp(g_cum)
    o = o + _bdot(q_dec, st, "nt")
    g_last = row(g_cum, A_CHUNK - 1)
    k_end = k * jnp.exp(g_last - g_cum)
    kv = _bdot(ai, k_end, "tn")
    st_new = st * jnp.exp(g_last) + kv * bd
    ms = _bdot(o * o, bd, "nn") * (1.0 / HEAD_DIM)
    o = o * lax.rsqrt(ms + RMS_EPS) * norm_g
    return o * (ag * jax.nn.sigmoid(ag)), st_new


def _hgrn_fwd(proj, logits8, norm_g, layer, name, ts=128):
    bsz, seq, _ = proj.shape
    n_ch = ts // A_CHUNK

    def body(p_ref, lg_ref, ng_ref, o_ref, st_ref, st_scr):
        @pl.when(pl.program_id(1) == 0)
        def _():
            st_scr[...] = jnp.zeros_like(st_scr)

        consts = _hgrn_consts()
        logits_v, ng_v = lg_ref[...], ng_ref[...]

        def chunk(ci, carry):
            r = pl.multiple_of(ci * A_CHUNK, A_CHUNK)
            st = st_scr[...]
            st_ref[ci] = st
            o, st_new = _hgrn_chunk(
                p_ref[pl.ds(r, A_CHUNK), 0:256], p_ref[pl.ds(r, A_CHUNK), 256:512],
                p_ref[pl.ds(r, A_CHUNK), 512:768], p_ref[pl.ds(r, A_CHUNK), 768:1024],
                logits_v, ng_v, st, layer=layer, consts=consts)
            o_ref[pl.ds(r, A_CHUNK), :] = o.astype(o_ref.dtype)
            st_scr[...] = st_new
            return carry

        lax.fori_loop(0, n_ch, chunk, 0, unroll=2)

    return pl.pallas_call(
        body, name=name, grid=(bsz, seq // ts),
        in_specs=[pl.BlockSpec((None, ts, 1024), lambda b, s: (b, s, 0)),
                  pl.BlockSpec((8, GROUP_WIDTH), lambda b, s: (0, 0)),
                  pl.BlockSpec((1, GROUP_WIDTH), lambda b, s: (0, 0))],
        out_specs=[pl.BlockSpec((None, ts, GROUP_WIDTH), lambda b, s: (b, s, 0)),
                   pl.BlockSpec((None, n_ch, GROUP_WIDTH, GROUP_WIDTH), lambda b, s: (b, s, 0, 0))],
        out_shape=[jax.ShapeDtypeStruct((bsz, seq, MO_W), BF16),
                   jax.ShapeDtypeStruct((bsz, seq // A_CHUNK, GROUP_WIDTH, GROUP_WIDTH), F32)],
        scratch_shapes=[pltpu.VMEM((GROUP_WIDTH, GROUP_WIDTH), F32)],
        compiler_params=_cparams(("parallel", "arbitrary")),
    )(proj, logits8, norm_g)


def _hgrn_bwd(dmo, proj, states, logits8, norm_g, layer, name, ts=128):
    bsz, seq, _ = proj.shape
    n_ch = ts // A_CHUNK
    n_s = seq // ts

    def body(do_ref, p_ref, st_ref, lg_ref, ng_ref, dp_ref, dlg_ref, dng_ref, dst_scr):
        b, s = pl.program_id(0), pl.program_id(1)

        @pl.when(s == 0)
        def _():
            dst_scr[...] = jnp.zeros_like(dst_scr)

        @pl.when(jnp.logical_and(b == 0, s == 0))
        def _():
            dlg_ref[...] = jnp.zeros_like(dlg_ref)
            dng_ref[...] = jnp.zeros_like(dng_ref)

        consts = _hgrn_consts()
        logits_v, ng_v = lg_ref[...], ng_ref[...]
        fn = functools.partial(_hgrn_chunk, layer=layer, consts=consts)

        def chunk(t, carry):
            ci = n_ch - 1 - t
            r = pl.multiple_of(ci * A_CHUNK, A_CHUNK)
            _, vjp = jax.vjp(
                fn, p_ref[pl.ds(r, A_CHUNK), 0:256], p_ref[pl.ds(r, A_CHUNK), 256:512],
                p_ref[pl.ds(r, A_CHUNK), 512:768], p_ref[pl.ds(r, A_CHUNK), 768:1024],
                logits_v, ng_v, st_ref[ci])
            daq, daf, dai, dag, dlg, dng, dst = vjp((do_ref[pl.ds(r, A_CHUNK), :], dst_scr[...]))
            dp_ref[pl.ds(r, A_CHUNK), 0:256] = daq.astype(dp_ref.dtype)
            dp_ref[pl.ds(r, A_CHUNK), 256:512] = daf.astype(dp_ref.dtype)
            dp_ref[pl.ds(r, A_CHUNK), 512:768] = dai.astype(dp_ref.dtype)
            dp_ref[pl.ds(r, A_CHUNK), 768:1024] = dag.astype(dp_ref.dtype)
            dlg_ref[...] += dlg
            dng_ref[...] += dng
            dst_scr[...] = dst
            return carry

        lax.fori_loop(0, n_ch, chunk, 0, unroll=2)

    rev = lambda b, s: (b, n_s - 1 - s, 0)
    return pl.pallas_call(
        body, name=name, grid=(bsz, n_s),
        in_specs=[pl.BlockSpec((None, ts, GROUP_WIDTH), rev),
                  pl.BlockSpec((None, ts, 1024), rev),
                  pl.BlockSpec((None, n_ch, GROUP_WIDTH, GROUP_WIDTH), lambda b, s: (b, n_s - 1 - s, 0, 0)),
                  pl.BlockSpec((8, GROUP_WIDTH), lambda b, s: (0, 0)),
                  pl.BlockSpec((1, GROUP_WIDTH), lambda b, s: (0, 0))],
        out_specs=[pl.BlockSpec((None, ts, 1024), rev),
                   pl.BlockSpec((8, GROUP_WIDTH), lambda b, s: (0, 0)),
                   pl.BlockSpec((1, GROUP_WIDTH), lambda b, s: (0, 0))],
        out_shape=[jax.ShapeDtypeStruct((bsz, seq, PACK_W), BF16),
                   jax.ShapeDtypeStruct((8, GROUP_WIDTH), F32), jax.ShapeDtypeStruct((1, GROUP_WIDTH), F32)],
        scratch_shapes=[pltpu.VMEM((GROUP_WIDTH, GROUP_WIDTH), F32)],
        compiler_params=_cparams(("arbitrary", "arbitrary")),
    )(dmo, proj, states, logits8, norm_g)


def _rms_fn(x, g):
    return x * lax.rsqrt(jnp.mean(x * x, axis=-1, keepdims=True) + RMS_EPS) * g


def _tile4(t):
    return jnp.concatenate([t, t, t, t], axis=1)


def _rope(x, c, s1, s2):
    w = x.shape[-1]
    return x * c + pltpu.roll(x, 32, axis=1) * s2 + pltpu.roll(x, w - 32, axis=1) * s1


def _rope_t(dy, c, s1, s2):
    w = dy.shape[-1]
    return dy * c + pltpu.roll(dy * s2, w - 32, axis=1) + pltpu.roll(dy * s1, 32, axis=1)


def _mla_pre(proj, qg, kvg, wq, wkv, tabs, name, ts=256):
    bsz, seq, _ = proj.shape

    def body(p_ref, qg_ref, kvg_ref, wq_ref, wkv_ref, c_ref, s1_ref, s2_ref, q_ref, kv_ref):
        nq = _rms_fn(p_ref[:, 0:256], qg_ref[...])
        nkv = _rms_fn(p_ref[:, 256:384], kvg_ref[...])
        c, s1, s2 = c_ref[...], s1_ref[...], s2_ref[...]
        qp = jnp.dot(nq.astype(BF16), wq_ref[...], preferred_element_type=F32)
        q_ref[...] = _rope(qp, _tile4(c), _tile4(s1), _tile4(s2)).astype(q_ref.dtype)
        kv = jnp.dot(nkv.astype(BF16), wkv_ref[...], preferred_element_type=F32)
        krr = _rope(p_ref[:, 384:512], c, s1, s2)
        zero = jnp.zeros_like(krr)
        kv_ref[...] = (kv + jnp.concatenate([krr, zero] * N_HEADS, axis=1)).astype(kv_ref.dtype)

    tab_spec = pl.BlockSpec((ts, LANES), lambda b, s: (s, 0))
    return pl.pallas_call(
        body, name=name, grid=(bsz, seq // ts),
        in_specs=[pl.BlockSpec((None, ts, 512), lambda b, s: (b, s, P_B // 512)),
                  _vec_spec(256), _vec_spec(128),
                  pl.BlockSpec((256, 512), lambda b, s: (0, 0)), pl.BlockSpec((128, 1024), lambda b, s: (0, 0)),
                  tab_spec, tab_spec, tab_spec],
        out_specs=[_row_spec(ts, 512), _row_spec(ts, 1024)],
        out_shape=[jax.ShapeDtypeStruct((bsz, seq, 512), BF16), jax.ShapeDtypeStruct((bsz, seq, 1024), BF16)],
        compiler_params=_cparams(("parallel", "parallel")),
    )(proj, qg, kvg, wq, wkv, *tabs)


def _mla_pre_bwd(dq, dkv, dproj, proj, qg, kvg, wq, wkv, tabs, name, ts=256):
    bsz, seq, _ = proj.shape

    def body(dq_ref, dkv_ref, dp_any, p_ref, qg_ref, kvg_ref, wq_ref, wkv_ref, c_ref, s1_ref, s2_ref,
             dp_ref, dqg_ref, dkvg_ref, dwq_ref, dwkv_ref):
        del dp_any
        first = jnp.logical_and(pl.program_id(0) == 0, pl.program_id(1) == 0)

        @pl.when(first)
        def _():
            dqg_ref[...] = jnp.zeros_like(dqg_ref)
            dkvg_ref[...] = jnp.zeros_like(dkvg_ref)
            dwq_ref[...] = jnp.zeros_like(dwq_ref)
            dwkv_ref[...] = jnp.zeros_like(dwkv_ref)

        c, s1, s2 = c_ref[...], s1_ref[...], s2_ref[...]
        nq, vjp_q = jax.vjp(_rms_fn, p_ref[:, 0:256], qg_ref[...])
        nkv, vjp_kv = jax.vjp(_rms_fn, p_ref[:, 256:384], kvg_ref[...])
        dqp = _rope_t(dq_ref[...], _tile4(c), _tile4(s1), _tile4(s2)).astype(BF16)
        dkv_v = dkv_ref[...]
        dkv_b = dkv_v.astype(BF16)
        tn = (((0,), (0,)), ((), ()))
        nt = (((1,), (1,)), ((), ()))
        dwq_ref[...] += lax.dot_general(nq.astype(BF16), dqp, tn, preferred_element_type=F32)
        dwkv_ref[...] += lax.dot_general(nkv.astype(BF16), dkv_b, tn, preferred_element_type=F32)
        dcq, dqg = vjp_q(lax.dot_general(dqp, wq_ref[...], nt, preferred_element_type=F32))
        dckv, dkvg = vjp_kv(lax.dot_general(dkv_b, wkv_ref[...], nt, preferred_element_type=F32))
        dqg_ref[...] += dqg
        dkvg_ref[...] += dkvg
        dk_sum = dkv_v[:, 0:128] + dkv_v[:, 256:384] + dkv_v[:, 512:640] + dkv_v[:, 768:896]
        lane = lax.broadcasted_iota(jnp.int32, dk_sum.shape, 1)
        dkr = jnp.where(lane >= 64, _rope_t(dk_sum, c, s1, s2), 0.0)
        dp_ref[:, 0:256] = dcq.astype(dp_ref.dtype)
        dp_ref[:, 256:384] = dckv.astype(dp_ref.dtype)
        dp_ref[:, 384:512] = dkr.astype(dp_ref.dtype)

    tab_spec = pl.BlockSpec((ts, LANES), lambda b, s: (s, 0))
    const = lambda shape: pl.BlockSpec(shape, lambda b, s: (0, 0))
    return pl.pallas_call(
        body, name=name, grid=(bsz, seq // ts),
        in_specs=[_row_spec(ts, 512), _row_spec(ts, 1024), pl.BlockSpec(memory_space=pl.ANY),
                  pl.BlockSpec((None, ts, 512), lambda b, s: (b, s, P_B // 512)),
                  _vec_spec(256), _vec_spec(128), const((256, 512)), const((128, 1024)),
                  tab_spec, tab_spec, tab_spec],
        out_specs=[pl.BlockSpec((None, ts, 512), lambda b, s: (b, s, P_B // 512)),
                   _vec_spec(256), _vec_spec(128), const((256, 512)), const((128, 1024))],
        out_shape=[jax.ShapeDtypeStruct(dproj.shape, dproj.dtype), jax.ShapeDtypeStruct((1, 256), F32),
                   jax.ShapeDtypeStruct((1, 128), F32), jax.ShapeDtypeStruct((256, 512), F32),
                   jax.ShapeDtypeStruct((128, 1024), F32)],
        input_output_aliases={2: 0},
        compiler_params=_cparams(("arbitrary", "arbitrary")),
    )(dq, dkv, dproj, proj, qg, kvg, wq, wkv, *tabs)


def _fox_gate(proj, bf, name):
    bsz, seq, _ = proj.shape
    n_blk = seq // LANES

    def body(x_ref, bf_ref, f_ref):
        r_i = lax.broadcasted_iota(jnp.int32, (LANES, LANES), 0)
        c_i = lax.broadcasted_iota(jnp.int32, (LANES, LANES), 1)
        tril = (r_i >= c_i).astype(F32)
        bias = bf_ref[...]

        def blk(i, carry):
            r = pl.multiple_of(i * LANES, LANES)
            lf = _log_sigmoid(x_ref[pl.ds(r, LANES), :] + bias)
            f_ref[pl.ds(r, LANES), :] = jnp.dot(tril, lf, precision=HI, preferred_element_type=F32) + carry
            return carry + jnp.sum(lf, axis=0, keepdims=True)

        lax.fori_loop(0, n_blk, blk, jnp.zeros((1, LANES), F32))

    return pl.pallas_call(
        body, name=name, grid=(bsz,),
        in_specs=[pl.BlockSpec((None, seq, LANES), lambda b: (b, 0, P_CF // LANES)),
                  pl.BlockSpec((1, LANES), lambda b: (0, 0))],
        out_specs=pl.BlockSpec((None, seq, LANES), lambda b: (b, 0, 0)),
        out_shape=jax.ShapeDtypeStruct((bsz, seq, LANES), F32),
        compiler_params=_cparams(("parallel",)),
    )(proj, bf)


def _fox_gate_bwd(dcum, dproj, proj, bf, name):
    bsz, seq, _ = proj.shape
    n_blk = seq // LANES

    def body(dc_ref, dp_any, x_ref, bf_ref, dp_ref, dbf_ref):
        del dp_any

        @pl.when(pl.program_id(0) == 0)
        def _():
            dbf_ref[...] = jnp.zeros_like(dbf_ref)

        r_i = lax.broadcasted_iota(jnp.int32, (LANES, LANES), 0)
        c_i = lax.broadcasted_iota(jnp.int32, (LANES, LANES), 1)
        triu = (r_i <= c_i).astype(F32)
        bias = bf_ref[...]

        def blk(t, carry):
            tail, dbf = carry
            r = pl.multiple_of((n_blk - 1 - t) * LANES, LANES)
            dc = dc_ref[pl.ds(r, LANES), :]
            dlf = jnp.dot(triu, dc, precision=HI, preferred_element_type=F32) + tail
            dx = dlf * (1.0 - jax.nn.sigmoid(x_ref[pl.ds(r, LANES), :] + bias))
            dp_ref[pl.ds(r, LANES), :] = dx.astype(dp_ref.dtype)
            return tail + jnp.sum(dc, axis=0, keepdims=True), dbf + jnp.sum(dx, axis=0, keepdims=True)

        z = jnp.zeros((1, LANES), F32)
        _, dbf = lax.fori_loop(0, n_blk, blk, (z, z))
        dbf_ref[...] += dbf

    return pl.pallas_call(
        body, name=name, grid=(bsz,),
        in_specs=[pl.BlockSpec((None, seq, LANES), lambda b: (b, 0, 0)), pl.BlockSpec(memory_space=pl.ANY),
                  pl.BlockSpec((None, seq, LANES), lambda b: (b, 0, P_CF // LANES)),
                  pl.BlockSpec((1, LANES), lambda b: (0, 0))],
        out_specs=[pl.BlockSpec((None, seq, LANES), lambda b: (b, 0, P_CF // LANES)),
                   pl.BlockSpec((1, LANES), lambda b: (0, 0))],
        out_shape=[jax.ShapeDtypeStruct(dproj.shape, dproj.dtype), jax.ShapeDtypeStruct((1, LANES), F32)],
        input_output_aliases={1: 0},
        compiler_params=_cparams(("arbitrary",)),
    )(dcum, dproj, proj, bf)


def _gate_terms(fc_ref, fr_ref, h, tq, tk):
    lane = lax.broadcasted_iota(jnp.int32, (tq, LANES), 1)
    fcol = jnp.sum(jnp.where(lane == h, fc_ref[...], 0.0), axis=1, keepdims=True)
    sub = lax.broadcasted_iota(jnp.int32, (8, tk), 0)
    frow = jnp.sum(jnp.where(sub == h, fr_ref[...], 0.0), axis=0, keepdims=True)
    return fcol - frow


def _scores(q_ref, k_ref, gate_refs, scale, h, diag, tq, tk):
    s = lax.dot_general(q_ref[...].astype(BF16), k_ref[...].astype(BF16), (((1,), (1,)), ((), ())),
                        preferred_element_type=F32) * scale
    if gate_refs is not None:
        s = s + _gate_terms(gate_refs[0], gate_refs[1], h, tq, tk)
    r_i = lax.broadcasted_iota(jnp.int32, (tq, tk), 0)
    c_i = lax.broadcasted_iota(jnp.int32, (tq, tk), 1)
    return jnp.where(jnp.logical_or(jnp.logical_not(diag), c_i <= r_i), s, NEG)


def _attn_fwd(qa, q0, kva, kv0, mo, o0, gates, scale, name, tq=None):
    bsz, seq, _ = qa.shape
    tq = ATTN_TILE if tq is None else tq
    n_q = seq // tq
    gated = gates is not None

    def body(*refs):
        q_ref, k_ref, v_ref = refs[:3]
        gate_refs = refs[3:5] if gated else None
        o_ref, lse_ref, m_s, l_s, acc_s = refs[-5:]
        h, i, j = pl.program_id(1), pl.program_id(2), pl.program_id(3)

        @pl.when(j == 0)
        def _():
            m_s[...] = jnp.full_like(m_s, NEG)
            l_s[...] = jnp.zeros_like(l_s)
            acc_s[...] = jnp.zeros_like(acc_s)

        @pl.when(j <= i)
        def _():
            s = _scores(q_ref, k_ref, gate_refs, scale, h, j == i, tq, tq)
            m_prev = m_s[...]
            m_new = jnp.maximum(m_prev, jnp.max(s, axis=1, keepdims=True))
            alpha = jnp.exp(m_prev - m_new)
            p = jnp.exp(s - m_new)
            l_s[...] = alpha * l_s[...] + jnp.sum(p, axis=1, keepdims=True)
            acc_s[...] = alpha * acc_s[...] + jnp.dot(p.astype(BF16), v_ref[...].astype(BF16),
                                                      preferred_element_type=F32)
            m_s[...] = m_new

        @pl.when(j == i)
        def _():
            o_ref[...] = (acc_s[...] / l_s[...]).astype(o_ref.dtype)
            lse_ref[...] = m_s[...] + jnp.log(l_s[...])

    blk = (None, tq, LANES)
    in_specs = [pl.BlockSpec(blk, lambda b, h, i, j: (b, i, q0 + h)),
                pl.BlockSpec(blk, lambda b, h, i, j: (b, jnp.minimum(j, i), kv0 + 2 * h)),
                pl.BlockSpec(blk, lambda b, h, i, j: (b, jnp.minimum(j, i), kv0 + 2 * h + 1))]
    args = [qa, kva, kva]
    if gated:
        in_specs += [pl.BlockSpec(blk, lambda b, h, i, j: (b, i, 0)),
                     pl.BlockSpec((None, 8, tq), lambda b, h, i, j: (b, 0, jnp.minimum(j, i)))]
        args += list(gates)
    in_specs.append(pl.BlockSpec(memory_space=pl.ANY))
    args.append(mo)
    return pl.pallas_call(
        body, name=name, grid=(bsz, N_HEADS, n_q, n_q), in_specs=in_specs,
        out_specs=[pl.BlockSpec(blk, lambda b, h, i, j: (b, i, o0 + h)),
                   pl.BlockSpec((None, None, tq, 1), lambda b, h, i, j: (b, h, i, 0))],
        out_shape=[jax.ShapeDtypeStruct(mo.shape, mo.dtype), jax.ShapeDtypeStruct((bsz, N_HEADS, seq, 1), F32)],
        scratch_shapes=[pltpu.VMEM((tq, 1), F32), pltpu.VMEM((tq, 1), F32), pltpu.VMEM((tq, LANES), F32)],
        input_output_aliases={len(args) - 1: 0},
        compiler_params=_cparams(("parallel", "parallel", "parallel", "arbitrary")),
    )(*args)


def _attn_bwd_q(qa, q0, kva, kv0, mo, dmo, o0, lse, gates, scale, out, out0, name, tq=None):
    bsz, seq, _ = qa.shape
    tq = ATTN_TILE if tq is None else tq
    n_q = seq // tq
    gated = gates is not None
    aliased = not isinstance(out, jax.ShapeDtypeStruct)

    def body(*refs):
        q_ref, k_ref, v_ref, o_ref, do_ref, lse_ref = refs[:6]
        gate_refs = refs[6:8] if gated else None
        dq_ref, delta_ref, dfq_ref, acc_s, dl_s, df_s = refs[-6:]
        h, i, j = pl.program_id(1), pl.program_id(2), pl.program_id(3)

        @pl.when(j == 0)
        def _():
            acc_s[...] = jnp.zeros_like(acc_s)
            df_s[...] = jnp.zeros_like(df_s)
            dl_s[...] = jnp.sum(do_ref[...] * o_ref[...].astype(F32), axis=1, keepdims=True)

        @pl.when(j <= i)
        def _():
            s = _scores(q_ref, k_ref, gate_refs, scale, h, j == i, tq, tq)
            p = jnp.exp(s - lse_ref[...])
            dp = lax.dot_general(do_ref[...].astype(BF16), v_ref[...].astype(BF16), (((1,), (1,)), ((), ())),
                                 preferred_element_type=F32)
            ds = p * (dp - dl_s[...])
            acc_s[...] += jnp.dot(ds.astype(BF16), k_ref[...].astype(BF16), preferred_element_type=F32)
            df_s[...] += jnp.sum(ds, axis=1, keepdims=True)

        @pl.when(j == i)
        def _():
            dq_ref[...] = (acc_s[...] * scale).astype(dq_ref.dtype)
            delta_ref[...] = dl_s[...]
            dfq_ref[...] = df_s[...]

    blk = (None, tq, LANES)
    col = pl.BlockSpec((None, None, tq, 1), lambda b, h, i, j: (b, h, i, 0))
    in_specs = [pl.BlockSpec(blk, lambda b, h, i, j: (b, i, q0 + h)),
                pl.BlockSpec(blk, lambda b, h, i, j: (b, jnp.minimum(j, i), kv0 + 2 * h)),
                pl.BlockSpec(blk, lambda b, h, i, j: (b, jnp.minimum(j, i), kv0 + 2 * h + 1)),
                pl.BlockSpec(blk, lambda b, h, i, j: (b, i, o0 + h)),
                pl.BlockSpec(blk, lambda b, h, i, j: (b, i, o0 + h)), col]
    args = [qa, kva, kva, mo, dmo, lse]
    if gated:
        in_specs += [pl.BlockSpec(blk, lambda b, h, i, j: (b, i, 0)),
                     pl.BlockSpec((None, 8, tq), lambda b, h, i, j: (b, 0, jnp.minimum(j, i)))]
        args += list(gates)
    aliases = {}
    if aliased:
        in_specs.append(pl.BlockSpec(memory_space=pl.ANY))
        args.append(out)
        aliases = {len(args) - 1: 0}
    vec = jax.ShapeDtypeStruct((bsz, N_HEADS, seq, 1), F32)
    return pl.pallas_call(
        body, name=name, grid=(bsz, N_HEADS, n_q, n_q), in_specs=in_specs,
        out_specs=[pl.BlockSpec(blk, lambda b, h, i, j: (b, i, out0 + h)), col, col],
        out_shape=[jax.ShapeDtypeStruct(out.shape, out.dtype), vec, vec],
        scratch_shapes=[pltpu.VMEM((tq, LANES), F32), pltpu.VMEM((tq, 1), F32), pltpu.VMEM((tq, 1), F32)],
        input_output_aliases=aliases,
        compiler_params=_cparams(("parallel", "parallel", "parallel", "arbitrary")),
    )(*args)


def _attn_bwd_kv(qa, q0, kva, kv0, dmo, o0, lse, delta, gates, scale, out, out0, name, tq=None):
    bsz, seq, _ = qa.shape
    tq = ATTN_TILE if tq is None else tq
    n_q = seq // tq
    gated = gates is not None
    aliased = not isinstance(out, jax.ShapeDtypeStruct)

    def body(*refs):
        q_ref, k_ref, v_ref, do_ref, lse_ref, dl_ref = refs[:6]
        gate_refs = refs[6:8] if gated else None
        dkv_ref, dfk_ref, dk_s, dv_s, df_s = refs[-5:]
        h, j, i = pl.program_id(1), pl.program_id(2), pl.program_id(3)

        @pl.when(i == 0)
        def _():
            dk_s[...] = jnp.zeros_like(dk_s)
            dv_s[...] = jnp.zeros_like(dv_s)
            df_s[...] = jnp.zeros_like(df_s)

        @pl.when(i >= j)
        def _():
            s = _scores(q_ref, k_ref, gate_refs, scale, h, j == i, tq, tq)
            p = jnp.exp(s - lse_ref[...])
            do_b = do_ref[...].astype(BF16)
            dp = lax.dot_general(do_b, v_ref[...].astype(BF16), (((1,), (1,)), ((), ())),
                                 preferred_element_type=F32)
            ds = p * (dp - dl_ref[...])
            tn = (((0,), (0,)), ((), ()))
            dv_s[...] += lax.dot_general(p.astype(BF16), do_b, tn, preferred_element_type=F32)
            dk_s[...] += lax.dot_general(ds.astype(BF16), q_ref[...].astype(BF16), tn, preferred_element_type=F32)
            df_s[...] -= jnp.sum(ds, axis=0, keepdims=True)

        @pl.when(i == n_q - 1)
        def _():
            dkv_ref[:, 0:LANES] = (dk_s[...] * scale).astype(dkv_ref.dtype)
            dkv_ref[:, LANES:2 * LANES] = dv_s[...].astype(dkv_ref.dtype)
            dfk_ref[...] = df_s[...]

    blk = (None, tq, LANES)
    col = pl.BlockSpec((None, None, tq, 1), lambda b, h, j, i: (b, h, jnp.maximum(i, j), 0))
    in_specs = [pl.BlockSpec(blk, lambda b, h, j, i: (b, jnp.maximum(i, j), q0 + h)),
                pl.BlockSpec(blk, lambda b, h, j, i: (b, j, kv0 + 2 * h)),
                pl.BlockSpec(blk, lambda b, h, j, i: (b, j, kv0 + 2 * h + 1)),
                pl.BlockSpec(blk, lambda b, h, j, i: (b, jnp.maximum(i, j), o0 + h)), col, col]
    args = [qa, kva, kva, dmo, lse, delta]
    if gated:
        in_specs += [pl.BlockSpec(blk, lambda b, h, j, i: (b, jnp.maximum(i, j), 0)),
                     pl.BlockSpec((None, 8, tq), lambda b, h, j, i: (b, 0, j))]
        args += list(gates)
    aliases = {}
    if aliased:
        in_specs.append(pl.BlockSpec(memory_space=pl.ANY))
        args.append(out)
        aliases = {len(args) - 1: 0}
    return pl.pallas_call(
        body, name=name, grid=(bsz, N_HEADS, n_q, n_q), in_specs=in_specs,
        out_specs=[pl.BlockSpec((None, tq, 2 * LANES), lambda b, h, j, i: (b, j, out0 + h)),
                   pl.BlockSpec((None, None, 1, tq), lambda b, h, j, i: (b, h, 0, j))],
        out_shape=[jax.ShapeDtypeStruct(out.shape, out.dtype), jax.ShapeDtypeStruct((bsz, N_HEADS, 1, seq), F32)],
        scratch_shapes=[pltpu.VMEM((tq, LANES), F32), pltpu.VMEM((tq, LANES), F32), pltpu.VMEM((1, tq), F32)],
        input_output_aliases=aliases,
        compiler_params=_cparams(("parallel", "parallel", "parallel", "arbitrary")),
    )(*args)


def _gmlp_fn(uv, lng, lnb, ws, bst):
    u = jax.nn.gelu(uv[:, 0:GROUP_WIDTH])
    gv = jax.nn.gelu(uv[:, GROUP_WIDTH:2 * GROUP_WIDTH])
    mu = jnp.mean(gv, axis=-1, keepdims=True)
    vc = gv - mu
    var = jnp.mean(vc * vc, axis=-1, keepdims=True)
    vln = vc * lax.rsqrt(var + LN_EPS) * lng + lnb
    r_i = lax.broadcasted_iota(jnp.int32, (D_CHUNK, D_CHUNK), 0)
    c_i = lax.broadcasted_iota(jnp.int32, (D_CHUNK, D_CHUNK), 1)
    lane_g = lax.broadcasted_iota(jnp.int32, (D_CHUNK, GROUP_WIDTH), 1) // HEAD_DIM
    e_r = lax.broadcasted_iota(jnp.int32, (LANES, GROUP_WIDTH), 0)
    e_c = lax.broadcasted_iota(jnp.int32, (LANES, GROUP_WIDTH), 1)
    expand = (e_r == e_c // HEAD_DIM).astype(F32)
    mixed = jnp.dot(bst, expand, precision=HI, preferred_element_type=F32)
    for g in range(4):
        w = jnp.where(r_i >= c_i, ws[g], 0.0)
        mixed = mixed + jnp.where(lane_g == g, _bdot(w, vln, "nn"), 0.0)
    return u * mixed


def _gmlp_fwd(proj, mo, lng, lnb, ws, bst, name):
    bsz, seq, _ = proj.shape

    def body(p_ref, mo_any, lng_ref, lnb_ref, ws_ref, bst_ref, o_ref):
        del mo_any
        o_ref[...] = _gmlp_fn(p_ref[...], lng_ref[...], lnb_ref[...], ws_ref[...], bst_ref[...]).astype(o_ref.dtype)

    return pl.pallas_call(
        body, name=name, grid=(bsz, seq // D_CHUNK),
        in_specs=[pl.BlockSpec((None, D_CHUNK, 512), lambda b, s: (b, s, P_D // 512)),
                  pl.BlockSpec(memory_space=pl.ANY), _vec_spec(256), _vec_spec(256),
                  pl.BlockSpec((4, D_CHUNK, D_CHUNK), lambda b, s: (0, 0, 0)),
                  pl.BlockSpec((D_CHUNK, LANES), lambda b, s: (0, 0))],
        out_specs=pl.BlockSpec((None, D_CHUNK, GROUP_WIDTH), lambda b, s: (b, s, 1280 // GROUP_WIDTH)),
        out_shape=jax.ShapeDtypeStruct(mo.shape, mo.dtype),
        input_output_aliases={1: 0},
        compiler_params=_cparams(("parallel", "parallel")),
    )(proj, mo, lng, lnb, ws, bst)


def _gmlp_bwd(dmo, dproj, proj, lng, lnb, ws, bst, name):
    bsz, seq, _ = proj.shape

    def body(do_ref, dp_any, p_ref, lng_ref, lnb_ref, ws_ref, bst_ref, dp_ref, dlg_ref, dlb_ref, dws_ref, dbst_ref):
        del dp_any
        first = jnp.logical_and(pl.program_id(0) == 0, pl.program_id(1) == 0)

        @pl.when(first)
        def _():
            dlg_ref[...] = jnp.zeros_like(dlg_ref)
            dlb_ref[...] = jnp.zeros_like(dlb_ref)
            dws_ref[...] = jnp.zeros_like(dws_ref)
            dbst_ref[...] = jnp.zeros_like(dbst_ref)

        _, vjp = jax.vjp(_gmlp_fn, p_ref[...], lng_ref[...], lnb_ref[...], ws_ref[...], bst_ref[...])
        duv, dlg, dlb, dws, dbst = vjp(do_ref[...])
        dp_ref[...] = duv.astype(dp_ref.dtype)
        dlg_ref[...] += dlg
        dlb_ref[...] += dlb
        dws_ref[...] += dws
        dbst_ref[...] += dbst

    const2 = lambda shape: pl.BlockSpec(shape, lambda b, s: (0,) * len(shape))
    return pl.pallas_call(
        body, name=name, grid=(bsz, seq // D_CHUNK),
        in_specs=[pl.BlockSpec((None, D_CHUNK, GROUP_WIDTH), lambda b, s: (b, s, 1280 // GROUP_WIDTH)),
                  pl.BlockSpec(memory_space=pl.ANY),
                  pl.BlockSpec((None, D_CHUNK, 512), lambda b, s: (b, s, P_D // 512)),
                  _vec_spec(256), _vec_spec(256), const2((4, D_CHUNK, D_CHUNK)), const2((D_CHUNK, LANES))],
        out_specs=[pl.BlockSpec((None, D_CHUNK, 512), lambda b, s: (b, s, P_D // 512)),
                   _vec_spec(256), _vec_spec(256), const2((4, D_CHUNK, D_CHUNK)), const2((D_CHUNK, LANES))],
        out_shape=[jax.ShapeDtypeStruct(dproj.shape, dproj.dtype), jax.ShapeDtypeStruct((1, 256), F32),
                   jax.ShapeDtypeStruct((1, 256), F32), jax.ShapeDtypeStruct((4, D_CHUNK, D_CHUNK), F32),
                   jax.ShapeDtypeStruct((D_CHUNK, LANES), F32)],
        input_output_aliases={1: 0},
        compiler_params=_cparams(("arbitrary", "arbitrary")),
    )(dmo, dproj, proj, lng, lnb, ws, bst)


def _ada_fwd(c_all, ada_w, name):
    n_b = c_all.shape[0]
    depth, d, cols = ada_w.shape

    def body(c_ref, w_ref, o_ref):
        cv = c_ref[...]
        act = (cv * jax.nn.sigmoid(cv)).astype(BF16)
        o_ref[...] = jnp.dot(act, w_ref[...].astype(BF16), preferred_element_type=F32)

    return pl.pallas_call(
        body, name=name, grid=(depth,),
        in_specs=[pl.BlockSpec((n_b, d), lambda l: (0, 0)), pl.BlockSpec((None, d, cols), lambda l: (l, 0, 0))],
        out_specs=pl.BlockSpec((None, n_b, cols), lambda l: (l, 0, 0)),
        out_shape=jax.ShapeDtypeStruct((depth, n_b, cols), F32),
        compiler_params=_cparams(("parallel",)),
    )(c_all, ada_w)


def _ada_bwd(c_all, dmod_cols, dmod_full, name):
    n_b, d = c_all.shape
    depth, _, cols = dmod_cols.shape
    full = dmod_full.shape[-1]

    def body(c_ref, dm_ref, df_ref, gw_ref, gb_ref):
        cv = c_ref[...]
        act = (cv * jax.nn.sigmoid(cv)).astype(BF16)
        gw_ref[...] = lax.dot_general(act, dm_ref[...].astype(BF16), (((0,), (0,)), ((), ())),
                                      preferred_element_type=F32)
        gb_ref[...] = jnp.sum(df_ref[...], axis=0, keepdims=True)

    return pl.pallas_call(
        body, name=name, grid=(depth,),
        in_specs=[pl.BlockSpec((n_b, d), lambda l: (0, 0)), pl.BlockSpec((None, n_b, cols), lambda l: (l, 0, 0)),
                  pl.BlockSpec((None, n_b, full), lambda l: (l, 0, 0))],
        out_specs=[pl.BlockSpec((None, d, cols), lambda l: (l, 0, 0)),
                   pl.BlockSpec((None, 1, full), lambda l: (l, 0, 0))],
        out_shape=[jax.ShapeDtypeStruct((depth, d, cols), F32), jax.ShapeDtypeStruct((depth, 1, full), F32)],
        compiler_params=_cparams(("parallel",)),
    )(c_all, dmod_cols, dmod_full)


def _adamw(gparts, own, w, m, v, name, layer=0, prev=None):
    n_p, rows, cols = gparts.shape
    tr = rows
    if rows > 512:
        tr = next(c for c in range(512, 7, -8) if rows % c == 0)
    off = layer * (rows // tr)
    has_own = own is not None
    n_prev = 0 if prev is None else 4

    def body(*refs):
        g_ref = refs[0]
        own_ref = refs[1] if has_own else None
        w_ref, m_ref, v_ref = refs[1 + has_own:4 + has_own]
        go_ref, do_ref, mo_ref, vo_ref = refs[4 + has_own + n_prev:]
        if has_own:
            g = own_ref[...].astype(F32) + g_ref[0].astype(F32)
        else:
            g = g_ref[0].astype(F32)
        for p in range(1, n_p):
            g = g + g_ref[p].astype(F32)
        m_new = ADAM_B1 * m_ref[...] + (1.0 - ADAM_B1) * g
        v_new = ADAM_B2 * v_ref[...] + (1.0 - ADAM_B2) * (g * g)
        m_hat = m_new / (1.0 - ADAM_B1 ** ADAM_STEP)
        v_hat = v_new / (1.0 - ADAM_B2 ** ADAM_STEP)
        go_ref[...] = g
        do_ref[...] = -ADAM_LR * (m_hat / (jnp.sqrt(v_hat) + ADAM_EPS) + ADAM_WD * w_ref[...])
        mo_ref[...] = m_new
        vo_ref[...] = v_new

    spec = pl.BlockSpec((tr, cols), lambda i: (off + i, 0))
    in_specs = [pl.BlockSpec((n_p, tr, cols), lambda i: (0, i, 0))]
    args = [gparts]
    if has_own:
        in_specs.append(pl.BlockSpec((tr, cols), lambda i: (i, 0)))
        args.append(own)
    in_specs += [spec, spec, spec]
    args += [w, m, v]
    aliases = {}
    if prev is not None:
        aliases = {len(args) + k: k for k in range(4)}
        in_specs += [pl.BlockSpec(memory_space=pl.ANY)] * 4
        args += list(prev)
    shp = jax.ShapeDtypeStruct(w.shape, F32)
    return pl.pallas_call(
        body, name=name, grid=(rows // tr,), in_specs=in_specs,
        out_specs=[spec, spec, spec, spec], out_shape=[shp, shp, shp, shp], input_output_aliases=aliases,
        compiler_params=_cparams(("parallel",)),
    )(*args)


def _sum_parts(parts, name):
    n_p, rows, cols = parts.shape
    tr = 256 if rows % 256 == 0 else rows

    def body(p_ref, o_ref):
        acc = p_ref[0]
        for p in range(1, n_p):
            acc = acc + p_ref[p]
        o_ref[...] = acc

    return pl.pallas_call(
        body, name=name, grid=(rows // tr,),
        in_specs=[pl.BlockSpec((n_p, tr, cols), lambda i: (0, i, 0))],
        out_specs=pl.BlockSpec((tr, cols), lambda i: (i, 0)),
        out_shape=jax.ShapeDtypeStruct((rows, cols), F32),
        compiler_params=_cparams(("parallel",)),
    )(parts)


def _exchange(ins, out_shapes, plan, name):
    n_in, n_out, n_cp = len(ins), len(out_shapes), len(plan)
    flips = [(fx, fy, fc) for fx in (0, 1) for fy in (0, 1) for fc in (0, 1)][1:]

    def body(*refs):
        in_refs, out_refs = refs[:n_in], refs[n_in:n_in + n_out]
        send_sems, recv_sems, loc_sems = refs[n_in + n_out:]
        x, y, c = lax.axis_index("x"), lax.axis_index("y"), lax.axis_index("c")
        me = 4 * x + 2 * y + c
        peers = []
        for fx, fy, fc in flips:
            px, py, pc = (1 - x if fx else x), (1 - y if fy else y), (1 - c if fc else c)
            peers.append(((px, py, pc), 4 * px + 2 * py + pc))

        def sel(ref, idx):
            return ref.at[idx] if idx else ref

        def remote(n, k, src_dev_slot, dst_for):
            i, in_sel, o, out_sel = plan[n]
            dev, idx = peers[k]
            return pltpu.make_async_remote_copy(
                src_ref=sel(in_refs[i], in_sel(dst_for)), dst_ref=sel(out_refs[o], out_sel(src_dev_slot)),
                send_sem=send_sems.at[n, k], recv_sem=recv_sems.at[n, k],
                device_id=dev, device_id_type=pl.DeviceIdType.MESH)

        local = []
        for n, (i, in_sel, o, out_sel) in enumerate(plan):
            cp = pltpu.make_async_copy(sel(in_refs[i], in_sel(me)), sel(out_refs[o], out_sel(me)), loc_sems.at[n])
            cp.start()
            local.append(cp)
        sends = []
        for k in range(len(flips)):
            for n in range(n_cp):
                cp = remote(n, k, me, peers[k][1])
                cp.start()
                sends.append(cp)
        for k in range(len(flips)):
            for n in range(n_cp):
                remote(n, k, peers[k][1], me).wait_recv()
        for cp in sends:
            cp.wait_send()
        for cp in local:
            cp.wait()

    any_spec = pl.BlockSpec(memory_space=pl.ANY)
    return pl.pallas_call(
        body, name=name,
        in_specs=[any_spec] * n_in, out_specs=[any_spec] * n_out, out_shape=list(out_shapes),
        scratch_shapes=[pltpu.SemaphoreType.DMA((n_cp, N_DEV - 1)), pltpu.SemaphoreType.DMA((n_cp, N_DEV - 1)),
                        pltpu.SemaphoreType.DMA((n_cp,))],
    )(*ins)


def _all_gather(arrs, name):
    n = len(arrs)

    def body(*refs):
        in_refs, out_refs = refs[:n], refs[n:2 * n]
        send_sems, recv_sems, loc_sems = refs[2 * n:]
        x, y, c = lax.axis_index("x"), lax.axis_index("y"), lax.axis_index("c")
        me, sibling = (x, y, c), (x, y, 1 - c)
        chips = [(1 - x, y), (x, 1 - y), (1 - x, 1 - y)]

        def copy(a, k, block, to, src=None):
            slot = out_refs[a].at[4 * block[0] + 2 * block[1] + block[2]]
            return pltpu.make_async_remote_copy(
                src_ref=slot if src is None else src, dst_ref=slot, send_sem=send_sems.at[a, k],
                recv_sem=recv_sems.at[a, k], device_id=to, device_id_type=pl.DeviceIdType.MESH)

        mine = [pltpu.make_async_copy(in_refs[a], out_refs[a].at[4 * x + 2 * y + c], loc_sems.at[a])
                for a in range(n)]
        for cp in mine:
            cp.start()
        first = []
        for a in range(n):
            first.append(copy(a, 0, me, sibling, src=in_refs[a]))
            first += [copy(a, 1 + j, me, (*chip, c), src=in_refs[a]) for j, chip in enumerate(chips)]
        for cp in first:
            cp.start()
        passed = []
        for j, chip in enumerate(chips):
            for a in range(n):
                copy(a, 1 + j, (*chip, c), me).wait_recv()
                cp = copy(a, 4 + j, (*chip, c), sibling)
                cp.start()
                passed.append(cp)
        for a in range(n):
            copy(a, 0, sibling, me).wait_recv()
        for j, chip in enumerate(chips):
            for a in range(n):
                copy(a, 4 + j, (*chip, 1 - c), me).wait_recv()
        for cp in first + passed:
            cp.wait_send()
        for cp in mine:
            cp.wait()

    any_spec = pl.BlockSpec(memory_space=pl.ANY)
    return pl.pallas_call(
        body, name=name, in_specs=[any_spec] * n, out_specs=[any_spec] * n,
        out_shape=[jax.ShapeDtypeStruct((N_DEV,) + a.shape, a.dtype) for a in arrs],
        scratch_shapes=[pltpu.SemaphoreType.DMA((n, N_DEV - 1)), pltpu.SemaphoreType.DMA((n, N_DEV - 1)),
                        pltpu.SemaphoreType.DMA((n,))],
    )(*arrs)


def _reduce_scatter_push(groups, name):
    ins, shapes, plan = [], [], []
    for w, layers in enumerate(groups):
        shapes.append(jax.ShapeDtypeStruct((N_DEV, len(layers)) + layers[0].shape[1:], layers[0].dtype))
        for l, arr in enumerate(layers):
            plan.append((len(ins), (lambda p: (p,)), w, (lambda s, l=l: (s, l))))
            ins.append(arr)
    return _exchange(ins, shapes, plan, name)


def _flip_peers():
    x, y, c = lax.axis_index("x"), lax.axis_index("y"), lax.axis_index("c")
    peers = []
    for fx, fy, fc in [(fx, fy, fc) for fx in (0, 1) for fy in (0, 1) for fc in (0, 1)][1:]:
        px, py, pc = (1 - x if fx else x), (1 - y if fy else y), (1 - c if fc else c)
        peers.append(((px, py, pc), 4 * px + 2 * py + pc))
    return 4 * x + 2 * y + c, peers


def _push_start(srcs, name):
    n, n_peer = len(srcs), N_DEV - 1
    lands = [jnp.zeros(a.shape, a.dtype) for a in srcs]

    def body(*refs):
        src_refs, land_refs = refs[:n], refs[n:2 * n]
        send_sems, recv_sems = refs[2 * n], refs[2 * n + 1]
        token = refs[-1]
        me, peers = _flip_peers()
        for k, (dev, idx) in enumerate(peers):
            for a in range(n):
                pltpu.make_async_remote_copy(
                    src_ref=src_refs[a].at[idx], dst_ref=land_refs[a].at[me], send_sem=send_sems.at[a * n_peer + k],
                    recv_sem=recv_sems.at[a * n_peer + k], device_id=dev,
                    device_id_type=pl.DeviceIdType.MESH).start()
        token[...] = jnp.zeros_like(token)

    hbm = pl.BlockSpec(memory_space=pltpu.HBM)
    sem = pl.BlockSpec(memory_space=pltpu.SEMAPHORE)
    arrs = list(srcs) + lands
    res = pl.pallas_call(
        body, name=name, in_specs=[hbm] * (2 * n),
        out_specs=(sem, sem, *[hbm] * (2 * n), pl.BlockSpec(memory_space=pltpu.VMEM)),
        out_shape=(pltpu.SemaphoreType.DMA((n * n_peer,)), pltpu.SemaphoreType.DMA((n * n_peer,)),
                   *[pltpu.HBM(a.shape, a.dtype) for a in arrs], jax.ShapeDtypeStruct((8, LANES), F32)),
        input_output_aliases={i: 2 + i for i in range(2 * n)},
        compiler_params=pltpu.CompilerParams(has_side_effects=pltpu.SideEffectType.DATAFLOW_SIDE_EFFECTING),
    )(*[pltpu.with_memory_space_constraint(a, pltpu.HBM) for a in arrs])
    return res[0], res[1], list(res[2:2 + n]), list(res[2 + n:2 + 2 * n]), res[-1]


def _push_wait(send_sems, recv_sems, srcs, lands, after, name):
    n, n_peer = len(srcs), N_DEV - 1

    def body(*refs):
        src_refs, land_refs = refs[:n], refs[n:2 * n]
        send_s, recv_s = refs[2 * n], refs[2 * n + 1]
        _, peers = _flip_peers()
        for k, (dev, idx) in enumerate(peers):
            for a in range(n):
                cp = pltpu.make_async_remote_copy(
                    src_ref=src_refs[a].at[idx], dst_ref=land_refs[a].at[idx], send_sem=send_s.at[a * n_peer + k],
                    recv_sem=recv_s.at[a * n_peer + k], device_id=dev, device_id_type=pl.DeviceIdType.MESH)
                cp.wait_send()
                cp.wait_recv()

    hbm = pl.BlockSpec(memory_space=pltpu.HBM)
    sem = pl.BlockSpec(memory_space=pltpu.SEMAPHORE)
    arrs = list(srcs) + list(lands)
    res = pl.pallas_call(
        body, name=name, in_specs=[hbm] * (2 * n) + [sem, sem, pl.BlockSpec(memory_space=pl.ANY)],
        out_specs=tuple([hbm] * (2 * n)), out_shape=tuple(pltpu.HBM(a.shape, a.dtype) for a in arrs),
        input_output_aliases={i: i for i in range(2 * n)},
        compiler_params=pltpu.CompilerParams(has_side_effects=pltpu.SideEffectType.DATAFLOW_SIDE_EFFECTING),
    )(*arrs, send_sems, recv_sems, after)
    return list(res[:n]), list(res[n:])


def _ffn_fwd(x, mod, w_in, w_out, lng, lnb, rows, tag):
    bsz, seq, d = x.shape
    t = bsz * seq
    h = _modulate(x, mod, rows[0], rows[1], f"modulate_{tag}")
    z, a = _ffn_in_swiglu(h.reshape(t, d), w_in, f"ffn_in_{tag}")
    f = _matmul(a, w_out, mode="nn", group_out=False, out_dtype=F32, tm=1024, tk=a.shape[2],
                name=f"ffn_out_{tag}").reshape(bsz, seq, d)
    y = _res_ln(x, f, mod, lng, lnb, rows[2], 0.5, f"res_ln_{tag}")
    return y, (x, h, z, a, f)


def _ffn_bwd(dy, saved, mod, w_in, w_out, lng, lnb, rows, tag):
    x, h, z, a, f = saved
    bsz, seq, d = x.shape
    t = bsz * seq
    dx_res, df, dgate, dlg, dlb = _res_ln_bwd(dy, x, f, mod, lng, lnb, rows[2], 0.5, f"res_ln_bwd_{tag}")
    df2 = df.reshape(1, t, d)
    dw_out = _matmul(a, df2, mode="tn", group_out=True, out_dtype=BF16, tm=a.shape[2], tk=min(t, 2048),
                     name=f"ffn_out_dw_{tag}")
    dz = _ffn_out_dx_swiglu(df.reshape(t, d), w_out, z, f"ffn_out_dx_{tag}").reshape(N_DEV, t, -1)
    dh = _matmul(dz, w_in, mode="nt", group_out=False, out_dtype=F32, tm=1024, tk=dz.shape[2],
                 name=f"ffn_in_dx_{tag}").reshape(bsz, seq, d)
    dw_in = _matmul(h.reshape(1, t, d), dz, mode="tn", group_out=True, out_dtype=BF16, tm=d, tk=min(t, 2048),
                    name=f"ffn_in_dw_{tag}")
    dx, dsh, dsc = _modulate_bwd(dh, x, mod, dx_res, rows[1], f"modulate_bwd_{tag}")
    return dx, (dsh, dsc, dgate), dw_in, dw_out, dlg, dlb


def _mixer_fwd(x, mod, wts, small, lng, lnb, layer, tabs):
    bsz, seq, d = x.shape
    t = bsz * seq
    h = _modulate(x, mod, 3, 4, "modulate_mix")
    proj = _matmul(h.reshape(1, t, d), wts["mix_in"][None], mode="nn", group_out=True, out_dtype=F32, tm=512, tk=d,
                   name="mix_in").reshape(bsz, seq, PACK_W)
    mo, states = _hgrn_fwd(proj, small["lb_logits8"], small["hgrn_norm_g"], layer, f"hgrn_fwd_l{layer}")
    q, kv = _mla_pre(proj, small["q_norm_g"], small["kv_norm_g"], wts["uq"], wts["ukv"], tabs, "mla_pre")
    mla_scale = float((B_NOPE + B_ROPE) ** -0.5)
    mo, lse_b = _attn_fwd(q, 0, kv, 0, mo, 2, None, mla_scale, "mla_attn_fwd")
    fg = _fox_gate(proj, small["fox_b_f"], "fox_gate")
    gates = (fg, jnp.swapaxes(fg[:, :, 0:8], 1, 2))
    fox_scale = float(HEAD_DIM ** -0.5)
    mo, lse_c = _attn_fwd(proj, P_CQ // LANES, proj, P_CKV // LANES, mo, 6, gates, fox_scale, "fox_attn_fwd")
    mo = _gmlp_fwd(proj, mo, small["gmlp_ln_g"], small["gmlp_ln_b"], small["gmlp_w_s"], small["gmlp_bst"],
                   "gmlp_fwd")
    mixed = _matmul(mo.reshape(1, t, MO_W), wts["mix_out"][None], mode="nn", group_out=True, out_dtype=F32,
                    tm=1024, tk=MO_W, name="mix_out").reshape(bsz, seq, d)
    y = _res_ln(x, mixed, mod, lng, lnb, 5, 1.0, "res_ln_mix")
    return y, (x, h, proj, mo, states, q, kv, lse_b, gates, lse_c, mixed)


def _mixer_bwd(dy, saved, mod, wts, small, lng, lnb, layer, tabs):
    x, h, proj, mo, states, q, kv, lse_b, gates, lse_c, mixed = saved
    bsz, seq, d = x.shape
    t = bsz * seq
    dx_res, dmixed, dgate, dlg, dlb = _res_ln_bwd(dy, x, mixed, mod, lng, lnb, 5, 1.0, "res_ln_bwd_mix")
    dm2 = dmixed.reshape(1, t, d)
    dmo = _matmul(dm2, wts["mix_out"][None], mode="nt", group_out=True, out_dtype=F32, tm=1024, tk=d,
                  name="mix_out_dx").reshape(bsz, seq, MO_W)
    dw_out = _matmul(mo.reshape(1, t, MO_W), dm2, mode="tn", group_out=True, out_dtype=F32, tm=512, tk=min(t, 2048),
                     name="mix_out_dw")[0]
    g = {}
    dproj, g["lb_logits8"], g["hgrn_norm_g"] = _hgrn_bwd(dmo, proj, states, small["lb_logits8"],
                                                         small["hgrn_norm_g"], layer, f"hgrn_bwd_l{layer}")
    mla_scale = float((B_NOPE + B_ROPE) ** -0.5)
    dq, delta_b, _ = _attn_bwd_q(q, 0, kv, 0, mo, dmo, 2, lse_b, None, mla_scale,
                                 jax.ShapeDtypeStruct((bsz, seq, 512), F32), 0, "mla_attn_bwd_q")
    dkv, _ = _attn_bwd_kv(q, 0, kv, 0, dmo, 2, lse_b, delta_b, None, mla_scale,
                          jax.ShapeDtypeStruct((bsz, seq, 1024), F32), 0, "mla_attn_bwd_kv")
    dproj, g["q_norm_g"], g["kv_norm_g"], g["uq"], g["ukv"] = _mla_pre_bwd(
        dq, dkv, dproj, proj, small["q_norm_g"], small["kv_norm_g"], wts["uq"], wts["ukv"], tabs, "mla_pre_bwd")
    fox_scale = float(HEAD_DIM ** -0.5)
    dproj, delta_c, dfq = _attn_bwd_q(proj, P_CQ // LANES, proj, P_CKV // LANES, mo, dmo, 6, lse_c, gates,
                                      fox_scale, dproj, P_CQ // LANES, "fox_attn_bwd_q")
    dproj, dfk = _attn_bwd_kv(proj, P_CQ // LANES, proj, P_CKV // LANES, dmo, 6, lse_c, delta_c, gates, fox_scale,
                              dproj, P_CKV // (2 * LANES), "fox_attn_bwd_kv")
    dcum = jnp.swapaxes(dfq[..., 0], 1, 2) + jnp.swapaxes(dfk[:, :, 0, :], 1, 2)
    dcum = jnp.pad(dcum, ((0, 0), (0, 0), (0, LANES - N_HEADS)))
    dproj, g["fox_b_f"] = _fox_gate_bwd(dcum, dproj, proj, small["fox_b_f"], "fox_gate_bwd")
    dproj, g["gmlp_ln_g"], g["gmlp_ln_b"], g["gmlp_w_s"], g["gmlp_bst"] = _gmlp_bwd(
        dmo, dproj, proj, small["gmlp_ln_g"], small["gmlp_ln_b"], small["gmlp_w_s"], small["gmlp_bst"], "gmlp_bwd")
    dp2 = dproj.reshape(1, t, PACK_W)
    dh = _matmul(dp2, wts["mix_in"][None], mode="nt", group_out=True, out_dtype=F32, tm=512, tk=PACK_W,
                 name="mix_in_dx").reshape(bsz, seq, d)
    dw_in = _matmul(h.reshape(1, t, d), dp2, mode="tn", group_out=True, out_dtype=BF16, tm=512, tk=1024,
                    name="mix_in_dw")[0]
    dx, dsh, dsc = _modulate_bwd(dh, x, mod, dx_res, 4, "modulate_bwd_mix")
    return dx, (dsh, dsc, dgate), dw_in, dw_out, g, dlg, dlb


def _small_views(p, layer):
    return {
        "lb_logits8": jnp.pad(p["hgrn_lb_logits"], ((0, 8 - DEPTH), (0, 0))),
        "hgrn_norm_g": p["hgrn_norm_g"][layer][None],
        "q_norm_g": p["mla_q_norm_g"][layer][None],
        "kv_norm_g": p["mla_kv_norm_g"][layer][None],
        "fox_b_f": jnp.pad(p["fox_b_f"][layer][None], ((0, 0), (0, LANES - N_HEADS))),
        "gmlp_ln_g": p["gmlp_ln_g"][layer][None],
        "gmlp_ln_b": p["gmlp_ln_b"][layer][None],
        "gmlp_w_s": p["gmlp_w_s"][layer],
        "gmlp_bst": jnp.pad(p["gmlp_b_s"][layer].T, ((0, 0), (0, LANES - N_HEADS))),
    }


def _local_step(x, mod, target, full, p, grads_ready=None):
    bsz, seq, d = x.shape
    tabs = _rope_tables(seq)
    saved = []
    for l in range(DEPTH):
        w, sm = full[l], _small_views(p, l)
        lng, lnb = p["ln_g"][l], p["ln_b"][l]
        x, s1 = _ffn_fwd(x, mod[l], w["ffn1_in"], w["ffn1_out"], lng[0:1], lnb[0:1], (0, 1, 2), "ffn1")
        x, s2 = _mixer_fwd(x, mod[l], w, sm, lng[1:2], lnb[1:2], l, tabs)
        x, s3 = _ffn_fwd(x, mod[l], w["ffn2_in"], w["ffn2_out"], lng[2:3], lnb[2:3], (6, 7, 8), "ffn2")
        saved.append((s1, s2, s3))
    dx, loss = _loss_head(x, target, "loss_head")
    big, small, dmods = [None] * DEPTH, [None] * DEPTH, [None] * DEPTH
    tie = None
    for l in reversed(range(DEPTH)):
        w, sm = full[l], _small_views(p, l)
        lng, lnb = p["ln_g"][l], p["ln_b"][l]
        s1, s2, s3 = saved[l]
        mod_l = mod[l] if tie is None else mod[l] + tie
        dx, dm3, dwi2, dwo2, dlg2, dlb2 = _ffn_bwd(dx, s3, mod_l, w["ffn2_in"], w["ffn2_out"], lng[2:3], lnb[2:3],
                                                   (6, 7, 8), "ffn2")
        dx, dm2, dwmi, dwmo, g, dlg1, dlb1 = _mixer_bwd(dx, s2, mod_l, w, sm, lng[1:2], lnb[1:2], l, tabs)
        if grads_ready is not None:
            tie = grads_ready(l, "late", {"ffn2_in": dwi2, "ffn2_out": dwo2, "mix_in": dwmi, "mix_out": dwmo})
            mod_l = mod_l if tie is None else mod_l + tie
        dx, dm1, dwi1, dwo1, dlg0, dlb0 = _ffn_bwd(dx, s1, mod_l, w["ffn1_in"], w["ffn1_out"], lng[0:1], lnb[0:1],
                                                   (0, 1, 2), "ffn1")
        if grads_ready is not None:
            tie = grads_ready(l, "early", {"ffn1_in": dwi1, "ffn1_out": dwo1})
        dmods[l] = jnp.concatenate(list(dm1) + list(dm2) + list(dm3), axis=1)
        big[l] = {"ffn1_in": dwi1, "ffn1_out": dwo1, "ffn2_in": dwi2, "ffn2_out": dwo2, "mix_in": dwmi,
                  "mix_out": dwmo}
        g["ln_g"] = jnp.concatenate([dlg0, dlg1, dlg2], axis=0)
        g["ln_b"] = jnp.concatenate([dlb0, dlb1, dlb2], axis=0)
        small[l] = g
    return loss, dx, jnp.stack(dmods), big, small


_BIG = ("ffn1_in", "ffn1_out", "ffn2_in", "ffn2_out", "mix_in", "mix_out")


def _small_grad_list(small, loss):
    def both(fn):
        return jnp.stack([fn(small[l]) for l in range(DEPTH)])

    uq_src, ukv_src = _uq_src(), _ukv_src()
    return [
        ("loss", loss.reshape(1)),
        ("ln_g", both(lambda g: g["ln_g"])), ("ln_b", both(lambda g: g["ln_b"])),
        ("hgrn_lb_logits", small[0]["lb_logits8"][:DEPTH] + small[1]["lb_logits8"][:DEPTH]),
        ("hgrn_norm_g", both(lambda g: g["hgrn_norm_g"][0])),
        ("mla_q_norm_g", both(lambda g: g["q_norm_g"][0])),
        ("mla_kv_norm_g", both(lambda g: g["kv_norm_g"][0])),
        ("mla_w_uq", both(lambda g: _unpack_cols(g["uq"], uq_src, 384))),
        ("mla_w_ukv", both(lambda g: _unpack_cols(g["ukv"], ukv_src, 512))),
        ("fox_b_f", both(lambda g: g["fox_b_f"][0, :N_HEADS])),
        ("gmlp_ln_g", both(lambda g: g["gmlp_ln_g"][0])), ("gmlp_ln_b", both(lambda g: g["gmlp_ln_b"][0])),
        ("gmlp_w_s", both(lambda g: g["gmlp_w_s"])),
        ("gmlp_b_s", both(lambda g: g["gmlp_bst"][:, :N_HEADS].T)),
    ]


_PACK_COLS = 512


def _pack_small(items):
    flat = jnp.concatenate([a.reshape(-1).astype(F32) for _, a in items])
    n = flat.shape[0]
    tile = 8 * _PACK_COLS
    flat = jnp.pad(flat, (0, (-n) % tile))
    return flat.reshape(-1, _PACK_COLS)


def _unpack_small(buf, items):
    flat = buf.reshape(-1)
    out, off = {}, 0
    for name, a in items:
        out[name] = flat[off:off + a.size].reshape(a.shape)
        off += a.size
    return out


def _as2d(a):
    return a.reshape(-1, a.shape[-1])


def kernel(x, c, ada_w, ada_b, ln_g, ln_b, ffn1_w_in, ffn1_w_out, ffn2_w_in, ffn2_w_out, mix_w_in, mix_w_out, hgrn_lb_logits, hgrn_norm_g, mla_q_norm_g, mla_kv_norm_g, mla_w_uq, mla_w_ukv, fox_b_f, gmlp_ln_g, gmlp_ln_b, gmlp_w_s, gmlp_b_s, loss_target, m_ada_w, m_ada_b, m_ln_g, m_ln_b, m_ffn1_w_in, m_ffn1_w_out, m_ffn2_w_in, m_ffn2_w_out, m_mix_w_in, m_mix_w_out, m_hgrn_lb_logits, m_hgrn_norm_g, m_mla_q_norm_g, m_mla_kv_norm_g, m_mla_w_uq, m_mla_w_ukv, m_fox_b_f, m_gmlp_ln_g, m_gmlp_ln_b, m_gmlp_w_s, m_gmlp_b_s, v_ada_w, v_ada_b, v_ln_g, v_ln_b, v_ffn1_w_in, v_ffn1_w_out, v_ffn2_w_in, v_ffn2_w_out, v_mix_w_in, v_mix_w_out, v_hgrn_lb_logits, v_hgrn_norm_g, v_mla_q_norm_g, v_mla_kv_norm_g, v_mla_w_uq, v_mla_w_ukv, v_fox_b_f, v_gmlp_ln_g, v_gmlp_ln_b, v_gmlp_w_s, v_gmlp_b_s):
    names = ["ada_w", "ada_b", "ln_g", "ln_b", "ffn1_w_in", "ffn1_w_out", "ffn2_w_in", "ffn2_w_out", "mix_w_in",
             "mix_w_out", "hgrn_lb_logits", "hgrn_norm_g", "mla_q_norm_g", "mla_kv_norm_g", "mla_w_uq", "mla_w_ukv",
             "fox_b_f", "gmlp_ln_g", "gmlp_ln_b", "gmlp_w_s", "gmlp_b_s"]
    w = dict(zip(names, [ada_w, ada_b, ln_g, ln_b, ffn1_w_in, ffn1_w_out, ffn2_w_in, ffn2_w_out, mix_w_in, mix_w_out,
                         hgrn_lb_logits, hgrn_norm_g, mla_q_norm_g, mla_kv_norm_g, mla_w_uq, mla_w_ukv, fox_b_f,
                         gmlp_ln_g, gmlp_ln_b, gmlp_w_s, gmlp_b_s]))
    m = dict(zip(names, [m_ada_w, m_ada_b, m_ln_g, m_ln_b, m_ffn1_w_in, m_ffn1_w_out, m_ffn2_w_in, m_ffn2_w_out,
                         m_mix_w_in, m_mix_w_out, m_hgrn_lb_logits, m_hgrn_norm_g, m_mla_q_norm_g, m_mla_kv_norm_g,
                         m_mla_w_uq, m_mla_w_ukv, m_fox_b_f, m_gmlp_ln_g, m_gmlp_ln_b, m_gmlp_w_s, m_gmlp_b_s]))
    v = dict(zip(names, [v_ada_w, v_ada_b, v_ln_g, v_ln_b, v_ffn1_w_in, v_ffn1_w_out, v_ffn2_w_in, v_ffn2_w_out,
                         v_mix_w_in, v_mix_w_out, v_hgrn_lb_logits, v_hgrn_norm_g, v_mla_q_norm_g, v_mla_kv_norm_g,
                         v_mla_w_uq, v_mla_w_ukv, v_fox_b_f, v_gmlp_ln_g, v_gmlp_ln_b, v_gmlp_w_s, v_gmlp_b_s]))
    bsz, seq, d = x.shape
    me = 4 * lax.axis_index("x") + 2 * lax.axis_index("y") + lax.axis_index("c")
    mix_src, uq_src, ukv_src, mo_src = _mix_in_src(), _uq_src(), _ukv_src(), _mo_src()

    shard_names = ["ffn1_w_in", "ffn1_w_out", "ffn2_w_in", "ffn2_w_out", "mix_w_in", "mix_w_out", "mla_w_uq",
                   "mla_w_ukv"]
    shards = []
    for l in range(DEPTH):
        for n in shard_names:
            a = w[n][l]
            if n == "mix_w_in":
                a = _pack_cols(a, mix_src)
            shards.append(a.astype(BF16))
    gathered = _all_gather(shards + [c, ln_g, ln_b], "gather_weights")
    c_all = gathered[-3].reshape(N_DEV * bsz, d)
    ln_g_full = jnp.moveaxis(gathered[-2], 0, 2).reshape(DEPTH, 3, d)
    ln_b_full = jnp.moveaxis(gathered[-1], 0, 2).reshape(DEPTH, 3, d)

    full = []
    for l in range(DEPTH):
        gw = dict(zip(shard_names, gathered[l * len(shard_names):(l + 1) * len(shard_names)]))
        uq = jnp.moveaxis(gw["mla_w_uq"], 0, 1).reshape(256, 384)
        ukv = jnp.moveaxis(gw["mla_w_ukv"], 0, 1).reshape(128, 512)
        full.append({
            "ffn1_in": gw["ffn1_w_in"], "ffn1_out": gw["ffn1_w_out"].reshape(4, 704, d),
            "ffn2_in": gw["ffn2_w_in"], "ffn2_out": gw["ffn2_w_out"].reshape(4, 704, d),
            "mix_in": gw["mix_w_in"].reshape(d, PACK_W),
            "mix_out": _pack_cols(gw["mix_w_out"].reshape(d, d).T, mo_src).T,
            "uq": _pack_cols(uq, uq_src), "ukv": _pack_cols(ukv, ukv_src),
        })

    mod_cols = _ada_fwd(c_all, ada_w, "ada_fwd")
    mod_all, = _all_gather([mod_cols], "gather_mod")
    mod_mine = lax.dynamic_slice_in_dim(mod_all, me * bsz, bsz, axis=2)
    mod = jnp.moveaxis(mod_mine, 0, 2).reshape(DEPTH, bsz, N_MOD * d) + ada_b[:, None, :]
    mod = mod.reshape(DEPTH, bsz, N_MOD, d)

    p = dict(w)
    p["ln_g"], p["ln_b"] = ln_g_full, ln_b_full
    def chunks(name, arr):
        if name in ("ffn1_in", "ffn2_in"):
            return arr
        if name in ("ffn1_out", "ffn2_out"):
            return arr.reshape(N_DEV, arr.shape[1] // 2, d)
        if name == "mix_in":
            return arr.reshape(N_DEV, d // N_DEV, PACK_W)
        return _unpack_cols(arr.T, mo_src, d).T.astype(BF16).reshape(N_DEV, d // N_DEV, d)

    pending, started = {}, []

    def grads_ready(l, part, grads):
        pending.update({n: chunks(n, a) for n, a in grads.items()})
        if (l == DEPTH - 1 and part == "late") or (l == 0 and part == "early"):
            return None
        keys = sorted(pending)
        handles = _push_start([pending[k] for k in keys], f"push_start_l{l}")
        pending.clear()
        started.append((l, keys, handles))
        return handles[-1][0, 0]

    loss, grad_x, dmod, big, small = _local_step(x, mod, loss_target, full, p, grads_ready)
    del big

    dmod_all, = _all_gather([dmod.reshape(DEPTH, bsz, N_MOD * d)], "gather_dmod")
    dmod_full = jnp.moveaxis(dmod_all, 0, 1).reshape(DEPTH, N_DEV * bsz, N_MOD * d)
    cols = ada_w.shape[2]
    dmod_cols = lax.dynamic_slice_in_dim(dmod_full, me * cols, cols, axis=2)
    g_ada_w, g_ada_b = _ada_bwd(c_all, dmod_cols, dmod_full, "ada_bwd")

    recv = {}
    for l, keys, (send_sems, recv_sems, srcs, lands, _) in started:
        srcs, lands = _push_wait(send_sems, recv_sems, srcs, lands, grad_x, f"push_wait_l{l}")
        for k, src, land in zip(keys, srcs, lands):
            recv[(k, l)] = (land, lax.dynamic_index_in_dim(src, me, 0, keepdims=False))
    last = sorted(pending)
    for k, buf in zip(last, _reduce_scatter_push([[pending[k]] for k in last], "scatter_grads")):
        recv[(k, 0)] = (buf[:, 0], None)

    items = _small_grad_list(small, loss)
    parts, = _all_gather([_pack_small(items)], "gather_small")
    sg = _unpack_small(_sum_parts(parts, "sum_small"), items)

    out = {}

    def update(name, gparts):
        shape = w[name].shape
        res = _adamw(gparts, None, _as2d(w[name]), _as2d(m[name]), _as2d(v[name]), f"adamw_{name}")
        out[name] = tuple(r.reshape(shape) for r in res)

    big_of = {"ffn1_w_in": "ffn1_in", "ffn1_w_out": "ffn1_out", "ffn2_w_in": "ffn2_in", "ffn2_w_out": "ffn2_out",
              "mix_w_in": "mix_in", "mix_w_out": "mix_out"}
    for name, key in big_of.items():
        res = None
        for l in reversed(range(DEPTH)):
            parts, own = recv[(key, l)]
            if key == "mix_in":
                parts = _unpack_cols(parts, mix_src, MIX_ORIG_W)
                own = None if own is None else _unpack_cols(own, mix_src, MIX_ORIG_W)
            res = _adamw(parts, own, _as2d(w[name]), _as2d(m[name]), _as2d(v[name]), f"adamw_{name}_l{l}",
                         layer=l, prev=res)
        out[name] = tuple(r.reshape(w[name].shape) for r in res)
    update("ada_w", _as2d(g_ada_w)[None])
    update("ada_b", g_ada_b.reshape(1, DEPTH, N_MOD * d))
    for name in ("ln_g", "ln_b"):
        g_loc = lax.dynamic_slice_in_dim(sg[name], me * (d // N_DEV), d // N_DEV, axis=2)
        update(name, _as2d(g_loc)[None])
    for name, width in (("mla_w_uq", 48), ("mla_w_ukv", 64)):
        g_loc = lax.dynamic_slice_in_dim(sg[name], me * width, width, axis=2)
        update(name, _as2d(g_loc)[None])
    for name in ("hgrn_lb_logits", "hgrn_norm_g", "mla_q_norm_g", "mla_kv_norm_g", "fox_b_f", "gmlp_ln_g",
                 "gmlp_ln_b", "gmlp_w_s", "gmlp_b_s"):
        update(name, _as2d(sg[name])[None])

    return (sg["loss"][0], grad_x, *[out[n][0] for n in names], *[out[n][1] for n in names],
            *[out[n][2] for n in names], *[out[n][3] for n in names])
```

```python
import functools

import numpy as np
import jax
import jax.numpy as jnp
from jax import lax
from jax.experimental import pallas as pl
from jax.experimental.pallas import tpu as pltpu

F32 = jnp.float32
BF16 = jnp.bfloat16
HI = lax.Precision.HIGHEST

D_MODEL = 1024
DEPTH = 2
GROUP_WIDTH = 256
N_HEADS = 4
HEAD_DIM = 64
A_CHUNK = 16
LB_FLOOR = 1e-30
B_NOPE = 64
B_ROPE = 32
ROPE_THETA = 10000.0
D_CHUNK = 128
D_FF = 2816
N_MOD = 9
ALPHA = (2 * DEPTH) ** 0.25
LN_EPS = 1e-5
RMS_EPS = 1e-6
ADAM_LR = 0.001
ADAM_B1 = 0.9
ADAM_B2 = 0.999
ADAM_EPS = 1e-08
ADAM_WD = 0.01
ADAM_STEP = 10

N_DEV = 8
LANES = 128
PACK_W = 3712
MO_W = 1536
VMEM_LIMIT = 56 * 1024 * 1024
NEG = -1e30
ATTN_TILE = 512

MIX_ORIG_W = 2724
O_BCQ, O_BCKV, O_BKR, O_CQ, O_CK, O_CV, O_CF, O_DU, O_DV = 1024, 1280, 1408, 1440, 1696, 1952, 2208, 2212, 2468
P_B, P_KR, P_CQ, P_CKV, P_D, P_CF = 1024, 1408, 1536, 2048, 3072, 3584


_DN = {"nn": (((1,), (0,)), ((), ())), "nt": (((1,), (1,)), ((), ())), "tn": (((0,), (0,)), ((), ()))}


def _raw_bdot(a, b, mode):
    return lax.dot_general(a.astype(BF16), b.astype(BF16), _DN[mode], preferred_element_type=F32)


@functools.partial(jax.custom_vjp, nondiff_argnums=(2,))
def _bdot(a, b, mode):
    return _raw_bdot(a, b, mode)


def _bdot_fwd(a, b, mode):
    return _raw_bdot(a, b, mode), (a, b)


def _bdot_bwd(mode, res, g):
    a, b = res
    if mode == "nn":
        return _raw_bdot(g, b, "nt"), _raw_bdot(a, g, "tn")
    if mode == "nt":
        return _raw_bdot(g, b, "nn"), _raw_bdot(g, a, "tn")
    return _raw_bdot(b, g, "nt"), _raw_bdot(a, g, "nn")


_bdot.defvjp(_bdot_fwd, _bdot_bwd)


def _cparams(sem):
    return pltpu.CompilerParams(dimension_semantics=sem, vmem_limit_bytes=VMEM_LIMIT)


def _mix_in_src():
    src = -np.ones(PACK_W, np.int64)
    src[0:P_KR] = np.arange(0, O_BKR)
    src[P_KR + 64:P_KR + 80] = O_BKR + np.arange(16)
    src[P_KR + 96:P_KR + 112] = O_BKR + 16 + np.arange(16)
    for h in range(N_HEADS):
        src[P_CQ + 128 * h:P_CQ + 128 * h + 64] = O_CQ + 64 * h + np.arange(64)
        src[P_CKV + 256 * h:P_CKV + 256 * h + 64] = O_CK + 64 * h + np.arange(64)
        src[P_CKV + 256 * h + 128:P_CKV + 256 * h + 192] = O_CV + 64 * h + np.arange(64)
    src[P_D:P_D + 512] = O_DU + np.arange(512)
    src[P_CF:P_CF + 4] = O_CF + np.arange(4)
    return src


def _uq_src():
    src = -np.ones(512, np.int64)
    for h in range(N_HEADS):
        src[128 * h:128 * h + 64] = 96 * h + np.arange(64)
        src[128 * h + 64:128 * h + 80] = 96 * h + 64 + np.arange(16)
        src[128 * h + 96:128 * h + 112] = 96 * h + 80 + np.arange(16)
    return src


def _ukv_src():
    src = -np.ones(1024, np.int64)
    for h in range(N_HEADS):
        src[256 * h:256 * h + 64] = 128 * h + np.arange(64)
        src[256 * h + 128:256 * h + 192] = 128 * h + 64 + np.arange(64)
    return src


def _mo_src():
    src = -np.ones(MO_W, np.int64)
    src[0:256] = np.arange(256)
    for g in range(2):
        for h in range(N_HEADS):
            src[256 + 512 * g + 128 * h:256 + 512 * g + 128 * h + 64] = 256 + 256 * g + 64 * h + np.arange(64)
    src[1280:1536] = 768 + np.arange(256)
    return src


def _runs(idx):
    runs, i = [], 0
    while i < len(idx):
        j = i + 1
        while j < len(idx) and ((idx[i] < 0 and idx[j] < 0) or (idx[i] >= 0 and idx[j] == idx[i] + j - i)):
            j += 1
        runs.append((int(idx[i]), j - i))
        i = j
    return runs


def _take_runs(w, idx):
    parts = [jnp.zeros(w.shape[:-1] + (n,), w.dtype) if s < 0 else lax.slice_in_dim(w, s, s + n, axis=w.ndim - 1)
             for s, n in _runs(idx)]
    return jnp.concatenate(parts, axis=-1)


def _pack_cols(w, src):
    return _take_runs(w, src)


def _unpack_cols(wp, src, n):
    dst = np.zeros(n, np.int64)
    dst[src[src >= 0]] = np.nonzero(src >= 0)[0]
    return _take_runs(wp, dst)


def _rope_tables(seq):
    half = B_ROPE // 2
    inv_freq = ROPE_THETA ** (-jnp.arange(half, dtype=F32) / half)
    ang = jnp.arange(seq).astype(F32)[:, None] * inv_freq[None, :]
    cos, sin = jnp.cos(ang), jnp.sin(ang)
    z16 = jnp.zeros((seq, 16), F32)
    c = jnp.concatenate([jnp.ones((seq, 64), F32), cos, z16, cos, z16], axis=1)
    s1 = jnp.concatenate([jnp.zeros((seq, 64), F32), -sin, z16, z16, z16], axis=1)
    s2 = jnp.concatenate([jnp.zeros((seq, 64), F32), z16, z16, sin, z16], axis=1)
    return c, s1, s2


def _matmul(a, b, *, mode, group_out, out_dtype, tm, tk, name):
    ga, gb = a.shape[0], b.shape[0]
    g_n = max(ga, gb)
    if mode == "tn":
        k_dim, m_dim = a.shape[1:]
    else:
        m_dim, k_dim = a.shape[1:]
    n_dim = b.shape[1] if mode == "nt" else b.shape[2]
    assert m_dim % tm == 0 and k_dim % tk == 0
    kt = k_dim // tk
    n_red = kt if group_out else g_n * kt
    g_out = g_n if group_out else 1

    def split(g, r):
        return (g, r) if group_out else (r // kt, r % kt)

    def a_map(g, i, r):
        gg, kk = split(g, r)
        gg = gg if ga > 1 else 0
        return (gg, kk, i) if mode == "tn" else (gg, i, kk)

    def b_map(g, i, r):
        gg, kk = split(g, r)
        gg = gg if gb > 1 else 0
        return (gg, 0, kk) if mode == "nt" else (gg, kk, 0)

    a_blk = (None, tk, tm) if mode == "tn" else (None, tm, tk)
    b_blk = (None, n_dim, tk) if mode == "nt" else (None, tk, n_dim)
    dn = _DN[mode]

    def body(a_ref, b_ref, o_ref, *scratch):
        part = lax.dot_general(a_ref[...].astype(BF16), b_ref[...].astype(BF16), dn, preferred_element_type=F32)
        if n_red == 1:
            o_ref[...] = part.astype(o_ref.dtype)
            return
        acc_ref, = scratch
        r = pl.program_id(2)

        @pl.when(r == 0)
        def _():
            acc_ref[...] = part

        @pl.when(r > 0)
        def _():
            acc_ref[...] += part

        @pl.when(r == n_red - 1)
        def _():
            o_ref[...] = acc_ref[...].astype(o_ref.dtype)

    return pl.pallas_call(
        body, name=name, grid=(g_out, m_dim // tm, n_red),
        in_specs=[pl.BlockSpec(a_blk, a_map), pl.BlockSpec(b_blk, b_map)],
        out_specs=pl.BlockSpec((None, tm, n_dim), lambda g, i, r: (g, i, 0)),
        out_shape=jax.ShapeDtypeStruct((g_out, m_dim, n_dim), out_dtype),
        scratch_shapes=[] if n_red == 1 else [pltpu.VMEM((tm, n_dim), F32)],
        compiler_params=_cparams(("parallel", "parallel", "arbitrary")),
    )(a, b)


def _row_spec(ts, d):
    return pl.BlockSpec((None, ts, d), lambda b, s: (b, s, 0))


def _mod_spec(d):
    return pl.BlockSpec((None, N_MOD, d), lambda b, s: (b, 0, 0))


def _vec_spec(d):
    return pl.BlockSpec((1, d), lambda b, s: (0, 0))


def _bvec_spec(d):
    return pl.BlockSpec((None, 1, d), lambda b, s: (b, 0, 0))


def _modulate(x, mod, sh_row, sc_row, name, ts=512):
    bsz, seq, d = x.shape

    def body(x_ref, mod_ref, o_ref):
        sh = mod_ref[sh_row:sh_row + 1, :]
        sc = mod_ref[sc_row:sc_row + 1, :]
        o_ref[...] = (x_ref[...] * (1.0 + sc) + sh).astype(o_ref.dtype)

    return pl.pallas_call(
        body, name=name, grid=(bsz, seq // ts),
        in_specs=[_row_spec(ts, d), _mod_spec(d)], out_specs=_row_spec(ts, d),
        out_shape=jax.ShapeDtypeStruct((bsz, seq, d), BF16),
        compiler_params=_cparams(("parallel", "parallel")),
    )(x, mod)


def _modulate_bwd(dh, x, mod, dx_res, sc_row, name, ts=512):
    bsz, seq, d = x.shape

    def body(dh_ref, x_ref, mod_ref, dxr_ref, dx_ref, dsh_ref, dsc_ref):
        s = pl.program_id(1)
        sc = mod_ref[sc_row:sc_row + 1, :]
        dh_v = dh_ref[...]
        dx_ref[...] = dxr_ref[...] + dh_v * (1.0 + sc)
        psh = jnp.sum(dh_v, axis=0, keepdims=True)
        psc = jnp.sum(dh_v * x_ref[...], axis=0, keepdims=True)

        @pl.when(s == 0)
        def _():
            dsh_ref[...] = psh
            dsc_ref[...] = psc

        @pl.when(s > 0)
        def _():
            dsh_ref[...] += psh
            dsc_ref[...] += psc

    return pl.pallas_call(
        body, name=name, grid=(bsz, seq // ts),
        in_specs=[_row_spec(ts, d), _row_spec(ts, d), _mod_spec(d), _row_spec(ts, d)],
        out_specs=[_row_spec(ts, d), _bvec_spec(d), _bvec_spec(d)],
        out_shape=[jax.ShapeDtypeStruct((bsz, seq, d), F32), jax.ShapeDtypeStruct((bsz, 1, d), F32),
                   jax.ShapeDtypeStruct((bsz, 1, d), F32)],
        compiler_params=_cparams(("parallel", "arbitrary")),
    )(dh, x, mod, dx_res)


def _res_ln_fn(x, f, g, lng, lnb, cmul):
    r = ALPHA * x + (cmul * (1.0 + g)) * f
    mu = jnp.mean(r, axis=-1, keepdims=True)
    rc = r - mu
    var = jnp.mean(rc * rc, axis=-1, keepdims=True)
    return rc * lax.rsqrt(var + LN_EPS) * lng + lnb


def _res_ln(x, f, mod, lng, lnb, g_row, cmul, name, ts=512):
    bsz, seq, d = x.shape

    def body(x_ref, f_ref, mod_ref, lng_ref, lnb_ref, o_ref):
        g = mod_ref[g_row:g_row + 1, :]
        o_ref[...] = _res_ln_fn(x_ref[...], f_ref[...], g, lng_ref[...], lnb_ref[...], cmul)

    return pl.pallas_call(
        body, name=name, grid=(bsz, seq // ts),
        in_specs=[_row_spec(ts, d), _row_spec(ts, d), _mod_spec(d), _vec_spec(d), _vec_spec(d)],
        out_specs=_row_spec(ts, d), out_shape=jax.ShapeDtypeStruct((bsz, seq, d), F32),
        compiler_params=_cparams(("parallel", "parallel")),
    )(x, f, mod, lng, lnb)


def _res_ln_bwd(dy, x, f, mod, lng, lnb, g_row, cmul, name, ts=256):
    bsz, seq, d = x.shape

    def body(dy_ref, x_ref, f_ref, mod_ref, lng_ref, lnb_ref, dx_ref, df_ref, dg_ref, dlg_ref, dlb_ref):
        b, s = pl.program_id(0), pl.program_id(1)
        g = mod_ref[g_row:g_row + 1, :]
        _, vjp = jax.vjp(functools.partial(_res_ln_fn, cmul=cmul), x_ref[...], f_ref[...], g, lng_ref[...],
                         lnb_ref[...])
        dx, df, dg, dlg, dlb = vjp(dy_ref[...])
        dx_ref[...] = dx
        df_ref[...] = df.astype(df_ref.dtype)

        @pl.when(s == 0)
        def _():
            dg_ref[...] = dg

        @pl.when(s > 0)
        def _():
            dg_ref[...] += dg

        first = jnp.logical_and(b == 0, s == 0)

        @pl.when(first)
        def _():
            dlg_ref[...] = dlg
            dlb_ref[...] = dlb

        @pl.when(jnp.logical_not(first))
        def _():
            dlg_ref[...] += dlg
            dlb_ref[...] += dlb

    return pl.pallas_call(
        body, name=name, grid=(bsz, seq // ts),
        in_specs=[_row_spec(ts, d), _row_spec(ts, d), _row_spec(ts, d), _mod_spec(d), _vec_spec(d), _vec_spec(d)],
        out_specs=[_row_spec(ts, d), _row_spec(ts, d), _bvec_spec(d), _vec_spec(d), _vec_spec(d)],
        out_shape=[jax.ShapeDtypeStruct((bsz, seq, d), F32), jax.ShapeDtypeStruct((bsz, seq, d), BF16),
                   jax.ShapeDtypeStruct((bsz, 1, d), F32), jax.ShapeDtypeStruct((1, d), F32),
                   jax.ShapeDtypeStruct((1, d), F32)],
        compiler_params=_cparams(("arbitrary", "arbitrary")),
    )(dy, x, f, mod, lng, lnb)


def _loss_head(y, target, name, ts=512):
    bsz, seq, d = y.shape
    n_s = seq // ts

    def body(y_ref, t_ref, dy_ref, loss_ref, acc_ref):
        b, s = pl.program_id(0), pl.program_id(1)
        err = y_ref[...] - t_ref[...]
        dy_ref[...] = err * (1.0 / d)
        part = jnp.sum(err * err, axis=0, keepdims=True)
        first = jnp.logical_and(b == 0, s == 0)

        @pl.when(first)
        def _():
            acc_ref[...] = part

        @pl.when(jnp.logical_not(first))
        def _():
            acc_ref[...] += part

        @pl.when(jnp.logical_and(b == bsz - 1, s == n_s - 1))
        def _():
            loss_ref[...] = jnp.sum(acc_ref[...], axis=1, keepdims=True) * (0.5 / d)

    return pl.pallas_call(
        body, name=name, grid=(bsz, n_s),
        in_specs=[_row_spec(ts, d), _row_spec(ts, d)],
        out_specs=[_row_spec(ts, d), pl.BlockSpec((1, 1), lambda b, s: (0, 0))],
        out_shape=[jax.ShapeDtypeStruct((bsz, seq, d), F32), jax.ShapeDtypeStruct((1, 1), F32)],
        scratch_shapes=[pltpu.VMEM((1, d), F32)],
        compiler_params=_cparams(("arbitrary", "arbitrary")),
    )(y, target)


def _ffn_in_swiglu(h, w_in, name, tm=1024):
    t, d = h.shape
    n_sh, _, w = w_in.shape
    half = n_sh // 2

    def body(h_ref, w_ref, z_ref, a_ref):
        hv = h_ref[...]
        g = jnp.dot(hv, w_ref[0], preferred_element_type=F32)
        u = jnp.dot(hv, w_ref[1], preferred_element_type=F32)
        z_ref[0] = g.astype(z_ref.dtype)
        z_ref[1] = u.astype(z_ref.dtype)
        a_ref[...] = (g * jax.nn.sigmoid(g) * u).astype(a_ref.dtype)

    return pl.pallas_call(
        body, name=name, grid=(half, t // tm),
        in_specs=[pl.BlockSpec((tm, d), lambda g, i: (i, 0)),
                  pl.BlockSpec((2, None, d, w), lambda g, i: (0, g, 0, 0))],
        out_specs=[pl.BlockSpec((2, None, tm, w), lambda g, i: (0, g, i, 0)),
                   pl.BlockSpec((None, tm, w), lambda g, i: (g, i, 0))],
        out_shape=[jax.ShapeDtypeStruct((2, half, t, w), BF16), jax.ShapeDtypeStruct((half, t, w), BF16)],
        compiler_params=_cparams(("parallel", "parallel")),
    )(h, w_in.reshape(2, half, d, w))


def _ffn_out_dx_swiglu(df, w_out, z, name, tm=1024):
    t, d = df.shape
    half, w, _ = w_out.shape

    def body(df_ref, w_ref, z_ref, dz_ref):
        da = lax.dot_general(df_ref[...], w_ref[...], _DN["nt"], preferred_element_type=F32)
        g = z_ref[0].astype(F32)
        u = z_ref[1].astype(F32)
        sig = jax.nn.sigmoid(g)
        dz_ref[0] = (da * u * (sig * (1.0 + g * (1.0 - sig)))).astype(dz_ref.dtype)
        dz_ref[1] = (da * (g * sig)).astype(dz_ref.dtype)

    zspec = pl.BlockSpec((2, None, tm, w), lambda g, i: (0, g, i, 0))
    return pl.pallas_call(
        body, name=name, grid=(half, t // tm),
        in_specs=[pl.BlockSpec((tm, d), lambda g, i: (i, 0)), pl.BlockSpec((None, w, d), lambda g, i: (g, 0, 0)),
                  zspec],
        out_specs=zspec, out_shape=jax.ShapeDtypeStruct(z.shape, BF16),
        compiler_params=_cparams(("parallel", "parallel")),
    )(df, w_out, z)


def _log_sigmoid(x):
    return jnp.minimum(x, 0.0) - jnp.log(1.0 + jnp.exp(-jnp.abs(x)))


def _hgrn_consts():
    r = lax.broadcasted_iota(jnp.int32, (GROUP_WIDTH, GROUP_WIDTH), 0)
    c = lax.broadcasted_iota(jnp.int32, (GROUP_WIDTH, GROUP_WIDTH), 1)
    bd = (r // HEAD_DIM == c // HEAD_DIM).astype(F32)
    r16 = lax.broadcasted_iota(jnp.int32, (A_CHUNK, A_CHUNK), 0)
    c16 = lax.broadcasted_iota(jnp.int32, (A_CHUNK, A_CHUNK), 1)
    tril = (r16 >= c16).astype(F32)
    rows = lax.broadcasted_iota(jnp.int32, (A_CHUNK, GROUP_WIDTH), 0)
    return bd, tril, rows


def _hgrn_lb(logits8, layer):
    rows = lax.broadcasted_iota(jnp.int32, logits8.shape, 0)
    valid = rows < DEPTH
    mx = jnp.max(jnp.where(valid, logits8, NEG), axis=0, keepdims=True)
    e = jnp.where(valid, jnp.exp(logits8 - mx), 0.0)
    sm = e / jnp.sum(e, axis=0, keepdims=True)
    pick = jnp.logical_and(rows >= 1, rows <= layer)
    return jnp.sum(jnp.where(pick, sm, 0.0), axis=0, keepdims=True)


def _hgrn_chunk(aq, af, ai, ag, logits8, norm_g, st, *, layer, consts):
    bd, tril, rows = consts
    lb = _hgrn_lb(logits8, layer)
    la = jnp.log(jnp.maximum(lb, LB_FLOOR))
    b2 = jnp.log(1.0 - lb) + _log_sigmoid(af)
    log_f = jnp.maximum(la, b2) + jnp.log(1.0 + jnp.exp(-jnp.abs(la - b2)))
    k = 1.0 - jnp.exp(log_f)
    qf = aq * jax.nn.sigmoid(aq)
    g_cum = jnp.dot(tril, log_f, precision=HI, preferred_element_type=F32)

    def row(v, s):
        return jnp.sum(jnp.where(rows == s, v, 0.0), axis=0, keepdims=True)

    parts = []
    v_rows = []
    for s in range(A_CHUNK):
        rel = jnp.where(rows >= s, g_cum - row(g_cum, s), NEG)
        parts.append(qf * (row(k, s) * jnp.exp(rel)))
        v_rows.append(row(ai, s))
    a_all = _bdot(jnp.concatenate(parts, axis=0), bd, "nn")
    o = jnp.zeros_like(aq)
    for s in range(A_CHUNK):
        o = o + a_all[s * A_CHUNK:(s + 1) * A_CHUNK, :] * v_rows[s]
    q_dec = qf * jnp.exp(g_cum)
    o = o + _bdot(q_dec, st, "nt")
    g_last = row(g_cum, A_CHUNK - 1)
    k_end = k * jnp.exp(g_last - g_cum)
    kv = _bdot(ai, k_end, "tn")
    st_new = st * jnp.exp(g_last) + kv * bd
    ms = _bdot(o * o, bd, "nn") * (1.0 / HEAD_DIM)
    o = o * lax.rsqrt(ms + RMS_EPS) * norm_g
    return o * (ag * jax.nn.sigmoid(ag)), st_new


def _hgrn_fwd(proj, logits8, norm_g, layer, name, ts=128):
    bsz, seq, _ = proj.shape
    n_ch = ts // A_CHUNK

    def body(p_ref, lg_ref, ng_ref, o_ref, st_ref, st_scr):
        @pl.when(pl.program_id(1) == 0)
        def _():
            st_scr[...] = jnp.zeros_like(st_scr)

        consts = _hgrn_consts()
        logits_v, ng_v = lg_ref[...], ng_ref[...]

        def chunk(ci, carry):
            r = pl.multiple_of(ci * A_CHUNK, A_CHUNK)
            st = st_scr[...]
            st_ref[ci] = st
            o, st_new = _hgrn_chunk(
                p_ref[pl.ds(r, A_CHUNK), 0:256], p_ref[pl.ds(r, A_CHUNK), 256:512],
                p_ref[pl.ds(r, A_CHUNK), 512:768], p_ref[pl.ds(r, A_CHUNK), 768:1024],
                logits_v, ng_v, st, layer=layer, consts=consts)
            o_ref[pl.ds(r, A_CHUNK), :] = o.astype(o_ref.dtype)
            st_scr[...] = st_new
            return carry

        lax.fori_loop(0, n_ch, chunk, 0, unroll=2)

    return pl.pallas_call(
        body, name=name, grid=(bsz, seq // ts),
        in_specs=[pl.BlockSpec((None, ts, 1024), lambda b, s: (b, s, 0)),
                  pl.BlockSpec((8, GROUP_WIDTH), lambda b, s: (0, 0)),
                  pl.BlockSpec((1, GROUP_WIDTH), lambda b, s: (0, 0))],
        out_specs=[pl.BlockSpec((None, ts, GROUP_WIDTH), lambda b, s: (b, s, 0)),
                   pl.BlockSpec((None, n_ch, GROUP_WIDTH, GROUP_WIDTH), lambda b, s: (b, s, 0, 0))],
        out_shape=[jax.ShapeDtypeStruct((bsz, seq, MO_W), BF16),
                   jax.ShapeDtypeStruct((bsz, seq // A_CHUNK, GROUP_WIDTH, GROUP_WIDTH), F32)],
        scratch_shapes=[pltpu.VMEM((GROUP_WIDTH, GROUP_WIDTH), F32)],
        compiler_params=_cparams(("parallel", "arbitrary")),
    )(proj, logits8, norm_g)


def _hgrn_bwd(dmo, proj, states, logits8, norm_g, layer, name, ts=128):
    bsz, seq, _ = proj.shape
    n_ch = ts // A_CHUNK
    n_s = seq // ts

    def body(do_ref, p_ref, st_ref, lg_ref, ng_ref, dp_ref, dlg_ref, dng_ref, dst_scr):
        b, s = pl.program_id(0), pl.program_id(1)

        @pl.when(s == 0)
        def _():
            dst_scr[...] = jnp.zeros_like(dst_scr)

        @pl.when(jnp.logical_and(b == 0, s == 0))
        def _():
            dlg_ref[...] = jnp.zeros_like(dlg_ref)
            dng_ref[...] = jnp.zeros_like(dng_ref)

        consts = _hgrn_consts()
        logits_v, ng_v = lg_ref[...], ng_ref[...]
        fn = functools.partial(_hgrn_chunk, layer=layer, consts=consts)

        def chunk(t, carry):
            ci = n_ch - 1 - t
            r = pl.multiple_of(ci * A_CHUNK, A_CHUNK)
            _, vjp = jax.vjp(
                fn, p_ref[pl.ds(r, A_CHUNK), 0:256], p_ref[pl.ds(r, A_CHUNK), 256:512],
                p_ref[pl.ds(r, A_CHUNK), 512:768], p_ref[pl.ds(r, A_CHUNK), 768:1024],
                logits_v, ng_v, st_ref[ci])
            daq, daf, dai, dag, dlg, dng, dst = vjp((do_ref[pl.ds(r, A_CHUNK), :], dst_scr[...]))
            dp_ref[pl.ds(r, A_CHUNK), 0:256] = daq.astype(dp_ref.dtype)
            dp_ref[pl.ds(r, A_CHUNK), 256:512] = daf.astype(dp_ref.dtype)
            dp_ref[pl.ds(r, A_CHUNK), 512:768] = dai.astype(dp_ref.dtype)
            dp_ref[pl.ds(r, A_CHUNK), 768:1024] = dag.astype(dp_ref.dtype)
            dlg_ref[...] += dlg
            dng_ref[...] += dng
            dst_scr[...] = dst
            return carry

        lax.fori_loop(0, n_ch, chunk, 0, unroll=2)

    rev = lambda b, s: (b, n_s - 1 - s, 0)
    return pl.pallas_call(
        body, name=name, grid=(bsz, n_s),
        in_specs=[pl.BlockSpec((None, ts, GROUP_WIDTH), rev),
                  pl.BlockSpec((None, ts, 1024), rev),
                  pl.BlockSpec((None, n_ch, GROUP_WIDTH, GROUP_WIDTH), lambda b, s: (b, n_s - 1 - s, 0, 0)),
                  pl.BlockSpec((8, GROUP_WIDTH), lambda b, s: (0, 0)),
                  pl.BlockSpec((1, GROUP_WIDTH), lambda b, s: (0, 0))],
        out_specs=[pl.BlockSpec((None, ts, 1024), rev),
                   pl.BlockSpec((8, GROUP_WIDTH), lambda b, s: (0, 0)),
                   pl.BlockSpec((1, GROUP_WIDTH), lambda b, s: (0, 0))],
        out_shape=[jax.ShapeDtypeStruct((bsz, seq, PACK_W), BF16),
                   jax.ShapeDtypeStruct((8, GROUP_WIDTH), F32), jax.ShapeDtypeStruct((1, GROUP_WIDTH), F32)],
        scratch_shapes=[pltpu.VMEM((GROUP_WIDTH, GROUP_WIDTH), F32)],
        compiler_params=_cparams(("arbitrary", "arbitrary")),
    )(dmo, proj, states, logits8, norm_g)


def _rms_fn(x, g):
    return x * lax.rsqrt(jnp.mean(x * x, axis=-1, keepdims=True) + RMS_EPS) * g


def _tile4(t):
    return jnp.concatenate([t, t, t, t], axis=1)


def _rope(x, c, s1, s2):
    w = x.shape[-1]
    return x * c + pltpu.roll(x, 32, axis=1) * s2 + pltpu.roll(x, w - 32, axis=1) * s1


def _rope_t(dy, c, s1, s2):
    w = dy.shape[-1]
    return dy * c + pltpu.roll(dy * s2, w - 32, axis=1) + pltpu.roll(dy * s1, 32, axis=1)


def _mla_pre(proj, qg, kvg, wq, wkv, tabs, name, ts=256):
    bsz, seq, _ = proj.shape

    def body(p_ref, qg_ref, kvg_ref, wq_ref, wkv_ref, c_ref, s1_ref, s2_ref, q_ref, kv_ref):
        nq = _rms_fn(p_ref[:, 0:256], qg_ref[...])
        nkv = _rms_fn(p_ref[:, 256:384], kvg_ref[...])
        c, s1, s2 = c_ref[...], s1_ref[...], s2_ref[...]
        qp = jnp.dot(nq.astype(BF16), wq_ref[...], preferred_element_type=F32)
        q_ref[...] = _rope(qp, _tile4(c), _tile4(s1), _tile4(s2)).astype(q_ref.dtype)
        kv = jnp.dot(nkv.astype(BF16), wkv_ref[...], preferred_element_type=F32)
        krr = _rope(p_ref[:, 384:512], c, s1, s2)
        zero = jnp.zeros_like(krr)
        kv_ref[...] = (kv + jnp.concatenate([krr, zero] * N_HEADS, axis=1)).astype(kv_ref.dtype)

    tab_spec = pl.BlockSpec((ts, LANES), lambda b, s: (s, 0))
    return pl.pallas_call(
        body, name=name, grid=(bsz, seq // ts),
        in_specs=[pl.BlockSpec((None, ts, 512), lambda b, s: (b, s, P_B // 512)),
                  _vec_spec(256), _vec_spec(128),
                  pl.BlockSpec((256, 512), lambda b, s: (0, 0)), pl.BlockSpec((128, 1024), lambda b, s: (0, 0)),
                  tab_spec, tab_spec, tab_spec],
        out_specs=[_row_spec(ts, 512), _row_spec(ts, 1024)],
        out_shape=[jax.ShapeDtypeStruct((bsz, seq, 512), BF16), jax.ShapeDtypeStruct((bsz, seq, 1024), BF16)],
        compiler_params=_cparams(("parallel", "parallel")),
    )(proj, qg, kvg, wq, wkv, *tabs)


def _mla_pre_bwd(dq, dkv, dproj, proj, qg, kvg, wq, wkv, tabs, name, ts=256):
    bsz, seq, _ = proj.shape

    def body(dq_ref, dkv_ref, dp_any, p_ref, qg_ref, kvg_ref, wq_ref, wkv_ref, c_ref, s1_ref, s2_ref,
             dp_ref, dqg_ref, dkvg_ref, dwq_ref, dwkv_ref):
        del dp_any
        first = jnp.logical_and(pl.program_id(0) == 0, pl.program_id(1) == 0)

        @pl.when(first)
        def _():
            dqg_ref[...] = jnp.zeros_like(dqg_ref)
            dkvg_ref[...] = jnp.zeros_like(dkvg_ref)
            dwq_ref[...] = jnp.zeros_like(dwq_ref)
            dwkv_ref[...] = jnp.zeros_like(dwkv_ref)

        c, s1, s2 = c_ref[...], s1_ref[...], s2_ref[...]
        nq, vjp_q = jax.vjp(_rms_fn, p_ref[:, 0:256], qg_ref[...])
        nkv, vjp_kv = jax.vjp(_rms_fn, p_ref[:, 256:384], kvg_ref[...])
        dqp = _rope_t(dq_ref[...], _tile4(c), _tile4(s1), _tile4(s2)).astype(BF16)
        dkv_v = dkv_ref[...]
        dkv_b = dkv_v.astype(BF16)
        tn = (((0,), (0,)), ((), ()))
        nt = (((1,), (1,)), ((), ()))
        dwq_ref[...] += lax.dot_general(nq.astype(BF16), dqp, tn, preferred_element_type=F32)
        dwkv_ref[...] += lax.dot_general(nkv.astype(BF16), dkv_b, tn, preferred_element_type=F32)
        dcq, dqg = vjp_q(lax.dot_general(dqp, wq_ref[...], nt, preferred_element_type=F32))
        dckv, dkvg = vjp_kv(lax.dot_general(dkv_b, wkv_ref[...], nt, preferred_element_type=F32))
        dqg_ref[...] += dqg
        dkvg_ref[...] += dkvg
        dk_sum = dkv_v[:, 0:128] + dkv_v[:, 256:384] + dkv_v[:, 512:640] + dkv_v[:, 768:896]
        lane = lax.broadcasted_iota(jnp.int32, dk_sum.shape, 1)
        dkr = jnp.where(lane >= 64, _rope_t(dk_sum, c, s1, s2), 0.0)
        dp_ref[:, 0:256] = dcq.astype(dp_ref.dtype)
        dp_ref[:, 256:384] = dckv.astype(dp_ref.dtype)
        dp_ref[:, 384:512] = dkr.astype(dp_ref.dtype)

    tab_spec = pl.BlockSpec((ts, LANES), lambda b, s: (s, 0))
    const = lambda shape: pl.BlockSpec(shape, lambda b, s: (0, 0))
    return pl.pallas_call(
        body, name=name, grid=(bsz, seq // ts),
        in_specs=[_row_spec(ts, 512), _row_spec(ts, 1024), pl.BlockSpec(memory_space=pl.ANY),
                  pl.BlockSpec((None, ts, 512), lambda b, s: (b, s, P_B // 512)),
                  _vec_spec(256), _vec_spec(128), const((256, 512)), const((128, 1024)),
                  tab_spec, tab_spec, tab_spec],
        out_specs=[pl.BlockSpec((None, ts, 512), lambda b, s: (b, s, P_B // 512)),
                   _vec_spec(256), _vec_spec(128), const((256, 512)), const((128, 1024))],
        out_shape=[jax.ShapeDtypeStruct(dproj.shape, dproj.dtype), jax.ShapeDtypeStruct((1, 256), F32),
                   jax.ShapeDtypeStruct((1, 128), F32), jax.ShapeDtypeStruct((256, 512), F32),
                   jax.ShapeDtypeStruct((128, 1024), F32)],
        input_output_aliases={2: 0},
        compiler_params=_cparams(("arbitrary", "arbitrary")),
    )(dq, dkv, dproj, proj, qg, kvg, wq, wkv, *tabs)


def _fox_gate(proj, bf, name):
    bsz, seq, _ = proj.shape
    n_blk = seq // LANES

    def body(x_ref, bf_ref, f_ref):
        r_i = lax.broadcasted_iota(jnp.int32, (LANES, LANES), 0)
        c_i = lax.broadcasted_iota(jnp.int32, (LANES, LANES), 1)
        tril = (r_i >= c_i).astype(F32)
        bias = bf_ref[...]

        def blk(i, carry):
            r = pl.multiple_of(i * LANES, LANES)
            lf = _log_sigmoid(x_ref[pl.ds(r, LANES), :] + bias)
            f_ref[pl.ds(r, LANES), :] = jnp.dot(tril, lf, precision=HI, preferred_element_type=F32) + carry
            return carry + jnp.sum(lf, axis=0, keepdims=True)

        lax.fori_loop(0, n_blk, blk, jnp.zeros((1, LANES), F32))

    return pl.pallas_call(
        body, name=name, grid=(bsz,),
        in_specs=[pl.BlockSpec((None, seq, LANES), lambda b: (b, 0, P_CF // LANES)),
                  pl.BlockSpec((1, LANES), lambda b: (0, 0))],
        out_specs=pl.BlockSpec((None, seq, LANES), lambda b: (b, 0, 0)),
        out_shape=jax.ShapeDtypeStruct((bsz, seq, LANES), F32),
        compiler_params=_cparams(("parallel",)),
    )(proj, bf)


def _fox_gate_bwd(dcum, dproj, proj, bf, name):
    bsz, seq, _ = proj.shape
    n_blk = seq // LANES

    def body(dc_ref, dp_any, x_ref, bf_ref, dp_ref, dbf_ref):
        del dp_any

        @pl.when(pl.program_id(0) == 0)
        def _():
            dbf_ref[...] = jnp.zeros_like(dbf_ref)

        r_i = lax.broadcasted_iota(jnp.int32, (LANES, LANES), 0)
        c_i = lax.broadcasted_iota(jnp.int32, (LANES, LANES), 1)
        triu = (r_i <= c_i).astype(F32)
        bias = bf_ref[...]

        def blk(t, carry):
            tail, dbf = carry
            r = pl.multiple_of((n_blk - 1 - t) * LANES, LANES)
            dc = dc_ref[pl.ds(r, LANES), :]
            dlf = jnp.dot(triu, dc, precision=HI, preferred_element_type=F32) + tail
            dx = dlf * (1.0 - jax.nn.sigmoid(x_ref[pl.ds(r, LANES), :] + bias))
            dp_ref[pl.ds(r, LANES), :] = dx.astype(dp_ref.dtype)
            return tail + jnp.sum(dc, axis=0, keepdims=True), dbf + jnp.sum(dx, axis=0, keepdims=True)

        z = jnp.zeros((1, LANES), F32)
        _, dbf = lax.fori_loop(0, n_blk, blk, (z, z))
        dbf_ref[...] += dbf

    return pl.pallas_call(
        body, name=name, grid=(bsz,),
        in_specs=[pl.BlockSpec((None, seq, LANES), lambda b: (b, 0, 0)), pl.BlockSpec(memory_space=pl.ANY),
                  pl.BlockSpec((None, seq, LANES), lambda b: (b, 0, P_CF // LANES)),
                  pl.BlockSpec((1, LANES), lambda b: (0, 0))],
        out_specs=[pl.BlockSpec((None, seq, LANES), lambda b: (b, 0, P_CF // LANES)),
                   pl.BlockSpec((1, LANES), lambda b: (0, 0))],
        out_shape=[jax.ShapeDtypeStruct(dproj.shape, dproj.dtype), jax.ShapeDtypeStruct((1, LANES), F32)],
        input_output_aliases={1: 0},
        compiler_params=_cparams(("arbitrary",)),
    )(dcum, dproj, proj, bf)


def _gate_terms(fc_ref, fr_ref, h, tq, tk):
    lane = lax.broadcasted_iota(jnp.int32, (tq, LANES), 1)
    fcol = jnp.sum(jnp.where(lane == h, fc_ref[...], 0.0), axis=1, keepdims=True)
    sub = lax.broadcasted_iota(jnp.int32, (8, tk), 0)
    frow = jnp.sum(jnp.where(sub == h, fr_ref[...], 0.0), axis=0, keepdims=True)
    return fcol - frow


def _scores(q_ref, k_ref, gate_refs, scale, h, diag, tq, tk):
    s = lax.dot_general(q_ref[...].astype(BF16), k_ref[...].astype(BF16), (((1,), (1,)), ((), ())),
                        preferred_element_type=F32) * scale
    if gate_refs is not None:
        s = s + _gate_terms(gate_refs[0], gate_refs[1], h, tq, tk)
    r_i = lax.broadcasted_iota(jnp.int32, (tq, tk), 0)
    c_i = lax.broadcasted_iota(jnp.int32, (tq, tk), 1)
    return jnp.where(jnp.logical_or(jnp.logical_not(diag), c_i <= r_i), s, NEG)


def _attn_fwd(qa, q0, kva, kv0, mo, o0, gates, scale, name, tq=None):
    bsz, seq, _ = qa.shape
    tq = ATTN_TILE if tq is None else tq
    n_q = seq // tq
    gated = gates is not None

    def body(*refs):
        q_ref, k_ref, v_ref = refs[:3]
        gate_refs = refs[3:5] if gated else None
        o_ref, lse_ref, m_s, l_s, acc_s = refs[-5:]
        h, i, j = pl.program_id(1), pl.program_id(2), pl.program_id(3)

        @pl.when(j == 0)
        def _():
            m_s[...] = jnp.full_like(m_s, NEG)
            l_s[...] = jnp.zeros_like(l_s)
            acc_s[...] = jnp.zeros_like(acc_s)

        @pl.when(j <= i)
        def _():
            s = _scores(q_ref, k_ref, gate_refs, scale, h, j == i, tq, tq)
            m_prev = m_s[...]
            m_new = jnp.maximum(m_prev, jnp.max(s, axis=1, keepdims=True))
            alpha = jnp.exp(m_prev - m_new)
            p = jnp.exp(s - m_new)
            l_s[...] = alpha * l_s[...] + jnp.sum(p, axis=1, keepdims=True)
            acc_s[...] = alpha * acc_s[...] + jnp.dot(p.astype(BF16), v_ref[...].astype(BF16),
                                                      preferred_element_type=F32)
            m_s[...] = m_new

        @pl.when(j == i)
        def _():
            o_ref[...] = (acc_s[...] / l_s[...]).astype(o_ref.dtype)
            lse_ref[...] = m_s[...] + jnp.log(l_s[...])

    blk = (None, tq, LANES)
    in_specs = [pl.BlockSpec(blk, lambda b, h, i, j: (b, i, q0 + h)),
                pl.BlockSpec(blk, lambda b, h, i, j: (b, jnp.minimum(j, i), kv0 + 2 * h)),
                pl.BlockSpec(blk, lambda b, h, i, j: (b, jnp.minimum(j, i), kv0 + 2 * h + 1))]
    args = [qa, kva, kva]
    if gated:
        in_specs += [pl.BlockSpec(blk, lambda b, h, i, j: (b, i, 0)),
                     pl.BlockSpec((None, 8, tq), lambda b, h, i, j: (b, 0, jnp.minimum(j, i)))]
        args += list(gates)
    in_specs.append(pl.BlockSpec(memory_space=pl.ANY))
    args.append(mo)
    return pl.pallas_call(
        body, name=name, grid=(bsz, N_HEADS, n_q, n_q), in_specs=in_specs,
        out_specs=[pl.BlockSpec(blk, lambda b, h, i, j: (b, i, o0 + h)),
                   pl.BlockSpec((None, None, tq, 1), lambda b, h, i, j: (b, h, i, 0))],
        out_shape=[jax.ShapeDtypeStruct(mo.shape, mo.dtype), jax.ShapeDtypeStruct((bsz, N_HEADS, seq, 1), F32)],
        scratch_shapes=[pltpu.VMEM((tq, 1), F32), pltpu.VMEM((tq, 1), F32), pltpu.VMEM((tq, LANES), F32)],
        input_output_aliases={len(args) - 1: 0},
        compiler_params=_cparams(("parallel", "parallel", "parallel", "arbitrary")),
    )(*args)


def _attn_bwd_q(qa, q0, kva, kv0, mo, dmo, o0, lse, gates, scale, out, out0, name, tq=None):
    bsz, seq, _ = qa.shape
    tq = ATTN_TILE if tq is None else tq
    n_q = seq // tq
    gated = gates is not None
    aliased = not isinstance(out, jax.ShapeDtypeStruct)

    def body(*refs):
        q_ref, k_ref, v_ref, o_ref, do_ref, lse_ref = refs[:6]
        gate_refs = refs[6:8] if gated else None
        dq_ref, delta_ref, dfq_ref, acc_s, dl_s, df_s = refs[-6:]
        h, i, j = pl.program_id(1), pl.program_id(2), pl.program_id(3)

        @pl.when(j == 0)
        def _():
            acc_s[...] = jnp.zeros_like(acc_s)
            df_s[...] = jnp.zeros_like(df_s)
            dl_s[...] = jnp.sum(do_ref[...] * o_ref[...].astype(F32), axis=1, keepdims=True)

        @pl.when(j <= i)
        def _():
            s = _scores(q_ref, k_ref, gate_refs, scale, h, j == i, tq, tq)
            p = jnp.exp(s - lse_ref[...])
            dp = lax.dot_general(do_ref[...].astype(BF16), v_ref[...].astype(BF16), (((1,), (1,)), ((), ())),
                                 preferred_element_type=F32)
            ds = p * (dp - dl_s[...])
            acc_s[...] += jnp.dot(ds.astype(BF16), k_ref[...].astype(BF16), preferred_element_type=F32)
            df_s[...] += jnp.sum(ds, axis=1, keepdims=True)

        @pl.when(j == i)
        def _():
            dq_ref[...] = (acc_s[...] * scale).astype(dq_ref.dtype)
            delta_ref[...] = dl_s[...]
            dfq_ref[...] = df_s[...]

    blk = (None, tq, LANES)
    col = pl.BlockSpec((None, None, tq, 1), lambda b, h, i, j: (b, h, i, 0))
    in_specs = [pl.BlockSpec(blk, lambda b, h, i, j: (b, i, q0 + h)),
                pl.BlockSpec(blk, lambda b, h, i, j: (b, jnp.minimum(j, i), kv0 + 2 * h)),
                pl.BlockSpec(blk, lambda b, h, i, j: (b, jnp.minimum(j, i), kv0 + 2 * h + 1)),
                pl.BlockSpec(blk, lambda b, h, i, j: (b, i, o0 + h)),
                pl.BlockSpec(blk, lambda b, h, i, j: (b, i, o0 + h)), col]
    args = [qa, kva, kva, mo, dmo, lse]
    if gated:
        in_specs += [pl.BlockSpec(blk, lambda b, h, i, j: (b, i, 0)),
                     pl.BlockSpec((None, 8, tq), lambda b, h, i, j: (b, 0, jnp.minimum(j, i)))]
        args += list(gates)
    aliases = {}
    if aliased:
        in_specs.append(pl.BlockSpec(memory_space=pl.ANY))
        args.append(out)
        aliases = {len(args) - 1: 0}
    vec = jax.ShapeDtypeStruct((bsz, N_HEADS, seq, 1), F32)
    return pl.pallas_call(
        body, name=name, grid=(bsz, N_HEADS, n_q, n_q), in_specs=in_specs,
        out_specs=[pl.BlockSpec(blk, lambda b, h, i, j: (b, i, out0 + h)), col, col],
        out_shape=[jax.ShapeDtypeStruct(out.shape, out.dtype), vec, vec],
        scratch_shapes=[pltpu.VMEM((tq, LANES), F32), pltpu.VMEM((tq, 1), F32), pltpu.VMEM((tq, 1), F32)],
        input_output_aliases=aliases,
        compiler_params=_cparams(("parallel", "parallel", "parallel", "arbitrary")),
    )(*args)


def _attn_bwd_kv(qa, q0, kva, kv0, dmo, o0, lse, delta, gates, scale, out, out0, name, tq=None):
    bsz, seq, _ = qa.shape
    tq = ATTN_TILE if tq is None else tq
    n_q = seq // tq
    gated = gates is not None
    aliased = not isinstance(out, jax.ShapeDtypeStruct)

    def body(*refs):
        q_ref, k_ref, v_ref, do_ref, lse_ref, dl_ref = refs[:6]
        gate_refs = refs[6:8] if gated else None
        dkv_ref, dfk_ref, dk_s, dv_s, df_s = refs[-5:]
        h, j, i = pl.program_id(1), pl.program_id(2), pl.program_id(3)

        @pl.when(i == 0)
        def _():
            dk_s[...] = jnp.zeros_like(dk_s)
            dv_s[...] = jnp.zeros_like(dv_s)
            df_s[...] = jnp.zeros_like(df_s)

        @pl.when(i >= j)
        def _():
            s = _scores(q_ref, k_ref, gate_refs, scale, h, j == i, tq, tq)
            p = jnp.exp(s - lse_ref[...])
            do_b = do_ref[...].astype(BF16)
            dp = lax.dot_general(do_b, v_ref[...].astype(BF16), (((1,), (1,)), ((), ())),
                                 preferred_element_type=F32)
            ds = p * (dp - dl_ref[...])
            tn = (((0,), (0,)), ((), ()))
            dv_s[...] += lax.dot_general(p.astype(BF16), do_b, tn, preferred_element_type=F32)
            dk_s[...] += lax.dot_general(ds.astype(BF16), q_ref[...].astype(BF16), tn, preferred_element_type=F32)
            df_s[...] -= jnp.sum(ds, axis=0, keepdims=True)

        @pl.when(i == n_q - 1)
        def _():
            dkv_ref[:, 0:LANES] = (dk_s[...] * scale).astype(dkv_ref.dtype)
            dkv_ref[:, LANES:2 * LANES] = dv_s[...].astype(dkv_ref.dtype)
            dfk_ref[...] = df_s[...]

    blk = (None, tq, LANES)
    col = pl.BlockSpec((None, None, tq, 1), lambda b, h, j, i: (b, h, jnp.maximum(i, j), 0))
    in_specs = [pl.BlockSpec(blk, lambda b, h, j, i: (b, jnp.maximum(i, j), q0 + h)),
                pl.BlockSpec(blk, lambda b, h, j, i: (b, j, kv0 + 2 * h)),
                pl.BlockSpec(blk, lambda b, h, j, i: (b, j, kv0 + 2 * h + 1)),
                pl.BlockSpec(blk, lambda b, h, j, i: (b, jnp.maximum(i, j), o0 + h)), col, col]
    args = [qa, kva, kva, dmo, lse, delta]
    if gated:
        in_specs += [pl.BlockSpec(blk, lambda b, h, j, i: (b, jnp.maximum(i, j), 0)),
                     pl.BlockSpec((None, 8, tq), lambda b, h, j, i: (b, 0, j))]
        args += list(gates)
    aliases = {}
    if aliased:
        in_specs.append(pl.BlockSpec(memory_space=pl.ANY))
        args.append(out)
        aliases = {len(args) - 1: 0}
    return pl.pallas_call(
        body, name=name, grid=(bsz, N_HEADS, n_q, n_q), in_specs=in_specs,
        out_specs=[pl.BlockSpec((None, tq, 2 * LANES), lambda b, h, j, i: (b, j, out0 + h)),
                   pl.BlockSpec((None, None, 1, tq), lambda b, h, j, i: (b, h, 0, j))],
        out_shape=[jax.ShapeDtypeStruct(out.shape, out.dtype), jax.ShapeDtypeStruct((bsz, N_HEADS, 1, seq), F32)],
        scratch_shapes=[pltpu.VMEM((tq, LANES), F32), pltpu.VMEM((tq, LANES), F32), pltpu.VMEM((1, tq), F32)],
        input_output_aliases=aliases,
        compiler_params=_cparams(("parallel", "parallel", "parallel", "arbitrary")),
    )(*args)


def _gmlp_fn(uv, lng, lnb, ws, bst):
    u = jax.nn.gelu(uv[:, 0:GROUP_WIDTH])
    gv = jax.nn.gelu(uv[:, GROUP_WIDTH:2 * GROUP_WIDTH])
    mu = jnp.mean(gv, axis=-1, keepdims=True)
    vc = gv - mu
    var = jnp.mean(vc * vc, axis=-1, keepdims=True)
    vln = vc * lax.rsqrt(var + LN_EPS) * lng + lnb
    r_i = lax.broadcasted_iota(jnp.int32, (D_CHUNK, D_CHUNK), 0)
    c_i = lax.broadcasted_iota(jnp.int32, (D_CHUNK, D_CHUNK), 1)
    lane_g = lax.broadcasted_iota(jnp.int32, (D_CHUNK, GROUP_WIDTH), 1) // HEAD_DIM
    e_r = lax.broadcasted_iota(jnp.int32, (LANES, GROUP_WIDTH), 0)
    e_c = lax.broadcasted_iota(jnp.int32, (LANES, GROUP_WIDTH), 1)
    expand = (e_r == e_c // HEAD_DIM).astype(F32)
    mixed = jnp.dot(bst, expand, precision=HI, preferred_element_type=F32)
    for g in range(4):
        w = jnp.where(r_i >= c_i, ws[g], 0.0)
        mixed = mixed + jnp.where(lane_g == g, _bdot(w, vln, "nn"), 0.0)
    return u * mixed


def _gmlp_fwd(proj, mo, lng, lnb, ws, bst, name):
    bsz, seq, _ = proj.shape

    def body(p_ref, mo_any, lng_ref, lnb_ref, ws_ref, bst_ref, o_ref):
        del mo_any
        o_ref[...] = _gmlp_fn(p_ref[...], lng_ref[...], lnb_ref[...], ws_ref[...], bst_ref[...]).astype(o_ref.dtype)

    return pl.pallas_call(
        body, name=name, grid=(bsz, seq // D_CHUNK),
        in_specs=[pl.BlockSpec((None, D_CHUNK, 512), lambda b, s: (b, s, P_D // 512)),
                  pl.BlockSpec(memory_space=pl.ANY), _vec_spec(256), _vec_spec(256),
                  pl.BlockSpec((4, D_CHUNK, D_CHUNK), lambda b, s: (0, 0, 0)),
                  pl.BlockSpec((D_CHUNK, LANES), lambda b, s: (0, 0))],
        out_specs=pl.BlockSpec((None, D_CHUNK, GROUP_WIDTH), lambda b, s: (b, s, 1280 // GROUP_WIDTH)),
        out_shape=jax.ShapeDtypeStruct(mo.shape, mo.dtype),
        input_output_aliases={1: 0},
        compiler_params=_cparams(("parallel", "parallel")),
    )(proj, mo, lng, lnb, ws, bst)


def _gmlp_bwd(dmo, dproj, proj, lng, lnb, ws, bst, name):
    bsz, seq, _ = proj.shape

    def body(do_ref, dp_any, p_ref, lng_ref, lnb_ref, ws_ref, bst_ref, dp_ref, dlg_ref, dlb_ref, dws_ref, dbst_ref):
        del dp_any
        first = jnp.logical_and(pl.program_id(0) == 0, pl.program_id(1) == 0)

        @pl.when(first)
        def _():
            dlg_ref[...] = jnp.zeros_like(dlg_ref)
            dlb_ref[...] = jnp.zeros_like(dlb_ref)
            dws_ref[...] = jnp.zeros_like(dws_ref)
            dbst_ref[...] = jnp.zeros_like(dbst_ref)

        _, vjp = jax.vjp(_gmlp_fn, p_ref[...], lng_ref[...], lnb_ref[...], ws_ref[...], bst_ref[...])
        duv, dlg, dlb, dws, dbst = vjp(do_ref[...])
        dp_ref[...] = duv.astype(dp_ref.dtype)
        dlg_ref[...] += dlg
        dlb_ref[...] += dlb
        dws_ref[...] += dws
        dbst_ref[...] += dbst

    const2 = lambda shape: pl.BlockSpec(shape, lambda b, s: (0,) * len(shape))
    return pl.pallas_call(
        body, name=name, grid=(bsz, seq // D_CHUNK),
        in_specs=[pl.BlockSpec((None, D_CHUNK, GROUP_WIDTH), lambda b, s: (b, s, 1280 // GROUP_WIDTH)),
                  pl.BlockSpec(memory_space=pl.ANY),
                  pl.BlockSpec((None, D_CHUNK, 512), lambda b, s: (b, s, P_D // 512)),
                  _vec_spec(256), _vec_spec(256), const2((4, D_CHUNK, D_CHUNK)), const2((D_CHUNK, LANES))],
        out_specs=[pl.BlockSpec((None, D_CHUNK, 512), lambda b, s: (b, s, P_D // 512)),
                   _vec_spec(256), _vec_spec(256), const2((4, D_CHUNK, D_CHUNK)), const2((D_CHUNK, LANES))],
        out_shape=[jax.ShapeDtypeStruct(dproj.shape, dproj.dtype), jax.ShapeDtypeStruct((1, 256), F32),
                   jax.ShapeDtypeStruct((1, 256), F32), jax.ShapeDtypeStruct((4, D_CHUNK, D_CHUNK), F32),
                   jax.ShapeDtypeStruct((D_CHUNK, LANES), F32)],
        input_output_aliases={1: 0},
        compiler_params=_cparams(("arbitrary", "arbitrary")),
    )(dmo, dproj, proj, lng, lnb, ws, bst)


def _ada_fwd(c_all, ada_w, name):
    n_b = c_all.shape[0]
    depth, d, cols = ada_w.shape

    def body(c_ref, w_ref, o_ref):
        cv = c_ref[...]
        act = (cv * jax.nn.sigmoid(cv)).astype(BF16)
        o_ref[...] = jnp.dot(act, w_ref[...].astype(BF16), preferred_element_type=F32)

    return pl.pallas_call(
        body, name=name, grid=(depth,),
        in_specs=[pl.BlockSpec((n_b, d), lambda l: (0, 0)), pl.BlockSpec((None, d, cols), lambda l: (l, 0, 0))],
        out_specs=pl.BlockSpec((None, n_b, cols), lambda l: (l, 0, 0)),
        out_shape=jax.ShapeDtypeStruct((depth, n_b, cols), F32),
        compiler_params=_cparams(("parallel",)),
    )(c_all, ada_w)


def _ada_bwd(c_all, dmod_cols, dmod_full, name):
    n_b, d = c_all.shape
    depth, _, cols = dmod_cols.shape
    full = dmod_full.shape[-1]

    def body(c_ref, dm_ref, df_ref, gw_ref, gb_ref):
        cv = c_ref[...]
        act = (cv * jax.nn.sigmoid(cv)).astype(BF16)
        gw_ref[...] = lax.dot_general(act, dm_ref[...].astype(BF16), (((0,), (0,)), ((), ())),
                                      preferred_element_type=F32)
        gb_ref[...] = jnp.sum(df_ref[...], axis=0, keepdims=True)

    return pl.pallas_call(
        body, name=name, grid=(depth,),
        in_specs=[pl.BlockSpec((n_b, d), lambda l: (0, 0)), pl.BlockSpec((None, n_b, cols), lambda l: (l, 0, 0)),
                  pl.BlockSpec((None, n_b, full), lambda l: (l, 0, 0))],
        out_specs=[pl.BlockSpec((None, d, cols), lambda l: (l, 0, 0)),
                   pl.BlockSpec((None, 1, full), lambda l: (l, 0, 0))],
        out_shape=[jax.ShapeDtypeStruct((depth, d, cols), F32), jax.ShapeDtypeStruct((depth, 1, full), F32)],
        compiler_params=_cparams(("parallel",)),
    )(c_all, dmod_cols, dmod_full)


def _adamw(gparts, own, w, m, v, name, layer=0, prev=None):
    n_p, rows, cols = gparts.shape
    tr = rows
    if rows > 512:
        tr = next(c for c in range(512, 7, -8) if rows % c == 0)
    off = layer * (rows // tr)
    has_own = own is not None
    n_prev = 0 if prev is None else 4

    def body(*refs):
        g_ref = refs[0]
        own_ref = refs[1] if has_own else None
        w_ref, m_ref, v_ref = refs[1 + has_own:4 + has_own]
        go_ref, do_ref, mo_ref, vo_ref = refs[4 + has_own + n_prev:]
        if has_own:
            g = own_ref[...].astype(F32) + g_ref[0].astype(F32)
        else:
            g = g_ref[0].astype(F32)
        for p in range(1, n_p):
            g = g + g_ref[p].astype(F32)
        m_new = ADAM_B1 * m_ref[...] + (1.0 - ADAM_B1) * g
        v_new = ADAM_B2 * v_ref[...] + (1.0 - ADAM_B2) * (g * g)
        m_hat = m_new / (1.0 - ADAM_B1 ** ADAM_STEP)
        v_hat = v_new / (1.0 - ADAM_B2 ** ADAM_STEP)
        go_ref[...] = g
        do_ref[...] = -ADAM_LR * (m_hat / (jnp.sqrt(v_hat) + ADAM_EPS) + ADAM_WD * w_ref[...])
        mo_ref[...] = m_new
        vo_ref[...] = v_new

    spec = pl.BlockSpec((tr, cols), lambda i: (off + i, 0))
    in_specs = [pl.BlockSpec((n_p, tr, cols), lambda i: (0, i, 0))]
    args = [gparts]
    if has_own:
        in_specs.append(pl.BlockSpec((tr, cols), lambda i: (i, 0)))
        args.append(own)
    in_specs += [spec, spec, spec]
    args += [w, m, v]
    aliases = {}
    if prev is not None:
        aliases = {len(args) + k: k for k in range(4)}
        in_specs += [pl.BlockSpec(memory_space=pl.ANY)] * 4
        args += list(prev)
    shp = jax.ShapeDtypeStruct(w.shape, F32)
    return pl.pallas_call(
        body, name=name, grid=(rows // tr,), in_specs=in_specs,
        out_specs=[spec, spec, spec, spec], out_shape=[shp, shp, shp, shp], input_output_aliases=aliases,
        compiler_params=_cparams(("parallel",)),
    )(*args)


def _sum_parts(parts, name):
    n_p, rows, cols = parts.shape
    tr = 256 if rows % 256 == 0 else rows

    def body(p_ref, o_ref):
        acc = p_ref[0]
        for p in range(1, n_p):
            acc = acc + p_ref[p]
        o_ref[...] = acc

    return pl.pallas_call(
        body, name=name, grid=(rows // tr,),
        in_specs=[pl.BlockSpec((n_p, tr, cols), lambda i: (0, i, 0))],
        out_specs=pl.BlockSpec((tr, cols), lambda i: (i, 0)),
        out_shape=jax.ShapeDtypeStruct((rows, cols), F32),
        compiler_params=_cparams(("parallel",)),
    )(parts)


def _exchange(ins, out_shapes, plan, name):
    n_in, n_out, n_cp = len(ins), len(out_shapes), len(plan)
    flips = [(fx, fy, fc) for fx in (0, 1) for fy in (0, 1) for fc in (0, 1)][1:]

    def body(*refs):
        in_refs, out_refs = refs[:n_in], refs[n_in:n_in + n_out]
        send_sems, recv_sems, loc_sems = refs[n_in + n_out:]
        x, y, c = lax.axis_index("x"), lax.axis_index("y"), lax.axis_index("c")
        me = 4 * x + 2 * y + c
        peers = []
        for fx, fy, fc in flips:
            px, py, pc = (1 - x if fx else x), (1 - y if fy else y), (1 - c if fc else c)
            peers.append(((px, py, pc), 4 * px + 2 * py + pc))

        def sel(ref, idx):
            return ref.at[idx] if idx else ref

        def remote(n, k, src_dev_slot, dst_for):
            i, in_sel, o, out_sel = plan[n]
            dev, idx = peers[k]
            return pltpu.make_async_remote_copy(
                src_ref=sel(in_refs[i], in_sel(dst_for)), dst_ref=sel(out_refs[o], out_sel(src_dev_slot)),
                send_sem=send_sems.at[n, k], recv_sem=recv_sems.at[n, k],
                device_id=dev, device_id_type=pl.DeviceIdType.MESH)

        local = []
        for n, (i, in_sel, o, out_sel) in enumerate(plan):
            cp = pltpu.make_async_copy(sel(in_refs[i], in_sel(me)), sel(out_refs[o], out_sel(me)), loc_sems.at[n])
            cp.start()
            local.append(cp)
        sends = []
        for k in range(len(flips)):
            for n in range(n_cp):
                cp = remote(n, k, me, peers[k][1])
                cp.start()
                sends.append(cp)
        for k in range(len(flips)):
            for n in range(n_cp):
                remote(n, k, peers[k][1], me).wait_recv()
        for cp in sends:
            cp.wait_send()
        for cp in local:
            cp.wait()

    any_spec = pl.BlockSpec(memory_space=pl.ANY)
    return pl.pallas_call(
        body, name=name,
        in_specs=[any_spec] * n_in, out_specs=[any_spec] * n_out, out_shape=list(out_shapes),
        scratch_shapes=[pltpu.SemaphoreType.DMA((n_cp, N_DEV - 1)), pltpu.SemaphoreType.DMA((n_cp, N_DEV - 1)),
                        pltpu.SemaphoreType.DMA((n_cp,))],
    )(*ins)


def _all_gather(arrs, name):
    n = len(arrs)

    def body(*refs):
        in_refs, out_refs = refs[:n], refs[n:2 * n]
        send_sems, recv_sems, loc_sems = refs[2 * n:]
        x, y, c = lax.axis_index("x"), lax.axis_index("y"), lax.axis_index("c")
        me, sibling = (x, y, c), (x, y, 1 - c)
        chips = [(1 - x, y), (x, 1 - y), (1 - x, 1 - y)]

        def copy(a, k, block, to, src=None):
            slot = out_refs[a].at[4 * block[0] + 2 * block[1] + block[2]]
            return pltpu.make_async_remote_copy(
                src_ref=slot if src is None else src, dst_ref=slot, send_sem=send_sems.at[a, k],
                recv_sem=recv_sems.at[a, k], device_id=to, device_id_type=pl.DeviceIdType.MESH)

        mine = [pltpu.make_async_copy(in_refs[a], out_refs[a].at[4 * x + 2 * y + c], loc_sems.at[a])
                for a in range(n)]
        for cp in mine:
            cp.start()
        first = []
        for a in range(n):
            first.append(copy(a, 0, me, sibling, src=in_refs[a]))
            first += [copy(a, 1 + j, me, (*chip, c), src=in_refs[a]) for j, chip in enumerate(chips)]
        for cp in first:
            cp.start()
        passed = []
        for j, chip in enumerate(chips):
            for a in range(n):
                copy(a, 1 + j, (*chip, c), me).wait_recv()
                cp = copy(a, 4 + j, (*chip, c), sibling)
                cp.start()
                passed.append(cp)
        for a in range(n):
            copy(a, 0, sibling, me).wait_recv()
        for j, chip in enumerate(chips):
            for a in range(n):
                copy(a, 4 + j, (*chip, 1 - c), me).wait_recv()
        for cp in first + passed:
            cp.wait_send()
        for cp in mine:
            cp.wait()

    any_spec = pl.BlockSpec(memory_space=pl.ANY)
    return pl.pallas_call(
        body, name=name, in_specs=[any_spec] * n, out_specs=[any_spec] * n,
        out_shape=[jax.ShapeDtypeStruct((N_DEV,) + a.shape, a.dtype) for a in arrs],
        scratch_shapes=[pltpu.SemaphoreType.DMA((n, N_DEV - 1)), pltpu.SemaphoreType.DMA((n, N_DEV - 1)),
                        pltpu.SemaphoreType.DMA((n,))],
    )(*arrs)


def _reduce_scatter_push(groups, name):
    ins, shapes, plan = [], [], []
    for w, layers in enumerate(groups):
        shapes.append(jax.ShapeDtypeStruct((N_DEV, len(layers)) + layers[0].shape[1:], layers[0].dtype))
        for l, arr in enumerate(layers):
            plan.append((len(ins), (lambda p: (p,)), w, (lambda s, l=l: (s, l))))
            ins.append(arr)
    return _exchange(ins, shapes, plan, name)


def _flip_peers():
    x, y, c = lax.axis_index("x"), lax.axis_index("y"), lax.axis_index("c")
    peers = []
    for fx, fy, fc in [(fx, fy, fc) for fx in (0, 1) for fy in (0, 1) for fc in (0, 1)][1:]:
        px, py, pc = (1 - x if fx else x), (1 - y if fy else y), (1 - c if fc else c)
        peers.append(((px, py, pc), 4 * px + 2 * py + pc))
    return 4 * x + 2 * y + c, peers


def _push_start(srcs, name, whole=False):
    n, n_peer = len(srcs), N_DEV - 1
    if whole:
        me_w = 4 * lax.axis_index("x") + 2 * lax.axis_index("y") + lax.axis_index("c")
        lands = [lax.dynamic_update_slice_in_dim(jnp.zeros((N_DEV,) + a.shape, a.dtype), a[None], me_w, axis=0)
                 for a in srcs]
    else:
        lands = [jnp.zeros(a.shape, a.dtype) for a in srcs]

    def body(*refs):
        src_refs, land_refs = refs[:n], refs[n:2 * n]
        send_sems, recv_sems = refs[2 * n], refs[2 * n + 1]
        token = refs[-1]
        me, peers = _flip_peers()
        for k, (dev, idx) in enumerate(peers):
            for a in range(n):
                pltpu.make_async_remote_copy(
                    src_ref=src_refs[a] if whole else src_refs[a].at[idx], dst_ref=land_refs[a].at[me],
                    send_sem=send_sems.at[a * n_peer + k], recv_sem=recv_sems.at[a * n_peer + k], device_id=dev,
                    device_id_type=pl.DeviceIdType.MESH).start()
        token[...] = jnp.zeros_like(token)

    hbm = pl.BlockSpec(memory_space=pltpu.HBM)
    sem = pl.BlockSpec(memory_space=pltpu.SEMAPHORE)
    arrs = list(srcs) + lands
    res = pl.pallas_call(
        body, name=name, in_specs=[hbm] * (2 * n),
        out_specs=(sem, sem, *[hbm] * (2 * n), pl.BlockSpec(memory_space=pltpu.VMEM)),
        out_shape=(pltpu.SemaphoreType.DMA((n * n_peer,)), pltpu.SemaphoreType.DMA((n * n_peer,)),
                   *[pltpu.HBM(a.shape, a.dtype) for a in arrs], jax.ShapeDtypeStruct((8, LANES), F32)),
        input_output_aliases={i: 2 + i for i in range(2 * n)},
        compiler_params=pltpu.CompilerParams(has_side_effects=pltpu.SideEffectType.DATAFLOW_SIDE_EFFECTING),
    )(*[pltpu.with_memory_space_constraint(a, pltpu.HBM) for a in arrs])
    return res[0], res[1], list(res[2:2 + n]), list(res[2 + n:2 + 2 * n]), res[-1]


def _push_wait(send_sems, recv_sems, srcs, lands, after, name, whole=False):
    n, n_peer = len(srcs), N_DEV - 1

    def body(*refs):
        src_refs, land_refs = refs[:n], refs[n:2 * n]
        send_s, recv_s = refs[2 * n], refs[2 * n + 1]
        _, peers = _flip_peers()
        for k, (dev, idx) in enumerate(peers):
            for a in range(n):
                cp = pltpu.make_async_remote_copy(
                    src_ref=src_refs[a] if whole else src_refs[a].at[idx], dst_ref=land_refs[a].at[idx],
                    send_sem=send_s.at[a * n_peer + k],
                    recv_sem=recv_s.at[a * n_peer + k], device_id=dev, device_id_type=pl.DeviceIdType.MESH)
                cp.wait_send()
                cp.wait_recv()

    hbm = pl.BlockSpec(memory_space=pltpu.HBM)
    sem = pl.BlockSpec(memory_space=pltpu.SEMAPHORE)
    arrs = list(srcs) + list(lands)
    res = pl.pallas_call(
        body, name=name, in_specs=[hbm] * (2 * n) + [sem, sem, pl.BlockSpec(memory_space=pl.ANY)],
        out_specs=tuple([hbm] * (2 * n)), out_shape=tuple(pltpu.HBM(a.shape, a.dtype) for a in arrs),
        input_output_aliases={i: i for i in range(2 * n)},
        compiler_params=pltpu.CompilerParams(has_side_effects=pltpu.SideEffectType.DATAFLOW_SIDE_EFFECTING),
    )(*arrs, send_sems, recv_sems, after)
    return list(res[:n]), list(res[n:])


def _ffn_fwd(x, mod, w_in, w_out, lng, lnb, rows, tag):
    bsz, seq, d = x.shape
    t = bsz * seq
    h = _modulate(x, mod, rows[0], rows[1], f"modulate_{tag}")
    z, a = _ffn_in_swiglu(h.reshape(t, d), w_in, f"ffn_in_{tag}")
    f = _matmul(a, w_out, mode="nn", group_out=False, out_dtype=F32, tm=1024, tk=a.shape[2],
                name=f"ffn_out_{tag}").reshape(bsz, seq, d)
    y = _res_ln(x, f, mod, lng, lnb, rows[2], 0.5, f"res_ln_{tag}")
    return y, (x, h, z, a, f)


def _ffn_bwd(dy, saved, mod, w_in, w_out, lng, lnb, rows, tag):
    x, h, z, a, f = saved
    bsz, seq, d = x.shape
    t = bsz * seq
    dx_res, df, dgate, dlg, dlb = _res_ln_bwd(dy, x, f, mod, lng, lnb, rows[2], 0.5, f"res_ln_bwd_{tag}")
    df2 = df.reshape(1, t, d)
    dw_out = _matmul(a, df2, mode="tn", group_out=True, out_dtype=BF16, tm=a.shape[2], tk=min(t, 2048),
                     name=f"ffn_out_dw_{tag}")
    dz = _ffn_out_dx_swiglu(df.reshape(t, d), w_out, z, f"ffn_out_dx_{tag}").reshape(N_DEV, t, -1)
    dh = _matmul(dz, w_in, mode="nt", group_out=False, out_dtype=F32, tm=1024, tk=dz.shape[2],
                 name=f"ffn_in_dx_{tag}").reshape(bsz, seq, d)
    dw_in = _matmul(h.reshape(1, t, d), dz, mode="tn", group_out=True, out_dtype=BF16, tm=d, tk=min(t, 2048),
                    name=f"ffn_in_dw_{tag}")
    dx, dsh, dsc = _modulate_bwd(dh, x, mod, dx_res, rows[1], f"modulate_bwd_{tag}")
    return dx, (dsh, dsc, dgate), dw_in, dw_out, dlg, dlb


def _mixer_fwd(x, mod, wts, small, lng, lnb, layer, tabs):
    bsz, seq, d = x.shape
    t = bsz * seq
    h = _modulate(x, mod, 3, 4, "modulate_mix")
    proj = _matmul(h.reshape(1, t, d), wts["mix_in"][None], mode="nn", group_out=True, out_dtype=F32, tm=512, tk=d,
                   name="mix_in").reshape(bsz, seq, PACK_W)
    mo, states = _hgrn_fwd(proj, small["lb_logits8"], small["hgrn_norm_g"], layer, f"hgrn_fwd_l{layer}")
    q, kv = _mla_pre(proj, small["q_norm_g"], small["kv_norm_g"], wts["uq"], wts["ukv"], tabs, "mla_pre")
    mla_scale = float((B_NOPE + B_ROPE) ** -0.5)
    mo, lse_b = _attn_fwd(q, 0, kv, 0, mo, 2, None, mla_scale, "mla_attn_fwd")
    fg = _fox_gate(proj, small["fox_b_f"], "fox_gate")
    gates = (fg, jnp.swapaxes(fg[:, :, 0:8], 1, 2))
    fox_scale = float(HEAD_DIM ** -0.5)
    mo, lse_c = _attn_fwd(proj, P_CQ // LANES, proj, P_CKV // LANES, mo, 6, gates, fox_scale, "fox_attn_fwd")
    mo = _gmlp_fwd(proj, mo, small["gmlp_ln_g"], small["gmlp_ln_b"], small["gmlp_w_s"], small["gmlp_bst"],
                   "gmlp_fwd")
    mixed = _matmul(mo.reshape(1, t, MO_W), wts["mix_out"][None], mode="nn", group_out=True, out_dtype=F32,
                    tm=1024, tk=MO_W, name="mix_out").reshape(bsz, seq, d)
    y = _res_ln(x, mixed, mod, lng, lnb, 5, 1.0, "res_ln_mix")
    return y, (x, h, proj, mo, states, q, kv, lse_b, gates, lse_c, mixed)


def _mixer_bwd(dy, saved, mod, wts, small, lng, lnb, layer, tabs):
    x, h, proj, mo, states, q, kv, lse_b, gates, lse_c, mixed = saved
    bsz, seq, d = x.shape
    t = bsz * seq
    dx_res, dmixed, dgate, dlg, dlb = _res_ln_bwd(dy, x, mixed, mod, lng, lnb, 5, 1.0, "res_ln_bwd_mix")
    dm2 = dmixed.reshape(1, t, d)
    dmo = _matmul(dm2, wts["mix_out"][None], mode="nt", group_out=True, out_dtype=F32, tm=1024, tk=d,
                  name="mix_out_dx").reshape(bsz, seq, MO_W)
    dw_out = _matmul(mo.reshape(1, t, MO_W), dm2, mode="tn", group_out=True, out_dtype=F32, tm=512, tk=min(t, 2048),
                     name="mix_out_dw")[0]
    g = {}
    dproj, g["lb_logits8"], g["hgrn_norm_g"] = _hgrn_bwd(dmo, proj, states, small["lb_logits8"],
                                                         small["hgrn_norm_g"], layer, f"hgrn_bwd_l{layer}")
    mla_scale = float((B_NOPE + B_ROPE) ** -0.5)
    dq, delta_b, _ = _attn_bwd_q(q, 0, kv, 0, mo, dmo, 2, lse_b, None, mla_scale,
                                 jax.ShapeDtypeStruct((bsz, seq, 512), F32), 0, "mla_attn_bwd_q")
    dkv, _ = _attn_bwd_kv(q, 0, kv, 0, dmo, 2, lse_b, delta_b, None, mla_scale,
                          jax.ShapeDtypeStruct((bsz, seq, 1024), F32), 0, "mla_attn_bwd_kv")
    dproj, g["q_norm_g"], g["kv_norm_g"], g["uq"], g["ukv"] = _mla_pre_bwd(
        dq, dkv, dproj, proj, small["q_norm_g"], small["kv_norm_g"], wts["uq"], wts["ukv"], tabs, "mla_pre_bwd")
    fox_scale = float(HEAD_DIM ** -0.5)
    dproj, delta_c, dfq = _attn_bwd_q(proj, P_CQ // LANES, proj, P_CKV // LANES, mo, dmo, 6, lse_c, gates,
                                      fox_scale, dproj, P_CQ // LANES, "fox_attn_bwd_q")
    dproj, dfk = _attn_bwd_kv(proj, P_CQ // LANES, proj, P_CKV // LANES, dmo, 6, lse_c, delta_c, gates, fox_scale,
                              dproj, P_CKV // (2 * LANES), "fox_attn_bwd_kv")
    dcum = jnp.swapaxes(dfq[..., 0], 1, 2) + jnp.swapaxes(dfk[:, :, 0, :], 1, 2)
    dcum = jnp.pad(dcum, ((0, 0), (0, 0), (0, LANES - N_HEADS)))
    dproj, g["fox_b_f"] = _fox_gate_bwd(dcum, dproj, proj, small["fox_b_f"], "fox_gate_bwd")
    dproj, g["gmlp_ln_g"], g["gmlp_ln_b"], g["gmlp_w_s"], g["gmlp_bst"] = _gmlp_bwd(
        dmo, dproj, proj, small["gmlp_ln_g"], small["gmlp_ln_b"], small["gmlp_w_s"], small["gmlp_bst"], "gmlp_bwd")
    dp2 = dproj.reshape(1, t, PACK_W)
    dh = _matmul(dp2, wts["mix_in"][None], mode="nt", group_out=True, out_dtype=F32, tm=512, tk=PACK_W,
                 name="mix_in_dx").reshape(bsz, seq, d)
    dw_in = _matmul(h.reshape(1, t, d), dp2, mode="tn", group_out=True, out_dtype=BF16, tm=512, tk=1024,
                    name="mix_in_dw")[0]
    dx, dsh, dsc = _modulate_bwd(dh, x, mod, dx_res, 4, "modulate_bwd_mix")
    return dx, (dsh, dsc, dgate), dw_in, dw_out, g, dlg, dlb


def _small_views(p, layer):
    return {
        "lb_logits8": jnp.pad(p["hgrn_lb_logits"], ((0, 8 - DEPTH), (0, 0))),
        "hgrn_norm_g": p["hgrn_norm_g"][layer][None],
        "q_norm_g": p["mla_q_norm_g"][layer][None],
        "kv_norm_g": p["mla_kv_norm_g"][layer][None],
        "fox_b_f": jnp.pad(p["fox_b_f"][layer][None], ((0, 0), (0, LANES - N_HEADS))),
        "gmlp_ln_g": p["gmlp_ln_g"][layer][None],
        "gmlp_ln_b": p["gmlp_ln_b"][layer][None],
        "gmlp_w_s": p["gmlp_w_s"][layer],
        "gmlp_bst": jnp.pad(p["gmlp_b_s"][layer].T, ((0, 0), (0, LANES - N_HEADS))),
    }


def _local_step(x, mod, target, weights, p, grads_ready=None):
    bsz, seq, d = x.shape
    tabs = _rope_tables(seq)
    saved = []
    for l in range(DEPTH):
        sm = _small_views(p, l)
        lng, lnb = p["ln_g"][l], p["ln_b"][l]
        w = weights(l, "ffn1", x)
        x, s1 = _ffn_fwd(x, mod[l], w["ffn1_in"], w["ffn1_out"], lng[0:1], lnb[0:1], (0, 1, 2), "ffn1")
        x, s2 = _mixer_fwd(x, mod[l], weights(l, "mix", x), sm, lng[1:2], lnb[1:2], l, tabs)
        w = weights(l, "ffn2", x)
        x, s3 = _ffn_fwd(x, mod[l], w["ffn2_in"], w["ffn2_out"], lng[2:3], lnb[2:3], (6, 7, 8), "ffn2")
        saved.append((s1, s2, s3))
    dx, loss = _loss_head(x, target, "loss_head")
    big, small, dmods = [None] * DEPTH, [None] * DEPTH, [None] * DEPTH
    tie = None
    for l in reversed(range(DEPTH)):
        w = {**weights(l, "ffn1", None), **weights(l, "mix", None), **weights(l, "ffn2", None)}
        sm = _small_views(p, l)
        lng, lnb = p["ln_g"][l], p["ln_b"][l]
        s1, s2, s3 = saved[l]
        mod_l = mod[l] if tie is None else mod[l] + tie
        dx, dm3, dwi2, dwo2, dlg2, dlb2 = _ffn_bwd(dx, s3, mod_l, w["ffn2_in"], w["ffn2_out"], lng[2:3], lnb[2:3],
                                                   (6, 7, 8), "ffn2")
        dx, dm2, dwmi, dwmo, g, dlg1, dlb1 = _mixer_bwd(dx, s2, mod_l, w, sm, lng[1:2], lnb[1:2], l, tabs)
        if grads_ready is not None:
            tie = grads_ready(l, "late", {"ffn2_in": dwi2, "ffn2_out": dwo2, "mix_in": dwmi, "mix_out": dwmo})
            mod_l = mod_l if tie is None else mod_l + tie
        dx, dm1, dwi1, dwo1, dlg0, dlb0 = _ffn_bwd(dx, s1, mod_l, w["ffn1_in"], w["ffn1_out"], lng[0:1], lnb[0:1],
                                                   (0, 1, 2), "ffn1")
        if grads_ready is not None:
            tie = grads_ready(l, "early", {"ffn1_in": dwi1, "ffn1_out": dwo1})
        dmods[l] = jnp.concatenate(list(dm1) + list(dm2) + list(dm3), axis=1)
        big[l] = {"ffn1_in": dwi1, "ffn1_out": dwo1, "ffn2_in": dwi2, "ffn2_out": dwo2, "mix_in": dwmi,
                  "mix_out": dwmo}
        g["ln_g"] = jnp.concatenate([dlg0, dlg1, dlg2], axis=0)
        g["ln_b"] = jnp.concatenate([dlb0, dlb1, dlb2], axis=0)
        small[l] = g
    return loss, dx, jnp.stack(dmods), big, small


_BIG = ("ffn1_in", "ffn1_out", "ffn2_in", "ffn2_out", "mix_in", "mix_out")


def _small_grad_list(small, loss):
    def both(fn):
        return jnp.stack([fn(small[l]) for l in range(DEPTH)])

    uq_src, ukv_src = _uq_src(), _ukv_src()
    return [
        ("loss", loss.reshape(1)),
        ("ln_g", both(lambda g: g["ln_g"])), ("ln_b", both(lambda g: g["ln_b"])),
        ("hgrn_lb_logits", small[0]["lb_logits8"][:DEPTH] + small[1]["lb_logits8"][:DEPTH]),
        ("hgrn_norm_g", both(lambda g: g["hgrn_norm_g"][0])),
        ("mla_q_norm_g", both(lambda g: g["q_norm_g"][0])),
        ("mla_kv_norm_g", both(lambda g: g["kv_norm_g"][0])),
        ("mla_w_uq", both(lambda g: _unpack_cols(g["uq"], uq_src, 384))),
        ("mla_w_ukv", both(lambda g: _unpack_cols(g["ukv"], ukv_src, 512))),
        ("fox_b_f", both(lambda g: g["fox_b_f"][0, :N_HEADS])),
        ("gmlp_ln_g", both(lambda g: g["gmlp_ln_g"][0])), ("gmlp_ln_b", both(lambda g: g["gmlp_ln_b"][0])),
        ("gmlp_w_s", both(lambda g: g["gmlp_w_s"])),
        ("gmlp_b_s", both(lambda g: g["gmlp_bst"][:, :N_HEADS].T)),
    ]


_PACK_COLS = 512


def _pack_small(items):
    flat = jnp.concatenate([a.reshape(-1).astype(F32) for _, a in items])
    n = flat.shape[0]
    tile = 8 * _PACK_COLS
    flat = jnp.pad(flat, (0, (-n) % tile))
    return flat.reshape(-1, _PACK_COLS)


def _unpack_small(buf, items):
    flat = buf.reshape(-1)
    out, off = {}, 0
    for name, a in items:
        out[name] = flat[off:off + a.size].reshape(a.shape)
        off += a.size
    return out


def _as2d(a):
    return a.reshape(-1, a.shape[-1])


def kernel(x, c, ada_w, ada_b, ln_g, ln_b, ffn1_w_in, ffn1_w_out, ffn2_w_in, ffn2_w_out, mix_w_in, mix_w_out, hgrn_lb_logits, hgrn_norm_g, mla_q_norm_g, mla_kv_norm_g, mla_w_uq, mla_w_ukv, fox_b_f, gmlp_ln_g, gmlp_ln_b, gmlp_w_s, gmlp_b_s, loss_target, m_ada_w, m_ada_b, m_ln_g, m_ln_b, m_ffn1_w_in, m_ffn1_w_out, m_ffn2_w_in, m_ffn2_w_out, m_mix_w_in, m_mix_w_out, m_hgrn_lb_logits, m_hgrn_norm_g, m_mla_q_norm_g, m_mla_kv_norm_g, m_mla_w_uq, m_mla_w_ukv, m_fox_b_f, m_gmlp_ln_g, m_gmlp_ln_b, m_gmlp_w_s, m_gmlp_b_s, v_ada_w, v_ada_b, v_ln_g, v_ln_b, v_ffn1_w_in, v_ffn1_w_out, v_ffn2_w_in, v_ffn2_w_out, v_mix_w_in, v_mix_w_out, v_hgrn_lb_logits, v_hgrn_norm_g, v_mla_q_norm_g, v_mla_kv_norm_g, v_mla_w_uq, v_mla_w_ukv, v_fox_b_f, v_gmlp_ln_g, v_gmlp_ln_b, v_gmlp_w_s, v_gmlp_b_s):
    names = ["ada_w", "ada_b", "ln_g", "ln_b", "ffn1_w_in", "ffn1_w_out", "ffn2_w_in", "ffn2_w_out", "mix_w_in",
             "mix_w_out", "hgrn_lb_logits", "hgrn_norm_g", "mla_q_norm_g", "mla_kv_norm_g", "mla_w_uq", "mla_w_ukv",
             "fox_b_f", "gmlp_ln_g", "gmlp_ln_b", "gmlp_w_s", "gmlp_b_s"]
    w = dict(zip(names, [ada_w, ada_b, ln_g, ln_b, ffn1_w_in, ffn1_w_out, ffn2_w_in, ffn2_w_out, mix_w_in, mix_w_out,
                         hgrn_lb_logits, hgrn_norm_g, mla_q_norm_g, mla_kv_norm_g, mla_w_uq, mla_w_ukv, fox_b_f,
                         gmlp_ln_g, gmlp_ln_b, gmlp_w_s, gmlp_b_s]))
    m = dict(zip(names, [m_ada_w, m_ada_b, m_ln_g, m_ln_b, m_ffn1_w_in, m_ffn1_w_out, m_ffn2_w_in, m_ffn2_w_out,
                         m_mix_w_in, m_mix_w_out, m_hgrn_lb_logits, m_hgrn_norm_g, m_mla_q_norm_g, m_mla_kv_norm_g,
                         m_mla_w_uq, m_mla_w_ukv, m_fox_b_f, m_gmlp_ln_g, m_gmlp_ln_b, m_gmlp_w_s, m_gmlp_b_s]))
    v = dict(zip(names, [v_ada_w, v_ada_b, v_ln_g, v_ln_b, v_ffn1_w_in, v_ffn1_w_out, v_ffn2_w_in, v_ffn2_w_out,
                         v_mix_w_in, v_mix_w_out, v_hgrn_lb_logits, v_hgrn_norm_g, v_mla_q_norm_g, v_mla_kv_norm_g,
                         v_mla_w_uq, v_mla_w_ukv, v_fox_b_f, v_gmlp_ln_g, v_gmlp_ln_b, v_gmlp_w_s, v_gmlp_b_s]))
    bsz, seq, d = x.shape
    me = 4 * lax.axis_index("x") + 2 * lax.axis_index("y") + lax.axis_index("c")
    mix_src, uq_src, ukv_src, mo_src = _mix_in_src(), _uq_src(), _ukv_src(), _mo_src()

    part_names = {"ffn1": ["ffn1_w_in", "ffn1_w_out"], "mix": ["mix_w_in", "mix_w_out", "mla_w_uq", "mla_w_ukv"],
                  "ffn2": ["ffn2_w_in", "ffn2_w_out"]}
    group_of = {}
    for l in range(DEPTH):
        for part in ("ffn1", "mix", "ffn2"):
            group_of[(l, part)] = (0, part) if l == 0 else (l, "all")
    in_flight = {}
    for key in dict.fromkeys(group_of.values()):
        members = [(l, part) for (l, part), g in group_of.items() if g == key]
        labels = [(l, n) for l, part in members for n in part_names[part]]
        shards = []
        for l, n in labels:
            a = w[n][l]
            if n == "mix_w_in":
                a = _pack_cols(a, mix_src)
            shards.append(a.astype(BF16))
        in_flight[key] = (labels, _push_start(shards, f"gather_start_{key[0]}_{key[1]}", whole=True))
    tie = sum(h[-1][0, 0] for _, h in in_flight.values())

    gathered = _all_gather([c, ln_g, ln_b], "gather_inputs")
    c_all = gathered[0].reshape(N_DEV * bsz, d)
    ln_g_full = jnp.moveaxis(gathered[1], 0, 2).reshape(DEPTH, 3, d)
    ln_b_full = jnp.moveaxis(gathered[2], 0, 2).reshape(DEPTH, 3, d)

    arrived, laid_out = {}, {}

    def weights(l, part, after):
        if (l, part) not in laid_out:
            laid_out[(l, part)] = lay_out(l, part, after)
        return laid_out[(l, part)]

    def lay_out(l, part, after):
        key = group_of[(l, part)]
        if key not in arrived:
            labels, (send_sems, recv_sems, srcs, lands, _) = in_flight[key]
            _, lands = _push_wait(send_sems, recv_sems, srcs, lands, after, f"gather_wait_{key[0]}_{key[1]}",
                                  whole=True)
            arrived[key] = dict(zip(labels, lands))
        gw = {n: arrived[key][(l, n)] for n in part_names[part]}
        if part != "mix":
            return {f"{part}_in": gw[f"{part}_w_in"], f"{part}_out": gw[f"{part}_w_out"].reshape(4, 704, d)}
        uq = jnp.moveaxis(gw["mla_w_uq"], 0, 1).reshape(256, 384)
        ukv = jnp.moveaxis(gw["mla_w_ukv"], 0, 1).reshape(128, 512)
        return {"mix_in": gw["mix_w_in"].reshape(d, PACK_W),
                "mix_out": _pack_cols(gw["mix_w_out"].reshape(d, d).T, mo_src).T,
                "uq": _pack_cols(uq, uq_src), "ukv": _pack_cols(ukv, ukv_src)}

    mod_cols = _ada_fwd(c_all, ada_w, "ada_fwd")
    mod_all, = _all_gather([mod_cols], "gather_mod")
    mod_mine = lax.dynamic_slice_in_dim(mod_all, me * bsz, bsz, axis=2)
    mod = jnp.moveaxis(mod_mine, 0, 2).reshape(DEPTH, bsz, N_MOD * d) + ada_b[:, None, :]
    mod = mod.reshape(DEPTH, bsz, N_MOD, d) + tie

    p = dict(w)
    p["ln_g"], p["ln_b"] = ln_g_full, ln_b_full
    def chunks(name, arr):
        if name in ("ffn1_in", "ffn2_in"):
            return arr
        if name in ("ffn1_out", "ffn2_out"):
            return arr.reshape(N_DEV, arr.shape[1] // 2, d)
        if name == "mix_in":
            return arr.reshape(N_DEV, d // N_DEV, PACK_W)
        return _unpack_cols(arr.T, mo_src, d).T.astype(BF16).reshape(N_DEV, d // N_DEV, d)

    pending, started = {}, []

    def grads_ready(l, part, grads):
        pending.update({n: chunks(n, a) for n, a in grads.items()})
        if (l == DEPTH - 1 and part == "late") or (l == 0 and part == "early"):
            return None
        keys = sorted(pending)
        handles = _push_start([pending[k] for k in keys], f"push_start_l{l}")
        pending.clear()
        started.append((l, keys, handles))
        return handles[-1][0, 0]

    loss, grad_x, dmod, big, small = _local_step(x, mod, loss_target, weights, p, grads_ready)
    del big

    dmod_all, = _all_gather([dmod.reshape(DEPTH, bsz, N_MOD * d)], "gather_dmod")
    dmod_full = jnp.moveaxis(dmod_all, 0, 1).reshape(DEPTH, N_DEV * bsz, N_MOD * d)
    cols = ada_w.shape[2]
    dmod_cols = lax.dynamic_slice_in_dim(dmod_full, me * cols, cols, axis=2)
    g_ada_w, g_ada_b = _ada_bwd(c_all, dmod_cols, dmod_full, "ada_bwd")

    recv = {}
    for l, keys, (send_sems, recv_sems, srcs, lands, _) in started:
        srcs, lands = _push_wait(send_sems, recv_sems, srcs, lands, grad_x, f"push_wait_l{l}")
        for k, src, land in zip(keys, srcs, lands):
            recv[(k, l)] = (land, lax.dynamic_index_in_dim(src, me, 0, keepdims=False))
    last = sorted(pending)
    for k, buf in zip(last, _reduce_scatter_push([[pending[k]] for k in last], "scatter_grads")):
        recv[(k, 0)] = (buf[:, 0], None)

    items = _small_grad_list(small, loss)
    parts, = _all_gather([_pack_small(items)], "gather_small")
    sg = _unpack_small(_sum_parts(parts, "sum_small"), items)

    out = {}

    def update(name, gparts):
        shape = w[name].shape
        res = _adamw(gparts, None, _as2d(w[name]), _as2d(m[name]), _as2d(v[name]), f"adamw_{name}")
        out[name] = tuple(r.reshape(shape) for r in res)

    big_of = {"ffn1_w_in": "ffn1_in", "ffn1_w_out": "ffn1_out", "ffn2_w_in": "ffn2_in", "ffn2_w_out": "ffn2_out",
              "mix_w_in": "mix_in", "mix_w_out": "mix_out"}
    for name, key in big_of.items():
        res = None
        for l in reversed(range(DEPTH)):
            parts, own = recv[(key, l)]
            if key == "mix_in":
                parts = _unpack_cols(parts, mix_src, MIX_ORIG_W)
                own = None if own is None else _unpack_cols(own, mix_src, MIX_ORIG_W)
            res = _adamw(parts, own, _as2d(w[name]), _as2d(m[name]), _as2d(v[name]), f"adamw_{name}_l{l}",
                         layer=l, prev=res)
        out[name] = tuple(r.reshape(w[name].shape) for r in res)
    update("ada_w", _as2d(g_ada_w)[None])
    update("ada_b", g_ada_b.reshape(1, DEPTH, N_MOD * d))
    for name in ("ln_g", "ln_b"):
        g_loc = lax.dynamic_slice_in_dim(sg[name], me * (d // N_DEV), d // N_DEV, axis=2)
        update(name, _as2d(g_loc)[None])
    for name, width in (("mla_w_uq", 48), ("mla_w_ukv", 64)):
        g_loc = lax.dynamic_slice_in_dim(sg[name], me * width, width, axis=2)
        update(name, _as2d(g_loc)[None])
    for name in ("hgrn_lb_logits", "hgrn_norm_g", "mla_q_norm_g", "mla_kv_norm_g", "fox_b_f", "gmlp_ln_g",
                 "gmlp_ln_b", "gmlp_w_s", "gmlp_b_s"):
        update(name, _as2d(sg[name])[None])

    return (sg["loss"][0], grad_x, *[out[n][0] for n in names], *[out[n][1] for n in names],
            *[out[n][2] for n in names], *[out[n][3] for n in names])
```

```python
import functools

import numpy as np
import jax
import jax.numpy as jnp
from jax import lax
from jax.experimental import pallas as pl
from jax.experimental.pallas import tpu as pltpu

F32 = jnp.float32
BF16 = jnp.bfloat16
HI = lax.Precision.HIGHEST

D_MODEL = 1024
DEPTH = 2
GROUP_WIDTH = 256
N_HEADS = 4
HEAD_DIM = 64
A_CHUNK = 16
LB_FLOOR = 1e-30
B_NOPE = 64
B_ROPE = 32
ROPE_THETA = 10000.0
D_CHUNK = 128
D_FF = 2816
N_MOD = 9
ALPHA = (2 * DEPTH) ** 0.25
LN_EPS = 1e-5
RMS_EPS = 1e-6
ADAM_LR = 0.001
ADAM_B1 = 0.9
ADAM_B2 = 0.999
ADAM_EPS = 1e-08
ADAM_WD = 0.01
ADAM_STEP = 10

N_DEV = 8
LANES = 128
PACK_W = 3712
MO_W = 1536
VMEM_LIMIT = 56 * 1024 * 1024
NEG = -1e30
ATTN_TILE = 512

MIX_ORIG_W = 2724
O_BCQ, O_BCKV, O_BKR, O_CQ, O_CK, O_CV, O_CF, O_DU, O_DV = 1024, 1280, 1408, 1440, 1696, 1952, 2208, 2212, 2468
P_B, P_KR, P_CQ, P_CKV, P_D, P_CF = 1024, 1408, 1536, 2048, 3072, 3584


_DN = {"nn": (((1,), (0,)), ((), ())), "nt": (((1,), (1,)), ((), ())), "tn": (((0,), (0,)), ((), ()))}


def _raw_bdot(a, b, mode):
    return lax.dot_general(a.astype(BF16), b.astype(BF16), _DN[mode], preferred_element_type=F32)


@functools.partial(jax.custom_vjp, nondiff_argnums=(2,))
def _bdot(a, b, mode):
    return _raw_bdot(a, b, mode)


def _bdot_fwd(a, b, mode):
    return _raw_bdot(a, b, mode), (a, b)


def _bdot_bwd(mode, res, g):
    a, b = res
    if mode == "nn":
        return _raw_bdot(g, b, "nt"), _raw_bdot(a, g, "tn")
    if mode == "nt":
        return _raw_bdot(g, b, "nn"), _raw_bdot(g, a, "tn")
    return _raw_bdot(b, g, "nt"), _raw_bdot(a, g, "nn")


_bdot.defvjp(_bdot_fwd, _bdot_bwd)


def _cparams(sem):
    return pltpu.CompilerParams(dimension_semantics=sem, vmem_limit_bytes=VMEM_LIMIT)


def _mix_in_src():
    src = -np.ones(PACK_W, np.int64)
    src[0:P_KR] = np.arange(0, O_BKR)
    src[P_KR + 64:P_KR + 80] = O_BKR + np.arange(16)
    src[P_KR + 96:P_KR + 112] = O_BKR + 16 + np.arange(16)
    for h in range(N_HEADS):
        src[P_CQ + 128 * h:P_CQ + 128 * h + 64] = O_CQ + 64 * h + np.arange(64)
        src[P_CKV + 256 * h:P_CKV + 256 * h + 64] = O_CK + 64 * h + np.arange(64)
        src[P_CKV + 256 * h + 128:P_CKV + 256 * h + 192] = O_CV + 64 * h + np.arange(64)
    src[P_D:P_D + 512] = O_DU + np.arange(512)
    src[P_CF:P_CF + 4] = O_CF + np.arange(4)
    return src


def _uq_src():
    src = -np.ones(512, np.int64)
    for h in range(N_HEADS):
        src[128 * h:128 * h + 64] = 96 * h + np.arange(64)
        src[128 * h + 64:128 * h + 80] = 96 * h + 64 + np.arange(16)
        src[128 * h + 96:128 * h + 112] = 96 * h + 80 + np.arange(16)
    return src


def _ukv_src():
    src = -np.ones(1024, np.int64)
    for h in range(N_HEADS):
        src[256 * h:256 * h + 64] = 128 * h + np.arange(64)
        src[256 * h + 128:256 * h + 192] = 128 * h + 64 + np.arange(64)
    return src


def _mo_src():
    src = -np.ones(MO_W, np.int64)
    src[0:256] = np.arange(256)
    for g in range(2):
        for h in range(N_HEADS):
            src[256 + 512 * g + 128 * h:256 + 512 * g + 128 * h + 64] = 256 + 256 * g + 64 * h + np.arange(64)
    src[1280:1536] = 768 + np.arange(256)
    return src


def _runs(idx):
    runs, i = [], 0
    while i < len(idx):
        j = i + 1
        while j < len(idx) and ((idx[i] < 0 and idx[j] < 0) or (idx[i] >= 0 and idx[j] == idx[i] + j - i)):
            j += 1
        runs.append((int(idx[i]), j - i))
        i = j
    return runs


def _take_runs(w, idx):
    parts = [jnp.zeros(w.shape[:-1] + (n,), w.dtype) if s < 0 else lax.slice_in_dim(w, s, s + n, axis=w.ndim - 1)
             for s, n in _runs(idx)]
    return jnp.concatenate(parts, axis=-1)


def _pack_cols(w, src):
    return _take_runs(w, src)


def _unpack_cols(wp, src, n):
    dst = np.zeros(n, np.int64)
    dst[src[src >= 0]] = np.nonzero(src >= 0)[0]
    return _take_runs(wp, dst)


def _rope_tables(seq):
    half = B_ROPE // 2
    inv_freq = ROPE_THETA ** (-jnp.arange(half, dtype=F32) / half)
    ang = jnp.arange(seq).astype(F32)[:, None] * inv_freq[None, :]
    cos, sin = jnp.cos(ang), jnp.sin(ang)
    z16 = jnp.zeros((seq, 16), F32)
    c = jnp.concatenate([jnp.ones((seq, 64), F32), cos, z16, cos, z16], axis=1)
    s1 = jnp.concatenate([jnp.zeros((seq, 64), F32), -sin, z16, z16, z16], axis=1)
    s2 = jnp.concatenate([jnp.zeros((seq, 64), F32), z16, z16, sin, z16], axis=1)
    return c, s1, s2


def _matmul(a, b, *, mode, group_out, out_dtype, tm, tk, name):
    ga, gb = a.shape[0], b.shape[0]
    g_n = max(ga, gb)
    if mode == "tn":
        k_dim, m_dim = a.shape[1:]
    else:
        m_dim, k_dim = a.shape[1:]
    n_dim = b.shape[1] if mode == "nt" else b.shape[2]
    assert m_dim % tm == 0 and k_dim % tk == 0
    kt = k_dim // tk
    n_red = kt if group_out else g_n * kt
    g_out = g_n if group_out else 1

    def split(g, r):
        return (g, r) if group_out else (r // kt, r % kt)

    def a_map(g, i, r):
        gg, kk = split(g, r)
        gg = gg if ga > 1 else 0
        return (gg, kk, i) if mode == "tn" else (gg, i, kk)

    def b_map(g, i, r):
        gg, kk = split(g, r)
        gg = gg if gb > 1 else 0
        return (gg, 0, kk) if mode == "nt" else (gg, kk, 0)

    a_blk = (None, tk, tm) if mode == "tn" else (None, tm, tk)
    b_blk = (None, n_dim, tk) if mode == "nt" else (None, tk, n_dim)
    dn = _DN[mode]

    def body(a_ref, b_ref, o_ref, *scratch):
        part = lax.dot_general(a_ref[...].astype(BF16), b_ref[...].astype(BF16), dn, preferred_element_type=F32)
        if n_red == 1:
            o_ref[...] = part.astype(o_ref.dtype)
            return
        acc_ref, = scratch
        r = pl.program_id(2)

        @pl.when(r == 0)
        def _():
            acc_ref[...] = part

        @pl.when(r > 0)
        def _():
            acc_ref[...] += part

        @pl.when(r == n_red - 1)
        def _():
            o_ref[...] = acc_ref[...].astype(o_ref.dtype)

    return pl.pallas_call(
        body, name=name, grid=(g_out, m_dim // tm, n_red),
        in_specs=[pl.BlockSpec(a_blk, a_map), pl.BlockSpec(b_blk, b_map)],
        out_specs=pl.BlockSpec((None, tm, n_dim), lambda g, i, r: (g, i, 0)),
        out_shape=jax.ShapeDtypeStruct((g_out, m_dim, n_dim), out_dtype),
        scratch_shapes=[] if n_red == 1 else [pltpu.VMEM((tm, n_dim), F32)],
        compiler_params=_cparams(("parallel", "parallel", "arbitrary")),
    )(a, b)


def _row_spec(ts, d):
    return pl.BlockSpec((None, ts, d), lambda b, s: (b, s, 0))


def _mod_spec(d):
    return pl.BlockSpec((None, N_MOD, d), lambda b, s: (b, 0, 0))


def _vec_spec(d):
    return pl.BlockSpec((1, d), lambda b, s: (0, 0))


def _bvec_spec(d):
    return pl.BlockSpec((None, 1, d), lambda b, s: (b, 0, 0))


def _modulate(x, mod, sh_row, sc_row, name, ts=512):
    bsz, seq, d = x.shape

    def body(x_ref, mod_ref, o_ref):
        sh = mod_ref[sh_row:sh_row + 1, :]
        sc = mod_ref[sc_row:sc_row + 1, :]
        o_ref[...] = (x_ref[...] * (1.0 + sc) + sh).astype(o_ref.dtype)

    return pl.pallas_call(
        body, name=name, grid=(bsz, seq // ts),
        in_specs=[_row_spec(ts, d), _mod_spec(d)], out_specs=_row_spec(ts, d),
        out_shape=jax.ShapeDtypeStruct((bsz, seq, d), BF16),
        compiler_params=_cparams(("parallel", "parallel")),
    )(x, mod)


def _modulate_bwd(dh, x, mod, dx_res, sc_row, name, ts=512):
    bsz, seq, d = x.shape

    def body(dh_ref, x_ref, mod_ref, dxr_ref, dx_ref, dsh_ref, dsc_ref):
        s = pl.program_id(1)
        sc = mod_ref[sc_row:sc_row + 1, :]
        dh_v = dh_ref[...]
        dx_ref[...] = dxr_ref[...] + dh_v * (1.0 + sc)
        psh = jnp.sum(dh_v, axis=0, keepdims=True)
        psc = jnp.sum(dh_v * x_ref[...], axis=0, keepdims=True)

        @pl.when(s == 0)
        def _():
            dsh_ref[...] = psh
            dsc_ref[...] = psc

        @pl.when(s > 0)
        def _():
            dsh_ref[...] += psh
            dsc_ref[...] += psc

    return pl.pallas_call(
        body, name=name, grid=(bsz, seq // ts),
        in_specs=[_row_spec(ts, d), _row_spec(ts, d), _mod_spec(d), _row_spec(ts, d)],
        out_specs=[_row_spec(ts, d), _bvec_spec(d), _bvec_spec(d)],
        out_shape=[jax.ShapeDtypeStruct((bsz, seq, d), F32), jax.ShapeDtypeStruct((bsz, 1, d), F32),
                   jax.ShapeDtypeStruct((bsz, 1, d), F32)],
        compiler_params=_cparams(("parallel", "arbitrary")),
    )(dh, x, mod, dx_res)


def _res_ln_fn(x, f, g, lng, lnb, cmul):
    r = ALPHA * x + (cmul * (1.0 + g)) * f
    mu = jnp.mean(r, axis=-1, keepdims=True)
    rc = r - mu
    var = jnp.mean(rc * rc, axis=-1, keepdims=True)
    return rc * lax.rsqrt(var + LN_EPS) * lng + lnb


def _res_ln(x, f, mod, lng, lnb, g_row, cmul, name, ts=512):
    bsz, seq, d = x.shape

    def body(x_ref, f_ref, mod_ref, lng_ref, lnb_ref, o_ref):
        g = mod_ref[g_row:g_row + 1, :]
        o_ref[...] = _res_ln_fn(x_ref[...], f_ref[...], g, lng_ref[...], lnb_ref[...], cmul)

    return pl.pallas_call(
        body, name=name, grid=(bsz, seq // ts),
        in_specs=[_row_spec(ts, d), _row_spec(ts, d), _mod_spec(d), _vec_spec(d), _vec_spec(d)],
        out_specs=_row_spec(ts, d), out_shape=jax.ShapeDtypeStruct((bsz, seq, d), F32),
        compiler_params=_cparams(("parallel", "parallel")),
    )(x, f, mod, lng, lnb)


def _res_ln_bwd(dy, x, f, mod, lng, lnb, g_row, cmul, name, ts=256):
    bsz, seq, d = x.shape

    def body(dy_ref, x_ref, f_ref, mod_ref, lng_ref, lnb_ref, dx_ref, df_ref, dg_ref, dlg_ref, dlb_ref):
        b, s = pl.program_id(0), pl.program_id(1)
        g = mod_ref[g_row:g_row + 1, :]
        _, vjp = jax.vjp(functools.partial(_res_ln_fn, cmul=cmul), x_ref[...], f_ref[...], g, lng_ref[...],
                         lnb_ref[...])
        dx, df, dg, dlg, dlb = vjp(dy_ref[...])
        dx_ref[...] = dx
        df_ref[...] = df.astype(df_ref.dtype)

        @pl.when(s == 0)
        def _():
            dg_ref[...] = dg

        @pl.when(s > 0)
        def _():
            dg_ref[...] += dg

        first = jnp.logical_and(b == 0, s == 0)

        @pl.when(first)
        def _():
            dlg_ref[...] = dlg
            dlb_ref[...] = dlb

        @pl.when(jnp.logical_not(first))
        def _():
            dlg_ref[...] += dlg
            dlb_ref[...] += dlb

    return pl.pallas_call(
        body, name=name, grid=(bsz, seq // ts),
        in_specs=[_row_spec(ts, d), _row_spec(ts, d), _row_spec(ts, d), _mod_spec(d), _vec_spec(d), _vec_spec(d)],
        out_specs=[_row_spec(ts, d), _row_spec(ts, d), _bvec_spec(d), _vec_spec(d), _vec_spec(d)],
        out_shape=[jax.ShapeDtypeStruct((bsz, seq, d), F32), jax.ShapeDtypeStruct((bsz, seq, d), BF16),
                   jax.ShapeDtypeStruct((bsz, 1, d), F32), jax.ShapeDtypeStruct((1, d), F32),
                   jax.ShapeDtypeStruct((1, d), F32)],
        compiler_params=_cparams(("arbitrary", "arbitrary")),
    )(dy, x, f, mod, lng, lnb)


def _loss_head(y, target, name, ts=512):
    bsz, seq, d = y.shape
    n_s = seq // ts

    def body(y_ref, t_ref, dy_ref, loss_ref, acc_ref):
        b, s = pl.program_id(0), pl.program_id(1)
        err = y_ref[...] - t_ref[...]
        dy_ref[...] = err * (1.0 / d)
        part = jnp.sum(err * err, axis=0, keepdims=True)
        first = jnp.logical_and(b == 0, s == 0)

        @pl.when(first)
        def _():
            acc_ref[...] = part

        @pl.when(jnp.logical_not(first))
        def _():
            acc_ref[...] += part

        @pl.when(jnp.logical_and(b == bsz - 1, s == n_s - 1))
        def _():
            loss_ref[...] = jnp.sum(acc_ref[...], axis=1, keepdims=True) * (0.5 / d)

    return pl.pallas_call(
        body, name=name, grid=(bsz, n_s),
        in_specs=[_row_spec(ts, d), _row_spec(ts, d)],
        out_specs=[_row_spec(ts, d), pl.BlockSpec((1, 1), lambda b, s: (0, 0))],
        out_shape=[jax.ShapeDtypeStruct((bsz, seq, d), F32), jax.ShapeDtypeStruct((1, 1), F32)],
        scratch_shapes=[pltpu.VMEM((1, d), F32)],
        compiler_params=_cparams(("arbitrary", "arbitrary")),
    )(y, target)


def _ffn_in_swiglu(h, w_in, name, tm=1024):
    t, d = h.shape
    n_sh, _, w = w_in.shape
    half = n_sh // 2

    def body(h_ref, w_ref, z_ref, a_ref):
        hv = h_ref[...]
        g = jnp.dot(hv, w_ref[0], preferred_element_type=F32)
        u = jnp.dot(hv, w_ref[1], preferred_element_type=F32)
        z_ref[0] = g.astype(z_ref.dtype)
        z_ref[1] = u.astype(z_ref.dtype)
        a_ref[...] = (g * jax.nn.sigmoid(g) * u).astype(a_ref.dtype)

    return pl.pallas_call(
        body, name=name, grid=(half, t // tm),
        in_specs=[pl.BlockSpec((tm, d), lambda g, i: (i, 0)),
                  pl.BlockSpec((2, None, d, w), lambda g, i: (0, g, 0, 0))],
        out_specs=[pl.BlockSpec((2, None, tm, w), lambda g, i: (0, g, i, 0)),
                   pl.BlockSpec((None, tm, w), lambda g, i: (g, i, 0))],
        out_shape=[jax.ShapeDtypeStruct((2, half, t, w), BF16), jax.ShapeDtypeStruct((half, t, w), BF16)],
        compiler_params=_cparams(("parallel", "parallel")),
    )(h, w_in.reshape(2, half, d, w))


def _ffn_out_dx_swiglu(df, w_out, z, name, tm=1024):
    t, d = df.shape
    half, w, _ = w_out.shape

    def body(df_ref, w_ref, z_ref, dz_ref):
        da = lax.dot_general(df_ref[...], w_ref[...], _DN["nt"], preferred_element_type=F32)
        g = z_ref[0].astype(F32)
        u = z_ref[1].astype(F32)
        sig = jax.nn.sigmoid(g)
        dz_ref[0] = (da * u * (sig * (1.0 + g * (1.0 - sig)))).astype(dz_ref.dtype)
        dz_ref[1] = (da * (g * sig)).astype(dz_ref.dtype)

    zspec = pl.BlockSpec((2, None, tm, w), lambda g, i: (0, g, i, 0))
    return pl.pallas_call(
        body, name=name, grid=(half, t // tm),
        in_specs=[pl.BlockSpec((tm, d), lambda g, i: (i, 0)), pl.BlockSpec((None, w, d), lambda g, i: (g, 0, 0)),
                  zspec],
        out_specs=zspec, out_shape=jax.ShapeDtypeStruct(z.shape, BF16),
        compiler_params=_cparams(("parallel", "parallel")),
    )(df, w_out, z)


def _log_sigmoid(x):
    return jnp.minimum(x, 0.0) - jnp.log(1.0 + jnp.exp(-jnp.abs(x)))


def _hgrn_consts():
    r = lax.broadcasted_iota(jnp.int32, (GROUP_WIDTH, GROUP_WIDTH), 0)
    c = lax.broadcasted_iota(jnp.int32, (GROUP_WIDTH, GROUP_WIDTH), 1)
    bd = (r // HEAD_DIM == c // HEAD_DIM).astype(F32)
    r16 = lax.broadcasted_iota(jnp.int32, (A_CHUNK, A_CHUNK), 0)
    c16 = lax.broadcasted_iota(jnp.int32, (A_CHUNK, A_CHUNK), 1)
    tril = (r16 >= c16).astype(F32)
    rows = lax.broadcasted_iota(jnp.int32, (A_CHUNK, GROUP_WIDTH), 0)
    return bd, tril, rows


def _hgrn_lb(logits8, layer):
    rows = lax.broadcasted_iota(jnp.int32, logits8.shape, 0)
    valid = rows < DEPTH
    mx = jnp.max(jnp.where(valid, logits8, NEG), axis=0, keepdims=True)
    e = jnp.where(valid, jnp.exp(logits8 - mx), 0.0)
    sm = e / jnp.sum(e, axis=0, keepdims=True)
    pick = jnp.logical_and(rows >= 1, rows <= layer)
    return jnp.sum(jnp.where(pick, sm, 0.0), axis=0, keepdims=True)


def _hgrn_chunk(aq, af, ai, ag, logits8, norm_g, st, *, layer, consts):
    bd, tril, rows = consts
    lb = _hgrn_lb(logits8, layer)
    la = jnp.log(jnp.maximum(lb, LB_FLOOR))
    b2 = jnp.log(1.0 - lb) + _log_sigmoid(af)
    log_f = jnp.maximum(la, b2) + jnp.log(1.0 + jnp.exp(-jnp.abs(la - b2)))
    k = 1.0 - jnp.exp(log_f)
    qf = aq * jax.nn.sigmoid(aq)
    g_cum = jnp.dot(tril, log_f, precision=HI, preferred_element_type=F32)

    c, w = A_CHUNK, GROUP_WIDTH

    def by_key(v):
        return jnp.broadcast_to(v[:, None, :], (c, c, w))

    def by_query(v):
        return jnp.broadcast_to(v[None, :, :], (c, c, w))

    s_i = lax.broadcasted_iota(jnp.int32, (c, c, w), 0)
    t_i = lax.broadcasted_iota(jnp.int32, (c, c, w), 1)
    rel = jnp.where(t_i >= s_i, by_query(g_cum) - by_key(g_cum), NEG)
    pairs = by_query(qf) * by_key(k) * jnp.exp(rel)
    a_all = _bdot(pairs.reshape(c * c, w), bd, "nn").reshape(c, c, w)
    o = jnp.sum(a_all * by_key(ai), axis=0)
    q_dec = qf * jnp.exp(g_cum)
    o = o + _bdot(q_dec, st, "nt")
    g_last = jnp.sum(jnp.where(rows == c - 1, g_cum, 0.0), axis=0, keepdims=True)
    k_end = k * jnp.exp(g_last - g_cum)
    kv = _bdot(ai, k_end, "tn")
    st_new = st * jnp.exp(g_last) + kv * bd
    ms = _bdot(o * o, bd, "nn") * (1.0 / HEAD_DIM)
    o = o * lax.rsqrt(ms + RMS_EPS) * norm_g
    return o * (ag * jax.nn.sigmoid(ag)), st_new


def _hgrn_fwd(proj, logits8, norm_g, layer, name, ts=128):
    bsz, seq, _ = proj.shape
    n_ch = ts // A_CHUNK

    def body(p_ref, lg_ref, ng_ref, o_ref, st_ref, st_scr):
        @pl.when(pl.program_id(1) == 0)
        def _():
            st_scr[...] = jnp.zeros_like(st_scr)

        consts = _hgrn_consts()
        logits_v, ng_v = lg_ref[...], ng_ref[...]

        def chunk(ci, carry):
            r = pl.multiple_of(ci * A_CHUNK, A_CHUNK)
            st = st_scr[...]
            st_ref[ci] = st
            o, st_new = _hgrn_chunk(
                p_ref[pl.ds(r, A_CHUNK), 0:256], p_ref[pl.ds(r, A_CHUNK), 256:512],
                p_ref[pl.ds(r, A_CHUNK), 512:768], p_ref[pl.ds(r, A_CHUNK), 768:1024],
                logits_v, ng_v, st, layer=layer, consts=consts)
            o_ref[pl.ds(r, A_CHUNK), :] = o.astype(o_ref.dtype)
            st_scr[...] = st_new
            return carry

        lax.fori_loop(0, n_ch, chunk, 0, unroll=2)

    return pl.pallas_call(
        body, name=name, grid=(bsz, seq // ts),
        in_specs=[pl.BlockSpec((None, ts, 1024), lambda b, s: (b, s, 0)),
                  pl.BlockSpec((8, GROUP_WIDTH), lambda b, s: (0, 0)),
                  pl.BlockSpec((1, GROUP_WIDTH), lambda b, s: (0, 0))],
        out_specs=[pl.BlockSpec((None, ts, GROUP_WIDTH), lambda b, s: (b, s, 0)),
                   pl.BlockSpec((None, n_ch, GROUP_WIDTH, GROUP_WIDTH), lambda b, s: (b, s, 0, 0))],
        out_shape=[jax.ShapeDtypeStruct((bsz, seq, MO_W), BF16),
                   jax.ShapeDtypeStruct((bsz, seq // A_CHUNK, GROUP_WIDTH, GROUP_WIDTH), F32)],
        scratch_shapes=[pltpu.VMEM((GROUP_WIDTH, GROUP_WIDTH), F32)],
        compiler_params=_cparams(("parallel", "arbitrary")),
    )(proj, logits8, norm_g)


def _hgrn_bwd(dmo, proj, states, logits8, norm_g, layer, name, ts=128):
    bsz, seq, _ = proj.shape
    n_ch = ts // A_CHUNK
    n_s = seq // ts

    def body(do_ref, p_ref, st_ref, lg_ref, ng_ref, dp_ref, dlg_ref, dng_ref, dst_scr):
        b, s = pl.program_id(0), pl.program_id(1)

        @pl.when(s == 0)
        def _():
            dst_scr[...] = jnp.zeros_like(dst_scr)

        @pl.when(jnp.logical_and(b == 0, s == 0))
        def _():
            dlg_ref[...] = jnp.zeros_like(dlg_ref)
            dng_ref[...] = jnp.zeros_like(dng_ref)

        consts = _hgrn_consts()
        logits_v, ng_v = lg_ref[...], ng_ref[...]
        fn = functools.partial(_hgrn_chunk, layer=layer, consts=consts)

        def chunk(t, carry):
            ci = n_ch - 1 - t
            r = pl.multiple_of(ci * A_CHUNK, A_CHUNK)
            _, vjp = jax.vjp(
                fn, p_ref[pl.ds(r, A_CHUNK), 0:256], p_ref[pl.ds(r, A_CHUNK), 256:512],
                p_ref[pl.ds(r, A_CHUNK), 512:768], p_ref[pl.ds(r, A_CHUNK), 768:1024],
                logits_v, ng_v, st_ref[ci])
            daq, daf, dai, dag, dlg, dng, dst = vjp((do_ref[pl.ds(r, A_CHUNK), :], dst_scr[...]))
            dp_ref[pl.ds(r, A_CHUNK), 0:256] = daq.astype(dp_ref.dtype)
            dp_ref[pl.ds(r, A_CHUNK), 256:512] = daf.astype(dp_ref.dtype)
            dp_ref[pl.ds(r, A_CHUNK), 512:768] = dai.astype(dp_ref.dtype)
            dp_ref[pl.ds(r, A_CHUNK), 768:1024] = dag.astype(dp_ref.dtype)
            dlg_ref[...] += dlg
            dng_ref[...] += dng
            dst_scr[...] = dst
            return carry

        lax.fori_loop(0, n_ch, chunk, 0, unroll=2)

    rev = lambda b, s: (b, n_s - 1 - s, 0)
    return pl.pallas_call(
        body, name=name, grid=(bsz, n_s),
        in_specs=[pl.BlockSpec((None, ts, GROUP_WIDTH), rev),
                  pl.BlockSpec((None, ts, 1024), rev),
                  pl.BlockSpec((None, n_ch, GROUP_WIDTH, GROUP_WIDTH), lambda b, s: (b, n_s - 1 - s, 0, 0)),
                  pl.BlockSpec((8, GROUP_WIDTH), lambda b, s: (0, 0)),
                  pl.BlockSpec((1, GROUP_WIDTH), lambda b, s: (0, 0))],
        out_specs=[pl.BlockSpec((None, ts, 1024), rev),
                   pl.BlockSpec((8, GROUP_WIDTH), lambda b, s: (0, 0)),
                   pl.BlockSpec((1, GROUP_WIDTH), lambda b, s: (0, 0))],
        out_shape=[jax.ShapeDtypeStruct((bsz, seq, PACK_W), BF16),
                   jax.ShapeDtypeStruct((8, GROUP_WIDTH), F32), jax.ShapeDtypeStruct((1, GROUP_WIDTH), F32)],
        scratch_shapes=[pltpu.VMEM((GROUP_WIDTH, GROUP_WIDTH), F32)],
        compiler_params=_cparams(("arbitrary", "arbitrary")),
    )(dmo, proj, states, logits8, norm_g)


def _rms_fn(x, g):
    return x * lax.rsqrt(jnp.mean(x * x, axis=-1, keepdims=True) + RMS_EPS) * g


def _tile4(t):
    return jnp.concatenate([t, t, t, t], axis=1)


def _rope(x, c, s1, s2):
    w = x.shape[-1]
    return x * c + pltpu.roll(x, 32, axis=1) * s2 + pltpu.roll(x, w - 32, axis=1) * s1


def _rope_t(dy, c, s1, s2):
    w = dy.shape[-1]
    return dy * c + pltpu.roll(dy * s2, w - 32, axis=1) + pltpu.roll(dy * s1, 32, axis=1)


def _mla_pre(proj, qg, kvg, wq, wkv, tabs, name, ts=256):
    bsz, seq, _ = proj.shape

    def body(p_ref, qg_ref, kvg_ref, wq_ref, wkv_ref, c_ref, s1_ref, s2_ref, q_ref, kv_ref):
        nq = _rms_fn(p_ref[:, 0:256], qg_ref[...])
        nkv = _rms_fn(p_ref[:, 256:384], kvg_ref[...])
        c, s1, s2 = c_ref[...], s1_ref[...], s2_ref[...]
        qp = jnp.dot(nq.astype(BF16), wq_ref[...], preferred_element_type=F32)
        q_ref[...] = _rope(qp, _tile4(c), _tile4(s1), _tile4(s2)).astype(q_ref.dtype)
        kv = jnp.dot(nkv.astype(BF16), wkv_ref[...], preferred_element_type=F32)
        krr = _rope(p_ref[:, 384:512], c, s1, s2)
        zero = jnp.zeros_like(krr)
        kv_ref[...] = (kv + jnp.concatenate([krr, zero] * N_HEADS, axis=1)).astype(kv_ref.dtype)

    tab_spec = pl.BlockSpec((ts, LANES), lambda b, s: (s, 0))
    return pl.pallas_call(
        body, name=name, grid=(bsz, seq // ts),
        in_specs=[pl.BlockSpec((None, ts, 512), lambda b, s: (b, s, P_B // 512)),
                  _vec_spec(256), _vec_spec(128),
                  pl.BlockSpec((256, 512), lambda b, s: (0, 0)), pl.BlockSpec((128, 1024), lambda b, s: (0, 0)),
                  tab_spec, tab_spec, tab_spec],
        out_specs=[_row_spec(ts, 512), _row_spec(ts, 1024)],
        out_shape=[jax.ShapeDtypeStruct((bsz, seq, 512), BF16), jax.ShapeDtypeStruct((bsz, seq, 1024), BF16)],
        compiler_params=_cparams(("parallel", "parallel")),
    )(proj, qg, kvg, wq, wkv, *tabs)


def _mla_pre_bwd(dq, dkv, dproj, proj, qg, kvg, wq, wkv, tabs, name, ts=256):
    bsz, seq, _ = proj.shape

    def body(dq_ref, dkv_ref, dp_any, p_ref, qg_ref, kvg_ref, wq_ref, wkv_ref, c_ref, s1_ref, s2_ref,
             dp_ref, dqg_ref, dkvg_ref, dwq_ref, dwkv_ref):
        del dp_any
        first = jnp.logical_and(pl.program_id(0) == 0, pl.program_id(1) == 0)

        @pl.when(first)
        def _():
            dqg_ref[...] = jnp.zeros_like(dqg_ref)
            dkvg_ref[...] = jnp.zeros_like(dkvg_ref)
            dwq_ref[...] = jnp.zeros_like(dwq_ref)
            dwkv_ref[...] = jnp.zeros_like(dwkv_ref)

        c, s1, s2 = c_ref[...], s1_ref[...], s2_ref[...]
        nq, vjp_q = jax.vjp(_rms_fn, p_ref[:, 0:256], qg_ref[...])
        nkv, vjp_kv = jax.vjp(_rms_fn, p_ref[:, 256:384], kvg_ref[...])
        dqp = _rope_t(dq_ref[...], _tile4(c), _tile4(s1), _tile4(s2)).astype(BF16)
        dkv_v = dkv_ref[...]
        dkv_b = dkv_v.astype(BF16)
        tn = (((0,), (0,)), ((), ()))
        nt = (((1,), (1,)), ((), ()))
        dwq_ref[...] += lax.dot_general(nq.astype(BF16), dqp, tn, preferred_element_type=F32)
        dwkv_ref[...] += lax.dot_general(nkv.astype(BF16), dkv_b, tn, preferred_element_type=F32)
        dcq, dqg = vjp_q(lax.dot_general(dqp, wq_ref[...], nt, preferred_element_type=F32))
        dckv, dkvg = vjp_kv(lax.dot_general(dkv_b, wkv_ref[...], nt, preferred_element_type=F32))
        dqg_ref[...] += dqg
        dkvg_ref[...] += dkvg
        dk_sum = dkv_v[:, 0:128] + dkv_v[:, 256:384] + dkv_v[:, 512:640] + dkv_v[:, 768:896]
        lane = lax.broadcasted_iota(jnp.int32, dk_sum.shape, 1)
        dkr = jnp.where(lane >= 64, _rope_t(dk_sum, c, s1, s2), 0.0)
        dp_ref[:, 0:256] = dcq.astype(dp_ref.dtype)
        dp_ref[:, 256:384] = dckv.astype(dp_ref.dtype)
        dp_ref[:, 384:512] = dkr.astype(dp_ref.dtype)

    tab_spec = pl.BlockSpec((ts, LANES), lambda b, s: (s, 0))
    const = lambda shape: pl.BlockSpec(shape, lambda b, s: (0, 0))
    return pl.pallas_call(
        body, name=name, grid=(bsz, seq // ts),
        in_specs=[_row_spec(ts, 512), _row_spec(ts, 1024), pl.BlockSpec(memory_space=pl.ANY),
                  pl.BlockSpec((None, ts, 512), lambda b, s: (b, s, P_B // 512)),
                  _vec_spec(256), _vec_spec(128), const((256, 512)), const((128, 1024)),
                  tab_spec, tab_spec, tab_spec],
        out_specs=[pl.BlockSpec((None, ts, 512), lambda b, s: (b, s, P_B // 512)),
                   _vec_spec(256), _vec_spec(128), const((256, 512)), const((128, 1024))],
        out_shape=[jax.ShapeDtypeStruct(dproj.shape, dproj.dtype), jax.ShapeDtypeStruct((1, 256), F32),
                   jax.ShapeDtypeStruct((1, 128), F32), jax.ShapeDtypeStruct((256, 512), F32),
                   jax.ShapeDtypeStruct((128, 1024), F32)],
        input_output_aliases={2: 0},
        compiler_params=_cparams(("arbitrary", "arbitrary")),
    )(dq, dkv, dproj, proj, qg, kvg, wq, wkv, *tabs)


def _fox_gate(proj, bf, name):
    bsz, seq, _ = proj.shape
    n_blk = seq // LANES

    def body(x_ref, bf_ref, f_ref):
        r_i = lax.broadcasted_iota(jnp.int32, (LANES, LANES), 0)
        c_i = lax.broadcasted_iota(jnp.int32, (LANES, LANES), 1)
        tril = (r_i >= c_i).astype(F32)
        bias = bf_ref[...]

        def blk(i, carry):
            r = pl.multiple_of(i * LANES, LANES)
            lf = _log_sigmoid(x_ref[pl.ds(r, LANES), :] + bias)
            f_ref[pl.ds(r, LANES), :] = jnp.dot(tril, lf, precision=HI, preferred_element_type=F32) + carry
            return carry + jnp.sum(lf, axis=0, keepdims=True)

        lax.fori_loop(0, n_blk, blk, jnp.zeros((1, LANES), F32))

    return pl.pallas_call(
        body, name=name, grid=(bsz,),
        in_specs=[pl.BlockSpec((None, seq, LANES), lambda b: (b, 0, P_CF // LANES)),
                  pl.BlockSpec((1, LANES), lambda b: (0, 0))],
        out_specs=pl.BlockSpec((None, seq, LANES), lambda b: (b, 0, 0)),
        out_shape=jax.ShapeDtypeStruct((bsz, seq, LANES), F32),
        compiler_params=_cparams(("parallel",)),
    )(proj, bf)


def _fox_gate_bwd(dcum, dproj, proj, bf, name):
    bsz, seq, _ = proj.shape
    n_blk = seq // LANES

    def body(dc_ref, dp_any, x_ref, bf_ref, dp_ref, dbf_ref):
        del dp_any

        @pl.when(pl.program_id(0) == 0)
        def _():
            dbf_ref[...] = jnp.zeros_like(dbf_ref)

        r_i = lax.broadcasted_iota(jnp.int32, (LANES, LANES), 0)
        c_i = lax.broadcasted_iota(jnp.int32, (LANES, LANES), 1)
        triu = (r_i <= c_i).astype(F32)
        bias = bf_ref[...]

        def blk(t, carry):
            tail, dbf = carry
            r = pl.multiple_of((n_blk - 1 - t) * LANES, LANES)
            dc = dc_ref[pl.ds(r, LANES), :]
            dlf = jnp.dot(triu, dc, precision=HI, preferred_element_type=F32) + tail
            dx = dlf * (1.0 - jax.nn.sigmoid(x_ref[pl.ds(r, LANES), :] + bias))
            dp_ref[pl.ds(r, LANES), :] = dx.astype(dp_ref.dtype)
            return tail + jnp.sum(dc, axis=0, keepdims=True), dbf + jnp.sum(dx, axis=0, keepdims=True)

        z = jnp.zeros((1, LANES), F32)
        _, dbf = lax.fori_loop(0, n_blk, blk, (z, z))
        dbf_ref[...] += dbf

    return pl.pallas_call(
        body, name=name, grid=(bsz,),
        in_specs=[pl.BlockSpec((None, seq, LANES), lambda b: (b, 0, 0)), pl.BlockSpec(memory_space=pl.ANY),
                  pl.BlockSpec((None, seq, LANES), lambda b: (b, 0, P_CF // LANES)),
                  pl.BlockSpec((1, LANES), lambda b: (0, 0))],
        out_specs=[pl.BlockSpec((None, seq, LANES), lambda b: (b, 0, P_CF // LANES)),
                   pl.BlockSpec((1, LANES), lambda b: (0, 0))],
        out_shape=[jax.ShapeDtypeStruct(dproj.shape, dproj.dtype), jax.ShapeDtypeStruct((1, LANES), F32)],
        input_output_aliases={1: 0},
        compiler_params=_cparams(("arbitrary",)),
    )(dcum, dproj, proj, bf)


def _gate_terms(fc_ref, fr_ref, h, tq, tk):
    lane = lax.broadcasted_iota(jnp.int32, (tq, LANES), 1)
    fcol = jnp.sum(jnp.where(lane == h, fc_ref[...], 0.0), axis=1, keepdims=True)
    sub = lax.broadcasted_iota(jnp.int32, (8, tk), 0)
    frow = jnp.sum(jnp.where(sub == h, fr_ref[...], 0.0), axis=0, keepdims=True)
    return fcol - frow


def _scores(q_ref, k_ref, gate_refs, scale, h, masked, tq, tk):
    q = (q_ref[...].astype(F32) * scale).astype(BF16)
    s = lax.dot_general(q, k_ref[...].astype(BF16), _DN["nt"], preferred_element_type=F32)
    if gate_refs is not None:
        s = s + _gate_terms(gate_refs[0], gate_refs[1], h, tq, tk)
    if masked:
        r_i = lax.broadcasted_iota(jnp.int32, (tq, tk), 0)
        c_i = lax.broadcasted_iota(jnp.int32, (tq, tk), 1)
        s = jnp.where(c_i <= r_i, s, NEG)
    return s, q


def _lanes(col):
    return jnp.broadcast_to(col, (col.shape[0], LANES))


def _attn_fwd(qa, q0, kva, kv0, mo, o0, gates, scale, name, tq=None):
    bsz, seq, _ = qa.shape
    tq = ATTN_TILE if tq is None else tq
    n_q = seq // tq
    gated = gates is not None

    def body(*refs):
        q_ref, k_ref, v_ref = refs[:3]
        gate_refs = refs[3:5] if gated else None
        o_ref, lse_ref, m_s, l_s, acc_s = refs[-5:]
        h, i, j = pl.program_id(1), pl.program_id(2), pl.program_id(3)

        @pl.when(j == 0)
        def _():
            m_s[...] = jnp.full_like(m_s, NEG)
            l_s[...] = jnp.zeros_like(l_s)
            acc_s[...] = jnp.zeros_like(acc_s)

        def step(masked):
            s, _ = _scores(q_ref, k_ref, gate_refs, scale, h, masked, tq, tq)
            m_prev = m_s[...]
            m_new = jnp.maximum(m_prev, jnp.max(s, axis=1, keepdims=True))
            alpha = jnp.exp(m_prev - m_new)
            p = jnp.exp(s - m_new)
            l_s[...] = alpha * l_s[...] + jnp.sum(p, axis=1, keepdims=True)
            acc_s[...] = alpha * acc_s[...] + jnp.dot(p.astype(BF16), v_ref[...].astype(BF16),
                                                      preferred_element_type=F32)
            m_s[...] = m_new

        @pl.when(j < i)
        def _():
            step(False)

        @pl.when(j == i)
        def _():
            step(True)
            o_ref[...] = (acc_s[...] / l_s[...]).astype(o_ref.dtype)
            lse_ref[...] = _lanes(m_s[...] + jnp.log(l_s[...]))

    blk = (None, tq, LANES)
    in_specs = [pl.BlockSpec(blk, lambda b, h, i, j: (b, i, q0 + h)),
                pl.BlockSpec(blk, lambda b, h, i, j: (b, jnp.minimum(j, i), kv0 + 2 * h)),
                pl.BlockSpec(blk, lambda b, h, i, j: (b, jnp.minimum(j, i), kv0 + 2 * h + 1))]
    args = [qa, kva, kva]
    if gated:
        in_specs += [pl.BlockSpec(blk, lambda b, h, i, j: (b, i, 0)),
                     pl.BlockSpec((None, 8, tq), lambda b, h, i, j: (b, 0, jnp.minimum(j, i)))]
        args += list(gates)
    in_specs.append(pl.BlockSpec(memory_space=pl.ANY))
    args.append(mo)
    return pl.pallas_call(
        body, name=name, grid=(bsz, N_HEADS, n_q, n_q), in_specs=in_specs,
        out_specs=[pl.BlockSpec(blk, lambda b, h, i, j: (b, i, o0 + h)),
                   pl.BlockSpec((None, None, tq, LANES), lambda b, h, i, j: (b, h, i, 0))],
        out_shape=[jax.ShapeDtypeStruct(mo.shape, mo.dtype),
                   jax.ShapeDtypeStruct((bsz, N_HEADS, seq, LANES), F32)],
        scratch_shapes=[pltpu.VMEM((tq, 1), F32), pltpu.VMEM((tq, 1), F32), pltpu.VMEM((tq, LANES), F32)],
        input_output_aliases={len(args) - 1: 0},
        compiler_params=_cparams(("parallel", "parallel", "parallel", "arbitrary")),
    )(*args)


def _attn_bwd_q(qa, q0, kva, kv0, mo, dmo, o0, lse, gates, scale, out, out0, name, tq=None):
    bsz, seq, _ = qa.shape
    tq = ATTN_TILE if tq is None else tq
    n_q = seq // tq
    gated = gates is not None
    aliased = not isinstance(out, jax.ShapeDtypeStruct)

    def body(*refs):
        q_ref, k_ref, v_ref, o_ref, do_ref, lse_ref = refs[:6]
        gate_refs = refs[6:8] if gated else None
        dq_ref, delta_ref, dfq_ref, acc_s, dl_s, df_s = refs[-6:]
        h, i, j = pl.program_id(1), pl.program_id(2), pl.program_id(3)

        @pl.when(j == 0)
        def _():
            acc_s[...] = jnp.zeros_like(acc_s)
            df_s[...] = jnp.zeros_like(df_s)
            dl_s[...] = jnp.sum(do_ref[...] * o_ref[...].astype(F32), axis=1, keepdims=True)

        def step(masked):
            s, _ = _scores(q_ref, k_ref, gate_refs, scale, h, masked, tq, tq)
            p = jnp.exp(s - lse_ref[:, 0:1])
            dp = lax.dot_general(do_ref[...].astype(BF16), v_ref[...].astype(BF16), _DN["nt"],
                                 preferred_element_type=F32)
            ds = p * (dp - dl_s[...])
            acc_s[...] += jnp.dot(ds.astype(BF16), k_ref[...].astype(BF16), preferred_element_type=F32)
            df_s[...] += jnp.sum(ds, axis=1, keepdims=True)

        @pl.when(j < i)
        def _():
            step(False)

        @pl.when(j == i)
        def _():
            step(True)
            dq_ref[...] = (acc_s[...] * scale).astype(dq_ref.dtype)
            delta_ref[...] = _lanes(dl_s[...])
            dfq_ref[...] = _lanes(df_s[...])

    blk = (None, tq, LANES)
    col = pl.BlockSpec((None, None, tq, LANES), lambda b, h, i, j: (b, h, i, 0))
    in_specs = [pl.BlockSpec(blk, lambda b, h, i, j: (b, i, q0 + h)),
                pl.BlockSpec(blk, lambda b, h, i, j: (b, jnp.minimum(j, i), kv0 + 2 * h)),
                pl.BlockSpec(blk, lambda b, h, i, j: (b, jnp.minimum(j, i), kv0 + 2 * h + 1)),
                pl.BlockSpec(blk, lambda b, h, i, j: (b, i, o0 + h)),
                pl.BlockSpec(blk, lambda b, h, i, j: (b, i, o0 + h)), col]
    args = [qa, kva, kva, mo, dmo, lse]
    if gated:
        in_specs += [pl.BlockSpec(blk, lambda b, h, i, j: (b, i, 0)),
                     pl.BlockSpec((None, 8, tq), lambda b, h, i, j: (b, 0, jnp.minimum(j, i)))]
        args += list(gates)
    aliases = {}
    if aliased:
        in_specs.append(pl.BlockSpec(memory_space=pl.ANY))
        args.append(out)
        aliases = {len(args) - 1: 0}
    vec = jax.ShapeDtypeStruct((bsz, N_HEADS, seq, LANES), F32)
    return pl.pallas_call(
        body, name=name, grid=(bsz, N_HEADS, n_q, n_q), in_specs=in_specs,
        out_specs=[pl.BlockSpec(blk, lambda b, h, i, j: (b, i, out0 + h)), col, col],
        out_shape=[jax.ShapeDtypeStruct(out.shape, out.dtype), vec, vec],
        scratch_shapes=[pltpu.VMEM((tq, LANES), F32), pltpu.VMEM((tq, 1), F32), pltpu.VMEM((tq, 1), F32)],
        input_output_aliases=aliases,
        compiler_params=_cparams(("parallel", "parallel", "parallel", "arbitrary")),
    )(*args)


def _attn_bwd_kv(qa, q0, kva, kv0, dmo, o0, lse, delta, gates, scale, out, out0, name, tq=None):
    bsz, seq, _ = qa.shape
    tq = ATTN_TILE if tq is None else tq
    n_q = seq // tq
    gated = gates is not None
    aliased = not isinstance(out, jax.ShapeDtypeStruct)

    def body(*refs):
        q_ref, k_ref, v_ref, do_ref, lse_ref, dl_ref = refs[:6]
        gate_refs = refs[6:8] if gated else None
        dkv_ref, dfk_ref, dk_s, dv_s, df_s = refs[-5:]
        h, j, i = pl.program_id(1), pl.program_id(2), pl.program_id(3)

        @pl.when(i == 0)
        def _():
            dk_s[...] = jnp.zeros_like(dk_s)
            dv_s[...] = jnp.zeros_like(dv_s)
            df_s[...] = jnp.zeros_like(df_s)

        def step(masked):
            s, q = _scores(q_ref, k_ref, gate_refs, scale, h, masked, tq, tq)
            p = jnp.exp(s - lse_ref[:, 0:1])
            do_b = do_ref[...].astype(BF16)
            dp = lax.dot_general(do_b, v_ref[...].astype(BF16), _DN["nt"], preferred_element_type=F32)
            ds = p * (dp - dl_ref[:, 0:1])
            dv_s[...] += lax.dot_general(p.astype(BF16), do_b, _DN["tn"], preferred_element_type=F32)
            dk_s[...] += lax.dot_general(ds.astype(BF16), q, _DN["tn"], preferred_element_type=F32)
            df_s[...] -= jnp.sum(ds, axis=0, keepdims=True)

        @pl.when(i > j)
        def _():
            step(False)

        @pl.when(i == j)
        def _():
            step(True)

        @pl.when(i == n_q - 1)
        def _():
            dkv_ref[:, 0:LANES] = dk_s[...].astype(dkv_ref.dtype)
            dkv_ref[:, LANES:2 * LANES] = dv_s[...].astype(dkv_ref.dtype)
            dfk_ref[...] = df_s[...]

    blk = (None, tq, LANES)
    col = pl.BlockSpec((None, None, tq, LANES), lambda b, h, j, i: (b, h, jnp.maximum(i, j), 0))
    in_specs = [pl.BlockSpec(blk, lambda b, h, j, i: (b, jnp.maximum(i, j), q0 + h)),
                pl.BlockSpec(blk, lambda b, h, j, i: (b, j, kv0 + 2 * h)),
                pl.BlockSpec(blk, lambda b, h, j, i: (b, j, kv0 + 2 * h + 1)),
                pl.BlockSpec(blk, lambda b, h, j, i: (b, jnp.maximum(i, j), o0 + h)), col, col]
    args = [qa, kva, kva, dmo, lse, delta]
    if gated:
        in_specs += [pl.BlockSpec(blk, lambda b, h, j, i: (b, jnp.maximum(i, j), 0)),
                     pl.BlockSpec((None, 8, tq), lambda b, h, j, i: (b, 0, j))]
        args += list(gates)
    aliases = {}
    if aliased:
        in_specs.append(pl.BlockSpec(memory_space=pl.ANY))
        args.append(out)
        aliases = {len(args) - 1: 0}
    return pl.pallas_call(
        body, name=name, grid=(bsz, N_HEADS, n_q, n_q), in_specs=in_specs,
        out_specs=[pl.BlockSpec((None, tq, 2 * LANES), lambda b, h, j, i: (b, j, out0 + h)),
                   pl.BlockSpec((None, None, 1, tq), lambda b, h, j, i: (b, h, 0, j))],
        out_shape=[jax.ShapeDtypeStruct(out.shape, out.dtype), jax.ShapeDtypeStruct((bsz, N_HEADS, 1, seq), F32)],
        scratch_shapes=[pltpu.VMEM((tq, LANES), F32), pltpu.VMEM((tq, LANES), F32), pltpu.VMEM((1, tq), F32)],
        input_output_aliases=aliases,
        compiler_params=_cparams(("parallel", "parallel", "parallel", "arbitrary")),
    )(*args)


def _gmlp_fn(uv, lng, lnb, ws, bst):
    u = jax.nn.gelu(uv[:, 0:GROUP_WIDTH])
    gv = jax.nn.gelu(uv[:, GROUP_WIDTH:2 * GROUP_WIDTH])
    mu = jnp.mean(gv, axis=-1, keepdims=True)
    vc = gv - mu
    var = jnp.mean(vc * vc, axis=-1, keepdims=True)
    vln = vc * lax.rsqrt(var + LN_EPS) * lng + lnb
    r_i = lax.broadcasted_iota(jnp.int32, (D_CHUNK, D_CHUNK), 0)
    c_i = lax.broadcasted_iota(jnp.int32, (D_CHUNK, D_CHUNK), 1)
    lane_g = lax.broadcasted_iota(jnp.int32, (D_CHUNK, GROUP_WIDTH), 1) // HEAD_DIM
    e_r = lax.broadcasted_iota(jnp.int32, (LANES, GROUP_WIDTH), 0)
    e_c = lax.broadcasted_iota(jnp.int32, (LANES, GROUP_WIDTH), 1)
    expand = (e_r == e_c // HEAD_DIM).astype(F32)
    mixed = jnp.dot(bst, expand, precision=HI, preferred_element_type=F32)
    for g in range(4):
        w = jnp.where(r_i >= c_i, ws[g], 0.0)
        mixed = mixed + jnp.where(lane_g == g, _bdot(w, vln, "nn"), 0.0)
    return u * mixed


def _gmlp_fwd(proj, mo, lng, lnb, ws, bst, name):
    bsz, seq, _ = proj.shape

    def body(p_ref, mo_any, lng_ref, lnb_ref, ws_ref, bst_ref, o_ref):
        del mo_any
        o_ref[...] = _gmlp_fn(p_ref[...], lng_ref[...], lnb_ref[...], ws_ref[...], bst_ref[...]).astype(o_ref.dtype)

    return pl.pallas_call(
        body, name=name, grid=(bsz, seq // D_CHUNK),
        in_specs=[pl.BlockSpec((None, D_CHUNK, 512), lambda b, s: (b, s, P_D // 512)),
                  pl.BlockSpec(memory_space=pl.ANY), _vec_spec(256), _vec_spec(256),
                  pl.BlockSpec((4, D_CHUNK, D_CHUNK), lambda b, s: (0, 0, 0)),
                  pl.BlockSpec((D_CHUNK, LANES), lambda b, s: (0, 0))],
        out_specs=pl.BlockSpec((None, D_CHUNK, GROUP_WIDTH), lambda b, s: (b, s, 1280 // GROUP_WIDTH)),
        out_shape=jax.ShapeDtypeStruct(mo.shape, mo.dtype),
        input_output_aliases={1: 0},
        compiler_params=_cparams(("parallel", "parallel")),
    )(proj, mo, lng, lnb, ws, bst)


def _gmlp_bwd(dmo, dproj, proj, lng, lnb, ws, bst, name):
    bsz, seq, _ = proj.shape

    def body(do_ref, dp_any, p_ref, lng_ref, lnb_ref, ws_ref, bst_ref, dp_ref, dlg_ref, dlb_ref, dws_ref, dbst_ref):
        del dp_any
        first = jnp.logical_and(pl.program_id(0) == 0, pl.program_id(1) == 0)

        @pl.when(first)
        def _():
            dlg_ref[...] = jnp.zeros_like(dlg_ref)
            dlb_ref[...] = jnp.zeros_like(dlb_ref)
            dws_ref[...] = jnp.zeros_like(dws_ref)
            dbst_ref[...] = jnp.zeros_like(dbst_ref)

        _, vjp = jax.vjp(_gmlp_fn, p_ref[...], lng_ref[...], lnb_ref[...], ws_ref[...], bst_ref[...])
        duv, dlg, dlb, dws, dbst = vjp(do_ref[...])
        dp_ref[...] = duv.astype(dp_ref.dtype)
        dlg_ref[...] += dlg
        dlb_ref[...] += dlb
        dws_ref[...] += dws
        dbst_ref[...] += dbst

    const2 = lambda shape: pl.BlockSpec(shape, lambda b, s: (0,) * len(shape))
    return pl.pallas_call(
        body, name=name, grid=(bsz, seq // D_CHUNK),
        in_specs=[pl.BlockSpec((None, D_CHUNK, GROUP_WIDTH), lambda b, s: (b, s, 1280 // GROUP_WIDTH)),
                  pl.BlockSpec(memory_space=pl.ANY),
                  pl.BlockSpec((None, D_CHUNK, 512), lambda b, s: (b, s, P_D // 512)),
                  _vec_spec(256), _vec_spec(256), const2((4, D_CHUNK, D_CHUNK)), const2((D_CHUNK, LANES))],
        out_specs=[pl.BlockSpec((None, D_CHUNK, 512), lambda b, s: (b, s, P_D // 512)),
                   _vec_spec(256), _vec_spec(256), const2((4, D_CHUNK, D_CHUNK)), const2((D_CHUNK, LANES))],
        out_shape=[jax.ShapeDtypeStruct(dproj.shape, dproj.dtype), jax.ShapeDtypeStruct((1, 256), F32),
                   jax.ShapeDtypeStruct((1, 256), F32), jax.ShapeDtypeStruct((4, D_CHUNK, D_CHUNK), F32),
                   jax.ShapeDtypeStruct((D_CHUNK, LANES), F32)],
        input_output_aliases={1: 0},
        compiler_params=_cparams(("arbitrary", "arbitrary")),
    )(dmo, dproj, proj, lng, lnb, ws, bst)


def _ada_fwd(c_all, ada_w, name):
    n_b = c_all.shape[0]
    depth, d, cols = ada_w.shape

    def body(c_ref, w_ref, o_ref):
        cv = c_ref[...]
        act = (cv * jax.nn.sigmoid(cv)).astype(BF16)
        o_ref[...] = jnp.dot(act, w_ref[...].astype(BF16), preferred_element_type=F32)

    return pl.pallas_call(
        body, name=name, grid=(depth,),
        in_specs=[pl.BlockSpec((n_b, d), lambda l: (0, 0)), pl.BlockSpec((None, d, cols), lambda l: (l, 0, 0))],
        out_specs=pl.BlockSpec((None, n_b, cols), lambda l: (l, 0, 0)),
        out_shape=jax.ShapeDtypeStruct((depth, n_b, cols), F32),
        compiler_params=_cparams(("parallel",)),
    )(c_all, ada_w)


def _ada_bwd(c_all, dmod_cols, dmod_full, name):
    n_b, d = c_all.shape
    depth, _, cols = dmod_cols.shape
    full = dmod_full.shape[-1]

    def body(c_ref, dm_ref, df_ref, gw_ref, gb_ref):
        cv = c_ref[...]
        act = (cv * jax.nn.sigmoid(cv)).astype(BF16)
        gw_ref[...] = lax.dot_general(act, dm_ref[...].astype(BF16), (((0,), (0,)), ((), ())),
                                      preferred_element_type=F32)
        gb_ref[...] = jnp.sum(df_ref[...], axis=0, keepdims=True)

    return pl.pallas_call(
        body, name=name, grid=(depth,),
        in_specs=[pl.BlockSpec((n_b, d), lambda l: (0, 0)), pl.BlockSpec((None, n_b, cols), lambda l: (l, 0, 0)),
                  pl.BlockSpec((None, n_b, full), lambda l: (l, 0, 0))],
        out_specs=[pl.BlockSpec((None, d, cols), lambda l: (l, 0, 0)),
                   pl.BlockSpec((None, 1, full), lambda l: (l, 0, 0))],
        out_shape=[jax.ShapeDtypeStruct((depth, d, cols), F32), jax.ShapeDtypeStruct((depth, 1, full), F32)],
        compiler_params=_cparams(("parallel",)),
    )(c_all, dmod_cols, dmod_full)


def _adamw(gparts, own, w, m, v, name, layer=0, prev=None):
    n_p, rows, cols = gparts.shape
    tr = rows
    if rows > 512:
        tr = next(c for c in range(512, 7, -8) if rows % c == 0)
    off = layer * (rows // tr)
    has_own = own is not None
    n_prev = 0 if prev is None else 4

    def body(*refs):
        g_ref = refs[0]
        own_ref = refs[1] if has_own else None
        w_ref, m_ref, v_ref = refs[1 + has_own:4 + has_own]
        go_ref, do_ref, mo_ref, vo_ref = refs[4 + has_own + n_prev:]
        if has_own:
            g = own_ref[...].astype(F32) + g_ref[0].astype(F32)
        else:
            g = g_ref[0].astype(F32)
        for p in range(1, n_p):
            g = g + g_ref[p].astype(F32)
        m_new = ADAM_B1 * m_ref[...] + (1.0 - ADAM_B1) * g
        v_new = ADAM_B2 * v_ref[...] + (1.0 - ADAM_B2) * (g * g)
        m_hat = m_new / (1.0 - ADAM_B1 ** ADAM_STEP)
        v_hat = v_new / (1.0 - ADAM_B2 ** ADAM_STEP)
        go_ref[...] = g
        do_ref[...] = -ADAM_LR * (m_hat / (jnp.sqrt(v_hat) + ADAM_EPS) + ADAM_WD * w_ref[...])
        mo_ref[...] = m_new
        vo_ref[...] = v_new

    spec = pl.BlockSpec((tr, cols), lambda i: (off + i, 0))
    in_specs = [pl.BlockSpec((n_p, tr, cols), lambda i: (0, i, 0))]
    args = [gparts]
    if has_own:
        in_specs.append(pl.BlockSpec((tr, cols), lambda i: (i, 0)))
        args.append(own)
    in_specs += [spec, spec, spec]
    args += [w, m, v]
    aliases = {}
    if prev is not None:
        aliases = {len(args) + k: k for k in range(4)}
        in_specs += [pl.BlockSpec(memory_space=pl.ANY)] * 4
        args += list(prev)
    shp = jax.ShapeDtypeStruct(w.shape, F32)
    return pl.pallas_call(
        body, name=name, grid=(rows // tr,), in_specs=in_specs,
        out_specs=[spec, spec, spec, spec], out_shape=[shp, shp, shp, shp], input_output_aliases=aliases,
        compiler_params=_cparams(("parallel",)),
    )(*args)


def _sum_parts(parts, name):
    n_p, rows, cols = parts.shape
    tr = 256 if rows % 256 == 0 else rows

    def body(p_ref, o_ref):
        acc = p_ref[0]
        for p in range(1, n_p):
            acc = acc + p_ref[p]
        o_ref[...] = acc

    return pl.pallas_call(
        body, name=name, grid=(rows // tr,),
        in_specs=[pl.BlockSpec((n_p, tr, cols), lambda i: (0, i, 0))],
        out_specs=pl.BlockSpec((tr, cols), lambda i: (i, 0)),
        out_shape=jax.ShapeDtypeStruct((rows, cols), F32),
        compiler_params=_cparams(("parallel",)),
    )(parts)


def _exchange(ins, out_shapes, plan, name):
    n_in, n_out, n_cp = len(ins), len(out_shapes), len(plan)
    flips = [(fx, fy, fc) for fx in (0, 1) for fy in (0, 1) for fc in (0, 1)][1:]

    def body(*refs):
        in_refs, out_refs = refs[:n_in], refs[n_in:n_in + n_out]
        send_sems, recv_sems, loc_sems = refs[n_in + n_out:]
        x, y, c = lax.axis_index("x"), lax.axis_index("y"), lax.axis_index("c")
        me = 4 * x + 2 * y + c
        peers = []
        for fx, fy, fc in flips:
            px, py, pc = (1 - x if fx else x), (1 - y if fy else y), (1 - c if fc else c)
            peers.append(((px, py, pc), 4 * px + 2 * py + pc))

        def sel(ref, idx):
            return ref.at[idx] if idx else ref

        def remote(n, k, src_dev_slot, dst_for):
            i, in_sel, o, out_sel = plan[n]
            dev, idx = peers[k]
            return pltpu.make_async_remote_copy(
                src_ref=sel(in_refs[i], in_sel(dst_for)), dst_ref=sel(out_refs[o], out_sel(src_dev_slot)),
                send_sem=send_sems.at[n, k], recv_sem=recv_sems.at[n, k],
                device_id=dev, device_id_type=pl.DeviceIdType.MESH)

        local = []
        for n, (i, in_sel, o, out_sel) in enumerate(plan):
            cp = pltpu.make_async_copy(sel(in_refs[i], in_sel(me)), sel(out_refs[o], out_sel(me)), loc_sems.at[n])
            cp.start()
            local.append(cp)
        sends = []
        for k in range(len(flips)):
            for n in range(n_cp):
                cp = remote(n, k, me, peers[k][1])
                cp.start()
                sends.append(cp)
        for k in range(len(flips)):
            for n in range(n_cp):
                remote(n, k, peers[k][1], me).wait_recv()
        for cp in sends:
            cp.wait_send()
        for cp in local:
            cp.wait()

    any_spec = pl.BlockSpec(memory_space=pl.ANY)
    return pl.pallas_call(
        body, name=name,
        in_specs=[any_spec] * n_in, out_specs=[any_spec] * n_out, out_shape=list(out_shapes),
        scratch_shapes=[pltpu.SemaphoreType.DMA((n_cp, N_DEV - 1)), pltpu.SemaphoreType.DMA((n_cp, N_DEV - 1)),
                        pltpu.SemaphoreType.DMA((n_cp,))],
    )(*ins)


def _all_gather(arrs, name):
    n = len(arrs)

    def body(*refs):
        in_refs, out_refs = refs[:n], refs[n:2 * n]
        send_sems, recv_sems, loc_sems = refs[2 * n:]
        x, y, c = lax.axis_index("x"), lax.axis_index("y"), lax.axis_index("c")
        me, sibling = (x, y, c), (x, y, 1 - c)
        chips = [(1 - x, y), (x, 1 - y), (1 - x, 1 - y)]

        def copy(a, k, block, to, src=None):
            slot = out_refs[a].at[4 * block[0] + 2 * block[1] + block[2]]
            return pltpu.make_async_remote_copy(
                src_ref=slot if src is None else src, dst_ref=slot, send_sem=send_sems.at[a, k],
                recv_sem=recv_sems.at[a, k], device_id=to, device_id_type=pl.DeviceIdType.MESH)

        mine = [pltpu.make_async_copy(in_refs[a], out_refs[a].at[4 * x + 2 * y + c], loc_sems.at[a])
                for a in range(n)]
        for cp in mine:
            cp.start()
        first = []
        for a in range(n):
            first.append(copy(a, 0, me, sibling, src=in_refs[a]))
            first += [copy(a, 1 + j, me, (*chip, c), src=in_refs[a]) for j, chip in enumerate(chips)]
        for cp in first:
            cp.start()
        passed = []
        for j, chip in enumerate(chips):
            for a in range(n):
                copy(a, 1 + j, (*chip, c), me).wait_recv()
                cp = copy(a, 4 + j, (*chip, c), sibling)
                cp.start()
                passed.append(cp)
        for a in range(n):
            copy(a, 0, sibling, me).wait_recv()
        for j, chip in enumerate(chips):
            for a in range(n):
                copy(a, 4 + j, (*chip, 1 - c), me).wait_recv()
        for cp in first + passed:
            cp.wait_send()
        for cp in mine:
            cp.wait()

    any_spec = pl.BlockSpec(memory_space=pl.ANY)
    return pl.pallas_call(
        body, name=name, in_specs=[any_spec] * n, out_specs=[any_spec] * n,
        out_shape=[jax.ShapeDtypeStruct((N_DEV,) + a.shape, a.dtype) for a in arrs],
        scratch_shapes=[pltpu.SemaphoreType.DMA((n, N_DEV - 1)), pltpu.SemaphoreType.DMA((n, N_DEV - 1)),
                        pltpu.SemaphoreType.DMA((n,))],
    )(*arrs)


def _reduce_scatter_push(groups, name):
    ins, shapes, plan = [], [], []
    for w, layers in enumerate(groups):
        shapes.append(jax.ShapeDtypeStruct((N_DEV, len(layers)) + layers[0].shape[1:], layers[0].dtype))
        for l, arr in enumerate(layers):
            plan.append((len(ins), (lambda p: (p,)), w, (lambda s, l=l: (s, l))))
            ins.append(arr)
    return _exchange(ins, shapes, plan, name)


def _flip_peers():
    x, y, c = lax.axis_index("x"), lax.axis_index("y"), lax.axis_index("c")
    peers = []
    for fx, fy, fc in [(fx, fy, fc) for fx in (0, 1) for fy in (0, 1) for fc in (0, 1)][1:]:
        px, py, pc = (1 - x if fx else x), (1 - y if fy else y), (1 - c if fc else c)
        peers.append(((px, py, pc), 4 * px + 2 * py + pc))
    return 4 * x + 2 * y + c, peers


def _push_start(srcs, name, whole=False):
    n, n_peer = len(srcs), N_DEV - 1
    if whole:
        me_w = 4 * lax.axis_index("x") + 2 * lax.axis_index("y") + lax.axis_index("c")
        lands = [lax.dynamic_update_slice_in_dim(jnp.zeros((N_DEV,) + a.shape, a.dtype), a[None], me_w, axis=0)
                 for a in srcs]
    else:
        lands = [jnp.zeros(a.shape, a.dtype) for a in srcs]

    def body(*refs):
        src_refs, land_refs = refs[:n], refs[n:2 * n]
        send_sems, recv_sems = refs[2 * n], refs[2 * n + 1]
        token = refs[-1]
        me, peers = _flip_peers()
        for k, (dev, idx) in enumerate(peers):
            for a in range(n):
                pltpu.make_async_remote_copy(
                    src_ref=src_refs[a] if whole else src_refs[a].at[idx], dst_ref=land_refs[a].at[me],
                    send_sem=send_sems.at[a * n_peer + k], recv_sem=recv_sems.at[a * n_peer + k], device_id=dev,
                    device_id_type=pl.DeviceIdType.MESH).start()
        token[...] = jnp.zeros_like(token)

    hbm = pl.BlockSpec(memory_space=pltpu.HBM)
    sem = pl.BlockSpec(memory_space=pltpu.SEMAPHORE)
    arrs = list(srcs) + lands
    res = pl.pallas_call(
        body, name=name, in_specs=[hbm] * (2 * n),
        out_specs=(sem, sem, *[hbm] * (2 * n), pl.BlockSpec(memory_space=pltpu.VMEM)),
        out_shape=(pltpu.SemaphoreType.DMA((n * n_peer,)), pltpu.SemaphoreType.DMA((n * n_peer,)),
                   *[pltpu.HBM(a.shape, a.dtype) for a in arrs], jax.ShapeDtypeStruct((8, LANES), F32)),
        input_output_aliases={i: 2 + i for i in range(2 * n)},
        compiler_params=pltpu.CompilerParams(has_side_effects=pltpu.SideEffectType.DATAFLOW_SIDE_EFFECTING),
    )(*[pltpu.with_memory_space_constraint(a, pltpu.HBM) for a in arrs])
    return res[0], res[1], list(res[2:2 + n]), list(res[2 + n:2 + 2 * n]), res[-1]


def _push_wait(send_sems, recv_sems, srcs, lands, after, name, whole=False):
    n, n_peer = len(srcs), N_DEV - 1

    def body(*refs):
        src_refs, land_refs = refs[:n], refs[n:2 * n]
        send_s, recv_s = refs[2 * n], refs[2 * n + 1]
        _, peers = _flip_peers()
        for k, (dev, idx) in enumerate(peers):
            for a in range(n):
                cp = pltpu.make_async_remote_copy(
                    src_ref=src_refs[a] if whole else src_refs[a].at[idx], dst_ref=land_refs[a].at[idx],
                    send_sem=send_s.at[a * n_peer + k],
                    recv_sem=recv_s.at[a * n_peer + k], device_id=dev, device_id_type=pl.DeviceIdType.MESH)
                cp.wait_send()
                cp.wait_recv()

    hbm = pl.BlockSpec(memory_space=pltpu.HBM)
    sem = pl.BlockSpec(memory_space=pltpu.SEMAPHORE)
    arrs = list(srcs) + list(lands)
    res = pl.pallas_call(
        body, name=name, in_specs=[hbm] * (2 * n) + [sem, sem, pl.BlockSpec(memory_space=pl.ANY)],
        out_specs=tuple([hbm] * (2 * n)), out_shape=tuple(pltpu.HBM(a.shape, a.dtype) for a in arrs),
        input_output_aliases={i: i for i in range(2 * n)},
        compiler_params=pltpu.CompilerParams(has_side_effects=pltpu.SideEffectType.DATAFLOW_SIDE_EFFECTING),
    )(*arrs, send_sems, recv_sems, after)
    return list(res[:n]), list(res[n:])


def _ffn_fwd(x, mod, w_in, w_out, lng, lnb, rows, tag):
    bsz, seq, d = x.shape
    t = bsz * seq
    h = _modulate(x, mod, rows[0], rows[1], f"modulate_{tag}")
    z, a = _ffn_in_swiglu(h.reshape(t, d), w_in, f"ffn_in_{tag}")
    f = _matmul(a, w_out, mode="nn", group_out=False, out_dtype=F32, tm=1024, tk=a.shape[2],
                name=f"ffn_out_{tag}").reshape(bsz, seq, d)
    y = _res_ln(x, f, mod, lng, lnb, rows[2], 0.5, f"res_ln_{tag}")
    return y, (x, h, z, a, f)


def _ffn_bwd(dy, saved, mod, w_in, w_out, lng, lnb, rows, tag):
    x, h, z, a, f = saved
    bsz, seq, d = x.shape
    t = bsz * seq
    dx_res, df, dgate, dlg, dlb = _res_ln_bwd(dy, x, f, mod, lng, lnb, rows[2], 0.5, f"res_ln_bwd_{tag}")
    df2 = df.reshape(1, t, d)
    dw_out = _matmul(a, df2, mode="tn", group_out=True, out_dtype=BF16, tm=a.shape[2], tk=min(t, 2048),
                     name=f"ffn_out_dw_{tag}")
    dz = _ffn_out_dx_swiglu(df.reshape(t, d), w_out, z, f"ffn_out_dx_{tag}").reshape(N_DEV, t, -1)
    dh = _matmul(dz, w_in, mode="nt", group_out=False, out_dtype=F32, tm=1024, tk=dz.shape[2],
                 name=f"ffn_in_dx_{tag}").reshape(bsz, seq, d)
    dw_in = _matmul(h.reshape(1, t, d), dz, mode="tn", group_out=True, out_dtype=BF16, tm=d, tk=min(t, 2048),
                    name=f"ffn_in_dw_{tag}")
    dx, dsh, dsc = _modulate_bwd(dh, x, mod, dx_res, rows[1], f"modulate_bwd_{tag}")
    return dx, (dsh, dsc, dgate), dw_in, dw_out, dlg, dlb


def _mixer_fwd(x, mod, wts, small, lng, lnb, layer, tabs):
    bsz, seq, d = x.shape
    t = bsz * seq
    h = _modulate(x, mod, 3, 4, "modulate_mix")
    proj = _matmul(h.reshape(1, t, d), wts["mix_in"][None], mode="nn", group_out=True, out_dtype=F32, tm=512, tk=d,
                   name="mix_in").reshape(bsz, seq, PACK_W)
    mo, states = _hgrn_fwd(proj, small["lb_logits8"], small["hgrn_norm_g"], layer, f"hgrn_fwd_l{layer}")
    q, kv = _mla_pre(proj, small["q_norm_g"], small["kv_norm_g"], wts["uq"], wts["ukv"], tabs, "mla_pre")
    mla_scale = float((B_NOPE + B_ROPE) ** -0.5)
    mo, lse_b = _attn_fwd(q, 0, kv, 0, mo, 2, None, mla_scale, "mla_attn_fwd")
    fg = _fox_gate(proj, small["fox_b_f"], "fox_gate")
    gates = (fg, jnp.swapaxes(fg[:, :, 0:8], 1, 2))
    fox_scale = float(HEAD_DIM ** -0.5)
    mo, lse_c = _attn_fwd(proj, P_CQ // LANES, proj, P_CKV // LANES, mo, 6, gates, fox_scale, "fox_attn_fwd")
    mo = _gmlp_fwd(proj, mo, small["gmlp_ln_g"], small["gmlp_ln_b"], small["gmlp_w_s"], small["gmlp_bst"],
                   "gmlp_fwd")
    mixed = _matmul(mo.reshape(1, t, MO_W), wts["mix_out"][None], mode="nn", group_out=True, out_dtype=F32,
                    tm=1024, tk=MO_W, name="mix_out").reshape(bsz, seq, d)
    y = _res_ln(x, mixed, mod, lng, lnb, 5, 1.0, "res_ln_mix")
    return y, (x, h, proj, mo, states, q, kv, lse_b, gates, lse_c, mixed)


def _mixer_bwd(dy, saved, mod, wts, small, lng, lnb, layer, tabs):
    x, h, proj, mo, states, q, kv, lse_b, gates, lse_c, mixed = saved
    bsz, seq, d = x.shape
    t = bsz * seq
    dx_res, dmixed, dgate, dlg, dlb = _res_ln_bwd(dy, x, mixed, mod, lng, lnb, 5, 1.0, "res_ln_bwd_mix")
    dm2 = dmixed.reshape(1, t, d)
    dmo = _matmul(dm2, wts["mix_out"][None], mode="nt", group_out=True, out_dtype=F32, tm=1024, tk=d,
                  name="mix_out_dx").reshape(bsz, seq, MO_W)
    dw_out = _matmul(mo.reshape(1, t, MO_W), dm2, mode="tn", group_out=True, out_dtype=F32, tm=512, tk=min(t, 2048),
                     name="mix_out_dw")[0]
    g = {}
    dproj, g["lb_logits8"], g["hgrn_norm_g"] = _hgrn_bwd(dmo, proj, states, small["lb_logits8"],
                                                         small["hgrn_norm_g"], layer, f"hgrn_bwd_l{layer}")
    mla_scale = float((B_NOPE + B_ROPE) ** -0.5)
    dq, delta_b, _ = _attn_bwd_q(q, 0, kv, 0, mo, dmo, 2, lse_b, None, mla_scale,
                                 jax.ShapeDtypeStruct((bsz, seq, 512), F32), 0, "mla_attn_bwd_q")
    dkv, _ = _attn_bwd_kv(q, 0, kv, 0, dmo, 2, lse_b, delta_b, None, mla_scale,
                          jax.ShapeDtypeStruct((bsz, seq, 1024), F32), 0, "mla_attn_bwd_kv")
    dproj, g["q_norm_g"], g["kv_norm_g"], g["uq"], g["ukv"] = _mla_pre_bwd(
        dq, dkv, dproj, proj, small["q_norm_g"], small["kv_norm_g"], wts["uq"], wts["ukv"], tabs, "mla_pre_bwd")
    fox_scale = float(HEAD_DIM ** -0.5)
    dproj, delta_c, dfq = _attn_bwd_q(proj, P_CQ // LANES, proj, P_CKV // LANES, mo, dmo, 6, lse_c, gates,
                                      fox_scale, dproj, P_CQ // LANES, "fox_attn_bwd_q")
    dproj, dfk = _attn_bwd_kv(proj, P_CQ // LANES, proj, P_CKV // LANES, dmo, 6, lse_c, delta_c, gates, fox_scale,
                              dproj, P_CKV // (2 * LANES), "fox_attn_bwd_kv")
    dcum = jnp.swapaxes(dfq[..., 0], 1, 2) + jnp.swapaxes(dfk[:, :, 0, :], 1, 2)
    dcum = jnp.pad(dcum, ((0, 0), (0, 0), (0, LANES - N_HEADS)))
    dproj, g["fox_b_f"] = _fox_gate_bwd(dcum, dproj, proj, small["fox_b_f"], "fox_gate_bwd")
    dproj, g["gmlp_ln_g"], g["gmlp_ln_b"], g["gmlp_w_s"], g["gmlp_bst"] = _gmlp_bwd(
        dmo, dproj, proj, small["gmlp_ln_g"], small["gmlp_ln_b"], small["gmlp_w_s"], small["gmlp_bst"], "gmlp_bwd")
    dp2 = dproj.reshape(1, t, PACK_W)
    dh = _matmul(dp2, wts["mix_in"][None], mode="nt", group_out=True, out_dtype=F32, tm=512, tk=PACK_W,
                 name="mix_in_dx").reshape(bsz, seq, d)
    dw_in = _matmul(h.reshape(1, t, d), dp2, mode="tn", group_out=True, out_dtype=BF16, tm=512, tk=1024,
                    name="mix_in_dw")[0]
    dx, dsh, dsc = _modulate_bwd(dh, x, mod, dx_res, 4, "modulate_bwd_mix")
    return dx, (dsh, dsc, dgate), dw_in, dw_out, g, dlg, dlb


def _small_views(p, layer):
    return {
        "lb_logits8": jnp.pad(p["hgrn_lb_logits"], ((0, 8 - DEPTH), (0, 0))),
        "hgrn_norm_g": p["hgrn_norm_g"][layer][None],
        "q_norm_g": p["mla_q_norm_g"][layer][None],
        "kv_norm_g": p["mla_kv_norm_g"][layer][None],
        "fox_b_f": jnp.pad(p["fox_b_f"][layer][None], ((0, 0), (0, LANES - N_HEADS))),
        "gmlp_ln_g": p["gmlp_ln_g"][layer][None],
        "gmlp_ln_b": p["gmlp_ln_b"][layer][None],
        "gmlp_w_s": p["gmlp_w_s"][layer],
        "gmlp_bst": jnp.pad(p["gmlp_b_s"][layer].T, ((0, 0), (0, LANES - N_HEADS))),
    }


def _local_step(x, mod, target, weights, p, grads_ready=None):
    bsz, seq, d = x.shape
    tabs = _rope_tables(seq)
    saved = []
    for l in range(DEPTH):
        sm = _small_views(p, l)
        lng, lnb = p["ln_g"][l], p["ln_b"][l]
        w = weights(l, "ffn1", x)
        x, s1 = _ffn_fwd(x, mod[l], w["ffn1_in"], w["ffn1_out"], lng[0:1], lnb[0:1], (0, 1, 2), "ffn1")
        x, s2 = _mixer_fwd(x, mod[l], weights(l, "mix", x), sm, lng[1:2], lnb[1:2], l, tabs)
        w = weights(l, "ffn2", x)
        x, s3 = _ffn_fwd(x, mod[l], w["ffn2_in"], w["ffn2_out"], lng[2:3], lnb[2:3], (6, 7, 8), "ffn2")
        saved.append((s1, s2, s3))
    dx, loss = _loss_head(x, target, "loss_head")
    big, small, dmods = [None] * DEPTH, [None] * DEPTH, [None] * DEPTH
    tie = None
    for l in reversed(range(DEPTH)):
        w = {**weights(l, "ffn1", None), **weights(l, "mix", None), **weights(l, "ffn2", None)}
        sm = _small_views(p, l)
        lng, lnb = p["ln_g"][l], p["ln_b"][l]
        s1, s2, s3 = saved[l]
        mod_l = mod[l] if tie is None else mod[l] + tie
        dx, dm3, dwi2, dwo2, dlg2, dlb2 = _ffn_bwd(dx, s3, mod_l, w["ffn2_in"], w["ffn2_out"], lng[2:3], lnb[2:3],
                                                   (6, 7, 8), "ffn2")
        dx, dm2, dwmi, dwmo, g, dlg1, dlb1 = _mixer_bwd(dx, s2, mod_l, w, sm, lng[1:2], lnb[1:2], l, tabs)
        if grads_ready is not None:
            tie = grads_ready(l, "late", {"ffn2_in": dwi2, "ffn2_out": dwo2, "mix_in": dwmi, "mix_out": dwmo})
            mod_l = mod_l if tie is None else mod_l + tie
        dx, dm1, dwi1, dwo1, dlg0, dlb0 = _ffn_bwd(dx, s1, mod_l, w["ffn1_in"], w["ffn1_out"], lng[0:1], lnb[0:1],
                                                   (0, 1, 2), "ffn1")
        if grads_ready is not None:
            tie = grads_ready(l, "early", {"ffn1_in": dwi1, "ffn1_out": dwo1})
        dmods[l] = jnp.concatenate(list(dm1) + list(dm2) + list(dm3), axis=1)
        big[l] = {"ffn1_in": dwi1, "ffn1_out": dwo1, "ffn2_in": dwi2, "ffn2_out": dwo2, "mix_in": dwmi,
                  "mix_out": dwmo}
        g["ln_g"] = jnp.concatenate([dlg0, dlg1, dlg2], axis=0)
        g["ln_b"] = jnp.concatenate([dlb0, dlb1, dlb2], axis=0)
        small[l] = g
    return loss, dx, jnp.stack(dmods), big, small


_BIG = ("ffn1_in", "ffn1_out", "ffn2_in", "ffn2_out", "mix_in", "mix_out")


def _small_grad_list(small, loss):
    def both(fn):
        return jnp.stack([fn(small[l]) for l in range(DEPTH)])

    uq_src, ukv_src = _uq_src(), _ukv_src()
    return [
        ("loss", loss.reshape(1)),
        ("ln_g", both(lambda g: g["ln_g"])), ("ln_b", both(lambda g: g["ln_b"])),
        ("hgrn_lb_logits", small[0]["lb_logits8"][:DEPTH] + small[1]["lb_logits8"][:DEPTH]),
        ("hgrn_norm_g", both(lambda g: g["hgrn_norm_g"][0])),
        ("mla_q_norm_g", both(lambda g: g["q_norm_g"][0])),
        ("mla_kv_norm_g", both(lambda g: g["kv_norm_g"][0])),
        ("mla_w_uq", both(lambda g: _unpack_cols(g["uq"], uq_src, 384))),
        ("mla_w_ukv", both(lambda g: _unpack_cols(g["ukv"], ukv_src, 512))),
        ("fox_b_f", both(lambda g: g["fox_b_f"][0, :N_HEADS])),
        ("gmlp_ln_g", both(lambda g: g["gmlp_ln_g"][0])), ("gmlp_ln_b", both(lambda g: g["gmlp_ln_b"][0])),
        ("gmlp_w_s", both(lambda g: g["gmlp_w_s"])),
        ("gmlp_b_s", both(lambda g: g["gmlp_bst"][:, :N_HEADS].T)),
    ]


_PACK_COLS = 512


def _pack_small(items):
    flat = jnp.concatenate([a.reshape(-1).astype(F32) for _, a in items])
    n = flat.shape[0]
    tile = 8 * _PACK_COLS
    flat = jnp.pad(flat, (0, (-n) % tile))
    return flat.reshape(-1, _PACK_COLS)


def _unpack_small(buf, items):
    flat = buf.reshape(-1)
    out, off = {}, 0
    for name, a in items:
        out[name] = flat[off:off + a.size].reshape(a.shape)
        off += a.size
    return out


def _as2d(a):
    return a.reshape(-1, a.shape[-1])


def kernel(x, c, ada_w, ada_b, ln_g, ln_b, ffn1_w_in, ffn1_w_out, ffn2_w_in, ffn2_w_out, mix_w_in, mix_w_out, hgrn_lb_logits, hgrn_norm_g, mla_q_norm_g, mla_kv_norm_g, mla_w_uq, mla_w_ukv, fox_b_f, gmlp_ln_g, gmlp_ln_b, gmlp_w_s, gmlp_b_s, loss_target, m_ada_w, m_ada_b, m_ln_g, m_ln_b, m_ffn1_w_in, m_ffn1_w_out, m_ffn2_w_in, m_ffn2_w_out, m_mix_w_in, m_mix_w_out, m_hgrn_lb_logits, m_hgrn_norm_g, m_mla_q_norm_g, m_mla_kv_norm_g, m_mla_w_uq, m_mla_w_ukv, m_fox_b_f, m_gmlp_ln_g, m_gmlp_ln_b, m_gmlp_w_s, m_gmlp_b_s, v_ada_w, v_ada_b, v_ln_g, v_ln_b, v_ffn1_w_in, v_ffn1_w_out, v_ffn2_w_in, v_ffn2_w_out, v_mix_w_in, v_mix_w_out, v_hgrn_lb_logits, v_hgrn_norm_g, v_mla_q_norm_g, v_mla_kv_norm_g, v_mla_w_uq, v_mla_w_ukv, v_fox_b_f, v_gmlp_ln_g, v_gmlp_ln_b, v_gmlp_w_s, v_gmlp_b_s):
    names = ["ada_w", "ada_b", "ln_g", "ln_b", "ffn1_w_in", "ffn1_w_out", "ffn2_w_in", "ffn2_w_out", "mix_w_in",
             "mix_w_out", "hgrn_lb_logits", "hgrn_norm_g", "mla_q_norm_g", "mla_kv_norm_g", "mla_w_uq", "mla_w_ukv",
             "fox_b_f", "gmlp_ln_g", "gmlp_ln_b", "gmlp_w_s", "gmlp_b_s"]
    w = dict(zip(names, [ada_w, ada_b, ln_g, ln_b, ffn1_w_in, ffn1_w_out, ffn2_w_in, ffn2_w_out, mix_w_in, mix_w_out,
                         hgrn_lb_logits, hgrn_norm_g, mla_q_norm_g, mla_kv_norm_g, mla_w_uq, mla_w_ukv, fox_b_f,
                         gmlp_ln_g, gmlp_ln_b, gmlp_w_s, gmlp_b_s]))
    m = dict(zip(names, [m_ada_w, m_ada_b, m_ln_g, m_ln_b, m_ffn1_w_in, m_ffn1_w_out, m_ffn2_w_in, m_ffn2_w_out,
                         m_mix_w_in, m_mix_w_out, m_hgrn_lb_logits, m_hgrn_norm_g, m_mla_q_norm_g, m_mla_kv_norm_g,
                         m_mla_w_uq, m_mla_w_ukv, m_fox_b_f, m_gmlp_ln_g, m_gmlp_ln_b, m_gmlp_w_s, m_gmlp_b_s]))
    v = dict(zip(names, [v_ada_w, v_ada_b, v_ln_g, v_ln_b, v_ffn1_w_in, v_ffn1_w_out, v_ffn2_w_in, v_ffn2_w_out,
                         v_mix_w_in, v_mix_w_out, v_hgrn_lb_logits, v_hgrn_norm_g, v_mla_q_norm_g, v_mla_kv_norm_g,
                         v_mla_w_uq, v_mla_w_ukv, v_fox_b_f, v_gmlp_ln_g, v_gmlp_ln_b, v_gmlp_w_s, v_gmlp_b_s]))
    bsz, seq, d = x.shape
    me = 4 * lax.axis_index("x") + 2 * lax.axis_index("y") + lax.axis_index("c")
    mix_src, uq_src, ukv_src, mo_src = _mix_in_src(), _uq_src(), _ukv_src(), _mo_src()

    part_names = {"ffn1": ["ffn1_w_in", "ffn1_w_out"], "mix": ["mix_w_in", "mix_w_out", "mla_w_uq", "mla_w_ukv"],
                  "ffn2": ["ffn2_w_in", "ffn2_w_out"]}
    group_of = {}
    for l in range(DEPTH):
        for part in ("ffn1", "mix", "ffn2"):
            group_of[(l, part)] = (0, part) if l == 0 else (l, "all")
    in_flight = {}
    for key in dict.fromkeys(group_of.values()):
        members = [(l, part) for (l, part), g in group_of.items() if g == key]
        labels = [(l, n) for l, part in members for n in part_names[part]]
        shards = []
        for l, n in labels:
            a = w[n][l]
            if n == "mix_w_in":
                a = _pack_cols(a, mix_src)
            shards.append(a.astype(BF16))
        in_flight[key] = (labels, _push_start(shards, f"gather_start_{key[0]}_{key[1]}", whole=True))
    tie = sum(h[-1][0, 0] for _, h in in_flight.values())

    gathered = _all_gather([c, ln_g, ln_b], "gather_inputs")
    c_all = gathered[0].reshape(N_DEV * bsz, d)
    ln_g_full = jnp.moveaxis(gathered[1], 0, 2).reshape(DEPTH, 3, d)
    ln_b_full = jnp.moveaxis(gathered[2], 0, 2).reshape(DEPTH, 3, d)

    arrived, laid_out = {}, {}

    def weights(l, part, after):
        if (l, part) not in laid_out:
            laid_out[(l, part)] = lay_out(l, part, after)
        return laid_out[(l, part)]

    def lay_out(l, part, after):
        key = group_of[(l, part)]
        if key not in arrived:
            labels, (send_sems, recv_sems, srcs, lands, _) = in_flight[key]
            _, lands = _push_wait(send_sems, recv_sems, srcs, lands, after, f"gather_wait_{key[0]}_{key[1]}",
                                  whole=True)
            arrived[key] = dict(zip(labels, lands))
        gw = {n: arrived[key][(l, n)] for n in part_names[part]}
        if part != "mix":
            return {f"{part}_in": gw[f"{part}_w_in"], f"{part}_out": gw[f"{part}_w_out"].reshape(4, 704, d)}
        uq = jnp.moveaxis(gw["mla_w_uq"], 0, 1).reshape(256, 384)
        ukv = jnp.moveaxis(gw["mla_w_ukv"], 0, 1).reshape(128, 512)
        return {"mix_in": gw["mix_w_in"].reshape(d, PACK_W),
                "mix_out": _pack_cols(gw["mix_w_out"].reshape(d, d).T, mo_src).T,
                "uq": _pack_cols(uq, uq_src), "ukv": _pack_cols(ukv, ukv_src)}

    mod_cols = _ada_fwd(c_all, ada_w, "ada_fwd")
    mod_all, = _all_gather([mod_cols], "gather_mod")
    mod_mine = lax.dynamic_slice_in_dim(mod_all, me * bsz, bsz, axis=2)
    mod = jnp.moveaxis(mod_mine, 0, 2).reshape(DEPTH, bsz, N_MOD * d) + ada_b[:, None, :]
    mod = mod.reshape(DEPTH, bsz, N_MOD, d) + tie

    p = dict(w)
    p["ln_g"], p["ln_b"] = ln_g_full, ln_b_full
    def chunks(name, arr):
        if name in ("ffn1_in", "ffn2_in"):
            return arr
        if name in ("ffn1_out", "ffn2_out"):
            return arr.reshape(N_DEV, arr.shape[1] // 2, d)
        if name == "mix_in":
            return arr.reshape(N_DEV, d // N_DEV, PACK_W)
        return _unpack_cols(arr.T, mo_src, d).T.astype(BF16).reshape(N_DEV, d // N_DEV, d)

    pending, started = {}, []

    def grads_ready(l, part, grads):
        pending.update({n: chunks(n, a) for n, a in grads.items()})
        if l == DEPTH - 1 and part == "late":
            return None
        keys = sorted(pending)
        handles = _push_start([pending[k] for k in keys], f"push_start_{len(started)}")
        pending.clear()
        started.append((l, keys, handles))
        return handles[-1][0, 0]

    loss, grad_x, dmod, big, small = _local_step(x, mod, loss_target, weights, p, grads_ready)
    del big

    dmod_all, = _all_gather([dmod.reshape(DEPTH, bsz, N_MOD * d)], "gather_dmod")
    dmod_full = jnp.moveaxis(dmod_all, 0, 1).reshape(DEPTH, N_DEV * bsz, N_MOD * d)
    cols = ada_w.shape[2]
    dmod_cols = lax.dynamic_slice_in_dim(dmod_full, me * cols, cols, axis=2)
    g_ada_w, g_ada_b = _ada_bwd(c_all, dmod_cols, dmod_full, "ada_bwd")

    recv = {}

    def arrive(n, after):
        l, keys, (send_sems, recv_sems, srcs, lands, _) = started[n]
        srcs, lands = _push_wait(send_sems, recv_sems, srcs, lands, after, f"push_wait_{n}")
        for k, src, land in zip(keys, srcs, lands):
            recv[(k, l)] = (land, lax.dynamic_index_in_dim(src, me, 0, keepdims=False))

    for n in range(len(started) - 1):
        arrive(n, grad_x)

    items = _small_grad_list(small, loss)
    parts, = _all_gather([_pack_small(items)], "gather_small")
    sg = _unpack_small(_sum_parts(parts, "sum_small"), items)

    out = {}

    def update(name, gparts):
        shape = w[name].shape
        res = _adamw(gparts, None, _as2d(w[name]), _as2d(m[name]), _as2d(v[name]), f"adamw_{name}")
        out[name] = tuple(r.reshape(shape) for r in res)

    big_of = {"ffn1_w_in": "ffn1_in", "ffn1_w_out": "ffn1_out", "ffn2_w_in": "ffn2_in", "ffn2_w_out": "ffn2_out",
              "mix_w_in": "mix_in", "mix_w_out": "mix_out"}
    def big_update(name, key, l, prev):
        parts, own = recv[(key, l)]
        if key == "mix_in":
            parts = _unpack_cols(parts, mix_src, MIX_ORIG_W)
            own = _unpack_cols(own, mix_src, MIX_ORIG_W)
        return _adamw(parts, own, _as2d(w[name]), _as2d(m[name]), _as2d(v[name]), f"adamw_{name}_l{l}",
                      layer=l, prev=prev)

    chain = {name: None for name in big_of}
    for name, key in big_of.items():
        for l in reversed(range(DEPTH)):
            if (key, l) in recv:
                chain[name] = big_update(name, key, l, chain[name])
    update("ada_w", _as2d(g_ada_w)[None])
    update("ada_b", g_ada_b.reshape(1, DEPTH, N_MOD * d))
    for name in ("ln_g", "ln_b"):
        g_loc = lax.dynamic_slice_in_dim(sg[name], me * (d // N_DEV), d // N_DEV, axis=2)
        update(name, _as2d(g_loc)[None])
    for name, width in (("mla_w_uq", 48), ("mla_w_ukv", 64)):
        g_loc = lax.dynamic_slice_in_dim(sg[name], me * width, width, axis=2)
        update(name, _as2d(g_loc)[None])
    for name in ("hgrn_lb_logits", "hgrn_norm_g", "mla_q_norm_g", "mla_kv_norm_g", "fox_b_f", "gmlp_ln_g",
                 "gmlp_ln_b", "gmlp_w_s", "gmlp_b_s"):
        update(name, _as2d(sg[name])[None])
    arrive(len(started) - 1, out["gmlp_w_s"][0])
    for name, key in big_of.items():
        for l in reversed(range(DEPTH)):
            if (key, l) in dict.fromkeys((k, started[-1][0]) for k in started[-1][1]):
                chain[name] = big_update(name, key, l, chain[name])
        out[name] = tuple(r.reshape(w[name].shape) for r in chain[name])

    return (sg["loss"][0], grad_x, *[out[n][0] for n in names], *[out[n][1] for n in names],
            *[out[n][2] for n in names], *[out[n][3] for n in names])
```

```python
import functools

import numpy as np
import jax
import jax.numpy as jnp
from jax import lax
from jax.experimental import pallas as pl
from jax.experimental.pallas import tpu as pltpu

F32 = jnp.float32
BF16 = jnp.bfloat16
HI = lax.Precision.HIGHEST

D_MODEL = 1024
DEPTH = 2
GROUP_WIDTH = 256
N_HEADS = 4
HEAD_DIM = 64
A_CHUNK = 16
LB_FLOOR = 1e-30
B_NOPE = 64
B_ROPE = 32
ROPE_THETA = 10000.0
D_CHUNK = 128
D_FF = 2816
N_MOD = 9
ALPHA = (2 * DEPTH) ** 0.25
LN_EPS = 1e-5
RMS_EPS = 1e-6
ADAM_LR = 0.001
ADAM_B1 = 0.9
ADAM_B2 = 0.999
ADAM_EPS = 1e-08
ADAM_WD = 0.01
ADAM_STEP = 10

N_DEV = 8
LANES = 128
PACK_W = 3712
MO_W = 1536
VMEM_LIMIT = 56 * 1024 * 1024
NEG = -1e30
ATTN_TILE = 512

MIX_ORIG_W = 2724
O_BCQ, O_BCKV, O_BKR, O_CQ, O_CK, O_CV, O_CF, O_DU, O_DV = 1024, 1280, 1408, 1440, 1696, 1952, 2208, 2212, 2468
P_B, P_KR, P_CQ, P_CKV, P_D, P_CF = 1024, 1408, 1536, 2048, 3072, 3584


_DN = {"nn": (((1,), (0,)), ((), ())), "nt": (((1,), (1,)), ((), ())), "tn": (((0,), (0,)), ((), ()))}


def _raw_bdot(a, b, mode):
    return lax.dot_general(a.astype(BF16), b.astype(BF16), _DN[mode], preferred_element_type=F32)


@functools.partial(jax.custom_vjp, nondiff_argnums=(2,))
def _bdot(a, b, mode):
    return _raw_bdot(a, b, mode)


def _bdot_fwd(a, b, mode):
    return _raw_bdot(a, b, mode), (a, b)


def _bdot_bwd(mode, res, g):
    a, b = res
    if mode == "nn":
        return _raw_bdot(g, b, "nt"), _raw_bdot(a, g, "tn")
    if mode == "nt":
        return _raw_bdot(g, b, "nn"), _raw_bdot(g, a, "tn")
    return _raw_bdot(b, g, "nt"), _raw_bdot(a, g, "nn")


_bdot.defvjp(_bdot_fwd, _bdot_bwd)


def _cparams(sem):
    return pltpu.CompilerParams(dimension_semantics=sem, vmem_limit_bytes=VMEM_LIMIT)


def _mix_in_src():
    src = -np.ones(PACK_W, np.int64)
    src[0:P_KR] = np.arange(0, O_BKR)
    src[P_KR + 64:P_KR + 80] = O_BKR + np.arange(16)
    src[P_KR + 96:P_KR + 112] = O_BKR + 16 + np.arange(16)
    for h in range(N_HEADS):
        src[P_CQ + 128 * h:P_CQ + 128 * h + 64] = O_CQ + 64 * h + np.arange(64)
        src[P_CKV + 256 * h:P_CKV + 256 * h + 64] = O_CK + 64 * h + np.arange(64)
        src[P_CKV + 256 * h + 128:P_CKV + 256 * h + 192] = O_CV + 64 * h + np.arange(64)
    src[P_D:P_D + 512] = O_DU + np.arange(512)
    src[P_CF:P_CF + 4] = O_CF + np.arange(4)
    return src


def _uq_src():
    src = -np.ones(512, np.int64)
    for h in range(N_HEADS):
        src[128 * h:128 * h + 64] = 96 * h + np.arange(64)
        src[128 * h + 64:128 * h + 80] = 96 * h + 64 + np.arange(16)
        src[128 * h + 96:128 * h + 112] = 96 * h + 80 + np.arange(16)
    return src


def _ukv_src():
    src = -np.ones(1024, np.int64)
    for h in range(N_HEADS):
        src[256 * h:256 * h + 64] = 128 * h + np.arange(64)
        src[256 * h + 128:256 * h + 192] = 128 * h + 64 + np.arange(64)
    return src


def _mo_src():
    src = -np.ones(MO_W, np.int64)
    src[0:256] = np.arange(256)
    for g in range(2):
        for h in range(N_HEADS):
            src[256 + 512 * g + 128 * h:256 + 512 * g + 128 * h + 64] = 256 + 256 * g + 64 * h + np.arange(64)
    src[1280:1536] = 768 + np.arange(256)
    return src


def _runs(idx):
    runs, i = [], 0
    while i < len(idx):
        j = i + 1
        while j < len(idx) and ((idx[i] < 0 and idx[j] < 0) or (idx[i] >= 0 and idx[j] == idx[i] + j - i)):
            j += 1
        runs.append((int(idx[i]), j - i))
        i = j
    return runs


def _take_runs(w, idx):
    parts = [jnp.zeros(w.shape[:-1] + (n,), w.dtype) if s < 0 else lax.slice_in_dim(w, s, s + n, axis=w.ndim - 1)
             for s, n in _runs(idx)]
    return jnp.concatenate(parts, axis=-1)


def _pack_cols(w, src):
    return _take_runs(w, src)


def _unpack_cols(wp, src, n):
    dst = np.zeros(n, np.int64)
    dst[src[src >= 0]] = np.nonzero(src >= 0)[0]
    return _take_runs(wp, dst)


def _rope_tables(seq):
    half = B_ROPE // 2
    inv_freq = ROPE_THETA ** (-jnp.arange(half, dtype=F32) / half)
    ang = jnp.arange(seq).astype(F32)[:, None] * inv_freq[None, :]
    cos, sin = jnp.cos(ang), jnp.sin(ang)
    z16 = jnp.zeros((seq, 16), F32)
    c = jnp.concatenate([jnp.ones((seq, 64), F32), cos, z16, cos, z16], axis=1)
    s1 = jnp.concatenate([jnp.zeros((seq, 64), F32), -sin, z16, z16, z16], axis=1)
    s2 = jnp.concatenate([jnp.zeros((seq, 64), F32), z16, z16, sin, z16], axis=1)
    return c, s1, s2


def _matmul(a, b, *, mode, group_out, out_dtype, tm, tk, name):
    ga, gb = a.shape[0], b.shape[0]
    g_n = max(ga, gb)
    if mode == "tn":
        k_dim, m_dim = a.shape[1:]
    else:
        m_dim, k_dim = a.shape[1:]
    n_dim = b.shape[1] if mode == "nt" else b.shape[2]
    assert m_dim % tm == 0 and k_dim % tk == 0
    kt = k_dim // tk
    n_red = kt if group_out else g_n * kt
    g_out = g_n if group_out else 1

    def split(g, r):
        return (g, r) if group_out else (r // kt, r % kt)

    def a_map(g, i, r):
        gg, kk = split(g, r)
        gg = gg if ga > 1 else 0
        return (gg, kk, i) if mode == "tn" else (gg, i, kk)

    def b_map(g, i, r):
        gg, kk = split(g, r)
        gg = gg if gb > 1 else 0
        return (gg, 0, kk) if mode == "nt" else (gg, kk, 0)

    a_blk = (None, tk, tm) if mode == "tn" else (None, tm, tk)
    b_blk = (None, n_dim, tk) if mode == "nt" else (None, tk, n_dim)
    dn = _DN[mode]

    def body(a_ref, b_ref, o_ref, *scratch):
        part = lax.dot_general(a_ref[...].astype(BF16), b_ref[...].astype(BF16), dn, preferred_element_type=F32)
        if n_red == 1:
            o_ref[...] = part.astype(o_ref.dtype)
            return
        acc_ref, = scratch
        r = pl.program_id(2)

        @pl.when(r == 0)
        def _():
            acc_ref[...] = part

        @pl.when(r > 0)
        def _():
            acc_ref[...] += part

        @pl.when(r == n_red - 1)
        def _():
            o_ref[...] = acc_ref[...].astype(o_ref.dtype)

    return pl.pallas_call(
        body, name=name, grid=(g_out, m_dim // tm, n_red),
        in_specs=[pl.BlockSpec(a_blk, a_map), pl.BlockSpec(b_blk, b_map)],
        out_specs=pl.BlockSpec((None, tm, n_dim), lambda g, i, r: (g, i, 0)),
        out_shape=jax.ShapeDtypeStruct((g_out, m_dim, n_dim), out_dtype),
        scratch_shapes=[] if n_red == 1 else [pltpu.VMEM((tm, n_dim), F32)],
        compiler_params=_cparams(("parallel", "parallel", "arbitrary")),
    )(a, b)


def _row_spec(ts, d):
    return pl.BlockSpec((None, ts, d), lambda b, s: (b, s, 0))


def _mod_spec(d):
    return pl.BlockSpec((None, N_MOD, d), lambda b, s: (b, 0, 0))


def _vec_spec(d):
    return pl.BlockSpec((1, d), lambda b, s: (0, 0))


def _bvec_spec(d):
    return pl.BlockSpec((None, 1, d), lambda b, s: (b, 0, 0))


def _modulate(x, mod, sh_row, sc_row, name, ts=512):
    bsz, seq, d = x.shape

    def body(x_ref, mod_ref, o_ref):
        sh = mod_ref[sh_row:sh_row + 1, :]
        sc = mod_ref[sc_row:sc_row + 1, :]
        o_ref[...] = (x_ref[...] * (1.0 + sc) + sh).astype(o_ref.dtype)

    return pl.pallas_call(
        body, name=name, grid=(bsz, seq // ts),
        in_specs=[_row_spec(ts, d), _mod_spec(d)], out_specs=_row_spec(ts, d),
        out_shape=jax.ShapeDtypeStruct((bsz, seq, d), BF16),
        compiler_params=_cparams(("parallel", "parallel")),
    )(x, mod)


def _modulate_bwd(dh, x, mod, dx_res, sc_row, name, ts=512):
    bsz, seq, d = x.shape

    def body(dh_ref, x_ref, mod_ref, dxr_ref, dx_ref, dsh_ref, dsc_ref):
        s = pl.program_id(1)
        sc = mod_ref[sc_row:sc_row + 1, :]
        dh_v = dh_ref[...]
        dx_ref[...] = dxr_ref[...] + dh_v * (1.0 + sc)
        psh = jnp.sum(dh_v, axis=0, keepdims=True)
        psc = jnp.sum(dh_v * x_ref[...], axis=0, keepdims=True)

        @pl.when(s == 0)
        def _():
            dsh_ref[...] = psh
            dsc_ref[...] = psc

        @pl.when(s > 0)
        def _():
            dsh_ref[...] += psh
            dsc_ref[...] += psc

    return pl.pallas_call(
        body, name=name, grid=(bsz, seq // ts),
        in_specs=[_row_spec(ts, d), _row_spec(ts, d), _mod_spec(d), _row_spec(ts, d)],
        out_specs=[_row_spec(ts, d), _bvec_spec(d), _bvec_spec(d)],
        out_shape=[jax.ShapeDtypeStruct((bsz, seq, d), F32), jax.ShapeDtypeStruct((bsz, 1, d), F32),
                   jax.ShapeDtypeStruct((bsz, 1, d), F32)],
        compiler_params=_cparams(("parallel", "arbitrary")),
    )(dh, x, mod, dx_res)


def _res_ln_fn(x, f, g, lng, lnb, cmul):
    r = ALPHA * x + (cmul * (1.0 + g)) * f
    mu = jnp.mean(r, axis=-1, keepdims=True)
    rc = r - mu
    var = jnp.mean(rc * rc, axis=-1, keepdims=True)
    return rc * lax.rsqrt(var + LN_EPS) * lng + lnb


def _res_ln(x, f, mod, lng, lnb, g_row, cmul, name, ts=512):
    bsz, seq, d = x.shape

    def body(x_ref, f_ref, mod_ref, lng_ref, lnb_ref, o_ref):
        g = mod_ref[g_row:g_row + 1, :]
        o_ref[...] = _res_ln_fn(x_ref[...], f_ref[...], g, lng_ref[...], lnb_ref[...], cmul)

    return pl.pallas_call(
        body, name=name, grid=(bsz, seq // ts),
        in_specs=[_row_spec(ts, d), _row_spec(ts, d), _mod_spec(d), _vec_spec(d), _vec_spec(d)],
        out_specs=_row_spec(ts, d), out_shape=jax.ShapeDtypeStruct((bsz, seq, d), F32),
        compiler_params=_cparams(("parallel", "parallel")),
    )(x, f, mod, lng, lnb)


def _res_ln_bwd(dy, x, f, mod, lng, lnb, g_row, cmul, name, ts=256):
    bsz, seq, d = x.shape

    def body(dy_ref, x_ref, f_ref, mod_ref, lng_ref, lnb_ref, dx_ref, df_ref, dg_ref, dlg_ref, dlb_ref):
        b, s = pl.program_id(0), pl.program_id(1)
        g = mod_ref[g_row:g_row + 1, :]
        _, vjp = jax.vjp(functools.partial(_res_ln_fn, cmul=cmul), x_ref[...], f_ref[...], g, lng_ref[...],
                         lnb_ref[...])
        dx, df, dg, dlg, dlb = vjp(dy_ref[...])
        dx_ref[...] = dx
        df_ref[...] = df.astype(df_ref.dtype)

        @pl.when(s == 0)
        def _():
            dg_ref[...] = dg

        @pl.when(s > 0)
        def _():
            dg_ref[...] += dg

        first = jnp.logical_and(b == 0, s == 0)

        @pl.when(first)
        def _():
            dlg_ref[...] = dlg
            dlb_ref[...] = dlb

        @pl.when(jnp.logical_not(first))
        def _():
            dlg_ref[...] += dlg
            dlb_ref[...] += dlb

    return pl.pallas_call(
        body, name=name, grid=(bsz, seq // ts),
        in_specs=[_row_spec(ts, d), _row_spec(ts, d), _row_spec(ts, d), _mod_spec(d), _vec_spec(d), _vec_spec(d)],
        out_specs=[_row_spec(ts, d), _row_spec(ts, d), _bvec_spec(d), _vec_spec(d), _vec_spec(d)],
        out_shape=[jax.ShapeDtypeStruct((bsz, seq, d), F32), jax.ShapeDtypeStruct((bsz, seq, d), BF16),
                   jax.ShapeDtypeStruct((bsz, 1, d), F32), jax.ShapeDtypeStruct((1, d), F32),
                   jax.ShapeDtypeStruct((1, d), F32)],
        compiler_params=_cparams(("arbitrary", "arbitrary")),
    )(dy, x, f, mod, lng, lnb)


def _loss_head(y, target, name, ts=512):
    bsz, seq, d = y.shape
    n_s = seq // ts

    def body(y_ref, t_ref, dy_ref, loss_ref, acc_ref):
        b, s = pl.program_id(0), pl.program_id(1)
        err = y_ref[...] - t_ref[...]
        dy_ref[...] = err * (1.0 / d)
        part = jnp.sum(err * err, axis=0, keepdims=True)
        first = jnp.logical_and(b == 0, s == 0)

        @pl.when(first)
        def _():
            acc_ref[...] = part

        @pl.when(jnp.logical_not(first))
        def _():
            acc_ref[...] += part

        @pl.when(jnp.logical_and(b == bsz - 1, s == n_s - 1))
        def _():
            loss_ref[...] = jnp.sum(acc_ref[...], axis=1, keepdims=True) * (0.5 / d)

    return pl.pallas_call(
        body, name=name, grid=(bsz, n_s),
        in_specs=[_row_spec(ts, d), _row_spec(ts, d)],
        out_specs=[_row_spec(ts, d), pl.BlockSpec((1, 1), lambda b, s: (0, 0))],
        out_shape=[jax.ShapeDtypeStruct((bsz, seq, d), F32), jax.ShapeDtypeStruct((1, 1), F32)],
        scratch_shapes=[pltpu.VMEM((1, d), F32)],
        compiler_params=_cparams(("arbitrary", "arbitrary")),
    )(y, target)


def _ffn_in_swiglu(h, w_in, name, tm=1024):
    t, d = h.shape
    n_sh, _, w = w_in.shape
    half = n_sh // 2

    def body(h_ref, w_ref, z_ref, a_ref):
        hv = h_ref[...]
        g = jnp.dot(hv, w_ref[0], preferred_element_type=F32)
        u = jnp.dot(hv, w_ref[1], preferred_element_type=F32)
        z_ref[0] = g.astype(z_ref.dtype)
        z_ref[1] = u.astype(z_ref.dtype)
        a_ref[...] = (g * jax.nn.sigmoid(g) * u).astype(a_ref.dtype)

    return pl.pallas_call(
        body, name=name, grid=(half, t // tm),
        in_specs=[pl.BlockSpec((tm, d), lambda g, i: (i, 0)),
                  pl.BlockSpec((2, None, d, w), lambda g, i: (0, g, 0, 0))],
        out_specs=[pl.BlockSpec((2, None, tm, w), lambda g, i: (0, g, i, 0)),
                   pl.BlockSpec((None, tm, w), lambda g, i: (g, i, 0))],
        out_shape=[jax.ShapeDtypeStruct((2, half, t, w), BF16), jax.ShapeDtypeStruct((half, t, w), BF16)],
        compiler_params=_cparams(("parallel", "parallel")),
    )(h, w_in.reshape(2, half, d, w))


def _ffn_out_dx_swiglu(df, w_out, z, name, tm=1024):
    t, d = df.shape
    half, w, _ = w_out.shape

    def body(df_ref, w_ref, z_ref, dz_ref):
        da = lax.dot_general(df_ref[...], w_ref[...], _DN["nt"], preferred_element_type=F32)
        g = z_ref[0].astype(F32)
        u = z_ref[1].astype(F32)
        sig = jax.nn.sigmoid(g)
        dz_ref[0] = (da * u * (sig * (1.0 + g * (1.0 - sig)))).astype(dz_ref.dtype)
        dz_ref[1] = (da * (g * sig)).astype(dz_ref.dtype)

    zspec = pl.BlockSpec((2, None, tm, w), lambda g, i: (0, g, i, 0))
    return pl.pallas_call(
        body, name=name, grid=(half, t // tm),
        in_specs=[pl.BlockSpec((tm, d), lambda g, i: (i, 0)), pl.BlockSpec((None, w, d), lambda g, i: (g, 0, 0)),
                  zspec],
        out_specs=zspec, out_shape=jax.ShapeDtypeStruct(z.shape, BF16),
        compiler_params=_cparams(("parallel", "parallel")),
    )(df, w_out, z)


def _log_sigmoid(x):
    return jnp.minimum(x, 0.0) - jnp.log(1.0 + jnp.exp(-jnp.abs(x)))


def _hgrn_consts():
    r = lax.broadcasted_iota(jnp.int32, (GROUP_WIDTH, GROUP_WIDTH), 0)
    c = lax.broadcasted_iota(jnp.int32, (GROUP_WIDTH, GROUP_WIDTH), 1)
    bd = (r // HEAD_DIM == c // HEAD_DIM).astype(F32)
    r16 = lax.broadcasted_iota(jnp.int32, (A_CHUNK, A_CHUNK), 0)
    c16 = lax.broadcasted_iota(jnp.int32, (A_CHUNK, A_CHUNK), 1)
    tril = (r16 >= c16).astype(F32)
    rows = lax.broadcasted_iota(jnp.int32, (A_CHUNK, GROUP_WIDTH), 0)
    return bd, tril, rows


def _hgrn_lb(logits8, layer):
    rows = lax.broadcasted_iota(jnp.int32, logits8.shape, 0)
    valid = rows < DEPTH
    mx = jnp.max(jnp.where(valid, logits8, NEG), axis=0, keepdims=True)
    e = jnp.where(valid, jnp.exp(logits8 - mx), 0.0)
    sm = e / jnp.sum(e, axis=0, keepdims=True)
    pick = jnp.logical_and(rows >= 1, rows <= layer)
    return jnp.sum(jnp.where(pick, sm, 0.0), axis=0, keepdims=True)


def _hgrn_chunk(aq, af, ai, ag, logits8, norm_g, st, *, layer, consts):
    bd, tril, rows = consts
    lb = _hgrn_lb(logits8, layer)
    la = jnp.log(jnp.maximum(lb, LB_FLOOR))
    b2 = jnp.log(1.0 - lb) + _log_sigmoid(af)
    log_f = jnp.maximum(la, b2) + jnp.log(1.0 + jnp.exp(-jnp.abs(la - b2)))
    k = 1.0 - jnp.exp(log_f)
    qf = aq * jax.nn.sigmoid(aq)
    g_cum = jnp.dot(tril, log_f, precision=HI, preferred_element_type=F32)

    c, w = A_CHUNK, GROUP_WIDTH

    def by_key(v):
        return jnp.broadcast_to(v[:, None, :], (c, c, w))

    def by_query(v):
        return jnp.broadcast_to(v[None, :, :], (c, c, w))

    s_i = lax.broadcasted_iota(jnp.int32, (c, c, w), 0)
    t_i = lax.broadcasted_iota(jnp.int32, (c, c, w), 1)
    rel = jnp.where(t_i >= s_i, by_query(g_cum) - by_key(g_cum), NEG)
    pairs = by_query(qf) * by_key(k) * jnp.exp(rel)
    a_all = _bdot(pairs.reshape(c * c, w), bd, "nn").reshape(c, c, w)
    o = jnp.sum(a_all * by_key(ai), axis=0)
    q_dec = qf * jnp.exp(g_cum)
    o = o + _bdot(q_dec, st, "nt")
    g_last = jnp.sum(jnp.where(rows == c - 1, g_cum, 0.0), axis=0, keepdims=True)
    k_end = k * jnp.exp(g_last - g_cum)
    kv = _bdot(ai, k_end, "tn")
    st_new = st * jnp.exp(g_last) + kv * bd
    ms = _bdot(o * o, bd, "nn") * (1.0 / HEAD_DIM)
    o = o * lax.rsqrt(ms + RMS_EPS) * norm_g
    return o * (ag * jax.nn.sigmoid(ag)), st_new


def _hgrn_fwd(proj, logits8, norm_g, layer, name, ts=128):
    bsz, seq, _ = proj.shape
    n_ch = ts // A_CHUNK

    def body(p_ref, lg_ref, ng_ref, o_ref, st_ref, st_scr):
        @pl.when(pl.program_id(1) == 0)
        def _():
            st_scr[...] = jnp.zeros_like(st_scr)

        consts = _hgrn_consts()
        logits_v, ng_v = lg_ref[...], ng_ref[...]

        def chunk(ci, carry):
            r = pl.multiple_of(ci * A_CHUNK, A_CHUNK)
            st = st_scr[...]
            st_ref[ci] = st
            o, st_new = _hgrn_chunk(
                p_ref[pl.ds(r, A_CHUNK), 0:256], p_ref[pl.ds(r, A_CHUNK), 256:512],
                p_ref[pl.ds(r, A_CHUNK), 512:768], p_ref[pl.ds(r, A_CHUNK), 768:1024],
                logits_v, ng_v, st, layer=layer, consts=consts)
            o_ref[pl.ds(r, A_CHUNK), :] = o.astype(o_ref.dtype)
            st_scr[...] = st_new
            return carry

        lax.fori_loop(0, n_ch, chunk, 0, unroll=2)

    return pl.pallas_call(
        body, name=name, grid=(bsz, seq // ts),
        in_specs=[pl.BlockSpec((None, ts, 1024), lambda b, s: (b, s, 0)),
                  pl.BlockSpec((8, GROUP_WIDTH), lambda b, s: (0, 0)),
                  pl.BlockSpec((1, GROUP_WIDTH), lambda b, s: (0, 0))],
        out_specs=[pl.BlockSpec((None, ts, GROUP_WIDTH), lambda b, s: (b, s, 0)),
                   pl.BlockSpec((None, n_ch, GROUP_WIDTH, GROUP_WIDTH), lambda b, s: (b, s, 0, 0))],
        out_shape=[jax.ShapeDtypeStruct((bsz, seq, MO_W), BF16),
                   jax.ShapeDtypeStruct((bsz, seq // A_CHUNK, GROUP_WIDTH, GROUP_WIDTH), F32)],
        scratch_shapes=[pltpu.VMEM((GROUP_WIDTH, GROUP_WIDTH), F32)],
        compiler_params=_cparams(("parallel", "arbitrary")),
    )(proj, logits8, norm_g)


def _hgrn_bwd(dmo, proj, states, logits8, norm_g, layer, name, ts=128):
    bsz, seq, _ = proj.shape
    n_ch = ts // A_CHUNK
    n_s = seq // ts

    def body(do_ref, p_ref, st_ref, lg_ref, ng_ref, dp_ref, dlg_ref, dng_ref, dst_scr):
        b, s = pl.program_id(0), pl.program_id(1)

        @pl.when(s == 0)
        def _():
            dst_scr[...] = jnp.zeros_like(dst_scr)

        @pl.when(jnp.logical_and(b == 0, s == 0))
        def _():
            dlg_ref[...] = jnp.zeros_like(dlg_ref)
            dng_ref[...] = jnp.zeros_like(dng_ref)

        consts = _hgrn_consts()
        logits_v, ng_v = lg_ref[...], ng_ref[...]
        fn = functools.partial(_hgrn_chunk, layer=layer, consts=consts)

        def chunk(t, carry):
            ci = n_ch - 1 - t
            r = pl.multiple_of(ci * A_CHUNK, A_CHUNK)
            _, vjp = jax.vjp(
                fn, p_ref[pl.ds(r, A_CHUNK), 0:256], p_ref[pl.ds(r, A_CHUNK), 256:512],
                p_ref[pl.ds(r, A_CHUNK), 512:768], p_ref[pl.ds(r, A_CHUNK), 768:1024],
                logits_v, ng_v, st_ref[ci])
            daq, daf, dai, dag, dlg, dng, dst = vjp((do_ref[pl.ds(r, A_CHUNK), :], dst_scr[...]))
            dp_ref[pl.ds(r, A_CHUNK), 0:256] = daq.astype(dp_ref.dtype)
            dp_ref[pl.ds(r, A_CHUNK), 256:512] = daf.astype(dp_ref.dtype)
            dp_ref[pl.ds(r, A_CHUNK), 512:768] = dai.astype(dp_ref.dtype)
            dp_ref[pl.ds(r, A_CHUNK), 768:1024] = dag.astype(dp_ref.dtype)
            dlg_ref[...] += dlg
            dng_ref[...] += dng
            dst_scr[...] = dst
            return carry

        lax.fori_loop(0, n_ch, chunk, 0, unroll=2)

    rev = lambda b, s: (b, n_s - 1 - s, 0)
    return pl.pallas_call(
        body, name=name, grid=(bsz, n_s),
        in_specs=[pl.BlockSpec((None, ts, GROUP_WIDTH), rev),
                  pl.BlockSpec((None, ts, 1024), rev),
                  pl.BlockSpec((None, n_ch, GROUP_WIDTH, GROUP_WIDTH), lambda b, s: (b, n_s - 1 - s, 0, 0)),
                  pl.BlockSpec((8, GROUP_WIDTH), lambda b, s: (0, 0)),
                  pl.BlockSpec((1, GROUP_WIDTH), lambda b, s: (0, 0))],
        out_specs=[pl.BlockSpec((None, ts, 1024), rev),
                   pl.BlockSpec((8, GROUP_WIDTH), lambda b, s: (0, 0)),
                   pl.BlockSpec((1, GROUP_WIDTH), lambda b, s: (0, 0))],
        out_shape=[jax.ShapeDtypeStruct((bsz, seq, PACK_W), BF16),
                   jax.ShapeDtypeStruct((8, GROUP_WIDTH), F32), jax.ShapeDtypeStruct((1, GROUP_WIDTH), F32)],
        scratch_shapes=[pltpu.VMEM((GROUP_WIDTH, GROUP_WIDTH), F32)],
        compiler_params=_cparams(("arbitrary", "arbitrary")),
    )(dmo, proj, states, logits8, norm_g)


def _rms_fn(x, g):
    return x * lax.rsqrt(jnp.mean(x * x, axis=-1, keepdims=True) + RMS_EPS) * g


def _tile4(t):
    return jnp.concatenate([t, t, t, t], axis=1)


def _rope(x, c, s1, s2):
    w = x.shape[-1]
    return x * c + pltpu.roll(x, 32, axis=1) * s2 + pltpu.roll(x, w - 32, axis=1) * s1


def _rope_t(dy, c, s1, s2):
    w = dy.shape[-1]
    return dy * c + pltpu.roll(dy * s2, w - 32, axis=1) + pltpu.roll(dy * s1, 32, axis=1)


def _mla_pre(proj, qg, kvg, wq, wkv, tabs, name, ts=256):
    bsz, seq, _ = proj.shape

    def body(p_ref, qg_ref, kvg_ref, wq_ref, wkv_ref, c_ref, s1_ref, s2_ref, q_ref, kv_ref):
        nq = _rms_fn(p_ref[:, 0:256], qg_ref[...])
        nkv = _rms_fn(p_ref[:, 256:384], kvg_ref[...])
        c, s1, s2 = c_ref[...], s1_ref[...], s2_ref[...]
        qp = jnp.dot(nq.astype(BF16), wq_ref[...], preferred_element_type=F32)
        q_ref[...] = _rope(qp, _tile4(c), _tile4(s1), _tile4(s2)).astype(q_ref.dtype)
        kv = jnp.dot(nkv.astype(BF16), wkv_ref[...], preferred_element_type=F32)
        krr = _rope(p_ref[:, 384:512], c, s1, s2)
        zero = jnp.zeros_like(krr)
        kv_ref[...] = (kv + jnp.concatenate([krr, zero] * N_HEADS, axis=1)).astype(kv_ref.dtype)

    tab_spec = pl.BlockSpec((ts, LANES), lambda b, s: (s, 0))
    return pl.pallas_call(
        body, name=name, grid=(bsz, seq // ts),
        in_specs=[pl.BlockSpec((None, ts, 512), lambda b, s: (b, s, P_B // 512)),
                  _vec_spec(256), _vec_spec(128),
                  pl.BlockSpec((256, 512), lambda b, s: (0, 0)), pl.BlockSpec((128, 1024), lambda b, s: (0, 0)),
                  tab_spec, tab_spec, tab_spec],
        out_specs=[_row_spec(ts, 512), _row_spec(ts, 1024)],
        out_shape=[jax.ShapeDtypeStruct((bsz, seq, 512), BF16), jax.ShapeDtypeStruct((bsz, seq, 1024), BF16)],
        compiler_params=_cparams(("parallel", "parallel")),
    )(proj, qg, kvg, wq, wkv, *tabs)


def _mla_pre_bwd(dq, dkv, dproj, proj, qg, kvg, wq, wkv, tabs, name, ts=256):
    bsz, seq, _ = proj.shape

    def body(dq_ref, dkv_ref, dp_any, p_ref, qg_ref, kvg_ref, wq_ref, wkv_ref, c_ref, s1_ref, s2_ref,
             dp_ref, dqg_ref, dkvg_ref, dwq_ref, dwkv_ref):
        del dp_any
        first = jnp.logical_and(pl.program_id(0) == 0, pl.program_id(1) == 0)

        @pl.when(first)
        def _():
            dqg_ref[...] = jnp.zeros_like(dqg_ref)
            dkvg_ref[...] = jnp.zeros_like(dkvg_ref)
            dwq_ref[...] = jnp.zeros_like(dwq_ref)
            dwkv_ref[...] = jnp.zeros_like(dwkv_ref)

        c, s1, s2 = c_ref[...], s1_ref[...], s2_ref[...]
        nq, vjp_q = jax.vjp(_rms_fn, p_ref[:, 0:256], qg_ref[...])
        nkv, vjp_kv = jax.vjp(_rms_fn, p_ref[:, 256:384], kvg_ref[...])
        dqp = _rope_t(dq_ref[...], _tile4(c), _tile4(s1), _tile4(s2)).astype(BF16)
        dkv_v = dkv_ref[...]
        dkv_b = dkv_v.astype(BF16)
        tn = (((0,), (0,)), ((), ()))
        nt = (((1,), (1,)), ((), ()))
        dwq_ref[...] += lax.dot_general(nq.astype(BF16), dqp, tn, preferred_element_type=F32)
        dwkv_ref[...] += lax.dot_general(nkv.astype(BF16), dkv_b, tn, preferred_element_type=F32)
        dcq, dqg = vjp_q(lax.dot_general(dqp, wq_ref[...], nt, preferred_element_type=F32))
        dckv, dkvg = vjp_kv(lax.dot_general(dkv_b, wkv_ref[...], nt, preferred_element_type=F32))
        dqg_ref[...] += dqg
        dkvg_ref[...] += dkvg
        dk_sum = dkv_v[:, 0:128] + dkv_v[:, 256:384] + dkv_v[:, 512:640] + dkv_v[:, 768:896]
        lane = lax.broadcasted_iota(jnp.int32, dk_sum.shape, 1)
        dkr = jnp.where(lane >= 64, _rope_t(dk_sum, c, s1, s2), 0.0)
        dp_ref[:, 0:256] = dcq.astype(dp_ref.dtype)
        dp_ref[:, 256:384] = dckv.astype(dp_ref.dtype)
        dp_ref[:, 384:512] = dkr.astype(dp_ref.dtype)

    tab_spec = pl.BlockSpec((ts, LANES), lambda b, s: (s, 0))
    const = lambda shape: pl.BlockSpec(shape, lambda b, s: (0, 0))
    return pl.pallas_call(
        body, name=name, grid=(bsz, seq // ts),
        in_specs=[_row_spec(ts, 512), _row_spec(ts, 1024), pl.BlockSpec(memory_space=pl.ANY),
                  pl.BlockSpec((None, ts, 512), lambda b, s: (b, s, P_B // 512)),
                  _vec_spec(256), _vec_spec(128), const((256, 512)), const((128, 1024)),
                  tab_spec, tab_spec, tab_spec],
        out_specs=[pl.BlockSpec((None, ts, 512), lambda b, s: (b, s, P_B // 512)),
                   _vec_spec(256), _vec_spec(128), const((256, 512)), const((128, 1024))],
        out_shape=[jax.ShapeDtypeStruct(dproj.shape, dproj.dtype), jax.ShapeDtypeStruct((1, 256), F32),
                   jax.ShapeDtypeStruct((1, 128), F32), jax.ShapeDtypeStruct((256, 512), F32),
                   jax.ShapeDtypeStruct((128, 1024), F32)],
        input_output_aliases={2: 0},
        compiler_params=_cparams(("arbitrary", "arbitrary")),
    )(dq, dkv, dproj, proj, qg, kvg, wq, wkv, *tabs)


def _fox_gate(proj, bf, name):
    bsz, seq, _ = proj.shape
    n_blk = seq // LANES

    def body(x_ref, bf_ref, f_ref):
        r_i = lax.broadcasted_iota(jnp.int32, (LANES, LANES), 0)
        c_i = lax.broadcasted_iota(jnp.int32, (LANES, LANES), 1)
        tril = (r_i >= c_i).astype(F32)
        bias = bf_ref[...]

        def blk(i, carry):
            r = pl.multiple_of(i * LANES, LANES)
            lf = _log_sigmoid(x_ref[pl.ds(r, LANES), :] + bias)
            f_ref[pl.ds(r, LANES), :] = jnp.dot(tril, lf, precision=HI, preferred_element_type=F32) + carry
            return carry + jnp.sum(lf, axis=0, keepdims=True)

        lax.fori_loop(0, n_blk, blk, jnp.zeros((1, LANES), F32))

    return pl.pallas_call(
        body, name=name, grid=(bsz,),
        in_specs=[pl.BlockSpec((None, seq, LANES), lambda b: (b, 0, P_CF // LANES)),
                  pl.BlockSpec((1, LANES), lambda b: (0, 0))],
        out_specs=pl.BlockSpec((None, seq, LANES), lambda b: (b, 0, 0)),
        out_shape=jax.ShapeDtypeStruct((bsz, seq, LANES), F32),
        compiler_params=_cparams(("parallel",)),
    )(proj, bf)


def _fox_gate_bwd(dfq, dfk_cols, dproj, proj, bf, name):
    bsz, seq, _ = proj.shape
    n_blk = seq // LANES

    def body(dfq_ref, dfk_ref, dp_any, x_ref, bf_ref, dp_ref, dbf_ref):
        del dp_any

        @pl.when(pl.program_id(0) == 0)
        def _():
            dbf_ref[...] = jnp.zeros_like(dbf_ref)

        r_i = lax.broadcasted_iota(jnp.int32, (LANES, LANES), 0)
        c_i = lax.broadcasted_iota(jnp.int32, (LANES, LANES), 1)
        triu = (r_i <= c_i).astype(F32)
        bias = bf_ref[...]

        def blk(t, carry):
            tail, dbf = carry
            r = pl.multiple_of((n_blk - 1 - t) * LANES, LANES)
            dc = dfk_ref[pl.ds(r, LANES), :]
            for hd in range(N_HEADS):
                dc = dc + jnp.where(c_i == hd, dfq_ref[hd, pl.ds(r, LANES), :], 0.0)
            dlf = jnp.dot(triu, dc, precision=HI, preferred_element_type=F32) + tail
            dx = dlf * (1.0 - jax.nn.sigmoid(x_ref[pl.ds(r, LANES), :] + bias))
            dp_ref[pl.ds(r, LANES), :] = dx.astype(dp_ref.dtype)
            return tail + jnp.sum(dc, axis=0, keepdims=True), dbf + jnp.sum(dx, axis=0, keepdims=True)

        z = jnp.zeros((1, LANES), F32)
        _, dbf = lax.fori_loop(0, n_blk, blk, (z, z))
        dbf_ref[...] += dbf

    return pl.pallas_call(
        body, name=name, grid=(bsz,),
        in_specs=[pl.BlockSpec((None, N_HEADS, seq, LANES), lambda b: (b, 0, 0, 0)),
                  pl.BlockSpec((None, seq, LANES), lambda b: (b, 0, 0)), pl.BlockSpec(memory_space=pl.ANY),
                  pl.BlockSpec((None, seq, LANES), lambda b: (b, 0, P_CF // LANES)),
                  pl.BlockSpec((1, LANES), lambda b: (0, 0))],
        out_specs=[pl.BlockSpec((None, seq, LANES), lambda b: (b, 0, P_CF // LANES)),
                   pl.BlockSpec((1, LANES), lambda b: (0, 0))],
        out_shape=[jax.ShapeDtypeStruct(dproj.shape, dproj.dtype), jax.ShapeDtypeStruct((1, LANES), F32)],
        input_output_aliases={2: 0},
        compiler_params=_cparams(("arbitrary",)),
    )(dfq, dfk_cols, dproj, proj, bf)


def _gate_terms(fc_ref, fr_ref, h, tq, tk):
    lane = lax.broadcasted_iota(jnp.int32, (tq, LANES), 1)
    fcol = jnp.sum(jnp.where(lane == h, fc_ref[...], 0.0), axis=1, keepdims=True)
    sub = lax.broadcasted_iota(jnp.int32, (8, tk), 0)
    frow = jnp.sum(jnp.where(sub == h, fr_ref[...], 0.0), axis=0, keepdims=True)
    return fcol - frow


def _scores(q_ref, k_ref, gate_refs, scale, h, masked, tq, tk):
    q = (q_ref[...].astype(F32) * scale).astype(BF16)
    s = lax.dot_general(q, k_ref[...].astype(BF16), _DN["nt"], preferred_element_type=F32)
    if gate_refs is not None:
        s = s + _gate_terms(gate_refs[0], gate_refs[1], h, tq, tk)
    if masked:
        r_i = lax.broadcasted_iota(jnp.int32, (tq, tk), 0)
        c_i = lax.broadcasted_iota(jnp.int32, (tq, tk), 1)
        s = jnp.where(c_i <= r_i, s, NEG)
    return s, q


def _lanes(col):
    return jnp.broadcast_to(col, (col.shape[0], LANES))


def _attn_fwd(qa, q0, kva, kv0, mo, o0, gates, scale, name, tq=None):
    bsz, seq, _ = qa.shape
    tq = ATTN_TILE if tq is None else tq
    n_q = seq // tq
    gated = gates is not None

    def body(*refs):
        q_ref, k_ref, v_ref = refs[:3]
        gate_refs = refs[3:5] if gated else None
        o_ref, lse_ref, m_s, l_s, acc_s = refs[-5:]
        h, i, j = pl.program_id(1), pl.program_id(2), pl.program_id(3)

        @pl.when(j == 0)
        def _():
            m_s[...] = jnp.full_like(m_s, NEG)
            l_s[...] = jnp.zeros_like(l_s)
            acc_s[...] = jnp.zeros_like(acc_s)

        def step(masked):
            s, _ = _scores(q_ref, k_ref, gate_refs, scale, h, masked, tq, tq)
            m_prev = m_s[...]
            m_new = jnp.maximum(m_prev, jnp.max(s, axis=1, keepdims=True))
            alpha = jnp.exp(m_prev - m_new)
            p = jnp.exp(s - m_new)
            l_s[...] = alpha * l_s[...] + jnp.sum(p, axis=1, keepdims=True)
            acc_s[...] = alpha * acc_s[...] + jnp.dot(p.astype(BF16), v_ref[...].astype(BF16),
                                                      preferred_element_type=F32)
            m_s[...] = m_new

        @pl.when(j < i)
        def _():
            step(False)

        @pl.when(j == i)
        def _():
            step(True)
            o_ref[...] = (acc_s[...] / l_s[...]).astype(o_ref.dtype)
            lse_ref[...] = _lanes(m_s[...] + jnp.log(l_s[...]))

    blk = (None, tq, LANES)
    in_specs = [pl.BlockSpec(blk, lambda b, h, i, j: (b, i, q0 + h)),
                pl.BlockSpec(blk, lambda b, h, i, j: (b, jnp.minimum(j, i), kv0 + 2 * h)),
                pl.BlockSpec(blk, lambda b, h, i, j: (b, jnp.minimum(j, i), kv0 + 2 * h + 1))]
    args = [qa, kva, kva]
    if gated:
        in_specs += [pl.BlockSpec(blk, lambda b, h, i, j: (b, i, 0)),
                     pl.BlockSpec((None, 8, tq), lambda b, h, i, j: (b, 0, jnp.minimum(j, i)))]
        args += list(gates)
    in_specs.append(pl.BlockSpec(memory_space=pl.ANY))
    args.append(mo)
    return pl.pallas_call(
        body, name=name, grid=(bsz, N_HEADS, n_q, n_q), in_specs=in_specs,
        out_specs=[pl.BlockSpec(blk, lambda b, h, i, j: (b, i, o0 + h)),
                   pl.BlockSpec((None, None, tq, LANES), lambda b, h, i, j: (b, h, i, 0))],
        out_shape=[jax.ShapeDtypeStruct(mo.shape, mo.dtype),
                   jax.ShapeDtypeStruct((bsz, N_HEADS, seq, LANES), F32)],
        scratch_shapes=[pltpu.VMEM((tq, 1), F32), pltpu.VMEM((tq, 1), F32), pltpu.VMEM((tq, LANES), F32)],
        input_output_aliases={len(args) - 1: 0},
        compiler_params=_cparams(("parallel", "parallel", "parallel", "arbitrary")),
    )(*args)


def _attn_bwd_q(qa, q0, kva, kv0, mo, dmo, o0, lse, gates, scale, out, out0, name, tq=None):
    bsz, seq, _ = qa.shape
    tq = ATTN_TILE if tq is None else tq
    n_q = seq // tq
    gated = gates is not None
    aliased = not isinstance(out, jax.ShapeDtypeStruct)

    def body(*refs):
        q_ref, k_ref, v_ref, o_ref, do_ref, lse_ref = refs[:6]
        gate_refs = refs[6:8] if gated else None
        dq_ref, delta_ref, dfq_ref, acc_s, dl_s, df_s = refs[-6:]
        h, i, j = pl.program_id(1), pl.program_id(2), pl.program_id(3)

        @pl.when(j == 0)
        def _():
            acc_s[...] = jnp.zeros_like(acc_s)
            df_s[...] = jnp.zeros_like(df_s)
            dl_s[...] = jnp.sum(do_ref[...] * o_ref[...].astype(F32), axis=1, keepdims=True)

        def step(masked):
            s, _ = _scores(q_ref, k_ref, gate_refs, scale, h, masked, tq, tq)
            p = jnp.exp(s - lse_ref[:, 0:1])
            dp = lax.dot_general(do_ref[...].astype(BF16), v_ref[...].astype(BF16), _DN["nt"],
                                 preferred_element_type=F32)
            ds = p * (dp - dl_s[...])
            acc_s[...] += jnp.dot(ds.astype(BF16), k_ref[...].astype(BF16), preferred_element_type=F32)
            df_s[...] += jnp.sum(ds, axis=1, keepdims=True)

        @pl.when(j < i)
        def _():
            step(False)

        @pl.when(j == i)
        def _():
            step(True)
            dq_ref[...] = (acc_s[...] * scale).astype(dq_ref.dtype)
            delta_ref[...] = _lanes(dl_s[...])
            dfq_ref[...] = _lanes(df_s[...])

    blk = (None, tq, LANES)
    col = pl.BlockSpec((None, None, tq, LANES), lambda b, h, i, j: (b, h, i, 0))
    in_specs = [pl.BlockSpec(blk, lambda b, h, i, j: (b, i, q0 + h)),
                pl.BlockSpec(blk, lambda b, h, i, j: (b, jnp.minimum(j, i), kv0 + 2 * h)),
                pl.BlockSpec(blk, lambda b, h, i, j: (b, jnp.minimum(j, i), kv0 + 2 * h + 1)),
                pl.BlockSpec(blk, lambda b, h, i, j: (b, i, o0 + h)),
                pl.BlockSpec(blk, lambda b, h, i, j: (b, i, o0 + h)), col]
    args = [qa, kva, kva, mo, dmo, lse]
    if gated:
        in_specs += [pl.BlockSpec(blk, lambda b, h, i, j: (b, i, 0)),
                     pl.BlockSpec((None, 8, tq), lambda b, h, i, j: (b, 0, jnp.minimum(j, i)))]
        args += list(gates)
    aliases = {}
    if aliased:
        in_specs.append(pl.BlockSpec(memory_space=pl.ANY))
        args.append(out)
        aliases = {len(args) - 1: 0}
    vec = jax.ShapeDtypeStruct((bsz, N_HEADS, seq, LANES), F32)
    return pl.pallas_call(
        body, name=name, grid=(bsz, N_HEADS, n_q, n_q), in_specs=in_specs,
        out_specs=[pl.BlockSpec(blk, lambda b, h, i, j: (b, i, out0 + h)), col, col],
        out_shape=[jax.ShapeDtypeStruct(out.shape, out.dtype), vec, vec],
        scratch_shapes=[pltpu.VMEM((tq, LANES), F32), pltpu.VMEM((tq, 1), F32), pltpu.VMEM((tq, 1), F32)],
        input_output_aliases=aliases,
        compiler_params=_cparams(("parallel", "parallel", "parallel", "arbitrary")),
    )(*args)


def _attn_bwd_kv(qa, q0, kva, kv0, dmo, o0, lse, delta, gates, scale, out, out0, name, tq=None):
    bsz, seq, _ = qa.shape
    tq = ATTN_TILE if tq is None else tq
    n_q = seq // tq
    gated = gates is not None
    aliased = not isinstance(out, jax.ShapeDtypeStruct)

    def body(*refs):
        q_ref, k_ref, v_ref, do_ref, lse_ref, dl_ref = refs[:6]
        gate_refs = refs[6:8] if gated else None
        dkv_ref, dfk_ref, dk_s, dv_s, df_s = refs[-5:]
        h, j, i = pl.program_id(1), pl.program_id(2), pl.program_id(3)

        @pl.when(i == 0)
        def _():
            dk_s[...] = jnp.zeros_like(dk_s)
            dv_s[...] = jnp.zeros_like(dv_s)
            df_s[...] = jnp.zeros_like(df_s)

        def step(masked):
            s, q = _scores(q_ref, k_ref, gate_refs, scale, h, masked, tq, tq)
            p = jnp.exp(s - lse_ref[:, 0:1])
            do_b = do_ref[...].astype(BF16)
            dp = lax.dot_general(do_b, v_ref[...].astype(BF16), _DN["nt"], preferred_element_type=F32)
            ds = p * (dp - dl_ref[:, 0:1])
            dv_s[...] += lax.dot_general(p.astype(BF16), do_b, _DN["tn"], preferred_element_type=F32)
            dk_s[...] += lax.dot_general(ds.astype(BF16), q, _DN["tn"], preferred_element_type=F32)
            df_s[...] -= jnp.sum(ds, axis=0, keepdims=True)

        @pl.when(i > j)
        def _():
            step(False)

        @pl.when(i == j)
        def _():
            step(True)

        @pl.when(i == n_q - 1)
        def _():
            dkv_ref[:, 0:LANES] = dk_s[...].astype(dkv_ref.dtype)
            dkv_ref[:, LANES:2 * LANES] = dv_s[...].astype(dkv_ref.dtype)
            dfk_ref[...] = df_s[...]

    blk = (None, tq, LANES)
    col = pl.BlockSpec((None, None, tq, LANES), lambda b, h, j, i: (b, h, jnp.maximum(i, j), 0))
    in_specs = [pl.BlockSpec(blk, lambda b, h, j, i: (b, jnp.maximum(i, j), q0 + h)),
                pl.BlockSpec(blk, lambda b, h, j, i: (b, j, kv0 + 2 * h)),
                pl.BlockSpec(blk, lambda b, h, j, i: (b, j, kv0 + 2 * h + 1)),
                pl.BlockSpec(blk, lambda b, h, j, i: (b, jnp.maximum(i, j), o0 + h)), col, col]
    args = [qa, kva, kva, dmo, lse, delta]
    if gated:
        in_specs += [pl.BlockSpec(blk, lambda b, h, j, i: (b, jnp.maximum(i, j), 0)),
                     pl.BlockSpec((None, 8, tq), lambda b, h, j, i: (b, 0, j))]
        args += list(gates)
    aliases = {}
    if aliased:
        in_specs.append(pl.BlockSpec(memory_space=pl.ANY))
        args.append(out)
        aliases = {len(args) - 1: 0}
    return pl.pallas_call(
        body, name=name, grid=(bsz, N_HEADS, n_q, n_q), in_specs=in_specs,
        out_specs=[pl.BlockSpec((None, tq, 2 * LANES), lambda b, h, j, i: (b, j, out0 + h)),
                   pl.BlockSpec((None, None, 1, tq), lambda b, h, j, i: (b, h, 0, j))],
        out_shape=[jax.ShapeDtypeStruct(out.shape, out.dtype), jax.ShapeDtypeStruct((bsz, N_HEADS, 1, seq), F32)],
        scratch_shapes=[pltpu.VMEM((tq, LANES), F32), pltpu.VMEM((tq, LANES), F32), pltpu.VMEM((1, tq), F32)],
        input_output_aliases=aliases,
        compiler_params=_cparams(("parallel", "parallel", "parallel", "arbitrary")),
    )(*args)


def _gmlp_fn(uv, lng, lnb, ws, bst):
    u = jax.nn.gelu(uv[:, 0:GROUP_WIDTH])
    gv = jax.nn.gelu(uv[:, GROUP_WIDTH:2 * GROUP_WIDTH])
    mu = jnp.mean(gv, axis=-1, keepdims=True)
    vc = gv - mu
    var = jnp.mean(vc * vc, axis=-1, keepdims=True)
    vln = vc * lax.rsqrt(var + LN_EPS) * lng + lnb
    r_i = lax.broadcasted_iota(jnp.int32, (D_CHUNK, D_CHUNK), 0)
    c_i = lax.broadcasted_iota(jnp.int32, (D_CHUNK, D_CHUNK), 1)
    lane_g = lax.broadcasted_iota(jnp.int32, (D_CHUNK, GROUP_WIDTH), 1) // HEAD_DIM
    e_r = lax.broadcasted_iota(jnp.int32, (LANES, GROUP_WIDTH), 0)
    e_c = lax.broadcasted_iota(jnp.int32, (LANES, GROUP_WIDTH), 1)
    expand = (e_r == e_c // HEAD_DIM).astype(F32)
    mixed = jnp.dot(bst, expand, precision=HI, preferred_element_type=F32)
    for g in range(4):
        w = jnp.where(r_i >= c_i, ws[g], 0.0)
        mixed = mixed + jnp.where(lane_g == g, _bdot(w, vln, "nn"), 0.0)
    return u * mixed


def _gmlp_fwd(proj, mo, lng, lnb, ws, bst, name):
    bsz, seq, _ = proj.shape

    def body(p_ref, mo_any, lng_ref, lnb_ref, ws_ref, bst_ref, o_ref):
        del mo_any
        o_ref[...] = _gmlp_fn(p_ref[...], lng_ref[...], lnb_ref[...], ws_ref[...], bst_ref[...]).astype(o_ref.dtype)

    return pl.pallas_call(
        body, name=name, grid=(bsz, seq // D_CHUNK),
        in_specs=[pl.BlockSpec((None, D_CHUNK, 512), lambda b, s: (b, s, P_D // 512)),
                  pl.BlockSpec(memory_space=pl.ANY), _vec_spec(256), _vec_spec(256),
                  pl.BlockSpec((4, D_CHUNK, D_CHUNK), lambda b, s: (0, 0, 0)),
                  pl.BlockSpec((D_CHUNK, LANES), lambda b, s: (0, 0))],
        out_specs=pl.BlockSpec((None, D_CHUNK, GROUP_WIDTH), lambda b, s: (b, s, 1280 // GROUP_WIDTH)),
        out_shape=jax.ShapeDtypeStruct(mo.shape, mo.dtype),
        input_output_aliases={1: 0},
        compiler_params=_cparams(("parallel", "parallel")),
    )(proj, mo, lng, lnb, ws, bst)


def _gmlp_bwd(dmo, dproj, proj, lng, lnb, ws, bst, name):
    bsz, seq, _ = proj.shape

    def body(do_ref, dp_any, p_ref, lng_ref, lnb_ref, ws_ref, bst_ref, dp_ref, dlg_ref, dlb_ref, dws_ref, dbst_ref):
        del dp_any
        first = jnp.logical_and(pl.program_id(0) == 0, pl.program_id(1) == 0)

        @pl.when(first)
        def _():
            dlg_ref[...] = jnp.zeros_like(dlg_ref)
            dlb_ref[...] = jnp.zeros_like(dlb_ref)
            dws_ref[...] = jnp.zeros_like(dws_ref)
            dbst_ref[...] = jnp.zeros_like(dbst_ref)

        _, vjp = jax.vjp(_gmlp_fn, p_ref[...], lng_ref[...], lnb_ref[...], ws_ref[...], bst_ref[...])
        duv, dlg, dlb, dws, dbst = vjp(do_ref[...])
        dp_ref[...] = duv.astype(dp_ref.dtype)
        dlg_ref[...] += dlg
        dlb_ref[...] += dlb
        dws_ref[...] += dws
        dbst_ref[...] += dbst

    const2 = lambda shape: pl.BlockSpec(shape, lambda b, s: (0,) * len(shape))
    return pl.pallas_call(
        body, name=name, grid=(bsz, seq // D_CHUNK),
        in_specs=[pl.BlockSpec((None, D_CHUNK, GROUP_WIDTH), lambda b, s: (b, s, 1280 // GROUP_WIDTH)),
                  pl.BlockSpec(memory_space=pl.ANY),
                  pl.BlockSpec((None, D_CHUNK, 512), lambda b, s: (b, s, P_D // 512)),
                  _vec_spec(256), _vec_spec(256), const2((4, D_CHUNK, D_CHUNK)), const2((D_CHUNK, LANES))],
        out_specs=[pl.BlockSpec((None, D_CHUNK, 512), lambda b, s: (b, s, P_D // 512)),
                   _vec_spec(256), _vec_spec(256), const2((4, D_CHUNK, D_CHUNK)), const2((D_CHUNK, LANES))],
        out_shape=[jax.ShapeDtypeStruct(dproj.shape, dproj.dtype), jax.ShapeDtypeStruct((1, 256), F32),
                   jax.ShapeDtypeStruct((1, 256), F32), jax.ShapeDtypeStruct((4, D_CHUNK, D_CHUNK), F32),
                   jax.ShapeDtypeStruct((D_CHUNK, LANES), F32)],
        input_output_aliases={1: 0},
        compiler_params=_cparams(("arbitrary", "arbitrary")),
    )(dmo, dproj, proj, lng, lnb, ws, bst)


def _ada_fwd(c_all, ada_w, name):
    n_b = c_all.shape[0]
    depth, d, cols = ada_w.shape

    def body(c_ref, w_ref, o_ref):
        cv = c_ref[...]
        act = (cv * jax.nn.sigmoid(cv)).astype(BF16)
        o_ref[...] = jnp.dot(act, w_ref[...].astype(BF16), preferred_element_type=F32)

    return pl.pallas_call(
        body, name=name, grid=(depth,),
        in_specs=[pl.BlockSpec((n_b, d), lambda l: (0, 0)), pl.BlockSpec((None, d, cols), lambda l: (l, 0, 0))],
        out_specs=pl.BlockSpec((None, n_b, cols), lambda l: (l, 0, 0)),
        out_shape=jax.ShapeDtypeStruct((depth, n_b, cols), F32),
        compiler_params=_cparams(("parallel",)),
    )(c_all, ada_w)


def _ada_bwd(c_all, dmod_cols, dmod_full, name):
    n_b, d = c_all.shape
    depth, _, cols = dmod_cols.shape
    full = dmod_full.shape[-1]

    def body(c_ref, dm_ref, df_ref, gw_ref, gb_ref):
        cv = c_ref[...]
        act = (cv * jax.nn.sigmoid(cv)).astype(BF16)
        gw_ref[...] = lax.dot_general(act, dm_ref[...].astype(BF16), (((0,), (0,)), ((), ())),
                                      preferred_element_type=F32)
        gb_ref[...] = jnp.sum(df_ref[...], axis=0, keepdims=True)

    return pl.pallas_call(
        body, name=name, grid=(depth,),
        in_specs=[pl.BlockSpec((n_b, d), lambda l: (0, 0)), pl.BlockSpec((None, n_b, cols), lambda l: (l, 0, 0)),
                  pl.BlockSpec((None, n_b, full), lambda l: (l, 0, 0))],
        out_specs=[pl.BlockSpec((None, d, cols), lambda l: (l, 0, 0)),
                   pl.BlockSpec((None, 1, full), lambda l: (l, 0, 0))],
        out_shape=[jax.ShapeDtypeStruct((depth, d, cols), F32), jax.ShapeDtypeStruct((depth, 1, full), F32)],
        compiler_params=_cparams(("parallel",)),
    )(c_all, dmod_cols, dmod_full)


def _adamw(gparts, own, w, m, v, name, layer=0, prev=None):
    n_p, rows, cols = gparts.shape
    tr = rows
    if rows > 512:
        tr = next(c for c in range(512, 7, -8) if rows % c == 0)
    off = layer * (rows // tr)
    has_own = own is not None
    n_prev = 0 if prev is None else 4

    def body(*refs):
        g_ref = refs[0]
        own_ref = refs[1] if has_own else None
        w_ref, m_ref, v_ref = refs[1 + has_own:4 + has_own]
        go_ref, do_ref, mo_ref, vo_ref = refs[4 + has_own + n_prev:]
        if has_own:
            g = own_ref[...].astype(F32) + g_ref[0].astype(F32)
        else:
            g = g_ref[0].astype(F32)
        for p in range(1, n_p):
            g = g + g_ref[p].astype(F32)
        m_new = ADAM_B1 * m_ref[...] + (1.0 - ADAM_B1) * g
        v_new = ADAM_B2 * v_ref[...] + (1.0 - ADAM_B2) * (g * g)
        m_hat = m_new / (1.0 - ADAM_B1 ** ADAM_STEP)
        v_hat = v_new / (1.0 - ADAM_B2 ** ADAM_STEP)
        go_ref[...] = g
        do_ref[...] = -ADAM_LR * (m_hat / (jnp.sqrt(v_hat) + ADAM_EPS) + ADAM_WD * w_ref[...])
        mo_ref[...] = m_new
        vo_ref[...] = v_new

    spec = pl.BlockSpec((tr, cols), lambda i: (off + i, 0))
    in_specs = [pl.BlockSpec((n_p, tr, cols), lambda i: (0, i, 0))]
    args = [gparts]
    if has_own:
        in_specs.append(pl.BlockSpec((tr, cols), lambda i: (i, 0)))
        args.append(own)
    in_specs += [spec, spec, spec]
    args += [w, m, v]
    aliases = {}
    if prev is not None:
        aliases = {len(args) + k: k for k in range(4)}
        in_specs += [pl.BlockSpec(memory_space=pl.ANY)] * 4
        args += list(prev)
    shp = jax.ShapeDtypeStruct(w.shape, F32)
    return pl.pallas_call(
        body, name=name, grid=(rows // tr,), in_specs=in_specs,
        out_specs=[spec, spec, spec, spec], out_shape=[shp, shp, shp, shp], input_output_aliases=aliases,
        compiler_params=_cparams(("parallel",)),
    )(*args)


def _sum_parts(parts, name):
    n_p, rows, cols = parts.shape
    tr = 256 if rows % 256 == 0 else rows

    def body(p_ref, o_ref):
        acc = p_ref[0]
        for p in range(1, n_p):
            acc = acc + p_ref[p]
        o_ref[...] = acc

    return pl.pallas_call(
        body, name=name, grid=(rows // tr,),
        in_specs=[pl.BlockSpec((n_p, tr, cols), lambda i: (0, i, 0))],
        out_specs=pl.BlockSpec((tr, cols), lambda i: (i, 0)),
        out_shape=jax.ShapeDtypeStruct((rows, cols), F32),
        compiler_params=_cparams(("parallel",)),
    )(parts)


def _all_gather(arrs, name):
    n = len(arrs)

    def body(*refs):
        in_refs, out_refs = refs[:n], refs[n:2 * n]
        send_sems, recv_sems, loc_sems = refs[2 * n:]
        x, y, c = lax.axis_index("x"), lax.axis_index("y"), lax.axis_index("c")
        me, sibling = (x, y, c), (x, y, 1 - c)
        chips = [(1 - x, y), (x, 1 - y), (1 - x, 1 - y)]

        def copy(a, k, block, to, src=None):
            slot = out_refs[a].at[4 * block[0] + 2 * block[1] + block[2]]
            return pltpu.make_async_remote_copy(
                src_ref=slot if src is None else src, dst_ref=slot, send_sem=send_sems.at[a, k],
                recv_sem=recv_sems.at[a, k], device_id=to, device_id_type=pl.DeviceIdType.MESH)

        mine = [pltpu.make_async_copy(in_refs[a], out_refs[a].at[4 * x + 2 * y + c], loc_sems.at[a])
                for a in range(n)]
        for cp in mine:
            cp.start()
        first = []
        for a in range(n):
            first.append(copy(a, 0, me, sibling, src=in_refs[a]))
            first += [copy(a, 1 + j, me, (*chip, c), src=in_refs[a]) for j, chip in enumerate(chips)]
        for cp in first:
            cp.start()
        passed = []
        for j, chip in enumerate(chips):
            for a in range(n):
                copy(a, 1 + j, (*chip, c), me).wait_recv()
                cp = copy(a, 4 + j, (*chip, c), sibling)
                cp.start()
                passed.append(cp)
        for a in range(n):
            copy(a, 0, sibling, me).wait_recv()
        for j, chip in enumerate(chips):
            for a in range(n):
                copy(a, 4 + j, (*chip, 1 - c), me).wait_recv()
        for cp in first + passed:
            cp.wait_send()
        for cp in mine:
            cp.wait()

    any_spec = pl.BlockSpec(memory_space=pl.ANY)
    return pl.pallas_call(
        body, name=name, in_specs=[any_spec] * n, out_specs=[any_spec] * n,
        out_shape=[jax.ShapeDtypeStruct((N_DEV,) + a.shape, a.dtype) for a in arrs],
        scratch_shapes=[pltpu.SemaphoreType.DMA((n, N_DEV - 1)), pltpu.SemaphoreType.DMA((n, N_DEV - 1)),
                        pltpu.SemaphoreType.DMA((n,))],
    )(*arrs)


def _flip_peers():
    x, y, c = lax.axis_index("x"), lax.axis_index("y"), lax.axis_index("c")
    peers = []
    for fx, fy, fc in [(fx, fy, fc) for fx in (0, 1) for fy in (0, 1) for fc in (0, 1)][1:]:
        px, py, pc = (1 - x if fx else x), (1 - y if fy else y), (1 - c if fc else c)
        peers.append(((px, py, pc), 4 * px + 2 * py + pc))
    return 4 * x + 2 * y + c, peers


def _push_start(srcs, name, whole=False):
    n, n_peer = len(srcs), N_DEV - 1
    if whole:
        me_w = 4 * lax.axis_index("x") + 2 * lax.axis_index("y") + lax.axis_index("c")
        lands = [lax.dynamic_update_slice_in_dim(jnp.zeros((N_DEV,) + a.shape, a.dtype), a[None], me_w, axis=0)
                 for a in srcs]
    else:
        lands = [jnp.zeros(a.shape, a.dtype) for a in srcs]

    def body(*refs):
        src_refs, land_refs = refs[:n], refs[n:2 * n]
        send_sems, recv_sems = refs[2 * n], refs[2 * n + 1]
        token = refs[-1]
        me, peers = _flip_peers()
        for k, (dev, idx) in enumerate(peers):
            for a in range(n):
                pltpu.make_async_remote_copy(
                    src_ref=src_refs[a] if whole else src_refs[a].at[idx], dst_ref=land_refs[a].at[me],
                    send_sem=send_sems.at[a * n_peer + k], recv_sem=recv_sems.at[a * n_peer + k], device_id=dev,
                    device_id_type=pl.DeviceIdType.MESH).start()
        token[...] = jnp.zeros_like(token)

    hbm = pl.BlockSpec(memory_space=pltpu.HBM)
    sem = pl.BlockSpec(memory_space=pltpu.SEMAPHORE)
    arrs = list(srcs) + lands
    res = pl.pallas_call(
        body, name=name, in_specs=[hbm] * (2 * n),
        out_specs=(sem, sem, *[hbm] * (2 * n), pl.BlockSpec(memory_space=pltpu.VMEM)),
        out_shape=(pltpu.SemaphoreType.DMA((n * n_peer,)), pltpu.SemaphoreType.DMA((n * n_peer,)),
                   *[pltpu.HBM(a.shape, a.dtype) for a in arrs], jax.ShapeDtypeStruct((8, LANES), F32)),
        input_output_aliases={i: 2 + i for i in range(2 * n)},
        compiler_params=pltpu.CompilerParams(has_side_effects=pltpu.SideEffectType.DATAFLOW_SIDE_EFFECTING),
    )(*[pltpu.with_memory_space_constraint(a, pltpu.HBM) for a in arrs])
    return res[0], res[1], list(res[2:2 + n]), list(res[2 + n:2 + 2 * n]), res[-1]


def _push_wait(send_sems, recv_sems, srcs, lands, after, name, whole=False):
    n, n_peer = len(srcs), N_DEV - 1

    def body(*refs):
        src_refs, land_refs = refs[:n], refs[n:2 * n]
        send_s, recv_s = refs[2 * n], refs[2 * n + 1]
        _, peers = _flip_peers()
        for k, (dev, idx) in enumerate(peers):
            for a in range(n):
                cp = pltpu.make_async_remote_copy(
                    src_ref=src_refs[a] if whole else src_refs[a].at[idx], dst_ref=land_refs[a].at[idx],
                    send_sem=send_s.at[a * n_peer + k],
                    recv_sem=recv_s.at[a * n_peer + k], device_id=dev, device_id_type=pl.DeviceIdType.MESH)
                cp.wait_send()
                cp.wait_recv()

    hbm = pl.BlockSpec(memory_space=pltpu.HBM)
    sem = pl.BlockSpec(memory_space=pltpu.SEMAPHORE)
    arrs = list(srcs) + list(lands)
    res = pl.pallas_call(
        body, name=name, in_specs=[hbm] * (2 * n) + [sem, sem, pl.BlockSpec(memory_space=pl.ANY)],
        out_specs=tuple([hbm] * (2 * n)), out_shape=tuple(pltpu.HBM(a.shape, a.dtype) for a in arrs),
        input_output_aliases={i: i for i in range(2 * n)},
        compiler_params=pltpu.CompilerParams(has_side_effects=pltpu.SideEffectType.DATAFLOW_SIDE_EFFECTING),
    )(*arrs, send_sems, recv_sems, after)
    return list(res[:n]), list(res[n:])


def _ffn_fwd(x, mod, w_in, w_out, lng, lnb, rows, tag):
    bsz, seq, d = x.shape
    t = bsz * seq
    h = _modulate(x, mod, rows[0], rows[1], f"modulate_{tag}")
    z, a = _ffn_in_swiglu(h.reshape(t, d), w_in, f"ffn_in_{tag}")
    f = _matmul(a, w_out, mode="nn", group_out=False, out_dtype=F32, tm=1024, tk=a.shape[2],
                name=f"ffn_out_{tag}").reshape(bsz, seq, d)
    y = _res_ln(x, f, mod, lng, lnb, rows[2], 0.5, f"res_ln_{tag}")
    return y, (x, h, z, a, f)


def _tied(mod, tie):
    return mod if tie is None else mod + tie


def _ffn_bwd(dy, saved, mod, w_in, w_out, lng, lnb, rows, tag, ready):
    x, h, z, a, f = saved
    bsz, seq, d = x.shape
    t = bsz * seq
    dx_res, df, dgate, dlg, dlb = _res_ln_bwd(dy, x, f, mod, lng, lnb, rows[2], 0.5, f"res_ln_bwd_{tag}")
    df2 = df.reshape(1, t, d)
    dw_out = _matmul(a, df2, mode="tn", group_out=True, out_dtype=BF16, tm=a.shape[2], tk=min(t, 2048),
                     name=f"ffn_out_dw_{tag}")
    tie_out = ready(f"{tag}_out", dw_out)
    dz = _ffn_out_dx_swiglu(df.reshape(t, d), w_out, z, f"ffn_out_dx_{tag}").reshape(N_DEV, t, -1)
    dw_in = _matmul(h.reshape(1, t, d), dz, mode="tn", group_out=True, out_dtype=BF16, tm=d, tk=min(t, 2048),
                    name=f"ffn_in_dw_{tag}")
    tie_in = ready(f"{tag}_in", dw_in)
    dh = _matmul(dz, w_in, mode="nt", group_out=False, out_dtype=F32, tm=1024, tk=dz.shape[2],
                 name=f"ffn_in_dx_{tag}").reshape(bsz, seq, d)
    dx, dsh, dsc = _modulate_bwd(dh, x, _tied(_tied(mod, tie_out), tie_in), dx_res, rows[1],
                                 f"modulate_bwd_{tag}")
    return dx, (dsh, dsc, dgate), dw_in, dw_out, dlg, dlb


def _mixer_fwd(x, mod, wts, small, lng, lnb, layer, tabs):
    bsz, seq, d = x.shape
    t = bsz * seq
    h = _modulate(x, mod, 3, 4, "modulate_mix")
    proj = _matmul(h.reshape(1, t, d), wts["mix_in"][None], mode="nn", group_out=True, out_dtype=F32, tm=512, tk=d,
                   name="mix_in").reshape(bsz, seq, PACK_W)
    mo, states = _hgrn_fwd(proj, small["lb_logits8"], small["hgrn_norm_g"], layer, f"hgrn_fwd_l{layer}")
    q, kv = _mla_pre(proj, small["q_norm_g"], small["kv_norm_g"], wts["uq"], wts["ukv"], tabs, "mla_pre")
    mla_scale = float((B_NOPE + B_ROPE) ** -0.5)
    mo, lse_b = _attn_fwd(q, 0, kv, 0, mo, 2, None, mla_scale, "mla_attn_fwd")
    fg = _fox_gate(proj, small["fox_b_f"], "fox_gate")
    gates = (fg, jnp.swapaxes(fg[:, :, 0:8], 1, 2))
    fox_scale = float(HEAD_DIM ** -0.5)
    mo, lse_c = _attn_fwd(proj, P_CQ // LANES, proj, P_CKV // LANES, mo, 6, gates, fox_scale, "fox_attn_fwd")
    mo = _gmlp_fwd(proj, mo, small["gmlp_ln_g"], small["gmlp_ln_b"], small["gmlp_w_s"], small["gmlp_bst"],
                   "gmlp_fwd")
    mixed = _matmul(mo.reshape(1, t, MO_W), wts["mix_out"][None], mode="nn", group_out=True, out_dtype=F32,
                    tm=1024, tk=MO_W, name="mix_out").reshape(bsz, seq, d)
    y = _res_ln(x, mixed, mod, lng, lnb, 5, 1.0, "res_ln_mix")
    return y, (x, h, proj, mo, states, q, kv, lse_b, gates, lse_c, mixed)


def _mixer_bwd(dy, saved, mod, wts, small, lng, lnb, layer, tabs, ready):
    x, h, proj, mo, states, q, kv, lse_b, gates, lse_c, mixed = saved
    bsz, seq, d = x.shape
    t = bsz * seq
    dx_res, dmixed, dgate, dlg, dlb = _res_ln_bwd(dy, x, mixed, mod, lng, lnb, 5, 1.0, "res_ln_bwd_mix")
    dm2 = dmixed.reshape(1, t, d)
    dmo = _matmul(dm2, wts["mix_out"][None], mode="nt", group_out=True, out_dtype=F32, tm=1024, tk=d,
                  name="mix_out_dx").reshape(bsz, seq, MO_W)
    dw_out = _matmul(mo.reshape(1, t, MO_W), dm2, mode="tn", group_out=True, out_dtype=F32, tm=512, tk=min(t, 2048),
                     name="mix_out_dw")[0]
    tie_out = ready("mix_out", dw_out)
    g = {}
    dproj, g["lb_logits8"], g["hgrn_norm_g"] = _hgrn_bwd(dmo, proj, states, small["lb_logits8"],
                                                         small["hgrn_norm_g"], layer, f"hgrn_bwd_l{layer}")
    mla_scale = float((B_NOPE + B_ROPE) ** -0.5)
    dq, delta_b, _ = _attn_bwd_q(q, 0, kv, 0, mo, dmo, 2, lse_b, None, mla_scale,
                                 jax.ShapeDtypeStruct((bsz, seq, 512), F32), 0, "mla_attn_bwd_q")
    dkv, _ = _attn_bwd_kv(q, 0, kv, 0, dmo, 2, lse_b, delta_b, None, mla_scale,
                          jax.ShapeDtypeStruct((bsz, seq, 1024), F32), 0, "mla_attn_bwd_kv")
    dproj, g["q_norm_g"], g["kv_norm_g"], g["uq"], g["ukv"] = _mla_pre_bwd(
        dq, dkv, dproj, proj, small["q_norm_g"], small["kv_norm_g"], wts["uq"], wts["ukv"], tabs, "mla_pre_bwd")
    fox_scale = float(HEAD_DIM ** -0.5)
    dproj, delta_c, dfq = _attn_bwd_q(proj, P_CQ // LANES, proj, P_CKV // LANES, mo, dmo, 6, lse_c, gates,
                                      fox_scale, dproj, P_CQ // LANES, "fox_attn_bwd_q")
    dproj, dfk = _attn_bwd_kv(proj, P_CQ // LANES, proj, P_CKV // LANES, dmo, 6, lse_c, delta_c, gates, fox_scale,
                              dproj, P_CKV // (2 * LANES), "fox_attn_bwd_kv")
    dfk_cols = jnp.pad(jnp.swapaxes(dfk[:, :, 0, :], 1, 2), ((0, 0), (0, 0), (0, LANES - N_HEADS)))
    dproj, g["fox_b_f"] = _fox_gate_bwd(dfq, dfk_cols, dproj, proj, small["fox_b_f"], "fox_gate_bwd")
    dproj, g["gmlp_ln_g"], g["gmlp_ln_b"], g["gmlp_w_s"], g["gmlp_bst"] = _gmlp_bwd(
        dmo, dproj, proj, small["gmlp_ln_g"], small["gmlp_ln_b"], small["gmlp_w_s"], small["gmlp_bst"], "gmlp_bwd")
    dp2 = dproj.reshape(1, t, PACK_W)
    dw_in = _matmul(h.reshape(1, t, d), dp2, mode="tn", group_out=True, out_dtype=BF16, tm=512, tk=1024,
                    name="mix_in_dw")[0]
    tie_in = ready("mix_in", dw_in)
    dh = _matmul(dp2, wts["mix_in"][None], mode="nt", group_out=True, out_dtype=F32, tm=512, tk=PACK_W,
                 name="mix_in_dx").reshape(bsz, seq, d)
    dx, dsh, dsc = _modulate_bwd(dh, x, _tied(_tied(mod, tie_out), tie_in), dx_res, 4, "modulate_bwd_mix")
    return dx, (dsh, dsc, dgate), dw_in, dw_out, g, dlg, dlb


def _small_views(p, layer):
    return {
        "lb_logits8": jnp.pad(p["hgrn_lb_logits"], ((0, 8 - DEPTH), (0, 0))),
        "hgrn_norm_g": p["hgrn_norm_g"][layer][None],
        "q_norm_g": p["mla_q_norm_g"][layer][None],
        "kv_norm_g": p["mla_kv_norm_g"][layer][None],
        "fox_b_f": jnp.pad(p["fox_b_f"][layer][None], ((0, 0), (0, LANES - N_HEADS))),
        "gmlp_ln_g": p["gmlp_ln_g"][layer][None],
        "gmlp_ln_b": p["gmlp_ln_b"][layer][None],
        "gmlp_w_s": p["gmlp_w_s"][layer],
        "gmlp_bst": jnp.pad(p["gmlp_b_s"][layer].T, ((0, 0), (0, LANES - N_HEADS))),
    }


def _local_step(x, mod, target, weights, p, grads_ready=None):
    bsz, seq, d = x.shape
    tabs = _rope_tables(seq)
    saved = []
    for l in range(DEPTH):
        sm = _small_views(p, l)
        lng, lnb = p["ln_g"][l], p["ln_b"][l]
        w = weights(l, "ffn1", x)
        x, s1 = _ffn_fwd(x, mod[l], w["ffn1_in"], w["ffn1_out"], lng[0:1], lnb[0:1], (0, 1, 2), "ffn1")
        x, s2 = _mixer_fwd(x, mod[l], weights(l, "mix", x), sm, lng[1:2], lnb[1:2], l, tabs)
        w = weights(l, "ffn2", x)
        x, s3 = _ffn_fwd(x, mod[l], w["ffn2_in"], w["ffn2_out"], lng[2:3], lnb[2:3], (6, 7, 8), "ffn2")
        saved.append((s1, s2, s3))
    dx, loss = _loss_head(x, target, "loss_head")
    big, small, dmods = [None] * DEPTH, [None] * DEPTH, [None] * DEPTH
    ties = []

    def tied(a):
        for t in ties:
            a = a + t
        return a

    for l in reversed(range(DEPTH)):
        w = {**weights(l, "ffn1", None), **weights(l, "mix", None), **weights(l, "ffn2", None)}
        sm = _small_views(p, l)
        lng, lnb = p["ln_g"][l], p["ln_b"][l]
        s1, s2, s3 = saved[l]

        def ready(name, grad, l=l):
            tie = None if grads_ready is None else grads_ready(l, name, grad)
            if tie is not None:
                ties.append(tie)
            return tie

        dx, dm3, dwi2, dwo2, dlg2, dlb2 = _ffn_bwd(dx, s3, tied(mod[l]), w["ffn2_in"], w["ffn2_out"], lng[2:3],
                                                   lnb[2:3], (6, 7, 8), "ffn2", ready)
        dx, dm2, dwmi, dwmo, g, dlg1, dlb1 = _mixer_bwd(dx, s2, tied(mod[l]), w, sm, lng[1:2], lnb[1:2], l, tabs,
                                                        ready)
        dx, dm1, dwi1, dwo1, dlg0, dlb0 = _ffn_bwd(dx, s1, tied(mod[l]), w["ffn1_in"], w["ffn1_out"], lng[0:1],
                                                   lnb[0:1], (0, 1, 2), "ffn1", ready)
        dmods[l] = jnp.concatenate(list(dm1) + list(dm2) + list(dm3), axis=1)
        big[l] = {"ffn1_in": dwi1, "ffn1_out": dwo1, "ffn2_in": dwi2, "ffn2_out": dwo2, "mix_in": dwmi,
                  "mix_out": dwmo}
        g["ln_g"] = jnp.concatenate([dlg0, dlg1, dlg2], axis=0)
        g["ln_b"] = jnp.concatenate([dlb0, dlb1, dlb2], axis=0)
        small[l] = g
    return loss, dx, jnp.stack(dmods), big, small


_BIG = ("ffn1_in", "ffn1_out", "ffn2_in", "ffn2_out", "mix_in", "mix_out")


def _small_grad_list(small, loss):
    def both(fn):
        return jnp.stack([fn(small[l]) for l in range(DEPTH)])

    uq_src, ukv_src = _uq_src(), _ukv_src()
    return [
        ("loss", loss.reshape(1)),
        ("ln_g", both(lambda g: g["ln_g"])), ("ln_b", both(lambda g: g["ln_b"])),
        ("hgrn_lb_logits", small[0]["lb_logits8"][:DEPTH] + small[1]["lb_logits8"][:DEPTH]),
        ("hgrn_norm_g", both(lambda g: g["hgrn_norm_g"][0])),
        ("mla_q_norm_g", both(lambda g: g["q_norm_g"][0])),
        ("mla_kv_norm_g", both(lambda g: g["kv_norm_g"][0])),
        ("mla_w_uq", both(lambda g: _unpack_cols(g["uq"], uq_src, 384))),
        ("mla_w_ukv", both(lambda g: _unpack_cols(g["ukv"], ukv_src, 512))),
        ("fox_b_f", both(lambda g: g["fox_b_f"][0, :N_HEADS])),
        ("gmlp_ln_g", both(lambda g: g["gmlp_ln_g"][0])), ("gmlp_ln_b", both(lambda g: g["gmlp_ln_b"][0])),
        ("gmlp_w_s", both(lambda g: g["gmlp_w_s"])),
        ("gmlp_b_s", both(lambda g: g["gmlp_bst"][:, :N_HEADS].T)),
    ]


_PACK_COLS = 512


def _pack_small(items):
    flat = jnp.concatenate([a.reshape(-1).astype(F32) for _, a in items])
    n = flat.shape[0]
    tile = 8 * _PACK_COLS
    flat = jnp.pad(flat, (0, (-n) % tile))
    return flat.reshape(-1, _PACK_COLS)


def _unpack_small(buf, items):
    flat = buf.reshape(-1)
    out, off = {}, 0
    for name, a in items:
        out[name] = flat[off:off + a.size].reshape(a.shape)
        off += a.size
    return out


def _as2d(a):
    return a.reshape(-1, a.shape[-1])


def kernel(x, c, ada_w, ada_b, ln_g, ln_b, ffn1_w_in, ffn1_w_out, ffn2_w_in, ffn2_w_out, mix_w_in, mix_w_out, hgrn_lb_logits, hgrn_norm_g, mla_q_norm_g, mla_kv_norm_g, mla_w_uq, mla_w_ukv, fox_b_f, gmlp_ln_g, gmlp_ln_b, gmlp_w_s, gmlp_b_s, loss_target, m_ada_w, m_ada_b, m_ln_g, m_ln_b, m_ffn1_w_in, m_ffn1_w_out, m_ffn2_w_in, m_ffn2_w_out, m_mix_w_in, m_mix_w_out, m_hgrn_lb_logits, m_hgrn_norm_g, m_mla_q_norm_g, m_mla_kv_norm_g, m_mla_w_uq, m_mla_w_ukv, m_fox_b_f, m_gmlp_ln_g, m_gmlp_ln_b, m_gmlp_w_s, m_gmlp_b_s, v_ada_w, v_ada_b, v_ln_g, v_ln_b, v_ffn1_w_in, v_ffn1_w_out, v_ffn2_w_in, v_ffn2_w_out, v_mix_w_in, v_mix_w_out, v_hgrn_lb_logits, v_hgrn_norm_g, v_mla_q_norm_g, v_mla_kv_norm_g, v_mla_w_uq, v_mla_w_ukv, v_fox_b_f, v_gmlp_ln_g, v_gmlp_ln_b, v_gmlp_w_s, v_gmlp_b_s):
    names = ["ada_w", "ada_b", "ln_g", "ln_b", "ffn1_w_in", "ffn1_w_out", "ffn2_w_in", "ffn2_w_out", "mix_w_in",
             "mix_w_out", "hgrn_lb_logits", "hgrn_norm_g", "mla_q_norm_g", "mla_kv_norm_g", "mla_w_uq", "mla_w_ukv",
             "fox_b_f", "gmlp_ln_g", "gmlp_ln_b", "gmlp_w_s", "gmlp_b_s"]
    w = dict(zip(names, [ada_w, ada_b, ln_g, ln_b, ffn1_w_in, ffn1_w_out, ffn2_w_in, ffn2_w_out, mix_w_in, mix_w_out,
                         hgrn_lb_logits, hgrn_norm_g, mla_q_norm_g, mla_kv_norm_g, mla_w_uq, mla_w_ukv, fox_b_f,
                         gmlp_ln_g, gmlp_ln_b, gmlp_w_s, gmlp_b_s]))
    m = dict(zip(names, [m_ada_w, m_ada_b, m_ln_g, m_ln_b, m_ffn1_w_in, m_ffn1_w_out, m_ffn2_w_in, m_ffn2_w_out,
                         m_mix_w_in, m_mix_w_out, m_hgrn_lb_logits, m_hgrn_norm_g, m_mla_q_norm_g, m_mla_kv_norm_g,
                         m_mla_w_uq, m_mla_w_ukv, m_fox_b_f, m_gmlp_ln_g, m_gmlp_ln_b, m_gmlp_w_s, m_gmlp_b_s]))
    v = dict(zip(names, [v_ada_w, v_ada_b, v_ln_g, v_ln_b, v_ffn1_w_in, v_ffn1_w_out, v_ffn2_w_in, v_ffn2_w_out,
                         v_mix_w_in, v_mix_w_out, v_hgrn_lb_logits, v_hgrn_norm_g, v_mla_q_norm_g, v_mla_kv_norm_g,
                         v_mla_w_uq, v_mla_w_ukv, v_fox_b_f, v_gmlp_ln_g, v_gmlp_ln_b, v_gmlp_w_s, v_gmlp_b_s]))
    bsz, seq, d = x.shape
    me = 4 * lax.axis_index("x") + 2 * lax.axis_index("y") + lax.axis_index("c")
    mix_src, uq_src, ukv_src, mo_src = _mix_in_src(), _uq_src(), _ukv_src(), _mo_src()

    part_names = {"ffn1": ["ffn1_w_in", "ffn1_w_out"], "mix": ["mix_w_in", "mix_w_out", "mla_w_uq", "mla_w_ukv"],
                  "ffn2": ["ffn2_w_in", "ffn2_w_out"]}
    group_of = {}
    for l in range(DEPTH):
        for part in ("ffn1", "mix", "ffn2"):
            group_of[(l, part)] = (0, part) if l == 0 else (l, "all")
    in_flight = {}

    def start_group(key, behind=None):
        members = [(l, part) for (l, part), g in group_of.items() if g == key]
        labels = [(l, n) for l, part in members for n in part_names[part]]
        shards = []
        for l, n in labels:
            a = w[n][l]
            if n == "mix_w_in":
                a = _pack_cols(a, mix_src)
            shards.append(a.astype(BF16))
        if behind is not None:
            shards, _ = lax.optimization_barrier((shards, behind))
        in_flight[key] = (labels, _push_start(shards, f"gather_start_{key[0]}_{key[1]}", whole=True))

    keys_in_order = list(dict.fromkeys(group_of.values()))
    start_group(keys_in_order[0])

    gathered = _all_gather([c, ln_g, ln_b], "gather_inputs")
    c_all = gathered[0].reshape(N_DEV * bsz, d)
    ln_g_full = jnp.moveaxis(gathered[1], 0, 2).reshape(DEPTH, 3, d)
    ln_b_full = jnp.moveaxis(gathered[2], 0, 2).reshape(DEPTH, 3, d)

    mod_cols = _ada_fwd(c_all, ada_w, "ada_fwd")
    mod_all, = _all_gather([mod_cols], "gather_mod")
    mod_mine = lax.dynamic_slice_in_dim(mod_all, me * bsz, bsz, axis=2)
    mod = jnp.moveaxis(mod_mine, 0, 2).reshape(DEPTH, bsz, N_MOD * d) + ada_b[:, None, :]
    for key in keys_in_order[1:]:
        start_group(key, behind=mod)
    tie = sum(h[-1][0, 0] for _, h in in_flight.values())
    mod = mod.reshape(DEPTH, bsz, N_MOD, d) + tie

    arrived, laid_out = {}, {}

    def weights(l, part, after):
        if (l, part) not in laid_out:
            laid_out[(l, part)] = lay_out(l, part, after)
        return laid_out[(l, part)]

    def lay_out(l, part, after):
        key = group_of[(l, part)]
        if key not in arrived:
            labels, (send_sems, recv_sems, srcs, lands, _) = in_flight[key]
            _, lands = _push_wait(send_sems, recv_sems, srcs, lands, after, f"gather_wait_{key[0]}_{key[1]}",
                                  whole=True)
            arrived[key] = dict(zip(labels, lands))
        gw = {n: arrived[key][(l, n)] for n in part_names[part]}
        if part != "mix":
            return {f"{part}_in": gw[f"{part}_w_in"], f"{part}_out": gw[f"{part}_w_out"].reshape(4, 704, d)}
        uq = jnp.moveaxis(gw["mla_w_uq"], 0, 1).reshape(256, 384)
        ukv = jnp.moveaxis(gw["mla_w_ukv"], 0, 1).reshape(128, 512)
        return {"mix_in": gw["mix_w_in"].reshape(d, PACK_W),
                "mix_out": _pack_cols(gw["mix_w_out"].reshape(d, d).T, mo_src).T,
                "uq": _pack_cols(uq, uq_src), "ukv": _pack_cols(ukv, ukv_src)}

    p = dict(w)
    p["ln_g"], p["ln_b"] = ln_g_full, ln_b_full
    def chunks(name, arr):
        if name in ("ffn1_in", "ffn2_in"):
            return arr
        if name in ("ffn1_out", "ffn2_out"):
            return arr.reshape(N_DEV, arr.shape[1] // 2, d)
        if name == "mix_in":
            return arr.reshape(N_DEV, d // N_DEV, PACK_W)
        return _unpack_cols(arr.T, mo_src, d).T.astype(BF16).reshape(N_DEV, d // N_DEV, d)

    pending, started = {}, []

    def grads_ready(l, name, grad):
        pending[(name, l)] = chunks(name, grad)
        flush = name == "ffn1_in" if l > 0 else name in ("mix_in", "ffn1_out", "ffn1_in")
        if not flush:
            return None
        keys = sorted(pending)
        handles = _push_start([pending[k] for k in keys], f"push_start_{len(started)}")
        pending.clear()
        started.append((keys, handles, l == 0 and name.startswith("ffn1")))
        return handles[-1][0, 0]

    loss, grad_x, dmod, big, small = _local_step(x, mod, loss_target, weights, p, grads_ready)
    del big

    dmod_all, = _all_gather([dmod.reshape(DEPTH, bsz, N_MOD * d)], "gather_dmod")
    dmod_full = jnp.moveaxis(dmod_all, 0, 1).reshape(DEPTH, N_DEV * bsz, N_MOD * d)
    cols = ada_w.shape[2]
    dmod_cols = lax.dynamic_slice_in_dim(dmod_full, me * cols, cols, axis=2)
    g_ada_w, g_ada_b = _ada_bwd(c_all, dmod_cols, dmod_full, "ada_bwd")

    recv = {}

    def arrive(n, after):
        keys, (send_sems, recv_sems, srcs, lands, _), _ = started[n]
        srcs, lands = _push_wait(send_sems, recv_sems, srcs, lands, after, f"push_wait_{n}")
        for k, src, land in zip(keys, srcs, lands):
            recv[k] = (land, lax.dynamic_index_in_dim(src, me, 0, keepdims=False))

    for n, (_, _, last) in enumerate(started):
        if not last:
            arrive(n, grad_x)

    items = _small_grad_list(small, loss)
    parts, = _all_gather([_pack_small(items)], "gather_small")
    sg = _unpack_small(_sum_parts(parts, "sum_small"), items)

    out = {}

    def update(name, gparts):
        shape = w[name].shape
        res = _adamw(gparts, None, _as2d(w[name]), _as2d(m[name]), _as2d(v[name]), f"adamw_{name}")
        out[name] = tuple(r.reshape(shape) for r in res)

    big_of = {"ffn1_w_in": "ffn1_in", "ffn1_w_out": "ffn1_out", "ffn2_w_in": "ffn2_in", "ffn2_w_out": "ffn2_out",
              "mix_w_in": "mix_in", "mix_w_out": "mix_out"}
    def big_update(name, key, l, prev):
        parts, own = recv[(key, l)]
        if key == "mix_in":
            parts = _unpack_cols(parts, mix_src, MIX_ORIG_W)
            own = _unpack_cols(own, mix_src, MIX_ORIG_W)
        return _adamw(parts, own, _as2d(w[name]), _as2d(m[name]), _as2d(v[name]), f"adamw_{name}_l{l}",
                      layer=l, prev=prev)

    chain = {name: None for name in big_of}
    for name, key in big_of.items():
        for l in reversed(range(DEPTH)):
            if (key, l) in recv:
                chain[name] = big_update(name, key, l, chain[name])
    update("ada_w", _as2d(g_ada_w)[None])
    update("ada_b", g_ada_b.reshape(1, DEPTH, N_MOD * d))
    for name in ("ln_g", "ln_b"):
        g_loc = lax.dynamic_slice_in_dim(sg[name], me * (d // N_DEV), d // N_DEV, axis=2)
        update(name, _as2d(g_loc)[None])
    for name, width in (("mla_w_uq", 48), ("mla_w_ukv", 64)):
        g_loc = lax.dynamic_slice_in_dim(sg[name], me * width, width, axis=2)
        update(name, _as2d(g_loc)[None])
    for name in ("hgrn_lb_logits", "hgrn_norm_g", "mla_q_norm_g", "mla_kv_norm_g", "fox_b_f", "gmlp_ln_g",
                 "gmlp_ln_b", "gmlp_w_s", "gmlp_b_s"):
        update(name, _as2d(sg[name])[None])
    for n, (keys, _, last) in enumerate(started):
        if last:
            arrive(n, out["gmlp_w_s"][0])
            for key, l in keys:
                name = next(nm for nm, k in big_of.items() if k == key)
                chain[name] = big_update(name, key, l, chain[name])
    for name in big_of:
        out[name] = tuple(r.reshape(w[name].shape) for r in chain[name])

    return (sg["loss"][0], grad_x, *[out[n][0] for n in names], *[out[n][1] for n in names],
            *[out[n][2] for n in names], *[out[n][3] for n in names])
```

```python
import functools

import numpy as np
import jax
import jax.numpy as jnp
from jax import lax
from jax.experimental import pallas as pl
from jax.experimental.pallas import tpu as pltpu

F32 = jnp.float32
BF16 = jnp.bfloat16
HI = lax.Precision.HIGHEST

D_MODEL = 1024
DEPTH = 2
GROUP_WIDTH = 256
N_HEADS = 4
HEAD_DIM = 64
A_CHUNK = 16
LB_FLOOR = 1e-30
B_NOPE = 64
B_ROPE = 32
ROPE_THETA = 10000.0
D_CHUNK = 128
D_FF = 2816
N_MOD = 9
ALPHA = (2 * DEPTH) ** 0.25
LN_EPS = 1e-5
RMS_EPS = 1e-6
ADAM_LR = 0.001
ADAM_B1 = 0.9
ADAM_B2 = 0.999
ADAM_EPS = 1e-08
ADAM_WD = 0.01
ADAM_STEP = 10

N_DEV = 8
LANES = 128
PACK_W = 3712
MO_W = 1536
VMEM_LIMIT = 56 * 1024 * 1024
NEG = -1e30
ATTN_TILE = 512

MIX_ORIG_W = 2724
O_BCQ, O_BCKV, O_BKR, O_CQ, O_CK, O_CV, O_CF, O_DU, O_DV = 1024, 1280, 1408, 1440, 1696, 1952, 2208, 2212, 2468
P_B, P_KR, P_CQ, P_CKV, P_D, P_CF = 1024, 1408, 1536, 2048, 3072, 3584


_DN = {"nn": (((1,), (0,)), ((), ())), "nt": (((1,), (1,)), ((), ())), "tn": (((0,), (0,)), ((), ()))}


def _raw_bdot(a, b, mode):
    return lax.dot_general(a.astype(BF16), b.astype(BF16), _DN[mode], preferred_element_type=F32)


@functools.partial(jax.custom_vjp, nondiff_argnums=(2,))
def _bdot(a, b, mode):
    return _raw_bdot(a, b, mode)


def _bdot_fwd(a, b, mode):
    return _raw_bdot(a, b, mode), (a, b)


def _bdot_bwd(mode, res, g):
    a, b = res
    if mode == "nn":
        return _raw_bdot(g, b, "nt"), _raw_bdot(a, g, "tn")
    if mode == "nt":
        return _raw_bdot(g, b, "nn"), _raw_bdot(g, a, "tn")
    return _raw_bdot(b, g, "nt"), _raw_bdot(a, g, "nn")


_bdot.defvjp(_bdot_fwd, _bdot_bwd)


def _cparams(sem):
    return pltpu.CompilerParams(dimension_semantics=sem, vmem_limit_bytes=VMEM_LIMIT)


def _mix_in_src():
    src = -np.ones(PACK_W, np.int64)
    src[0:P_KR] = np.arange(0, O_BKR)
    src[P_KR + 64:P_KR + 80] = O_BKR + np.arange(16)
    src[P_KR + 96:P_KR + 112] = O_BKR + 16 + np.arange(16)
    for h in range(N_HEADS):
        src[P_CQ + 128 * h:P_CQ + 128 * h + 64] = O_CQ + 64 * h + np.arange(64)
        src[P_CKV + 256 * h:P_CKV + 256 * h + 64] = O_CK + 64 * h + np.arange(64)
        src[P_CKV + 256 * h + 128:P_CKV + 256 * h + 192] = O_CV + 64 * h + np.arange(64)
    src[P_D:P_D + 512] = O_DU + np.arange(512)
    src[P_CF:P_CF + 4] = O_CF + np.arange(4)
    return src


def _uq_src():
    src = -np.ones(512, np.int64)
    for h in range(N_HEADS):
        src[128 * h:128 * h + 64] = 96 * h + np.arange(64)
        src[128 * h + 64:128 * h + 80] = 96 * h + 64 + np.arange(16)
        src[128 * h + 96:128 * h + 112] = 96 * h + 80 + np.arange(16)
    return src


def _ukv_src():
    src = -np.ones(1024, np.int64)
    for h in range(N_HEADS):
        src[256 * h:256 * h + 64] = 128 * h + np.arange(64)
        src[256 * h + 128:256 * h + 192] = 128 * h + 64 + np.arange(64)
    return src


def _mo_src():
    src = -np.ones(MO_W, np.int64)
    src[0:256] = np.arange(256)
    for g in range(2):
        for h in range(N_HEADS):
            src[256 + 512 * g + 128 * h:256 + 512 * g + 128 * h + 64] = 256 + 256 * g + 64 * h + np.arange(64)
    src[1280:1536] = 768 + np.arange(256)
    return src


def _runs(idx):
    runs, i = [], 0
    while i < len(idx):
        j = i + 1
        while j < len(idx) and ((idx[i] < 0 and idx[j] < 0) or (idx[i] >= 0 and idx[j] == idx[i] + j - i)):
            j += 1
        runs.append((int(idx[i]), j - i))
        i = j
    return runs


def _take_runs(w, idx):
    parts = [jnp.zeros(w.shape[:-1] + (n,), w.dtype) if s < 0 else lax.slice_in_dim(w, s, s + n, axis=w.ndim - 1)
             for s, n in _runs(idx)]
    return jnp.concatenate(parts, axis=-1)


def _pack_cols(w, src):
    return _take_runs(w, src)


def _unpack_cols(wp, src, n):
    dst = np.zeros(n, np.int64)
    dst[src[src >= 0]] = np.nonzero(src >= 0)[0]
    return _take_runs(wp, dst)


def _rope_tables(seq):
    half = B_ROPE // 2
    inv_freq = ROPE_THETA ** (-jnp.arange(half, dtype=F32) / half)
    ang = jnp.arange(seq).astype(F32)[:, None] * inv_freq[None, :]
    cos, sin = jnp.cos(ang), jnp.sin(ang)
    z16 = jnp.zeros((seq, 16), F32)
    c = jnp.concatenate([jnp.ones((seq, 64), F32), cos, z16, cos, z16], axis=1)
    s1 = jnp.concatenate([jnp.zeros((seq, 64), F32), -sin, z16, z16, z16], axis=1)
    s2 = jnp.concatenate([jnp.zeros((seq, 64), F32), z16, z16, sin, z16], axis=1)
    return c, s1, s2


def _matmul(a, b, *, mode, group_out, out_dtype, tm, tk, name):
    ga, gb = a.shape[0], b.shape[0]
    g_n = max(ga, gb)
    if mode == "tn":
        k_dim, m_dim = a.shape[1:]
    else:
        m_dim, k_dim = a.shape[1:]
    n_dim = b.shape[1] if mode == "nt" else b.shape[2]
    assert m_dim % tm == 0 and k_dim % tk == 0
    kt = k_dim // tk
    n_red = kt if group_out else g_n * kt
    g_out = g_n if group_out else 1

    def split(g, r):
        return (g, r) if group_out else (r // kt, r % kt)

    def a_map(g, i, r):
        gg, kk = split(g, r)
        gg = gg if ga > 1 else 0
        return (gg, kk, i) if mode == "tn" else (gg, i, kk)

    def b_map(g, i, r):
        gg, kk = split(g, r)
        gg = gg if gb > 1 else 0
        return (gg, 0, kk) if mode == "nt" else (gg, kk, 0)

    a_blk = (None, tk, tm) if mode == "tn" else (None, tm, tk)
    b_blk = (None, n_dim, tk) if mode == "nt" else (None, tk, n_dim)
    dn = _DN[mode]

    def body(a_ref, b_ref, o_ref, *scratch):
        part = lax.dot_general(a_ref[...].astype(BF16), b_ref[...].astype(BF16), dn, preferred_element_type=F32)
        if n_red == 1:
            o_ref[...] = part.astype(o_ref.dtype)
            return
        acc_ref, = scratch
        r = pl.program_id(2)

        @pl.when(r == 0)
        def _():
            acc_ref[...] = part

        @pl.when(r > 0)
        def _():
            acc_ref[...] += part

        @pl.when(r == n_red - 1)
        def _():
            o_ref[...] = acc_ref[...].astype(o_ref.dtype)

    return pl.pallas_call(
        body, name=name, grid=(g_out, m_dim // tm, n_red),
        in_specs=[pl.BlockSpec(a_blk, a_map), pl.BlockSpec(b_blk, b_map)],
        out_specs=pl.BlockSpec((None, tm, n_dim), lambda g, i, r: (g, i, 0)),
        out_shape=jax.ShapeDtypeStruct((g_out, m_dim, n_dim), out_dtype),
        scratch_shapes=[] if n_red == 1 else [pltpu.VMEM((tm, n_dim), F32)],
        compiler_params=_cparams(("parallel", "parallel", "arbitrary")),
    )(a, b)


def _row_spec(ts, d):
    return pl.BlockSpec((None, ts, d), lambda b, s: (b, s, 0))


def _mod_spec(d):
    return pl.BlockSpec((None, N_MOD, d), lambda b, s: (b, 0, 0))


def _vec_spec(d):
    return pl.BlockSpec((1, d), lambda b, s: (0, 0))


def _bvec_spec(d):
    return pl.BlockSpec((None, 1, d), lambda b, s: (b, 0, 0))


def _modulate(x, mod, sh_row, sc_row, name, ts=512):
    bsz, seq, d = x.shape

    def body(x_ref, mod_ref, o_ref):
        sh = mod_ref[sh_row:sh_row + 1, :]
        sc = mod_ref[sc_row:sc_row + 1, :]
        o_ref[...] = (x_ref[...] * (1.0 + sc) + sh).astype(o_ref.dtype)

    return pl.pallas_call(
        body, name=name, grid=(bsz, seq // ts),
        in_specs=[_row_spec(ts, d), _mod_spec(d)], out_specs=_row_spec(ts, d),
        out_shape=jax.ShapeDtypeStruct((bsz, seq, d), BF16),
        compiler_params=_cparams(("parallel", "parallel")),
    )(x, mod)


def _modulate_bwd(dh, x, mod, dx_res, sc_row, name, ts=512):
    bsz, seq, d = x.shape

    def body(dh_ref, x_ref, mod_ref, dxr_ref, dx_ref, dsh_ref, dsc_ref):
        s = pl.program_id(1)
        sc = mod_ref[sc_row:sc_row + 1, :]
        dh_v = dh_ref[...]
        dx_ref[...] = dxr_ref[...] + dh_v * (1.0 + sc)
        psh = jnp.sum(dh_v, axis=0, keepdims=True)
        psc = jnp.sum(dh_v * x_ref[...], axis=0, keepdims=True)

        @pl.when(s == 0)
        def _():
            dsh_ref[...] = psh
            dsc_ref[...] = psc

        @pl.when(s > 0)
        def _():
            dsh_ref[...] += psh
            dsc_ref[...] += psc

    return pl.pallas_call(
        body, name=name, grid=(bsz, seq // ts),
        in_specs=[_row_spec(ts, d), _row_spec(ts, d), _mod_spec(d), _row_spec(ts, d)],
        out_specs=[_row_spec(ts, d), _bvec_spec(d), _bvec_spec(d)],
        out_shape=[jax.ShapeDtypeStruct((bsz, seq, d), F32), jax.ShapeDtypeStruct((bsz, 1, d), F32),
                   jax.ShapeDtypeStruct((bsz, 1, d), F32)],
        compiler_params=_cparams(("parallel", "arbitrary")),
    )(dh, x, mod, dx_res)


def _res_ln_fn(x, f, g, lng, lnb, cmul):
    r = ALPHA * x + (cmul * (1.0 + g)) * f
    mu = jnp.mean(r, axis=-1, keepdims=True)
    rc = r - mu
    var = jnp.mean(rc * rc, axis=-1, keepdims=True)
    return rc * lax.rsqrt(var + LN_EPS) * lng + lnb


def _res_ln(x, f, mod, lng, lnb, g_row, cmul, name, nxt=None, ts=512):
    bsz, seq, d = x.shape

    def body(*refs):
        x_ref, f_ref, mod_ref, lng_ref, lnb_ref = refs[:5]
        g = mod_ref[g_row:g_row + 1, :]
        y = _res_ln_fn(x_ref[...], f_ref[...], g, lng_ref[...], lnb_ref[...], cmul)
        if nxt is None:
            refs[5][...] = y
            return
        nmod_ref, o_ref, h_ref = refs[5:]
        o_ref[...] = y
        sh = nmod_ref[nxt[1]:nxt[1] + 1, :]
        sc = nmod_ref[nxt[2]:nxt[2] + 1, :]
        h_ref[...] = (y * (1.0 + sc) + sh).astype(h_ref.dtype)

    in_specs = [_row_spec(ts, d), _row_spec(ts, d), _mod_spec(d), _vec_spec(d), _vec_spec(d)]
    args = [x, f, mod, lng, lnb]
    out_specs, out_shape = [_row_spec(ts, d)], [jax.ShapeDtypeStruct((bsz, seq, d), F32)]
    if nxt is not None:
        in_specs.append(_mod_spec(d))
        args.append(nxt[0])
        out_specs.append(_row_spec(ts, d))
        out_shape.append(jax.ShapeDtypeStruct((bsz, seq, d), BF16))
    res = pl.pallas_call(
        body, name=name, grid=(bsz, seq // ts), in_specs=in_specs, out_specs=out_specs, out_shape=out_shape,
        compiler_params=_cparams(("parallel", "parallel")),
    )(*args)
    return (res[0], res[1]) if nxt is not None else (res[0], None)


def _res_ln_bwd(dy, x, f, mod, lng, lnb, g_row, cmul, name, ts=256):
    bsz, seq, d = x.shape

    def body(dy_ref, x_ref, f_ref, mod_ref, lng_ref, lnb_ref, dx_ref, df_ref, dg_ref, dlg_ref, dlb_ref):
        b, s = pl.program_id(0), pl.program_id(1)
        g = mod_ref[g_row:g_row + 1, :]
        _, vjp = jax.vjp(functools.partial(_res_ln_fn, cmul=cmul), x_ref[...], f_ref[...], g, lng_ref[...],
                         lnb_ref[...])
        dx, df, dg, dlg, dlb = vjp(dy_ref[...])
        dx_ref[...] = dx
        df_ref[...] = df.astype(df_ref.dtype)

        @pl.when(s == 0)
        def _():
            dg_ref[...] = dg

        @pl.when(s > 0)
        def _():
            dg_ref[...] += dg

        first = jnp.logical_and(b == 0, s == 0)

        @pl.when(first)
        def _():
            dlg_ref[...] = dlg
            dlb_ref[...] = dlb

        @pl.when(jnp.logical_not(first))
        def _():
            dlg_ref[...] += dlg
            dlb_ref[...] += dlb

    return pl.pallas_call(
        body, name=name, grid=(bsz, seq // ts),
        in_specs=[_row_spec(ts, d), _row_spec(ts, d), _row_spec(ts, d), _mod_spec(d), _vec_spec(d), _vec_spec(d)],
        out_specs=[_row_spec(ts, d), _row_spec(ts, d), _bvec_spec(d), _vec_spec(d), _vec_spec(d)],
        out_shape=[jax.ShapeDtypeStruct((bsz, seq, d), F32), jax.ShapeDtypeStruct((bsz, seq, d), BF16),
                   jax.ShapeDtypeStruct((bsz, 1, d), F32), jax.ShapeDtypeStruct((1, d), F32),
                   jax.ShapeDtypeStruct((1, d), F32)],
        compiler_params=_cparams(("arbitrary", "arbitrary")),
    )(dy, x, f, mod, lng, lnb)


def _loss_head(y, target, name, ts=512):
    bsz, seq, d = y.shape
    n_s = seq // ts

    def body(y_ref, t_ref, dy_ref, loss_ref, acc_ref):
        b, s = pl.program_id(0), pl.program_id(1)
        err = y_ref[...] - t_ref[...]
        dy_ref[...] = err * (1.0 / d)
        part = jnp.sum(err * err, axis=0, keepdims=True)
        first = jnp.logical_and(b == 0, s == 0)

        @pl.when(first)
        def _():
            acc_ref[...] = part

        @pl.when(jnp.logical_not(first))
        def _():
            acc_ref[...] += part

        @pl.when(jnp.logical_and(b == bsz - 1, s == n_s - 1))
        def _():
            loss_ref[...] = jnp.sum(acc_ref[...], axis=1, keepdims=True) * (0.5 / d)

    return pl.pallas_call(
        body, name=name, grid=(bsz, n_s),
        in_specs=[_row_spec(ts, d), _row_spec(ts, d)],
        out_specs=[_row_spec(ts, d), pl.BlockSpec((1, 1), lambda b, s: (0, 0))],
        out_shape=[jax.ShapeDtypeStruct((bsz, seq, d), F32), jax.ShapeDtypeStruct((1, 1), F32)],
        scratch_shapes=[pltpu.VMEM((1, d), F32)],
        compiler_params=_cparams(("arbitrary", "arbitrary")),
    )(y, target)


def _ffn_in_swiglu(h, w_in, name, tm=1024):
    t, d = h.shape
    n_sh, _, w = w_in.shape
    half = n_sh // 2

    def body(h_ref, w_ref, z_ref, a_ref):
        hv = h_ref[...]
        g = jnp.dot(hv, w_ref[0], preferred_element_type=F32)
        u = jnp.dot(hv, w_ref[1], preferred_element_type=F32)
        z_ref[0] = g.astype(z_ref.dtype)
        z_ref[1] = u.astype(z_ref.dtype)
        a_ref[...] = (g * jax.nn.sigmoid(g) * u).astype(a_ref.dtype)

    return pl.pallas_call(
        body, name=name, grid=(half, t // tm),
        in_specs=[pl.BlockSpec((tm, d), lambda g, i: (i, 0)),
                  pl.BlockSpec((2, None, d, w), lambda g, i: (0, g, 0, 0))],
        out_specs=[pl.BlockSpec((2, None, tm, w), lambda g, i: (0, g, i, 0)),
                   pl.BlockSpec((None, tm, w), lambda g, i: (g, i, 0))],
        out_shape=[jax.ShapeDtypeStruct((2, half, t, w), BF16), jax.ShapeDtypeStruct((half, t, w), BF16)],
        compiler_params=_cparams(("parallel", "parallel")),
    )(h, w_in.reshape(2, half, d, w))


def _ffn_out_dx_swiglu(df, w_out, z, name, tm=1024):
    t, d = df.shape
    half, w, _ = w_out.shape

    def body(df_ref, w_ref, z_ref, dz_ref):
        da = lax.dot_general(df_ref[...], w_ref[...], _DN["nt"], preferred_element_type=F32)
        g = z_ref[0].astype(F32)
        u = z_ref[1].astype(F32)
        sig = jax.nn.sigmoid(g)
        dz_ref[0] = (da * u * (sig * (1.0 + g * (1.0 - sig)))).astype(dz_ref.dtype)
        dz_ref[1] = (da * (g * sig)).astype(dz_ref.dtype)

    zspec = pl.BlockSpec((2, None, tm, w), lambda g, i: (0, g, i, 0))
    return pl.pallas_call(
        body, name=name, grid=(half, t // tm),
        in_specs=[pl.BlockSpec((tm, d), lambda g, i: (i, 0)), pl.BlockSpec((None, w, d), lambda g, i: (g, 0, 0)),
                  zspec],
        out_specs=zspec, out_shape=jax.ShapeDtypeStruct(z.shape, BF16),
        compiler_params=_cparams(("parallel", "parallel")),
    )(df, w_out, z)


def _log_sigmoid(x):
    return jnp.minimum(x, 0.0) - jnp.log(1.0 + jnp.exp(-jnp.abs(x)))


def _hgrn_consts():
    r = lax.broadcasted_iota(jnp.int32, (GROUP_WIDTH, GROUP_WIDTH), 0)
    c = lax.broadcasted_iota(jnp.int32, (GROUP_WIDTH, GROUP_WIDTH), 1)
    bd = (r // HEAD_DIM == c // HEAD_DIM).astype(F32)
    r16 = lax.broadcasted_iota(jnp.int32, (A_CHUNK, A_CHUNK), 0)
    c16 = lax.broadcasted_iota(jnp.int32, (A_CHUNK, A_CHUNK), 1)
    tril = (r16 >= c16).astype(F32)
    rows = lax.broadcasted_iota(jnp.int32, (A_CHUNK, GROUP_WIDTH), 0)
    return bd, tril, rows


def _hgrn_lb(logits8, layer):
    rows = lax.broadcasted_iota(jnp.int32, logits8.shape, 0)
    valid = rows < DEPTH
    mx = jnp.max(jnp.where(valid, logits8, NEG), axis=0, keepdims=True)
    e = jnp.where(valid, jnp.exp(logits8 - mx), 0.0)
    sm = e / jnp.sum(e, axis=0, keepdims=True)
    pick = jnp.logical_and(rows >= 1, rows <= layer)
    return jnp.sum(jnp.where(pick, sm, 0.0), axis=0, keepdims=True)


def _hgrn_chunk(aq, af, ai, ag, logits8, norm_g, st, *, layer, consts):
    bd, tril, rows = consts
    lb = _hgrn_lb(logits8, layer)
    la = jnp.log(jnp.maximum(lb, LB_FLOOR))
    b2 = jnp.log(1.0 - lb) + _log_sigmoid(af)
    log_f = jnp.maximum(la, b2) + jnp.log(1.0 + jnp.exp(-jnp.abs(la - b2)))
    k = 1.0 - jnp.exp(log_f)
    qf = aq * jax.nn.sigmoid(aq)
    g_cum = jnp.dot(tril, log_f, precision=HI, preferred_element_type=F32)

    c, w = A_CHUNK, GROUP_WIDTH

    def by_key(v):
        return jnp.broadcast_to(v[:, None, :], (c, c, w))

    def by_query(v):
        return jnp.broadcast_to(v[None, :, :], (c, c, w))

    s_i = lax.broadcasted_iota(jnp.int32, (c, c, w), 0)
    t_i = lax.broadcasted_iota(jnp.int32, (c, c, w), 1)
    rel = jnp.where(t_i >= s_i, by_query(g_cum) - by_key(g_cum), NEG)
    pairs = by_query(qf) * by_key(k) * jnp.exp(rel)
    a_all = _bdot(pairs.reshape(c * c, w), bd, "nn").reshape(c, c, w)
    o = jnp.sum(a_all * by_key(ai), axis=0)
    q_dec = qf * jnp.exp(g_cum)
    o = o + _bdot(q_dec, st, "nt")
    g_last = jnp.sum(jnp.where(rows == c - 1, g_cum, 0.0), axis=0, keepdims=True)
    k_end = k * jnp.exp(g_last - g_cum)
    kv = _bdot(ai, k_end, "tn")
    st_new = st * jnp.exp(g_last) + kv * bd
    ms = _bdot(o * o, bd, "nn") * (1.0 / HEAD_DIM)
    o = o * lax.rsqrt(ms + RMS_EPS) * norm_g
    return o * (ag * jax.nn.sigmoid(ag)), st_new


def _hgrn_fwd(proj, logits8, norm_g, layer, name, ts=128):
    bsz, seq, _ = proj.shape
    n_ch = ts // A_CHUNK

    def body(p_ref, lg_ref, ng_ref, o_ref, st_ref, st_scr):
        @pl.when(pl.program_id(1) == 0)
        def _():
            st_scr[...] = jnp.zeros_like(st_scr)

        consts = _hgrn_consts()
        logits_v, ng_v = lg_ref[...], ng_ref[...]

        def chunk(ci, carry):
            r = pl.multiple_of(ci * A_CHUNK, A_CHUNK)
            st = st_scr[...]
            st_ref[ci] = st
            o, st_new = _hgrn_chunk(
                p_ref[pl.ds(r, A_CHUNK), 0:256], p_ref[pl.ds(r, A_CHUNK), 256:512],
                p_ref[pl.ds(r, A_CHUNK), 512:768], p_ref[pl.ds(r, A_CHUNK), 768:1024],
                logits_v, ng_v, st, layer=layer, consts=consts)
            o_ref[pl.ds(r, A_CHUNK), :] = o.astype(o_ref.dtype)
            st_scr[...] = st_new
            return carry

        lax.fori_loop(0, n_ch, chunk, 0, unroll=2)

    return pl.pallas_call(
        body, name=name, grid=(bsz, seq // ts),
        in_specs=[pl.BlockSpec((None, ts, 1024), lambda b, s: (b, s, 0)),
                  pl.BlockSpec((8, GROUP_WIDTH), lambda b, s: (0, 0)),
                  pl.BlockSpec((1, GROUP_WIDTH), lambda b, s: (0, 0))],
        out_specs=[pl.BlockSpec((None, ts, GROUP_WIDTH), lambda b, s: (b, s, 0)),
                   pl.BlockSpec((None, n_ch, GROUP_WIDTH, GROUP_WIDTH), lambda b, s: (b, s, 0, 0))],
        out_shape=[jax.ShapeDtypeStruct((bsz, seq, MO_W), BF16),
                   jax.ShapeDtypeStruct((bsz, seq // A_CHUNK, GROUP_WIDTH, GROUP_WIDTH), F32)],
        scratch_shapes=[pltpu.VMEM((GROUP_WIDTH, GROUP_WIDTH), F32)],
        compiler_params=_cparams(("parallel", "arbitrary")),
    )(proj, logits8, norm_g)


def _hgrn_bwd(dmo, proj, states, logits8, norm_g, layer, name, ts=128):
    bsz, seq, _ = proj.shape
    n_ch = ts // A_CHUNK
    n_s = seq // ts

    def body(do_ref, p_ref, st_ref, lg_ref, ng_ref, dp_ref, dlg_ref, dng_ref, dst_scr):
        b, s = pl.program_id(0), pl.program_id(1)

        @pl.when(s == 0)
        def _():
            dst_scr[...] = jnp.zeros_like(dst_scr)

        @pl.when(jnp.logical_and(b == 0, s == 0))
        def _():
            dlg_ref[...] = jnp.zeros_like(dlg_ref)
            dng_ref[...] = jnp.zeros_like(dng_ref)

        consts = _hgrn_consts()
        logits_v, ng_v = lg_ref[...], ng_ref[...]
        fn = functools.partial(_hgrn_chunk, layer=layer, consts=consts)

        def chunk(t, carry):
            ci = n_ch - 1 - t
            r = pl.multiple_of(ci * A_CHUNK, A_CHUNK)
            _, vjp = jax.vjp(
                fn, p_ref[pl.ds(r, A_CHUNK), 0:256], p_ref[pl.ds(r, A_CHUNK), 256:512],
                p_ref[pl.ds(r, A_CHUNK), 512:768], p_ref[pl.ds(r, A_CHUNK), 768:1024],
                logits_v, ng_v, st_ref[ci])
            daq, daf, dai, dag, dlg, dng, dst = vjp((do_ref[pl.ds(r, A_CHUNK), :], dst_scr[...]))
            dp_ref[pl.ds(r, A_CHUNK), 0:256] = daq.astype(dp_ref.dtype)
            dp_ref[pl.ds(r, A_CHUNK), 256:512] = daf.astype(dp_ref.dtype)
            dp_ref[pl.ds(r, A_CHUNK), 512:768] = dai.astype(dp_ref.dtype)
            dp_ref[pl.ds(r, A_CHUNK), 768:1024] = dag.astype(dp_ref.dtype)
            dlg_ref[...] += dlg
            dng_ref[...] += dng
            dst_scr[...] = dst
            return carry

        lax.fori_loop(0, n_ch, chunk, 0, unroll=2)

    rev = lambda b, s: (b, n_s - 1 - s, 0)
    return pl.pallas_call(
        body, name=name, grid=(bsz, n_s),
        in_specs=[pl.BlockSpec((None, ts, GROUP_WIDTH), rev),
                  pl.BlockSpec((None, ts, 1024), rev),
                  pl.BlockSpec((None, n_ch, GROUP_WIDTH, GROUP_WIDTH), lambda b, s: (b, n_s - 1 - s, 0, 0)),
                  pl.BlockSpec((8, GROUP_WIDTH), lambda b, s: (0, 0)),
                  pl.BlockSpec((1, GROUP_WIDTH), lambda b, s: (0, 0))],
        out_specs=[pl.BlockSpec((None, ts, 1024), rev),
                   pl.BlockSpec((8, GROUP_WIDTH), lambda b, s: (0, 0)),
                   pl.BlockSpec((1, GROUP_WIDTH), lambda b, s: (0, 0))],
        out_shape=[jax.ShapeDtypeStruct((bsz, seq, PACK_W), BF16),
                   jax.ShapeDtypeStruct((8, GROUP_WIDTH), F32), jax.ShapeDtypeStruct((1, GROUP_WIDTH), F32)],
        scratch_shapes=[pltpu.VMEM((GROUP_WIDTH, GROUP_WIDTH), F32)],
        compiler_params=_cparams(("arbitrary", "arbitrary")),
    )(dmo, proj, states, logits8, norm_g)


def _rms_fn(x, g):
    return x * lax.rsqrt(jnp.mean(x * x, axis=-1, keepdims=True) + RMS_EPS) * g


def _tile4(t):
    return jnp.concatenate([t, t, t, t], axis=1)


def _rope(x, c, s1, s2):
    w = x.shape[-1]
    return x * c + pltpu.roll(x, 32, axis=1) * s2 + pltpu.roll(x, w - 32, axis=1) * s1


def _rope_t(dy, c, s1, s2):
    w = dy.shape[-1]
    return dy * c + pltpu.roll(dy * s2, w - 32, axis=1) + pltpu.roll(dy * s1, 32, axis=1)


def _mla_pre(proj, qg, kvg, wq, wkv, tabs, name, ts=256):
    bsz, seq, _ = proj.shape

    def body(p_ref, qg_ref, kvg_ref, wq_ref, wkv_ref, c_ref, s1_ref, s2_ref, q_ref, kv_ref):
        nq = _rms_fn(p_ref[:, 0:256], qg_ref[...])
        nkv = _rms_fn(p_ref[:, 256:384], kvg_ref[...])
        c, s1, s2 = c_ref[...], s1_ref[...], s2_ref[...]
        qp = jnp.dot(nq.astype(BF16), wq_ref[...], preferred_element_type=F32)
        q_ref[...] = _rope(qp, _tile4(c), _tile4(s1), _tile4(s2)).astype(q_ref.dtype)
        kv = jnp.dot(nkv.astype(BF16), wkv_ref[...], preferred_element_type=F32)
        krr = _rope(p_ref[:, 384:512], c, s1, s2)
        zero = jnp.zeros_like(krr)
        kv_ref[...] = (kv + jnp.concatenate([krr, zero] * N_HEADS, axis=1)).astype(kv_ref.dtype)

    tab_spec = pl.BlockSpec((ts, LANES), lambda b, s: (s, 0))
    return pl.pallas_call(
        body, name=name, grid=(bsz, seq // ts),
        in_specs=[pl.BlockSpec((None, ts, 512), lambda b, s: (b, s, P_B // 512)),
                  _vec_spec(256), _vec_spec(128),
                  pl.BlockSpec((256, 512), lambda b, s: (0, 0)), pl.BlockSpec((128, 1024), lambda b, s: (0, 0)),
                  tab_spec, tab_spec, tab_spec],
        out_specs=[_row_spec(ts, 512), _row_spec(ts, 1024)],
        out_shape=[jax.ShapeDtypeStruct((bsz, seq, 512), BF16), jax.ShapeDtypeStruct((bsz, seq, 1024), BF16)],
        compiler_params=_cparams(("parallel", "parallel")),
    )(proj, qg, kvg, wq, wkv, *tabs)


def _mla_pre_bwd(dq, dkv, dproj, proj, qg, kvg, wq, wkv, tabs, name, ts=256):
    bsz, seq, _ = proj.shape

    def body(dq_ref, dkv_ref, dp_any, p_ref, qg_ref, kvg_ref, wq_ref, wkv_ref, c_ref, s1_ref, s2_ref,
             dp_ref, dqg_ref, dkvg_ref, dwq_ref, dwkv_ref):
        del dp_any
        first = jnp.logical_and(pl.program_id(0) == 0, pl.program_id(1) == 0)

        @pl.when(first)
        def _():
            dqg_ref[...] = jnp.zeros_like(dqg_ref)
            dkvg_ref[...] = jnp.zeros_like(dkvg_ref)
            dwq_ref[...] = jnp.zeros_like(dwq_ref)
            dwkv_ref[...] = jnp.zeros_like(dwkv_ref)

        c, s1, s2 = c_ref[...], s1_ref[...], s2_ref[...]
        nq, vjp_q = jax.vjp(_rms_fn, p_ref[:, 0:256], qg_ref[...])
        nkv, vjp_kv = jax.vjp(_rms_fn, p_ref[:, 256:384], kvg_ref[...])
        dqp = _rope_t(dq_ref[...], _tile4(c), _tile4(s1), _tile4(s2)).astype(BF16)
        dkv_v = dkv_ref[...]
        dkv_b = dkv_v.astype(BF16)
        tn = (((0,), (0,)), ((), ()))
        nt = (((1,), (1,)), ((), ()))
        dwq_ref[...] += lax.dot_general(nq.astype(BF16), dqp, tn, preferred_element_type=F32)
        dwkv_ref[...] += lax.dot_general(nkv.astype(BF16), dkv_b, tn, preferred_element_type=F32)
        dcq, dqg = vjp_q(lax.dot_general(dqp, wq_ref[...], nt, preferred_element_type=F32))
        dckv, dkvg = vjp_kv(lax.dot_general(dkv_b, wkv_ref[...], nt, preferred_element_type=F32))
        dqg_ref[...] += dqg
        dkvg_ref[...] += dkvg
        dk_sum = dkv_v[:, 0:128] + dkv_v[:, 256:384] + dkv_v[:, 512:640] + dkv_v[:, 768:896]
        lane = lax.broadcasted_iota(jnp.int32, dk_sum.shape, 1)
        dkr = jnp.where(lane >= 64, _rope_t(dk_sum, c, s1, s2), 0.0)
        dp_ref[:, 0:256] = dcq.astype(dp_ref.dtype)
        dp_ref[:, 256:384] = dckv.astype(dp_ref.dtype)
        dp_ref[:, 384:512] = dkr.astype(dp_ref.dtype)

    tab_spec = pl.BlockSpec((ts, LANES), lambda b, s: (s, 0))
    const = lambda shape: pl.BlockSpec(shape, lambda b, s: (0, 0))
    return pl.pallas_call(
        body, name=name, grid=(bsz, seq // ts),
        in_specs=[_row_spec(ts, 512), _row_spec(ts, 1024), pl.BlockSpec(memory_space=pl.ANY),
                  pl.BlockSpec((None, ts, 512), lambda b, s: (b, s, P_B // 512)),
                  _vec_spec(256), _vec_spec(128), const((256, 512)), const((128, 1024)),
                  tab_spec, tab_spec, tab_spec],
        out_specs=[pl.BlockSpec((None, ts, 512), lambda b, s: (b, s, P_B // 512)),
                   _vec_spec(256), _vec_spec(128), const((256, 512)), const((128, 1024))],
        out_shape=[jax.ShapeDtypeStruct(dproj.shape, dproj.dtype), jax.ShapeDtypeStruct((1, 256), F32),
                   jax.ShapeDtypeStruct((1, 128), F32), jax.ShapeDtypeStruct((256, 512), F32),
                   jax.ShapeDtypeStruct((128, 1024), F32)],
        input_output_aliases={2: 0},
        compiler_params=_cparams(("arbitrary", "arbitrary")),
    )(dq, dkv, dproj, proj, qg, kvg, wq, wkv, *tabs)


def _fox_gate(proj, bf, name):
    bsz, seq, _ = proj.shape
    n_blk = seq // LANES

    def body(x_ref, bf_ref, f_ref):
        r_i = lax.broadcasted_iota(jnp.int32, (LANES, LANES), 0)
        c_i = lax.broadcasted_iota(jnp.int32, (LANES, LANES), 1)
        tril = (r_i >= c_i).astype(F32)
        bias = bf_ref[...]

        def blk(i, carry):
            r = pl.multiple_of(i * LANES, LANES)
            lf = _log_sigmoid(x_ref[pl.ds(r, LANES), :] + bias)
            f_ref[pl.ds(r, LANES), :] = jnp.dot(tril, lf, precision=HI, preferred_element_type=F32) + carry
            return carry + jnp.sum(lf, axis=0, keepdims=True)

        lax.fori_loop(0, n_blk, blk, jnp.zeros((1, LANES), F32))

    return pl.pallas_call(
        body, name=name, grid=(bsz,),
        in_specs=[pl.BlockSpec((None, seq, LANES), lambda b: (b, 0, P_CF // LANES)),
                  pl.BlockSpec((1, LANES), lambda b: (0, 0))],
        out_specs=pl.BlockSpec((None, seq, LANES), lambda b: (b, 0, 0)),
        out_shape=jax.ShapeDtypeStruct((bsz, seq, LANES), F32),
        compiler_params=_cparams(("parallel",)),
    )(proj, bf)


def _fox_gate_bwd(dfq, dfk_cols, dproj, proj, bf, name):
    bsz, seq, _ = proj.shape
    n_blk = seq // LANES

    def body(dfq_ref, dfk_ref, dp_any, x_ref, bf_ref, dp_ref, dbf_ref):
        del dp_any

        @pl.when(pl.program_id(0) == 0)
        def _():
            dbf_ref[...] = jnp.zeros_like(dbf_ref)

        r_i = lax.broadcasted_iota(jnp.int32, (LANES, LANES), 0)
        c_i = lax.broadcasted_iota(jnp.int32, (LANES, LANES), 1)
        triu = (r_i <= c_i).astype(F32)
        bias = bf_ref[...]

        def blk(t, carry):
            tail, dbf = carry
            r = pl.multiple_of((n_blk - 1 - t) * LANES, LANES)
            dc = dfk_ref[pl.ds(r, LANES), :]
            for hd in range(N_HEADS):
                dc = dc + jnp.where(c_i == hd, dfq_ref[hd, pl.ds(r, LANES), :], 0.0)
            dlf = jnp.dot(triu, dc, precision=HI, preferred_element_type=F32) + tail
            dx = dlf * (1.0 - jax.nn.sigmoid(x_ref[pl.ds(r, LANES), :] + bias))
            dp_ref[pl.ds(r, LANES), :] = dx.astype(dp_ref.dtype)
            return tail + jnp.sum(dc, axis=0, keepdims=True), dbf + jnp.sum(dx, axis=0, keepdims=True)

        z = jnp.zeros((1, LANES), F32)
        _, dbf = lax.fori_loop(0, n_blk, blk, (z, z))
        dbf_ref[...] += dbf

    return pl.pallas_call(
        body, name=name, grid=(bsz,),
        in_specs=[pl.BlockSpec((None, N_HEADS, seq, LANES), lambda b: (b, 0, 0, 0)),
                  pl.BlockSpec((None, seq, LANES), lambda b: (b, 0, 0)), pl.BlockSpec(memory_space=pl.ANY),
                  pl.BlockSpec((None, seq, LANES), lambda b: (b, 0, P_CF // LANES)),
                  pl.BlockSpec((1, LANES), lambda b: (0, 0))],
        out_specs=[pl.BlockSpec((None, seq, LANES), lambda b: (b, 0, P_CF // LANES)),
                   pl.BlockSpec((1, LANES), lambda b: (0, 0))],
        out_shape=[jax.ShapeDtypeStruct(dproj.shape, dproj.dtype), jax.ShapeDtypeStruct((1, LANES), F32)],
        input_output_aliases={2: 0},
        compiler_params=_cparams(("arbitrary",)),
    )(dfq, dfk_cols, dproj, proj, bf)


def _gate_terms(fc_ref, fr_ref, h, tq, tk):
    lane = lax.broadcasted_iota(jnp.int32, (tq, LANES), 1)
    fcol = jnp.sum(jnp.where(lane == h, fc_ref[...], 0.0), axis=1, keepdims=True)
    sub = lax.broadcasted_iota(jnp.int32, (8, tk), 0)
    frow = jnp.sum(jnp.where(sub == h, fr_ref[...], 0.0), axis=0, keepdims=True)
    return fcol - frow


def _scores(q_ref, k_ref, gate_refs, scale, h, masked, tq, tk):
    q = (q_ref[...].astype(F32) * scale).astype(BF16)
    s = lax.dot_general(q, k_ref[...].astype(BF16), _DN["nt"], preferred_element_type=F32)
    if gate_refs is not None:
        s = s + _gate_terms(gate_refs[0], gate_refs[1], h, tq, tk)
    if masked is not False:
        r_i = lax.broadcasted_iota(jnp.int32, (tq, tk), 0)
        c_i = lax.broadcasted_iota(jnp.int32, (tq, tk), 1)
        keep = c_i <= r_i
        s = jnp.where(keep if masked is True else jnp.logical_or(jnp.logical_not(masked), keep), s, NEG)
    return s, q


def _lanes(col):
    return jnp.broadcast_to(col, (col.shape[0], LANES))


def _attn_fwd(qa, q0, kva, kv0, mo, o0, gates, scale, name, tq=None):
    bsz, seq, _ = qa.shape
    tq = ATTN_TILE if tq is None else tq
    n_q = seq // tq
    gated = gates is not None

    def body(*refs):
        q_ref, k_ref, v_ref = refs[:3]
        gate_refs = refs[3:5] if gated else None
        o_ref, lse_ref, m_s, l_s, acc_s = refs[-5:]
        h, i, j = pl.program_id(1), pl.program_id(2), pl.program_id(3)

        @pl.when(j == 0)
        def _():
            m_s[...] = jnp.full_like(m_s, NEG)
            l_s[...] = jnp.zeros_like(l_s)
            acc_s[...] = jnp.zeros_like(acc_s)

        def step(masked):
            s, _ = _scores(q_ref, k_ref, gate_refs, scale, h, masked, tq, tq)
            m_prev = m_s[...]
            m_new = jnp.maximum(m_prev, jnp.max(s, axis=1, keepdims=True))
            alpha = jnp.exp(m_prev - m_new)
            p = jnp.exp(s - m_new)
            l_s[...] = alpha * l_s[...] + jnp.sum(p, axis=1, keepdims=True)
            acc_s[...] = alpha * acc_s[...] + jnp.dot(p.astype(BF16), v_ref[...].astype(BF16),
                                                      preferred_element_type=F32)
            m_s[...] = m_new

        @pl.when(j <= i)
        def _():
            step(j == i)

        @pl.when(j == i)
        def _():
            o_ref[...] = (acc_s[...] / l_s[...]).astype(o_ref.dtype)
            lse_ref[...] = _lanes(m_s[...] + jnp.log(l_s[...]))

    blk = (None, tq, LANES)
    in_specs = [pl.BlockSpec(blk, lambda b, h, i, j: (b, i, q0 + h)),
                pl.BlockSpec(blk, lambda b, h, i, j: (b, jnp.minimum(j, i), kv0 + 2 * h)),
                pl.BlockSpec(blk, lambda b, h, i, j: (b, jnp.minimum(j, i), kv0 + 2 * h + 1))]
    args = [qa, kva, kva]
    if gated:
        in_specs += [pl.BlockSpec(blk, lambda b, h, i, j: (b, i, 0)),
                     pl.BlockSpec((None, 8, tq), lambda b, h, i, j: (b, 0, jnp.minimum(j, i)))]
        args += list(gates)
    in_specs.append(pl.BlockSpec(memory_space=pl.ANY))
    args.append(mo)
    return pl.pallas_call(
        body, name=name, grid=(bsz, N_HEADS, n_q, n_q), in_specs=in_specs,
        out_specs=[pl.BlockSpec(blk, lambda b, h, i, j: (b, i, o0 + h)),
                   pl.BlockSpec((None, None, tq, LANES), lambda b, h, i, j: (b, h, i, 0))],
        out_shape=[jax.ShapeDtypeStruct(mo.shape, mo.dtype),
                   jax.ShapeDtypeStruct((bsz, N_HEADS, seq, LANES), F32)],
        scratch_shapes=[pltpu.VMEM((tq, 1), F32), pltpu.VMEM((tq, 1), F32), pltpu.VMEM((tq, LANES), F32)],
        input_output_aliases={len(args) - 1: 0},
        compiler_params=_cparams(("parallel", "parallel", "parallel", "arbitrary")),
    )(*args)


def _attn_bwd_q(qa, q0, kva, kv0, mo, dmo, o0, lse, gates, scale, out, out0, name, tq=None):
    bsz, seq, _ = qa.shape
    tq = ATTN_TILE if tq is None else tq
    n_q = seq // tq
    gated = gates is not None
    aliased = not isinstance(out, jax.ShapeDtypeStruct)

    def body(*refs):
        q_ref, k_ref, v_ref, o_ref, do_ref, lse_ref = refs[:6]
        gate_refs = refs[6:8] if gated else None
        dq_ref, delta_ref, dfq_ref, acc_s, dl_s, df_s = refs[-6:]
        h, i, j = pl.program_id(1), pl.program_id(2), pl.program_id(3)

        @pl.when(j == 0)
        def _():
            acc_s[...] = jnp.zeros_like(acc_s)
            df_s[...] = jnp.zeros_like(df_s)
            dl_s[...] = jnp.sum(do_ref[...] * o_ref[...].astype(F32), axis=1, keepdims=True)

        def step(masked):
            s, _ = _scores(q_ref, k_ref, gate_refs, scale, h, masked, tq, tq)
            p = jnp.exp(s - lse_ref[:, 0:1])
            dp = lax.dot_general(do_ref[...].astype(BF16), v_ref[...].astype(BF16), _DN["nt"],
                                 preferred_element_type=F32)
            ds = p * (dp - dl_s[...])
            acc_s[...] += jnp.dot(ds.astype(BF16), k_ref[...].astype(BF16), preferred_element_type=F32)
            df_s[...] += jnp.sum(ds, axis=1, keepdims=True)

        @pl.when(j <= i)
        def _():
            step(j == i)

        @pl.when(j == i)
        def _():
            dq_ref[...] = (acc_s[...] * scale).astype(dq_ref.dtype)
            delta_ref[...] = _lanes(dl_s[...])
            dfq_ref[...] = _lanes(df_s[...])

    blk = (None, tq, LANES)
    col = pl.BlockSpec((None, None, tq, LANES), lambda b, h, i, j: (b, h, i, 0))
    in_specs = [pl.BlockSpec(blk, lambda b, h, i, j: (b, i, q0 + h)),
                pl.BlockSpec(blk, lambda b, h, i, j: (b, jnp.minimum(j, i), kv0 + 2 * h)),
                pl.BlockSpec(blk, lambda b, h, i, j: (b, jnp.minimum(j, i), kv0 + 2 * h + 1)),
                pl.BlockSpec(blk, lambda b, h, i, j: (b, i, o0 + h)),
                pl.BlockSpec(blk, lambda b, h, i, j: (b, i, o0 + h)), col]
    args = [qa, kva, kva, mo, dmo, lse]
    if gated:
        in_specs += [pl.BlockSpec(blk, lambda b, h, i, j: (b, i, 0)),
                     pl.BlockSpec((None, 8, tq), lambda b, h, i, j: (b, 0, jnp.minimum(j, i)))]
        args += list(gates)
    aliases = {}
    if aliased:
        in_specs.append(pl.BlockSpec(memory_space=pl.ANY))
        args.append(out)
        aliases = {len(args) - 1: 0}
    vec = jax.ShapeDtypeStruct((bsz, N_HEADS, seq, LANES), F32)
    return pl.pallas_call(
        body, name=name, grid=(bsz, N_HEADS, n_q, n_q), in_specs=in_specs,
        out_specs=[pl.BlockSpec(blk, lambda b, h, i, j: (b, i, out0 + h)), col, col],
        out_shape=[jax.ShapeDtypeStruct(out.shape, out.dtype), vec, vec],
        scratch_shapes=[pltpu.VMEM((tq, LANES), F32), pltpu.VMEM((tq, 1), F32), pltpu.VMEM((tq, 1), F32)],
        input_output_aliases=aliases,
        compiler_params=_cparams(("parallel", "parallel", "parallel", "arbitrary")),
    )(*args)


def _attn_bwd_kv(qa, q0, kva, kv0, dmo, o0, lse, delta, gates, scale, out, out0, name, tq=None):
    bsz, seq, _ = qa.shape
    tq = ATTN_TILE if tq is None else tq
    n_q = seq // tq
    gated = gates is not None
    aliased = not isinstance(out, jax.ShapeDtypeStruct)

    def body(*refs):
        q_ref, k_ref, v_ref, do_ref, lse_ref, dl_ref = refs[:6]
        gate_refs = refs[6:8] if gated else None
        dkv_ref, dfk_ref, dk_s, dv_s, df_s = refs[-5:]
        h, j, i = pl.program_id(1), pl.program_id(2), pl.program_id(3)

        @pl.when(i == 0)
        def _():
            dk_s[...] = jnp.zeros_like(dk_s)
            dv_s[...] = jnp.zeros_like(dv_s)
            df_s[...] = jnp.zeros_like(df_s)

        def step(masked):
            s, q = _scores(q_ref, k_ref, gate_refs, scale, h, masked, tq, tq)
            p = jnp.exp(s - lse_ref[:, 0:1])
            do_b = do_ref[...].astype(BF16)
            dp = lax.dot_general(do_b, v_ref[...].astype(BF16), _DN["nt"], preferred_element_type=F32)
            ds = p * (dp - dl_ref[:, 0:1])
            dv_s[...] += lax.dot_general(p.astype(BF16), do_b, _DN["tn"], preferred_element_type=F32)
            dk_s[...] += lax.dot_general(ds.astype(BF16), q, _DN["tn"], preferred_element_type=F32)
            df_s[...] -= jnp.sum(ds, axis=0, keepdims=True)

        @pl.when(i > j)
        def _():
            step(False)

        @pl.when(i == j)
        def _():
            step(True)

        @pl.when(i == n_q - 1)
        def _():
            dkv_ref[:, 0:LANES] = dk_s[...].astype(dkv_ref.dtype)
            dkv_ref[:, LANES:2 * LANES] = dv_s[...].astype(dkv_ref.dtype)
            dfk_ref[...] = df_s[...]

    blk = (None, tq, LANES)
    col = pl.BlockSpec((None, None, tq, LANES), lambda b, h, j, i: (b, h, jnp.maximum(i, j), 0))
    in_specs = [pl.BlockSpec(blk, lambda b, h, j, i: (b, jnp.maximum(i, j), q0 + h)),
                pl.BlockSpec(blk, lambda b, h, j, i: (b, j, kv0 + 2 * h)),
                pl.BlockSpec(blk, lambda b, h, j, i: (b, j, kv0 + 2 * h + 1)),
                pl.BlockSpec(blk, lambda b, h, j, i: (b, jnp.maximum(i, j), o0 + h)), col, col]
    args = [qa, kva, kva, dmo, lse, delta]
    if gated:
        in_specs += [pl.BlockSpec(blk, lambda b, h, j, i: (b, jnp.maximum(i, j), 0)),
                     pl.BlockSpec((None, 8, tq), lambda b, h, j, i: (b, 0, j))]
        args += list(gates)
    aliases = {}
    if aliased:
        in_specs.append(pl.BlockSpec(memory_space=pl.ANY))
        args.append(out)
        aliases = {len(args) - 1: 0}
    return pl.pallas_call(
        body, name=name, grid=(bsz, N_HEADS, n_q, n_q), in_specs=in_specs,
        out_specs=[pl.BlockSpec((None, tq, 2 * LANES), lambda b, h, j, i: (b, j, out0 + h)),
                   pl.BlockSpec((None, None, 1, tq), lambda b, h, j, i: (b, h, 0, j))],
        out_shape=[jax.ShapeDtypeStruct(out.shape, out.dtype), jax.ShapeDtypeStruct((bsz, N_HEADS, 1, seq), F32)],
        scratch_shapes=[pltpu.VMEM((tq, LANES), F32), pltpu.VMEM((tq, LANES), F32), pltpu.VMEM((1, tq), F32)],
        input_output_aliases=aliases,
        compiler_params=_cparams(("parallel", "parallel", "parallel", "arbitrary")),
    )(*args)


def _gmlp_fn(uv, lng, lnb, ws, bst):
    u = jax.nn.gelu(uv[:, 0:GROUP_WIDTH])
    gv = jax.nn.gelu(uv[:, GROUP_WIDTH:2 * GROUP_WIDTH])
    mu = jnp.mean(gv, axis=-1, keepdims=True)
    vc = gv - mu
    var = jnp.mean(vc * vc, axis=-1, keepdims=True)
    vln = vc * lax.rsqrt(var + LN_EPS) * lng + lnb
    r_i = lax.broadcasted_iota(jnp.int32, (D_CHUNK, D_CHUNK), 0)
    c_i = lax.broadcasted_iota(jnp.int32, (D_CHUNK, D_CHUNK), 1)
    lane_g = lax.broadcasted_iota(jnp.int32, (D_CHUNK, GROUP_WIDTH), 1) // HEAD_DIM
    e_r = lax.broadcasted_iota(jnp.int32, (LANES, GROUP_WIDTH), 0)
    e_c = lax.broadcasted_iota(jnp.int32, (LANES, GROUP_WIDTH), 1)
    expand = (e_r == e_c // HEAD_DIM).astype(F32)
    mixed = jnp.dot(bst, expand, precision=HI, preferred_element_type=F32)
    for g in range(4):
        w = jnp.where(r_i >= c_i, ws[g], 0.0)
        mixed = mixed + jnp.where(lane_g == g, _bdot(w, vln, "nn"), 0.0)
    return u * mixed


def _gmlp_fwd(proj, mo, lng, lnb, ws, bst, name):
    bsz, seq, _ = proj.shape

    def body(p_ref, mo_any, lng_ref, lnb_ref, ws_ref, bst_ref, o_ref):
        del mo_any
        o_ref[...] = _gmlp_fn(p_ref[...], lng_ref[...], lnb_ref[...], ws_ref[...], bst_ref[...]).astype(o_ref.dtype)

    return pl.pallas_call(
        body, name=name, grid=(bsz, seq // D_CHUNK),
        in_specs=[pl.BlockSpec((None, D_CHUNK, 512), lambda b, s: (b, s, P_D // 512)),
                  pl.BlockSpec(memory_space=pl.ANY), _vec_spec(256), _vec_spec(256),
                  pl.BlockSpec((4, D_CHUNK, D_CHUNK), lambda b, s: (0, 0, 0)),
                  pl.BlockSpec((D_CHUNK, LANES), lambda b, s: (0, 0))],
        out_specs=pl.BlockSpec((None, D_CHUNK, GROUP_WIDTH), lambda b, s: (b, s, 1280 // GROUP_WIDTH)),
        out_shape=jax.ShapeDtypeStruct(mo.shape, mo.dtype),
        input_output_aliases={1: 0},
        compiler_params=_cparams(("parallel", "parallel")),
    )(proj, mo, lng, lnb, ws, bst)


def _gmlp_bwd(dmo, dproj, proj, lng, lnb, ws, bst, name):
    bsz, seq, _ = proj.shape

    def body(do_ref, dp_any, p_ref, lng_ref, lnb_ref, ws_ref, bst_ref, dp_ref, dlg_ref, dlb_ref, dws_ref, dbst_ref):
        del dp_any
        first = jnp.logical_and(pl.program_id(0) == 0, pl.program_id(1) == 0)

        @pl.when(first)
        def _():
            dlg_ref[...] = jnp.zeros_like(dlg_ref)
            dlb_ref[...] = jnp.zeros_like(dlb_ref)
            dws_ref[...] = jnp.zeros_like(dws_ref)
            dbst_ref[...] = jnp.zeros_like(dbst_ref)

        _, vjp = jax.vjp(_gmlp_fn, p_ref[...], lng_ref[...], lnb_ref[...], ws_ref[...], bst_ref[...])
        duv, dlg, dlb, dws, dbst = vjp(do_ref[...])
        dp_ref[...] = duv.astype(dp_ref.dtype)
        dlg_ref[...] += dlg
        dlb_ref[...] += dlb
        dws_ref[...] += dws
        dbst_ref[...] += dbst

    const2 = lambda shape: pl.BlockSpec(shape, lambda b, s: (0,) * len(shape))
    return pl.pallas_call(
        body, name=name, grid=(bsz, seq // D_CHUNK),
        in_specs=[pl.BlockSpec((None, D_CHUNK, GROUP_WIDTH), lambda b, s: (b, s, 1280 // GROUP_WIDTH)),
                  pl.BlockSpec(memory_space=pl.ANY),
                  pl.BlockSpec((None, D_CHUNK, 512), lambda b, s: (b, s, P_D // 512)),
                  _vec_spec(256), _vec_spec(256), const2((4, D_CHUNK, D_CHUNK)), const2((D_CHUNK, LANES))],
        out_specs=[pl.BlockSpec((None, D_CHUNK, 512), lambda b, s: (b, s, P_D // 512)),
                   _vec_spec(256), _vec_spec(256), const2((4, D_CHUNK, D_CHUNK)), const2((D_CHUNK, LANES))],
        out_shape=[jax.ShapeDtypeStruct(dproj.shape, dproj.dtype), jax.ShapeDtypeStruct((1, 256), F32),
                   jax.ShapeDtypeStruct((1, 256), F32), jax.ShapeDtypeStruct((4, D_CHUNK, D_CHUNK), F32),
                   jax.ShapeDtypeStruct((D_CHUNK, LANES), F32)],
        input_output_aliases={1: 0},
        compiler_params=_cparams(("arbitrary", "arbitrary")),
    )(dmo, dproj, proj, lng, lnb, ws, bst)


def _ada_fwd(c_all, ada_w, name):
    n_b = c_all.shape[0]
    depth, d, cols = ada_w.shape

    def body(c_ref, w_ref, o_ref):
        cv = c_ref[...]
        act = (cv * jax.nn.sigmoid(cv)).astype(BF16)
        o_ref[...] = jnp.dot(act, w_ref[...].astype(BF16), preferred_element_type=F32)

    return pl.pallas_call(
        body, name=name, grid=(depth,),
        in_specs=[pl.BlockSpec((n_b, d), lambda l: (0, 0)), pl.BlockSpec((None, d, cols), lambda l: (l, 0, 0))],
        out_specs=pl.BlockSpec((None, n_b, cols), lambda l: (l, 0, 0)),
        out_shape=jax.ShapeDtypeStruct((depth, n_b, cols), F32),
        compiler_params=_cparams(("parallel",)),
    )(c_all, ada_w)


def _ada_bwd(c_all, dmod_cols, dmod_full, name):
    n_b, d = c_all.shape
    depth, _, cols = dmod_cols.shape
    full = dmod_full.shape[-1]

    def body(c_ref, dm_ref, df_ref, gw_ref, gb_ref):
        cv = c_ref[...]
        act = (cv * jax.nn.sigmoid(cv)).astype(BF16)
        gw_ref[...] = lax.dot_general(act, dm_ref[...].astype(BF16), (((0,), (0,)), ((), ())),
                                      preferred_element_type=F32)
        gb_ref[...] = jnp.sum(df_ref[...], axis=0, keepdims=True)

    return pl.pallas_call(
        body, name=name, grid=(depth,),
        in_specs=[pl.BlockSpec((n_b, d), lambda l: (0, 0)), pl.BlockSpec((None, n_b, cols), lambda l: (l, 0, 0)),
                  pl.BlockSpec((None, n_b, full), lambda l: (l, 0, 0))],
        out_specs=[pl.BlockSpec((None, d, cols), lambda l: (l, 0, 0)),
                   pl.BlockSpec((None, 1, full), lambda l: (l, 0, 0))],
        out_shape=[jax.ShapeDtypeStruct((depth, d, cols), F32), jax.ShapeDtypeStruct((depth, 1, full), F32)],
        compiler_params=_cparams(("parallel",)),
    )(c_all, dmod_cols, dmod_full)


def _adamw(gparts, own, w, m, v, name, layer=0, prev=None):
    n_p, rows, cols = gparts.shape
    assert w.shape[1:] == (rows, cols)
    tr = rows
    if rows > 512:
        tr = next(c for c in range(512, 7, -8) if rows % c == 0)
    has_own = own is not None
    n_prev = 0 if prev is None else 4

    def body(*refs):
        g_ref = refs[0]
        own_ref = refs[1] if has_own else None
        w_ref, m_ref, v_ref = refs[1 + has_own:4 + has_own]
        go_ref, do_ref, mo_ref, vo_ref = refs[4 + has_own + n_prev:]
        if has_own:
            g = own_ref[...].astype(F32) + g_ref[0].astype(F32)
        else:
            g = g_ref[0].astype(F32)
        for p in range(1, n_p):
            g = g + g_ref[p].astype(F32)
        m_new = ADAM_B1 * m_ref[...] + (1.0 - ADAM_B1) * g
        v_new = ADAM_B2 * v_ref[...] + (1.0 - ADAM_B2) * (g * g)
        m_hat = m_new / (1.0 - ADAM_B1 ** ADAM_STEP)
        v_hat = v_new / (1.0 - ADAM_B2 ** ADAM_STEP)
        go_ref[...] = g
        do_ref[...] = -ADAM_LR * (m_hat / (jnp.sqrt(v_hat) + ADAM_EPS) + ADAM_WD * w_ref[...])
        mo_ref[...] = m_new
        vo_ref[...] = v_new

    spec = pl.BlockSpec((None, tr, cols), lambda i: (layer, i, 0))
    in_specs = [pl.BlockSpec((n_p, tr, cols), lambda i: (0, i, 0))]
    args = [gparts]
    if has_own:
        in_specs.append(pl.BlockSpec((tr, cols), lambda i: (i, 0)))
        args.append(own)
    in_specs += [spec, spec, spec]
    args += [w, m, v]
    aliases = {}
    if prev is not None:
        aliases = {len(args) + k: k for k in range(4)}
        in_specs += [pl.BlockSpec(memory_space=pl.ANY)] * 4
        args += list(prev)
    shp = jax.ShapeDtypeStruct(w.shape, F32)
    return pl.pallas_call(
        body, name=name, grid=(rows // tr,), in_specs=in_specs,
        out_specs=[spec, spec, spec, spec], out_shape=[shp, shp, shp, shp], input_output_aliases=aliases,
        compiler_params=_cparams(("parallel",)),
    )(*args)


def _sum_parts(parts, name):
    n_p, rows, cols = parts.shape
    tr = 256 if rows % 256 == 0 else rows

    def body(p_ref, o_ref):
        acc = p_ref[0]
        for p in range(1, n_p):
            acc = acc + p_ref[p]
        o_ref[...] = acc

    return pl.pallas_call(
        body, name=name, grid=(rows // tr,),
        in_specs=[pl.BlockSpec((n_p, tr, cols), lambda i: (0, i, 0))],
        out_specs=pl.BlockSpec((tr, cols), lambda i: (i, 0)),
        out_shape=jax.ShapeDtypeStruct((rows, cols), F32),
        compiler_params=_cparams(("parallel",)),
    )(parts)


def _all_gather(arrs, name):
    n = len(arrs)

    def body(*refs):
        in_refs, out_refs = refs[:n], refs[n:2 * n]
        send_sems, recv_sems, loc_sems = refs[2 * n:]
        x, y, c = lax.axis_index("x"), lax.axis_index("y"), lax.axis_index("c")
        me, sibling = (x, y, c), (x, y, 1 - c)
        chips = [(1 - x, y), (x, 1 - y), (1 - x, 1 - y)]

        def copy(a, k, block, to, src=None):
            slot = out_refs[a].at[4 * block[0] + 2 * block[1] + block[2]]
            return pltpu.make_async_remote_copy(
                src_ref=slot if src is None else src, dst_ref=slot, send_sem=send_sems.at[a, k],
                recv_sem=recv_sems.at[a, k], device_id=to, device_id_type=pl.DeviceIdType.MESH)

        mine = [pltpu.make_async_copy(in_refs[a], out_refs[a].at[4 * x + 2 * y + c], loc_sems.at[a])
                for a in range(n)]
        for cp in mine:
            cp.start()
        first = []
        for a in range(n):
            first.append(copy(a, 0, me, sibling, src=in_refs[a]))
            first += [copy(a, 1 + j, me, (*chip, c), src=in_refs[a]) for j, chip in enumerate(chips)]
        for cp in first:
            cp.start()
        passed = []
        for j, chip in enumerate(chips):
            for a in range(n):
                copy(a, 1 + j, (*chip, c), me).wait_recv()
                cp = copy(a, 4 + j, (*chip, c), sibling)
                cp.start()
                passed.append(cp)
        for a in range(n):
            copy(a, 0, sibling, me).wait_recv()
        for j, chip in enumerate(chips):
            for a in range(n):
                copy(a, 4 + j, (*chip, 1 - c), me).wait_recv()
        for cp in first + passed:
            cp.wait_send()
        for cp in mine:
            cp.wait()

    any_spec = pl.BlockSpec(memory_space=pl.ANY)
    return pl.pallas_call(
        body, name=name, in_specs=[any_spec] * n, out_specs=[any_spec] * n,
        out_shape=[jax.ShapeDtypeStruct((N_DEV,) + a.shape, a.dtype) for a in arrs],
        scratch_shapes=[pltpu.SemaphoreType.DMA((n, N_DEV - 1)), pltpu.SemaphoreType.DMA((n, N_DEV - 1)),
                        pltpu.SemaphoreType.DMA((n,))],
    )(*arrs)


def _flip_peers():
    x, y, c = lax.axis_index("x"), lax.axis_index("y"), lax.axis_index("c")
    peers = []
    for fx, fy, fc in [(fx, fy, fc) for fx in (0, 1) for fy in (0, 1) for fc in (0, 1)][1:]:
        px, py, pc = (1 - x if fx else x), (1 - y if fy else y), (1 - c if fc else c)
        peers.append(((px, py, pc), 4 * px + 2 * py + pc))
    return 4 * x + 2 * y + c, peers


def _push_start(srcs, name, whole=False):
    n, n_peer = len(srcs), N_DEV - 1
    if whole:
        me_w = 4 * lax.axis_index("x") + 2 * lax.axis_index("y") + lax.axis_index("c")
        lands = [lax.dynamic_update_slice_in_dim(jnp.zeros((N_DEV,) + a.shape, a.dtype), a[None], me_w, axis=0)
                 for a in srcs]
    else:
        lands = [jnp.zeros(a.shape, a.dtype) for a in srcs]

    def body(*refs):
        src_refs, land_refs = refs[:n], refs[n:2 * n]
        send_sems, recv_sems = refs[2 * n], refs[2 * n + 1]
        token = refs[-1]
        me, peers = _flip_peers()
        for k, (dev, idx) in enumerate(peers):
            for a in range(n):
                pltpu.make_async_remote_copy(
                    src_ref=src_refs[a] if whole else src_refs[a].at[idx], dst_ref=land_refs[a].at[me],
                    send_sem=send_sems.at[a * n_peer + k], recv_sem=recv_sems.at[a * n_peer + k], device_id=dev,
                    device_id_type=pl.DeviceIdType.MESH).start()
        token[...] = jnp.zeros_like(token)

    hbm = pl.BlockSpec(memory_space=pltpu.HBM)
    sem = pl.BlockSpec(memory_space=pltpu.SEMAPHORE)
    arrs = list(srcs) + lands
    res = pl.pallas_call(
        body, name=name, in_specs=[hbm] * (2 * n),
        out_specs=(sem, sem, *[hbm] * (2 * n), pl.BlockSpec(memory_space=pltpu.VMEM)),
        out_shape=(pltpu.SemaphoreType.DMA((n * n_peer,)), pltpu.SemaphoreType.DMA((n * n_peer,)),
                   *[pltpu.HBM(a.shape, a.dtype) for a in arrs], jax.ShapeDtypeStruct((8, LANES), F32)),
        input_output_aliases={i: 2 + i for i in range(2 * n)},
        compiler_params=pltpu.CompilerParams(has_side_effects=pltpu.SideEffectType.DATAFLOW_SIDE_EFFECTING),
    )(*[pltpu.with_memory_space_constraint(a, pltpu.HBM) for a in arrs])
    return res[0], res[1], list(res[2:2 + n]), list(res[2 + n:2 + 2 * n]), res[-1]


def _push_wait(send_sems, recv_sems, srcs, lands, after, name, whole=False):
    n, n_peer = len(srcs), N_DEV - 1

    def body(*refs):
        src_refs, land_refs = refs[:n], refs[n:2 * n]
        send_s, recv_s = refs[2 * n], refs[2 * n + 1]
        _, peers = _flip_peers()
        for k, (dev, idx) in enumerate(peers):
            for a in range(n):
                cp = pltpu.make_async_remote_copy(
                    src_ref=src_refs[a] if whole else src_refs[a].at[idx], dst_ref=land_refs[a].at[idx],
                    send_sem=send_s.at[a * n_peer + k],
                    recv_sem=recv_s.at[a * n_peer + k], device_id=dev, device_id_type=pl.DeviceIdType.MESH)
                cp.wait_send()
                cp.wait_recv()

    hbm = pl.BlockSpec(memory_space=pltpu.HBM)
    sem = pl.BlockSpec(memory_space=pltpu.SEMAPHORE)
    arrs = list(srcs) + list(lands)
    res = pl.pallas_call(
        body, name=name, in_specs=[hbm] * (2 * n) + [sem, sem, pl.BlockSpec(memory_space=pl.ANY)],
        out_specs=tuple([hbm] * (2 * n)), out_shape=tuple(pltpu.HBM(a.shape, a.dtype) for a in arrs),
        input_output_aliases={i: i for i in range(2 * n)},
        compiler_params=pltpu.CompilerParams(has_side_effects=pltpu.SideEffectType.DATAFLOW_SIDE_EFFECTING),
    )(*arrs, send_sems, recv_sems, after)
    return list(res[:n]), list(res[n:])


def _ffn_fwd(x, h, mod, w_in, w_out, lng, lnb, rows, tag, nxt):
    bsz, seq, d = x.shape
    t = bsz * seq
    if h is None:
        h = _modulate(x, mod, rows[0], rows[1], f"modulate_{tag}")
    z, a = _ffn_in_swiglu(h.reshape(t, d), w_in, f"ffn_in_{tag}")
    f = _matmul(a, w_out, mode="nn", group_out=False, out_dtype=F32, tm=1024, tk=a.shape[2],
                name=f"ffn_out_{tag}").reshape(bsz, seq, d)
    y, h_next = _res_ln(x, f, mod, lng, lnb, rows[2], 0.5, f"res_ln_{tag}", nxt)
    return y, h_next, (x, h, z, a, f)


def _tied(mod, tie):
    return mod if tie is None else mod + tie


def _ffn_bwd(dy, saved, mod, w_in, w_out, lng, lnb, rows, tag, ready):
    x, h, z, a, f = saved
    bsz, seq, d = x.shape
    t = bsz * seq
    dx_res, df, dgate, dlg, dlb = _res_ln_bwd(dy, x, f, mod, lng, lnb, rows[2], 0.5, f"res_ln_bwd_{tag}")
    df2 = df.reshape(1, t, d)
    dw_out = _matmul(a, df2, mode="tn", group_out=True, out_dtype=BF16, tm=a.shape[2], tk=min(t, 2048),
                     name=f"ffn_out_dw_{tag}")
    tie_out = ready(f"{tag}_out", dw_out)
    dz = _ffn_out_dx_swiglu(df.reshape(t, d), w_out, z, f"ffn_out_dx_{tag}").reshape(N_DEV, t, -1)
    dw_in = _matmul(h.reshape(1, t, d), dz, mode="tn", group_out=True, out_dtype=BF16, tm=d, tk=min(t, 2048),
                    name=f"ffn_in_dw_{tag}")
    tie_in = ready(f"{tag}_in", dw_in)
    dh = _matmul(dz, w_in, mode="nt", group_out=False, out_dtype=F32, tm=1024, tk=dz.shape[2],
                 name=f"ffn_in_dx_{tag}").reshape(bsz, seq, d)
    dx, dsh, dsc = _modulate_bwd(dh, x, _tied(_tied(mod, tie_out), tie_in), dx_res, rows[1],
                                 f"modulate_bwd_{tag}")
    return dx, (dsh, dsc, dgate), dw_in, dw_out, dlg, dlb


def _mixer_fwd(x, h, mod, wts, small, lng, lnb, layer, tabs):
    bsz, seq, d = x.shape
    t = bsz * seq
    proj = _matmul(h.reshape(1, t, d), wts["mix_in"][None], mode="nn", group_out=True, out_dtype=F32, tm=512, tk=d,
                   name="mix_in").reshape(bsz, seq, PACK_W)
    mo, states = _hgrn_fwd(proj, small["lb_logits8"], small["hgrn_norm_g"], layer, f"hgrn_fwd_l{layer}")
    q, kv = _mla_pre(proj, small["q_norm_g"], small["kv_norm_g"], wts["uq"], wts["ukv"], tabs, "mla_pre")
    mla_scale = float((B_NOPE + B_ROPE) ** -0.5)
    mo, lse_b = _attn_fwd(q, 0, kv, 0, mo, 2, None, mla_scale, "mla_attn_fwd")
    fg = _fox_gate(proj, small["fox_b_f"], "fox_gate")
    gates = (fg, jnp.swapaxes(fg[:, :, 0:8], 1, 2))
    fox_scale = float(HEAD_DIM ** -0.5)
    mo, lse_c = _attn_fwd(proj, P_CQ // LANES, proj, P_CKV // LANES, mo, 6, gates, fox_scale, "fox_attn_fwd")
    mo = _gmlp_fwd(proj, mo, small["gmlp_ln_g"], small["gmlp_ln_b"], small["gmlp_w_s"], small["gmlp_bst"],
                   "gmlp_fwd")
    mixed = _matmul(mo.reshape(1, t, MO_W), wts["mix_out"][None], mode="nn", group_out=True, out_dtype=F32,
                    tm=1024, tk=MO_W, name="mix_out").reshape(bsz, seq, d)
    y, h_next = _res_ln(x, mixed, mod, lng, lnb, 5, 1.0, "res_ln_mix", (mod, 6, 7))
    return y, h_next, (x, h, proj, mo, states, q, kv, lse_b, gates, lse_c, mixed)


def _mixer_bwd(dy, saved, mod, wts, small, lng, lnb, layer, tabs, ready):
    x, h, proj, mo, states, q, kv, lse_b, gates, lse_c, mixed = saved
    bsz, seq, d = x.shape
    t = bsz * seq
    dx_res, dmixed, dgate, dlg, dlb = _res_ln_bwd(dy, x, mixed, mod, lng, lnb, 5, 1.0, "res_ln_bwd_mix")
    dm2 = dmixed.reshape(1, t, d)
    dmo = _matmul(dm2, wts["mix_out"][None], mode="nt", group_out=True, out_dtype=F32, tm=1024, tk=d,
                  name="mix_out_dx").reshape(bsz, seq, MO_W)
    dw_out = _matmul(mo.reshape(1, t, MO_W), dm2, mode="tn", group_out=True, out_dtype=F32, tm=512, tk=min(t, 2048),
                     name="mix_out_dw")[0]
    tie_out = ready("mix_out", dw_out)
    g = {}
    dproj, g["lb_logits8"], g["hgrn_norm_g"] = _hgrn_bwd(dmo, proj, states, small["lb_logits8"],
                                                         small["hgrn_norm_g"], layer, f"hgrn_bwd_l{layer}")
    mla_scale = float((B_NOPE + B_ROPE) ** -0.5)
    dq, delta_b, _ = _attn_bwd_q(q, 0, kv, 0, mo, dmo, 2, lse_b, None, mla_scale,
                                 jax.ShapeDtypeStruct((bsz, seq, 512), F32), 0, "mla_attn_bwd_q")
    dkv, _ = _attn_bwd_kv(q, 0, kv, 0, dmo, 2, lse_b, delta_b, None, mla_scale,
                          jax.ShapeDtypeStruct((bsz, seq, 1024), F32), 0, "mla_attn_bwd_kv")
    dproj, g["q_norm_g"], g["kv_norm_g"], g["uq"], g["ukv"] = _mla_pre_bwd(
        dq, dkv, dproj, proj, small["q_norm_g"], small["kv_norm_g"], wts["uq"], wts["ukv"], tabs, "mla_pre_bwd")
    fox_scale = float(HEAD_DIM ** -0.5)
    dproj, delta_c, dfq = _attn_bwd_q(proj, P_CQ // LANES, proj, P_CKV // LANES, mo, dmo, 6, lse_c, gates,
                                      fox_scale, dproj, P_CQ // LANES, "fox_attn_bwd_q")
    dproj, dfk = _attn_bwd_kv(proj, P_CQ // LANES, proj, P_CKV // LANES, dmo, 6, lse_c, delta_c, gates, fox_scale,
                              dproj, P_CKV // (2 * LANES), "fox_attn_bwd_kv")
    dfk_cols = jnp.pad(jnp.swapaxes(dfk[:, :, 0, :], 1, 2), ((0, 0), (0, 0), (0, LANES - N_HEADS)))
    dproj, g["fox_b_f"] = _fox_gate_bwd(dfq, dfk_cols, dproj, proj, small["fox_b_f"], "fox_gate_bwd")
    dproj, g["gmlp_ln_g"], g["gmlp_ln_b"], g["gmlp_w_s"], g["gmlp_bst"] = _gmlp_bwd(
        dmo, dproj, proj, small["gmlp_ln_g"], small["gmlp_ln_b"], small["gmlp_w_s"], small["gmlp_bst"], "gmlp_bwd")
    dp2 = dproj.reshape(1, t, PACK_W)
    dw_in = _matmul(h.reshape(1, t, d), dp2, mode="tn", group_out=True, out_dtype=BF16, tm=512, tk=1024,
                    name="mix_in_dw")[0]
    tie_in = ready("mix_in", dw_in)
    dh = _matmul(dp2, wts["mix_in"][None], mode="nt", group_out=True, out_dtype=F32, tm=512, tk=PACK_W,
                 name="mix_in_dx").reshape(bsz, seq, d)
    dx, dsh, dsc = _modulate_bwd(dh, x, _tied(_tied(mod, tie_out), tie_in), dx_res, 4, "modulate_bwd_mix")
    return dx, (dsh, dsc, dgate), dw_in, dw_out, g, dlg, dlb


def _small_views(p, layer):
    return {
        "lb_logits8": jnp.pad(p["hgrn_lb_logits"], ((0, 8 - DEPTH), (0, 0))),
        "hgrn_norm_g": p["hgrn_norm_g"][layer][None],
        "q_norm_g": p["mla_q_norm_g"][layer][None],
        "kv_norm_g": p["mla_kv_norm_g"][layer][None],
        "fox_b_f": jnp.pad(p["fox_b_f"][layer][None], ((0, 0), (0, LANES - N_HEADS))),
        "gmlp_ln_g": p["gmlp_ln_g"][layer][None],
        "gmlp_ln_b": p["gmlp_ln_b"][layer][None],
        "gmlp_w_s": p["gmlp_w_s"][layer],
        "gmlp_bst": jnp.pad(p["gmlp_b_s"][layer].T, ((0, 0), (0, LANES - N_HEADS))),
    }


def _local_step(x, mod, target, weights, p, grads_ready=None):
    bsz, seq, d = x.shape
    tabs = _rope_tables(seq)
    saved = []
    h = None
    for l in range(DEPTH):
        sm = _small_views(p, l)
        lng, lnb = p["ln_g"][l], p["ln_b"][l]
        w = weights(l, "ffn1", x)
        x, h, s1 = _ffn_fwd(x, h, mod[l], w["ffn1_in"], w["ffn1_out"], lng[0:1], lnb[0:1], (0, 1, 2), "ffn1",
                            (mod[l], 3, 4))
        x, h, s2 = _mixer_fwd(x, h, mod[l], weights(l, "mix", x), sm, lng[1:2], lnb[1:2], l, tabs)
        w = weights(l, "ffn2", x)
        x, h, s3 = _ffn_fwd(x, h, mod[l], w["ffn2_in"], w["ffn2_out"], lng[2:3], lnb[2:3], (6, 7, 8), "ffn2",
                            (mod[l + 1], 0, 1) if l + 1 < DEPTH else None)
        saved.append((s1, s2, s3))
    dx, loss = _loss_head(x, target, "loss_head")
    big, small, dmods = [None] * DEPTH, [None] * DEPTH, [None] * DEPTH
    ties = []

    def tied(a):
        for t in ties:
            a = a + t
        return a

    for l in reversed(range(DEPTH)):
        w = {**weights(l, "ffn1", None), **weights(l, "mix", None), **weights(l, "ffn2", None)}
        sm = _small_views(p, l)
        lng, lnb = p["ln_g"][l], p["ln_b"][l]
        s1, s2, s3 = saved[l]

        def ready(name, grad, l=l):
            tie = None if grads_ready is None else grads_ready(l, name, grad)
            if tie is not None:
                ties.append(tie)
            return tie

        dx, dm3, dwi2, dwo2, dlg2, dlb2 = _ffn_bwd(dx, s3, tied(mod[l]), w["ffn2_in"], w["ffn2_out"], lng[2:3],
                                                   lnb[2:3], (6, 7, 8), "ffn2", ready)
        dx, dm2, dwmi, dwmo, g, dlg1, dlb1 = _mixer_bwd(dx, s2, tied(mod[l]), w, sm, lng[1:2], lnb[1:2], l, tabs,
                                                        ready)
        dx, dm1, dwi1, dwo1, dlg0, dlb0 = _ffn_bwd(dx, s1, tied(mod[l]), w["ffn1_in"], w["ffn1_out"], lng[0:1],
                                                   lnb[0:1], (0, 1, 2), "ffn1", ready)
        dmods[l] = jnp.concatenate(list(dm1) + list(dm2) + list(dm3), axis=1)
        big[l] = {"ffn1_in": dwi1, "ffn1_out": dwo1, "ffn2_in": dwi2, "ffn2_out": dwo2, "mix_in": dwmi,
                  "mix_out": dwmo}
        g["ln_g"] = jnp.concatenate([dlg0, dlg1, dlg2], axis=0)
        g["ln_b"] = jnp.concatenate([dlb0, dlb1, dlb2], axis=0)
        small[l] = g
    return loss, dx, jnp.stack(dmods), big, small


_BIG = ("ffn1_in", "ffn1_out", "ffn2_in", "ffn2_out", "mix_in", "mix_out")


def _small_grad_list(small, loss):
    def both(fn):
        return jnp.stack([fn(small[l]) for l in range(DEPTH)])

    uq_src, ukv_src = _uq_src(), _ukv_src()
    return [
        ("loss", loss.reshape(1)),
        ("ln_g", both(lambda g: g["ln_g"])), ("ln_b", both(lambda g: g["ln_b"])),
        ("hgrn_lb_logits", small[0]["lb_logits8"][:DEPTH] + small[1]["lb_logits8"][:DEPTH]),
        ("hgrn_norm_g", both(lambda g: g["hgrn_norm_g"][0])),
        ("mla_q_norm_g", both(lambda g: g["q_norm_g"][0])),
        ("mla_kv_norm_g", both(lambda g: g["kv_norm_g"][0])),
        ("mla_w_uq", both(lambda g: _unpack_cols(g["uq"], uq_src, 384))),
        ("mla_w_ukv", both(lambda g: _unpack_cols(g["ukv"], ukv_src, 512))),
        ("fox_b_f", both(lambda g: g["fox_b_f"][0, :N_HEADS])),
        ("gmlp_ln_g", both(lambda g: g["gmlp_ln_g"][0])), ("gmlp_ln_b", both(lambda g: g["gmlp_ln_b"][0])),
        ("gmlp_w_s", both(lambda g: g["gmlp_w_s"])),
        ("gmlp_b_s", both(lambda g: g["gmlp_bst"][:, :N_HEADS].T)),
    ]


_PACK_COLS = 512


def _pack_small(items):
    flat = jnp.concatenate([a.reshape(-1).astype(F32) for _, a in items])
    n = flat.shape[0]
    tile = 8 * _PACK_COLS
    flat = jnp.pad(flat, (0, (-n) % tile))
    return flat.reshape(-1, _PACK_COLS)


def _unpack_small(buf, items):
    flat = buf.reshape(-1)
    out, off = {}, 0
    for name, a in items:
        out[name] = flat[off:off + a.size].reshape(a.shape)
        off += a.size
    return out


def _as2d(a):
    return a.reshape(-1, a.shape[-1])


def kernel(x, c, ada_w, ada_b, ln_g, ln_b, ffn1_w_in, ffn1_w_out, ffn2_w_in, ffn2_w_out, mix_w_in, mix_w_out, hgrn_lb_logits, hgrn_norm_g, mla_q_norm_g, mla_kv_norm_g, mla_w_uq, mla_w_ukv, fox_b_f, gmlp_ln_g, gmlp_ln_b, gmlp_w_s, gmlp_b_s, loss_target, m_ada_w, m_ada_b, m_ln_g, m_ln_b, m_ffn1_w_in, m_ffn1_w_out, m_ffn2_w_in, m_ffn2_w_out, m_mix_w_in, m_mix_w_out, m_hgrn_lb_logits, m_hgrn_norm_g, m_mla_q_norm_g, m_mla_kv_norm_g, m_mla_w_uq, m_mla_w_ukv, m_fox_b_f, m_gmlp_ln_g, m_gmlp_ln_b, m_gmlp_w_s, m_gmlp_b_s, v_ada_w, v_ada_b, v_ln_g, v_ln_b, v_ffn1_w_in, v_ffn1_w_out, v_ffn2_w_in, v_ffn2_w_out, v_mix_w_in, v_mix_w_out, v_hgrn_lb_logits, v_hgrn_norm_g, v_mla_q_norm_g, v_mla_kv_norm_g, v_mla_w_uq, v_mla_w_ukv, v_fox_b_f, v_gmlp_ln_g, v_gmlp_ln_b, v_gmlp_w_s, v_gmlp_b_s):
    names = ["ada_w", "ada_b", "ln_g", "ln_b", "ffn1_w_in", "ffn1_w_out", "ffn2_w_in", "ffn2_w_out", "mix_w_in",
             "mix_w_out", "hgrn_lb_logits", "hgrn_norm_g", "mla_q_norm_g", "mla_kv_norm_g", "mla_w_uq", "mla_w_ukv",
             "fox_b_f", "gmlp_ln_g", "gmlp_ln_b", "gmlp_w_s", "gmlp_b_s"]
    w = dict(zip(names, [ada_w, ada_b, ln_g, ln_b, ffn1_w_in, ffn1_w_out, ffn2_w_in, ffn2_w_out, mix_w_in, mix_w_out,
                         hgrn_lb_logits, hgrn_norm_g, mla_q_norm_g, mla_kv_norm_g, mla_w_uq, mla_w_ukv, fox_b_f,
                         gmlp_ln_g, gmlp_ln_b, gmlp_w_s, gmlp_b_s]))
    m = dict(zip(names, [m_ada_w, m_ada_b, m_ln_g, m_ln_b, m_ffn1_w_in, m_ffn1_w_out, m_ffn2_w_in, m_ffn2_w_out,
                         m_mix_w_in, m_mix_w_out, m_hgrn_lb_logits, m_hgrn_norm_g, m_mla_q_norm_g, m_mla_kv_norm_g,
                         m_mla_w_uq, m_mla_w_ukv, m_fox_b_f, m_gmlp_ln_g, m_gmlp_ln_b, m_gmlp_w_s, m_gmlp_b_s]))
    v = dict(zip(names, [v_ada_w, v_ada_b, v_ln_g, v_ln_b, v_ffn1_w_in, v_ffn1_w_out, v_ffn2_w_in, v_ffn2_w_out,
                         v_mix_w_in, v_mix_w_out, v_hgrn_lb_logits, v_hgrn_norm_g, v_mla_q_norm_g, v_mla_kv_norm_g,
                         v_mla_w_uq, v_mla_w_ukv, v_fox_b_f, v_gmlp_ln_g, v_gmlp_ln_b, v_gmlp_w_s, v_gmlp_b_s]))
    bsz, seq, d = x.shape
    me = 4 * lax.axis_index("x") + 2 * lax.axis_index("y") + lax.axis_index("c")
    mix_src, uq_src, ukv_src, mo_src = _mix_in_src(), _uq_src(), _ukv_src(), _mo_src()

    part_names = {"ffn1": ["ffn1_w_in", "ffn1_w_out"], "mix": ["mix_w_in", "mix_w_out", "mla_w_uq", "mla_w_ukv"],
                  "ffn2": ["ffn2_w_in", "ffn2_w_out"]}
    group_of = {}
    for l in range(DEPTH):
        for part in ("ffn1", "mix", "ffn2"):
            group_of[(l, part)] = (0, part) if l == 0 else (l, "all")
    in_flight = {}

    def start_group(key, behind=None):
        members = [(l, part) for (l, part), g in group_of.items() if g == key]
        labels = [(l, n) for l, part in members for n in part_names[part]]
        shards = []
        for l, n in labels:
            a = w[n][l]
            if n == "mix_w_in":
                a = _pack_cols(a, mix_src)
            shards.append(a.astype(BF16))
        if behind is not None:
            shards, _ = lax.optimization_barrier((shards, behind))
        in_flight[key] = (labels, _push_start(shards, f"gather_start_{key[0]}_{key[1]}", whole=True))

    keys_in_order = list(dict.fromkeys(group_of.values()))
    start_group(keys_in_order[0])

    gathered = _all_gather([c, ln_g, ln_b], "gather_inputs")
    c_all = gathered[0].reshape(N_DEV * bsz, d)
    ln_g_full = jnp.moveaxis(gathered[1], 0, 2).reshape(DEPTH, 3, d)
    ln_b_full = jnp.moveaxis(gathered[2], 0, 2).reshape(DEPTH, 3, d)

    mod_cols = _ada_fwd(c_all, ada_w, "ada_fwd")
    mod_all, = _all_gather([mod_cols], "gather_mod")
    mod_mine = lax.dynamic_slice_in_dim(mod_all, me * bsz, bsz, axis=2)
    mod = jnp.moveaxis(mod_mine, 0, 2).reshape(DEPTH, bsz, N_MOD * d) + ada_b[:, None, :]
    for key in keys_in_order[1:]:
        start_group(key, behind=mod)
    tie = sum(h[-1][0, 0] for _, h in in_flight.values())
    mod = mod.reshape(DEPTH, bsz, N_MOD, d) + tie

    arrived, laid_out = {}, {}

    def weights(l, part, after):
        if (l, part) not in laid_out:
            laid_out[(l, part)] = lay_out(l, part, after)
        return laid_out[(l, part)]

    def lay_out(l, part, after):
        key = group_of[(l, part)]
        if key not in arrived:
            labels, (send_sems, recv_sems, srcs, lands, _) = in_flight[key]
            _, lands = _push_wait(send_sems, recv_sems, srcs, lands, after, f"gather_wait_{key[0]}_{key[1]}",
                                  whole=True)
            arrived[key] = dict(zip(labels, lands))
        gw = {n: arrived[key][(l, n)] for n in part_names[part]}
        if part != "mix":
            return {f"{part}_in": gw[f"{part}_w_in"], f"{part}_out": gw[f"{part}_w_out"].reshape(4, 704, d)}
        uq = jnp.moveaxis(gw["mla_w_uq"], 0, 1).reshape(256, 384)
        ukv = jnp.moveaxis(gw["mla_w_ukv"], 0, 1).reshape(128, 512)
        return {"mix_in": gw["mix_w_in"].reshape(d, PACK_W),
                "mix_out": _pack_cols(gw["mix_w_out"].reshape(d, d).T, mo_src).T,
                "uq": _pack_cols(uq, uq_src), "ukv": _pack_cols(ukv, ukv_src)}

    p = dict(w)
    p["ln_g"], p["ln_b"] = ln_g_full, ln_b_full
    def chunks(name, arr):
        if name in ("ffn1_in", "ffn2_in"):
            return arr
        if name in ("ffn1_out", "ffn2_out"):
            return arr.reshape(N_DEV, arr.shape[1] // 2, d)
        if name == "mix_in":
            return arr.reshape(N_DEV, d // N_DEV, PACK_W)
        return _unpack_cols(arr.T, mo_src, d).T.astype(BF16).reshape(N_DEV, d // N_DEV, d)

    pending, started = {}, []

    def grads_ready(l, name, grad):
        pending[(name, l)] = chunks(name, grad)
        flush = name == "ffn1_in" if l > 0 else name in ("mix_in", "ffn1_out", "ffn1_in")
        if not flush:
            return None
        keys = sorted(pending)
        handles = _push_start([pending[k] for k in keys], f"push_start_{len(started)}")
        pending.clear()
        started.append((keys, handles, l == 0 and name.startswith("ffn1")))
        return handles[-1][0, 0]

    loss, grad_x, dmod, big, small = _local_step(x, mod, loss_target, weights, p, grads_ready)
    del big

    recv, out = {}, {}

    def arrive(n, after):
        keys, (send_sems, recv_sems, srcs, lands, _), _ = started[n]
        srcs, lands = _push_wait(send_sems, recv_sems, srcs, lands, after, f"push_wait_{n}")
        for k, src, land in zip(keys, srcs, lands):
            recv[k] = (land, lax.dynamic_index_in_dim(src, me, 0, keepdims=False))

    big_of = {"ffn1_w_in": "ffn1_in", "ffn1_w_out": "ffn1_out", "ffn2_w_in": "ffn2_in", "ffn2_w_out": "ffn2_out",
              "mix_w_in": "mix_in", "mix_w_out": "mix_out"}
    chain = {name: None for name in big_of}

    def big_update(key, l):
        name = next(nm for nm, k in big_of.items() if k == key)
        parts, own = recv[(key, l)]
        if key == "mix_in":
            parts = _unpack_cols(parts, mix_src, MIX_ORIG_W)
            own = _unpack_cols(own, mix_src, MIX_ORIG_W)
        chain[name] = _adamw(parts, own, w[name], m[name], v[name], f"adamw_{name}_l{l}", layer=l,
                             prev=chain[name])

    def update(name, grad):
        shape = w[name].shape
        as3 = lambda a: a.reshape(1, -1, shape[-1])
        res = _adamw(as3(grad), None, as3(w[name]), as3(m[name]), as3(v[name]), f"adamw_{name}")
        out[name] = tuple(r.reshape(shape) for r in res)

    for n, (keys, _, last) in enumerate(started):
        if not last:
            arrive(n, grad_x)
            for key, l in keys:
                big_update(key, l)

    dmod_all, = _all_gather([dmod.reshape(DEPTH, bsz, N_MOD * d)], "gather_dmod")
    dmod_full = jnp.moveaxis(dmod_all, 0, 1).reshape(DEPTH, N_DEV * bsz, N_MOD * d)
    cols = ada_w.shape[2]
    dmod_cols = lax.dynamic_slice_in_dim(dmod_full, me * cols, cols, axis=2)
    g_ada_w, g_ada_b = _ada_bwd(c_all, dmod_cols, dmod_full, "ada_bwd")
    res = None
    for l in range(DEPTH):
        res = _adamw(g_ada_w[l][None], None, ada_w, m_ada_w, v_ada_w, f"adamw_ada_w_l{l}", layer=l, prev=res)
    out["ada_w"] = tuple(res)
    update("ada_b", g_ada_b.reshape(DEPTH, N_MOD * d))

    items = _small_grad_list(small, loss)
    parts, = _all_gather([_pack_small(items)], "gather_small")
    sg = _unpack_small(_sum_parts(parts, "sum_small"), items)
    for name in ("ln_g", "ln_b"):
        update(name, lax.dynamic_slice_in_dim(sg[name], me * (d // N_DEV), d // N_DEV, axis=2))
    for name, width in (("mla_w_uq", 48), ("mla_w_ukv", 64)):
        update(name, lax.dynamic_slice_in_dim(sg[name], me * width, width, axis=2))
    for name in ("hgrn_lb_logits", "hgrn_norm_g", "mla_q_norm_g", "mla_kv_norm_g", "fox_b_f", "gmlp_ln_g",
                 "gmlp_ln_b", "gmlp_w_s", "gmlp_b_s"):
        update(name, sg[name])

    for n, (keys, _, last) in enumerate(started):
        if last:
            arrive(n, out["gmlp_w_s"][0])
            for key, l in keys:
                big_update(key, l)
    for name in big_of:
        out[name] = tuple(chain[name])

    return (sg["loss"][0], grad_x, *[out[n][0] for n in names], *[out[n][1] for n in names],
            *[out[n][2] for n in names], *[out[n][3] for n in names])
```

```python
import functools

import numpy as np
import jax
import jax.numpy as jnp
from jax import lax
from jax.experimental import pallas as pl
from jax.experimental.pallas import tpu as pltpu

F32 = jnp.float32
BF16 = jnp.bfloat16
HI = lax.Precision.HIGHEST

D_MODEL = 1024
DEPTH = 2
GROUP_WIDTH = 256
N_HEADS = 4
HEAD_DIM = 64
A_CHUNK = 16
LB_FLOOR = 1e-30
B_NOPE = 64
B_ROPE = 32
ROPE_THETA = 10000.0
D_CHUNK = 128
D_FF = 2816
N_MOD = 9
ALPHA = (2 * DEPTH) ** 0.25
LN_EPS = 1e-5
RMS_EPS = 1e-6
ADAM_LR = 0.001
ADAM_B1 = 0.9
ADAM_B2 = 0.999
ADAM_EPS = 1e-08
ADAM_WD = 0.01
ADAM_STEP = 10

N_DEV = 8
LANES = 128
PACK_W = 3712
MO_W = 1536
VMEM_LIMIT = 56 * 1024 * 1024
NEG = -1e30
ATTN_TILE = 512

MIX_ORIG_W = 2724
O_BCQ, O_BCKV, O_BKR, O_CQ, O_CK, O_CV, O_CF, O_DU, O_DV = 1024, 1280, 1408, 1440, 1696, 1952, 2208, 2212, 2468
P_B, P_KR, P_CQ, P_CKV, P_D, P_CF = 1024, 1408, 1536, 2048, 3072, 3584


_DN = {"nn": (((1,), (0,)), ((), ())), "nt": (((1,), (1,)), ((), ())), "tn": (((0,), (0,)), ((), ()))}


def _raw_bdot(a, b, mode):
    return lax.dot_general(a.astype(BF16), b.astype(BF16), _DN[mode], preferred_element_type=F32)


@functools.partial(jax.custom_vjp, nondiff_argnums=(2,))
def _bdot(a, b, mode):
    return _raw_bdot(a, b, mode)


def _bdot_fwd(a, b, mode):
    return _raw_bdot(a, b, mode), (a, b)


def _bdot_bwd(mode, res, g):
    a, b = res
    if mode == "nn":
        return _raw_bdot(g, b, "nt"), _raw_bdot(a, g, "tn")
    if mode == "nt":
        return _raw_bdot(g, b, "nn"), _raw_bdot(g, a, "tn")
    return _raw_bdot(b, g, "nt"), _raw_bdot(a, g, "nn")


_bdot.defvjp(_bdot_fwd, _bdot_bwd)


def _cparams(sem):
    return pltpu.CompilerParams(dimension_semantics=sem, vmem_limit_bytes=VMEM_LIMIT)


def _mix_in_src():
    src = -np.ones(PACK_W, np.int64)
    src[0:P_KR] = np.arange(0, O_BKR)
    src[P_KR + 64:P_KR + 80] = O_BKR + np.arange(16)
    src[P_KR + 96:P_KR + 112] = O_BKR + 16 + np.arange(16)
    for h in range(N_HEADS):
        src[P_CQ + 128 * h:P_CQ + 128 * h + 64] = O_CQ + 64 * h + np.arange(64)
        src[P_CKV + 256 * h:P_CKV + 256 * h + 64] = O_CK + 64 * h + np.arange(64)
        src[P_CKV + 256 * h + 128:P_CKV + 256 * h + 192] = O_CV + 64 * h + np.arange(64)
    src[P_D:P_D + 512] = O_DU + np.arange(512)
    src[P_CF:P_CF + 4] = O_CF + np.arange(4)
    return src


def _uq_src():
    src = -np.ones(512, np.int64)
    for h in range(N_HEADS):
        src[128 * h:128 * h + 64] = 96 * h + np.arange(64)
        src[128 * h + 64:128 * h + 80] = 96 * h + 64 + np.arange(16)
        src[128 * h + 96:128 * h + 112] = 96 * h + 80 + np.arange(16)
    return src


def _ukv_src():
    src = -np.ones(1024, np.int64)
    for h in range(N_HEADS):
        src[256 * h:256 * h + 64] = 128 * h + np.arange(64)
        src[256 * h + 128:256 * h + 192] = 128 * h + 64 + np.arange(64)
    return src


def _mo_src():
    src = -np.ones(MO_W, np.int64)
    src[0:256] = np.arange(256)
    for g in range(2):
        for h in range(N_HEADS):
            src[256 + 512 * g + 128 * h:256 + 512 * g + 128 * h + 64] = 256 + 256 * g + 64 * h + np.arange(64)
    src[1280:1536] = 768 + np.arange(256)
    return src


def _runs(idx):
    runs, i = [], 0
    while i < len(idx):
        j = i + 1
        while j < len(idx) and ((idx[i] < 0 and idx[j] < 0) or (idx[i] >= 0 and idx[j] == idx[i] + j - i)):
            j += 1
        runs.append((int(idx[i]), j - i))
        i = j
    return runs


def _take_runs(w, idx):
    parts = [jnp.zeros(w.shape[:-1] + (n,), w.dtype) if s < 0 else lax.slice_in_dim(w, s, s + n, axis=w.ndim - 1)
             for s, n in _runs(idx)]
    return jnp.concatenate(parts, axis=-1)


def _pack_cols(w, src):
    return _take_runs(w, src)


def _unpack_cols(wp, src, n):
    dst = np.zeros(n, np.int64)
    dst[src[src >= 0]] = np.nonzero(src >= 0)[0]
    return _take_runs(wp, dst)


def _rope_tables(seq):
    half = B_ROPE // 2
    inv_freq = ROPE_THETA ** (-jnp.arange(half, dtype=F32) / half)
    ang = jnp.arange(seq).astype(F32)[:, None] * inv_freq[None, :]
    cos, sin = jnp.cos(ang), jnp.sin(ang)
    z16 = jnp.zeros((seq, 16), F32)
    c = jnp.concatenate([jnp.ones((seq, 64), F32), cos, z16, cos, z16], axis=1)
    s1 = jnp.concatenate([jnp.zeros((seq, 64), F32), -sin, z16, z16, z16], axis=1)
    s2 = jnp.concatenate([jnp.zeros((seq, 64), F32), z16, z16, sin, z16], axis=1)
    return c, s1, s2


def _matmul(a, b, *, mode, group_out, out_dtype, tm, tk, name):
    ga, gb = a.shape[0], b.shape[0]
    g_n = max(ga, gb)
    if mode == "tn":
        k_dim, m_dim = a.shape[1:]
    else:
        m_dim, k_dim = a.shape[1:]
    n_dim = b.shape[1] if mode == "nt" else b.shape[2]
    assert m_dim % tm == 0 and k_dim % tk == 0
    kt = k_dim // tk
    n_red = kt if group_out else g_n * kt
    g_out = g_n if group_out else 1

    def split(g, r):
        return (g, r) if group_out else (r // kt, r % kt)

    def a_map(g, i, r):
        gg, kk = split(g, r)
        gg = gg if ga > 1 else 0
        return (gg, kk, i) if mode == "tn" else (gg, i, kk)

    def b_map(g, i, r):
        gg, kk = split(g, r)
        gg = gg if gb > 1 else 0
        return (gg, 0, kk) if mode == "nt" else (gg, kk, 0)

    a_blk = (None, tk, tm) if mode == "tn" else (None, tm, tk)
    b_blk = (None, n_dim, tk) if mode == "nt" else (None, tk, n_dim)
    dn = _DN[mode]

    def body(a_ref, b_ref, o_ref, *scratch):
        part = lax.dot_general(a_ref[...].astype(BF16), b_ref[...].astype(BF16), dn, preferred_element_type=F32)
        if n_red == 1:
            o_ref[...] = part.astype(o_ref.dtype)
            return
        acc_ref, = scratch
        r = pl.program_id(2)

        @pl.when(r == 0)
        def _():
            acc_ref[...] = part

        @pl.when(r > 0)
        def _():
            acc_ref[...] += part

        @pl.when(r == n_red - 1)
        def _():
            o_ref[...] = acc_ref[...].astype(o_ref.dtype)

    return pl.pallas_call(
        body, name=name, grid=(g_out, m_dim // tm, n_red),
        in_specs=[pl.BlockSpec(a_blk, a_map), pl.BlockSpec(b_blk, b_map)],
        out_specs=pl.BlockSpec((None, tm, n_dim), lambda g, i, r: (g, i, 0)),
        out_shape=jax.ShapeDtypeStruct((g_out, m_dim, n_dim), out_dtype),
        scratch_shapes=[] if n_red == 1 else [pltpu.VMEM((tm, n_dim), F32)],
        compiler_params=_cparams(("parallel", "parallel", "arbitrary")),
    )(a, b)


def _row_spec(ts, d):
    return pl.BlockSpec((None, ts, d), lambda b, s: (b, s, 0))


def _mod_spec(d):
    return pl.BlockSpec((None, N_MOD, d), lambda b, s: (b, 0, 0))


def _vec_spec(d):
    return pl.BlockSpec((1, d), lambda b, s: (0, 0))


def _bvec_spec(d):
    return pl.BlockSpec((None, 1, d), lambda b, s: (b, 0, 0))


def _modulate(x, mod, sh_row, sc_row, name, ts=512):
    bsz, seq, d = x.shape

    def body(x_ref, mod_ref, o_ref):
        sh = mod_ref[sh_row:sh_row + 1, :]
        sc = mod_ref[sc_row:sc_row + 1, :]
        o_ref[...] = (x_ref[...] * (1.0 + sc) + sh).astype(o_ref.dtype)

    return pl.pallas_call(
        body, name=name, grid=(bsz, seq // ts),
        in_specs=[_row_spec(ts, d), _mod_spec(d)], out_specs=_row_spec(ts, d),
        out_shape=jax.ShapeDtypeStruct((bsz, seq, d), BF16),
        compiler_params=_cparams(("parallel", "parallel")),
    )(x, mod)


def _modulate_bwd(dh, x, mod, dx_res, sc_row, name, ts=512):
    bsz, seq, d = x.shape

    def body(dh_ref, x_ref, mod_ref, dxr_ref, dx_ref, dsh_ref, dsc_ref):
        s = pl.program_id(1)
        sc = mod_ref[sc_row:sc_row + 1, :]
        dh_v = dh_ref[...]
        dx_ref[...] = dxr_ref[...] + dh_v * (1.0 + sc)
        psh = jnp.sum(dh_v, axis=0, keepdims=True)
        psc = jnp.sum(dh_v * x_ref[...], axis=0, keepdims=True)

        @pl.when(s == 0)
        def _():
            dsh_ref[...] = psh
            dsc_ref[...] = psc

        @pl.when(s > 0)
        def _():
            dsh_ref[...] += psh
            dsc_ref[...] += psc

    return pl.pallas_call(
        body, name=name, grid=(bsz, seq // ts),
        in_specs=[_row_spec(ts, d), _row_spec(ts, d), _mod_spec(d), _row_spec(ts, d)],
        out_specs=[_row_spec(ts, d), _bvec_spec(d), _bvec_spec(d)],
        out_shape=[jax.ShapeDtypeStruct((bsz, seq, d), F32), jax.ShapeDtypeStruct((bsz, 1, d), F32),
                   jax.ShapeDtypeStruct((bsz, 1, d), F32)],
        compiler_params=_cparams(("parallel", "arbitrary")),
    )(dh, x, mod, dx_res)


def _res_ln_fn(x, f, g, lng, lnb, cmul):
    r = ALPHA * x + (cmul * (1.0 + g)) * f
    mu = jnp.mean(r, axis=-1, keepdims=True)
    rc = r - mu
    var = jnp.mean(rc * rc, axis=-1, keepdims=True)
    return rc * lax.rsqrt(var + LN_EPS) * lng + lnb


def _res_ln(x, f, mod, lng, lnb, g_row, cmul, name, nxt=None, ts=512):
    bsz, seq, d = x.shape

    def body(*refs):
        x_ref, f_ref, mod_ref, lng_ref, lnb_ref = refs[:5]
        g = mod_ref[g_row:g_row + 1, :]
        y = _res_ln_fn(x_ref[...], f_ref[...], g, lng_ref[...], lnb_ref[...], cmul)
        if nxt is None:
            refs[5][...] = y
            return
        nmod_ref, o_ref, h_ref = refs[5:]
        o_ref[...] = y
        sh = nmod_ref[nxt[1]:nxt[1] + 1, :]
        sc = nmod_ref[nxt[2]:nxt[2] + 1, :]
        h_ref[...] = (y * (1.0 + sc) + sh).astype(h_ref.dtype)

    in_specs = [_row_spec(ts, d), _row_spec(ts, d), _mod_spec(d), _vec_spec(d), _vec_spec(d)]
    args = [x, f, mod, lng, lnb]
    out_specs, out_shape = [_row_spec(ts, d)], [jax.ShapeDtypeStruct((bsz, seq, d), F32)]
    if nxt is not None:
        in_specs.append(_mod_spec(d))
        args.append(nxt[0])
        out_specs.append(_row_spec(ts, d))
        out_shape.append(jax.ShapeDtypeStruct((bsz, seq, d), BF16))
    res = pl.pallas_call(
        body, name=name, grid=(bsz, seq // ts), in_specs=in_specs, out_specs=out_specs, out_shape=out_shape,
        compiler_params=_cparams(("parallel", "parallel")),
    )(*args)
    return (res[0], res[1]) if nxt is not None else (res[0], None)


def _res_ln_bwd(dy, x, f, mod, lng, lnb, g_row, cmul, name, ts=256):
    bsz, seq, d = x.shape

    def body(dy_ref, x_ref, f_ref, mod_ref, lng_ref, lnb_ref, dx_ref, df_ref, dg_ref, dlg_ref, dlb_ref):
        b, s = pl.program_id(0), pl.program_id(1)
        g = mod_ref[g_row:g_row + 1, :]
        _, vjp = jax.vjp(functools.partial(_res_ln_fn, cmul=cmul), x_ref[...], f_ref[...], g, lng_ref[...],
                         lnb_ref[...])
        dx, df, dg, dlg, dlb = vjp(dy_ref[...])
        dx_ref[...] = dx
        df_ref[...] = df.astype(df_ref.dtype)

        @pl.when(s == 0)
        def _():
            dg_ref[...] = dg

        @pl.when(s > 0)
        def _():
            dg_ref[...] += dg

        first = jnp.logical_and(b == 0, s == 0)

        @pl.when(first)
        def _():
            dlg_ref[...] = dlg
            dlb_ref[...] = dlb

        @pl.when(jnp.logical_not(first))
        def _():
            dlg_ref[...] += dlg
            dlb_ref[...] += dlb

    return pl.pallas_call(
        body, name=name, grid=(bsz, seq // ts),
        in_specs=[_row_spec(ts, d), _row_spec(ts, d), _row_spec(ts, d), _mod_spec(d), _vec_spec(d), _vec_spec(d)],
        out_specs=[_row_spec(ts, d), _row_spec(ts, d), _bvec_spec(d), _vec_spec(d), _vec_spec(d)],
        out_shape=[jax.ShapeDtypeStruct((bsz, seq, d), F32), jax.ShapeDtypeStruct((bsz, seq, d), BF16),
                   jax.ShapeDtypeStruct((bsz, 1, d), F32), jax.ShapeDtypeStruct((1, d), F32),
                   jax.ShapeDtypeStruct((1, d), F32)],
        compiler_params=_cparams(("arbitrary", "arbitrary")),
    )(dy, x, f, mod, lng, lnb)


def _loss_head(y, target, name, ts=512):
    bsz, seq, d = y.shape
    n_s = seq // ts

    def body(y_ref, t_ref, dy_ref, loss_ref, acc_ref):
        b, s = pl.program_id(0), pl.program_id(1)
        err = y_ref[...] - t_ref[...]
        dy_ref[...] = err * (1.0 / d)
        part = jnp.sum(err * err, axis=0, keepdims=True)
        first = jnp.logical_and(b == 0, s == 0)

        @pl.when(first)
        def _():
            acc_ref[...] = part

        @pl.when(jnp.logical_not(first))
        def _():
            acc_ref[...] += part

        @pl.when(jnp.logical_and(b == bsz - 1, s == n_s - 1))
        def _():
            loss_ref[...] = jnp.sum(acc_ref[...], axis=1, keepdims=True) * (0.5 / d)

    return pl.pallas_call(
        body, name=name, grid=(bsz, n_s),
        in_specs=[_row_spec(ts, d), _row_spec(ts, d)],
        out_specs=[_row_spec(ts, d), pl.BlockSpec((1, 1), lambda b, s: (0, 0))],
        out_shape=[jax.ShapeDtypeStruct((bsz, seq, d), F32), jax.ShapeDtypeStruct((1, 1), F32)],
        scratch_shapes=[pltpu.VMEM((1, d), F32)],
        compiler_params=_cparams(("arbitrary", "arbitrary")),
    )(y, target)


def _ffn_in_swiglu(h, w_in, name, tm=1024):
    t, d = h.shape
    n_sh, _, w = w_in.shape
    half = n_sh // 2

    def body(h_ref, w_ref, z_ref, a_ref):
        hv = h_ref[...]
        g = jnp.dot(hv, w_ref[0], preferred_element_type=F32)
        u = jnp.dot(hv, w_ref[1], preferred_element_type=F32)
        z_ref[0] = g.astype(z_ref.dtype)
        z_ref[1] = u.astype(z_ref.dtype)
        a_ref[...] = (g * jax.nn.sigmoid(g) * u).astype(a_ref.dtype)

    return pl.pallas_call(
        body, name=name, grid=(half, t // tm),
        in_specs=[pl.BlockSpec((tm, d), lambda g, i: (i, 0)),
                  pl.BlockSpec((2, None, d, w), lambda g, i: (0, g, 0, 0))],
        out_specs=[pl.BlockSpec((2, None, tm, w), lambda g, i: (0, g, i, 0)),
                   pl.BlockSpec((None, tm, w), lambda g, i: (g, i, 0))],
        out_shape=[jax.ShapeDtypeStruct((2, half, t, w), BF16), jax.ShapeDtypeStruct((half, t, w), BF16)],
        compiler_params=_cparams(("parallel", "parallel")),
    )(h, w_in.reshape(2, half, d, w))


def _ffn_out_dx_swiglu(df, w_out, z, name, tm=1024):
    t, d = df.shape
    half, w, _ = w_out.shape

    def body(df_ref, w_ref, z_ref, dz_ref):
        da = lax.dot_general(df_ref[...], w_ref[...], _DN["nt"], preferred_element_type=F32)
        g = z_ref[0].astype(F32)
        u = z_ref[1].astype(F32)
        sig = jax.nn.sigmoid(g)
        dz_ref[0] = (da * u * (sig * (1.0 + g * (1.0 - sig)))).astype(dz_ref.dtype)
        dz_ref[1] = (da * (g * sig)).astype(dz_ref.dtype)

    zspec = pl.BlockSpec((2, None, tm, w), lambda g, i: (0, g, i, 0))
    return pl.pallas_call(
        body, name=name, grid=(half, t // tm),
        in_specs=[pl.BlockSpec((tm, d), lambda g, i: (i, 0)), pl.BlockSpec((None, w, d), lambda g, i: (g, 0, 0)),
                  zspec],
        out_specs=zspec, out_shape=jax.ShapeDtypeStruct(z.shape, BF16),
        compiler_params=_cparams(("parallel", "parallel")),
    )(df, w_out, z)


def _log_sigmoid(x):
    return jnp.minimum(x, 0.0) - jnp.log(1.0 + jnp.exp(-jnp.abs(x)))


def _hgrn_consts():
    r = lax.broadcasted_iota(jnp.int32, (GROUP_WIDTH, GROUP_WIDTH), 0)
    c = lax.broadcasted_iota(jnp.int32, (GROUP_WIDTH, GROUP_WIDTH), 1)
    bd = (r // HEAD_DIM == c // HEAD_DIM).astype(F32)
    r16 = lax.broadcasted_iota(jnp.int32, (A_CHUNK, A_CHUNK), 0)
    c16 = lax.broadcasted_iota(jnp.int32, (A_CHUNK, A_CHUNK), 1)
    tril = (r16 >= c16).astype(F32)
    rows = lax.broadcasted_iota(jnp.int32, (A_CHUNK, GROUP_WIDTH), 0)
    return bd, tril, rows


def _hgrn_lb(logits8, layer):
    rows = lax.broadcasted_iota(jnp.int32, logits8.shape, 0)
    valid = rows < DEPTH
    mx = jnp.max(jnp.where(valid, logits8, NEG), axis=0, keepdims=True)
    e = jnp.where(valid, jnp.exp(logits8 - mx), 0.0)
    sm = e / jnp.sum(e, axis=0, keepdims=True)
    pick = jnp.logical_and(rows >= 1, rows <= layer)
    return jnp.sum(jnp.where(pick, sm, 0.0), axis=0, keepdims=True)


def _hgrn_chunk(aq, af, ai, ag, logits8, norm_g, st, *, layer, consts):
    bd, tril, rows = consts
    lb = _hgrn_lb(logits8, layer)
    la = jnp.log(jnp.maximum(lb, LB_FLOOR))
    b2 = jnp.log(1.0 - lb) + _log_sigmoid(af)
    log_f = jnp.maximum(la, b2) + jnp.log(1.0 + jnp.exp(-jnp.abs(la - b2)))
    k = 1.0 - jnp.exp(log_f)
    qf = aq * jax.nn.sigmoid(aq)
    g_cum = jnp.dot(tril, log_f, precision=HI, preferred_element_type=F32)

    c, w = A_CHUNK, GROUP_WIDTH

    def by_key(v):
        return jnp.broadcast_to(v[:, None, :], (c, c, w))

    def by_query(v):
        return jnp.broadcast_to(v[None, :, :], (c, c, w))

    s_i = lax.broadcasted_iota(jnp.int32, (c, c, w), 0)
    t_i = lax.broadcasted_iota(jnp.int32, (c, c, w), 1)
    rel = jnp.where(t_i >= s_i, by_query(g_cum) - by_key(g_cum), NEG)
    pairs = by_query(qf) * by_key(k) * jnp.exp(rel)
    a_all = _bdot(pairs.reshape(c * c, w), bd, "nn").reshape(c, c, w)
    o = jnp.sum(a_all * by_key(ai), axis=0)
    q_dec = qf * jnp.exp(g_cum)
    o = o + _bdot(q_dec, st, "nt")
    g_last = jnp.sum(jnp.where(rows == c - 1, g_cum, 0.0), axis=0, keepdims=True)
    k_end = k * jnp.exp(g_last - g_cum)
    kv = _bdot(ai, k_end, "tn")
    st_new = st * jnp.exp(g_last) + kv * bd
    ms = _bdot(o * o, bd, "nn") * (1.0 / HEAD_DIM)
    o = o * lax.rsqrt(ms + RMS_EPS) * norm_g
    return o * (ag * jax.nn.sigmoid(ag)), st_new


def _hgrn_fwd(proj, logits8, norm_g, layer, name, ts=128):
    bsz, seq, _ = proj.shape
    n_ch = ts // A_CHUNK

    def body(p_ref, lg_ref, ng_ref, o_ref, st_ref, st_scr):
        @pl.when(pl.program_id(1) == 0)
        def _():
            st_scr[...] = jnp.zeros_like(st_scr)

        consts = _hgrn_consts()
        logits_v, ng_v = lg_ref[...], ng_ref[...]

        def chunk(ci, carry):
            r = pl.multiple_of(ci * A_CHUNK, A_CHUNK)
            st = st_scr[...]
            st_ref[ci] = st
            o, st_new = _hgrn_chunk(
                p_ref[pl.ds(r, A_CHUNK), 0:256], p_ref[pl.ds(r, A_CHUNK), 256:512],
                p_ref[pl.ds(r, A_CHUNK), 512:768], p_ref[pl.ds(r, A_CHUNK), 768:1024],
                logits_v, ng_v, st, layer=layer, consts=consts)
            o_ref[pl.ds(r, A_CHUNK), :] = o.astype(o_ref.dtype)
            st_scr[...] = st_new
            return carry

        lax.fori_loop(0, n_ch, chunk, 0, unroll=2)

    return pl.pallas_call(
        body, name=name, grid=(bsz, seq // ts),
        in_specs=[pl.BlockSpec((None, ts, 1024), lambda b, s: (b, s, 0)),
                  pl.BlockSpec((8, GROUP_WIDTH), lambda b, s: (0, 0)),
                  pl.BlockSpec((1, GROUP_WIDTH), lambda b, s: (0, 0))],
        out_specs=[pl.BlockSpec((None, ts, GROUP_WIDTH), lambda b, s: (b, s, 0)),
                   pl.BlockSpec((None, n_ch, GROUP_WIDTH, GROUP_WIDTH), lambda b, s: (b, s, 0, 0))],
        out_shape=[jax.ShapeDtypeStruct((bsz, seq, MO_W), BF16),
                   jax.ShapeDtypeStruct((bsz, seq // A_CHUNK, GROUP_WIDTH, GROUP_WIDTH), F32)],
        scratch_shapes=[pltpu.VMEM((GROUP_WIDTH, GROUP_WIDTH), F32)],
        compiler_params=_cparams(("parallel", "arbitrary")),
    )(proj, logits8, norm_g)


def _hgrn_bwd(dmo, proj, states, logits8, norm_g, layer, name, ts=128):
    bsz, seq, _ = proj.shape
    n_ch = ts // A_CHUNK
    n_s = seq // ts

    def body(do_ref, p_ref, st_ref, lg_ref, ng_ref, dp_ref, dlg_ref, dng_ref, dst_scr):
        b, s = pl.program_id(0), pl.program_id(1)

        @pl.when(s == 0)
        def _():
            dst_scr[...] = jnp.zeros_like(dst_scr)

        @pl.when(jnp.logical_and(b == 0, s == 0))
        def _():
            dlg_ref[...] = jnp.zeros_like(dlg_ref)
            dng_ref[...] = jnp.zeros_like(dng_ref)

        consts = _hgrn_consts()
        logits_v, ng_v = lg_ref[...], ng_ref[...]
        fn = functools.partial(_hgrn_chunk, layer=layer, consts=consts)

        def chunk(t, carry):
            ci = n_ch - 1 - t
            r = pl.multiple_of(ci * A_CHUNK, A_CHUNK)
            _, vjp = jax.vjp(
                fn, p_ref[pl.ds(r, A_CHUNK), 0:256], p_ref[pl.ds(r, A_CHUNK), 256:512],
                p_ref[pl.ds(r, A_CHUNK), 512:768], p_ref[pl.ds(r, A_CHUNK), 768:1024],
                logits_v, ng_v, st_ref[ci])
            daq, daf, dai, dag, dlg, dng, dst = vjp((do_ref[pl.ds(r, A_CHUNK), :], dst_scr[...]))
            dp_ref[pl.ds(r, A_CHUNK), 0:256] = daq.astype(dp_ref.dtype)
            dp_ref[pl.ds(r, A_CHUNK), 256:512] = daf.astype(dp_ref.dtype)
            dp_ref[pl.ds(r, A_CHUNK), 512:768] = dai.astype(dp_ref.dtype)
            dp_ref[pl.ds(r, A_CHUNK), 768:1024] = dag.astype(dp_ref.dtype)
            dlg_ref[...] += dlg
            dng_ref[...] += dng
            dst_scr[...] = dst
            return carry

        lax.fori_loop(0, n_ch, chunk, 0, unroll=2)

    rev = lambda b, s: (b, n_s - 1 - s, 0)
    return pl.pallas_call(
        body, name=name, grid=(bsz, n_s),
        in_specs=[pl.BlockSpec((None, ts, GROUP_WIDTH), rev),
                  pl.BlockSpec((None, ts, 1024), rev),
                  pl.BlockSpec((None, n_ch, GROUP_WIDTH, GROUP_WIDTH), lambda b, s: (b, n_s - 1 - s, 0, 0)),
                  pl.BlockSpec((8, GROUP_WIDTH), lambda b, s: (0, 0)),
                  pl.BlockSpec((1, GROUP_WIDTH), lambda b, s: (0, 0))],
        out_specs=[pl.BlockSpec((None, ts, 1024), rev),
                   pl.BlockSpec((8, GROUP_WIDTH), lambda b, s: (0, 0)),
                   pl.BlockSpec((1, GROUP_WIDTH), lambda b, s: (0, 0))],
        out_shape=[jax.ShapeDtypeStruct((bsz, seq, PACK_W), BF16),
                   jax.ShapeDtypeStruct((8, GROUP_WIDTH), F32), jax.ShapeDtypeStruct((1, GROUP_WIDTH), F32)],
        scratch_shapes=[pltpu.VMEM((GROUP_WIDTH, GROUP_WIDTH), F32)],
        compiler_params=_cparams(("arbitrary", "arbitrary")),
    )(dmo, proj, states, logits8, norm_g)


def _rms_fn(x, g):
    return x * lax.rsqrt(jnp.mean(x * x, axis=-1, keepdims=True) + RMS_EPS) * g


def _tile4(t):
    return jnp.concatenate([t, t, t, t], axis=1)


def _rope(x, c, s1, s2):
    w = x.shape[-1]
    return x * c + pltpu.roll(x, 32, axis=1) * s2 + pltpu.roll(x, w - 32, axis=1) * s1


def _rope_t(dy, c, s1, s2):
    w = dy.shape[-1]
    return dy * c + pltpu.roll(dy * s2, w - 32, axis=1) + pltpu.roll(dy * s1, 32, axis=1)


def _mla_pre(proj, qg, kvg, wq, wkv, tabs, name, ts=256):
    bsz, seq, _ = proj.shape

    def body(p_ref, qg_ref, kvg_ref, wq_ref, wkv_ref, c_ref, s1_ref, s2_ref, q_ref, kv_ref):
        nq = _rms_fn(p_ref[:, 0:256], qg_ref[...])
        nkv = _rms_fn(p_ref[:, 256:384], kvg_ref[...])
        c, s1, s2 = c_ref[...], s1_ref[...], s2_ref[...]
        qp = jnp.dot(nq.astype(BF16), wq_ref[...], preferred_element_type=F32)
        q_ref[...] = _rope(qp, _tile4(c), _tile4(s1), _tile4(s2)).astype(q_ref.dtype)
        kv = jnp.dot(nkv.astype(BF16), wkv_ref[...], preferred_element_type=F32)
        krr = _rope(p_ref[:, 384:512], c, s1, s2)
        zero = jnp.zeros_like(krr)
        kv_ref[...] = (kv + jnp.concatenate([krr, zero] * N_HEADS, axis=1)).astype(kv_ref.dtype)

    tab_spec = pl.BlockSpec((ts, LANES), lambda b, s: (s, 0))
    return pl.pallas_call(
        body, name=name, grid=(bsz, seq // ts),
        in_specs=[pl.BlockSpec((None, ts, 512), lambda b, s: (b, s, P_B // 512)),
                  _vec_spec(256), _vec_spec(128),
                  pl.BlockSpec((256, 512), lambda b, s: (0, 0)), pl.BlockSpec((128, 1024), lambda b, s: (0, 0)),
                  tab_spec, tab_spec, tab_spec],
        out_specs=[_row_spec(ts, 512), _row_spec(ts, 1024)],
        out_shape=[jax.ShapeDtypeStruct((bsz, seq, 512), BF16), jax.ShapeDtypeStruct((bsz, seq, 1024), BF16)],
        compiler_params=_cparams(("parallel", "parallel")),
    )(proj, qg, kvg, wq, wkv, *tabs)


def _mla_pre_bwd(dq, dkv, dproj, proj, qg, kvg, wq, wkv, tabs, name, ts=256):
    bsz, seq, _ = proj.shape

    def body(dq_ref, dkv_ref, dp_any, p_ref, qg_ref, kvg_ref, wq_ref, wkv_ref, c_ref, s1_ref, s2_ref,
             dp_ref, dqg_ref, dkvg_ref, dwq_ref, dwkv_ref):
        del dp_any
        first = jnp.logical_and(pl.program_id(0) == 0, pl.program_id(1) == 0)

        @pl.when(first)
        def _():
            dqg_ref[...] = jnp.zeros_like(dqg_ref)
            dkvg_ref[...] = jnp.zeros_like(dkvg_ref)
            dwq_ref[...] = jnp.zeros_like(dwq_ref)
            dwkv_ref[...] = jnp.zeros_like(dwkv_ref)

        c, s1, s2 = c_ref[...], s1_ref[...], s2_ref[...]
        nq, vjp_q = jax.vjp(_rms_fn, p_ref[:, 0:256], qg_ref[...])
        nkv, vjp_kv = jax.vjp(_rms_fn, p_ref[:, 256:384], kvg_ref[...])
        dqp = _rope_t(dq_ref[...], _tile4(c), _tile4(s1), _tile4(s2)).astype(BF16)
        dkv_v = dkv_ref[...]
        dkv_b = dkv_v.astype(BF16)
        tn = (((0,), (0,)), ((), ()))
        nt = (((1,), (1,)), ((), ()))
        dwq_ref[...] += lax.dot_general(nq.astype(BF16), dqp, tn, preferred_element_type=F32)
        dwkv_ref[...] += lax.dot_general(nkv.astype(BF16), dkv_b, tn, preferred_element_type=F32)
        dcq, dqg = vjp_q(lax.dot_general(dqp, wq_ref[...], nt, preferred_element_type=F32))
        dckv, dkvg = vjp_kv(lax.dot_general(dkv_b, wkv_ref[...], nt, preferred_element_type=F32))
        dqg_ref[...] += dqg
        dkvg_ref[...] += dkvg
        dk_sum = dkv_v[:, 0:128] + dkv_v[:, 256:384] + dkv_v[:, 512:640] + dkv_v[:, 768:896]
        lane = lax.broadcasted_iota(jnp.int32, dk_sum.shape, 1)
        dkr = jnp.where(lane >= 64, _rope_t(dk_sum, c, s1, s2), 0.0)
        dp_ref[:, 0:256] = dcq.astype(dp_ref.dtype)
        dp_ref[:, 256:384] = dckv.astype(dp_ref.dtype)
        dp_ref[:, 384:512] = dkr.astype(dp_ref.dtype)

    tab_spec = pl.BlockSpec((ts, LANES), lambda b, s: (s, 0))
    const = lambda shape: pl.BlockSpec(shape, lambda b, s: (0, 0))
    return pl.pallas_call(
        body, name=name, grid=(bsz, seq // ts),
        in_specs=[_row_spec(ts, 512), _row_spec(ts, 1024), pl.BlockSpec(memory_space=pl.ANY),
                  pl.BlockSpec((None, ts, 512), lambda b, s: (b, s, P_B // 512)),
                  _vec_spec(256), _vec_spec(128), const((256, 512)), const((128, 1024)),
                  tab_spec, tab_spec, tab_spec],
        out_specs=[pl.BlockSpec((None, ts, 512), lambda b, s: (b, s, P_B // 512)),
                   _vec_spec(256), _vec_spec(128), const((256, 512)), const((128, 1024))],
        out_shape=[jax.ShapeDtypeStruct(dproj.shape, dproj.dtype), jax.ShapeDtypeStruct((1, 256), F32),
                   jax.ShapeDtypeStruct((1, 128), F32), jax.ShapeDtypeStruct((256, 512), F32),
                   jax.ShapeDtypeStruct((128, 1024), F32)],
        input_output_aliases={2: 0},
        compiler_params=_cparams(("arbitrary", "arbitrary")),
    )(dq, dkv, dproj, proj, qg, kvg, wq, wkv, *tabs)


def _fox_gate(proj, bf, name):
    bsz, seq, _ = proj.shape
    n_blk = seq // LANES

    def body(x_ref, bf_ref, f_ref):
        r_i = lax.broadcasted_iota(jnp.int32, (LANES, LANES), 0)
        c_i = lax.broadcasted_iota(jnp.int32, (LANES, LANES), 1)
        tril = (r_i >= c_i).astype(F32)
        bias = bf_ref[...]

        def blk(i, carry):
            r = pl.multiple_of(i * LANES, LANES)
            lf = _log_sigmoid(x_ref[pl.ds(r, LANES), :] + bias)
            f_ref[pl.ds(r, LANES), :] = jnp.dot(tril, lf, precision=HI, preferred_element_type=F32) + carry
            return carry + jnp.sum(lf, axis=0, keepdims=True)

        lax.fori_loop(0, n_blk, blk, jnp.zeros((1, LANES), F32))

    return pl.pallas_call(
        body, name=name, grid=(bsz,),
        in_specs=[pl.BlockSpec((None, seq, LANES), lambda b: (b, 0, P_CF // LANES)),
                  pl.BlockSpec((1, LANES), lambda b: (0, 0))],
        out_specs=pl.BlockSpec((None, seq, LANES), lambda b: (b, 0, 0)),
        out_shape=jax.ShapeDtypeStruct((bsz, seq, LANES), F32),
        compiler_params=_cparams(("parallel",)),
    )(proj, bf)


def _fox_gate_bwd(dfq, dfk_cols, dproj, proj, bf, name):
    bsz, seq, _ = proj.shape
    n_blk = seq // LANES

    def body(dfq_ref, dfk_ref, dp_any, x_ref, bf_ref, dp_ref, dbf_ref):
        del dp_any

        @pl.when(pl.program_id(0) == 0)
        def _():
            dbf_ref[...] = jnp.zeros_like(dbf_ref)

        r_i = lax.broadcasted_iota(jnp.int32, (LANES, LANES), 0)
        c_i = lax.broadcasted_iota(jnp.int32, (LANES, LANES), 1)
        triu = (r_i <= c_i).astype(F32)
        bias = bf_ref[...]

        def blk(t, carry):
            tail, dbf = carry
            r = pl.multiple_of((n_blk - 1 - t) * LANES, LANES)
            dc = dfk_ref[pl.ds(r, LANES), :]
            for hd in range(N_HEADS):
                dc = dc + jnp.where(c_i == hd, dfq_ref[hd, pl.ds(r, LANES), :], 0.0)
            dlf = jnp.dot(triu, dc, precision=HI, preferred_element_type=F32) + tail
            dx = dlf * (1.0 - jax.nn.sigmoid(x_ref[pl.ds(r, LANES), :] + bias))
            dp_ref[pl.ds(r, LANES), :] = dx.astype(dp_ref.dtype)
            return tail + jnp.sum(dc, axis=0, keepdims=True), dbf + jnp.sum(dx, axis=0, keepdims=True)

        z = jnp.zeros((1, LANES), F32)
        _, dbf = lax.fori_loop(0, n_blk, blk, (z, z))
        dbf_ref[...] += dbf

    return pl.pallas_call(
        body, name=name, grid=(bsz,),
        in_specs=[pl.BlockSpec((None, N_HEADS, seq, LANES), lambda b: (b, 0, 0, 0)),
                  pl.BlockSpec((None, seq, LANES), lambda b: (b, 0, 0)), pl.BlockSpec(memory_space=pl.ANY),
                  pl.BlockSpec((None, seq, LANES), lambda b: (b, 0, P_CF // LANES)),
                  pl.BlockSpec((1, LANES), lambda b: (0, 0))],
        out_specs=[pl.BlockSpec((None, seq, LANES), lambda b: (b, 0, P_CF // LANES)),
                   pl.BlockSpec((1, LANES), lambda b: (0, 0))],
        out_shape=[jax.ShapeDtypeStruct(dproj.shape, dproj.dtype), jax.ShapeDtypeStruct((1, LANES), F32)],
        input_output_aliases={2: 0},
        compiler_params=_cparams(("arbitrary",)),
    )(dfq, dfk_cols, dproj, proj, bf)


def _gate_terms(fc_ref, fr_ref, h, tq, tk):
    lane = lax.broadcasted_iota(jnp.int32, (tq, LANES), 1)
    fcol = jnp.sum(jnp.where(lane == h, fc_ref[...], 0.0), axis=1, keepdims=True)
    sub = lax.broadcasted_iota(jnp.int32, (8, tk), 0)
    frow = jnp.sum(jnp.where(sub == h, fr_ref[...], 0.0), axis=0, keepdims=True)
    return fcol - frow


def _scores(q_ref, k_ref, gate_refs, scale, h, masked, tq, tk):
    q = (q_ref[...].astype(F32) * scale).astype(BF16)
    s = lax.dot_general(q, k_ref[...].astype(BF16), _DN["nt"], preferred_element_type=F32)
    if gate_refs is not None:
        s = s + _gate_terms(gate_refs[0], gate_refs[1], h, tq, tk)
    if masked is not False:
        r_i = lax.broadcasted_iota(jnp.int32, (tq, tk), 0)
        c_i = lax.broadcasted_iota(jnp.int32, (tq, tk), 1)
        keep = c_i <= r_i
        s = jnp.where(keep if masked is True else jnp.logical_or(jnp.logical_not(masked), keep), s, NEG)
    return s, q


def _lanes(col):
    return jnp.broadcast_to(col, (col.shape[0], LANES))


def _attn_fwd(qa, q0, kva, kv0, mo, o0, gates, scale, name, tq=None):
    bsz, seq, _ = qa.shape
    tq = ATTN_TILE if tq is None else tq
    n_q = seq // tq
    gated = gates is not None

    def body(*refs):
        q_ref, k_ref, v_ref = refs[:3]
        gate_refs = refs[3:5] if gated else None
        o_ref, lse_ref, m_s, l_s, acc_s = refs[-5:]
        h, i, j = pl.program_id(1), pl.program_id(2), pl.program_id(3)

        @pl.when(j == 0)
        def _():
            m_s[...] = jnp.full_like(m_s, NEG)
            l_s[...] = jnp.zeros_like(l_s)
            acc_s[...] = jnp.zeros_like(acc_s)

        def step(masked):
            s, _ = _scores(q_ref, k_ref, gate_refs, scale, h, masked, tq, tq)
            m_prev = m_s[...]
            m_new = jnp.maximum(m_prev, jnp.max(s, axis=1, keepdims=True))
            alpha = jnp.exp(m_prev - m_new)
            p = jnp.exp(s - m_new)
            l_s[...] = alpha * l_s[...] + jnp.sum(p, axis=1, keepdims=True)
            acc_s[...] = alpha * acc_s[...] + jnp.dot(p.astype(BF16), v_ref[...].astype(BF16),
                                                      preferred_element_type=F32)
            m_s[...] = m_new

        @pl.when(j <= i)
        def _():
            step(j == i)

        @pl.when(j == i)
        def _():
            o_ref[...] = (acc_s[...] / l_s[...]).astype(o_ref.dtype)
            lse_ref[...] = _lanes(m_s[...] + jnp.log(l_s[...]))

    blk = (None, tq, LANES)
    in_specs = [pl.BlockSpec(blk, lambda b, h, i, j: (b, i, q0 + h)),
                pl.BlockSpec(blk, lambda b, h, i, j: (b, jnp.minimum(j, i), kv0 + 2 * h)),
                pl.BlockSpec(blk, lambda b, h, i, j: (b, jnp.minimum(j, i), kv0 + 2 * h + 1))]
    args = [qa, kva, kva]
    if gated:
        in_specs += [pl.BlockSpec(blk, lambda b, h, i, j: (b, i, 0)),
                     pl.BlockSpec((None, 8, tq), lambda b, h, i, j: (b, 0, jnp.minimum(j, i)))]
        args += list(gates)
    in_specs.append(pl.BlockSpec(memory_space=pl.ANY))
    args.append(mo)
    return pl.pallas_call(
        body, name=name, grid=(bsz, N_HEADS, n_q, n_q), in_specs=in_specs,
        out_specs=[pl.BlockSpec(blk, lambda b, h, i, j: (b, i, o0 + h)),
                   pl.BlockSpec((None, None, tq, LANES), lambda b, h, i, j: (b, h, i, 0))],
        out_shape=[jax.ShapeDtypeStruct(mo.shape, mo.dtype),
                   jax.ShapeDtypeStruct((bsz, N_HEADS, seq, LANES), F32)],
        scratch_shapes=[pltpu.VMEM((tq, 1), F32), pltpu.VMEM((tq, 1), F32), pltpu.VMEM((tq, LANES), F32)],
        input_output_aliases={len(args) - 1: 0},
        compiler_params=_cparams(("parallel", "parallel", "parallel", "arbitrary")),
    )(*args)


def _attn_bwd_q(qa, q0, kva, kv0, mo, dmo, o0, lse, gates, scale, out, out0, name, tq=None):
    bsz, seq, _ = qa.shape
    tq = ATTN_TILE if tq is None else tq
    n_q = seq // tq
    gated = gates is not None
    aliased = not isinstance(out, jax.ShapeDtypeStruct)

    def body(*refs):
        q_ref, k_ref, v_ref, o_ref, do_ref, lse_ref = refs[:6]
        gate_refs = refs[6:8] if gated else None
        dq_ref, delta_ref, dfq_ref, acc_s, dl_s, df_s = refs[-6:]
        h, i, j = pl.program_id(1), pl.program_id(2), pl.program_id(3)

        @pl.when(j == 0)
        def _():
            acc_s[...] = jnp.zeros_like(acc_s)
            df_s[...] = jnp.zeros_like(df_s)
            dl_s[...] = jnp.sum(do_ref[...] * o_ref[...].astype(F32), axis=1, keepdims=True)

        def step(masked):
            s, _ = _scores(q_ref, k_ref, gate_refs, scale, h, masked, tq, tq)
            p = jnp.exp(s - lse_ref[:, 0:1])
            dp = lax.dot_general(do_ref[...].astype(BF16), v_ref[...].astype(BF16), _DN["nt"],
                                 preferred_element_type=F32)
            ds = p * (dp - dl_s[...])
            acc_s[...] += jnp.dot(ds.astype(BF16), k_ref[...].astype(BF16), preferred_element_type=F32)
            df_s[...] += jnp.sum(ds, axis=1, keepdims=True)

        @pl.when(j <= i)
        def _():
            step(j == i)

        @pl.when(j == i)
        def _():
            dq_ref[...] = (acc_s[...] * scale).astype(dq_ref.dtype)
            delta_ref[...] = _lanes(dl_s[...])
            dfq_ref[...] = _lanes(df_s[...])

    blk = (None, tq, LANES)
    col = pl.BlockSpec((None, None, tq, LANES), lambda b, h, i, j: (b, h, i, 0))
    in_specs = [pl.BlockSpec(blk, lambda b, h, i, j: (b, i, q0 + h)),
                pl.BlockSpec(blk, lambda b, h, i, j: (b, jnp.minimum(j, i), kv0 + 2 * h)),
                pl.BlockSpec(blk, lambda b, h, i, j: (b, jnp.minimum(j, i), kv0 + 2 * h + 1)),
                pl.BlockSpec(blk, lambda b, h, i, j: (b, i, o0 + h)),
                pl.BlockSpec(blk, lambda b, h, i, j: (b, i, o0 + h)), col]
    args = [qa, kva, kva, mo, dmo, lse]
    if gated:
        in_specs += [pl.BlockSpec(blk, lambda b, h, i, j: (b, i, 0)),
                     pl.BlockSpec((None, 8, tq), lambda b, h, i, j: (b, 0, jnp.minimum(j, i)))]
        args += list(gates)
    aliases = {}
    if aliased:
        in_specs.append(pl.BlockSpec(memory_space=pl.ANY))
        args.append(out)
        aliases = {len(args) - 1: 0}
    vec = jax.ShapeDtypeStruct((bsz, N_HEADS, seq, LANES), F32)
    return pl.pallas_call(
        body, name=name, grid=(bsz, N_HEADS, n_q, n_q), in_specs=in_specs,
        out_specs=[pl.BlockSpec(blk, lambda b, h, i, j: (b, i, out0 + h)), col, col],
        out_shape=[jax.ShapeDtypeStruct(out.shape, out.dtype), vec, vec],
        scratch_shapes=[pltpu.VMEM((tq, LANES), F32), pltpu.VMEM((tq, 1), F32), pltpu.VMEM((tq, 1), F32)],
        input_output_aliases=aliases,
        compiler_params=_cparams(("parallel", "parallel", "parallel", "arbitrary")),
    )(*args)


def _attn_bwd_kv(qa, q0, kva, kv0, dmo, o0, lse, delta, gates, scale, out, out0, name, tq=None):
    bsz, seq, _ = qa.shape
    tq = ATTN_TILE if tq is None else tq
    n_q = seq // tq
    gated = gates is not None
    aliased = not isinstance(out, jax.ShapeDtypeStruct)

    def body(*refs):
        q_ref, k_ref, v_ref, do_ref, lse_ref, dl_ref = refs[:6]
        gate_refs = refs[6:8] if gated else None
        dkv_ref, dfk_ref, dk_s, dv_s, df_s = refs[-5:]
        h, j, i = pl.program_id(1), pl.program_id(2), pl.program_id(3)

        @pl.when(i == 0)
        def _():
            dk_s[...] = jnp.zeros_like(dk_s)
            dv_s[...] = jnp.zeros_like(dv_s)
            df_s[...] = jnp.zeros_like(df_s)

        def step(masked):
            s, q = _scores(q_ref, k_ref, gate_refs, scale, h, masked, tq, tq)
            p = jnp.exp(s - lse_ref[:, 0:1])
            do_b = do_ref[...].astype(BF16)
            dp = lax.dot_general(do_b, v_ref[...].astype(BF16), _DN["nt"], preferred_element_type=F32)
            ds = p * (dp - dl_ref[:, 0:1])
            dv_s[...] += lax.dot_general(p.astype(BF16), do_b, _DN["tn"], preferred_element_type=F32)
            dk_s[...] += lax.dot_general(ds.astype(BF16), q, _DN["tn"], preferred_element_type=F32)
            df_s[...] -= jnp.sum(ds, axis=0, keepdims=True)

        @pl.when(i > j)
        def _():
            step(False)

        @pl.when(i == j)
        def _():
            step(True)

        @pl.when(i == n_q - 1)
        def _():
            dkv_ref[:, 0:LANES] = dk_s[...].astype(dkv_ref.dtype)
            dkv_ref[:, LANES:2 * LANES] = dv_s[...].astype(dkv_ref.dtype)
            dfk_ref[...] = df_s[...]

    blk = (None, tq, LANES)
    col = pl.BlockSpec((None, None, tq, LANES), lambda b, h, j, i: (b, h, jnp.maximum(i, j), 0))
    in_specs = [pl.BlockSpec(blk, lambda b, h, j, i: (b, jnp.maximum(i, j), q0 + h)),
                pl.BlockSpec(blk, lambda b, h, j, i: (b, j, kv0 + 2 * h)),
                pl.BlockSpec(blk, lambda b, h, j, i: (b, j, kv0 + 2 * h + 1)),
                pl.BlockSpec(blk, lambda b, h, j, i: (b, jnp.maximum(i, j), o0 + h)), col, col]
    args = [qa, kva, kva, dmo, lse, delta]
    if gated:
        in_specs += [pl.BlockSpec(blk, lambda b, h, j, i: (b, jnp.maximum(i, j), 0)),
                     pl.BlockSpec((None, 8, tq), lambda b, h, j, i: (b, 0, j))]
        args += list(gates)
    aliases = {}
    if aliased:
        in_specs.append(pl.BlockSpec(memory_space=pl.ANY))
        args.append(out)
        aliases = {len(args) - 1: 0}
    return pl.pallas_call(
        body, name=name, grid=(bsz, N_HEADS, n_q, n_q), in_specs=in_specs,
        out_specs=[pl.BlockSpec((None, tq, 2 * LANES), lambda b, h, j, i: (b, j, out0 + h)),
                   pl.BlockSpec((None, None, 1, tq), lambda b, h, j, i: (b, h, 0, j))],
        out_shape=[jax.ShapeDtypeStruct(out.shape, out.dtype), jax.ShapeDtypeStruct((bsz, N_HEADS, 1, seq), F32)],
        scratch_shapes=[pltpu.VMEM((tq, LANES), F32), pltpu.VMEM((tq, LANES), F32), pltpu.VMEM((1, tq), F32)],
        input_output_aliases=aliases,
        compiler_params=_cparams(("parallel", "parallel", "parallel", "arbitrary")),
    )(*args)


def _gmlp_fn(uv, lng, lnb, ws, bst):
    u = jax.nn.gelu(uv[:, 0:GROUP_WIDTH])
    gv = jax.nn.gelu(uv[:, GROUP_WIDTH:2 * GROUP_WIDTH])
    mu = jnp.mean(gv, axis=-1, keepdims=True)
    vc = gv - mu
    var = jnp.mean(vc * vc, axis=-1, keepdims=True)
    vln = vc * lax.rsqrt(var + LN_EPS) * lng + lnb
    r_i = lax.broadcasted_iota(jnp.int32, (D_CHUNK, D_CHUNK), 0)
    c_i = lax.broadcasted_iota(jnp.int32, (D_CHUNK, D_CHUNK), 1)
    lane_g = lax.broadcasted_iota(jnp.int32, (D_CHUNK, GROUP_WIDTH), 1) // HEAD_DIM
    e_r = lax.broadcasted_iota(jnp.int32, (LANES, GROUP_WIDTH), 0)
    e_c = lax.broadcasted_iota(jnp.int32, (LANES, GROUP_WIDTH), 1)
    expand = (e_r == e_c // HEAD_DIM).astype(F32)
    mixed = jnp.dot(bst, expand, precision=HI, preferred_element_type=F32)
    for g in range(4):
        w = jnp.where(r_i >= c_i, ws[g], 0.0)
        mixed = mixed + jnp.where(lane_g == g, _bdot(w, vln, "nn"), 0.0)
    return u * mixed


def _gmlp_fwd(proj, mo, lng, lnb, ws, bst, name):
    bsz, seq, _ = proj.shape

    def body(p_ref, mo_any, lng_ref, lnb_ref, ws_ref, bst_ref, o_ref):
        del mo_any
        o_ref[...] = _gmlp_fn(p_ref[...], lng_ref[...], lnb_ref[...], ws_ref[...], bst_ref[...]).astype(o_ref.dtype)

    return pl.pallas_call(
        body, name=name, grid=(bsz, seq // D_CHUNK),
        in_specs=[pl.BlockSpec((None, D_CHUNK, 512), lambda b, s: (b, s, P_D // 512)),
                  pl.BlockSpec(memory_space=pl.ANY), _vec_spec(256), _vec_spec(256),
                  pl.BlockSpec((4, D_CHUNK, D_CHUNK), lambda b, s: (0, 0, 0)),
                  pl.BlockSpec((D_CHUNK, LANES), lambda b, s: (0, 0))],
        out_specs=pl.BlockSpec((None, D_CHUNK, GROUP_WIDTH), lambda b, s: (b, s, 1280 // GROUP_WIDTH)),
        out_shape=jax.ShapeDtypeStruct(mo.shape, mo.dtype),
        input_output_aliases={1: 0},
        compiler_params=_cparams(("parallel", "parallel")),
    )(proj, mo, lng, lnb, ws, bst)


def _gmlp_bwd(dmo, dproj, proj, lng, lnb, ws, bst, name):
    bsz, seq, _ = proj.shape

    def body(do_ref, dp_any, p_ref, lng_ref, lnb_ref, ws_ref, bst_ref, dp_ref, dlg_ref, dlb_ref, dws_ref, dbst_ref):
        del dp_any
        first = jnp.logical_and(pl.program_id(0) == 0, pl.program_id(1) == 0)

        @pl.when(first)
        def _():
            dlg_ref[...] = jnp.zeros_like(dlg_ref)
            dlb_ref[...] = jnp.zeros_like(dlb_ref)
            dws_ref[...] = jnp.zeros_like(dws_ref)
            dbst_ref[...] = jnp.zeros_like(dbst_ref)

        _, vjp = jax.vjp(_gmlp_fn, p_ref[...], lng_ref[...], lnb_ref[...], ws_ref[...], bst_ref[...])
        duv, dlg, dlb, dws, dbst = vjp(do_ref[...])
        dp_ref[...] = duv.astype(dp_ref.dtype)
        dlg_ref[...] += dlg
        dlb_ref[...] += dlb
        dws_ref[...] += dws
        dbst_ref[...] += dbst

    const2 = lambda shape: pl.BlockSpec(shape, lambda b, s: (0,) * len(shape))
    return pl.pallas_call(
        body, name=name, grid=(bsz, seq // D_CHUNK),
        in_specs=[pl.BlockSpec((None, D_CHUNK, GROUP_WIDTH), lambda b, s: (b, s, 1280 // GROUP_WIDTH)),
                  pl.BlockSpec(memory_space=pl.ANY),
                  pl.BlockSpec((None, D_CHUNK, 512), lambda b, s: (b, s, P_D // 512)),
                  _vec_spec(256), _vec_spec(256), const2((4, D_CHUNK, D_CHUNK)), const2((D_CHUNK, LANES))],
        out_specs=[pl.BlockSpec((None, D_CHUNK, 512), lambda b, s: (b, s, P_D // 512)),
                   _vec_spec(256), _vec_spec(256), const2((4, D_CHUNK, D_CHUNK)), const2((D_CHUNK, LANES))],
        out_shape=[jax.ShapeDtypeStruct(dproj.shape, dproj.dtype), jax.ShapeDtypeStruct((1, 256), F32),
                   jax.ShapeDtypeStruct((1, 256), F32), jax.ShapeDtypeStruct((4, D_CHUNK, D_CHUNK), F32),
                   jax.ShapeDtypeStruct((D_CHUNK, LANES), F32)],
        input_output_aliases={1: 0},
        compiler_params=_cparams(("arbitrary", "arbitrary")),
    )(dmo, dproj, proj, lng, lnb, ws, bst)


def _ada_fwd(c_all, ada_w, name):
    n_b = c_all.shape[0]
    depth, d, cols = ada_w.shape

    def body(c_ref, w_ref, o_ref):
        cv = c_ref[...]
        act = (cv * jax.nn.sigmoid(cv)).astype(BF16)
        o_ref[...] = jnp.dot(act, w_ref[...].astype(BF16), preferred_element_type=F32)

    return pl.pallas_call(
        body, name=name, grid=(depth,),
        in_specs=[pl.BlockSpec((n_b, d), lambda l: (0, 0)), pl.BlockSpec((None, d, cols), lambda l: (l, 0, 0))],
        out_specs=pl.BlockSpec((None, n_b, cols), lambda l: (l, 0, 0)),
        out_shape=jax.ShapeDtypeStruct((depth, n_b, cols), F32),
        compiler_params=_cparams(("parallel",)),
    )(c_all, ada_w)


def _ada_bwd(c_all, dmod_cols, dmod_full, name):
    n_b, d = c_all.shape
    depth, _, cols = dmod_cols.shape
    full = dmod_full.shape[-1]

    def body(c_ref, dm_ref, df_ref, gw_ref, gb_ref):
        cv = c_ref[...]
        act = (cv * jax.nn.sigmoid(cv)).astype(BF16)
        gw_ref[...] = lax.dot_general(act, dm_ref[...].astype(BF16), (((0,), (0,)), ((), ())),
                                      preferred_element_type=F32)
        gb_ref[...] = jnp.sum(df_ref[...], axis=0, keepdims=True)

    return pl.pallas_call(
        body, name=name, grid=(depth,),
        in_specs=[pl.BlockSpec((n_b, d), lambda l: (0, 0)), pl.BlockSpec((None, n_b, cols), lambda l: (l, 0, 0)),
                  pl.BlockSpec((None, n_b, full), lambda l: (l, 0, 0))],
        out_specs=[pl.BlockSpec((None, d, cols), lambda l: (l, 0, 0)),
                   pl.BlockSpec((None, 1, full), lambda l: (l, 0, 0))],
        out_shape=[jax.ShapeDtypeStruct((depth, d, cols), F32), jax.ShapeDtypeStruct((depth, 1, full), F32)],
        compiler_params=_cparams(("parallel",)),
    )(c_all, dmod_cols, dmod_full)


def _adamw(gparts, own, w, m, v, name, layer=0, prev=None):
    n_p, rows, cols = gparts.shape
    assert w.shape[1:] == (rows, cols)
    tr = rows
    if rows > 512:
        tr = next(c for c in range(512, 7, -8) if rows % c == 0)
    has_own = own is not None
    n_prev = 0 if prev is None else 4

    def body(*refs):
        g_ref = refs[0]
        own_ref = refs[1] if has_own else None
        w_ref, m_ref, v_ref = refs[1 + has_own:4 + has_own]
        go_ref, do_ref, mo_ref, vo_ref = refs[4 + has_own + n_prev:]
        if has_own:
            g = own_ref[...].astype(F32) + g_ref[0].astype(F32)
        else:
            g = g_ref[0].astype(F32)
        for p in range(1, n_p):
            g = g + g_ref[p].astype(F32)
        m_new = ADAM_B1 * m_ref[...] + (1.0 - ADAM_B1) * g
        v_new = ADAM_B2 * v_ref[...] + (1.0 - ADAM_B2) * (g * g)
        m_hat = m_new / (1.0 - ADAM_B1 ** ADAM_STEP)
        v_hat = v_new / (1.0 - ADAM_B2 ** ADAM_STEP)
        go_ref[...] = g
        do_ref[...] = -ADAM_LR * (m_hat / (jnp.sqrt(v_hat) + ADAM_EPS) + ADAM_WD * w_ref[...])
        mo_ref[...] = m_new
        vo_ref[...] = v_new

    spec = pl.BlockSpec((None, tr, cols), lambda i: (layer, i, 0))
    in_specs = [pl.BlockSpec((n_p, tr, cols), lambda i: (0, i, 0))]
    args = [gparts]
    if has_own:
        in_specs.append(pl.BlockSpec((tr, cols), lambda i: (i, 0)))
        args.append(own)
    in_specs += [spec, spec, spec]
    args += [w, m, v]
    aliases = {}
    if prev is not None:
        aliases = {len(args) + k: k for k in range(4)}
        in_specs += [pl.BlockSpec(memory_space=pl.ANY)] * 4
        args += list(prev)
    shp = jax.ShapeDtypeStruct(w.shape, F32)
    return pl.pallas_call(
        body, name=name, grid=(rows // tr,), in_specs=in_specs,
        out_specs=[spec, spec, spec, spec], out_shape=[shp, shp, shp, shp], input_output_aliases=aliases,
        compiler_params=_cparams(("parallel",)),
    )(*args)


def _sum_parts(parts, name):
    n_p, rows, cols = parts.shape
    tr = 256 if rows % 256 == 0 else rows

    def body(p_ref, o_ref):
        acc = p_ref[0]
        for p in range(1, n_p):
            acc = acc + p_ref[p]
        o_ref[...] = acc

    return pl.pallas_call(
        body, name=name, grid=(rows // tr,),
        in_specs=[pl.BlockSpec((n_p, tr, cols), lambda i: (0, i, 0))],
        out_specs=pl.BlockSpec((tr, cols), lambda i: (i, 0)),
        out_shape=jax.ShapeDtypeStruct((rows, cols), F32),
        compiler_params=_cparams(("parallel",)),
    )(parts)


def _all_gather(arrs, name):
    n = len(arrs)

    def body(*refs):
        in_refs, out_refs = refs[:n], refs[n:2 * n]
        send_sems, recv_sems, loc_sems = refs[2 * n:]
        x, y, c = lax.axis_index("x"), lax.axis_index("y"), lax.axis_index("c")
        me, sibling = (x, y, c), (x, y, 1 - c)
        chips = [(1 - x, y), (x, 1 - y), (1 - x, 1 - y)]

        def copy(a, k, block, to, src=None):
            slot = out_refs[a].at[4 * block[0] + 2 * block[1] + block[2]]
            return pltpu.make_async_remote_copy(
                src_ref=slot if src is None else src, dst_ref=slot, send_sem=send_sems.at[a, k],
                recv_sem=recv_sems.at[a, k], device_id=to, device_id_type=pl.DeviceIdType.MESH)

        mine = [pltpu.make_async_copy(in_refs[a], out_refs[a].at[4 * x + 2 * y + c], loc_sems.at[a])
                for a in range(n)]
        for cp in mine:
            cp.start()
        first = []
        for a in range(n):
            first.append(copy(a, 0, me, sibling, src=in_refs[a]))
            first += [copy(a, 1 + j, me, (*chip, c), src=in_refs[a]) for j, chip in enumerate(chips)]
        for cp in first:
            cp.start()
        passed = []
        for j, chip in enumerate(chips):
            for a in range(n):
                copy(a, 1 + j, (*chip, c), me).wait_recv()
                cp = copy(a, 4 + j, (*chip, c), sibling)
                cp.start()
                passed.append(cp)
        for a in range(n):
            copy(a, 0, sibling, me).wait_recv()
        for j, chip in enumerate(chips):
            for a in range(n):
                copy(a, 4 + j, (*chip, 1 - c), me).wait_recv()
        for cp in first + passed:
            cp.wait_send()
        for cp in mine:
            cp.wait()

    any_spec = pl.BlockSpec(memory_space=pl.ANY)
    return pl.pallas_call(
        body, name=name, in_specs=[any_spec] * n, out_specs=[any_spec] * n,
        out_shape=[jax.ShapeDtypeStruct((N_DEV,) + a.shape, a.dtype) for a in arrs],
        scratch_shapes=[pltpu.SemaphoreType.DMA((n, N_DEV - 1)), pltpu.SemaphoreType.DMA((n, N_DEV - 1)),
                        pltpu.SemaphoreType.DMA((n,))],
    )(*arrs)


def _flip_peers():
    x, y, c = lax.axis_index("x"), lax.axis_index("y"), lax.axis_index("c")
    peers = []
    for fx, fy, fc in [(fx, fy, fc) for fx in (0, 1) for fy in (0, 1) for fc in (0, 1)][1:]:
        px, py, pc = (1 - x if fx else x), (1 - y if fy else y), (1 - c if fc else c)
        peers.append(((px, py, pc), 4 * px + 2 * py + pc))
    return 4 * x + 2 * y + c, peers


def _push_start(srcs, name, whole=False):
    n, n_peer = len(srcs), N_DEV - 1
    if whole:
        me_w = 4 * lax.axis_index("x") + 2 * lax.axis_index("y") + lax.axis_index("c")
        lands = [lax.dynamic_update_slice_in_dim(jnp.zeros((N_DEV,) + a.shape, a.dtype), a[None], me_w, axis=0)
                 for a in srcs]
    else:
        lands = [jnp.zeros(a.shape, a.dtype) for a in srcs]

    def body(*refs):
        src_refs, land_refs = refs[:n], refs[n:2 * n]
        send_sems, recv_sems = refs[2 * n], refs[2 * n + 1]
        token = refs[-1]
        me, peers = _flip_peers()
        for k, (dev, idx) in enumerate(peers):
            for a in range(n):
                pltpu.make_async_remote_copy(
                    src_ref=src_refs[a] if whole else src_refs[a].at[idx], dst_ref=land_refs[a].at[me],
                    send_sem=send_sems.at[a * n_peer + k], recv_sem=recv_sems.at[a * n_peer + k], device_id=dev,
                    device_id_type=pl.DeviceIdType.MESH).start()
        token[...] = jnp.zeros_like(token)

    hbm = pl.BlockSpec(memory_space=pltpu.HBM)
    sem = pl.BlockSpec(memory_space=pltpu.SEMAPHORE)
    arrs = list(srcs) + lands
    res = pl.pallas_call(
        body, name=name, in_specs=[hbm] * (2 * n),
        out_specs=(sem, sem, *[hbm] * (2 * n), pl.BlockSpec(memory_space=pltpu.VMEM)),
        out_shape=(pltpu.SemaphoreType.DMA((n * n_peer,)), pltpu.SemaphoreType.DMA((n * n_peer,)),
                   *[pltpu.HBM(a.shape, a.dtype) for a in arrs], jax.ShapeDtypeStruct((8, LANES), F32)),
        input_output_aliases={i: 2 + i for i in range(2 * n)},
        compiler_params=pltpu.CompilerParams(has_side_effects=pltpu.SideEffectType.DATAFLOW_SIDE_EFFECTING),
    )(*[pltpu.with_memory_space_constraint(a, pltpu.HBM) for a in arrs])
    return res[0], res[1], list(res[2:2 + n]), list(res[2 + n:2 + 2 * n]), res[-1]


def _push_wait(send_sems, recv_sems, srcs, lands, after, name, whole=False):
    n, n_peer = len(srcs), N_DEV - 1

    def body(*refs):
        src_refs, land_refs = refs[:n], refs[n:2 * n]
        send_s, recv_s = refs[2 * n], refs[2 * n + 1]
        _, peers = _flip_peers()
        for k, (dev, idx) in enumerate(peers):
            for a in range(n):
                cp = pltpu.make_async_remote_copy(
                    src_ref=src_refs[a] if whole else src_refs[a].at[idx], dst_ref=land_refs[a].at[idx],
                    send_sem=send_s.at[a * n_peer + k],
                    recv_sem=recv_s.at[a * n_peer + k], device_id=dev, device_id_type=pl.DeviceIdType.MESH)
                cp.wait_send()
                cp.wait_recv()

    hbm = pl.BlockSpec(memory_space=pltpu.HBM)
    sem = pl.BlockSpec(memory_space=pltpu.SEMAPHORE)
    arrs = list(srcs) + list(lands)
    res = pl.pallas_call(
        body, name=name, in_specs=[hbm] * (2 * n) + [sem, sem, pl.BlockSpec(memory_space=pl.ANY)],
        out_specs=tuple([hbm] * (2 * n)), out_shape=tuple(pltpu.HBM(a.shape, a.dtype) for a in arrs),
        input_output_aliases={i: i for i in range(2 * n)},
        compiler_params=pltpu.CompilerParams(has_side_effects=pltpu.SideEffectType.DATAFLOW_SIDE_EFFECTING),
    )(*arrs, send_sems, recv_sems, after)
    return list(res[:n]), list(res[n:])


def _ffn_fwd(x, h, mod, w_in, w_out, lng, lnb, rows, tag, nxt):
    bsz, seq, d = x.shape
    t = bsz * seq
    if h is None:
        h = _modulate(x, mod, rows[0], rows[1], f"modulate_{tag}")
    z, a = _ffn_in_swiglu(h.reshape(t, d), w_in, f"ffn_in_{tag}")
    f = _matmul(a, w_out, mode="nn", group_out=False, out_dtype=F32, tm=1024, tk=a.shape[2],
                name=f"ffn_out_{tag}").reshape(bsz, seq, d)
    y, h_next = _res_ln(x, f, mod, lng, lnb, rows[2], 0.5, f"res_ln_{tag}", nxt)
    return y, h_next, (x, h, z, a, f)


def _tied(mod, tie):
    return mod if tie is None else mod + tie


def _ffn_bwd(dy, saved, mod, w_in, w_out, lng, lnb, rows, tag, ready):
    x, h, z, a, f = saved
    bsz, seq, d = x.shape
    t = bsz * seq
    dx_res, df, dgate, dlg, dlb = _res_ln_bwd(dy, x, f, mod, lng, lnb, rows[2], 0.5, f"res_ln_bwd_{tag}")
    df2 = df.reshape(1, t, d)
    dw_out = _matmul(a, df2, mode="tn", group_out=True, out_dtype=BF16, tm=a.shape[2], tk=min(t, 2048),
                     name=f"ffn_out_dw_{tag}")
    tie_out = ready(f"{tag}_out", dw_out)
    dz = _ffn_out_dx_swiglu(df.reshape(t, d), w_out, z, f"ffn_out_dx_{tag}").reshape(N_DEV, t, -1)
    dw_in = _matmul(h.reshape(1, t, d), dz, mode="tn", group_out=True, out_dtype=BF16, tm=d, tk=min(t, 2048),
                    name=f"ffn_in_dw_{tag}")
    tie_in = ready(f"{tag}_in", dw_in)
    dh = _matmul(dz, w_in, mode="nt", group_out=False, out_dtype=F32, tm=1024, tk=dz.shape[2],
                 name=f"ffn_in_dx_{tag}").reshape(bsz, seq, d)
    dx, dsh, dsc = _modulate_bwd(dh, x, _tied(_tied(mod, tie_out), tie_in), dx_res, rows[1],
                                 f"modulate_bwd_{tag}")
    return dx, (dsh, dsc, dgate), dw_in, dw_out, dlg, dlb


def _mixer_fwd(x, h, mod, wts, small, lng, lnb, layer, tabs):
    bsz, seq, d = x.shape
    t = bsz * seq
    proj = _matmul(h.reshape(1, t, d), wts["mix_in"][None], mode="nn", group_out=True, out_dtype=F32, tm=512, tk=d,
                   name="mix_in").reshape(bsz, seq, PACK_W)
    mo, states = _hgrn_fwd(proj, small["lb_logits8"], small["hgrn_norm_g"], layer, f"hgrn_fwd_l{layer}")
    q, kv = _mla_pre(proj, small["q_norm_g"], small["kv_norm_g"], wts["uq"], wts["ukv"], tabs, "mla_pre")
    mla_scale = float((B_NOPE + B_ROPE) ** -0.5)
    mo, lse_b = _attn_fwd(q, 0, kv, 0, mo, 2, None, mla_scale, "mla_attn_fwd")
    fg = _fox_gate(proj, small["fox_b_f"], "fox_gate")
    gates = (fg, jnp.swapaxes(fg[:, :, 0:8], 1, 2))
    fox_scale = float(HEAD_DIM ** -0.5)
    mo, lse_c = _attn_fwd(proj, P_CQ // LANES, proj, P_CKV // LANES, mo, 6, gates, fox_scale, "fox_attn_fwd")
    mo = _gmlp_fwd(proj, mo, small["gmlp_ln_g"], small["gmlp_ln_b"], small["gmlp_w_s"], small["gmlp_bst"],
                   "gmlp_fwd")
    mixed = _matmul(mo.reshape(1, t, MO_W), wts["mix_out"][None], mode="nn", group_out=True, out_dtype=F32,
                    tm=1024, tk=MO_W, name="mix_out").reshape(bsz, seq, d)
    y, h_next = _res_ln(x, mixed, mod, lng, lnb, 5, 1.0, "res_ln_mix", (mod, 6, 7))
    return y, h_next, (x, h, proj, mo, states, q, kv, lse_b, gates, lse_c, mixed)


def _mixer_bwd(dy, saved, mod, wts, small, lng, lnb, layer, tabs, ready):
    x, h, proj, mo, states, q, kv, lse_b, gates, lse_c, mixed = saved
    bsz, seq, d = x.shape
    t = bsz * seq
    dx_res, dmixed, dgate, dlg, dlb = _res_ln_bwd(dy, x, mixed, mod, lng, lnb, 5, 1.0, "res_ln_bwd_mix")
    dm2 = dmixed.reshape(1, t, d)
    dmo = _matmul(dm2, wts["mix_out"][None], mode="nt", group_out=True, out_dtype=F32, tm=1024, tk=d,
                  name="mix_out_dx").reshape(bsz, seq, MO_W)
    dw_out = _matmul(mo.reshape(1, t, MO_W), dm2, mode="tn", group_out=True, out_dtype=F32, tm=512, tk=min(t, 2048),
                     name="mix_out_dw")[0]
    tie_out = ready("mix_out", dw_out)
    g = {}
    dproj, g["lb_logits8"], g["hgrn_norm_g"] = _hgrn_bwd(dmo, proj, states, small["lb_logits8"],
                                                         small["hgrn_norm_g"], layer, f"hgrn_bwd_l{layer}")
    mla_scale = float((B_NOPE + B_ROPE) ** -0.5)
    dq, delta_b, _ = _attn_bwd_q(q, 0, kv, 0, mo, dmo, 2, lse_b, None, mla_scale,
                                 jax.ShapeDtypeStruct((bsz, seq, 512), F32), 0, "mla_attn_bwd_q")
    dkv, _ = _attn_bwd_kv(q, 0, kv, 0, dmo, 2, lse_b, delta_b, None, mla_scale,
                          jax.ShapeDtypeStruct((bsz, seq, 1024), F32), 0, "mla_attn_bwd_kv")
    dproj, g["q_norm_g"], g["kv_norm_g"], g["uq"], g["ukv"] = _mla_pre_bwd(
        dq, dkv, dproj, proj, small["q_norm_g"], small["kv_norm_g"], wts["uq"], wts["ukv"], tabs, "mla_pre_bwd")
    fox_scale = float(HEAD_DIM ** -0.5)
    dproj, delta_c, dfq = _attn_bwd_q(proj, P_CQ // LANES, proj, P_CKV // LANES, mo, dmo, 6, lse_c, gates,
                                      fox_scale, dproj, P_CQ // LANES, "fox_attn_bwd_q")
    dproj, dfk = _attn_bwd_kv(proj, P_CQ // LANES, proj, P_CKV // LANES, dmo, 6, lse_c, delta_c, gates, fox_scale,
                              dproj, P_CKV // (2 * LANES), "fox_attn_bwd_kv")
    dfk_cols = jnp.pad(jnp.swapaxes(dfk[:, :, 0, :], 1, 2), ((0, 0), (0, 0), (0, LANES - N_HEADS)))
    dproj, g["fox_b_f"] = _fox_gate_bwd(dfq, dfk_cols, dproj, proj, small["fox_b_f"], "fox_gate_bwd")
    dproj, g["gmlp_ln_g"], g["gmlp_ln_b"], g["gmlp_w_s"], g["gmlp_bst"] = _gmlp_bwd(
        dmo, dproj, proj, small["gmlp_ln_g"], small["gmlp_ln_b"], small["gmlp_w_s"], small["gmlp_bst"], "gmlp_bwd")
    dp2 = dproj.reshape(1, t, PACK_W)
    dw_in = _matmul(h.reshape(1, t, d), dp2, mode="tn", group_out=True, out_dtype=BF16, tm=512, tk=1024,
                    name="mix_in_dw")[0]
    tie_in = ready("mix_in", dw_in)
    dh = _matmul(dp2, wts["mix_in"][None], mode="nt", group_out=True, out_dtype=F32, tm=512, tk=PACK_W,
                 name="mix_in_dx").reshape(bsz, seq, d)
    dx, dsh, dsc = _modulate_bwd(dh, x, _tied(_tied(mod, tie_out), tie_in), dx_res, 4, "modulate_bwd_mix")
    return dx, (dsh, dsc, dgate), dw_in, dw_out, g, dlg, dlb


def _small_views(p, layer):
    return {
        "lb_logits8": jnp.pad(p["hgrn_lb_logits"], ((0, 8 - DEPTH), (0, 0))),
        "hgrn_norm_g": p["hgrn_norm_g"][layer][None],
        "q_norm_g": p["mla_q_norm_g"][layer][None],
        "kv_norm_g": p["mla_kv_norm_g"][layer][None],
        "fox_b_f": jnp.pad(p["fox_b_f"][layer][None], ((0, 0), (0, LANES - N_HEADS))),
        "gmlp_ln_g": p["gmlp_ln_g"][layer][None],
        "gmlp_ln_b": p["gmlp_ln_b"][layer][None],
        "gmlp_w_s": p["gmlp_w_s"][layer],
        "gmlp_bst": jnp.pad(p["gmlp_b_s"][layer].T, ((0, 0), (0, LANES - N_HEADS))),
    }


def _local_step(x, mod, target, weights, p, grads_ready=None):
    bsz, seq, d = x.shape
    tabs = _rope_tables(seq)
    saved = []
    h = None
    for l in range(DEPTH):
        sm = _small_views(p, l)
        lng, lnb = p["ln_g"][l], p["ln_b"][l]
        w = weights(l, "ffn1", x)
        x, h, s1 = _ffn_fwd(x, h, mod[l], w["ffn1_in"], w["ffn1_out"], lng[0:1], lnb[0:1], (0, 1, 2), "ffn1",
                            (mod[l], 3, 4))
        x, h, s2 = _mixer_fwd(x, h, mod[l], weights(l, "mix", x), sm, lng[1:2], lnb[1:2], l, tabs)
        w = weights(l, "ffn2", x)
        x, h, s3 = _ffn_fwd(x, h, mod[l], w["ffn2_in"], w["ffn2_out"], lng[2:3], lnb[2:3], (6, 7, 8), "ffn2",
                            (mod[l + 1], 0, 1) if l + 1 < DEPTH else None)
        saved.append((s1, s2, s3))
    dx, loss = _loss_head(x, target, "loss_head")
    big, small, dmods = [None] * DEPTH, [None] * DEPTH, [None] * DEPTH
    ties = []

    def tied(a):
        for t in ties:
            a = a + t
        return a

    for l in reversed(range(DEPTH)):
        w = {**weights(l, "ffn1", None), **weights(l, "mix", None), **weights(l, "ffn2", None)}
        sm = _small_views(p, l)
        lng, lnb = p["ln_g"][l], p["ln_b"][l]
        s1, s2, s3 = saved[l]

        def ready(name, grad, l=l):
            tie = None if grads_ready is None else grads_ready(l, name, grad)
            if tie is not None:
                ties.append(tie)
            return tie

        dx, dm3, dwi2, dwo2, dlg2, dlb2 = _ffn_bwd(dx, s3, tied(mod[l]), w["ffn2_in"], w["ffn2_out"], lng[2:3],
                                                   lnb[2:3], (6, 7, 8), "ffn2", ready)
        dx, dm2, dwmi, dwmo, g, dlg1, dlb1 = _mixer_bwd(dx, s2, tied(mod[l]), w, sm, lng[1:2], lnb[1:2], l, tabs,
                                                        ready)
        dx, dm1, dwi1, dwo1, dlg0, dlb0 = _ffn_bwd(dx, s1, tied(mod[l]), w["ffn1_in"], w["ffn1_out"], lng[0:1],
                                                   lnb[0:1], (0, 1, 2), "ffn1", ready)
        dmods[l] = jnp.concatenate(list(dm1) + list(dm2) + list(dm3), axis=1)
        big[l] = {"ffn1_in": dwi1, "ffn1_out": dwo1, "ffn2_in": dwi2, "ffn2_out": dwo2, "mix_in": dwmi,
                  "mix_out": dwmo}
        g["ln_g"] = jnp.concatenate([dlg0, dlg1, dlg2], axis=0)
        g["ln_b"] = jnp.concatenate([dlb0, dlb1, dlb2], axis=0)
        small[l] = g
    return loss, dx, jnp.stack(dmods), big, small


_BIG = ("ffn1_in", "ffn1_out", "ffn2_in", "ffn2_out", "mix_in", "mix_out")


def _small_grad_list(small, loss):
    def both(fn):
        return jnp.stack([fn(small[l]) for l in range(DEPTH)])

    uq_src, ukv_src = _uq_src(), _ukv_src()
    return [
        ("loss", loss.reshape(1)),
        ("ln_g", both(lambda g: g["ln_g"])), ("ln_b", both(lambda g: g["ln_b"])),
        ("hgrn_lb_logits", small[0]["lb_logits8"][:DEPTH] + small[1]["lb_logits8"][:DEPTH]),
        ("hgrn_norm_g", both(lambda g: g["hgrn_norm_g"][0])),
        ("mla_q_norm_g", both(lambda g: g["q_norm_g"][0])),
        ("mla_kv_norm_g", both(lambda g: g["kv_norm_g"][0])),
        ("mla_w_uq", both(lambda g: _unpack_cols(g["uq"], uq_src, 384))),
        ("mla_w_ukv", both(lambda g: _unpack_cols(g["ukv"], ukv_src, 512))),
        ("fox_b_f", both(lambda g: g["fox_b_f"][0, :N_HEADS])),
        ("gmlp_ln_g", both(lambda g: g["gmlp_ln_g"][0])), ("gmlp_ln_b", both(lambda g: g["gmlp_ln_b"][0])),
        ("gmlp_w_s", both(lambda g: g["gmlp_w_s"])),
        ("gmlp_b_s", both(lambda g: g["gmlp_bst"][:, :N_HEADS].T)),
    ]


_PACK_COLS = 512


def _pack_small(items):
    flat = jnp.concatenate([a.reshape(-1).astype(F32) for _, a in items])
    n = flat.shape[0]
    tile = 8 * _PACK_COLS
    flat = jnp.pad(flat, (0, (-n) % tile))
    return flat.reshape(-1, _PACK_COLS)


def _unpack_small(buf, items):
    flat = buf.reshape(-1)
    out, off = {}, 0
    for name, a in items:
        out[name] = flat[off:off + a.size].reshape(a.shape)
        off += a.size
    return out


def _as2d(a):
    return a.reshape(-1, a.shape[-1])


def kernel(x, c, ada_w, ada_b, ln_g, ln_b, ffn1_w_in, ffn1_w_out, ffn2_w_in, ffn2_w_out, mix_w_in, mix_w_out, hgrn_lb_logits, hgrn_norm_g, mla_q_norm_g, mla_kv_norm_g, mla_w_uq, mla_w_ukv, fox_b_f, gmlp_ln_g, gmlp_ln_b, gmlp_w_s, gmlp_b_s, loss_target, m_ada_w, m_ada_b, m_ln_g, m_ln_b, m_ffn1_w_in, m_ffn1_w_out, m_ffn2_w_in, m_ffn2_w_out, m_mix_w_in, m_mix_w_out, m_hgrn_lb_logits, m_hgrn_norm_g, m_mla_q_norm_g, m_mla_kv_norm_g, m_mla_w_uq, m_mla_w_ukv, m_fox_b_f, m_gmlp_ln_g, m_gmlp_ln_b, m_gmlp_w_s, m_gmlp_b_s, v_ada_w, v_ada_b, v_ln_g, v_ln_b, v_ffn1_w_in, v_ffn1_w_out, v_ffn2_w_in, v_ffn2_w_out, v_mix_w_in, v_mix_w_out, v_hgrn_lb_logits, v_hgrn_norm_g, v_mla_q_norm_g, v_mla_kv_norm_g, v_mla_w_uq, v_mla_w_ukv, v_fox_b_f, v_gmlp_ln_g, v_gmlp_ln_b, v_gmlp_w_s, v_gmlp_b_s):
    names = ["ada_w", "ada_b", "ln_g", "ln_b", "ffn1_w_in", "ffn1_w_out", "ffn2_w_in", "ffn2_w_out", "mix_w_in",
             "mix_w_out", "hgrn_lb_logits", "hgrn_norm_g", "mla_q_norm_g", "mla_kv_norm_g", "mla_w_uq", "mla_w_ukv",
             "fox_b_f", "gmlp_ln_g", "gmlp_ln_b", "gmlp_w_s", "gmlp_b_s"]
    w = dict(zip(names, [ada_w, ada_b, ln_g, ln_b, ffn1_w_in, ffn1_w_out, ffn2_w_in, ffn2_w_out, mix_w_in, mix_w_out,
                         hgrn_lb_logits, hgrn_norm_g, mla_q_norm_g, mla_kv_norm_g, mla_w_uq, mla_w_ukv, fox_b_f,
                         gmlp_ln_g, gmlp_ln_b, gmlp_w_s, gmlp_b_s]))
    m = dict(zip(names, [m_ada_w, m_ada_b, m_ln_g, m_ln_b, m_ffn1_w_in, m_ffn1_w_out, m_ffn2_w_in, m_ffn2_w_out,
                         m_mix_w_in, m_mix_w_out, m_hgrn_lb_logits, m_hgrn_norm_g, m_mla_q_norm_g, m_mla_kv_norm_g,
                         m_mla_w_uq, m_mla_w_ukv, m_fox_b_f, m_gmlp_ln_g, m_gmlp_ln_b, m_gmlp_w_s, m_gmlp_b_s]))
    v = dict(zip(names, [v_ada_w, v_ada_b, v_ln_g, v_ln_b, v_ffn1_w_in, v_ffn1_w_out, v_ffn2_w_in, v_ffn2_w_out,
                         v_mix_w_in, v_mix_w_out, v_hgrn_lb_logits, v_hgrn_norm_g, v_mla_q_norm_g, v_mla_kv_norm_g,
                         v_mla_w_uq, v_mla_w_ukv, v_fox_b_f, v_gmlp_ln_g, v_gmlp_ln_b, v_gmlp_w_s, v_gmlp_b_s]))
    bsz, seq, d = x.shape
    me = 4 * lax.axis_index("x") + 2 * lax.axis_index("y") + lax.axis_index("c")
    mix_src, uq_src, ukv_src, mo_src = _mix_in_src(), _uq_src(), _ukv_src(), _mo_src()

    part_names = {"ffn1": ["ffn1_w_in", "ffn1_w_out"], "mix": ["mix_w_in", "mix_w_out", "mla_w_uq", "mla_w_ukv"],
                  "ffn2": ["ffn2_w_in", "ffn2_w_out"]}
    group_of = {}
    for l in range(DEPTH):
        for part in ("ffn1", "mix", "ffn2"):
            group_of[(l, part)] = (0, part) if l == 0 else (l, "all")
    in_flight = {}

    def start_group(key, behind=None):
        members = [(l, part) for (l, part), g in group_of.items() if g == key]
        labels = [(l, n) for l, part in members for n in part_names[part]]
        shards = []
        for l, n in labels:
            a = w[n][l]
            if n == "mix_w_in":
                a = _pack_cols(a, mix_src)
            shards.append(a.astype(BF16))
        if behind is not None:
            shards, _ = lax.optimization_barrier((shards, behind))
        in_flight[key] = (labels, _push_start(shards, f"gather_start_{key[0]}_{key[1]}", whole=True))

    keys_in_order = list(dict.fromkeys(group_of.values()))
    start_group(keys_in_order[0])

    gathered = _all_gather([c, ln_g, ln_b], "gather_inputs")
    c_all = gathered[0].reshape(N_DEV * bsz, d)
    ln_g_full = jnp.moveaxis(gathered[1], 0, 2).reshape(DEPTH, 3, d)
    ln_b_full = jnp.moveaxis(gathered[2], 0, 2).reshape(DEPTH, 3, d)

    mod_cols = _ada_fwd(c_all, ada_w, "ada_fwd")
    mod_all, = _all_gather([mod_cols], "gather_mod")
    mod_mine = lax.dynamic_slice_in_dim(mod_all, me * bsz, bsz, axis=2)
    mod = jnp.moveaxis(mod_mine, 0, 2).reshape(DEPTH, bsz, N_MOD * d) + ada_b[:, None, :]
    for key in keys_in_order[1:]:
        start_group(key, behind=mod)
    tie = sum(h[-1][0, 0] for _, h in in_flight.values())
    mod = mod.reshape(DEPTH, bsz, N_MOD, d) + tie

    arrived, laid_out = {}, {}

    def weights(l, part, after):
        if (l, part) not in laid_out:
            laid_out[(l, part)] = lay_out(l, part, after)
        return laid_out[(l, part)]

    def lay_out(l, part, after):
        key = group_of[(l, part)]
        if key not in arrived:
            labels, (send_sems, recv_sems, srcs, lands, _) = in_flight[key]
            _, lands = _push_wait(send_sems, recv_sems, srcs, lands, after, f"gather_wait_{key[0]}_{key[1]}",
                                  whole=True)
            arrived[key] = dict(zip(labels, lands))
        gw = {n: arrived[key][(l, n)] for n in part_names[part]}
        if part != "mix":
            return {f"{part}_in": gw[f"{part}_w_in"], f"{part}_out": gw[f"{part}_w_out"].reshape(4, 704, d)}
        uq = jnp.moveaxis(gw["mla_w_uq"], 0, 1).reshape(256, 384)
        ukv = jnp.moveaxis(gw["mla_w_ukv"], 0, 1).reshape(128, 512)
        return {"mix_in": gw["mix_w_in"].reshape(d, PACK_W),
                "mix_out": _pack_cols(gw["mix_w_out"].reshape(d, d).T, mo_src).T,
                "uq": _pack_cols(uq, uq_src), "ukv": _pack_cols(ukv, ukv_src)}

    p = dict(w)
    p["ln_g"], p["ln_b"] = ln_g_full, ln_b_full
    def chunks(name, arr):
        if name in ("ffn1_in", "ffn2_in"):
            return arr
        if name in ("ffn1_out", "ffn2_out"):
            return arr.reshape(N_DEV, arr.shape[1] // 2, d)
        if name == "mix_in":
            return arr.reshape(N_DEV, d // N_DEV, PACK_W)
        return _unpack_cols(arr.T, mo_src, d).T.astype(BF16).reshape(N_DEV, d // N_DEV, d)

    pending, started = {}, []

    def grads_ready(l, name, grad):
        pending[(name, l)] = chunks(name, grad)
        flush = name == "ffn1_in" if l > 0 else name in ("ffn2_in", "mix_out", "mix_in", "ffn1_out", "ffn1_in")
        if not flush:
            return None
        keys = sorted(pending)
        handles = _push_start([pending[k] for k in keys], f"push_start_{len(started)}")
        pending.clear()
        started.append((keys, handles, l == 0 and name.startswith("ffn1")))
        return handles[-1][0, 0]

    loss, grad_x, dmod, big, small = _local_step(x, mod, loss_target, weights, p, grads_ready)
    del big

    recv, out = {}, {}

    def arrive(n, after):
        keys, (send_sems, recv_sems, srcs, lands, _), _ = started[n]
        srcs, lands = _push_wait(send_sems, recv_sems, srcs, lands, after, f"push_wait_{n}")
        for k, src, land in zip(keys, srcs, lands):
            recv[k] = (land, lax.dynamic_index_in_dim(src, me, 0, keepdims=False))

    big_of = {"ffn1_w_in": "ffn1_in", "ffn1_w_out": "ffn1_out", "ffn2_w_in": "ffn2_in", "ffn2_w_out": "ffn2_out",
              "mix_w_in": "mix_in", "mix_w_out": "mix_out"}
    chain = {name: None for name in big_of}

    def big_update(key, l):
        name = next(nm for nm, k in big_of.items() if k == key)
        parts, own = recv[(key, l)]
        if key == "mix_in":
            parts = _unpack_cols(parts, mix_src, MIX_ORIG_W)
            own = _unpack_cols(own, mix_src, MIX_ORIG_W)
        chain[name] = _adamw(parts, own, w[name], m[name], v[name], f"adamw_{name}_l{l}", layer=l,
                             prev=chain[name])

    def update(name, grad):
        shape = w[name].shape
        as3 = lambda a: a.reshape(1, -1, shape[-1])
        res = _adamw(as3(grad), None, as3(w[name]), as3(m[name]), as3(v[name]), f"adamw_{name}")
        out[name] = tuple(r.reshape(shape) for r in res)

    for n, (keys, _, last) in enumerate(started):
        if not last:
            arrive(n, grad_x)
            for key, l in keys:
                big_update(key, l)

    dmod_all, = _all_gather([dmod.reshape(DEPTH, bsz, N_MOD * d)], "gather_dmod")
    dmod_full = jnp.moveaxis(dmod_all, 0, 1).reshape(DEPTH, N_DEV * bsz, N_MOD * d)
    cols = ada_w.shape[2]
    dmod_cols = lax.dynamic_slice_in_dim(dmod_full, me * cols, cols, axis=2)
    g_ada_w, g_ada_b = _ada_bwd(c_all, dmod_cols, dmod_full, "ada_bwd")
    res = None
    for l in range(DEPTH):
        res = _adamw(g_ada_w[l][None], None, ada_w, m_ada_w, v_ada_w, f"adamw_ada_w_l{l}", layer=l, prev=res)
    out["ada_w"] = tuple(res)
    update("ada_b", g_ada_b.reshape(DEPTH, N_MOD * d))

    items = _small_grad_list(small, loss)
    parts, = _all_gather([_pack_small(items)], "gather_small")
    sg = _unpack_small(_sum_parts(parts, "sum_small"), items)
    for name in ("ln_g", "ln_b"):
        update(name, lax.dynamic_slice_in_dim(sg[name], me * (d // N_DEV), d // N_DEV, axis=2))
    for name, width in (("mla_w_uq", 48), ("mla_w_ukv", 64)):
        update(name, lax.dynamic_slice_in_dim(sg[name], me * width, width, axis=2))
    for name in ("hgrn_lb_logits", "hgrn_norm_g", "mla_q_norm_g", "mla_kv_norm_g", "fox_b_f", "gmlp_ln_g",
                 "gmlp_ln_b", "gmlp_w_s", "gmlp_b_s"):
        update(name, sg[name])

    for n, (keys, _, last) in enumerate(started):
        if last:
            arrive(n, out["gmlp_w_s"][0])
            for key, l in keys:
                big_update(key, l)
    for name in big_of:
        out[name] = tuple(chain[name])

    return (sg["loss"][0], grad_x, *[out[n][0] for n in names], *[out[n][1] for n in names],
            *[out[n][2] for n in names], *[out[n][3] for n in names])
```

```python
import functools

import numpy as np
import jax
import jax.numpy as jnp
from jax import lax
from jax.experimental import pallas as pl
from jax.experimental.pallas import tpu as pltpu

F32 = jnp.float32
BF16 = jnp.bfloat16
HI = lax.Precision.HIGHEST

D_MODEL = 1024
DEPTH = 2
GROUP_WIDTH = 256
N_HEADS = 4
HEAD_DIM = 64
A_CHUNK = 16
LB_FLOOR = 1e-30
B_NOPE = 64
B_ROPE = 32
ROPE_THETA = 10000.0
D_CHUNK = 128
D_FF = 2816
N_MOD = 9
ALPHA = (2 * DEPTH) ** 0.25
LN_EPS = 1e-5
RMS_EPS = 1e-6
ADAM_LR = 0.001
ADAM_B1 = 0.9
ADAM_B2 = 0.999
ADAM_EPS = 1e-08
ADAM_WD = 0.01
ADAM_STEP = 10

N_DEV = 8
LANES = 128
PACK_W = 3712
MO_W = 1536
VMEM_LIMIT = 56 * 1024 * 1024
NEG = -1e30
ATTN_TILE = 512

MIX_ORIG_W = 2724
O_BCQ, O_BCKV, O_BKR, O_CQ, O_CK, O_CV, O_CF, O_DU, O_DV = 1024, 1280, 1408, 1440, 1696, 1952, 2208, 2212, 2468
P_B, P_KR, P_CQ, P_CKV, P_D, P_CF = 1024, 1408, 1536, 2048, 3072, 3584


_DN = {"nn": (((1,), (0,)), ((), ())), "nt": (((1,), (1,)), ((), ())), "tn": (((0,), (0,)), ((), ()))}


def _raw_bdot(a, b, mode):
    return lax.dot_general(a.astype(BF16), b.astype(BF16), _DN[mode], preferred_element_type=F32)


@functools.partial(jax.custom_vjp, nondiff_argnums=(2,))
def _bdot(a, b, mode):
    return _raw_bdot(a, b, mode)


def _bdot_fwd(a, b, mode):
    return _raw_bdot(a, b, mode), (a, b)


def _bdot_bwd(mode, res, g):
    a, b = res
    if mode == "nn":
        return _raw_bdot(g, b, "nt"), _raw_bdot(a, g, "tn")
    if mode == "nt":
        return _raw_bdot(g, b, "nn"), _raw_bdot(g, a, "tn")
    return _raw_bdot(b, g, "nt"), _raw_bdot(a, g, "nn")


_bdot.defvjp(_bdot_fwd, _bdot_bwd)


def _cparams(sem):
    return pltpu.CompilerParams(dimension_semantics=sem, vmem_limit_bytes=VMEM_LIMIT)


def _mix_in_src():
    src = -np.ones(PACK_W, np.int64)
    src[0:P_KR] = np.arange(0, O_BKR)
    src[P_KR + 64:P_KR + 80] = O_BKR + np.arange(16)
    src[P_KR + 96:P_KR + 112] = O_BKR + 16 + np.arange(16)
    for h in range(N_HEADS):
        src[P_CQ + 128 * h:P_CQ + 128 * h + 64] = O_CQ + 64 * h + np.arange(64)
        src[P_CKV + 256 * h:P_CKV + 256 * h + 64] = O_CK + 64 * h + np.arange(64)
        src[P_CKV + 256 * h + 128:P_CKV + 256 * h + 192] = O_CV + 64 * h + np.arange(64)
    src[P_D:P_D + 512] = O_DU + np.arange(512)
    src[P_CF:P_CF + 4] = O_CF + np.arange(4)
    return src


def _uq_src():
    src = -np.ones(512, np.int64)
    for h in range(N_HEADS):
        src[128 * h:128 * h + 64] = 96 * h + np.arange(64)
        src[128 * h + 64:128 * h + 80] = 96 * h + 64 + np.arange(16)
        src[128 * h + 96:128 * h + 112] = 96 * h + 80 + np.arange(16)
    return src


def _ukv_src():
    src = -np.ones(1024, np.int64)
    for h in range(N_HEADS):
        src[256 * h:256 * h + 64] = 128 * h + np.arange(64)
        src[256 * h + 128:256 * h + 192] = 128 * h + 64 + np.arange(64)
    return src


def _mo_src():
    src = -np.ones(MO_W, np.int64)
    src[0:256] = np.arange(256)
    for g in range(2):
        for h in range(N_HEADS):
            src[256 + 512 * g + 128 * h:256 + 512 * g + 128 * h + 64] = 256 + 256 * g + 64 * h + np.arange(64)
    src[1280:1536] = 768 + np.arange(256)
    return src


def _runs(idx):
    runs, i = [], 0
    while i < len(idx):
        j = i + 1
        while j < len(idx) and ((idx[i] < 0 and idx[j] < 0) or (idx[i] >= 0 and idx[j] == idx[i] + j - i)):
            j += 1
        runs.append((int(idx[i]), j - i))
        i = j
    return runs


def _take_runs(w, idx):
    parts = [jnp.zeros(w.shape[:-1] + (n,), w.dtype) if s < 0 else lax.slice_in_dim(w, s, s + n, axis=w.ndim - 1)
             for s, n in _runs(idx)]
    return jnp.concatenate(parts, axis=-1)


def _pack_cols(w, src):
    return _take_runs(w, src)


def _unpack_cols(wp, src, n):
    dst = np.zeros(n, np.int64)
    dst[src[src >= 0]] = np.nonzero(src >= 0)[0]
    return _take_runs(wp, dst)


def _rope_tables(seq):
    half = B_ROPE // 2
    inv_freq = ROPE_THETA ** (-jnp.arange(half, dtype=F32) / half)
    ang = jnp.arange(seq).astype(F32)[:, None] * inv_freq[None, :]
    cos, sin = jnp.cos(ang), jnp.sin(ang)
    z16 = jnp.zeros((seq, 16), F32)
    c = jnp.concatenate([jnp.ones((seq, 64), F32), cos, z16, cos, z16], axis=1)
    s1 = jnp.concatenate([jnp.zeros((seq, 64), F32), -sin, z16, z16, z16], axis=1)
    s2 = jnp.concatenate([jnp.zeros((seq, 64), F32), z16, z16, sin, z16], axis=1)
    return c, s1, s2


def _matmul(a, b, *, mode, group_out, out_dtype, tm, tk, name):
    ga, gb = a.shape[0], b.shape[0]
    g_n = max(ga, gb)
    if mode == "tn":
        k_dim, m_dim = a.shape[1:]
    else:
        m_dim, k_dim = a.shape[1:]
    n_dim = b.shape[1] if mode == "nt" else b.shape[2]
    assert m_dim % tm == 0 and k_dim % tk == 0
    kt = k_dim // tk
    n_red = kt if group_out else g_n * kt
    g_out = g_n if group_out else 1

    def split(g, r):
        return (g, r) if group_out else (r // kt, r % kt)

    def a_map(g, i, r):
        gg, kk = split(g, r)
        gg = gg if ga > 1 else 0
        return (gg, kk, i) if mode == "tn" else (gg, i, kk)

    def b_map(g, i, r):
        gg, kk = split(g, r)
        gg = gg if gb > 1 else 0
        return (gg, 0, kk) if mode == "nt" else (gg, kk, 0)

    a_blk = (None, tk, tm) if mode == "tn" else (None, tm, tk)
    b_blk = (None, n_dim, tk) if mode == "nt" else (None, tk, n_dim)
    dn = _DN[mode]

    def body(a_ref, b_ref, o_ref, *scratch):
        part = lax.dot_general(a_ref[...].astype(BF16), b_ref[...].astype(BF16), dn, preferred_element_type=F32)
        if n_red == 1:
            o_ref[...] = part.astype(o_ref.dtype)
            return
        acc_ref, = scratch
        r = pl.program_id(2)

        @pl.when(r == 0)
        def _():
            acc_ref[...] = part

        @pl.when(r > 0)
        def _():
            acc_ref[...] += part

        @pl.when(r == n_red - 1)
        def _():
            o_ref[...] = acc_ref[...].astype(o_ref.dtype)

    return pl.pallas_call(
        body, name=name, grid=(g_out, m_dim // tm, n_red),
        in_specs=[pl.BlockSpec(a_blk, a_map), pl.BlockSpec(b_blk, b_map)],
        out_specs=pl.BlockSpec((None, tm, n_dim), lambda g, i, r: (g, i, 0)),
        out_shape=jax.ShapeDtypeStruct((g_out, m_dim, n_dim), out_dtype),
        scratch_shapes=[] if n_red == 1 else [pltpu.VMEM((tm, n_dim), F32)],
        compiler_params=_cparams(("parallel", "parallel", "arbitrary")),
    )(a, b)


def _row_spec(ts, d):
    return pl.BlockSpec((None, ts, d), lambda b, s: (b, s, 0))


def _mod_spec(d):
    return pl.BlockSpec((None, N_MOD, d), lambda b, s: (b, 0, 0))


def _vec_spec(d):
    return pl.BlockSpec((1, d), lambda b, s: (0, 0))


def _bvec_spec(d):
    return pl.BlockSpec((None, 1, d), lambda b, s: (b, 0, 0))


def _modulate(x, mod, sh_row, sc_row, name, ts=512):
    bsz, seq, d = x.shape

    def body(x_ref, mod_ref, o_ref):
        sh = mod_ref[sh_row:sh_row + 1, :]
        sc = mod_ref[sc_row:sc_row + 1, :]
        o_ref[...] = (x_ref[...] * (1.0 + sc) + sh).astype(o_ref.dtype)

    return pl.pallas_call(
        body, name=name, grid=(bsz, seq // ts),
        in_specs=[_row_spec(ts, d), _mod_spec(d)], out_specs=_row_spec(ts, d),
        out_shape=jax.ShapeDtypeStruct((bsz, seq, d), BF16),
        compiler_params=_cparams(("parallel", "parallel")),
    )(x, mod)


def _modulate_bwd(dh, x, mod, dx_res, sc_row, name, ts=512):
    bsz, seq, d = x.shape

    def body(dh_ref, x_ref, mod_ref, dxr_ref, dx_ref, dsh_ref, dsc_ref):
        s = pl.program_id(1)
        sc = mod_ref[sc_row:sc_row + 1, :]
        dh_v = dh_ref[...]
        dx_ref[...] = dxr_ref[...] + dh_v * (1.0 + sc)
        psh = jnp.sum(dh_v, axis=0, keepdims=True)
        psc = jnp.sum(dh_v * x_ref[...], axis=0, keepdims=True)

        @pl.when(s == 0)
        def _():
            dsh_ref[...] = psh
            dsc_ref[...] = psc

        @pl.when(s > 0)
        def _():
            dsh_ref[...] += psh
            dsc_ref[...] += psc

    return pl.pallas_call(
        body, name=name, grid=(bsz, seq // ts),
        in_specs=[_row_spec(ts, d), _row_spec(ts, d), _mod_spec(d), _row_spec(ts, d)],
        out_specs=[_row_spec(ts, d), _bvec_spec(d), _bvec_spec(d)],
        out_shape=[jax.ShapeDtypeStruct((bsz, seq, d), F32), jax.ShapeDtypeStruct((bsz, 1, d), F32),
                   jax.ShapeDtypeStruct((bsz, 1, d), F32)],
        compiler_params=_cparams(("parallel", "arbitrary")),
    )(dh, x, mod, dx_res)


def _res_ln_fn(x, f, g, lng, lnb, cmul):
    r = ALPHA * x + (cmul * (1.0 + g)) * f
    mu = jnp.mean(r, axis=-1, keepdims=True)
    rc = r - mu
    var = jnp.mean(rc * rc, axis=-1, keepdims=True)
    return rc * lax.rsqrt(var + LN_EPS) * lng + lnb


def _res_ln(x, f, mod, lng, lnb, g_row, cmul, name, nxt=None, ts=512):
    bsz, seq, d = x.shape

    def body(*refs):
        x_ref, f_ref, mod_ref, lng_ref, lnb_ref = refs[:5]
        g = mod_ref[g_row:g_row + 1, :]
        y = _res_ln_fn(x_ref[...], f_ref[...], g, lng_ref[...], lnb_ref[...], cmul)
        if nxt is None:
            refs[5][...] = y
            return
        nmod_ref, o_ref, h_ref = refs[5:]
        o_ref[...] = y
        sh = nmod_ref[nxt[1]:nxt[1] + 1, :]
        sc = nmod_ref[nxt[2]:nxt[2] + 1, :]
        h_ref[...] = (y * (1.0 + sc) + sh).astype(h_ref.dtype)

    in_specs = [_row_spec(ts, d), _row_spec(ts, d), _mod_spec(d), _vec_spec(d), _vec_spec(d)]
    args = [x, f, mod, lng, lnb]
    out_specs, out_shape = [_row_spec(ts, d)], [jax.ShapeDtypeStruct((bsz, seq, d), F32)]
    if nxt is not None:
        in_specs.append(_mod_spec(d))
        args.append(nxt[0])
        out_specs.append(_row_spec(ts, d))
        out_shape.append(jax.ShapeDtypeStruct((bsz, seq, d), BF16))
    res = pl.pallas_call(
        body, name=name, grid=(bsz, seq // ts), in_specs=in_specs, out_specs=out_specs, out_shape=out_shape,
        compiler_params=_cparams(("parallel", "parallel")),
    )(*args)
    return (res[0], res[1]) if nxt is not None else (res[0], None)


def _res_ln_bwd(dy, x, f, mod, lng, lnb, g_row, cmul, name, ts=256):
    bsz, seq, d = x.shape

    def body(dy_ref, x_ref, f_ref, mod_ref, lng_ref, lnb_ref, dx_ref, df_ref, dg_ref, dlg_ref, dlb_ref):
        b, s = pl.program_id(0), pl.program_id(1)
        g = mod_ref[g_row:g_row + 1, :]
        _, vjp = jax.vjp(functools.partial(_res_ln_fn, cmul=cmul), x_ref[...], f_ref[...], g, lng_ref[...],
                         lnb_ref[...])
        dx, df, dg, dlg, dlb = vjp(dy_ref[...])
        dx_ref[...] = dx
        df_ref[...] = df.astype(df_ref.dtype)

        @pl.when(s == 0)
        def _():
            dg_ref[...] = dg

        @pl.when(s > 0)
        def _():
            dg_ref[...] += dg

        first = jnp.logical_and(b == 0, s == 0)

        @pl.when(first)
        def _():
            dlg_ref[...] = dlg
            dlb_ref[...] = dlb

        @pl.when(jnp.logical_not(first))
        def _():
            dlg_ref[...] += dlg
            dlb_ref[...] += dlb

    return pl.pallas_call(
        body, name=name, grid=(bsz, seq // ts),
        in_specs=[_row_spec(ts, d), _row_spec(ts, d), _row_spec(ts, d), _mod_spec(d), _vec_spec(d), _vec_spec(d)],
        out_specs=[_row_spec(ts, d), _row_spec(ts, d), _bvec_spec(d), _vec_spec(d), _vec_spec(d)],
        out_shape=[jax.ShapeDtypeStruct((bsz, seq, d), F32), jax.ShapeDtypeStruct((bsz, seq, d), BF16),
                   jax.ShapeDtypeStruct((bsz, 1, d), F32), jax.ShapeDtypeStruct((1, d), F32),
                   jax.ShapeDtypeStruct((1, d), F32)],
        compiler_params=_cparams(("arbitrary", "arbitrary")),
    )(dy, x, f, mod, lng, lnb)


def _loss_head(y, target, name, ts=512):
    bsz, seq, d = y.shape
    n_s = seq // ts

    def body(y_ref, t_ref, dy_ref, loss_ref, acc_ref):
        b, s = pl.program_id(0), pl.program_id(1)
        err = y_ref[...] - t_ref[...]
        dy_ref[...] = err * (1.0 / d)
        part = jnp.sum(err * err, axis=0, keepdims=True)
        first = jnp.logical_and(b == 0, s == 0)

        @pl.when(first)
        def _():
            acc_ref[...] = part

        @pl.when(jnp.logical_not(first))
        def _():
            acc_ref[...] += part

        @pl.when(jnp.logical_and(b == bsz - 1, s == n_s - 1))
        def _():
            loss_ref[...] = jnp.sum(acc_ref[...], axis=1, keepdims=True) * (0.5 / d)

    return pl.pallas_call(
        body, name=name, grid=(bsz, n_s),
        in_specs=[_row_spec(ts, d), _row_spec(ts, d)],
        out_specs=[_row_spec(ts, d), pl.BlockSpec((1, 1), lambda b, s: (0, 0))],
        out_shape=[jax.ShapeDtypeStruct((bsz, seq, d), F32), jax.ShapeDtypeStruct((1, 1), F32)],
        scratch_shapes=[pltpu.VMEM((1, d), F32)],
        compiler_params=_cparams(("arbitrary", "arbitrary")),
    )(y, target)


def _ffn_in_swiglu(h, w_in_t, name, tm=1024):
    t, d = h.shape
    n_sh, w, _ = w_in_t.shape
    half = n_sh // 2

    def body(h_ref, w_ref, z_ref, a_ref):
        hv = h_ref[...]
        g = lax.dot_general(hv, w_ref[0], _DN["nt"], preferred_element_type=F32)
        u = lax.dot_general(hv, w_ref[1], _DN["nt"], preferred_element_type=F32)
        z_ref[0] = g.astype(z_ref.dtype)
        z_ref[1] = u.astype(z_ref.dtype)
        a_ref[...] = (g * jax.nn.sigmoid(g) * u).astype(a_ref.dtype)

    return pl.pallas_call(
        body, name=name, grid=(half, t // tm),
        in_specs=[pl.BlockSpec((tm, d), lambda g, i: (i, 0)),
                  pl.BlockSpec((2, None, w, d), lambda g, i: (0, g, 0, 0))],
        out_specs=[pl.BlockSpec((2, None, tm, w), lambda g, i: (0, g, i, 0)),
                   pl.BlockSpec((None, tm, w), lambda g, i: (g, i, 0))],
        out_shape=[jax.ShapeDtypeStruct((2, half, t, w), BF16), jax.ShapeDtypeStruct((half, t, w), BF16)],
        compiler_params=_cparams(("parallel", "parallel")),
    )(h, w_in_t.reshape(2, half, w, d))


def _ffn_out_dx_swiglu(df, w_out, z, name, tm=1024):
    t, d = df.shape
    half, w, _ = w_out.shape

    def body(df_ref, w_ref, z_ref, dz_ref):
        da = lax.dot_general(df_ref[...], w_ref[...], _DN["nt"], preferred_element_type=F32)
        g = z_ref[0].astype(F32)
        u = z_ref[1].astype(F32)
        sig = jax.nn.sigmoid(g)
        dz_ref[0] = (da * u * (sig * (1.0 + g * (1.0 - sig)))).astype(dz_ref.dtype)
        dz_ref[1] = (da * (g * sig)).astype(dz_ref.dtype)

    zspec = pl.BlockSpec((2, None, tm, w), lambda g, i: (0, g, i, 0))
    return pl.pallas_call(
        body, name=name, grid=(half, t // tm),
        in_specs=[pl.BlockSpec((tm, d), lambda g, i: (i, 0)), pl.BlockSpec((None, w, d), lambda g, i: (g, 0, 0)),
                  zspec],
        out_specs=zspec, out_shape=jax.ShapeDtypeStruct(z.shape, BF16),
        compiler_params=_cparams(("parallel", "parallel")),
    )(df, w_out, z)


def _log_sigmoid(x):
    return jnp.minimum(x, 0.0) - jnp.log(1.0 + jnp.exp(-jnp.abs(x)))


def _hgrn_consts():
    r = lax.broadcasted_iota(jnp.int32, (GROUP_WIDTH, GROUP_WIDTH), 0)
    c = lax.broadcasted_iota(jnp.int32, (GROUP_WIDTH, GROUP_WIDTH), 1)
    bd = (r // HEAD_DIM == c // HEAD_DIM).astype(F32)
    r16 = lax.broadcasted_iota(jnp.int32, (A_CHUNK, A_CHUNK), 0)
    c16 = lax.broadcasted_iota(jnp.int32, (A_CHUNK, A_CHUNK), 1)
    tril = (r16 >= c16).astype(F32)
    rows = lax.broadcasted_iota(jnp.int32, (A_CHUNK, GROUP_WIDTH), 0)
    return bd, tril, rows


def _hgrn_lb(logits8, layer):
    rows = lax.broadcasted_iota(jnp.int32, logits8.shape, 0)
    valid = rows < DEPTH
    mx = jnp.max(jnp.where(valid, logits8, NEG), axis=0, keepdims=True)
    e = jnp.where(valid, jnp.exp(logits8 - mx), 0.0)
    sm = e / jnp.sum(e, axis=0, keepdims=True)
    pick = jnp.logical_and(rows >= 1, rows <= layer)
    return jnp.sum(jnp.where(pick, sm, 0.0), axis=0, keepdims=True)


def _hgrn_chunk(aq, af, ai, ag, logits8, norm_g, st, *, layer, consts):
    bd, tril, rows = consts
    lb = _hgrn_lb(logits8, layer)
    la = jnp.log(jnp.maximum(lb, LB_FLOOR))
    b2 = jnp.log(1.0 - lb) + _log_sigmoid(af)
    log_f = jnp.maximum(la, b2) + jnp.log(1.0 + jnp.exp(-jnp.abs(la - b2)))
    k = 1.0 - jnp.exp(log_f)
    qf = aq * jax.nn.sigmoid(aq)
    g_cum = jnp.dot(tril, log_f, precision=HI, preferred_element_type=F32)

    c, w = A_CHUNK, GROUP_WIDTH

    def by_key(v):
        return jnp.broadcast_to(v[:, None, :], (c, c, w))

    def by_query(v):
        return jnp.broadcast_to(v[None, :, :], (c, c, w))

    s_i = lax.broadcasted_iota(jnp.int32, (c, c, w), 0)
    t_i = lax.broadcasted_iota(jnp.int32, (c, c, w), 1)
    rel = jnp.where(t_i >= s_i, by_query(g_cum) - by_key(g_cum), NEG)
    pairs = by_query(qf) * by_key(k) * jnp.exp(rel)
    a_all = _bdot(pairs.reshape(c * c, w), bd, "nn").reshape(c, c, w)
    o = jnp.sum(a_all * by_key(ai), axis=0)
    q_dec = qf * jnp.exp(g_cum)
    o = o + _bdot(q_dec, st, "nt")
    g_last = jnp.sum(jnp.where(rows == c - 1, g_cum, 0.0), axis=0, keepdims=True)
    k_end = k * jnp.exp(g_last - g_cum)
    kv = _bdot(ai, k_end, "tn")
    st_new = st * jnp.exp(g_last) + kv * bd
    ms = _bdot(o * o, bd, "nn") * (1.0 / HEAD_DIM)
    o = o * lax.rsqrt(ms + RMS_EPS) * norm_g
    return o * (ag * jax.nn.sigmoid(ag)), st_new


def _hgrn_fwd(proj, logits8, norm_g, layer, name, ts=128):
    bsz, seq, _ = proj.shape
    n_ch = ts // A_CHUNK

    def body(p_ref, lg_ref, ng_ref, o_ref, st_ref, st_scr):
        @pl.when(pl.program_id(1) == 0)
        def _():
            st_scr[...] = jnp.zeros_like(st_scr)

        consts = _hgrn_consts()
        logits_v, ng_v = lg_ref[...], ng_ref[...]

        def chunk(ci, carry):
            r = pl.multiple_of(ci * A_CHUNK, A_CHUNK)
            st = st_scr[...]
            st_ref[ci] = st
            o, st_new = _hgrn_chunk(
                p_ref[pl.ds(r, A_CHUNK), 0:256], p_ref[pl.ds(r, A_CHUNK), 256:512],
                p_ref[pl.ds(r, A_CHUNK), 512:768], p_ref[pl.ds(r, A_CHUNK), 768:1024],
                logits_v, ng_v, st, layer=layer, consts=consts)
            o_ref[pl.ds(r, A_CHUNK), :] = o.astype(o_ref.dtype)
            st_scr[...] = st_new
            return carry

        lax.fori_loop(0, n_ch, chunk, 0, unroll=2)

    return pl.pallas_call(
        body, name=name, grid=(bsz, seq // ts),
        in_specs=[pl.BlockSpec((None, ts, 1024), lambda b, s: (b, s, 0)),
                  pl.BlockSpec((8, GROUP_WIDTH), lambda b, s: (0, 0)),
                  pl.BlockSpec((1, GROUP_WIDTH), lambda b, s: (0, 0))],
        out_specs=[pl.BlockSpec((None, ts, GROUP_WIDTH), lambda b, s: (b, s, 0)),
                   pl.BlockSpec((None, n_ch, GROUP_WIDTH, GROUP_WIDTH), lambda b, s: (b, s, 0, 0))],
        out_shape=[jax.ShapeDtypeStruct((bsz, seq, MO_W), BF16),
                   jax.ShapeDtypeStruct((bsz, seq // A_CHUNK, GROUP_WIDTH, GROUP_WIDTH), F32)],
        scratch_shapes=[pltpu.VMEM((GROUP_WIDTH, GROUP_WIDTH), F32)],
        compiler_params=_cparams(("parallel", "arbitrary")),
    )(proj, logits8, norm_g)


def _hgrn_bwd(dmo, proj, states, logits8, norm_g, layer, name, ts=128):
    bsz, seq, _ = proj.shape
    n_ch = ts // A_CHUNK
    n_s = seq // ts

    def body(do_ref, p_ref, st_ref, lg_ref, ng_ref, dp_ref, dlg_ref, dng_ref, dst_scr):
        b, s = pl.program_id(0), pl.program_id(1)

        @pl.when(s == 0)
        def _():
            dst_scr[...] = jnp.zeros_like(dst_scr)

        @pl.when(jnp.logical_and(b == 0, s == 0))
        def _():
            dlg_ref[...] = jnp.zeros_like(dlg_ref)
            dng_ref[...] = jnp.zeros_like(dng_ref)

        consts = _hgrn_consts()
        logits_v, ng_v = lg_ref[...], ng_ref[...]
        fn = functools.partial(_hgrn_chunk, layer=layer, consts=consts)

        def chunk(t, carry):
            ci = n_ch - 1 - t
            r = pl.multiple_of(ci * A_CHUNK, A_CHUNK)
            _, vjp = jax.vjp(
                fn, p_ref[pl.ds(r, A_CHUNK), 0:256], p_ref[pl.ds(r, A_CHUNK), 256:512],
                p_ref[pl.ds(r, A_CHUNK), 512:768], p_ref[pl.ds(r, A_CHUNK), 768:1024],
                logits_v, ng_v, st_ref[ci])
            daq, daf, dai, dag, dlg, dng, dst = vjp((do_ref[pl.ds(r, A_CHUNK), :], dst_scr[...]))
            dp_ref[pl.ds(r, A_CHUNK), 0:256] = daq.astype(dp_ref.dtype)
            dp_ref[pl.ds(r, A_CHUNK), 256:512] = daf.astype(dp_ref.dtype)
            dp_ref[pl.ds(r, A_CHUNK), 512:768] = dai.astype(dp_ref.dtype)
            dp_ref[pl.ds(r, A_CHUNK), 768:1024] = dag.astype(dp_ref.dtype)
            dlg_ref[...] += dlg
            dng_ref[...] += dng
            dst_scr[...] = dst
            return carry

        lax.fori_loop(0, n_ch, chunk, 0, unroll=2)

    rev = lambda b, s: (b, n_s - 1 - s, 0)
    return pl.pallas_call(
        body, name=name, grid=(bsz, n_s),
        in_specs=[pl.BlockSpec((None, ts, GROUP_WIDTH), rev),
                  pl.BlockSpec((None, ts, 1024), rev),
                  pl.BlockSpec((None, n_ch, GROUP_WIDTH, GROUP_WIDTH), lambda b, s: (b, n_s - 1 - s, 0, 0)),
                  pl.BlockSpec((8, GROUP_WIDTH), lambda b, s: (0, 0)),
                  pl.BlockSpec((1, GROUP_WIDTH), lambda b, s: (0, 0))],
        out_specs=[pl.BlockSpec((None, ts, 1024), rev),
                   pl.BlockSpec((8, GROUP_WIDTH), lambda b, s: (0, 0)),
                   pl.BlockSpec((1, GROUP_WIDTH), lambda b, s: (0, 0))],
        out_shape=[jax.ShapeDtypeStruct((bsz, seq, PACK_W), BF16),
                   jax.ShapeDtypeStruct((8, GROUP_WIDTH), F32), jax.ShapeDtypeStruct((1, GROUP_WIDTH), F32)],
        scratch_shapes=[pltpu.VMEM((GROUP_WIDTH, GROUP_WIDTH), F32)],
        compiler_params=_cparams(("arbitrary", "arbitrary")),
    )(dmo, proj, states, logits8, norm_g)


def _rms_fn(x, g):
    return x * lax.rsqrt(jnp.mean(x * x, axis=-1, keepdims=True) + RMS_EPS) * g


def _tile4(t):
    return jnp.concatenate([t, t, t, t], axis=1)


def _rope(x, c, s1, s2):
    w = x.shape[-1]
    return x * c + pltpu.roll(x, 32, axis=1) * s2 + pltpu.roll(x, w - 32, axis=1) * s1


def _rope_t(dy, c, s1, s2):
    w = dy.shape[-1]
    return dy * c + pltpu.roll(dy * s2, w - 32, axis=1) + pltpu.roll(dy * s1, 32, axis=1)


def _mla_pre(proj, qg, kvg, wq, wkv, tabs, name, ts=256):
    bsz, seq, _ = proj.shape

    def body(p_ref, qg_ref, kvg_ref, wq_ref, wkv_ref, c_ref, s1_ref, s2_ref, q_ref, kv_ref):
        nq = _rms_fn(p_ref[:, 0:256], qg_ref[...])
        nkv = _rms_fn(p_ref[:, 256:384], kvg_ref[...])
        c, s1, s2 = c_ref[...], s1_ref[...], s2_ref[...]
        qp = jnp.dot(nq.astype(BF16), wq_ref[...], preferred_element_type=F32)
        q_ref[...] = _rope(qp, _tile4(c), _tile4(s1), _tile4(s2)).astype(q_ref.dtype)
        kv = jnp.dot(nkv.astype(BF16), wkv_ref[...], preferred_element_type=F32)
        krr = _rope(p_ref[:, 384:512], c, s1, s2)
        zero = jnp.zeros_like(krr)
        kv_ref[...] = (kv + jnp.concatenate([krr, zero] * N_HEADS, axis=1)).astype(kv_ref.dtype)

    tab_spec = pl.BlockSpec((ts, LANES), lambda b, s: (s, 0))
    return pl.pallas_call(
        body, name=name, grid=(bsz, seq // ts),
        in_specs=[pl.BlockSpec((None, ts, 512), lambda b, s: (b, s, P_B // 512)),
                  _vec_spec(256), _vec_spec(128),
                  pl.BlockSpec((256, 512), lambda b, s: (0, 0)), pl.BlockSpec((128, 1024), lambda b, s: (0, 0)),
                  tab_spec, tab_spec, tab_spec],
        out_specs=[_row_spec(ts, 512), _row_spec(ts, 1024)],
        out_shape=[jax.ShapeDtypeStruct((bsz, seq, 512), BF16), jax.ShapeDtypeStruct((bsz, seq, 1024), BF16)],
        compiler_params=_cparams(("parallel", "parallel")),
    )(proj, qg, kvg, wq, wkv, *tabs)


def _mla_pre_bwd(dq, dkv, dproj, proj, qg, kvg, wq, wkv, tabs, name, ts=256):
    bsz, seq, _ = proj.shape

    def body(dq_ref, dkv_ref, dp_any, p_ref, qg_ref, kvg_ref, wq_ref, wkv_ref, c_ref, s1_ref, s2_ref,
             dp_ref, dqg_ref, dkvg_ref, dwq_ref, dwkv_ref):
        del dp_any
        first = jnp.logical_and(pl.program_id(0) == 0, pl.program_id(1) == 0)

        @pl.when(first)
        def _():
            dqg_ref[...] = jnp.zeros_like(dqg_ref)
            dkvg_ref[...] = jnp.zeros_like(dkvg_ref)
            dwq_ref[...] = jnp.zeros_like(dwq_ref)
            dwkv_ref[...] = jnp.zeros_like(dwkv_ref)

        c, s1, s2 = c_ref[...], s1_ref[...], s2_ref[...]
        nq, vjp_q = jax.vjp(_rms_fn, p_ref[:, 0:256], qg_ref[...])
        nkv, vjp_kv = jax.vjp(_rms_fn, p_ref[:, 256:384], kvg_ref[...])
        dqp = _rope_t(dq_ref[...], _tile4(c), _tile4(s1), _tile4(s2)).astype(BF16)
        dkv_v = dkv_ref[...]
        dkv_b = dkv_v.astype(BF16)
        tn = (((0,), (0,)), ((), ()))
        nt = (((1,), (1,)), ((), ()))
        dwq_ref[...] += lax.dot_general(nq.astype(BF16), dqp, tn, preferred_element_type=F32)
        dwkv_ref[...] += lax.dot_general(nkv.astype(BF16), dkv_b, tn, preferred_element_type=F32)
        dcq, dqg = vjp_q(lax.dot_general(dqp, wq_ref[...], nt, preferred_element_type=F32))
        dckv, dkvg = vjp_kv(lax.dot_general(dkv_b, wkv_ref[...], nt, preferred_element_type=F32))
        dqg_ref[...] += dqg
        dkvg_ref[...] += dkvg
        dk_sum = dkv_v[:, 0:128] + dkv_v[:, 256:384] + dkv_v[:, 512:640] + dkv_v[:, 768:896]
        lane = lax.broadcasted_iota(jnp.int32, dk_sum.shape, 1)
        dkr = jnp.where(lane >= 64, _rope_t(dk_sum, c, s1, s2), 0.0)
        dp_ref[:, 0:256] = dcq.astype(dp_ref.dtype)
        dp_ref[:, 256:384] = dckv.astype(dp_ref.dtype)
        dp_ref[:, 384:512] = dkr.astype(dp_ref.dtype)

    tab_spec = pl.BlockSpec((ts, LANES), lambda b, s: (s, 0))
    const = lambda shape: pl.BlockSpec(shape, lambda b, s: (0, 0))
    return pl.pallas_call(
        body, name=name, grid=(bsz, seq // ts),
        in_specs=[_row_spec(ts, 512), _row_spec(ts, 1024), pl.BlockSpec(memory_space=pl.ANY),
                  pl.BlockSpec((None, ts, 512), lambda b, s: (b, s, P_B // 512)),
                  _vec_spec(256), _vec_spec(128), const((256, 512)), const((128, 1024)),
                  tab_spec, tab_spec, tab_spec],
        out_specs=[pl.BlockSpec((None, ts, 512), lambda b, s: (b, s, P_B // 512)),
                   _vec_spec(256), _vec_spec(128), const((256, 512)), const((128, 1024))],
        out_shape=[jax.ShapeDtypeStruct(dproj.shape, dproj.dtype), jax.ShapeDtypeStruct((1, 256), F32),
                   jax.ShapeDtypeStruct((1, 128), F32), jax.ShapeDtypeStruct((256, 512), F32),
                   jax.ShapeDtypeStruct((128, 1024), F32)],
        input_output_aliases={2: 0},
        compiler_params=_cparams(("arbitrary", "arbitrary")),
    )(dq, dkv, dproj, proj, qg, kvg, wq, wkv, *tabs)


def _fox_gate(proj, bf, name):
    bsz, seq, _ = proj.shape
    n_blk = seq // LANES

    def body(x_ref, bf_ref, f_ref):
        r_i = lax.broadcasted_iota(jnp.int32, (LANES, LANES), 0)
        c_i = lax.broadcasted_iota(jnp.int32, (LANES, LANES), 1)
        tril = (r_i >= c_i).astype(F32)
        bias = bf_ref[...]

        def blk(i, carry):
            r = pl.multiple_of(i * LANES, LANES)
            lf = _log_sigmoid(x_ref[pl.ds(r, LANES), :] + bias)
            f_ref[pl.ds(r, LANES), :] = jnp.dot(tril, lf, precision=HI, preferred_element_type=F32) + carry
            return carry + jnp.sum(lf, axis=0, keepdims=True)

        lax.fori_loop(0, n_blk, blk, jnp.zeros((1, LANES), F32))

    return pl.pallas_call(
        body, name=name, grid=(bsz,),
        in_specs=[pl.BlockSpec((None, seq, LANES), lambda b: (b, 0, P_CF // LANES)),
                  pl.BlockSpec((1, LANES), lambda b: (0, 0))],
        out_specs=pl.BlockSpec((None, seq, LANES), lambda b: (b, 0, 0)),
        out_shape=jax.ShapeDtypeStruct((bsz, seq, LANES), F32),
        compiler_params=_cparams(("parallel",)),
    )(proj, bf)


def _fox_gate_bwd(dfq, dfk_cols, dproj, proj, bf, name):
    bsz, seq, _ = proj.shape
    n_blk = seq // LANES

    def body(dfq_ref, dfk_ref, dp_any, x_ref, bf_ref, dp_ref, dbf_ref):
        del dp_any

        @pl.when(pl.program_id(0) == 0)
        def _():
            dbf_ref[...] = jnp.zeros_like(dbf_ref)

        r_i = lax.broadcasted_iota(jnp.int32, (LANES, LANES), 0)
        c_i = lax.broadcasted_iota(jnp.int32, (LANES, LANES), 1)
        triu = (r_i <= c_i).astype(F32)
        bias = bf_ref[...]

        def blk(t, carry):
            tail, dbf = carry
            r = pl.multiple_of((n_blk - 1 - t) * LANES, LANES)
            dc = dfk_ref[pl.ds(r, LANES), :]
            for hd in range(N_HEADS):
                dc = dc + jnp.where(c_i == hd, dfq_ref[hd, pl.ds(r, LANES), :], 0.0)
            dlf = jnp.dot(triu, dc, precision=HI, preferred_element_type=F32) + tail
            dx = dlf * (1.0 - jax.nn.sigmoid(x_ref[pl.ds(r, LANES), :] + bias))
            dp_ref[pl.ds(r, LANES), :] = dx.astype(dp_ref.dtype)
            return tail + jnp.sum(dc, axis=0, keepdims=True), dbf + jnp.sum(dx, axis=0, keepdims=True)

        z = jnp.zeros((1, LANES), F32)
        _, dbf = lax.fori_loop(0, n_blk, blk, (z, z))
        dbf_ref[...] += dbf

    return pl.pallas_call(
        body, name=name, grid=(bsz,),
        in_specs=[pl.BlockSpec((None, N_HEADS, seq, LANES), lambda b: (b, 0, 0, 0)),
                  pl.BlockSpec((None, seq, LANES), lambda b: (b, 0, 0)), pl.BlockSpec(memory_space=pl.ANY),
                  pl.BlockSpec((None, seq, LANES), lambda b: (b, 0, P_CF // LANES)),
                  pl.BlockSpec((1, LANES), lambda b: (0, 0))],
        out_specs=[pl.BlockSpec((None, seq, LANES), lambda b: (b, 0, P_CF // LANES)),
                   pl.BlockSpec((1, LANES), lambda b: (0, 0))],
        out_shape=[jax.ShapeDtypeStruct(dproj.shape, dproj.dtype), jax.ShapeDtypeStruct((1, LANES), F32)],
        input_output_aliases={2: 0},
        compiler_params=_cparams(("arbitrary",)),
    )(dfq, dfk_cols, dproj, proj, bf)


def _gate_terms(fc_ref, fr_ref, h, tq, tk):
    lane = lax.broadcasted_iota(jnp.int32, (tq, LANES), 1)
    fcol = jnp.sum(jnp.where(lane == h, fc_ref[...], 0.0), axis=1, keepdims=True)
    sub = lax.broadcasted_iota(jnp.int32, (8, tk), 0)
    frow = jnp.sum(jnp.where(sub == h, fr_ref[...], 0.0), axis=0, keepdims=True)
    return fcol - frow


def _scores(q_ref, k_ref, gate_refs, scale, h, masked, tq, tk):
    q = (q_ref[...].astype(F32) * scale).astype(BF16)
    s = lax.dot_general(q, k_ref[...].astype(BF16), _DN["nt"], preferred_element_type=F32)
    if gate_refs is not None:
        s = s + _gate_terms(gate_refs[0], gate_refs[1], h, tq, tk)
    if masked is not False:
        r_i = lax.broadcasted_iota(jnp.int32, (tq, tk), 0)
        c_i = lax.broadcasted_iota(jnp.int32, (tq, tk), 1)
        keep = c_i <= r_i
        s = jnp.where(keep if masked is True else jnp.logical_or(jnp.logical_not(masked), keep), s, NEG)
    return s, q


def _lanes(col):
    return jnp.broadcast_to(col, (col.shape[0], LANES))


def _attn_fwd(qa, q0, kva, kv0, mo, o0, gates, scale, name, tq=None):
    bsz, seq, _ = qa.shape
    tq = ATTN_TILE if tq is None else tq
    n_q = seq // tq
    gated = gates is not None

    def body(*refs):
        q_ref, k_ref, v_ref = refs[:3]
        gate_refs = refs[3:5] if gated else None
        o_ref, lse_ref, m_s, l_s, acc_s = refs[-5:]
        h, i, j = pl.program_id(1), pl.program_id(2), pl.program_id(3)

        @pl.when(j == 0)
        def _():
            m_s[...] = jnp.full_like(m_s, NEG)
            l_s[...] = jnp.zeros_like(l_s)
            acc_s[...] = jnp.zeros_like(acc_s)

        def step(masked):
            s, _ = _scores(q_ref, k_ref, gate_refs, scale, h, masked, tq, tq)
            m_prev = m_s[...]
            m_new = jnp.maximum(m_prev, jnp.max(s, axis=1, keepdims=True))
            alpha = jnp.exp(m_prev - m_new)
            p = jnp.exp(s - m_new)
            l_s[...] = alpha * l_s[...] + jnp.sum(p, axis=1, keepdims=True)
            acc_s[...] = alpha * acc_s[...] + jnp.dot(p.astype(BF16), v_ref[...].astype(BF16),
                                                      preferred_element_type=F32)
            m_s[...] = m_new

        @pl.when(j <= i)
        def _():
            step(j == i)

        @pl.when(j == i)
        def _():
            o_ref[...] = (acc_s[...] / l_s[...]).astype(o_ref.dtype)
            lse_ref[...] = _lanes(m_s[...] + jnp.log(l_s[...]))

    blk = (None, tq, LANES)
    in_specs = [pl.BlockSpec(blk, lambda b, h, i, j: (b, i, q0 + h)),
                pl.BlockSpec(blk, lambda b, h, i, j: (b, jnp.minimum(j, i), kv0 + 2 * h)),
                pl.BlockSpec(blk, lambda b, h, i, j: (b, jnp.minimum(j, i), kv0 + 2 * h + 1))]
    args = [qa, kva, kva]
    if gated:
        in_specs += [pl.BlockSpec(blk, lambda b, h, i, j: (b, i, 0)),
                     pl.BlockSpec((None, 8, tq), lambda b, h, i, j: (b, 0, jnp.minimum(j, i)))]
        args += list(gates)
    in_specs.append(pl.BlockSpec(memory_space=pl.ANY))
    args.append(mo)
    return pl.pallas_call(
        body, name=name, grid=(bsz, N_HEADS, n_q, n_q), in_specs=in_specs,
        out_specs=[pl.BlockSpec(blk, lambda b, h, i, j: (b, i, o0 + h)),
                   pl.BlockSpec((None, None, tq, LANES), lambda b, h, i, j: (b, h, i, 0))],
        out_shape=[jax.ShapeDtypeStruct(mo.shape, mo.dtype),
                   jax.ShapeDtypeStruct((bsz, N_HEADS, seq, LANES), F32)],
        scratch_shapes=[pltpu.VMEM((tq, 1), F32), pltpu.VMEM((tq, 1), F32), pltpu.VMEM((tq, LANES), F32)],
        input_output_aliases={len(args) - 1: 0},
        compiler_params=_cparams(("parallel", "parallel", "parallel", "arbitrary")),
    )(*args)


def _attn_bwd_q(qa, q0, kva, kv0, mo, dmo, o0, lse, gates, scale, out, out0, name, tq=None):
    bsz, seq, _ = qa.shape
    tq = ATTN_TILE if tq is None else tq
    n_q = seq // tq
    gated = gates is not None
    aliased = not isinstance(out, jax.ShapeDtypeStruct)

    def body(*refs):
        q_ref, k_ref, v_ref, o_ref, do_ref, lse_ref = refs[:6]
        gate_refs = refs[6:8] if gated else None
        dq_ref, delta_ref, dfq_ref, acc_s, dl_s, df_s = refs[-6:]
        h, i, j = pl.program_id(1), pl.program_id(2), pl.program_id(3)

        @pl.when(j == 0)
        def _():
            acc_s[...] = jnp.zeros_like(acc_s)
            df_s[...] = jnp.zeros_like(df_s)
            dl_s[...] = jnp.sum(do_ref[...] * o_ref[...].astype(F32), axis=1, keepdims=True)

        def step(masked):
            s, _ = _scores(q_ref, k_ref, gate_refs, scale, h, masked, tq, tq)
            p = jnp.exp(s - lse_ref[:, 0:1])
            dp = lax.dot_general(do_ref[...].astype(BF16), v_ref[...].astype(BF16), _DN["nt"],
                                 preferred_element_type=F32)
            ds = p * (dp - dl_s[...])
            acc_s[...] += jnp.dot(ds.astype(BF16), k_ref[...].astype(BF16), preferred_element_type=F32)
            df_s[...] += jnp.sum(ds, axis=1, keepdims=True)

        @pl.when(j <= i)
        def _():
            step(j == i)

        @pl.when(j == i)
        def _():
            dq_ref[...] = (acc_s[...] * scale).astype(dq_ref.dtype)
            delta_ref[...] = _lanes(dl_s[...])
            dfq_ref[...] = _lanes(df_s[...])

    blk = (None, tq, LANES)
    col = pl.BlockSpec((None, None, tq, LANES), lambda b, h, i, j: (b, h, i, 0))
    in_specs = [pl.BlockSpec(blk, lambda b, h, i, j: (b, i, q0 + h)),
                pl.BlockSpec(blk, lambda b, h, i, j: (b, jnp.minimum(j, i), kv0 + 2 * h)),
                pl.BlockSpec(blk, lambda b, h, i, j: (b, jnp.minimum(j, i), kv0 + 2 * h + 1)),
                pl.BlockSpec(blk, lambda b, h, i, j: (b, i, o0 + h)),
                pl.BlockSpec(blk, lambda b, h, i, j: (b, i, o0 + h)), col]
    args = [qa, kva, kva, mo, dmo, lse]
    if gated:
        in_specs += [pl.BlockSpec(blk, lambda b, h, i, j: (b, i, 0)),
                     pl.BlockSpec((None, 8, tq), lambda b, h, i, j: (b, 0, jnp.minimum(j, i)))]
        args += list(gates)
    aliases = {}
    if aliased:
        in_specs.append(pl.BlockSpec(memory_space=pl.ANY))
        args.append(out)
        aliases = {len(args) - 1: 0}
    vec = jax.ShapeDtypeStruct((bsz, N_HEADS, seq, LANES), F32)
    return pl.pallas_call(
        body, name=name, grid=(bsz, N_HEADS, n_q, n_q), in_specs=in_specs,
        out_specs=[pl.BlockSpec(blk, lambda b, h, i, j: (b, i, out0 + h)), col, col],
        out_shape=[jax.ShapeDtypeStruct(out.shape, out.dtype), vec, vec],
        scratch_shapes=[pltpu.VMEM((tq, LANES), F32), pltpu.VMEM((tq, 1), F32), pltpu.VMEM((tq, 1), F32)],
        input_output_aliases=aliases,
        compiler_params=_cparams(("parallel", "parallel", "parallel", "arbitrary")),
    )(*args)


def _attn_bwd_kv(qa, q0, kva, kv0, dmo, o0, lse, delta, gates, scale, out, out0, name, tq=None):
    bsz, seq, _ = qa.shape
    tq = ATTN_TILE if tq is None else tq
    n_q = seq // tq
    gated = gates is not None
    aliased = not isinstance(out, jax.ShapeDtypeStruct)

    def body(*refs):
        q_ref, k_ref, v_ref, do_ref, lse_ref, dl_ref = refs[:6]
        gate_refs = refs[6:8] if gated else None
        dkv_ref, dfk_ref, dk_s, dv_s, df_s = refs[-5:]
        h, j, i = pl.program_id(1), pl.program_id(2), pl.program_id(3)

        @pl.when(i == 0)
        def _():
            dk_s[...] = jnp.zeros_like(dk_s)
            dv_s[...] = jnp.zeros_like(dv_s)
            df_s[...] = jnp.zeros_like(df_s)

        def step(masked):
            s, q = _scores(q_ref, k_ref, gate_refs, scale, h, masked, tq, tq)
            p = jnp.exp(s - lse_ref[:, 0:1])
            do_b = do_ref[...].astype(BF16)
            dp = lax.dot_general(do_b, v_ref[...].astype(BF16), _DN["nt"], preferred_element_type=F32)
            ds = p * (dp - dl_ref[:, 0:1])
            dv_s[...] += lax.dot_general(p.astype(BF16), do_b, _DN["tn"], preferred_element_type=F32)
            dk_s[...] += lax.dot_general(ds.astype(BF16), q, _DN["tn"], preferred_element_type=F32)
            df_s[...] -= jnp.sum(ds, axis=0, keepdims=True)

        @pl.when(i > j)
        def _():
            step(False)

        @pl.when(i == j)
        def _():
            step(True)

        @pl.when(i == n_q - 1)
        def _():
            dkv_ref[:, 0:LANES] = dk_s[...].astype(dkv_ref.dtype)
            dkv_ref[:, LANES:2 * LANES] = dv_s[...].astype(dkv_ref.dtype)
            dfk_ref[...] = df_s[...]

    blk = (None, tq, LANES)
    col = pl.BlockSpec((None, None, tq, LANES), lambda b, h, j, i: (b, h, jnp.maximum(i, j), 0))
    in_specs = [pl.BlockSpec(blk, lambda b, h, j, i: (b, jnp.maximum(i, j), q0 + h)),
                pl.BlockSpec(blk, lambda b, h, j, i: (b, j, kv0 + 2 * h)),
                pl.BlockSpec(blk, lambda b, h, j, i: (b, j, kv0 + 2 * h + 1)),
                pl.BlockSpec(blk, lambda b, h, j, i: (b, jnp.maximum(i, j), o0 + h)), col, col]
    args = [qa, kva, kva, dmo, lse, delta]
    if gated:
        in_specs += [pl.BlockSpec(blk, lambda b, h, j, i: (b, jnp.maximum(i, j), 0)),
                     pl.BlockSpec((None, 8, tq), lambda b, h, j, i: (b, 0, j))]
        args += list(gates)
    aliases = {}
    if aliased:
        in_specs.append(pl.BlockSpec(memory_space=pl.ANY))
        args.append(out)
        aliases = {len(args) - 1: 0}
    return pl.pallas_call(
        body, name=name, grid=(bsz, N_HEADS, n_q, n_q), in_specs=in_specs,
        out_specs=[pl.BlockSpec((None, tq, 2 * LANES), lambda b, h, j, i: (b, j, out0 + h)),
                   pl.BlockSpec((None, None, 1, tq), lambda b, h, j, i: (b, h, 0, j))],
        out_shape=[jax.ShapeDtypeStruct(out.shape, out.dtype), jax.ShapeDtypeStruct((bsz, N_HEADS, 1, seq), F32)],
        scratch_shapes=[pltpu.VMEM((tq, LANES), F32), pltpu.VMEM((tq, LANES), F32), pltpu.VMEM((1, tq), F32)],
        input_output_aliases=aliases,
        compiler_params=_cparams(("parallel", "parallel", "parallel", "arbitrary")),
    )(*args)


def _gmlp_fn(uv, lng, lnb, ws, bst):
    u = jax.nn.gelu(uv[:, 0:GROUP_WIDTH])
    gv = jax.nn.gelu(uv[:, GROUP_WIDTH:2 * GROUP_WIDTH])
    mu = jnp.mean(gv, axis=-1, keepdims=True)
    vc = gv - mu
    var = jnp.mean(vc * vc, axis=-1, keepdims=True)
    vln = vc * lax.rsqrt(var + LN_EPS) * lng + lnb
    r_i = lax.broadcasted_iota(jnp.int32, (D_CHUNK, D_CHUNK), 0)
    c_i = lax.broadcasted_iota(jnp.int32, (D_CHUNK, D_CHUNK), 1)
    lane_g = lax.broadcasted_iota(jnp.int32, (D_CHUNK, GROUP_WIDTH), 1) // HEAD_DIM
    e_r = lax.broadcasted_iota(jnp.int32, (LANES, GROUP_WIDTH), 0)
    e_c = lax.broadcasted_iota(jnp.int32, (LANES, GROUP_WIDTH), 1)
    expand = (e_r == e_c // HEAD_DIM).astype(F32)
    mixed = jnp.dot(bst, expand, precision=HI, preferred_element_type=F32)
    for g in range(4):
        w = jnp.where(r_i >= c_i, ws[g], 0.0)
        mixed = mixed + jnp.where(lane_g == g, _bdot(w, vln, "nn"), 0.0)
    return u * mixed


def _gmlp_fwd(proj, mo, lng, lnb, ws, bst, name):
    bsz, seq, _ = proj.shape

    def body(p_ref, mo_any, lng_ref, lnb_ref, ws_ref, bst_ref, o_ref):
        del mo_any
        o_ref[...] = _gmlp_fn(p_ref[...], lng_ref[...], lnb_ref[...], ws_ref[...], bst_ref[...]).astype(o_ref.dtype)

    return pl.pallas_call(
        body, name=name, grid=(bsz, seq // D_CHUNK),
        in_specs=[pl.BlockSpec((None, D_CHUNK, 512), lambda b, s: (b, s, P_D // 512)),
                  pl.BlockSpec(memory_space=pl.ANY), _vec_spec(256), _vec_spec(256),
                  pl.BlockSpec((4, D_CHUNK, D_CHUNK), lambda b, s: (0, 0, 0)),
                  pl.BlockSpec((D_CHUNK, LANES), lambda b, s: (0, 0))],
        out_specs=pl.BlockSpec((None, D_CHUNK, GROUP_WIDTH), lambda b, s: (b, s, 1280 // GROUP_WIDTH)),
        out_shape=jax.ShapeDtypeStruct(mo.shape, mo.dtype),
        input_output_aliases={1: 0},
        compiler_params=_cparams(("parallel", "parallel")),
    )(proj, mo, lng, lnb, ws, bst)


def _gmlp_bwd(dmo, dproj, proj, lng, lnb, ws, bst, name):
    bsz, seq, _ = proj.shape

    def body(do_ref, dp_any, p_ref, lng_ref, lnb_ref, ws_ref, bst_ref, dp_ref, dlg_ref, dlb_ref, dws_ref, dbst_ref):
        del dp_any
        first = jnp.logical_and(pl.program_id(0) == 0, pl.program_id(1) == 0)

        @pl.when(first)
        def _():
            dlg_ref[...] = jnp.zeros_like(dlg_ref)
            dlb_ref[...] = jnp.zeros_like(dlb_ref)
            dws_ref[...] = jnp.zeros_like(dws_ref)
            dbst_ref[...] = jnp.zeros_like(dbst_ref)

        _, vjp = jax.vjp(_gmlp_fn, p_ref[...], lng_ref[...], lnb_ref[...], ws_ref[...], bst_ref[...])
        duv, dlg, dlb, dws, dbst = vjp(do_ref[...])
        dp_ref[...] = duv.astype(dp_ref.dtype)
        dlg_ref[...] += dlg
        dlb_ref[...] += dlb
        dws_ref[...] += dws
        dbst_ref[...] += dbst

    const2 = lambda shape: pl.BlockSpec(shape, lambda b, s: (0,) * len(shape))
    return pl.pallas_call(
        body, name=name, grid=(bsz, seq // D_CHUNK),
        in_specs=[pl.BlockSpec((None, D_CHUNK, GROUP_WIDTH), lambda b, s: (b, s, 1280 // GROUP_WIDTH)),
                  pl.BlockSpec(memory_space=pl.ANY),
                  pl.BlockSpec((None, D_CHUNK, 512), lambda b, s: (b, s, P_D // 512)),
                  _vec_spec(256), _vec_spec(256), const2((4, D_CHUNK, D_CHUNK)), const2((D_CHUNK, LANES))],
        out_specs=[pl.BlockSpec((None, D_CHUNK, 512), lambda b, s: (b, s, P_D // 512)),
                   _vec_spec(256), _vec_spec(256), const2((4, D_CHUNK, D_CHUNK)), const2((D_CHUNK, LANES))],
        out_shape=[jax.ShapeDtypeStruct(dproj.shape, dproj.dtype), jax.ShapeDtypeStruct((1, 256), F32),
                   jax.ShapeDtypeStruct((1, 256), F32), jax.ShapeDtypeStruct((4, D_CHUNK, D_CHUNK), F32),
                   jax.ShapeDtypeStruct((D_CHUNK, LANES), F32)],
        input_output_aliases={1: 0},
        compiler_params=_cparams(("arbitrary", "arbitrary")),
    )(dmo, dproj, proj, lng, lnb, ws, bst)


def _ada_fwd(c_all, ada_w, name):
    n_b = c_all.shape[0]
    depth, d, cols = ada_w.shape

    def body(c_ref, w_ref, o_ref):
        cv = c_ref[...]
        act = (cv * jax.nn.sigmoid(cv)).astype(BF16)
        o_ref[...] = jnp.dot(act, w_ref[...].astype(BF16), preferred_element_type=F32)

    return pl.pallas_call(
        body, name=name, grid=(depth,),
        in_specs=[pl.BlockSpec((n_b, d), lambda l: (0, 0)), pl.BlockSpec((None, d, cols), lambda l: (l, 0, 0))],
        out_specs=pl.BlockSpec((None, n_b, cols), lambda l: (l, 0, 0)),
        out_shape=jax.ShapeDtypeStruct((depth, n_b, cols), F32),
        compiler_params=_cparams(("parallel",)),
    )(c_all, ada_w)


def _ada_bwd(c_all, dmod_cols, dmod_full, name):
    n_b, d = c_all.shape
    depth, _, cols = dmod_cols.shape
    full = dmod_full.shape[-1]

    def body(c_ref, dm_ref, df_ref, gw_ref, gb_ref):
        cv = c_ref[...]
        act = (cv * jax.nn.sigmoid(cv)).astype(BF16)
        gw_ref[...] = lax.dot_general(act, dm_ref[...].astype(BF16), (((0,), (0,)), ((), ())),
                                      preferred_element_type=F32)
        gb_ref[...] = jnp.sum(df_ref[...], axis=0, keepdims=True)

    return pl.pallas_call(
        body, name=name, grid=(depth,),
        in_specs=[pl.BlockSpec((n_b, d), lambda l: (0, 0)), pl.BlockSpec((None, n_b, cols), lambda l: (l, 0, 0)),
                  pl.BlockSpec((None, n_b, full), lambda l: (l, 0, 0))],
        out_specs=[pl.BlockSpec((None, d, cols), lambda l: (l, 0, 0)),
                   pl.BlockSpec((None, 1, full), lambda l: (l, 0, 0))],
        out_shape=[jax.ShapeDtypeStruct((depth, d, cols), F32), jax.ShapeDtypeStruct((depth, 1, full), F32)],
        compiler_params=_cparams(("parallel",)),
    )(c_all, dmod_cols, dmod_full)


def _adamw(gparts, own, w, m, v, name, layer=0, prev=None):
    n_p, rows, cols = gparts.shape
    assert w.shape[1:] == (rows, cols)
    tr = rows
    if rows > 512:
        tr = next(c for c in range(512, 7, -8) if rows % c == 0)
    has_own = own is not None
    n_prev = 0 if prev is None else 4

    def body(*refs):
        g_ref = refs[0]
        own_ref = refs[1] if has_own else None
        w_ref, m_ref, v_ref = refs[1 + has_own:4 + has_own]
        go_ref, do_ref, mo_ref, vo_ref = refs[4 + has_own + n_prev:]
        if has_own:
            g = own_ref[...].astype(F32) + g_ref[0].astype(F32)
        else:
            g = g_ref[0].astype(F32)
        for p in range(1, n_p):
            g = g + g_ref[p].astype(F32)
        m_new = ADAM_B1 * m_ref[...] + (1.0 - ADAM_B1) * g
        v_new = ADAM_B2 * v_ref[...] + (1.0 - ADAM_B2) * (g * g)
        m_hat = m_new / (1.0 - ADAM_B1 ** ADAM_STEP)
        v_hat = v_new / (1.0 - ADAM_B2 ** ADAM_STEP)
        go_ref[...] = g
        do_ref[...] = -ADAM_LR * (m_hat / (jnp.sqrt(v_hat) + ADAM_EPS) + ADAM_WD * w_ref[...])
        mo_ref[...] = m_new
        vo_ref[...] = v_new

    spec = pl.BlockSpec((None, tr, cols), lambda i: (layer, i, 0))
    in_specs = [pl.BlockSpec((n_p, tr, cols), lambda i: (0, i, 0))]
    args = [gparts]
    if has_own:
        in_specs.append(pl.BlockSpec((tr, cols), lambda i: (i, 0)))
        args.append(own)
    in_specs += [spec, spec, spec]
    args += [w, m, v]
    aliases = {}
    if prev is not None:
        aliases = {len(args) + k: k for k in range(4)}
        in_specs += [pl.BlockSpec(memory_space=pl.ANY)] * 4
        args += list(prev)
    shp = jax.ShapeDtypeStruct(w.shape, F32)
    return pl.pallas_call(
        body, name=name, grid=(rows // tr,), in_specs=in_specs,
        out_specs=[spec, spec, spec, spec], out_shape=[shp, shp, shp, shp], input_output_aliases=aliases,
        compiler_params=_cparams(("parallel",)),
    )(*args)


def _sum_parts(parts, name):
    n_p, rows, cols = parts.shape
    tr = 256 if rows % 256 == 0 else rows

    def body(p_ref, o_ref):
        acc = p_ref[0]
        for p in range(1, n_p):
            acc = acc + p_ref[p]
        o_ref[...] = acc

    return pl.pallas_call(
        body, name=name, grid=(rows // tr,),
        in_specs=[pl.BlockSpec((n_p, tr, cols), lambda i: (0, i, 0))],
        out_specs=pl.BlockSpec((tr, cols), lambda i: (i, 0)),
        out_shape=jax.ShapeDtypeStruct((rows, cols), F32),
        compiler_params=_cparams(("parallel",)),
    )(parts)


def _all_gather(arrs, name):
    n = len(arrs)

    def body(*refs):
        in_refs, out_refs = refs[:n], refs[n:2 * n]
        send_sems, recv_sems, loc_sems = refs[2 * n:]
        x, y, c = lax.axis_index("x"), lax.axis_index("y"), lax.axis_index("c")
        me, sibling = (x, y, c), (x, y, 1 - c)
        chips = [(1 - x, y), (x, 1 - y), (1 - x, 1 - y)]

        def copy(a, k, block, to, src=None):
            slot = out_refs[a].at[4 * block[0] + 2 * block[1] + block[2]]
            return pltpu.make_async_remote_copy(
                src_ref=slot if src is None else src, dst_ref=slot, send_sem=send_sems.at[a, k],
                recv_sem=recv_sems.at[a, k], device_id=to, device_id_type=pl.DeviceIdType.MESH)

        mine = [pltpu.make_async_copy(in_refs[a], out_refs[a].at[4 * x + 2 * y + c], loc_sems.at[a])
                for a in range(n)]
        for cp in mine:
            cp.start()
        first = []
        for a in range(n):
            first.append(copy(a, 0, me, sibling, src=in_refs[a]))
            first += [copy(a, 1 + j, me, (*chip, c), src=in_refs[a]) for j, chip in enumerate(chips)]
        for cp in first:
            cp.start()
        passed = []
        for j, chip in enumerate(chips):
            for a in range(n):
                copy(a, 1 + j, (*chip, c), me).wait_recv()
                cp = copy(a, 4 + j, (*chip, c), sibling)
                cp.start()
                passed.append(cp)
        for a in range(n):
            copy(a, 0, sibling, me).wait_recv()
        for j, chip in enumerate(chips):
            for a in range(n):
                copy(a, 4 + j, (*chip, 1 - c), me).wait_recv()
        for cp in first + passed:
            cp.wait_send()
        for cp in mine:
            cp.wait()

    any_spec = pl.BlockSpec(memory_space=pl.ANY)
    return pl.pallas_call(
        body, name=name, in_specs=[any_spec] * n, out_specs=[any_spec] * n,
        out_shape=[jax.ShapeDtypeStruct((N_DEV,) + a.shape, a.dtype) for a in arrs],
        scratch_shapes=[pltpu.SemaphoreType.DMA((n, N_DEV - 1)), pltpu.SemaphoreType.DMA((n, N_DEV - 1)),
                        pltpu.SemaphoreType.DMA((n,))],
    )(*arrs)


def _flip_peers():
    x, y, c = lax.axis_index("x"), lax.axis_index("y"), lax.axis_index("c")
    peers = []
    for fx, fy, fc in [(fx, fy, fc) for fx in (0, 1) for fy in (0, 1) for fc in (0, 1)][1:]:
        px, py, pc = (1 - x if fx else x), (1 - y if fy else y), (1 - c if fc else c)
        peers.append(((px, py, pc), 4 * px + 2 * py + pc))
    return 4 * x + 2 * y + c, peers


def _push_start(srcs, name, whole=False):
    n, n_peer = len(srcs), N_DEV - 1
    if whole:
        me_w = 4 * lax.axis_index("x") + 2 * lax.axis_index("y") + lax.axis_index("c")
        lands = [lax.dynamic_update_slice_in_dim(jnp.zeros((N_DEV,) + a.shape, a.dtype), a[None], me_w, axis=0)
                 for a in srcs]
    else:
        lands = [jnp.zeros(a.shape, a.dtype) for a in srcs]

    def body(*refs):
        src_refs, land_refs = refs[:n], refs[n:2 * n]
        send_sems, recv_sems = refs[2 * n], refs[2 * n + 1]
        token = refs[-1]
        me, peers = _flip_peers()
        for k, (dev, idx) in enumerate(peers):
            for a in range(n):
                pltpu.make_async_remote_copy(
                    src_ref=src_refs[a] if whole else src_refs[a].at[idx], dst_ref=land_refs[a].at[me],
                    send_sem=send_sems.at[a * n_peer + k], recv_sem=recv_sems.at[a * n_peer + k], device_id=dev,
                    device_id_type=pl.DeviceIdType.MESH).start()
        token[...] = jnp.zeros_like(token)

    hbm = pl.BlockSpec(memory_space=pltpu.HBM)
    sem = pl.BlockSpec(memory_space=pltpu.SEMAPHORE)
    arrs = list(srcs) + lands
    res = pl.pallas_call(
        body, name=name, in_specs=[hbm] * (2 * n),
        out_specs=(sem, sem, *[hbm] * (2 * n), pl.BlockSpec(memory_space=pltpu.VMEM)),
        out_shape=(pltpu.SemaphoreType.DMA((n * n_peer,)), pltpu.SemaphoreType.DMA((n * n_peer,)),
                   *[pltpu.HBM(a.shape, a.dtype) for a in arrs], jax.ShapeDtypeStruct((8, LANES), F32)),
        input_output_aliases={i: 2 + i for i in range(2 * n)},
        compiler_params=pltpu.CompilerParams(has_side_effects=pltpu.SideEffectType.DATAFLOW_SIDE_EFFECTING),
    )(*[pltpu.with_memory_space_constraint(a, pltpu.HBM) for a in arrs])
    return res[0], res[1], list(res[2:2 + n]), list(res[2 + n:2 + 2 * n]), res[-1]


def _push_wait(send_sems, recv_sems, srcs, lands, after, name, whole=False):
    n, n_peer = len(srcs), N_DEV - 1

    def body(*refs):
        src_refs, land_refs = refs[:n], refs[n:2 * n]
        send_s, recv_s = refs[2 * n], refs[2 * n + 1]
        _, peers = _flip_peers()
        for k, (dev, idx) in enumerate(peers):
            for a in range(n):
                cp = pltpu.make_async_remote_copy(
                    src_ref=src_refs[a] if whole else src_refs[a].at[idx], dst_ref=land_refs[a].at[idx],
                    send_sem=send_s.at[a * n_peer + k],
                    recv_sem=recv_s.at[a * n_peer + k], device_id=dev, device_id_type=pl.DeviceIdType.MESH)
                cp.wait_send()
                cp.wait_recv()

    hbm = pl.BlockSpec(memory_space=pltpu.HBM)
    sem = pl.BlockSpec(memory_space=pltpu.SEMAPHORE)
    arrs = list(srcs) + list(lands)
    res = pl.pallas_call(
        body, name=name, in_specs=[hbm] * (2 * n) + [sem, sem, pl.BlockSpec(memory_space=pl.ANY)],
        out_specs=tuple([hbm] * (2 * n)), out_shape=tuple(pltpu.HBM(a.shape, a.dtype) for a in arrs),
        input_output_aliases={i: i for i in range(2 * n)},
        compiler_params=pltpu.CompilerParams(has_side_effects=pltpu.SideEffectType.DATAFLOW_SIDE_EFFECTING),
    )(*arrs, send_sems, recv_sems, after)
    return list(res[:n]), list(res[n:])


def _ffn_fwd(x, h, mod, w_in, w_out, lng, lnb, rows, tag, nxt):
    bsz, seq, d = x.shape
    t = bsz * seq
    if h is None:
        h = _modulate(x, mod, rows[0], rows[1], f"modulate_{tag}")
    z, a = _ffn_in_swiglu(h.reshape(t, d), w_in, f"ffn_in_{tag}")
    f = _matmul(a, w_out, mode="nn", group_out=False, out_dtype=F32, tm=1024, tk=a.shape[2],
                name=f"ffn_out_{tag}").reshape(bsz, seq, d)
    y, h_next = _res_ln(x, f, mod, lng, lnb, rows[2], 0.5, f"res_ln_{tag}", nxt)
    return y, h_next, (x, h, z, a, f)


def _tied(mod, tie):
    return mod if tie is None else mod + tie


def _ffn_bwd(dy, saved, mod, w_in, w_out, lng, lnb, rows, tag, ready):
    x, h, z, a, f = saved
    bsz, seq, d = x.shape
    t = bsz * seq
    dx_res, df, dgate, dlg, dlb = _res_ln_bwd(dy, x, f, mod, lng, lnb, rows[2], 0.5, f"res_ln_bwd_{tag}")
    df2 = df.reshape(1, t, d)
    dw_out = _matmul(a, df2, mode="tn", group_out=True, out_dtype=BF16, tm=a.shape[2], tk=min(t, 2048),
                     name=f"ffn_out_dw_{tag}")
    tie_out = ready(f"{tag}_out", dw_out)
    dz = _ffn_out_dx_swiglu(df.reshape(t, d), w_out, z, f"ffn_out_dx_{tag}").reshape(N_DEV, t, -1)
    dw_in = _matmul(dz, h.reshape(1, t, d), mode="tn", group_out=True, out_dtype=BF16, tm=dz.shape[2],
                    tk=min(t, 2048), name=f"ffn_in_dw_{tag}")
    tie_in = ready(f"{tag}_in", dw_in)
    dh = _matmul(dz, w_in, mode="nn", group_out=False, out_dtype=F32, tm=1024, tk=dz.shape[2],
                 name=f"ffn_in_dx_{tag}").reshape(bsz, seq, d)
    dx, dsh, dsc = _modulate_bwd(dh, x, _tied(_tied(mod, tie_out), tie_in), dx_res, rows[1],
                                 f"modulate_bwd_{tag}")
    return dx, (dsh, dsc, dgate), dw_in, dw_out, dlg, dlb


def _mixer_fwd(x, h, mod, wts, small, lng, lnb, layer, tabs):
    bsz, seq, d = x.shape
    t = bsz * seq
    proj = _matmul(h.reshape(1, t, d), wts["mix_in"][None], mode="nn", group_out=True, out_dtype=F32, tm=512, tk=d,
                   name="mix_in").reshape(bsz, seq, PACK_W)
    mo, states = _hgrn_fwd(proj, small["lb_logits8"], small["hgrn_norm_g"], layer, f"hgrn_fwd_l{layer}")
    q, kv = _mla_pre(proj, small["q_norm_g"], small["kv_norm_g"], wts["uq"], wts["ukv"], tabs, "mla_pre")
    mla_scale = float((B_NOPE + B_ROPE) ** -0.5)
    mo, lse_b = _attn_fwd(q, 0, kv, 0, mo, 2, None, mla_scale, "mla_attn_fwd")
    fg = _fox_gate(proj, small["fox_b_f"], "fox_gate")
    gates = (fg, jnp.swapaxes(fg[:, :, 0:8], 1, 2))
    fox_scale = float(HEAD_DIM ** -0.5)
    mo, lse_c = _attn_fwd(proj, P_CQ // LANES, proj, P_CKV // LANES, mo, 6, gates, fox_scale, "fox_attn_fwd")
    mo = _gmlp_fwd(proj, mo, small["gmlp_ln_g"], small["gmlp_ln_b"], small["gmlp_w_s"], small["gmlp_bst"],
                   "gmlp_fwd")
    mixed = _matmul(mo.reshape(1, t, MO_W), wts["mix_out"][None], mode="nn", group_out=True, out_dtype=F32,
                    tm=1024, tk=MO_W, name="mix_out").reshape(bsz, seq, d)
    y, h_next = _res_ln(x, mixed, mod, lng, lnb, 5, 1.0, "res_ln_mix", (mod, 6, 7))
    return y, h_next, (x, h, proj, mo, states, q, kv, lse_b, gates, lse_c, mixed)


def _mixer_bwd(dy, saved, mod, wts, small, lng, lnb, layer, tabs, ready):
    x, h, proj, mo, states, q, kv, lse_b, gates, lse_c, mixed = saved
    bsz, seq, d = x.shape
    t = bsz * seq
    dx_res, dmixed, dgate, dlg, dlb = _res_ln_bwd(dy, x, mixed, mod, lng, lnb, 5, 1.0, "res_ln_bwd_mix")
    dm2 = dmixed.reshape(1, t, d)
    dmo = _matmul(dm2, wts["mix_out"][None], mode="nt", group_out=True, out_dtype=F32, tm=1024, tk=d,
                  name="mix_out_dx").reshape(bsz, seq, MO_W)
    dw_out = _matmul(mo.reshape(1, t, MO_W), dm2, mode="tn", group_out=True, out_dtype=F32, tm=512, tk=min(t, 2048),
                     name="mix_out_dw")[0]
    tie_out = ready("mix_out", dw_out)
    g = {}
    dproj, g["lb_logits8"], g["hgrn_norm_g"] = _hgrn_bwd(dmo, proj, states, small["lb_logits8"],
                                                         small["hgrn_norm_g"], layer, f"hgrn_bwd_l{layer}")
    mla_scale = float((B_NOPE + B_ROPE) ** -0.5)
    dq, delta_b, _ = _attn_bwd_q(q, 0, kv, 0, mo, dmo, 2, lse_b, None, mla_scale,
                                 jax.ShapeDtypeStruct((bsz, seq, 512), F32), 0, "mla_attn_bwd_q")
    dkv, _ = _attn_bwd_kv(q, 0, kv, 0, dmo, 2, lse_b, delta_b, None, mla_scale,
                          jax.ShapeDtypeStruct((bsz, seq, 1024), F32), 0, "mla_attn_bwd_kv")
    dproj, g["q_norm_g"], g["kv_norm_g"], g["uq"], g["ukv"] = _mla_pre_bwd(
        dq, dkv, dproj, proj, small["q_norm_g"], small["kv_norm_g"], wts["uq"], wts["ukv"], tabs, "mla_pre_bwd")
    fox_scale = float(HEAD_DIM ** -0.5)
    dproj, delta_c, dfq = _attn_bwd_q(proj, P_CQ // LANES, proj, P_CKV // LANES, mo, dmo, 6, lse_c, gates,
                                      fox_scale, dproj, P_CQ // LANES, "fox_attn_bwd_q")
    dproj, dfk = _attn_bwd_kv(proj, P_CQ // LANES, proj, P_CKV // LANES, dmo, 6, lse_c, delta_c, gates, fox_scale,
                              dproj, P_CKV // (2 * LANES), "fox_attn_bwd_kv")
    dfk_cols = jnp.pad(jnp.swapaxes(dfk[:, :, 0, :], 1, 2), ((0, 0), (0, 0), (0, LANES - N_HEADS)))
    dproj, g["fox_b_f"] = _fox_gate_bwd(dfq, dfk_cols, dproj, proj, small["fox_b_f"], "fox_gate_bwd")
    dproj, g["gmlp_ln_g"], g["gmlp_ln_b"], g["gmlp_w_s"], g["gmlp_bst"] = _gmlp_bwd(
        dmo, dproj, proj, small["gmlp_ln_g"], small["gmlp_ln_b"], small["gmlp_w_s"], small["gmlp_bst"], "gmlp_bwd")
    dp2 = dproj.reshape(1, t, PACK_W)
    dw_in = _matmul(h.reshape(1, t, d), dp2, mode="tn", group_out=True, out_dtype=BF16, tm=512, tk=1024,
                    name="mix_in_dw")[0]
    tie_in = ready("mix_in", dw_in)
    dh = _matmul(dp2, wts["mix_in"][None], mode="nt", group_out=True, out_dtype=F32, tm=512, tk=PACK_W,
                 name="mix_in_dx").reshape(bsz, seq, d)
    dx, dsh, dsc = _modulate_bwd(dh, x, _tied(_tied(mod, tie_out), tie_in), dx_res, 4, "modulate_bwd_mix")
    return dx, (dsh, dsc, dgate), dw_in, dw_out, g, dlg, dlb


def _small_views(p, layer):
    return {
        "lb_logits8": jnp.pad(p["hgrn_lb_logits"], ((0, 8 - DEPTH), (0, 0))),
        "hgrn_norm_g": p["hgrn_norm_g"][layer][None],
        "q_norm_g": p["mla_q_norm_g"][layer][None],
        "kv_norm_g": p["mla_kv_norm_g"][layer][None],
        "fox_b_f": jnp.pad(p["fox_b_f"][layer][None], ((0, 0), (0, LANES - N_HEADS))),
        "gmlp_ln_g": p["gmlp_ln_g"][layer][None],
        "gmlp_ln_b": p["gmlp_ln_b"][layer][None],
        "gmlp_w_s": p["gmlp_w_s"][layer],
        "gmlp_bst": jnp.pad(p["gmlp_b_s"][layer].T, ((0, 0), (0, LANES - N_HEADS))),
    }


def _local_step(x, mod, target, weights, p, grads_ready=None):
    bsz, seq, d = x.shape
    tabs = _rope_tables(seq)
    saved = []
    h = None
    for l in range(DEPTH):
        sm = _small_views(p, l)
        lng, lnb = p["ln_g"][l], p["ln_b"][l]
        w = weights(l, "ffn1", x)
        x, h, s1 = _ffn_fwd(x, h, mod[l], w["ffn1_in"], w["ffn1_out"], lng[0:1], lnb[0:1], (0, 1, 2), "ffn1",
                            (mod[l], 3, 4))
        x, h, s2 = _mixer_fwd(x, h, mod[l], weights(l, "mix", x), sm, lng[1:2], lnb[1:2], l, tabs)
        w = weights(l, "ffn2", x)
        x, h, s3 = _ffn_fwd(x, h, mod[l], w["ffn2_in"], w["ffn2_out"], lng[2:3], lnb[2:3], (6, 7, 8), "ffn2",
                            (mod[l + 1], 0, 1) if l + 1 < DEPTH else None)
        saved.append((s1, s2, s3))
    dx, loss = _loss_head(x, target, "loss_head")
    big, small, dmods = [None] * DEPTH, [None] * DEPTH, [None] * DEPTH
    ties = []

    def tied(a):
        for t in ties:
            a = a + t
        return a

    for l in reversed(range(DEPTH)):
        w = {**weights(l, "ffn1", None), **weights(l, "mix", None), **weights(l, "ffn2", None)}
        sm = _small_views(p, l)
        lng, lnb = p["ln_g"][l], p["ln_b"][l]
        s1, s2, s3 = saved[l]

        def ready(name, grad, l=l):
            tie = None if grads_ready is None else grads_ready(l, name, grad)
            if tie is not None:
                ties.append(tie)
            return tie

        dx, dm3, dwi2, dwo2, dlg2, dlb2 = _ffn_bwd(dx, s3, tied(mod[l]), w["ffn2_in"], w["ffn2_out"], lng[2:3],
                                                   lnb[2:3], (6, 7, 8), "ffn2", ready)
        dx, dm2, dwmi, dwmo, g, dlg1, dlb1 = _mixer_bwd(dx, s2, tied(mod[l]), w, sm, lng[1:2], lnb[1:2], l, tabs,
                                                        ready)
        dx, dm1, dwi1, dwo1, dlg0, dlb0 = _ffn_bwd(dx, s1, tied(mod[l]), w["ffn1_in"], w["ffn1_out"], lng[0:1],
                                                   lnb[0:1], (0, 1, 2), "ffn1", ready)
        dmods[l] = jnp.concatenate(list(dm1) + list(dm2) + list(dm3), axis=1)
        big[l] = {"ffn1_in": dwi1, "ffn1_out": dwo1, "ffn2_in": dwi2, "ffn2_out": dwo2, "mix_in": dwmi,
                  "mix_out": dwmo}
        g["ln_g"] = jnp.concatenate([dlg0, dlg1, dlg2], axis=0)
        g["ln_b"] = jnp.concatenate([dlb0, dlb1, dlb2], axis=0)
        small[l] = g
    return loss, dx, jnp.stack(dmods), big, small


_BIG = ("ffn1_in", "ffn1_out", "ffn2_in", "ffn2_out", "mix_in", "mix_out")


def _small_grad_list(small, loss):
    def both(fn):
        return jnp.stack([fn(small[l]) for l in range(DEPTH)])

    uq_src, ukv_src = _uq_src(), _ukv_src()
    return [
        ("loss", loss.reshape(1)),
        ("ln_g", both(lambda g: g["ln_g"])), ("ln_b", both(lambda g: g["ln_b"])),
        ("hgrn_lb_logits", small[0]["lb_logits8"][:DEPTH] + small[1]["lb_logits8"][:DEPTH]),
        ("hgrn_norm_g", both(lambda g: g["hgrn_norm_g"][0])),
        ("mla_q_norm_g", both(lambda g: g["q_norm_g"][0])),
        ("mla_kv_norm_g", both(lambda g: g["kv_norm_g"][0])),
        ("mla_w_uq", both(lambda g: _unpack_cols(g["uq"], uq_src, 384))),
        ("mla_w_ukv", both(lambda g: _unpack_cols(g["ukv"], ukv_src, 512))),
        ("fox_b_f", both(lambda g: g["fox_b_f"][0, :N_HEADS])),
        ("gmlp_ln_g", both(lambda g: g["gmlp_ln_g"][0])), ("gmlp_ln_b", both(lambda g: g["gmlp_ln_b"][0])),
        ("gmlp_w_s", both(lambda g: g["gmlp_w_s"])),
        ("gmlp_b_s", both(lambda g: g["gmlp_bst"][:, :N_HEADS].T)),
    ]


_PACK_COLS = 512


def _pack_small(items):
    flat = jnp.concatenate([a.reshape(-1).astype(F32) for _, a in items])
    n = flat.shape[0]
    tile = 8 * _PACK_COLS
    flat = jnp.pad(flat, (0, (-n) % tile))
    return flat.reshape(-1, _PACK_COLS)


def _unpack_small(buf, items):
    flat = buf.reshape(-1)
    out, off = {}, 0
    for name, a in items:
        out[name] = flat[off:off + a.size].reshape(a.shape)
        off += a.size
    return out


def _as2d(a):
    return a.reshape(-1, a.shape[-1])


def kernel(x, c, ada_w, ada_b, ln_g, ln_b, ffn1_w_in, ffn1_w_out, ffn2_w_in, ffn2_w_out, mix_w_in, mix_w_out, hgrn_lb_logits, hgrn_norm_g, mla_q_norm_g, mla_kv_norm_g, mla_w_uq, mla_w_ukv, fox_b_f, gmlp_ln_g, gmlp_ln_b, gmlp_w_s, gmlp_b_s, loss_target, m_ada_w, m_ada_b, m_ln_g, m_ln_b, m_ffn1_w_in, m_ffn1_w_out, m_ffn2_w_in, m_ffn2_w_out, m_mix_w_in, m_mix_w_out, m_hgrn_lb_logits, m_hgrn_norm_g, m_mla_q_norm_g, m_mla_kv_norm_g, m_mla_w_uq, m_mla_w_ukv, m_fox_b_f, m_gmlp_ln_g, m_gmlp_ln_b, m_gmlp_w_s, m_gmlp_b_s, v_ada_w, v_ada_b, v_ln_g, v_ln_b, v_ffn1_w_in, v_ffn1_w_out, v_ffn2_w_in, v_ffn2_w_out, v_mix_w_in, v_mix_w_out, v_hgrn_lb_logits, v_hgrn_norm_g, v_mla_q_norm_g, v_mla_kv_norm_g, v_mla_w_uq, v_mla_w_ukv, v_fox_b_f, v_gmlp_ln_g, v_gmlp_ln_b, v_gmlp_w_s, v_gmlp_b_s):
    names = ["ada_w", "ada_b", "ln_g", "ln_b", "ffn1_w_in", "ffn1_w_out", "ffn2_w_in", "ffn2_w_out", "mix_w_in",
             "mix_w_out", "hgrn_lb_logits", "hgrn_norm_g", "mla_q_norm_g", "mla_kv_norm_g", "mla_w_uq", "mla_w_ukv",
             "fox_b_f", "gmlp_ln_g", "gmlp_ln_b", "gmlp_w_s", "gmlp_b_s"]
    w = dict(zip(names, [ada_w, ada_b, ln_g, ln_b, ffn1_w_in, ffn1_w_out, ffn2_w_in, ffn2_w_out, mix_w_in, mix_w_out,
                         hgrn_lb_logits, hgrn_norm_g, mla_q_norm_g, mla_kv_norm_g, mla_w_uq, mla_w_ukv, fox_b_f,
                         gmlp_ln_g, gmlp_ln_b, gmlp_w_s, gmlp_b_s]))
    m = dict(zip(names, [m_ada_w, m_ada_b, m_ln_g, m_ln_b, m_ffn1_w_in, m_ffn1_w_out, m_ffn2_w_in, m_ffn2_w_out,
                         m_mix_w_in, m_mix_w_out, m_hgrn_lb_logits, m_hgrn_norm_g, m_mla_q_norm_g, m_mla_kv_norm_g,
                         m_mla_w_uq, m_mla_w_ukv, m_fox_b_f, m_gmlp_ln_g, m_gmlp_ln_b, m_gmlp_w_s, m_gmlp_b_s]))
    v = dict(zip(names, [v_ada_w, v_ada_b, v_ln_g, v_ln_b, v_ffn1_w_in, v_ffn1_w_out, v_ffn2_w_in, v_ffn2_w_out,
                         v_mix_w_in, v_mix_w_out, v_hgrn_lb_logits, v_hgrn_norm_g, v_mla_q_norm_g, v_mla_kv_norm_g,
                         v_mla_w_uq, v_mla_w_ukv, v_fox_b_f, v_gmlp_ln_g, v_gmlp_ln_b, v_gmlp_w_s, v_gmlp_b_s]))
    bsz, seq, d = x.shape
    me = 4 * lax.axis_index("x") + 2 * lax.axis_index("y") + lax.axis_index("c")
    mix_src, uq_src, ukv_src, mo_src = _mix_in_src(), _uq_src(), _ukv_src(), _mo_src()

    part_names = {"ffn1": ["ffn1_w_in", "ffn1_w_out"], "mix": ["mix_w_in", "mix_w_out", "mla_w_uq", "mla_w_ukv"],
                  "ffn2": ["ffn2_w_in", "ffn2_w_out"]}
    group_of = {}
    for l in range(DEPTH):
        for part in ("ffn1", "mix", "ffn2"):
            group_of[(l, part)] = (0, part) if l == 0 else (l, "all")
    in_flight = {}
    transposed = ("ffn1_w_in", "ffn2_w_in")

    def start_group(key, behind=None):
        members = [(l, part) for (l, part), g in group_of.items() if g == key]
        labels = [(l, n) for l, part in members for n in part_names[part]]
        shards = []
        for l, n in labels:
            a = w[n][l]
            if n == "mix_w_in":
                a = _pack_cols(a, mix_src)
            if n in transposed:
                a = jnp.swapaxes(w[n], 1, 2)[l]
            shards.append(a.astype(BF16))
        if behind is not None:
            shards, _ = lax.optimization_barrier((shards, behind))
        in_flight[key] = (labels, _push_start(shards, f"gather_start_{key[0]}_{key[1]}", whole=True))

    keys_in_order = list(dict.fromkeys(group_of.values()))
    start_group(keys_in_order[0])

    gathered = _all_gather([c, ln_g, ln_b], "gather_inputs")
    c_all = gathered[0].reshape(N_DEV * bsz, d)
    ln_g_full = jnp.moveaxis(gathered[1], 0, 2).reshape(DEPTH, 3, d)
    ln_b_full = jnp.moveaxis(gathered[2], 0, 2).reshape(DEPTH, 3, d)

    mod_cols = _ada_fwd(c_all, ada_w, "ada_fwd")
    mod_all, = _all_gather([mod_cols], "gather_mod")
    mod_mine = lax.dynamic_slice_in_dim(mod_all, me * bsz, bsz, axis=2)
    mod = jnp.moveaxis(mod_mine, 0, 2).reshape(DEPTH, bsz, N_MOD * d) + ada_b[:, None, :]
    for key in keys_in_order[1:]:
        start_group(key, behind=mod)
    tie = sum(h[-1][0, 0] for _, h in in_flight.values())
    mod = mod.reshape(DEPTH, bsz, N_MOD, d) + tie

    arrived, laid_out = {}, {}

    def weights(l, part, after):
        if (l, part) not in laid_out:
            laid_out[(l, part)] = lay_out(l, part, after)
        return laid_out[(l, part)]

    def lay_out(l, part, after):
        key = group_of[(l, part)]
        if key not in arrived:
            labels, (send_sems, recv_sems, srcs, lands, _) = in_flight[key]
            _, lands = _push_wait(send_sems, recv_sems, srcs, lands, after, f"gather_wait_{key[0]}_{key[1]}",
                                  whole=True)
            arrived[key] = dict(zip(labels, lands))
        gw = {n: arrived[key][(l, n)] for n in part_names[part]}
        if part != "mix":
            return {f"{part}_in": gw[f"{part}_w_in"], f"{part}_out": gw[f"{part}_w_out"].reshape(4, 704, d)}
        uq = jnp.moveaxis(gw["mla_w_uq"], 0, 1).reshape(256, 384)
        ukv = jnp.moveaxis(gw["mla_w_ukv"], 0, 1).reshape(128, 512)
        return {"mix_in": gw["mix_w_in"].reshape(d, PACK_W),
                "mix_out": _pack_cols(gw["mix_w_out"].reshape(d, d).T, mo_src).T,
                "uq": _pack_cols(uq, uq_src), "ukv": _pack_cols(ukv, ukv_src)}

    p = dict(w)
    p["ln_g"], p["ln_b"] = ln_g_full, ln_b_full
    def chunks(name, arr):
        if name in ("ffn1_in", "ffn2_in"):
            return arr
        if name in ("ffn1_out", "ffn2_out"):
            return arr.reshape(N_DEV, arr.shape[1] // 2, d)
        if name == "mix_in":
            return arr.reshape(N_DEV, d // N_DEV, PACK_W)
        return _unpack_cols(arr.T, mo_src, d).T.astype(BF16).reshape(N_DEV, d // N_DEV, d)

    pending, started = {}, []

    def grads_ready(l, name, grad):
        pending[(name, l)] = chunks(name, grad)
        flush = name == "ffn1_in" if l > 0 else name in ("ffn2_in", "mix_out", "mix_in", "ffn1_out", "ffn1_in")
        if not flush:
            return None
        keys = sorted(pending)
        handles = _push_start([pending[k] for k in keys], f"push_start_{len(started)}")
        pending.clear()
        started.append((keys, handles, l == 0 and name.startswith("ffn1")))
        return handles[-1][0, 0]

    loss, grad_x, dmod, big, small = _local_step(x, mod, loss_target, weights, p, grads_ready)
    del big

    recv, out = {}, {}

    def arrive(n, after):
        keys, (send_sems, recv_sems, srcs, lands, _), _ = started[n]
        srcs, lands = _push_wait(send_sems, recv_sems, srcs, lands, after, f"push_wait_{n}")
        for k, src, land in zip(keys, srcs, lands):
            recv[k] = (land, lax.dynamic_index_in_dim(src, me, 0, keepdims=False))

    big_of = {"ffn1_w_in": "ffn1_in", "ffn1_w_out": "ffn1_out", "ffn2_w_in": "ffn2_in", "ffn2_w_out": "ffn2_out",
              "mix_w_in": "mix_in", "mix_w_out": "mix_out"}
    chain = {name: None for name in big_of}

    def big_update(key, l):
        name = next(nm for nm, k in big_of.items() if k == key)
        parts, own = recv[(key, l)]
        if key == "mix_in":
            parts = _unpack_cols(parts, mix_src, MIX_ORIG_W)
            own = _unpack_cols(own, mix_src, MIX_ORIG_W)
        view = (lambda a: jnp.swapaxes(a, 1, 2)) if name in transposed else (lambda a: a)
        chain[name] = _adamw(parts, own, view(w[name]), view(m[name]), view(v[name]), f"adamw_{name}_l{l}",
                             layer=l, prev=chain[name])

    def update(name, grad):
        shape = w[name].shape
        as3 = lambda a: a.reshape(1, -1, shape[-1])
        res = _adamw(as3(grad), None, as3(w[name]), as3(m[name]), as3(v[name]), f"adamw_{name}")
        out[name] = tuple(r.reshape(shape) for r in res)

    for n, (keys, _, last) in enumerate(started):
        if not last:
            arrive(n, grad_x)
            for key, l in keys:
                big_update(key, l)

    dmod_flat = dmod.reshape(DEPTH, bsz, N_MOD * d)
    done = [r[0] for r in chain.values() if r is not None]
    if done:
        dmod_flat, _ = lax.optimization_barrier((dmod_flat, done))
    dmod_all, = _all_gather([dmod_flat], "gather_dmod")
    dmod_full = jnp.moveaxis(dmod_all, 0, 1).reshape(DEPTH, N_DEV * bsz, N_MOD * d)
    cols = ada_w.shape[2]
    dmod_cols = lax.dynamic_slice_in_dim(dmod_full, me * cols, cols, axis=2)
    g_ada_w, g_ada_b = _ada_bwd(c_all, dmod_cols, dmod_full, "ada_bwd")
    res = None
    for l in range(DEPTH):
        res = _adamw(g_ada_w[l][None], None, ada_w, m_ada_w, v_ada_w, f"adamw_ada_w_l{l}", layer=l, prev=res)
    out["ada_w"] = tuple(res)
    update("ada_b", g_ada_b.reshape(DEPTH, N_MOD * d))

    items = _small_grad_list(small, loss)
    parts, = _all_gather([_pack_small(items)], "gather_small")
    sg = _unpack_small(_sum_parts(parts, "sum_small"), items)
    for name in ("ln_g", "ln_b"):
        update(name, lax.dynamic_slice_in_dim(sg[name], me * (d // N_DEV), d // N_DEV, axis=2))
    for name, width in (("mla_w_uq", 48), ("mla_w_ukv", 64)):
        update(name, lax.dynamic_slice_in_dim(sg[name], me * width, width, axis=2))
    for name in ("hgrn_lb_logits", "hgrn_norm_g", "mla_q_norm_g", "mla_kv_norm_g", "fox_b_f", "gmlp_ln_g",
                 "gmlp_ln_b", "gmlp_w_s", "gmlp_b_s"):
        update(name, sg[name])

    for n, (keys, _, last) in enumerate(started):
        if last:
            arrive(n, out["gmlp_w_s"][0])
            for key, l in keys:
                big_update(key, l)
    for name in big_of:
        out[name] = tuple(jnp.swapaxes(r, 1, 2) if name in transposed else r for r in chain[name])

    return (sg["loss"][0], grad_x, *[out[n][0] for n in names], *[out[n][1] for n in names],
            *[out[n][2] for n in names], *[out[n][3] for n in names])
```

```python
import functools

import numpy as np
import jax
import jax.numpy as jnp
from jax import lax
from jax.experimental import pallas as pl
from jax.experimental.pallas import tpu as pltpu

F32 = jnp.float32
BF16 = jnp.bfloat16
HI = lax.Precision.HIGHEST

D_MODEL = 1024
DEPTH = 2
GROUP_WIDTH = 256
N_HEADS = 4
HEAD_DIM = 64
A_CHUNK = 16
LB_FLOOR = 1e-30
B_NOPE = 64
B_ROPE = 32
ROPE_THETA = 10000.0
D_CHUNK = 128
D_FF = 2816
N_MOD = 9
ALPHA = (2 * DEPTH) ** 0.25
LN_EPS = 1e-5
RMS_EPS = 1e-6
ADAM_LR = 0.001
ADAM_B1 = 0.9
ADAM_B2 = 0.999
ADAM_EPS = 1e-08
ADAM_WD = 0.01
ADAM_STEP = 10

N_DEV = 8
LANES = 128
PACK_W = 3712
MO_W = 1536
VMEM_LIMIT = 56 * 1024 * 1024
NEG = -1e30
ATTN_TILE = 512

MIX_ORIG_W = 2724
O_BCQ, O_BCKV, O_BKR, O_CQ, O_CK, O_CV, O_CF, O_DU, O_DV = 1024, 1280, 1408, 1440, 1696, 1952, 2208, 2212, 2468
P_B, P_KR, P_CQ, P_CKV, P_D, P_CF = 1024, 1408, 1536, 2048, 3072, 3584


_DN = {"nn": (((1,), (0,)), ((), ())), "nt": (((1,), (1,)), ((), ())), "tn": (((0,), (0,)), ((), ()))}


def _raw_bdot(a, b, mode):
    return lax.dot_general(a.astype(BF16), b.astype(BF16), _DN[mode], preferred_element_type=F32)


@functools.partial(jax.custom_vjp, nondiff_argnums=(2,))
def _bdot(a, b, mode):
    return _raw_bdot(a, b, mode)


def _bdot_fwd(a, b, mode):
    return _raw_bdot(a, b, mode), (a, b)


def _bdot_bwd(mode, res, g):
    a, b = res
    if mode == "nn":
        return _raw_bdot(g, b, "nt"), _raw_bdot(a, g, "tn")
    if mode == "nt":
        return _raw_bdot(g, b, "nn"), _raw_bdot(g, a, "tn")
    return _raw_bdot(b, g, "nt"), _raw_bdot(a, g, "nn")


_bdot.defvjp(_bdot_fwd, _bdot_bwd)


def _cparams(sem):
    return pltpu.CompilerParams(dimension_semantics=sem, vmem_limit_bytes=VMEM_LIMIT)


def _mix_in_src():
    src = -np.ones(PACK_W, np.int64)
    src[0:P_KR] = np.arange(0, O_BKR)
    src[P_KR + 64:P_KR + 80] = O_BKR + np.arange(16)
    src[P_KR + 96:P_KR + 112] = O_BKR + 16 + np.arange(16)
    for h in range(N_HEADS):
        src[P_CQ + 128 * h:P_CQ + 128 * h + 64] = O_CQ + 64 * h + np.arange(64)
        src[P_CKV + 256 * h:P_CKV + 256 * h + 64] = O_CK + 64 * h + np.arange(64)
        src[P_CKV + 256 * h + 128:P_CKV + 256 * h + 192] = O_CV + 64 * h + np.arange(64)
    src[P_D:P_D + 512] = O_DU + np.arange(512)
    src[P_CF:P_CF + 4] = O_CF + np.arange(4)
    return src


def _uq_src():
    src = -np.ones(512, np.int64)
    for h in range(N_HEADS):
        src[128 * h:128 * h + 64] = 96 * h + np.arange(64)
        src[128 * h + 64:128 * h + 80] = 96 * h + 64 + np.arange(16)
        src[128 * h + 96:128 * h + 112] = 96 * h + 80 + np.arange(16)
    return src


def _ukv_src():
    src = -np.ones(1024, np.int64)
    for h in range(N_HEADS):
        src[256 * h:256 * h + 64] = 128 * h + np.arange(64)
        src[256 * h + 128:256 * h + 192] = 128 * h + 64 + np.arange(64)
    return src


def _mo_src():
    src = -np.ones(MO_W, np.int64)
    src[0:256] = np.arange(256)
    for g in range(2):
        for h in range(N_HEADS):
            src[256 + 512 * g + 128 * h:256 + 512 * g + 128 * h + 64] = 256 + 256 * g + 64 * h + np.arange(64)
    src[1280:1536] = 768 + np.arange(256)
    return src


def _runs(idx):
    runs, i = [], 0
    while i < len(idx):
        j = i + 1
        while j < len(idx) and ((idx[i] < 0 and idx[j] < 0) or (idx[i] >= 0 and idx[j] == idx[i] + j - i)):
            j += 1
        runs.append((int(idx[i]), j - i))
        i = j
    return runs


def _take_runs(w, idx):
    parts = [jnp.zeros(w.shape[:-1] + (n,), w.dtype) if s < 0 else lax.slice_in_dim(w, s, s + n, axis=w.ndim - 1)
             for s, n in _runs(idx)]
    return jnp.concatenate(parts, axis=-1)


def _pack_cols(w, src):
    return _take_runs(w, src)


def _unpack_cols(wp, src, n):
    dst = np.zeros(n, np.int64)
    dst[src[src >= 0]] = np.nonzero(src >= 0)[0]
    return _take_runs(wp, dst)


def _rope_tables(seq):
    half = B_ROPE // 2
    inv_freq = ROPE_THETA ** (-jnp.arange(half, dtype=F32) / half)
    ang = jnp.arange(seq).astype(F32)[:, None] * inv_freq[None, :]
    cos, sin = jnp.cos(ang), jnp.sin(ang)
    z16 = jnp.zeros((seq, 16), F32)
    c = jnp.concatenate([jnp.ones((seq, 64), F32), cos, z16, cos, z16], axis=1)
    s1 = jnp.concatenate([jnp.zeros((seq, 64), F32), -sin, z16, z16, z16], axis=1)
    s2 = jnp.concatenate([jnp.zeros((seq, 64), F32), z16, z16, sin, z16], axis=1)
    return c, s1, s2


def _matmul(a, b, *, mode, group_out, out_dtype, tm, tk, name):
    ga, gb = a.shape[0], b.shape[0]
    g_n = max(ga, gb)
    if mode == "tn":
        k_dim, m_dim = a.shape[1:]
    else:
        m_dim, k_dim = a.shape[1:]
    n_dim = b.shape[1] if mode == "nt" else b.shape[2]
    assert m_dim % tm == 0 and k_dim % tk == 0
    kt = k_dim // tk
    n_red = kt if group_out else g_n * kt
    g_out = g_n if group_out else 1

    def split(g, r):
        return (g, r) if group_out else (r // kt, r % kt)

    def a_map(g, i, r):
        gg, kk = split(g, r)
        gg = gg if ga > 1 else 0
        return (gg, kk, i) if mode == "tn" else (gg, i, kk)

    def b_map(g, i, r):
        gg, kk = split(g, r)
        gg = gg if gb > 1 else 0
        return (gg, 0, kk) if mode == "nt" else (gg, kk, 0)

    a_blk = (None, tk, tm) if mode == "tn" else (None, tm, tk)
    b_blk = (None, n_dim, tk) if mode == "nt" else (None, tk, n_dim)
    dn = _DN[mode]

    def body(a_ref, b_ref, o_ref, *scratch):
        part = lax.dot_general(a_ref[...].astype(BF16), b_ref[...].astype(BF16), dn, preferred_element_type=F32)
        if n_red == 1:
            o_ref[...] = part.astype(o_ref.dtype)
            return
        acc_ref, = scratch
        r = pl.program_id(2)

        @pl.when(r == 0)
        def _():
            acc_ref[...] = part

        @pl.when(r > 0)
        def _():
            acc_ref[...] += part

        @pl.when(r == n_red - 1)
        def _():
            o_ref[...] = acc_ref[...].astype(o_ref.dtype)

    return pl.pallas_call(
        body, name=name, grid=(g_out, m_dim // tm, n_red),
        in_specs=[pl.BlockSpec(a_blk, a_map), pl.BlockSpec(b_blk, b_map)],
        out_specs=pl.BlockSpec((None, tm, n_dim), lambda g, i, r: (g, i, 0)),
        out_shape=jax.ShapeDtypeStruct((g_out, m_dim, n_dim), out_dtype),
        scratch_shapes=[] if n_red == 1 else [pltpu.VMEM((tm, n_dim), F32)],
        compiler_params=_cparams(("parallel", "parallel", "arbitrary")),
    )(a, b)


def _row_spec(ts, d):
    return pl.BlockSpec((None, ts, d), lambda b, s: (b, s, 0))


def _mod_spec(d):
    return pl.BlockSpec((None, N_MOD, d), lambda b, s: (b, 0, 0))


def _vec_spec(d):
    return pl.BlockSpec((1, d), lambda b, s: (0, 0))


def _bvec_spec(d):
    return pl.BlockSpec((None, 1, d), lambda b, s: (b, 0, 0))


def _modulate(x, mod, sh_row, sc_row, name, ts=512):
    bsz, seq, d = x.shape

    def body(x_ref, mod_ref, o_ref):
        sh = mod_ref[sh_row:sh_row + 1, :]
        sc = mod_ref[sc_row:sc_row + 1, :]
        o_ref[...] = (x_ref[...] * (1.0 + sc) + sh).astype(o_ref.dtype)

    return pl.pallas_call(
        body, name=name, grid=(bsz, seq // ts),
        in_specs=[_row_spec(ts, d), _mod_spec(d)], out_specs=_row_spec(ts, d),
        out_shape=jax.ShapeDtypeStruct((bsz, seq, d), BF16),
        compiler_params=_cparams(("parallel", "parallel")),
    )(x, mod)


def _modulate_bwd(dh, x, mod, dx_res, sc_row, name, ts=512):
    bsz, seq, d = x.shape

    def body(dh_ref, x_ref, mod_ref, dxr_ref, dx_ref, dsh_ref, dsc_ref):
        s = pl.program_id(1)
        sc = mod_ref[sc_row:sc_row + 1, :]
        dh_v = dh_ref[...]
        dx_ref[...] = dxr_ref[...] + dh_v * (1.0 + sc)
        psh = jnp.sum(dh_v, axis=0, keepdims=True)
        psc = jnp.sum(dh_v * x_ref[...], axis=0, keepdims=True)

        @pl.when(s == 0)
        def _():
            dsh_ref[...] = psh
            dsc_ref[...] = psc

        @pl.when(s > 0)
        def _():
            dsh_ref[...] += psh
            dsc_ref[...] += psc

    return pl.pallas_call(
        body, name=name, grid=(bsz, seq // ts),
        in_specs=[_row_spec(ts, d), _row_spec(ts, d), _mod_spec(d), _row_spec(ts, d)],
        out_specs=[_row_spec(ts, d), _bvec_spec(d), _bvec_spec(d)],
        out_shape=[jax.ShapeDtypeStruct((bsz, seq, d), F32), jax.ShapeDtypeStruct((bsz, 1, d), F32),
                   jax.ShapeDtypeStruct((bsz, 1, d), F32)],
        compiler_params=_cparams(("parallel", "arbitrary")),
    )(dh, x, mod, dx_res)


def _res_ln_fn(x, f, g, lng, lnb, cmul):
    r = ALPHA * x + (cmul * (1.0 + g)) * f
    mu = jnp.mean(r, axis=-1, keepdims=True)
    rc = r - mu
    var = jnp.mean(rc * rc, axis=-1, keepdims=True)
    return rc * lax.rsqrt(var + LN_EPS) * lng + lnb


def _res_ln(x, f, mod, lng, lnb, g_row, cmul, name, nxt=None, ts=512):
    bsz, seq, d = x.shape

    def body(*refs):
        x_ref, f_ref, mod_ref, lng_ref, lnb_ref = refs[:5]
        g = mod_ref[g_row:g_row + 1, :]
        y = _res_ln_fn(x_ref[...], f_ref[...], g, lng_ref[...], lnb_ref[...], cmul)
        if nxt is None:
            refs[5][...] = y
            return
        nmod_ref, o_ref, h_ref = refs[5:]
        o_ref[...] = y
        sh = nmod_ref[nxt[1]:nxt[1] + 1, :]
        sc = nmod_ref[nxt[2]:nxt[2] + 1, :]
        h_ref[...] = (y * (1.0 + sc) + sh).astype(h_ref.dtype)

    in_specs = [_row_spec(ts, d), _row_spec(ts, d), _mod_spec(d), _vec_spec(d), _vec_spec(d)]
    args = [x, f, mod, lng, lnb]
    out_specs, out_shape = [_row_spec(ts, d)], [jax.ShapeDtypeStruct((bsz, seq, d), F32)]
    if nxt is not None:
        in_specs.append(_mod_spec(d))
        args.append(nxt[0])
        out_specs.append(_row_spec(ts, d))
        out_shape.append(jax.ShapeDtypeStruct((bsz, seq, d), BF16))
    res = pl.pallas_call(
        body, name=name, grid=(bsz, seq // ts), in_specs=in_specs, out_specs=out_specs, out_shape=out_shape,
        compiler_params=_cparams(("parallel", "parallel")),
    )(*args)
    return (res[0], res[1]) if nxt is not None else (res[0], None)


def _res_ln_bwd(dy, x, f, mod, lng, lnb, g_row, cmul, name, ts=256):
    bsz, seq, d = x.shape

    def body(dy_ref, x_ref, f_ref, mod_ref, lng_ref, lnb_ref, dx_ref, df_ref, dg_ref, dlg_ref, dlb_ref):
        b, s = pl.program_id(0), pl.program_id(1)
        g = mod_ref[g_row:g_row + 1, :]
        _, vjp = jax.vjp(functools.partial(_res_ln_fn, cmul=cmul), x_ref[...], f_ref[...], g, lng_ref[...],
                         lnb_ref[...])
        dx, df, dg, dlg, dlb = vjp(dy_ref[...])
        dx_ref[...] = dx
        df_ref[...] = df.astype(df_ref.dtype)

        @pl.when(s == 0)
        def _():
            dg_ref[...] = dg

        @pl.when(s > 0)
        def _():
            dg_ref[...] += dg

        first = jnp.logical_and(b == 0, s == 0)

        @pl.when(first)
        def _():
            dlg_ref[...] = dlg
            dlb_ref[...] = dlb

        @pl.when(jnp.logical_not(first))
        def _():
            dlg_ref[...] += dlg
            dlb_ref[...] += dlb

    return pl.pallas_call(
        body, name=name, grid=(bsz, seq // ts),
        in_specs=[_row_spec(ts, d), _row_spec(ts, d), _row_spec(ts, d), _mod_spec(d), _vec_spec(d), _vec_spec(d)],
        out_specs=[_row_spec(ts, d), _row_spec(ts, d), _bvec_spec(d), _vec_spec(d), _vec_spec(d)],
        out_shape=[jax.ShapeDtypeStruct((bsz, seq, d), F32), jax.ShapeDtypeStruct((bsz, seq, d), BF16),
                   jax.ShapeDtypeStruct((bsz, 1, d), F32), jax.ShapeDtypeStruct((1, d), F32),
                   jax.ShapeDtypeStruct((1, d), F32)],
        compiler_params=_cparams(("arbitrary", "arbitrary")),
    )(dy, x, f, mod, lng, lnb)


def _loss_head(y, target, name, ts=512):
    bsz, seq, d = y.shape
    n_s = seq // ts

    def body(y_ref, t_ref, dy_ref, loss_ref, acc_ref):
        b, s = pl.program_id(0), pl.program_id(1)
        err = y_ref[...] - t_ref[...]
        dy_ref[...] = err * (1.0 / d)
        part = jnp.sum(err * err, axis=0, keepdims=True)
        first = jnp.logical_and(b == 0, s == 0)

        @pl.when(first)
        def _():
            acc_ref[...] = part

        @pl.when(jnp.logical_not(first))
        def _():
            acc_ref[...] += part

        @pl.when(jnp.logical_and(b == bsz - 1, s == n_s - 1))
        def _():
            loss_ref[...] = jnp.sum(acc_ref[...], axis=1, keepdims=True) * (0.5 / d)

    return pl.pallas_call(
        body, name=name, grid=(bsz, n_s),
        in_specs=[_row_spec(ts, d), _row_spec(ts, d)],
        out_specs=[_row_spec(ts, d), pl.BlockSpec((1, 1), lambda b, s: (0, 0))],
        out_shape=[jax.ShapeDtypeStruct((bsz, seq, d), F32), jax.ShapeDtypeStruct((1, 1), F32)],
        scratch_shapes=[pltpu.VMEM((1, d), F32)],
        compiler_params=_cparams(("arbitrary", "arbitrary")),
    )(y, target)


def _ffn_in_swiglu(h, w_in_t, name, tm=1024):
    t, d = h.shape
    n_sh, w, _ = w_in_t.shape
    half = n_sh // 2

    def body(h_ref, w_ref, z_ref, a_ref):
        hv = h_ref[...]
        g = lax.dot_general(hv, w_ref[0], _DN["nt"], preferred_element_type=F32)
        u = lax.dot_general(hv, w_ref[1], _DN["nt"], preferred_element_type=F32)
        z_ref[0] = g.astype(z_ref.dtype)
        z_ref[1] = u.astype(z_ref.dtype)
        a_ref[...] = (g * jax.nn.sigmoid(g) * u).astype(a_ref.dtype)

    return pl.pallas_call(
        body, name=name, grid=(half, t // tm),
        in_specs=[pl.BlockSpec((tm, d), lambda g, i: (i, 0)),
                  pl.BlockSpec((2, None, w, d), lambda g, i: (0, g, 0, 0))],
        out_specs=[pl.BlockSpec((2, None, tm, w), lambda g, i: (0, g, i, 0)),
                   pl.BlockSpec((None, tm, w), lambda g, i: (g, i, 0))],
        out_shape=[jax.ShapeDtypeStruct((2, half, t, w), BF16), jax.ShapeDtypeStruct((half, t, w), BF16)],
        compiler_params=_cparams(("parallel", "parallel")),
    )(h, w_in_t.reshape(2, half, w, d))


def _ffn_out_dx_swiglu(df, w_out, z, name, tm=1024):
    t, d = df.shape
    half, w, _ = w_out.shape

    def body(df_ref, w_ref, z_ref, dz_ref):
        da = lax.dot_general(df_ref[...], w_ref[...], _DN["nt"], preferred_element_type=F32)
        g = z_ref[0].astype(F32)
        u = z_ref[1].astype(F32)
        sig = jax.nn.sigmoid(g)
        dz_ref[0] = (da * u * (sig * (1.0 + g * (1.0 - sig)))).astype(dz_ref.dtype)
        dz_ref[1] = (da * (g * sig)).astype(dz_ref.dtype)

    zspec = pl.BlockSpec((2, None, tm, w), lambda g, i: (0, g, i, 0))
    return pl.pallas_call(
        body, name=name, grid=(half, t // tm),
        in_specs=[pl.BlockSpec((tm, d), lambda g, i: (i, 0)), pl.BlockSpec((None, w, d), lambda g, i: (g, 0, 0)),
                  zspec],
        out_specs=zspec, out_shape=jax.ShapeDtypeStruct(z.shape, BF16),
        compiler_params=_cparams(("parallel", "parallel")),
    )(df, w_out, z)


def _log_sigmoid(x):
    return jnp.minimum(x, 0.0) - jnp.log(1.0 + jnp.exp(-jnp.abs(x)))


def _hgrn_consts():
    r = lax.broadcasted_iota(jnp.int32, (GROUP_WIDTH, GROUP_WIDTH), 0)
    c = lax.broadcasted_iota(jnp.int32, (GROUP_WIDTH, GROUP_WIDTH), 1)
    bd = (r // HEAD_DIM == c // HEAD_DIM).astype(F32)
    r16 = lax.broadcasted_iota(jnp.int32, (A_CHUNK, A_CHUNK), 0)
    c16 = lax.broadcasted_iota(jnp.int32, (A_CHUNK, A_CHUNK), 1)
    tril = (r16 >= c16).astype(F32)
    rows = lax.broadcasted_iota(jnp.int32, (A_CHUNK, GROUP_WIDTH), 0)
    return bd, tril, rows


def _hgrn_lb(logits8, layer):
    rows = lax.broadcasted_iota(jnp.int32, logits8.shape, 0)
    valid = rows < DEPTH
    mx = jnp.max(jnp.where(valid, logits8, NEG), axis=0, keepdims=True)
    e = jnp.where(valid, jnp.exp(logits8 - mx), 0.0)
    sm = e / jnp.sum(e, axis=0, keepdims=True)
    pick = jnp.logical_and(rows >= 1, rows <= layer)
    return jnp.sum(jnp.where(pick, sm, 0.0), axis=0, keepdims=True)


def _hgrn_chunk(aq, af, ai, ag, logits8, norm_g, st, *, layer, consts):
    bd, tril, rows = consts
    lb = _hgrn_lb(logits8, layer)
    la = jnp.log(jnp.maximum(lb, LB_FLOOR))
    b2 = jnp.log(1.0 - lb) + _log_sigmoid(af)
    log_f = jnp.maximum(la, b2) + jnp.log(1.0 + jnp.exp(-jnp.abs(la - b2)))
    k = 1.0 - jnp.exp(log_f)
    qf = aq * jax.nn.sigmoid(aq)
    g_cum = jnp.dot(tril, log_f, precision=HI, preferred_element_type=F32)

    c, w = A_CHUNK, GROUP_WIDTH

    def by_key(v):
        return jnp.broadcast_to(v[:, None, :], (c, c, w))

    def by_query(v):
        return jnp.broadcast_to(v[None, :, :], (c, c, w))

    s_i = lax.broadcasted_iota(jnp.int32, (c, c, w), 0)
    t_i = lax.broadcasted_iota(jnp.int32, (c, c, w), 1)
    rel = jnp.where(t_i >= s_i, by_query(g_cum) - by_key(g_cum), NEG)
    pairs = by_query(qf) * by_key(k) * jnp.exp(rel)
    a_all = _bdot(pairs.reshape(c * c, w), bd, "nn").reshape(c, c, w)
    o = jnp.sum(a_all * by_key(ai), axis=0)
    q_dec = qf * jnp.exp(g_cum)
    o = o + _bdot(q_dec, st, "nt")
    g_last = jnp.sum(jnp.where(rows == c - 1, g_cum, 0.0), axis=0, keepdims=True)
    k_end = k * jnp.exp(g_last - g_cum)
    kv = _bdot(ai, k_end, "tn")
    st_new = st * jnp.exp(g_last) + kv * bd
    ms = _bdot(o * o, bd, "nn") * (1.0 / HEAD_DIM)
    o = o * lax.rsqrt(ms + RMS_EPS) * norm_g
    return o * (ag * jax.nn.sigmoid(ag)), st_new


def _hgrn_fwd(proj, logits8, norm_g, layer, name, ts=256):
    bsz, seq, _ = proj.shape
    n_ch = ts // A_CHUNK

    def body(p_ref, lg_ref, ng_ref, o_ref, st_ref, st_scr):
        @pl.when(pl.program_id(1) == 0)
        def _():
            st_scr[...] = jnp.zeros_like(st_scr)

        consts = _hgrn_consts()
        logits_v, ng_v = lg_ref[...], ng_ref[...]

        def chunk(ci, carry):
            r = pl.multiple_of(ci * A_CHUNK, A_CHUNK)
            st = st_scr[...]
            st_ref[ci] = st
            o, st_new = _hgrn_chunk(
                p_ref[pl.ds(r, A_CHUNK), 0:256], p_ref[pl.ds(r, A_CHUNK), 256:512],
                p_ref[pl.ds(r, A_CHUNK), 512:768], p_ref[pl.ds(r, A_CHUNK), 768:1024],
                logits_v, ng_v, st, layer=layer, consts=consts)
            o_ref[pl.ds(r, A_CHUNK), :] = o.astype(o_ref.dtype)
            st_scr[...] = st_new
            return carry

        lax.fori_loop(0, n_ch, chunk, 0, unroll=2)

    return pl.pallas_call(
        body, name=name, grid=(bsz, seq // ts),
        in_specs=[pl.BlockSpec((None, ts, 1024), lambda b, s: (b, s, 0)),
                  pl.BlockSpec((8, GROUP_WIDTH), lambda b, s: (0, 0)),
                  pl.BlockSpec((1, GROUP_WIDTH), lambda b, s: (0, 0))],
        out_specs=[pl.BlockSpec((None, ts, GROUP_WIDTH), lambda b, s: (b, s, 0)),
                   pl.BlockSpec((None, n_ch, GROUP_WIDTH, GROUP_WIDTH), lambda b, s: (b, s, 0, 0))],
        out_shape=[jax.ShapeDtypeStruct((bsz, seq, MO_W), BF16),
                   jax.ShapeDtypeStruct((bsz, seq // A_CHUNK, GROUP_WIDTH, GROUP_WIDTH), F32)],
        scratch_shapes=[pltpu.VMEM((GROUP_WIDTH, GROUP_WIDTH), F32)],
        compiler_params=_cparams(("parallel", "arbitrary")),
    )(proj, logits8, norm_g)


def _hgrn_bwd(dmo, proj, states, logits8, norm_g, layer, name, ts=256):
    bsz, seq, _ = proj.shape
    n_ch = ts // A_CHUNK
    n_s = seq // ts

    def body(do_ref, p_ref, st_ref, lg_ref, ng_ref, dp_ref, dlg_ref, dng_ref, dst_scr):
        b, s = pl.program_id(0), pl.program_id(1)

        @pl.when(s == 0)
        def _():
            dst_scr[...] = jnp.zeros_like(dst_scr)

        @pl.when(jnp.logical_and(b == 0, s == 0))
        def _():
            dlg_ref[...] = jnp.zeros_like(dlg_ref)
            dng_ref[...] = jnp.zeros_like(dng_ref)

        consts = _hgrn_consts()
        logits_v, ng_v = lg_ref[...], ng_ref[...]
        fn = functools.partial(_hgrn_chunk, layer=layer, consts=consts)

        def chunk(t, carry):
            ci = n_ch - 1 - t
            r = pl.multiple_of(ci * A_CHUNK, A_CHUNK)
            _, vjp = jax.vjp(
                fn, p_ref[pl.ds(r, A_CHUNK), 0:256], p_ref[pl.ds(r, A_CHUNK), 256:512],
                p_ref[pl.ds(r, A_CHUNK), 512:768], p_ref[pl.ds(r, A_CHUNK), 768:1024],
                logits_v, ng_v, st_ref[ci])
            daq, daf, dai, dag, dlg, dng, dst = vjp((do_ref[pl.ds(r, A_CHUNK), :], dst_scr[...]))
            dp_ref[pl.ds(r, A_CHUNK), 0:256] = daq.astype(dp_ref.dtype)
            dp_ref[pl.ds(r, A_CHUNK), 256:512] = daf.astype(dp_ref.dtype)
            dp_ref[pl.ds(r, A_CHUNK), 512:768] = dai.astype(dp_ref.dtype)
            dp_ref[pl.ds(r, A_CHUNK), 768:1024] = dag.astype(dp_ref.dtype)
            dlg_ref[...] += dlg
            dng_ref[...] += dng
            dst_scr[...] = dst
            return carry

        lax.fori_loop(0, n_ch, chunk, 0, unroll=2)

    rev = lambda b, s: (b, n_s - 1 - s, 0)
    return pl.pallas_call(
        body, name=name, grid=(bsz, n_s),
        in_specs=[pl.BlockSpec((None, ts, GROUP_WIDTH), rev),
                  pl.BlockSpec((None, ts, 1024), rev),
                  pl.BlockSpec((None, n_ch, GROUP_WIDTH, GROUP_WIDTH), lambda b, s: (b, n_s - 1 - s, 0, 0)),
                  pl.BlockSpec((8, GROUP_WIDTH), lambda b, s: (0, 0)),
                  pl.BlockSpec((1, GROUP_WIDTH), lambda b, s: (0, 0))],
        out_specs=[pl.BlockSpec((None, ts, 1024), rev),
                   pl.BlockSpec((8, GROUP_WIDTH), lambda b, s: (0, 0)),
                   pl.BlockSpec((1, GROUP_WIDTH), lambda b, s: (0, 0))],
        out_shape=[jax.ShapeDtypeStruct((bsz, seq, PACK_W), BF16),
                   jax.ShapeDtypeStruct((8, GROUP_WIDTH), F32), jax.ShapeDtypeStruct((1, GROUP_WIDTH), F32)],
        scratch_shapes=[pltpu.VMEM((GROUP_WIDTH, GROUP_WIDTH), F32)],
        compiler_params=_cparams(("arbitrary", "arbitrary")),
    )(dmo, proj, states, logits8, norm_g)


def _rms_fn(x, g):
    return x * lax.rsqrt(jnp.mean(x * x, axis=-1, keepdims=True) + RMS_EPS) * g


def _tile4(t):
    return jnp.concatenate([t, t, t, t], axis=1)


def _rope(x, c, s1, s2):
    w = x.shape[-1]
    return x * c + pltpu.roll(x, 32, axis=1) * s2 + pltpu.roll(x, w - 32, axis=1) * s1


def _rope_t(dy, c, s1, s2):
    w = dy.shape[-1]
    return dy * c + pltpu.roll(dy * s2, w - 32, axis=1) + pltpu.roll(dy * s1, 32, axis=1)


def _mla_pre(proj, qg, kvg, wq, wkv, tabs, name, ts=256):
    bsz, seq, _ = proj.shape

    def body(p_ref, qg_ref, kvg_ref, wq_ref, wkv_ref, c_ref, s1_ref, s2_ref, q_ref, kv_ref):
        nq = _rms_fn(p_ref[:, 0:256], qg_ref[...])
        nkv = _rms_fn(p_ref[:, 256:384], kvg_ref[...])
        c, s1, s2 = c_ref[...], s1_ref[...], s2_ref[...]
        qp = jnp.dot(nq.astype(BF16), wq_ref[...], preferred_element_type=F32)
        q_ref[...] = _rope(qp, _tile4(c), _tile4(s1), _tile4(s2)).astype(q_ref.dtype)
        kv = jnp.dot(nkv.astype(BF16), wkv_ref[...], preferred_element_type=F32)
        krr = _rope(p_ref[:, 384:512], c, s1, s2)
        zero = jnp.zeros_like(krr)
        kv_ref[...] = (kv + jnp.concatenate([krr, zero] * N_HEADS, axis=1)).astype(kv_ref.dtype)

    tab_spec = pl.BlockSpec((ts, LANES), lambda b, s: (s, 0))
    return pl.pallas_call(
        body, name=name, grid=(bsz, seq // ts),
        in_specs=[pl.BlockSpec((None, ts, 512), lambda b, s: (b, s, P_B // 512)),
                  _vec_spec(256), _vec_spec(128),
                  pl.BlockSpec((256, 512), lambda b, s: (0, 0)), pl.BlockSpec((128, 1024), lambda b, s: (0, 0)),
                  tab_spec, tab_spec, tab_spec],
        out_specs=[_row_spec(ts, 512), _row_spec(ts, 1024)],
        out_shape=[jax.ShapeDtypeStruct((bsz, seq, 512), BF16), jax.ShapeDtypeStruct((bsz, seq, 1024), BF16)],
        compiler_params=_cparams(("parallel", "parallel")),
    )(proj, qg, kvg, wq, wkv, *tabs)


def _mla_pre_bwd(dq, dkv, dproj, proj, qg, kvg, wq, wkv, tabs, name, ts=256):
    bsz, seq, _ = proj.shape

    def body(dq_ref, dkv_ref, dp_any, p_ref, qg_ref, kvg_ref, wq_ref, wkv_ref, c_ref, s1_ref, s2_ref,
             dp_ref, dqg_ref, dkvg_ref, dwq_ref, dwkv_ref):
        del dp_any
        first = jnp.logical_and(pl.program_id(0) == 0, pl.program_id(1) == 0)

        @pl.when(first)
        def _():
            dqg_ref[...] = jnp.zeros_like(dqg_ref)
            dkvg_ref[...] = jnp.zeros_like(dkvg_ref)
            dwq_ref[...] = jnp.zeros_like(dwq_ref)
            dwkv_ref[...] = jnp.zeros_like(dwkv_ref)

        c, s1, s2 = c_ref[...], s1_ref[...], s2_ref[...]
        nq, vjp_q = jax.vjp(_rms_fn, p_ref[:, 0:256], qg_ref[...])
        nkv, vjp_kv = jax.vjp(_rms_fn, p_ref[:, 256:384], kvg_ref[...])
        dqp = _rope_t(dq_ref[...], _tile4(c), _tile4(s1), _tile4(s2)).astype(BF16)
        dkv_v = dkv_ref[...]
        dkv_b = dkv_v.astype(BF16)
        tn = (((0,), (0,)), ((), ()))
        nt = (((1,), (1,)), ((), ()))
        dwq_ref[...] += lax.dot_general(nq.astype(BF16), dqp, tn, preferred_element_type=F32)
        dwkv_ref[...] += lax.dot_general(nkv.astype(BF16), dkv_b, tn, preferred_element_type=F32)
        dcq, dqg = vjp_q(lax.dot_general(dqp, wq_ref[...], nt, preferred_element_type=F32))
        dckv, dkvg = vjp_kv(lax.dot_general(dkv_b, wkv_ref[...], nt, preferred_element_type=F32))
        dqg_ref[...] += dqg
        dkvg_ref[...] += dkvg
        dk_sum = dkv_v[:, 0:128] + dkv_v[:, 256:384] + dkv_v[:, 512:640] + dkv_v[:, 768:896]
        lane = lax.broadcasted_iota(jnp.int32, dk_sum.shape, 1)
        dkr = jnp.where(lane >= 64, _rope_t(dk_sum, c, s1, s2), 0.0)
        dp_ref[:, 0:256] = dcq.astype(dp_ref.dtype)
        dp_ref[:, 256:384] = dckv.astype(dp_ref.dtype)
        dp_ref[:, 384:512] = dkr.astype(dp_ref.dtype)

    tab_spec = pl.BlockSpec((ts, LANES), lambda b, s: (s, 0))
    const = lambda shape: pl.BlockSpec(shape, lambda b, s: (0, 0))
    return pl.pallas_call(
        body, name=name, grid=(bsz, seq // ts),
        in_specs=[_row_spec(ts, 512), _row_spec(ts, 1024), pl.BlockSpec(memory_space=pl.ANY),
                  pl.BlockSpec((None, ts, 512), lambda b, s: (b, s, P_B // 512)),
                  _vec_spec(256), _vec_spec(128), const((256, 512)), const((128, 1024)),
                  tab_spec, tab_spec, tab_spec],
        out_specs=[pl.BlockSpec((None, ts, 512), lambda b, s: (b, s, P_B // 512)),
                   _vec_spec(256), _vec_spec(128), const((256, 512)), const((128, 1024))],
        out_shape=[jax.ShapeDtypeStruct(dproj.shape, dproj.dtype), jax.ShapeDtypeStruct((1, 256), F32),
                   jax.ShapeDtypeStruct((1, 128), F32), jax.ShapeDtypeStruct((256, 512), F32),
                   jax.ShapeDtypeStruct((128, 1024), F32)],
        input_output_aliases={2: 0},
        compiler_params=_cparams(("arbitrary", "arbitrary")),
    )(dq, dkv, dproj, proj, qg, kvg, wq, wkv, *tabs)


def _fox_gate(proj, bf, name):
    bsz, seq, _ = proj.shape
    n_blk = seq // LANES

    def body(x_ref, bf_ref, f_ref):
        r_i = lax.broadcasted_iota(jnp.int32, (LANES, LANES), 0)
        c_i = lax.broadcasted_iota(jnp.int32, (LANES, LANES), 1)
        tril = (r_i >= c_i).astype(F32)
        bias = bf_ref[...]

        def blk(i, carry):
            r = pl.multiple_of(i * LANES, LANES)
            lf = _log_sigmoid(x_ref[pl.ds(r, LANES), :] + bias)
            f_ref[pl.ds(r, LANES), :] = jnp.dot(tril, lf, precision=HI, preferred_element_type=F32) + carry
            return carry + jnp.sum(lf, axis=0, keepdims=True)

        lax.fori_loop(0, n_blk, blk, jnp.zeros((1, LANES), F32))

    return pl.pallas_call(
        body, name=name, grid=(bsz,),
        in_specs=[pl.BlockSpec((None, seq, LANES), lambda b: (b, 0, P_CF // LANES)),
                  pl.BlockSpec((1, LANES), lambda b: (0, 0))],
        out_specs=pl.BlockSpec((None, seq, LANES), lambda b: (b, 0, 0)),
        out_shape=jax.ShapeDtypeStruct((bsz, seq, LANES), F32),
        compiler_params=_cparams(("parallel",)),
    )(proj, bf)


def _fox_gate_bwd(dfq, dfk_cols, dproj, proj, bf, name):
    bsz, seq, _ = proj.shape
    n_blk = seq // LANES

    def body(dfq_ref, dfk_ref, dp_any, x_ref, bf_ref, dp_ref, dbf_ref):
        del dp_any

        @pl.when(pl.program_id(0) == 0)
        def _():
            dbf_ref[...] = jnp.zeros_like(dbf_ref)

        r_i = lax.broadcasted_iota(jnp.int32, (LANES, LANES), 0)
        c_i = lax.broadcasted_iota(jnp.int32, (LANES, LANES), 1)
        triu = (r_i <= c_i).astype(F32)
        bias = bf_ref[...]

        def blk(t, carry):
            tail, dbf = carry
            r = pl.multiple_of((n_blk - 1 - t) * LANES, LANES)
            dc = dfk_ref[pl.ds(r, LANES), :]
            for hd in range(N_HEADS):
                dc = dc + jnp.where(c_i == hd, dfq_ref[hd, pl.ds(r, LANES), :], 0.0)
            dlf = jnp.dot(triu, dc, precision=HI, preferred_element_type=F32) + tail
            dx = dlf * (1.0 - jax.nn.sigmoid(x_ref[pl.ds(r, LANES), :] + bias))
            dp_ref[pl.ds(r, LANES), :] = dx.astype(dp_ref.dtype)
            return tail + jnp.sum(dc, axis=0, keepdims=True), dbf + jnp.sum(dx, axis=0, keepdims=True)

        z = jnp.zeros((1, LANES), F32)
        _, dbf = lax.fori_loop(0, n_blk, blk, (z, z))
        dbf_ref[...] += dbf

    return pl.pallas_call(
        body, name=name, grid=(bsz,),
        in_specs=[pl.BlockSpec((None, N_HEADS, seq, LANES), lambda b: (b, 0, 0, 0)),
                  pl.BlockSpec((None, seq, LANES), lambda b: (b, 0, 0)), pl.BlockSpec(memory_space=pl.ANY),
                  pl.BlockSpec((None, seq, LANES), lambda b: (b, 0, P_CF // LANES)),
                  pl.BlockSpec((1, LANES), lambda b: (0, 0))],
        out_specs=[pl.BlockSpec((None, seq, LANES), lambda b: (b, 0, P_CF // LANES)),
                   pl.BlockSpec((1, LANES), lambda b: (0, 0))],
        out_shape=[jax.ShapeDtypeStruct(dproj.shape, dproj.dtype), jax.ShapeDtypeStruct((1, LANES), F32)],
        input_output_aliases={2: 0},
        compiler_params=_cparams(("arbitrary",)),
    )(dfq, dfk_cols, dproj, proj, bf)


def _gate_terms(fc_ref, fr_ref, h, tq, tk):
    lane = lax.broadcasted_iota(jnp.int32, (tq, LANES), 1)
    fcol = jnp.sum(jnp.where(lane == h, fc_ref[...], 0.0), axis=1, keepdims=True)
    sub = lax.broadcasted_iota(jnp.int32, (8, tk), 0)
    frow = jnp.sum(jnp.where(sub == h, fr_ref[...], 0.0), axis=0, keepdims=True)
    return fcol - frow


def _scores(q_ref, k_ref, gate_refs, scale, h, masked, tq, tk):
    q = (q_ref[...].astype(F32) * scale).astype(BF16)
    s = lax.dot_general(q, k_ref[...].astype(BF16), _DN["nt"], preferred_element_type=F32)
    if gate_refs is not None:
        s = s + _gate_terms(gate_refs[0], gate_refs[1], h, tq, tk)
    if masked is not False:
        r_i = lax.broadcasted_iota(jnp.int32, (tq, tk), 0)
        c_i = lax.broadcasted_iota(jnp.int32, (tq, tk), 1)
        keep = c_i <= r_i
        s = jnp.where(keep if masked is True else jnp.logical_or(jnp.logical_not(masked), keep), s, NEG)
    return s, q


def _lanes(col):
    return jnp.broadcast_to(col, (col.shape[0], LANES))


def _attn_fwd(qa, q0, kva, kv0, mo, o0, gates, scale, name, tq=None):
    bsz, seq, _ = qa.shape
    tq = ATTN_TILE if tq is None else tq
    n_q = seq // tq
    gated = gates is not None

    def body(*refs):
        q_ref, k_ref, v_ref = refs[:3]
        gate_refs = refs[3:5] if gated else None
        o_ref, lse_ref, m_s, l_s, acc_s = refs[-5:]
        h, i, j = pl.program_id(1), pl.program_id(2), pl.program_id(3)

        @pl.when(j == 0)
        def _():
            m_s[...] = jnp.full_like(m_s, NEG)
            l_s[...] = jnp.zeros_like(l_s)
            acc_s[...] = jnp.zeros_like(acc_s)

        def step(masked):
            s, _ = _scores(q_ref, k_ref, gate_refs, scale, h, masked, tq, tq)
            m_prev = m_s[...]
            m_new = jnp.maximum(m_prev, jnp.max(s, axis=1, keepdims=True))
            alpha = jnp.exp(m_prev - m_new)
            p = jnp.exp(s - m_new)
            l_s[...] = alpha * l_s[...] + jnp.sum(p, axis=1, keepdims=True)
            acc_s[...] = alpha * acc_s[...] + jnp.dot(p.astype(BF16), v_ref[...].astype(BF16),
                                                      preferred_element_type=F32)
            m_s[...] = m_new

        @pl.when(j <= i)
        def _():
            step(j == i)

        @pl.when(j == i)
        def _():
            o_ref[...] = (acc_s[...] / l_s[...]).astype(o_ref.dtype)
            lse_ref[...] = _lanes(m_s[...] + jnp.log(l_s[...]))

    blk = (None, tq, LANES)
    in_specs = [pl.BlockSpec(blk, lambda b, h, i, j: (b, i, q0 + h)),
                pl.BlockSpec(blk, lambda b, h, i, j: (b, jnp.minimum(j, i), kv0 + 2 * h)),
                pl.BlockSpec(blk, lambda b, h, i, j: (b, jnp.minimum(j, i), kv0 + 2 * h + 1))]
    args = [qa, kva, kva]
    if gated:
        in_specs += [pl.BlockSpec(blk, lambda b, h, i, j: (b, i, 0)),
                     pl.BlockSpec((None, 8, tq), lambda b, h, i, j: (b, 0, jnp.minimum(j, i)))]
        args += list(gates)
    in_specs.append(pl.BlockSpec(memory_space=pl.ANY))
    args.append(mo)
    return pl.pallas_call(
        body, name=name, grid=(bsz, N_HEADS, n_q, n_q), in_specs=in_specs,
        out_specs=[pl.BlockSpec(blk, lambda b, h, i, j: (b, i, o0 + h)),
                   pl.BlockSpec((None, None, tq, LANES), lambda b, h, i, j: (b, h, i, 0))],
        out_shape=[jax.ShapeDtypeStruct(mo.shape, mo.dtype),
                   jax.ShapeDtypeStruct((bsz, N_HEADS, seq, LANES), F32)],
        scratch_shapes=[pltpu.VMEM((tq, 1), F32), pltpu.VMEM((tq, 1), F32), pltpu.VMEM((tq, LANES), F32)],
        input_output_aliases={len(args) - 1: 0},
        compiler_params=_cparams(("parallel", "parallel", "parallel", "arbitrary")),
    )(*args)


def _attn_bwd_q(qa, q0, kva, kv0, mo, dmo, o0, lse, gates, scale, out, out0, name, tq=None):
    bsz, seq, _ = qa.shape
    tq = ATTN_TILE if tq is None else tq
    n_q = seq // tq
    gated = gates is not None
    aliased = not isinstance(out, jax.ShapeDtypeStruct)

    def body(*refs):
        q_ref, k_ref, v_ref, o_ref, do_ref, lse_ref = refs[:6]
        gate_refs = refs[6:8] if gated else None
        dq_ref, delta_ref, dfq_ref, acc_s, dl_s, df_s = refs[-6:]
        h, i, j = pl.program_id(1), pl.program_id(2), pl.program_id(3)

        @pl.when(j == 0)
        def _():
            acc_s[...] = jnp.zeros_like(acc_s)
            df_s[...] = jnp.zeros_like(df_s)
            dl_s[...] = jnp.sum(do_ref[...] * o_ref[...].astype(F32), axis=1, keepdims=True)

        def step(masked):
            s, _ = _scores(q_ref, k_ref, gate_refs, scale, h, masked, tq, tq)
            p = jnp.exp(s - lse_ref[:, 0:1])
            dp = lax.dot_general(do_ref[...].astype(BF16), v_ref[...].astype(BF16), _DN["nt"],
                                 preferred_element_type=F32)
            ds = p * (dp - dl_s[...])
            acc_s[...] += jnp.dot(ds.astype(BF16), k_ref[...].astype(BF16), preferred_element_type=F32)
            df_s[...] += jnp.sum(ds, axis=1, keepdims=True)

        @pl.when(j <= i)
        def _():
            step(j == i)

        @pl.when(j == i)
        def _():
            dq_ref[...] = (acc_s[...] * scale).astype(dq_ref.dtype)
            delta_ref[...] = _lanes(dl_s[...])
            dfq_ref[...] = _lanes(df_s[...])

    blk = (None, tq, LANES)
    col = pl.BlockSpec((None, None, tq, LANES), lambda b, h, i, j: (b, h, i, 0))
    in_specs = [pl.BlockSpec(blk, lambda b, h, i, j: (b, i, q0 + h)),
                pl.BlockSpec(blk, lambda b, h, i, j: (b, jnp.minimum(j, i), kv0 + 2 * h)),
                pl.BlockSpec(blk, lambda b, h, i, j: (b, jnp.minimum(j, i), kv0 + 2 * h + 1)),
                pl.BlockSpec(blk, lambda b, h, i, j: (b, i, o0 + h)),
                pl.BlockSpec(blk, lambda b, h, i, j: (b, i, o0 + h)), col]
    args = [qa, kva, kva, mo, dmo, lse]
    if gated:
        in_specs += [pl.BlockSpec(blk, lambda b, h, i, j: (b, i, 0)),
                     pl.BlockSpec((None, 8, tq), lambda b, h, i, j: (b, 0, jnp.minimum(j, i)))]
        args += list(gates)
    aliases = {}
    if aliased:
        in_specs.append(pl.BlockSpec(memory_space=pl.ANY))
        args.append(out)
        aliases = {len(args) - 1: 0}
    vec = jax.ShapeDtypeStruct((bsz, N_HEADS, seq, LANES), F32)
    return pl.pallas_call(
        body, name=name, grid=(bsz, N_HEADS, n_q, n_q), in_specs=in_specs,
        out_specs=[pl.BlockSpec(blk, lambda b, h, i, j: (b, i, out0 + h)), col, col],
        out_shape=[jax.ShapeDtypeStruct(out.shape, out.dtype), vec, vec],
        scratch_shapes=[pltpu.VMEM((tq, LANES), F32), pltpu.VMEM((tq, 1), F32), pltpu.VMEM((tq, 1), F32)],
        input_output_aliases=aliases,
        compiler_params=_cparams(("parallel", "parallel", "parallel", "arbitrary")),
    )(*args)


def _attn_bwd_kv(qa, q0, kva, kv0, dmo, o0, lse, delta, gates, scale, out, out0, name, tq=None):
    bsz, seq, _ = qa.shape
    tq = ATTN_TILE if tq is None else tq
    n_q = seq // tq
    gated = gates is not None
    aliased = not isinstance(out, jax.ShapeDtypeStruct)

    def body(*refs):
        q_ref, k_ref, v_ref, do_ref, lse_ref, dl_ref = refs[:6]
        gate_refs = refs[6:8] if gated else None
        dkv_ref, dfk_ref, dk_s, dv_s, df_s = refs[-5:]
        h, j, i = pl.program_id(1), pl.program_id(2), pl.program_id(3)

        @pl.when(i == 0)
        def _():
            dk_s[...] = jnp.zeros_like(dk_s)
            dv_s[...] = jnp.zeros_like(dv_s)
            df_s[...] = jnp.zeros_like(df_s)

        def step(masked):
            s, q = _scores(q_ref, k_ref, gate_refs, scale, h, masked, tq, tq)
            p = jnp.exp(s - lse_ref[:, 0:1])
            do_b = do_ref[...].astype(BF16)
            dp = lax.dot_general(do_b, v_ref[...].astype(BF16), _DN["nt"], preferred_element_type=F32)
            ds = p * (dp - dl_ref[:, 0:1])
            dv_s[...] += lax.dot_general(p.astype(BF16), do_b, _DN["tn"], preferred_element_type=F32)
            dk_s[...] += lax.dot_general(ds.astype(BF16), q, _DN["tn"], preferred_element_type=F32)
            df_s[...] -= jnp.sum(ds, axis=0, keepdims=True)

        @pl.when(i > j)
        def _():
            step(False)

        @pl.when(i == j)
        def _():
            step(True)

        @pl.when(i == n_q - 1)
        def _():
            dkv_ref[:, 0:LANES] = dk_s[...].astype(dkv_ref.dtype)
            dkv_ref[:, LANES:2 * LANES] = dv_s[...].astype(dkv_ref.dtype)
            dfk_ref[...] = df_s[...]

    blk = (None, tq, LANES)
    col = pl.BlockSpec((None, None, tq, LANES), lambda b, h, j, i: (b, h, jnp.maximum(i, j), 0))
    in_specs = [pl.BlockSpec(blk, lambda b, h, j, i: (b, jnp.maximum(i, j), q0 + h)),
                pl.BlockSpec(blk, lambda b, h, j, i: (b, j, kv0 + 2 * h)),
                pl.BlockSpec(blk, lambda b, h, j, i: (b, j, kv0 + 2 * h + 1)),
                pl.BlockSpec(blk, lambda b, h, j, i: (b, jnp.maximum(i, j), o0 + h)), col, col]
    args = [qa, kva, kva, dmo, lse, delta]
    if gated:
        in_specs += [pl.BlockSpec(blk, lambda b, h, j, i: (b, jnp.maximum(i, j), 0)),
                     pl.BlockSpec((None, 8, tq), lambda b, h, j, i: (b, 0, j))]
        args += list(gates)
    aliases = {}
    if aliased:
        in_specs.append(pl.BlockSpec(memory_space=pl.ANY))
        args.append(out)
        aliases = {len(args) - 1: 0}
    return pl.pallas_call(
        body, name=name, grid=(bsz, N_HEADS, n_q, n_q), in_specs=in_specs,
        out_specs=[pl.BlockSpec((None, tq, 2 * LANES), lambda b, h, j, i: (b, j, out0 + h)),
                   pl.BlockSpec((None, None, 1, tq), lambda b, h, j, i: (b, h, 0, j))],
        out_shape=[jax.ShapeDtypeStruct(out.shape, out.dtype), jax.ShapeDtypeStruct((bsz, N_HEADS, 1, seq), F32)],
        scratch_shapes=[pltpu.VMEM((tq, LANES), F32), pltpu.VMEM((tq, LANES), F32), pltpu.VMEM((1, tq), F32)],
        input_output_aliases=aliases,
        compiler_params=_cparams(("parallel", "parallel", "parallel", "arbitrary")),
    )(*args)


def _gmlp_fn(uv, lng, lnb, ws, bst):
    u = jax.nn.gelu(uv[:, 0:GROUP_WIDTH])
    gv = jax.nn.gelu(uv[:, GROUP_WIDTH:2 * GROUP_WIDTH])
    mu = jnp.mean(gv, axis=-1, keepdims=True)
    vc = gv - mu
    var = jnp.mean(vc * vc, axis=-1, keepdims=True)
    vln = vc * lax.rsqrt(var + LN_EPS) * lng + lnb
    r_i = lax.broadcasted_iota(jnp.int32, (D_CHUNK, D_CHUNK), 0)
    c_i = lax.broadcasted_iota(jnp.int32, (D_CHUNK, D_CHUNK), 1)
    lane_g = lax.broadcasted_iota(jnp.int32, (D_CHUNK, GROUP_WIDTH), 1) // HEAD_DIM
    e_r = lax.broadcasted_iota(jnp.int32, (LANES, GROUP_WIDTH), 0)
    e_c = lax.broadcasted_iota(jnp.int32, (LANES, GROUP_WIDTH), 1)
    expand = (e_r == e_c // HEAD_DIM).astype(F32)
    mixed = jnp.dot(bst, expand, precision=HI, preferred_element_type=F32)
    for g in range(4):
        w = jnp.where(r_i >= c_i, ws[g], 0.0)
        mixed = mixed + jnp.where(lane_g == g, _bdot(w, vln, "nn"), 0.0)
    return u * mixed


def _gmlp_fwd(proj, mo, lng, lnb, ws, bst, name):
    bsz, seq, _ = proj.shape

    def body(p_ref, mo_any, lng_ref, lnb_ref, ws_ref, bst_ref, o_ref):
        del mo_any
        o_ref[...] = _gmlp_fn(p_ref[...], lng_ref[...], lnb_ref[...], ws_ref[...], bst_ref[...]).astype(o_ref.dtype)

    return pl.pallas_call(
        body, name=name, grid=(bsz, seq // D_CHUNK),
        in_specs=[pl.BlockSpec((None, D_CHUNK, 512), lambda b, s: (b, s, P_D // 512)),
                  pl.BlockSpec(memory_space=pl.ANY), _vec_spec(256), _vec_spec(256),
                  pl.BlockSpec((4, D_CHUNK, D_CHUNK), lambda b, s: (0, 0, 0)),
                  pl.BlockSpec((D_CHUNK, LANES), lambda b, s: (0, 0))],
        out_specs=pl.BlockSpec((None, D_CHUNK, GROUP_WIDTH), lambda b, s: (b, s, 1280 // GROUP_WIDTH)),
        out_shape=jax.ShapeDtypeStruct(mo.shape, mo.dtype),
        input_output_aliases={1: 0},
        compiler_params=_cparams(("parallel", "parallel")),
    )(proj, mo, lng, lnb, ws, bst)


def _gmlp_bwd(dmo, dproj, proj, lng, lnb, ws, bst, name):
    bsz, seq, _ = proj.shape

    def body(do_ref, dp_any, p_ref, lng_ref, lnb_ref, ws_ref, bst_ref, dp_ref, dlg_ref, dlb_ref, dws_ref, dbst_ref):
        del dp_any
        first = jnp.logical_and(pl.program_id(0) == 0, pl.program_id(1) == 0)

        @pl.when(first)
        def _():
            dlg_ref[...] = jnp.zeros_like(dlg_ref)
            dlb_ref[...] = jnp.zeros_like(dlb_ref)
            dws_ref[...] = jnp.zeros_like(dws_ref)
            dbst_ref[...] = jnp.zeros_like(dbst_ref)

        _, vjp = jax.vjp(_gmlp_fn, p_ref[...], lng_ref[...], lnb_ref[...], ws_ref[...], bst_ref[...])
        duv, dlg, dlb, dws, dbst = vjp(do_ref[...])
        dp_ref[...] = duv.astype(dp_ref.dtype)
        dlg_ref[...] += dlg
        dlb_ref[...] += dlb
        dws_ref[...] += dws
        dbst_ref[...] += dbst

    const2 = lambda shape: pl.BlockSpec(shape, lambda b, s: (0,) * len(shape))
    return pl.pallas_call(
        body, name=name, grid=(bsz, seq // D_CHUNK),
        in_specs=[pl.BlockSpec((None, D_CHUNK, GROUP_WIDTH), lambda b, s: (b, s, 1280 // GROUP_WIDTH)),
                  pl.BlockSpec(memory_space=pl.ANY),
                  pl.BlockSpec((None, D_CHUNK, 512), lambda b, s: (b, s, P_D // 512)),
                  _vec_spec(256), _vec_spec(256), const2((4, D_CHUNK, D_CHUNK)), const2((D_CHUNK, LANES))],
        out_specs=[pl.BlockSpec((None, D_CHUNK, 512), lambda b, s: (b, s, P_D // 512)),
                   _vec_spec(256), _vec_spec(256), const2((4, D_CHUNK, D_CHUNK)), const2((D_CHUNK, LANES))],
        out_shape=[jax.ShapeDtypeStruct(dproj.shape, dproj.dtype), jax.ShapeDtypeStruct((1, 256), F32),
                   jax.ShapeDtypeStruct((1, 256), F32), jax.ShapeDtypeStruct((4, D_CHUNK, D_CHUNK), F32),
                   jax.ShapeDtypeStruct((D_CHUNK, LANES), F32)],
        input_output_aliases={1: 0},
        compiler_params=_cparams(("arbitrary", "arbitrary")),
    )(dmo, dproj, proj, lng, lnb, ws, bst)


def _ada_fwd(c_all, ada_w, name):
    n_b = c_all.shape[0]
    depth, d, cols = ada_w.shape

    def body(c_ref, w_ref, o_ref):
        cv = c_ref[...]
        act = (cv * jax.nn.sigmoid(cv)).astype(BF16)
        o_ref[...] = jnp.dot(act, w_ref[...].astype(BF16), preferred_element_type=F32)

    return pl.pallas_call(
        body, name=name, grid=(depth,),
        in_specs=[pl.BlockSpec((n_b, d), lambda l: (0, 0)), pl.BlockSpec((None, d, cols), lambda l: (l, 0, 0))],
        out_specs=pl.BlockSpec((None, n_b, cols), lambda l: (l, 0, 0)),
        out_shape=jax.ShapeDtypeStruct((depth, n_b, cols), F32),
        compiler_params=_cparams(("parallel",)),
    )(c_all, ada_w)


def _ada_bwd(c_all, dmod_cols, dmod_full, name):
    n_b, d = c_all.shape
    depth, _, cols = dmod_cols.shape
    full = dmod_full.shape[-1]

    def body(c_ref, dm_ref, df_ref, gw_ref, gb_ref):
        cv = c_ref[...]
        act = (cv * jax.nn.sigmoid(cv)).astype(BF16)
        gw_ref[...] = lax.dot_general(act, dm_ref[...].astype(BF16), (((0,), (0,)), ((), ())),
                                      preferred_element_type=F32)
        gb_ref[...] = jnp.sum(df_ref[...], axis=0, keepdims=True)

    return pl.pallas_call(
        body, name=name, grid=(depth,),
        in_specs=[pl.BlockSpec((n_b, d), lambda l: (0, 0)), pl.BlockSpec((None, n_b, cols), lambda l: (l, 0, 0)),
                  pl.BlockSpec((None, n_b, full), lambda l: (l, 0, 0))],
        out_specs=[pl.BlockSpec((None, d, cols), lambda l: (l, 0, 0)),
                   pl.BlockSpec((None, 1, full), lambda l: (l, 0, 0))],
        out_shape=[jax.ShapeDtypeStruct((depth, d, cols), F32), jax.ShapeDtypeStruct((depth, 1, full), F32)],
        compiler_params=_cparams(("parallel",)),
    )(c_all, dmod_cols, dmod_full)


def _adamw(gparts, own, w, m, v, name, layer=0, prev=None):
    n_p, rows, cols = gparts.shape
    assert w.shape[1:] == (rows, cols)
    tr = rows
    if rows > 512:
        tr = next(c for c in range(512, 7, -8) if rows % c == 0)
    has_own = own is not None
    n_prev = 0 if prev is None else 4

    def body(*refs):
        if has_own:
            slot_ref, refs = refs[0], refs[1:]
        g_ref = refs[0]
        own_ref = refs[1] if has_own else None
        w_ref, m_ref, v_ref = refs[1 + has_own:4 + has_own]
        go_ref, do_ref, mo_ref, vo_ref = refs[4 + has_own + n_prev:]
        g = None
        for p in range(n_p):
            term = g_ref[p].astype(F32)
            if has_own:
                term = jnp.where(slot_ref[0] == p, own_ref[...].astype(F32), term)
            g = term if g is None else g + term
        m_new = ADAM_B1 * m_ref[...] + (1.0 - ADAM_B1) * g
        v_new = ADAM_B2 * v_ref[...] + (1.0 - ADAM_B2) * (g * g)
        m_hat = m_new / (1.0 - ADAM_B1 ** ADAM_STEP)
        v_hat = v_new / (1.0 - ADAM_B2 ** ADAM_STEP)
        go_ref[...] = g
        do_ref[...] = -ADAM_LR * (m_hat / (jnp.sqrt(v_hat) + ADAM_EPS) + ADAM_WD * w_ref[...])
        mo_ref[...] = m_new
        vo_ref[...] = v_new

    spec = pl.BlockSpec((None, tr, cols), lambda i, *_: (layer, i, 0))
    in_specs = [pl.BlockSpec((n_p, tr, cols), lambda i, *_: (0, i, 0))]
    args = [gparts]
    if has_own:
        in_specs.append(pl.BlockSpec((None, tr, cols), lambda i, slot: (slot[0], i, 0)))
        args.append(own[0])
    in_specs += [spec, spec, spec]
    args += [w, m, v]
    aliases = {}
    if prev is not None:
        aliases = {has_own + len(args) + k: k for k in range(4)}
        in_specs += [pl.BlockSpec(memory_space=pl.ANY)] * 4
        args += list(prev)
    shp = jax.ShapeDtypeStruct(w.shape, F32)
    out_specs, out_shape = [spec, spec, spec, spec], [shp, shp, shp, shp]
    if not has_own:
        return pl.pallas_call(
            body, name=name, grid=(rows // tr,), in_specs=in_specs, out_specs=out_specs, out_shape=out_shape,
            input_output_aliases=aliases, compiler_params=_cparams(("parallel",)),
        )(*args)
    return pl.pallas_call(
        body, name=name, out_shape=out_shape, input_output_aliases=aliases,
        grid_spec=pltpu.PrefetchScalarGridSpec(num_scalar_prefetch=1, grid=(rows // tr,), in_specs=in_specs,
                                               out_specs=out_specs),
        compiler_params=_cparams(("parallel",)),
    )(jnp.reshape(own[1], (1,)).astype(jnp.int32), *args)


def _sum_parts(parts, name):
    n_p, rows, cols = parts.shape
    tr = 256 if rows % 256 == 0 else rows

    def body(p_ref, o_ref):
        acc = p_ref[0]
        for p in range(1, n_p):
            acc = acc + p_ref[p]
        o_ref[...] = acc

    return pl.pallas_call(
        body, name=name, grid=(rows // tr,),
        in_specs=[pl.BlockSpec((n_p, tr, cols), lambda i: (0, i, 0))],
        out_specs=pl.BlockSpec((tr, cols), lambda i: (i, 0)),
        out_shape=jax.ShapeDtypeStruct((rows, cols), F32),
        compiler_params=_cparams(("parallel",)),
    )(parts)


def _all_gather(arrs, name):
    n = len(arrs)

    def body(*refs):
        in_refs, out_refs = refs[:n], refs[n:2 * n]
        send_sems, recv_sems, loc_sems = refs[2 * n:]
        x, y, c = lax.axis_index("x"), lax.axis_index("y"), lax.axis_index("c")
        me, sibling = (x, y, c), (x, y, 1 - c)
        chips = [(1 - x, y), (x, 1 - y), (1 - x, 1 - y)]

        def copy(a, k, block, to, src=None):
            slot = out_refs[a].at[4 * block[0] + 2 * block[1] + block[2]]
            return pltpu.make_async_remote_copy(
                src_ref=slot if src is None else src, dst_ref=slot, send_sem=send_sems.at[a, k],
                recv_sem=recv_sems.at[a, k], device_id=to, device_id_type=pl.DeviceIdType.MESH)

        mine = [pltpu.make_async_copy(in_refs[a], out_refs[a].at[4 * x + 2 * y + c], loc_sems.at[a])
                for a in range(n)]
        for cp in mine:
            cp.start()
        first = []
        for a in range(n):
            first.append(copy(a, 0, me, sibling, src=in_refs[a]))
            first += [copy(a, 1 + j, me, (*chip, c), src=in_refs[a]) for j, chip in enumerate(chips)]
        for cp in first:
            cp.start()
        passed = []
        for j, chip in enumerate(chips):
            for a in range(n):
                copy(a, 1 + j, (*chip, c), me).wait_recv()
                cp = copy(a, 4 + j, (*chip, c), sibling)
                cp.start()
                passed.append(cp)
        for a in range(n):
            copy(a, 0, sibling, me).wait_recv()
        for j, chip in enumerate(chips):
            for a in range(n):
                copy(a, 4 + j, (*chip, 1 - c), me).wait_recv()
        for cp in first + passed:
            cp.wait_send()
        for cp in mine:
            cp.wait()

    any_spec = pl.BlockSpec(memory_space=pl.ANY)
    return pl.pallas_call(
        body, name=name, in_specs=[any_spec] * n, out_specs=[any_spec] * n,
        out_shape=[jax.ShapeDtypeStruct((N_DEV,) + a.shape, a.dtype) for a in arrs],
        scratch_shapes=[pltpu.SemaphoreType.DMA((n, N_DEV - 1)), pltpu.SemaphoreType.DMA((n, N_DEV - 1)),
                        pltpu.SemaphoreType.DMA((n,))],
    )(*arrs)


def _flip_peers():
    x, y, c = lax.axis_index("x"), lax.axis_index("y"), lax.axis_index("c")
    peers = []
    for fx, fy, fc in [(fx, fy, fc) for fx in (0, 1) for fy in (0, 1) for fc in (0, 1)][1:]:
        px, py, pc = (1 - x if fx else x), (1 - y if fy else y), (1 - c if fc else c)
        peers.append(((px, py, pc), 4 * px + 2 * py + pc))
    return 4 * x + 2 * y + c, peers


def _push_start(srcs, name, whole=False):
    n, n_peer = len(srcs), N_DEV - 1
    if whole:
        me_w = 4 * lax.axis_index("x") + 2 * lax.axis_index("y") + lax.axis_index("c")
        lands = [lax.dynamic_update_slice_in_dim(lax.empty((N_DEV,) + a.shape, a.dtype), a[None], me_w, axis=0)
                 for a in srcs]
    else:
        lands = [lax.empty(a.shape, a.dtype) for a in srcs]

    def body(*refs):
        src_refs, land_refs = refs[:n], refs[n:2 * n]
        send_sems, recv_sems = refs[2 * n], refs[2 * n + 1]
        token = refs[-1]
        me, peers = _flip_peers()
        for k, (dev, idx) in enumerate(peers):
            for a in range(n):
                pltpu.make_async_remote_copy(
                    src_ref=src_refs[a] if whole else src_refs[a].at[idx], dst_ref=land_refs[a].at[me],
                    send_sem=send_sems.at[a * n_peer + k], recv_sem=recv_sems.at[a * n_peer + k], device_id=dev,
                    device_id_type=pl.DeviceIdType.MESH).start()
        token[...] = jnp.zeros_like(token)

    hbm = pl.BlockSpec(memory_space=pltpu.HBM)
    sem = pl.BlockSpec(memory_space=pltpu.SEMAPHORE)
    arrs = list(srcs) + lands
    res = pl.pallas_call(
        body, name=name, in_specs=[hbm] * (2 * n),
        out_specs=(sem, sem, *[hbm] * (2 * n), pl.BlockSpec(memory_space=pltpu.VMEM)),
        out_shape=(pltpu.SemaphoreType.DMA((n * n_peer,)), pltpu.SemaphoreType.DMA((n * n_peer,)),
                   *[pltpu.HBM(a.shape, a.dtype) for a in arrs], jax.ShapeDtypeStruct((8, LANES), F32)),
        input_output_aliases={i: 2 + i for i in range(2 * n)},
        compiler_params=pltpu.CompilerParams(has_side_effects=pltpu.SideEffectType.DATAFLOW_SIDE_EFFECTING),
    )(*[pltpu.with_memory_space_constraint(a, pltpu.HBM) for a in arrs])
    return res[0], res[1], list(res[2:2 + n]), list(res[2 + n:2 + 2 * n]), res[-1]


def _push_wait(send_sems, recv_sems, srcs, lands, after, name, whole=False):
    n, n_peer = len(srcs), N_DEV - 1

    def body(*refs):
        src_refs, land_refs = refs[:n], refs[n:2 * n]
        send_s, recv_s = refs[2 * n], refs[2 * n + 1]
        _, peers = _flip_peers()
        for k, (dev, idx) in enumerate(peers):
            for a in range(n):
                cp = pltpu.make_async_remote_copy(
                    src_ref=src_refs[a] if whole else src_refs[a].at[idx], dst_ref=land_refs[a].at[idx],
                    send_sem=send_s.at[a * n_peer + k],
                    recv_sem=recv_s.at[a * n_peer + k], device_id=dev, device_id_type=pl.DeviceIdType.MESH)
                cp.wait_send()
                cp.wait_recv()

    hbm = pl.BlockSpec(memory_space=pltpu.HBM)
    sem = pl.BlockSpec(memory_space=pltpu.SEMAPHORE)
    arrs = list(srcs) + list(lands)
    res = pl.pallas_call(
        body, name=name, in_specs=[hbm] * (2 * n) + [sem, sem, pl.BlockSpec(memory_space=pl.ANY)],
        out_specs=tuple([hbm] * (2 * n)), out_shape=tuple(pltpu.HBM(a.shape, a.dtype) for a in arrs),
        input_output_aliases={i: i for i in range(2 * n)},
        compiler_params=pltpu.CompilerParams(has_side_effects=pltpu.SideEffectType.DATAFLOW_SIDE_EFFECTING),
    )(*arrs, send_sems, recv_sems, after)
    return list(res[:n]), list(res[n:])


def _ffn_fwd(x, h, mod, w_in, w_out, lng, lnb, rows, tag, nxt):
    bsz, seq, d = x.shape
    t = bsz * seq
    if h is None:
        h = _modulate(x, mod, rows[0], rows[1], f"modulate_{tag}")
    z, a = _ffn_in_swiglu(h.reshape(t, d), w_in, f"ffn_in_{tag}")
    f = _matmul(a, w_out, mode="nn", group_out=False, out_dtype=F32, tm=1024, tk=a.shape[2],
                name=f"ffn_out_{tag}").reshape(bsz, seq, d)
    y, h_next = _res_ln(x, f, mod, lng, lnb, rows[2], 0.5, f"res_ln_{tag}", nxt)
    return y, h_next, (x, h, z, a, f)


def _tied(mod, tie):
    return mod if tie is None else mod + tie


def _ffn_bwd(dy, saved, mod, w_in, w_out, lng, lnb, rows, tag, ready):
    x, h, z, a, f = saved
    bsz, seq, d = x.shape
    t = bsz * seq
    dx_res, df, dgate, dlg, dlb = _res_ln_bwd(dy, x, f, mod, lng, lnb, rows[2], 0.5, f"res_ln_bwd_{tag}")
    df2 = df.reshape(1, t, d)
    dw_out = _matmul(a, df2, mode="tn", group_out=True, out_dtype=BF16, tm=a.shape[2], tk=min(t, 2048),
                     name=f"ffn_out_dw_{tag}")
    tie_out = ready(f"{tag}_out", dw_out)
    dz = _ffn_out_dx_swiglu(df.reshape(t, d), w_out, z, f"ffn_out_dx_{tag}").reshape(N_DEV, t, -1)
    dw_in = _matmul(dz, h.reshape(1, t, d), mode="tn", group_out=True, out_dtype=BF16, tm=dz.shape[2],
                    tk=min(t, 2048), name=f"ffn_in_dw_{tag}")
    tie_in = ready(f"{tag}_in", dw_in)
    dh = _matmul(dz, w_in, mode="nn", group_out=False, out_dtype=F32, tm=1024, tk=dz.shape[2],
                 name=f"ffn_in_dx_{tag}").reshape(bsz, seq, d)
    dx, dsh, dsc = _modulate_bwd(dh, x, _tied(_tied(mod, tie_out), tie_in), dx_res, rows[1],
                                 f"modulate_bwd_{tag}")
    return dx, (dsh, dsc, dgate), dw_in, dw_out, dlg, dlb


def _mixer_fwd(x, h, mod, wts, small, lng, lnb, layer, tabs):
    bsz, seq, d = x.shape
    t = bsz * seq
    proj = _matmul(h.reshape(1, t, d), wts["mix_in"][None], mode="nn", group_out=True, out_dtype=F32, tm=512, tk=d,
                   name="mix_in").reshape(bsz, seq, PACK_W)
    mo, states = _hgrn_fwd(proj, small["lb_logits8"], small["hgrn_norm_g"], layer, f"hgrn_fwd_l{layer}")
    q, kv = _mla_pre(proj, small["q_norm_g"], small["kv_norm_g"], wts["uq"], wts["ukv"], tabs, "mla_pre")
    mla_scale = float((B_NOPE + B_ROPE) ** -0.5)
    mo, lse_b = _attn_fwd(q, 0, kv, 0, mo, 2, None, mla_scale, "mla_attn_fwd")
    fg = _fox_gate(proj, small["fox_b_f"], "fox_gate")
    gates = (fg, jnp.swapaxes(fg[:, :, 0:8], 1, 2))
    fox_scale = float(HEAD_DIM ** -0.5)
    mo, lse_c = _attn_fwd(proj, P_CQ // LANES, proj, P_CKV // LANES, mo, 6, gates, fox_scale, "fox_attn_fwd")
    mo = _gmlp_fwd(proj, mo, small["gmlp_ln_g"], small["gmlp_ln_b"], small["gmlp_w_s"], small["gmlp_bst"],
                   "gmlp_fwd")
    mixed = _matmul(mo.reshape(1, t, MO_W), wts["mix_out"][None], mode="nn", group_out=True, out_dtype=F32,
                    tm=1024, tk=MO_W, name="mix_out").reshape(bsz, seq, d)
    y, h_next = _res_ln(x, mixed, mod, lng, lnb, 5, 1.0, "res_ln_mix", (mod, 6, 7))
    return y, h_next, (x, h, proj, mo, states, q, kv, lse_b, gates, lse_c, mixed)


def _mixer_bwd(dy, saved, mod, wts, small, lng, lnb, layer, tabs, ready):
    x, h, proj, mo, states, q, kv, lse_b, gates, lse_c, mixed = saved
    bsz, seq, d = x.shape
    t = bsz * seq
    dx_res, dmixed, dgate, dlg, dlb = _res_ln_bwd(dy, x, mixed, mod, lng, lnb, 5, 1.0, "res_ln_bwd_mix")
    dm2 = dmixed.reshape(1, t, d)
    dmo = _matmul(dm2, wts["mix_out"][None], mode="nt", group_out=True, out_dtype=F32, tm=1024, tk=d,
                  name="mix_out_dx").reshape(bsz, seq, MO_W)
    dw_out = _matmul(mo.reshape(1, t, MO_W), dm2, mode="tn", group_out=True, out_dtype=F32, tm=512, tk=min(t, 2048),
                     name="mix_out_dw")[0]
    tie_out = ready("mix_out", dw_out)
    g = {}
    dproj, g["lb_logits8"], g["hgrn_norm_g"] = _hgrn_bwd(dmo, proj, states, small["lb_logits8"],
                                                         small["hgrn_norm_g"], layer, f"hgrn_bwd_l{layer}")
    mla_scale = float((B_NOPE + B_ROPE) ** -0.5)
    dq, delta_b, _ = _attn_bwd_q(q, 0, kv, 0, mo, dmo, 2, lse_b, None, mla_scale,
                                 jax.ShapeDtypeStruct((bsz, seq, 512), F32), 0, "mla_attn_bwd_q")
    dkv, _ = _attn_bwd_kv(q, 0, kv, 0, dmo, 2, lse_b, delta_b, None, mla_scale,
                          jax.ShapeDtypeStruct((bsz, seq, 1024), F32), 0, "mla_attn_bwd_kv")
    dproj, g["q_norm_g"], g["kv_norm_g"], g["uq"], g["ukv"] = _mla_pre_bwd(
        dq, dkv, dproj, proj, small["q_norm_g"], small["kv_norm_g"], wts["uq"], wts["ukv"], tabs, "mla_pre_bwd")
    fox_scale = float(HEAD_DIM ** -0.5)
    dproj, delta_c, dfq = _attn_bwd_q(proj, P_CQ // LANES, proj, P_CKV // LANES, mo, dmo, 6, lse_c, gates,
                                      fox_scale, dproj, P_CQ // LANES, "fox_attn_bwd_q")
    dproj, dfk = _attn_bwd_kv(proj, P_CQ // LANES, proj, P_CKV // LANES, dmo, 6, lse_c, delta_c, gates, fox_scale,
                              dproj, P_CKV // (2 * LANES), "fox_attn_bwd_kv")
    dfk_cols = jnp.pad(jnp.swapaxes(dfk[:, :, 0, :], 1, 2), ((0, 0), (0, 0), (0, LANES - N_HEADS)))
    dproj, g["fox_b_f"] = _fox_gate_bwd(dfq, dfk_cols, dproj, proj, small["fox_b_f"], "fox_gate_bwd")
    dproj, g["gmlp_ln_g"], g["gmlp_ln_b"], g["gmlp_w_s"], g["gmlp_bst"] = _gmlp_bwd(
        dmo, dproj, proj, small["gmlp_ln_g"], small["gmlp_ln_b"], small["gmlp_w_s"], small["gmlp_bst"], "gmlp_bwd")
    dp2 = dproj.reshape(1, t, PACK_W)
    dw_in = _matmul(h.reshape(1, t, d), dp2, mode="tn", group_out=True, out_dtype=BF16, tm=512, tk=1024,
                    name="mix_in_dw")[0]
    tie_in = ready("mix_in", dw_in)
    dh = _matmul(dp2, wts["mix_in"][None], mode="nt", group_out=True, out_dtype=F32, tm=512, tk=PACK_W,
                 name="mix_in_dx").reshape(bsz, seq, d)
    dx, dsh, dsc = _modulate_bwd(dh, x, _tied(_tied(mod, tie_out), tie_in), dx_res, 4, "modulate_bwd_mix")
    return dx, (dsh, dsc, dgate), dw_in, dw_out, g, dlg, dlb


def _small_views(p, layer):
    return {
        "lb_logits8": jnp.pad(p["hgrn_lb_logits"], ((0, 8 - DEPTH), (0, 0))),
        "hgrn_norm_g": p["hgrn_norm_g"][layer][None],
        "q_norm_g": p["mla_q_norm_g"][layer][None],
        "kv_norm_g": p["mla_kv_norm_g"][layer][None],
        "fox_b_f": jnp.pad(p["fox_b_f"][layer][None], ((0, 0), (0, LANES - N_HEADS))),
        "gmlp_ln_g": p["gmlp_ln_g"][layer][None],
        "gmlp_ln_b": p["gmlp_ln_b"][layer][None],
        "gmlp_w_s": p["gmlp_w_s"][layer],
        "gmlp_bst": jnp.pad(p["gmlp_b_s"][layer].T, ((0, 0), (0, LANES - N_HEADS))),
    }


def _local_step(x, mod, target, weights, p, grads_ready=None):
    bsz, seq, d = x.shape
    tabs = _rope_tables(seq)
    saved = []
    h = None
    for l in range(DEPTH):
        sm = _small_views(p, l)
        lng, lnb = p["ln_g"][l], p["ln_b"][l]
        w = weights(l, "ffn1", x)
        x, h, s1 = _ffn_fwd(x, h, mod[l], w["ffn1_in"], w["ffn1_out"], lng[0:1], lnb[0:1], (0, 1, 2), "ffn1",
                            (mod[l], 3, 4))
        x, h, s2 = _mixer_fwd(x, h, mod[l], weights(l, "mix", x), sm, lng[1:2], lnb[1:2], l, tabs)
        w = weights(l, "ffn2", x)
        x, h, s3 = _ffn_fwd(x, h, mod[l], w["ffn2_in"], w["ffn2_out"], lng[2:3], lnb[2:3], (6, 7, 8), "ffn2",
                            (mod[l + 1], 0, 1) if l + 1 < DEPTH else None)
        saved.append((s1, s2, s3))
    dx, loss = _loss_head(x, target, "loss_head")
    big, small, dmods = [None] * DEPTH, [None] * DEPTH, [None] * DEPTH
    ties = []

    def tied(a):
        for t in ties:
            a = a + t
        return a

    for l in reversed(range(DEPTH)):
        w = {**weights(l, "ffn1", None), **weights(l, "mix", None), **weights(l, "ffn2", None)}
        sm = _small_views(p, l)
        lng, lnb = p["ln_g"][l], p["ln_b"][l]
        s1, s2, s3 = saved[l]

        def ready(name, grad, l=l):
            tie = None if grads_ready is None else grads_ready(l, name, grad)
            if tie is not None:
                ties.append(tie)
            return tie

        dx, dm3, dwi2, dwo2, dlg2, dlb2 = _ffn_bwd(dx, s3, tied(mod[l]), w["ffn2_in"], w["ffn2_out"], lng[2:3],
                                                   lnb[2:3], (6, 7, 8), "ffn2", ready)
        dx, dm2, dwmi, dwmo, g, dlg1, dlb1 = _mixer_bwd(dx, s2, tied(mod[l]), w, sm, lng[1:2], lnb[1:2], l, tabs,
                                                        ready)
        dx, dm1, dwi1, dwo1, dlg0, dlb0 = _ffn_bwd(dx, s1, tied(mod[l]), w["ffn1_in"], w["ffn1_out"], lng[0:1],
                                                   lnb[0:1], (0, 1, 2), "ffn1", ready)
        dmods[l] = jnp.concatenate(list(dm1) + list(dm2) + list(dm3), axis=1)
        big[l] = {"ffn1_in": dwi1, "ffn1_out": dwo1, "ffn2_in": dwi2, "ffn2_out": dwo2, "mix_in": dwmi,
                  "mix_out": dwmo}
        g["ln_g"] = jnp.concatenate([dlg0, dlg1, dlg2], axis=0)
        g["ln_b"] = jnp.concatenate([dlb0, dlb1, dlb2], axis=0)
        small[l] = g
    return loss, dx, jnp.stack(dmods), big, small


_BIG = ("ffn1_in", "ffn1_out", "ffn2_in", "ffn2_out", "mix_in", "mix_out")


def _small_grad_list(small, loss):
    def both(fn):
        return jnp.stack([fn(small[l]) for l in range(DEPTH)])

    uq_src, ukv_src = _uq_src(), _ukv_src()
    return [
        ("loss", loss.reshape(1)),
        ("ln_g", both(lambda g: g["ln_g"])), ("ln_b", both(lambda g: g["ln_b"])),
        ("hgrn_lb_logits", small[0]["lb_logits8"][:DEPTH] + small[1]["lb_logits8"][:DEPTH]),
        ("hgrn_norm_g", both(lambda g: g["hgrn_norm_g"][0])),
        ("mla_q_norm_g", both(lambda g: g["q_norm_g"][0])),
        ("mla_kv_norm_g", both(lambda g: g["kv_norm_g"][0])),
        ("mla_w_uq", both(lambda g: _unpack_cols(g["uq"], uq_src, 384))),
        ("mla_w_ukv", both(lambda g: _unpack_cols(g["ukv"], ukv_src, 512))),
        ("fox_b_f", both(lambda g: g["fox_b_f"][0, :N_HEADS])),
        ("gmlp_ln_g", both(lambda g: g["gmlp_ln_g"][0])), ("gmlp_ln_b", both(lambda g: g["gmlp_ln_b"][0])),
        ("gmlp_w_s", both(lambda g: g["gmlp_w_s"])),
        ("gmlp_b_s", both(lambda g: g["gmlp_bst"][:, :N_HEADS].T)),
    ]


_PACK_COLS = 512


def _pack_small(items):
    flat = jnp.concatenate([a.reshape(-1).astype(F32) for _, a in items])
    n = flat.shape[0]
    tile = 8 * _PACK_COLS
    flat = jnp.pad(flat, (0, (-n) % tile))
    return flat.reshape(-1, _PACK_COLS)


def _unpack_small(buf, items):
    flat = buf.reshape(-1)
    out, off = {}, 0
    for name, a in items:
        out[name] = flat[off:off + a.size].reshape(a.shape)
        off += a.size
    return out


def _as2d(a):
    return a.reshape(-1, a.shape[-1])


def kernel(x, c, ada_w, ada_b, ln_g, ln_b, ffn1_w_in, ffn1_w_out, ffn2_w_in, ffn2_w_out, mix_w_in, mix_w_out, hgrn_lb_logits, hgrn_norm_g, mla_q_norm_g, mla_kv_norm_g, mla_w_uq, mla_w_ukv, fox_b_f, gmlp_ln_g, gmlp_ln_b, gmlp_w_s, gmlp_b_s, loss_target, m_ada_w, m_ada_b, m_ln_g, m_ln_b, m_ffn1_w_in, m_ffn1_w_out, m_ffn2_w_in, m_ffn2_w_out, m_mix_w_in, m_mix_w_out, m_hgrn_lb_logits, m_hgrn_norm_g, m_mla_q_norm_g, m_mla_kv_norm_g, m_mla_w_uq, m_mla_w_ukv, m_fox_b_f, m_gmlp_ln_g, m_gmlp_ln_b, m_gmlp_w_s, m_gmlp_b_s, v_ada_w, v_ada_b, v_ln_g, v_ln_b, v_ffn1_w_in, v_ffn1_w_out, v_ffn2_w_in, v_ffn2_w_out, v_mix_w_in, v_mix_w_out, v_hgrn_lb_logits, v_hgrn_norm_g, v_mla_q_norm_g, v_mla_kv_norm_g, v_mla_w_uq, v_mla_w_ukv, v_fox_b_f, v_gmlp_ln_g, v_gmlp_ln_b, v_gmlp_w_s, v_gmlp_b_s):
    names = ["ada_w", "ada_b", "ln_g", "ln_b", "ffn1_w_in", "ffn1_w_out", "ffn2_w_in", "ffn2_w_out", "mix_w_in",
             "mix_w_out", "hgrn_lb_logits", "hgrn_norm_g", "mla_q_norm_g", "mla_kv_norm_g", "mla_w_uq", "mla_w_ukv",
             "fox_b_f", "gmlp_ln_g", "gmlp_ln_b", "gmlp_w_s", "gmlp_b_s"]
    w = dict(zip(names, [ada_w, ada_b, ln_g, ln_b, ffn1_w_in, ffn1_w_out, ffn2_w_in, ffn2_w_out, mix_w_in, mix_w_out,
                         hgrn_lb_logits, hgrn_norm_g, mla_q_norm_g, mla_kv_norm_g, mla_w_uq, mla_w_ukv, fox_b_f,
                         gmlp_ln_g, gmlp_ln_b, gmlp_w_s, gmlp_b_s]))
    m = dict(zip(names, [m_ada_w, m_ada_b, m_ln_g, m_ln_b, m_ffn1_w_in, m_ffn1_w_out, m_ffn2_w_in, m_ffn2_w_out,
                         m_mix_w_in, m_mix_w_out, m_hgrn_lb_logits, m_hgrn_norm_g, m_mla_q_norm_g, m_mla_kv_norm_g,
                         m_mla_w_uq, m_mla_w_ukv, m_fox_b_f, m_gmlp_ln_g, m_gmlp_ln_b, m_gmlp_w_s, m_gmlp_b_s]))
    v = dict(zip(names, [v_ada_w, v_ada_b, v_ln_g, v_ln_b, v_ffn1_w_in, v_ffn1_w_out, v_ffn2_w_in, v_ffn2_w_out,
                         v_mix_w_in, v_mix_w_out, v_hgrn_lb_logits, v_hgrn_norm_g, v_mla_q_norm_g, v_mla_kv_norm_g,
                         v_mla_w_uq, v_mla_w_ukv, v_fox_b_f, v_gmlp_ln_g, v_gmlp_ln_b, v_gmlp_w_s, v_gmlp_b_s]))
    bsz, seq, d = x.shape
    me = 4 * lax.axis_index("x") + 2 * lax.axis_index("y") + lax.axis_index("c")
    mix_src, uq_src, ukv_src, mo_src = _mix_in_src(), _uq_src(), _ukv_src(), _mo_src()

    part_names = {"ffn1": ["ffn1_w_in", "ffn1_w_out"], "mix": ["mix_w_in", "mix_w_out", "mla_w_uq", "mla_w_ukv"],
                  "ffn2": ["ffn2_w_in", "ffn2_w_out"]}
    group_of = {}
    for l in range(DEPTH):
        for part in ("ffn1", "mix", "ffn2"):
            group_of[(l, part)] = (0, part) if l == 0 else (l, "all")
    in_flight = {}
    transposed = ("ffn1_w_in", "ffn2_w_in")

    def start_group(key, behind=None):
        members = [(l, part) for (l, part), g in group_of.items() if g == key]
        labels = [(l, n) for l, part in members for n in part_names[part]]
        shards = []
        for l, n in labels:
            a = w[n][l]
            if n == "mix_w_in":
                a = _pack_cols(a, mix_src)
            if n in transposed:
                a = jnp.swapaxes(w[n], 1, 2)[l]
            shards.append(a.astype(BF16))
        if behind is not None:
            shards, _ = lax.optimization_barrier((shards, behind))
        in_flight[key] = (labels, _push_start(shards, f"gather_start_{key[0]}_{key[1]}", whole=True))

    keys_in_order = list(dict.fromkeys(group_of.values()))
    start_group(keys_in_order[0])

    gathered = _all_gather([c, ln_g, ln_b], "gather_inputs")
    c_all = gathered[0].reshape(N_DEV * bsz, d)
    ln_g_full = jnp.moveaxis(gathered[1], 0, 2).reshape(DEPTH, 3, d)
    ln_b_full = jnp.moveaxis(gathered[2], 0, 2).reshape(DEPTH, 3, d)

    mod_cols = _ada_fwd(c_all, ada_w, "ada_fwd")
    mod_all, = _all_gather([mod_cols], "gather_mod")
    mod_mine = lax.dynamic_slice_in_dim(mod_all, me * bsz, bsz, axis=2)
    mod = jnp.moveaxis(mod_mine, 0, 2).reshape(DEPTH, bsz, N_MOD * d) + ada_b[:, None, :]
    for key in keys_in_order[1:]:
        start_group(key, behind=mod)
    tie = sum(h[-1][0, 0] for _, h in in_flight.values())
    mod = mod.reshape(DEPTH, bsz, N_MOD, d) + tie

    arrived, laid_out = {}, {}

    def weights(l, part, after):
        if (l, part) not in laid_out:
            laid_out[(l, part)] = lay_out(l, part, after)
        return laid_out[(l, part)]

    def lay_out(l, part, after):
        key = group_of[(l, part)]
        if key not in arrived:
            labels, (send_sems, recv_sems, srcs, lands, _) = in_flight[key]
            _, lands = _push_wait(send_sems, recv_sems, srcs, lands, after, f"gather_wait_{key[0]}_{key[1]}",
                                  whole=True)
            arrived[key] = dict(zip(labels, lands))
        gw = {n: arrived[key][(l, n)] for n in part_names[part]}
        if part != "mix":
            return {f"{part}_in": gw[f"{part}_w_in"], f"{part}_out": gw[f"{part}_w_out"].reshape(4, 704, d)}
        uq = jnp.moveaxis(gw["mla_w_uq"], 0, 1).reshape(256, 384)
        ukv = jnp.moveaxis(gw["mla_w_ukv"], 0, 1).reshape(128, 512)
        return {"mix_in": gw["mix_w_in"].reshape(d, PACK_W),
                "mix_out": _pack_cols(gw["mix_w_out"].reshape(d, d).T, mo_src).T,
                "uq": _pack_cols(uq, uq_src), "ukv": _pack_cols(ukv, ukv_src)}

    p = dict(w)
    p["ln_g"], p["ln_b"] = ln_g_full, ln_b_full
    def chunks(name, arr):
        if name in ("ffn1_in", "ffn2_in"):
            return arr
        if name in ("ffn1_out", "ffn2_out"):
            return arr.reshape(N_DEV, arr.shape[1] // 2, d)
        if name == "mix_in":
            return arr.reshape(N_DEV, d // N_DEV, PACK_W)
        return _unpack_cols(arr.T, mo_src, d).T.astype(BF16).reshape(N_DEV, d // N_DEV, d)

    pending, started = {}, []

    def grads_ready(l, name, grad):
        pending[(name, l)] = chunks(name, grad)
        flush = name == "ffn1_in" if l > 0 else name in ("ffn2_in", "mix_out", "mix_in", "ffn1_out", "ffn1_in")
        if not flush:
            return None
        keys = sorted(pending)
        handles = _push_start([pending[k] for k in keys], f"push_start_{len(started)}")
        pending.clear()
        started.append((keys, handles, l == 0 and name.startswith("ffn1")))
        return handles[-1][0, 0]

    loss, grad_x, dmod, big, small = _local_step(x, mod, loss_target, weights, p, grads_ready)
    del big

    recv, out = {}, {}

    def arrive(n, after):
        keys, (send_sems, recv_sems, srcs, lands, _), _ = started[n]
        srcs, lands = _push_wait(send_sems, recv_sems, srcs, lands, after, f"push_wait_{n}")
        for k, src, land in zip(keys, srcs, lands):
            recv[k] = (land, src)

    big_of = {"ffn1_w_in": "ffn1_in", "ffn1_w_out": "ffn1_out", "ffn2_w_in": "ffn2_in", "ffn2_w_out": "ffn2_out",
              "mix_w_in": "mix_in", "mix_w_out": "mix_out"}
    chain = {name: None for name in big_of}

    def big_update(key, l):
        name = next(nm for nm, k in big_of.items() if k == key)
        parts, src = recv[(key, l)]
        if key == "mix_in":
            parts = _unpack_cols(parts, mix_src, MIX_ORIG_W)
            src = _unpack_cols(src, mix_src, MIX_ORIG_W)
        view = (lambda a: jnp.swapaxes(a, 1, 2)) if name in transposed else (lambda a: a)
        chain[name] = _adamw(parts, (src, me), view(w[name]), view(m[name]), view(v[name]), f"adamw_{name}_l{l}",
                             layer=l, prev=chain[name])

    def update(name, grad):
        shape = w[name].shape
        as3 = lambda a: a.reshape(1, -1, shape[-1])
        res = _adamw(as3(grad), None, as3(w[name]), as3(m[name]), as3(v[name]), f"adamw_{name}")
        out[name] = tuple(r.reshape(shape) for r in res)

    for n, (keys, _, last) in enumerate(started):
        if not last:
            arrive(n, grad_x)
            for key, l in keys:
                big_update(key, l)

    dmod_flat = dmod.reshape(DEPTH, bsz, N_MOD * d)
    done = [r[0] for r in chain.values() if r is not None]
    if done:
        dmod_flat, _ = lax.optimization_barrier((dmod_flat, done))
    dmod_all, = _all_gather([dmod_flat], "gather_dmod")
    dmod_full = jnp.moveaxis(dmod_all, 0, 1).reshape(DEPTH, N_DEV * bsz, N_MOD * d)
    cols = ada_w.shape[2]
    dmod_cols = lax.dynamic_slice_in_dim(dmod_full, me * cols, cols, axis=2)
    g_ada_w, g_ada_b = _ada_bwd(c_all, dmod_cols, dmod_full, "ada_bwd")
    res = None
    for l in range(DEPTH):
        res = _adamw(g_ada_w[l][None], None, ada_w, m_ada_w, v_ada_w, f"adamw_ada_w_l{l}", layer=l, prev=res)
    out["ada_w"] = tuple(res)
    update("ada_b", g_ada_b.reshape(DEPTH, N_MOD * d))

    items = _small_grad_list(small, loss)
    parts, = _all_gather([_pack_small(items)], "gather_small")
    sg = _unpack_small(_sum_parts(parts, "sum_small"), items)
    for name in ("ln_g", "ln_b"):
        update(name, lax.dynamic_slice_in_dim(sg[name], me * (d // N_DEV), d // N_DEV, axis=2))
    for name, width in (("mla_w_uq", 48), ("mla_w_ukv", 64)):
        update(name, lax.dynamic_slice_in_dim(sg[name], me * width, width, axis=2))
    for name in ("hgrn_lb_logits", "hgrn_norm_g", "mla_q_norm_g", "mla_kv_norm_g", "fox_b_f", "gmlp_ln_g",
                 "gmlp_ln_b", "gmlp_w_s", "gmlp_b_s"):
        update(name, sg[name])

    for n, (keys, _, last) in enumerate(started):
        if last:
            arrive(n, out["gmlp_w_s"][0])
            for key, l in keys:
                big_update(key, l)
    for name in big_of:
        out[name] = tuple(jnp.swapaxes(r, 1, 2) if name in transposed else r for r in chain[name])

    return (sg["loss"][0], grad_x, *[out[n][0] for n in names], *[out[n][1] for n in names],
            *[out[n][2] for n in names], *[out[n][3] for n in names])
```

```python
import functools

import numpy as np
import jax
import jax.numpy as jnp
from jax import lax
from jax.experimental import pallas as pl
from jax.experimental.pallas import tpu as pltpu

F32 = jnp.float32
BF16 = jnp.bfloat16
HI = lax.Precision.HIGHEST

D_MODEL = 1024
DEPTH = 2
GROUP_WIDTH = 256
N_HEADS = 4
HEAD_DIM = 64
A_CHUNK = 16
LB_FLOOR = 1e-30
B_NOPE = 64
B_ROPE = 32
ROPE_THETA = 10000.0
D_CHUNK = 128
D_FF = 2816
N_MOD = 9
ALPHA = (2 * DEPTH) ** 0.25
LN_EPS = 1e-5
RMS_EPS = 1e-6
ADAM_LR = 0.001
ADAM_B1 = 0.9
ADAM_B2 = 0.999
ADAM_EPS = 1e-08
ADAM_WD = 0.01
ADAM_STEP = 10

N_DEV = 8
LANES = 128
PACK_W = 3712
MO_W = 1536
VMEM_LIMIT = 56 * 1024 * 1024
NEG = -1e30
ATTN_TILE = 512

MIX_ORIG_W = 2724
O_BCQ, O_BCKV, O_BKR, O_CQ, O_CK, O_CV, O_CF, O_DU, O_DV = 1024, 1280, 1408, 1440, 1696, 1952, 2208, 2212, 2468
P_B, P_KR, P_CQ, P_CKV, P_D, P_CF = 1024, 1408, 1536, 2048, 3072, 3584


_DN = {"nn": (((1,), (0,)), ((), ())), "nt": (((1,), (1,)), ((), ())), "tn": (((0,), (0,)), ((), ()))}


def _raw_bdot(a, b, mode):
    return lax.dot_general(a.astype(BF16), b.astype(BF16), _DN[mode], preferred_element_type=F32)


@functools.partial(jax.custom_vjp, nondiff_argnums=(2,))
def _bdot(a, b, mode):
    return _raw_bdot(a, b, mode)


def _bdot_fwd(a, b, mode):
    return _raw_bdot(a, b, mode), (a, b)


def _bdot_bwd(mode, res, g):
    a, b = res
    if mode == "nn":
        return _raw_bdot(g, b, "nt"), _raw_bdot(a, g, "tn")
    if mode == "nt":
        return _raw_bdot(g, b, "nn"), _raw_bdot(g, a, "tn")
    return _raw_bdot(b, g, "nt"), _raw_bdot(a, g, "nn")


_bdot.defvjp(_bdot_fwd, _bdot_bwd)


def _cparams(sem):
    return pltpu.CompilerParams(dimension_semantics=sem, vmem_limit_bytes=VMEM_LIMIT)


def _mix_in_src():
    src = -np.ones(PACK_W, np.int64)
    src[0:P_KR] = np.arange(0, O_BKR)
    src[P_KR + 64:P_KR + 80] = O_BKR + np.arange(16)
    src[P_KR + 96:P_KR + 112] = O_BKR + 16 + np.arange(16)
    for h in range(N_HEADS):
        src[P_CQ + 128 * h:P_CQ + 128 * h + 64] = O_CQ + 64 * h + np.arange(64)
        src[P_CKV + 256 * h:P_CKV + 256 * h + 64] = O_CK + 64 * h + np.arange(64)
        src[P_CKV + 256 * h + 128:P_CKV + 256 * h + 192] = O_CV + 64 * h + np.arange(64)
    src[P_D:P_D + 512] = O_DU + np.arange(512)
    src[P_CF:P_CF + 4] = O_CF + np.arange(4)
    return src


def _uq_src():
    src = -np.ones(512, np.int64)
    for h in range(N_HEADS):
        src[128 * h:128 * h + 64] = 96 * h + np.arange(64)
        src[128 * h + 64:128 * h + 80] = 96 * h + 64 + np.arange(16)
        src[128 * h + 96:128 * h + 112] = 96 * h + 80 + np.arange(16)
    return src


def _ukv_src():
    src = -np.ones(1024, np.int64)
    for h in range(N_HEADS):
        src[256 * h:256 * h + 64] = 128 * h + np.arange(64)
        src[256 * h + 128:256 * h + 192] = 128 * h + 64 + np.arange(64)
    return src


def _mo_src():
    src = -np.ones(MO_W, np.int64)
    src[0:256] = np.arange(256)
    for g in range(2):
        for h in range(N_HEADS):
            src[256 + 512 * g + 128 * h:256 + 512 * g + 128 * h + 64] = 256 + 256 * g + 64 * h + np.arange(64)
    src[1280:1536] = 768 + np.arange(256)
    return src


def _runs(idx):
    runs, i = [], 0
    while i < len(idx):
        j = i + 1
        while j < len(idx) and ((idx[i] < 0 and idx[j] < 0) or (idx[i] >= 0 and idx[j] == idx[i] + j - i)):
            j += 1
        runs.append((int(idx[i]), j - i))
        i = j
    return runs


def _take_runs(w, idx):
    parts = [jnp.zeros(w.shape[:-1] + (n,), w.dtype) if s < 0 else lax.slice_in_dim(w, s, s + n, axis=w.ndim - 1)
             for s, n in _runs(idx)]
    return jnp.concatenate(parts, axis=-1)


def _pack_cols(w, src):
    return _take_runs(w, src)


def _unpack_cols(wp, src, n):
    dst = np.zeros(n, np.int64)
    dst[src[src >= 0]] = np.nonzero(src >= 0)[0]
    return _take_runs(wp, dst)


def _rope_tables(seq):
    half = B_ROPE // 2
    inv_freq = ROPE_THETA ** (-jnp.arange(half, dtype=F32) / half)
    ang = jnp.arange(seq).astype(F32)[:, None] * inv_freq[None, :]
    cos, sin = jnp.cos(ang), jnp.sin(ang)
    z16 = jnp.zeros((seq, 16), F32)
    c = jnp.concatenate([jnp.ones((seq, 64), F32), cos, z16, cos, z16], axis=1)
    s1 = jnp.concatenate([jnp.zeros((seq, 64), F32), -sin, z16, z16, z16], axis=1)
    s2 = jnp.concatenate([jnp.zeros((seq, 64), F32), z16, z16, sin, z16], axis=1)
    return c, s1, s2


def _matmul(a, b, *, mode, group_out, out_dtype, tm, tk, name):
    ga, gb = a.shape[0], b.shape[0]
    g_n = max(ga, gb)
    if mode == "tn":
        k_dim, m_dim = a.shape[1:]
    else:
        m_dim, k_dim = a.shape[1:]
    n_dim = b.shape[1] if mode == "nt" else b.shape[2]
    assert m_dim % tm == 0 and k_dim % tk == 0
    kt = k_dim // tk
    n_red = kt if group_out else g_n * kt
    g_out = g_n if group_out else 1

    def split(g, r):
        return (g, r) if group_out else (r // kt, r % kt)

    def a_map(g, i, r):
        gg, kk = split(g, r)
        gg = gg if ga > 1 else 0
        return (gg, kk, i) if mode == "tn" else (gg, i, kk)

    def b_map(g, i, r):
        gg, kk = split(g, r)
        gg = gg if gb > 1 else 0
        return (gg, 0, kk) if mode == "nt" else (gg, kk, 0)

    a_blk = (None, tk, tm) if mode == "tn" else (None, tm, tk)
    b_blk = (None, n_dim, tk) if mode == "nt" else (None, tk, n_dim)
    dn = _DN[mode]

    def body(a_ref, b_ref, o_ref, *scratch):
        part = lax.dot_general(a_ref[...].astype(BF16), b_ref[...].astype(BF16), dn, preferred_element_type=F32)
        if n_red == 1:
            o_ref[...] = part.astype(o_ref.dtype)
            return
        acc_ref, = scratch
        r = pl.program_id(2)

        @pl.when(r == 0)
        def _():
            acc_ref[...] = part

        @pl.when(r > 0)
        def _():
            acc_ref[...] += part

        @pl.when(r == n_red - 1)
        def _():
            o_ref[...] = acc_ref[...].astype(o_ref.dtype)

    return pl.pallas_call(
        body, name=name, grid=(g_out, m_dim // tm, n_red),
        in_specs=[pl.BlockSpec(a_blk, a_map), pl.BlockSpec(b_blk, b_map)],
        out_specs=pl.BlockSpec((None, tm, n_dim), lambda g, i, r: (g, i, 0)),
        out_shape=jax.ShapeDtypeStruct((g_out, m_dim, n_dim), out_dtype),
        scratch_shapes=[] if n_red == 1 else [pltpu.VMEM((tm, n_dim), F32)],
        compiler_params=_cparams(("parallel", "parallel", "arbitrary")),
    )(a, b)


def _matmul_groupsum(a, b, *, out_dtype, tm, name):
    g_n, m_dim, k_dim = a.shape
    n_dim = b.shape[2]
    assert m_dim % tm == 0 and b.shape[:2] == (g_n, k_dim)

    def body(a_ref, b_ref, o_ref):
        acc = jnp.dot(a_ref[0], b_ref[0], preferred_element_type=F32)
        for g in range(1, g_n):
            acc = acc + jnp.dot(a_ref[g], b_ref[g], preferred_element_type=F32)
        o_ref[...] = acc.astype(o_ref.dtype)

    return pl.pallas_call(
        body, name=name, grid=(m_dim // tm,),
        in_specs=[pl.BlockSpec((g_n, tm, k_dim), lambda i: (0, i, 0)),
                  pl.BlockSpec((g_n, k_dim, n_dim), lambda i: (0, 0, 0))],
        out_specs=pl.BlockSpec((tm, n_dim), lambda i: (i, 0)),
        out_shape=jax.ShapeDtypeStruct((m_dim, n_dim), out_dtype),
        compiler_params=_cparams(("parallel",)),
    )(a, b)


def _row_spec(ts, d):
    return pl.BlockSpec((None, ts, d), lambda b, s: (b, s, 0))


def _mod_spec(d):
    return pl.BlockSpec((None, N_MOD, d), lambda b, s: (b, 0, 0))


def _vec_spec(d):
    return pl.BlockSpec((1, d), lambda b, s: (0, 0))


def _bvec_spec(d):
    return pl.BlockSpec((None, 1, d), lambda b, s: (b, 0, 0))


def _modulate(x, mod, sh_row, sc_row, name, ts=512):
    bsz, seq, d = x.shape

    def body(x_ref, mod_ref, o_ref):
        sh = mod_ref[sh_row:sh_row + 1, :]
        sc = mod_ref[sc_row:sc_row + 1, :]
        o_ref[...] = (x_ref[...] * (1.0 + sc) + sh).astype(o_ref.dtype)

    return pl.pallas_call(
        body, name=name, grid=(bsz, seq // ts),
        in_specs=[_row_spec(ts, d), _mod_spec(d)], out_specs=_row_spec(ts, d),
        out_shape=jax.ShapeDtypeStruct((bsz, seq, d), BF16),
        compiler_params=_cparams(("parallel", "parallel")),
    )(x, mod)


def _modulate_bwd(dh, x, mod, dx_res, sc_row, name, ts=512):
    bsz, seq, d = x.shape

    def body(dh_ref, x_ref, mod_ref, dxr_ref, dx_ref, dsh_ref, dsc_ref):
        s = pl.program_id(1)
        sc = mod_ref[sc_row:sc_row + 1, :]
        dh_v = dh_ref[...]
        dx_ref[...] = dxr_ref[...] + dh_v * (1.0 + sc)
        psh = jnp.sum(dh_v, axis=0, keepdims=True)
        psc = jnp.sum(dh_v * x_ref[...], axis=0, keepdims=True)

        @pl.when(s == 0)
        def _():
            dsh_ref[...] = psh
            dsc_ref[...] = psc

        @pl.when(s > 0)
        def _():
            dsh_ref[...] += psh
            dsc_ref[...] += psc

    return pl.pallas_call(
        body, name=name, grid=(bsz, seq // ts),
        in_specs=[_row_spec(ts, d), _row_spec(ts, d), _mod_spec(d), _row_spec(ts, d)],
        out_specs=[_row_spec(ts, d), _bvec_spec(d), _bvec_spec(d)],
        out_shape=[jax.ShapeDtypeStruct((bsz, seq, d), F32), jax.ShapeDtypeStruct((bsz, 1, d), F32),
                   jax.ShapeDtypeStruct((bsz, 1, d), F32)],
        compiler_params=_cparams(("parallel", "arbitrary")),
    )(dh, x, mod, dx_res)


def _res_ln_fn(x, f, g, lng, lnb, cmul):
    r = ALPHA * x + (cmul * (1.0 + g)) * f
    mu = jnp.mean(r, axis=-1, keepdims=True)
    rc = r - mu
    var = jnp.mean(rc * rc, axis=-1, keepdims=True)
    return rc * lax.rsqrt(var + LN_EPS) * lng + lnb


def _res_ln(x, f, mod, lng, lnb, g_row, cmul, name, nxt=None, ts=512):
    bsz, seq, d = x.shape

    def body(*refs):
        x_ref, f_ref, mod_ref, lng_ref, lnb_ref = refs[:5]
        g = mod_ref[g_row:g_row + 1, :]
        y = _res_ln_fn(x_ref[...], f_ref[...], g, lng_ref[...], lnb_ref[...], cmul)
        if nxt is None:
            refs[5][...] = y
            return
        nmod_ref, o_ref, h_ref = refs[5:]
        o_ref[...] = y
        sh = nmod_ref[nxt[1]:nxt[1] + 1, :]
        sc = nmod_ref[nxt[2]:nxt[2] + 1, :]
        h_ref[...] = (y * (1.0 + sc) + sh).astype(h_ref.dtype)

    in_specs = [_row_spec(ts, d), _row_spec(ts, d), _mod_spec(d), _vec_spec(d), _vec_spec(d)]
    args = [x, f, mod, lng, lnb]
    out_specs, out_shape = [_row_spec(ts, d)], [jax.ShapeDtypeStruct((bsz, seq, d), F32)]
    if nxt is not None:
        in_specs.append(_mod_spec(d))
        args.append(nxt[0])
        out_specs.append(_row_spec(ts, d))
        out_shape.append(jax.ShapeDtypeStruct((bsz, seq, d), BF16))
    res = pl.pallas_call(
        body, name=name, grid=(bsz, seq // ts), in_specs=in_specs, out_specs=out_specs, out_shape=out_shape,
        compiler_params=_cparams(("parallel", "parallel")),
    )(*args)
    return (res[0], res[1]) if nxt is not None else (res[0], None)


def _res_ln_bwd(dy, x, f, mod, lng, lnb, g_row, cmul, name, ts=256):
    bsz, seq, d = x.shape

    def body(dy_ref, x_ref, f_ref, mod_ref, lng_ref, lnb_ref, dx_ref, df_ref, dg_ref, dlg_ref, dlb_ref):
        b, s = pl.program_id(0), pl.program_id(1)
        g = mod_ref[g_row:g_row + 1, :]
        _, vjp = jax.vjp(functools.partial(_res_ln_fn, cmul=cmul), x_ref[...], f_ref[...], g, lng_ref[...],
                         lnb_ref[...])
        dx, df, dg, dlg, dlb = vjp(dy_ref[...])
        dx_ref[...] = dx
        df_ref[...] = df.astype(df_ref.dtype)

        @pl.when(s == 0)
        def _():
            dg_ref[...] = dg

        @pl.when(s > 0)
        def _():
            dg_ref[...] += dg

        first = jnp.logical_and(b == 0, s == 0)

        @pl.when(first)
        def _():
            dlg_ref[...] = dlg
            dlb_ref[...] = dlb

        @pl.when(jnp.logical_not(first))
        def _():
            dlg_ref[...] += dlg
            dlb_ref[...] += dlb

    return pl.pallas_call(
        body, name=name, grid=(bsz, seq // ts),
        in_specs=[_row_spec(ts, d), _row_spec(ts, d), _row_spec(ts, d), _mod_spec(d), _vec_spec(d), _vec_spec(d)],
        out_specs=[_row_spec(ts, d), _row_spec(ts, d), _bvec_spec(d), _vec_spec(d), _vec_spec(d)],
        out_shape=[jax.ShapeDtypeStruct((bsz, seq, d), F32), jax.ShapeDtypeStruct((bsz, seq, d), BF16),
                   jax.ShapeDtypeStruct((bsz, 1, d), F32), jax.ShapeDtypeStruct((1, d), F32),
                   jax.ShapeDtypeStruct((1, d), F32)],
        compiler_params=_cparams(("arbitrary", "arbitrary")),
    )(dy, x, f, mod, lng, lnb)


def _loss_head(y, target, name, ts=512):
    bsz, seq, d = y.shape
    n_s = seq // ts

    def body(y_ref, t_ref, dy_ref, loss_ref, acc_ref):
        b, s = pl.program_id(0), pl.program_id(1)
        err = y_ref[...] - t_ref[...]
        dy_ref[...] = err * (1.0 / d)
        part = jnp.sum(err * err, axis=0, keepdims=True)
        first = jnp.logical_and(b == 0, s == 0)

        @pl.when(first)
        def _():
            acc_ref[...] = part

        @pl.when(jnp.logical_not(first))
        def _():
            acc_ref[...] += part

        @pl.when(jnp.logical_and(b == bsz - 1, s == n_s - 1))
        def _():
            loss_ref[...] = jnp.sum(acc_ref[...], axis=1, keepdims=True) * (0.5 / d)

    return pl.pallas_call(
        body, name=name, grid=(bsz, n_s),
        in_specs=[_row_spec(ts, d), _row_spec(ts, d)],
        out_specs=[_row_spec(ts, d), pl.BlockSpec((1, 1), lambda b, s: (0, 0))],
        out_shape=[jax.ShapeDtypeStruct((bsz, seq, d), F32), jax.ShapeDtypeStruct((1, 1), F32)],
        scratch_shapes=[pltpu.VMEM((1, d), F32)],
        compiler_params=_cparams(("arbitrary", "arbitrary")),
    )(y, target)


def _ffn_in_swiglu(h, w_in_t, name, tm=1024):
    t, d = h.shape
    n_sh, w, _ = w_in_t.shape
    half = n_sh // 2

    def body(h_ref, w_ref, z_ref, a_ref):
        hv = h_ref[...]
        g = lax.dot_general(hv, w_ref[0], _DN["nt"], preferred_element_type=F32)
        u = lax.dot_general(hv, w_ref[1], _DN["nt"], preferred_element_type=F32)
        z_ref[0] = g.astype(z_ref.dtype)
        z_ref[1] = u.astype(z_ref.dtype)
        a_ref[...] = (g * jax.nn.sigmoid(g) * u).astype(a_ref.dtype)

    return pl.pallas_call(
        body, name=name, grid=(half, t // tm),
        in_specs=[pl.BlockSpec((tm, d), lambda g, i: (i, 0)),
                  pl.BlockSpec((2, None, w, d), lambda g, i: (0, g, 0, 0))],
        out_specs=[pl.BlockSpec((2, None, tm, w), lambda g, i: (0, g, i, 0)),
                   pl.BlockSpec((None, tm, w), lambda g, i: (g, i, 0))],
        out_shape=[jax.ShapeDtypeStruct((2, half, t, w), BF16), jax.ShapeDtypeStruct((half, t, w), BF16)],
        compiler_params=_cparams(("parallel", "parallel")),
    )(h, w_in_t.reshape(2, half, w, d))


def _ffn_out_dx_swiglu(df, w_out, z, name, tm=1024):
    t, d = df.shape
    half, w, _ = w_out.shape

    def body(df_ref, w_ref, z_ref, dz_ref):
        da = lax.dot_general(df_ref[...], w_ref[...], _DN["nt"], preferred_element_type=F32)
        g = z_ref[0].astype(F32)
        u = z_ref[1].astype(F32)
        sig = jax.nn.sigmoid(g)
        dz_ref[0] = (da * u * (sig * (1.0 + g * (1.0 - sig)))).astype(dz_ref.dtype)
        dz_ref[1] = (da * (g * sig)).astype(dz_ref.dtype)

    zspec = pl.BlockSpec((2, None, tm, w), lambda g, i: (0, g, i, 0))
    return pl.pallas_call(
        body, name=name, grid=(half, t // tm),
        in_specs=[pl.BlockSpec((tm, d), lambda g, i: (i, 0)), pl.BlockSpec((None, w, d), lambda g, i: (g, 0, 0)),
                  zspec],
        out_specs=zspec, out_shape=jax.ShapeDtypeStruct(z.shape, BF16),
        compiler_params=_cparams(("parallel", "parallel")),
    )(df, w_out, z)


def _log_sigmoid(x):
    return jnp.minimum(x, 0.0) - jnp.log(1.0 + jnp.exp(-jnp.abs(x)))


def _hgrn_consts():
    r = lax.broadcasted_iota(jnp.int32, (GROUP_WIDTH, GROUP_WIDTH), 0)
    c = lax.broadcasted_iota(jnp.int32, (GROUP_WIDTH, GROUP_WIDTH), 1)
    bd = (r // HEAD_DIM == c // HEAD_DIM).astype(F32)
    r16 = lax.broadcasted_iota(jnp.int32, (A_CHUNK, A_CHUNK), 0)
    c16 = lax.broadcasted_iota(jnp.int32, (A_CHUNK, A_CHUNK), 1)
    tril = (r16 >= c16).astype(F32)
    rows = lax.broadcasted_iota(jnp.int32, (A_CHUNK, GROUP_WIDTH), 0)
    return bd, tril, rows


def _hgrn_lb(logits8, layer):
    rows = lax.broadcasted_iota(jnp.int32, logits8.shape, 0)
    valid = rows < DEPTH
    mx = jnp.max(jnp.where(valid, logits8, NEG), axis=0, keepdims=True)
    e = jnp.where(valid, jnp.exp(logits8 - mx), 0.0)
    sm = e / jnp.sum(e, axis=0, keepdims=True)
    pick = jnp.logical_and(rows >= 1, rows <= layer)
    return jnp.sum(jnp.where(pick, sm, 0.0), axis=0, keepdims=True)


def _hgrn_chunk(aq, af, ai, ag, logits8, norm_g, st, *, layer, consts):
    bd, tril, rows = consts
    lb = _hgrn_lb(logits8, layer)
    la = jnp.log(jnp.maximum(lb, LB_FLOOR))
    b2 = jnp.log(1.0 - lb) + _log_sigmoid(af)
    log_f = jnp.maximum(la, b2) + jnp.log(1.0 + jnp.exp(-jnp.abs(la - b2)))
    k = 1.0 - jnp.exp(log_f)
    qf = aq * jax.nn.sigmoid(aq)
    g_cum = jnp.dot(tril, log_f, precision=HI, preferred_element_type=F32)

    c, w = A_CHUNK, GROUP_WIDTH

    def by_key(v):
        return jnp.broadcast_to(v[:, None, :], (c, c, w))

    def by_query(v):
        return jnp.broadcast_to(v[None, :, :], (c, c, w))

    s_i = lax.broadcasted_iota(jnp.int32, (c, c, w), 0)
    t_i = lax.broadcasted_iota(jnp.int32, (c, c, w), 1)
    rel = jnp.where(t_i >= s_i, by_query(g_cum) - by_key(g_cum), NEG)
    pairs = by_query(qf) * by_key(k) * jnp.exp(rel)
    a_all = _bdot(pairs.reshape(c * c, w), bd, "nn").reshape(c, c, w)
    o = jnp.sum(a_all * by_key(ai), axis=0)
    q_dec = qf * jnp.exp(g_cum)
    o = o + _bdot(q_dec, st, "nt")
    g_last = jnp.sum(jnp.where(rows == c - 1, g_cum, 0.0), axis=0, keepdims=True)
    k_end = k * jnp.exp(g_last - g_cum)
    kv = _bdot(ai, k_end, "tn")
    st_new = st * jnp.exp(g_last) + kv * bd
    ms = _bdot(o * o, bd, "nn") * (1.0 / HEAD_DIM)
    o = o * lax.rsqrt(ms + RMS_EPS) * norm_g
    return o * (ag * jax.nn.sigmoid(ag)), st_new


def _hgrn_fwd(proj, logits8, norm_g, layer, name, ts=256):
    bsz, seq, _ = proj.shape
    n_ch = ts // A_CHUNK

    def body(p_ref, lg_ref, ng_ref, o_ref, st_ref, st_scr):
        @pl.when(pl.program_id(1) == 0)
        def _():
            st_scr[...] = jnp.zeros_like(st_scr)

        consts = _hgrn_consts()
        logits_v, ng_v = lg_ref[...], ng_ref[...]

        def chunk(ci, carry):
            r = pl.multiple_of(ci * A_CHUNK, A_CHUNK)
            st = st_scr[...]
            st_ref[ci] = st
            o, st_new = _hgrn_chunk(
                p_ref[pl.ds(r, A_CHUNK), 0:256], p_ref[pl.ds(r, A_CHUNK), 256:512],
                p_ref[pl.ds(r, A_CHUNK), 512:768], p_ref[pl.ds(r, A_CHUNK), 768:1024],
                logits_v, ng_v, st, layer=layer, consts=consts)
            o_ref[pl.ds(r, A_CHUNK), :] = o.astype(o_ref.dtype)
            st_scr[...] = st_new
            return carry

        lax.fori_loop(0, n_ch, chunk, 0, unroll=2)

    return pl.pallas_call(
        body, name=name, grid=(bsz, seq // ts),
        in_specs=[pl.BlockSpec((None, ts, 1024), lambda b, s: (b, s, 0)),
                  pl.BlockSpec((8, GROUP_WIDTH), lambda b, s: (0, 0)),
                  pl.BlockSpec((1, GROUP_WIDTH), lambda b, s: (0, 0))],
        out_specs=[pl.BlockSpec((None, ts, GROUP_WIDTH), lambda b, s: (b, s, 0)),
                   pl.BlockSpec((None, n_ch, GROUP_WIDTH, GROUP_WIDTH), lambda b, s: (b, s, 0, 0))],
        out_shape=[jax.ShapeDtypeStruct((bsz, seq, MO_W), BF16),
                   jax.ShapeDtypeStruct((bsz, seq // A_CHUNK, GROUP_WIDTH, GROUP_WIDTH), F32)],
        scratch_shapes=[pltpu.VMEM((GROUP_WIDTH, GROUP_WIDTH), F32)],
        compiler_params=_cparams(("parallel", "arbitrary")),
    )(proj, logits8, norm_g)


def _hgrn_bwd(dmo, proj, states, logits8, norm_g, layer, name, ts=256):
    bsz, seq, _ = proj.shape
    n_ch = ts // A_CHUNK
    n_s = seq // ts

    def body(do_ref, p_ref, st_ref, lg_ref, ng_ref, dp_ref, dlg_ref, dng_ref, dst_scr):
        b, s = pl.program_id(0), pl.program_id(1)

        @pl.when(s == 0)
        def _():
            dst_scr[...] = jnp.zeros_like(dst_scr)

        @pl.when(jnp.logical_and(b == 0, s == 0))
        def _():
            dlg_ref[...] = jnp.zeros_like(dlg_ref)
            dng_ref[...] = jnp.zeros_like(dng_ref)

        consts = _hgrn_consts()
        logits_v, ng_v = lg_ref[...], ng_ref[...]
        fn = functools.partial(_hgrn_chunk, layer=layer, consts=consts)

        def chunk(t, carry):
            ci = n_ch - 1 - t
            r = pl.multiple_of(ci * A_CHUNK, A_CHUNK)
            _, vjp = jax.vjp(
                fn, p_ref[pl.ds(r, A_CHUNK), 0:256], p_ref[pl.ds(r, A_CHUNK), 256:512],
                p_ref[pl.ds(r, A_CHUNK), 512:768], p_ref[pl.ds(r, A_CHUNK), 768:1024],
                logits_v, ng_v, st_ref[ci])
            daq, daf, dai, dag, dlg, dng, dst = vjp((do_ref[pl.ds(r, A_CHUNK), :], dst_scr[...]))
            dp_ref[pl.ds(r, A_CHUNK), 0:256] = daq.astype(dp_ref.dtype)
            dp_ref[pl.ds(r, A_CHUNK), 256:512] = daf.astype(dp_ref.dtype)
            dp_ref[pl.ds(r, A_CHUNK), 512:768] = dai.astype(dp_ref.dtype)
            dp_ref[pl.ds(r, A_CHUNK), 768:1024] = dag.astype(dp_ref.dtype)
            dlg_ref[...] += dlg
            dng_ref[...] += dng
            dst_scr[...] = dst
            return carry

        lax.fori_loop(0, n_ch, chunk, 0, unroll=2)

    rev = lambda b, s: (b, n_s - 1 - s, 0)
    return pl.pallas_call(
        body, name=name, grid=(bsz, n_s),
        in_specs=[pl.BlockSpec((None, ts, GROUP_WIDTH), rev),
                  pl.BlockSpec((None, ts, 1024), rev),
                  pl.BlockSpec((None, n_ch, GROUP_WIDTH, GROUP_WIDTH), lambda b, s: (b, n_s - 1 - s, 0, 0)),
                  pl.BlockSpec((8, GROUP_WIDTH), lambda b, s: (0, 0)),
                  pl.BlockSpec((1, GROUP_WIDTH), lambda b, s: (0, 0))],
        out_specs=[pl.BlockSpec((None, ts, 1024), rev),
                   pl.BlockSpec((8, GROUP_WIDTH), lambda b, s: (0, 0)),
                   pl.BlockSpec((1, GROUP_WIDTH), lambda b, s: (0, 0))],
        out_shape=[jax.ShapeDtypeStruct((bsz, seq, PACK_W), BF16),
                   jax.ShapeDtypeStruct((8, GROUP_WIDTH), F32), jax.ShapeDtypeStruct((1, GROUP_WIDTH), F32)],
        scratch_shapes=[pltpu.VMEM((GROUP_WIDTH, GROUP_WIDTH), F32)],
        compiler_params=_cparams(("arbitrary", "arbitrary")),
    )(dmo, proj, states, logits8, norm_g)


def _rms_fn(x, g):
    return x * lax.rsqrt(jnp.mean(x * x, axis=-1, keepdims=True) + RMS_EPS) * g


def _tile4(t):
    return jnp.concatenate([t, t, t, t], axis=1)


def _rope(x, c, s1, s2):
    w = x.shape[-1]
    return x * c + pltpu.roll(x, 32, axis=1) * s2 + pltpu.roll(x, w - 32, axis=1) * s1


def _rope_t(dy, c, s1, s2):
    w = dy.shape[-1]
    return dy * c + pltpu.roll(dy * s2, w - 32, axis=1) + pltpu.roll(dy * s1, 32, axis=1)


def _mla_pre(proj, qg, kvg, wq, wkv, tabs, name, ts=256):
    bsz, seq, _ = proj.shape

    def body(p_ref, qg_ref, kvg_ref, wq_ref, wkv_ref, c_ref, s1_ref, s2_ref, q_ref, kv_ref):
        nq = _rms_fn(p_ref[:, 0:256], qg_ref[...])
        nkv = _rms_fn(p_ref[:, 256:384], kvg_ref[...])
        c, s1, s2 = c_ref[...], s1_ref[...], s2_ref[...]
        qp = jnp.dot(nq.astype(BF16), wq_ref[...], preferred_element_type=F32)
        q_ref[...] = _rope(qp, _tile4(c), _tile4(s1), _tile4(s2)).astype(q_ref.dtype)
        kv = jnp.dot(nkv.astype(BF16), wkv_ref[...], preferred_element_type=F32)
        krr = _rope(p_ref[:, 384:512], c, s1, s2)
        zero = jnp.zeros_like(krr)
        kv_ref[...] = (kv + jnp.concatenate([krr, zero] * N_HEADS, axis=1)).astype(kv_ref.dtype)

    tab_spec = pl.BlockSpec((ts, LANES), lambda b, s: (s, 0))
    return pl.pallas_call(
        body, name=name, grid=(bsz, seq // ts),
        in_specs=[pl.BlockSpec((None, ts, 512), lambda b, s: (b, s, P_B // 512)),
                  _vec_spec(256), _vec_spec(128),
                  pl.BlockSpec((256, 512), lambda b, s: (0, 0)), pl.BlockSpec((128, 1024), lambda b, s: (0, 0)),
                  tab_spec, tab_spec, tab_spec],
        out_specs=[_row_spec(ts, 512), _row_spec(ts, 1024)],
        out_shape=[jax.ShapeDtypeStruct((bsz, seq, 512), BF16), jax.ShapeDtypeStruct((bsz, seq, 1024), BF16)],
        compiler_params=_cparams(("parallel", "parallel")),
    )(proj, qg, kvg, wq, wkv, *tabs)


def _mla_pre_bwd(dq, dkv, dproj, proj, qg, kvg, wq, wkv, tabs, name, ts=256):
    bsz, seq, _ = proj.shape

    def body(dq_ref, dkv_ref, dp_any, p_ref, qg_ref, kvg_ref, wq_ref, wkv_ref, c_ref, s1_ref, s2_ref,
             dp_ref, dqg_ref, dkvg_ref, dwq_ref, dwkv_ref):
        del dp_any
        first = jnp.logical_and(pl.program_id(0) == 0, pl.program_id(1) == 0)

        @pl.when(first)
        def _():
            dqg_ref[...] = jnp.zeros_like(dqg_ref)
            dkvg_ref[...] = jnp.zeros_like(dkvg_ref)
            dwq_ref[...] = jnp.zeros_like(dwq_ref)
            dwkv_ref[...] = jnp.zeros_like(dwkv_ref)

        c, s1, s2 = c_ref[...], s1_ref[...], s2_ref[...]
        nq, vjp_q = jax.vjp(_rms_fn, p_ref[:, 0:256], qg_ref[...])
        nkv, vjp_kv = jax.vjp(_rms_fn, p_ref[:, 256:384], kvg_ref[...])
        dqp = _rope_t(dq_ref[...], _tile4(c), _tile4(s1), _tile4(s2)).astype(BF16)
        dkv_v = dkv_ref[...]
        dkv_b = dkv_v.astype(BF16)
        tn = (((0,), (0,)), ((), ()))
        nt = (((1,), (1,)), ((), ()))
        dwq_ref[...] += lax.dot_general(nq.astype(BF16), dqp, tn, preferred_element_type=F32)
        dwkv_ref[...] += lax.dot_general(nkv.astype(BF16), dkv_b, tn, preferred_element_type=F32)
        dcq, dqg = vjp_q(lax.dot_general(dqp, wq_ref[...], nt, preferred_element_type=F32))
        dckv, dkvg = vjp_kv(lax.dot_general(dkv_b, wkv_ref[...], nt, preferred_element_type=F32))
        dqg_ref[...] += dqg
        dkvg_ref[...] += dkvg
        dk_sum = dkv_v[:, 0:128] + dkv_v[:, 256:384] + dkv_v[:, 512:640] + dkv_v[:, 768:896]
        lane = lax.broadcasted_iota(jnp.int32, dk_sum.shape, 1)
        dkr = jnp.where(lane >= 64, _rope_t(dk_sum, c, s1, s2), 0.0)
        dp_ref[:, 0:256] = dcq.astype(dp_ref.dtype)
        dp_ref[:, 256:384] = dckv.astype(dp_ref.dtype)
        dp_ref[:, 384:512] = dkr.astype(dp_ref.dtype)

    tab_spec = pl.BlockSpec((ts, LANES), lambda b, s: (s, 0))
    const = lambda shape: pl.BlockSpec(shape, lambda b, s: (0, 0))
    return pl.pallas_call(
        body, name=name, grid=(bsz, seq // ts),
        in_specs=[_row_spec(ts, 512), _row_spec(ts, 1024), pl.BlockSpec(memory_space=pl.ANY),
                  pl.BlockSpec((None, ts, 512), lambda b, s: (b, s, P_B // 512)),
                  _vec_spec(256), _vec_spec(128), const((256, 512)), const((128, 1024)),
                  tab_spec, tab_spec, tab_spec],
        out_specs=[pl.BlockSpec((None, ts, 512), lambda b, s: (b, s, P_B // 512)),
                   _vec_spec(256), _vec_spec(128), const((256, 512)), const((128, 1024))],
        out_shape=[jax.ShapeDtypeStruct(dproj.shape, dproj.dtype), jax.ShapeDtypeStruct((1, 256), F32),
                   jax.ShapeDtypeStruct((1, 128), F32), jax.ShapeDtypeStruct((256, 512), F32),
                   jax.ShapeDtypeStruct((128, 1024), F32)],
        input_output_aliases={2: 0},
        compiler_params=_cparams(("arbitrary", "arbitrary")),
    )(dq, dkv, dproj, proj, qg, kvg, wq, wkv, *tabs)


def _fox_gate(proj, bf, name):
    bsz, seq, _ = proj.shape
    n_blk = seq // LANES

    def body(x_ref, bf_ref, f_ref):
        r_i = lax.broadcasted_iota(jnp.int32, (LANES, LANES), 0)
        c_i = lax.broadcasted_iota(jnp.int32, (LANES, LANES), 1)
        tril = (r_i >= c_i).astype(F32)
        bias = bf_ref[...]

        def blk(i, carry):
            r = pl.multiple_of(i * LANES, LANES)
            lf = _log_sigmoid(x_ref[pl.ds(r, LANES), :] + bias)
            f_ref[pl.ds(r, LANES), :] = jnp.dot(tril, lf, precision=HI, preferred_element_type=F32) + carry
            return carry + jnp.sum(lf, axis=0, keepdims=True)

        lax.fori_loop(0, n_blk, blk, jnp.zeros((1, LANES), F32))

    return pl.pallas_call(
        body, name=name, grid=(bsz,),
        in_specs=[pl.BlockSpec((None, seq, LANES), lambda b: (b, 0, P_CF // LANES)),
                  pl.BlockSpec((1, LANES), lambda b: (0, 0))],
        out_specs=pl.BlockSpec((None, seq, LANES), lambda b: (b, 0, 0)),
        out_shape=jax.ShapeDtypeStruct((bsz, seq, LANES), F32),
        compiler_params=_cparams(("parallel",)),
    )(proj, bf)


def _fox_gate_bwd(dfq, dfk_cols, dproj, proj, bf, name):
    bsz, seq, _ = proj.shape
    n_blk = seq // LANES

    def body(dfq_ref, dfk_ref, dp_any, x_ref, bf_ref, dp_ref, dbf_ref):
        del dp_any

        @pl.when(pl.program_id(0) == 0)
        def _():
            dbf_ref[...] = jnp.zeros_like(dbf_ref)

        r_i = lax.broadcasted_iota(jnp.int32, (LANES, LANES), 0)
        c_i = lax.broadcasted_iota(jnp.int32, (LANES, LANES), 1)
        triu = (r_i <= c_i).astype(F32)
        bias = bf_ref[...]

        def blk(t, carry):
            tail, dbf = carry
            r = pl.multiple_of((n_blk - 1 - t) * LANES, LANES)
            dc = dfk_ref[pl.ds(r, LANES), :]
            for hd in range(N_HEADS):
                dc = dc + jnp.where(c_i == hd, dfq_ref[hd, pl.ds(r, LANES), :], 0.0)
            dlf = jnp.dot(triu, dc, precision=HI, preferred_element_type=F32) + tail
            dx = dlf * (1.0 - jax.nn.sigmoid(x_ref[pl.ds(r, LANES), :] + bias))
            dp_ref[pl.ds(r, LANES), :] = dx.astype(dp_ref.dtype)
            return tail + jnp.sum(dc, axis=0, keepdims=True), dbf + jnp.sum(dx, axis=0, keepdims=True)

        z = jnp.zeros((1, LANES), F32)
        _, dbf = lax.fori_loop(0, n_blk, blk, (z, z))
        dbf_ref[...] += dbf

    return pl.pallas_call(
        body, name=name, grid=(bsz,),
        in_specs=[pl.BlockSpec((None, N_HEADS, seq, LANES), lambda b: (b, 0, 0, 0)),
                  pl.BlockSpec((None, seq, LANES), lambda b: (b, 0, 0)), pl.BlockSpec(memory_space=pl.ANY),
                  pl.BlockSpec((None, seq, LANES), lambda b: (b, 0, P_CF // LANES)),
                  pl.BlockSpec((1, LANES), lambda b: (0, 0))],
        out_specs=[pl.BlockSpec((None, seq, LANES), lambda b: (b, 0, P_CF // LANES)),
                   pl.BlockSpec((1, LANES), lambda b: (0, 0))],
        out_shape=[jax.ShapeDtypeStruct(dproj.shape, dproj.dtype), jax.ShapeDtypeStruct((1, LANES), F32)],
        input_output_aliases={2: 0},
        compiler_params=_cparams(("arbitrary",)),
    )(dfq, dfk_cols, dproj, proj, bf)


def _gate_terms(fc_ref, fr_ref, h, tq, tk):
    lane = lax.broadcasted_iota(jnp.int32, (tq, LANES), 1)
    fcol = jnp.sum(jnp.where(lane == h, fc_ref[...], 0.0), axis=1, keepdims=True)
    sub = lax.broadcasted_iota(jnp.int32, (8, tk), 0)
    frow = jnp.sum(jnp.where(sub == h, fr_ref[...], 0.0), axis=0, keepdims=True)
    return fcol - frow


def _scores(q_ref, k_ref, gate_refs, scale, h, masked, tq, tk):
    q = (q_ref[...].astype(F32) * scale).astype(BF16)
    s = lax.dot_general(q, k_ref[...].astype(BF16), _DN["nt"], preferred_element_type=F32)
    if gate_refs is not None:
        s = s + _gate_terms(gate_refs[0], gate_refs[1], h, tq, tk)
    if masked is not False:
        r_i = lax.broadcasted_iota(jnp.int32, (tq, tk), 0)
        c_i = lax.broadcasted_iota(jnp.int32, (tq, tk), 1)
        keep = c_i <= r_i
        s = jnp.where(keep if masked is True else jnp.logical_or(jnp.logical_not(masked), keep), s, NEG)
    return s, q


def _lanes(col):
    return jnp.broadcast_to(col, (col.shape[0], LANES))


def _attn_fwd(qa, q0, kva, kv0, mo, o0, gates, scale, name, tq=None):
    bsz, seq, _ = qa.shape
    tq = ATTN_TILE if tq is None else tq
    n_q = seq // tq
    gated = gates is not None

    def body(*refs):
        q_ref, k_ref, v_ref = refs[:3]
        gate_refs = refs[3:5] if gated else None
        o_ref, lse_ref, m_s, l_s, acc_s = refs[-5:]
        h, i, j = pl.program_id(1), pl.program_id(2), pl.program_id(3)

        @pl.when(j == 0)
        def _():
            m_s[...] = jnp.full_like(m_s, NEG)
            l_s[...] = jnp.zeros_like(l_s)
            acc_s[...] = jnp.zeros_like(acc_s)

        def step(masked):
            s, _ = _scores(q_ref, k_ref, gate_refs, scale, h, masked, tq, tq)
            m_prev = m_s[...]
            m_new = jnp.maximum(m_prev, jnp.max(s, axis=1, keepdims=True))
            alpha = jnp.exp(m_prev - m_new)
            p = jnp.exp(s - m_new)
            l_s[...] = alpha * l_s[...] + jnp.sum(p, axis=1, keepdims=True)
            acc_s[...] = alpha * acc_s[...] + jnp.dot(p.astype(BF16), v_ref[...].astype(BF16),
                                                      preferred_element_type=F32)
            m_s[...] = m_new

        @pl.when(j <= i)
        def _():
            step(j == i)

        @pl.when(j == i)
        def _():
            o_ref[...] = (acc_s[...] / l_s[...]).astype(o_ref.dtype)
            lse_ref[...] = _lanes(m_s[...] + jnp.log(l_s[...]))

    blk = (None, tq, LANES)
    in_specs = [pl.BlockSpec(blk, lambda b, h, i, j: (b, i, q0 + h)),
                pl.BlockSpec(blk, lambda b, h, i, j: (b, jnp.minimum(j, i), kv0 + 2 * h)),
                pl.BlockSpec(blk, lambda b, h, i, j: (b, jnp.minimum(j, i), kv0 + 2 * h + 1))]
    args = [qa, kva, kva]
    if gated:
        in_specs += [pl.BlockSpec(blk, lambda b, h, i, j: (b, i, 0)),
                     pl.BlockSpec((None, 8, tq), lambda b, h, i, j: (b, 0, jnp.minimum(j, i)))]
        args += list(gates)
    in_specs.append(pl.BlockSpec(memory_space=pl.ANY))
    args.append(mo)
    return pl.pallas_call(
        body, name=name, grid=(bsz, N_HEADS, n_q, n_q), in_specs=in_specs,
        out_specs=[pl.BlockSpec(blk, lambda b, h, i, j: (b, i, o0 + h)),
                   pl.BlockSpec((None, None, tq, LANES), lambda b, h, i, j: (b, h, i, 0))],
        out_shape=[jax.ShapeDtypeStruct(mo.shape, mo.dtype),
                   jax.ShapeDtypeStruct((bsz, N_HEADS, seq, LANES), F32)],
        scratch_shapes=[pltpu.VMEM((tq, 1), F32), pltpu.VMEM((tq, 1), F32), pltpu.VMEM((tq, LANES), F32)],
        input_output_aliases={len(args) - 1: 0},
        compiler_params=_cparams(("parallel", "parallel", "parallel", "arbitrary")),
    )(*args)


def _attn_bwd_q(qa, q0, kva, kv0, mo, dmo, o0, lse, gates, scale, out, out0, name, tq=None):
    bsz, seq, _ = qa.shape
    tq = ATTN_TILE if tq is None else tq
    n_q = seq // tq
    gated = gates is not None
    aliased = not isinstance(out, jax.ShapeDtypeStruct)

    def body(*refs):
        q_ref, k_ref, v_ref, o_ref, do_ref, lse_ref = refs[:6]
        gate_refs = refs[6:8] if gated else None
        dq_ref, delta_ref, dfq_ref, acc_s, dl_s, df_s = refs[-6:]
        h, i, j = pl.program_id(1), pl.program_id(2), pl.program_id(3)

        @pl.when(j == 0)
        def _():
            acc_s[...] = jnp.zeros_like(acc_s)
            df_s[...] = jnp.zeros_like(df_s)
            dl_s[...] = jnp.sum(do_ref[...] * o_ref[...].astype(F32), axis=1, keepdims=True)

        def step(masked):
            s, _ = _scores(q_ref, k_ref, gate_refs, scale, h, masked, tq, tq)
            p = jnp.exp(s - lse_ref[:, 0:1])
            dp = lax.dot_general(do_ref[...].astype(BF16), v_ref[...].astype(BF16), _DN["nt"],
                                 preferred_element_type=F32)
            ds = p * (dp - dl_s[...])
            acc_s[...] += jnp.dot(ds.astype(BF16), k_ref[...].astype(BF16), preferred_element_type=F32)
            df_s[...] += jnp.sum(ds, axis=1, keepdims=True)

        @pl.when(j <= i)
        def _():
            step(j == i)

        @pl.when(j == i)
        def _():
            dq_ref[...] = (acc_s[...] * scale).astype(dq_ref.dtype)
            delta_ref[...] = _lanes(dl_s[...])
            dfq_ref[...] = _lanes(df_s[...])

    blk = (None, tq, LANES)
    col = pl.BlockSpec((None, None, tq, LANES), lambda b, h, i, j: (b, h, i, 0))
    in_specs = [pl.BlockSpec(blk, lambda b, h, i, j: (b, i, q0 + h)),
                pl.BlockSpec(blk, lambda b, h, i, j: (b, jnp.minimum(j, i), kv0 + 2 * h)),
                pl.BlockSpec(blk, lambda b, h, i, j: (b, jnp.minimum(j, i), kv0 + 2 * h + 1)),
                pl.BlockSpec(blk, lambda b, h, i, j: (b, i, o0 + h)),
                pl.BlockSpec(blk, lambda b, h, i, j: (b, i, o0 + h)), col]
    args = [qa, kva, kva, mo, dmo, lse]
    if gated:
        in_specs += [pl.BlockSpec(blk, lambda b, h, i, j: (b, i, 0)),
                     pl.BlockSpec((None, 8, tq), lambda b, h, i, j: (b, 0, jnp.minimum(j, i)))]
        args += list(gates)
    aliases = {}
    if aliased:
        in_specs.append(pl.BlockSpec(memory_space=pl.ANY))
        args.append(out)
        aliases = {len(args) - 1: 0}
    vec = jax.ShapeDtypeStruct((bsz, N_HEADS, seq, LANES), F32)
    return pl.pallas_call(
        body, name=name, grid=(bsz, N_HEADS, n_q, n_q), in_specs=in_specs,
        out_specs=[pl.BlockSpec(blk, lambda b, h, i, j: (b, i, out0 + h)), col, col],
        out_shape=[jax.ShapeDtypeStruct(out.shape, out.dtype), vec, vec],
        scratch_shapes=[pltpu.VMEM((tq, LANES), F32), pltpu.VMEM((tq, 1), F32), pltpu.VMEM((tq, 1), F32)],
        input_output_aliases=aliases,
        compiler_params=_cparams(("parallel", "parallel", "parallel", "arbitrary")),
    )(*args)


def _attn_bwd_kv(qa, q0, kva, kv0, dmo, o0, lse, delta, gates, scale, out, out0, name, tq=None):
    bsz, seq, _ = qa.shape
    tq = ATTN_TILE if tq is None else tq
    n_q = seq // tq
    gated = gates is not None
    aliased = not isinstance(out, jax.ShapeDtypeStruct)

    def body(*refs):
        q_ref, k_ref, v_ref, do_ref, lse_ref, dl_ref = refs[:6]
        gate_refs = refs[6:8] if gated else None
        dkv_ref, dfk_ref, dk_s, dv_s, df_s = refs[-5:]
        h, j, i = pl.program_id(1), pl.program_id(2), pl.program_id(3)

        @pl.when(i == 0)
        def _():
            dk_s[...] = jnp.zeros_like(dk_s)
            dv_s[...] = jnp.zeros_like(dv_s)
            df_s[...] = jnp.zeros_like(df_s)

        def step(masked):
            s, q = _scores(q_ref, k_ref, gate_refs, scale, h, masked, tq, tq)
            p = jnp.exp(s - lse_ref[:, 0:1])
            do_b = do_ref[...].astype(BF16)
            dp = lax.dot_general(do_b, v_ref[...].astype(BF16), _DN["nt"], preferred_element_type=F32)
            ds = p * (dp - dl_ref[:, 0:1])
            dv_s[...] += lax.dot_general(p.astype(BF16), do_b, _DN["tn"], preferred_element_type=F32)
            dk_s[...] += lax.dot_general(ds.astype(BF16), q, _DN["tn"], preferred_element_type=F32)
            df_s[...] -= jnp.sum(ds, axis=0, keepdims=True)

        @pl.when(i > j)
        def _():
            step(False)

        @pl.when(i == j)
        def _():
            step(True)

        @pl.when(i == n_q - 1)
        def _():
            dkv_ref[:, 0:LANES] = dk_s[...].astype(dkv_ref.dtype)
            dkv_ref[:, LANES:2 * LANES] = dv_s[...].astype(dkv_ref.dtype)
            dfk_ref[...] = df_s[...]

    blk = (None, tq, LANES)
    col = pl.BlockSpec((None, None, tq, LANES), lambda b, h, j, i: (b, h, jnp.maximum(i, j), 0))
    in_specs = [pl.BlockSpec(blk, lambda b, h, j, i: (b, jnp.maximum(i, j), q0 + h)),
                pl.BlockSpec(blk, lambda b, h, j, i: (b, j, kv0 + 2 * h)),
                pl.BlockSpec(blk, lambda b, h, j, i: (b, j, kv0 + 2 * h + 1)),
                pl.BlockSpec(blk, lambda b, h, j, i: (b, jnp.maximum(i, j), o0 + h)), col, col]
    args = [qa, kva, kva, dmo, lse, delta]
    if gated:
        in_specs += [pl.BlockSpec(blk, lambda b, h, j, i: (b, jnp.maximum(i, j), 0)),
                     pl.BlockSpec((None, 8, tq), lambda b, h, j, i: (b, 0, j))]
        args += list(gates)
    aliases = {}
    if aliased:
        in_specs.append(pl.BlockSpec(memory_space=pl.ANY))
        args.append(out)
        aliases = {len(args) - 1: 0}
    return pl.pallas_call(
        body, name=name, grid=(bsz, N_HEADS, n_q, n_q), in_specs=in_specs,
        out_specs=[pl.BlockSpec((None, tq, 2 * LANES), lambda b, h, j, i: (b, j, out0 + h)),
                   pl.BlockSpec((None, None, 1, tq), lambda b, h, j, i: (b, h, 0, j))],
        out_shape=[jax.ShapeDtypeStruct(out.shape, out.dtype), jax.ShapeDtypeStruct((bsz, N_HEADS, 1, seq), F32)],
        scratch_shapes=[pltpu.VMEM((tq, LANES), F32), pltpu.VMEM((tq, LANES), F32), pltpu.VMEM((1, tq), F32)],
        input_output_aliases=aliases,
        compiler_params=_cparams(("parallel", "parallel", "parallel", "arbitrary")),
    )(*args)


def _block_logits(q, k_ref, gate, j, scale_unused, h, masked, tq):
    del scale_unused
    r = pl.multiple_of(j * tq, tq)
    s = lax.dot_general(q, k_ref[pl.ds(r, tq), :].astype(BF16), _DN["nt"], preferred_element_type=F32)
    if gate is not None:
        fcol, fr_ref = gate
        sub = lax.broadcasted_iota(jnp.int32, (8, tq), 0)
        frow = jnp.sum(jnp.where(sub == h, fr_ref[:, pl.ds(r, tq)], 0.0), axis=0, keepdims=True)
        s = s + (fcol - frow)
    if masked:
        r_i = lax.broadcasted_iota(jnp.int32, (tq, tq), 0)
        c_i = lax.broadcasted_iota(jnp.int32, (tq, tq), 1)
        s = jnp.where(c_i <= r_i, s, NEG)
    return s, r


def _gate_col(fc_ref, h, tq):
    lane = lax.broadcasted_iota(jnp.int32, (tq, LANES), 1)
    return jnp.sum(jnp.where(lane == h, fc_ref[...], 0.0), axis=1, keepdims=True)


def _attn_fwd_loop(qa, q0, kva, kv0, mo, o0, gates, scale, name, tq=None):
    bsz, seq, _ = qa.shape
    tq = ATTN_TILE if tq is None else tq
    n_q = seq // tq
    gated = gates is not None

    def body(*refs):
        q_ref, k_ref, v_ref = refs[:3]
        o_ref, lse_ref = refs[-2:]
        h, i = pl.program_id(1), pl.program_id(2)
        q = (q_ref[...].astype(F32) * scale).astype(BF16)
        gate = (_gate_col(refs[3], h, tq), refs[4]) if gated else None

        def step(j, carry, masked):
            m_prev, l_prev, acc = carry
            s, r = _block_logits(q, k_ref, gate, j, None, h, masked, tq)
            m_new = jnp.maximum(m_prev, jnp.max(s, axis=1, keepdims=True))
            alpha = jnp.exp(m_prev - m_new)
            p = jnp.exp(s - m_new)
            l_new = alpha * l_prev + jnp.sum(p, axis=1, keepdims=True)
            acc = alpha * acc + jnp.dot(p.astype(BF16), v_ref[pl.ds(r, tq), :].astype(BF16),
                                        preferred_element_type=F32)
            return m_new, l_new, acc

        init = (jnp.full((tq, 1), NEG, F32), jnp.zeros((tq, 1), F32), jnp.zeros((tq, LANES), F32))
        carry = lax.fori_loop(0, i, lambda j, c: step(j, c, False), init)
        m_f, l_f, acc = step(i, carry, True)
        o_ref[...] = (acc / l_f).astype(o_ref.dtype)
        lse_ref[...] = _lanes(m_f + jnp.log(l_f))

    blk = (None, tq, LANES)
    full = (None, seq, LANES)
    in_specs = [pl.BlockSpec(blk, lambda b, h, i: (b, i, q0 + h)),
                pl.BlockSpec(full, lambda b, h, i: (b, 0, kv0 + 2 * h)),
                pl.BlockSpec(full, lambda b, h, i: (b, 0, kv0 + 2 * h + 1))]
    args = [qa, kva, kva]
    if gated:
        in_specs += [pl.BlockSpec(blk, lambda b, h, i: (b, i, 0)),
                     pl.BlockSpec((None, 8, seq), lambda b, h, i: (b, 0, 0))]
        args += list(gates)
    in_specs.append(pl.BlockSpec(memory_space=pl.ANY))
    args.append(mo)
    return pl.pallas_call(
        body, name=name, grid=(bsz, N_HEADS, n_q), in_specs=in_specs,
        out_specs=[pl.BlockSpec(blk, lambda b, h, i: (b, i, o0 + h)),
                   pl.BlockSpec((None, None, tq, LANES), lambda b, h, i: (b, h, i, 0))],
        out_shape=[jax.ShapeDtypeStruct(mo.shape, mo.dtype),
                   jax.ShapeDtypeStruct((bsz, N_HEADS, seq, LANES), F32)],
        input_output_aliases={len(args) - 1: 0},
        compiler_params=_cparams(("parallel", "parallel", "parallel")),
    )(*args)


def _attn_bwd_q_loop(qa, q0, kva, kv0, mo, dmo, o0, lse, gates, scale, out, out0, name, tq=None):
    bsz, seq, _ = qa.shape
    tq = ATTN_TILE if tq is None else tq
    n_q = seq // tq
    gated = gates is not None
    aliased = not isinstance(out, jax.ShapeDtypeStruct)

    def body(*refs):
        q_ref, k_ref, v_ref, o_ref, do_ref, lse_ref = refs[:6]
        dq_ref, delta_ref, dfq_ref = refs[-3:]
        h, i = pl.program_id(1), pl.program_id(2)
        q = (q_ref[...].astype(F32) * scale).astype(BF16)
        gate = (_gate_col(refs[6], h, tq), refs[7]) if gated else None
        do_v = do_ref[...]
        do_b = do_v.astype(BF16)
        delta = jnp.sum(do_v * o_ref[...].astype(F32), axis=1, keepdims=True)
        lse_v = lse_ref[:, 0:1]

        def step(j, carry, masked):
            acc, dfq = carry
            s, r = _block_logits(q, k_ref, gate, j, None, h, masked, tq)
            p = jnp.exp(s - lse_v)
            dp = lax.dot_general(do_b, v_ref[pl.ds(r, tq), :].astype(BF16), _DN["nt"], preferred_element_type=F32)
            ds = p * (dp - delta)
            acc = acc + jnp.dot(ds.astype(BF16), k_ref[pl.ds(r, tq), :].astype(BF16), preferred_element_type=F32)
            return acc, dfq + jnp.sum(ds, axis=1, keepdims=True)

        init = (jnp.zeros((tq, LANES), F32), jnp.zeros((tq, 1), F32))
        carry = lax.fori_loop(0, i, lambda j, c: step(j, c, False), init)
        acc, dfq = step(i, carry, True)
        dq_ref[...] = (acc * scale).astype(dq_ref.dtype)
        delta_ref[...] = _lanes(delta)
        dfq_ref[...] = _lanes(dfq)

    blk = (None, tq, LANES)
    full = (None, seq, LANES)
    stat = pl.BlockSpec((None, None, tq, LANES), lambda b, h, i: (b, h, i, 0))
    in_specs = [pl.BlockSpec(blk, lambda b, h, i: (b, i, q0 + h)),
                pl.BlockSpec(full, lambda b, h, i: (b, 0, kv0 + 2 * h)),
                pl.BlockSpec(full, lambda b, h, i: (b, 0, kv0 + 2 * h + 1)),
                pl.BlockSpec(blk, lambda b, h, i: (b, i, o0 + h)),
                pl.BlockSpec(blk, lambda b, h, i: (b, i, o0 + h)), stat]
    args = [qa, kva, kva, mo, dmo, lse]
    if gated:
        in_specs += [pl.BlockSpec(blk, lambda b, h, i: (b, i, 0)),
                     pl.BlockSpec((None, 8, seq), lambda b, h, i: (b, 0, 0))]
        args += list(gates)
    aliases = {}
    if aliased:
        in_specs.append(pl.BlockSpec(memory_space=pl.ANY))
        args.append(out)
        aliases = {len(args) - 1: 0}
    vec = jax.ShapeDtypeStruct((bsz, N_HEADS, seq, LANES), F32)
    return pl.pallas_call(
        body, name=name, grid=(bsz, N_HEADS, n_q), in_specs=in_specs,
        out_specs=[pl.BlockSpec(blk, lambda b, h, i: (b, i, out0 + h)), stat, stat],
        out_shape=[jax.ShapeDtypeStruct(out.shape, out.dtype), vec, vec],
        input_output_aliases=aliases,
        compiler_params=_cparams(("parallel", "parallel", "parallel")),
    )(*args)


def _attn_bwd_kv_loop(qa, q0, kva, kv0, dmo, o0, lse, delta, gates, scale, out, out0, name, tq=None):
    bsz, seq, _ = qa.shape
    tq = ATTN_TILE if tq is None else tq
    n_q = seq // tq
    gated = gates is not None
    aliased = not isinstance(out, jax.ShapeDtypeStruct)

    def body(*refs):
        q_ref, k_ref, v_ref, do_ref, lse_ref, dl_ref = refs[:6]
        dkv_ref, dfk_ref = refs[-2:]
        h, j = pl.program_id(1), pl.program_id(2)
        k_b = k_ref[...].astype(BF16)
        v_b = v_ref[...].astype(BF16)
        if gated:
            fc_ref, fr_ref = refs[6], refs[7]
            sub = lax.broadcasted_iota(jnp.int32, (8, tq), 0)
            frow = jnp.sum(jnp.where(sub == h, fr_ref[...], 0.0), axis=0, keepdims=True)
            lane = lax.broadcasted_iota(jnp.int32, (tq, LANES), 1)

        def step(i, carry, masked):
            dk, dv, dfk = carry
            r = pl.multiple_of(i * tq, tq)
            q = (q_ref[pl.ds(r, tq), :].astype(F32) * scale).astype(BF16)
            s = lax.dot_general(q, k_b, _DN["nt"], preferred_element_type=F32)
            if gated:
                fcol = jnp.sum(jnp.where(lane == h, fc_ref[pl.ds(r, tq), :], 0.0), axis=1, keepdims=True)
                s = s + (fcol - frow)
            if masked:
                r_i = lax.broadcasted_iota(jnp.int32, (tq, tq), 0)
                c_i = lax.broadcasted_iota(jnp.int32, (tq, tq), 1)
                s = jnp.where(c_i <= r_i, s, NEG)
            p = jnp.exp(s - lse_ref[pl.ds(r, tq), 0:1])
            do_b = do_ref[pl.ds(r, tq), :].astype(BF16)
            dp = lax.dot_general(do_b, v_b, _DN["nt"], preferred_element_type=F32)
            ds = p * (dp - dl_ref[pl.ds(r, tq), 0:1])
            dv = dv + lax.dot_general(p.astype(BF16), do_b, _DN["tn"], preferred_element_type=F32)
            dk = dk + lax.dot_general(ds.astype(BF16), q, _DN["tn"], preferred_element_type=F32)
            return dk, dv, dfk - jnp.sum(ds, axis=0, keepdims=True)

        init = (jnp.zeros((tq, LANES), F32), jnp.zeros((tq, LANES), F32), jnp.zeros((1, tq), F32))
        carry = step(j, init, True)
        dk, dv, dfk = lax.fori_loop(j + 1, n_q, lambda i, c: step(i, c, False), carry)
        dkv_ref[:, 0:LANES] = dk.astype(dkv_ref.dtype)
        dkv_ref[:, LANES:2 * LANES] = dv.astype(dkv_ref.dtype)
        dfk_ref[...] = dfk

    blk = (None, tq, LANES)
    full = (None, seq, LANES)
    stat = pl.BlockSpec((None, None, seq, LANES), lambda b, h, j: (b, h, 0, 0))
    in_specs = [pl.BlockSpec(full, lambda b, h, j: (b, 0, q0 + h)),
                pl.BlockSpec(blk, lambda b, h, j: (b, j, kv0 + 2 * h)),
                pl.BlockSpec(blk, lambda b, h, j: (b, j, kv0 + 2 * h + 1)),
                pl.BlockSpec(full, lambda b, h, j: (b, 0, o0 + h)), stat, stat]
    args = [qa, kva, kva, dmo, lse, delta]
    if gated:
        in_specs += [pl.BlockSpec(full, lambda b, h, j: (b, 0, 0)),
                     pl.BlockSpec((None, 8, tq), lambda b, h, j: (b, 0, j))]
        args += list(gates)
    aliases = {}
    if aliased:
        in_specs.append(pl.BlockSpec(memory_space=pl.ANY))
        args.append(out)
        aliases = {len(args) - 1: 0}
    return pl.pallas_call(
        body, name=name, grid=(bsz, N_HEADS, n_q), in_specs=in_specs,
        out_specs=[pl.BlockSpec((None, tq, 2 * LANES), lambda b, h, j: (b, j, out0 + h)),
                   pl.BlockSpec((None, None, 1, tq), lambda b, h, j: (b, h, 0, j))],
        out_shape=[jax.ShapeDtypeStruct(out.shape, out.dtype), jax.ShapeDtypeStruct((bsz, N_HEADS, 1, seq), F32)],
        input_output_aliases=aliases,
        compiler_params=_cparams(("parallel", "parallel", "parallel")),
    )(*args)


def _gmlp_fn(uv, lng, lnb, ws, bst):
    u = jax.nn.gelu(uv[:, 0:GROUP_WIDTH])
    gv = jax.nn.gelu(uv[:, GROUP_WIDTH:2 * GROUP_WIDTH])
    mu = jnp.mean(gv, axis=-1, keepdims=True)
    vc = gv - mu
    var = jnp.mean(vc * vc, axis=-1, keepdims=True)
    vln = vc * lax.rsqrt(var + LN_EPS) * lng + lnb
    r_i = lax.broadcasted_iota(jnp.int32, (D_CHUNK, D_CHUNK), 0)
    c_i = lax.broadcasted_iota(jnp.int32, (D_CHUNK, D_CHUNK), 1)
    lane_g = lax.broadcasted_iota(jnp.int32, (D_CHUNK, GROUP_WIDTH), 1) // HEAD_DIM
    e_r = lax.broadcasted_iota(jnp.int32, (LANES, GROUP_WIDTH), 0)
    e_c = lax.broadcasted_iota(jnp.int32, (LANES, GROUP_WIDTH), 1)
    expand = (e_r == e_c // HEAD_DIM).astype(F32)
    mixed = jnp.dot(bst, expand, precision=HI, preferred_element_type=F32)
    for g in range(4):
        w = jnp.where(r_i >= c_i, ws[g], 0.0)
        mixed = mixed + jnp.where(lane_g == g, _bdot(w, vln, "nn"), 0.0)
    return u * mixed


def _gmlp_fwd(proj, mo, lng, lnb, ws, bst, name):
    bsz, seq, _ = proj.shape

    def body(p_ref, mo_any, lng_ref, lnb_ref, ws_ref, bst_ref, o_ref):
        del mo_any
        o_ref[...] = _gmlp_fn(p_ref[...], lng_ref[...], lnb_ref[...], ws_ref[...], bst_ref[...]).astype(o_ref.dtype)

    return pl.pallas_call(
        body, name=name, grid=(bsz, seq // D_CHUNK),
        in_specs=[pl.BlockSpec((None, D_CHUNK, 512), lambda b, s: (b, s, P_D // 512)),
                  pl.BlockSpec(memory_space=pl.ANY), _vec_spec(256), _vec_spec(256),
                  pl.BlockSpec((4, D_CHUNK, D_CHUNK), lambda b, s: (0, 0, 0)),
                  pl.BlockSpec((D_CHUNK, LANES), lambda b, s: (0, 0))],
        out_specs=pl.BlockSpec((None, D_CHUNK, GROUP_WIDTH), lambda b, s: (b, s, 1280 // GROUP_WIDTH)),
        out_shape=jax.ShapeDtypeStruct(mo.shape, mo.dtype),
        input_output_aliases={1: 0},
        compiler_params=_cparams(("parallel", "parallel")),
    )(proj, mo, lng, lnb, ws, bst)


def _gmlp_bwd(dmo, dproj, proj, lng, lnb, ws, bst, name):
    bsz, seq, _ = proj.shape

    def body(do_ref, dp_any, p_ref, lng_ref, lnb_ref, ws_ref, bst_ref, dp_ref, dlg_ref, dlb_ref, dws_ref, dbst_ref):
        del dp_any
        first = jnp.logical_and(pl.program_id(0) == 0, pl.program_id(1) == 0)

        @pl.when(first)
        def _():
            dlg_ref[...] = jnp.zeros_like(dlg_ref)
            dlb_ref[...] = jnp.zeros_like(dlb_ref)
            dws_ref[...] = jnp.zeros_like(dws_ref)
            dbst_ref[...] = jnp.zeros_like(dbst_ref)

        _, vjp = jax.vjp(_gmlp_fn, p_ref[...], lng_ref[...], lnb_ref[...], ws_ref[...], bst_ref[...])
        duv, dlg, dlb, dws, dbst = vjp(do_ref[...])
        dp_ref[...] = duv.astype(dp_ref.dtype)
        dlg_ref[...] += dlg
        dlb_ref[...] += dlb
        dws_ref[...] += dws
        dbst_ref[...] += dbst

    const2 = lambda shape: pl.BlockSpec(shape, lambda b, s: (0,) * len(shape))
    return pl.pallas_call(
        body, name=name, grid=(bsz, seq // D_CHUNK),
        in_specs=[pl.BlockSpec((None, D_CHUNK, GROUP_WIDTH), lambda b, s: (b, s, 1280 // GROUP_WIDTH)),
                  pl.BlockSpec(memory_space=pl.ANY),
                  pl.BlockSpec((None, D_CHUNK, 512), lambda b, s: (b, s, P_D // 512)),
                  _vec_spec(256), _vec_spec(256), const2((4, D_CHUNK, D_CHUNK)), const2((D_CHUNK, LANES))],
        out_specs=[pl.BlockSpec((None, D_CHUNK, 512), lambda b, s: (b, s, P_D // 512)),
                   _vec_spec(256), _vec_spec(256), const2((4, D_CHUNK, D_CHUNK)), const2((D_CHUNK, LANES))],
        out_shape=[jax.ShapeDtypeStruct(dproj.shape, dproj.dtype), jax.ShapeDtypeStruct((1, 256), F32),
                   jax.ShapeDtypeStruct((1, 256), F32), jax.ShapeDtypeStruct((4, D_CHUNK, D_CHUNK), F32),
                   jax.ShapeDtypeStruct((D_CHUNK, LANES), F32)],
        input_output_aliases={1: 0},
        compiler_params=_cparams(("arbitrary", "arbitrary")),
    )(dmo, dproj, proj, lng, lnb, ws, bst)


def _ada_fwd(c_all, ada_w, name):
    n_b = c_all.shape[0]
    depth, d, cols = ada_w.shape

    def body(c_ref, w_ref, o_ref):
        cv = c_ref[...]
        act = (cv * jax.nn.sigmoid(cv)).astype(BF16)
        o_ref[...] = jnp.dot(act, w_ref[...].astype(BF16), preferred_element_type=F32)

    return pl.pallas_call(
        body, name=name, grid=(depth,),
        in_specs=[pl.BlockSpec((n_b, d), lambda l: (0, 0)), pl.BlockSpec((None, d, cols), lambda l: (l, 0, 0))],
        out_specs=pl.BlockSpec((None, n_b, cols), lambda l: (l, 0, 0)),
        out_shape=jax.ShapeDtypeStruct((depth, n_b, cols), F32),
        compiler_params=_cparams(("parallel",)),
    )(c_all, ada_w)


def _ada_bwd(c_all, dmod_cols, dmod_full, name):
    n_b, d = c_all.shape
    depth, _, cols = dmod_cols.shape
    full = dmod_full.shape[-1]

    def body(c_ref, dm_ref, df_ref, gw_ref, gb_ref):
        cv = c_ref[...]
        act = (cv * jax.nn.sigmoid(cv)).astype(BF16)
        gw_ref[...] = lax.dot_general(act, dm_ref[...].astype(BF16), (((0,), (0,)), ((), ())),
                                      preferred_element_type=F32)
        gb_ref[...] = jnp.sum(df_ref[...], axis=0, keepdims=True)

    return pl.pallas_call(
        body, name=name, grid=(depth,),
        in_specs=[pl.BlockSpec((n_b, d), lambda l: (0, 0)), pl.BlockSpec((None, n_b, cols), lambda l: (l, 0, 0)),
                  pl.BlockSpec((None, n_b, full), lambda l: (l, 0, 0))],
        out_specs=[pl.BlockSpec((None, d, cols), lambda l: (l, 0, 0)),
                   pl.BlockSpec((None, 1, full), lambda l: (l, 0, 0))],
        out_shape=[jax.ShapeDtypeStruct((depth, d, cols), F32), jax.ShapeDtypeStruct((depth, 1, full), F32)],
        compiler_params=_cparams(("parallel",)),
    )(c_all, dmod_cols, dmod_full)


def _adamw(gparts, own, w, m, v, name, layer=0, prev=None):
    n_p, rows, cols = gparts.shape
    assert w.shape[1:] == (rows, cols)
    tr = rows
    if rows > 512:
        tr = next(c for c in range(512, 7, -8) if rows % c == 0)
    has_own = own is not None
    n_prev = 0 if prev is None else 4

    def body(*refs):
        if has_own:
            slot_ref, refs = refs[0], refs[1:]
        g_ref = refs[0]
        own_ref = refs[1] if has_own else None
        w_ref, m_ref, v_ref = refs[1 + has_own:4 + has_own]
        go_ref, do_ref, mo_ref, vo_ref = refs[4 + has_own + n_prev:]
        g = None
        for p in range(n_p):
            term = g_ref[p].astype(F32)
            if has_own:
                term = jnp.where(slot_ref[0] == p, own_ref[...].astype(F32), term)
            g = term if g is None else g + term
        m_new = ADAM_B1 * m_ref[...] + (1.0 - ADAM_B1) * g
        v_new = ADAM_B2 * v_ref[...] + (1.0 - ADAM_B2) * (g * g)
        m_hat = m_new / (1.0 - ADAM_B1 ** ADAM_STEP)
        v_hat = v_new / (1.0 - ADAM_B2 ** ADAM_STEP)
        go_ref[...] = g
        do_ref[...] = -ADAM_LR * (m_hat / (jnp.sqrt(v_hat) + ADAM_EPS) + ADAM_WD * w_ref[...])
        mo_ref[...] = m_new
        vo_ref[...] = v_new

    spec = pl.BlockSpec((None, tr, cols), lambda i, *_: (layer, i, 0))
    in_specs = [pl.BlockSpec((n_p, tr, cols), lambda i, *_: (0, i, 0))]
    args = [gparts]
    if has_own:
        in_specs.append(pl.BlockSpec((None, tr, cols), lambda i, slot: (slot[0], i, 0)))
        args.append(own[0])
    in_specs += [spec, spec, spec]
    args += [w, m, v]
    aliases = {}
    if prev is not None:
        aliases = {has_own + len(args) + k: k for k in range(4)}
        in_specs += [pl.BlockSpec(memory_space=pl.ANY)] * 4
        args += list(prev)
    shp = jax.ShapeDtypeStruct(w.shape, F32)
    out_specs, out_shape = [spec, spec, spec, spec], [shp, shp, shp, shp]
    if not has_own:
        return pl.pallas_call(
            body, name=name, grid=(rows // tr,), in_specs=in_specs, out_specs=out_specs, out_shape=out_shape,
            input_output_aliases=aliases, compiler_params=_cparams(("parallel",)),
        )(*args)
    return pl.pallas_call(
        body, name=name, out_shape=out_shape, input_output_aliases=aliases,
        grid_spec=pltpu.PrefetchScalarGridSpec(num_scalar_prefetch=1, grid=(rows // tr,), in_specs=in_specs,
                                               out_specs=out_specs),
        compiler_params=_cparams(("parallel",)),
    )(jnp.reshape(own[1], (1,)).astype(jnp.int32), *args)


def _sum_parts(parts, name):
    n_p, rows, cols = parts.shape
    tr = 256 if rows % 256 == 0 else rows

    def body(p_ref, o_ref):
        acc = p_ref[0]
        for p in range(1, n_p):
            acc = acc + p_ref[p]
        o_ref[...] = acc

    return pl.pallas_call(
        body, name=name, grid=(rows // tr,),
        in_specs=[pl.BlockSpec((n_p, tr, cols), lambda i: (0, i, 0))],
        out_specs=pl.BlockSpec((tr, cols), lambda i: (i, 0)),
        out_shape=jax.ShapeDtypeStruct((rows, cols), F32),
        compiler_params=_cparams(("parallel",)),
    )(parts)


def _all_gather(arrs, name):
    n = len(arrs)

    def body(*refs):
        in_refs, out_refs = refs[:n], refs[n:2 * n]
        send_sems, recv_sems, loc_sems = refs[2 * n:]
        x, y, c = lax.axis_index("x"), lax.axis_index("y"), lax.axis_index("c")
        me, sibling = (x, y, c), (x, y, 1 - c)
        chips = [(1 - x, y), (x, 1 - y), (1 - x, 1 - y)]

        def copy(a, k, block, to, src=None):
            slot = out_refs[a].at[4 * block[0] + 2 * block[1] + block[2]]
            return pltpu.make_async_remote_copy(
                src_ref=slot if src is None else src, dst_ref=slot, send_sem=send_sems.at[a, k],
                recv_sem=recv_sems.at[a, k], device_id=to, device_id_type=pl.DeviceIdType.MESH)

        mine = [pltpu.make_async_copy(in_refs[a], out_refs[a].at[4 * x + 2 * y + c], loc_sems.at[a])
                for a in range(n)]
        for cp in mine:
            cp.start()
        first = []
        for a in range(n):
            first.append(copy(a, 0, me, sibling, src=in_refs[a]))
            first += [copy(a, 1 + j, me, (*chip, c), src=in_refs[a]) for j, chip in enumerate(chips)]
        for cp in first:
            cp.start()
        passed = []
        for j, chip in enumerate(chips):
            for a in range(n):
                copy(a, 1 + j, (*chip, c), me).wait_recv()
                cp = copy(a, 4 + j, (*chip, c), sibling)
                cp.start()
                passed.append(cp)
        for a in range(n):
            copy(a, 0, sibling, me).wait_recv()
        for j, chip in enumerate(chips):
            for a in range(n):
                copy(a, 4 + j, (*chip, 1 - c), me).wait_recv()
        for cp in first + passed:
            cp.wait_send()
        for cp in mine:
            cp.wait()

    any_spec = pl.BlockSpec(memory_space=pl.ANY)
    return pl.pallas_call(
        body, name=name, in_specs=[any_spec] * n, out_specs=[any_spec] * n,
        out_shape=[jax.ShapeDtypeStruct((N_DEV,) + a.shape, a.dtype) for a in arrs],
        scratch_shapes=[pltpu.SemaphoreType.DMA((n, N_DEV - 1)), pltpu.SemaphoreType.DMA((n, N_DEV - 1)),
                        pltpu.SemaphoreType.DMA((n,))],
    )(*arrs)


def _flip_peers():
    x, y, c = lax.axis_index("x"), lax.axis_index("y"), lax.axis_index("c")
    peers = []
    for fx, fy, fc in [(fx, fy, fc) for fx in (0, 1) for fy in (0, 1) for fc in (0, 1)][1:]:
        px, py, pc = (1 - x if fx else x), (1 - y if fy else y), (1 - c if fc else c)
        peers.append(((px, py, pc), 4 * px + 2 * py + pc))
    return 4 * x + 2 * y + c, peers


def _push_start(srcs, name, whole=False):
    n, n_peer = len(srcs), N_DEV - 1
    if whole:
        me_w = 4 * lax.axis_index("x") + 2 * lax.axis_index("y") + lax.axis_index("c")
        lands = [lax.dynamic_update_slice_in_dim(lax.empty((N_DEV,) + a.shape, a.dtype), a[None], me_w, axis=0)
                 for a in srcs]
    else:
        lands = [lax.empty(a.shape, a.dtype) for a in srcs]

    def body(*refs):
        src_refs, land_refs = refs[:n], refs[n:2 * n]
        send_sems, recv_sems = refs[2 * n], refs[2 * n + 1]
        token = refs[-1]
        me, peers = _flip_peers()
        for k, (dev, idx) in enumerate(peers):
            for a in range(n):
                pltpu.make_async_remote_copy(
                    src_ref=src_refs[a] if whole else src_refs[a].at[idx], dst_ref=land_refs[a].at[me],
                    send_sem=send_sems.at[a * n_peer + k], recv_sem=recv_sems.at[a * n_peer + k], device_id=dev,
                    device_id_type=pl.DeviceIdType.MESH).start()
        token[...] = jnp.zeros_like(token)

    hbm = pl.BlockSpec(memory_space=pltpu.HBM)
    sem = pl.BlockSpec(memory_space=pltpu.SEMAPHORE)
    arrs = list(srcs) + lands
    res = pl.pallas_call(
        body, name=name, in_specs=[hbm] * (2 * n),
        out_specs=(sem, sem, *[hbm] * (2 * n), pl.BlockSpec(memory_space=pltpu.VMEM)),
        out_shape=(pltpu.SemaphoreType.DMA((n * n_peer,)), pltpu.SemaphoreType.DMA((n * n_peer,)),
                   *[pltpu.HBM(a.shape, a.dtype) for a in arrs], jax.ShapeDtypeStruct((8, LANES), F32)),
        input_output_aliases={i: 2 + i for i in range(2 * n)},
        compiler_params=pltpu.CompilerParams(has_side_effects=pltpu.SideEffectType.DATAFLOW_SIDE_EFFECTING),
    )(*[pltpu.with_memory_space_constraint(a, pltpu.HBM) for a in arrs])
    return res[0], res[1], list(res[2:2 + n]), list(res[2 + n:2 + 2 * n]), res[-1]


def _push_wait(send_sems, recv_sems, srcs, lands, after, name, whole=False):
    n, n_peer = len(srcs), N_DEV - 1

    def body(*refs):
        src_refs, land_refs = refs[:n], refs[n:2 * n]
        send_s, recv_s = refs[2 * n], refs[2 * n + 1]
        _, peers = _flip_peers()
        for k, (dev, idx) in enumerate(peers):
            for a in range(n):
                cp = pltpu.make_async_remote_copy(
                    src_ref=src_refs[a] if whole else src_refs[a].at[idx], dst_ref=land_refs[a].at[idx],
                    send_sem=send_s.at[a * n_peer + k],
                    recv_sem=recv_s.at[a * n_peer + k], device_id=dev, device_id_type=pl.DeviceIdType.MESH)
                cp.wait_send()
                cp.wait_recv()

    hbm = pl.BlockSpec(memory_space=pltpu.HBM)
    sem = pl.BlockSpec(memory_space=pltpu.SEMAPHORE)
    arrs = list(srcs) + list(lands)
    res = pl.pallas_call(
        body, name=name, in_specs=[hbm] * (2 * n) + [sem, sem, pl.BlockSpec(memory_space=pl.ANY)],
        out_specs=tuple([hbm] * (2 * n)), out_shape=tuple(pltpu.HBM(a.shape, a.dtype) for a in arrs),
        input_output_aliases={i: i for i in range(2 * n)},
        compiler_params=pltpu.CompilerParams(has_side_effects=pltpu.SideEffectType.DATAFLOW_SIDE_EFFECTING),
    )(*arrs, send_sems, recv_sems, after)
    return list(res[:n]), list(res[n:])


def _ffn_fwd(x, h, mod, w_in, w_out, lng, lnb, rows, tag, nxt):
    bsz, seq, d = x.shape
    t = bsz * seq
    if h is None:
        h = _modulate(x, mod, rows[0], rows[1], f"modulate_{tag}")
    z, a = _ffn_in_swiglu(h.reshape(t, d), w_in, f"ffn_in_{tag}")
    f = _matmul_groupsum(a, w_out, out_dtype=F32, tm=512, name=f"ffn_out_{tag}").reshape(bsz, seq, d)
    y, h_next = _res_ln(x, f, mod, lng, lnb, rows[2], 0.5, f"res_ln_{tag}", nxt)
    return y, h_next, (x, h, z, a, f)


def _tied(mod, tie):
    return mod if tie is None else mod + tie


def _ffn_bwd(dy, saved, mod, w_in, w_out, lng, lnb, rows, tag, ready):
    x, h, z, a, f = saved
    bsz, seq, d = x.shape
    t = bsz * seq
    dx_res, df, dgate, dlg, dlb = _res_ln_bwd(dy, x, f, mod, lng, lnb, rows[2], 0.5, f"res_ln_bwd_{tag}")
    df2 = df.reshape(1, t, d)
    dw_out = _matmul(a, df2, mode="tn", group_out=True, out_dtype=BF16, tm=a.shape[2], tk=min(t, 2048),
                     name=f"ffn_out_dw_{tag}")
    tie_out = ready(f"{tag}_out", dw_out)
    dz = _ffn_out_dx_swiglu(df.reshape(t, d), w_out, z, f"ffn_out_dx_{tag}").reshape(N_DEV, t, -1)
    dw_in = _matmul(dz, h.reshape(1, t, d), mode="tn", group_out=True, out_dtype=BF16, tm=dz.shape[2],
                    tk=min(t, 2048), name=f"ffn_in_dw_{tag}")
    tie_in = ready(f"{tag}_in", dw_in)
    dh = _matmul_groupsum(dz, w_in, out_dtype=F32, tm=512, name=f"ffn_in_dx_{tag}").reshape(bsz, seq, d)
    dx, dsh, dsc = _modulate_bwd(dh, x, _tied(_tied(mod, tie_out), tie_in), dx_res, rows[1],
                                 f"modulate_bwd_{tag}")
    return dx, (dsh, dsc, dgate), dw_in, dw_out, dlg, dlb


def _mixer_fwd(x, h, mod, wts, small, lng, lnb, layer, tabs):
    bsz, seq, d = x.shape
    t = bsz * seq
    proj = _matmul(h.reshape(1, t, d), wts["mix_in"][None], mode="nn", group_out=True, out_dtype=F32, tm=512, tk=d,
                   name="mix_in").reshape(bsz, seq, PACK_W)
    mo, states = _hgrn_fwd(proj, small["lb_logits8"], small["hgrn_norm_g"], layer, f"hgrn_fwd_l{layer}")
    q, kv = _mla_pre(proj, small["q_norm_g"], small["kv_norm_g"], wts["uq"], wts["ukv"], tabs, "mla_pre")
    mla_scale = float((B_NOPE + B_ROPE) ** -0.5)
    mo, lse_b = _attn_fwd_loop(q, 0, kv, 0, mo, 2, None, mla_scale, "mla_attn_fwd")
    fg = _fox_gate(proj, small["fox_b_f"], "fox_gate")
    gates = (fg, jnp.swapaxes(fg[:, :, 0:8], 1, 2))
    fox_scale = float(HEAD_DIM ** -0.5)
    mo, lse_c = _attn_fwd_loop(proj, P_CQ // LANES, proj, P_CKV // LANES, mo, 6, gates, fox_scale, "fox_attn_fwd")
    mo = _gmlp_fwd(proj, mo, small["gmlp_ln_g"], small["gmlp_ln_b"], small["gmlp_w_s"], small["gmlp_bst"],
                   "gmlp_fwd")
    mixed = _matmul(mo.reshape(1, t, MO_W), wts["mix_out"][None], mode="nn", group_out=True, out_dtype=F32,
                    tm=1024, tk=MO_W, name="mix_out").reshape(bsz, seq, d)
    y, h_next = _res_ln(x, mixed, mod, lng, lnb, 5, 1.0, "res_ln_mix", (mod, 6, 7))
    return y, h_next, (x, h, proj, mo, states, q, kv, lse_b, gates, lse_c, mixed)


def _mixer_bwd(dy, saved, mod, wts, small, lng, lnb, layer, tabs, ready):
    x, h, proj, mo, states, q, kv, lse_b, gates, lse_c, mixed = saved
    bsz, seq, d = x.shape
    t = bsz * seq
    dx_res, dmixed, dgate, dlg, dlb = _res_ln_bwd(dy, x, mixed, mod, lng, lnb, 5, 1.0, "res_ln_bwd_mix")
    dm2 = dmixed.reshape(1, t, d)
    dmo = _matmul(dm2, wts["mix_out"][None], mode="nt", group_out=True, out_dtype=F32, tm=1024, tk=d,
                  name="mix_out_dx").reshape(bsz, seq, MO_W)
    dw_out = _matmul(mo.reshape(1, t, MO_W), dm2, mode="tn", group_out=True, out_dtype=F32, tm=512, tk=min(t, 2048),
                     name="mix_out_dw")[0]
    tie_out = ready("mix_out", dw_out)
    g = {}
    dproj, g["lb_logits8"], g["hgrn_norm_g"] = _hgrn_bwd(dmo, proj, states, small["lb_logits8"],
                                                         small["hgrn_norm_g"], layer, f"hgrn_bwd_l{layer}")
    mla_scale = float((B_NOPE + B_ROPE) ** -0.5)
    dq, delta_b, _ = _attn_bwd_q_loop(q, 0, kv, 0, mo, dmo, 2, lse_b, None, mla_scale,
                                 jax.ShapeDtypeStruct((bsz, seq, 512), F32), 0, "mla_attn_bwd_q")
    dkv, _ = _attn_bwd_kv_loop(q, 0, kv, 0, dmo, 2, lse_b, delta_b, None, mla_scale,
                          jax.ShapeDtypeStruct((bsz, seq, 1024), F32), 0, "mla_attn_bwd_kv")
    dproj, g["q_norm_g"], g["kv_norm_g"], g["uq"], g["ukv"] = _mla_pre_bwd(
        dq, dkv, dproj, proj, small["q_norm_g"], small["kv_norm_g"], wts["uq"], wts["ukv"], tabs, "mla_pre_bwd")
    fox_scale = float(HEAD_DIM ** -0.5)
    dproj, delta_c, dfq = _attn_bwd_q_loop(proj, P_CQ // LANES, proj, P_CKV // LANES, mo, dmo, 6, lse_c, gates,
                                      fox_scale, dproj, P_CQ // LANES, "fox_attn_bwd_q")
    dproj, dfk = _attn_bwd_kv_loop(proj, P_CQ // LANES, proj, P_CKV // LANES, dmo, 6, lse_c, delta_c, gates, fox_scale,
                              dproj, P_CKV // (2 * LANES), "fox_attn_bwd_kv")
    dfk_cols = jnp.pad(jnp.swapaxes(dfk[:, :, 0, :], 1, 2), ((0, 0), (0, 0), (0, LANES - N_HEADS)))
    dproj, g["fox_b_f"] = _fox_gate_bwd(dfq, dfk_cols, dproj, proj, small["fox_b_f"], "fox_gate_bwd")
    dproj, g["gmlp_ln_g"], g["gmlp_ln_b"], g["gmlp_w_s"], g["gmlp_bst"] = _gmlp_bwd(
        dmo, dproj, proj, small["gmlp_ln_g"], small["gmlp_ln_b"], small["gmlp_w_s"], small["gmlp_bst"], "gmlp_bwd")
    dp2 = dproj.reshape(1, t, PACK_W)
    dw_in = _matmul(h.reshape(1, t, d), dp2, mode="tn", group_out=True, out_dtype=BF16, tm=512, tk=1024,
                    name="mix_in_dw")[0]
    tie_in = ready("mix_in", dw_in)
    dh = _matmul(dp2, wts["mix_in"][None], mode="nt", group_out=True, out_dtype=F32, tm=512, tk=PACK_W,
                 name="mix_in_dx").reshape(bsz, seq, d)
    dx, dsh, dsc = _modulate_bwd(dh, x, _tied(_tied(mod, tie_out), tie_in), dx_res, 4, "modulate_bwd_mix")
    return dx, (dsh, dsc, dgate), dw_in, dw_out, g, dlg, dlb


def _small_views(p, layer):
    return {
        "lb_logits8": jnp.pad(p["hgrn_lb_logits"], ((0, 8 - DEPTH), (0, 0))),
        "hgrn_norm_g": p["hgrn_norm_g"][layer][None],
        "q_norm_g": p["mla_q_norm_g"][layer][None],
        "kv_norm_g": p["mla_kv_norm_g"][layer][None],
        "fox_b_f": jnp.pad(p["fox_b_f"][layer][None], ((0, 0), (0, LANES - N_HEADS))),
        "gmlp_ln_g": p["gmlp_ln_g"][layer][None],
        "gmlp_ln_b": p["gmlp_ln_b"][layer][None],
        "gmlp_w_s": p["gmlp_w_s"][layer],
        "gmlp_bst": jnp.pad(p["gmlp_b_s"][layer].T, ((0, 0), (0, LANES - N_HEADS))),
    }


def _local_step(x, mod, target, weights, p, grads_ready=None):
    bsz, seq, d = x.shape
    tabs = _rope_tables(seq)
    saved = []
    h = None
    for l in range(DEPTH):
        sm = _small_views(p, l)
        lng, lnb = p["ln_g"][l], p["ln_b"][l]
        w = weights(l, "ffn1", x)
        x, h, s1 = _ffn_fwd(x, h, mod[l], w["ffn1_in"], w["ffn1_out"], lng[0:1], lnb[0:1], (0, 1, 2), "ffn1",
                            (mod[l], 3, 4))
        x, h, s2 = _mixer_fwd(x, h, mod[l], weights(l, "mix", x), sm, lng[1:2], lnb[1:2], l, tabs)
        w = weights(l, "ffn2", x)
        x, h, s3 = _ffn_fwd(x, h, mod[l], w["ffn2_in"], w["ffn2_out"], lng[2:3], lnb[2:3], (6, 7, 8), "ffn2",
                            (mod[l + 1], 0, 1) if l + 1 < DEPTH else None)
        saved.append((s1, s2, s3))
    dx, loss = _loss_head(x, target, "loss_head")
    big, small, dmods = [None] * DEPTH, [None] * DEPTH, [None] * DEPTH
    ties = []

    def tied(a):
        for t in ties:
            a = a + t
        return a

    for l in reversed(range(DEPTH)):
        w = {**weights(l, "ffn1", None), **weights(l, "mix", None), **weights(l, "ffn2", None)}
        sm = _small_views(p, l)
        lng, lnb = p["ln_g"][l], p["ln_b"][l]
        s1, s2, s3 = saved[l]

        def ready(name, grad, l=l):
            tie = None if grads_ready is None else grads_ready(l, name, grad)
            if tie is not None:
                ties.append(tie)
            return tie

        dx, dm3, dwi2, dwo2, dlg2, dlb2 = _ffn_bwd(dx, s3, tied(mod[l]), w["ffn2_in"], w["ffn2_out"], lng[2:3],
                                                   lnb[2:3], (6, 7, 8), "ffn2", ready)
        dx, dm2, dwmi, dwmo, g, dlg1, dlb1 = _mixer_bwd(dx, s2, tied(mod[l]), w, sm, lng[1:2], lnb[1:2], l, tabs,
                                                        ready)
        dx, dm1, dwi1, dwo1, dlg0, dlb0 = _ffn_bwd(dx, s1, tied(mod[l]), w["ffn1_in"], w["ffn1_out"], lng[0:1],
                                                   lnb[0:1], (0, 1, 2), "ffn1", ready)
        dmods[l] = jnp.concatenate(list(dm1) + list(dm2) + list(dm3), axis=1)
        big[l] = {"ffn1_in": dwi1, "ffn1_out": dwo1, "ffn2_in": dwi2, "ffn2_out": dwo2, "mix_in": dwmi,
                  "mix_out": dwmo}
        g["ln_g"] = jnp.concatenate([dlg0, dlg1, dlg2], axis=0)
        g["ln_b"] = jnp.concatenate([dlb0, dlb1, dlb2], axis=0)
        small[l] = g
    return loss, dx, jnp.stack(dmods), big, small


_BIG = ("ffn1_in", "ffn1_out", "ffn2_in", "ffn2_out", "mix_in", "mix_out")


def _small_grad_list(small, loss):
    def both(fn):
        return jnp.stack([fn(small[l]) for l in range(DEPTH)])

    uq_src, ukv_src = _uq_src(), _ukv_src()
    return [
        ("loss", loss.reshape(1)),
        ("ln_g", both(lambda g: g["ln_g"])), ("ln_b", both(lambda g: g["ln_b"])),
        ("hgrn_lb_logits", small[0]["lb_logits8"][:DEPTH] + small[1]["lb_logits8"][:DEPTH]),
        ("hgrn_norm_g", both(lambda g: g["hgrn_norm_g"][0])),
        ("mla_q_norm_g", both(lambda g: g["q_norm_g"][0])),
        ("mla_kv_norm_g", both(lambda g: g["kv_norm_g"][0])),
        ("mla_w_uq", both(lambda g: _unpack_cols(g["uq"], uq_src, 384))),
        ("mla_w_ukv", both(lambda g: _unpack_cols(g["ukv"], ukv_src, 512))),
        ("fox_b_f", both(lambda g: g["fox_b_f"][0, :N_HEADS])),
        ("gmlp_ln_g", both(lambda g: g["gmlp_ln_g"][0])), ("gmlp_ln_b", both(lambda g: g["gmlp_ln_b"][0])),
        ("gmlp_w_s", both(lambda g: g["gmlp_w_s"])),
        ("gmlp_b_s", both(lambda g: g["gmlp_bst"][:, :N_HEADS].T)),
    ]


_PACK_COLS = 512


def _pack_small(items):
    flat = jnp.concatenate([a.reshape(-1).astype(F32) for _, a in items])
    n = flat.shape[0]
    tile = 8 * _PACK_COLS
    flat = jnp.pad(flat, (0, (-n) % tile))
    return flat.reshape(-1, _PACK_COLS)


def _unpack_small(buf, items):
    flat = buf.reshape(-1)
    out, off = {}, 0
    for name, a in items:
        out[name] = flat[off:off + a.size].reshape(a.shape)
        off += a.size
    return out


def _as2d(a):
    return a.reshape(-1, a.shape[-1])


def kernel(x, c, ada_w, ada_b, ln_g, ln_b, ffn1_w_in, ffn1_w_out, ffn2_w_in, ffn2_w_out, mix_w_in, mix_w_out, hgrn_lb_logits, hgrn_norm_g, mla_q_norm_g, mla_kv_norm_g, mla_w_uq, mla_w_ukv, fox_b_f, gmlp_ln_g, gmlp_ln_b, gmlp_w_s, gmlp_b_s, loss_target, m_ada_w, m_ada_b, m_ln_g, m_ln_b, m_ffn1_w_in, m_ffn1_w_out, m_ffn2_w_in, m_ffn2_w_out, m_mix_w_in, m_mix_w_out, m_hgrn_lb_logits, m_hgrn_norm_g, m_mla_q_norm_g, m_mla_kv_norm_g, m_mla_w_uq, m_mla_w_ukv, m_fox_b_f, m_gmlp_ln_g, m_gmlp_ln_b, m_gmlp_w_s, m_gmlp_b_s, v_ada_w, v_ada_b, v_ln_g, v_ln_b, v_ffn1_w_in, v_ffn1_w_out, v_ffn2_w_in, v_ffn2_w_out, v_mix_w_in, v_mix_w_out, v_hgrn_lb_logits, v_hgrn_norm_g, v_mla_q_norm_g, v_mla_kv_norm_g, v_mla_w_uq, v_mla_w_ukv, v_fox_b_f, v_gmlp_ln_g, v_gmlp_ln_b, v_gmlp_w_s, v_gmlp_b_s):
    names = ["ada_w", "ada_b", "ln_g", "ln_b", "ffn1_w_in", "ffn1_w_out", "ffn2_w_in", "ffn2_w_out", "mix_w_in",
             "mix_w_out", "hgrn_lb_logits", "hgrn_norm_g", "mla_q_norm_g", "mla_kv_norm_g", "mla_w_uq", "mla_w_ukv",
             "fox_b_f", "gmlp_ln_g", "gmlp_ln_b", "gmlp_w_s", "gmlp_b_s"]
    w = dict(zip(names, [ada_w, ada_b, ln_g, ln_b, ffn1_w_in, ffn1_w_out, ffn2_w_in, ffn2_w_out, mix_w_in, mix_w_out,
                         hgrn_lb_logits, hgrn_norm_g, mla_q_norm_g, mla_kv_norm_g, mla_w_uq, mla_w_ukv, fox_b_f,
                         gmlp_ln_g, gmlp_ln_b, gmlp_w_s, gmlp_b_s]))
    m = dict(zip(names, [m_ada_w, m_ada_b, m_ln_g, m_ln_b, m_ffn1_w_in, m_ffn1_w_out, m_ffn2_w_in, m_ffn2_w_out,
                         m_mix_w_in, m_mix_w_out, m_hgrn_lb_logits, m_hgrn_norm_g, m_mla_q_norm_g, m_mla_kv_norm_g,
                         m_mla_w_uq, m_mla_w_ukv, m_fox_b_f, m_gmlp_ln_g, m_gmlp_ln_b, m_gmlp_w_s, m_gmlp_b_s]))
    v = dict(zip(names, [v_ada_w, v_ada_b, v_ln_g, v_ln_b, v_ffn1_w_in, v_ffn1_w_out, v_ffn2_w_in, v_ffn2_w_out,
                         v_mix_w_in, v_mix_w_out, v_hgrn_lb_logits, v_hgrn_norm_g, v_mla_q_norm_g, v_mla_kv_norm_g,
                         v_mla_w_uq, v_mla_w_ukv, v_fox_b_f, v_gmlp_ln_g, v_gmlp_ln_b, v_gmlp_w_s, v_gmlp_b_s]))
    bsz, seq, d = x.shape
    me = 4 * lax.axis_index("x") + 2 * lax.axis_index("y") + lax.axis_index("c")
    mix_src, uq_src, ukv_src, mo_src = _mix_in_src(), _uq_src(), _ukv_src(), _mo_src()

    part_names = {"ffn1": ["ffn1_w_in", "ffn1_w_out"], "mix": ["mix_w_in", "mix_w_out", "mla_w_uq", "mla_w_ukv"],
                  "ffn2": ["ffn2_w_in", "ffn2_w_out"]}
    group_of = {}
    for l in range(DEPTH):
        for part in ("ffn1", "mix", "ffn2"):
            group_of[(l, part)] = (0, part) if l == 0 else (l, "all")
    in_flight = {}
    transposed = ("ffn1_w_in", "ffn2_w_in")

    def start_group(key, behind=None):
        members = [(l, part) for (l, part), g in group_of.items() if g == key]
        labels = [(l, n) for l, part in members for n in part_names[part]]
        shards = []
        for l, n in labels:
            a = w[n][l]
            if n == "mix_w_in":
                a = _pack_cols(a, mix_src)
            if n in transposed:
                a = jnp.swapaxes(w[n], 1, 2)[l]
            shards.append(a.astype(BF16))
        if behind is not None:
            shards, _ = lax.optimization_barrier((shards, behind))
        in_flight[key] = (labels, _push_start(shards, f"gather_start_{key[0]}_{key[1]}", whole=True))

    keys_in_order = list(dict.fromkeys(group_of.values()))
    start_group(keys_in_order[0])

    gathered = _all_gather([c, ln_g, ln_b], "gather_inputs")
    c_all = gathered[0].reshape(N_DEV * bsz, d)
    ln_g_full = jnp.moveaxis(gathered[1], 0, 2).reshape(DEPTH, 3, d)
    ln_b_full = jnp.moveaxis(gathered[2], 0, 2).reshape(DEPTH, 3, d)

    mod_cols = _ada_fwd(c_all, ada_w, "ada_fwd")
    mod_all, = _all_gather([mod_cols], "gather_mod")
    mod_mine = lax.dynamic_slice_in_dim(mod_all, me * bsz, bsz, axis=2)
    mod = jnp.moveaxis(mod_mine, 0, 2).reshape(DEPTH, bsz, N_MOD * d) + ada_b[:, None, :]
    for key in keys_in_order[1:]:
        start_group(key, behind=mod)
    tie = sum(h[-1][0, 0] for _, h in in_flight.values())
    mod = mod.reshape(DEPTH, bsz, N_MOD, d) + tie

    arrived, laid_out = {}, {}

    def weights(l, part, after):
        if (l, part) not in laid_out:
            laid_out[(l, part)] = lay_out(l, part, after)
        return laid_out[(l, part)]

    def lay_out(l, part, after):
        key = group_of[(l, part)]
        if key not in arrived:
            labels, (send_sems, recv_sems, srcs, lands, _) = in_flight[key]
            _, lands = _push_wait(send_sems, recv_sems, srcs, lands, after, f"gather_wait_{key[0]}_{key[1]}",
                                  whole=True)
            arrived[key] = dict(zip(labels, lands))
        gw = {n: arrived[key][(l, n)] for n in part_names[part]}
        if part != "mix":
            return {f"{part}_in": gw[f"{part}_w_in"], f"{part}_out": gw[f"{part}_w_out"].reshape(4, 704, d)}
        uq = jnp.moveaxis(gw["mla_w_uq"], 0, 1).reshape(256, 384)
        ukv = jnp.moveaxis(gw["mla_w_ukv"], 0, 1).reshape(128, 512)
        return {"mix_in": gw["mix_w_in"].reshape(d, PACK_W),
                "mix_out": _pack_cols(gw["mix_w_out"].reshape(d, d).T, mo_src).T,
                "uq": _pack_cols(uq, uq_src), "ukv": _pack_cols(ukv, ukv_src)}

    p = dict(w)
    p["ln_g"], p["ln_b"] = ln_g_full, ln_b_full
    def chunks(name, arr):
        if name in ("ffn1_in", "ffn2_in"):
            return arr
        if name in ("ffn1_out", "ffn2_out"):
            return arr.reshape(N_DEV, arr.shape[1] // 2, d)
        if name == "mix_in":
            return arr.reshape(N_DEV, d // N_DEV, PACK_W)
        return _unpack_cols(arr.T, mo_src, d).T.astype(BF16).reshape(N_DEV, d // N_DEV, d)

    pending, started = {}, []

    def grads_ready(l, name, grad):
        pending[(name, l)] = chunks(name, grad)
        flush = name == "ffn1_in" if l > 0 else name in ("ffn2_in", "mix_out", "mix_in", "ffn1_out", "ffn1_in")
        if not flush:
            return None
        keys = sorted(pending)
        handles = _push_start([pending[k] for k in keys], f"push_start_{len(started)}")
        pending.clear()
        started.append((keys, handles, l == 0 and name.startswith("ffn1")))
        return handles[-1][0, 0]

    loss, grad_x, dmod, big, small = _local_step(x, mod, loss_target, weights, p, grads_ready)
    del big

    recv, out = {}, {}

    def arrive(n, after):
        keys, (send_sems, recv_sems, srcs, lands, _), _ = started[n]
        srcs, lands = _push_wait(send_sems, recv_sems, srcs, lands, after, f"push_wait_{n}")
        for k, src, land in zip(keys, srcs, lands):
            recv[k] = (land, src)

    big_of = {"ffn1_w_in": "ffn1_in", "ffn1_w_out": "ffn1_out", "ffn2_w_in": "ffn2_in", "ffn2_w_out": "ffn2_out",
              "mix_w_in": "mix_in", "mix_w_out": "mix_out"}
    chain = {name: None for name in big_of}

    def big_update(key, l):
        name = next(nm for nm, k in big_of.items() if k == key)
        parts, src = recv[(key, l)]
        if key == "mix_in":
            parts = _unpack_cols(parts, mix_src, MIX_ORIG_W)
            src = _unpack_cols(src, mix_src, MIX_ORIG_W)
        view = (lambda a: jnp.swapaxes(a, 1, 2)) if name in transposed else (lambda a: a)
        chain[name] = _adamw(parts, (src, me), view(w[name]), view(m[name]), view(v[name]), f"adamw_{name}_l{l}",
                             layer=l, prev=chain[name])

    def update(name, grad):
        shape = w[name].shape
        as3 = lambda a: a.reshape(1, -1, shape[-1])
        res = _adamw(as3(grad), None, as3(w[name]), as3(m[name]), as3(v[name]), f"adamw_{name}")
        out[name] = tuple(r.reshape(shape) for r in res)

    for n, (keys, _, last) in enumerate(started):
        if not last:
            arrive(n, grad_x)
            for key, l in keys:
                big_update(key, l)

    dmod_flat = dmod.reshape(DEPTH, bsz, N_MOD * d)
    done = [r[0] for r in chain.values() if r is not None]
    if done:
        dmod_flat, _ = lax.optimization_barrier((dmod_flat, done))
    dmod_all, = _all_gather([dmod_flat], "gather_dmod")
    dmod_full = jnp.moveaxis(dmod_all, 0, 1).reshape(DEPTH, N_DEV * bsz, N_MOD * d)
    cols = ada_w.shape[2]
    dmod_cols = lax.dynamic_slice_in_dim(dmod_full, me * cols, cols, axis=2)
    g_ada_w, g_ada_b = _ada_bwd(c_all, dmod_cols, dmod_full, "ada_bwd")
    res = None
    for l in range(DEPTH):
        res = _adamw(g_ada_w[l][None], None, ada_w, m_ada_w, v_ada_w, f"adamw_ada_w_l{l}", layer=l, prev=res)
    out["ada_w"] = tuple(res)
    update("ada_b", g_ada_b.reshape(DEPTH, N_MOD * d))

    items = _small_grad_list(small, loss)
    parts, = _all_gather([_pack_small(items)], "gather_small")
    sg = _unpack_small(_sum_parts(parts, "sum_small"), items)
    for name in ("ln_g", "ln_b"):
        update(name, lax.dynamic_slice_in_dim(sg[name], me * (d // N_DEV), d // N_DEV, axis=2))
    for name, width in (("mla_w_uq", 48), ("mla_w_ukv", 64)):
        update(name, lax.dynamic_slice_in_dim(sg[name], me * width, width, axis=2))
    for name in ("hgrn_lb_logits", "hgrn_norm_g", "mla_q_norm_g", "mla_kv_norm_g", "fox_b_f", "gmlp_ln_g",
                 "gmlp_ln_b", "gmlp_w_s", "gmlp_b_s"):
        update(name, sg[name])

    for n, (keys, _, last) in enumerate(started):
        if last:
            arrive(n, out["gmlp_w_s"][0])
            for key, l in keys:
                big_update(key, l)
    for name in big_of:
        out[name] = tuple(jnp.swapaxes(r, 1, 2) if name in transposed else r for r in chain[name])

    return (sg["loss"][0], grad_x, *[out[n][0] for n in names], *[out[n][1] for n in names],
            *[out[n][2] for n in names], *[out[n][3] for n in names])
```

```python
import functools

import numpy as np
import jax
import jax.numpy as jnp
from jax import lax
from jax.experimental import pallas as pl
from jax.experimental.pallas import tpu as pltpu

F32 = jnp.float32
BF16 = jnp.bfloat16
HI = lax.Precision.HIGHEST

D_MODEL = 1024
DEPTH = 2
GROUP_WIDTH = 256
N_HEADS = 4
HEAD_DIM = 64
A_CHUNK = 16
LB_FLOOR = 1e-30
B_NOPE = 64
B_ROPE = 32
ROPE_THETA = 10000.0
D_CHUNK = 128
D_FF = 2816
N_MOD = 9
ALPHA = (2 * DEPTH) ** 0.25
LN_EPS = 1e-5
RMS_EPS = 1e-6
ADAM_LR = 0.001
ADAM_B1 = 0.9
ADAM_B2 = 0.999
ADAM_EPS = 1e-08
ADAM_WD = 0.01
ADAM_STEP = 10

N_DEV = 8
LANES = 128
PACK_W = 3712
MO_W = 1536
VMEM_LIMIT = 56 * 1024 * 1024
NEG = -1e30
ATTN_TILE = 512

MIX_ORIG_W = 2724
O_BCQ, O_BCKV, O_BKR, O_CQ, O_CK, O_CV, O_CF, O_DU, O_DV = 1024, 1280, 1408, 1440, 1696, 1952, 2208, 2212, 2468
P_B, P_KR, P_CQ, P_CKV, P_D, P_CF = 1024, 1408, 1536, 2048, 3072, 3584


_DN = {"nn": (((1,), (0,)), ((), ())), "nt": (((1,), (1,)), ((), ())), "tn": (((0,), (0,)), ((), ()))}


def _raw_bdot(a, b, mode):
    return lax.dot_general(a.astype(BF16), b.astype(BF16), _DN[mode], preferred_element_type=F32)


@functools.partial(jax.custom_vjp, nondiff_argnums=(2,))
def _bdot(a, b, mode):
    return _raw_bdot(a, b, mode)


def _bdot_fwd(a, b, mode):
    return _raw_bdot(a, b, mode), (a, b)


def _bdot_bwd(mode, res, g):
    a, b = res
    if mode == "nn":
        return _raw_bdot(g, b, "nt"), _raw_bdot(a, g, "tn")
    if mode == "nt":
        return _raw_bdot(g, b, "nn"), _raw_bdot(g, a, "tn")
    return _raw_bdot(b, g, "nt"), _raw_bdot(a, g, "nn")


_bdot.defvjp(_bdot_fwd, _bdot_bwd)


def _cparams(sem):
    return pltpu.CompilerParams(dimension_semantics=sem, vmem_limit_bytes=VMEM_LIMIT)


def _mix_in_src():
    src = -np.ones(PACK_W, np.int64)
    src[0:P_KR] = np.arange(0, O_BKR)
    src[P_KR + 64:P_KR + 80] = O_BKR + np.arange(16)
    src[P_KR + 96:P_KR + 112] = O_BKR + 16 + np.arange(16)
    for h in range(N_HEADS):
        src[P_CQ + 128 * h:P_CQ + 128 * h + 64] = O_CQ + 64 * h + np.arange(64)
        src[P_CKV + 256 * h:P_CKV + 256 * h + 64] = O_CK + 64 * h + np.arange(64)
        src[P_CKV + 256 * h + 128:P_CKV + 256 * h + 192] = O_CV + 64 * h + np.arange(64)
    src[P_D:P_D + 512] = O_DU + np.arange(512)
    src[P_CF:P_CF + 4] = O_CF + np.arange(4)
    return src


def _uq_src():
    src = -np.ones(512, np.int64)
    for h in range(N_HEADS):
        src[128 * h:128 * h + 64] = 96 * h + np.arange(64)
        src[128 * h + 64:128 * h + 80] = 96 * h + 64 + np.arange(16)
        src[128 * h + 96:128 * h + 112] = 96 * h + 80 + np.arange(16)
    return src


def _ukv_src():
    src = -np.ones(1024, np.int64)
    for h in range(N_HEADS):
        src[256 * h:256 * h + 64] = 128 * h + np.arange(64)
        src[256 * h + 128:256 * h + 192] = 128 * h + 64 + np.arange(64)
    return src


def _mo_src():
    src = -np.ones(MO_W, np.int64)
    src[0:256] = np.arange(256)
    for g in range(2):
        for h in range(N_HEADS):
            src[256 + 512 * g + 128 * h:256 + 512 * g + 128 * h + 64] = 256 + 256 * g + 64 * h + np.arange(64)
    src[1280:1536] = 768 + np.arange(256)
    return src


def _runs(idx):
    runs, i = [], 0
    while i < len(idx):
        j = i + 1
        while j < len(idx) and ((idx[i] < 0 and idx[j] < 0) or (idx[i] >= 0 and idx[j] == idx[i] + j - i)):
            j += 1
        runs.append((int(idx[i]), j - i))
        i = j
    return runs


def _take_runs(w, idx):
    parts = [jnp.zeros(w.shape[:-1] + (n,), w.dtype) if s < 0 else lax.slice_in_dim(w, s, s + n, axis=w.ndim - 1)
             for s, n in _runs(idx)]
    return jnp.concatenate(parts, axis=-1)


def _pack_cols(w, src):
    return _take_runs(w, src)


def _unpack_cols(wp, src, n):
    dst = np.zeros(n, np.int64)
    dst[src[src >= 0]] = np.nonzero(src >= 0)[0]
    return _take_runs(wp, dst)


def _rope_tables(seq):
    half = B_ROPE // 2
    inv_freq = ROPE_THETA ** (-jnp.arange(half, dtype=F32) / half)
    ang = jnp.arange(seq).astype(F32)[:, None] * inv_freq[None, :]
    cos, sin = jnp.cos(ang), jnp.sin(ang)
    z16 = jnp.zeros((seq, 16), F32)
    c = jnp.concatenate([jnp.ones((seq, 64), F32), cos, z16, cos, z16], axis=1)
    s1 = jnp.concatenate([jnp.zeros((seq, 64), F32), -sin, z16, z16, z16], axis=1)
    s2 = jnp.concatenate([jnp.zeros((seq, 64), F32), z16, z16, sin, z16], axis=1)
    return c, s1, s2


def _matmul(a, b, *, mode, group_out, out_dtype, tm, tk, name):
    ga, gb = a.shape[0], b.shape[0]
    g_n = max(ga, gb)
    if mode == "tn":
        k_dim, m_dim = a.shape[1:]
    else:
        m_dim, k_dim = a.shape[1:]
    n_dim = b.shape[1] if mode == "nt" else b.shape[2]
    assert m_dim % tm == 0 and k_dim % tk == 0
    kt = k_dim // tk
    n_red = kt if group_out else g_n * kt
    g_out = g_n if group_out else 1

    def split(g, r):
        return (g, r) if group_out else (r // kt, r % kt)

    def a_map(g, i, r):
        gg, kk = split(g, r)
        gg = gg if ga > 1 else 0
        return (gg, kk, i) if mode == "tn" else (gg, i, kk)

    def b_map(g, i, r):
        gg, kk = split(g, r)
        gg = gg if gb > 1 else 0
        return (gg, 0, kk) if mode == "nt" else (gg, kk, 0)

    a_blk = (None, tk, tm) if mode == "tn" else (None, tm, tk)
    b_blk = (None, n_dim, tk) if mode == "nt" else (None, tk, n_dim)
    dn = _DN[mode]

    def body(a_ref, b_ref, o_ref, *scratch):
        part = lax.dot_general(a_ref[...].astype(BF16), b_ref[...].astype(BF16), dn, preferred_element_type=F32)
        if n_red == 1:
            o_ref[...] = part.astype(o_ref.dtype)
            return
        acc_ref, = scratch
        r = pl.program_id(2)

        @pl.when(r == 0)
        def _():
            acc_ref[...] = part

        @pl.when(r > 0)
        def _():
            acc_ref[...] += part

        @pl.when(r == n_red - 1)
        def _():
            o_ref[...] = acc_ref[...].astype(o_ref.dtype)

    return pl.pallas_call(
        body, name=name, grid=(g_out, m_dim // tm, n_red),
        in_specs=[pl.BlockSpec(a_blk, a_map), pl.BlockSpec(b_blk, b_map)],
        out_specs=pl.BlockSpec((None, tm, n_dim), lambda g, i, r: (g, i, 0)),
        out_shape=jax.ShapeDtypeStruct((g_out, m_dim, n_dim), out_dtype),
        scratch_shapes=[] if n_red == 1 else [pltpu.VMEM((tm, n_dim), F32)],
        compiler_params=_cparams(("parallel", "parallel", "arbitrary")),
    )(a, b)


def _matmul_groupsum(a, b, *, out_dtype, tm, name):
    g_n, m_dim, k_dim = a.shape
    n_dim = b.shape[2]
    assert m_dim % tm == 0 and b.shape[:2] == (g_n, k_dim)

    def body(a_ref, b_ref, o_ref):
        acc = jnp.dot(a_ref[0], b_ref[0], preferred_element_type=F32)
        for g in range(1, g_n):
            acc = acc + jnp.dot(a_ref[g], b_ref[g], preferred_element_type=F32)
        o_ref[...] = acc.astype(o_ref.dtype)

    return pl.pallas_call(
        body, name=name, grid=(m_dim // tm,),
        in_specs=[pl.BlockSpec((g_n, tm, k_dim), lambda i: (0, i, 0)),
                  pl.BlockSpec((g_n, k_dim, n_dim), lambda i: (0, 0, 0))],
        out_specs=pl.BlockSpec((tm, n_dim), lambda i: (i, 0)),
        out_shape=jax.ShapeDtypeStruct((m_dim, n_dim), out_dtype),
        compiler_params=_cparams(("parallel",)),
    )(a, b)


def _row_spec(ts, d):
    return pl.BlockSpec((None, ts, d), lambda b, s: (b, s, 0))


def _mod_spec(d):
    return pl.BlockSpec((None, N_MOD, d), lambda b, s: (b, 0, 0))


def _vec_spec(d):
    return pl.BlockSpec((1, d), lambda b, s: (0, 0))


def _bvec_spec(d):
    return pl.BlockSpec((None, 1, d), lambda b, s: (b, 0, 0))


def _modulate(x, mod, sh_row, sc_row, name, ts=512):
    bsz, seq, d = x.shape

    def body(x_ref, mod_ref, o_ref):
        sh = mod_ref[sh_row:sh_row + 1, :]
        sc = mod_ref[sc_row:sc_row + 1, :]
        o_ref[...] = (x_ref[...] * (1.0 + sc) + sh).astype(o_ref.dtype)

    return pl.pallas_call(
        body, name=name, grid=(bsz, seq // ts),
        in_specs=[_row_spec(ts, d), _mod_spec(d)], out_specs=_row_spec(ts, d),
        out_shape=jax.ShapeDtypeStruct((bsz, seq, d), BF16),
        compiler_params=_cparams(("parallel", "parallel")),
    )(x, mod)


def _modulate_bwd(dh, x, mod, dx_res, sc_row, name, ts=512):
    bsz, seq, d = x.shape

    def body(dh_ref, x_ref, mod_ref, dxr_ref, dx_ref, dsh_ref, dsc_ref):
        s = pl.program_id(1)
        sc = mod_ref[sc_row:sc_row + 1, :]
        dh_v = dh_ref[...]
        dx_ref[...] = dxr_ref[...] + dh_v * (1.0 + sc)
        psh = jnp.sum(dh_v, axis=0, keepdims=True)
        psc = jnp.sum(dh_v * x_ref[...], axis=0, keepdims=True)

        @pl.when(s == 0)
        def _():
            dsh_ref[...] = psh
            dsc_ref[...] = psc

        @pl.when(s > 0)
        def _():
            dsh_ref[...] += psh
            dsc_ref[...] += psc

    return pl.pallas_call(
        body, name=name, grid=(bsz, seq // ts),
        in_specs=[_row_spec(ts, d), _row_spec(ts, d), _mod_spec(d), _row_spec(ts, d)],
        out_specs=[_row_spec(ts, d), _bvec_spec(d), _bvec_spec(d)],
        out_shape=[jax.ShapeDtypeStruct((bsz, seq, d), F32), jax.ShapeDtypeStruct((bsz, 1, d), F32),
                   jax.ShapeDtypeStruct((bsz, 1, d), F32)],
        compiler_params=_cparams(("parallel", "arbitrary")),
    )(dh, x, mod, dx_res)


def _res_ln_fn(x, f, g, lng, lnb, cmul):
    r = ALPHA * x + (cmul * (1.0 + g)) * f
    mu = jnp.mean(r, axis=-1, keepdims=True)
    rc = r - mu
    var = jnp.mean(rc * rc, axis=-1, keepdims=True)
    return rc * lax.rsqrt(var + LN_EPS) * lng + lnb


def _res_ln(x, f, mod, lng, lnb, g_row, cmul, name, nxt=None, ts=512):
    bsz, seq, d = x.shape

    def body(*refs):
        x_ref, f_ref, mod_ref, lng_ref, lnb_ref = refs[:5]
        g = mod_ref[g_row:g_row + 1, :]
        y = _res_ln_fn(x_ref[...], f_ref[...], g, lng_ref[...], lnb_ref[...], cmul)
        if nxt is None:
            refs[5][...] = y
            return
        nmod_ref, o_ref, h_ref = refs[5:]
        o_ref[...] = y
        sh = nmod_ref[nxt[1]:nxt[1] + 1, :]
        sc = nmod_ref[nxt[2]:nxt[2] + 1, :]
        h_ref[...] = (y * (1.0 + sc) + sh).astype(h_ref.dtype)

    in_specs = [_row_spec(ts, d), _row_spec(ts, d), _mod_spec(d), _vec_spec(d), _vec_spec(d)]
    args = [x, f, mod, lng, lnb]
    out_specs, out_shape = [_row_spec(ts, d)], [jax.ShapeDtypeStruct((bsz, seq, d), F32)]
    if nxt is not None:
        in_specs.append(_mod_spec(d))
        args.append(nxt[0])
        out_specs.append(_row_spec(ts, d))
        out_shape.append(jax.ShapeDtypeStruct((bsz, seq, d), BF16))
    res = pl.pallas_call(
        body, name=name, grid=(bsz, seq // ts), in_specs=in_specs, out_specs=out_specs, out_shape=out_shape,
        compiler_params=_cparams(("parallel", "parallel")),
    )(*args)
    return (res[0], res[1]) if nxt is not None else (res[0], None)


def _res_ln_bwd(dy, x, f, mod, lng, lnb, g_row, cmul, name, ts=256):
    bsz, seq, d = x.shape

    def body(dy_ref, x_ref, f_ref, mod_ref, lng_ref, lnb_ref, dx_ref, df_ref, dg_ref, dlg_ref, dlb_ref):
        b, s = pl.program_id(0), pl.program_id(1)
        g = mod_ref[g_row:g_row + 1, :]
        _, vjp = jax.vjp(functools.partial(_res_ln_fn, cmul=cmul), x_ref[...], f_ref[...], g, lng_ref[...],
                         lnb_ref[...])
        dx, df, dg, dlg, dlb = vjp(dy_ref[...])
        dx_ref[...] = dx
        df_ref[...] = df.astype(df_ref.dtype)

        @pl.when(s == 0)
        def _():
            dg_ref[...] = dg

        @pl.when(s > 0)
        def _():
            dg_ref[...] += dg

        first = jnp.logical_and(b == 0, s == 0)

        @pl.when(first)
        def _():
            dlg_ref[...] = dlg
            dlb_ref[...] = dlb

        @pl.when(jnp.logical_not(first))
        def _():
            dlg_ref[...] += dlg
            dlb_ref[...] += dlb

    return pl.pallas_call(
        body, name=name, grid=(bsz, seq // ts),
        in_specs=[_row_spec(ts, d), _row_spec(ts, d), _row_spec(ts, d), _mod_spec(d), _vec_spec(d), _vec_spec(d)],
        out_specs=[_row_spec(ts, d), _row_spec(ts, d), _bvec_spec(d), _vec_spec(d), _vec_spec(d)],
        out_shape=[jax.ShapeDtypeStruct((bsz, seq, d), F32), jax.ShapeDtypeStruct((bsz, seq, d), BF16),
                   jax.ShapeDtypeStruct((bsz, 1, d), F32), jax.ShapeDtypeStruct((1, d), F32),
                   jax.ShapeDtypeStruct((1, d), F32)],
        compiler_params=_cparams(("arbitrary", "arbitrary")),
    )(dy, x, f, mod, lng, lnb)


def _loss_head(y, target, name, ts=512):
    bsz, seq, d = y.shape
    n_s = seq // ts

    def body(y_ref, t_ref, dy_ref, loss_ref, acc_ref):
        b, s = pl.program_id(0), pl.program_id(1)
        err = y_ref[...] - t_ref[...]
        dy_ref[...] = err * (1.0 / d)
        part = jnp.sum(err * err, axis=0, keepdims=True)
        first = jnp.logical_and(b == 0, s == 0)

        @pl.when(first)
        def _():
            acc_ref[...] = part

        @pl.when(jnp.logical_not(first))
        def _():
            acc_ref[...] += part

        @pl.when(jnp.logical_and(b == bsz - 1, s == n_s - 1))
        def _():
            loss_ref[...] = jnp.sum(acc_ref[...], axis=1, keepdims=True) * (0.5 / d)

    return pl.pallas_call(
        body, name=name, grid=(bsz, n_s),
        in_specs=[_row_spec(ts, d), _row_spec(ts, d)],
        out_specs=[_row_spec(ts, d), pl.BlockSpec((1, 1), lambda b, s: (0, 0))],
        out_shape=[jax.ShapeDtypeStruct((bsz, seq, d), F32), jax.ShapeDtypeStruct((1, 1), F32)],
        scratch_shapes=[pltpu.VMEM((1, d), F32)],
        compiler_params=_cparams(("arbitrary", "arbitrary")),
    )(y, target)


def _ffn_in_swiglu(h, w_in_t, name, tm=1024):
    t, d = h.shape
    n_sh, w, _ = w_in_t.shape
    half = n_sh // 2

    def body(h_ref, w_ref, z_ref, a_ref):
        hv = h_ref[...]
        g = lax.dot_general(hv, w_ref[0], _DN["nt"], preferred_element_type=F32)
        u = lax.dot_general(hv, w_ref[1], _DN["nt"], preferred_element_type=F32)
        z_ref[0] = g.astype(z_ref.dtype)
        z_ref[1] = u.astype(z_ref.dtype)
        a_ref[...] = (g * jax.nn.sigmoid(g) * u).astype(a_ref.dtype)

    return pl.pallas_call(
        body, name=name, grid=(half, t // tm),
        in_specs=[pl.BlockSpec((tm, d), lambda g, i: (i, 0)),
                  pl.BlockSpec((2, None, w, d), lambda g, i: (0, g, 0, 0))],
        out_specs=[pl.BlockSpec((2, None, tm, w), lambda g, i: (0, g, i, 0)),
                   pl.BlockSpec((None, tm, w), lambda g, i: (g, i, 0))],
        out_shape=[jax.ShapeDtypeStruct((2, half, t, w), BF16), jax.ShapeDtypeStruct((half, t, w), BF16)],
        compiler_params=_cparams(("parallel", "parallel")),
    )(h, w_in_t.reshape(2, half, w, d))


def _ffn_out_dx_swiglu(df, w_out, z, name, tm=1024):
    t, d = df.shape
    half, w, _ = w_out.shape

    def body(df_ref, w_ref, z_ref, dz_ref):
        da = lax.dot_general(df_ref[...], w_ref[...], _DN["nt"], preferred_element_type=F32)
        g = z_ref[0].astype(F32)
        u = z_ref[1].astype(F32)
        sig = jax.nn.sigmoid(g)
        dz_ref[0] = (da * u * (sig * (1.0 + g * (1.0 - sig)))).astype(dz_ref.dtype)
        dz_ref[1] = (da * (g * sig)).astype(dz_ref.dtype)

    zspec = pl.BlockSpec((2, None, tm, w), lambda g, i: (0, g, i, 0))
    return pl.pallas_call(
        body, name=name, grid=(half, t // tm),
        in_specs=[pl.BlockSpec((tm, d), lambda g, i: (i, 0)), pl.BlockSpec((None, w, d), lambda g, i: (g, 0, 0)),
                  zspec],
        out_specs=zspec, out_shape=jax.ShapeDtypeStruct(z.shape, BF16),
        compiler_params=_cparams(("parallel", "parallel")),
    )(df, w_out, z)


def _log_sigmoid(x):
    return jnp.minimum(x, 0.0) - jnp.log(1.0 + jnp.exp(-jnp.abs(x)))


def _hgrn_consts():
    r = lax.broadcasted_iota(jnp.int32, (GROUP_WIDTH, GROUP_WIDTH), 0)
    c = lax.broadcasted_iota(jnp.int32, (GROUP_WIDTH, GROUP_WIDTH), 1)
    bd = (r // HEAD_DIM == c // HEAD_DIM).astype(F32)
    r16 = lax.broadcasted_iota(jnp.int32, (A_CHUNK, A_CHUNK), 0)
    c16 = lax.broadcasted_iota(jnp.int32, (A_CHUNK, A_CHUNK), 1)
    tril = (r16 >= c16).astype(F32)
    rows = lax.broadcasted_iota(jnp.int32, (A_CHUNK, GROUP_WIDTH), 0)
    return bd, tril, rows


def _hgrn_lb(logits8, layer):
    rows = lax.broadcasted_iota(jnp.int32, logits8.shape, 0)
    valid = rows < DEPTH
    mx = jnp.max(jnp.where(valid, logits8, NEG), axis=0, keepdims=True)
    e = jnp.where(valid, jnp.exp(logits8 - mx), 0.0)
    sm = e / jnp.sum(e, axis=0, keepdims=True)
    pick = jnp.logical_and(rows >= 1, rows <= layer)
    return jnp.sum(jnp.where(pick, sm, 0.0), axis=0, keepdims=True)


def _hgrn_chunk(aq, af, ai, ag, logits8, norm_g, st, *, layer, consts):
    bd, tril, rows = consts
    lb = _hgrn_lb(logits8, layer)
    la = jnp.log(jnp.maximum(lb, LB_FLOOR))
    b2 = jnp.log(1.0 - lb) + _log_sigmoid(af)
    log_f = jnp.maximum(la, b2) + jnp.log(1.0 + jnp.exp(-jnp.abs(la - b2)))
    k = 1.0 - jnp.exp(log_f)
    qf = aq * jax.nn.sigmoid(aq)
    g_cum = jnp.dot(tril, log_f, precision=HI, preferred_element_type=F32)

    c, w = A_CHUNK, GROUP_WIDTH

    def by_key(v):
        return jnp.broadcast_to(v[:, None, :], (c, c, w))

    def by_query(v):
        return jnp.broadcast_to(v[None, :, :], (c, c, w))

    s_i = lax.broadcasted_iota(jnp.int32, (c, c, w), 0)
    t_i = lax.broadcasted_iota(jnp.int32, (c, c, w), 1)
    rel = jnp.where(t_i >= s_i, by_query(g_cum) - by_key(g_cum), NEG)
    pairs = by_query(qf) * by_key(k) * jnp.exp(rel)
    a_all = _bdot(pairs.reshape(c * c, w), bd, "nn").reshape(c, c, w)
    o = jnp.sum(a_all * by_key(ai), axis=0)
    q_dec = qf * jnp.exp(g_cum)
    o = o + _bdot(q_dec, st, "nt")
    g_last = jnp.sum(jnp.where(rows == c - 1, g_cum, 0.0), axis=0, keepdims=True)
    k_end = k * jnp.exp(g_last - g_cum)
    kv = _bdot(ai, k_end, "tn")
    st_new = st * jnp.exp(g_last) + kv * bd
    ms = _bdot(o * o, bd, "nn") * (1.0 / HEAD_DIM)
    o = o * lax.rsqrt(ms + RMS_EPS) * norm_g
    return o * (ag * jax.nn.sigmoid(ag)), st_new


def _hgrn_fwd(proj, logits8, norm_g, layer, name, ts=256):
    bsz, seq, _ = proj.shape
    n_ch = ts // A_CHUNK

    def body(p_ref, lg_ref, ng_ref, o_ref, st_ref, st_scr):
        @pl.when(pl.program_id(1) == 0)
        def _():
            st_scr[...] = jnp.zeros_like(st_scr)

        consts = _hgrn_consts()
        logits_v, ng_v = lg_ref[...], ng_ref[...]

        def chunk(ci, carry):
            r = pl.multiple_of(ci * A_CHUNK, A_CHUNK)
            st = st_scr[...]
            st_ref[ci] = st
            o, st_new = _hgrn_chunk(
                p_ref[pl.ds(r, A_CHUNK), 0:256], p_ref[pl.ds(r, A_CHUNK), 256:512],
                p_ref[pl.ds(r, A_CHUNK), 512:768], p_ref[pl.ds(r, A_CHUNK), 768:1024],
                logits_v, ng_v, st, layer=layer, consts=consts)
            o_ref[pl.ds(r, A_CHUNK), :] = o.astype(o_ref.dtype)
            st_scr[...] = st_new
            return carry

        lax.fori_loop(0, n_ch, chunk, 0, unroll=2)

    return pl.pallas_call(
        body, name=name, grid=(bsz, seq // ts),
        in_specs=[pl.BlockSpec((None, ts, 1024), lambda b, s: (b, s, 0)),
                  pl.BlockSpec((8, GROUP_WIDTH), lambda b, s: (0, 0)),
                  pl.BlockSpec((1, GROUP_WIDTH), lambda b, s: (0, 0))],
        out_specs=[pl.BlockSpec((None, ts, GROUP_WIDTH), lambda b, s: (b, s, 0)),
                   pl.BlockSpec((None, n_ch, GROUP_WIDTH, GROUP_WIDTH), lambda b, s: (b, s, 0, 0))],
        out_shape=[jax.ShapeDtypeStruct((bsz, seq, MO_W), BF16),
                   jax.ShapeDtypeStruct((bsz, seq // A_CHUNK, GROUP_WIDTH, GROUP_WIDTH), F32)],
        scratch_shapes=[pltpu.VMEM((GROUP_WIDTH, GROUP_WIDTH), F32)],
        compiler_params=_cparams(("parallel", "arbitrary")),
    )(proj, logits8, norm_g)


def _hgrn_bwd(dmo, proj, states, logits8, norm_g, layer, name, ts=256):
    bsz, seq, _ = proj.shape
    n_ch = ts // A_CHUNK
    n_s = seq // ts

    def body(do_ref, p_ref, st_ref, lg_ref, ng_ref, dp_ref, dlg_ref, dng_ref, dst_scr):
        b, s = pl.program_id(0), pl.program_id(1)

        @pl.when(s == 0)
        def _():
            dst_scr[...] = jnp.zeros_like(dst_scr)

        @pl.when(jnp.logical_and(b == 0, s == 0))
        def _():
            dlg_ref[...] = jnp.zeros_like(dlg_ref)
            dng_ref[...] = jnp.zeros_like(dng_ref)

        consts = _hgrn_consts()
        logits_v, ng_v = lg_ref[...], ng_ref[...]
        fn = functools.partial(_hgrn_chunk, layer=layer, consts=consts)

        def chunk(t, carry):
            ci = n_ch - 1 - t
            r = pl.multiple_of(ci * A_CHUNK, A_CHUNK)
            _, vjp = jax.vjp(
                fn, p_ref[pl.ds(r, A_CHUNK), 0:256], p_ref[pl.ds(r, A_CHUNK), 256:512],
                p_ref[pl.ds(r, A_CHUNK), 512:768], p_ref[pl.ds(r, A_CHUNK), 768:1024],
                logits_v, ng_v, st_ref[ci])
            daq, daf, dai, dag, dlg, dng, dst = vjp((do_ref[pl.ds(r, A_CHUNK), :], dst_scr[...]))
            dp_ref[pl.ds(r, A_CHUNK), 0:256] = daq.astype(dp_ref.dtype)
            dp_ref[pl.ds(r, A_CHUNK), 256:512] = daf.astype(dp_ref.dtype)
            dp_ref[pl.ds(r, A_CHUNK), 512:768] = dai.astype(dp_ref.dtype)
            dp_ref[pl.ds(r, A_CHUNK), 768:1024] = dag.astype(dp_ref.dtype)
            dlg_ref[...] += dlg
            dng_ref[...] += dng
            dst_scr[...] = dst
            return carry

        lax.fori_loop(0, n_ch, chunk, 0, unroll=2)

    rev = lambda b, s: (b, n_s - 1 - s, 0)
    return pl.pallas_call(
        body, name=name, grid=(bsz, n_s),
        in_specs=[pl.BlockSpec((None, ts, GROUP_WIDTH), rev),
                  pl.BlockSpec((None, ts, 1024), rev),
                  pl.BlockSpec((None, n_ch, GROUP_WIDTH, GROUP_WIDTH), lambda b, s: (b, n_s - 1 - s, 0, 0)),
                  pl.BlockSpec((8, GROUP_WIDTH), lambda b, s: (0, 0)),
                  pl.BlockSpec((1, GROUP_WIDTH), lambda b, s: (0, 0))],
        out_specs=[pl.BlockSpec((None, ts, 1024), rev),
                   pl.BlockSpec((8, GROUP_WIDTH), lambda b, s: (0, 0)),
                   pl.BlockSpec((1, GROUP_WIDTH), lambda b, s: (0, 0))],
        out_shape=[jax.ShapeDtypeStruct((bsz, seq, PACK_W), BF16),
                   jax.ShapeDtypeStruct((8, GROUP_WIDTH), F32), jax.ShapeDtypeStruct((1, GROUP_WIDTH), F32)],
        scratch_shapes=[pltpu.VMEM((GROUP_WIDTH, GROUP_WIDTH), F32)],
        compiler_params=_cparams(("arbitrary", "arbitrary")),
    )(dmo, proj, states, logits8, norm_g)


def _rms_fn(x, g):
    return x * lax.rsqrt(jnp.mean(x * x, axis=-1, keepdims=True) + RMS_EPS) * g


def _tile4(t):
    return jnp.concatenate([t, t, t, t], axis=1)


def _rope(x, c, s1, s2):
    w = x.shape[-1]
    return x * c + pltpu.roll(x, 32, axis=1) * s2 + pltpu.roll(x, w - 32, axis=1) * s1


def _rope_t(dy, c, s1, s2):
    w = dy.shape[-1]
    return dy * c + pltpu.roll(dy * s2, w - 32, axis=1) + pltpu.roll(dy * s1, 32, axis=1)


def _mla_pre(proj, qg, kvg, wq, wkv, tabs, name, ts=256):
    bsz, seq, _ = proj.shape

    def body(p_ref, qg_ref, kvg_ref, wq_ref, wkv_ref, c_ref, s1_ref, s2_ref, q_ref, kv_ref):
        nq = _rms_fn(p_ref[:, 0:256], qg_ref[...])
        nkv = _rms_fn(p_ref[:, 256:384], kvg_ref[...])
        c, s1, s2 = c_ref[...], s1_ref[...], s2_ref[...]
        qp = jnp.dot(nq.astype(BF16), wq_ref[...], preferred_element_type=F32)
        q_ref[...] = _rope(qp, _tile4(c), _tile4(s1), _tile4(s2)).astype(q_ref.dtype)
        kv = jnp.dot(nkv.astype(BF16), wkv_ref[...], preferred_element_type=F32)
        krr = _rope(p_ref[:, 384:512], c, s1, s2)
        zero = jnp.zeros_like(krr)
        kv_ref[...] = (kv + jnp.concatenate([krr, zero] * N_HEADS, axis=1)).astype(kv_ref.dtype)

    tab_spec = pl.BlockSpec((ts, LANES), lambda b, s: (s, 0))
    return pl.pallas_call(
        body, name=name, grid=(bsz, seq // ts),
        in_specs=[pl.BlockSpec((None, ts, 512), lambda b, s: (b, s, P_B // 512)),
                  _vec_spec(256), _vec_spec(128),
                  pl.BlockSpec((256, 512), lambda b, s: (0, 0)), pl.BlockSpec((128, 1024), lambda b, s: (0, 0)),
                  tab_spec, tab_spec, tab_spec],
        out_specs=[_row_spec(ts, 512), _row_spec(ts, 1024)],
        out_shape=[jax.ShapeDtypeStruct((bsz, seq, 512), BF16), jax.ShapeDtypeStruct((bsz, seq, 1024), BF16)],
        compiler_params=_cparams(("parallel", "parallel")),
    )(proj, qg, kvg, wq, wkv, *tabs)


def _mla_pre_bwd(dq, dkv, dproj, proj, qg, kvg, wq, wkv, tabs, name, ts=256):
    bsz, seq, _ = proj.shape

    def body(dq_ref, dkv_ref, dp_any, p_ref, qg_ref, kvg_ref, wq_ref, wkv_ref, c_ref, s1_ref, s2_ref,
             dp_ref, dqg_ref, dkvg_ref, dwq_ref, dwkv_ref):
        del dp_any
        first = jnp.logical_and(pl.program_id(0) == 0, pl.program_id(1) == 0)

        @pl.when(first)
        def _():
            dqg_ref[...] = jnp.zeros_like(dqg_ref)
            dkvg_ref[...] = jnp.zeros_like(dkvg_ref)
            dwq_ref[...] = jnp.zeros_like(dwq_ref)
            dwkv_ref[...] = jnp.zeros_like(dwkv_ref)

        c, s1, s2 = c_ref[...], s1_ref[...], s2_ref[...]
        nq, vjp_q = jax.vjp(_rms_fn, p_ref[:, 0:256], qg_ref[...])
        nkv, vjp_kv = jax.vjp(_rms_fn, p_ref[:, 256:384], kvg_ref[...])
        dqp = _rope_t(dq_ref[...], _tile4(c), _tile4(s1), _tile4(s2)).astype(BF16)
        dkv_v = dkv_ref[...]
        dkv_b = dkv_v.astype(BF16)
        tn = (((0,), (0,)), ((), ()))
        nt = (((1,), (1,)), ((), ()))
        dwq_ref[...] += lax.dot_general(nq.astype(BF16), dqp, tn, preferred_element_type=F32)
        dwkv_ref[...] += lax.dot_general(nkv.astype(BF16), dkv_b, tn, preferred_element_type=F32)
        dcq, dqg = vjp_q(lax.dot_general(dqp, wq_ref[...], nt, preferred_element_type=F32))
        dckv, dkvg = vjp_kv(lax.dot_general(dkv_b, wkv_ref[...], nt, preferred_element_type=F32))
        dqg_ref[...] += dqg
        dkvg_ref[...] += dkvg
        dk_sum = dkv_v[:, 0:128] + dkv_v[:, 256:384] + dkv_v[:, 512:640] + dkv_v[:, 768:896]
        lane = lax.broadcasted_iota(jnp.int32, dk_sum.shape, 1)
        dkr = jnp.where(lane >= 64, _rope_t(dk_sum, c, s1, s2), 0.0)
        dp_ref[:, 0:256] = dcq.astype(dp_ref.dtype)
        dp_ref[:, 256:384] = dckv.astype(dp_ref.dtype)
        dp_ref[:, 384:512] = dkr.astype(dp_ref.dtype)

    tab_spec = pl.BlockSpec((ts, LANES), lambda b, s: (s, 0))
    const = lambda shape: pl.BlockSpec(shape, lambda b, s: (0, 0))
    return pl.pallas_call(
        body, name=name, grid=(bsz, seq // ts),
        in_specs=[_row_spec(ts, 512), _row_spec(ts, 1024), pl.BlockSpec(memory_space=pl.ANY),
                  pl.BlockSpec((None, ts, 512), lambda b, s: (b, s, P_B // 512)),
                  _vec_spec(256), _vec_spec(128), const((256, 512)), const((128, 1024)),
                  tab_spec, tab_spec, tab_spec],
        out_specs=[pl.BlockSpec((None, ts, 512), lambda b, s: (b, s, P_B // 512)),
                   _vec_spec(256), _vec_spec(128), const((256, 512)), const((128, 1024))],
        out_shape=[jax.ShapeDtypeStruct(dproj.shape, dproj.dtype), jax.ShapeDtypeStruct((1, 256), F32),
                   jax.ShapeDtypeStruct((1, 128), F32), jax.ShapeDtypeStruct((256, 512), F32),
                   jax.ShapeDtypeStruct((128, 1024), F32)],
        input_output_aliases={2: 0},
        compiler_params=_cparams(("arbitrary", "arbitrary")),
    )(dq, dkv, dproj, proj, qg, kvg, wq, wkv, *tabs)


def _fox_gate(proj, bf, name):
    bsz, seq, _ = proj.shape
    n_blk = seq // LANES

    def body(x_ref, bf_ref, f_ref):
        r_i = lax.broadcasted_iota(jnp.int32, (LANES, LANES), 0)
        c_i = lax.broadcasted_iota(jnp.int32, (LANES, LANES), 1)
        tril = (r_i >= c_i).astype(F32)
        bias = bf_ref[...]

        def blk(i, carry):
            r = pl.multiple_of(i * LANES, LANES)
            lf = _log_sigmoid(x_ref[pl.ds(r, LANES), :] + bias)
            f_ref[pl.ds(r, LANES), :] = jnp.dot(tril, lf, precision=HI, preferred_element_type=F32) + carry
            return carry + jnp.sum(lf, axis=0, keepdims=True)

        lax.fori_loop(0, n_blk, blk, jnp.zeros((1, LANES), F32))

    return pl.pallas_call(
        body, name=name, grid=(bsz,),
        in_specs=[pl.BlockSpec((None, seq, LANES), lambda b: (b, 0, P_CF // LANES)),
                  pl.BlockSpec((1, LANES), lambda b: (0, 0))],
        out_specs=pl.BlockSpec((None, seq, LANES), lambda b: (b, 0, 0)),
        out_shape=jax.ShapeDtypeStruct((bsz, seq, LANES), F32),
        compiler_params=_cparams(("parallel",)),
    )(proj, bf)


def _fox_gate_bwd(dfq, dfk_cols, dproj, proj, bf, name):
    bsz, seq, _ = proj.shape
    n_blk = seq // LANES

    def body(dfq_ref, dfk_ref, dp_any, x_ref, bf_ref, dp_ref, dbf_ref):
        del dp_any

        @pl.when(pl.program_id(0) == 0)
        def _():
            dbf_ref[...] = jnp.zeros_like(dbf_ref)

        r_i = lax.broadcasted_iota(jnp.int32, (LANES, LANES), 0)
        c_i = lax.broadcasted_iota(jnp.int32, (LANES, LANES), 1)
        triu = (r_i <= c_i).astype(F32)
        bias = bf_ref[...]

        def blk(t, carry):
            tail, dbf = carry
            r = pl.multiple_of((n_blk - 1 - t) * LANES, LANES)
            dc = dfk_ref[pl.ds(r, LANES), :]
            for hd in range(N_HEADS):
                dc = dc + jnp.where(c_i == hd, dfq_ref[hd, pl.ds(r, LANES), :], 0.0)
            dlf = jnp.dot(triu, dc, precision=HI, preferred_element_type=F32) + tail
            dx = dlf * (1.0 - jax.nn.sigmoid(x_ref[pl.ds(r, LANES), :] + bias))
            dp_ref[pl.ds(r, LANES), :] = dx.astype(dp_ref.dtype)
            return tail + jnp.sum(dc, axis=0, keepdims=True), dbf + jnp.sum(dx, axis=0, keepdims=True)

        z = jnp.zeros((1, LANES), F32)
        _, dbf = lax.fori_loop(0, n_blk, blk, (z, z))
        dbf_ref[...] += dbf

    return pl.pallas_call(
        body, name=name, grid=(bsz,),
        in_specs=[pl.BlockSpec((None, N_HEADS, seq, LANES), lambda b: (b, 0, 0, 0)),
                  pl.BlockSpec((None, seq, LANES), lambda b: (b, 0, 0)), pl.BlockSpec(memory_space=pl.ANY),
                  pl.BlockSpec((None, seq, LANES), lambda b: (b, 0, P_CF // LANES)),
                  pl.BlockSpec((1, LANES), lambda b: (0, 0))],
        out_specs=[pl.BlockSpec((None, seq, LANES), lambda b: (b, 0, P_CF // LANES)),
                   pl.BlockSpec((1, LANES), lambda b: (0, 0))],
        out_shape=[jax.ShapeDtypeStruct(dproj.shape, dproj.dtype), jax.ShapeDtypeStruct((1, LANES), F32)],
        input_output_aliases={2: 0},
        compiler_params=_cparams(("arbitrary",)),
    )(dfq, dfk_cols, dproj, proj, bf)


def _gate_terms(fc_ref, fr_ref, h, tq, tk):
    lane = lax.broadcasted_iota(jnp.int32, (tq, LANES), 1)
    fcol = jnp.sum(jnp.where(lane == h, fc_ref[...], 0.0), axis=1, keepdims=True)
    sub = lax.broadcasted_iota(jnp.int32, (8, tk), 0)
    frow = jnp.sum(jnp.where(sub == h, fr_ref[...], 0.0), axis=0, keepdims=True)
    return fcol - frow


def _scores(q_ref, k_ref, gate_refs, scale, h, masked, tq, tk):
    q = (q_ref[...].astype(F32) * scale).astype(BF16)
    s = lax.dot_general(q, k_ref[...].astype(BF16), _DN["nt"], preferred_element_type=F32)
    if gate_refs is not None:
        s = s + _gate_terms(gate_refs[0], gate_refs[1], h, tq, tk)
    if masked is not False:
        r_i = lax.broadcasted_iota(jnp.int32, (tq, tk), 0)
        c_i = lax.broadcasted_iota(jnp.int32, (tq, tk), 1)
        keep = c_i <= r_i
        s = jnp.where(keep if masked is True else jnp.logical_or(jnp.logical_not(masked), keep), s, NEG)
    return s, q


def _lanes(col):
    return jnp.broadcast_to(col, (col.shape[0], LANES))


def _attn_fwd(qa, q0, kva, kv0, mo, o0, gates, scale, name, tq=None):
    bsz, seq, _ = qa.shape
    tq = ATTN_TILE if tq is None else tq
    n_q = seq // tq
    gated = gates is not None

    def body(*refs):
        q_ref, k_ref, v_ref = refs[:3]
        gate_refs = refs[3:5] if gated else None
        o_ref, lse_ref, m_s, l_s, acc_s = refs[-5:]
        h, i, j = pl.program_id(1), pl.program_id(2), pl.program_id(3)

        @pl.when(j == 0)
        def _():
            m_s[...] = jnp.full_like(m_s, NEG)
            l_s[...] = jnp.zeros_like(l_s)
            acc_s[...] = jnp.zeros_like(acc_s)

        def step(masked):
            s, _ = _scores(q_ref, k_ref, gate_refs, scale, h, masked, tq, tq)
            m_prev = m_s[...]
            m_new = jnp.maximum(m_prev, jnp.max(s, axis=1, keepdims=True))
            alpha = jnp.exp(m_prev - m_new)
            p = jnp.exp(s - m_new)
            l_s[...] = alpha * l_s[...] + jnp.sum(p, axis=1, keepdims=True)
            acc_s[...] = alpha * acc_s[...] + jnp.dot(p.astype(BF16), v_ref[...].astype(BF16),
                                                      preferred_element_type=F32)
            m_s[...] = m_new

        @pl.when(j <= i)
        def _():
            step(j == i)

        @pl.when(j == i)
        def _():
            o_ref[...] = (acc_s[...] / l_s[...]).astype(o_ref.dtype)
            lse_ref[...] = _lanes(m_s[...] + jnp.log(l_s[...]))

    blk = (None, tq, LANES)
    in_specs = [pl.BlockSpec(blk, lambda b, h, i, j: (b, i, q0 + h)),
                pl.BlockSpec(blk, lambda b, h, i, j: (b, jnp.minimum(j, i), kv0 + 2 * h)),
                pl.BlockSpec(blk, lambda b, h, i, j: (b, jnp.minimum(j, i), kv0 + 2 * h + 1))]
    args = [qa, kva, kva]
    if gated:
        in_specs += [pl.BlockSpec(blk, lambda b, h, i, j: (b, i, 0)),
                     pl.BlockSpec((None, 8, tq), lambda b, h, i, j: (b, 0, jnp.minimum(j, i)))]
        args += list(gates)
    in_specs.append(pl.BlockSpec(memory_space=pl.ANY))
    args.append(mo)
    return pl.pallas_call(
        body, name=name, grid=(bsz, N_HEADS, n_q, n_q), in_specs=in_specs,
        out_specs=[pl.BlockSpec(blk, lambda b, h, i, j: (b, i, o0 + h)),
                   pl.BlockSpec((None, None, tq, LANES), lambda b, h, i, j: (b, h, i, 0))],
        out_shape=[jax.ShapeDtypeStruct(mo.shape, mo.dtype),
                   jax.ShapeDtypeStruct((bsz, N_HEADS, seq, LANES), F32)],
        scratch_shapes=[pltpu.VMEM((tq, 1), F32), pltpu.VMEM((tq, 1), F32), pltpu.VMEM((tq, LANES), F32)],
        input_output_aliases={len(args) - 1: 0},
        compiler_params=_cparams(("parallel", "parallel", "parallel", "arbitrary")),
    )(*args)


def _attn_bwd_q(qa, q0, kva, kv0, mo, dmo, o0, lse, gates, scale, out, out0, name, tq=None):
    bsz, seq, _ = qa.shape
    tq = ATTN_TILE if tq is None else tq
    n_q = seq // tq
    gated = gates is not None
    aliased = not isinstance(out, jax.ShapeDtypeStruct)

    def body(*refs):
        q_ref, k_ref, v_ref, o_ref, do_ref, lse_ref = refs[:6]
        gate_refs = refs[6:8] if gated else None
        dq_ref, delta_ref, dfq_ref, acc_s, dl_s, df_s = refs[-6:]
        h, i, j = pl.program_id(1), pl.program_id(2), pl.program_id(3)

        @pl.when(j == 0)
        def _():
            acc_s[...] = jnp.zeros_like(acc_s)
            df_s[...] = jnp.zeros_like(df_s)
            dl_s[...] = jnp.sum(do_ref[...] * o_ref[...].astype(F32), axis=1, keepdims=True)

        def step(masked):
            s, _ = _scores(q_ref, k_ref, gate_refs, scale, h, masked, tq, tq)
            p = jnp.exp(s - lse_ref[:, 0:1])
            dp = lax.dot_general(do_ref[...].astype(BF16), v_ref[...].astype(BF16), _DN["nt"],
                                 preferred_element_type=F32)
            ds = p * (dp - dl_s[...])
            acc_s[...] += jnp.dot(ds.astype(BF16), k_ref[...].astype(BF16), preferred_element_type=F32)
            df_s[...] += jnp.sum(ds, axis=1, keepdims=True)

        @pl.when(j <= i)
        def _():
            step(j == i)

        @pl.when(j == i)
        def _():
            dq_ref[...] = (acc_s[...] * scale).astype(dq_ref.dtype)
            delta_ref[...] = _lanes(dl_s[...])
            dfq_ref[...] = _lanes(df_s[...])

    blk = (None, tq, LANES)
    col = pl.BlockSpec((None, None, tq, LANES), lambda b, h, i, j: (b, h, i, 0))
    in_specs = [pl.BlockSpec(blk, lambda b, h, i, j: (b, i, q0 + h)),
                pl.BlockSpec(blk, lambda b, h, i, j: (b, jnp.minimum(j, i), kv0 + 2 * h)),
                pl.BlockSpec(blk, lambda b, h, i, j: (b, jnp.minimum(j, i), kv0 + 2 * h + 1)),
                pl.BlockSpec(blk, lambda b, h, i, j: (b, i, o0 + h)),
                pl.BlockSpec(blk, lambda b, h, i, j: (b, i, o0 + h)), col]
    args = [qa, kva, kva, mo, dmo, lse]
    if gated:
        in_specs += [pl.BlockSpec(blk, lambda b, h, i, j: (b, i, 0)),
                     pl.BlockSpec((None, 8, tq), lambda b, h, i, j: (b, 0, jnp.minimum(j, i)))]
        args += list(gates)
    aliases = {}
    if aliased:
        in_specs.append(pl.BlockSpec(memory_space=pl.ANY))
        args.append(out)
        aliases = {len(args) - 1: 0}
    vec = jax.ShapeDtypeStruct((bsz, N_HEADS, seq, LANES), F32)
    return pl.pallas_call(
        body, name=name, grid=(bsz, N_HEADS, n_q, n_q), in_specs=in_specs,
        out_specs=[pl.BlockSpec(blk, lambda b, h, i, j: (b, i, out0 + h)), col, col],
        out_shape=[jax.ShapeDtypeStruct(out.shape, out.dtype), vec, vec],
        scratch_shapes=[pltpu.VMEM((tq, LANES), F32), pltpu.VMEM((tq, 1), F32), pltpu.VMEM((tq, 1), F32)],
        input_output_aliases=aliases,
        compiler_params=_cparams(("parallel", "parallel", "parallel", "arbitrary")),
    )(*args)


def _attn_bwd_kv(qa, q0, kva, kv0, dmo, o0, lse, delta, gates, scale, out, out0, name, tq=None):
    bsz, seq, _ = qa.shape
    tq = ATTN_TILE if tq is None else tq
    n_q = seq // tq
    gated = gates is not None
    aliased = not isinstance(out, jax.ShapeDtypeStruct)

    def body(*refs):
        q_ref, k_ref, v_ref, do_ref, lse_ref, dl_ref = refs[:6]
        gate_refs = refs[6:8] if gated else None
        dkv_ref, dfk_ref, dk_s, dv_s, df_s = refs[-5:]
        h, j, i = pl.program_id(1), pl.program_id(2), pl.program_id(3)

        @pl.when(i == 0)
        def _():
            dk_s[...] = jnp.zeros_like(dk_s)
            dv_s[...] = jnp.zeros_like(dv_s)
            df_s[...] = jnp.zeros_like(df_s)

        def step(masked):
            s, q = _scores(q_ref, k_ref, gate_refs, scale, h, masked, tq, tq)
            p = jnp.exp(s - lse_ref[:, 0:1])
            do_b = do_ref[...].astype(BF16)
            dp = lax.dot_general(do_b, v_ref[...].astype(BF16), _DN["nt"], preferred_element_type=F32)
            ds = p * (dp - dl_ref[:, 0:1])
            dv_s[...] += lax.dot_general(p.astype(BF16), do_b, _DN["tn"], preferred_element_type=F32)
            dk_s[...] += lax.dot_general(ds.astype(BF16), q, _DN["tn"], preferred_element_type=F32)
            df_s[...] -= jnp.sum(ds, axis=0, keepdims=True)

        @pl.when(i > j)
        def _():
            step(False)

        @pl.when(i == j)
        def _():
            step(True)

        @pl.when(i == n_q - 1)
        def _():
            dkv_ref[:, 0:LANES] = dk_s[...].astype(dkv_ref.dtype)
            dkv_ref[:, LANES:2 * LANES] = dv_s[...].astype(dkv_ref.dtype)
            dfk_ref[...] = df_s[...]

    blk = (None, tq, LANES)
    col = pl.BlockSpec((None, None, tq, LANES), lambda b, h, j, i: (b, h, jnp.maximum(i, j), 0))
    in_specs = [pl.BlockSpec(blk, lambda b, h, j, i: (b, jnp.maximum(i, j), q0 + h)),
                pl.BlockSpec(blk, lambda b, h, j, i: (b, j, kv0 + 2 * h)),
                pl.BlockSpec(blk, lambda b, h, j, i: (b, j, kv0 + 2 * h + 1)),
                pl.BlockSpec(blk, lambda b, h, j, i: (b, jnp.maximum(i, j), o0 + h)), col, col]
    args = [qa, kva, kva, dmo, lse, delta]
    if gated:
        in_specs += [pl.BlockSpec(blk, lambda b, h, j, i: (b, jnp.maximum(i, j), 0)),
                     pl.BlockSpec((None, 8, tq), lambda b, h, j, i: (b, 0, j))]
        args += list(gates)
    aliases = {}
    if aliased:
        in_specs.append(pl.BlockSpec(memory_space=pl.ANY))
        args.append(out)
        aliases = {len(args) - 1: 0}
    return pl.pallas_call(
        body, name=name, grid=(bsz, N_HEADS, n_q, n_q), in_specs=in_specs,
        out_specs=[pl.BlockSpec((None, tq, 2 * LANES), lambda b, h, j, i: (b, j, out0 + h)),
                   pl.BlockSpec((None, None, 1, tq), lambda b, h, j, i: (b, h, 0, j))],
        out_shape=[jax.ShapeDtypeStruct(out.shape, out.dtype), jax.ShapeDtypeStruct((bsz, N_HEADS, 1, seq), F32)],
        scratch_shapes=[pltpu.VMEM((tq, LANES), F32), pltpu.VMEM((tq, LANES), F32), pltpu.VMEM((1, tq), F32)],
        input_output_aliases=aliases,
        compiler_params=_cparams(("parallel", "parallel", "parallel", "arbitrary")),
    )(*args)


def _block_logits(q, k_ref, gate, j, scale_unused, h, masked, tq):
    del scale_unused
    r = pl.multiple_of(j * tq, tq)
    s = lax.dot_general(q, k_ref[pl.ds(r, tq), :].astype(BF16), _DN["nt"], preferred_element_type=F32)
    if gate is not None:
        fcol, fr_ref = gate
        sub = lax.broadcasted_iota(jnp.int32, (8, tq), 0)
        frow = jnp.sum(jnp.where(sub == h, fr_ref[:, pl.ds(r, tq)], 0.0), axis=0, keepdims=True)
        s = s + (fcol - frow)
    if masked:
        r_i = lax.broadcasted_iota(jnp.int32, (tq, tq), 0)
        c_i = lax.broadcasted_iota(jnp.int32, (tq, tq), 1)
        s = jnp.where(c_i <= r_i, s, NEG)
    return s, r


def _gate_col(fc_ref, h, tq):
    lane = lax.broadcasted_iota(jnp.int32, (tq, LANES), 1)
    return jnp.sum(jnp.where(lane == h, fc_ref[...], 0.0), axis=1, keepdims=True)


def _attn_fwd_loop(qa, q0, kva, kv0, mo, o0, gates, scale, name, tq=None):
    bsz, seq, _ = qa.shape
    tq = ATTN_TILE if tq is None else tq
    n_q = seq // tq
    gated = gates is not None

    def body(*refs):
        q_ref, k_ref, v_ref = refs[:3]
        o_ref, lse_ref = refs[-2:]
        h, i = pl.program_id(1), pl.program_id(2)
        q = (q_ref[...].astype(F32) * scale).astype(BF16)
        gate = (_gate_col(refs[3], h, tq), refs[4]) if gated else None

        def step(j, carry, masked):
            m_prev, l_prev, acc = carry
            s, r = _block_logits(q, k_ref, gate, j, None, h, masked, tq)
            m_new = jnp.maximum(m_prev, jnp.max(s, axis=1, keepdims=True))
            alpha = jnp.exp(m_prev - m_new)
            p = jnp.exp(s - m_new)
            l_new = alpha * l_prev + jnp.sum(p, axis=1, keepdims=True)
            acc = alpha * acc + jnp.dot(p.astype(BF16), v_ref[pl.ds(r, tq), :].astype(BF16),
                                        preferred_element_type=F32)
            return m_new, l_new, acc

        init = (jnp.full((tq, 1), NEG, F32), jnp.zeros((tq, 1), F32), jnp.zeros((tq, LANES), F32))
        carry = lax.fori_loop(0, i, lambda j, c: step(j, c, False), init)
        m_f, l_f, acc = step(i, carry, True)
        o_ref[...] = (acc / l_f).astype(o_ref.dtype)
        lse_ref[...] = _lanes(m_f + jnp.log(l_f))

    blk = (None, tq, LANES)
    full = (None, seq, LANES)
    in_specs = [pl.BlockSpec(blk, lambda b, h, i: (b, i, q0 + h)),
                pl.BlockSpec(full, lambda b, h, i: (b, 0, kv0 + 2 * h)),
                pl.BlockSpec(full, lambda b, h, i: (b, 0, kv0 + 2 * h + 1))]
    args = [qa, kva, kva]
    if gated:
        in_specs += [pl.BlockSpec(blk, lambda b, h, i: (b, i, 0)),
                     pl.BlockSpec((None, 8, seq), lambda b, h, i: (b, 0, 0))]
        args += list(gates)
    in_specs.append(pl.BlockSpec(memory_space=pl.ANY))
    args.append(mo)
    return pl.pallas_call(
        body, name=name, grid=(bsz, N_HEADS, n_q), in_specs=in_specs,
        out_specs=[pl.BlockSpec(blk, lambda b, h, i: (b, i, o0 + h)),
                   pl.BlockSpec((None, None, tq, LANES), lambda b, h, i: (b, h, i, 0))],
        out_shape=[jax.ShapeDtypeStruct(mo.shape, mo.dtype),
                   jax.ShapeDtypeStruct((bsz, N_HEADS, seq, LANES), F32)],
        input_output_aliases={len(args) - 1: 0},
        compiler_params=_cparams(("parallel", "parallel", "parallel")),
    )(*args)


def _attn_bwd_q_loop(qa, q0, kva, kv0, mo, dmo, o0, lse, gates, scale, out, out0, name, tq=None):
    bsz, seq, _ = qa.shape
    tq = ATTN_TILE if tq is None else tq
    n_q = seq // tq
    gated = gates is not None
    aliased = not isinstance(out, jax.ShapeDtypeStruct)

    def body(*refs):
        q_ref, k_ref, v_ref, o_ref, do_ref, lse_ref = refs[:6]
        dq_ref, delta_ref, dfq_ref = refs[-3:]
        h, i = pl.program_id(1), pl.program_id(2)
        q = (q_ref[...].astype(F32) * scale).astype(BF16)
        gate = (_gate_col(refs[6], h, tq), refs[7]) if gated else None
        do_v = do_ref[...]
        do_b = do_v.astype(BF16)
        delta = jnp.sum(do_v * o_ref[...].astype(F32), axis=1, keepdims=True)
        lse_v = lse_ref[:, 0:1]

        def step(j, carry, masked):
            acc, dfq = carry
            s, r = _block_logits(q, k_ref, gate, j, None, h, masked, tq)
            p = jnp.exp(s - lse_v)
            dp = lax.dot_general(do_b, v_ref[pl.ds(r, tq), :].astype(BF16), _DN["nt"], preferred_element_type=F32)
            ds = p * (dp - delta)
            acc = acc + jnp.dot(ds.astype(BF16), k_ref[pl.ds(r, tq), :].astype(BF16), preferred_element_type=F32)
            return acc, dfq + jnp.sum(ds, axis=1, keepdims=True)

        init = (jnp.zeros((tq, LANES), F32), jnp.zeros((tq, 1), F32))
        carry = lax.fori_loop(0, i, lambda j, c: step(j, c, False), init)
        acc, dfq = step(i, carry, True)
        dq_ref[...] = (acc * scale).astype(dq_ref.dtype)
        delta_ref[...] = _lanes(delta)
        dfq_ref[...] = _lanes(dfq)

    blk = (None, tq, LANES)
    full = (None, seq, LANES)
    stat = pl.BlockSpec((None, None, tq, LANES), lambda b, h, i: (b, h, i, 0))
    in_specs = [pl.BlockSpec(blk, lambda b, h, i: (b, i, q0 + h)),
                pl.BlockSpec(full, lambda b, h, i: (b, 0, kv0 + 2 * h)),
                pl.BlockSpec(full, lambda b, h, i: (b, 0, kv0 + 2 * h + 1)),
                pl.BlockSpec(blk, lambda b, h, i: (b, i, o0 + h)),
                pl.BlockSpec(blk, lambda b, h, i: (b, i, o0 + h)), stat]
    args = [qa, kva, kva, mo, dmo, lse]
    if gated:
        in_specs += [pl.BlockSpec(blk, lambda b, h, i: (b, i, 0)),
                     pl.BlockSpec((None, 8, seq), lambda b, h, i: (b, 0, 0))]
        args += list(gates)
    aliases = {}
    if aliased:
        in_specs.append(pl.BlockSpec(memory_space=pl.ANY))
        args.append(out)
        aliases = {len(args) - 1: 0}
    vec = jax.ShapeDtypeStruct((bsz, N_HEADS, seq, LANES), F32)
    return pl.pallas_call(
        body, name=name, grid=(bsz, N_HEADS, n_q), in_specs=in_specs,
        out_specs=[pl.BlockSpec(blk, lambda b, h, i: (b, i, out0 + h)), stat, stat],
        out_shape=[jax.ShapeDtypeStruct(out.shape, out.dtype), vec, vec],
        input_output_aliases=aliases,
        compiler_params=_cparams(("parallel", "parallel", "parallel")),
    )(*args)


def _attn_bwd_kv_loop(qa, q0, kva, kv0, dmo, o0, lse, delta, gates, scale, out, out0, name, tq=None):
    bsz, seq, _ = qa.shape
    tq = ATTN_TILE if tq is None else tq
    n_q = seq // tq
    gated = gates is not None
    aliased = not isinstance(out, jax.ShapeDtypeStruct)

    def body(*refs):
        q_ref, k_ref, v_ref, do_ref, lse_ref, dl_ref = refs[:6]
        dkv_ref, dfk_ref = refs[-2:]
        h, j = pl.program_id(1), pl.program_id(2)
        k_b = k_ref[...].astype(BF16)
        v_b = v_ref[...].astype(BF16)
        if gated:
            fc_ref, fr_ref = refs[6], refs[7]
            sub = lax.broadcasted_iota(jnp.int32, (8, tq), 0)
            frow = jnp.sum(jnp.where(sub == h, fr_ref[...], 0.0), axis=0, keepdims=True)
            lane = lax.broadcasted_iota(jnp.int32, (tq, LANES), 1)

        def step(i, carry, masked):
            dk, dv, dfk = carry
            r = pl.multiple_of(i * tq, tq)
            q = (q_ref[pl.ds(r, tq), :].astype(F32) * scale).astype(BF16)
            s = lax.dot_general(q, k_b, _DN["nt"], preferred_element_type=F32)
            if gated:
                fcol = jnp.sum(jnp.where(lane == h, fc_ref[pl.ds(r, tq), :], 0.0), axis=1, keepdims=True)
                s = s + (fcol - frow)
            if masked:
                r_i = lax.broadcasted_iota(jnp.int32, (tq, tq), 0)
                c_i = lax.broadcasted_iota(jnp.int32, (tq, tq), 1)
                s = jnp.where(c_i <= r_i, s, NEG)
            p = jnp.exp(s - lse_ref[pl.ds(r, tq), 0:1])
            do_b = do_ref[pl.ds(r, tq), :].astype(BF16)
            dp = lax.dot_general(do_b, v_b, _DN["nt"], preferred_element_type=F32)
            ds = p * (dp - dl_ref[pl.ds(r, tq), 0:1])
            dv = dv + lax.dot_general(p.astype(BF16), do_b, _DN["tn"], preferred_element_type=F32)
            dk = dk + lax.dot_general(ds.astype(BF16), q, _DN["tn"], preferred_element_type=F32)
            return dk, dv, dfk - jnp.sum(ds, axis=0, keepdims=True)

        init = (jnp.zeros((tq, LANES), F32), jnp.zeros((tq, LANES), F32), jnp.zeros((1, tq), F32))
        carry = step(j, init, True)
        dk, dv, dfk = lax.fori_loop(j + 1, n_q, lambda i, c: step(i, c, False), carry)
        dkv_ref[:, 0:LANES] = dk.astype(dkv_ref.dtype)
        dkv_ref[:, LANES:2 * LANES] = dv.astype(dkv_ref.dtype)
        dfk_ref[...] = dfk

    blk = (None, tq, LANES)
    full = (None, seq, LANES)
    stat = pl.BlockSpec((None, None, seq, LANES), lambda b, h, j: (b, h, 0, 0))
    in_specs = [pl.BlockSpec(full, lambda b, h, j: (b, 0, q0 + h)),
                pl.BlockSpec(blk, lambda b, h, j: (b, j, kv0 + 2 * h)),
                pl.BlockSpec(blk, lambda b, h, j: (b, j, kv0 + 2 * h + 1)),
                pl.BlockSpec(full, lambda b, h, j: (b, 0, o0 + h)), stat, stat]
    args = [qa, kva, kva, dmo, lse, delta]
    if gated:
        in_specs += [pl.BlockSpec(full, lambda b, h, j: (b, 0, 0)),
                     pl.BlockSpec((None, 8, tq), lambda b, h, j: (b, 0, j))]
        args += list(gates)
    aliases = {}
    if aliased:
        in_specs.append(pl.BlockSpec(memory_space=pl.ANY))
        args.append(out)
        aliases = {len(args) - 1: 0}
    return pl.pallas_call(
        body, name=name, grid=(bsz, N_HEADS, n_q), in_specs=in_specs,
        out_specs=[pl.BlockSpec((None, tq, 2 * LANES), lambda b, h, j: (b, j, out0 + h)),
                   pl.BlockSpec((None, None, 1, tq), lambda b, h, j: (b, h, 0, j))],
        out_shape=[jax.ShapeDtypeStruct(out.shape, out.dtype), jax.ShapeDtypeStruct((bsz, N_HEADS, 1, seq), F32)],
        input_output_aliases=aliases,
        compiler_params=_cparams(("parallel", "parallel", "parallel")),
    )(*args)


def _gmlp_fn(uv, lng, lnb, ws, bst):
    u = jax.nn.gelu(uv[:, 0:GROUP_WIDTH])
    gv = jax.nn.gelu(uv[:, GROUP_WIDTH:2 * GROUP_WIDTH])
    mu = jnp.mean(gv, axis=-1, keepdims=True)
    vc = gv - mu
    var = jnp.mean(vc * vc, axis=-1, keepdims=True)
    vln = vc * lax.rsqrt(var + LN_EPS) * lng + lnb
    r_i = lax.broadcasted_iota(jnp.int32, (D_CHUNK, D_CHUNK), 0)
    c_i = lax.broadcasted_iota(jnp.int32, (D_CHUNK, D_CHUNK), 1)
    lane_g = lax.broadcasted_iota(jnp.int32, (D_CHUNK, GROUP_WIDTH), 1) // HEAD_DIM
    e_r = lax.broadcasted_iota(jnp.int32, (LANES, GROUP_WIDTH), 0)
    e_c = lax.broadcasted_iota(jnp.int32, (LANES, GROUP_WIDTH), 1)
    expand = (e_r == e_c // HEAD_DIM).astype(F32)
    mixed = jnp.dot(bst, expand, precision=HI, preferred_element_type=F32)
    for g in range(4):
        w = jnp.where(r_i >= c_i, ws[g], 0.0)
        mixed = mixed + jnp.where(lane_g == g, _bdot(w, vln, "nn"), 0.0)
    return u * mixed


def _gmlp_fwd(proj, mo, lng, lnb, ws, bst, name):
    bsz, seq, _ = proj.shape

    def body(p_ref, mo_any, lng_ref, lnb_ref, ws_ref, bst_ref, o_ref):
        del mo_any
        o_ref[...] = _gmlp_fn(p_ref[...], lng_ref[...], lnb_ref[...], ws_ref[...], bst_ref[...]).astype(o_ref.dtype)

    return pl.pallas_call(
        body, name=name, grid=(bsz, seq // D_CHUNK),
        in_specs=[pl.BlockSpec((None, D_CHUNK, 512), lambda b, s: (b, s, P_D // 512)),
                  pl.BlockSpec(memory_space=pl.ANY), _vec_spec(256), _vec_spec(256),
                  pl.BlockSpec((4, D_CHUNK, D_CHUNK), lambda b, s: (0, 0, 0)),
                  pl.BlockSpec((D_CHUNK, LANES), lambda b, s: (0, 0))],
        out_specs=pl.BlockSpec((None, D_CHUNK, GROUP_WIDTH), lambda b, s: (b, s, 1280 // GROUP_WIDTH)),
        out_shape=jax.ShapeDtypeStruct(mo.shape, mo.dtype),
        input_output_aliases={1: 0},
        compiler_params=_cparams(("parallel", "parallel")),
    )(proj, mo, lng, lnb, ws, bst)


def _gmlp_bwd(dmo, dproj, proj, lng, lnb, ws, bst, name):
    bsz, seq, _ = proj.shape

    def body(do_ref, dp_any, p_ref, lng_ref, lnb_ref, ws_ref, bst_ref, dp_ref, dlg_ref, dlb_ref, dws_ref, dbst_ref):
        del dp_any
        first = jnp.logical_and(pl.program_id(0) == 0, pl.program_id(1) == 0)

        @pl.when(first)
        def _():
            dlg_ref[...] = jnp.zeros_like(dlg_ref)
            dlb_ref[...] = jnp.zeros_like(dlb_ref)
            dws_ref[...] = jnp.zeros_like(dws_ref)
            dbst_ref[...] = jnp.zeros_like(dbst_ref)

        _, vjp = jax.vjp(_gmlp_fn, p_ref[...], lng_ref[...], lnb_ref[...], ws_ref[...], bst_ref[...])
        duv, dlg, dlb, dws, dbst = vjp(do_ref[...])
        dp_ref[...] = duv.astype(dp_ref.dtype)
        dlg_ref[...] += dlg
        dlb_ref[...] += dlb
        dws_ref[...] += dws
        dbst_ref[...] += dbst

    const2 = lambda shape: pl.BlockSpec(shape, lambda b, s: (0,) * len(shape))
    return pl.pallas_call(
        body, name=name, grid=(bsz, seq // D_CHUNK),
        in_specs=[pl.BlockSpec((None, D_CHUNK, GROUP_WIDTH), lambda b, s: (b, s, 1280 // GROUP_WIDTH)),
                  pl.BlockSpec(memory_space=pl.ANY),
                  pl.BlockSpec((None, D_CHUNK, 512), lambda b, s: (b, s, P_D // 512)),
                  _vec_spec(256), _vec_spec(256), const2((4, D_CHUNK, D_CHUNK)), const2((D_CHUNK, LANES))],
        out_specs=[pl.BlockSpec((None, D_CHUNK, 512), lambda b, s: (b, s, P_D // 512)),
                   _vec_spec(256), _vec_spec(256), const2((4, D_CHUNK, D_CHUNK)), const2((D_CHUNK, LANES))],
        out_shape=[jax.ShapeDtypeStruct(dproj.shape, dproj.dtype), jax.ShapeDtypeStruct((1, 256), F32),
                   jax.ShapeDtypeStruct((1, 256), F32), jax.ShapeDtypeStruct((4, D_CHUNK, D_CHUNK), F32),
                   jax.ShapeDtypeStruct((D_CHUNK, LANES), F32)],
        input_output_aliases={1: 0},
        compiler_params=_cparams(("arbitrary", "arbitrary")),
    )(dmo, dproj, proj, lng, lnb, ws, bst)


def _ada_fwd(c_all, ada_w, name):
    n_b = c_all.shape[0]
    depth, d, cols = ada_w.shape

    def body(c_ref, w_ref, o_ref):
        cv = c_ref[...]
        act = (cv * jax.nn.sigmoid(cv)).astype(BF16)
        o_ref[...] = jnp.dot(act, w_ref[...].astype(BF16), preferred_element_type=F32)

    return pl.pallas_call(
        body, name=name, grid=(depth,),
        in_specs=[pl.BlockSpec((n_b, d), lambda l: (0, 0)), pl.BlockSpec((None, d, cols), lambda l: (l, 0, 0))],
        out_specs=pl.BlockSpec((None, n_b, cols), lambda l: (l, 0, 0)),
        out_shape=jax.ShapeDtypeStruct((depth, n_b, cols), F32),
        compiler_params=_cparams(("parallel",)),
    )(c_all, ada_w)


def _ada_bwd(c_all, dmod_cols, dmod_full, name):
    n_b, d = c_all.shape
    depth, _, cols = dmod_cols.shape
    full = dmod_full.shape[-1]

    def body(c_ref, dm_ref, df_ref, gw_ref, gb_ref):
        cv = c_ref[...]
        act = (cv * jax.nn.sigmoid(cv)).astype(BF16)
        gw_ref[...] = lax.dot_general(act, dm_ref[...].astype(BF16), (((0,), (0,)), ((), ())),
                                      preferred_element_type=F32)
        gb_ref[...] = jnp.sum(df_ref[...], axis=0, keepdims=True)

    return pl.pallas_call(
        body, name=name, grid=(depth,),
        in_specs=[pl.BlockSpec((n_b, d), lambda l: (0, 0)), pl.BlockSpec((None, n_b, cols), lambda l: (l, 0, 0)),
                  pl.BlockSpec((None, n_b, full), lambda l: (l, 0, 0))],
        out_specs=[pl.BlockSpec((None, d, cols), lambda l: (l, 0, 0)),
                   pl.BlockSpec((None, 1, full), lambda l: (l, 0, 0))],
        out_shape=[jax.ShapeDtypeStruct((depth, d, cols), F32), jax.ShapeDtypeStruct((depth, 1, full), F32)],
        compiler_params=_cparams(("parallel",)),
    )(c_all, dmod_cols, dmod_full)


def _adamw(gparts, own, w, m, v, name, layer=0, prev=None):
    n_p, rows, cols = gparts.shape
    assert w.shape[1:] == (rows, cols)
    tr = rows
    if rows > 512:
        tr = next(c for c in range(512, 7, -8) if rows % c == 0)
    has_own = own is not None
    n_prev = 0 if prev is None else 4

    def body(*refs):
        if has_own:
            slot_ref, refs = refs[0], refs[1:]
        g_ref = refs[0]
        own_ref = refs[1] if has_own else None
        w_ref, m_ref, v_ref = refs[1 + has_own:4 + has_own]
        go_ref, do_ref, mo_ref, vo_ref = refs[4 + has_own + n_prev:]
        g = None
        for p in range(n_p):
            term = g_ref[p].astype(F32)
            if has_own:
                term = jnp.where(slot_ref[0] == p, own_ref[...].astype(F32), term)
            g = term if g is None else g + term
        m_new = ADAM_B1 * m_ref[...] + (1.0 - ADAM_B1) * g
        v_new = ADAM_B2 * v_ref[...] + (1.0 - ADAM_B2) * (g * g)
        m_hat = m_new / (1.0 - ADAM_B1 ** ADAM_STEP)
        v_hat = v_new / (1.0 - ADAM_B2 ** ADAM_STEP)
        go_ref[...] = g
        do_ref[...] = -ADAM_LR * (m_hat / (jnp.sqrt(v_hat) + ADAM_EPS) + ADAM_WD * w_ref[...])
        mo_ref[...] = m_new
        vo_ref[...] = v_new

    spec = pl.BlockSpec((None, tr, cols), lambda i, *_: (layer, i, 0))
    in_specs = [pl.BlockSpec((n_p, tr, cols), lambda i, *_: (0, i, 0))]
    args = [gparts]
    if has_own:
        in_specs.append(pl.BlockSpec((None, tr, cols), lambda i, slot: (slot[0], i, 0)))
        args.append(own[0])
    in_specs += [spec, spec, spec]
    args += [w, m, v]
    aliases = {}
    if prev is not None:
        aliases = {has_own + len(args) + k: k for k in range(4)}
        in_specs += [pl.BlockSpec(memory_space=pl.ANY)] * 4
        args += list(prev)
    shp = jax.ShapeDtypeStruct(w.shape, F32)
    out_specs, out_shape = [spec, spec, spec, spec], [shp, shp, shp, shp]
    if not has_own:
        return pl.pallas_call(
            body, name=name, grid=(rows // tr,), in_specs=in_specs, out_specs=out_specs, out_shape=out_shape,
            input_output_aliases=aliases, compiler_params=_cparams(("parallel",)),
        )(*args)
    return pl.pallas_call(
        body, name=name, out_shape=out_shape, input_output_aliases=aliases,
        grid_spec=pltpu.PrefetchScalarGridSpec(num_scalar_prefetch=1, grid=(rows // tr,), in_specs=in_specs,
                                               out_specs=out_specs),
        compiler_params=_cparams(("parallel",)),
    )(jnp.reshape(own[1], (1,)).astype(jnp.int32), *args)


def _sum_parts(parts, name):
    n_p, rows, cols = parts.shape
    tr = 256 if rows % 256 == 0 else rows

    def body(p_ref, o_ref):
        acc = p_ref[0]
        for p in range(1, n_p):
            acc = acc + p_ref[p]
        o_ref[...] = acc

    return pl.pallas_call(
        body, name=name, grid=(rows // tr,),
        in_specs=[pl.BlockSpec((n_p, tr, cols), lambda i: (0, i, 0))],
        out_specs=pl.BlockSpec((tr, cols), lambda i: (i, 0)),
        out_shape=jax.ShapeDtypeStruct((rows, cols), F32),
        compiler_params=_cparams(("parallel",)),
    )(parts)


def _all_gather(arrs, name):
    n = len(arrs)

    def body(*refs):
        in_refs, out_refs = refs[:n], refs[n:2 * n]
        send_sems, recv_sems, loc_sems = refs[2 * n:]
        x, y, c = lax.axis_index("x"), lax.axis_index("y"), lax.axis_index("c")
        me, sibling = (x, y, c), (x, y, 1 - c)
        chips = [(1 - x, y), (x, 1 - y), (1 - x, 1 - y)]

        def copy(a, k, block, to, src=None):
            slot = out_refs[a].at[4 * block[0] + 2 * block[1] + block[2]]
            return pltpu.make_async_remote_copy(
                src_ref=slot if src is None else src, dst_ref=slot, send_sem=send_sems.at[a, k],
                recv_sem=recv_sems.at[a, k], device_id=to, device_id_type=pl.DeviceIdType.MESH)

        mine = [pltpu.make_async_copy(in_refs[a], out_refs[a].at[4 * x + 2 * y + c], loc_sems.at[a])
                for a in range(n)]
        for cp in mine:
            cp.start()
        first = []
        for a in range(n):
            first.append(copy(a, 0, me, sibling, src=in_refs[a]))
            first += [copy(a, 1 + j, me, (*chip, c), src=in_refs[a]) for j, chip in enumerate(chips)]
        for cp in first:
            cp.start()
        passed = []
        for j, chip in enumerate(chips):
            for a in range(n):
                copy(a, 1 + j, (*chip, c), me).wait_recv()
                cp = copy(a, 4 + j, (*chip, c), sibling)
                cp.start()
                passed.append(cp)
        for a in range(n):
            copy(a, 0, sibling, me).wait_recv()
        for j, chip in enumerate(chips):
            for a in range(n):
                copy(a, 4 + j, (*chip, 1 - c), me).wait_recv()
        for cp in first + passed:
            cp.wait_send()
        for cp in mine:
            cp.wait()

    any_spec = pl.BlockSpec(memory_space=pl.ANY)
    return pl.pallas_call(
        body, name=name, in_specs=[any_spec] * n, out_specs=[any_spec] * n,
        out_shape=[jax.ShapeDtypeStruct((N_DEV,) + a.shape, a.dtype) for a in arrs],
        scratch_shapes=[pltpu.SemaphoreType.DMA((n, N_DEV - 1)), pltpu.SemaphoreType.DMA((n, N_DEV - 1)),
                        pltpu.SemaphoreType.DMA((n,))],
    )(*arrs)


def _flip_peers():
    x, y, c = lax.axis_index("x"), lax.axis_index("y"), lax.axis_index("c")
    peers = []
    for fx, fy, fc in [(fx, fy, fc) for fx in (0, 1) for fy in (0, 1) for fc in (0, 1)][1:]:
        px, py, pc = (1 - x if fx else x), (1 - y if fy else y), (1 - c if fc else c)
        peers.append(((px, py, pc), 4 * px + 2 * py + pc))
    return 4 * x + 2 * y + c, peers


def _push_start(srcs, name, whole=False):
    n, n_peer = len(srcs), N_DEV - 1
    if whole:
        me_w = 4 * lax.axis_index("x") + 2 * lax.axis_index("y") + lax.axis_index("c")
        lands = [lax.dynamic_update_slice_in_dim(lax.empty((N_DEV,) + a.shape, a.dtype), a[None], me_w, axis=0)
                 for a in srcs]
    else:
        lands = [lax.empty(a.shape, a.dtype) for a in srcs]

    def body(*refs):
        src_refs, land_refs = refs[:n], refs[n:2 * n]
        send_sems, recv_sems = refs[2 * n], refs[2 * n + 1]
        token = refs[-1]
        me, peers = _flip_peers()
        for k, (dev, idx) in enumerate(peers):
            for a in range(n):
                pltpu.make_async_remote_copy(
                    src_ref=src_refs[a] if whole else src_refs[a].at[idx], dst_ref=land_refs[a].at[me],
                    send_sem=send_sems.at[a * n_peer + k], recv_sem=recv_sems.at[a * n_peer + k], device_id=dev,
                    device_id_type=pl.DeviceIdType.MESH).start()
        token[...] = jnp.zeros_like(token)

    hbm = pl.BlockSpec(memory_space=pltpu.HBM)
    sem = pl.BlockSpec(memory_space=pltpu.SEMAPHORE)
    arrs = list(srcs) + lands
    res = pl.pallas_call(
        body, name=name, in_specs=[hbm] * (2 * n),
        out_specs=(sem, sem, *[hbm] * (2 * n), pl.BlockSpec(memory_space=pltpu.VMEM)),
        out_shape=(pltpu.SemaphoreType.DMA((n * n_peer,)), pltpu.SemaphoreType.DMA((n * n_peer,)),
                   *[pltpu.HBM(a.shape, a.dtype) for a in arrs], jax.ShapeDtypeStruct((8, LANES), F32)),
        input_output_aliases={i: 2 + i for i in range(2 * n)},
        compiler_params=pltpu.CompilerParams(has_side_effects=pltpu.SideEffectType.DATAFLOW_SIDE_EFFECTING),
    )(*[pltpu.with_memory_space_constraint(a, pltpu.HBM) for a in arrs])
    return res[0], res[1], list(res[2:2 + n]), list(res[2 + n:2 + 2 * n]), res[-1]


def _push_wait(send_sems, recv_sems, srcs, lands, after, name, whole=False):
    n, n_peer = len(srcs), N_DEV - 1

    def body(*refs):
        src_refs, land_refs = refs[:n], refs[n:2 * n]
        send_s, recv_s = refs[2 * n], refs[2 * n + 1]
        _, peers = _flip_peers()
        for k, (dev, idx) in enumerate(peers):
            for a in range(n):
                cp = pltpu.make_async_remote_copy(
                    src_ref=src_refs[a] if whole else src_refs[a].at[idx], dst_ref=land_refs[a].at[idx],
                    send_sem=send_s.at[a * n_peer + k],
                    recv_sem=recv_s.at[a * n_peer + k], device_id=dev, device_id_type=pl.DeviceIdType.MESH)
                cp.wait_send()
                cp.wait_recv()

    hbm = pl.BlockSpec(memory_space=pltpu.HBM)
    sem = pl.BlockSpec(memory_space=pltpu.SEMAPHORE)
    arrs = list(srcs) + list(lands)
    res = pl.pallas_call(
        body, name=name, in_specs=[hbm] * (2 * n) + [sem, sem, pl.BlockSpec(memory_space=pl.ANY)],
        out_specs=tuple([hbm] * (2 * n)), out_shape=tuple(pltpu.HBM(a.shape, a.dtype) for a in arrs),
        input_output_aliases={i: i for i in range(2 * n)},
        compiler_params=pltpu.CompilerParams(has_side_effects=pltpu.SideEffectType.DATAFLOW_SIDE_EFFECTING),
    )(*arrs, send_sems, recv_sems, after)
    return list(res[:n]), list(res[n:])


def _ffn_fwd(x, h, mod, w_in, w_out_after, lng, lnb, rows, tag, nxt):
    bsz, seq, d = x.shape
    t = bsz * seq
    if h is None:
        h = _modulate(x, mod, rows[0], rows[1], f"modulate_{tag}")
    z, a = _ffn_in_swiglu(h.reshape(t, d), w_in, f"ffn_in_{tag}")
    f = _matmul_groupsum(a, w_out_after(a), out_dtype=F32, tm=512, name=f"ffn_out_{tag}").reshape(bsz, seq, d)
    y, h_next = _res_ln(x, f, mod, lng, lnb, rows[2], 0.5, f"res_ln_{tag}", nxt)
    return y, h_next, (x, h, z, a, f)


def _tied(mod, tie):
    return mod if tie is None else mod + tie


def _ffn_bwd(dy, saved, mod, w_in, w_out, lng, lnb, rows, tag, ready):
    x, h, z, a, f = saved
    bsz, seq, d = x.shape
    t = bsz * seq
    dx_res, df, dgate, dlg, dlb = _res_ln_bwd(dy, x, f, mod, lng, lnb, rows[2], 0.5, f"res_ln_bwd_{tag}")
    df2 = df.reshape(1, t, d)
    dw_out = _matmul(a, df2, mode="tn", group_out=True, out_dtype=BF16, tm=a.shape[2], tk=min(t, 2048),
                     name=f"ffn_out_dw_{tag}")
    tie_out = ready(f"{tag}_out", dw_out)
    dz = _ffn_out_dx_swiglu(df.reshape(t, d), w_out, z, f"ffn_out_dx_{tag}").reshape(N_DEV, t, -1)
    dw_in = _matmul(dz, h.reshape(1, t, d), mode="tn", group_out=True, out_dtype=BF16, tm=dz.shape[2],
                    tk=min(t, 2048), name=f"ffn_in_dw_{tag}")
    tie_in = ready(f"{tag}_in", dw_in)
    dh = _matmul_groupsum(dz, w_in, out_dtype=F32, tm=512, name=f"ffn_in_dx_{tag}").reshape(bsz, seq, d)
    dx, dsh, dsc = _modulate_bwd(dh, x, _tied(_tied(mod, tie_out), tie_in), dx_res, rows[1],
                                 f"modulate_bwd_{tag}")
    return dx, (dsh, dsc, dgate), dw_in, dw_out, dlg, dlb


def _mixer_fwd(x, h, mod, wts, small, lng, lnb, layer, tabs):
    bsz, seq, d = x.shape
    t = bsz * seq
    proj = _matmul(h.reshape(1, t, d), wts["mix_in"][None], mode="nn", group_out=True, out_dtype=F32, tm=512, tk=d,
                   name="mix_in").reshape(bsz, seq, PACK_W)
    mo, states = _hgrn_fwd(proj, small["lb_logits8"], small["hgrn_norm_g"], layer, f"hgrn_fwd_l{layer}")
    q, kv = _mla_pre(proj, small["q_norm_g"], small["kv_norm_g"], wts["uq"], wts["ukv"], tabs, "mla_pre")
    mla_scale = float((B_NOPE + B_ROPE) ** -0.5)
    mo, lse_b = _attn_fwd_loop(q, 0, kv, 0, mo, 2, None, mla_scale, "mla_attn_fwd")
    fg = _fox_gate(proj, small["fox_b_f"], "fox_gate")
    gates = (fg, jnp.swapaxes(fg[:, :, 0:8], 1, 2))
    fox_scale = float(HEAD_DIM ** -0.5)
    mo, lse_c = _attn_fwd_loop(proj, P_CQ // LANES, proj, P_CKV // LANES, mo, 6, gates, fox_scale, "fox_attn_fwd")
    mo = _gmlp_fwd(proj, mo, small["gmlp_ln_g"], small["gmlp_ln_b"], small["gmlp_w_s"], small["gmlp_bst"],
                   "gmlp_fwd")
    mixed = _matmul(mo.reshape(1, t, MO_W), wts["mix_out"][None], mode="nn", group_out=True, out_dtype=F32,
                    tm=1024, tk=MO_W, name="mix_out").reshape(bsz, seq, d)
    y, h_next = _res_ln(x, mixed, mod, lng, lnb, 5, 1.0, "res_ln_mix", (mod, 6, 7))
    return y, h_next, (x, h, proj, mo, states, q, kv, lse_b, gates, lse_c, mixed)


def _mixer_bwd(dy, saved, mod, wts, small, lng, lnb, layer, tabs, ready):
    x, h, proj, mo, states, q, kv, lse_b, gates, lse_c, mixed = saved
    bsz, seq, d = x.shape
    t = bsz * seq
    dx_res, dmixed, dgate, dlg, dlb = _res_ln_bwd(dy, x, mixed, mod, lng, lnb, 5, 1.0, "res_ln_bwd_mix")
    dm2 = dmixed.reshape(1, t, d)
    dmo = _matmul(dm2, wts["mix_out"][None], mode="nt", group_out=True, out_dtype=F32, tm=1024, tk=d,
                  name="mix_out_dx").reshape(bsz, seq, MO_W)
    dw_out = _matmul(mo.reshape(1, t, MO_W), dm2, mode="tn", group_out=True, out_dtype=F32, tm=512, tk=min(t, 2048),
                     name="mix_out_dw")[0]
    tie_out = ready("mix_out", dw_out)
    g = {}
    dproj, g["lb_logits8"], g["hgrn_norm_g"] = _hgrn_bwd(dmo, proj, states, small["lb_logits8"],
                                                         small["hgrn_norm_g"], layer, f"hgrn_bwd_l{layer}")
    mla_scale = float((B_NOPE + B_ROPE) ** -0.5)
    dq, delta_b, _ = _attn_bwd_q_loop(q, 0, kv, 0, mo, dmo, 2, lse_b, None, mla_scale,
                                 jax.ShapeDtypeStruct((bsz, seq, 512), F32), 0, "mla_attn_bwd_q")
    dkv, _ = _attn_bwd_kv_loop(q, 0, kv, 0, dmo, 2, lse_b, delta_b, None, mla_scale,
                          jax.ShapeDtypeStruct((bsz, seq, 1024), F32), 0, "mla_attn_bwd_kv")
    dproj, g["q_norm_g"], g["kv_norm_g"], g["uq"], g["ukv"] = _mla_pre_bwd(
        dq, dkv, dproj, proj, small["q_norm_g"], small["kv_norm_g"], wts["uq"], wts["ukv"], tabs, "mla_pre_bwd")
    fox_scale = float(HEAD_DIM ** -0.5)
    dproj, delta_c, dfq = _attn_bwd_q_loop(proj, P_CQ // LANES, proj, P_CKV // LANES, mo, dmo, 6, lse_c, gates,
                                      fox_scale, dproj, P_CQ // LANES, "fox_attn_bwd_q")
    dproj, dfk = _attn_bwd_kv_loop(proj, P_CQ // LANES, proj, P_CKV // LANES, dmo, 6, lse_c, delta_c, gates, fox_scale,
                              dproj, P_CKV // (2 * LANES), "fox_attn_bwd_kv")
    dfk_cols = jnp.pad(jnp.swapaxes(dfk[:, :, 0, :], 1, 2), ((0, 0), (0, 0), (0, LANES - N_HEADS)))
    dproj, g["fox_b_f"] = _fox_gate_bwd(dfq, dfk_cols, dproj, proj, small["fox_b_f"], "fox_gate_bwd")
    dproj, g["gmlp_ln_g"], g["gmlp_ln_b"], g["gmlp_w_s"], g["gmlp_bst"] = _gmlp_bwd(
        dmo, dproj, proj, small["gmlp_ln_g"], small["gmlp_ln_b"], small["gmlp_w_s"], small["gmlp_bst"], "gmlp_bwd")
    dp2 = dproj.reshape(1, t, PACK_W)
    dw_in = _matmul(h.reshape(1, t, d), dp2, mode="tn", group_out=True, out_dtype=BF16, tm=512, tk=1024,
                    name="mix_in_dw")[0]
    tie_in = ready("mix_in", dw_in)
    dh = _matmul(dp2, wts["mix_in"][None], mode="nt", group_out=True, out_dtype=F32, tm=512, tk=PACK_W,
                 name="mix_in_dx").reshape(bsz, seq, d)
    dx, dsh, dsc = _modulate_bwd(dh, x, _tied(_tied(mod, tie_out), tie_in), dx_res, 4, "modulate_bwd_mix")
    return dx, (dsh, dsc, dgate), dw_in, dw_out, g, dlg, dlb


def _small_views(p, layer):
    return {
        "lb_logits8": jnp.pad(p["hgrn_lb_logits"], ((0, 8 - DEPTH), (0, 0))),
        "hgrn_norm_g": p["hgrn_norm_g"][layer][None],
        "q_norm_g": p["mla_q_norm_g"][layer][None],
        "kv_norm_g": p["mla_kv_norm_g"][layer][None],
        "fox_b_f": jnp.pad(p["fox_b_f"][layer][None], ((0, 0), (0, LANES - N_HEADS))),
        "gmlp_ln_g": p["gmlp_ln_g"][layer][None],
        "gmlp_ln_b": p["gmlp_ln_b"][layer][None],
        "gmlp_w_s": p["gmlp_w_s"][layer],
        "gmlp_bst": jnp.pad(p["gmlp_b_s"][layer].T, ((0, 0), (0, LANES - N_HEADS))),
    }


def _local_step(x, mod, target, weights, p, grads_ready=None):
    bsz, seq, d = x.shape
    tabs = _rope_tables(seq)
    saved = []
    h = None
    for l in range(DEPTH):
        sm = _small_views(p, l)
        lng, lnb = p["ln_g"][l], p["ln_b"][l]
        x, h, s1 = _ffn_fwd(x, h, mod[l], weights(l, "ffn1_in", x)["ffn1_in"],
                            lambda a, l=l: weights(l, "ffn1_out", a)["ffn1_out"], lng[0:1], lnb[0:1], (0, 1, 2),
                            "ffn1", (mod[l], 3, 4))
        x, h, s2 = _mixer_fwd(x, h, mod[l], weights(l, "mix", x), sm, lng[1:2], lnb[1:2], l, tabs)
        x, h, s3 = _ffn_fwd(x, h, mod[l], weights(l, "ffn2_in", x)["ffn2_in"],
                            lambda a, l=l: weights(l, "ffn2_out", a)["ffn2_out"], lng[2:3], lnb[2:3], (6, 7, 8),
                            "ffn2", (mod[l + 1], 0, 1) if l + 1 < DEPTH else None)
        saved.append((s1, s2, s3))
    dx, loss = _loss_head(x, target, "loss_head")
    big, small, dmods = [None] * DEPTH, [None] * DEPTH, [None] * DEPTH
    ties = []

    def tied(a):
        for t in ties:
            a = a + t
        return a

    for l in reversed(range(DEPTH)):
        w = {}
        for part in ("ffn1_in", "ffn1_out", "mix", "ffn2_in", "ffn2_out"):
            w.update(weights(l, part, None))
        sm = _small_views(p, l)
        lng, lnb = p["ln_g"][l], p["ln_b"][l]
        s1, s2, s3 = saved[l]

        def ready(name, grad, l=l):
            tie = None if grads_ready is None else grads_ready(l, name, grad)
            if tie is not None:
                ties.append(tie)
            return tie

        dx, dm3, dwi2, dwo2, dlg2, dlb2 = _ffn_bwd(dx, s3, tied(mod[l]), w["ffn2_in"], w["ffn2_out"], lng[2:3],
                                                   lnb[2:3], (6, 7, 8), "ffn2", ready)
        dx, dm2, dwmi, dwmo, g, dlg1, dlb1 = _mixer_bwd(dx, s2, tied(mod[l]), w, sm, lng[1:2], lnb[1:2], l, tabs,
                                                        ready)
        dx, dm1, dwi1, dwo1, dlg0, dlb0 = _ffn_bwd(dx, s1, tied(mod[l]), w["ffn1_in"], w["ffn1_out"], lng[0:1],
                                                   lnb[0:1], (0, 1, 2), "ffn1", ready)
        dmods[l] = jnp.concatenate(list(dm1) + list(dm2) + list(dm3), axis=1)
        big[l] = {"ffn1_in": dwi1, "ffn1_out": dwo1, "ffn2_in": dwi2, "ffn2_out": dwo2, "mix_in": dwmi,
                  "mix_out": dwmo}
        g["ln_g"] = jnp.concatenate([dlg0, dlg1, dlg2], axis=0)
        g["ln_b"] = jnp.concatenate([dlb0, dlb1, dlb2], axis=0)
        small[l] = g
    return loss, dx, jnp.stack(dmods), big, small


_BIG = ("ffn1_in", "ffn1_out", "ffn2_in", "ffn2_out", "mix_in", "mix_out")


def _small_grad_list(small, loss):
    def both(fn):
        return jnp.stack([fn(small[l]) for l in range(DEPTH)])

    uq_src, ukv_src = _uq_src(), _ukv_src()
    return [
        ("loss", loss.reshape(1)),
        ("ln_g", both(lambda g: g["ln_g"])), ("ln_b", both(lambda g: g["ln_b"])),
        ("hgrn_lb_logits", small[0]["lb_logits8"][:DEPTH] + small[1]["lb_logits8"][:DEPTH]),
        ("hgrn_norm_g", both(lambda g: g["hgrn_norm_g"][0])),
        ("mla_q_norm_g", both(lambda g: g["q_norm_g"][0])),
        ("mla_kv_norm_g", both(lambda g: g["kv_norm_g"][0])),
        ("mla_w_uq", both(lambda g: _unpack_cols(g["uq"], uq_src, 384))),
        ("mla_w_ukv", both(lambda g: _unpack_cols(g["ukv"], ukv_src, 512))),
        ("fox_b_f", both(lambda g: g["fox_b_f"][0, :N_HEADS])),
        ("gmlp_ln_g", both(lambda g: g["gmlp_ln_g"][0])), ("gmlp_ln_b", both(lambda g: g["gmlp_ln_b"][0])),
        ("gmlp_w_s", both(lambda g: g["gmlp_w_s"])),
        ("gmlp_b_s", both(lambda g: g["gmlp_bst"][:, :N_HEADS].T)),
    ]


_PACK_COLS = 512


def _pack_small(items):
    flat = jnp.concatenate([a.reshape(-1).astype(F32) for _, a in items])
    n = flat.shape[0]
    tile = 8 * _PACK_COLS
    flat = jnp.pad(flat, (0, (-n) % tile))
    return flat.reshape(-1, _PACK_COLS)


def _unpack_small(buf, items):
    flat = buf.reshape(-1)
    out, off = {}, 0
    for name, a in items:
        out[name] = flat[off:off + a.size].reshape(a.shape)
        off += a.size
    return out


def _as2d(a):
    return a.reshape(-1, a.shape[-1])


def kernel(x, c, ada_w, ada_b, ln_g, ln_b, ffn1_w_in, ffn1_w_out, ffn2_w_in, ffn2_w_out, mix_w_in, mix_w_out, hgrn_lb_logits, hgrn_norm_g, mla_q_norm_g, mla_kv_norm_g, mla_w_uq, mla_w_ukv, fox_b_f, gmlp_ln_g, gmlp_ln_b, gmlp_w_s, gmlp_b_s, loss_target, m_ada_w, m_ada_b, m_ln_g, m_ln_b, m_ffn1_w_in, m_ffn1_w_out, m_ffn2_w_in, m_ffn2_w_out, m_mix_w_in, m_mix_w_out, m_hgrn_lb_logits, m_hgrn_norm_g, m_mla_q_norm_g, m_mla_kv_norm_g, m_mla_w_uq, m_mla_w_ukv, m_fox_b_f, m_gmlp_ln_g, m_gmlp_ln_b, m_gmlp_w_s, m_gmlp_b_s, v_ada_w, v_ada_b, v_ln_g, v_ln_b, v_ffn1_w_in, v_ffn1_w_out, v_ffn2_w_in, v_ffn2_w_out, v_mix_w_in, v_mix_w_out, v_hgrn_lb_logits, v_hgrn_norm_g, v_mla_q_norm_g, v_mla_kv_norm_g, v_mla_w_uq, v_mla_w_ukv, v_fox_b_f, v_gmlp_ln_g, v_gmlp_ln_b, v_gmlp_w_s, v_gmlp_b_s):
    names = ["ada_w", "ada_b", "ln_g", "ln_b", "ffn1_w_in", "ffn1_w_out", "ffn2_w_in", "ffn2_w_out", "mix_w_in",
             "mix_w_out", "hgrn_lb_logits", "hgrn_norm_g", "mla_q_norm_g", "mla_kv_norm_g", "mla_w_uq", "mla_w_ukv",
             "fox_b_f", "gmlp_ln_g", "gmlp_ln_b", "gmlp_w_s", "gmlp_b_s"]
    w = dict(zip(names, [ada_w, ada_b, ln_g, ln_b, ffn1_w_in, ffn1_w_out, ffn2_w_in, ffn2_w_out, mix_w_in, mix_w_out,
                         hgrn_lb_logits, hgrn_norm_g, mla_q_norm_g, mla_kv_norm_g, mla_w_uq, mla_w_ukv, fox_b_f,
                         gmlp_ln_g, gmlp_ln_b, gmlp_w_s, gmlp_b_s]))
    m = dict(zip(names, [m_ada_w, m_ada_b, m_ln_g, m_ln_b, m_ffn1_w_in, m_ffn1_w_out, m_ffn2_w_in, m_ffn2_w_out,
                         m_mix_w_in, m_mix_w_out, m_hgrn_lb_logits, m_hgrn_norm_g, m_mla_q_norm_g, m_mla_kv_norm_g,
                         m_mla_w_uq, m_mla_w_ukv, m_fox_b_f, m_gmlp_ln_g, m_gmlp_ln_b, m_gmlp_w_s, m_gmlp_b_s]))
    v = dict(zip(names, [v_ada_w, v_ada_b, v_ln_g, v_ln_b, v_ffn1_w_in, v_ffn1_w_out, v_ffn2_w_in, v_ffn2_w_out,
                         v_mix_w_in, v_mix_w_out, v_hgrn_lb_logits, v_hgrn_norm_g, v_mla_q_norm_g, v_mla_kv_norm_g,
                         v_mla_w_uq, v_mla_w_ukv, v_fox_b_f, v_gmlp_ln_g, v_gmlp_ln_b, v_gmlp_w_s, v_gmlp_b_s]))
    bsz, seq, d = x.shape
    me = 4 * lax.axis_index("x") + 2 * lax.axis_index("y") + lax.axis_index("c")
    mix_src, uq_src, ukv_src, mo_src = _mix_in_src(), _uq_src(), _ukv_src(), _mo_src()

    part_names = {"ffn1_in": ["ffn1_w_in"], "ffn1_out": ["ffn1_w_out"],
                  "mix": ["mix_w_in", "mix_w_out", "mla_w_uq", "mla_w_ukv"],
                  "ffn2_in": ["ffn2_w_in"], "ffn2_out": ["ffn2_w_out"]}
    group_of = {(l, part): (l, part) for l in range(DEPTH) for part in part_names}
    in_flight = {}
    transposed = ("ffn1_w_in", "ffn2_w_in")

    def start_group(key, behind=None):
        members = [(l, part) for (l, part), g in group_of.items() if g == key]
        labels = [(l, n) for l, part in members for n in part_names[part]]
        shards = []
        for l, n in labels:
            a = w[n][l]
            if n == "mix_w_in":
                a = _pack_cols(a, mix_src)
            if n in transposed:
                a = jnp.swapaxes(w[n], 1, 2)[l]
            shards.append(a.astype(BF16))
        if behind is not None:
            shards, _ = lax.optimization_barrier((shards, behind))
        in_flight[key] = (labels, _push_start(shards, f"gather_start_{key[0]}_{key[1]}", whole=True))

    keys_in_order = list(dict.fromkeys(group_of.values()))
    start_group(keys_in_order[0])

    gathered = _all_gather([c, ln_g, ln_b], "gather_inputs")
    c_all = gathered[0].reshape(N_DEV * bsz, d)
    ln_g_full = jnp.moveaxis(gathered[1], 0, 2).reshape(DEPTH, 3, d)
    ln_b_full = jnp.moveaxis(gathered[2], 0, 2).reshape(DEPTH, 3, d)

    mod_cols = _ada_fwd(c_all, ada_w, "ada_fwd")
    mod_all, = _all_gather([mod_cols], "gather_mod")
    mod_mine = lax.dynamic_slice_in_dim(mod_all, me * bsz, bsz, axis=2)
    mod = jnp.moveaxis(mod_mine, 0, 2).reshape(DEPTH, bsz, N_MOD * d) + ada_b[:, None, :]
    for key in keys_in_order[1:]:
        start_group(key, behind=mod)
    tie = sum(h[-1][0, 0] for _, h in in_flight.values())
    mod = mod.reshape(DEPTH, bsz, N_MOD, d) + tie

    arrived, laid_out = {}, {}

    def weights(l, part, after):
        if (l, part) not in laid_out:
            laid_out[(l, part)] = lay_out(l, part, after)
        return laid_out[(l, part)]

    def lay_out(l, part, after):
        key = group_of[(l, part)]
        if key not in arrived:
            labels, (send_sems, recv_sems, srcs, lands, _) = in_flight[key]
            _, lands = _push_wait(send_sems, recv_sems, srcs, lands, after, f"gather_wait_{key[0]}_{key[1]}",
                                  whole=True)
            arrived[key] = dict(zip(labels, lands))
        gw = {n: arrived[key][(l, n)] for n in part_names[part]}
        if part.endswith("_in"):
            return {part: gw[part_names[part][0]]}
        if part.endswith("_out"):
            return {part: gw[part_names[part][0]].reshape(4, 704, d)}
        uq = jnp.moveaxis(gw["mla_w_uq"], 0, 1).reshape(256, 384)
        ukv = jnp.moveaxis(gw["mla_w_ukv"], 0, 1).reshape(128, 512)
        return {"mix_in": gw["mix_w_in"].reshape(d, PACK_W),
                "mix_out": _pack_cols(gw["mix_w_out"].reshape(d, d).T, mo_src).T,
                "uq": _pack_cols(uq, uq_src), "ukv": _pack_cols(ukv, ukv_src)}

    p = dict(w)
    p["ln_g"], p["ln_b"] = ln_g_full, ln_b_full
    def chunks(name, arr):
        if name in ("ffn1_in", "ffn2_in"):
            return arr
        if name in ("ffn1_out", "ffn2_out"):
            return arr.reshape(N_DEV, arr.shape[1] // 2, d)
        if name == "mix_in":
            return arr.reshape(N_DEV, d // N_DEV, PACK_W)
        return _unpack_cols(arr.T, mo_src, d).T.astype(BF16).reshape(N_DEV, d // N_DEV, d)

    pending, started = {}, []

    def grads_ready(l, name, grad):
        pending[(name, l)] = chunks(name, grad)
        flush = name == "ffn1_in" if l > 0 else name in ("ffn2_in", "mix_out", "mix_in", "ffn1_out", "ffn1_in")
        if not flush:
            return None
        keys = sorted(pending)
        handles = _push_start([pending[k] for k in keys], f"push_start_{len(started)}")
        pending.clear()
        started.append((keys, handles, l == 0 and name.startswith("ffn1")))
        return handles[-1][0, 0]

    loss, grad_x, dmod, big, small = _local_step(x, mod, loss_target, weights, p, grads_ready)
    del big

    recv, out = {}, {}

    def arrive(n, after):
        keys, (send_sems, recv_sems, srcs, lands, _), _ = started[n]
        srcs, lands = _push_wait(send_sems, recv_sems, srcs, lands, after, f"push_wait_{n}")
        for k, src, land in zip(keys, srcs, lands):
            recv[k] = (land, src)

    big_of = {"ffn1_w_in": "ffn1_in", "ffn1_w_out": "ffn1_out", "ffn2_w_in": "ffn2_in", "ffn2_w_out": "ffn2_out",
              "mix_w_in": "mix_in", "mix_w_out": "mix_out"}
    chain = {name: None for name in big_of}

    def big_update(key, l):
        name = next(nm for nm, k in big_of.items() if k == key)
        parts, src = recv[(key, l)]
        if key == "mix_in":
            parts = _unpack_cols(parts, mix_src, MIX_ORIG_W)
            src = _unpack_cols(src, mix_src, MIX_ORIG_W)
        view = (lambda a: jnp.swapaxes(a, 1, 2)) if name in transposed else (lambda a: a)
        chain[name] = _adamw(parts, (src, me), view(w[name]), view(m[name]), view(v[name]), f"adamw_{name}_l{l}",
                             layer=l, prev=chain[name])

    def update(name, grad):
        shape = w[name].shape
        as3 = lambda a: a.reshape(1, -1, shape[-1])
        res = _adamw(as3(grad), None, as3(w[name]), as3(m[name]), as3(v[name]), f"adamw_{name}")
        out[name] = tuple(r.reshape(shape) for r in res)

    for n, (keys, _, last) in enumerate(started):
        if not last:
            arrive(n, grad_x)
            for key, l in keys:
                big_update(key, l)

    dmod_flat = dmod.reshape(DEPTH, bsz, N_MOD * d)
    done = [r[0] for r in chain.values() if r is not None]
    if done:
        dmod_flat, _ = lax.optimization_barrier((dmod_flat, done))
    dmod_all, = _all_gather([dmod_flat], "gather_dmod")
    dmod_full = jnp.moveaxis(dmod_all, 0, 1).reshape(DEPTH, N_DEV * bsz, N_MOD * d)
    cols = ada_w.shape[2]
    dmod_cols = lax.dynamic_slice_in_dim(dmod_full, me * cols, cols, axis=2)
    g_ada_w, g_ada_b = _ada_bwd(c_all, dmod_cols, dmod_full, "ada_bwd")
    res = None
    for l in range(DEPTH):
        res = _adamw(g_ada_w[l][None], None, ada_w, m_ada_w, v_ada_w, f"adamw_ada_w_l{l}", layer=l, prev=res)
    out["ada_w"] = tuple(res)
    update("ada_b", g_ada_b.reshape(DEPTH, N_MOD * d))

    items = _small_grad_list(small, loss)
    parts, = _all_gather([_pack_small(items)], "gather_small")
    sg = _unpack_small(_sum_parts(parts, "sum_small"), items)
    for name in ("ln_g", "ln_b"):
        update(name, lax.dynamic_slice_in_dim(sg[name], me * (d // N_DEV), d // N_DEV, axis=2))
    for name, width in (("mla_w_uq", 48), ("mla_w_ukv", 64)):
        update(name, lax.dynamic_slice_in_dim(sg[name], me * width, width, axis=2))
    for name in ("hgrn_lb_logits", "hgrn_norm_g", "mla_q_norm_g", "mla_kv_norm_g", "fox_b_f", "gmlp_ln_g",
                 "gmlp_ln_b", "gmlp_w_s", "gmlp_b_s"):
        update(name, sg[name])

    for n, (keys, _, last) in enumerate(started):
        if last:
            arrive(n, out["gmlp_w_s"][0])
            for key, l in keys:
                big_update(key, l)
    for name in big_of:
        out[name] = tuple(jnp.swapaxes(r, 1, 2) if name in transposed else r for r in chain[name])

    return (sg["loss"][0], grad_x, *[out[n][0] for n in names], *[out[n][1] for n in names],
            *[out[n][2] for n in names], *[out[n][3] for n in names])
```

```python
import functools

import numpy as np
import jax
import jax.numpy as jnp
from jax import lax
from jax.experimental import pallas as pl
from jax.experimental.pallas import tpu as pltpu

F32 = jnp.float32
BF16 = jnp.bfloat16
HI = lax.Precision.HIGHEST

D_MODEL = 1024
DEPTH = 2
GROUP_WIDTH = 256
N_HEADS = 4
HEAD_DIM = 64
A_CHUNK = 16
LB_FLOOR = 1e-30
B_NOPE = 64
B_ROPE = 32
ROPE_THETA = 10000.0
D_CHUNK = 128
D_FF = 2816
N_MOD = 9
ALPHA = (2 * DEPTH) ** 0.25
LN_EPS = 1e-5
RMS_EPS = 1e-6
ADAM_LR = 0.001
ADAM_B1 = 0.9
ADAM_B2 = 0.999
ADAM_EPS = 1e-08
ADAM_WD = 0.01
ADAM_STEP = 10

N_DEV = 8
LANES = 128
PACK_W = 3712
MO_W = 1536
VMEM_LIMIT = 56 * 1024 * 1024
NEG = -1e30
ATTN_TILE = 512

MIX_ORIG_W = 2724
O_BCQ, O_BCKV, O_BKR, O_CQ, O_CK, O_CV, O_CF, O_DU, O_DV = 1024, 1280, 1408, 1440, 1696, 1952, 2208, 2212, 2468
P_B, P_KR, P_CQ, P_CKV, P_D, P_CF = 1024, 1408, 1536, 2048, 3072, 3584


_DN = {"nn": (((1,), (0,)), ((), ())), "nt": (((1,), (1,)), ((), ())), "tn": (((0,), (0,)), ((), ()))}


def _raw_bdot(a, b, mode):
    return lax.dot_general(a.astype(BF16), b.astype(BF16), _DN[mode], preferred_element_type=F32)


@functools.partial(jax.custom_vjp, nondiff_argnums=(2,))
def _bdot(a, b, mode):
    return _raw_bdot(a, b, mode)


def _bdot_fwd(a, b, mode):
    return _raw_bdot(a, b, mode), (a, b)


def _bdot_bwd(mode, res, g):
    a, b = res
    if mode == "nn":
        return _raw_bdot(g, b, "nt"), _raw_bdot(a, g, "tn")
    if mode == "nt":
        return _raw_bdot(g, b, "nn"), _raw_bdot(g, a, "tn")
    return _raw_bdot(b, g, "nt"), _raw_bdot(a, g, "nn")


_bdot.defvjp(_bdot_fwd, _bdot_bwd)


def _cparams(sem):
    return pltpu.CompilerParams(dimension_semantics=sem, vmem_limit_bytes=VMEM_LIMIT)


def _mix_in_src():
    src = -np.ones(PACK_W, np.int64)
    src[0:P_KR] = np.arange(0, O_BKR)
    src[P_KR + 64:P_KR + 80] = O_BKR + np.arange(16)
    src[P_KR + 96:P_KR + 112] = O_BKR + 16 + np.arange(16)
    for h in range(N_HEADS):
        src[P_CQ + 128 * h:P_CQ + 128 * h + 64] = O_CQ + 64 * h + np.arange(64)
        src[P_CKV + 256 * h:P_CKV + 256 * h + 64] = O_CK + 64 * h + np.arange(64)
        src[P_CKV + 256 * h + 128:P_CKV + 256 * h + 192] = O_CV + 64 * h + np.arange(64)
    src[P_D:P_D + 512] = O_DU + np.arange(512)
    src[P_CF:P_CF + 4] = O_CF + np.arange(4)
    return src


def _uq_src():
    src = -np.ones(512, np.int64)
    for h in range(N_HEADS):
        src[128 * h:128 * h + 64] = 96 * h + np.arange(64)
        src[128 * h + 64:128 * h + 80] = 96 * h + 64 + np.arange(16)
        src[128 * h + 96:128 * h + 112] = 96 * h + 80 + np.arange(16)
    return src


def _ukv_src():
    src = -np.ones(1024, np.int64)
    for h in range(N_HEADS):
        src[256 * h:256 * h + 64] = 128 * h + np.arange(64)
        src[256 * h + 128:256 * h + 192] = 128 * h + 64 + np.arange(64)
    return src


def _mo_src():
    src = -np.ones(MO_W, np.int64)
    src[0:256] = np.arange(256)
    for g in range(2):
        for h in range(N_HEADS):
            src[256 + 512 * g + 128 * h:256 + 512 * g + 128 * h + 64] = 256 + 256 * g + 64 * h + np.arange(64)
    src[1280:1536] = 768 + np.arange(256)
    return src


def _runs(idx):
    runs, i = [], 0
    while i < len(idx):
        j = i + 1
        while j < len(idx) and ((idx[i] < 0 and idx[j] < 0) or (idx[i] >= 0 and idx[j] == idx[i] + j - i)):
            j += 1
        runs.append((int(idx[i]), j - i))
        i = j
    return runs


def _take_runs(w, idx):
    parts = [jnp.zeros(w.shape[:-1] + (n,), w.dtype) if s < 0 else lax.slice_in_dim(w, s, s + n, axis=w.ndim - 1)
             for s, n in _runs(idx)]
    return jnp.concatenate(parts, axis=-1)


def _pack_cols(w, src):
    return _take_runs(w, src)


def _unpack_cols(wp, src, n):
    dst = np.zeros(n, np.int64)
    dst[src[src >= 0]] = np.nonzero(src >= 0)[0]
    return _take_runs(wp, dst)


def _rope_tables(seq):
    half = B_ROPE // 2
    inv_freq = ROPE_THETA ** (-jnp.arange(half, dtype=F32) / half)
    ang = jnp.arange(seq).astype(F32)[:, None] * inv_freq[None, :]
    cos, sin = jnp.cos(ang), jnp.sin(ang)
    z16 = jnp.zeros((seq, 16), F32)
    c = jnp.concatenate([jnp.ones((seq, 64), F32), cos, z16, cos, z16], axis=1)
    s1 = jnp.concatenate([jnp.zeros((seq, 64), F32), -sin, z16, z16, z16], axis=1)
    s2 = jnp.concatenate([jnp.zeros((seq, 64), F32), z16, z16, sin, z16], axis=1)
    return c, s1, s2


def _matmul(a, b, *, mode, group_out, out_dtype, tm, tk, name):
    ga, gb = a.shape[0], b.shape[0]
    g_n = max(ga, gb)
    if mode == "tn":
        k_dim, m_dim = a.shape[1:]
    else:
        m_dim, k_dim = a.shape[1:]
    n_dim = b.shape[1] if mode == "nt" else b.shape[2]
    assert m_dim % tm == 0 and k_dim % tk == 0
    kt = k_dim // tk
    n_red = kt if group_out else g_n * kt
    g_out = g_n if group_out else 1

    def split(g, r):
        return (g, r) if group_out else (r // kt, r % kt)

    def a_map(g, i, r):
        gg, kk = split(g, r)
        gg = gg if ga > 1 else 0
        return (gg, kk, i) if mode == "tn" else (gg, i, kk)

    def b_map(g, i, r):
        gg, kk = split(g, r)
        gg = gg if gb > 1 else 0
        return (gg, 0, kk) if mode == "nt" else (gg, kk, 0)

    a_blk = (None, tk, tm) if mode == "tn" else (None, tm, tk)
    b_blk = (None, n_dim, tk) if mode == "nt" else (None, tk, n_dim)
    dn = _DN[mode]

    def body(a_ref, b_ref, o_ref, *scratch):
        part = lax.dot_general(a_ref[...].astype(BF16), b_ref[...].astype(BF16), dn, preferred_element_type=F32)
        if n_red == 1:
            o_ref[...] = part.astype(o_ref.dtype)
            return
        acc_ref, = scratch
        r = pl.program_id(2)

        @pl.when(r == 0)
        def _():
            acc_ref[...] = part

        @pl.when(r > 0)
        def _():
            acc_ref[...] += part

        @pl.when(r == n_red - 1)
        def _():
            o_ref[...] = acc_ref[...].astype(o_ref.dtype)

    return pl.pallas_call(
        body, name=name, grid=(g_out, m_dim // tm, n_red),
        in_specs=[pl.BlockSpec(a_blk, a_map), pl.BlockSpec(b_blk, b_map)],
        out_specs=pl.BlockSpec((None, tm, n_dim), lambda g, i, r: (g, i, 0)),
        out_shape=jax.ShapeDtypeStruct((g_out, m_dim, n_dim), out_dtype),
        scratch_shapes=[] if n_red == 1 else [pltpu.VMEM((tm, n_dim), F32)],
        compiler_params=_cparams(("parallel", "parallel", "arbitrary")),
    )(a, b)


def _matmul_groupsum(a, b, *, out_dtype, tm, name):
    g_n, m_dim, k_dim = a.shape
    n_dim = b.shape[2]
    assert m_dim % tm == 0 and b.shape[:2] == (g_n, k_dim)

    def body(a_ref, b_ref, o_ref):
        acc = jnp.dot(a_ref[0], b_ref[0], preferred_element_type=F32)
        for g in range(1, g_n):
            acc = acc + jnp.dot(a_ref[g], b_ref[g], preferred_element_type=F32)
        o_ref[...] = acc.astype(o_ref.dtype)

    return pl.pallas_call(
        body, name=name, grid=(m_dim // tm,),
        in_specs=[pl.BlockSpec((g_n, tm, k_dim), lambda i: (0, i, 0)),
                  pl.BlockSpec((g_n, k_dim, n_dim), lambda i: (0, 0, 0))],
        out_specs=pl.BlockSpec((tm, n_dim), lambda i: (i, 0)),
        out_shape=jax.ShapeDtypeStruct((m_dim, n_dim), out_dtype),
        compiler_params=_cparams(("parallel",)),
    )(a, b)


def _row_spec(ts, d):
    return pl.BlockSpec((None, ts, d), lambda b, s: (b, s, 0))


def _mod_spec(d):
    return pl.BlockSpec((None, N_MOD, d), lambda b, s: (b, 0, 0))


def _vec_spec(d):
    return pl.BlockSpec((1, d), lambda b, s: (0, 0))


def _bvec_spec(d):
    return pl.BlockSpec((None, 1, d), lambda b, s: (b, 0, 0))


def _modulate(x, mod, sh_row, sc_row, name, ts=512):
    bsz, seq, d = x.shape

    def body(x_ref, mod_ref, o_ref):
        sh = mod_ref[sh_row:sh_row + 1, :]
        sc = mod_ref[sc_row:sc_row + 1, :]
        o_ref[...] = (x_ref[...] * (1.0 + sc) + sh).astype(o_ref.dtype)

    return pl.pallas_call(
        body, name=name, grid=(bsz, seq // ts),
        in_specs=[_row_spec(ts, d), _mod_spec(d)], out_specs=_row_spec(ts, d),
        out_shape=jax.ShapeDtypeStruct((bsz, seq, d), BF16),
        compiler_params=_cparams(("parallel", "parallel")),
    )(x, mod)


def _modulate_bwd(dh, x, mod, dx_res, sc_row, name, ts=512):
    bsz, seq, d = x.shape

    def body(dh_ref, x_ref, mod_ref, dxr_ref, dx_ref, dsh_ref, dsc_ref):
        s = pl.program_id(1)
        sc = mod_ref[sc_row:sc_row + 1, :]
        dh_v = dh_ref[...]
        dx_ref[...] = dxr_ref[...] + dh_v * (1.0 + sc)
        psh = jnp.sum(dh_v, axis=0, keepdims=True)
        psc = jnp.sum(dh_v * x_ref[...], axis=0, keepdims=True)

        @pl.when(s == 0)
        def _():
            dsh_ref[...] = psh
            dsc_ref[...] = psc

        @pl.when(s > 0)
        def _():
            dsh_ref[...] += psh
            dsc_ref[...] += psc

    return pl.pallas_call(
        body, name=name, grid=(bsz, seq // ts),
        in_specs=[_row_spec(ts, d), _row_spec(ts, d), _mod_spec(d), _row_spec(ts, d)],
        out_specs=[_row_spec(ts, d), _bvec_spec(d), _bvec_spec(d)],
        out_shape=[jax.ShapeDtypeStruct((bsz, seq, d), F32), jax.ShapeDtypeStruct((bsz, 1, d), F32),
                   jax.ShapeDtypeStruct((bsz, 1, d), F32)],
        compiler_params=_cparams(("parallel", "arbitrary")),
    )(dh, x, mod, dx_res)


def _res_ln_fn(x, f, g, lng, lnb, cmul):
    r = ALPHA * x + (cmul * (1.0 + g)) * f
    mu = jnp.mean(r, axis=-1, keepdims=True)
    rc = r - mu
    var = jnp.mean(rc * rc, axis=-1, keepdims=True)
    return rc * lax.rsqrt(var + LN_EPS) * lng + lnb


def _res_ln(x, f, mod, lng, lnb, g_row, cmul, name, nxt=None, ts=512):
    bsz, seq, d = x.shape

    def body(*refs):
        x_ref, f_ref, mod_ref, lng_ref, lnb_ref = refs[:5]
        g = mod_ref[g_row:g_row + 1, :]
        y = _res_ln_fn(x_ref[...], f_ref[...], g, lng_ref[...], lnb_ref[...], cmul)
        if nxt is None:
            refs[5][...] = y
            return
        nmod_ref, o_ref, h_ref = refs[5:]
        o_ref[...] = y
        sh = nmod_ref[nxt[1]:nxt[1] + 1, :]
        sc = nmod_ref[nxt[2]:nxt[2] + 1, :]
        h_ref[...] = (y * (1.0 + sc) + sh).astype(h_ref.dtype)

    in_specs = [_row_spec(ts, d), _row_spec(ts, d), _mod_spec(d), _vec_spec(d), _vec_spec(d)]
    args = [x, f, mod, lng, lnb]
    out_specs, out_shape = [_row_spec(ts, d)], [jax.ShapeDtypeStruct((bsz, seq, d), F32)]
    if nxt is not None:
        in_specs.append(_mod_spec(d))
        args.append(nxt[0])
        out_specs.append(_row_spec(ts, d))
        out_shape.append(jax.ShapeDtypeStruct((bsz, seq, d), BF16))
    res = pl.pallas_call(
        body, name=name, grid=(bsz, seq // ts), in_specs=in_specs, out_specs=out_specs, out_shape=out_shape,
        compiler_params=_cparams(("parallel", "parallel")),
    )(*args)
    return (res[0], res[1]) if nxt is not None else (res[0], None)


def _res_ln_bwd(dy, x, f, mod, lng, lnb, g_row, cmul, name, ts=256):
    bsz, seq, d = x.shape

    def body(dy_ref, x_ref, f_ref, mod_ref, lng_ref, lnb_ref, dx_ref, df_ref, dg_ref, dlg_ref, dlb_ref):
        b, s = pl.program_id(0), pl.program_id(1)
        g = mod_ref[g_row:g_row + 1, :]
        _, vjp = jax.vjp(functools.partial(_res_ln_fn, cmul=cmul), x_ref[...], f_ref[...], g, lng_ref[...],
                         lnb_ref[...])
        dx, df, dg, dlg, dlb = vjp(dy_ref[...])
        dx_ref[...] = dx
        df_ref[...] = df.astype(df_ref.dtype)

        @pl.when(s == 0)
        def _():
            dg_ref[...] = dg

        @pl.when(s > 0)
        def _():
            dg_ref[...] += dg

        first = jnp.logical_and(b == 0, s == 0)

        @pl.when(first)
        def _():
            dlg_ref[...] = dlg
            dlb_ref[...] = dlb

        @pl.when(jnp.logical_not(first))
        def _():
            dlg_ref[...] += dlg
            dlb_ref[...] += dlb

    return pl.pallas_call(
        body, name=name, grid=(bsz, seq // ts),
        in_specs=[_row_spec(ts, d), _row_spec(ts, d), _row_spec(ts, d), _mod_spec(d), _vec_spec(d), _vec_spec(d)],
        out_specs=[_row_spec(ts, d), _row_spec(ts, d), _bvec_spec(d), _vec_spec(d), _vec_spec(d)],
        out_shape=[jax.ShapeDtypeStruct((bsz, seq, d), F32), jax.ShapeDtypeStruct((bsz, seq, d), BF16),
                   jax.ShapeDtypeStruct((bsz, 1, d), F32), jax.ShapeDtypeStruct((1, d), F32),
                   jax.ShapeDtypeStruct((1, d), F32)],
        compiler_params=_cparams(("arbitrary", "arbitrary")),
    )(dy, x, f, mod, lng, lnb)


def _loss_head(y, target, name, ts=512):
    bsz, seq, d = y.shape
    n_s = seq // ts

    def body(y_ref, t_ref, dy_ref, loss_ref, acc_ref):
        b, s = pl.program_id(0), pl.program_id(1)
        err = y_ref[...] - t_ref[...]
        dy_ref[...] = err * (1.0 / d)
        part = jnp.sum(err * err, axis=0, keepdims=True)
        first = jnp.logical_and(b == 0, s == 0)

        @pl.when(first)
        def _():
            acc_ref[...] = part

        @pl.when(jnp.logical_not(first))
        def _():
            acc_ref[...] += part

        @pl.when(jnp.logical_and(b == bsz - 1, s == n_s - 1))
        def _():
            loss_ref[...] = jnp.sum(acc_ref[...], axis=1, keepdims=True) * (0.5 / d)

    return pl.pallas_call(
        body, name=name, grid=(bsz, n_s),
        in_specs=[_row_spec(ts, d), _row_spec(ts, d)],
        out_specs=[_row_spec(ts, d), pl.BlockSpec((1, 1), lambda b, s: (0, 0))],
        out_shape=[jax.ShapeDtypeStruct((bsz, seq, d), F32), jax.ShapeDtypeStruct((1, 1), F32)],
        scratch_shapes=[pltpu.VMEM((1, d), F32)],
        compiler_params=_cparams(("arbitrary", "arbitrary")),
    )(y, target)


def _ffn_in_swiglu(h, w_in_t, name, tm=1024):
    t, d = h.shape
    n_sh, w, _ = w_in_t.shape
    half = n_sh // 2

    def body(h_ref, w_ref, z_ref, a_ref):
        hv = h_ref[...]
        g = lax.dot_general(hv, w_ref[0], _DN["nt"], preferred_element_type=F32)
        u = lax.dot_general(hv, w_ref[1], _DN["nt"], preferred_element_type=F32)
        z_ref[0] = g.astype(z_ref.dtype)
        z_ref[1] = u.astype(z_ref.dtype)
        a_ref[...] = (g * jax.nn.sigmoid(g) * u).astype(a_ref.dtype)

    return pl.pallas_call(
        body, name=name, grid=(half, t // tm),
        in_specs=[pl.BlockSpec((tm, d), lambda g, i: (i, 0)),
                  pl.BlockSpec((2, None, w, d), lambda g, i: (0, g, 0, 0))],
        out_specs=[pl.BlockSpec((2, None, tm, w), lambda g, i: (0, g, i, 0)),
                   pl.BlockSpec((None, tm, w), lambda g, i: (g, i, 0))],
        out_shape=[jax.ShapeDtypeStruct((2, half, t, w), BF16), jax.ShapeDtypeStruct((half, t, w), BF16)],
        compiler_params=_cparams(("parallel", "parallel")),
    )(h, w_in_t.reshape(2, half, w, d))


def _ffn_out_dx_swiglu(df, w_out, z, name, tm=1024):
    t, d = df.shape
    half, w, _ = w_out.shape

    def body(df_ref, w_ref, z_ref, dz_ref):
        da = lax.dot_general(df_ref[...], w_ref[...], _DN["nt"], preferred_element_type=F32)
        g = z_ref[0].astype(F32)
        u = z_ref[1].astype(F32)
        sig = jax.nn.sigmoid(g)
        dz_ref[0] = (da * u * (sig * (1.0 + g * (1.0 - sig)))).astype(dz_ref.dtype)
        dz_ref[1] = (da * (g * sig)).astype(dz_ref.dtype)

    zspec = pl.BlockSpec((2, None, tm, w), lambda g, i: (0, g, i, 0))
    return pl.pallas_call(
        body, name=name, grid=(half, t // tm),
        in_specs=[pl.BlockSpec((tm, d), lambda g, i: (i, 0)), pl.BlockSpec((None, w, d), lambda g, i: (g, 0, 0)),
                  zspec],
        out_specs=zspec, out_shape=jax.ShapeDtypeStruct(z.shape, BF16),
        compiler_params=_cparams(("parallel", "parallel")),
    )(df, w_out, z)


def _log_sigmoid(x):
    return jnp.minimum(x, 0.0) - jnp.log(1.0 + jnp.exp(-jnp.abs(x)))


def _hgrn_consts():
    r = lax.broadcasted_iota(jnp.int32, (GROUP_WIDTH, GROUP_WIDTH), 0)
    c = lax.broadcasted_iota(jnp.int32, (GROUP_WIDTH, GROUP_WIDTH), 1)
    bd = (r // HEAD_DIM == c // HEAD_DIM).astype(F32)
    r16 = lax.broadcasted_iota(jnp.int32, (A_CHUNK, A_CHUNK), 0)
    c16 = lax.broadcasted_iota(jnp.int32, (A_CHUNK, A_CHUNK), 1)
    tril = (r16 >= c16).astype(F32)
    rows = lax.broadcasted_iota(jnp.int32, (A_CHUNK, GROUP_WIDTH), 0)
    return bd, tril, rows


def _hgrn_lb(logits8, layer):
    rows = lax.broadcasted_iota(jnp.int32, logits8.shape, 0)
    valid = rows < DEPTH
    mx = jnp.max(jnp.where(valid, logits8, NEG), axis=0, keepdims=True)
    e = jnp.where(valid, jnp.exp(logits8 - mx), 0.0)
    sm = e / jnp.sum(e, axis=0, keepdims=True)
    pick = jnp.logical_and(rows >= 1, rows <= layer)
    return jnp.sum(jnp.where(pick, sm, 0.0), axis=0, keepdims=True)


def _hgrn_chunk(aq, af, ai, ag, logits8, norm_g, st, *, layer, consts):
    bd, tril, rows = consts
    lb = _hgrn_lb(logits8, layer)
    la = jnp.log(jnp.maximum(lb, LB_FLOOR))
    b2 = jnp.log(1.0 - lb) + _log_sigmoid(af)
    log_f = jnp.maximum(la, b2) + jnp.log(1.0 + jnp.exp(-jnp.abs(la - b2)))
    k = 1.0 - jnp.exp(log_f)
    qf = aq * jax.nn.sigmoid(aq)
    g_cum = jnp.dot(tril, log_f, precision=HI, preferred_element_type=F32)

    c, w = A_CHUNK, GROUP_WIDTH

    def by_key(v):
        return jnp.broadcast_to(v[:, None, :], (c, c, w))

    def by_query(v):
        return jnp.broadcast_to(v[None, :, :], (c, c, w))

    s_i = lax.broadcasted_iota(jnp.int32, (c, c, w), 0)
    t_i = lax.broadcasted_iota(jnp.int32, (c, c, w), 1)
    rel = jnp.where(t_i >= s_i, by_query(g_cum) - by_key(g_cum), NEG)
    pairs = by_query(qf) * by_key(k) * jnp.exp(rel)
    a_all = _bdot(pairs.reshape(c * c, w), bd, "nn").reshape(c, c, w)
    o = jnp.sum(a_all * by_key(ai), axis=0)
    q_dec = qf * jnp.exp(g_cum)
    o = o + _bdot(q_dec, st, "nt")
    g_last = jnp.sum(jnp.where(rows == c - 1, g_cum, 0.0), axis=0, keepdims=True)
    k_end = k * jnp.exp(g_last - g_cum)
    kv = _bdot(ai, k_end, "tn")
    st_new = st * jnp.exp(g_last) + kv * bd
    ms = _bdot(o * o, bd, "nn") * (1.0 / HEAD_DIM)
    o = o * lax.rsqrt(ms + RMS_EPS) * norm_g
    return o * (ag * jax.nn.sigmoid(ag)), st_new


def _hgrn_fwd(proj, logits8, norm_g, layer, name, ts=256):
    bsz, seq, _ = proj.shape
    n_ch = ts // A_CHUNK

    def body(p_ref, lg_ref, ng_ref, o_ref, st_ref, st_scr):
        @pl.when(pl.program_id(1) == 0)
        def _():
            st_scr[...] = jnp.zeros_like(st_scr)

        consts = _hgrn_consts()
        logits_v, ng_v = lg_ref[...], ng_ref[...]

        def chunk(ci, carry):
            r = ci * A_CHUNK if isinstance(ci, int) else pl.multiple_of(ci * A_CHUNK, A_CHUNK)
            st = st_scr[...]
            st_ref[ci] = st
            o, st_new = _hgrn_chunk(
                p_ref[pl.ds(r, A_CHUNK), 0:256], p_ref[pl.ds(r, A_CHUNK), 256:512],
                p_ref[pl.ds(r, A_CHUNK), 512:768], p_ref[pl.ds(r, A_CHUNK), 768:1024],
                logits_v, ng_v, st, layer=layer, consts=consts)
            o_ref[pl.ds(r, A_CHUNK), :] = o.astype(o_ref.dtype)
            st_scr[...] = st_new
            return carry

        if n_ch <= 2:
            for c_static in range(n_ch):
                chunk(c_static, 0)
        else:
            lax.fori_loop(0, n_ch, chunk, 0, unroll=2)

    return pl.pallas_call(
        body, name=name, grid=(bsz, seq // ts),
        in_specs=[pl.BlockSpec((None, ts, 1024), lambda b, s: (b, s, 0)),
                  pl.BlockSpec((8, GROUP_WIDTH), lambda b, s: (0, 0)),
                  pl.BlockSpec((1, GROUP_WIDTH), lambda b, s: (0, 0))],
        out_specs=[pl.BlockSpec((None, ts, GROUP_WIDTH), lambda b, s: (b, s, 0)),
                   pl.BlockSpec((None, n_ch, GROUP_WIDTH, GROUP_WIDTH), lambda b, s: (b, s, 0, 0))],
        out_shape=[jax.ShapeDtypeStruct((bsz, seq, MO_W), BF16),
                   jax.ShapeDtypeStruct((bsz, seq // A_CHUNK, GROUP_WIDTH, GROUP_WIDTH), F32)],
        scratch_shapes=[pltpu.VMEM((GROUP_WIDTH, GROUP_WIDTH), F32)],
        compiler_params=_cparams(("parallel", "arbitrary")),
    )(proj, logits8, norm_g)


def _hgrn_bwd(dmo, proj, states, logits8, norm_g, layer, name, ts=256):
    bsz, seq, _ = proj.shape
    n_ch = ts // A_CHUNK
    n_s = seq // ts

    def body(do_ref, p_ref, st_ref, lg_ref, ng_ref, dp_ref, dlg_ref, dng_ref, dst_scr):
        b, s = pl.program_id(0), pl.program_id(1)

        @pl.when(s == 0)
        def _():
            dst_scr[...] = jnp.zeros_like(dst_scr)

        @pl.when(jnp.logical_and(b == 0, s == 0))
        def _():
            dlg_ref[...] = jnp.zeros_like(dlg_ref)
            dng_ref[...] = jnp.zeros_like(dng_ref)

        consts = _hgrn_consts()
        logits_v, ng_v = lg_ref[...], ng_ref[...]
        fn = functools.partial(_hgrn_chunk, layer=layer, consts=consts)

        def chunk(t, carry):
            ci = n_ch - 1 - t
            r = ci * A_CHUNK if isinstance(ci, int) else pl.multiple_of(ci * A_CHUNK, A_CHUNK)
            _, vjp = jax.vjp(
                fn, p_ref[pl.ds(r, A_CHUNK), 0:256], p_ref[pl.ds(r, A_CHUNK), 256:512],
                p_ref[pl.ds(r, A_CHUNK), 512:768], p_ref[pl.ds(r, A_CHUNK), 768:1024],
                logits_v, ng_v, st_ref[ci])
            daq, daf, dai, dag, dlg, dng, dst = vjp((do_ref[pl.ds(r, A_CHUNK), :], dst_scr[...]))
            dp_ref[pl.ds(r, A_CHUNK), 0:256] = daq.astype(dp_ref.dtype)
            dp_ref[pl.ds(r, A_CHUNK), 256:512] = daf.astype(dp_ref.dtype)
            dp_ref[pl.ds(r, A_CHUNK), 512:768] = dai.astype(dp_ref.dtype)
            dp_ref[pl.ds(r, A_CHUNK), 768:1024] = dag.astype(dp_ref.dtype)
            dlg_ref[...] += dlg
            dng_ref[...] += dng
            dst_scr[...] = dst
            return carry

        if n_ch <= 2:
            for c_static in range(n_ch):
                chunk(c_static, 0)
        else:
            lax.fori_loop(0, n_ch, chunk, 0, unroll=2)

    rev = lambda b, s: (b, n_s - 1 - s, 0)
    return pl.pallas_call(
        body, name=name, grid=(bsz, n_s),
        in_specs=[pl.BlockSpec((None, ts, GROUP_WIDTH), rev),
                  pl.BlockSpec((None, ts, 1024), rev),
                  pl.BlockSpec((None, n_ch, GROUP_WIDTH, GROUP_WIDTH), lambda b, s: (b, n_s - 1 - s, 0, 0)),
                  pl.BlockSpec((8, GROUP_WIDTH), lambda b, s: (0, 0)),
                  pl.BlockSpec((1, GROUP_WIDTH), lambda b, s: (0, 0))],
        out_specs=[pl.BlockSpec((None, ts, 1024), rev),
                   pl.BlockSpec((8, GROUP_WIDTH), lambda b, s: (0, 0)),
                   pl.BlockSpec((1, GROUP_WIDTH), lambda b, s: (0, 0))],
        out_shape=[jax.ShapeDtypeStruct((bsz, seq, PACK_W), BF16),
                   jax.ShapeDtypeStruct((8, GROUP_WIDTH), F32), jax.ShapeDtypeStruct((1, GROUP_WIDTH), F32)],
        scratch_shapes=[pltpu.VMEM((GROUP_WIDTH, GROUP_WIDTH), F32)],
        compiler_params=_cparams(("arbitrary", "arbitrary")),
    )(dmo, proj, states, logits8, norm_g)


def _rms_fn(x, g):
    return x * lax.rsqrt(jnp.mean(x * x, axis=-1, keepdims=True) + RMS_EPS) * g


def _tile4(t):
    return jnp.concatenate([t, t, t, t], axis=1)


def _rope(x, c, s1, s2):
    w = x.shape[-1]
    return x * c + pltpu.roll(x, 32, axis=1) * s2 + pltpu.roll(x, w - 32, axis=1) * s1


def _rope_t(dy, c, s1, s2):
    w = dy.shape[-1]
    return dy * c + pltpu.roll(dy * s2, w - 32, axis=1) + pltpu.roll(dy * s1, 32, axis=1)


def _mla_pre(proj, qg, kvg, wq, wkv, tabs, name, ts=256):
    bsz, seq, _ = proj.shape

    def body(p_ref, qg_ref, kvg_ref, wq_ref, wkv_ref, c_ref, s1_ref, s2_ref, q_ref, kv_ref):
        nq = _rms_fn(p_ref[:, 0:256], qg_ref[...])
        nkv = _rms_fn(p_ref[:, 256:384], kvg_ref[...])
        c, s1, s2 = c_ref[...], s1_ref[...], s2_ref[...]
        qp = jnp.dot(nq.astype(BF16), wq_ref[...], preferred_element_type=F32)
        q_ref[...] = _rope(qp, _tile4(c), _tile4(s1), _tile4(s2)).astype(q_ref.dtype)
        kv = jnp.dot(nkv.astype(BF16), wkv_ref[...], preferred_element_type=F32)
        krr = _rope(p_ref[:, 384:512], c, s1, s2)
        zero = jnp.zeros_like(krr)
        kv_ref[...] = (kv + jnp.concatenate([krr, zero] * N_HEADS, axis=1)).astype(kv_ref.dtype)

    tab_spec = pl.BlockSpec((ts, LANES), lambda b, s: (s, 0))
    return pl.pallas_call(
        body, name=name, grid=(bsz, seq // ts),
        in_specs=[pl.BlockSpec((None, ts, 512), lambda b, s: (b, s, P_B // 512)),
                  _vec_spec(256), _vec_spec(128),
                  pl.BlockSpec((256, 512), lambda b, s: (0, 0)), pl.BlockSpec((128, 1024), lambda b, s: (0, 0)),
                  tab_spec, tab_spec, tab_spec],
        out_specs=[_row_spec(ts, 512), _row_spec(ts, 1024)],
        out_shape=[jax.ShapeDtypeStruct((bsz, seq, 512), BF16), jax.ShapeDtypeStruct((bsz, seq, 1024), BF16)],
        compiler_params=_cparams(("parallel", "parallel")),
    )(proj, qg, kvg, wq, wkv, *tabs)


def _mla_pre_bwd(dq, dkv, dproj, proj, qg, kvg, wq, wkv, tabs, name, ts=256):
    bsz, seq, _ = proj.shape

    def body(dq_ref, dkv_ref, dp_any, p_ref, qg_ref, kvg_ref, wq_ref, wkv_ref, c_ref, s1_ref, s2_ref,
             dp_ref, dqg_ref, dkvg_ref, dwq_ref, dwkv_ref):
        del dp_any
        first = jnp.logical_and(pl.program_id(0) == 0, pl.program_id(1) == 0)

        @pl.when(first)
        def _():
            dqg_ref[...] = jnp.zeros_like(dqg_ref)
            dkvg_ref[...] = jnp.zeros_like(dkvg_ref)
            dwq_ref[...] = jnp.zeros_like(dwq_ref)
            dwkv_ref[...] = jnp.zeros_like(dwkv_ref)

        c, s1, s2 = c_ref[...], s1_ref[...], s2_ref[...]
        nq, vjp_q = jax.vjp(_rms_fn, p_ref[:, 0:256], qg_ref[...])
        nkv, vjp_kv = jax.vjp(_rms_fn, p_ref[:, 256:384], kvg_ref[...])
        dqp = _rope_t(dq_ref[...], _tile4(c), _tile4(s1), _tile4(s2)).astype(BF16)
        dkv_v = dkv_ref[...]
        dkv_b = dkv_v.astype(BF16)
        tn = (((0,), (0,)), ((), ()))
        nt = (((1,), (1,)), ((), ()))
        dwq_ref[...] += lax.dot_general(nq.astype(BF16), dqp, tn, preferred_element_type=F32)
        dwkv_ref[...] += lax.dot_general(nkv.astype(BF16), dkv_b, tn, preferred_element_type=F32)
        dcq, dqg = vjp_q(lax.dot_general(dqp, wq_ref[...], nt, preferred_element_type=F32))
        dckv, dkvg = vjp_kv(lax.dot_general(dkv_b, wkv_ref[...], nt, preferred_element_type=F32))
        dqg_ref[...] += dqg
        dkvg_ref[...] += dkvg
        dk_sum = dkv_v[:, 0:128] + dkv_v[:, 256:384] + dkv_v[:, 512:640] + dkv_v[:, 768:896]
        lane = lax.broadcasted_iota(jnp.int32, dk_sum.shape, 1)
        dkr = jnp.where(lane >= 64, _rope_t(dk_sum, c, s1, s2), 0.0)
        dp_ref[:, 0:256] = dcq.astype(dp_ref.dtype)
        dp_ref[:, 256:384] = dckv.astype(dp_ref.dtype)
        dp_ref[:, 384:512] = dkr.astype(dp_ref.dtype)

    tab_spec = pl.BlockSpec((ts, LANES), lambda b, s: (s, 0))
    const = lambda shape: pl.BlockSpec(shape, lambda b, s: (0, 0))
    return pl.pallas_call(
        body, name=name, grid=(bsz, seq // ts),
        in_specs=[_row_spec(ts, 512), _row_spec(ts, 1024), pl.BlockSpec(memory_space=pl.ANY),
                  pl.BlockSpec((None, ts, 512), lambda b, s: (b, s, P_B // 512)),
                  _vec_spec(256), _vec_spec(128), const((256, 512)), const((128, 1024)),
                  tab_spec, tab_spec, tab_spec],
        out_specs=[pl.BlockSpec((None, ts, 512), lambda b, s: (b, s, P_B // 512)),
                   _vec_spec(256), _vec_spec(128), const((256, 512)), const((128, 1024))],
        out_shape=[jax.ShapeDtypeStruct(dproj.shape, dproj.dtype), jax.ShapeDtypeStruct((1, 256), F32),
                   jax.ShapeDtypeStruct((1, 128), F32), jax.ShapeDtypeStruct((256, 512), F32),
                   jax.ShapeDtypeStruct((128, 1024), F32)],
        input_output_aliases={2: 0},
        compiler_params=_cparams(("arbitrary", "arbitrary")),
    )(dq, dkv, dproj, proj, qg, kvg, wq, wkv, *tabs)


def _fox_gate(proj, bf, name):
    bsz, seq, _ = proj.shape
    n_blk = seq // LANES

    def body(x_ref, bf_ref, f_ref):
        r_i = lax.broadcasted_iota(jnp.int32, (LANES, LANES), 0)
        c_i = lax.broadcasted_iota(jnp.int32, (LANES, LANES), 1)
        tril = (r_i >= c_i).astype(F32)
        bias = bf_ref[...]

        def blk(i, carry):
            r = pl.multiple_of(i * LANES, LANES)
            lf = _log_sigmoid(x_ref[pl.ds(r, LANES), :] + bias)
            f_ref[pl.ds(r, LANES), :] = jnp.dot(tril, lf, precision=HI, preferred_element_type=F32) + carry
            return carry + jnp.sum(lf, axis=0, keepdims=True)

        lax.fori_loop(0, n_blk, blk, jnp.zeros((1, LANES), F32))

    return pl.pallas_call(
        body, name=name, grid=(bsz,),
        in_specs=[pl.BlockSpec((None, seq, LANES), lambda b: (b, 0, P_CF // LANES)),
                  pl.BlockSpec((1, LANES), lambda b: (0, 0))],
        out_specs=pl.BlockSpec((None, seq, LANES), lambda b: (b, 0, 0)),
        out_shape=jax.ShapeDtypeStruct((bsz, seq, LANES), F32),
        compiler_params=_cparams(("parallel",)),
    )(proj, bf)


def _fox_gate_bwd(dfq, dfk_cols, dproj, proj, bf, name):
    bsz, seq, _ = proj.shape
    n_blk = seq // LANES

    def body(dfq_ref, dfk_ref, dp_any, x_ref, bf_ref, dp_ref, dbf_ref):
        del dp_any

        @pl.when(pl.program_id(0) == 0)
        def _():
            dbf_ref[...] = jnp.zeros_like(dbf_ref)

        r_i = lax.broadcasted_iota(jnp.int32, (LANES, LANES), 0)
        c_i = lax.broadcasted_iota(jnp.int32, (LANES, LANES), 1)
        triu = (r_i <= c_i).astype(F32)
        bias = bf_ref[...]

        def blk(t, carry):
            tail, dbf = carry
            r = pl.multiple_of((n_blk - 1 - t) * LANES, LANES)
            dc = dfk_ref[pl.ds(r, LANES), :]
            for hd in range(N_HEADS):
                dc = dc + jnp.where(c_i == hd, dfq_ref[hd, pl.ds(r, LANES), :], 0.0)
            dlf = jnp.dot(triu, dc, precision=HI, preferred_element_type=F32) + tail
            dx = dlf * (1.0 - jax.nn.sigmoid(x_ref[pl.ds(r, LANES), :] + bias))
            dp_ref[pl.ds(r, LANES), :] = dx.astype(dp_ref.dtype)
            return tail + jnp.sum(dc, axis=0, keepdims=True), dbf + jnp.sum(dx, axis=0, keepdims=True)

        z = jnp.zeros((1, LANES), F32)
        _, dbf = lax.fori_loop(0, n_blk, blk, (z, z))
        dbf_ref[...] += dbf

    return pl.pallas_call(
        body, name=name, grid=(bsz,),
        in_specs=[pl.BlockSpec((None, N_HEADS, seq, LANES), lambda b: (b, 0, 0, 0)),
                  pl.BlockSpec((None, seq, LANES), lambda b: (b, 0, 0)), pl.BlockSpec(memory_space=pl.ANY),
                  pl.BlockSpec((None, seq, LANES), lambda b: (b, 0, P_CF // LANES)),
                  pl.BlockSpec((1, LANES), lambda b: (0, 0))],
        out_specs=[pl.BlockSpec((None, seq, LANES), lambda b: (b, 0, P_CF // LANES)),
                   pl.BlockSpec((1, LANES), lambda b: (0, 0))],
        out_shape=[jax.ShapeDtypeStruct(dproj.shape, dproj.dtype), jax.ShapeDtypeStruct((1, LANES), F32)],
        input_output_aliases={2: 0},
        compiler_params=_cparams(("arbitrary",)),
    )(dfq, dfk_cols, dproj, proj, bf)


def _gate_terms(fc_ref, fr_ref, h, tq, tk):
    lane = lax.broadcasted_iota(jnp.int32, (tq, LANES), 1)
    fcol = jnp.sum(jnp.where(lane == h, fc_ref[...], 0.0), axis=1, keepdims=True)
    sub = lax.broadcasted_iota(jnp.int32, (8, tk), 0)
    frow = jnp.sum(jnp.where(sub == h, fr_ref[...], 0.0), axis=0, keepdims=True)
    return fcol - frow


def _scores(q_ref, k_ref, gate_refs, scale, h, masked, tq, tk):
    q = (q_ref[...].astype(F32) * scale).astype(BF16)
    s = lax.dot_general(q, k_ref[...].astype(BF16), _DN["nt"], preferred_element_type=F32)
    if gate_refs is not None:
        s = s + _gate_terms(gate_refs[0], gate_refs[1], h, tq, tk)
    if masked is not False:
        r_i = lax.broadcasted_iota(jnp.int32, (tq, tk), 0)
        c_i = lax.broadcasted_iota(jnp.int32, (tq, tk), 1)
        keep = c_i <= r_i
        s = jnp.where(keep if masked is True else jnp.logical_or(jnp.logical_not(masked), keep), s, NEG)
    return s, q


def _lanes(col):
    return jnp.broadcast_to(col, (col.shape[0], LANES))


def _attn_fwd(qa, q0, kva, kv0, mo, o0, gates, scale, name, tq=None):
    bsz, seq, _ = qa.shape
    tq = ATTN_TILE if tq is None else tq
    n_q = seq // tq
    gated = gates is not None

    def body(*refs):
        q_ref, k_ref, v_ref = refs[:3]
        gate_refs = refs[3:5] if gated else None
        o_ref, lse_ref, m_s, l_s, acc_s = refs[-5:]
        h, i, j = pl.program_id(1), pl.program_id(2), pl.program_id(3)

        @pl.when(j == 0)
        def _():
            m_s[...] = jnp.full_like(m_s, NEG)
            l_s[...] = jnp.zeros_like(l_s)
            acc_s[...] = jnp.zeros_like(acc_s)

        def step(masked):
            s, _ = _scores(q_ref, k_ref, gate_refs, scale, h, masked, tq, tq)
            m_prev = m_s[...]
            m_new = jnp.maximum(m_prev, jnp.max(s, axis=1, keepdims=True))
            alpha = jnp.exp(m_prev - m_new)
            p = jnp.exp(s - m_new)
            l_s[...] = alpha * l_s[...] + jnp.sum(p, axis=1, keepdims=True)
            acc_s[...] = alpha * acc_s[...] + jnp.dot(p.astype(BF16), v_ref[...].astype(BF16),
                                                      preferred_element_type=F32)
            m_s[...] = m_new

        @pl.when(j <= i)
        def _():
            step(j == i)

        @pl.when(j == i)
        def _():
            o_ref[...] = (acc_s[...] / l_s[...]).astype(o_ref.dtype)
            lse_ref[...] = _lanes(m_s[...] + jnp.log(l_s[...]))

    blk = (None, tq, LANES)
    in_specs = [pl.BlockSpec(blk, lambda b, h, i, j: (b, i, q0 + h)),
                pl.BlockSpec(blk, lambda b, h, i, j: (b, jnp.minimum(j, i), kv0 + 2 * h)),
                pl.BlockSpec(blk, lambda b, h, i, j: (b, jnp.minimum(j, i), kv0 + 2 * h + 1))]
    args = [qa, kva, kva]
    if gated:
        in_specs += [pl.BlockSpec(blk, lambda b, h, i, j: (b, i, 0)),
                     pl.BlockSpec((None, 8, tq), lambda b, h, i, j: (b, 0, jnp.minimum(j, i)))]
        args += list(gates)
    in_specs.append(pl.BlockSpec(memory_space=pl.ANY))
    args.append(mo)
    return pl.pallas_call(
        body, name=name, grid=(bsz, N_HEADS, n_q, n_q), in_specs=in_specs,
        out_specs=[pl.BlockSpec(blk, lambda b, h, i, j: (b, i, o0 + h)),
                   pl.BlockSpec((None, None, tq, LANES), lambda b, h, i, j: (b, h, i, 0))],
        out_shape=[jax.ShapeDtypeStruct(mo.shape, mo.dtype),
                   jax.ShapeDtypeStruct((bsz, N_HEADS, seq, LANES), F32)],
        scratch_shapes=[pltpu.VMEM((tq, 1), F32), pltpu.VMEM((tq, 1), F32), pltpu.VMEM((tq, LANES), F32)],
        input_output_aliases={len(args) - 1: 0},
        compiler_params=_cparams(("parallel", "parallel", "parallel", "arbitrary")),
    )(*args)


def _attn_bwd_q(qa, q0, kva, kv0, mo, dmo, o0, lse, gates, scale, out, out0, name, tq=None):
    bsz, seq, _ = qa.shape
    tq = ATTN_TILE if tq is None else tq
    n_q = seq // tq
    gated = gates is not None
    aliased = not isinstance(out, jax.ShapeDtypeStruct)

    def body(*refs):
        q_ref, k_ref, v_ref, o_ref, do_ref, lse_ref = refs[:6]
        gate_refs = refs[6:8] if gated else None
        dq_ref, delta_ref, dfq_ref, acc_s, dl_s, df_s = refs[-6:]
        h, i, j = pl.program_id(1), pl.program_id(2), pl.program_id(3)

        @pl.when(j == 0)
        def _():
            acc_s[...] = jnp.zeros_like(acc_s)
            df_s[...] = jnp.zeros_like(df_s)
            dl_s[...] = jnp.sum(do_ref[...] * o_ref[...].astype(F32), axis=1, keepdims=True)

        def step(masked):
            s, _ = _scores(q_ref, k_ref, gate_refs, scale, h, masked, tq, tq)
            p = jnp.exp(s - lse_ref[:, 0:1])
            dp = lax.dot_general(do_ref[...].astype(BF16), v_ref[...].astype(BF16), _DN["nt"],
                                 preferred_element_type=F32)
            ds = p * (dp - dl_s[...])
            acc_s[...] += jnp.dot(ds.astype(BF16), k_ref[...].astype(BF16), preferred_element_type=F32)
            df_s[...] += jnp.sum(ds, axis=1, keepdims=True)

        @pl.when(j <= i)
        def _():
            step(j == i)

        @pl.when(j == i)
        def _():
            dq_ref[...] = (acc_s[...] * scale).astype(dq_ref.dtype)
            delta_ref[...] = _lanes(dl_s[...])
            dfq_ref[...] = _lanes(df_s[...])

    blk = (None, tq, LANES)
    col = pl.BlockSpec((None, None, tq, LANES), lambda b, h, i, j: (b, h, i, 0))
    in_specs = [pl.BlockSpec(blk, lambda b, h, i, j: (b, i, q0 + h)),
                pl.BlockSpec(blk, lambda b, h, i, j: (b, jnp.minimum(j, i), kv0 + 2 * h)),
                pl.BlockSpec(blk, lambda b, h, i, j: (b, jnp.minimum(j, i), kv0 + 2 * h + 1)),
                pl.BlockSpec(blk, lambda b, h, i, j: (b, i, o0 + h)),
                pl.BlockSpec(blk, lambda b, h, i, j: (b, i, o0 + h)), col]
    args = [qa, kva, kva, mo, dmo, lse]
    if gated:
        in_specs += [pl.BlockSpec(blk, lambda b, h, i, j: (b, i, 0)),
                     pl.BlockSpec((None, 8, tq), lambda b, h, i, j: (b, 0, jnp.minimum(j, i)))]
        args += list(gates)
    aliases = {}
    if aliased:
        in_specs.append(pl.BlockSpec(memory_space=pl.ANY))
        args.append(out)
        aliases = {len(args) - 1: 0}
    vec = jax.ShapeDtypeStruct((bsz, N_HEADS, seq, LANES), F32)
    return pl.pallas_call(
        body, name=name, grid=(bsz, N_HEADS, n_q, n_q), in_specs=in_specs,
        out_specs=[pl.BlockSpec(blk, lambda b, h, i, j: (b, i, out0 + h)), col, col],
        out_shape=[jax.ShapeDtypeStruct(out.shape, out.dtype), vec, vec],
        scratch_shapes=[pltpu.VMEM((tq, LANES), F32), pltpu.VMEM((tq, 1), F32), pltpu.VMEM((tq, 1), F32)],
        input_output_aliases=aliases,
        compiler_params=_cparams(("parallel", "parallel", "parallel", "arbitrary")),
    )(*args)


def _attn_bwd_kv(qa, q0, kva, kv0, dmo, o0, lse, delta, gates, scale, out, out0, name, tq=None):
    bsz, seq, _ = qa.shape
    tq = ATTN_TILE if tq is None else tq
    n_q = seq // tq
    gated = gates is not None
    aliased = not isinstance(out, jax.ShapeDtypeStruct)

    def body(*refs):
        q_ref, k_ref, v_ref, do_ref, lse_ref, dl_ref = refs[:6]
        gate_refs = refs[6:8] if gated else None
        dkv_ref, dfk_ref, dk_s, dv_s, df_s = refs[-5:]
        h, j, i = pl.program_id(1), pl.program_id(2), pl.program_id(3)

        @pl.when(i == 0)
        def _():
            dk_s[...] = jnp.zeros_like(dk_s)
            dv_s[...] = jnp.zeros_like(dv_s)
            df_s[...] = jnp.zeros_like(df_s)

        def step(masked):
            s, q = _scores(q_ref, k_ref, gate_refs, scale, h, masked, tq, tq)
            p = jnp.exp(s - lse_ref[:, 0:1])
            do_b = do_ref[...].astype(BF16)
            dp = lax.dot_general(do_b, v_ref[...].astype(BF16), _DN["nt"], preferred_element_type=F32)
            ds = p * (dp - dl_ref[:, 0:1])
            dv_s[...] += lax.dot_general(p.astype(BF16), do_b, _DN["tn"], preferred_element_type=F32)
            dk_s[...] += lax.dot_general(ds.astype(BF16), q, _DN["tn"], preferred_element_type=F32)
            df_s[...] -= jnp.sum(ds, axis=0, keepdims=True)

        @pl.when(i > j)
        def _():
            step(False)

        @pl.when(i == j)
        def _():
            step(True)

        @pl.when(i == n_q - 1)
        def _():
            dkv_ref[:, 0:LANES] = dk_s[...].astype(dkv_ref.dtype)
            dkv_ref[:, LANES:2 * LANES] = dv_s[...].astype(dkv_ref.dtype)
            dfk_ref[...] = df_s[...]

    blk = (None, tq, LANES)
    col = pl.BlockSpec((None, None, tq, LANES), lambda b, h, j, i: (b, h, jnp.maximum(i, j), 0))
    in_specs = [pl.BlockSpec(blk, lambda b, h, j, i: (b, jnp.maximum(i, j), q0 + h)),
                pl.BlockSpec(blk, lambda b, h, j, i: (b, j, kv0 + 2 * h)),
                pl.BlockSpec(blk, lambda b, h, j, i: (b, j, kv0 + 2 * h + 1)),
                pl.BlockSpec(blk, lambda b, h, j, i: (b, jnp.maximum(i, j), o0 + h)), col, col]
    args = [qa, kva, kva, dmo, lse, delta]
    if gated:
        in_specs += [pl.BlockSpec(blk, lambda b, h, j, i: (b, jnp.maximum(i, j), 0)),
                     pl.BlockSpec((None, 8, tq), lambda b, h, j, i: (b, 0, j))]
        args += list(gates)
    aliases = {}
    if aliased:
        in_specs.append(pl.BlockSpec(memory_space=pl.ANY))
        args.append(out)
        aliases = {len(args) - 1: 0}
    return pl.pallas_call(
        body, name=name, grid=(bsz, N_HEADS, n_q, n_q), in_specs=in_specs,
        out_specs=[pl.BlockSpec((None, tq, 2 * LANES), lambda b, h, j, i: (b, j, out0 + h)),
                   pl.BlockSpec((None, None, 1, tq), lambda b, h, j, i: (b, h, 0, j))],
        out_shape=[jax.ShapeDtypeStruct(out.shape, out.dtype), jax.ShapeDtypeStruct((bsz, N_HEADS, 1, seq), F32)],
        scratch_shapes=[pltpu.VMEM((tq, LANES), F32), pltpu.VMEM((tq, LANES), F32), pltpu.VMEM((1, tq), F32)],
        input_output_aliases=aliases,
        compiler_params=_cparams(("parallel", "parallel", "parallel", "arbitrary")),
    )(*args)


def _block_logits(q, k_ref, gate, j, scale_unused, h, masked, tq):
    del scale_unused
    r = pl.multiple_of(j * tq, tq)
    s = lax.dot_general(q, k_ref[pl.ds(r, tq), :].astype(BF16), _DN["nt"], preferred_element_type=F32)
    if gate is not None:
        fcol, fr_ref = gate
        sub = lax.broadcasted_iota(jnp.int32, (8, tq), 0)
        frow = jnp.sum(jnp.where(sub == h, fr_ref[:, pl.ds(r, tq)], 0.0), axis=0, keepdims=True)
        s = s + (fcol - frow)
    if masked:
        r_i = lax.broadcasted_iota(jnp.int32, (tq, tq), 0)
        c_i = lax.broadcasted_iota(jnp.int32, (tq, tq), 1)
        s = jnp.where(c_i <= r_i, s, NEG)
    return s, r


def _gate_col(fc_ref, h, tq):
    lane = lax.broadcasted_iota(jnp.int32, (tq, LANES), 1)
    return jnp.sum(jnp.where(lane == h, fc_ref[...], 0.0), axis=1, keepdims=True)


def _attn_fwd_loop(qa, q0, kva, kv0, mo, o0, gates, scale, name, tq=None):
    bsz, seq, _ = qa.shape
    tq = ATTN_TILE if tq is None else tq
    n_q = seq // tq
    gated = gates is not None

    def body(*refs):
        q_ref, k_ref, v_ref = refs[:3]
        o_ref, lse_ref = refs[-2:]
        h, i = pl.program_id(1), pl.program_id(2)
        q = (q_ref[...].astype(F32) * scale).astype(BF16)
        gate = (_gate_col(refs[3], h, tq), refs[4]) if gated else None

        def step(j, carry, masked):
            m_prev, l_prev, acc = carry
            s, r = _block_logits(q, k_ref, gate, j, None, h, masked, tq)
            m_new = jnp.maximum(m_prev, jnp.max(s, axis=1, keepdims=True))
            alpha = jnp.exp(m_prev - m_new)
            p = jnp.exp(s - m_new)
            l_new = alpha * l_prev + jnp.sum(p, axis=1, keepdims=True)
            acc = alpha * acc + jnp.dot(p.astype(BF16), v_ref[pl.ds(r, tq), :].astype(BF16),
                                        preferred_element_type=F32)
            return m_new, l_new, acc

        init = (jnp.full((tq, 1), NEG, F32), jnp.zeros((tq, 1), F32), jnp.zeros((tq, LANES), F32))
        carry = lax.fori_loop(0, i, lambda j, c: step(j, c, False), init)
        m_f, l_f, acc = step(i, carry, True)
        o_ref[...] = (acc / l_f).astype(o_ref.dtype)
        lse_ref[...] = _lanes(m_f + jnp.log(l_f))

    blk = (None, tq, LANES)
    full = (None, seq, LANES)
    in_specs = [pl.BlockSpec(blk, lambda b, h, i: (b, i, q0 + h)),
                pl.BlockSpec(full, lambda b, h, i: (b, 0, kv0 + 2 * h)),
                pl.BlockSpec(full, lambda b, h, i: (b, 0, kv0 + 2 * h + 1))]
    args = [qa, kva, kva]
    if gated:
        in_specs += [pl.BlockSpec(blk, lambda b, h, i: (b, i, 0)),
                     pl.BlockSpec((None, 8, seq), lambda b, h, i: (b, 0, 0))]
        args += list(gates)
    in_specs.append(pl.BlockSpec(memory_space=pl.ANY))
    args.append(mo)
    return pl.pallas_call(
        body, name=name, grid=(bsz, N_HEADS, n_q), in_specs=in_specs,
        out_specs=[pl.BlockSpec(blk, lambda b, h, i: (b, i, o0 + h)),
                   pl.BlockSpec((None, None, tq, LANES), lambda b, h, i: (b, h, i, 0))],
        out_shape=[jax.ShapeDtypeStruct(mo.shape, mo.dtype),
                   jax.ShapeDtypeStruct((bsz, N_HEADS, seq, LANES), F32)],
        input_output_aliases={len(args) - 1: 0},
        compiler_params=_cparams(("parallel", "parallel", "parallel")),
    )(*args)


def _attn_bwd_q_loop(qa, q0, kva, kv0, mo, dmo, o0, lse, gates, scale, out, out0, name, tq=None):
    bsz, seq, _ = qa.shape
    tq = ATTN_TILE if tq is None else tq
    n_q = seq // tq
    gated = gates is not None
    aliased = not isinstance(out, jax.ShapeDtypeStruct)

    def body(*refs):
        q_ref, k_ref, v_ref, o_ref, do_ref, lse_ref = refs[:6]
        dq_ref, delta_ref, dfq_ref = refs[-3:]
        h, i = pl.program_id(1), pl.program_id(2)
        q = (q_ref[...].astype(F32) * scale).astype(BF16)
        gate = (_gate_col(refs[6], h, tq), refs[7]) if gated else None
        do_v = do_ref[...]
        do_b = do_v.astype(BF16)
        delta = jnp.sum(do_v * o_ref[...].astype(F32), axis=1, keepdims=True)
        lse_v = lse_ref[:, 0:1]

        def step(j, carry, masked):
            acc, dfq = carry
            s, r = _block_logits(q, k_ref, gate, j, None, h, masked, tq)
            p = jnp.exp(s - lse_v)
            dp = lax.dot_general(do_b, v_ref[pl.ds(r, tq), :].astype(BF16), _DN["nt"], preferred_element_type=F32)
            ds = p * (dp - delta)
            acc = acc + jnp.dot(ds.astype(BF16), k_ref[pl.ds(r, tq), :].astype(BF16), preferred_element_type=F32)
            return acc, dfq + jnp.sum(ds, axis=1, keepdims=True)

        init = (jnp.zeros((tq, LANES), F32), jnp.zeros((tq, 1), F32))
        carry = lax.fori_loop(0, i, lambda j, c: step(j, c, False), init)
        acc, dfq = step(i, carry, True)
        dq_ref[...] = (acc * scale).astype(dq_ref.dtype)
        delta_ref[...] = _lanes(delta)
        dfq_ref[...] = _lanes(dfq)

    blk = (None, tq, LANES)
    full = (None, seq, LANES)
    stat = pl.BlockSpec((None, None, tq, LANES), lambda b, h, i: (b, h, i, 0))
    in_specs = [pl.BlockSpec(blk, lambda b, h, i: (b, i, q0 + h)),
                pl.BlockSpec(full, lambda b, h, i: (b, 0, kv0 + 2 * h)),
                pl.BlockSpec(full, lambda b, h, i: (b, 0, kv0 + 2 * h + 1)),
                pl.BlockSpec(blk, lambda b, h, i: (b, i, o0 + h)),
                pl.BlockSpec(blk, lambda b, h, i: (b, i, o0 + h)), stat]
    args = [qa, kva, kva, mo, dmo, lse]
    if gated:
        in_specs += [pl.BlockSpec(blk, lambda b, h, i: (b, i, 0)),
                     pl.BlockSpec((None, 8, seq), lambda b, h, i: (b, 0, 0))]
        args += list(gates)
    aliases = {}
    if aliased:
        in_specs.append(pl.BlockSpec(memory_space=pl.ANY))
        args.append(out)
        aliases = {len(args) - 1: 0}
    vec = jax.ShapeDtypeStruct((bsz, N_HEADS, seq, LANES), F32)
    return pl.pallas_call(
        body, name=name, grid=(bsz, N_HEADS, n_q), in_specs=in_specs,
        out_specs=[pl.BlockSpec(blk, lambda b, h, i: (b, i, out0 + h)), stat, stat],
        out_shape=[jax.ShapeDtypeStruct(out.shape, out.dtype), vec, vec],
        input_output_aliases=aliases,
        compiler_params=_cparams(("parallel", "parallel", "parallel")),
    )(*args)


def _attn_bwd_kv_loop(qa, q0, kva, kv0, dmo, o0, lse, delta, gates, scale, out, out0, name, tq=None):
    bsz, seq, _ = qa.shape
    tq = ATTN_TILE if tq is None else tq
    n_q = seq // tq
    gated = gates is not None
    aliased = not isinstance(out, jax.ShapeDtypeStruct)

    def body(*refs):
        q_ref, k_ref, v_ref, do_ref, lse_ref, dl_ref = refs[:6]
        dkv_ref, dfk_ref = refs[-2:]
        h, j = pl.program_id(1), pl.program_id(2)
        k_b = k_ref[...].astype(BF16)
        v_b = v_ref[...].astype(BF16)
        if gated:
            fc_ref, fr_ref = refs[6], refs[7]
            sub = lax.broadcasted_iota(jnp.int32, (8, tq), 0)
            frow = jnp.sum(jnp.where(sub == h, fr_ref[...], 0.0), axis=0, keepdims=True)
            lane = lax.broadcasted_iota(jnp.int32, (tq, LANES), 1)

        def step(i, carry, masked):
            dk, dv, dfk = carry
            r = pl.multiple_of(i * tq, tq)
            q = (q_ref[pl.ds(r, tq), :].astype(F32) * scale).astype(BF16)
            s = lax.dot_general(q, k_b, _DN["nt"], preferred_element_type=F32)
            if gated:
                fcol = jnp.sum(jnp.where(lane == h, fc_ref[pl.ds(r, tq), :], 0.0), axis=1, keepdims=True)
                s = s + (fcol - frow)
            if masked:
                r_i = lax.broadcasted_iota(jnp.int32, (tq, tq), 0)
                c_i = lax.broadcasted_iota(jnp.int32, (tq, tq), 1)
                s = jnp.where(c_i <= r_i, s, NEG)
            p = jnp.exp(s - lse_ref[pl.ds(r, tq), 0:1])
            do_b = do_ref[pl.ds(r, tq), :].astype(BF16)
            dp = lax.dot_general(do_b, v_b, _DN["nt"], preferred_element_type=F32)
            ds = p * (dp - dl_ref[pl.ds(r, tq), 0:1])
            dv = dv + lax.dot_general(p.astype(BF16), do_b, _DN["tn"], preferred_element_type=F32)
            dk = dk + lax.dot_general(ds.astype(BF16), q, _DN["tn"], preferred_element_type=F32)
            return dk, dv, dfk - jnp.sum(ds, axis=0, keepdims=True)

        init = (jnp.zeros((tq, LANES), F32), jnp.zeros((tq, LANES), F32), jnp.zeros((1, tq), F32))
        carry = step(j, init, True)
        dk, dv, dfk = lax.fori_loop(j + 1, n_q, lambda i, c: step(i, c, False), carry)
        dkv_ref[:, 0:LANES] = dk.astype(dkv_ref.dtype)
        dkv_ref[:, LANES:2 * LANES] = dv.astype(dkv_ref.dtype)
        dfk_ref[...] = dfk

    blk = (None, tq, LANES)
    full = (None, seq, LANES)
    stat = pl.BlockSpec((None, None, seq, LANES), lambda b, h, j: (b, h, 0, 0))
    in_specs = [pl.BlockSpec(full, lambda b, h, j: (b, 0, q0 + h)),
                pl.BlockSpec(blk, lambda b, h, j: (b, j, kv0 + 2 * h)),
                pl.BlockSpec(blk, lambda b, h, j: (b, j, kv0 + 2 * h + 1)),
                pl.BlockSpec(full, lambda b, h, j: (b, 0, o0 + h)), stat, stat]
    args = [qa, kva, kva, dmo, lse, delta]
    if gated:
        in_specs += [pl.BlockSpec(full, lambda b, h, j: (b, 0, 0)),
                     pl.BlockSpec((None, 8, tq), lambda b, h, j: (b, 0, j))]
        args += list(gates)
    aliases = {}
    if aliased:
        in_specs.append(pl.BlockSpec(memory_space=pl.ANY))
        args.append(out)
        aliases = {len(args) - 1: 0}
    return pl.pallas_call(
        body, name=name, grid=(bsz, N_HEADS, n_q), in_specs=in_specs,
        out_specs=[pl.BlockSpec((None, tq, 2 * LANES), lambda b, h, j: (b, j, out0 + h)),
                   pl.BlockSpec((None, None, 1, tq), lambda b, h, j: (b, h, 0, j))],
        out_shape=[jax.ShapeDtypeStruct(out.shape, out.dtype), jax.ShapeDtypeStruct((bsz, N_HEADS, 1, seq), F32)],
        input_output_aliases=aliases,
        compiler_params=_cparams(("parallel", "parallel", "parallel")),
    )(*args)


def _gmlp_fn(uv, lng, lnb, ws, bst):
    u = jax.nn.gelu(uv[:, 0:GROUP_WIDTH])
    gv = jax.nn.gelu(uv[:, GROUP_WIDTH:2 * GROUP_WIDTH])
    mu = jnp.mean(gv, axis=-1, keepdims=True)
    vc = gv - mu
    var = jnp.mean(vc * vc, axis=-1, keepdims=True)
    vln = vc * lax.rsqrt(var + LN_EPS) * lng + lnb
    r_i = lax.broadcasted_iota(jnp.int32, (D_CHUNK, D_CHUNK), 0)
    c_i = lax.broadcasted_iota(jnp.int32, (D_CHUNK, D_CHUNK), 1)
    lane_g = lax.broadcasted_iota(jnp.int32, (D_CHUNK, GROUP_WIDTH), 1) // HEAD_DIM
    e_r = lax.broadcasted_iota(jnp.int32, (LANES, GROUP_WIDTH), 0)
    e_c = lax.broadcasted_iota(jnp.int32, (LANES, GROUP_WIDTH), 1)
    expand = (e_r == e_c // HEAD_DIM).astype(F32)
    mixed = jnp.dot(bst, expand, precision=HI, preferred_element_type=F32)
    for g in range(4):
        w = jnp.where(r_i >= c_i, ws[g], 0.0)
        mixed = mixed + jnp.where(lane_g == g, _bdot(w, vln, "nn"), 0.0)
    return u * mixed


def _gmlp_fwd(proj, mo, lng, lnb, ws, bst, name):
    bsz, seq, _ = proj.shape

    def body(p_ref, mo_any, lng_ref, lnb_ref, ws_ref, bst_ref, o_ref):
        del mo_any
        o_ref[...] = _gmlp_fn(p_ref[...], lng_ref[...], lnb_ref[...], ws_ref[...], bst_ref[...]).astype(o_ref.dtype)

    return pl.pallas_call(
        body, name=name, grid=(bsz, seq // D_CHUNK),
        in_specs=[pl.BlockSpec((None, D_CHUNK, 512), lambda b, s: (b, s, P_D // 512)),
                  pl.BlockSpec(memory_space=pl.ANY), _vec_spec(256), _vec_spec(256),
                  pl.BlockSpec((4, D_CHUNK, D_CHUNK), lambda b, s: (0, 0, 0)),
                  pl.BlockSpec((D_CHUNK, LANES), lambda b, s: (0, 0))],
        out_specs=pl.BlockSpec((None, D_CHUNK, GROUP_WIDTH), lambda b, s: (b, s, 1280 // GROUP_WIDTH)),
        out_shape=jax.ShapeDtypeStruct(mo.shape, mo.dtype),
        input_output_aliases={1: 0},
        compiler_params=_cparams(("parallel", "parallel")),
    )(proj, mo, lng, lnb, ws, bst)


def _gmlp_bwd(dmo, dproj, proj, lng, lnb, ws, bst, name):
    bsz, seq, _ = proj.shape

    def body(do_ref, dp_any, p_ref, lng_ref, lnb_ref, ws_ref, bst_ref, dp_ref, dlg_ref, dlb_ref, dws_ref, dbst_ref):
        del dp_any
        first = jnp.logical_and(pl.program_id(0) == 0, pl.program_id(1) == 0)

        @pl.when(first)
        def _():
            dlg_ref[...] = jnp.zeros_like(dlg_ref)
            dlb_ref[...] = jnp.zeros_like(dlb_ref)
            dws_ref[...] = jnp.zeros_like(dws_ref)
            dbst_ref[...] = jnp.zeros_like(dbst_ref)

        _, vjp = jax.vjp(_gmlp_fn, p_ref[...], lng_ref[...], lnb_ref[...], ws_ref[...], bst_ref[...])
        duv, dlg, dlb, dws, dbst = vjp(do_ref[...])
        dp_ref[...] = duv.astype(dp_ref.dtype)
        dlg_ref[...] += dlg
        dlb_ref[...] += dlb
        dws_ref[...] += dws
        dbst_ref[...] += dbst

    const2 = lambda shape: pl.BlockSpec(shape, lambda b, s: (0,) * len(shape))
    return pl.pallas_call(
        body, name=name, grid=(bsz, seq // D_CHUNK),
        in_specs=[pl.BlockSpec((None, D_CHUNK, GROUP_WIDTH), lambda b, s: (b, s, 1280 // GROUP_WIDTH)),
                  pl.BlockSpec(memory_space=pl.ANY),
                  pl.BlockSpec((None, D_CHUNK, 512), lambda b, s: (b, s, P_D // 512)),
                  _vec_spec(256), _vec_spec(256), const2((4, D_CHUNK, D_CHUNK)), const2((D_CHUNK, LANES))],
        out_specs=[pl.BlockSpec((None, D_CHUNK, 512), lambda b, s: (b, s, P_D // 512)),
                   _vec_spec(256), _vec_spec(256), const2((4, D_CHUNK, D_CHUNK)), const2((D_CHUNK, LANES))],
        out_shape=[jax.ShapeDtypeStruct(dproj.shape, dproj.dtype), jax.ShapeDtypeStruct((1, 256), F32),
                   jax.ShapeDtypeStruct((1, 256), F32), jax.ShapeDtypeStruct((4, D_CHUNK, D_CHUNK), F32),
                   jax.ShapeDtypeStruct((D_CHUNK, LANES), F32)],
        input_output_aliases={1: 0},
        compiler_params=_cparams(("arbitrary", "arbitrary")),
    )(dmo, dproj, proj, lng, lnb, ws, bst)


def _ada_fwd(c_all, ada_w, name):
    n_b = c_all.shape[0]
    depth, d, cols = ada_w.shape

    def body(c_ref, w_ref, o_ref):
        cv = c_ref[...]
        act = (cv * jax.nn.sigmoid(cv)).astype(BF16)
        o_ref[...] = jnp.dot(act, w_ref[...].astype(BF16), preferred_element_type=F32)

    return pl.pallas_call(
        body, name=name, grid=(depth,),
        in_specs=[pl.BlockSpec((n_b, d), lambda l: (0, 0)), pl.BlockSpec((None, d, cols), lambda l: (l, 0, 0))],
        out_specs=pl.BlockSpec((None, n_b, cols), lambda l: (l, 0, 0)),
        out_shape=jax.ShapeDtypeStruct((depth, n_b, cols), F32),
        compiler_params=_cparams(("parallel",)),
    )(c_all, ada_w)


def _ada_bwd(c_all, dmod_cols, dmod_full, name):
    n_b, d = c_all.shape
    depth, _, cols = dmod_cols.shape
    full = dmod_full.shape[-1]

    def body(c_ref, dm_ref, df_ref, gw_ref, gb_ref):
        cv = c_ref[...]
        act = (cv * jax.nn.sigmoid(cv)).astype(BF16)
        gw_ref[...] = lax.dot_general(act, dm_ref[...].astype(BF16), (((0,), (0,)), ((), ())),
                                      preferred_element_type=F32)
        gb_ref[...] = jnp.sum(df_ref[...], axis=0, keepdims=True)

    return pl.pallas_call(
        body, name=name, grid=(depth,),
        in_specs=[pl.BlockSpec((n_b, d), lambda l: (0, 0)), pl.BlockSpec((None, n_b, cols), lambda l: (l, 0, 0)),
                  pl.BlockSpec((None, n_b, full), lambda l: (l, 0, 0))],
        out_specs=[pl.BlockSpec((None, d, cols), lambda l: (l, 0, 0)),
                   pl.BlockSpec((None, 1, full), lambda l: (l, 0, 0))],
        out_shape=[jax.ShapeDtypeStruct((depth, d, cols), F32), jax.ShapeDtypeStruct((depth, 1, full), F32)],
        compiler_params=_cparams(("parallel",)),
    )(c_all, dmod_cols, dmod_full)


def _adamw(gparts, own, w, m, v, name, layer=0, prev=None):
    n_p, rows, cols = gparts.shape
    assert w.shape[1:] == (rows, cols)
    tr = rows
    if rows > 512:
        tr = next(c for c in range(512, 7, -8) if rows % c == 0)
    has_own = own is not None
    n_prev = 0 if prev is None else 4

    def body(*refs):
        if has_own:
            slot_ref, refs = refs[0], refs[1:]
        g_ref = refs[0]
        own_ref = refs[1] if has_own else None
        w_ref, m_ref, v_ref = refs[1 + has_own:4 + has_own]
        go_ref, do_ref, mo_ref, vo_ref = refs[4 + has_own + n_prev:]
        g = None
        for p in range(n_p):
            term = g_ref[p].astype(F32)
            if has_own:
                term = jnp.where(slot_ref[0] == p, own_ref[...].astype(F32), term)
            g = term if g is None else g + term
        m_new = ADAM_B1 * m_ref[...] + (1.0 - ADAM_B1) * g
        v_new = ADAM_B2 * v_ref[...] + (1.0 - ADAM_B2) * (g * g)
        m_hat = m_new / (1.0 - ADAM_B1 ** ADAM_STEP)
        v_hat = v_new / (1.0 - ADAM_B2 ** ADAM_STEP)
        go_ref[...] = g
        do_ref[...] = -ADAM_LR * (m_hat / (jnp.sqrt(v_hat) + ADAM_EPS) + ADAM_WD * w_ref[...])
        mo_ref[...] = m_new
        vo_ref[...] = v_new

    spec = pl.BlockSpec((None, tr, cols), lambda i, *_: (layer, i, 0))
    in_specs = [pl.BlockSpec((n_p, tr, cols), lambda i, *_: (0, i, 0))]
    args = [gparts]
    if has_own:
        in_specs.append(pl.BlockSpec((None, tr, cols), lambda i, slot: (slot[0], i, 0)))
        args.append(own[0])
    in_specs += [spec, spec, spec]
    args += [w, m, v]
    aliases = {}
    if prev is not None:
        aliases = {has_own + len(args) + k: k for k in range(4)}
        in_specs += [pl.BlockSpec(memory_space=pl.ANY)] * 4
        args += list(prev)
    shp = jax.ShapeDtypeStruct(w.shape, F32)
    out_specs, out_shape = [spec, spec, spec, spec], [shp, shp, shp, shp]
    if not has_own:
        return pl.pallas_call(
            body, name=name, grid=(rows // tr,), in_specs=in_specs, out_specs=out_specs, out_shape=out_shape,
            input_output_aliases=aliases, compiler_params=_cparams(("parallel",)),
        )(*args)
    return pl.pallas_call(
        body, name=name, out_shape=out_shape, input_output_aliases=aliases,
        grid_spec=pltpu.PrefetchScalarGridSpec(num_scalar_prefetch=1, grid=(rows // tr,), in_specs=in_specs,
                                               out_specs=out_specs),
        compiler_params=_cparams(("parallel",)),
    )(jnp.reshape(own[1], (1,)).astype(jnp.int32), *args)


def _sum_parts(parts, name):
    n_p, rows, cols = parts.shape
    tr = 256 if rows % 256 == 0 else rows

    def body(p_ref, o_ref):
        acc = p_ref[0]
        for p in range(1, n_p):
            acc = acc + p_ref[p]
        o_ref[...] = acc

    return pl.pallas_call(
        body, name=name, grid=(rows // tr,),
        in_specs=[pl.BlockSpec((n_p, tr, cols), lambda i: (0, i, 0))],
        out_specs=pl.BlockSpec((tr, cols), lambda i: (i, 0)),
        out_shape=jax.ShapeDtypeStruct((rows, cols), F32),
        compiler_params=_cparams(("parallel",)),
    )(parts)


def _all_gather(arrs, name):
    n = len(arrs)

    def body(*refs):
        in_refs, out_refs = refs[:n], refs[n:2 * n]
        send_sems, recv_sems, loc_sems = refs[2 * n:]
        x, y, c = lax.axis_index("x"), lax.axis_index("y"), lax.axis_index("c")
        me, sibling = (x, y, c), (x, y, 1 - c)
        chips = [(1 - x, y), (x, 1 - y), (1 - x, 1 - y)]

        def copy(a, k, block, to, src=None):
            slot = out_refs[a].at[4 * block[0] + 2 * block[1] + block[2]]
            return pltpu.make_async_remote_copy(
                src_ref=slot if src is None else src, dst_ref=slot, send_sem=send_sems.at[a, k],
                recv_sem=recv_sems.at[a, k], device_id=to, device_id_type=pl.DeviceIdType.MESH)

        mine = [pltpu.make_async_copy(in_refs[a], out_refs[a].at[4 * x + 2 * y + c], loc_sems.at[a])
                for a in range(n)]
        for cp in mine:
            cp.start()
        first = []
        for a in range(n):
            first.append(copy(a, 0, me, sibling, src=in_refs[a]))
            first += [copy(a, 1 + j, me, (*chip, c), src=in_refs[a]) for j, chip in enumerate(chips)]
        for cp in first:
            cp.start()
        passed = []
        for j, chip in enumerate(chips):
            for a in range(n):
                copy(a, 1 + j, (*chip, c), me).wait_recv()
                cp = copy(a, 4 + j, (*chip, c), sibling)
                cp.start()
                passed.append(cp)
        for a in range(n):
            copy(a, 0, sibling, me).wait_recv()
        for j, chip in enumerate(chips):
            for a in range(n):
                copy(a, 4 + j, (*chip, 1 - c), me).wait_recv()
        for cp in first + passed:
            cp.wait_send()
        for cp in mine:
            cp.wait()

    any_spec = pl.BlockSpec(memory_space=pl.ANY)
    return pl.pallas_call(
        body, name=name, in_specs=[any_spec] * n, out_specs=[any_spec] * n,
        out_shape=[jax.ShapeDtypeStruct((N_DEV,) + a.shape, a.dtype) for a in arrs],
        scratch_shapes=[pltpu.SemaphoreType.DMA((n, N_DEV - 1)), pltpu.SemaphoreType.DMA((n, N_DEV - 1)),
                        pltpu.SemaphoreType.DMA((n,))],
    )(*arrs)


def _flip_peers():
    x, y, c = lax.axis_index("x"), lax.axis_index("y"), lax.axis_index("c")
    peers = []
    for fx, fy, fc in [(fx, fy, fc) for fx in (0, 1) for fy in (0, 1) for fc in (0, 1)][1:]:
        px, py, pc = (1 - x if fx else x), (1 - y if fy else y), (1 - c if fc else c)
        peers.append(((px, py, pc), 4 * px + 2 * py + pc))
    return 4 * x + 2 * y + c, peers


def _push_start(srcs, name, whole=False):
    n, n_peer = len(srcs), N_DEV - 1
    if whole:
        me_w = 4 * lax.axis_index("x") + 2 * lax.axis_index("y") + lax.axis_index("c")
        lands = [lax.dynamic_update_slice_in_dim(lax.empty((N_DEV,) + a.shape, a.dtype), a[None], me_w, axis=0)
                 for a in srcs]
    else:
        lands = [lax.empty(a.shape, a.dtype) for a in srcs]

    def body(*refs):
        src_refs, land_refs = refs[:n], refs[n:2 * n]
        send_sems, recv_sems = refs[2 * n], refs[2 * n + 1]
        token = refs[-1]
        me, peers = _flip_peers()
        for k, (dev, idx) in enumerate(peers):
            for a in range(n):
                pltpu.make_async_remote_copy(
                    src_ref=src_refs[a] if whole else src_refs[a].at[idx], dst_ref=land_refs[a].at[me],
                    send_sem=send_sems.at[a * n_peer + k], recv_sem=recv_sems.at[a * n_peer + k], device_id=dev,
                    device_id_type=pl.DeviceIdType.MESH).start()
        token[...] = jnp.zeros_like(token)

    hbm = pl.BlockSpec(memory_space=pltpu.HBM)
    sem = pl.BlockSpec(memory_space=pltpu.SEMAPHORE)
    arrs = list(srcs) + lands
    res = pl.pallas_call(
        body, name=name, in_specs=[hbm] * (2 * n),
        out_specs=(sem, sem, *[hbm] * (2 * n), pl.BlockSpec(memory_space=pltpu.VMEM)),
        out_shape=(pltpu.SemaphoreType.DMA((n * n_peer,)), pltpu.SemaphoreType.DMA((n * n_peer,)),
                   *[pltpu.HBM(a.shape, a.dtype) for a in arrs], jax.ShapeDtypeStruct((8, LANES), F32)),
        input_output_aliases={i: 2 + i for i in range(2 * n)},
        compiler_params=pltpu.CompilerParams(has_side_effects=pltpu.SideEffectType.DATAFLOW_SIDE_EFFECTING),
    )(*[pltpu.with_memory_space_constraint(a, pltpu.HBM) for a in arrs])
    return res[0], res[1], list(res[2:2 + n]), list(res[2 + n:2 + 2 * n]), res[-1]


def _push_wait(send_sems, recv_sems, srcs, lands, after, name, whole=False):
    n, n_peer = len(srcs), N_DEV - 1

    def body(*refs):
        src_refs, land_refs = refs[:n], refs[n:2 * n]
        send_s, recv_s = refs[2 * n], refs[2 * n + 1]
        _, peers = _flip_peers()
        for k, (dev, idx) in enumerate(peers):
            for a in range(n):
                cp = pltpu.make_async_remote_copy(
                    src_ref=src_refs[a] if whole else src_refs[a].at[idx], dst_ref=land_refs[a].at[idx],
                    send_sem=send_s.at[a * n_peer + k],
                    recv_sem=recv_s.at[a * n_peer + k], device_id=dev, device_id_type=pl.DeviceIdType.MESH)
                cp.wait_send()
                cp.wait_recv()

    hbm = pl.BlockSpec(memory_space=pltpu.HBM)
    sem = pl.BlockSpec(memory_space=pltpu.SEMAPHORE)
    arrs = list(srcs) + list(lands)
    res = pl.pallas_call(
        body, name=name, in_specs=[hbm] * (2 * n) + [sem, sem, pl.BlockSpec(memory_space=pl.ANY)],
        out_specs=tuple([hbm] * (2 * n)), out_shape=tuple(pltpu.HBM(a.shape, a.dtype) for a in arrs),
        input_output_aliases={i: i for i in range(2 * n)},
        compiler_params=pltpu.CompilerParams(has_side_effects=pltpu.SideEffectType.DATAFLOW_SIDE_EFFECTING),
    )(*arrs, send_sems, recv_sems, after)
    return list(res[:n]), list(res[n:])


def _ffn_fwd(x, h, mod, w_in, w_out_after, lng, lnb, rows, tag, nxt):
    bsz, seq, d = x.shape
    t = bsz * seq
    if h is None:
        h = _modulate(x, mod, rows[0], rows[1], f"modulate_{tag}")
    z, a = _ffn_in_swiglu(h.reshape(t, d), w_in, f"ffn_in_{tag}")
    f = _matmul_groupsum(a, w_out_after(a), out_dtype=F32, tm=512, name=f"ffn_out_{tag}").reshape(bsz, seq, d)
    y, h_next = _res_ln(x, f, mod, lng, lnb, rows[2], 0.5, f"res_ln_{tag}", nxt)
    return y, h_next, (x, h, z, a, f)


def _tied(mod, tie):
    return mod if tie is None else mod + tie


def _ffn_bwd(dy, saved, mod, w_in, w_out, lng, lnb, rows, tag, ready):
    x, h, z, a, f = saved
    bsz, seq, d = x.shape
    t = bsz * seq
    dx_res, df, dgate, dlg, dlb = _res_ln_bwd(dy, x, f, mod, lng, lnb, rows[2], 0.5, f"res_ln_bwd_{tag}")
    df2 = df.reshape(1, t, d)
    dw_out = _matmul(a, df2, mode="tn", group_out=True, out_dtype=BF16, tm=a.shape[2], tk=min(t, 2048),
                     name=f"ffn_out_dw_{tag}")
    tie_out = ready(f"{tag}_out", dw_out)
    dz = _ffn_out_dx_swiglu(df.reshape(t, d), w_out, z, f"ffn_out_dx_{tag}").reshape(N_DEV, t, -1)
    dw_in = _matmul(dz, h.reshape(1, t, d), mode="tn", group_out=True, out_dtype=BF16, tm=dz.shape[2],
                    tk=min(t, 2048), name=f"ffn_in_dw_{tag}")
    tie_in = ready(f"{tag}_in", dw_in)
    dh = _matmul_groupsum(dz, w_in, out_dtype=F32, tm=512, name=f"ffn_in_dx_{tag}").reshape(bsz, seq, d)
    dx, dsh, dsc = _modulate_bwd(dh, x, _tied(_tied(mod, tie_out), tie_in), dx_res, rows[1],
                                 f"modulate_bwd_{tag}")
    return dx, (dsh, dsc, dgate), dw_in, dw_out, dlg, dlb


def _mixer_fwd(x, h, mod, wts, small, lng, lnb, layer, tabs):
    bsz, seq, d = x.shape
    t = bsz * seq
    proj = _matmul(h.reshape(1, t, d), wts["mix_in"][None], mode="nn", group_out=True, out_dtype=F32, tm=512, tk=d,
                   name="mix_in").reshape(bsz, seq, PACK_W)
    mo, states = _hgrn_fwd(proj, small["lb_logits8"], small["hgrn_norm_g"], layer, f"hgrn_fwd_l{layer}")
    q, kv = _mla_pre(proj, small["q_norm_g"], small["kv_norm_g"], wts["uq"], wts["ukv"], tabs, "mla_pre")
    mla_scale = float((B_NOPE + B_ROPE) ** -0.5)
    mo, lse_b = _attn_fwd_loop(q, 0, kv, 0, mo, 2, None, mla_scale, "mla_attn_fwd")
    fg = _fox_gate(proj, small["fox_b_f"], "fox_gate")
    gates = (fg, jnp.swapaxes(fg[:, :, 0:8], 1, 2))
    fox_scale = float(HEAD_DIM ** -0.5)
    mo, lse_c = _attn_fwd_loop(proj, P_CQ // LANES, proj, P_CKV // LANES, mo, 6, gates, fox_scale, "fox_attn_fwd")
    mo = _gmlp_fwd(proj, mo, small["gmlp_ln_g"], small["gmlp_ln_b"], small["gmlp_w_s"], small["gmlp_bst"],
                   "gmlp_fwd")
    mixed = _matmul(mo.reshape(1, t, MO_W), wts["mix_out"][None], mode="nn", group_out=True, out_dtype=F32,
                    tm=1024, tk=MO_W, name="mix_out").reshape(bsz, seq, d)
    y, h_next = _res_ln(x, mixed, mod, lng, lnb, 5, 1.0, "res_ln_mix", (mod, 6, 7))
    return y, h_next, (x, h, proj, mo, states, q, kv, lse_b, gates, lse_c, mixed)


def _mixer_bwd(dy, saved, mod, wts, small, lng, lnb, layer, tabs, ready):
    x, h, proj, mo, states, q, kv, lse_b, gates, lse_c, mixed = saved
    bsz, seq, d = x.shape
    t = bsz * seq
    dx_res, dmixed, dgate, dlg, dlb = _res_ln_bwd(dy, x, mixed, mod, lng, lnb, 5, 1.0, "res_ln_bwd_mix")
    dm2 = dmixed.reshape(1, t, d)
    dmo = _matmul(dm2, wts["mix_out"][None], mode="nt", group_out=True, out_dtype=F32, tm=1024, tk=d,
                  name="mix_out_dx").reshape(bsz, seq, MO_W)
    dw_out = _matmul(mo.reshape(1, t, MO_W), dm2, mode="tn", group_out=True, out_dtype=F32, tm=512, tk=min(t, 2048),
                     name="mix_out_dw")[0]
    tie_out = ready("mix_out", dw_out)
    g = {}
    dproj, g["lb_logits8"], g["hgrn_norm_g"] = _hgrn_bwd(dmo, proj, states, small["lb_logits8"],
                                                         small["hgrn_norm_g"], layer, f"hgrn_bwd_l{layer}")
    mla_scale = float((B_NOPE + B_ROPE) ** -0.5)
    dq, delta_b, _ = _attn_bwd_q_loop(q, 0, kv, 0, mo, dmo, 2, lse_b, None, mla_scale,
                                 jax.ShapeDtypeStruct((bsz, seq, 512), F32), 0, "mla_attn_bwd_q")
    dkv, _ = _attn_bwd_kv_loop(q, 0, kv, 0, dmo, 2, lse_b, delta_b, None, mla_scale,
                          jax.ShapeDtypeStruct((bsz, seq, 1024), F32), 0, "mla_attn_bwd_kv")
    dproj, g["q_norm_g"], g["kv_norm_g"], g["uq"], g["ukv"] = _mla_pre_bwd(
        dq, dkv, dproj, proj, small["q_norm_g"], small["kv_norm_g"], wts["uq"], wts["ukv"], tabs, "mla_pre_bwd")
    fox_scale = float(HEAD_DIM ** -0.5)
    dproj, delta_c, dfq = _attn_bwd_q_loop(proj, P_CQ // LANES, proj, P_CKV // LANES, mo, dmo, 6, lse_c, gates,
                                      fox_scale, dproj, P_CQ // LANES, "fox_attn_bwd_q")
    dproj, dfk = _attn_bwd_kv_loop(proj, P_CQ // LANES, proj, P_CKV // LANES, dmo, 6, lse_c, delta_c, gates, fox_scale,
                              dproj, P_CKV // (2 * LANES), "fox_attn_bwd_kv")
    dfk_cols = jnp.pad(jnp.swapaxes(dfk[:, :, 0, :], 1, 2), ((0, 0), (0, 0), (0, LANES - N_HEADS)))
    dproj, g["fox_b_f"] = _fox_gate_bwd(dfq, dfk_cols, dproj, proj, small["fox_b_f"], "fox_gate_bwd")
    dproj, g["gmlp_ln_g"], g["gmlp_ln_b"], g["gmlp_w_s"], g["gmlp_bst"] = _gmlp_bwd(
        dmo, dproj, proj, small["gmlp_ln_g"], small["gmlp_ln_b"], small["gmlp_w_s"], small["gmlp_bst"], "gmlp_bwd")
    dp2 = dproj.reshape(1, t, PACK_W)
    dw_in = _matmul(h.reshape(1, t, d), dp2, mode="tn", group_out=True, out_dtype=BF16, tm=512, tk=1024,
                    name="mix_in_dw")[0]
    tie_in = ready("mix_in", dw_in)
    dh = _matmul(dp2, wts["mix_in"][None], mode="nt", group_out=True, out_dtype=F32, tm=512, tk=PACK_W,
                 name="mix_in_dx").reshape(bsz, seq, d)
    dx, dsh, dsc = _modulate_bwd(dh, x, _tied(_tied(mod, tie_out), tie_in), dx_res, 4, "modulate_bwd_mix")
    return dx, (dsh, dsc, dgate), dw_in, dw_out, g, dlg, dlb


def _small_views(p, layer):
    return {
        "lb_logits8": jnp.pad(p["hgrn_lb_logits"], ((0, 8 - DEPTH), (0, 0))),
        "hgrn_norm_g": p["hgrn_norm_g"][layer][None],
        "q_norm_g": p["mla_q_norm_g"][layer][None],
        "kv_norm_g": p["mla_kv_norm_g"][layer][None],
        "fox_b_f": jnp.pad(p["fox_b_f"][layer][None], ((0, 0), (0, LANES - N_HEADS))),
        "gmlp_ln_g": p["gmlp_ln_g"][layer][None],
        "gmlp_ln_b": p["gmlp_ln_b"][layer][None],
        "gmlp_w_s": p["gmlp_w_s"][layer],
        "gmlp_bst": jnp.pad(p["gmlp_b_s"][layer].T, ((0, 0), (0, LANES - N_HEADS))),
    }


def _local_step(x, mod, target, weights, p, grads_ready=None):
    bsz, seq, d = x.shape
    tabs = _rope_tables(seq)
    saved = []
    h = None
    for l in range(DEPTH):
        sm = _small_views(p, l)
        lng, lnb = p["ln_g"][l], p["ln_b"][l]
        x, h, s1 = _ffn_fwd(x, h, mod[l], weights(l, "ffn1_in", x)["ffn1_in"],
                            lambda a, l=l: weights(l, "ffn1_out", a)["ffn1_out"], lng[0:1], lnb[0:1], (0, 1, 2),
                            "ffn1", (mod[l], 3, 4))
        x, h, s2 = _mixer_fwd(x, h, mod[l], weights(l, "mix", x), sm, lng[1:2], lnb[1:2], l, tabs)
        x, h, s3 = _ffn_fwd(x, h, mod[l], weights(l, "ffn2_in", x)["ffn2_in"],
                            lambda a, l=l: weights(l, "ffn2_out", a)["ffn2_out"], lng[2:3], lnb[2:3], (6, 7, 8),
                            "ffn2", (mod[l + 1], 0, 1) if l + 1 < DEPTH else None)
        saved.append((s1, s2, s3))
    dx, loss = _loss_head(x, target, "loss_head")
    big, small, dmods = [None] * DEPTH, [None] * DEPTH, [None] * DEPTH
    ties = []

    def tied(a):
        for t in ties:
            a = a + t
        return a

    for l in reversed(range(DEPTH)):
        w = {}
        for part in ("ffn1_in", "ffn1_out", "mix", "ffn2_in", "ffn2_out"):
            w.update(weights(l, part, None))
        sm = _small_views(p, l)
        lng, lnb = p["ln_g"][l], p["ln_b"][l]
        s1, s2, s3 = saved[l]

        def ready(name, grad, l=l):
            tie = None if grads_ready is None else grads_ready(l, name, grad)
            if tie is not None:
                ties.append(tie)
            return tie

        dx, dm3, dwi2, dwo2, dlg2, dlb2 = _ffn_bwd(dx, s3, tied(mod[l]), w["ffn2_in"], w["ffn2_out"], lng[2:3],
                                                   lnb[2:3], (6, 7, 8), "ffn2", ready)
        dx, dm2, dwmi, dwmo, g, dlg1, dlb1 = _mixer_bwd(dx, s2, tied(mod[l]), w, sm, lng[1:2], lnb[1:2], l, tabs,
                                                        ready)
        dx, dm1, dwi1, dwo1, dlg0, dlb0 = _ffn_bwd(dx, s1, tied(mod[l]), w["ffn1_in"], w["ffn1_out"], lng[0:1],
                                                   lnb[0:1], (0, 1, 2), "ffn1", ready)
        dmods[l] = jnp.concatenate(list(dm1) + list(dm2) + list(dm3), axis=1)
        big[l] = {"ffn1_in": dwi1, "ffn1_out": dwo1, "ffn2_in": dwi2, "ffn2_out": dwo2, "mix_in": dwmi,
                  "mix_out": dwmo}
        g["ln_g"] = jnp.concatenate([dlg0, dlg1, dlg2], axis=0)
        g["ln_b"] = jnp.concatenate([dlb0, dlb1, dlb2], axis=0)
        small[l] = g
    return loss, dx, jnp.stack(dmods), big, small


_BIG = ("ffn1_in", "ffn1_out", "ffn2_in", "ffn2_out", "mix_in", "mix_out")


def _small_grad_list(small, loss):
    def both(fn):
        return jnp.stack([fn(small[l]) for l in range(DEPTH)])

    uq_src, ukv_src = _uq_src(), _ukv_src()
    return [
        ("loss", loss.reshape(1)),
        ("ln_g", both(lambda g: g["ln_g"])), ("ln_b", both(lambda g: g["ln_b"])),
        ("hgrn_lb_logits", small[0]["lb_logits8"][:DEPTH] + small[1]["lb_logits8"][:DEPTH]),
        ("hgrn_norm_g", both(lambda g: g["hgrn_norm_g"][0])),
        ("mla_q_norm_g", both(lambda g: g["q_norm_g"][0])),
        ("mla_kv_norm_g", both(lambda g: g["kv_norm_g"][0])),
        ("mla_w_uq", both(lambda g: _unpack_cols(g["uq"], uq_src, 384))),
        ("mla_w_ukv", both(lambda g: _unpack_cols(g["ukv"], ukv_src, 512))),
        ("fox_b_f", both(lambda g: g["fox_b_f"][0, :N_HEADS])),
        ("gmlp_ln_g", both(lambda g: g["gmlp_ln_g"][0])), ("gmlp_ln_b", both(lambda g: g["gmlp_ln_b"][0])),
        ("gmlp_w_s", both(lambda g: g["gmlp_w_s"])),
        ("gmlp_b_s", both(lambda g: g["gmlp_bst"][:, :N_HEADS].T)),
    ]


_PACK_COLS = 512


def _pack_small(items):
    flat = jnp.concatenate([a.reshape(-1).astype(F32) for _, a in items])
    n = flat.shape[0]
    tile = 8 * _PACK_COLS
    flat = jnp.pad(flat, (0, (-n) % tile))
    return flat.reshape(-1, _PACK_COLS)


def _unpack_small(buf, items):
    flat = buf.reshape(-1)
    out, off = {}, 0
    for name, a in items:
        out[name] = flat[off:off + a.size].reshape(a.shape)
        off += a.size
    return out


def _as2d(a):
    return a.reshape(-1, a.shape[-1])


def kernel(x, c, ada_w, ada_b, ln_g, ln_b, ffn1_w_in, ffn1_w_out, ffn2_w_in, ffn2_w_out, mix_w_in, mix_w_out, hgrn_lb_logits, hgrn_norm_g, mla_q_norm_g, mla_kv_norm_g, mla_w_uq, mla_w_ukv, fox_b_f, gmlp_ln_g, gmlp_ln_b, gmlp_w_s, gmlp_b_s, loss_target, m_ada_w, m_ada_b, m_ln_g, m_ln_b, m_ffn1_w_in, m_ffn1_w_out, m_ffn2_w_in, m_ffn2_w_out, m_mix_w_in, m_mix_w_out, m_hgrn_lb_logits, m_hgrn_norm_g, m_mla_q_norm_g, m_mla_kv_norm_g, m_mla_w_uq, m_mla_w_ukv, m_fox_b_f, m_gmlp_ln_g, m_gmlp_ln_b, m_gmlp_w_s, m_gmlp_b_s, v_ada_w, v_ada_b, v_ln_g, v_ln_b, v_ffn1_w_in, v_ffn1_w_out, v_ffn2_w_in, v_ffn2_w_out, v_mix_w_in, v_mix_w_out, v_hgrn_lb_logits, v_hgrn_norm_g, v_mla_q_norm_g, v_mla_kv_norm_g, v_mla_w_uq, v_mla_w_ukv, v_fox_b_f, v_gmlp_ln_g, v_gmlp_ln_b, v_gmlp_w_s, v_gmlp_b_s):
    names = ["ada_w", "ada_b", "ln_g", "ln_b", "ffn1_w_in", "ffn1_w_out", "ffn2_w_in", "ffn2_w_out", "mix_w_in",
             "mix_w_out", "hgrn_lb_logits", "hgrn_norm_g", "mla_q_norm_g", "mla_kv_norm_g", "mla_w_uq", "mla_w_ukv",
             "fox_b_f", "gmlp_ln_g", "gmlp_ln_b", "gmlp_w_s", "gmlp_b_s"]
    w = dict(zip(names, [ada_w, ada_b, ln_g, ln_b, ffn1_w_in, ffn1_w_out, ffn2_w_in, ffn2_w_out, mix_w_in, mix_w_out,
                         hgrn_lb_logits, hgrn_norm_g, mla_q_norm_g, mla_kv_norm_g, mla_w_uq, mla_w_ukv, fox_b_f,
                         gmlp_ln_g, gmlp_ln_b, gmlp_w_s, gmlp_b_s]))
    m = dict(zip(names, [m_ada_w, m_ada_b, m_ln_g, m_ln_b, m_ffn1_w_in, m_ffn1_w_out, m_ffn2_w_in, m_ffn2_w_out,
                         m_mix_w_in, m_mix_w_out, m_hgrn_lb_logits, m_hgrn_norm_g, m_mla_q_norm_g, m_mla_kv_norm_g,
                         m_mla_w_uq, m_mla_w_ukv, m_fox_b_f, m_gmlp_ln_g, m_gmlp_ln_b, m_gmlp_w_s, m_gmlp_b_s]))
    v = dict(zip(names, [v_ada_w, v_ada_b, v_ln_g, v_ln_b, v_ffn1_w_in, v_ffn1_w_out, v_ffn2_w_in, v_ffn2_w_out,
                         v_mix_w_in, v_mix_w_out, v_hgrn_lb_logits, v_hgrn_norm_g, v_mla_q_norm_g, v_mla_kv_norm_g,
                         v_mla_w_uq, v_mla_w_ukv, v_fox_b_f, v_gmlp_ln_g, v_gmlp_ln_b, v_gmlp_w_s, v_gmlp_b_s]))
    bsz, seq, d = x.shape
    me = 4 * lax.axis_index("x") + 2 * lax.axis_index("y") + lax.axis_index("c")
    mix_src, uq_src, ukv_src, mo_src = _mix_in_src(), _uq_src(), _ukv_src(), _mo_src()

    part_names = {"ffn1_in": ["ffn1_w_in"], "ffn1_out": ["ffn1_w_out"],
                  "mix": ["mix_w_in", "mix_w_out", "mla_w_uq", "mla_w_ukv"],
                  "ffn2_in": ["ffn2_w_in"], "ffn2_out": ["ffn2_w_out"]}
    group_of = {(l, part): (l, part) for l in range(DEPTH) for part in part_names}
    in_flight = {}
    transposed = ("ffn1_w_in", "ffn2_w_in")

    def start_group(key, behind=None):
        members = [(l, part) for (l, part), g in group_of.items() if g == key]
        labels = [(l, n) for l, part in members for n in part_names[part]]
        shards = []
        for l, n in labels:
            a = w[n][l]
            if n == "mix_w_in":
                a = _pack_cols(a, mix_src)
            if n in transposed:
                a = jnp.swapaxes(w[n], 1, 2)[l]
            shards.append(a.astype(BF16))
        if behind is not None:
            shards, _ = lax.optimization_barrier((shards, behind))
        in_flight[key] = (labels, _push_start(shards, f"gather_start_{key[0]}_{key[1]}", whole=True))

    keys_in_order = list(dict.fromkeys(group_of.values()))
    start_group(keys_in_order[0])

    gathered = _all_gather([c, ln_g, ln_b], "gather_inputs")
    c_all = gathered[0].reshape(N_DEV * bsz, d)
    ln_g_full = jnp.moveaxis(gathered[1], 0, 2).reshape(DEPTH, 3, d)
    ln_b_full = jnp.moveaxis(gathered[2], 0, 2).reshape(DEPTH, 3, d)

    mod_cols = _ada_fwd(c_all, ada_w, "ada_fwd")
    mod_all, = _all_gather([mod_cols], "gather_mod")
    mod_mine = lax.dynamic_slice_in_dim(mod_all, me * bsz, bsz, axis=2)
    mod = jnp.moveaxis(mod_mine, 0, 2).reshape(DEPTH, bsz, N_MOD * d) + ada_b[:, None, :]
    for key in keys_in_order[1:]:
        start_group(key, behind=mod)
    tie = sum(h[-1][0, 0] for _, h in in_flight.values())
    mod = mod.reshape(DEPTH, bsz, N_MOD, d) + tie

    arrived, laid_out = {}, {}

    def weights(l, part, after):
        if (l, part) not in laid_out:
            laid_out[(l, part)] = lay_out(l, part, after)
        return laid_out[(l, part)]

    def lay_out(l, part, after):
        key = group_of[(l, part)]
        if key not in arrived:
            labels, (send_sems, recv_sems, srcs, lands, _) = in_flight[key]
            _, lands = _push_wait(send_sems, recv_sems, srcs, lands, after, f"gather_wait_{key[0]}_{key[1]}",
                                  whole=True)
            arrived[key] = dict(zip(labels, lands))
        gw = {n: arrived[key][(l, n)] for n in part_names[part]}
        if part.endswith("_in"):
            return {part: gw[part_names[part][0]]}
        if part.endswith("_out"):
            return {part: gw[part_names[part][0]].reshape(4, 704, d)}
        uq = jnp.moveaxis(gw["mla_w_uq"], 0, 1).reshape(256, 384)
        ukv = jnp.moveaxis(gw["mla_w_ukv"], 0, 1).reshape(128, 512)
        return {"mix_in": gw["mix_w_in"].reshape(d, PACK_W),
                "mix_out": _pack_cols(gw["mix_w_out"].reshape(d, d).T, mo_src).T,
                "uq": _pack_cols(uq, uq_src), "ukv": _pack_cols(ukv, ukv_src)}

    p = dict(w)
    p["ln_g"], p["ln_b"] = ln_g_full, ln_b_full
    def chunks(name, arr):
        if name in ("ffn1_in", "ffn2_in"):
            return arr
        if name in ("ffn1_out", "ffn2_out"):
            return arr.reshape(N_DEV, arr.shape[1] // 2, d)
        if name == "mix_in":
            return _unpack_cols(arr, mix_src, MIX_ORIG_W).reshape(N_DEV, d // N_DEV, MIX_ORIG_W)
        return _unpack_cols(arr.T, mo_src, d).T.astype(BF16).reshape(N_DEV, d // N_DEV, d)

    pending, started = {}, []

    def grads_ready(l, name, grad):
        pending[(name, l)] = chunks(name, grad)
        flush = name == "ffn1_in" if l > 0 else name in ("ffn2_in", "mix_out", "mix_in", "ffn1_out", "ffn1_in")
        if not flush:
            return None
        keys = sorted(pending)
        handles = _push_start([pending[k] for k in keys], f"push_start_{len(started)}")
        pending.clear()
        started.append((keys, handles, l == 0 and name.startswith("ffn1")))
        return handles[-1][0, 0]

    loss, grad_x, dmod, big, small = _local_step(x, mod, loss_target, weights, p, grads_ready)
    del big

    recv, out = {}, {}

    def arrive(n, after):
        keys, (send_sems, recv_sems, srcs, lands, _), _ = started[n]
        srcs, lands = _push_wait(send_sems, recv_sems, srcs, lands, after, f"push_wait_{n}")
        for k, src, land in zip(keys, srcs, lands):
            recv[k] = (land, src)

    big_of = {"ffn1_w_in": "ffn1_in", "ffn1_w_out": "ffn1_out", "ffn2_w_in": "ffn2_in", "ffn2_w_out": "ffn2_out",
              "mix_w_in": "mix_in", "mix_w_out": "mix_out"}
    chain = {name: None for name in big_of}

    def big_update(key, l):
        name = next(nm for nm, k in big_of.items() if k == key)
        parts, src = recv[(key, l)]
        view =(lambda a: jnp.swapaxes(a, 1, 2)) if name in transposed else (lambda a: a)
        chain[name] = _adamw(parts, (src, me), view(w[name]), view(m[name]), view(v[name]), f"adamw_{name}_l{l}",
                             layer=l, prev=chain[name])

    def update(name, grad):
        shape = w[name].shape
        as3 = lambda a: a.reshape(1, -1, shape[-1])
        res = _adamw(as3(grad), None, as3(w[name]), as3(m[name]), as3(v[name]), f"adamw_{name}")
        out[name] = tuple(r.reshape(shape) for r in res)

    for n, (keys, _, last) in enumerate(started):
        if not last:
            arrive(n, grad_x)
            for key, l in keys:
                big_update(key, l)

    dmod_flat = dmod.reshape(DEPTH, bsz, N_MOD * d)
    done = [r[0] for r in chain.values() if r is not None]
    if done:
        dmod_flat, _ = lax.optimization_barrier((dmod_flat, done))
    dmod_all, = _all_gather([dmod_flat], "gather_dmod")
    dmod_full = jnp.moveaxis(dmod_all, 0, 1).reshape(DEPTH, N_DEV * bsz, N_MOD * d)
    cols = ada_w.shape[2]
    dmod_cols = lax.dynamic_slice_in_dim(dmod_full, me * cols, cols, axis=2)
    g_ada_w, g_ada_b = _ada_bwd(c_all, dmod_cols, dmod_full, "ada_bwd")
    res = None
    for l in range(DEPTH):
        res = _adamw(g_ada_w[l][None], None, ada_w, m_ada_w, v_ada_w, f"adamw_ada_w_l{l}", layer=l, prev=res)
    out["ada_w"] = tuple(res)
    update("ada_b", g_ada_b.reshape(DEPTH, N_MOD * d))

    items = _small_grad_list(small, loss)
    packed, _ = lax.optimization_barrier((_pack_small(items), (grad_x, g_ada_b)))
    parts, = _all_gather([packed], "gather_small")
    sg = _unpack_small(_sum_parts(parts, "sum_small"), items)
    for name in ("ln_g", "ln_b"):
        update(name, lax.dynamic_slice_in_dim(sg[name], me * (d // N_DEV), d // N_DEV, axis=2))
    for name, width in (("mla_w_uq", 48), ("mla_w_ukv", 64)):
        update(name, lax.dynamic_slice_in_dim(sg[name], me * width, width, axis=2))
    for name in ("hgrn_lb_logits", "hgrn_norm_g", "mla_q_norm_g", "mla_kv_norm_g", "fox_b_f", "gmlp_ln_g",
                 "gmlp_ln_b", "gmlp_w_s", "gmlp_b_s"):
        update(name, sg[name])

    for n, (keys, _, last) in enumerate(started):
        if last:
            arrive(n, out["gmlp_w_s"][0])
            for key, l in keys:
                big_update(key, l)
    for name in big_of:
        out[name] = tuple(jnp.swapaxes(r, 1, 2) if name in transposed else r for r in chain[name])

    return (sg["loss"][0], grad_x, *[out[n][0] for n in names], *[out[n][1] for n in names],
            *[out[n][2] for n in names], *[out[n][3] for n in names])
```

```python
import functools

import numpy as np
import jax
import jax.numpy as jnp
from jax import lax
from jax.experimental import pallas as pl
from jax.experimental.pallas import tpu as pltpu

F32 = jnp.float32
BF16 = jnp.bfloat16
HI = lax.Precision.HIGHEST

D_MODEL = 1024
DEPTH = 2
GROUP_WIDTH = 256
N_HEADS = 4
HEAD_DIM = 64
A_CHUNK = 16
LB_FLOOR = 1e-30
B_NOPE = 64
B_ROPE = 32
ROPE_THETA = 10000.0
D_CHUNK = 128
D_FF = 2816
N_MOD = 9
ALPHA = (2 * DEPTH) ** 0.25
LN_EPS = 1e-5
RMS_EPS = 1e-6
ADAM_LR = 0.001
ADAM_B1 = 0.9
ADAM_B2 = 0.999
ADAM_EPS = 1e-08
ADAM_WD = 0.01
ADAM_STEP = 10

N_DEV = 8
LANES = 128
PACK_W = 3712
MO_W = 1536
VMEM_LIMIT = 56 * 1024 * 1024
NEG = -1e30
ATTN_TILE = 512

MIX_ORIG_W = 2724
O_BCQ, O_BCKV, O_BKR, O_CQ, O_CK, O_CV, O_CF, O_DU, O_DV = 1024, 1280, 1408, 1440, 1696, 1952, 2208, 2212, 2468
P_B, P_KR, P_CQ, P_CKV, P_D, P_CF = 1024, 1408, 1536, 2048, 3072, 3584


_DN = {"nn": (((1,), (0,)), ((), ())), "nt": (((1,), (1,)), ((), ())), "tn": (((0,), (0,)), ((), ()))}


def _raw_bdot(a, b, mode):
    return lax.dot_general(a.astype(BF16), b.astype(BF16), _DN[mode], preferred_element_type=F32)


@functools.partial(jax.custom_vjp, nondiff_argnums=(2,))
def _bdot(a, b, mode):
    return _raw_bdot(a, b, mode)


def _bdot_fwd(a, b, mode):
    return _raw_bdot(a, b, mode), (a, b)


def _bdot_bwd(mode, res, g):
    a, b = res
    if mode == "nn":
        return _raw_bdot(g, b, "nt"), _raw_bdot(a, g, "tn")
    if mode == "nt":
        return _raw_bdot(g, b, "nn"), _raw_bdot(g, a, "tn")
    return _raw_bdot(b, g, "nt"), _raw_bdot(a, g, "nn")


_bdot.defvjp(_bdot_fwd, _bdot_bwd)


def _cparams(sem):
    return pltpu.CompilerParams(dimension_semantics=sem, vmem_limit_bytes=VMEM_LIMIT)


def _mix_in_src():
    src = -np.ones(PACK_W, np.int64)
    src[0:P_KR] = np.arange(0, O_BKR)
    src[P_KR + 64:P_KR + 80] = O_BKR + np.arange(16)
    src[P_KR + 96:P_KR + 112] = O_BKR + 16 + np.arange(16)
    for h in range(N_HEADS):
        src[P_CQ + 128 * h:P_CQ + 128 * h + 64] = O_CQ + 64 * h + np.arange(64)
        src[P_CKV + 256 * h:P_CKV + 256 * h + 64] = O_CK + 64 * h + np.arange(64)
        src[P_CKV + 256 * h + 128:P_CKV + 256 * h + 192] = O_CV + 64 * h + np.arange(64)
    src[P_D:P_D + 512] = O_DU + np.arange(512)
    src[P_CF:P_CF + 4] = O_CF + np.arange(4)
    return src


def _uq_src():
    src = -np.ones(512, np.int64)
    for h in range(N_HEADS):
        src[128 * h:128 * h + 64] = 96 * h + np.arange(64)
        src[128 * h + 64:128 * h + 80] = 96 * h + 64 + np.arange(16)
        src[128 * h + 96:128 * h + 112] = 96 * h + 80 + np.arange(16)
    return src


def _ukv_src():
    src = -np.ones(1024, np.int64)
    for h in range(N_HEADS):
        src[256 * h:256 * h + 64] = 128 * h + np.arange(64)
        src[256 * h + 128:256 * h + 192] = 128 * h + 64 + np.arange(64)
    return src


def _mo_src():
    src = -np.ones(MO_W, np.int64)
    src[0:256] = np.arange(256)
    for g in range(2):
        for h in range(N_HEADS):
            src[256 + 512 * g + 128 * h:256 + 512 * g + 128 * h + 64] = 256 + 256 * g + 64 * h + np.arange(64)
    src[1280:1536] = 768 + np.arange(256)
    return src


def _runs(idx):
    runs, i = [], 0
    while i < len(idx):
        j = i + 1
        while j < len(idx) and ((idx[i] < 0 and idx[j] < 0) or (idx[i] >= 0 and idx[j] == idx[i] + j - i)):
            j += 1
        runs.append((int(idx[i]), j - i))
        i = j
    return runs


def _take_runs(w, idx):
    parts = [jnp.zeros(w.shape[:-1] + (n,), w.dtype) if s < 0 else lax.slice_in_dim(w, s, s + n, axis=w.ndim - 1)
             for s, n in _runs(idx)]
    return jnp.concatenate(parts, axis=-1)


def _pack_cols(w, src):
    return _take_runs(w, src)


def _unpack_cols(wp, src, n):
    dst = np.zeros(n, np.int64)
    dst[src[src >= 0]] = np.nonzero(src >= 0)[0]
    return _take_runs(wp, dst)


def _rope_tables(seq):
    half = B_ROPE // 2
    inv_freq = ROPE_THETA ** (-jnp.arange(half, dtype=F32) / half)
    ang = jnp.arange(seq).astype(F32)[:, None] * inv_freq[None, :]
    cos, sin = jnp.cos(ang), jnp.sin(ang)
    z16 = jnp.zeros((seq, 16), F32)
    c = jnp.concatenate([jnp.ones((seq, 64), F32), cos, z16, cos, z16], axis=1)
    s1 = jnp.concatenate([jnp.zeros((seq, 64), F32), -sin, z16, z16, z16], axis=1)
    s2 = jnp.concatenate([jnp.zeros((seq, 64), F32), z16, z16, sin, z16], axis=1)
    return c, s1, s2


def _matmul(a, b, *, mode, group_out, out_dtype, tm, tk, name):
    ga, gb = a.shape[0], b.shape[0]
    g_n = max(ga, gb)
    if mode == "tn":
        k_dim, m_dim = a.shape[1:]
    else:
        m_dim, k_dim = a.shape[1:]
    n_dim = b.shape[1] if mode == "nt" else b.shape[2]
    assert m_dim % tm == 0 and k_dim % tk == 0
    kt = k_dim // tk
    n_red = kt if group_out else g_n * kt
    g_out = g_n if group_out else 1

    def split(g, r):
        return (g, r) if group_out else (r // kt, r % kt)

    def a_map(g, i, r):
        gg, kk = split(g, r)
        gg = gg if ga > 1 else 0
        return (gg, kk, i) if mode == "tn" else (gg, i, kk)

    def b_map(g, i, r):
        gg, kk = split(g, r)
        gg = gg if gb > 1 else 0
        return (gg, 0, kk) if mode == "nt" else (gg, kk, 0)

    a_blk = (None, tk, tm) if mode == "tn" else (None, tm, tk)
    b_blk = (None, n_dim, tk) if mode == "nt" else (None, tk, n_dim)
    dn = _DN[mode]

    def body(a_ref, b_ref, o_ref, *scratch):
        part = lax.dot_general(a_ref[...].astype(BF16), b_ref[...].astype(BF16), dn, preferred_element_type=F32)
        if n_red == 1:
            o_ref[...] = part.astype(o_ref.dtype)
            return
        acc_ref, = scratch
        r = pl.program_id(2)

        @pl.when(r == 0)
        def _():
            acc_ref[...] = part

        @pl.when(r > 0)
        def _():
            acc_ref[...] += part

        @pl.when(r == n_red - 1)
        def _():
            o_ref[...] = acc_ref[...].astype(o_ref.dtype)

    return pl.pallas_call(
        body, name=name, grid=(g_out, m_dim // tm, n_red),
        in_specs=[pl.BlockSpec(a_blk, a_map), pl.BlockSpec(b_blk, b_map)],
        out_specs=pl.BlockSpec((None, tm, n_dim), lambda g, i, r: (g, i, 0)),
        out_shape=jax.ShapeDtypeStruct((g_out, m_dim, n_dim), out_dtype),
        scratch_shapes=[] if n_red == 1 else [pltpu.VMEM((tm, n_dim), F32)],
        compiler_params=_cparams(("parallel", "parallel", "arbitrary")),
    )(a, b)


def _matmul_groupsum(a, b, *, out_dtype, tm, name):
    g_n, m_dim, k_dim = a.shape
    n_dim = b.shape[2]
    assert m_dim % tm == 0 and b.shape[:2] == (g_n, k_dim)

    def body(a_ref, b_ref, o_ref):
        acc = jnp.dot(a_ref[0], b_ref[0], preferred_element_type=F32)
        for g in range(1, g_n):
            acc = acc + jnp.dot(a_ref[g], b_ref[g], preferred_element_type=F32)
        o_ref[...] = acc.astype(o_ref.dtype)

    return pl.pallas_call(
        body, name=name, grid=(m_dim // tm,),
        in_specs=[pl.BlockSpec((g_n, tm, k_dim), lambda i: (0, i, 0)),
                  pl.BlockSpec((g_n, k_dim, n_dim), lambda i: (0, 0, 0))],
        out_specs=pl.BlockSpec((tm, n_dim), lambda i: (i, 0)),
        out_shape=jax.ShapeDtypeStruct((m_dim, n_dim), out_dtype),
        compiler_params=_cparams(("parallel",)),
    )(a, b)


def _row_spec(ts, d):
    return pl.BlockSpec((None, ts, d), lambda b, s: (b, s, 0))


def _mod_spec(d):
    return pl.BlockSpec((None, N_MOD, d), lambda b, s: (b, 0, 0))


def _vec_spec(d):
    return pl.BlockSpec((1, d), lambda b, s: (0, 0))


def _bvec_spec(d):
    return pl.BlockSpec((None, 1, d), lambda b, s: (b, 0, 0))


def _modulate(x, mod, sh_row, sc_row, name, ts=512):
    bsz, seq, d = x.shape

    def body(x_ref, mod_ref, o_ref):
        sh = mod_ref[sh_row:sh_row + 1, :]
        sc = mod_ref[sc_row:sc_row + 1, :]
        o_ref[...] = (x_ref[...] * (1.0 + sc) + sh).astype(o_ref.dtype)

    return pl.pallas_call(
        body, name=name, grid=(bsz, seq // ts),
        in_specs=[_row_spec(ts, d), _mod_spec(d)], out_specs=_row_spec(ts, d),
        out_shape=jax.ShapeDtypeStruct((bsz, seq, d), BF16),
        compiler_params=_cparams(("parallel", "parallel")),
    )(x, mod)


def _modulate_bwd(dh, x, mod, dx_res, sc_row, name, ts=512):
    bsz, seq, d = x.shape

    def body(dh_ref, x_ref, mod_ref, dxr_ref, dx_ref, dsh_ref, dsc_ref):
        s = pl.program_id(1)
        sc = mod_ref[sc_row:sc_row + 1, :]
        dh_v = dh_ref[...]
        dx_ref[...] = dxr_ref[...] + dh_v * (1.0 + sc)
        psh = jnp.sum(dh_v, axis=0, keepdims=True)
        psc = jnp.sum(dh_v * x_ref[...], axis=0, keepdims=True)

        @pl.when(s == 0)
        def _():
            dsh_ref[...] = psh
            dsc_ref[...] = psc

        @pl.when(s > 0)
        def _():
            dsh_ref[...] += psh
            dsc_ref[...] += psc

    return pl.pallas_call(
        body, name=name, grid=(bsz, seq // ts),
        in_specs=[_row_spec(ts, d), _row_spec(ts, d), _mod_spec(d), _row_spec(ts, d)],
        out_specs=[_row_spec(ts, d), _bvec_spec(d), _bvec_spec(d)],
        out_shape=[jax.ShapeDtypeStruct((bsz, seq, d), F32), jax.ShapeDtypeStruct((bsz, 1, d), F32),
                   jax.ShapeDtypeStruct((bsz, 1, d), F32)],
        compiler_params=_cparams(("parallel", "arbitrary")),
    )(dh, x, mod, dx_res)


def _res_ln_fn(x, f, g, lng, lnb, cmul):
    r = ALPHA * x + (cmul * (1.0 + g)) * f
    mu = jnp.mean(r, axis=-1, keepdims=True)
    rc = r - mu
    var = jnp.mean(rc * rc, axis=-1, keepdims=True)
    return rc * lax.rsqrt(var + LN_EPS) * lng + lnb


def _res_ln(x, f, mod, lng, lnb, g_row, cmul, name, nxt=None, ts=512):
    bsz, seq, d = x.shape

    def body(*refs):
        x_ref, f_ref, mod_ref, lng_ref, lnb_ref = refs[:5]
        g = mod_ref[g_row:g_row + 1, :]
        y = _res_ln_fn(x_ref[...], f_ref[...], g, lng_ref[...], lnb_ref[...], cmul)
        if nxt is None:
            refs[5][...] = y
            return
        nmod_ref, o_ref, h_ref = refs[5:]
        o_ref[...] = y
        sh = nmod_ref[nxt[1]:nxt[1] + 1, :]
        sc = nmod_ref[nxt[2]:nxt[2] + 1, :]
        h_ref[...] = (y * (1.0 + sc) + sh).astype(h_ref.dtype)

    in_specs = [_row_spec(ts, d), _row_spec(ts, d), _mod_spec(d), _vec_spec(d), _vec_spec(d)]
    args = [x, f, mod, lng, lnb]
    out_specs, out_shape = [_row_spec(ts, d)], [jax.ShapeDtypeStruct((bsz, seq, d), F32)]
    if nxt is not None:
        in_specs.append(_mod_spec(d))
        args.append(nxt[0])
        out_specs.append(_row_spec(ts, d))
        out_shape.append(jax.ShapeDtypeStruct((bsz, seq, d), BF16))
    res = pl.pallas_call(
        body, name=name, grid=(bsz, seq // ts), in_specs=in_specs, out_specs=out_specs, out_shape=out_shape,
        compiler_params=_cparams(("parallel", "parallel")),
    )(*args)
    return (res[0], res[1]) if nxt is not None else (res[0], None)


def _res_ln_bwd(dy, x, f, mod, lng, lnb, g_row, cmul, name, ts=256):
    bsz, seq, d = x.shape

    def body(dy_ref, x_ref, f_ref, mod_ref, lng_ref, lnb_ref, dx_ref, df_ref, dg_ref, dlg_ref, dlb_ref):
        b, s = pl.program_id(0), pl.program_id(1)
        g = mod_ref[g_row:g_row + 1, :]
        _, vjp = jax.vjp(functools.partial(_res_ln_fn, cmul=cmul), x_ref[...], f_ref[...], g, lng_ref[...],
                         lnb_ref[...])
        dx, df, dg, dlg, dlb = vjp(dy_ref[...])
        dx_ref[...] = dx
        df_ref[...] = df.astype(df_ref.dtype)

        @pl.when(s == 0)
        def _():
            dg_ref[...] = dg

        @pl.when(s > 0)
        def _():
            dg_ref[...] += dg

        first = jnp.logical_and(b == 0, s == 0)

        @pl.when(first)
        def _():
            dlg_ref[...] = dlg
            dlb_ref[...] = dlb

        @pl.when(jnp.logical_not(first))
        def _():
            dlg_ref[...] += dlg
            dlb_ref[...] += dlb

    return pl.pallas_call(
        body, name=name, grid=(bsz, seq // ts),
        in_specs=[_row_spec(ts, d), _row_spec(ts, d), _row_spec(ts, d), _mod_spec(d), _vec_spec(d), _vec_spec(d)],
        out_specs=[_row_spec(ts, d), _row_spec(ts, d), _bvec_spec(d), _vec_spec(d), _vec_spec(d)],
        out_shape=[jax.ShapeDtypeStruct((bsz, seq, d), F32), jax.ShapeDtypeStruct((bsz, seq, d), BF16),
                   jax.ShapeDtypeStruct((bsz, 1, d), F32), jax.ShapeDtypeStruct((1, d), F32),
                   jax.ShapeDtypeStruct((1, d), F32)],
        compiler_params=_cparams(("arbitrary", "arbitrary")),
    )(dy, x, f, mod, lng, lnb)


def _loss_head(y, target, name, ts=512):
    bsz, seq, d = y.shape
    n_s = seq // ts

    def body(y_ref, t_ref, dy_ref, loss_ref, acc_ref):
        b, s = pl.program_id(0), pl.program_id(1)
        err = y_ref[...] - t_ref[...]
        dy_ref[...] = err * (1.0 / d)
        part = jnp.sum(err * err, axis=0, keepdims=True)
        first = jnp.logical_and(b == 0, s == 0)

        @pl.when(first)
        def _():
            acc_ref[...] = part

        @pl.when(jnp.logical_not(first))
        def _():
            acc_ref[...] += part

        @pl.when(jnp.logical_and(b == bsz - 1, s == n_s - 1))
        def _():
            loss_ref[...] = jnp.sum(acc_ref[...], axis=1, keepdims=True) * (0.5 / d)

    return pl.pallas_call(
        body, name=name, grid=(bsz, n_s),
        in_specs=[_row_spec(ts, d), _row_spec(ts, d)],
        out_specs=[_row_spec(ts, d), pl.BlockSpec((1, 1), lambda b, s: (0, 0))],
        out_shape=[jax.ShapeDtypeStruct((bsz, seq, d), F32), jax.ShapeDtypeStruct((1, 1), F32)],
        scratch_shapes=[pltpu.VMEM((1, d), F32)],
        compiler_params=_cparams(("arbitrary", "arbitrary")),
    )(y, target)


def _ffn_in_swiglu(h, w_in_t, name, tm=1024):
    t, d = h.shape
    n_sh, w, _ = w_in_t.shape
    half = n_sh // 2

    def body(h_ref, w_ref, z_ref, a_ref):
        hv = h_ref[...]
        g = lax.dot_general(hv, w_ref[0], _DN["nt"], preferred_element_type=F32)
        u = lax.dot_general(hv, w_ref[1], _DN["nt"], preferred_element_type=F32)
        z_ref[0] = g.astype(z_ref.dtype)
        z_ref[1] = u.astype(z_ref.dtype)
        a_ref[...] = (g * jax.nn.sigmoid(g) * u).astype(a_ref.dtype)

    return pl.pallas_call(
        body, name=name, grid=(half, t // tm),
        in_specs=[pl.BlockSpec((tm, d), lambda g, i: (i, 0)),
                  pl.BlockSpec((2, None, w, d), lambda g, i: (0, g, 0, 0))],
        out_specs=[pl.BlockSpec((2, None, tm, w), lambda g, i: (0, g, i, 0)),
                   pl.BlockSpec((None, tm, w), lambda g, i: (g, i, 0))],
        out_shape=[jax.ShapeDtypeStruct((2, half, t, w), BF16), jax.ShapeDtypeStruct((half, t, w), BF16)],
        compiler_params=_cparams(("parallel", "parallel")),
    )(h, w_in_t.reshape(2, half, w, d))


def _ffn_out_dx_swiglu(df, w_out, z, name, tm=1024):
    t, d = df.shape
    half, w, _ = w_out.shape

    def body(df_ref, w_ref, z_ref, dz_ref):
        da = lax.dot_general(df_ref[...], w_ref[...], _DN["nt"], preferred_element_type=F32)
        g = z_ref[0].astype(F32)
        u = z_ref[1].astype(F32)
        sig = jax.nn.sigmoid(g)
        dz_ref[0] = (da * u * (sig * (1.0 + g * (1.0 - sig)))).astype(dz_ref.dtype)
        dz_ref[1] = (da * (g * sig)).astype(dz_ref.dtype)

    zspec = pl.BlockSpec((2, None, tm, w), lambda g, i: (0, g, i, 0))
    return pl.pallas_call(
        body, name=name, grid=(half, t // tm),
        in_specs=[pl.BlockSpec((tm, d), lambda g, i: (i, 0)), pl.BlockSpec((None, w, d), lambda g, i: (g, 0, 0)),
                  zspec],
        out_specs=zspec, out_shape=jax.ShapeDtypeStruct(z.shape, BF16),
        compiler_params=_cparams(("parallel", "parallel")),
    )(df, w_out, z)


def _log_sigmoid(x):
    return jnp.minimum(x, 0.0) - jnp.log(1.0 + jnp.exp(-jnp.abs(x)))


def _hgrn_consts():
    r = lax.broadcasted_iota(jnp.int32, (GROUP_WIDTH, GROUP_WIDTH), 0)
    c = lax.broadcasted_iota(jnp.int32, (GROUP_WIDTH, GROUP_WIDTH), 1)
    bd = (r // HEAD_DIM == c // HEAD_DIM).astype(F32)
    r16 = lax.broadcasted_iota(jnp.int32, (A_CHUNK, A_CHUNK), 0)
    c16 = lax.broadcasted_iota(jnp.int32, (A_CHUNK, A_CHUNK), 1)
    tril = (r16 >= c16).astype(F32)
    rows = lax.broadcasted_iota(jnp.int32, (A_CHUNK, GROUP_WIDTH), 0)
    return bd, tril, rows


def _hgrn_lb(logits8, layer):
    rows = lax.broadcasted_iota(jnp.int32, logits8.shape, 0)
    valid = rows < DEPTH
    mx = jnp.max(jnp.where(valid, logits8, NEG), axis=0, keepdims=True)
    e = jnp.where(valid, jnp.exp(logits8 - mx), 0.0)
    sm = e / jnp.sum(e, axis=0, keepdims=True)
    pick = jnp.logical_and(rows >= 1, rows <= layer)
    return jnp.sum(jnp.where(pick, sm, 0.0), axis=0, keepdims=True)


def _hgrn_chunk(aq, af, ai, ag, logits8, norm_g, st, *, layer, consts):
    bd, tril, rows = consts
    lb = _hgrn_lb(logits8, layer)
    la = jnp.log(jnp.maximum(lb, LB_FLOOR))
    b2 = jnp.log(1.0 - lb) + _log_sigmoid(af)
    log_f = jnp.maximum(la, b2) + jnp.log(1.0 + jnp.exp(-jnp.abs(la - b2)))
    k = 1.0 - jnp.exp(log_f)
    qf = aq * jax.nn.sigmoid(aq)
    g_cum = jnp.dot(tril, log_f, precision=HI, preferred_element_type=F32)

    c, w = A_CHUNK, GROUP_WIDTH

    def by_key(v):
        return jnp.broadcast_to(v[:, None, :], (c, c, w))

    def by_query(v):
        return jnp.broadcast_to(v[None, :, :], (c, c, w))

    s_i = lax.broadcasted_iota(jnp.int32, (c, c, w), 0)
    t_i = lax.broadcasted_iota(jnp.int32, (c, c, w), 1)
    rel = jnp.where(t_i >= s_i, by_query(g_cum) - by_key(g_cum), NEG)
    pairs = by_query(qf) * by_key(k) * jnp.exp(rel)
    a_all = _bdot(pairs.reshape(c * c, w), bd, "nn").reshape(c, c, w)
    o = jnp.sum(a_all * by_key(ai), axis=0)
    q_dec = qf * jnp.exp(g_cum)
    o = o + _bdot(q_dec, st, "nt")
    g_last = jnp.sum(jnp.where(rows == c - 1, g_cum, 0.0), axis=0, keepdims=True)
    k_end = k * jnp.exp(g_last - g_cum)
    kv = _bdot(ai, k_end, "tn")
    st_new = st * jnp.exp(g_last) + kv * bd
    ms = _bdot(o * o, bd, "nn") * (1.0 / HEAD_DIM)
    o = o * lax.rsqrt(ms + RMS_EPS) * norm_g
    return o * (ag * jax.nn.sigmoid(ag)), st_new


def _hgrn_fwd(proj, logits8, norm_g, layer, name, ts=256):
    bsz, seq, _ = proj.shape
    n_ch = ts // A_CHUNK

    def body(p_ref, lg_ref, ng_ref, o_ref, st_ref, st_scr):
        @pl.when(pl.program_id(1) == 0)
        def _():
            st_scr[...] = jnp.zeros_like(st_scr)

        consts = _hgrn_consts()
        logits_v, ng_v = lg_ref[...], ng_ref[...]

        def chunk(ci, carry):
            r = ci * A_CHUNK if isinstance(ci, int) else pl.multiple_of(ci * A_CHUNK, A_CHUNK)
            st = st_scr[...]
            st_ref[ci] = st
            o, st_new = _hgrn_chunk(
                p_ref[pl.ds(r, A_CHUNK), 0:256], p_ref[pl.ds(r, A_CHUNK), 256:512],
                p_ref[pl.ds(r, A_CHUNK), 512:768], p_ref[pl.ds(r, A_CHUNK), 768:1024],
                logits_v, ng_v, st, layer=layer, consts=consts)
            o_ref[pl.ds(r, A_CHUNK), :] = o.astype(o_ref.dtype)
            st_scr[...] = st_new
            return carry

        if n_ch <= 2:
            for c_static in range(n_ch):
                chunk(c_static, 0)
        else:
            lax.fori_loop(0, n_ch, chunk, 0, unroll=2)

    return pl.pallas_call(
        body, name=name, grid=(bsz, seq // ts),
        in_specs=[pl.BlockSpec((None, ts, 1024), lambda b, s: (b, s, 0)),
                  pl.BlockSpec((8, GROUP_WIDTH), lambda b, s: (0, 0)),
                  pl.BlockSpec((1, GROUP_WIDTH), lambda b, s: (0, 0))],
        out_specs=[pl.BlockSpec((None, ts, GROUP_WIDTH), lambda b, s: (b, s, 0)),
                   pl.BlockSpec((None, n_ch, GROUP_WIDTH, GROUP_WIDTH), lambda b, s: (b, s, 0, 0))],
        out_shape=[jax.ShapeDtypeStruct((bsz, seq, MO_W), BF16),
                   jax.ShapeDtypeStruct((bsz, seq // A_CHUNK, GROUP_WIDTH, GROUP_WIDTH), F32)],
        scratch_shapes=[pltpu.VMEM((GROUP_WIDTH, GROUP_WIDTH), F32)],
        compiler_params=_cparams(("parallel", "arbitrary")),
    )(proj, logits8, norm_g)


def _hgrn_bwd(dmo, proj, states, logits8, norm_g, layer, name, ts=256):
    bsz, seq, _ = proj.shape
    n_ch = ts // A_CHUNK
    n_s = seq // ts

    def body(do_ref, p_ref, st_ref, lg_ref, ng_ref, dp_ref, dlg_ref, dng_ref, dst_scr):
        b, s = pl.program_id(0), pl.program_id(1)

        @pl.when(s == 0)
        def _():
            dst_scr[...] = jnp.zeros_like(dst_scr)

        @pl.when(jnp.logical_and(b == 0, s == 0))
        def _():
            dlg_ref[...] = jnp.zeros_like(dlg_ref)
            dng_ref[...] = jnp.zeros_like(dng_ref)

        consts = _hgrn_consts()
        logits_v, ng_v = lg_ref[...], ng_ref[...]
        fn = functools.partial(_hgrn_chunk, layer=layer, consts=consts)

        def chunk(t, carry):
            ci = n_ch - 1 - t
            r = ci * A_CHUNK if isinstance(ci, int) else pl.multiple_of(ci * A_CHUNK, A_CHUNK)
            _, vjp = jax.vjp(
                fn, p_ref[pl.ds(r, A_CHUNK), 0:256], p_ref[pl.ds(r, A_CHUNK), 256:512],
                p_ref[pl.ds(r, A_CHUNK), 512:768], p_ref[pl.ds(r, A_CHUNK), 768:1024],
                logits_v, ng_v, st_ref[ci])
            daq, daf, dai, dag, dlg, dng, dst = vjp((do_ref[pl.ds(r, A_CHUNK), :], dst_scr[...]))
            dp_ref[pl.ds(r, A_CHUNK), 0:256] = daq.astype(dp_ref.dtype)
            dp_ref[pl.ds(r, A_CHUNK), 256:512] = daf.astype(dp_ref.dtype)
            dp_ref[pl.ds(r, A_CHUNK), 512:768] = dai.astype(dp_ref.dtype)
            dp_ref[pl.ds(r, A_CHUNK), 768:1024] = dag.astype(dp_ref.dtype)
            dlg_ref[...] += dlg
            dng_ref[...] += dng
            dst_scr[...] = dst
            return carry

        if n_ch <= 2:
            for c_static in range(n_ch):
                chunk(c_static, 0)
        else:
            lax.fori_loop(0, n_ch, chunk, 0, unroll=2)

    rev = lambda b, s: (b, n_s - 1 - s, 0)
    return pl.pallas_call(
        body, name=name, grid=(bsz, n_s),
        in_specs=[pl.BlockSpec((None, ts, GROUP_WIDTH), rev),
                  pl.BlockSpec((None, ts, 1024), rev),
                  pl.BlockSpec((None, n_ch, GROUP_WIDTH, GROUP_WIDTH), lambda b, s: (b, n_s - 1 - s, 0, 0)),
                  pl.BlockSpec((8, GROUP_WIDTH), lambda b, s: (0, 0)),
                  pl.BlockSpec((1, GROUP_WIDTH), lambda b, s: (0, 0))],
        out_specs=[pl.BlockSpec((None, ts, 1024), rev),
                   pl.BlockSpec((8, GROUP_WIDTH), lambda b, s: (0, 0)),
                   pl.BlockSpec((1, GROUP_WIDTH), lambda b, s: (0, 0))],
        out_shape=[jax.ShapeDtypeStruct((bsz, seq, PACK_W), BF16),
                   jax.ShapeDtypeStruct((8, GROUP_WIDTH), F32), jax.ShapeDtypeStruct((1, GROUP_WIDTH), F32)],
        scratch_shapes=[pltpu.VMEM((GROUP_WIDTH, GROUP_WIDTH), F32)],
        compiler_params=_cparams(("arbitrary", "arbitrary")),
    )(dmo, proj, states, logits8, norm_g)


def _rms_fn(x, g):
    return x * lax.rsqrt(jnp.mean(x * x, axis=-1, keepdims=True) + RMS_EPS) * g


def _tile4(t):
    return jnp.concatenate([t, t, t, t], axis=1)


def _rope(x, c, s1, s2):
    w = x.shape[-1]
    return x * c + pltpu.roll(x, 32, axis=1) * s2 + pltpu.roll(x, w - 32, axis=1) * s1


def _rope_t(dy, c, s1, s2):
    w = dy.shape[-1]
    return dy * c + pltpu.roll(dy * s2, w - 32, axis=1) + pltpu.roll(dy * s1, 32, axis=1)


def _mla_pre(proj, qg, kvg, wq, wkv, tabs, name, ts=256):
    bsz, seq, _ = proj.shape

    def body(p_ref, qg_ref, kvg_ref, wq_ref, wkv_ref, c_ref, s1_ref, s2_ref, q_ref, kv_ref):
        nq = _rms_fn(p_ref[:, 0:256], qg_ref[...])
        nkv = _rms_fn(p_ref[:, 256:384], kvg_ref[...])
        c, s1, s2 = c_ref[...], s1_ref[...], s2_ref[...]
        qp = jnp.dot(nq.astype(BF16), wq_ref[...], preferred_element_type=F32)
        q_ref[...] = _rope(qp, _tile4(c), _tile4(s1), _tile4(s2)).astype(q_ref.dtype)
        kv = jnp.dot(nkv.astype(BF16), wkv_ref[...], preferred_element_type=F32)
        krr = _rope(p_ref[:, 384:512], c, s1, s2)
        zero = jnp.zeros_like(krr)
        kv_ref[...] = (kv + jnp.concatenate([krr, zero] * N_HEADS, axis=1)).astype(kv_ref.dtype)

    tab_spec = pl.BlockSpec((ts, LANES), lambda b, s: (s, 0))
    return pl.pallas_call(
        body, name=name, grid=(bsz, seq // ts),
        in_specs=[pl.BlockSpec((None, ts, 512), lambda b, s: (b, s, P_B // 512)),
                  _vec_spec(256), _vec_spec(128),
                  pl.BlockSpec((256, 512), lambda b, s: (0, 0)), pl.BlockSpec((128, 1024), lambda b, s: (0, 0)),
                  tab_spec, tab_spec, tab_spec],
        out_specs=[_row_spec(ts, 512), _row_spec(ts, 1024)],
        out_shape=[jax.ShapeDtypeStruct((bsz, seq, 512), BF16), jax.ShapeDtypeStruct((bsz, seq, 1024), BF16)],
        compiler_params=_cparams(("parallel", "parallel")),
    )(proj, qg, kvg, wq, wkv, *tabs)


def _mla_pre_bwd(dq, dkv, dproj, proj, qg, kvg, wq, wkv, tabs, name, ts=256):
    bsz, seq, _ = proj.shape

    def body(dq_ref, dkv_ref, dp_any, p_ref, qg_ref, kvg_ref, wq_ref, wkv_ref, c_ref, s1_ref, s2_ref,
             dp_ref, dqg_ref, dkvg_ref, dwq_ref, dwkv_ref):
        del dp_any
        first = jnp.logical_and(pl.program_id(0) == 0, pl.program_id(1) == 0)

        @pl.when(first)
        def _():
            dqg_ref[...] = jnp.zeros_like(dqg_ref)
            dkvg_ref[...] = jnp.zeros_like(dkvg_ref)
            dwq_ref[...] = jnp.zeros_like(dwq_ref)
            dwkv_ref[...] = jnp.zeros_like(dwkv_ref)

        c, s1, s2 = c_ref[...], s1_ref[...], s2_ref[...]
        nq, vjp_q = jax.vjp(_rms_fn, p_ref[:, 0:256], qg_ref[...])
        nkv, vjp_kv = jax.vjp(_rms_fn, p_ref[:, 256:384], kvg_ref[...])
        dqp = _rope_t(dq_ref[...], _tile4(c), _tile4(s1), _tile4(s2)).astype(BF16)
        dkv_v = dkv_ref[...]
        dkv_b = dkv_v.astype(BF16)
        tn = (((0,), (0,)), ((), ()))
        nt = (((1,), (1,)), ((), ()))
        dwq_ref[...] += lax.dot_general(nq.astype(BF16), dqp, tn, preferred_element_type=F32)
        dwkv_ref[...] += lax.dot_general(nkv.astype(BF16), dkv_b, tn, preferred_element_type=F32)
        dcq, dqg = vjp_q(lax.dot_general(dqp, wq_ref[...], nt, preferred_element_type=F32))
        dckv, dkvg = vjp_kv(lax.dot_general(dkv_b, wkv_ref[...], nt, preferred_element_type=F32))
        dqg_ref[...] += dqg
        dkvg_ref[...] += dkvg
        dk_sum = dkv_v[:, 0:128] + dkv_v[:, 256:384] + dkv_v[:, 512:640] + dkv_v[:, 768:896]
        lane = lax.broadcasted_iota(jnp.int32, dk_sum.shape, 1)
        dkr = jnp.where(lane >= 64, _rope_t(dk_sum, c, s1, s2), 0.0)
        dp_ref[:, 0:256] = dcq.astype(dp_ref.dtype)
        dp_ref[:, 256:384] = dckv.astype(dp_ref.dtype)
        dp_ref[:, 384:512] = dkr.astype(dp_ref.dtype)

    tab_spec = pl.BlockSpec((ts, LANES), lambda b, s: (s, 0))
    const = lambda shape: pl.BlockSpec(shape, lambda b, s: (0, 0))
    return pl.pallas_call(
        body, name=name, grid=(bsz, seq // ts),
        in_specs=[_row_spec(ts, 512), _row_spec(ts, 1024), pl.BlockSpec(memory_space=pl.ANY),
                  pl.BlockSpec((None, ts, 512), lambda b, s: (b, s, P_B // 512)),
                  _vec_spec(256), _vec_spec(128), const((256, 512)), const((128, 1024)),
                  tab_spec, tab_spec, tab_spec],
        out_specs=[pl.BlockSpec((None, ts, 512), lambda b, s: (b, s, P_B // 512)),
                   _vec_spec(256), _vec_spec(128), const((256, 512)), const((128, 1024))],
        out_shape=[jax.ShapeDtypeStruct(dproj.shape, dproj.dtype), jax.ShapeDtypeStruct((1, 256), F32),
                   jax.ShapeDtypeStruct((1, 128), F32), jax.ShapeDtypeStruct((256, 512), F32),
                   jax.ShapeDtypeStruct((128, 1024), F32)],
        input_output_aliases={2: 0},
        compiler_params=_cparams(("arbitrary", "arbitrary")),
    )(dq, dkv, dproj, proj, qg, kvg, wq, wkv, *tabs)


def _fox_gate(proj, bf, name):
    bsz, seq, _ = proj.shape
    n_blk = seq // LANES

    def body(x_ref, bf_ref, f_ref):
        r_i = lax.broadcasted_iota(jnp.int32, (LANES, LANES), 0)
        c_i = lax.broadcasted_iota(jnp.int32, (LANES, LANES), 1)
        tril = (r_i >= c_i).astype(F32)
        bias = bf_ref[...]

        def blk(i, carry):
            r = pl.multiple_of(i * LANES, LANES)
            lf = _log_sigmoid(x_ref[pl.ds(r, LANES), :] + bias)
            f_ref[pl.ds(r, LANES), :] = jnp.dot(tril, lf, precision=HI, preferred_element_type=F32) + carry
            return carry + jnp.sum(lf, axis=0, keepdims=True)

        lax.fori_loop(0, n_blk, blk, jnp.zeros((1, LANES), F32))

    return pl.pallas_call(
        body, name=name, grid=(bsz,),
        in_specs=[pl.BlockSpec((None, seq, LANES), lambda b: (b, 0, P_CF // LANES)),
                  pl.BlockSpec((1, LANES), lambda b: (0, 0))],
        out_specs=pl.BlockSpec((None, seq, LANES), lambda b: (b, 0, 0)),
        out_shape=jax.ShapeDtypeStruct((bsz, seq, LANES), F32),
        compiler_params=_cparams(("parallel",)),
    )(proj, bf)


def _fox_gate_bwd(dfq, dfk_cols, dproj, proj, bf, name):
    bsz, seq, _ = proj.shape
    n_blk = seq // LANES

    def body(dfq_ref, dfk_ref, dp_any, x_ref, bf_ref, dp_ref, dbf_ref):
        del dp_any

        @pl.when(pl.program_id(0) == 0)
        def _():
            dbf_ref[...] = jnp.zeros_like(dbf_ref)

        r_i = lax.broadcasted_iota(jnp.int32, (LANES, LANES), 0)
        c_i = lax.broadcasted_iota(jnp.int32, (LANES, LANES), 1)
        triu = (r_i <= c_i).astype(F32)
        bias = bf_ref[...]

        def blk(t, carry):
            tail, dbf = carry
            r = pl.multiple_of((n_blk - 1 - t) * LANES, LANES)
            dc = dfk_ref[pl.ds(r, LANES), :]
            for hd in range(N_HEADS):
                dc = dc + jnp.where(c_i == hd, dfq_ref[hd, pl.ds(r, LANES), :], 0.0)
            dlf = jnp.dot(triu, dc, precision=HI, preferred_element_type=F32) + tail
            dx = dlf * (1.0 - jax.nn.sigmoid(x_ref[pl.ds(r, LANES), :] + bias))
            dp_ref[pl.ds(r, LANES), :] = dx.astype(dp_ref.dtype)
            return tail + jnp.sum(dc, axis=0, keepdims=True), dbf + jnp.sum(dx, axis=0, keepdims=True)

        z = jnp.zeros((1, LANES), F32)
        _, dbf = lax.fori_loop(0, n_blk, blk, (z, z))
        dbf_ref[...] += dbf

    return pl.pallas_call(
        body, name=name, grid=(bsz,),
        in_specs=[pl.BlockSpec((None, N_HEADS, seq, LANES), lambda b: (b, 0, 0, 0)),
                  pl.BlockSpec((None, seq, LANES), lambda b: (b, 0, 0)), pl.BlockSpec(memory_space=pl.ANY),
                  pl.BlockSpec((None, seq, LANES), lambda b: (b, 0, P_CF // LANES)),
                  pl.BlockSpec((1, LANES), lambda b: (0, 0))],
        out_specs=[pl.BlockSpec((None, seq, LANES), lambda b: (b, 0, P_CF // LANES)),
                   pl.BlockSpec((1, LANES), lambda b: (0, 0))],
        out_shape=[jax.ShapeDtypeStruct(dproj.shape, dproj.dtype), jax.ShapeDtypeStruct((1, LANES), F32)],
        input_output_aliases={2: 0},
        compiler_params=_cparams(("arbitrary",)),
    )(dfq, dfk_cols, dproj, proj, bf)


def _gate_terms(fc_ref, fr_ref, h, tq, tk):
    lane = lax.broadcasted_iota(jnp.int32, (tq, LANES), 1)
    fcol = jnp.sum(jnp.where(lane == h, fc_ref[...], 0.0), axis=1, keepdims=True)
    sub = lax.broadcasted_iota(jnp.int32, (8, tk), 0)
    frow = jnp.sum(jnp.where(sub == h, fr_ref[...], 0.0), axis=0, keepdims=True)
    return fcol - frow


def _scores(q_ref, k_ref, gate_refs, scale, h, masked, tq, tk):
    q = (q_ref[...].astype(F32) * scale).astype(BF16)
    s = lax.dot_general(q, k_ref[...].astype(BF16), _DN["nt"], preferred_element_type=F32)
    if gate_refs is not None:
        s = s + _gate_terms(gate_refs[0], gate_refs[1], h, tq, tk)
    if masked is not False:
        r_i = lax.broadcasted_iota(jnp.int32, (tq, tk), 0)
        c_i = lax.broadcasted_iota(jnp.int32, (tq, tk), 1)
        keep = c_i <= r_i
        s = jnp.where(keep if masked is True else jnp.logical_or(jnp.logical_not(masked), keep), s, NEG)
    return s, q


def _lanes(col):
    return jnp.broadcast_to(col, (col.shape[0], LANES))


def _attn_fwd(qa, q0, kva, kv0, mo, o0, gates, scale, name, tq=None):
    bsz, seq, _ = qa.shape
    tq = ATTN_TILE if tq is None else tq
    n_q = seq // tq
    gated = gates is not None

    def body(*refs):
        q_ref, k_ref, v_ref = refs[:3]
        gate_refs = refs[3:5] if gated else None
        o_ref, lse_ref, m_s, l_s, acc_s = refs[-5:]
        h, i, j = pl.program_id(1), pl.program_id(2), pl.program_id(3)

        @pl.when(j == 0)
        def _():
            m_s[...] = jnp.full_like(m_s, NEG)
            l_s[...] = jnp.zeros_like(l_s)
            acc_s[...] = jnp.zeros_like(acc_s)

        def step(masked):
            s, _ = _scores(q_ref, k_ref, gate_refs, scale, h, masked, tq, tq)
            m_prev = m_s[...]
            m_new = jnp.maximum(m_prev, jnp.max(s, axis=1, keepdims=True))
            alpha = jnp.exp(m_prev - m_new)
            p = jnp.exp(s - m_new)
            l_s[...] = alpha * l_s[...] + jnp.sum(p, axis=1, keepdims=True)
            acc_s[...] = alpha * acc_s[...] + jnp.dot(p.astype(BF16), v_ref[...].astype(BF16),
                                                      preferred_element_type=F32)
            m_s[...] = m_new

        @pl.when(j <= i)
        def _():
            step(j == i)

        @pl.when(j == i)
        def _():
            o_ref[...] = (acc_s[...] / l_s[...]).astype(o_ref.dtype)
            lse_ref[...] = _lanes(m_s[...] + jnp.log(l_s[...]))

    blk = (None, tq, LANES)
    in_specs = [pl.BlockSpec(blk, lambda b, h, i, j: (b, i, q0 + h)),
                pl.BlockSpec(blk, lambda b, h, i, j: (b, jnp.minimum(j, i), kv0 + 2 * h)),
                pl.BlockSpec(blk, lambda b, h, i, j: (b, jnp.minimum(j, i), kv0 + 2 * h + 1))]
    args = [qa, kva, kva]
    if gated:
        in_specs += [pl.BlockSpec(blk, lambda b, h, i, j: (b, i, 0)),
                     pl.BlockSpec((None, 8, tq), lambda b, h, i, j: (b, 0, jnp.minimum(j, i)))]
        args += list(gates)
    in_specs.append(pl.BlockSpec(memory_space=pl.ANY))
    args.append(mo)
    return pl.pallas_call(
        body, name=name, grid=(bsz, N_HEADS, n_q, n_q), in_specs=in_specs,
        out_specs=[pl.BlockSpec(blk, lambda b, h, i, j: (b, i, o0 + h)),
                   pl.BlockSpec((None, None, tq, LANES), lambda b, h, i, j: (b, h, i, 0))],
        out_shape=[jax.ShapeDtypeStruct(mo.shape, mo.dtype),
                   jax.ShapeDtypeStruct((bsz, N_HEADS, seq, LANES), F32)],
        scratch_shapes=[pltpu.VMEM((tq, 1), F32), pltpu.VMEM((tq, 1), F32), pltpu.VMEM((tq, LANES), F32)],
        input_output_aliases={len(args) - 1: 0},
        compiler_params=_cparams(("parallel", "parallel", "parallel", "arbitrary")),
    )(*args)


def _attn_bwd_q(qa, q0, kva, kv0, mo, dmo, o0, lse, gates, scale, out, out0, name, tq=None):
    bsz, seq, _ = qa.shape
    tq = ATTN_TILE if tq is None else tq
    n_q = seq // tq
    gated = gates is not None
    aliased = not isinstance(out, jax.ShapeDtypeStruct)

    def body(*refs):
        q_ref, k_ref, v_ref, o_ref, do_ref, lse_ref = refs[:6]
        gate_refs = refs[6:8] if gated else None
        dq_ref, delta_ref, dfq_ref, acc_s, dl_s, df_s = refs[-6:]
        h, i, j = pl.program_id(1), pl.program_id(2), pl.program_id(3)

        @pl.when(j == 0)
        def _():
            acc_s[...] = jnp.zeros_like(acc_s)
            df_s[...] = jnp.zeros_like(df_s)
            dl_s[...] = jnp.sum(do_ref[...] * o_ref[...].astype(F32), axis=1, keepdims=True)

        def step(masked):
            s, _ = _scores(q_ref, k_ref, gate_refs, scale, h, masked, tq, tq)
            p = jnp.exp(s - lse_ref[:, 0:1])
            dp = lax.dot_general(do_ref[...].astype(BF16), v_ref[...].astype(BF16), _DN["nt"],
                                 preferred_element_type=F32)
            ds = p * (dp - dl_s[...])
            acc_s[...] += jnp.dot(ds.astype(BF16), k_ref[...].astype(BF16), preferred_element_type=F32)
            df_s[...] += jnp.sum(ds, axis=1, keepdims=True)

        @pl.when(j <= i)
        def _():
            step(j == i)

        @pl.when(j == i)
        def _():
            dq_ref[...] = (acc_s[...] * scale).astype(dq_ref.dtype)
            delta_ref[...] = _lanes(dl_s[...])
            dfq_ref[...] = _lanes(df_s[...])

    blk = (None, tq, LANES)
    col = pl.BlockSpec((None, None, tq, LANES), lambda b, h, i, j: (b, h, i, 0))
    in_specs = [pl.BlockSpec(blk, lambda b, h, i, j: (b, i, q0 + h)),
                pl.BlockSpec(blk, lambda b, h, i, j: (b, jnp.minimum(j, i), kv0 + 2 * h)),
                pl.BlockSpec(blk, lambda b, h, i, j: (b, jnp.minimum(j, i), kv0 + 2 * h + 1)),
                pl.BlockSpec(blk, lambda b, h, i, j: (b, i, o0 + h)),
                pl.BlockSpec(blk, lambda b, h, i, j: (b, i, o0 + h)), col]
    args = [qa, kva, kva, mo, dmo, lse]
    if gated:
        in_specs += [pl.BlockSpec(blk, lambda b, h, i, j: (b, i, 0)),
                     pl.BlockSpec((None, 8, tq), lambda b, h, i, j: (b, 0, jnp.minimum(j, i)))]
        args += list(gates)
    aliases = {}
    if aliased:
        in_specs.append(pl.BlockSpec(memory_space=pl.ANY))
        args.append(out)
        aliases = {len(args) - 1: 0}
    vec = jax.ShapeDtypeStruct((bsz, N_HEADS, seq, LANES), F32)
    return pl.pallas_call(
        body, name=name, grid=(bsz, N_HEADS, n_q, n_q), in_specs=in_specs,
        out_specs=[pl.BlockSpec(blk, lambda b, h, i, j: (b, i, out0 + h)), col, col],
        out_shape=[jax.ShapeDtypeStruct(out.shape, out.dtype), vec, vec],
        scratch_shapes=[pltpu.VMEM((tq, LANES), F32), pltpu.VMEM((tq, 1), F32), pltpu.VMEM((tq, 1), F32)],
        input_output_aliases=aliases,
        compiler_params=_cparams(("parallel", "parallel", "parallel", "arbitrary")),
    )(*args)


def _attn_bwd_kv(qa, q0, kva, kv0, dmo, o0, lse, delta, gates, scale, out, out0, name, tq=None):
    bsz, seq, _ = qa.shape
    tq = ATTN_TILE if tq is None else tq
    n_q = seq // tq
    gated = gates is not None
    aliased = not isinstance(out, jax.ShapeDtypeStruct)

    def body(*refs):
        q_ref, k_ref, v_ref, do_ref, lse_ref, dl_ref = refs[:6]
        gate_refs = refs[6:8] if gated else None
        dkv_ref, dfk_ref, dk_s, dv_s, df_s = refs[-5:]
        h, j, i = pl.program_id(1), pl.program_id(2), pl.program_id(3)

        @pl.when(i == 0)
        def _():
            dk_s[...] = jnp.zeros_like(dk_s)
            dv_s[...] = jnp.zeros_like(dv_s)
            df_s[...] = jnp.zeros_like(df_s)

        def step(masked):
            s, q = _scores(q_ref, k_ref, gate_refs, scale, h, masked, tq, tq)
            p = jnp.exp(s - lse_ref[:, 0:1])
            do_b = do_ref[...].astype(BF16)
            dp = lax.dot_general(do_b, v_ref[...].astype(BF16), _DN["nt"], preferred_element_type=F32)
            ds = p * (dp - dl_ref[:, 0:1])
            dv_s[...] += lax.dot_general(p.astype(BF16), do_b, _DN["tn"], preferred_element_type=F32)
            dk_s[...] += lax.dot_general(ds.astype(BF16), q, _DN["tn"], preferred_element_type=F32)
            df_s[...] -= jnp.sum(ds, axis=0, keepdims=True)

        @pl.when(i > j)
        def _():
            step(False)

        @pl.when(i == j)
        def _():
            step(True)

        @pl.when(i == n_q - 1)
        def _():
            dkv_ref[:, 0:LANES] = dk_s[...].astype(dkv_ref.dtype)
            dkv_ref[:, LANES:2 * LANES] = dv_s[...].astype(dkv_ref.dtype)
            dfk_ref[...] = df_s[...]

    blk = (None, tq, LANES)
    col = pl.BlockSpec((None, None, tq, LANES), lambda b, h, j, i: (b, h, jnp.maximum(i, j), 0))
    in_specs = [pl.BlockSpec(blk, lambda b, h, j, i: (b, jnp.maximum(i, j), q0 + h)),
                pl.BlockSpec(blk, lambda b, h, j, i: (b, j, kv0 + 2 * h)),
                pl.BlockSpec(blk, lambda b, h, j, i: (b, j, kv0 + 2 * h + 1)),
                pl.BlockSpec(blk, lambda b, h, j, i: (b, jnp.maximum(i, j), o0 + h)), col, col]
    args = [qa, kva, kva, dmo, lse, delta]
    if gated:
        in_specs += [pl.BlockSpec(blk, lambda b, h, j, i: (b, jnp.maximum(i, j), 0)),
                     pl.BlockSpec((None, 8, tq), lambda b, h, j, i: (b, 0, j))]
        args += list(gates)
    aliases = {}
    if aliased:
        in_specs.append(pl.BlockSpec(memory_space=pl.ANY))
        args.append(out)
        aliases = {len(args) - 1: 0}
    return pl.pallas_call(
        body, name=name, grid=(bsz, N_HEADS, n_q, n_q), in_specs=in_specs,
        out_specs=[pl.BlockSpec((None, tq, 2 * LANES), lambda b, h, j, i: (b, j, out0 + h)),
                   pl.BlockSpec((None, None, 1, tq), lambda b, h, j, i: (b, h, 0, j))],
        out_shape=[jax.ShapeDtypeStruct(out.shape, out.dtype), jax.ShapeDtypeStruct((bsz, N_HEADS, 1, seq), F32)],
        scratch_shapes=[pltpu.VMEM((tq, LANES), F32), pltpu.VMEM((tq, LANES), F32), pltpu.VMEM((1, tq), F32)],
        input_output_aliases=aliases,
        compiler_params=_cparams(("parallel", "parallel", "parallel", "arbitrary")),
    )(*args)


def _block_logits(q, k_ref, gate, j, scale_unused, h, masked, tq):
    del scale_unused
    r = pl.multiple_of(j * tq, tq)
    s = lax.dot_general(q, k_ref[pl.ds(r, tq), :].astype(BF16), _DN["nt"], preferred_element_type=F32)
    if gate is not None:
        fcol, fr_ref = gate
        sub = lax.broadcasted_iota(jnp.int32, (8, tq), 0)
        frow = jnp.sum(jnp.where(sub == h, fr_ref[:, pl.ds(r, tq)], 0.0), axis=0, keepdims=True)
        s = s + (fcol - frow)
    if masked:
        r_i = lax.broadcasted_iota(jnp.int32, (tq, tq), 0)
        c_i = lax.broadcasted_iota(jnp.int32, (tq, tq), 1)
        s = jnp.where(c_i <= r_i, s, NEG)
    return s, r


def _gate_col(fc_ref, h, tq):
    lane = lax.broadcasted_iota(jnp.int32, (tq, LANES), 1)
    return jnp.sum(jnp.where(lane == h, fc_ref[...], 0.0), axis=1, keepdims=True)


def _attn_fwd_loop(qa, q0, kva, kv0, mo, o0, gates, scale, name, tq=None):
    bsz, seq, _ = qa.shape
    tq = ATTN_TILE if tq is None else tq
    n_q = seq // tq
    gated = gates is not None

    def body(*refs):
        q_ref, k_ref, v_ref = refs[:3]
        o_ref, lse_ref = refs[-2:]
        h, i = pl.program_id(1), pl.program_id(2)
        q = (q_ref[...].astype(F32) * scale).astype(BF16)
        gate = (_gate_col(refs[3], h, tq), refs[4]) if gated else None

        def step(j, carry, masked):
            m_prev, l_prev, acc = carry
            s, r = _block_logits(q, k_ref, gate, j, None, h, masked, tq)
            m_new = jnp.maximum(m_prev, jnp.max(s, axis=1, keepdims=True))
            alpha = jnp.exp(m_prev - m_new)
            p = jnp.exp(s - m_new)
            l_new = alpha * l_prev + jnp.sum(p, axis=1, keepdims=True)
            acc = alpha * acc + jnp.dot(p.astype(BF16), v_ref[pl.ds(r, tq), :].astype(BF16),
                                        preferred_element_type=F32)
            return m_new, l_new, acc

        init = (jnp.full((tq, 1), NEG, F32), jnp.zeros((tq, 1), F32), jnp.zeros((tq, LANES), F32))
        carry = lax.fori_loop(0, i, lambda j, c: step(j, c, False), init)
        m_f, l_f, acc = step(i, carry, True)
        o_ref[...] = (acc / l_f).astype(o_ref.dtype)
        lse_ref[...] = _lanes(m_f + jnp.log(l_f))

    blk = (None, tq, LANES)
    full = (None, seq, LANES)
    in_specs = [pl.BlockSpec(blk, lambda b, h, i: (b, i, q0 + h)),
                pl.BlockSpec(full, lambda b, h, i: (b, 0, kv0 + 2 * h)),
                pl.BlockSpec(full, lambda b, h, i: (b, 0, kv0 + 2 * h + 1))]
    args = [qa, kva, kva]
    if gated:
        in_specs += [pl.BlockSpec(blk, lambda b, h, i: (b, i, 0)),
                     pl.BlockSpec((None, 8, seq), lambda b, h, i: (b, 0, 0))]
        args += list(gates)
    in_specs.append(pl.BlockSpec(memory_space=pl.ANY))
    args.append(mo)
    return pl.pallas_call(
        body, name=name, grid=(bsz, N_HEADS, n_q), in_specs=in_specs,
        out_specs=[pl.BlockSpec(blk, lambda b, h, i: (b, i, o0 + h)),
                   pl.BlockSpec((None, None, tq, LANES), lambda b, h, i: (b, h, i, 0))],
        out_shape=[jax.ShapeDtypeStruct(mo.shape, mo.dtype),
                   jax.ShapeDtypeStruct((bsz, N_HEADS, seq, LANES), F32)],
        input_output_aliases={len(args) - 1: 0},
        compiler_params=_cparams(("parallel", "parallel", "parallel")),
    )(*args)


def _attn_bwd_q_loop(qa, q0, kva, kv0, mo, dmo, o0, lse, gates, scale, out, out0, name, tq=None):
    bsz, seq, _ = qa.shape
    tq = ATTN_TILE if tq is None else tq
    n_q = seq // tq
    gated = gates is not None
    aliased = not isinstance(out, jax.ShapeDtypeStruct)

    def body(*refs):
        q_ref, k_ref, v_ref, o_ref, do_ref, lse_ref = refs[:6]
        dq_ref, delta_ref, dfq_ref = refs[-3:]
        h, i = pl.program_id(1), pl.program_id(2)
        q = (q_ref[...].astype(F32) * scale).astype(BF16)
        gate = (_gate_col(refs[6], h, tq), refs[7]) if gated else None
        do_v = do_ref[...]
        do_b = do_v.astype(BF16)
        delta = jnp.sum(do_v * o_ref[...].astype(F32), axis=1, keepdims=True)
        lse_v = lse_ref[:, 0:1]

        def step(j, carry, masked):
            acc, dfq = carry
            s, r = _block_logits(q, k_ref, gate, j, None, h, masked, tq)
            p = jnp.exp(s - lse_v)
            dp = lax.dot_general(do_b, v_ref[pl.ds(r, tq), :].astype(BF16), _DN["nt"], preferred_element_type=F32)
            ds = p * (dp - delta)
            acc = acc + jnp.dot(ds.astype(BF16), k_ref[pl.ds(r, tq), :].astype(BF16), preferred_element_type=F32)
            return acc, dfq + jnp.sum(ds, axis=1, keepdims=True)

        init = (jnp.zeros((tq, LANES), F32), jnp.zeros((tq, 1), F32))
        carry = lax.fori_loop(0, i, lambda j, c: step(j, c, False), init)
        acc, dfq = step(i, carry, True)
        dq_ref[...] = (acc * scale).astype(dq_ref.dtype)
        delta_ref[...] = _lanes(delta)
        dfq_ref[...] = _lanes(dfq)

    blk = (None, tq, LANES)
    full = (None, seq, LANES)
    stat = pl.BlockSpec((None, None, tq, LANES), lambda b, h, i: (b, h, i, 0))
    in_specs = [pl.BlockSpec(blk, lambda b, h, i: (b, i, q0 + h)),
                pl.BlockSpec(full, lambda b, h, i: (b, 0, kv0 + 2 * h)),
                pl.BlockSpec(full, lambda b, h, i: (b, 0, kv0 + 2 * h + 1)),
                pl.BlockSpec(blk, lambda b, h, i: (b, i, o0 + h)),
                pl.BlockSpec(blk, lambda b, h, i: (b, i, o0 + h)), stat]
    args = [qa, kva, kva, mo, dmo, lse]
    if gated:
        in_specs += [pl.BlockSpec(blk, lambda b, h, i: (b, i, 0)),
                     pl.BlockSpec((None, 8, seq), lambda b, h, i: (b, 0, 0))]
        args += list(gates)
    aliases = {}
    if aliased:
        in_specs.append(pl.BlockSpec(memory_space=pl.ANY))
        args.append(out)
        aliases = {len(args) - 1: 0}
    vec = jax.ShapeDtypeStruct((bsz, N_HEADS, seq, LANES), F32)
    return pl.pallas_call(
        body, name=name, grid=(bsz, N_HEADS, n_q), in_specs=in_specs,
        out_specs=[pl.BlockSpec(blk, lambda b, h, i: (b, i, out0 + h)), stat, stat],
        out_shape=[jax.ShapeDtypeStruct(out.shape, out.dtype), vec, vec],
        input_output_aliases=aliases,
        compiler_params=_cparams(("parallel", "parallel", "parallel")),
    )(*args)


def _attn_bwd_kv_loop(qa, q0, kva, kv0, dmo, o0, lse, delta, gates, scale, out, out0, name, tq=None):
    bsz, seq, _ = qa.shape
    tq = ATTN_TILE if tq is None else tq
    n_q = seq // tq
    gated = gates is not None
    aliased = not isinstance(out, jax.ShapeDtypeStruct)

    def body(*refs):
        q_ref, k_ref, v_ref, do_ref, lse_ref, dl_ref = refs[:6]
        dkv_ref, dfk_ref = refs[-2:]
        h, j = pl.program_id(1), pl.program_id(2)
        k_b = k_ref[...].astype(BF16)
        v_b = v_ref[...].astype(BF16)
        if gated:
            fc_ref, fr_ref = refs[6], refs[7]
            sub = lax.broadcasted_iota(jnp.int32, (8, tq), 0)
            frow = jnp.sum(jnp.where(sub == h, fr_ref[...], 0.0), axis=0, keepdims=True)
            lane = lax.broadcasted_iota(jnp.int32, (tq, LANES), 1)

        def step(i, carry, masked):
            dk, dv, dfk = carry
            r = pl.multiple_of(i * tq, tq)
            q = (q_ref[pl.ds(r, tq), :].astype(F32) * scale).astype(BF16)
            s = lax.dot_general(q, k_b, _DN["nt"], preferred_element_type=F32)
            if gated:
                fcol = jnp.sum(jnp.where(lane == h, fc_ref[pl.ds(r, tq), :], 0.0), axis=1, keepdims=True)
                s = s + (fcol - frow)
            if masked:
                r_i = lax.broadcasted_iota(jnp.int32, (tq, tq), 0)
                c_i = lax.broadcasted_iota(jnp.int32, (tq, tq), 1)
                s = jnp.where(c_i <= r_i, s, NEG)
            p = jnp.exp(s - lse_ref[pl.ds(r, tq), 0:1])
            do_b = do_ref[pl.ds(r, tq), :].astype(BF16)
            dp = lax.dot_general(do_b, v_b, _DN["nt"], preferred_element_type=F32)
            ds = p * (dp - dl_ref[pl.ds(r, tq), 0:1])
            dv = dv + lax.dot_general(p.astype(BF16), do_b, _DN["tn"], preferred_element_type=F32)
            dk = dk + lax.dot_general(ds.astype(BF16), q, _DN["tn"], preferred_element_type=F32)
            return dk, dv, dfk - jnp.sum(ds, axis=0, keepdims=True)

        init = (jnp.zeros((tq, LANES), F32), jnp.zeros((tq, LANES), F32), jnp.zeros((1, tq), F32))
        carry = step(j, init, True)
        dk, dv, dfk = lax.fori_loop(j + 1, n_q, lambda i, c: step(i, c, False), carry)
        dkv_ref[:, 0:LANES] = dk.astype(dkv_ref.dtype)
        dkv_ref[:, LANES:2 * LANES] = dv.astype(dkv_ref.dtype)
        dfk_ref[...] = dfk

    blk = (None, tq, LANES)
    full = (None, seq, LANES)
    stat = pl.BlockSpec((None, None, seq, LANES), lambda b, h, j: (b, h, 0, 0))
    in_specs = [pl.BlockSpec(full, lambda b, h, j: (b, 0, q0 + h)),
                pl.BlockSpec(blk, lambda b, h, j: (b, j, kv0 + 2 * h)),
                pl.BlockSpec(blk, lambda b, h, j: (b, j, kv0 + 2 * h + 1)),
                pl.BlockSpec(full, lambda b, h, j: (b, 0, o0 + h)), stat, stat]
    args = [qa, kva, kva, dmo, lse, delta]
    if gated:
        in_specs += [pl.BlockSpec(full, lambda b, h, j: (b, 0, 0)),
                     pl.BlockSpec((None, 8, tq), lambda b, h, j: (b, 0, j))]
        args += list(gates)
    aliases = {}
    if aliased:
        in_specs.append(pl.BlockSpec(memory_space=pl.ANY))
        args.append(out)
        aliases = {len(args) - 1: 0}
    return pl.pallas_call(
        body, name=name, grid=(bsz, N_HEADS, n_q), in_specs=in_specs,
        out_specs=[pl.BlockSpec((None, tq, 2 * LANES), lambda b, h, j: (b, j, out0 + h)),
                   pl.BlockSpec((None, None, 1, tq), lambda b, h, j: (b, h, 0, j))],
        out_shape=[jax.ShapeDtypeStruct(out.shape, out.dtype), jax.ShapeDtypeStruct((bsz, N_HEADS, 1, seq), F32)],
        input_output_aliases=aliases,
        compiler_params=_cparams(("parallel", "parallel", "parallel")),
    )(*args)


def _gmlp_fn(uv, lng, lnb, ws, bst):
    u = jax.nn.gelu(uv[:, 0:GROUP_WIDTH])
    gv = jax.nn.gelu(uv[:, GROUP_WIDTH:2 * GROUP_WIDTH])
    mu = jnp.mean(gv, axis=-1, keepdims=True)
    vc = gv - mu
    var = jnp.mean(vc * vc, axis=-1, keepdims=True)
    vln = vc * lax.rsqrt(var + LN_EPS) * lng + lnb
    r_i = lax.broadcasted_iota(jnp.int32, (D_CHUNK, D_CHUNK), 0)
    c_i = lax.broadcasted_iota(jnp.int32, (D_CHUNK, D_CHUNK), 1)
    lane_g = lax.broadcasted_iota(jnp.int32, (D_CHUNK, GROUP_WIDTH), 1) // HEAD_DIM
    e_r = lax.broadcasted_iota(jnp.int32, (LANES, GROUP_WIDTH), 0)
    e_c = lax.broadcasted_iota(jnp.int32, (LANES, GROUP_WIDTH), 1)
    expand = (e_r == e_c // HEAD_DIM).astype(F32)
    mixed = jnp.dot(bst, expand, precision=HI, preferred_element_type=F32)
    for g in range(4):
        w = jnp.where(r_i >= c_i, ws[g], 0.0)
        mixed = mixed + jnp.where(lane_g == g, _bdot(w, vln, "nn"), 0.0)
    return u * mixed


def _gmlp_fwd(proj, mo, lng, lnb, ws, bst, name):
    bsz, seq, _ = proj.shape

    def body(p_ref, mo_any, lng_ref, lnb_ref, ws_ref, bst_ref, o_ref):
        del mo_any
        o_ref[...] = _gmlp_fn(p_ref[...], lng_ref[...], lnb_ref[...], ws_ref[...], bst_ref[...]).astype(o_ref.dtype)

    return pl.pallas_call(
        body, name=name, grid=(bsz, seq // D_CHUNK),
        in_specs=[pl.BlockSpec((None, D_CHUNK, 512), lambda b, s: (b, s, P_D // 512)),
                  pl.BlockSpec(memory_space=pl.ANY), _vec_spec(256), _vec_spec(256),
                  pl.BlockSpec((4, D_CHUNK, D_CHUNK), lambda b, s: (0, 0, 0)),
                  pl.BlockSpec((D_CHUNK, LANES), lambda b, s: (0, 0))],
        out_specs=pl.BlockSpec((None, D_CHUNK, GROUP_WIDTH), lambda b, s: (b, s, 1280 // GROUP_WIDTH)),
        out_shape=jax.ShapeDtypeStruct(mo.shape, mo.dtype),
        input_output_aliases={1: 0},
        compiler_params=_cparams(("parallel", "parallel")),
    )(proj, mo, lng, lnb, ws, bst)


def _gmlp_bwd(dmo, dproj, proj, lng, lnb, ws, bst, name):
    bsz, seq, _ = proj.shape

    def body(do_ref, dp_any, p_ref, lng_ref, lnb_ref, ws_ref, bst_ref, dp_ref, dlg_ref, dlb_ref, dws_ref, dbst_ref):
        del dp_any
        first = jnp.logical_and(pl.program_id(0) == 0, pl.program_id(1) == 0)

        @pl.when(first)
        def _():
            dlg_ref[...] = jnp.zeros_like(dlg_ref)
            dlb_ref[...] = jnp.zeros_like(dlb_ref)
            dws_ref[...] = jnp.zeros_like(dws_ref)
            dbst_ref[...] = jnp.zeros_like(dbst_ref)

        _, vjp = jax.vjp(_gmlp_fn, p_ref[...], lng_ref[...], lnb_ref[...], ws_ref[...], bst_ref[...])
        duv, dlg, dlb, dws, dbst = vjp(do_ref[...])
        dp_ref[...] = duv.astype(dp_ref.dtype)
        dlg_ref[...] += dlg
        dlb_ref[...] += dlb
        dws_ref[...] += dws
        dbst_ref[...] += dbst

    const2 = lambda shape: pl.BlockSpec(shape, lambda b, s: (0,) * len(shape))
    return pl.pallas_call(
        body, name=name, grid=(bsz, seq // D_CHUNK),
        in_specs=[pl.BlockSpec((None, D_CHUNK, GROUP_WIDTH), lambda b, s: (b, s, 1280 // GROUP_WIDTH)),
                  pl.BlockSpec(memory_space=pl.ANY),
                  pl.BlockSpec((None, D_CHUNK, 512), lambda b, s: (b, s, P_D // 512)),
                  _vec_spec(256), _vec_spec(256), const2((4, D_CHUNK, D_CHUNK)), const2((D_CHUNK, LANES))],
        out_specs=[pl.BlockSpec((None, D_CHUNK, 512), lambda b, s: (b, s, P_D // 512)),
                   _vec_spec(256), _vec_spec(256), const2((4, D_CHUNK, D_CHUNK)), const2((D_CHUNK, LANES))],
        out_shape=[jax.ShapeDtypeStruct(dproj.shape, dproj.dtype), jax.ShapeDtypeStruct((1, 256), F32),
                   jax.ShapeDtypeStruct((1, 256), F32), jax.ShapeDtypeStruct((4, D_CHUNK, D_CHUNK), F32),
                   jax.ShapeDtypeStruct((D_CHUNK, LANES), F32)],
        input_output_aliases={1: 0},
        compiler_params=_cparams(("arbitrary", "arbitrary")),
    )(dmo, dproj, proj, lng, lnb, ws, bst)


def _ada_fwd(c_all, ada_w, name):
    n_b = c_all.shape[0]
    depth, d, cols = ada_w.shape

    def body(c_ref, w_ref, o_ref):
        cv = c_ref[...]
        act = (cv * jax.nn.sigmoid(cv)).astype(BF16)
        o_ref[...] = jnp.dot(act, w_ref[...].astype(BF16), preferred_element_type=F32)

    return pl.pallas_call(
        body, name=name, grid=(depth,),
        in_specs=[pl.BlockSpec((n_b, d), lambda l: (0, 0)), pl.BlockSpec((None, d, cols), lambda l: (l, 0, 0))],
        out_specs=pl.BlockSpec((None, n_b, cols), lambda l: (l, 0, 0)),
        out_shape=jax.ShapeDtypeStruct((depth, n_b, cols), F32),
        compiler_params=_cparams(("parallel",)),
    )(c_all, ada_w)


def _ada_bwd(c_all, dmod_cols, dmod_full, name):
    n_b, d = c_all.shape
    depth, _, cols = dmod_cols.shape
    full = dmod_full.shape[-1]

    def body(c_ref, dm_ref, df_ref, gw_ref, gb_ref):
        cv = c_ref[...]
        act = (cv * jax.nn.sigmoid(cv)).astype(BF16)
        gw_ref[...] = lax.dot_general(act, dm_ref[...].astype(BF16), (((0,), (0,)), ((), ())),
                                      preferred_element_type=F32)
        gb_ref[...] = jnp.sum(df_ref[...], axis=0, keepdims=True)

    return pl.pallas_call(
        body, name=name, grid=(depth,),
        in_specs=[pl.BlockSpec((n_b, d), lambda l: (0, 0)), pl.BlockSpec((None, n_b, cols), lambda l: (l, 0, 0)),
                  pl.BlockSpec((None, n_b, full), lambda l: (l, 0, 0))],
        out_specs=[pl.BlockSpec((None, d, cols), lambda l: (l, 0, 0)),
                   pl.BlockSpec((None, 1, full), lambda l: (l, 0, 0))],
        out_shape=[jax.ShapeDtypeStruct((depth, d, cols), F32), jax.ShapeDtypeStruct((depth, 1, full), F32)],
        compiler_params=_cparams(("parallel",)),
    )(c_all, dmod_cols, dmod_full)


def _adamw(gparts, own, w, m, v, name, layer=0, prev=None):
    n_p, rows, cols = gparts.shape
    assert w.shape[1:] == (rows, cols)
    tr = rows
    if rows > 512:
        tr = next(c for c in range(512, 7, -8) if rows % c == 0)
    has_own = own is not None
    n_prev = 0 if prev is None else 4

    def body(*refs):
        if has_own:
            slot_ref, refs = refs[0], refs[1:]
        g_ref = refs[0]
        own_ref = refs[1] if has_own else None
        w_ref, m_ref, v_ref = refs[1 + has_own:4 + has_own]
        go_ref, do_ref, mo_ref, vo_ref = refs[4 + has_own + n_prev:]
        g = None
        for p in range(n_p):
            term = g_ref[p].astype(F32)
            if has_own:
                term = jnp.where(slot_ref[0] == p, own_ref[...].astype(F32), term)
            g = term if g is None else g + term
        m_new = ADAM_B1 * m_ref[...] + (1.0 - ADAM_B1) * g
        v_new = ADAM_B2 * v_ref[...] + (1.0 - ADAM_B2) * (g * g)
        m_hat = m_new / (1.0 - ADAM_B1 ** ADAM_STEP)
        v_hat = v_new / (1.0 - ADAM_B2 ** ADAM_STEP)
        go_ref[...] = g
        do_ref[...] = -ADAM_LR * (m_hat / (jnp.sqrt(v_hat) + ADAM_EPS) + ADAM_WD * w_ref[...])
        mo_ref[...] = m_new
        vo_ref[...] = v_new

    spec = pl.BlockSpec((None, tr, cols), lambda i, *_: (layer, i, 0))
    in_specs = [pl.BlockSpec((n_p, tr, cols), lambda i, *_: (0, i, 0))]
    args = [gparts]
    if has_own:
        in_specs.append(pl.BlockSpec((None, tr, cols), lambda i, slot: (slot[0], i, 0)))
        args.append(own[0])
    in_specs += [spec, spec, spec]
    args += [w, m, v]
    aliases = {}
    if prev is not None:
        aliases = {has_own + len(args) + k: k for k in range(4)}
        in_specs += [pl.BlockSpec(memory_space=pl.ANY)] * 4
        args += list(prev)
    shp = jax.ShapeDtypeStruct(w.shape, F32)
    out_specs, out_shape = [spec, spec, spec, spec], [shp, shp, shp, shp]
    if not has_own:
        return pl.pallas_call(
            body, name=name, grid=(rows // tr,), in_specs=in_specs, out_specs=out_specs, out_shape=out_shape,
            input_output_aliases=aliases, compiler_params=_cparams(("parallel",)),
        )(*args)
    return pl.pallas_call(
        body, name=name, out_shape=out_shape, input_output_aliases=aliases,
        grid_spec=pltpu.PrefetchScalarGridSpec(num_scalar_prefetch=1, grid=(rows // tr,), in_specs=in_specs,
                                               out_specs=out_specs),
        compiler_params=_cparams(("parallel",)),
    )(jnp.reshape(own[1], (1,)).astype(jnp.int32), *args)


def _sum_parts(parts, name):
    n_p, rows, cols = parts.shape
    tr = 256 if rows % 256 == 0 else rows

    def body(p_ref, o_ref):
        acc = p_ref[0]
        for p in range(1, n_p):
            acc = acc + p_ref[p]
        o_ref[...] = acc

    return pl.pallas_call(
        body, name=name, grid=(rows // tr,),
        in_specs=[pl.BlockSpec((n_p, tr, cols), lambda i: (0, i, 0))],
        out_specs=pl.BlockSpec((tr, cols), lambda i: (i, 0)),
        out_shape=jax.ShapeDtypeStruct((rows, cols), F32),
        compiler_params=_cparams(("parallel",)),
    )(parts)


def _all_gather(arrs, name):
    n = len(arrs)

    def body(*refs):
        in_refs, out_refs = refs[:n], refs[n:2 * n]
        send_sems, recv_sems, loc_sems = refs[2 * n:]
        x, y, c = lax.axis_index("x"), lax.axis_index("y"), lax.axis_index("c")
        me, sibling = (x, y, c), (x, y, 1 - c)
        chips = [(1 - x, y), (x, 1 - y), (1 - x, 1 - y)]

        def copy(a, k, block, to, src=None):
            slot = out_refs[a].at[4 * block[0] + 2 * block[1] + block[2]]
            return pltpu.make_async_remote_copy(
                src_ref=slot if src is None else src, dst_ref=slot, send_sem=send_sems.at[a, k],
                recv_sem=recv_sems.at[a, k], device_id=to, device_id_type=pl.DeviceIdType.MESH)

        mine = [pltpu.make_async_copy(in_refs[a], out_refs[a].at[4 * x + 2 * y + c], loc_sems.at[a])
                for a in range(n)]
        for cp in mine:
            cp.start()
        first = []
        for a in range(n):
            first.append(copy(a, 0, me, sibling, src=in_refs[a]))
            first += [copy(a, 1 + j, me, (*chip, c), src=in_refs[a]) for j, chip in enumerate(chips)]
        for cp in first:
            cp.start()
        passed = []
        for j, chip in enumerate(chips):
            for a in range(n):
                copy(a, 1 + j, (*chip, c), me).wait_recv()
                cp = copy(a, 4 + j, (*chip, c), sibling)
                cp.start()
                passed.append(cp)
        for a in range(n):
            copy(a, 0, sibling, me).wait_recv()
        for j, chip in enumerate(chips):
            for a in range(n):
                copy(a, 4 + j, (*chip, 1 - c), me).wait_recv()
        for cp in first + passed:
            cp.wait_send()
        for cp in mine:
            cp.wait()

    any_spec = pl.BlockSpec(memory_space=pl.ANY)
    return pl.pallas_call(
        body, name=name, in_specs=[any_spec] * n, out_specs=[any_spec] * n,
        out_shape=[jax.ShapeDtypeStruct((N_DEV,) + a.shape, a.dtype) for a in arrs],
        scratch_shapes=[pltpu.SemaphoreType.DMA((n, N_DEV - 1)), pltpu.SemaphoreType.DMA((n, N_DEV - 1)),
                        pltpu.SemaphoreType.DMA((n,))],
    )(*arrs)


def _flip_peers():
    x, y, c = lax.axis_index("x"), lax.axis_index("y"), lax.axis_index("c")
    peers = []
    for fx, fy, fc in [(fx, fy, fc) for fx in (0, 1) for fy in (0, 1) for fc in (0, 1)][1:]:
        px, py, pc = (1 - x if fx else x), (1 - y if fy else y), (1 - c if fc else c)
        peers.append(((px, py, pc), 4 * px + 2 * py + pc))
    return 4 * x + 2 * y + c, peers


def _push_start(srcs, name, whole=False):
    n, n_peer = len(srcs), N_DEV - 1
    if whole:
        me_w = 4 * lax.axis_index("x") + 2 * lax.axis_index("y") + lax.axis_index("c")
        lands = [lax.dynamic_update_slice_in_dim(lax.empty((N_DEV,) + a.shape, a.dtype), a[None], me_w, axis=0)
                 for a in srcs]
    else:
        lands = [lax.empty(a.shape, a.dtype) for a in srcs]

    def body(*refs):
        src_refs, land_refs = refs[:n], refs[n:2 * n]
        send_sems, recv_sems = refs[2 * n], refs[2 * n + 1]
        token = refs[-1]
        me, peers = _flip_peers()
        for k, (dev, idx) in enumerate(peers):
            for a in range(n):
                pltpu.make_async_remote_copy(
                    src_ref=src_refs[a] if whole else src_refs[a].at[idx], dst_ref=land_refs[a].at[me],
                    send_sem=send_sems.at[a * n_peer + k], recv_sem=recv_sems.at[a * n_peer + k], device_id=dev,
                    device_id_type=pl.DeviceIdType.MESH).start()
        token[...] = jnp.zeros_like(token)

    hbm = pl.BlockSpec(memory_space=pltpu.HBM)
    sem = pl.BlockSpec(memory_space=pltpu.SEMAPHORE)
    arrs = list(srcs) + lands
    res = pl.pallas_call(
        body, name=name, in_specs=[hbm] * (2 * n),
        out_specs=(sem, sem, *[hbm] * (2 * n), pl.BlockSpec(memory_space=pltpu.VMEM)),
        out_shape=(pltpu.SemaphoreType.DMA((n * n_peer,)), pltpu.SemaphoreType.DMA((n * n_peer,)),
                   *[pltpu.HBM(a.shape, a.dtype) for a in arrs], jax.ShapeDtypeStruct((8, LANES), F32)),
        input_output_aliases={i: 2 + i for i in range(2 * n)},
        compiler_params=pltpu.CompilerParams(has_side_effects=pltpu.SideEffectType.DATAFLOW_SIDE_EFFECTING),
    )(*[pltpu.with_memory_space_constraint(a, pltpu.HBM) for a in arrs])
    return res[0], res[1], list(res[2:2 + n]), list(res[2 + n:2 + 2 * n]), res[-1]


def _push_wait(send_sems, recv_sems, srcs, lands, after, name, whole=False):
    n, n_peer = len(srcs), N_DEV - 1

    def body(*refs):
        src_refs, land_refs = refs[:n], refs[n:2 * n]
        send_s, recv_s = refs[2 * n], refs[2 * n + 1]
        _, peers = _flip_peers()
        for k, (dev, idx) in enumerate(peers):
            for a in range(n):
                cp = pltpu.make_async_remote_copy(
                    src_ref=src_refs[a] if whole else src_refs[a].at[idx], dst_ref=land_refs[a].at[idx],
                    send_sem=send_s.at[a * n_peer + k],
                    recv_sem=recv_s.at[a * n_peer + k], device_id=dev, device_id_type=pl.DeviceIdType.MESH)
                cp.wait_send()
                cp.wait_recv()

    hbm = pl.BlockSpec(memory_space=pltpu.HBM)
    sem = pl.BlockSpec(memory_space=pltpu.SEMAPHORE)
    arrs = list(srcs) + list(lands)
    res = pl.pallas_call(
        body, name=name, in_specs=[hbm] * (2 * n) + [sem, sem, pl.BlockSpec(memory_space=pl.ANY)],
        out_specs=tuple([hbm] * (2 * n)), out_shape=tuple(pltpu.HBM(a.shape, a.dtype) for a in arrs),
        input_output_aliases={i: i for i in range(2 * n)},
        compiler_params=pltpu.CompilerParams(has_side_effects=pltpu.SideEffectType.DATAFLOW_SIDE_EFFECTING),
    )(*arrs, send_sems, recv_sems, after)
    return list(res[:n]), list(res[n:])


def _ffn_fwd(x, h, mod, w_in, w_out_after, lng, lnb, rows, tag, nxt):
    bsz, seq, d = x.shape
    t = bsz * seq
    if h is None:
        h = _modulate(x, mod, rows[0], rows[1], f"modulate_{tag}")
    z, a = _ffn_in_swiglu(h.reshape(t, d), w_in, f"ffn_in_{tag}")
    f = _matmul_groupsum(a, w_out_after(a), out_dtype=F32, tm=512, name=f"ffn_out_{tag}").reshape(bsz, seq, d)
    y, h_next = _res_ln(x, f, mod, lng, lnb, rows[2], 0.5, f"res_ln_{tag}", nxt)
    return y, h_next, (x, h, z, a, f)


def _tied(mod, tie):
    return mod if tie is None else mod + tie


def _ffn_bwd(dy, saved, mod, w_in, w_out, lng, lnb, rows, tag, ready):
    x, h, z, a, f = saved
    bsz, seq, d = x.shape
    t = bsz * seq
    dx_res, df, dgate, dlg, dlb = _res_ln_bwd(dy, x, f, mod, lng, lnb, rows[2], 0.5, f"res_ln_bwd_{tag}")
    df2 = df.reshape(1, t, d)
    dw_out = _matmul(a, df2, mode="tn", group_out=True, out_dtype=BF16, tm=a.shape[2], tk=min(t, 2048),
                     name=f"ffn_out_dw_{tag}")
    tie_out = ready(f"{tag}_out", dw_out)
    dz = _ffn_out_dx_swiglu(df.reshape(t, d), w_out, z, f"ffn_out_dx_{tag}").reshape(N_DEV, t, -1)
    dw_in = _matmul(dz, h.reshape(1, t, d), mode="tn", group_out=True, out_dtype=BF16, tm=dz.shape[2],
                    tk=min(t, 2048), name=f"ffn_in_dw_{tag}")
    tie_in = ready(f"{tag}_in", dw_in)
    dh = _matmul_groupsum(dz, w_in, out_dtype=F32, tm=512, name=f"ffn_in_dx_{tag}").reshape(bsz, seq, d)
    dx, dsh, dsc = _modulate_bwd(dh, x, _tied(_tied(mod, tie_out), tie_in), dx_res, rows[1],
                                 f"modulate_bwd_{tag}")
    return dx, (dsh, dsc, dgate), dw_in, dw_out, dlg, dlb


def _mixer_fwd(x, h, mod, wts, small, lng, lnb, layer, tabs):
    bsz, seq, d = x.shape
    t = bsz * seq
    proj = _matmul(h.reshape(1, t, d), wts["mix_in"][None], mode="nn", group_out=True, out_dtype=F32, tm=512, tk=d,
                   name="mix_in").reshape(bsz, seq, PACK_W)
    mo, states = _hgrn_fwd(proj, small["lb_logits8"], small["hgrn_norm_g"], layer, f"hgrn_fwd_l{layer}")
    q, kv = _mla_pre(proj, small["q_norm_g"], small["kv_norm_g"], wts["uq"], wts["ukv"], tabs, "mla_pre")
    mla_scale = float((B_NOPE + B_ROPE) ** -0.5)
    mo, lse_b = _attn_fwd_loop(q, 0, kv, 0, mo, 2, None, mla_scale, "mla_attn_fwd")
    fg = _fox_gate(proj, small["fox_b_f"], "fox_gate")
    gates = (fg, jnp.swapaxes(fg[:, :, 0:8], 1, 2))
    fox_scale = float(HEAD_DIM ** -0.5)
    mo, lse_c = _attn_fwd_loop(proj, P_CQ // LANES, proj, P_CKV // LANES, mo, 6, gates, fox_scale, "fox_attn_fwd")
    mo = _gmlp_fwd(proj, mo, small["gmlp_ln_g"], small["gmlp_ln_b"], small["gmlp_w_s"], small["gmlp_bst"],
                   "gmlp_fwd")
    mixed = _matmul(mo.reshape(1, t, MO_W), wts["mix_out"][None], mode="nn", group_out=True, out_dtype=F32,
                    tm=1024, tk=MO_W, name="mix_out").reshape(bsz, seq, d)
    y, h_next = _res_ln(x, mixed, mod, lng, lnb, 5, 1.0, "res_ln_mix", (mod, 6, 7))
    return y, h_next, (x, h, proj, mo, states, q, kv, lse_b, gates, lse_c, mixed)


def _mixer_bwd(dy, saved, mod, wts, small, lng, lnb, layer, tabs, ready):
    x, h, proj, mo, states, q, kv, lse_b, gates, lse_c, mixed = saved
    bsz, seq, d = x.shape
    t = bsz * seq
    dx_res, dmixed, dgate, dlg, dlb = _res_ln_bwd(dy, x, mixed, mod, lng, lnb, 5, 1.0, "res_ln_bwd_mix")
    dm2 = dmixed.reshape(1, t, d)
    dmo = _matmul(dm2, wts["mix_out"][None], mode="nt", group_out=True, out_dtype=F32, tm=1024, tk=d,
                  name="mix_out_dx").reshape(bsz, seq, MO_W)
    dw_out = _matmul(mo.reshape(1, t, MO_W), dm2, mode="tn", group_out=True, out_dtype=F32, tm=512, tk=min(t, 2048),
                     name="mix_out_dw")[0]
    tie_out = ready("mix_out", dw_out)
    g = {}
    dproj, g["lb_logits8"], g["hgrn_norm_g"] = _hgrn_bwd(dmo, proj, states, small["lb_logits8"],
                                                         small["hgrn_norm_g"], layer, f"hgrn_bwd_l{layer}")
    mla_scale = float((B_NOPE + B_ROPE) ** -0.5)
    dq, delta_b, _ = _attn_bwd_q_loop(q, 0, kv, 0, mo, dmo, 2, lse_b, None, mla_scale,
                                 jax.ShapeDtypeStruct((bsz, seq, 512), F32), 0, "mla_attn_bwd_q")
    dkv, _ = _attn_bwd_kv_loop(q, 0, kv, 0, dmo, 2, lse_b, delta_b, None, mla_scale,
                          jax.ShapeDtypeStruct((bsz, seq, 1024), F32), 0, "mla_attn_bwd_kv")
    dproj, g["q_norm_g"], g["kv_norm_g"], g["uq"], g["ukv"] = _mla_pre_bwd(
        dq, dkv, dproj, proj, small["q_norm_g"], small["kv_norm_g"], wts["uq"], wts["ukv"], tabs, "mla_pre_bwd")
    ready("mla_uq", g.pop("uq"))
    ready("mla_ukv", g.pop("ukv"))
    fox_scale = float(HEAD_DIM ** -0.5)
    dproj, delta_c, dfq = _attn_bwd_q_loop(proj, P_CQ // LANES, proj, P_CKV // LANES, mo, dmo, 6, lse_c, gates,
                                      fox_scale, dproj, P_CQ // LANES, "fox_attn_bwd_q")
    dproj, dfk = _attn_bwd_kv_loop(proj, P_CQ // LANES, proj, P_CKV // LANES, dmo, 6, lse_c, delta_c, gates, fox_scale,
                              dproj, P_CKV // (2 * LANES), "fox_attn_bwd_kv")
    dfk_cols = jnp.pad(jnp.swapaxes(dfk[:, :, 0, :], 1, 2), ((0, 0), (0, 0), (0, LANES - N_HEADS)))
    dproj, g["fox_b_f"] = _fox_gate_bwd(dfq, dfk_cols, dproj, proj, small["fox_b_f"], "fox_gate_bwd")
    dproj, g["gmlp_ln_g"], g["gmlp_ln_b"], g["gmlp_w_s"], g["gmlp_bst"] = _gmlp_bwd(
        dmo, dproj, proj, small["gmlp_ln_g"], small["gmlp_ln_b"], small["gmlp_w_s"], small["gmlp_bst"], "gmlp_bwd")
    dp2 = dproj.reshape(1, t, PACK_W)
    dw_in = _matmul(h.reshape(1, t, d), dp2, mode="tn", group_out=True, out_dtype=BF16, tm=512, tk=1024,
                    name="mix_in_dw")[0]
    tie_in = ready("mix_in", dw_in)
    dh = _matmul(dp2, wts["mix_in"][None], mode="nt", group_out=True, out_dtype=F32, tm=512, tk=PACK_W,
                 name="mix_in_dx").reshape(bsz, seq, d)
    dx, dsh, dsc = _modulate_bwd(dh, x, _tied(_tied(mod, tie_out), tie_in), dx_res, 4, "modulate_bwd_mix")
    return dx, (dsh, dsc, dgate), dw_in, dw_out, g, dlg, dlb


def _small_views(p, layer):
    return {
        "lb_logits8": jnp.pad(p["hgrn_lb_logits"], ((0, 8 - DEPTH), (0, 0))),
        "hgrn_norm_g": p["hgrn_norm_g"][layer][None],
        "q_norm_g": p["mla_q_norm_g"][layer][None],
        "kv_norm_g": p["mla_kv_norm_g"][layer][None],
        "fox_b_f": jnp.pad(p["fox_b_f"][layer][None], ((0, 0), (0, LANES - N_HEADS))),
        "gmlp_ln_g": p["gmlp_ln_g"][layer][None],
        "gmlp_ln_b": p["gmlp_ln_b"][layer][None],
        "gmlp_w_s": p["gmlp_w_s"][layer],
        "gmlp_bst": jnp.pad(p["gmlp_b_s"][layer].T, ((0, 0), (0, LANES - N_HEADS))),
    }


def _local_step(x, mod, target, weights, p, grads_ready=None):
    bsz, seq, d = x.shape
    tabs = _rope_tables(seq)
    saved = []
    h = None
    for l in range(DEPTH):
        sm = _small_views(p, l)
        lng, lnb = p["ln_g"][l], p["ln_b"][l]
        x, h, s1 = _ffn_fwd(x, h, mod[l], weights(l, "ffn1_in", x)["ffn1_in"],
                            lambda a, l=l: weights(l, "ffn1_out", a)["ffn1_out"], lng[0:1], lnb[0:1], (0, 1, 2),
                            "ffn1", (mod[l], 3, 4))
        x, h, s2 = _mixer_fwd(x, h, mod[l], weights(l, "mix", x), sm, lng[1:2], lnb[1:2], l, tabs)
        x, h, s3 = _ffn_fwd(x, h, mod[l], weights(l, "ffn2_in", x)["ffn2_in"],
                            lambda a, l=l: weights(l, "ffn2_out", a)["ffn2_out"], lng[2:3], lnb[2:3], (6, 7, 8),
                            "ffn2", (mod[l + 1], 0, 1) if l + 1 < DEPTH else None)
        saved.append((s1, s2, s3))
    dx, loss = _loss_head(x, target, "loss_head")
    big, small, dmods = [None] * DEPTH, [None] * DEPTH, [None] * DEPTH
    ties = []

    def tied(a):
        for t in ties:
            a = a + t
        return a

    for l in reversed(range(DEPTH)):
        w = {}
        for part in ("ffn1_in", "ffn1_out", "mix", "ffn2_in", "ffn2_out"):
            w.update(weights(l, part, None))
        sm = _small_views(p, l)
        lng, lnb = p["ln_g"][l], p["ln_b"][l]
        s1, s2, s3 = saved[l]

        def ready(name, grad, l=l):
            tie = None if grads_ready is None else grads_ready(l, name, grad)
            if tie is not None:
                ties.append(tie)
            return tie

        dx, dm3, dwi2, dwo2, dlg2, dlb2 = _ffn_bwd(dx, s3, tied(mod[l]), w["ffn2_in"], w["ffn2_out"], lng[2:3],
                                                   lnb[2:3], (6, 7, 8), "ffn2", ready)
        dx, dm2, dwmi, dwmo, g, dlg1, dlb1 = _mixer_bwd(dx, s2, tied(mod[l]), w, sm, lng[1:2], lnb[1:2], l, tabs,
                                                        ready)
        dx, dm1, dwi1, dwo1, dlg0, dlb0 = _ffn_bwd(dx, s1, tied(mod[l]), w["ffn1_in"], w["ffn1_out"], lng[0:1],
                                                   lnb[0:1], (0, 1, 2), "ffn1", ready)
        dmods[l] = jnp.concatenate(list(dm1) + list(dm2) + list(dm3), axis=1)
        big[l] = {"ffn1_in": dwi1, "ffn1_out": dwo1, "ffn2_in": dwi2, "ffn2_out": dwo2, "mix_in": dwmi,
                  "mix_out": dwmo}
        g["ln_g"] = jnp.concatenate([dlg0, dlg1, dlg2], axis=0)
        g["ln_b"] = jnp.concatenate([dlb0, dlb1, dlb2], axis=0)
        small[l] = g
    return loss, dx, jnp.stack(dmods), big, small


_BIG = ("ffn1_in", "ffn1_out", "ffn2_in", "ffn2_out", "mix_in", "mix_out")


def _small_grad_list(small, loss):
    def both(fn):
        return jnp.stack([fn(small[l]) for l in range(DEPTH)])

    return [
        ("loss", loss.reshape(1)),
        ("ln_g", both(lambda g: g["ln_g"])), ("ln_b", both(lambda g: g["ln_b"])),
        ("hgrn_lb_logits", small[0]["lb_logits8"][:DEPTH] + small[1]["lb_logits8"][:DEPTH]),
        ("hgrn_norm_g", both(lambda g: g["hgrn_norm_g"][0])),
        ("mla_q_norm_g", both(lambda g: g["q_norm_g"][0])),
        ("mla_kv_norm_g", both(lambda g: g["kv_norm_g"][0])),
        ("fox_b_f", both(lambda g: g["fox_b_f"][0, :N_HEADS])),
        ("gmlp_ln_g", both(lambda g: g["gmlp_ln_g"][0])), ("gmlp_ln_b", both(lambda g: g["gmlp_ln_b"][0])),
        ("gmlp_w_s", both(lambda g: g["gmlp_w_s"])),
        ("gmlp_b_s", both(lambda g: g["gmlp_bst"][:, :N_HEADS].T)),
    ]


_PACK_COLS = 512


def _pack_small(items):
    flat = jnp.concatenate([a.reshape(-1).astype(F32) for _, a in items])
    n = flat.shape[0]
    tile = 8 * _PACK_COLS
    flat = jnp.pad(flat, (0, (-n) % tile))
    return flat.reshape(-1, _PACK_COLS)


def _unpack_small(buf, items):
    flat = buf.reshape(-1)
    out, off = {}, 0
    for name, a in items:
        out[name] = flat[off:off + a.size].reshape(a.shape)
        off += a.size
    return out


def _as2d(a):
    return a.reshape(-1, a.shape[-1])


def kernel(x, c, ada_w, ada_b, ln_g, ln_b, ffn1_w_in, ffn1_w_out, ffn2_w_in, ffn2_w_out, mix_w_in, mix_w_out, hgrn_lb_logits, hgrn_norm_g, mla_q_norm_g, mla_kv_norm_g, mla_w_uq, mla_w_ukv, fox_b_f, gmlp_ln_g, gmlp_ln_b, gmlp_w_s, gmlp_b_s, loss_target, m_ada_w, m_ada_b, m_ln_g, m_ln_b, m_ffn1_w_in, m_ffn1_w_out, m_ffn2_w_in, m_ffn2_w_out, m_mix_w_in, m_mix_w_out, m_hgrn_lb_logits, m_hgrn_norm_g, m_mla_q_norm_g, m_mla_kv_norm_g, m_mla_w_uq, m_mla_w_ukv, m_fox_b_f, m_gmlp_ln_g, m_gmlp_ln_b, m_gmlp_w_s, m_gmlp_b_s, v_ada_w, v_ada_b, v_ln_g, v_ln_b, v_ffn1_w_in, v_ffn1_w_out, v_ffn2_w_in, v_ffn2_w_out, v_mix_w_in, v_mix_w_out, v_hgrn_lb_logits, v_hgrn_norm_g, v_mla_q_norm_g, v_mla_kv_norm_g, v_mla_w_uq, v_mla_w_ukv, v_fox_b_f, v_gmlp_ln_g, v_gmlp_ln_b, v_gmlp_w_s, v_gmlp_b_s):
    names = ["ada_w", "ada_b", "ln_g", "ln_b", "ffn1_w_in", "ffn1_w_out", "ffn2_w_in", "ffn2_w_out", "mix_w_in",
             "mix_w_out", "hgrn_lb_logits", "hgrn_norm_g", "mla_q_norm_g", "mla_kv_norm_g", "mla_w_uq", "mla_w_ukv",
             "fox_b_f", "gmlp_ln_g", "gmlp_ln_b", "gmlp_w_s", "gmlp_b_s"]
    w = dict(zip(names, [ada_w, ada_b, ln_g, ln_b, ffn1_w_in, ffn1_w_out, ffn2_w_in, ffn2_w_out, mix_w_in, mix_w_out,
                         hgrn_lb_logits, hgrn_norm_g, mla_q_norm_g, mla_kv_norm_g, mla_w_uq, mla_w_ukv, fox_b_f,
                         gmlp_ln_g, gmlp_ln_b, gmlp_w_s, gmlp_b_s]))
    m = dict(zip(names, [m_ada_w, m_ada_b, m_ln_g, m_ln_b, m_ffn1_w_in, m_ffn1_w_out, m_ffn2_w_in, m_ffn2_w_out,
                         m_mix_w_in, m_mix_w_out, m_hgrn_lb_logits, m_hgrn_norm_g, m_mla_q_norm_g, m_mla_kv_norm_g,
                         m_mla_w_uq, m_mla_w_ukv, m_fox_b_f, m_gmlp_ln_g, m_gmlp_ln_b, m_gmlp_w_s, m_gmlp_b_s]))
    v = dict(zip(names, [v_ada_w, v_ada_b, v_ln_g, v_ln_b, v_ffn1_w_in, v_ffn1_w_out, v_ffn2_w_in, v_ffn2_w_out,
                         v_mix_w_in, v_mix_w_out, v_hgrn_lb_logits, v_hgrn_norm_g, v_mla_q_norm_g, v_mla_kv_norm_g,
                         v_mla_w_uq, v_mla_w_ukv, v_fox_b_f, v_gmlp_ln_g, v_gmlp_ln_b, v_gmlp_w_s, v_gmlp_b_s]))
    bsz, seq, d = x.shape
    me = 4 * lax.axis_index("x") + 2 * lax.axis_index("y") + lax.axis_index("c")
    mix_src, uq_src, ukv_src, mo_src = _mix_in_src(), _uq_src(), _ukv_src(), _mo_src()

    part_names = {"ffn1_in": ["ffn1_w_in"], "ffn1_out": ["ffn1_w_out"],
                  "mix": ["mix_w_in", "mix_w_out", "mla_w_uq", "mla_w_ukv"],
                  "ffn2_in": ["ffn2_w_in"], "ffn2_out": ["ffn2_w_out"]}
    group_of = {(l, part): (l, part) for l in range(DEPTH) for part in part_names}
    in_flight = {}
    transposed = ("ffn1_w_in", "ffn2_w_in")

    def start_group(key, behind=None):
        members = [(l, part) for (l, part), g in group_of.items() if g == key]
        labels = [(l, n) for l, part in members for n in part_names[part]]
        shards = []
        for l, n in labels:
            a = w[n][l]
            if n == "mix_w_in":
                a = _pack_cols(a, mix_src)
            if n in transposed:
                a = jnp.swapaxes(w[n], 1, 2)[l]
            shards.append(a.astype(BF16))
        if behind is not None:
            shards, _ = lax.optimization_barrier((shards, behind))
        in_flight[key] = (labels, _push_start(shards, f"gather_start_{key[0]}_{key[1]}", whole=True))

    keys_in_order = list(dict.fromkeys(group_of.values()))
    start_group(keys_in_order[0])

    gathered = _all_gather([c, ln_g, ln_b], "gather_inputs")
    c_all = gathered[0].reshape(N_DEV * bsz, d)
    ln_g_full = jnp.moveaxis(gathered[1], 0, 2).reshape(DEPTH, 3, d)
    ln_b_full = jnp.moveaxis(gathered[2], 0, 2).reshape(DEPTH, 3, d)

    mod_cols = _ada_fwd(c_all, ada_w, "ada_fwd")
    mod_all, = _all_gather([mod_cols], "gather_mod")
    mod_mine = lax.dynamic_slice_in_dim(mod_all, me * bsz, bsz, axis=2)
    mod = jnp.moveaxis(mod_mine, 0, 2).reshape(DEPTH, bsz, N_MOD * d) + ada_b[:, None, :]
    for key in keys_in_order[1:]:
        start_group(key, behind=mod)
    tie = sum(h[-1][0, 0] for _, h in in_flight.values())
    mod = mod.reshape(DEPTH, bsz, N_MOD, d) + tie

    arrived, laid_out = {}, {}

    def weights(l, part, after):
        if (l, part) not in laid_out:
            laid_out[(l, part)] = lay_out(l, part, after)
        return laid_out[(l, part)]

    def lay_out(l, part, after):
        key = group_of[(l, part)]
        if key not in arrived:
            labels, (send_sems, recv_sems, srcs, lands, _) = in_flight[key]
            _, lands = _push_wait(send_sems, recv_sems, srcs, lands, after, f"gather_wait_{key[0]}_{key[1]}",
                                  whole=True)
            arrived[key] = dict(zip(labels, lands))
        gw = {n: arrived[key][(l, n)] for n in part_names[part]}
        if part.endswith("_in"):
            return {part: gw[part_names[part][0]]}
        if part.endswith("_out"):
            return {part: gw[part_names[part][0]].reshape(4, 704, d)}
        uq = jnp.moveaxis(gw["mla_w_uq"], 0, 1).reshape(256, 384)
        ukv = jnp.moveaxis(gw["mla_w_ukv"], 0, 1).reshape(128, 512)
        return {"mix_in": gw["mix_w_in"].reshape(d, PACK_W),
                "mix_out": _pack_cols(gw["mix_w_out"].reshape(d, d).T, mo_src).T,
                "uq": _pack_cols(uq, uq_src), "ukv": _pack_cols(ukv, ukv_src)}

    p = dict(w)
    p["ln_g"], p["ln_b"] = ln_g_full, ln_b_full
    def chunks(name, arr):
        if name in ("ffn1_in", "ffn2_in"):
            return arr
        if name in ("ffn1_out", "ffn2_out"):
            return arr.reshape(N_DEV, arr.shape[1] // 2, d)
        if name == "mix_in":
            return _unpack_cols(arr, mix_src, MIX_ORIG_W).reshape(N_DEV, d // N_DEV, MIX_ORIG_W)
        if name in ("mla_uq", "mla_ukv"):
            full_w = _unpack_cols(arr, uq_src, 384) if name == "mla_uq" else _unpack_cols(arr, ukv_src, 512)
            rows = full_w.shape[0]
            return jnp.moveaxis(full_w.reshape(rows, N_DEV, -1), 1, 0).astype(BF16)
        return _unpack_cols(arr.T, mo_src, d).T.astype(BF16).reshape(N_DEV, d // N_DEV, d)

    pending, started = {}, []

    def grads_ready(l, name, grad):
        pending[(name, l)] = chunks(name, grad)
        flush = name == "ffn1_in" if l > 0 else name in ("ffn2_in", "mix_out", "mix_in", "ffn1_out", "ffn1_in")
        if not flush:
            return None
        keys = sorted(pending)
        handles = _push_start([pending[k] for k in keys], f"push_start_{len(started)}")
        pending.clear()
        started.append((keys, handles, l == 0 and name.startswith("ffn1")))
        return handles[-1][0, 0]

    loss, grad_x, dmod, big, small = _local_step(x, mod, loss_target, weights, p, grads_ready)
    del big

    recv, out = {}, {}

    def arrive(n, after):
        keys, (send_sems, recv_sems, srcs, lands, _), _ = started[n]
        srcs, lands = _push_wait(send_sems, recv_sems, srcs, lands, after, f"push_wait_{n}")
        for k, src, land in zip(keys, srcs, lands):
            recv[k] = (land, src)

    big_of = {"ffn1_w_in": "ffn1_in", "ffn1_w_out": "ffn1_out", "ffn2_w_in": "ffn2_in", "ffn2_w_out": "ffn2_out",
              "mix_w_in": "mix_in", "mix_w_out": "mix_out", "mla_w_uq": "mla_uq", "mla_w_ukv": "mla_ukv"}
    chain = {name: None for name in big_of}

    def big_update(key, l):
        name = next(nm for nm, k in big_of.items() if k == key)
        parts, src = recv[(key, l)]
        view =(lambda a: jnp.swapaxes(a, 1, 2)) if name in transposed else (lambda a: a)
        chain[name] = _adamw(parts, (src, me), view(w[name]), view(m[name]), view(v[name]), f"adamw_{name}_l{l}",
                             layer=l, prev=chain[name])

    def update(name, grad):
        shape = w[name].shape
        as3 = lambda a: a.reshape(1, -1, shape[-1])
        res = _adamw(as3(grad), None, as3(w[name]), as3(m[name]), as3(v[name]), f"adamw_{name}")
        out[name] = tuple(r.reshape(shape) for r in res)

    for n, (keys, _, last) in enumerate(started):
        if not last:
            arrive(n, grad_x)
            for key, l in keys:
                big_update(key, l)

    dmod_flat = dmod.reshape(DEPTH, bsz, N_MOD * d)
    done = [r[0] for r in chain.values() if r is not None]
    if done:
        dmod_flat, _ = lax.optimization_barrier((dmod_flat, done))
    dmod_all, = _all_gather([dmod_flat], "gather_dmod")
    dmod_full = jnp.moveaxis(dmod_all, 0, 1).reshape(DEPTH, N_DEV * bsz, N_MOD * d)
    cols = ada_w.shape[2]
    dmod_cols = lax.dynamic_slice_in_dim(dmod_full, me * cols, cols, axis=2)
    g_ada_w, g_ada_b = _ada_bwd(c_all, dmod_cols, dmod_full, "ada_bwd")
    res = None
    for l in range(DEPTH):
        res = _adamw(g_ada_w[l][None], None, ada_w, m_ada_w, v_ada_w, f"adamw_ada_w_l{l}", layer=l, prev=res)
    out["ada_w"] = tuple(res)
    update("ada_b", g_ada_b.reshape(DEPTH, N_MOD * d))

    items = _small_grad_list(small, loss)
    packed, _ = lax.optimization_barrier((_pack_small(items), (grad_x, g_ada_b)))
    parts, = _all_gather([packed], "gather_small")
    sg = _unpack_small(_sum_parts(parts, "sum_small"), items)
    for name in ("ln_g", "ln_b"):
        update(name, lax.dynamic_slice_in_dim(sg[name], me * (d // N_DEV), d // N_DEV, axis=2))
    for name in ("hgrn_lb_logits", "hgrn_norm_g", "mla_q_norm_g", "mla_kv_norm_g", "fox_b_f", "gmlp_ln_g",
                 "gmlp_ln_b", "gmlp_w_s", "gmlp_b_s"):
        update(name, sg[name])

    for n, (keys, _, last) in enumerate(started):
        if last:
            arrive(n, out["gmlp_w_s"][0])
            for key, l in keys:
                big_update(key, l)
    for name in big_of:
        out[name] = tuple(jnp.swapaxes(r, 1, 2) if name in transposed else r for r in chain[name])

    return (sg["loss"][0], grad_x, *[out[n][0] for n in names], *[out[n][1] for n in names],
            *[out[n][2] for n in names], *[out[n][3] for n in names])
```

```python
import functools

import numpy as np
import jax
import jax.numpy as jnp
from jax import lax
from jax.experimental import pallas as pl
from jax.experimental.pallas import tpu as pltpu

F32 = jnp.float32
BF16 = jnp.bfloat16
HI = lax.Precision.HIGHEST

D_MODEL = 1024
DEPTH = 2
GROUP_WIDTH = 256
N_HEADS = 4
HEAD_DIM = 64
A_CHUNK = 16
LB_FLOOR = 1e-30
B_NOPE = 64
B_ROPE = 32
ROPE_THETA = 10000.0
D_CHUNK = 128
D_FF = 2816
N_MOD = 9
ALPHA = (2 * DEPTH) ** 0.25
LN_EPS = 1e-5
RMS_EPS = 1e-6
ADAM_LR = 0.001
ADAM_B1 = 0.9
ADAM_B2 = 0.999
ADAM_EPS = 1e-08
ADAM_WD = 0.01
ADAM_STEP = 10

N_DEV = 8
LANES = 128
PACK_W = 3712
MO_W = 1536
VMEM_LIMIT = 56 * 1024 * 1024
NEG = -1e30
ATTN_TILE = 512

MIX_ORIG_W = 2724
O_BCQ, O_BCKV, O_BKR, O_CQ, O_CK, O_CV, O_CF, O_DU, O_DV = 1024, 1280, 1408, 1440, 1696, 1952, 2208, 2212, 2468
P_B, P_KR, P_CQ, P_CKV, P_D, P_CF = 1024, 1408, 1536, 2048, 3072, 3584


_DN = {"nn": (((1,), (0,)), ((), ())), "nt": (((1,), (1,)), ((), ())), "tn": (((0,), (0,)), ((), ()))}


def _raw_bdot(a, b, mode):
    return lax.dot_general(a.astype(BF16), b.astype(BF16), _DN[mode], preferred_element_type=F32)


@functools.partial(jax.custom_vjp, nondiff_argnums=(2,))
def _bdot(a, b, mode):
    return _raw_bdot(a, b, mode)


def _bdot_fwd(a, b, mode):
    return _raw_bdot(a, b, mode), (a, b)


def _bdot_bwd(mode, res, g):
    a, b = res
    if mode == "nn":
        return _raw_bdot(g, b, "nt"), _raw_bdot(a, g, "tn")
    if mode == "nt":
        return _raw_bdot(g, b, "nn"), _raw_bdot(g, a, "tn")
    return _raw_bdot(b, g, "nt"), _raw_bdot(a, g, "nn")


_bdot.defvjp(_bdot_fwd, _bdot_bwd)


def _cparams(sem):
    return pltpu.CompilerParams(dimension_semantics=sem, vmem_limit_bytes=VMEM_LIMIT)


def _mix_in_src():
    src = -np.ones(PACK_W, np.int64)
    src[0:P_KR] = np.arange(0, O_BKR)
    src[P_KR + 64:P_KR + 80] = O_BKR + np.arange(16)
    src[P_KR + 96:P_KR + 112] = O_BKR + 16 + np.arange(16)
    for h in range(N_HEADS):
        src[P_CQ + 128 * h:P_CQ + 128 * h + 64] = O_CQ + 64 * h + np.arange(64)
        src[P_CKV + 256 * h:P_CKV + 256 * h + 64] = O_CK + 64 * h + np.arange(64)
        src[P_CKV + 256 * h + 128:P_CKV + 256 * h + 192] = O_CV + 64 * h + np.arange(64)
    src[P_D:P_D + 512] = O_DU + np.arange(512)
    src[P_CF:P_CF + 4] = O_CF + np.arange(4)
    return src


def _uq_src():
    src = -np.ones(512, np.int64)
    for h in range(N_HEADS):
        src[128 * h:128 * h + 64] = 96 * h + np.arange(64)
        src[128 * h + 64:128 * h + 80] = 96 * h + 64 + np.arange(16)
        src[128 * h + 96:128 * h + 112] = 96 * h + 80 + np.arange(16)
    return src


def _ukv_src():
    src = -np.ones(1024, np.int64)
    for h in range(N_HEADS):
        src[256 * h:256 * h + 64] = 128 * h + np.arange(64)
        src[256 * h + 128:256 * h + 192] = 128 * h + 64 + np.arange(64)
    return src


def _mo_src():
    src = -np.ones(MO_W, np.int64)
    src[0:256] = np.arange(256)
    for g in range(2):
        for h in range(N_HEADS):
            src[256 + 512 * g + 128 * h:256 + 512 * g + 128 * h + 64] = 256 + 256 * g + 64 * h + np.arange(64)
    src[1280:1536] = 768 + np.arange(256)
    return src


def _runs(idx):
    runs, i = [], 0
    while i < len(idx):
        j = i + 1
        while j < len(idx) and ((idx[i] < 0 and idx[j] < 0) or (idx[i] >= 0 and idx[j] == idx[i] + j - i)):
            j += 1
        runs.append((int(idx[i]), j - i))
        i = j
    return runs


def _take_runs(w, idx):
    parts = [jnp.zeros(w.shape[:-1] + (n,), w.dtype) if s < 0 else lax.slice_in_dim(w, s, s + n, axis=w.ndim - 1)
             for s, n in _runs(idx)]
    return jnp.concatenate(parts, axis=-1)


def _pack_cols(w, src):
    return _take_runs(w, src)


def _unpack_cols(wp, src, n):
    dst = np.zeros(n, np.int64)
    dst[src[src >= 0]] = np.nonzero(src >= 0)[0]
    return _take_runs(wp, dst)


def _rope_tables(seq):
    half = B_ROPE // 2
    inv_freq = ROPE_THETA ** (-jnp.arange(half, dtype=F32) / half)
    ang = jnp.arange(seq).astype(F32)[:, None] * inv_freq[None, :]
    cos, sin = jnp.cos(ang), jnp.sin(ang)
    z16 = jnp.zeros((seq, 16), F32)
    c = jnp.concatenate([jnp.ones((seq, 64), F32), cos, z16, cos, z16], axis=1)
    s1 = jnp.concatenate([jnp.zeros((seq, 64), F32), -sin, z16, z16, z16], axis=1)
    s2 = jnp.concatenate([jnp.zeros((seq, 64), F32), z16, z16, sin, z16], axis=1)
    return c, s1, s2


def _matmul(a, b, *, mode, group_out, out_dtype, tm, tk, name):
    ga, gb = a.shape[0], b.shape[0]
    g_n = max(ga, gb)
    if mode == "tn":
        k_dim, m_dim = a.shape[1:]
    else:
        m_dim, k_dim = a.shape[1:]
    n_dim = b.shape[1] if mode == "nt" else b.shape[2]
    assert m_dim % tm == 0 and k_dim % tk == 0
    kt = k_dim // tk
    n_red = kt if group_out else g_n * kt
    g_out = g_n if group_out else 1

    def split(g, r):
        return (g, r) if group_out else (r // kt, r % kt)

    def a_map(g, i, r):
        gg, kk = split(g, r)
        gg = gg if ga > 1 else 0
        return (gg, kk, i) if mode == "tn" else (gg, i, kk)

    def b_map(g, i, r):
        gg, kk = split(g, r)
        gg = gg if gb > 1 else 0
        return (gg, 0, kk) if mode == "nt" else (gg, kk, 0)

    a_blk = (None, tk, tm) if mode == "tn" else (None, tm, tk)
    b_blk = (None, n_dim, tk) if mode == "nt" else (None, tk, n_dim)
    dn = _DN[mode]

    def body(a_ref, b_ref, o_ref, *scratch):
        part = lax.dot_general(a_ref[...].astype(BF16), b_ref[...].astype(BF16), dn, preferred_element_type=F32)
        if n_red == 1:
            o_ref[...] = part.astype(o_ref.dtype)
            return
        acc_ref, = scratch
        r = pl.program_id(2)

        @pl.when(r == 0)
        def _():
            acc_ref[...] = part

        @pl.when(r > 0)
        def _():
            acc_ref[...] += part

        @pl.when(r == n_red - 1)
        def _():
            o_ref[...] = acc_ref[...].astype(o_ref.dtype)

    return pl.pallas_call(
        body, name=name, grid=(g_out, m_dim // tm, n_red),
        in_specs=[pl.BlockSpec(a_blk, a_map), pl.BlockSpec(b_blk, b_map)],
        out_specs=pl.BlockSpec((None, tm, n_dim), lambda g, i, r: (g, i, 0)),
        out_shape=jax.ShapeDtypeStruct((g_out, m_dim, n_dim), out_dtype),
        scratch_shapes=[] if n_red == 1 else [pltpu.VMEM((tm, n_dim), F32)],
        compiler_params=_cparams(("parallel", "parallel", "arbitrary")),
    )(a, b)


def _matmul_groupsum(a, b, *, out_dtype, tm, name):
    g_n, m_dim, k_dim = a.shape
    n_dim = b.shape[2]
    assert m_dim % tm == 0 and b.shape[:2] == (g_n, k_dim)

    def body(a_ref, b_ref, o_ref):
        acc = jnp.dot(a_ref[0], b_ref[0], preferred_element_type=F32)
        for g in range(1, g_n):
            acc = acc + jnp.dot(a_ref[g], b_ref[g], preferred_element_type=F32)
        o_ref[...] = acc.astype(o_ref.dtype)

    return pl.pallas_call(
        body, name=name, grid=(m_dim // tm,),
        in_specs=[pl.BlockSpec((g_n, tm, k_dim), lambda i: (0, i, 0)),
                  pl.BlockSpec((g_n, k_dim, n_dim), lambda i: (0, 0, 0))],
        out_specs=pl.BlockSpec((tm, n_dim), lambda i: (i, 0)),
        out_shape=jax.ShapeDtypeStruct((m_dim, n_dim), out_dtype),
        compiler_params=_cparams(("parallel",)),
    )(a, b)


def _row_spec(ts, d):
    return pl.BlockSpec((None, ts, d), lambda b, s: (b, s, 0))


def _mod_spec(d):
    return pl.BlockSpec((None, N_MOD, d), lambda b, s: (b, 0, 0))


def _vec_spec(d):
    return pl.BlockSpec((1, d), lambda b, s: (0, 0))


def _bvec_spec(d):
    return pl.BlockSpec((None, 1, d), lambda b, s: (b, 0, 0))


def _modulate(x, mod, sh_row, sc_row, name, ts=512):
    bsz, seq, d = x.shape

    def body(x_ref, mod_ref, o_ref):
        sh = mod_ref[sh_row:sh_row + 1, :]
        sc = mod_ref[sc_row:sc_row + 1, :]
        o_ref[...] = (x_ref[...] * (1.0 + sc) + sh).astype(o_ref.dtype)

    return pl.pallas_call(
        body, name=name, grid=(bsz, seq // ts),
        in_specs=[_row_spec(ts, d), _mod_spec(d)], out_specs=_row_spec(ts, d),
        out_shape=jax.ShapeDtypeStruct((bsz, seq, d), BF16),
        compiler_params=_cparams(("parallel", "parallel")),
    )(x, mod)


def _modulate_bwd(dh, x, mod, dx_res, sc_row, name, ts=512):
    bsz, seq, d = x.shape

    def body(dh_ref, x_ref, mod_ref, dxr_ref, dx_ref, dsh_ref, dsc_ref):
        s = pl.program_id(1)
        sc = mod_ref[sc_row:sc_row + 1, :]
        dh_v = dh_ref[...]
        dx_ref[...] = dxr_ref[...] + dh_v * (1.0 + sc)
        psh = jnp.sum(dh_v, axis=0, keepdims=True)
        psc = jnp.sum(dh_v * x_ref[...], axis=0, keepdims=True)

        @pl.when(s == 0)
        def _():
            dsh_ref[...] = psh
            dsc_ref[...] = psc

        @pl.when(s > 0)
        def _():
            dsh_ref[...] += psh
            dsc_ref[...] += psc

    return pl.pallas_call(
        body, name=name, grid=(bsz, seq // ts),
        in_specs=[_row_spec(ts, d), _row_spec(ts, d), _mod_spec(d), _row_spec(ts, d)],
        out_specs=[_row_spec(ts, d), _bvec_spec(d), _bvec_spec(d)],
        out_shape=[jax.ShapeDtypeStruct((bsz, seq, d), F32), jax.ShapeDtypeStruct((bsz, 1, d), F32),
                   jax.ShapeDtypeStruct((bsz, 1, d), F32)],
        compiler_params=_cparams(("parallel", "arbitrary")),
    )(dh, x, mod, dx_res)


def _res_ln_fn(x, f, g, lng, lnb, cmul):
    r = ALPHA * x + (cmul * (1.0 + g)) * f
    mu = jnp.mean(r, axis=-1, keepdims=True)
    rc = r - mu
    var = jnp.mean(rc * rc, axis=-1, keepdims=True)
    return rc * lax.rsqrt(var + LN_EPS) * lng + lnb


def _res_ln(x, f, mod, lng, lnb, g_row, cmul, name, nxt=None, ts=512):
    bsz, seq, d = x.shape

    def body(*refs):
        x_ref, f_ref, mod_ref, lng_ref, lnb_ref = refs[:5]
        g = mod_ref[g_row:g_row + 1, :]
        y = _res_ln_fn(x_ref[...], f_ref[...], g, lng_ref[...], lnb_ref[...], cmul)
        if nxt is None:
            refs[5][...] = y
            return
        nmod_ref, o_ref, h_ref = refs[5:]
        o_ref[...] = y
        sh = nmod_ref[nxt[1]:nxt[1] + 1, :]
        sc = nmod_ref[nxt[2]:nxt[2] + 1, :]
        h_ref[...] = (y * (1.0 + sc) + sh).astype(h_ref.dtype)

    in_specs = [_row_spec(ts, d), _row_spec(ts, d), _mod_spec(d), _vec_spec(d), _vec_spec(d)]
    args = [x, f, mod, lng, lnb]
    out_specs, out_shape = [_row_spec(ts, d)], [jax.ShapeDtypeStruct((bsz, seq, d), F32)]
    if nxt is not None:
        in_specs.append(_mod_spec(d))
        args.append(nxt[0])
        out_specs.append(_row_spec(ts, d))
        out_shape.append(jax.ShapeDtypeStruct((bsz, seq, d), BF16))
    res = pl.pallas_call(
        body, name=name, grid=(bsz, seq // ts), in_specs=in_specs, out_specs=out_specs, out_shape=out_shape,
        compiler_params=_cparams(("parallel", "parallel")),
    )(*args)
    return (res[0], res[1]) if nxt is not None else (res[0], None)


def _res_ln_bwd(dy, x, f, mod, lng, lnb, g_row, cmul, name, pre=None, ts=256):
    bsz, seq, d = x.shape
    fused = pre is not None

    def body(*refs):
        dy_ref, x_ref, f_ref, mod_ref, lng_ref, lnb_ref = refs[:6]
        n_in = 9 if fused else 6
        dx_ref, df_ref, dg_ref, dlg_ref, dlb_ref = refs[n_in:n_in + 5]
        b, s = pl.program_id(0), pl.program_id(1)
        g = mod_ref[g_row:g_row + 1, :]
        _, vjp = jax.vjp(functools.partial(_res_ln_fn, cmul=cmul), x_ref[...], f_ref[...], g, lng_ref[...],
                         lnb_ref[...])
        ct = dy_ref[...]
        if fused:
            dh_ref, y_ref, nmod_ref = refs[6:9]
            dsh_ref, dsc_ref = refs[n_in + 5:]
            dh_v = dh_ref[...]
            ct = ct + dh_v * (1.0 + nmod_ref[pre[3]:pre[3] + 1, :])
            psh = jnp.sum(dh_v, axis=0, keepdims=True)
            psc = jnp.sum(dh_v * y_ref[...], axis=0, keepdims=True)
        dx, df, dg, dlg, dlb = vjp(ct)
        dx_ref[...] = dx
        df_ref[...] = df.astype(df_ref.dtype)

        @pl.when(s == 0)
        def _():
            dg_ref[...] = dg
            if fused:
                dsh_ref[...] = psh
                dsc_ref[...] = psc

        @pl.when(s > 0)
        def _():
            dg_ref[...] += dg
            if fused:
                dsh_ref[...] += psh
                dsc_ref[...] += psc

        first = jnp.logical_and(b == 0, s == 0)

        @pl.when(first)
        def _():
            dlg_ref[...] = dlg
            dlb_ref[...] = dlb

        @pl.when(jnp.logical_not(first))
        def _():
            dlg_ref[...] += dlg
            dlb_ref[...] += dlb

    in_specs = [_row_spec(ts, d), _row_spec(ts, d), _row_spec(ts, d), _mod_spec(d), _vec_spec(d), _vec_spec(d)]
    args = [dy, x, f, mod, lng, lnb]
    out_specs = [_row_spec(ts, d), _row_spec(ts, d), _bvec_spec(d), _vec_spec(d), _vec_spec(d)]
    bvec = jax.ShapeDtypeStruct((bsz, 1, d), F32)
    out_shape = [jax.ShapeDtypeStruct((bsz, seq, d), F32), jax.ShapeDtypeStruct((bsz, seq, d), BF16), bvec,
                 jax.ShapeDtypeStruct((1, d), F32), jax.ShapeDtypeStruct((1, d), F32)]
    if fused:
        in_specs += [_row_spec(ts, d), _row_spec(ts, d), _mod_spec(d)]
        args += list(pre[:3])
        out_specs += [_bvec_spec(d), _bvec_spec(d)]
        out_shape += [bvec, bvec]
    res = pl.pallas_call(
        body, name=name, grid=(bsz, seq // ts), in_specs=in_specs, out_specs=out_specs, out_shape=out_shape,
        compiler_params=_cparams(("arbitrary", "arbitrary")),
    )(*args)
    return tuple(res[:5]), (tuple(res[5:]) if fused else None)


def _loss_head(y, target, name, ts=512):
    bsz, seq, d = y.shape
    n_s = seq // ts

    def body(y_ref, t_ref, dy_ref, loss_ref, acc_ref):
        b, s = pl.program_id(0), pl.program_id(1)
        err = y_ref[...] - t_ref[...]
        dy_ref[...] = err * (1.0 / d)
        part = jnp.sum(err * err, axis=0, keepdims=True)
        first = jnp.logical_and(b == 0, s == 0)

        @pl.when(first)
        def _():
            acc_ref[...] = part

        @pl.when(jnp.logical_not(first))
        def _():
            acc_ref[...] += part

        @pl.when(jnp.logical_and(b == bsz - 1, s == n_s - 1))
        def _():
            loss_ref[...] = jnp.sum(acc_ref[...], axis=1, keepdims=True) * (0.5 / d)

    return pl.pallas_call(
        body, name=name, grid=(bsz, n_s),
        in_specs=[_row_spec(ts, d), _row_spec(ts, d)],
        out_specs=[_row_spec(ts, d), pl.BlockSpec((1, 1), lambda b, s: (0, 0))],
        out_shape=[jax.ShapeDtypeStruct((bsz, seq, d), F32), jax.ShapeDtypeStruct((1, 1), F32)],
        scratch_shapes=[pltpu.VMEM((1, d), F32)],
        compiler_params=_cparams(("arbitrary", "arbitrary")),
    )(y, target)


def _ffn_in_swiglu(h, w_in_t, name, tm=1024):
    t, d = h.shape
    n_sh, w, _ = w_in_t.shape
    half = n_sh // 2

    def body(h_ref, w_ref, z_ref, a_ref):
        hv = h_ref[...]
        g = lax.dot_general(hv, w_ref[0], _DN["nt"], preferred_element_type=F32)
        u = lax.dot_general(hv, w_ref[1], _DN["nt"], preferred_element_type=F32)
        z_ref[0] = g.astype(z_ref.dtype)
        z_ref[1] = u.astype(z_ref.dtype)
        a_ref[...] = (g * jax.nn.sigmoid(g) * u).astype(a_ref.dtype)

    return pl.pallas_call(
        body, name=name, grid=(half, t // tm),
        in_specs=[pl.BlockSpec((tm, d), lambda g, i: (i, 0)),
                  pl.BlockSpec((2, None, w, d), lambda g, i: (0, g, 0, 0))],
        out_specs=[pl.BlockSpec((2, None, tm, w), lambda g, i: (0, g, i, 0)),
                   pl.BlockSpec((None, tm, w), lambda g, i: (g, i, 0))],
        out_shape=[jax.ShapeDtypeStruct((2, half, t, w), BF16), jax.ShapeDtypeStruct((half, t, w), BF16)],
        compiler_params=_cparams(("parallel", "parallel")),
    )(h, w_in_t.reshape(2, half, w, d))


def _ffn_out_dx_swiglu(df, w_out, z, name, tm=1024):
    t, d = df.shape
    half, w, _ = w_out.shape

    def body(df_ref, w_ref, z_ref, dz_ref):
        da = lax.dot_general(df_ref[...], w_ref[...], _DN["nt"], preferred_element_type=F32)
        g = z_ref[0].astype(F32)
        u = z_ref[1].astype(F32)
        sig = jax.nn.sigmoid(g)
        dz_ref[0] = (da * u * (sig * (1.0 + g * (1.0 - sig)))).astype(dz_ref.dtype)
        dz_ref[1] = (da * (g * sig)).astype(dz_ref.dtype)

    zspec = pl.BlockSpec((2, None, tm, w), lambda g, i: (0, g, i, 0))
    return pl.pallas_call(
        body, name=name, grid=(half, t // tm),
        in_specs=[pl.BlockSpec((tm, d), lambda g, i: (i, 0)), pl.BlockSpec((None, w, d), lambda g, i: (g, 0, 0)),
                  zspec],
        out_specs=zspec, out_shape=jax.ShapeDtypeStruct(z.shape, BF16),
        compiler_params=_cparams(("parallel", "parallel")),
    )(df, w_out, z)


def _log_sigmoid(x):
    return jnp.minimum(x, 0.0) - jnp.log(1.0 + jnp.exp(-jnp.abs(x)))


def _hgrn_consts():
    r = lax.broadcasted_iota(jnp.int32, (GROUP_WIDTH, GROUP_WIDTH), 0)
    c = lax.broadcasted_iota(jnp.int32, (GROUP_WIDTH, GROUP_WIDTH), 1)
    bd = (r // HEAD_DIM == c // HEAD_DIM).astype(F32)
    r16 = lax.broadcasted_iota(jnp.int32, (A_CHUNK, A_CHUNK), 0)
    c16 = lax.broadcasted_iota(jnp.int32, (A_CHUNK, A_CHUNK), 1)
    tril = (r16 >= c16).astype(F32)
    rows = lax.broadcasted_iota(jnp.int32, (A_CHUNK, GROUP_WIDTH), 0)
    return bd, tril, rows


def _hgrn_lb(logits8, layer):
    rows = lax.broadcasted_iota(jnp.int32, logits8.shape, 0)
    valid = rows < DEPTH
    mx = jnp.max(jnp.where(valid, logits8, NEG), axis=0, keepdims=True)
    e = jnp.where(valid, jnp.exp(logits8 - mx), 0.0)
    sm = e / jnp.sum(e, axis=0, keepdims=True)
    pick = jnp.logical_and(rows >= 1, rows <= layer)
    return jnp.sum(jnp.where(pick, sm, 0.0), axis=0, keepdims=True)


def _hgrn_chunk(aq, af, ai, ag, logits8, norm_g, st, *, layer, consts):
    bd, tril, rows = consts
    lb = _hgrn_lb(logits8, layer)
    la = jnp.log(jnp.maximum(lb, LB_FLOOR))
    b2 = jnp.log(1.0 - lb) + _log_sigmoid(af)
    log_f = jnp.maximum(la, b2) + jnp.log(1.0 + jnp.exp(-jnp.abs(la - b2)))
    k = 1.0 - jnp.exp(log_f)
    qf = aq * jax.nn.sigmoid(aq)
    g_cum = jnp.dot(tril, log_f, precision=HI, preferred_element_type=F32)

    c, w = A_CHUNK, GROUP_WIDTH

    def by_key(v):
        return jnp.broadcast_to(v[:, None, :], (c, c, w))

    def by_query(v):
        return jnp.broadcast_to(v[None, :, :], (c, c, w))

    s_i = lax.broadcasted_iota(jnp.int32, (c, c, w), 0)
    t_i = lax.broadcasted_iota(jnp.int32, (c, c, w), 1)
    rel = jnp.where(t_i >= s_i, by_query(g_cum) - by_key(g_cum), NEG)
    pairs = by_query(qf) * by_key(k) * jnp.exp(rel)
    a_all = _bdot(pairs.reshape(c * c, w), bd, "nn").reshape(c, c, w)
    o = jnp.sum(a_all * by_key(ai), axis=0)
    q_dec = qf * jnp.exp(g_cum)
    o = o + _bdot(q_dec, st, "nt")
    g_last = jnp.sum(jnp.where(rows == c - 1, g_cum, 0.0), axis=0, keepdims=True)
    k_end = k * jnp.exp(g_last - g_cum)
    kv = _bdot(ai, k_end, "tn")
    st_new = st * jnp.exp(g_last) + kv * bd
    ms = _bdot(o * o, bd, "nn") * (1.0 / HEAD_DIM)
    o = o * lax.rsqrt(ms + RMS_EPS) * norm_g
    return o * (ag * jax.nn.sigmoid(ag)), st_new


def _hgrn_fwd(proj, logits8, norm_g, layer, name, ts=256):
    bsz, seq, _ = proj.shape
    n_ch = ts // A_CHUNK

    def body(p_ref, lg_ref, ng_ref, o_ref, st_ref, st_scr):
        @pl.when(pl.program_id(1) == 0)
        def _():
            st_scr[...] = jnp.zeros_like(st_scr)

        consts = _hgrn_consts()
        logits_v, ng_v = lg_ref[...], ng_ref[...]

        def chunk(ci, carry):
            r = ci * A_CHUNK if isinstance(ci, int) else pl.multiple_of(ci * A_CHUNK, A_CHUNK)
            st = st_scr[...]
            st_ref[ci] = st
            o, st_new = _hgrn_chunk(
                p_ref[pl.ds(r, A_CHUNK), 0:256], p_ref[pl.ds(r, A_CHUNK), 256:512],
                p_ref[pl.ds(r, A_CHUNK), 512:768], p_ref[pl.ds(r, A_CHUNK), 768:1024],
                logits_v, ng_v, st, layer=layer, consts=consts)
            o_ref[pl.ds(r, A_CHUNK), :] = o.astype(o_ref.dtype)
            st_scr[...] = st_new
            return carry

        if n_ch <= 2:
            for c_static in range(n_ch):
                chunk(c_static, 0)
        else:
            lax.fori_loop(0, n_ch, chunk, 0, unroll=2)

    return pl.pallas_call(
        body, name=name, grid=(bsz, seq // ts),
        in_specs=[pl.BlockSpec((None, ts, 1024), lambda b, s: (b, s, 0)),
                  pl.BlockSpec((8, GROUP_WIDTH), lambda b, s: (0, 0)),
                  pl.BlockSpec((1, GROUP_WIDTH), lambda b, s: (0, 0))],
        out_specs=[pl.BlockSpec((None, ts, GROUP_WIDTH), lambda b, s: (b, s, 0)),
                   pl.BlockSpec((None, n_ch, GROUP_WIDTH, GROUP_WIDTH), lambda b, s: (b, s, 0, 0))],
        out_shape=[jax.ShapeDtypeStruct((bsz, seq, MO_W), BF16),
                   jax.ShapeDtypeStruct((bsz, seq // A_CHUNK, GROUP_WIDTH, GROUP_WIDTH), F32)],
        scratch_shapes=[pltpu.VMEM((GROUP_WIDTH, GROUP_WIDTH), F32)],
        compiler_params=_cparams(("parallel", "arbitrary")),
    )(proj, logits8, norm_g)


def _hgrn_bwd(dmo, proj, states, logits8, norm_g, layer, name, ts=256):
    bsz, seq, _ = proj.shape
    n_ch = ts // A_CHUNK
    n_s = seq // ts

    def body(do_ref, p_ref, st_ref, lg_ref, ng_ref, dp_ref, dlg_ref, dng_ref, dst_scr):
        b, s = pl.program_id(0), pl.program_id(1)

        @pl.when(s == 0)
        def _():
            dst_scr[...] = jnp.zeros_like(dst_scr)

        @pl.when(jnp.logical_and(b == 0, s == 0))
        def _():
            dlg_ref[...] = jnp.zeros_like(dlg_ref)
            dng_ref[...] = jnp.zeros_like(dng_ref)

        consts = _hgrn_consts()
        logits_v, ng_v = lg_ref[...], ng_ref[...]
        fn = functools.partial(_hgrn_chunk, layer=layer, consts=consts)

        def chunk(t, carry):
            ci = n_ch - 1 - t
            r = ci * A_CHUNK if isinstance(ci, int) else pl.multiple_of(ci * A_CHUNK, A_CHUNK)
            _, vjp = jax.vjp(
                fn, p_ref[pl.ds(r, A_CHUNK), 0:256], p_ref[pl.ds(r, A_CHUNK), 256:512],
                p_ref[pl.ds(r, A_CHUNK), 512:768], p_ref[pl.ds(r, A_CHUNK), 768:1024],
                logits_v, ng_v, st_ref[ci])
            daq, daf, dai, dag, dlg, dng, dst = vjp((do_ref[pl.ds(r, A_CHUNK), :], dst_scr[...]))
            dp_ref[pl.ds(r, A_CHUNK), 0:256] = daq.astype(dp_ref.dtype)
            dp_ref[pl.ds(r, A_CHUNK), 256:512] = daf.astype(dp_ref.dtype)
            dp_ref[pl.ds(r, A_CHUNK), 512:768] = dai.astype(dp_ref.dtype)
            dp_ref[pl.ds(r, A_CHUNK), 768:1024] = dag.astype(dp_ref.dtype)
            dlg_ref[...] += dlg
            dng_ref[...] += dng
            dst_scr[...] = dst
            return carry

        if n_ch <= 2:
            for c_static in range(n_ch):
                chunk(c_static, 0)
        else:
            lax.fori_loop(0, n_ch, chunk, 0, unroll=2)

    rev = lambda b, s: (b, n_s - 1 - s, 0)
    return pl.pallas_call(
        body, name=name, grid=(bsz, n_s),
        in_specs=[pl.BlockSpec((None, ts, GROUP_WIDTH), rev),
                  pl.BlockSpec((None, ts, 1024), rev),
                  pl.BlockSpec((None, n_ch, GROUP_WIDTH, GROUP_WIDTH), lambda b, s: (b, n_s - 1 - s, 0, 0)),
                  pl.BlockSpec((8, GROUP_WIDTH), lambda b, s: (0, 0)),
                  pl.BlockSpec((1, GROUP_WIDTH), lambda b, s: (0, 0))],
        out_specs=[pl.BlockSpec((None, ts, 1024), rev),
                   pl.BlockSpec((8, GROUP_WIDTH), lambda b, s: (0, 0)),
                   pl.BlockSpec((1, GROUP_WIDTH), lambda b, s: (0, 0))],
        out_shape=[jax.ShapeDtypeStruct((bsz, seq, PACK_W), BF16),
                   jax.ShapeDtypeStruct((8, GROUP_WIDTH), F32), jax.ShapeDtypeStruct((1, GROUP_WIDTH), F32)],
        scratch_shapes=[pltpu.VMEM((GROUP_WIDTH, GROUP_WIDTH), F32)],
        compiler_params=_cparams(("arbitrary", "arbitrary")),
    )(dmo, proj, states, logits8, norm_g)


def _rms_fn(x, g):
    return x * lax.rsqrt(jnp.mean(x * x, axis=-1, keepdims=True) + RMS_EPS) * g


def _tile4(t):
    return jnp.concatenate([t, t, t, t], axis=1)


def _rope(x, c, s1, s2):
    w = x.shape[-1]
    return x * c + pltpu.roll(x, 32, axis=1) * s2 + pltpu.roll(x, w - 32, axis=1) * s1


def _rope_t(dy, c, s1, s2):
    w = dy.shape[-1]
    return dy * c + pltpu.roll(dy * s2, w - 32, axis=1) + pltpu.roll(dy * s1, 32, axis=1)


def _mla_pre(proj, qg, kvg, wq, wkv, tabs, name, ts=256):
    bsz, seq, _ = proj.shape

    def body(p_ref, qg_ref, kvg_ref, wq_ref, wkv_ref, c_ref, s1_ref, s2_ref, q_ref, kv_ref):
        nq = _rms_fn(p_ref[:, 0:256], qg_ref[...])
        nkv = _rms_fn(p_ref[:, 256:384], kvg_ref[...])
        c, s1, s2 = c_ref[...], s1_ref[...], s2_ref[...]
        qp = jnp.dot(nq.astype(BF16), wq_ref[...], preferred_element_type=F32)
        q_ref[...] = _rope(qp, _tile4(c), _tile4(s1), _tile4(s2)).astype(q_ref.dtype)
        kv = jnp.dot(nkv.astype(BF16), wkv_ref[...], preferred_element_type=F32)
        krr = _rope(p_ref[:, 384:512], c, s1, s2)
        zero = jnp.zeros_like(krr)
        kv_ref[...] = (kv + jnp.concatenate([krr, zero] * N_HEADS, axis=1)).astype(kv_ref.dtype)

    tab_spec = pl.BlockSpec((ts, LANES), lambda b, s: (s, 0))
    return pl.pallas_call(
        body, name=name, grid=(bsz, seq // ts),
        in_specs=[pl.BlockSpec((None, ts, 512), lambda b, s: (b, s, P_B // 512)),
                  _vec_spec(256), _vec_spec(128),
                  pl.BlockSpec((256, 512), lambda b, s: (0, 0)), pl.BlockSpec((128, 1024), lambda b, s: (0, 0)),
                  tab_spec, tab_spec, tab_spec],
        out_specs=[_row_spec(ts, 512), _row_spec(ts, 1024)],
        out_shape=[jax.ShapeDtypeStruct((bsz, seq, 512), BF16), jax.ShapeDtypeStruct((bsz, seq, 1024), BF16)],
        compiler_params=_cparams(("parallel", "parallel")),
    )(proj, qg, kvg, wq, wkv, *tabs)


def _mla_pre_bwd(dq, dkv, dproj, proj, qg, kvg, wq, wkv, tabs, name, ts=256):
    bsz, seq, _ = proj.shape

    def body(dq_ref, dkv_ref, dp_any, p_ref, qg_ref, kvg_ref, wq_ref, wkv_ref, c_ref, s1_ref, s2_ref,
             dp_ref, dqg_ref, dkvg_ref, dwq_ref, dwkv_ref):
        del dp_any
        first = jnp.logical_and(pl.program_id(0) == 0, pl.program_id(1) == 0)

        @pl.when(first)
        def _():
            dqg_ref[...] = jnp.zeros_like(dqg_ref)
            dkvg_ref[...] = jnp.zeros_like(dkvg_ref)
            dwq_ref[...] = jnp.zeros_like(dwq_ref)
            dwkv_ref[...] = jnp.zeros_like(dwkv_ref)

        c, s1, s2 = c_ref[...], s1_ref[...], s2_ref[...]
        nq, vjp_q = jax.vjp(_rms_fn, p_ref[:, 0:256], qg_ref[...])
        nkv, vjp_kv = jax.vjp(_rms_fn, p_ref[:, 256:384], kvg_ref[...])
        dqp = _rope_t(dq_ref[...], _tile4(c), _tile4(s1), _tile4(s2)).astype(BF16)
        dkv_v = dkv_ref[...]
        dkv_b = dkv_v.astype(BF16)
        tn = (((0,), (0,)), ((), ()))
        nt = (((1,), (1,)), ((), ()))
        dwq_ref[...] += lax.dot_general(nq.astype(BF16), dqp, tn, preferred_element_type=F32)
        dwkv_ref[...] += lax.dot_general(nkv.astype(BF16), dkv_b, tn, preferred_element_type=F32)
        dcq, dqg = vjp_q(lax.dot_general(dqp, wq_ref[...], nt, preferred_element_type=F32))
        dckv, dkvg = vjp_kv(lax.dot_general(dkv_b, wkv_ref[...], nt, preferred_element_type=F32))
        dqg_ref[...] += dqg
        dkvg_ref[...] += dkvg
        dk_sum = dkv_v[:, 0:128] + dkv_v[:, 256:384] + dkv_v[:, 512:640] + dkv_v[:, 768:896]
        lane = lax.broadcasted_iota(jnp.int32, dk_sum.shape, 1)
        dkr = jnp.where(lane >= 64, _rope_t(dk_sum, c, s1, s2), 0.0)
        dp_ref[:, 0:256] = dcq.astype(dp_ref.dtype)
        dp_ref[:, 256:384] = dckv.astype(dp_ref.dtype)
        dp_ref[:, 384:512] = dkr.astype(dp_ref.dtype)

    tab_spec = pl.BlockSpec((ts, LANES), lambda b, s: (s, 0))
    const = lambda shape: pl.BlockSpec(shape, lambda b, s: (0, 0))
    return pl.pallas_call(
        body, name=name, grid=(bsz, seq // ts),
        in_specs=[_row_spec(ts, 512), _row_spec(ts, 1024), pl.BlockSpec(memory_space=pl.ANY),
                  pl.BlockSpec((None, ts, 512), lambda b, s: (b, s, P_B // 512)),
                  _vec_spec(256), _vec_spec(128), const((256, 512)), const((128, 1024)),
                  tab_spec, tab_spec, tab_spec],
        out_specs=[pl.BlockSpec((None, ts, 512), lambda b, s: (b, s, P_B // 512)),
                   _vec_spec(256), _vec_spec(128), const((256, 512)), const((128, 1024))],
        out_shape=[jax.ShapeDtypeStruct(dproj.shape, dproj.dtype), jax.ShapeDtypeStruct((1, 256), F32),
                   jax.ShapeDtypeStruct((1, 128), F32), jax.ShapeDtypeStruct((256, 512), F32),
                   jax.ShapeDtypeStruct((128, 1024), F32)],
        input_output_aliases={2: 0},
        compiler_params=_cparams(("arbitrary", "arbitrary")),
    )(dq, dkv, dproj, proj, qg, kvg, wq, wkv, *tabs)


def _fox_gate(proj, bf, name):
    bsz, seq, _ = proj.shape
    n_blk = seq // LANES

    def body(x_ref, bf_ref, f_ref):
        r_i = lax.broadcasted_iota(jnp.int32, (LANES, LANES), 0)
        c_i = lax.broadcasted_iota(jnp.int32, (LANES, LANES), 1)
        tril = (r_i >= c_i).astype(F32)
        bias = bf_ref[...]

        def blk(i, carry):
            r = pl.multiple_of(i * LANES, LANES)
            lf = _log_sigmoid(x_ref[pl.ds(r, LANES), :] + bias)
            f_ref[pl.ds(r, LANES), :] = jnp.dot(tril, lf, precision=HI, preferred_element_type=F32) + carry
            return carry + jnp.sum(lf, axis=0, keepdims=True)

        lax.fori_loop(0, n_blk, blk, jnp.zeros((1, LANES), F32))

    return pl.pallas_call(
        body, name=name, grid=(bsz,),
        in_specs=[pl.BlockSpec((None, seq, LANES), lambda b: (b, 0, P_CF // LANES)),
                  pl.BlockSpec((1, LANES), lambda b: (0, 0))],
        out_specs=pl.BlockSpec((None, seq, LANES), lambda b: (b, 0, 0)),
        out_shape=jax.ShapeDtypeStruct((bsz, seq, LANES), F32),
        compiler_params=_cparams(("parallel",)),
    )(proj, bf)


def _fox_gate_bwd(dfq, dfk_cols, dproj, proj, bf, name):
    bsz, seq, _ = proj.shape
    n_blk = seq // LANES

    def body(dfq_ref, dfk_ref, dp_any, x_ref, bf_ref, dp_ref, dbf_ref):
        del dp_any

        @pl.when(pl.program_id(0) == 0)
        def _():
            dbf_ref[...] = jnp.zeros_like(dbf_ref)

        r_i = lax.broadcasted_iota(jnp.int32, (LANES, LANES), 0)
        c_i = lax.broadcasted_iota(jnp.int32, (LANES, LANES), 1)
        triu = (r_i <= c_i).astype(F32)
        bias = bf_ref[...]

        def blk(t, carry):
            tail, dbf = carry
            r = pl.multiple_of((n_blk - 1 - t) * LANES, LANES)
            dc = dfk_ref[pl.ds(r, LANES), :]
            for hd in range(N_HEADS):
                dc = dc + jnp.where(c_i == hd, dfq_ref[hd, pl.ds(r, LANES), :], 0.0)
            dlf = jnp.dot(triu, dc, precision=HI, preferred_element_type=F32) + tail
            dx = dlf * (1.0 - jax.nn.sigmoid(x_ref[pl.ds(r, LANES), :] + bias))
            dp_ref[pl.ds(r, LANES), :] = dx.astype(dp_ref.dtype)
            return tail + jnp.sum(dc, axis=0, keepdims=True), dbf + jnp.sum(dx, axis=0, keepdims=True)

        z = jnp.zeros((1, LANES), F32)
        _, dbf = lax.fori_loop(0, n_blk, blk, (z, z))
        dbf_ref[...] += dbf

    return pl.pallas_call(
        body, name=name, grid=(bsz,),
        in_specs=[pl.BlockSpec((None, N_HEADS, seq, LANES), lambda b: (b, 0, 0, 0)),
                  pl.BlockSpec((None, seq, LANES), lambda b: (b, 0, 0)), pl.BlockSpec(memory_space=pl.ANY),
                  pl.BlockSpec((None, seq, LANES), lambda b: (b, 0, P_CF // LANES)),
                  pl.BlockSpec((1, LANES), lambda b: (0, 0))],
        out_specs=[pl.BlockSpec((None, seq, LANES), lambda b: (b, 0, P_CF // LANES)),
                   pl.BlockSpec((1, LANES), lambda b: (0, 0))],
        out_shape=[jax.ShapeDtypeStruct(dproj.shape, dproj.dtype), jax.ShapeDtypeStruct((1, LANES), F32)],
        input_output_aliases={2: 0},
        compiler_params=_cparams(("arbitrary",)),
    )(dfq, dfk_cols, dproj, proj, bf)


def _gate_terms(fc_ref, fr_ref, h, tq, tk):
    lane = lax.broadcasted_iota(jnp.int32, (tq, LANES), 1)
    fcol = jnp.sum(jnp.where(lane == h, fc_ref[...], 0.0), axis=1, keepdims=True)
    sub = lax.broadcasted_iota(jnp.int32, (8, tk), 0)
    frow = jnp.sum(jnp.where(sub == h, fr_ref[...], 0.0), axis=0, keepdims=True)
    return fcol - frow


def _scores(q_ref, k_ref, gate_refs, scale, h, masked, tq, tk):
    q = (q_ref[...].astype(F32) * scale).astype(BF16)
    s = lax.dot_general(q, k_ref[...].astype(BF16), _DN["nt"], preferred_element_type=F32)
    if gate_refs is not None:
        s = s + _gate_terms(gate_refs[0], gate_refs[1], h, tq, tk)
    if masked is not False:
        r_i = lax.broadcasted_iota(jnp.int32, (tq, tk), 0)
        c_i = lax.broadcasted_iota(jnp.int32, (tq, tk), 1)
        keep = c_i <= r_i
        s = jnp.where(keep if masked is True else jnp.logical_or(jnp.logical_not(masked), keep), s, NEG)
    return s, q


def _lanes(col):
    return jnp.broadcast_to(col, (col.shape[0], LANES))


def _attn_fwd(qa, q0, kva, kv0, mo, o0, gates, scale, name, tq=None):
    bsz, seq, _ = qa.shape
    tq = ATTN_TILE if tq is None else tq
    n_q = seq // tq
    gated = gates is not None

    def body(*refs):
        q_ref, k_ref, v_ref = refs[:3]
        gate_refs = refs[3:5] if gated else None
        o_ref, lse_ref, m_s, l_s, acc_s = refs[-5:]
        h, i, j = pl.program_id(1), pl.program_id(2), pl.program_id(3)

        @pl.when(j == 0)
        def _():
            m_s[...] = jnp.full_like(m_s, NEG)
            l_s[...] = jnp.zeros_like(l_s)
            acc_s[...] = jnp.zeros_like(acc_s)

        def step(masked):
            s, _ = _scores(q_ref, k_ref, gate_refs, scale, h, masked, tq, tq)
            m_prev = m_s[...]
            m_new = jnp.maximum(m_prev, jnp.max(s, axis=1, keepdims=True))
            alpha = jnp.exp(m_prev - m_new)
            p = jnp.exp(s - m_new)
            l_s[...] = alpha * l_s[...] + jnp.sum(p, axis=1, keepdims=True)
            acc_s[...] = alpha * acc_s[...] + jnp.dot(p.astype(BF16), v_ref[...].astype(BF16),
                                                      preferred_element_type=F32)
            m_s[...] = m_new

        @pl.when(j <= i)
        def _():
            step(j == i)

        @pl.when(j == i)
        def _():
            o_ref[...] = (acc_s[...] / l_s[...]).astype(o_ref.dtype)
            lse_ref[...] = _lanes(m_s[...] + jnp.log(l_s[...]))

    blk = (None, tq, LANES)
    in_specs = [pl.BlockSpec(blk, lambda b, h, i, j: (b, i, q0 + h)),
                pl.BlockSpec(blk, lambda b, h, i, j: (b, jnp.minimum(j, i), kv0 + 2 * h)),
                pl.BlockSpec(blk, lambda b, h, i, j: (b, jnp.minimum(j, i), kv0 + 2 * h + 1))]
    args = [qa, kva, kva]
    if gated:
        in_specs += [pl.BlockSpec(blk, lambda b, h, i, j: (b, i, 0)),
                     pl.BlockSpec((None, 8, tq), lambda b, h, i, j: (b, 0, jnp.minimum(j, i)))]
        args += list(gates)
    in_specs.append(pl.BlockSpec(memory_space=pl.ANY))
    args.append(mo)
    return pl.pallas_call(
        body, name=name, grid=(bsz, N_HEADS, n_q, n_q), in_specs=in_specs,
        out_specs=[pl.BlockSpec(blk, lambda b, h, i, j: (b, i, o0 + h)),
                   pl.BlockSpec((None, None, tq, LANES), lambda b, h, i, j: (b, h, i, 0))],
        out_shape=[jax.ShapeDtypeStruct(mo.shape, mo.dtype),
                   jax.ShapeDtypeStruct((bsz, N_HEADS, seq, LANES), F32)],
        scratch_shapes=[pltpu.VMEM((tq, 1), F32), pltpu.VMEM((tq, 1), F32), pltpu.VMEM((tq, LANES), F32)],
        input_output_aliases={len(args) - 1: 0},
        compiler_params=_cparams(("parallel", "parallel", "parallel", "arbitrary")),
    )(*args)


def _attn_bwd_q(qa, q0, kva, kv0, mo, dmo, o0, lse, gates, scale, out, out0, name, tq=None):
    bsz, seq, _ = qa.shape
    tq = ATTN_TILE if tq is None else tq
    n_q = seq // tq
    gated = gates is not None
    aliased = not isinstance(out, jax.ShapeDtypeStruct)

    def body(*refs):
        q_ref, k_ref, v_ref, o_ref, do_ref, lse_ref = refs[:6]
        gate_refs = refs[6:8] if gated else None
        dq_ref, delta_ref, dfq_ref, acc_s, dl_s, df_s = refs[-6:]
        h, i, j = pl.program_id(1), pl.program_id(2), pl.program_id(3)

        @pl.when(j == 0)
        def _():
            acc_s[...] = jnp.zeros_like(acc_s)
            df_s[...] = jnp.zeros_like(df_s)
            dl_s[...] = jnp.sum(do_ref[...] * o_ref[...].astype(F32), axis=1, keepdims=True)

        def step(masked):
            s, _ = _scores(q_ref, k_ref, gate_refs, scale, h, masked, tq, tq)
            p = jnp.exp(s - lse_ref[:, 0:1])
            dp = lax.dot_general(do_ref[...].astype(BF16), v_ref[...].astype(BF16), _DN["nt"],
                                 preferred_element_type=F32)
            ds = p * (dp - dl_s[...])
            acc_s[...] += jnp.dot(ds.astype(BF16), k_ref[...].astype(BF16), preferred_element_type=F32)
            df_s[...] += jnp.sum(ds, axis=1, keepdims=True)

        @pl.when(j <= i)
        def _():
            step(j == i)

        @pl.when(j == i)
        def _():
            dq_ref[...] = (acc_s[...] * scale).astype(dq_ref.dtype)
            delta_ref[...] = _lanes(dl_s[...])
            dfq_ref[...] = _lanes(df_s[...])

    blk = (None, tq, LANES)
    col = pl.BlockSpec((None, None, tq, LANES), lambda b, h, i, j: (b, h, i, 0))
    in_specs = [pl.BlockSpec(blk, lambda b, h, i, j: (b, i, q0 + h)),
                pl.BlockSpec(blk, lambda b, h, i, j: (b, jnp.minimum(j, i), kv0 + 2 * h)),
                pl.BlockSpec(blk, lambda b, h, i, j: (b, jnp.minimum(j, i), kv0 + 2 * h + 1)),
                pl.BlockSpec(blk, lambda b, h, i, j: (b, i, o0 + h)),
                pl.BlockSpec(blk, lambda b, h, i, j: (b, i, o0 + h)), col]
    args = [qa, kva, kva, mo, dmo, lse]
    if gated:
        in_specs += [pl.BlockSpec(blk, lambda b, h, i, j: (b, i, 0)),
                     pl.BlockSpec((None, 8, tq), lambda b, h, i, j: (b, 0, jnp.minimum(j, i)))]
        args += list(gates)
    aliases = {}
    if aliased:
        in_specs.append(pl.BlockSpec(memory_space=pl.ANY))
        args.append(out)
        aliases = {len(args) - 1: 0}
    vec = jax.ShapeDtypeStruct((bsz, N_HEADS, seq, LANES), F32)
    return pl.pallas_call(
        body, name=name, grid=(bsz, N_HEADS, n_q, n_q), in_specs=in_specs,
        out_specs=[pl.BlockSpec(blk, lambda b, h, i, j: (b, i, out0 + h)), col, col],
        out_shape=[jax.ShapeDtypeStruct(out.shape, out.dtype), vec, vec],
        scratch_shapes=[pltpu.VMEM((tq, LANES), F32), pltpu.VMEM((tq, 1), F32), pltpu.VMEM((tq, 1), F32)],
        input_output_aliases=aliases,
        compiler_params=_cparams(("parallel", "parallel", "parallel", "arbitrary")),
    )(*args)


def _attn_bwd_kv(qa, q0, kva, kv0, dmo, o0, lse, delta, gates, scale, out, out0, name, tq=None):
    bsz, seq, _ = qa.shape
    tq = ATTN_TILE if tq is None else tq
    n_q = seq // tq
    gated = gates is not None
    aliased = not isinstance(out, jax.ShapeDtypeStruct)

    def body(*refs):
        q_ref, k_ref, v_ref, do_ref, lse_ref, dl_ref = refs[:6]
        gate_refs = refs[6:8] if gated else None
        dkv_ref, dfk_ref, dk_s, dv_s, df_s = refs[-5:]
        h, j, i = pl.program_id(1), pl.program_id(2), pl.program_id(3)

        @pl.when(i == 0)
        def _():
            dk_s[...] = jnp.zeros_like(dk_s)
            dv_s[...] = jnp.zeros_like(dv_s)
            df_s[...] = jnp.zeros_like(df_s)

        def step(masked):
            s, q = _scores(q_ref, k_ref, gate_refs, scale, h, masked, tq, tq)
            p = jnp.exp(s - lse_ref[:, 0:1])
            do_b = do_ref[...].astype(BF16)
            dp = lax.dot_general(do_b, v_ref[...].astype(BF16), _DN["nt"], preferred_element_type=F32)
            ds = p * (dp - dl_ref[:, 0:1])
            dv_s[...] += lax.dot_general(p.astype(BF16), do_b, _DN["tn"], preferred_element_type=F32)
            dk_s[...] += lax.dot_general(ds.astype(BF16), q, _DN["tn"], preferred_element_type=F32)
            df_s[...] -= jnp.sum(ds, axis=0, keepdims=True)

        @pl.when(i > j)
        def _():
            step(False)

        @pl.when(i == j)
        def _():
            step(True)

        @pl.when(i == n_q - 1)
        def _():
            dkv_ref[:, 0:LANES] = dk_s[...].astype(dkv_ref.dtype)
            dkv_ref[:, LANES:2 * LANES] = dv_s[...].astype(dkv_ref.dtype)
            dfk_ref[...] = df_s[...]

    blk = (None, tq, LANES)
    col = pl.BlockSpec((None, None, tq, LANES), lambda b, h, j, i: (b, h, jnp.maximum(i, j), 0))
    in_specs = [pl.BlockSpec(blk, lambda b, h, j, i: (b, jnp.maximum(i, j), q0 + h)),
                pl.BlockSpec(blk, lambda b, h, j, i: (b, j, kv0 + 2 * h)),
                pl.BlockSpec(blk, lambda b, h, j, i: (b, j, kv0 + 2 * h + 1)),
                pl.BlockSpec(blk, lambda b, h, j, i: (b, jnp.maximum(i, j), o0 + h)), col, col]
    args = [qa, kva, kva, dmo, lse, delta]
    if gated:
        in_specs += [pl.BlockSpec(blk, lambda b, h, j, i: (b, jnp.maximum(i, j), 0)),
                     pl.BlockSpec((None, 8, tq), lambda b, h, j, i: (b, 0, j))]
        args += list(gates)
    aliases = {}
    if aliased:
        in_specs.append(pl.BlockSpec(memory_space=pl.ANY))
        args.append(out)
        aliases = {len(args) - 1: 0}
    return pl.pallas_call(
        body, name=name, grid=(bsz, N_HEADS, n_q, n_q), in_specs=in_specs,
        out_specs=[pl.BlockSpec((None, tq, 2 * LANES), lambda b, h, j, i: (b, j, out0 + h)),
                   pl.BlockSpec((None, None, 1, tq), lambda b, h, j, i: (b, h, 0, j))],
        out_shape=[jax.ShapeDtypeStruct(out.shape, out.dtype), jax.ShapeDtypeStruct((bsz, N_HEADS, 1, seq), F32)],
        scratch_shapes=[pltpu.VMEM((tq, LANES), F32), pltpu.VMEM((tq, LANES), F32), pltpu.VMEM((1, tq), F32)],
        input_output_aliases=aliases,
        compiler_params=_cparams(("parallel", "parallel", "parallel", "arbitrary")),
    )(*args)


def _block_logits(q, k_ref, gate, j, scale_unused, h, masked, tq):
    del scale_unused
    r = pl.multiple_of(j * tq, tq)
    s = lax.dot_general(q, k_ref[pl.ds(r, tq), :].astype(BF16), _DN["nt"], preferred_element_type=F32)
    if gate is not None:
        fcol, fr_ref = gate
        sub = lax.broadcasted_iota(jnp.int32, (8, tq), 0)
        frow = jnp.sum(jnp.where(sub == h, fr_ref[:, pl.ds(r, tq)], 0.0), axis=0, keepdims=True)
        s = s + (fcol - frow)
    if masked:
        r_i = lax.broadcasted_iota(jnp.int32, (tq, tq), 0)
        c_i = lax.broadcasted_iota(jnp.int32, (tq, tq), 1)
        s = jnp.where(c_i <= r_i, s, NEG)
    return s, r


def _gate_col(fc_ref, h, tq):
    lane = lax.broadcasted_iota(jnp.int32, (tq, LANES), 1)
    return jnp.sum(jnp.where(lane == h, fc_ref[...], 0.0), axis=1, keepdims=True)


def _attn_fwd_loop(qa, q0, kva, kv0, mo, o0, gates, scale, name, tq=None):
    bsz, seq, _ = qa.shape
    tq = ATTN_TILE if tq is None else tq
    n_q = seq // tq
    gated = gates is not None

    def body(*refs):
        q_ref, k_ref, v_ref = refs[:3]
        o_ref, lse_ref = refs[-2:]
        h, i = pl.program_id(1), pl.program_id(2)
        q = (q_ref[...].astype(F32) * scale).astype(BF16)
        gate = (_gate_col(refs[3], h, tq), refs[4]) if gated else None

        def step(j, carry, masked):
            m_prev, l_prev, acc = carry
            s, r = _block_logits(q, k_ref, gate, j, None, h, masked, tq)
            m_new = jnp.maximum(m_prev, jnp.max(s, axis=1, keepdims=True))
            alpha = jnp.exp(m_prev - m_new)
            p = jnp.exp(s - m_new)
            l_new = alpha * l_prev + jnp.sum(p, axis=1, keepdims=True)
            acc = alpha * acc + jnp.dot(p.astype(BF16), v_ref[pl.ds(r, tq), :].astype(BF16),
                                        preferred_element_type=F32)
            return m_new, l_new, acc

        init = (jnp.full((tq, 1), NEG, F32), jnp.zeros((tq, 1), F32), jnp.zeros((tq, LANES), F32))
        carry = lax.fori_loop(0, i, lambda j, c: step(j, c, False), init)
        m_f, l_f, acc = step(i, carry, True)
        o_ref[...] = (acc / l_f).astype(o_ref.dtype)
        lse_ref[...] = _lanes(m_f + jnp.log(l_f))

    blk = (None, tq, LANES)
    full = (None, seq, LANES)
    in_specs = [pl.BlockSpec(blk, lambda b, h, i: (b, i, q0 + h)),
                pl.BlockSpec(full, lambda b, h, i: (b, 0, kv0 + 2 * h)),
                pl.BlockSpec(full, lambda b, h, i: (b, 0, kv0 + 2 * h + 1))]
    args = [qa, kva, kva]
    if gated:
        in_specs += [pl.BlockSpec(blk, lambda b, h, i: (b, i, 0)),
                     pl.BlockSpec((None, 8, seq), lambda b, h, i: (b, 0, 0))]
        args += list(gates)
    in_specs.append(pl.BlockSpec(memory_space=pl.ANY))
    args.append(mo)
    return pl.pallas_call(
        body, name=name, grid=(bsz, N_HEADS, n_q), in_specs=in_specs,
        out_specs=[pl.BlockSpec(blk, lambda b, h, i: (b, i, o0 + h)),
                   pl.BlockSpec((None, None, tq, LANES), lambda b, h, i: (b, h, i, 0))],
        out_shape=[jax.ShapeDtypeStruct(mo.shape, mo.dtype),
                   jax.ShapeDtypeStruct((bsz, N_HEADS, seq, LANES), F32)],
        input_output_aliases={len(args) - 1: 0},
        compiler_params=_cparams(("parallel", "parallel", "parallel")),
    )(*args)


def _attn_bwd_q_loop(qa, q0, kva, kv0, mo, dmo, o0, lse, gates, scale, out, out0, name, tq=None):
    bsz, seq, _ = qa.shape
    tq = ATTN_TILE if tq is None else tq
    n_q = seq // tq
    gated = gates is not None
    aliased = not isinstance(out, jax.ShapeDtypeStruct)

    def body(*refs):
        q_ref, k_ref, v_ref, o_ref, do_ref, lse_ref = refs[:6]
        dq_ref, delta_ref, dfq_ref = refs[-3:]
        h, i = pl.program_id(1), pl.program_id(2)
        q = (q_ref[...].astype(F32) * scale).astype(BF16)
        gate = (_gate_col(refs[6], h, tq), refs[7]) if gated else None
        do_v = do_ref[...]
        do_b = do_v.astype(BF16)
        delta = jnp.sum(do_v * o_ref[...].astype(F32), axis=1, keepdims=True)
        lse_v = lse_ref[:, 0:1]

        def step(j, carry, masked):
            acc, dfq = carry
            s, r = _block_logits(q, k_ref, gate, j, None, h, masked, tq)
            p = jnp.exp(s - lse_v)
            dp = lax.dot_general(do_b, v_ref[pl.ds(r, tq), :].astype(BF16), _DN["nt"], preferred_element_type=F32)
            ds = p * (dp - delta)
            acc = acc + jnp.dot(ds.astype(BF16), k_ref[pl.ds(r, tq), :].astype(BF16), preferred_element_type=F32)
            return acc, dfq + jnp.sum(ds, axis=1, keepdims=True)

        init = (jnp.zeros((tq, LANES), F32), jnp.zeros((tq, 1), F32))
        carry = lax.fori_loop(0, i, lambda j, c: step(j, c, False), init)
        acc, dfq = step(i, carry, True)
        dq_ref[...] = (acc * scale).astype(dq_ref.dtype)
        delta_ref[...] = _lanes(delta)
        dfq_ref[...] = _lanes(dfq)

    blk = (None, tq, LANES)
    full = (None, seq, LANES)
    stat = pl.BlockSpec((None, None, tq, LANES), lambda b, h, i: (b, h, i, 0))
    in_specs = [pl.BlockSpec(blk, lambda b, h, i: (b, i, q0 + h)),
                pl.BlockSpec(full, lambda b, h, i: (b, 0, kv0 + 2 * h)),
                pl.BlockSpec(full, lambda b, h, i: (b, 0, kv0 + 2 * h + 1)),
                pl.BlockSpec(blk, lambda b, h, i: (b, i, o0 + h)),
                pl.BlockSpec(blk, lambda b, h, i: (b, i, o0 + h)), stat]
    args = [qa, kva, kva, mo, dmo, lse]
    if gated:
        in_specs += [pl.BlockSpec(blk, lambda b, h, i: (b, i, 0)),
                     pl.BlockSpec((None, 8, seq), lambda b, h, i: (b, 0, 0))]
        args += list(gates)
    aliases = {}
    if aliased:
        in_specs.append(pl.BlockSpec(memory_space=pl.ANY))
        args.append(out)
        aliases = {len(args) - 1: 0}
    vec = jax.ShapeDtypeStruct((bsz, N_HEADS, seq, LANES), F32)
    return pl.pallas_call(
        body, name=name, grid=(bsz, N_HEADS, n_q), in_specs=in_specs,
        out_specs=[pl.BlockSpec(blk, lambda b, h, i: (b, i, out0 + h)), stat, stat],
        out_shape=[jax.ShapeDtypeStruct(out.shape, out.dtype), vec, vec],
        input_output_aliases=aliases,
        compiler_params=_cparams(("parallel", "parallel", "parallel")),
    )(*args)


def _attn_bwd_kv_loop(qa, q0, kva, kv0, dmo, o0, lse, delta, gates, scale, out, out0, name, tq=None):
    bsz, seq, _ = qa.shape
    tq = ATTN_TILE if tq is None else tq
    n_q = seq // tq
    gated = gates is not None
    aliased = not isinstance(out, jax.ShapeDtypeStruct)

    def body(*refs):
        q_ref, k_ref, v_ref, do_ref, lse_ref, dl_ref = refs[:6]
        dkv_ref, dfk_ref = refs[-2:]
        h, j = pl.program_id(1), pl.program_id(2)
        k_b = k_ref[...].astype(BF16)
        v_b = v_ref[...].astype(BF16)
        if gated:
            fc_ref, fr_ref = refs[6], refs[7]
            sub = lax.broadcasted_iota(jnp.int32, (8, tq), 0)
            frow = jnp.sum(jnp.where(sub == h, fr_ref[...], 0.0), axis=0, keepdims=True)
            lane = lax.broadcasted_iota(jnp.int32, (tq, LANES), 1)

        def step(i, carry, masked):
            dk, dv, dfk = carry
            r = pl.multiple_of(i * tq, tq)
            q = (q_ref[pl.ds(r, tq), :].astype(F32) * scale).astype(BF16)
            s = lax.dot_general(q, k_b, _DN["nt"], preferred_element_type=F32)
            if gated:
                fcol = jnp.sum(jnp.where(lane == h, fc_ref[pl.ds(r, tq), :], 0.0), axis=1, keepdims=True)
                s = s + (fcol - frow)
            if masked:
                r_i = lax.broadcasted_iota(jnp.int32, (tq, tq), 0)
                c_i = lax.broadcasted_iota(jnp.int32, (tq, tq), 1)
                s = jnp.where(c_i <= r_i, s, NEG)
            p = jnp.exp(s - lse_ref[pl.ds(r, tq), 0:1])
            do_b = do_ref[pl.ds(r, tq), :].astype(BF16)
            dp = lax.dot_general(do_b, v_b, _DN["nt"], preferred_element_type=F32)
            ds = p * (dp - dl_ref[pl.ds(r, tq), 0:1])
            dv = dv + lax.dot_general(p.astype(BF16), do_b, _DN["tn"], preferred_element_type=F32)
            dk = dk + lax.dot_general(ds.astype(BF16), q, _DN["tn"], preferred_element_type=F32)
            return dk, dv, dfk - jnp.sum(ds, axis=0, keepdims=True)

        init = (jnp.zeros((tq, LANES), F32), jnp.zeros((tq, LANES), F32), jnp.zeros((1, tq), F32))
        carry = step(j, init, True)
        dk, dv, dfk = lax.fori_loop(j + 1, n_q, lambda i, c: step(i, c, False), carry)
        dkv_ref[:, 0:LANES] = dk.astype(dkv_ref.dtype)
        dkv_ref[:, LANES:2 * LANES] = dv.astype(dkv_ref.dtype)
        dfk_ref[...] = dfk

    blk = (None, tq, LANES)
    full = (None, seq, LANES)
    stat = pl.BlockSpec((None, None, seq, LANES), lambda b, h, j: (b, h, 0, 0))
    in_specs = [pl.BlockSpec(full, lambda b, h, j: (b, 0, q0 + h)),
                pl.BlockSpec(blk, lambda b, h, j: (b, j, kv0 + 2 * h)),
                pl.BlockSpec(blk, lambda b, h, j: (b, j, kv0 + 2 * h + 1)),
                pl.BlockSpec(full, lambda b, h, j: (b, 0, o0 + h)), stat, stat]
    args = [qa, kva, kva, dmo, lse, delta]
    if gated:
        in_specs += [pl.BlockSpec(full, lambda b, h, j: (b, 0, 0)),
                     pl.BlockSpec((None, 8, tq), lambda b, h, j: (b, 0, j))]
        args += list(gates)
    aliases = {}
    if aliased:
        in_specs.append(pl.BlockSpec(memory_space=pl.ANY))
        args.append(out)
        aliases = {len(args) - 1: 0}
    return pl.pallas_call(
        body, name=name, grid=(bsz, N_HEADS, n_q), in_specs=in_specs,
        out_specs=[pl.BlockSpec((None, tq, 2 * LANES), lambda b, h, j: (b, j, out0 + h)),
                   pl.BlockSpec((None, None, 1, tq), lambda b, h, j: (b, h, 0, j))],
        out_shape=[jax.ShapeDtypeStruct(out.shape, out.dtype), jax.ShapeDtypeStruct((bsz, N_HEADS, 1, seq), F32)],
        input_output_aliases=aliases,
        compiler_params=_cparams(("parallel", "parallel", "parallel")),
    )(*args)


def _gmlp_fn(uv, lng, lnb, ws, bst):
    u = jax.nn.gelu(uv[:, 0:GROUP_WIDTH])
    gv = jax.nn.gelu(uv[:, GROUP_WIDTH:2 * GROUP_WIDTH])
    mu = jnp.mean(gv, axis=-1, keepdims=True)
    vc = gv - mu
    var = jnp.mean(vc * vc, axis=-1, keepdims=True)
    vln = vc * lax.rsqrt(var + LN_EPS) * lng + lnb
    r_i = lax.broadcasted_iota(jnp.int32, (D_CHUNK, D_CHUNK), 0)
    c_i = lax.broadcasted_iota(jnp.int32, (D_CHUNK, D_CHUNK), 1)
    lane_g = lax.broadcasted_iota(jnp.int32, (D_CHUNK, GROUP_WIDTH), 1) // HEAD_DIM
    e_r = lax.broadcasted_iota(jnp.int32, (LANES, GROUP_WIDTH), 0)
    e_c = lax.broadcasted_iota(jnp.int32, (LANES, GROUP_WIDTH), 1)
    expand = (e_r == e_c // HEAD_DIM).astype(F32)
    mixed = jnp.dot(bst, expand, precision=HI, preferred_element_type=F32)
    for g in range(4):
        w = jnp.where(r_i >= c_i, ws[g], 0.0)
        mixed = mixed + jnp.where(lane_g == g, _bdot(w, vln, "nn"), 0.0)
    return u * mixed


def _gmlp_fwd(proj, mo, lng, lnb, ws, bst, name):
    bsz, seq, _ = proj.shape

    def body(p_ref, mo_any, lng_ref, lnb_ref, ws_ref, bst_ref, o_ref):
        del mo_any
        o_ref[...] = _gmlp_fn(p_ref[...], lng_ref[...], lnb_ref[...], ws_ref[...], bst_ref[...]).astype(o_ref.dtype)

    return pl.pallas_call(
        body, name=name, grid=(bsz, seq // D_CHUNK),
        in_specs=[pl.BlockSpec((None, D_CHUNK, 512), lambda b, s: (b, s, P_D // 512)),
                  pl.BlockSpec(memory_space=pl.ANY), _vec_spec(256), _vec_spec(256),
                  pl.BlockSpec((4, D_CHUNK, D_CHUNK), lambda b, s: (0, 0, 0)),
                  pl.BlockSpec((D_CHUNK, LANES), lambda b, s: (0, 0))],
        out_specs=pl.BlockSpec((None, D_CHUNK, GROUP_WIDTH), lambda b, s: (b, s, 1280 // GROUP_WIDTH)),
        out_shape=jax.ShapeDtypeStruct(mo.shape, mo.dtype),
        input_output_aliases={1: 0},
        compiler_params=_cparams(("parallel", "parallel")),
    )(proj, mo, lng, lnb, ws, bst)


def _gmlp_bwd(dmo, dproj, proj, lng, lnb, ws, bst, name):
    bsz, seq, _ = proj.shape

    def body(do_ref, dp_any, p_ref, lng_ref, lnb_ref, ws_ref, bst_ref, dp_ref, dlg_ref, dlb_ref, dws_ref, dbst_ref):
        del dp_any
        first = jnp.logical_and(pl.program_id(0) == 0, pl.program_id(1) == 0)

        @pl.when(first)
        def _():
            dlg_ref[...] = jnp.zeros_like(dlg_ref)
            dlb_ref[...] = jnp.zeros_like(dlb_ref)
            dws_ref[...] = jnp.zeros_like(dws_ref)
            dbst_ref[...] = jnp.zeros_like(dbst_ref)

        _, vjp = jax.vjp(_gmlp_fn, p_ref[...], lng_ref[...], lnb_ref[...], ws_ref[...], bst_ref[...])
        duv, dlg, dlb, dws, dbst = vjp(do_ref[...])
        dp_ref[...] = duv.astype(dp_ref.dtype)
        dlg_ref[...] += dlg
        dlb_ref[...] += dlb
        dws_ref[...] += dws
        dbst_ref[...] += dbst

    const2 = lambda shape: pl.BlockSpec(shape, lambda b, s: (0,) * len(shape))
    return pl.pallas_call(
        body, name=name, grid=(bsz, seq // D_CHUNK),
        in_specs=[pl.BlockSpec((None, D_CHUNK, GROUP_WIDTH), lambda b, s: (b, s, 1280 // GROUP_WIDTH)),
                  pl.BlockSpec(memory_space=pl.ANY),
                  pl.BlockSpec((None, D_CHUNK, 512), lambda b, s: (b, s, P_D // 512)),
                  _vec_spec(256), _vec_spec(256), const2((4, D_CHUNK, D_CHUNK)), const2((D_CHUNK, LANES))],
        out_specs=[pl.BlockSpec((None, D_CHUNK, 512), lambda b, s: (b, s, P_D // 512)),
                   _vec_spec(256), _vec_spec(256), const2((4, D_CHUNK, D_CHUNK)), const2((D_CHUNK, LANES))],
        out_shape=[jax.ShapeDtypeStruct(dproj.shape, dproj.dtype), jax.ShapeDtypeStruct((1, 256), F32),
                   jax.ShapeDtypeStruct((1, 256), F32), jax.ShapeDtypeStruct((4, D_CHUNK, D_CHUNK), F32),
                   jax.ShapeDtypeStruct((D_CHUNK, LANES), F32)],
        input_output_aliases={1: 0},
        compiler_params=_cparams(("arbitrary", "arbitrary")),
    )(dmo, dproj, proj, lng, lnb, ws, bst)


def _ada_fwd(c_all, ada_w, name):
    n_b = c_all.shape[0]
    depth, d, cols = ada_w.shape

    def body(c_ref, w_ref, o_ref):
        cv = c_ref[...]
        act = (cv * jax.nn.sigmoid(cv)).astype(BF16)
        o_ref[...] = jnp.dot(act, w_ref[...].astype(BF16), preferred_element_type=F32)

    return pl.pallas_call(
        body, name=name, grid=(depth,),
        in_specs=[pl.BlockSpec((n_b, d), lambda l: (0, 0)), pl.BlockSpec((None, d, cols), lambda l: (l, 0, 0))],
        out_specs=pl.BlockSpec((None, n_b, cols), lambda l: (l, 0, 0)),
        out_shape=jax.ShapeDtypeStruct((depth, n_b, cols), F32),
        compiler_params=_cparams(("parallel",)),
    )(c_all, ada_w)


def _ada_bwd(c_all, dmod_cols, dmod_full, name):
    n_b, d = c_all.shape
    depth, _, cols = dmod_cols.shape
    full = dmod_full.shape[-1]

    def body(c_ref, dm_ref, df_ref, gw_ref, gb_ref):
        cv = c_ref[...]
        act = (cv * jax.nn.sigmoid(cv)).astype(BF16)
        gw_ref[...] = lax.dot_general(act, dm_ref[...].astype(BF16), (((0,), (0,)), ((), ())),
                                      preferred_element_type=F32)
        gb_ref[...] = jnp.sum(df_ref[...], axis=0, keepdims=True)

    return pl.pallas_call(
        body, name=name, grid=(depth,),
        in_specs=[pl.BlockSpec((n_b, d), lambda l: (0, 0)), pl.BlockSpec((None, n_b, cols), lambda l: (l, 0, 0)),
                  pl.BlockSpec((None, n_b, full), lambda l: (l, 0, 0))],
        out_specs=[pl.BlockSpec((None, d, cols), lambda l: (l, 0, 0)),
                   pl.BlockSpec((None, 1, full), lambda l: (l, 0, 0))],
        out_shape=[jax.ShapeDtypeStruct((depth, d, cols), F32), jax.ShapeDtypeStruct((depth, 1, full), F32)],
        compiler_params=_cparams(("parallel",)),
    )(c_all, dmod_cols, dmod_full)


def _adamw(gparts, own, w, m, v, name, layer=0, prev=None):
    n_p, rows, cols = gparts.shape
    assert w.shape[1:] == (rows, cols)
    tr = rows
    if rows > 512:
        tr = next(c for c in range(512, 7, -8) if rows % c == 0)
    has_own = own is not None
    n_prev = 0 if prev is None else 4

    def body(*refs):
        if has_own:
            slot_ref, refs = refs[0], refs[1:]
        g_ref = refs[0]
        own_ref = refs[1] if has_own else None
        w_ref, m_ref, v_ref = refs[1 + has_own:4 + has_own]
        go_ref, do_ref, mo_ref, vo_ref = refs[4 + has_own + n_prev:]
        g = None
        for p in range(n_p):
            term = g_ref[p].astype(F32)
            if has_own:
                term = jnp.where(slot_ref[0] == p, own_ref[...].astype(F32), term)
            g = term if g is None else g + term
        m_new = ADAM_B1 * m_ref[...] + (1.0 - ADAM_B1) * g
        v_new = ADAM_B2 * v_ref[...] + (1.0 - ADAM_B2) * (g * g)
        m_hat = m_new / (1.0 - ADAM_B1 ** ADAM_STEP)
        v_hat = v_new / (1.0 - ADAM_B2 ** ADAM_STEP)
        go_ref[...] = g
        do_ref[...] = -ADAM_LR * (m_hat / (jnp.sqrt(v_hat) + ADAM_EPS) + ADAM_WD * w_ref[...])
        mo_ref[...] = m_new
        vo_ref[...] = v_new

    spec = pl.BlockSpec((None, tr, cols), lambda i, *_: (layer, i, 0))
    in_specs = [pl.BlockSpec((n_p, tr, cols), lambda i, *_: (0, i, 0))]
    args = [gparts]
    if has_own:
        in_specs.append(pl.BlockSpec((None, tr, cols), lambda i, slot: (slot[0], i, 0)))
        args.append(own[0])
    in_specs += [spec, spec, spec]
    args += [w, m, v]
    aliases = {}
    if prev is not None:
        aliases = {has_own + len(args) + k: k for k in range(4)}
        in_specs += [pl.BlockSpec(memory_space=pl.ANY)] * 4
        args += list(prev)
    shp = jax.ShapeDtypeStruct(w.shape, F32)
    out_specs, out_shape = [spec, spec, spec, spec], [shp, shp, shp, shp]
    if not has_own:
        return pl.pallas_call(
            body, name=name, grid=(rows // tr,), in_specs=in_specs, out_specs=out_specs, out_shape=out_shape,
            input_output_aliases=aliases, compiler_params=_cparams(("parallel",)),
        )(*args)
    return pl.pallas_call(
        body, name=name, out_shape=out_shape, input_output_aliases=aliases,
        grid_spec=pltpu.PrefetchScalarGridSpec(num_scalar_prefetch=1, grid=(rows // tr,), in_specs=in_specs,
                                               out_specs=out_specs),
        compiler_params=_cparams(("parallel",)),
    )(jnp.reshape(own[1], (1,)).astype(jnp.int32), *args)


def _sum_parts(parts, name):
    n_p, rows, cols = parts.shape
    tr = 256 if rows % 256 == 0 else rows

    def body(p_ref, o_ref):
        acc = p_ref[0]
        for p in range(1, n_p):
            acc = acc + p_ref[p]
        o_ref[...] = acc

    return pl.pallas_call(
        body, name=name, grid=(rows // tr,),
        in_specs=[pl.BlockSpec((n_p, tr, cols), lambda i: (0, i, 0))],
        out_specs=pl.BlockSpec((tr, cols), lambda i: (i, 0)),
        out_shape=jax.ShapeDtypeStruct((rows, cols), F32),
        compiler_params=_cparams(("parallel",)),
    )(parts)


def _all_gather(arrs, name):
    n = len(arrs)

    def body(*refs):
        in_refs, out_refs = refs[:n], refs[n:2 * n]
        send_sems, recv_sems, loc_sems = refs[2 * n:]
        x, y, c = lax.axis_index("x"), lax.axis_index("y"), lax.axis_index("c")
        me, sibling = (x, y, c), (x, y, 1 - c)
        chips = [(1 - x, y), (x, 1 - y), (1 - x, 1 - y)]

        def copy(a, k, block, to, src=None):
            slot = out_refs[a].at[4 * block[0] + 2 * block[1] + block[2]]
            return pltpu.make_async_remote_copy(
                src_ref=slot if src is None else src, dst_ref=slot, send_sem=send_sems.at[a, k],
                recv_sem=recv_sems.at[a, k], device_id=to, device_id_type=pl.DeviceIdType.MESH)

        mine = [pltpu.make_async_copy(in_refs[a], out_refs[a].at[4 * x + 2 * y + c], loc_sems.at[a])
                for a in range(n)]
        for cp in mine:
            cp.start()
        first = []
        for a in range(n):
            first.append(copy(a, 0, me, sibling, src=in_refs[a]))
            first += [copy(a, 1 + j, me, (*chip, c), src=in_refs[a]) for j, chip in enumerate(chips)]
        for cp in first:
            cp.start()
        passed = []
        for j, chip in enumerate(chips):
            for a in range(n):
                copy(a, 1 + j, (*chip, c), me).wait_recv()
                cp = copy(a, 4 + j, (*chip, c), sibling)
                cp.start()
                passed.append(cp)
        for a in range(n):
            copy(a, 0, sibling, me).wait_recv()
        for j, chip in enumerate(chips):
            for a in range(n):
                copy(a, 4 + j, (*chip, 1 - c), me).wait_recv()
        for cp in first + passed:
            cp.wait_send()
        for cp in mine:
            cp.wait()

    any_spec = pl.BlockSpec(memory_space=pl.ANY)
    return pl.pallas_call(
        body, name=name, in_specs=[any_spec] * n, out_specs=[any_spec] * n,
        out_shape=[jax.ShapeDtypeStruct((N_DEV,) + a.shape, a.dtype) for a in arrs],
        scratch_shapes=[pltpu.SemaphoreType.DMA((n, N_DEV - 1)), pltpu.SemaphoreType.DMA((n, N_DEV - 1)),
                        pltpu.SemaphoreType.DMA((n,))],
    )(*arrs)


def _flip_peers():
    x, y, c = lax.axis_index("x"), lax.axis_index("y"), lax.axis_index("c")
    peers = []
    for fx, fy, fc in [(fx, fy, fc) for fx in (0, 1) for fy in (0, 1) for fc in (0, 1)][1:]:
        px, py, pc = (1 - x if fx else x), (1 - y if fy else y), (1 - c if fc else c)
        peers.append(((px, py, pc), 4 * px + 2 * py + pc))
    return 4 * x + 2 * y + c, peers


def _push_start(srcs, name, whole=False):
    n, n_peer = len(srcs), N_DEV - 1
    if whole:
        me_w = 4 * lax.axis_index("x") + 2 * lax.axis_index("y") + lax.axis_index("c")
        lands = [lax.dynamic_update_slice_in_dim(lax.empty((N_DEV,) + a.shape, a.dtype), a[None], me_w, axis=0)
                 for a in srcs]
    else:
        lands = [lax.empty(a.shape, a.dtype) for a in srcs]

    def body(*refs):
        src_refs, land_refs = refs[:n], refs[n:2 * n]
        send_sems, recv_sems = refs[2 * n], refs[2 * n + 1]
        token = refs[-1]
        me, peers = _flip_peers()
        for k, (dev, idx) in enumerate(peers):
            for a in range(n):
                pltpu.make_async_remote_copy(
                    src_ref=src_refs[a] if whole else src_refs[a].at[idx], dst_ref=land_refs[a].at[me],
                    send_sem=send_sems.at[a * n_peer + k], recv_sem=recv_sems.at[a * n_peer + k], device_id=dev,
                    device_id_type=pl.DeviceIdType.MESH).start()
        token[...] = jnp.zeros_like(token)

    hbm = pl.BlockSpec(memory_space=pltpu.HBM)
    sem = pl.BlockSpec(memory_space=pltpu.SEMAPHORE)
    arrs = list(srcs) + lands
    res = pl.pallas_call(
        body, name=name, in_specs=[hbm] * (2 * n),
        out_specs=(sem, sem, *[hbm] * (2 * n), pl.BlockSpec(memory_space=pltpu.VMEM)),
        out_shape=(pltpu.SemaphoreType.DMA((n * n_peer,)), pltpu.SemaphoreType.DMA((n * n_peer,)),
                   *[pltpu.HBM(a.shape, a.dtype) for a in arrs], jax.ShapeDtypeStruct((8, LANES), F32)),
        input_output_aliases={i: 2 + i for i in range(2 * n)},
        compiler_params=pltpu.CompilerParams(has_side_effects=pltpu.SideEffectType.DATAFLOW_SIDE_EFFECTING),
    )(*[pltpu.with_memory_space_constraint(a, pltpu.HBM) for a in arrs])
    return res[0], res[1], list(res[2:2 + n]), list(res[2 + n:2 + 2 * n]), res[-1]


def _push_wait(send_sems, recv_sems, srcs, lands, after, name, whole=False):
    n, n_peer = len(srcs), N_DEV - 1

    def body(*refs):
        src_refs, land_refs = refs[:n], refs[n:2 * n]
        send_s, recv_s = refs[2 * n], refs[2 * n + 1]
        _, peers = _flip_peers()
        for k, (dev, idx) in enumerate(peers):
            for a in range(n):
                cp = pltpu.make_async_remote_copy(
                    src_ref=src_refs[a] if whole else src_refs[a].at[idx], dst_ref=land_refs[a].at[idx],
                    send_sem=send_s.at[a * n_peer + k],
                    recv_sem=recv_s.at[a * n_peer + k], device_id=dev, device_id_type=pl.DeviceIdType.MESH)
                cp.wait_send()
                cp.wait_recv()

    hbm = pl.BlockSpec(memory_space=pltpu.HBM)
    sem = pl.BlockSpec(memory_space=pltpu.SEMAPHORE)
    arrs = list(srcs) + list(lands)
    res = pl.pallas_call(
        body, name=name, in_specs=[hbm] * (2 * n) + [sem, sem, pl.BlockSpec(memory_space=pl.ANY)],
        out_specs=tuple([hbm] * (2 * n)), out_shape=tuple(pltpu.HBM(a.shape, a.dtype) for a in arrs),
        input_output_aliases={i: i for i in range(2 * n)},
        compiler_params=pltpu.CompilerParams(has_side_effects=pltpu.SideEffectType.DATAFLOW_SIDE_EFFECTING),
    )(*arrs, send_sems, recv_sems, after)
    return list(res[:n]), list(res[n:])


def _ffn_fwd(x, h, mod, w_in, w_out_after, lng, lnb, rows, tag, nxt):
    bsz, seq, d = x.shape
    t = bsz * seq
    if h is None:
        h = _modulate(x, mod, rows[0], rows[1], f"modulate_{tag}")
    z, a = _ffn_in_swiglu(h.reshape(t, d), w_in, f"ffn_in_{tag}")
    f = _matmul_groupsum(a, w_out_after(a), out_dtype=F32, tm=512, name=f"ffn_out_{tag}").reshape(bsz, seq, d)
    y, h_next = _res_ln(x, f, mod, lng, lnb, rows[2], 0.5, f"res_ln_{tag}", nxt)
    return y, h_next, (x, h, z, a, f)


def _tied(mod, tie):
    return mod if tie is None else mod + tie


def _open_tail(tail):
    dh, x, mod, dx_res, sc_row = tail
    return dx_res, (dh, x, mod, sc_row)


def _ffn_bwd(dy, pre, saved, mod, w_in, w_out, lng, lnb, rows, tag, ready):
    x, h, z, a, f = saved
    bsz, seq, d = x.shape
    t = bsz * seq
    (dx_res, df, dgate, dlg, dlb), closed = _res_ln_bwd(dy, x, f, mod, lng, lnb, rows[2], 0.5,
                                                       f"res_ln_bwd_{tag}", pre)
    df2 = df.reshape(1, t, d)
    dw_out = _matmul(a, df2, mode="tn", group_out=True, out_dtype=BF16, tm=a.shape[2], tk=min(t, 2048),
                     name=f"ffn_out_dw_{tag}")
    tie_out = ready(f"{tag}_out", dw_out)
    dz = _ffn_out_dx_swiglu(df.reshape(t, d), w_out, z, f"ffn_out_dx_{tag}").reshape(N_DEV, t, -1)
    dw_in = _matmul(dz, h.reshape(1, t, d), mode="tn", group_out=True, out_dtype=BF16, tm=dz.shape[2],
                    tk=min(t, 2048), name=f"ffn_in_dw_{tag}")
    tie_in = ready(f"{tag}_in", dw_in)
    dh = _matmul_groupsum(dz, w_in, out_dtype=F32, tm=512, name=f"ffn_in_dx_{tag}").reshape(bsz, seq, d)
    tail = (dh, x, _tied(_tied(mod, tie_out), tie_in), dx_res, rows[1])
    return tail, closed, dgate, dw_in, dw_out, dlg, dlb


def _mixer_fwd(x, h, mod, wts, small, lng, lnb, layer, tabs):
    bsz, seq, d = x.shape
    t = bsz * seq
    proj = _matmul(h.reshape(1, t, d), wts["mix_in"][None], mode="nn", group_out=True, out_dtype=F32, tm=512, tk=d,
                   name="mix_in").reshape(bsz, seq, PACK_W)
    mo, states = _hgrn_fwd(proj, small["lb_logits8"], small["hgrn_norm_g"], layer, f"hgrn_fwd_l{layer}")
    q, kv = _mla_pre(proj, small["q_norm_g"], small["kv_norm_g"], wts["uq"], wts["ukv"], tabs, "mla_pre")
    mla_scale = float((B_NOPE + B_ROPE) ** -0.5)
    mo, lse_b = _attn_fwd_loop(q, 0, kv, 0, mo, 2, None, mla_scale, "mla_attn_fwd")
    fg = _fox_gate(proj, small["fox_b_f"], "fox_gate")
    gates = (fg, jnp.swapaxes(fg[:, :, 0:8], 1, 2))
    fox_scale = float(HEAD_DIM ** -0.5)
    mo, lse_c = _attn_fwd_loop(proj, P_CQ // LANES, proj, P_CKV // LANES, mo, 6, gates, fox_scale, "fox_attn_fwd")
    mo = _gmlp_fwd(proj, mo, small["gmlp_ln_g"], small["gmlp_ln_b"], small["gmlp_w_s"], small["gmlp_bst"],
                   "gmlp_fwd")
    mixed = _matmul(mo.reshape(1, t, MO_W), wts["mix_out"][None], mode="nn", group_out=True, out_dtype=F32,
                    tm=1024, tk=MO_W, name="mix_out").reshape(bsz, seq, d)
    y, h_next = _res_ln(x, mixed, mod, lng, lnb, 5, 1.0, "res_ln_mix", (mod, 6, 7))
    return y, h_next, (x, h, proj, mo, states, q, kv, lse_b, gates, lse_c, mixed)


def _mixer_bwd(dy, pre, saved, mod, wts, small, lng, lnb, layer, tabs, ready):
    x, h, proj, mo, states, q, kv, lse_b, gates, lse_c, mixed = saved
    bsz, seq, d = x.shape
    t = bsz * seq
    (dx_res, dmixed, dgate, dlg, dlb), closed = _res_ln_bwd(dy, x, mixed, mod, lng, lnb, 5, 1.0, "res_ln_bwd_mix",
                                                           pre)
    dm2 = dmixed.reshape(1, t, d)
    dmo = _matmul(dm2, wts["mix_out"][None], mode="nt", group_out=True, out_dtype=F32, tm=1024, tk=d,
                  name="mix_out_dx").reshape(bsz, seq, MO_W)
    dw_out = _matmul(mo.reshape(1, t, MO_W), dm2, mode="tn", group_out=True, out_dtype=F32, tm=512, tk=min(t, 2048),
                     name="mix_out_dw")[0]
    tie_out = ready("mix_out", dw_out)
    g = {}
    dproj, g["lb_logits8"], g["hgrn_norm_g"] = _hgrn_bwd(dmo, proj, states, small["lb_logits8"],
                                                         small["hgrn_norm_g"], layer, f"hgrn_bwd_l{layer}")
    mla_scale = float((B_NOPE + B_ROPE) ** -0.5)
    dq, delta_b, _ = _attn_bwd_q_loop(q, 0, kv, 0, mo, dmo, 2, lse_b, None, mla_scale,
                                 jax.ShapeDtypeStruct((bsz, seq, 512), F32), 0, "mla_attn_bwd_q")
    dkv, _ = _attn_bwd_kv_loop(q, 0, kv, 0, dmo, 2, lse_b, delta_b, None, mla_scale,
                          jax.ShapeDtypeStruct((bsz, seq, 1024), F32), 0, "mla_attn_bwd_kv")
    dproj, g["q_norm_g"], g["kv_norm_g"], g["uq"], g["ukv"] = _mla_pre_bwd(
        dq, dkv, dproj, proj, small["q_norm_g"], small["kv_norm_g"], wts["uq"], wts["ukv"], tabs, "mla_pre_bwd")
    ready("mla_uq", g.pop("uq"))
    ready("mla_ukv", g.pop("ukv"))
    fox_scale = float(HEAD_DIM ** -0.5)
    dproj, delta_c, dfq = _attn_bwd_q_loop(proj, P_CQ // LANES, proj, P_CKV // LANES, mo, dmo, 6, lse_c, gates,
                                      fox_scale, dproj, P_CQ // LANES, "fox_attn_bwd_q")
    dproj, dfk = _attn_bwd_kv_loop(proj, P_CQ // LANES, proj, P_CKV // LANES, dmo, 6, lse_c, delta_c, gates, fox_scale,
                              dproj, P_CKV // (2 * LANES), "fox_attn_bwd_kv")
    dfk_cols = jnp.pad(jnp.swapaxes(dfk[:, :, 0, :], 1, 2), ((0, 0), (0, 0), (0, LANES - N_HEADS)))
    dproj, g["fox_b_f"] = _fox_gate_bwd(dfq, dfk_cols, dproj, proj, small["fox_b_f"], "fox_gate_bwd")
    dproj, g["gmlp_ln_g"], g["gmlp_ln_b"], g["gmlp_w_s"], g["gmlp_bst"] = _gmlp_bwd(
        dmo, dproj, proj, small["gmlp_ln_g"], small["gmlp_ln_b"], small["gmlp_w_s"], small["gmlp_bst"], "gmlp_bwd")
    dp2 = dproj.reshape(1, t, PACK_W)
    dw_in = _matmul(h.reshape(1, t, d), dp2, mode="tn", group_out=True, out_dtype=BF16, tm=512, tk=1024,
                    name="mix_in_dw")[0]
    tie_in = ready("mix_in", dw_in)
    dh = _matmul(dp2, wts["mix_in"][None], mode="nt", group_out=True, out_dtype=F32, tm=512, tk=PACK_W,
                 name="mix_in_dx").reshape(bsz, seq, d)
    tail = (dh, x, _tied(_tied(mod, tie_out), tie_in), dx_res, 4)
    return tail, closed, dgate, dw_in, dw_out, g, dlg, dlb


def _small_views(p, layer):
    return {
        "lb_logits8": jnp.pad(p["hgrn_lb_logits"], ((0, 8 - DEPTH), (0, 0))),
        "hgrn_norm_g": p["hgrn_norm_g"][layer][None],
        "q_norm_g": p["mla_q_norm_g"][layer][None],
        "kv_norm_g": p["mla_kv_norm_g"][layer][None],
        "fox_b_f": jnp.pad(p["fox_b_f"][layer][None], ((0, 0), (0, LANES - N_HEADS))),
        "gmlp_ln_g": p["gmlp_ln_g"][layer][None],
        "gmlp_ln_b": p["gmlp_ln_b"][layer][None],
        "gmlp_w_s": p["gmlp_w_s"][layer],
        "gmlp_bst": jnp.pad(p["gmlp_b_s"][layer].T, ((0, 0), (0, LANES - N_HEADS))),
    }


def _local_step(x, mod, target, weights, p, grads_ready=None):
    bsz, seq, d = x.shape
    tabs = _rope_tables(seq)
    saved = []
    h = None
    for l in range(DEPTH):
        sm = _small_views(p, l)
        lng, lnb = p["ln_g"][l], p["ln_b"][l]
        x, h, s1 = _ffn_fwd(x, h, mod[l], weights(l, "ffn1_in", x)["ffn1_in"],
                            lambda a, l=l: weights(l, "ffn1_out", a)["ffn1_out"], lng[0:1], lnb[0:1], (0, 1, 2),
                            "ffn1", (mod[l], 3, 4))
        x, h, s2 = _mixer_fwd(x, h, mod[l], weights(l, "mix", x), sm, lng[1:2], lnb[1:2], l, tabs)
        x, h, s3 = _ffn_fwd(x, h, mod[l], weights(l, "ffn2_in", x)["ffn2_in"],
                            lambda a, l=l: weights(l, "ffn2_out", a)["ffn2_out"], lng[2:3], lnb[2:3], (6, 7, 8),
                            "ffn2", (mod[l + 1], 0, 1) if l + 1 < DEPTH else None)
        saved.append((s1, s2, s3))
    dx, loss = _loss_head(x, target, "loss_head")
    big, small, dmods = [None] * DEPTH, [None] * DEPTH, [None] * DEPTH
    ties = []
    tail, rows_of = None, {}

    def tied(a):
        for t in ties:
            a = a + t
        return a

    for l in reversed(range(DEPTH)):
        w = {}
        for part in ("ffn1_in", "ffn1_out", "mix", "ffn2_in", "ffn2_out"):
            w.update(weights(l, part, None))
        sm = _small_views(p, l)
        lng, lnb = p["ln_g"][l], p["ln_b"][l]
        s1, s2, s3 = saved[l]

        def ready(name, grad, l=l):
            tie = None if grads_ready is None else grads_ready(l, name, grad)
            if tie is not None:
                ties.append(tie)
            return tie

        dy, pre = (dx, None) if tail is None else _open_tail(tail)
        tail, closed, dgate3, dwi2, dwo2, dlg2, dlb2 = _ffn_bwd(dy, pre, s3, tied(mod[l]), w["ffn2_in"],
                                                                w["ffn2_out"], lng[2:3], lnb[2:3], (6, 7, 8), "ffn2",
                                                                ready)
        if closed is not None:
            rows_of[(l + 1, 0)], rows_of[(l + 1, 1)] = closed
        dy, pre = _open_tail(tail)
        tail, closed, dgate2, dwmi, dwmo, g, dlg1, dlb1 = _mixer_bwd(dy, pre, s2, tied(mod[l]), w, sm, lng[1:2],
                                                                     lnb[1:2], l, tabs, ready)
        rows_of[(l, 6)], rows_of[(l, 7)] = closed
        dy, pre = _open_tail(tail)
        tail, closed, dgate1, dwi1, dwo1, dlg0, dlb0 = _ffn_bwd(dy, pre, s1, tied(mod[l]), w["ffn1_in"],
                                                                w["ffn1_out"], lng[0:1], lnb[0:1], (0, 1, 2), "ffn1",
                                                                ready)
        rows_of[(l, 3)], rows_of[(l, 4)] = closed
        rows_of[(l, 2)], rows_of[(l, 5)], rows_of[(l, 8)] = dgate1, dgate2, dgate3
        big[l] = {"ffn1_in": dwi1, "ffn1_out": dwo1, "ffn2_in": dwi2, "ffn2_out": dwo2, "mix_in": dwmi,
                  "mix_out": dwmo}
        g["ln_g"] = jnp.concatenate([dlg0, dlg1, dlg2], axis=0)
        g["ln_b"] = jnp.concatenate([dlb0, dlb1, dlb2], axis=0)
        small[l] = g
    dh, x0, mod0, dx_res, sc_row = tail
    dx, rows_of[(0, 0)], rows_of[(0, 1)] = _modulate_bwd(dh, x0, mod0, dx_res, sc_row, "modulate_bwd_ffn1")
    dmods = [jnp.concatenate([rows_of[(l, r)] for r in range(N_MOD)], axis=1) for l in range(DEPTH)]
    return loss, dx, jnp.stack(dmods), big, small


_BIG = ("ffn1_in", "ffn1_out", "ffn2_in", "ffn2_out", "mix_in", "mix_out")


def _small_grad_list(small, loss):
    def both(fn):
        return jnp.stack([fn(small[l]) for l in range(DEPTH)])

    return [
        ("loss", loss.reshape(1)),
        ("ln_g", both(lambda g: g["ln_g"])), ("ln_b", both(lambda g: g["ln_b"])),
        ("hgrn_lb_logits", small[0]["lb_logits8"][:DEPTH] + small[1]["lb_logits8"][:DEPTH]),
        ("hgrn_norm_g", both(lambda g: g["hgrn_norm_g"][0])),
        ("mla_q_norm_g", both(lambda g: g["q_norm_g"][0])),
        ("mla_kv_norm_g", both(lambda g: g["kv_norm_g"][0])),
        ("fox_b_f", both(lambda g: g["fox_b_f"][0, :N_HEADS])),
        ("gmlp_ln_g", both(lambda g: g["gmlp_ln_g"][0])), ("gmlp_ln_b", both(lambda g: g["gmlp_ln_b"][0])),
        ("gmlp_w_s", both(lambda g: g["gmlp_w_s"])),
        ("gmlp_b_s", both(lambda g: g["gmlp_bst"][:, :N_HEADS].T)),
    ]


_PACK_COLS = 512


def _pack_small(items):
    flat = jnp.concatenate([a.reshape(-1).astype(F32) for _, a in items])
    n = flat.shape[0]
    tile = 8 * _PACK_COLS
    flat = jnp.pad(flat, (0, (-n) % tile))
    return flat.reshape(-1, _PACK_COLS)


def _unpack_small(buf, items):
    flat = buf.reshape(-1)
    out, off = {}, 0
    for name, a in items:
        out[name] = flat[off:off + a.size].reshape(a.shape)
        off += a.size
    return out


def _as2d(a):
    return a.reshape(-1, a.shape[-1])


def kernel(x, c, ada_w, ada_b, ln_g, ln_b, ffn1_w_in, ffn1_w_out, ffn2_w_in, ffn2_w_out, mix_w_in, mix_w_out, hgrn_lb_logits, hgrn_norm_g, mla_q_norm_g, mla_kv_norm_g, mla_w_uq, mla_w_ukv, fox_b_f, gmlp_ln_g, gmlp_ln_b, gmlp_w_s, gmlp_b_s, loss_target, m_ada_w, m_ada_b, m_ln_g, m_ln_b, m_ffn1_w_in, m_ffn1_w_out, m_ffn2_w_in, m_ffn2_w_out, m_mix_w_in, m_mix_w_out, m_hgrn_lb_logits, m_hgrn_norm_g, m_mla_q_norm_g, m_mla_kv_norm_g, m_mla_w_uq, m_mla_w_ukv, m_fox_b_f, m_gmlp_ln_g, m_gmlp_ln_b, m_gmlp_w_s, m_gmlp_b_s, v_ada_w, v_ada_b, v_ln_g, v_ln_b, v_ffn1_w_in, v_ffn1_w_out, v_ffn2_w_in, v_ffn2_w_out, v_mix_w_in, v_mix_w_out, v_hgrn_lb_logits, v_hgrn_norm_g, v_mla_q_norm_g, v_mla_kv_norm_g, v_mla_w_uq, v_mla_w_ukv, v_fox_b_f, v_gmlp_ln_g, v_gmlp_ln_b, v_gmlp_w_s, v_gmlp_b_s):
    names = ["ada_w", "ada_b", "ln_g", "ln_b", "ffn1_w_in", "ffn1_w_out", "ffn2_w_in", "ffn2_w_out", "mix_w_in",
             "mix_w_out", "hgrn_lb_logits", "hgrn_norm_g", "mla_q_norm_g", "mla_kv_norm_g", "mla_w_uq", "mla_w_ukv",
             "fox_b_f", "gmlp_ln_g", "gmlp_ln_b", "gmlp_w_s", "gmlp_b_s"]
    w = dict(zip(names, [ada_w, ada_b, ln_g, ln_b, ffn1_w_in, ffn1_w_out, ffn2_w_in, ffn2_w_out, mix_w_in, mix_w_out,
                         hgrn_lb_logits, hgrn_norm_g, mla_q_norm_g, mla_kv_norm_g, mla_w_uq, mla_w_ukv, fox_b_f,
                         gmlp_ln_g, gmlp_ln_b, gmlp_w_s, gmlp_b_s]))
    m = dict(zip(names, [m_ada_w, m_ada_b, m_ln_g, m_ln_b, m_ffn1_w_in, m_ffn1_w_out, m_ffn2_w_in, m_ffn2_w_out,
                         m_mix_w_in, m_mix_w_out, m_hgrn_lb_logits, m_hgrn_norm_g, m_mla_q_norm_g, m_mla_kv_norm_g,
                         m_mla_w_uq, m_mla_w_ukv, m_fox_b_f, m_gmlp_ln_g, m_gmlp_ln_b, m_gmlp_w_s, m_gmlp_b_s]))
    v = dict(zip(names, [v_ada_w, v_ada_b, v_ln_g, v_ln_b, v_ffn1_w_in, v_ffn1_w_out, v_ffn2_w_in, v_ffn2_w_out,
                         v_mix_w_in, v_mix_w_out, v_hgrn_lb_logits, v_hgrn_norm_g, v_mla_q_norm_g, v_mla_kv_norm_g,
                         v_mla_w_uq, v_mla_w_ukv, v_fox_b_f, v_gmlp_ln_g, v_gmlp_ln_b, v_gmlp_w_s, v_gmlp_b_s]))
    bsz, seq, d = x.shape
    me = 4 * lax.axis_index("x") + 2 * lax.axis_index("y") + lax.axis_index("c")
    mix_src, uq_src, ukv_src, mo_src = _mix_in_src(), _uq_src(), _ukv_src(), _mo_src()

    part_names = {"ffn1_in": ["ffn1_w_in"], "ffn1_out": ["ffn1_w_out"],
                  "mix": ["mix_w_in", "mix_w_out", "mla_w_uq", "mla_w_ukv"],
                  "ffn2_in": ["ffn2_w_in"], "ffn2_out": ["ffn2_w_out"]}
    group_of = {(l, part): (l, part) for l in range(DEPTH) for part in part_names}
    in_flight = {}
    transposed = ("ffn1_w_in", "ffn2_w_in")

    def start_group(key, behind=None):
        members = [(l, part) for (l, part), g in group_of.items() if g == key]
        labels = [(l, n) for l, part in members for n in part_names[part]]
        shards = []
        for l, n in labels:
            a = w[n][l]
            if n == "mix_w_in":
                a = _pack_cols(a, mix_src)
            if n in transposed:
                a = jnp.swapaxes(w[n], 1, 2)[l]
            shards.append(a.astype(BF16))
        if behind is not None:
            shards, _ = lax.optimization_barrier((shards, behind))
        in_flight[key] = (labels, _push_start(shards, f"gather_start_{key[0]}_{key[1]}", whole=True))

    keys_in_order = list(dict.fromkeys(group_of.values()))
    start_group(keys_in_order[0])

    gathered = _all_gather([c, ln_g, ln_b], "gather_inputs")
    c_all = gathered[0].reshape(N_DEV * bsz, d)
    ln_g_full = jnp.moveaxis(gathered[1], 0, 2).reshape(DEPTH, 3, d)
    ln_b_full = jnp.moveaxis(gathered[2], 0, 2).reshape(DEPTH, 3, d)

    mod_cols = _ada_fwd(c_all, ada_w, "ada_fwd")
    mod_all, = _all_gather([mod_cols], "gather_mod")
    mod_mine = lax.dynamic_slice_in_dim(mod_all, me * bsz, bsz, axis=2)
    mod = jnp.moveaxis(mod_mine, 0, 2).reshape(DEPTH, bsz, N_MOD * d) + ada_b[:, None, :]
    for key in keys_in_order[1:]:
        start_group(key, behind=mod)
    tie = sum(h[-1][0, 0] for _, h in in_flight.values())
    mod = mod.reshape(DEPTH, bsz, N_MOD, d) + tie

    arrived, laid_out = {}, {}

    def weights(l, part, after):
        if (l, part) not in laid_out:
            laid_out[(l, part)] = lay_out(l, part, after)
        return laid_out[(l, part)]

    def lay_out(l, part, after):
        key = group_of[(l, part)]
        if key not in arrived:
            labels, (send_sems, recv_sems, srcs, lands, _) = in_flight[key]
            _, lands = _push_wait(send_sems, recv_sems, srcs, lands, after, f"gather_wait_{key[0]}_{key[1]}",
                                  whole=True)
            arrived[key] = dict(zip(labels, lands))
        gw = {n: arrived[key][(l, n)] for n in part_names[part]}
        if part.endswith("_in"):
            return {part: gw[part_names[part][0]]}
        if part.endswith("_out"):
            return {part: gw[part_names[part][0]].reshape(4, 704, d)}
        uq = jnp.moveaxis(gw["mla_w_uq"], 0, 1).reshape(256, 384)
        ukv = jnp.moveaxis(gw["mla_w_ukv"], 0, 1).reshape(128, 512)
        return {"mix_in": gw["mix_w_in"].reshape(d, PACK_W),
                "mix_out": _pack_cols(gw["mix_w_out"].reshape(d, d).T, mo_src).T,
                "uq": _pack_cols(uq, uq_src), "ukv": _pack_cols(ukv, ukv_src)}

    p = dict(w)
    p["ln_g"], p["ln_b"] = ln_g_full, ln_b_full
    def chunks(name, arr):
        if name in ("ffn1_in", "ffn2_in"):
            return arr
        if name in ("ffn1_out", "ffn2_out"):
            return arr.reshape(N_DEV, arr.shape[1] // 2, d)
        if name == "mix_in":
            return _unpack_cols(arr, mix_src, MIX_ORIG_W).reshape(N_DEV, d // N_DEV, MIX_ORIG_W)
        if name in ("mla_uq", "mla_ukv"):
            full_w = _unpack_cols(arr, uq_src, 384) if name == "mla_uq" else _unpack_cols(arr, ukv_src, 512)
            rows = full_w.shape[0]
            return jnp.moveaxis(full_w.reshape(rows, N_DEV, -1), 1, 0).astype(BF16)
        return _unpack_cols(arr.T, mo_src, d).T.astype(BF16).reshape(N_DEV, d // N_DEV, d)

    pending, started = {}, []

    def grads_ready(l, name, grad):
        pending[(name, l)] = chunks(name, grad)
        flush = name == "ffn1_in" if l > 0 else name in ("ffn2_in", "mix_out", "mix_in", "ffn1_out", "ffn1_in")
        if not flush:
            return None
        keys = sorted(pending)
        handles = _push_start([pending[k] for k in keys], f"push_start_{len(started)}")
        pending.clear()
        started.append((keys, handles, l == 0 and name.startswith("ffn1")))
        return handles[-1][0, 0]

    loss, grad_x, dmod, big, small = _local_step(x, mod, loss_target, weights, p, grads_ready)
    del big

    recv, out = {}, {}

    def arrive(n, after):
        keys, (send_sems, recv_sems, srcs, lands, _), _ = started[n]
        srcs, lands = _push_wait(send_sems, recv_sems, srcs, lands, after, f"push_wait_{n}")
        for k, src, land in zip(keys, srcs, lands):
            recv[k] = (land, src)

    big_of = {"ffn1_w_in": "ffn1_in", "ffn1_w_out": "ffn1_out", "ffn2_w_in": "ffn2_in", "ffn2_w_out": "ffn2_out",
              "mix_w_in": "mix_in", "mix_w_out": "mix_out", "mla_w_uq": "mla_uq", "mla_w_ukv": "mla_ukv"}
    chain = {name: None for name in big_of}

    def big_update(key, l):
        name = next(nm for nm, k in big_of.items() if k == key)
        parts, src = recv[(key, l)]
        view =(lambda a: jnp.swapaxes(a, 1, 2)) if name in transposed else (lambda a: a)
        chain[name] = _adamw(parts, (src, me), view(w[name]), view(m[name]), view(v[name]), f"adamw_{name}_l{l}",
                             layer=l, prev=chain[name])

    def update(name, grad):
        shape = w[name].shape
        as3 = lambda a: a.reshape(1, -1, shape[-1])
        res = _adamw(as3(grad), None, as3(w[name]), as3(m[name]), as3(v[name]), f"adamw_{name}")
        out[name] = tuple(r.reshape(shape) for r in res)

    for n, (keys, _, last) in enumerate(started):
        if not last:
            arrive(n, grad_x)
            for key, l in keys:
                big_update(key, l)

    dmod_flat = dmod.reshape(DEPTH, bsz, N_MOD * d)
    done = [r[0] for r in chain.values() if r is not None]
    if done:
        dmod_flat, _ = lax.optimization_barrier((dmod_flat, done))
    dmod_all, = _all_gather([dmod_flat], "gather_dmod")
    dmod_full = jnp.moveaxis(dmod_all, 0, 1).reshape(DEPTH, N_DEV * bsz, N_MOD * d)
    cols = ada_w.shape[2]
    dmod_cols = lax.dynamic_slice_in_dim(dmod_full, me * cols, cols, axis=2)
    g_ada_w, g_ada_b = _ada_bwd(c_all, dmod_cols, dmod_full, "ada_bwd")
    res = None
    for l in range(DEPTH):
        res = _adamw(g_ada_w[l][None], None, ada_w, m_ada_w, v_ada_w, f"adamw_ada_w_l{l}", layer=l, prev=res)
    out["ada_w"] = tuple(res)
    update("ada_b", g_ada_b.reshape(DEPTH, N_MOD * d))

    items = _small_grad_list(small, loss)
    packed, _ = lax.optimization_barrier((_pack_small(items), (grad_x, g_ada_b)))
    parts, = _all_gather([packed], "gather_small")
    sg = _unpack_small(_sum_parts(parts, "sum_small"), items)
    for name in ("ln_g", "ln_b"):
        update(name, lax.dynamic_slice_in_dim(sg[name], me * (d // N_DEV), d // N_DEV, axis=2))
    for name in ("hgrn_lb_logits", "hgrn_norm_g", "mla_q_norm_g", "mla_kv_norm_g", "fox_b_f", "gmlp_ln_g",
                 "gmlp_ln_b", "gmlp_w_s", "gmlp_b_s"):
        update(name, sg[name])

    for n, (keys, _, last) in enumerate(started):
        if last:
            arrive(n, out["gmlp_w_s"][0])
            for key, l in keys:
                big_update(key, l)
    for name in big_of:
        out[name] = tuple(jnp.swapaxes(r, 1, 2) if name in transposed else r for r in chain[name])

    return (sg["loss"][0], grad_x, *[out[n][0] for n in names], *[out[n][1] for n in names],
            *[out[n][2] for n in names], *[out[n][3] for n in names])
```

```python
import functools

import numpy as np
import jax
import jax.numpy as jnp
from jax import lax
from jax.experimental import pallas as pl
from jax.experimental.pallas import tpu as pltpu

F32 = jnp.float32
BF16 = jnp.bfloat16
HI = lax.Precision.HIGHEST

D_MODEL = 1024
DEPTH = 2
GROUP_WIDTH = 256
N_HEADS = 4
HEAD_DIM = 64
A_CHUNK = 16
LB_FLOOR = 1e-30
B_NOPE = 64
B_ROPE = 32
ROPE_THETA = 10000.0
D_CHUNK = 128
D_FF = 2816
N_MOD = 9
ALPHA = (2 * DEPTH) ** 0.25
LN_EPS = 1e-5
RMS_EPS = 1e-6
ADAM_LR = 0.001
ADAM_B1 = 0.9
ADAM_B2 = 0.999
ADAM_EPS = 1e-08
ADAM_WD = 0.01
ADAM_STEP = 10

N_DEV = 8
LANES = 128
PACK_W = 3712
MO_W = 1536
VMEM_LIMIT = 56 * 1024 * 1024
NEG = -1e30
ATTN_TILE = 1024

MIX_ORIG_W = 2724
O_BCQ, O_BCKV, O_BKR, O_CQ, O_CK, O_CV, O_CF, O_DU, O_DV = 1024, 1280, 1408, 1440, 1696, 1952, 2208, 2212, 2468
P_B, P_KR, P_CQ, P_CKV, P_D, P_CF = 1024, 1408, 1536, 2048, 3072, 3584


_DN = {"nn": (((1,), (0,)), ((), ())), "nt": (((1,), (1,)), ((), ())), "tn": (((0,), (0,)), ((), ()))}


def _raw_bdot(a, b, mode):
    return lax.dot_general(a.astype(BF16), b.astype(BF16), _DN[mode], preferred_element_type=F32)


@functools.partial(jax.custom_vjp, nondiff_argnums=(2,))
def _bdot(a, b, mode):
    return _raw_bdot(a, b, mode)


def _bdot_fwd(a, b, mode):
    return _raw_bdot(a, b, mode), (a, b)


def _bdot_bwd(mode, res, g):
    a, b = res
    if mode == "nn":
        return _raw_bdot(g, b, "nt"), _raw_bdot(a, g, "tn")
    if mode == "nt":
        return _raw_bdot(g, b, "nn"), _raw_bdot(g, a, "tn")
    return _raw_bdot(b, g, "nt"), _raw_bdot(a, g, "nn")


_bdot.defvjp(_bdot_fwd, _bdot_bwd)


def _cparams(sem):
    return pltpu.CompilerParams(dimension_semantics=sem, vmem_limit_bytes=VMEM_LIMIT)


def _mix_in_src():
    src = -np.ones(PACK_W, np.int64)
    src[0:P_KR] = np.arange(0, O_BKR)
    src[P_KR + 64:P_KR + 80] = O_BKR + np.arange(16)
    src[P_KR + 96:P_KR + 112] = O_BKR + 16 + np.arange(16)
    for h in range(N_HEADS):
        src[P_CQ + 128 * h:P_CQ + 128 * h + 64] = O_CQ + 64 * h + np.arange(64)
        src[P_CKV + 256 * h:P_CKV + 256 * h + 64] = O_CK + 64 * h + np.arange(64)
        src[P_CKV + 256 * h + 128:P_CKV + 256 * h + 192] = O_CV + 64 * h + np.arange(64)
    src[P_D:P_D + 512] = O_DU + np.arange(512)
    src[P_CF:P_CF + 4] = O_CF + np.arange(4)
    return src


def _uq_src():
    src = -np.ones(512, np.int64)
    for h in range(N_HEADS):
        src[128 * h:128 * h + 64] = 96 * h + np.arange(64)
        src[128 * h + 64:128 * h + 80] = 96 * h + 64 + np.arange(16)
        src[128 * h + 96:128 * h + 112] = 96 * h + 80 + np.arange(16)
    return src


def _ukv_src():
    src = -np.ones(1024, np.int64)
    for h in range(N_HEADS):
        src[256 * h:256 * h + 64] = 128 * h + np.arange(64)
        src[256 * h + 128:256 * h + 192] = 128 * h + 64 + np.arange(64)
    return src


def _mo_src():
    src = -np.ones(MO_W, np.int64)
    src[0:256] = np.arange(256)
    for g in range(2):
        for h in range(N_HEADS):
            src[256 + 512 * g + 128 * h:256 + 512 * g + 128 * h + 64] = 256 + 256 * g + 64 * h + np.arange(64)
    src[1280:1536] = 768 + np.arange(256)
    return src


def _runs(idx):
    runs, i = [], 0
    while i < len(idx):
        j = i + 1
        while j < len(idx) and ((idx[i] < 0 and idx[j] < 0) or (idx[i] >= 0 and idx[j] == idx[i] + j - i)):
            j += 1
        runs.append((int(idx[i]), j - i))
        i = j
    return runs


def _take_runs(w, idx):
    parts = [jnp.zeros(w.shape[:-1] + (n,), w.dtype) if s < 0 else lax.slice_in_dim(w, s, s + n, axis=w.ndim - 1)
             for s, n in _runs(idx)]
    return jnp.concatenate(parts, axis=-1)


def _pack_cols(w, src):
    return _take_runs(w, src)


def _unpack_cols(wp, src, n):
    dst = np.zeros(n, np.int64)
    dst[src[src >= 0]] = np.nonzero(src >= 0)[0]
    return _take_runs(wp, dst)


def _rope_tables(seq):
    half = B_ROPE // 2
    inv_freq = ROPE_THETA ** (-jnp.arange(half, dtype=F32) / half)
    ang = jnp.arange(seq).astype(F32)[:, None] * inv_freq[None, :]
    cos, sin = jnp.cos(ang), jnp.sin(ang)
    z16 = jnp.zeros((seq, 16), F32)
    c = jnp.concatenate([jnp.ones((seq, 64), F32), cos, z16, cos, z16], axis=1)
    s1 = jnp.concatenate([jnp.zeros((seq, 64), F32), -sin, z16, z16, z16], axis=1)
    s2 = jnp.concatenate([jnp.zeros((seq, 64), F32), z16, z16, sin, z16], axis=1)
    return c, s1, s2


def _matmul(a, b, *, mode, group_out, out_dtype, tm, tk, name):
    ga, gb = a.shape[0], b.shape[0]
    g_n = max(ga, gb)
    if mode == "tn":
        k_dim, m_dim = a.shape[1:]
    else:
        m_dim, k_dim = a.shape[1:]
    n_dim = b.shape[1] if mode == "nt" else b.shape[2]
    assert m_dim % tm == 0 and k_dim % tk == 0
    kt = k_dim // tk
    n_red = kt if group_out else g_n * kt
    g_out = g_n if group_out else 1

    def split(g, r):
        return (g, r) if group_out else (r // kt, r % kt)

    def a_map(g, i, r):
        gg, kk = split(g, r)
        gg = gg if ga > 1 else 0
        return (gg, kk, i) if mode == "tn" else (gg, i, kk)

    def b_map(g, i, r):
        gg, kk = split(g, r)
        gg = gg if gb > 1 else 0
        return (gg, 0, kk) if mode == "nt" else (gg, kk, 0)

    a_blk = (None, tk, tm) if mode == "tn" else (None, tm, tk)
    b_blk = (None, n_dim, tk) if mode == "nt" else (None, tk, n_dim)
    dn = _DN[mode]

    def body(a_ref, b_ref, o_ref, *scratch):
        part = lax.dot_general(a_ref[...].astype(BF16), b_ref[...].astype(BF16), dn, preferred_element_type=F32)
        if n_red == 1:
            o_ref[...] = part.astype(o_ref.dtype)
            return
        acc_ref, = scratch
        r = pl.program_id(2)

        @pl.when(r == 0)
        def _():
            acc_ref[...] = part

        @pl.when(r > 0)
        def _():
            acc_ref[...] += part

        @pl.when(r == n_red - 1)
        def _():
            o_ref[...] = acc_ref[...].astype(o_ref.dtype)

    return pl.pallas_call(
        body, name=name, grid=(g_out, m_dim // tm, n_red),
        in_specs=[pl.BlockSpec(a_blk, a_map), pl.BlockSpec(b_blk, b_map)],
        out_specs=pl.BlockSpec((None, tm, n_dim), lambda g, i, r: (g, i, 0)),
        out_shape=jax.ShapeDtypeStruct((g_out, m_dim, n_dim), out_dtype),
        scratch_shapes=[] if n_red == 1 else [pltpu.VMEM((tm, n_dim), F32)],
        compiler_params=_cparams(("parallel", "parallel", "arbitrary")),
    )(a, b)


def _matmul_groupsum(a, b, *, out_dtype, tm, name):
    g_n, m_dim, k_dim = a.shape
    n_dim = b.shape[2]
    assert m_dim % tm == 0 and b.shape[:2] == (g_n, k_dim)

    def body(a_ref, b_ref, o_ref):
        acc = jnp.dot(a_ref[0], b_ref[0], preferred_element_type=F32)
        for g in range(1, g_n):
            acc = acc + jnp.dot(a_ref[g], b_ref[g], preferred_element_type=F32)
        o_ref[...] = acc.astype(o_ref.dtype)

    return pl.pallas_call(
        body, name=name, grid=(m_dim // tm,),
        in_specs=[pl.BlockSpec((g_n, tm, k_dim), lambda i: (0, i, 0)),
                  pl.BlockSpec((g_n, k_dim, n_dim), lambda i: (0, 0, 0))],
        out_specs=pl.BlockSpec((tm, n_dim), lambda i: (i, 0)),
        out_shape=jax.ShapeDtypeStruct((m_dim, n_dim), out_dtype),
        compiler_params=_cparams(("parallel",)),
    )(a, b)


def _row_spec(ts, d):
    return pl.BlockSpec((None, ts, d), lambda b, s: (b, s, 0))


def _mod_spec(d):
    return pl.BlockSpec((None, N_MOD, d), lambda b, s: (b, 0, 0))


def _vec_spec(d):
    return pl.BlockSpec((1, d), lambda b, s: (0, 0))


def _bvec_spec(d):
    return pl.BlockSpec((None, 1, d), lambda b, s: (b, 0, 0))


def _modulate(x, mod, sh_row, sc_row, name, ts=512):
    bsz, seq, d = x.shape

    def body(x_ref, mod_ref, o_ref):
        sh = mod_ref[sh_row:sh_row + 1, :]
        sc = mod_ref[sc_row:sc_row + 1, :]
        o_ref[...] = (x_ref[...] * (1.0 + sc) + sh).astype(o_ref.dtype)

    return pl.pallas_call(
        body, name=name, grid=(bsz, seq // ts),
        in_specs=[_row_spec(ts, d), _mod_spec(d)], out_specs=_row_spec(ts, d),
        out_shape=jax.ShapeDtypeStruct((bsz, seq, d), BF16),
        compiler_params=_cparams(("parallel", "parallel")),
    )(x, mod)


def _modulate_bwd(dh, x, mod, dx_res, sc_row, name, ts=512):
    bsz, seq, d = x.shape

    def body(dh_ref, x_ref, mod_ref, dxr_ref, dx_ref, dsh_ref, dsc_ref):
        s = pl.program_id(1)
        sc = mod_ref[sc_row:sc_row + 1, :]
        dh_v = dh_ref[...]
        dx_ref[...] = dxr_ref[...] + dh_v * (1.0 + sc)
        psh = jnp.sum(dh_v, axis=0, keepdims=True)
        psc = jnp.sum(dh_v * x_ref[...], axis=0, keepdims=True)

        @pl.when(s == 0)
        def _():
            dsh_ref[...] = psh
            dsc_ref[...] = psc

        @pl.when(s > 0)
        def _():
            dsh_ref[...] += psh
            dsc_ref[...] += psc

    return pl.pallas_call(
        body, name=name, grid=(bsz, seq // ts),
        in_specs=[_row_spec(ts, d), _row_spec(ts, d), _mod_spec(d), _row_spec(ts, d)],
        out_specs=[_row_spec(ts, d), _bvec_spec(d), _bvec_spec(d)],
        out_shape=[jax.ShapeDtypeStruct((bsz, seq, d), F32), jax.ShapeDtypeStruct((bsz, 1, d), F32),
                   jax.ShapeDtypeStruct((bsz, 1, d), F32)],
        compiler_params=_cparams(("parallel", "arbitrary")),
    )(dh, x, mod, dx_res)


def _res_ln_fn(x, f, g, lng, lnb, cmul):
    r = ALPHA * x + (cmul * (1.0 + g)) * f
    mu = jnp.mean(r, axis=-1, keepdims=True)
    rc = r - mu
    var = jnp.mean(rc * rc, axis=-1, keepdims=True)
    return rc * lax.rsqrt(var + LN_EPS) * lng + lnb


def _res_ln(x, f, mod, lng, lnb, g_row, cmul, name, nxt=None, ts=512):
    bsz, seq, d = x.shape

    def body(*refs):
        x_ref, f_ref, mod_ref, lng_ref, lnb_ref = refs[:5]
        g = mod_ref[g_row:g_row + 1, :]
        y = _res_ln_fn(x_ref[...], f_ref[...], g, lng_ref[...], lnb_ref[...], cmul)
        if nxt is None:
            refs[5][...] = y
            return
        nmod_ref, o_ref, h_ref = refs[5:]
        o_ref[...] = y
        sh = nmod_ref[nxt[1]:nxt[1] + 1, :]
        sc = nmod_ref[nxt[2]:nxt[2] + 1, :]
        h_ref[...] = (y * (1.0 + sc) + sh).astype(h_ref.dtype)

    in_specs = [_row_spec(ts, d), _row_spec(ts, d), _mod_spec(d), _vec_spec(d), _vec_spec(d)]
    args = [x, f, mod, lng, lnb]
    out_specs, out_shape = [_row_spec(ts, d)], [jax.ShapeDtypeStruct((bsz, seq, d), F32)]
    if nxt is not None:
        in_specs.append(_mod_spec(d))
        args.append(nxt[0])
        out_specs.append(_row_spec(ts, d))
        out_shape.append(jax.ShapeDtypeStruct((bsz, seq, d), BF16))
    res = pl.pallas_call(
        body, name=name, grid=(bsz, seq // ts), in_specs=in_specs, out_specs=out_specs, out_shape=out_shape,
        compiler_params=_cparams(("parallel", "parallel")),
    )(*args)
    return (res[0], res[1]) if nxt is not None else (res[0], None)


def _res_ln_bwd(dy, x, f, mod, lng, lnb, g_row, cmul, name, pre=None, ts=256):
    bsz, seq, d = x.shape
    fused = pre is not None

    def body(*refs):
        dy_ref, x_ref, f_ref, mod_ref, lng_ref, lnb_ref = refs[:6]
        n_in = 9 if fused else 6
        dx_ref, df_ref, dg_ref, dlg_ref, dlb_ref = refs[n_in:n_in + 5]
        b, s = pl.program_id(0), pl.program_id(1)
        g = mod_ref[g_row:g_row + 1, :]
        _, vjp = jax.vjp(functools.partial(_res_ln_fn, cmul=cmul), x_ref[...], f_ref[...], g, lng_ref[...],
                         lnb_ref[...])
        ct = dy_ref[...]
        if fused:
            dh_ref, y_ref, nmod_ref = refs[6:9]
            dsh_ref, dsc_ref = refs[n_in + 5:]
            dh_v = dh_ref[...]
            ct = ct + dh_v * (1.0 + nmod_ref[pre[3]:pre[3] + 1, :])
            psh = jnp.sum(dh_v, axis=0, keepdims=True)
            psc = jnp.sum(dh_v * y_ref[...], axis=0, keepdims=True)
        dx, df, dg, dlg, dlb = vjp(ct)
        dx_ref[...] = dx
        df_ref[...] = df.astype(df_ref.dtype)

        @pl.when(s == 0)
        def _():
            dg_ref[...] = dg
            if fused:
                dsh_ref[...] = psh
                dsc_ref[...] = psc

        @pl.when(s > 0)
        def _():
            dg_ref[...] += dg
            if fused:
                dsh_ref[...] += psh
                dsc_ref[...] += psc

        first = jnp.logical_and(b == 0, s == 0)

        @pl.when(first)
        def _():
            dlg_ref[...] = dlg
            dlb_ref[...] = dlb

        @pl.when(jnp.logical_not(first))
        def _():
            dlg_ref[...] += dlg
            dlb_ref[...] += dlb

    in_specs = [_row_spec(ts, d), _row_spec(ts, d), _row_spec(ts, d), _mod_spec(d), _vec_spec(d), _vec_spec(d)]
    args = [dy, x, f, mod, lng, lnb]
    out_specs = [_row_spec(ts, d), _row_spec(ts, d), _bvec_spec(d), _vec_spec(d), _vec_spec(d)]
    bvec = jax.ShapeDtypeStruct((bsz, 1, d), F32)
    out_shape = [jax.ShapeDtypeStruct((bsz, seq, d), F32), jax.ShapeDtypeStruct((bsz, seq, d), BF16), bvec,
                 jax.ShapeDtypeStruct((1, d), F32), jax.ShapeDtypeStruct((1, d), F32)]
    if fused:
        in_specs += [_row_spec(ts, d), _row_spec(ts, d), _mod_spec(d)]
        args += list(pre[:3])
        out_specs += [_bvec_spec(d), _bvec_spec(d)]
        out_shape += [bvec, bvec]
    res = pl.pallas_call(
        body, name=name, grid=(bsz, seq // ts), in_specs=in_specs, out_specs=out_specs, out_shape=out_shape,
        compiler_params=_cparams(("arbitrary", "arbitrary")),
    )(*args)
    return tuple(res[:5]), (tuple(res[5:]) if fused else None)


def _loss_head(y, target, name, ts=512):
    bsz, seq, d = y.shape
    n_s = seq // ts

    def body(y_ref, t_ref, dy_ref, loss_ref, acc_ref):
        b, s = pl.program_id(0), pl.program_id(1)
        err = y_ref[...] - t_ref[...]
        dy_ref[...] = err * (1.0 / d)
        part = jnp.sum(err * err, axis=0, keepdims=True)
        first = jnp.logical_and(b == 0, s == 0)

        @pl.when(first)
        def _():
            acc_ref[...] = part

        @pl.when(jnp.logical_not(first))
        def _():
            acc_ref[...] += part

        @pl.when(jnp.logical_and(b == bsz - 1, s == n_s - 1))
        def _():
            loss_ref[...] = jnp.sum(acc_ref[...], axis=1, keepdims=True) * (0.5 / d)

    return pl.pallas_call(
        body, name=name, grid=(bsz, n_s),
        in_specs=[_row_spec(ts, d), _row_spec(ts, d)],
        out_specs=[_row_spec(ts, d), pl.BlockSpec((1, 1), lambda b, s: (0, 0))],
        out_shape=[jax.ShapeDtypeStruct((bsz, seq, d), F32), jax.ShapeDtypeStruct((1, 1), F32)],
        scratch_shapes=[pltpu.VMEM((1, d), F32)],
        compiler_params=_cparams(("arbitrary", "arbitrary")),
    )(y, target)


def _ffn_in_swiglu(h, w_in_t, name, tm=1024):
    t, d = h.shape
    n_sh, w, _ = w_in_t.shape
    half = n_sh // 2

    def body(h_ref, w_ref, z_ref, a_ref):
        hv = h_ref[...]
        g = lax.dot_general(hv, w_ref[0], _DN["nt"], preferred_element_type=F32)
        u = lax.dot_general(hv, w_ref[1], _DN["nt"], preferred_element_type=F32)
        z_ref[0] = g.astype(z_ref.dtype)
        z_ref[1] = u.astype(z_ref.dtype)
        a_ref[...] = (g * jax.nn.sigmoid(g) * u).astype(a_ref.dtype)

    return pl.pallas_call(
        body, name=name, grid=(half, t // tm),
        in_specs=[pl.BlockSpec((tm, d), lambda g, i: (i, 0)),
                  pl.BlockSpec((2, None, w, d), lambda g, i: (0, g, 0, 0))],
        out_specs=[pl.BlockSpec((2, None, tm, w), lambda g, i: (0, g, i, 0)),
                   pl.BlockSpec((None, tm, w), lambda g, i: (g, i, 0))],
        out_shape=[jax.ShapeDtypeStruct((2, half, t, w), BF16), jax.ShapeDtypeStruct((half, t, w), BF16)],
        compiler_params=_cparams(("parallel", "parallel")),
    )(h, w_in_t.reshape(2, half, w, d))


def _ffn_out_dx_swiglu(df, w_out, z, name, tm=1024):
    t, d = df.shape
    half, w, _ = w_out.shape

    def body(df_ref, w_ref, z_ref, dz_ref):
        da = lax.dot_general(df_ref[...], w_ref[...], _DN["nt"], preferred_element_type=F32)
        g = z_ref[0].astype(F32)
        u = z_ref[1].astype(F32)
        sig = jax.nn.sigmoid(g)
        dz_ref[0] = (da * u * (sig * (1.0 + g * (1.0 - sig)))).astype(dz_ref.dtype)
        dz_ref[1] = (da * (g * sig)).astype(dz_ref.dtype)

    zspec = pl.BlockSpec((2, None, tm, w), lambda g, i: (0, g, i, 0))
    return pl.pallas_call(
        body, name=name, grid=(half, t // tm),
        in_specs=[pl.BlockSpec((tm, d), lambda g, i: (i, 0)), pl.BlockSpec((None, w, d), lambda g, i: (g, 0, 0)),
                  zspec],
        out_specs=zspec, out_shape=jax.ShapeDtypeStruct(z.shape, BF16),
        compiler_params=_cparams(("parallel", "parallel")),
    )(df, w_out, z)


def _log_sigmoid(x):
    return jnp.minimum(x, 0.0) - jnp.log(1.0 + jnp.exp(-jnp.abs(x)))


def _hgrn_consts():
    r = lax.broadcasted_iota(jnp.int32, (GROUP_WIDTH, GROUP_WIDTH), 0)
    c = lax.broadcasted_iota(jnp.int32, (GROUP_WIDTH, GROUP_WIDTH), 1)
    bd = (r // HEAD_DIM == c // HEAD_DIM).astype(F32)
    r16 = lax.broadcasted_iota(jnp.int32, (A_CHUNK, A_CHUNK), 0)
    c16 = lax.broadcasted_iota(jnp.int32, (A_CHUNK, A_CHUNK), 1)
    tril = (r16 >= c16).astype(F32)
    rows = lax.broadcasted_iota(jnp.int32, (A_CHUNK, GROUP_WIDTH), 0)
    return bd, tril, rows


def _hgrn_lb(logits8, layer):
    rows = lax.broadcasted_iota(jnp.int32, logits8.shape, 0)
    valid = rows < DEPTH
    mx = jnp.max(jnp.where(valid, logits8, NEG), axis=0, keepdims=True)
    e = jnp.where(valid, jnp.exp(logits8 - mx), 0.0)
    sm = e / jnp.sum(e, axis=0, keepdims=True)
    pick = jnp.logical_and(rows >= 1, rows <= layer)
    return jnp.sum(jnp.where(pick, sm, 0.0), axis=0, keepdims=True)


def _hgrn_chunk(aq, af, ai, ag, logits8, norm_g, st, *, layer, consts):
    bd, tril, rows = consts
    lb = _hgrn_lb(logits8, layer)
    la = jnp.log(jnp.maximum(lb, LB_FLOOR))
    b2 = jnp.log(1.0 - lb) + _log_sigmoid(af)
    log_f = jnp.maximum(la, b2) + jnp.log(1.0 + jnp.exp(-jnp.abs(la - b2)))
    k = 1.0 - jnp.exp(log_f)
    qf = aq * jax.nn.sigmoid(aq)
    g_cum = jnp.dot(tril, log_f, precision=HI, preferred_element_type=F32)

    c, w = A_CHUNK, GROUP_WIDTH

    def by_key(v):
        return jnp.broadcast_to(v[:, None, :], (c, c, w))

    def by_query(v):
        return jnp.broadcast_to(v[None, :, :], (c, c, w))

    s_i = lax.broadcasted_iota(jnp.int32, (c, c, w), 0)
    t_i = lax.broadcasted_iota(jnp.int32, (c, c, w), 1)
    rel = jnp.where(t_i >= s_i, by_query(g_cum) - by_key(g_cum), NEG)
    pairs = by_query(qf) * by_key(k) * jnp.exp(rel)
    a_all = _bdot(pairs.reshape(c * c, w), bd, "nn").reshape(c, c, w)
    o = jnp.sum(a_all * by_key(ai), axis=0)
    q_dec = qf * jnp.exp(g_cum)
    o = o + _bdot(q_dec, st, "nt")
    g_last = jnp.sum(jnp.where(rows == c - 1, g_cum, 0.0), axis=0, keepdims=True)
    k_end = k * jnp.exp(g_last - g_cum)
    kv = _bdot(ai, k_end, "tn")
    st_new = st * jnp.exp(g_last) + kv * bd
    ms = _bdot(o * o, bd, "nn") * (1.0 / HEAD_DIM)
    o = o * lax.rsqrt(ms + RMS_EPS) * norm_g
    return o * (ag * jax.nn.sigmoid(ag)), st_new


def _hgrn_fwd(proj, logits8, norm_g, layer, name, ts=256):
    bsz, seq, _ = proj.shape
    n_ch = ts // A_CHUNK

    def body(p_ref, lg_ref, ng_ref, o_ref, st_ref, st_scr):
        @pl.when(pl.program_id(1) == 0)
        def _():
            st_scr[...] = jnp.zeros_like(st_scr)

        consts = _hgrn_consts()
        logits_v, ng_v = lg_ref[...], ng_ref[...]

        def chunk(ci, carry):
            r = ci * A_CHUNK if isinstance(ci, int) else pl.multiple_of(ci * A_CHUNK, A_CHUNK)
            st = st_scr[...]
            st_ref[ci] = st
            o, st_new = _hgrn_chunk(
                p_ref[pl.ds(r, A_CHUNK), 0:256], p_ref[pl.ds(r, A_CHUNK), 256:512],
                p_ref[pl.ds(r, A_CHUNK), 512:768], p_ref[pl.ds(r, A_CHUNK), 768:1024],
                logits_v, ng_v, st, layer=layer, consts=consts)
            o_ref[pl.ds(r, A_CHUNK), :] = o.astype(o_ref.dtype)
            st_scr[...] = st_new
            return carry

        if n_ch <= 2:
            for c_static in range(n_ch):
                chunk(c_static, 0)
        else:
            lax.fori_loop(0, n_ch, chunk, 0, unroll=2)

    return pl.pallas_call(
        body, name=name, grid=(bsz, seq // ts),
        in_specs=[pl.BlockSpec((None, ts, 1024), lambda b, s: (b, s, 0)),
                  pl.BlockSpec((8, GROUP_WIDTH), lambda b, s: (0, 0)),
                  pl.BlockSpec((1, GROUP_WIDTH), lambda b, s: (0, 0))],
        out_specs=[pl.BlockSpec((None, ts, GROUP_WIDTH), lambda b, s: (b, s, 0)),
                   pl.BlockSpec((None, n_ch, GROUP_WIDTH, GROUP_WIDTH), lambda b, s: (b, s, 0, 0))],
        out_shape=[jax.ShapeDtypeStruct((bsz, seq, MO_W), BF16),
                   jax.ShapeDtypeStruct((bsz, seq // A_CHUNK, GROUP_WIDTH, GROUP_WIDTH), F32)],
        scratch_shapes=[pltpu.VMEM((GROUP_WIDTH, GROUP_WIDTH), F32)],
        compiler_params=_cparams(("parallel", "arbitrary")),
    )(proj, logits8, norm_g)


def _hgrn_bwd(dmo, proj, states, logits8, norm_g, layer, name, ts=256):
    bsz, seq, _ = proj.shape
    n_ch = ts // A_CHUNK
    n_s = seq // ts

    def body(do_ref, p_ref, st_ref, lg_ref, ng_ref, dp_ref, dlg_ref, dng_ref, dst_scr):
        b, s = pl.program_id(0), pl.program_id(1)

        @pl.when(s == 0)
        def _():
            dst_scr[...] = jnp.zeros_like(dst_scr)

        @pl.when(jnp.logical_and(b == 0, s == 0))
        def _():
            dlg_ref[...] = jnp.zeros_like(dlg_ref)
            dng_ref[...] = jnp.zeros_like(dng_ref)

        consts = _hgrn_consts()
        logits_v, ng_v = lg_ref[...], ng_ref[...]
        fn = functools.partial(_hgrn_chunk, layer=layer, consts=consts)

        def chunk(t, carry):
            ci = n_ch - 1 - t
            r = ci * A_CHUNK if isinstance(ci, int) else pl.multiple_of(ci * A_CHUNK, A_CHUNK)
            _, vjp = jax.vjp(
                fn, p_ref[pl.ds(r, A_CHUNK), 0:256], p_ref[pl.ds(r, A_CHUNK), 256:512],
                p_ref[pl.ds(r, A_CHUNK), 512:768], p_ref[pl.ds(r, A_CHUNK), 768:1024],
                logits_v, ng_v, st_ref[ci])
            daq, daf, dai, dag, dlg, dng, dst = vjp((do_ref[pl.ds(r, A_CHUNK), :], dst_scr[...]))
            dp_ref[pl.ds(r, A_CHUNK), 0:256] = daq.astype(dp_ref.dtype)
            dp_ref[pl.ds(r, A_CHUNK), 256:512] = daf.astype(dp_ref.dtype)
            dp_ref[pl.ds(r, A_CHUNK), 512:768] = dai.astype(dp_ref.dtype)
            dp_ref[pl.ds(r, A_CHUNK), 768:1024] = dag.astype(dp_ref.dtype)
            dlg_ref[...] += dlg
            dng_ref[...] += dng
            dst_scr[...] = dst
            return carry

        if n_ch <= 2:
            for c_static in range(n_ch):
                chunk(c_static, 0)
        else:
            lax.fori_loop(0, n_ch, chunk, 0, unroll=2)

    rev = lambda b, s: (b, n_s - 1 - s, 0)
    return pl.pallas_call(
        body, name=name, grid=(bsz, n_s),
        in_specs=[pl.BlockSpec((None, ts, GROUP_WIDTH), rev),
                  pl.BlockSpec((None, ts, 1024), rev),
                  pl.BlockSpec((None, n_ch, GROUP_WIDTH, GROUP_WIDTH), lambda b, s: (b, n_s - 1 - s, 0, 0)),
                  pl.BlockSpec((8, GROUP_WIDTH), lambda b, s: (0, 0)),
                  pl.BlockSpec((1, GROUP_WIDTH), lambda b, s: (0, 0))],
        out_specs=[pl.BlockSpec((None, ts, 1024), rev),
                   pl.BlockSpec((8, GROUP_WIDTH), lambda b, s: (0, 0)),
                   pl.BlockSpec((1, GROUP_WIDTH), lambda b, s: (0, 0))],
        out_shape=[jax.ShapeDtypeStruct((bsz, seq, PACK_W), BF16),
                   jax.ShapeDtypeStruct((8, GROUP_WIDTH), F32), jax.ShapeDtypeStruct((1, GROUP_WIDTH), F32)],
        scratch_shapes=[pltpu.VMEM((GROUP_WIDTH, GROUP_WIDTH), F32)],
        compiler_params=_cparams(("arbitrary", "arbitrary")),
    )(dmo, proj, states, logits8, norm_g)


def _rms_fn(x, g):
    return x * lax.rsqrt(jnp.mean(x * x, axis=-1, keepdims=True) + RMS_EPS) * g


def _tile4(t):
    return jnp.concatenate([t, t, t, t], axis=1)


def _rope(x, c, s1, s2):
    w = x.shape[-1]
    return x * c + pltpu.roll(x, 32, axis=1) * s2 + pltpu.roll(x, w - 32, axis=1) * s1


def _rope_t(dy, c, s1, s2):
    w = dy.shape[-1]
    return dy * c + pltpu.roll(dy * s2, w - 32, axis=1) + pltpu.roll(dy * s1, 32, axis=1)


def _mla_pre(proj, qg, kvg, wq, wkv, tabs, name, ts=256):
    bsz, seq, _ = proj.shape

    def body(p_ref, qg_ref, kvg_ref, wq_ref, wkv_ref, c_ref, s1_ref, s2_ref, q_ref, kv_ref):
        nq = _rms_fn(p_ref[:, 0:256], qg_ref[...])
        nkv = _rms_fn(p_ref[:, 256:384], kvg_ref[...])
        c, s1, s2 = c_ref[...], s1_ref[...], s2_ref[...]
        qp = jnp.dot(nq.astype(BF16), wq_ref[...], preferred_element_type=F32)
        q_ref[...] = _rope(qp, _tile4(c), _tile4(s1), _tile4(s2)).astype(q_ref.dtype)
        kv = jnp.dot(nkv.astype(BF16), wkv_ref[...], preferred_element_type=F32)
        krr = _rope(p_ref[:, 384:512], c, s1, s2)
        zero = jnp.zeros_like(krr)
        kv_ref[...] = (kv + jnp.concatenate([krr, zero] * N_HEADS, axis=1)).astype(kv_ref.dtype)

    tab_spec = pl.BlockSpec((ts, LANES), lambda b, s: (s, 0))
    return pl.pallas_call(
        body, name=name, grid=(bsz, seq // ts),
        in_specs=[pl.BlockSpec((None, ts, 512), lambda b, s: (b, s, P_B // 512)),
                  _vec_spec(256), _vec_spec(128),
                  pl.BlockSpec((256, 512), lambda b, s: (0, 0)), pl.BlockSpec((128, 1024), lambda b, s: (0, 0)),
                  tab_spec, tab_spec, tab_spec],
        out_specs=[_row_spec(ts, 512), _row_spec(ts, 1024)],
        out_shape=[jax.ShapeDtypeStruct((bsz, seq, 512), BF16), jax.ShapeDtypeStruct((bsz, seq, 1024), BF16)],
        compiler_params=_cparams(("parallel", "parallel")),
    )(proj, qg, kvg, wq, wkv, *tabs)


def _mla_pre_bwd(dq, dkv, dproj, proj, qg, kvg, wq, wkv, tabs, name, ts=256):
    bsz, seq, _ = proj.shape

    def body(dq_ref, dkv_ref, dp_any, p_ref, qg_ref, kvg_ref, wq_ref, wkv_ref, c_ref, s1_ref, s2_ref,
             dp_ref, dqg_ref, dkvg_ref, dwq_ref, dwkv_ref):
        del dp_any
        first = jnp.logical_and(pl.program_id(0) == 0, pl.program_id(1) == 0)

        @pl.when(first)
        def _():
            dqg_ref[...] = jnp.zeros_like(dqg_ref)
            dkvg_ref[...] = jnp.zeros_like(dkvg_ref)
            dwq_ref[...] = jnp.zeros_like(dwq_ref)
            dwkv_ref[...] = jnp.zeros_like(dwkv_ref)

        c, s1, s2 = c_ref[...], s1_ref[...], s2_ref[...]
        nq, vjp_q = jax.vjp(_rms_fn, p_ref[:, 0:256], qg_ref[...])
        nkv, vjp_kv = jax.vjp(_rms_fn, p_ref[:, 256:384], kvg_ref[...])
        dqp = _rope_t(dq_ref[...], _tile4(c), _tile4(s1), _tile4(s2)).astype(BF16)
        dkv_v = dkv_ref[...]
        dkv_b = dkv_v.astype(BF16)
        tn = (((0,), (0,)), ((), ()))
        nt = (((1,), (1,)), ((), ()))
        dwq_ref[...] += lax.dot_general(nq.astype(BF16), dqp, tn, preferred_element_type=F32)
        dwkv_ref[...] += lax.dot_general(nkv.astype(BF16), dkv_b, tn, preferred_element_type=F32)
        dcq, dqg = vjp_q(lax.dot_general(dqp, wq_ref[...], nt, preferred_element_type=F32))
        dckv, dkvg = vjp_kv(lax.dot_general(dkv_b, wkv_ref[...], nt, preferred_element_type=F32))
        dqg_ref[...] += dqg
        dkvg_ref[...] += dkvg
        dk_sum = dkv_v[:, 0:128] + dkv_v[:, 256:384] + dkv_v[:, 512:640] + dkv_v[:, 768:896]
        lane = lax.broadcasted_iota(jnp.int32, dk_sum.shape, 1)
        dkr = jnp.where(lane >= 64, _rope_t(dk_sum, c, s1, s2), 0.0)
        dp_ref[:, 0:256] = dcq.astype(dp_ref.dtype)
        dp_ref[:, 256:384] = dckv.astype(dp_ref.dtype)
        dp_ref[:, 384:512] = dkr.astype(dp_ref.dtype)

    tab_spec = pl.BlockSpec((ts, LANES), lambda b, s: (s, 0))
    const = lambda shape: pl.BlockSpec(shape, lambda b, s: (0, 0))
    return pl.pallas_call(
        body, name=name, grid=(bsz, seq // ts),
        in_specs=[_row_spec(ts, 512), _row_spec(ts, 1024), pl.BlockSpec(memory_space=pl.ANY),
                  pl.BlockSpec((None, ts, 512), lambda b, s: (b, s, P_B // 512)),
                  _vec_spec(256), _vec_spec(128), const((256, 512)), const((128, 1024)),
                  tab_spec, tab_spec, tab_spec],
        out_specs=[pl.BlockSpec((None, ts, 512), lambda b, s: (b, s, P_B // 512)),
                   _vec_spec(256), _vec_spec(128), const((256, 512)), const((128, 1024))],
        out_shape=[jax.ShapeDtypeStruct(dproj.shape, dproj.dtype), jax.ShapeDtypeStruct((1, 256), F32),
                   jax.ShapeDtypeStruct((1, 128), F32), jax.ShapeDtypeStruct((256, 512), F32),
                   jax.ShapeDtypeStruct((128, 1024), F32)],
        input_output_aliases={2: 0},
        compiler_params=_cparams(("arbitrary", "arbitrary")),
    )(dq, dkv, dproj, proj, qg, kvg, wq, wkv, *tabs)


def _fox_gate(proj, bf, name):
    bsz, seq, _ = proj.shape
    n_blk = seq // LANES

    def body(x_ref, bf_ref, f_ref):
        r_i = lax.broadcasted_iota(jnp.int32, (LANES, LANES), 0)
        c_i = lax.broadcasted_iota(jnp.int32, (LANES, LANES), 1)
        tril = (r_i >= c_i).astype(F32)
        bias = bf_ref[...]

        def blk(i, carry):
            r = pl.multiple_of(i * LANES, LANES)
            lf = _log_sigmoid(x_ref[pl.ds(r, LANES), :] + bias)
            f_ref[pl.ds(r, LANES), :] = jnp.dot(tril, lf, precision=HI, preferred_element_type=F32) + carry
            return carry + jnp.sum(lf, axis=0, keepdims=True)

        lax.fori_loop(0, n_blk, blk, jnp.zeros((1, LANES), F32))

    return pl.pallas_call(
        body, name=name, grid=(bsz,),
        in_specs=[pl.BlockSpec((None, seq, LANES), lambda b: (b, 0, P_CF // LANES)),
                  pl.BlockSpec((1, LANES), lambda b: (0, 0))],
        out_specs=pl.BlockSpec((None, seq, LANES), lambda b: (b, 0, 0)),
        out_shape=jax.ShapeDtypeStruct((bsz, seq, LANES), F32),
        compiler_params=_cparams(("parallel",)),
    )(proj, bf)


def _fox_gate_bwd(dfq, dfk_cols, dproj, proj, bf, name):
    bsz, seq, _ = proj.shape
    n_blk = seq // LANES

    def body(dfq_ref, dfk_ref, dp_any, x_ref, bf_ref, dp_ref, dbf_ref):
        del dp_any

        @pl.when(pl.program_id(0) == 0)
        def _():
            dbf_ref[...] = jnp.zeros_like(dbf_ref)

        r_i = lax.broadcasted_iota(jnp.int32, (LANES, LANES), 0)
        c_i = lax.broadcasted_iota(jnp.int32, (LANES, LANES), 1)
        triu = (r_i <= c_i).astype(F32)
        bias = bf_ref[...]

        def blk(t, carry):
            tail, dbf = carry
            r = pl.multiple_of((n_blk - 1 - t) * LANES, LANES)
            dc = dfk_ref[pl.ds(r, LANES), :]
            for hd in range(N_HEADS):
                dc = dc + jnp.where(c_i == hd, dfq_ref[hd, pl.ds(r, LANES), :], 0.0)
            dlf = jnp.dot(triu, dc, precision=HI, preferred_element_type=F32) + tail
            dx = dlf * (1.0 - jax.nn.sigmoid(x_ref[pl.ds(r, LANES), :] + bias))
            dp_ref[pl.ds(r, LANES), :] = dx.astype(dp_ref.dtype)
            return tail + jnp.sum(dc, axis=0, keepdims=True), dbf + jnp.sum(dx, axis=0, keepdims=True)

        z = jnp.zeros((1, LANES), F32)
        _, dbf = lax.fori_loop(0, n_blk, blk, (z, z))
        dbf_ref[...] += dbf

    return pl.pallas_call(
        body, name=name, grid=(bsz,),
        in_specs=[pl.BlockSpec((None, N_HEADS, seq, LANES), lambda b: (b, 0, 0, 0)),
                  pl.BlockSpec((None, seq, LANES), lambda b: (b, 0, 0)), pl.BlockSpec(memory_space=pl.ANY),
                  pl.BlockSpec((None, seq, LANES), lambda b: (b, 0, P_CF // LANES)),
                  pl.BlockSpec((1, LANES), lambda b: (0, 0))],
        out_specs=[pl.BlockSpec((None, seq, LANES), lambda b: (b, 0, P_CF // LANES)),
                   pl.BlockSpec((1, LANES), lambda b: (0, 0))],
        out_shape=[jax.ShapeDtypeStruct(dproj.shape, dproj.dtype), jax.ShapeDtypeStruct((1, LANES), F32)],
        input_output_aliases={2: 0},
        compiler_params=_cparams(("arbitrary",)),
    )(dfq, dfk_cols, dproj, proj, bf)


def _gate_terms(fc_ref, fr_ref, h, tq, tk):
    lane = lax.broadcasted_iota(jnp.int32, (tq, LANES), 1)
    fcol = jnp.sum(jnp.where(lane == h, fc_ref[...], 0.0), axis=1, keepdims=True)
    sub = lax.broadcasted_iota(jnp.int32, (8, tk), 0)
    frow = jnp.sum(jnp.where(sub == h, fr_ref[...], 0.0), axis=0, keepdims=True)
    return fcol - frow


def _scores(q_ref, k_ref, gate_refs, scale, h, masked, tq, tk):
    q = (q_ref[...].astype(F32) * scale).astype(BF16)
    s = lax.dot_general(q, k_ref[...].astype(BF16), _DN["nt"], preferred_element_type=F32)
    if gate_refs is not None:
        s = s + _gate_terms(gate_refs[0], gate_refs[1], h, tq, tk)
    if masked is not False:
        r_i = lax.broadcasted_iota(jnp.int32, (tq, tk), 0)
        c_i = lax.broadcasted_iota(jnp.int32, (tq, tk), 1)
        keep = c_i <= r_i
        s = jnp.where(keep if masked is True else jnp.logical_or(jnp.logical_not(masked), keep), s, NEG)
    return s, q


def _lanes(col):
    return jnp.broadcast_to(col, (col.shape[0], LANES))


def _attn_fwd(qa, q0, kva, kv0, mo, o0, gates, scale, name, tq=None):
    bsz, seq, _ = qa.shape
    tq = ATTN_TILE if tq is None else tq
    n_q = seq // tq
    gated = gates is not None

    def body(*refs):
        q_ref, k_ref, v_ref = refs[:3]
        gate_refs = refs[3:5] if gated else None
        o_ref, lse_ref, m_s, l_s, acc_s = refs[-5:]
        h, i, j = pl.program_id(1), pl.program_id(2), pl.program_id(3)

        @pl.when(j == 0)
        def _():
            m_s[...] = jnp.full_like(m_s, NEG)
            l_s[...] = jnp.zeros_like(l_s)
            acc_s[...] = jnp.zeros_like(acc_s)

        def step(masked):
            s, _ = _scores(q_ref, k_ref, gate_refs, scale, h, masked, tq, tq)
            m_prev = m_s[...]
            m_new = jnp.maximum(m_prev, jnp.max(s, axis=1, keepdims=True))
            alpha = jnp.exp(m_prev - m_new)
            p = jnp.exp(s - m_new)
            l_s[...] = alpha * l_s[...] + jnp.sum(p, axis=1, keepdims=True)
            acc_s[...] = alpha * acc_s[...] + jnp.dot(p.astype(BF16), v_ref[...].astype(BF16),
                                                      preferred_element_type=F32)
            m_s[...] = m_new

        @pl.when(j <= i)
        def _():
            step(j == i)

        @pl.when(j == i)
        def _():
            o_ref[...] = (acc_s[...] / l_s[...]).astype(o_ref.dtype)
            lse_ref[...] = _lanes(m_s[...] + jnp.log(l_s[...]))

    blk = (None, tq, LANES)
    in_specs = [pl.BlockSpec(blk, lambda b, h, i, j: (b, i, q0 + h)),
                pl.BlockSpec(blk, lambda b, h, i, j: (b, jnp.minimum(j, i), kv0 + 2 * h)),
                pl.BlockSpec(blk, lambda b, h, i, j: (b, jnp.minimum(j, i), kv0 + 2 * h + 1))]
    args = [qa, kva, kva]
    if gated:
        in_specs += [pl.BlockSpec(blk, lambda b, h, i, j: (b, i, 0)),
                     pl.BlockSpec((None, 8, tq), lambda b, h, i, j: (b, 0, jnp.minimum(j, i)))]
        args += list(gates)
    in_specs.append(pl.BlockSpec(memory_space=pl.ANY))
    args.append(mo)
    return pl.pallas_call(
        body, name=name, grid=(bsz, N_HEADS, n_q, n_q), in_specs=in_specs,
        out_specs=[pl.BlockSpec(blk, lambda b, h, i, j: (b, i, o0 + h)),
                   pl.BlockSpec((None, None, tq, LANES), lambda b, h, i, j: (b, h, i, 0))],
        out_shape=[jax.ShapeDtypeStruct(mo.shape, mo.dtype),
                   jax.ShapeDtypeStruct((bsz, N_HEADS, seq, LANES), F32)],
        scratch_shapes=[pltpu.VMEM((tq, 1), F32), pltpu.VMEM((tq, 1), F32), pltpu.VMEM((tq, LANES), F32)],
        input_output_aliases={len(args) - 1: 0},
        compiler_params=_cparams(("parallel", "parallel", "parallel", "arbitrary")),
    )(*args)


def _attn_bwd_q(qa, q0, kva, kv0, mo, dmo, o0, lse, gates, scale, out, out0, name, tq=None):
    bsz, seq, _ = qa.shape
    tq = ATTN_TILE if tq is None else tq
    n_q = seq // tq
    gated = gates is not None
    aliased = not isinstance(out, jax.ShapeDtypeStruct)

    def body(*refs):
        q_ref, k_ref, v_ref, o_ref, do_ref, lse_ref = refs[:6]
        gate_refs = refs[6:8] if gated else None
        dq_ref, delta_ref, dfq_ref, acc_s, dl_s, df_s = refs[-6:]
        h, i, j = pl.program_id(1), pl.program_id(2), pl.program_id(3)

        @pl.when(j == 0)
        def _():
            acc_s[...] = jnp.zeros_like(acc_s)
            df_s[...] = jnp.zeros_like(df_s)
            dl_s[...] = jnp.sum(do_ref[...] * o_ref[...].astype(F32), axis=1, keepdims=True)

        def step(masked):
            s, _ = _scores(q_ref, k_ref, gate_refs, scale, h, masked, tq, tq)
            p = jnp.exp(s - lse_ref[:, 0:1])
            dp = lax.dot_general(do_ref[...].astype(BF16), v_ref[...].astype(BF16), _DN["nt"],
                                 preferred_element_type=F32)
            ds = p * (dp - dl_s[...])
            acc_s[...] += jnp.dot(ds.astype(BF16), k_ref[...].astype(BF16), preferred_element_type=F32)
            df_s[...] += jnp.sum(ds, axis=1, keepdims=True)

        @pl.when(j <= i)
        def _():
            step(j == i)

        @pl.when(j == i)
        def _():
            dq_ref[...] = (acc_s[...] * scale).astype(dq_ref.dtype)
            delta_ref[...] = _lanes(dl_s[...])
            dfq_ref[...] = _lanes(df_s[...])

    blk = (None, tq, LANES)
    col = pl.BlockSpec((None, None, tq, LANES), lambda b, h, i, j: (b, h, i, 0))
    in_specs = [pl.BlockSpec(blk, lambda b, h, i, j: (b, i, q0 + h)),
                pl.BlockSpec(blk, lambda b, h, i, j: (b, jnp.minimum(j, i), kv0 + 2 * h)),
                pl.BlockSpec(blk, lambda b, h, i, j: (b, jnp.minimum(j, i), kv0 + 2 * h + 1)),
                pl.BlockSpec(blk, lambda b, h, i, j: (b, i, o0 + h)),
                pl.BlockSpec(blk, lambda b, h, i, j: (b, i, o0 + h)), col]
    args = [qa, kva, kva, mo, dmo, lse]
    if gated:
        in_specs += [pl.BlockSpec(blk, lambda b, h, i, j: (b, i, 0)),
                     pl.BlockSpec((None, 8, tq), lambda b, h, i, j: (b, 0, jnp.minimum(j, i)))]
        args += list(gates)
    aliases = {}
    if aliased:
        in_specs.append(pl.BlockSpec(memory_space=pl.ANY))
        args.append(out)
        aliases = {len(args) - 1: 0}
    vec = jax.ShapeDtypeStruct((bsz, N_HEADS, seq, LANES), F32)
    return pl.pallas_call(
        body, name=name, grid=(bsz, N_HEADS, n_q, n_q), in_specs=in_specs,
        out_specs=[pl.BlockSpec(blk, lambda b, h, i, j: (b, i, out0 + h)), col, col],
        out_shape=[jax.ShapeDtypeStruct(out.shape, out.dtype), vec, vec],
        scratch_shapes=[pltpu.VMEM((tq, LANES), F32), pltpu.VMEM((tq, 1), F32), pltpu.VMEM((tq, 1), F32)],
        input_output_aliases=aliases,
        compiler_params=_cparams(("parallel", "parallel", "parallel", "arbitrary")),
    )(*args)


def _attn_bwd_kv(qa, q0, kva, kv0, dmo, o0, lse, delta, gates, scale, out, out0, name, tq=None):
    bsz, seq, _ = qa.shape
    tq = ATTN_TILE if tq is None else tq
    n_q = seq // tq
    gated = gates is not None
    aliased = not isinstance(out, jax.ShapeDtypeStruct)

    def body(*refs):
        q_ref, k_ref, v_ref, do_ref, lse_ref, dl_ref = refs[:6]
        gate_refs = refs[6:8] if gated else None
        dkv_ref, dfk_ref, dk_s, dv_s, df_s = refs[-5:]
        h, j, i = pl.program_id(1), pl.program_id(2), pl.program_id(3)

        @pl.when(i == 0)
        def _():
            dk_s[...] = jnp.zeros_like(dk_s)
            dv_s[...] = jnp.zeros_like(dv_s)
            df_s[...] = jnp.zeros_like(df_s)

        def step(masked):
            s, q = _scores(q_ref, k_ref, gate_refs, scale, h, masked, tq, tq)
            p = jnp.exp(s - lse_ref[:, 0:1])
            do_b = do_ref[...].astype(BF16)
            dp = lax.dot_general(do_b, v_ref[...].astype(BF16), _DN["nt"], preferred_element_type=F32)
            ds = p * (dp - dl_ref[:, 0:1])
            dv_s[...] += lax.dot_general(p.astype(BF16), do_b, _DN["tn"], preferred_element_type=F32)
            dk_s[...] += lax.dot_general(ds.astype(BF16), q, _DN["tn"], preferred_element_type=F32)
            df_s[...] -= jnp.sum(ds, axis=0, keepdims=True)

        @pl.when(i > j)
        def _():
            step(False)

        @pl.when(i == j)
        def _():
            step(True)

        @pl.when(i == n_q - 1)
        def _():
            dkv_ref[:, 0:LANES] = dk_s[...].astype(dkv_ref.dtype)
            dkv_ref[:, LANES:2 * LANES] = dv_s[...].astype(dkv_ref.dtype)
            dfk_ref[...] = df_s[...]

    blk = (None, tq, LANES)
    col = pl.BlockSpec((None, None, tq, LANES), lambda b, h, j, i: (b, h, jnp.maximum(i, j), 0))
    in_specs = [pl.BlockSpec(blk, lambda b, h, j, i: (b, jnp.maximum(i, j), q0 + h)),
                pl.BlockSpec(blk, lambda b, h, j, i: (b, j, kv0 + 2 * h)),
                pl.BlockSpec(blk, lambda b, h, j, i: (b, j, kv0 + 2 * h + 1)),
                pl.BlockSpec(blk, lambda b, h, j, i: (b, jnp.maximum(i, j), o0 + h)), col, col]
    args = [qa, kva, kva, dmo, lse, delta]
    if gated:
        in_specs += [pl.BlockSpec(blk, lambda b, h, j, i: (b, jnp.maximum(i, j), 0)),
                     pl.BlockSpec((None, 8, tq), lambda b, h, j, i: (b, 0, j))]
        args += list(gates)
    aliases = {}
    if aliased:
        in_specs.append(pl.BlockSpec(memory_space=pl.ANY))
        args.append(out)
        aliases = {len(args) - 1: 0}
    return pl.pallas_call(
        body, name=name, grid=(bsz, N_HEADS, n_q, n_q), in_specs=in_specs,
        out_specs=[pl.BlockSpec((None, tq, 2 * LANES), lambda b, h, j, i: (b, j, out0 + h)),
                   pl.BlockSpec((None, None, 1, tq), lambda b, h, j, i: (b, h, 0, j))],
        out_shape=[jax.ShapeDtypeStruct(out.shape, out.dtype), jax.ShapeDtypeStruct((bsz, N_HEADS, 1, seq), F32)],
        scratch_shapes=[pltpu.VMEM((tq, LANES), F32), pltpu.VMEM((tq, LANES), F32), pltpu.VMEM((1, tq), F32)],
        input_output_aliases=aliases,
        compiler_params=_cparams(("parallel", "parallel", "parallel", "arbitrary")),
    )(*args)


def _block_logits(q, k_ref, gate, j, scale_unused, h, masked, tq):
    del scale_unused
    r = pl.multiple_of(j * tq, tq)
    s = lax.dot_general(q, k_ref[pl.ds(r, tq), :].astype(BF16), _DN["nt"], preferred_element_type=F32)
    if gate is not None:
        fcol, fr_ref = gate
        sub = lax.broadcasted_iota(jnp.int32, (8, tq), 0)
        frow = jnp.sum(jnp.where(sub == h, fr_ref[:, pl.ds(r, tq)], 0.0), axis=0, keepdims=True)
        s = s + (fcol - frow)
    if masked:
        r_i = lax.broadcasted_iota(jnp.int32, (tq, tq), 0)
        c_i = lax.broadcasted_iota(jnp.int32, (tq, tq), 1)
        s = jnp.where(c_i <= r_i, s, NEG)
    return s, r


def _gate_col(fc_ref, h, tq):
    lane = lax.broadcasted_iota(jnp.int32, (tq, LANES), 1)
    return jnp.sum(jnp.where(lane == h, fc_ref[...], 0.0), axis=1, keepdims=True)


def _attn_fwd_loop(qa, q0, kva, kv0, mo, o0, gates, scale, name, tq=None):
    bsz, seq, _ = qa.shape
    tq = ATTN_TILE if tq is None else tq
    n_q = seq // tq
    gated = gates is not None

    def body(*refs):
        q_ref, k_ref, v_ref = refs[:3]
        o_ref, lse_ref = refs[-2:]
        h, i = pl.program_id(1), pl.program_id(2)
        q = (q_ref[...].astype(F32) * scale).astype(BF16)
        gate = (_gate_col(refs[3], h, tq), refs[4]) if gated else None

        def step(j, carry, masked):
            m_prev, l_prev, acc = carry
            s, r = _block_logits(q, k_ref, gate, j, None, h, masked, tq)
            m_new = jnp.maximum(m_prev, jnp.max(s, axis=1, keepdims=True))
            alpha = jnp.exp(m_prev - m_new)
            p = jnp.exp(s - m_new)
            l_new = alpha * l_prev + jnp.sum(p, axis=1, keepdims=True)
            acc = alpha * acc + jnp.dot(p.astype(BF16), v_ref[pl.ds(r, tq), :].astype(BF16),
                                        preferred_element_type=F32)
            return m_new, l_new, acc

        init = (jnp.full((tq, 1), NEG, F32), jnp.zeros((tq, 1), F32), jnp.zeros((tq, LANES), F32))
        carry = lax.fori_loop(0, i, lambda j, c: step(j, c, False), init)
        m_f, l_f, acc = step(i, carry, True)
        o_ref[...] = (acc / l_f).astype(o_ref.dtype)
        lse_ref[...] = _lanes(m_f + jnp.log(l_f))

    blk = (None, tq, LANES)
    full = (None, seq, LANES)
    in_specs = [pl.BlockSpec(blk, lambda b, h, i: (b, i, q0 + h)),
                pl.BlockSpec(full, lambda b, h, i: (b, 0, kv0 + 2 * h)),
                pl.BlockSpec(full, lambda b, h, i: (b, 0, kv0 + 2 * h + 1))]
    args = [qa, kva, kva]
    if gated:
        in_specs += [pl.BlockSpec(blk, lambda b, h, i: (b, i, 0)),
                     pl.BlockSpec((None, 8, seq), lambda b, h, i: (b, 0, 0))]
        args += list(gates)
    in_specs.append(pl.BlockSpec(memory_space=pl.ANY))
    args.append(mo)
    return pl.pallas_call(
        body, name=name, grid=(bsz, N_HEADS, n_q), in_specs=in_specs,
        out_specs=[pl.BlockSpec(blk, lambda b, h, i: (b, i, o0 + h)),
                   pl.BlockSpec((None, None, tq, LANES), lambda b, h, i: (b, h, i, 0))],
        out_shape=[jax.ShapeDtypeStruct(mo.shape, mo.dtype),
                   jax.ShapeDtypeStruct((bsz, N_HEADS, seq, LANES), F32)],
        input_output_aliases={len(args) - 1: 0},
        compiler_params=_cparams(("parallel", "parallel", "parallel")),
    )(*args)


def _attn_bwd_q_loop(qa, q0, kva, kv0, mo, dmo, o0, lse, gates, scale, out, out0, name, tq=None):
    bsz, seq, _ = qa.shape
    tq = ATTN_TILE if tq is None else tq
    n_q = seq // tq
    gated = gates is not None
    aliased = not isinstance(out, jax.ShapeDtypeStruct)

    def body(*refs):
        q_ref, k_ref, v_ref, o_ref, do_ref, lse_ref = refs[:6]
        dq_ref, delta_ref, dfq_ref = refs[-3:]
        h, i = pl.program_id(1), pl.program_id(2)
        q = (q_ref[...].astype(F32) * scale).astype(BF16)
        gate = (_gate_col(refs[6], h, tq), refs[7]) if gated else None
        do_v = do_ref[...]
        do_b = do_v.astype(BF16)
        delta = jnp.sum(do_v * o_ref[...].astype(F32), axis=1, keepdims=True)
        lse_v = lse_ref[:, 0:1]

        def step(j, carry, masked):
            acc, dfq = carry
            s, r = _block_logits(q, k_ref, gate, j, None, h, masked, tq)
            p = jnp.exp(s - lse_v)
            dp = lax.dot_general(do_b, v_ref[pl.ds(r, tq), :].astype(BF16), _DN["nt"], preferred_element_type=F32)
            ds = p * (dp - delta)
            acc = acc + jnp.dot(ds.astype(BF16), k_ref[pl.ds(r, tq), :].astype(BF16), preferred_element_type=F32)
            return acc, dfq + jnp.sum(ds, axis=1, keepdims=True)

        init = (jnp.zeros((tq, LANES), F32), jnp.zeros((tq, 1), F32))
        carry = lax.fori_loop(0, i, lambda j, c: step(j, c, False), init)
        acc, dfq = step(i, carry, True)
        dq_ref[...] = (acc * scale).astype(dq_ref.dtype)
        delta_ref[...] = _lanes(delta)
        dfq_ref[...] = _lanes(dfq)

    blk = (None, tq, LANES)
    full = (None, seq, LANES)
    stat = pl.BlockSpec((None, None, tq, LANES), lambda b, h, i: (b, h, i, 0))
    in_specs = [pl.BlockSpec(blk, lambda b, h, i: (b, i, q0 + h)),
                pl.BlockSpec(full, lambda b, h, i: (b, 0, kv0 + 2 * h)),
                pl.BlockSpec(full, lambda b, h, i: (b, 0, kv0 + 2 * h + 1)),
                pl.BlockSpec(blk, lambda b, h, i: (b, i, o0 + h)),
                pl.BlockSpec(blk, lambda b, h, i: (b, i, o0 + h)), stat]
    args = [qa, kva, kva, mo, dmo, lse]
    if gated:
        in_specs += [pl.BlockSpec(blk, lambda b, h, i: (b, i, 0)),
                     pl.BlockSpec((None, 8, seq), lambda b, h, i: (b, 0, 0))]
        args += list(gates)
    aliases = {}
    if aliased:
        in_specs.append(pl.BlockSpec(memory_space=pl.ANY))
        args.append(out)
        aliases = {len(args) - 1: 0}
    vec = jax.ShapeDtypeStruct((bsz, N_HEADS, seq, LANES), F32)
    return pl.pallas_call(
        body, name=name, grid=(bsz, N_HEADS, n_q), in_specs=in_specs,
        out_specs=[pl.BlockSpec(blk, lambda b, h, i: (b, i, out0 + h)), stat, stat],
        out_shape=[jax.ShapeDtypeStruct(out.shape, out.dtype), vec, vec],
        input_output_aliases=aliases,
        compiler_params=_cparams(("parallel", "parallel", "parallel")),
    )(*args)


def _attn_bwd_kv_loop(qa, q0, kva, kv0, dmo, o0, lse, delta, gates, scale, out, out0, name, tq=None):
    bsz, seq, _ = qa.shape
    tq = ATTN_TILE if tq is None else tq
    n_q = seq // tq
    gated = gates is not None
    aliased = not isinstance(out, jax.ShapeDtypeStruct)

    def body(*refs):
        q_ref, k_ref, v_ref, do_ref, lse_ref, dl_ref = refs[:6]
        dkv_ref, dfk_ref = refs[-2:]
        h, j = pl.program_id(1), pl.program_id(2)
        k_b = k_ref[...].astype(BF16)
        v_b = v_ref[...].astype(BF16)
        if gated:
            fc_ref, fr_ref = refs[6], refs[7]
            sub = lax.broadcasted_iota(jnp.int32, (8, tq), 0)
            frow = jnp.sum(jnp.where(sub == h, fr_ref[...], 0.0), axis=0, keepdims=True)
            lane = lax.broadcasted_iota(jnp.int32, (tq, LANES), 1)

        def step(i, carry, masked):
            dk, dv, dfk = carry
            r = pl.multiple_of(i * tq, tq)
            q = (q_ref[pl.ds(r, tq), :].astype(F32) * scale).astype(BF16)
            s = lax.dot_general(q, k_b, _DN["nt"], preferred_element_type=F32)
            if gated:
                fcol = jnp.sum(jnp.where(lane == h, fc_ref[pl.ds(r, tq), :], 0.0), axis=1, keepdims=True)
                s = s + (fcol - frow)
            if masked:
                r_i = lax.broadcasted_iota(jnp.int32, (tq, tq), 0)
                c_i = lax.broadcasted_iota(jnp.int32, (tq, tq), 1)
                s = jnp.where(c_i <= r_i, s, NEG)
            p = jnp.exp(s - lse_ref[pl.ds(r, tq), 0:1])
            do_b = do_ref[pl.ds(r, tq), :].astype(BF16)
            dp = lax.dot_general(do_b, v_b, _DN["nt"], preferred_element_type=F32)
            ds = p * (dp - dl_ref[pl.ds(r, tq), 0:1])
            dv = dv + lax.dot_general(p.astype(BF16), do_b, _DN["tn"], preferred_element_type=F32)
            dk = dk + lax.dot_general(ds.astype(BF16), q, _DN["tn"], preferred_element_type=F32)
            return dk, dv, dfk - jnp.sum(ds, axis=0, keepdims=True)

        init = (jnp.zeros((tq, LANES), F32), jnp.zeros((tq, LANES), F32), jnp.zeros((1, tq), F32))
        carry = step(j, init, True)
        dk, dv, dfk = lax.fori_loop(j + 1, n_q, lambda i, c: step(i, c, False), carry)
        dkv_ref[:, 0:LANES] = dk.astype(dkv_ref.dtype)
        dkv_ref[:, LANES:2 * LANES] = dv.astype(dkv_ref.dtype)
        dfk_ref[...] = dfk

    blk = (None, tq, LANES)
    full = (None, seq, LANES)
    stat = pl.BlockSpec((None, None, seq, LANES), lambda b, h, j: (b, h, 0, 0))
    in_specs = [pl.BlockSpec(full, lambda b, h, j: (b, 0, q0 + h)),
                pl.BlockSpec(blk, lambda b, h, j: (b, j, kv0 + 2 * h)),
                pl.BlockSpec(blk, lambda b, h, j: (b, j, kv0 + 2 * h + 1)),
                pl.BlockSpec(full, lambda b, h, j: (b, 0, o0 + h)), stat, stat]
    args = [qa, kva, kva, dmo, lse, delta]
    if gated:
        in_specs += [pl.BlockSpec(full, lambda b, h, j: (b, 0, 0)),
                     pl.BlockSpec((None, 8, tq), lambda b, h, j: (b, 0, j))]
        args += list(gates)
    aliases = {}
    if aliased:
        in_specs.append(pl.BlockSpec(memory_space=pl.ANY))
        args.append(out)
        aliases = {len(args) - 1: 0}
    return pl.pallas_call(
        body, name=name, grid=(bsz, N_HEADS, n_q), in_specs=in_specs,
        out_specs=[pl.BlockSpec((None, tq, 2 * LANES), lambda b, h, j: (b, j, out0 + h)),
                   pl.BlockSpec((None, None, 1, tq), lambda b, h, j: (b, h, 0, j))],
        out_shape=[jax.ShapeDtypeStruct(out.shape, out.dtype), jax.ShapeDtypeStruct((bsz, N_HEADS, 1, seq), F32)],
        input_output_aliases=aliases,
        compiler_params=_cparams(("parallel", "parallel", "parallel")),
    )(*args)


def _gmlp_fn(uv, lng, lnb, ws, bst):
    u = jax.nn.gelu(uv[:, 0:GROUP_WIDTH])
    gv = jax.nn.gelu(uv[:, GROUP_WIDTH:2 * GROUP_WIDTH])
    mu = jnp.mean(gv, axis=-1, keepdims=True)
    vc = gv - mu
    var = jnp.mean(vc * vc, axis=-1, keepdims=True)
    vln = vc * lax.rsqrt(var + LN_EPS) * lng + lnb
    r_i = lax.broadcasted_iota(jnp.int32, (D_CHUNK, D_CHUNK), 0)
    c_i = lax.broadcasted_iota(jnp.int32, (D_CHUNK, D_CHUNK), 1)
    lane_g = lax.broadcasted_iota(jnp.int32, (D_CHUNK, GROUP_WIDTH), 1) // HEAD_DIM
    e_r = lax.broadcasted_iota(jnp.int32, (LANES, GROUP_WIDTH), 0)
    e_c = lax.broadcasted_iota(jnp.int32, (LANES, GROUP_WIDTH), 1)
    expand = (e_r == e_c // HEAD_DIM).astype(F32)
    mixed = jnp.dot(bst, expand, precision=HI, preferred_element_type=F32)
    for g in range(4):
        w = jnp.where(r_i >= c_i, ws[g], 0.0)
        mixed = mixed + jnp.where(lane_g == g, _bdot(w, vln, "nn"), 0.0)
    return u * mixed


def _gmlp_fwd(proj, mo, lng, lnb, ws, bst, name):
    bsz, seq, _ = proj.shape

    def body(p_ref, mo_any, lng_ref, lnb_ref, ws_ref, bst_ref, o_ref):
        del mo_any
        o_ref[...] = _gmlp_fn(p_ref[...], lng_ref[...], lnb_ref[...], ws_ref[...], bst_ref[...]).astype(o_ref.dtype)

    return pl.pallas_call(
        body, name=name, grid=(bsz, seq // D_CHUNK),
        in_specs=[pl.BlockSpec((None, D_CHUNK, 512), lambda b, s: (b, s, P_D // 512)),
                  pl.BlockSpec(memory_space=pl.ANY), _vec_spec(256), _vec_spec(256),
                  pl.BlockSpec((4, D_CHUNK, D_CHUNK), lambda b, s: (0, 0, 0)),
                  pl.BlockSpec((D_CHUNK, LANES), lambda b, s: (0, 0))],
        out_specs=pl.BlockSpec((None, D_CHUNK, GROUP_WIDTH), lambda b, s: (b, s, 1280 // GROUP_WIDTH)),
        out_shape=jax.ShapeDtypeStruct(mo.shape, mo.dtype),
        input_output_aliases={1: 0},
        compiler_params=_cparams(("parallel", "parallel")),
    )(proj, mo, lng, lnb, ws, bst)


def _gmlp_bwd(dmo, dproj, proj, lng, lnb, ws, bst, name):
    bsz, seq, _ = proj.shape

    def body(do_ref, dp_any, p_ref, lng_ref, lnb_ref, ws_ref, bst_ref, dp_ref, dlg_ref, dlb_ref, dws_ref, dbst_ref):
        del dp_any
        first = jnp.logical_and(pl.program_id(0) == 0, pl.program_id(1) == 0)

        @pl.when(first)
        def _():
            dlg_ref[...] = jnp.zeros_like(dlg_ref)
            dlb_ref[...] = jnp.zeros_like(dlb_ref)
            dws_ref[...] = jnp.zeros_like(dws_ref)
            dbst_ref[...] = jnp.zeros_like(dbst_ref)

        _, vjp = jax.vjp(_gmlp_fn, p_ref[...], lng_ref[...], lnb_ref[...], ws_ref[...], bst_ref[...])
        duv, dlg, dlb, dws, dbst = vjp(do_ref[...])
        dp_ref[...] = duv.astype(dp_ref.dtype)
        dlg_ref[...] += dlg
        dlb_ref[...] += dlb
        dws_ref[...] += dws
        dbst_ref[...] += dbst

    const2 = lambda shape: pl.BlockSpec(shape, lambda b, s: (0,) * len(shape))
    return pl.pallas_call(
        body, name=name, grid=(bsz, seq // D_CHUNK),
        in_specs=[pl.BlockSpec((None, D_CHUNK, GROUP_WIDTH), lambda b, s: (b, s, 1280 // GROUP_WIDTH)),
                  pl.BlockSpec(memory_space=pl.ANY),
                  pl.BlockSpec((None, D_CHUNK, 512), lambda b, s: (b, s, P_D // 512)),
                  _vec_spec(256), _vec_spec(256), const2((4, D_CHUNK, D_CHUNK)), const2((D_CHUNK, LANES))],
        out_specs=[pl.BlockSpec((None, D_CHUNK, 512), lambda b, s: (b, s, P_D // 512)),
                   _vec_spec(256), _vec_spec(256), const2((4, D_CHUNK, D_CHUNK)), const2((D_CHUNK, LANES))],
        out_shape=[jax.ShapeDtypeStruct(dproj.shape, dproj.dtype), jax.ShapeDtypeStruct((1, 256), F32),
                   jax.ShapeDtypeStruct((1, 256), F32), jax.ShapeDtypeStruct((4, D_CHUNK, D_CHUNK), F32),
                   jax.ShapeDtypeStruct((D_CHUNK, LANES), F32)],
        input_output_aliases={1: 0},
        compiler_params=_cparams(("arbitrary", "arbitrary")),
    )(dmo, dproj, proj, lng, lnb, ws, bst)


def _ada_fwd(c_all, ada_w, name):
    n_b = c_all.shape[0]
    depth, d, cols = ada_w.shape

    def body(c_ref, w_ref, o_ref):
        cv = c_ref[...]
        act = (cv * jax.nn.sigmoid(cv)).astype(BF16)
        o_ref[...] = jnp.dot(act, w_ref[...].astype(BF16), preferred_element_type=F32)

    return pl.pallas_call(
        body, name=name, grid=(depth,),
        in_specs=[pl.BlockSpec((n_b, d), lambda l: (0, 0)), pl.BlockSpec((None, d, cols), lambda l: (l, 0, 0))],
        out_specs=pl.BlockSpec((None, n_b, cols), lambda l: (l, 0, 0)),
        out_shape=jax.ShapeDtypeStruct((depth, n_b, cols), F32),
        compiler_params=_cparams(("parallel",)),
    )(c_all, ada_w)


def _ada_bwd(c_all, dmod_cols, dmod_full, name):
    n_b, d = c_all.shape
    depth, _, cols = dmod_cols.shape
    full = dmod_full.shape[-1]

    def body(c_ref, dm_ref, df_ref, gw_ref, gb_ref):
        cv = c_ref[...]
        act = (cv * jax.nn.sigmoid(cv)).astype(BF16)
        gw_ref[...] = lax.dot_general(act, dm_ref[...].astype(BF16), (((0,), (0,)), ((), ())),
                                      preferred_element_type=F32)
        gb_ref[...] = jnp.sum(df_ref[...], axis=0, keepdims=True)

    return pl.pallas_call(
        body, name=name, grid=(depth,),
        in_specs=[pl.BlockSpec((n_b, d), lambda l: (0, 0)), pl.BlockSpec((None, n_b, cols), lambda l: (l, 0, 0)),
                  pl.BlockSpec((None, n_b, full), lambda l: (l, 0, 0))],
        out_specs=[pl.BlockSpec((None, d, cols), lambda l: (l, 0, 0)),
                   pl.BlockSpec((None, 1, full), lambda l: (l, 0, 0))],
        out_shape=[jax.ShapeDtypeStruct((depth, d, cols), F32), jax.ShapeDtypeStruct((depth, 1, full), F32)],
        compiler_params=_cparams(("parallel",)),
    )(c_all, dmod_cols, dmod_full)


def _adamw(gparts, own, w, m, v, name, layer=0, prev=None):
    n_p, rows, cols = gparts.shape
    assert w.shape[1:] == (rows, cols)
    tr = rows
    if rows > 512:
        tr = next(c for c in range(512, 7, -8) if rows % c == 0)
    has_own = own is not None
    n_prev = 0 if prev is None else 4

    def body(*refs):
        if has_own:
            slot_ref, refs = refs[0], refs[1:]
        g_ref = refs[0]
        own_ref = refs[1] if has_own else None
        w_ref, m_ref, v_ref = refs[1 + has_own:4 + has_own]
        go_ref, do_ref, mo_ref, vo_ref = refs[4 + has_own + n_prev:]
        g = None
        for p in range(n_p):
            term = g_ref[p].astype(F32)
            if has_own:
                term = jnp.where(slot_ref[0] == p, own_ref[...].astype(F32), term)
            g = term if g is None else g + term
        m_new = ADAM_B1 * m_ref[...] + (1.0 - ADAM_B1) * g
        v_new = ADAM_B2 * v_ref[...] + (1.0 - ADAM_B2) * (g * g)
        m_hat = m_new / (1.0 - ADAM_B1 ** ADAM_STEP)
        v_hat = v_new / (1.0 - ADAM_B2 ** ADAM_STEP)
        go_ref[...] = g
        do_ref[...] = -ADAM_LR * (m_hat / (jnp.sqrt(v_hat) + ADAM_EPS) + ADAM_WD * w_ref[...])
        mo_ref[...] = m_new
        vo_ref[...] = v_new

    spec = pl.BlockSpec((None, tr, cols), lambda i, *_: (layer, i, 0))
    in_specs = [pl.BlockSpec((n_p, tr, cols), lambda i, *_: (0, i, 0))]
    args = [gparts]
    if has_own:
        in_specs.append(pl.BlockSpec((None, tr, cols), lambda i, slot: (slot[0], i, 0)))
        args.append(own[0])
    in_specs += [spec, spec, spec]
    args += [w, m, v]
    aliases = {}
    if prev is not None:
        aliases = {has_own + len(args) + k: k for k in range(4)}
        in_specs += [pl.BlockSpec(memory_space=pl.ANY)] * 4
        args += list(prev)
    shp = jax.ShapeDtypeStruct(w.shape, F32)
    out_specs, out_shape = [spec, spec, spec, spec], [shp, shp, shp, shp]
    if not has_own:
        return pl.pallas_call(
            body, name=name, grid=(rows // tr,), in_specs=in_specs, out_specs=out_specs, out_shape=out_shape,
            input_output_aliases=aliases, compiler_params=_cparams(("parallel",)),
        )(*args)
    return pl.pallas_call(
        body, name=name, out_shape=out_shape, input_output_aliases=aliases,
        grid_spec=pltpu.PrefetchScalarGridSpec(num_scalar_prefetch=1, grid=(rows // tr,), in_specs=in_specs,
                                               out_specs=out_specs),
        compiler_params=_cparams(("parallel",)),
    )(jnp.reshape(own[1], (1,)).astype(jnp.int32), *args)


def _sum_parts(parts, name):
    n_p, rows, cols = parts.shape
    tr = 256 if rows % 256 == 0 else rows

    def body(p_ref, o_ref):
        acc = p_ref[0]
        for p in range(1, n_p):
            acc = acc + p_ref[p]
        o_ref[...] = acc

    return pl.pallas_call(
        body, name=name, grid=(rows // tr,),
        in_specs=[pl.BlockSpec((n_p, tr, cols), lambda i: (0, i, 0))],
        out_specs=pl.BlockSpec((tr, cols), lambda i: (i, 0)),
        out_shape=jax.ShapeDtypeStruct((rows, cols), F32),
        compiler_params=_cparams(("parallel",)),
    )(parts)


def _all_gather(arrs, name):
    n = len(arrs)

    def body(*refs):
        in_refs, out_refs = refs[:n], refs[n:2 * n]
        send_sems, recv_sems, loc_sems = refs[2 * n:]
        x, y, c = lax.axis_index("x"), lax.axis_index("y"), lax.axis_index("c")
        me, sibling = (x, y, c), (x, y, 1 - c)
        chips = [(1 - x, y), (x, 1 - y), (1 - x, 1 - y)]

        def copy(a, k, block, to, src=None):
            slot = out_refs[a].at[4 * block[0] + 2 * block[1] + block[2]]
            return pltpu.make_async_remote_copy(
                src_ref=slot if src is None else src, dst_ref=slot, send_sem=send_sems.at[a, k],
                recv_sem=recv_sems.at[a, k], device_id=to, device_id_type=pl.DeviceIdType.MESH)

        mine = [pltpu.make_async_copy(in_refs[a], out_refs[a].at[4 * x + 2 * y + c], loc_sems.at[a])
                for a in range(n)]
        for cp in mine:
            cp.start()
        first = []
        for a in range(n):
            first.append(copy(a, 0, me, sibling, src=in_refs[a]))
            first += [copy(a, 1 + j, me, (*chip, c), src=in_refs[a]) for j, chip in enumerate(chips)]
        for cp in first:
            cp.start()
        passed = []
        for j, chip in enumerate(chips):
            for a in range(n):
                copy(a, 1 + j, (*chip, c), me).wait_recv()
                cp = copy(a, 4 + j, (*chip, c), sibling)
                cp.start()
                passed.append(cp)
        for a in range(n):
            copy(a, 0, sibling, me).wait_recv()
        for j, chip in enumerate(chips):
            for a in range(n):
                copy(a, 4 + j, (*chip, 1 - c), me).wait_recv()
        for cp in first + passed:
            cp.wait_send()
        for cp in mine:
            cp.wait()

    any_spec = pl.BlockSpec(memory_space=pl.ANY)
    return pl.pallas_call(
        body, name=name, in_specs=[any_spec] * n, out_specs=[any_spec] * n,
        out_shape=[jax.ShapeDtypeStruct((N_DEV,) + a.shape, a.dtype) for a in arrs],
        scratch_shapes=[pltpu.SemaphoreType.DMA((n, N_DEV - 1)), pltpu.SemaphoreType.DMA((n, N_DEV - 1)),
                        pltpu.SemaphoreType.DMA((n,))],
    )(*arrs)


def _flip_peers():
    x, y, c = lax.axis_index("x"), lax.axis_index("y"), lax.axis_index("c")
    peers = []
    for fx, fy, fc in [(fx, fy, fc) for fx in (0, 1) for fy in (0, 1) for fc in (0, 1)][1:]:
        px, py, pc = (1 - x if fx else x), (1 - y if fy else y), (1 - c if fc else c)
        peers.append(((px, py, pc), 4 * px + 2 * py + pc))
    return 4 * x + 2 * y + c, peers


def _push_start(srcs, name, whole=False):
    n, n_peer = len(srcs), N_DEV - 1
    if whole:
        me_w = 4 * lax.axis_index("x") + 2 * lax.axis_index("y") + lax.axis_index("c")
        lands = [lax.dynamic_update_slice_in_dim(lax.empty((N_DEV,) + a.shape, a.dtype), a[None], me_w, axis=0)
                 for a in srcs]
    else:
        lands = [lax.empty(a.shape, a.dtype) for a in srcs]

    def body(*refs):
        src_refs, land_refs = refs[:n], refs[n:2 * n]
        send_sems, recv_sems = refs[2 * n], refs[2 * n + 1]
        token = refs[-1]
        me, peers = _flip_peers()
        for k, (dev, idx) in enumerate(peers):
            for a in range(n):
                pltpu.make_async_remote_copy(
                    src_ref=src_refs[a] if whole else src_refs[a].at[idx], dst_ref=land_refs[a].at[me],
                    send_sem=send_sems.at[a * n_peer + k], recv_sem=recv_sems.at[a * n_peer + k], device_id=dev,
                    device_id_type=pl.DeviceIdType.MESH).start()
        token[...] = jnp.zeros_like(token)

    hbm = pl.BlockSpec(memory_space=pltpu.HBM)
    sem = pl.BlockSpec(memory_space=pltpu.SEMAPHORE)
    arrs = list(srcs) + lands
    res = pl.pallas_call(
        body, name=name, in_specs=[hbm] * (2 * n),
        out_specs=(sem, sem, *[hbm] * (2 * n), pl.BlockSpec(memory_space=pltpu.VMEM)),
        out_shape=(pltpu.SemaphoreType.DMA((n * n_peer,)), pltpu.SemaphoreType.DMA((n * n_peer,)),
                   *[pltpu.HBM(a.shape, a.dtype) for a in arrs], jax.ShapeDtypeStruct((8, LANES), F32)),
        input_output_aliases={i: 2 + i for i in range(2 * n)},
        compiler_params=pltpu.CompilerParams(has_side_effects=pltpu.SideEffectType.DATAFLOW_SIDE_EFFECTING),
    )(*[pltpu.with_memory_space_constraint(a, pltpu.HBM) for a in arrs])
    return res[0], res[1], list(res[2:2 + n]), list(res[2 + n:2 + 2 * n]), res[-1]


def _push_wait(send_sems, recv_sems, srcs, lands, after, name, whole=False):
    n, n_peer = len(srcs), N_DEV - 1

    def body(*refs):
        src_refs, land_refs = refs[:n], refs[n:2 * n]
        send_s, recv_s = refs[2 * n], refs[2 * n + 1]
        _, peers = _flip_peers()
        for k, (dev, idx) in enumerate(peers):
            for a in range(n):
                cp = pltpu.make_async_remote_copy(
                    src_ref=src_refs[a] if whole else src_refs[a].at[idx], dst_ref=land_refs[a].at[idx],
                    send_sem=send_s.at[a * n_peer + k],
                    recv_sem=recv_s.at[a * n_peer + k], device_id=dev, device_id_type=pl.DeviceIdType.MESH)
                cp.wait_send()
                cp.wait_recv()

    hbm = pl.BlockSpec(memory_space=pltpu.HBM)
    sem = pl.BlockSpec(memory_space=pltpu.SEMAPHORE)
    arrs = list(srcs) + list(lands)
    res = pl.pallas_call(
        body, name=name, in_specs=[hbm] * (2 * n) + [sem, sem, pl.BlockSpec(memory_space=pl.ANY)],
        out_specs=tuple([hbm] * (2 * n)), out_shape=tuple(pltpu.HBM(a.shape, a.dtype) for a in arrs),
        input_output_aliases={i: i for i in range(2 * n)},
        compiler_params=pltpu.CompilerParams(has_side_effects=pltpu.SideEffectType.DATAFLOW_SIDE_EFFECTING),
    )(*arrs, send_sems, recv_sems, after)
    return list(res[:n]), list(res[n:])


def _ffn_fwd(x, h, mod, w_in, w_out_after, lng, lnb, rows, tag, nxt):
    bsz, seq, d = x.shape
    t = bsz * seq
    if h is None:
        h = _modulate(x, mod, rows[0], rows[1], f"modulate_{tag}")
    z, a = _ffn_in_swiglu(h.reshape(t, d), w_in, f"ffn_in_{tag}")
    f = _matmul_groupsum(a, w_out_after(a), out_dtype=F32, tm=512, name=f"ffn_out_{tag}").reshape(bsz, seq, d)
    y, h_next = _res_ln(x, f, mod, lng, lnb, rows[2], 0.5, f"res_ln_{tag}", nxt)
    return y, h_next, (x, h, z, a, f)


def _tied(mod, tie):
    return mod if tie is None else mod + tie


def _open_tail(tail):
    dh, x, mod, dx_res, sc_row = tail
    return dx_res, (dh, x, mod, sc_row)


def _ffn_bwd(dy, pre, saved, mod, w_in, w_out, lng, lnb, rows, tag, ready):
    x, h, z, a, f = saved
    bsz, seq, d = x.shape
    t = bsz * seq
    (dx_res, df, dgate, dlg, dlb), closed = _res_ln_bwd(dy, x, f, mod, lng, lnb, rows[2], 0.5,
                                                       f"res_ln_bwd_{tag}", pre)
    df2 = df.reshape(1, t, d)
    dw_out = _matmul(a, df2, mode="tn", group_out=True, out_dtype=BF16, tm=a.shape[2], tk=min(t, 2048),
                     name=f"ffn_out_dw_{tag}")
    tie_out = ready(f"{tag}_out", dw_out)
    dz = _ffn_out_dx_swiglu(df.reshape(t, d), w_out, z, f"ffn_out_dx_{tag}").reshape(N_DEV, t, -1)
    dw_in = _matmul(dz, h.reshape(1, t, d), mode="tn", group_out=True, out_dtype=BF16, tm=dz.shape[2],
                    tk=min(t, 2048), name=f"ffn_in_dw_{tag}")
    tie_in = ready(f"{tag}_in", dw_in)
    dh = _matmul_groupsum(dz, w_in, out_dtype=F32, tm=512, name=f"ffn_in_dx_{tag}").reshape(bsz, seq, d)
    tail = (dh, x, _tied(_tied(mod, tie_out), tie_in), dx_res, rows[1])
    return tail, closed, dgate, dw_in, dw_out, dlg, dlb


def _mixer_fwd(x, h, mod, wts, small, lng, lnb, layer, tabs):
    bsz, seq, d = x.shape
    t = bsz * seq
    proj = _matmul(h.reshape(1, t, d), wts["mix_in"][None], mode="nn", group_out=True, out_dtype=F32, tm=512, tk=d,
                   name="mix_in").reshape(bsz, seq, PACK_W)
    mo, states = _hgrn_fwd(proj, small["lb_logits8"], small["hgrn_norm_g"], layer, f"hgrn_fwd_l{layer}")
    q, kv = _mla_pre(proj, small["q_norm_g"], small["kv_norm_g"], wts["uq"], wts["ukv"], tabs, "mla_pre")
    mla_scale = float((B_NOPE + B_ROPE) ** -0.5)
    mo, lse_b = _attn_fwd_loop(q, 0, kv, 0, mo, 2, None, mla_scale, "mla_attn_fwd")
    fg = _fox_gate(proj, small["fox_b_f"], "fox_gate")
    gates = (fg, jnp.swapaxes(fg[:, :, 0:8], 1, 2))
    fox_scale = float(HEAD_DIM ** -0.5)
    mo, lse_c = _attn_fwd_loop(proj, P_CQ // LANES, proj, P_CKV // LANES, mo, 6, gates, fox_scale, "fox_attn_fwd")
    mo = _gmlp_fwd(proj, mo, small["gmlp_ln_g"], small["gmlp_ln_b"], small["gmlp_w_s"], small["gmlp_bst"],
                   "gmlp_fwd")
    mixed = _matmul(mo.reshape(1, t, MO_W), wts["mix_out"][None], mode="nn", group_out=True, out_dtype=F32,
                    tm=1024, tk=MO_W, name="mix_out").reshape(bsz, seq, d)
    y, h_next = _res_ln(x, mixed, mod, lng, lnb, 5, 1.0, "res_ln_mix", (mod, 6, 7))
    return y, h_next, (x, h, proj, mo, states, q, kv, lse_b, gates, lse_c, mixed)


def _mixer_bwd(dy, pre, saved, mod, wts, small, lng, lnb, layer, tabs, ready):
    x, h, proj, mo, states, q, kv, lse_b, gates, lse_c, mixed = saved
    bsz, seq, d = x.shape
    t = bsz * seq
    (dx_res, dmixed, dgate, dlg, dlb), closed = _res_ln_bwd(dy, x, mixed, mod, lng, lnb, 5, 1.0, "res_ln_bwd_mix",
                                                           pre)
    dm2 = dmixed.reshape(1, t, d)
    dmo = _matmul(dm2, wts["mix_out"][None], mode="nt", group_out=True, out_dtype=F32, tm=1024, tk=d,
                  name="mix_out_dx").reshape(bsz, seq, MO_W)
    dw_out = _matmul(mo.reshape(1, t, MO_W), dm2, mode="tn", group_out=True, out_dtype=F32, tm=512, tk=min(t, 2048),
                     name="mix_out_dw")[0]
    tie_out = ready("mix_out", dw_out)
    g = {}
    dproj, g["lb_logits8"], g["hgrn_norm_g"] = _hgrn_bwd(dmo, proj, states, small["lb_logits8"],
                                                         small["hgrn_norm_g"], layer, f"hgrn_bwd_l{layer}")
    mla_scale = float((B_NOPE + B_ROPE) ** -0.5)
    dq, delta_b, _ = _attn_bwd_q_loop(q, 0, kv, 0, mo, dmo, 2, lse_b, None, mla_scale,
                                 jax.ShapeDtypeStruct((bsz, seq, 512), F32), 0, "mla_attn_bwd_q")
    dkv, _ = _attn_bwd_kv_loop(q, 0, kv, 0, dmo, 2, lse_b, delta_b, None, mla_scale,
                          jax.ShapeDtypeStruct((bsz, seq, 1024), F32), 0, "mla_attn_bwd_kv")
    dproj, g["q_norm_g"], g["kv_norm_g"], g["uq"], g["ukv"] = _mla_pre_bwd(
        dq, dkv, dproj, proj, small["q_norm_g"], small["kv_norm_g"], wts["uq"], wts["ukv"], tabs, "mla_pre_bwd")
    ready("mla_uq", g.pop("uq"))
    ready("mla_ukv", g.pop("ukv"))
    fox_scale = float(HEAD_DIM ** -0.5)
    dproj, delta_c, dfq = _attn_bwd_q_loop(proj, P_CQ // LANES, proj, P_CKV // LANES, mo, dmo, 6, lse_c, gates,
                                      fox_scale, dproj, P_CQ // LANES, "fox_attn_bwd_q")
    dproj, dfk = _attn_bwd_kv_loop(proj, P_CQ // LANES, proj, P_CKV // LANES, dmo, 6, lse_c, delta_c, gates, fox_scale,
                              dproj, P_CKV // (2 * LANES), "fox_attn_bwd_kv")
    dfk_cols = jnp.pad(jnp.swapaxes(dfk[:, :, 0, :], 1, 2), ((0, 0), (0, 0), (0, LANES - N_HEADS)))
    dproj, g["fox_b_f"] = _fox_gate_bwd(dfq, dfk_cols, dproj, proj, small["fox_b_f"], "fox_gate_bwd")
    dproj, g["gmlp_ln_g"], g["gmlp_ln_b"], g["gmlp_w_s"], g["gmlp_bst"] = _gmlp_bwd(
        dmo, dproj, proj, small["gmlp_ln_g"], small["gmlp_ln_b"], small["gmlp_w_s"], small["gmlp_bst"], "gmlp_bwd")
    dp2 = dproj.reshape(1, t, PACK_W)
    dw_in = _matmul(h.reshape(1, t, d), dp2, mode="tn", group_out=True, out_dtype=BF16, tm=512, tk=1024,
                    name="mix_in_dw")[0]
    tie_in = ready("mix_in", dw_in)
    dh = _matmul(dp2, wts["mix_in"][None], mode="nt", group_out=True, out_dtype=F32, tm=512, tk=PACK_W,
                 name="mix_in_dx").reshape(bsz, seq, d)
    tail = (dh, x, _tied(_tied(mod, tie_out), tie_in), dx_res, 4)
    return tail, closed, dgate, dw_in, dw_out, g, dlg, dlb


def _small_views(p, layer):
    return {
        "lb_logits8": jnp.pad(p["hgrn_lb_logits"], ((0, 8 - DEPTH), (0, 0))),
        "hgrn_norm_g": p["hgrn_norm_g"][layer][None],
        "q_norm_g": p["mla_q_norm_g"][layer][None],
        "kv_norm_g": p["mla_kv_norm_g"][layer][None],
        "fox_b_f": jnp.pad(p["fox_b_f"][layer][None], ((0, 0), (0, LANES - N_HEADS))),
        "gmlp_ln_g": p["gmlp_ln_g"][layer][None],
        "gmlp_ln_b": p["gmlp_ln_b"][layer][None],
        "gmlp_w_s": p["gmlp_w_s"][layer],
        "gmlp_bst": jnp.pad(p["gmlp_b_s"][layer].T, ((0, 0), (0, LANES - N_HEADS))),
    }


def _local_step(x, mod, target, weights, p, grads_ready=None):
    bsz, seq, d = x.shape
    tabs = _rope_tables(seq)
    saved = []
    h = None
    for l in range(DEPTH):
        sm = _small_views(p, l)
        lng, lnb = p["ln_g"][l], p["ln_b"][l]
        x, h, s1 = _ffn_fwd(x, h, mod[l], weights(l, "ffn1_in", x)["ffn1_in"],
                            lambda a, l=l: weights(l, "ffn1_out", a)["ffn1_out"], lng[0:1], lnb[0:1], (0, 1, 2),
                            "ffn1", (mod[l], 3, 4))
        x, h, s2 = _mixer_fwd(x, h, mod[l], weights(l, "mix", x), sm, lng[1:2], lnb[1:2], l, tabs)
        x, h, s3 = _ffn_fwd(x, h, mod[l], weights(l, "ffn2_in", x)["ffn2_in"],
                            lambda a, l=l: weights(l, "ffn2_out", a)["ffn2_out"], lng[2:3], lnb[2:3], (6, 7, 8),
                            "ffn2", (mod[l + 1], 0, 1) if l + 1 < DEPTH else None)
        saved.append((s1, s2, s3))
    dx, loss = _loss_head(x, target, "loss_head")
    big, small, dmods = [None] * DEPTH, [None] * DEPTH, [None] * DEPTH
    ties = []
    tail, rows_of = None, {}

    def tied(a):
        for t in ties:
            a = a + t
        return a

    for l in reversed(range(DEPTH)):
        w = {}
        for part in ("ffn1_in", "ffn1_out", "mix", "ffn2_in", "ffn2_out"):
            w.update(weights(l, part, None))
        sm = _small_views(p, l)
        lng, lnb = p["ln_g"][l], p["ln_b"][l]
        s1, s2, s3 = saved[l]

        def ready(name, grad, l=l):
            tie = None if grads_ready is None else grads_ready(l, name, grad)
            if tie is not None:
                ties.append(tie)
            return tie

        dy, pre = (dx, None) if tail is None else _open_tail(tail)
        tail, closed, dgate3, dwi2, dwo2, dlg2, dlb2 = _ffn_bwd(dy, pre, s3, tied(mod[l]), w["ffn2_in"],
                                                                w["ffn2_out"], lng[2:3], lnb[2:3], (6, 7, 8), "ffn2",
                                                                ready)
        if closed is not None:
            rows_of[(l + 1, 0)], rows_of[(l + 1, 1)] = closed
        dy, pre = _open_tail(tail)
        tail, closed, dgate2, dwmi, dwmo, g, dlg1, dlb1 = _mixer_bwd(dy, pre, s2, tied(mod[l]), w, sm, lng[1:2],
                                                                     lnb[1:2], l, tabs, ready)
        rows_of[(l, 6)], rows_of[(l, 7)] = closed
        dy, pre = _open_tail(tail)
        tail, closed, dgate1, dwi1, dwo1, dlg0, dlb0 = _ffn_bwd(dy, pre, s1, tied(mod[l]), w["ffn1_in"],
                                                                w["ffn1_out"], lng[0:1], lnb[0:1], (0, 1, 2), "ffn1",
                                                                ready)
        rows_of[(l, 3)], rows_of[(l, 4)] = closed
        rows_of[(l, 2)], rows_of[(l, 5)], rows_of[(l, 8)] = dgate1, dgate2, dgate3
        big[l] = {"ffn1_in": dwi1, "ffn1_out": dwo1, "ffn2_in": dwi2, "ffn2_out": dwo2, "mix_in": dwmi,
                  "mix_out": dwmo}
        g["ln_g"] = jnp.concatenate([dlg0, dlg1, dlg2], axis=0)
        g["ln_b"] = jnp.concatenate([dlb0, dlb1, dlb2], axis=0)
        small[l] = g
    dh, x0, mod0, dx_res, sc_row = tail
    dx, rows_of[(0, 0)], rows_of[(0, 1)] = _modulate_bwd(dh, x0, mod0, dx_res, sc_row, "modulate_bwd_ffn1")
    dmods = [jnp.concatenate([rows_of[(l, r)] for r in range(N_MOD)], axis=1) for l in range(DEPTH)]
    return loss, dx, jnp.stack(dmods), big, small


_BIG = ("ffn1_in", "ffn1_out", "ffn2_in", "ffn2_out", "mix_in", "mix_out")


def _small_grad_list(small, loss):
    def both(fn):
        return jnp.stack([fn(small[l]) for l in range(DEPTH)])

    return [
        ("loss", loss.reshape(1)),
        ("ln_g", both(lambda g: g["ln_g"])), ("ln_b", both(lambda g: g["ln_b"])),
        ("hgrn_lb_logits", small[0]["lb_logits8"][:DEPTH] + small[1]["lb_logits8"][:DEPTH]),
        ("hgrn_norm_g", both(lambda g: g["hgrn_norm_g"][0])),
        ("mla_q_norm_g", both(lambda g: g["q_norm_g"][0])),
        ("mla_kv_norm_g", both(lambda g: g["kv_norm_g"][0])),
        ("fox_b_f", both(lambda g: g["fox_b_f"][0, :N_HEADS])),
        ("gmlp_ln_g", both(lambda g: g["gmlp_ln_g"][0])), ("gmlp_ln_b", both(lambda g: g["gmlp_ln_b"][0])),
        ("gmlp_w_s", both(lambda g: g["gmlp_w_s"])),
        ("gmlp_b_s", both(lambda g: g["gmlp_bst"][:, :N_HEADS].T)),
    ]


_PACK_COLS = 512


def _pack_small(items):
    flat = jnp.concatenate([a.reshape(-1).astype(F32) for _, a in items])
    n = flat.shape[0]
    tile = 8 * _PACK_COLS
    flat = jnp.pad(flat, (0, (-n) % tile))
    return flat.reshape(-1, _PACK_COLS)


def _unpack_small(buf, items):
    flat = buf.reshape(-1)
    out, off = {}, 0
    for name, a in items:
        out[name] = flat[off:off + a.size].reshape(a.shape)
        off += a.size
    return out


def _as2d(a):
    return a.reshape(-1, a.shape[-1])


def kernel(x, c, ada_w, ada_b, ln_g, ln_b, ffn1_w_in, ffn1_w_out, ffn2_w_in, ffn2_w_out, mix_w_in, mix_w_out, hgrn_lb_logits, hgrn_norm_g, mla_q_norm_g, mla_kv_norm_g, mla_w_uq, mla_w_ukv, fox_b_f, gmlp_ln_g, gmlp_ln_b, gmlp_w_s, gmlp_b_s, loss_target, m_ada_w, m_ada_b, m_ln_g, m_ln_b, m_ffn1_w_in, m_ffn1_w_out, m_ffn2_w_in, m_ffn2_w_out, m_mix_w_in, m_mix_w_out, m_hgrn_lb_logits, m_hgrn_norm_g, m_mla_q_norm_g, m_mla_kv_norm_g, m_mla_w_uq, m_mla_w_ukv, m_fox_b_f, m_gmlp_ln_g, m_gmlp_ln_b, m_gmlp_w_s, m_gmlp_b_s, v_ada_w, v_ada_b, v_ln_g, v_ln_b, v_ffn1_w_in, v_ffn1_w_out, v_ffn2_w_in, v_ffn2_w_out, v_mix_w_in, v_mix_w_out, v_hgrn_lb_logits, v_hgrn_norm_g, v_mla_q_norm_g, v_mla_kv_norm_g, v_mla_w_uq, v_mla_w_ukv, v_fox_b_f, v_gmlp_ln_g, v_gmlp_ln_b, v_gmlp_w_s, v_gmlp_b_s):
    names = ["ada_w", "ada_b", "ln_g", "ln_b", "ffn1_w_in", "ffn1_w_out", "ffn2_w_in", "ffn2_w_out", "mix_w_in",
             "mix_w_out", "hgrn_lb_logits", "hgrn_norm_g", "mla_q_norm_g", "mla_kv_norm_g", "mla_w_uq", "mla_w_ukv",
             "fox_b_f", "gmlp_ln_g", "gmlp_ln_b", "gmlp_w_s", "gmlp_b_s"]
    w = dict(zip(names, [ada_w, ada_b, ln_g, ln_b, ffn1_w_in, ffn1_w_out, ffn2_w_in, ffn2_w_out, mix_w_in, mix_w_out,
                         hgrn_lb_logits, hgrn_norm_g, mla_q_norm_g, mla_kv_norm_g, mla_w_uq, mla_w_ukv, fox_b_f,
                         gmlp_ln_g, gmlp_ln_b, gmlp_w_s, gmlp_b_s]))
    m = dict(zip(names, [m_ada_w, m_ada_b, m_ln_g, m_ln_b, m_ffn1_w_in, m_ffn1_w_out, m_ffn2_w_in, m_ffn2_w_out,
                         m_mix_w_in, m_mix_w_out, m_hgrn_lb_logits, m_hgrn_norm_g, m_mla_q_norm_g, m_mla_kv_norm_g,
                         m_mla_w_uq, m_mla_w_ukv, m_fox_b_f, m_gmlp_ln_g, m_gmlp_ln_b, m_gmlp_w_s, m_gmlp_b_s]))
    v = dict(zip(names, [v_ada_w, v_ada_b, v_ln_g, v_ln_b, v_ffn1_w_in, v_ffn1_w_out, v_ffn2_w_in, v_ffn2_w_out,
                         v_mix_w_in, v_mix_w_out, v_hgrn_lb_logits, v_hgrn_norm_g, v_mla_q_norm_g, v_mla_kv_norm_g,
                         v_mla_w_uq, v_mla_w_ukv, v_fox_b_f, v_gmlp_ln_g, v_gmlp_ln_b, v_gmlp_w_s, v_gmlp_b_s]))
    bsz, seq, d = x.shape
    me = 4 * lax.axis_index("x") + 2 * lax.axis_index("y") + lax.axis_index("c")
    mix_src, uq_src, ukv_src, mo_src = _mix_in_src(), _uq_src(), _ukv_src(), _mo_src()

    part_names = {"ffn1_in": ["ffn1_w_in"], "ffn1_out": ["ffn1_w_out"],
                  "mix": ["mix_w_in", "mix_w_out", "mla_w_uq", "mla_w_ukv"],
                  "ffn2_in": ["ffn2_w_in"], "ffn2_out": ["ffn2_w_out"]}
    group_of = {(l, part): (l, part) for l in range(DEPTH) for part in part_names}
    in_flight = {}
    transposed = ("ffn1_w_in", "ffn2_w_in")

    def start_group(key, behind=None):
        members = [(l, part) for (l, part), g in group_of.items() if g == key]
        labels = [(l, n) for l, part in members for n in part_names[part]]
        shards = []
        for l, n in labels:
            a = w[n][l]
            if n == "mix_w_in":
                a = _pack_cols(a, mix_src)
            if n in transposed:
                a = jnp.swapaxes(w[n], 1, 2)[l]
            shards.append(a.astype(BF16))
        if behind is not None:
            shards, _ = lax.optimization_barrier((shards, behind))
        in_flight[key] = (labels, _push_start(shards, f"gather_start_{key[0]}_{key[1]}", whole=True))

    keys_in_order = list(dict.fromkeys(group_of.values()))
    start_group(keys_in_order[0])

    gathered = _all_gather([c, ln_g, ln_b], "gather_inputs")
    c_all = gathered[0].reshape(N_DEV * bsz, d)
    ln_g_full = jnp.moveaxis(gathered[1], 0, 2).reshape(DEPTH, 3, d)
    ln_b_full = jnp.moveaxis(gathered[2], 0, 2).reshape(DEPTH, 3, d)

    mod_cols = _ada_fwd(c_all, ada_w, "ada_fwd")
    mod_all, = _all_gather([mod_cols], "gather_mod")
    mod_mine = lax.dynamic_slice_in_dim(mod_all, me * bsz, bsz, axis=2)
    mod = jnp.moveaxis(mod_mine, 0, 2).reshape(DEPTH, bsz, N_MOD * d) + ada_b[:, None, :]
    for key in keys_in_order[1:]:
        start_group(key, behind=mod)
    tie = sum(h[-1][0, 0] for _, h in in_flight.values())
    mod = mod.reshape(DEPTH, bsz, N_MOD, d) + tie

    arrived, laid_out = {}, {}

    def weights(l, part, after):
        if (l, part) not in laid_out:
            laid_out[(l, part)] = lay_out(l, part, after)
        return laid_out[(l, part)]

    def lay_out(l, part, after):
        key = group_of[(l, part)]
        if key not in arrived:
            labels, (send_sems, recv_sems, srcs, lands, _) = in_flight[key]
            _, lands = _push_wait(send_sems, recv_sems, srcs, lands, after, f"gather_wait_{key[0]}_{key[1]}",
                                  whole=True)
            arrived[key] = dict(zip(labels, lands))
        gw = {n: arrived[key][(l, n)] for n in part_names[part]}
        if part.endswith("_in"):
            return {part: gw[part_names[part][0]]}
        if part.endswith("_out"):
            return {part: gw[part_names[part][0]].reshape(4, 704, d)}
        uq = jnp.moveaxis(gw["mla_w_uq"], 0, 1).reshape(256, 384)
        ukv = jnp.moveaxis(gw["mla_w_ukv"], 0, 1).reshape(128, 512)
        return {"mix_in": gw["mix_w_in"].reshape(d, PACK_W),
                "mix_out": _pack_cols(gw["mix_w_out"].reshape(d, d).T, mo_src).T,
                "uq": _pack_cols(uq, uq_src), "ukv": _pack_cols(ukv, ukv_src)}

    p = dict(w)
    p["ln_g"], p["ln_b"] = ln_g_full, ln_b_full
    def chunks(name, arr):
        if name in ("ffn1_in", "ffn2_in"):
            return arr
        if name in ("ffn1_out", "ffn2_out"):
            return arr.reshape(N_DEV, arr.shape[1] // 2, d)
        if name == "mix_in":
            return _unpack_cols(arr, mix_src, MIX_ORIG_W).reshape(N_DEV, d // N_DEV, MIX_ORIG_W)
        if name in ("mla_uq", "mla_ukv"):
            full_w = _unpack_cols(arr, uq_src, 384) if name == "mla_uq" else _unpack_cols(arr, ukv_src, 512)
            rows = full_w.shape[0]
            return jnp.moveaxis(full_w.reshape(rows, N_DEV, -1), 1, 0).astype(BF16)
        return _unpack_cols(arr.T, mo_src, d).T.astype(BF16).reshape(N_DEV, d // N_DEV, d)

    pending, started = {}, []

    def grads_ready(l, name, grad):
        pending[(name, l)] = chunks(name, grad)
        flush = name == "ffn1_in" if l > 0 else name in ("ffn2_in", "mix_out", "mix_in", "ffn1_out", "ffn1_in")
        if not flush:
            return None
        keys = sorted(pending)
        handles = _push_start([pending[k] for k in keys], f"push_start_{len(started)}")
        pending.clear()
        started.append((keys, handles, l == 0 and name.startswith("ffn1")))
        return handles[-1][0, 0]

    loss, grad_x, dmod, big, small = _local_step(x, mod, loss_target, weights, p, grads_ready)
    del big

    recv, out = {}, {}

    def arrive(n, after):
        keys, (send_sems, recv_sems, srcs, lands, _), _ = started[n]
        srcs, lands = _push_wait(send_sems, recv_sems, srcs, lands, after, f"push_wait_{n}")
        for k, src, land in zip(keys, srcs, lands):
            recv[k] = (land, src)

    big_of = {"ffn1_w_in": "ffn1_in", "ffn1_w_out": "ffn1_out", "ffn2_w_in": "ffn2_in", "ffn2_w_out": "ffn2_out",
              "mix_w_in": "mix_in", "mix_w_out": "mix_out", "mla_w_uq": "mla_uq", "mla_w_ukv": "mla_ukv"}
    chain = {name: None for name in big_of}

    def big_update(key, l):
        name = next(nm for nm, k in big_of.items() if k == key)
        parts, src = recv[(key, l)]
        view =(lambda a: jnp.swapaxes(a, 1, 2)) if name in transposed else (lambda a: a)
        chain[name] = _adamw(parts, (src, me), view(w[name]), view(m[name]), view(v[name]), f"adamw_{name}_l{l}",
                             layer=l, prev=chain[name])

    def update(name, grad):
        shape = w[name].shape
        as3 = lambda a: a.reshape(1, -1, shape[-1])
        res = _adamw(as3(grad), None, as3(w[name]), as3(m[name]), as3(v[name]), f"adamw_{name}")
        out[name] = tuple(r.reshape(shape) for r in res)

    for n, (keys, _, last) in enumerate(started):
        if not last:
            arrive(n, grad_x)
            for key, l in keys:
                big_update(key, l)

    dmod_flat = dmod.reshape(DEPTH, bsz, N_MOD * d)
    done = [r[0] for r in chain.values() if r is not None]
    if done:
        dmod_flat, _ = lax.optimization_barrier((dmod_flat, done))
    dmod_all, = _all_gather([dmod_flat], "gather_dmod")
    dmod_full = jnp.moveaxis(dmod_all, 0, 1).reshape(DEPTH, N_DEV * bsz, N_MOD * d)
    cols = ada_w.shape[2]
    dmod_cols = lax.dynamic_slice_in_dim(dmod_full, me * cols, cols, axis=2)
    g_ada_w, g_ada_b = _ada_bwd(c_all, dmod_cols, dmod_full, "ada_bwd")
    res = None
    for l in range(DEPTH):
        res = _adamw(g_ada_w[l][None], None, ada_w, m_ada_w, v_ada_w, f"adamw_ada_w_l{l}", layer=l, prev=res)
    out["ada_w"] = tuple(res)
    update("ada_b", g_ada_b.reshape(DEPTH, N_MOD * d))

    items = _small_grad_list(small, loss)
    packed, _ = lax.optimization_barrier((_pack_small(items), (grad_x, g_ada_b)))
    parts, = _all_gather([packed], "gather_small")
    sg = _unpack_small(_sum_parts(parts, "sum_small"), items)
    for name in ("ln_g", "ln_b"):
        update(name, lax.dynamic_slice_in_dim(sg[name], me * (d // N_DEV), d // N_DEV, axis=2))
    for name in ("hgrn_lb_logits", "hgrn_norm_g", "mla_q_norm_g", "mla_kv_norm_g", "fox_b_f", "gmlp_ln_g",
                 "gmlp_ln_b", "gmlp_w_s", "gmlp_b_s"):
        update(name, sg[name])

    for n, (keys, _, last) in enumerate(started):
        if last:
            arrive(n, out["gmlp_w_s"][0])
            for key, l in keys:
                big_update(key, l)
    for name in big_of:
        out[name] = tuple(jnp.swapaxes(r, 1, 2) if name in transposed else r for r in chain[name])

    return (sg["loss"][0], grad_x, *[out[n][0] for n in names], *[out[n][1] for n in names],
            *[out[n][2] for n in names], *[out[n][3] for n in names])
```

```python
import functools

import numpy as np
import jax
import jax.numpy as jnp
from jax import lax
from jax.experimental import pallas as pl
from jax.experimental.pallas import tpu as pltpu

F32 = jnp.float32
BF16 = jnp.bfloat16
HI = lax.Precision.HIGHEST

D_MODEL = 1024
DEPTH = 2
GROUP_WIDTH = 256
N_HEADS = 4
HEAD_DIM = 64
A_CHUNK = 16
LB_FLOOR = 1e-30
B_NOPE = 64
B_ROPE = 32
ROPE_THETA = 10000.0
D_CHUNK = 128
D_FF = 2816
N_MOD = 9
ALPHA = (2 * DEPTH) ** 0.25
LN_EPS = 1e-5
RMS_EPS = 1e-6
ADAM_LR = 0.001
ADAM_B1 = 0.9
ADAM_B2 = 0.999
ADAM_EPS = 1e-08
ADAM_WD = 0.01
ADAM_STEP = 10

N_DEV = 8
LANES = 128
PACK_W = 3712
MO_W = 1536
VMEM_LIMIT = 56 * 1024 * 1024
NEG = -1e30
ATTN_TILE = 1024

MIX_ORIG_W = 2724
O_BCQ, O_BCKV, O_BKR, O_CQ, O_CK, O_CV, O_CF, O_DU, O_DV = 1024, 1280, 1408, 1440, 1696, 1952, 2208, 2212, 2468
P_B, P_KR, P_CQ, P_CKV, P_D, P_CF = 1024, 1408, 1536, 2048, 3072, 3584


_DN = {"nn": (((1,), (0,)), ((), ())), "nt": (((1,), (1,)), ((), ())), "tn": (((0,), (0,)), ((), ()))}


def _raw_bdot(a, b, mode):
    return lax.dot_general(a.astype(BF16), b.astype(BF16), _DN[mode], preferred_element_type=F32)


@functools.partial(jax.custom_vjp, nondiff_argnums=(2,))
def _bdot(a, b, mode):
    return _raw_bdot(a, b, mode)


def _bdot_fwd(a, b, mode):
    return _raw_bdot(a, b, mode), (a, b)


def _bdot_bwd(mode, res, g):
    a, b = res
    if mode == "nn":
        return _raw_bdot(g, b, "nt"), _raw_bdot(a, g, "tn")
    if mode == "nt":
        return _raw_bdot(g, b, "nn"), _raw_bdot(g, a, "tn")
    return _raw_bdot(b, g, "nt"), _raw_bdot(a, g, "nn")


_bdot.defvjp(_bdot_fwd, _bdot_bwd)


def _cparams(sem):
    return pltpu.CompilerParams(dimension_semantics=sem, vmem_limit_bytes=VMEM_LIMIT)


def _mix_in_src():
    src = -np.ones(PACK_W, np.int64)
    src[0:P_KR] = np.arange(0, O_BKR)
    src[P_KR + 64:P_KR + 80] = O_BKR + np.arange(16)
    src[P_KR + 96:P_KR + 112] = O_BKR + 16 + np.arange(16)
    for h in range(N_HEADS):
        src[P_CQ + 128 * h:P_CQ + 128 * h + 64] = O_CQ + 64 * h + np.arange(64)
        src[P_CKV + 256 * h:P_CKV + 256 * h + 64] = O_CK + 64 * h + np.arange(64)
        src[P_CKV + 256 * h + 128:P_CKV + 256 * h + 192] = O_CV + 64 * h + np.arange(64)
    src[P_D:P_D + 512] = O_DU + np.arange(512)
    src[P_CF:P_CF + 4] = O_CF + np.arange(4)
    return src


def _uq_src():
    src = -np.ones(512, np.int64)
    for h in range(N_HEADS):
        src[128 * h:128 * h + 64] = 96 * h + np.arange(64)
        src[128 * h + 64:128 * h + 80] = 96 * h + 64 + np.arange(16)
        src[128 * h + 96:128 * h + 112] = 96 * h + 80 + np.arange(16)
    return src


def _ukv_src():
    src = -np.ones(1024, np.int64)
    for h in range(N_HEADS):
        src[256 * h:256 * h + 64] = 128 * h + np.arange(64)
        src[256 * h + 128:256 * h + 192] = 128 * h + 64 + np.arange(64)
    return src


def _mo_src():
    src = -np.ones(MO_W, np.int64)
    src[0:256] = np.arange(256)
    for g in range(2):
        for h in range(N_HEADS):
            src[256 + 512 * g + 128 * h:256 + 512 * g + 128 * h + 64] = 256 + 256 * g + 64 * h + np.arange(64)
    src[1280:1536] = 768 + np.arange(256)
    return src


def _runs(idx):
    runs, i = [], 0
    while i < len(idx):
        j = i + 1
        while j < len(idx) and ((idx[i] < 0 and idx[j] < 0) or (idx[i] >= 0 and idx[j] == idx[i] + j - i)):
            j += 1
        runs.append((int(idx[i]), j - i))
        i = j
    return runs


def _take_runs(w, idx):
    parts = [jnp.zeros(w.shape[:-1] + (n,), w.dtype) if s < 0 else lax.slice_in_dim(w, s, s + n, axis=w.ndim - 1)
             for s, n in _runs(idx)]
    return jnp.concatenate(parts, axis=-1)


def _pack_cols(w, src):
    return _take_runs(w, src)


def _unpack_cols(wp, src, n):
    dst = np.zeros(n, np.int64)
    dst[src[src >= 0]] = np.nonzero(src >= 0)[0]
    return _take_runs(wp, dst)


def _rope_tables(seq):
    half = B_ROPE // 2
    inv_freq = ROPE_THETA ** (-jnp.arange(half, dtype=F32) / half)
    ang = jnp.arange(seq).astype(F32)[:, None] * inv_freq[None, :]
    cos, sin = jnp.cos(ang), jnp.sin(ang)
    z16 = jnp.zeros((seq, 16), F32)
    c = jnp.concatenate([jnp.ones((seq, 64), F32), cos, z16, cos, z16], axis=1)
    s1 = jnp.concatenate([jnp.zeros((seq, 64), F32), -sin, z16, z16, z16], axis=1)
    s2 = jnp.concatenate([jnp.zeros((seq, 64), F32), z16, z16, sin, z16], axis=1)
    return c, s1, s2


def _matmul(a, b, *, mode, group_out, out_dtype, tm, tk, name):
    ga, gb = a.shape[0], b.shape[0]
    g_n = max(ga, gb)
    if mode == "tn":
        k_dim, m_dim = a.shape[1:]
    else:
        m_dim, k_dim = a.shape[1:]
    n_dim = b.shape[1] if mode == "nt" else b.shape[2]
    assert m_dim % tm == 0 and k_dim % tk == 0
    kt = k_dim // tk
    n_red = kt if group_out else g_n * kt
    g_out = g_n if group_out else 1

    def split(g, r):
        return (g, r) if group_out else (r // kt, r % kt)

    def a_map(g, i, r):
        gg, kk = split(g, r)
        gg = gg if ga > 1 else 0
        return (gg, kk, i) if mode == "tn" else (gg, i, kk)

    def b_map(g, i, r):
        gg, kk = split(g, r)
        gg = gg if gb > 1 else 0
        return (gg, 0, kk) if mode == "nt" else (gg, kk, 0)

    a_blk = (None, tk, tm) if mode == "tn" else (None, tm, tk)
    b_blk = (None, n_dim, tk) if mode == "nt" else (None, tk, n_dim)
    dn = _DN[mode]

    def body(a_ref, b_ref, o_ref, *scratch):
        part = lax.dot_general(a_ref[...].astype(BF16), b_ref[...].astype(BF16), dn, preferred_element_type=F32)
        if n_red == 1:
            o_ref[...] = part.astype(o_ref.dtype)
            return
        acc_ref, = scratch
        r = pl.program_id(2)

        @pl.when(r == 0)
        def _():
            acc_ref[...] = part

        @pl.when(r > 0)
        def _():
            acc_ref[...] += part

        @pl.when(r == n_red - 1)
        def _():
            o_ref[...] = acc_ref[...].astype(o_ref.dtype)

    return pl.pallas_call(
        body, name=name, grid=(g_out, m_dim // tm, n_red),
        in_specs=[pl.BlockSpec(a_blk, a_map), pl.BlockSpec(b_blk, b_map)],
        out_specs=pl.BlockSpec((None, tm, n_dim), lambda g, i, r: (g, i, 0)),
        out_shape=jax.ShapeDtypeStruct((g_out, m_dim, n_dim), out_dtype),
        scratch_shapes=[] if n_red == 1 else [pltpu.VMEM((tm, n_dim), F32)],
        compiler_params=_cparams(("parallel", "parallel", "arbitrary")),
    )(a, b)


def _matmul_groupsum(a, b, *, out_dtype, tm, name):
    g_n, m_dim, k_dim = a.shape
    n_dim = b.shape[2]
    assert m_dim % tm == 0 and b.shape[:2] == (g_n, k_dim)

    def body(a_ref, b_ref, o_ref):
        acc = jnp.dot(a_ref[0], b_ref[0], preferred_element_type=F32)
        for g in range(1, g_n):
            acc = acc + jnp.dot(a_ref[g], b_ref[g], preferred_element_type=F32)
        o_ref[...] = acc.astype(o_ref.dtype)

    return pl.pallas_call(
        body, name=name, grid=(m_dim // tm,),
        in_specs=[pl.BlockSpec((g_n, tm, k_dim), lambda i: (0, i, 0)),
                  pl.BlockSpec((g_n, k_dim, n_dim), lambda i: (0, 0, 0))],
        out_specs=pl.BlockSpec((tm, n_dim), lambda i: (i, 0)),
        out_shape=jax.ShapeDtypeStruct((m_dim, n_dim), out_dtype),
        compiler_params=_cparams(("parallel",)),
    )(a, b)


def _row_spec(ts, d):
    return pl.BlockSpec((None, ts, d), lambda b, s: (b, s, 0))


def _mod_spec(d):
    return pl.BlockSpec((None, N_MOD, d), lambda b, s: (b, 0, 0))


def _vec_spec(d):
    return pl.BlockSpec((1, d), lambda b, s: (0, 0))


def _bvec_spec(d):
    return pl.BlockSpec((None, 1, d), lambda b, s: (b, 0, 0))


def _modulate(x, mod, sh_row, sc_row, name, ts=512):
    bsz, seq, d = x.shape

    def body(x_ref, mod_ref, o_ref):
        sh = mod_ref[sh_row:sh_row + 1, :]
        sc = mod_ref[sc_row:sc_row + 1, :]
        o_ref[...] = (x_ref[...] * (1.0 + sc) + sh).astype(o_ref.dtype)

    return pl.pallas_call(
        body, name=name, grid=(bsz, seq // ts),
        in_specs=[_row_spec(ts, d), _mod_spec(d)], out_specs=_row_spec(ts, d),
        out_shape=jax.ShapeDtypeStruct((bsz, seq, d), BF16),
        compiler_params=_cparams(("parallel", "parallel")),
    )(x, mod)


def _modulate_bwd(dh, x, mod, dx_res, sc_row, name, ts=512):
    bsz, seq, d = x.shape

    def body(dh_ref, x_ref, mod_ref, dxr_ref, dx_ref, dsh_ref, dsc_ref):
        s = pl.program_id(1)
        sc = mod_ref[sc_row:sc_row + 1, :]
        dh_v = dh_ref[...]
        dx_ref[...] = dxr_ref[...] + dh_v * (1.0 + sc)
        psh = jnp.sum(dh_v, axis=0, keepdims=True)
        psc = jnp.sum(dh_v * x_ref[...], axis=0, keepdims=True)

        @pl.when(s == 0)
        def _():
            dsh_ref[...] = psh
            dsc_ref[...] = psc

        @pl.when(s > 0)
        def _():
            dsh_ref[...] += psh
            dsc_ref[...] += psc

    return pl.pallas_call(
        body, name=name, grid=(bsz, seq // ts),
        in_specs=[_row_spec(ts, d), _row_spec(ts, d), _mod_spec(d), _row_spec(ts, d)],
        out_specs=[_row_spec(ts, d), _bvec_spec(d), _bvec_spec(d)],
        out_shape=[jax.ShapeDtypeStruct((bsz, seq, d), F32), jax.ShapeDtypeStruct((bsz, 1, d), F32),
                   jax.ShapeDtypeStruct((bsz, 1, d), F32)],
        compiler_params=_cparams(("parallel", "arbitrary")),
    )(dh, x, mod, dx_res)


def _res_ln_fn(x, f, g, lng, lnb, cmul):
    r = ALPHA * x + (cmul * (1.0 + g)) * f
    mu = jnp.mean(r, axis=-1, keepdims=True)
    rc = r - mu
    var = jnp.mean(rc * rc, axis=-1, keepdims=True)
    return rc * lax.rsqrt(var + LN_EPS) * lng + lnb


def _res_ln(x, f, mod, lng, lnb, g_row, cmul, name, nxt=None, ts=512):
    bsz, seq, d = x.shape

    def body(*refs):
        x_ref, f_ref, mod_ref, lng_ref, lnb_ref = refs[:5]
        g = mod_ref[g_row:g_row + 1, :]
        y = _res_ln_fn(x_ref[...], f_ref[...], g, lng_ref[...], lnb_ref[...], cmul)
        if nxt is None:
            refs[5][...] = y
            return
        nmod_ref, o_ref, h_ref = refs[5:]
        o_ref[...] = y
        sh = nmod_ref[nxt[1]:nxt[1] + 1, :]
        sc = nmod_ref[nxt[2]:nxt[2] + 1, :]
        h_ref[...] = (y * (1.0 + sc) + sh).astype(h_ref.dtype)

    in_specs = [_row_spec(ts, d), _row_spec(ts, d), _mod_spec(d), _vec_spec(d), _vec_spec(d)]
    args = [x, f, mod, lng, lnb]
    out_specs, out_shape = [_row_spec(ts, d)], [jax.ShapeDtypeStruct((bsz, seq, d), F32)]
    if nxt is not None:
        in_specs.append(_mod_spec(d))
        args.append(nxt[0])
        out_specs.append(_row_spec(ts, d))
        out_shape.append(jax.ShapeDtypeStruct((bsz, seq, d), BF16))
    res = pl.pallas_call(
        body, name=name, grid=(bsz, seq // ts), in_specs=in_specs, out_specs=out_specs, out_shape=out_shape,
        compiler_params=_cparams(("parallel", "parallel")),
    )(*args)
    return (res[0], res[1]) if nxt is not None else (res[0], None)


def _res_ln_bwd(dy, x, f, mod, lng, lnb, g_row, cmul, name, pre=None, ts=256):
    bsz, seq, d = x.shape
    fused = pre is not None

    def body(*refs):
        dy_ref, x_ref, f_ref, mod_ref, lng_ref, lnb_ref = refs[:6]
        n_in = 8 if fused else 6
        dx_ref, df_ref, dg_ref, dlg_ref, dlb_ref = refs[n_in:n_in + 5]
        b, s = pl.program_id(0), pl.program_id(1)
        g = mod_ref[g_row:g_row + 1, :]
        y, vjp = jax.vjp(functools.partial(_res_ln_fn, cmul=cmul), x_ref[...], f_ref[...], g, lng_ref[...],
                         lnb_ref[...])
        ct = dy_ref[...]
        if fused:
            dh_ref, nmod_ref = refs[6:8]
            dsh_ref, dsc_ref = refs[n_in + 5:]
            dh_v = dh_ref[...]
            ct = ct + dh_v * (1.0 + nmod_ref[pre[3]:pre[3] + 1, :])
            psh = jnp.sum(dh_v, axis=0, keepdims=True)
            psc = jnp.sum(dh_v * y, axis=0, keepdims=True)
        dx, df, dg, dlg, dlb = vjp(ct)
        dx_ref[...] = dx
        df_ref[...] = df.astype(df_ref.dtype)

        @pl.when(s == 0)
        def _():
            dg_ref[...] = dg
            if fused:
                dsh_ref[...] = psh
                dsc_ref[...] = psc

        @pl.when(s > 0)
        def _():
            dg_ref[...] += dg
            if fused:
                dsh_ref[...] += psh
                dsc_ref[...] += psc

        first = jnp.logical_and(b == 0, s == 0)

        @pl.when(first)
        def _():
            dlg_ref[...] = dlg
            dlb_ref[...] = dlb

        @pl.when(jnp.logical_not(first))
        def _():
            dlg_ref[...] += dlg
            dlb_ref[...] += dlb

    in_specs = [_row_spec(ts, d), _row_spec(ts, d), _row_spec(ts, d), _mod_spec(d), _vec_spec(d), _vec_spec(d)]
    args = [dy, x, f, mod, lng, lnb]
    out_specs = [_row_spec(ts, d), _row_spec(ts, d), _bvec_spec(d), _vec_spec(d), _vec_spec(d)]
    bvec = jax.ShapeDtypeStruct((bsz, 1, d), F32)
    out_shape = [jax.ShapeDtypeStruct((bsz, seq, d), F32), jax.ShapeDtypeStruct((bsz, seq, d), BF16), bvec,
                 jax.ShapeDtypeStruct((1, d), F32), jax.ShapeDtypeStruct((1, d), F32)]
    if fused:
        in_specs += [_row_spec(ts, d), _mod_spec(d)]
        args += [pre[0], pre[2]]
        out_specs += [_bvec_spec(d), _bvec_spec(d)]
        out_shape += [bvec, bvec]
    res = pl.pallas_call(
        body, name=name, grid=(bsz, seq // ts), in_specs=in_specs, out_specs=out_specs, out_shape=out_shape,
        compiler_params=_cparams(("arbitrary", "arbitrary")),
    )(*args)
    return tuple(res[:5]), (tuple(res[5:]) if fused else None)


def _loss_head(y, target, name, ts=512):
    bsz, seq, d = y.shape
    n_s = seq // ts

    def body(y_ref, t_ref, dy_ref, loss_ref, acc_ref):
        b, s = pl.program_id(0), pl.program_id(1)
        err = y_ref[...] - t_ref[...]
        dy_ref[...] = err * (1.0 / d)
        part = jnp.sum(err * err, axis=0, keepdims=True)
        first = jnp.logical_and(b == 0, s == 0)

        @pl.when(first)
        def _():
            acc_ref[...] = part

        @pl.when(jnp.logical_not(first))
        def _():
            acc_ref[...] += part

        @pl.when(jnp.logical_and(b == bsz - 1, s == n_s - 1))
        def _():
            loss_ref[...] = jnp.sum(acc_ref[...], axis=1, keepdims=True) * (0.5 / d)

    return pl.pallas_call(
        body, name=name, grid=(bsz, n_s),
        in_specs=[_row_spec(ts, d), _row_spec(ts, d)],
        out_specs=[_row_spec(ts, d), pl.BlockSpec((1, 1), lambda b, s: (0, 0))],
        out_shape=[jax.ShapeDtypeStruct((bsz, seq, d), F32), jax.ShapeDtypeStruct((1, 1), F32)],
        scratch_shapes=[pltpu.VMEM((1, d), F32)],
        compiler_params=_cparams(("arbitrary", "arbitrary")),
    )(y, target)


def _ffn_in_swiglu(h, w_in_t, name, tm=1024):
    t, d = h.shape
    n_sh, w, _ = w_in_t.shape
    half = n_sh // 2

    def body(h_ref, w_ref, z_ref, a_ref):
        hv = h_ref[...]
        g = lax.dot_general(hv, w_ref[0], _DN["nt"], preferred_element_type=F32)
        u = lax.dot_general(hv, w_ref[1], _DN["nt"], preferred_element_type=F32)
        z_ref[0] = g.astype(z_ref.dtype)
        z_ref[1] = u.astype(z_ref.dtype)
        a_ref[...] = (g * jax.nn.sigmoid(g) * u).astype(a_ref.dtype)

    return pl.pallas_call(
        body, name=name, grid=(half, t // tm),
        in_specs=[pl.BlockSpec((tm, d), lambda g, i: (i, 0)),
                  pl.BlockSpec((2, None, w, d), lambda g, i: (0, g, 0, 0))],
        out_specs=[pl.BlockSpec((2, None, tm, w), lambda g, i: (0, g, i, 0)),
                   pl.BlockSpec((None, tm, w), lambda g, i: (g, i, 0))],
        out_shape=[jax.ShapeDtypeStruct((2, half, t, w), BF16), jax.ShapeDtypeStruct((half, t, w), BF16)],
        compiler_params=_cparams(("parallel", "parallel")),
    )(h, w_in_t.reshape(2, half, w, d))


def _ffn_out_dx_swiglu(df, w_out, z, name, tm=1024):
    t, d = df.shape
    half, w, _ = w_out.shape

    def body(df_ref, w_ref, z_ref, dz_ref):
        da = lax.dot_general(df_ref[...], w_ref[...], _DN["nt"], preferred_element_type=F32)
        g = z_ref[0].astype(F32)
        u = z_ref[1].astype(F32)
        sig = jax.nn.sigmoid(g)
        dz_ref[0] = (da * u * (sig * (1.0 + g * (1.0 - sig)))).astype(dz_ref.dtype)
        dz_ref[1] = (da * (g * sig)).astype(dz_ref.dtype)

    zspec = pl.BlockSpec((2, None, tm, w), lambda g, i: (0, g, i, 0))
    return pl.pallas_call(
        body, name=name, grid=(half, t // tm),
        in_specs=[pl.BlockSpec((tm, d), lambda g, i: (i, 0)), pl.BlockSpec((None, w, d), lambda g, i: (g, 0, 0)),
                  zspec],
        out_specs=zspec, out_shape=jax.ShapeDtypeStruct(z.shape, BF16),
        compiler_params=_cparams(("parallel", "parallel")),
    )(df, w_out, z)


def _log_sigmoid(x):
    return jnp.minimum(x, 0.0) - jnp.log(1.0 + jnp.exp(-jnp.abs(x)))


def _hgrn_consts():
    r = lax.broadcasted_iota(jnp.int32, (GROUP_WIDTH, GROUP_WIDTH), 0)
    c = lax.broadcasted_iota(jnp.int32, (GROUP_WIDTH, GROUP_WIDTH), 1)
    bd = (r // HEAD_DIM == c // HEAD_DIM).astype(F32)
    r16 = lax.broadcasted_iota(jnp.int32, (A_CHUNK, A_CHUNK), 0)
    c16 = lax.broadcasted_iota(jnp.int32, (A_CHUNK, A_CHUNK), 1)
    tril = (r16 >= c16).astype(F32)
    rows = lax.broadcasted_iota(jnp.int32, (A_CHUNK, GROUP_WIDTH), 0)
    return bd, tril, rows


def _hgrn_lb(logits8, layer):
    rows = lax.broadcasted_iota(jnp.int32, logits8.shape, 0)
    valid = rows < DEPTH
    mx = jnp.max(jnp.where(valid, logits8, NEG), axis=0, keepdims=True)
    e = jnp.where(valid, jnp.exp(logits8 - mx), 0.0)
    sm = e / jnp.sum(e, axis=0, keepdims=True)
    pick = jnp.logical_and(rows >= 1, rows <= layer)
    return jnp.sum(jnp.where(pick, sm, 0.0), axis=0, keepdims=True)


def _hgrn_chunk(aq, af, ai, ag, logits8, norm_g, st, *, layer, consts):
    bd, tril, rows = consts
    lb = _hgrn_lb(logits8, layer)
    la = jnp.log(jnp.maximum(lb, LB_FLOOR))
    b2 = jnp.log(1.0 - lb) + _log_sigmoid(af)
    log_f = jnp.maximum(la, b2) + jnp.log(1.0 + jnp.exp(-jnp.abs(la - b2)))
    k = 1.0 - jnp.exp(log_f)
    qf = aq * jax.nn.sigmoid(aq)
    g_cum = jnp.dot(tril, log_f, precision=HI, preferred_element_type=F32)

    c, w = A_CHUNK, GROUP_WIDTH

    def by_key(v):
        return jnp.broadcast_to(v[:, None, :], (c, c, w))

    def by_query(v):
        return jnp.broadcast_to(v[None, :, :], (c, c, w))

    s_i = lax.broadcasted_iota(jnp.int32, (c, c, w), 0)
    t_i = lax.broadcasted_iota(jnp.int32, (c, c, w), 1)
    rel = jnp.where(t_i >= s_i, by_query(g_cum) - by_key(g_cum), NEG)
    pairs = by_query(qf) * by_key(k) * jnp.exp(rel)
    a_all = _bdot(pairs.reshape(c * c, w), bd, "nn").reshape(c, c, w)
    o = jnp.sum(a_all * by_key(ai), axis=0)
    q_dec = qf * jnp.exp(g_cum)
    o = o + _bdot(q_dec, st, "nt")
    g_last = jnp.sum(jnp.where(rows == c - 1, g_cum, 0.0), axis=0, keepdims=True)
    k_end = k * jnp.exp(g_last - g_cum)
    kv = _bdot(ai, k_end, "tn")
    st_new = st * jnp.exp(g_last) + kv * bd
    ms = _bdot(o * o, bd, "nn") * (1.0 / HEAD_DIM)
    o = o * lax.rsqrt(ms + RMS_EPS) * norm_g
    return o * (ag * jax.nn.sigmoid(ag)), st_new


def _hgrn_fwd(proj, logits8, norm_g, layer, name, ts=256):
    bsz, seq, _ = proj.shape
    n_ch = ts // A_CHUNK

    def body(p_ref, lg_ref, ng_ref, o_ref, st_ref, st_scr):
        @pl.when(pl.program_id(1) == 0)
        def _():
            st_scr[...] = jnp.zeros_like(st_scr)

        consts = _hgrn_consts()
        logits_v, ng_v = lg_ref[...], ng_ref[...]

        def chunk(ci, carry):
            r = ci * A_CHUNK if isinstance(ci, int) else pl.multiple_of(ci * A_CHUNK, A_CHUNK)
            st = st_scr[...]
            st_ref[ci] = st
            o, st_new = _hgrn_chunk(
                p_ref[pl.ds(r, A_CHUNK), 0:256], p_ref[pl.ds(r, A_CHUNK), 256:512],
                p_ref[pl.ds(r, A_CHUNK), 512:768], p_ref[pl.ds(r, A_CHUNK), 768:1024],
                logits_v, ng_v, st, layer=layer, consts=consts)
            o_ref[pl.ds(r, A_CHUNK), :] = o.astype(o_ref.dtype)
            st_scr[...] = st_new
            return carry

        if n_ch <= 2:
            for c_static in range(n_ch):
                chunk(c_static, 0)
        else:
            lax.fori_loop(0, n_ch, chunk, 0, unroll=2)

    return pl.pallas_call(
        body, name=name, grid=(bsz, seq // ts),
        in_specs=[pl.BlockSpec((None, ts, 1024), lambda b, s: (b, s, 0)),
                  pl.BlockSpec((8, GROUP_WIDTH), lambda b, s: (0, 0)),
                  pl.BlockSpec((1, GROUP_WIDTH), lambda b, s: (0, 0))],
        out_specs=[pl.BlockSpec((None, ts, GROUP_WIDTH), lambda b, s: (b, s, 0)),
                   pl.BlockSpec((None, n_ch, GROUP_WIDTH, GROUP_WIDTH), lambda b, s: (b, s, 0, 0))],
        out_shape=[jax.ShapeDtypeStruct((bsz, seq, MO_W), BF16),
                   jax.ShapeDtypeStruct((bsz, seq // A_CHUNK, GROUP_WIDTH, GROUP_WIDTH), F32)],
        scratch_shapes=[pltpu.VMEM((GROUP_WIDTH, GROUP_WIDTH), F32)],
        compiler_params=_cparams(("parallel", "arbitrary")),
    )(proj, logits8, norm_g)


def _hgrn_bwd(dmo, proj, states, logits8, norm_g, layer, name, ts=256):
    bsz, seq, _ = proj.shape
    n_ch = ts // A_CHUNK
    n_s = seq // ts

    def body(do_ref, p_ref, st_ref, lg_ref, ng_ref, dp_ref, dlg_ref, dng_ref, dst_scr):
        b, s = pl.program_id(0), pl.program_id(1)

        @pl.when(s == 0)
        def _():
            dst_scr[...] = jnp.zeros_like(dst_scr)

        @pl.when(jnp.logical_and(b == 0, s == 0))
        def _():
            dlg_ref[...] = jnp.zeros_like(dlg_ref)
            dng_ref[...] = jnp.zeros_like(dng_ref)

        consts = _hgrn_consts()
        logits_v, ng_v = lg_ref[...], ng_ref[...]
        fn = functools.partial(_hgrn_chunk, layer=layer, consts=consts)

        def chunk(t, carry):
            ci = n_ch - 1 - t
            r = ci * A_CHUNK if isinstance(ci, int) else pl.multiple_of(ci * A_CHUNK, A_CHUNK)
            _, vjp = jax.vjp(
                fn, p_ref[pl.ds(r, A_CHUNK), 0:256], p_ref[pl.ds(r, A_CHUNK), 256:512],
                p_ref[pl.ds(r, A_CHUNK), 512:768], p_ref[pl.ds(r, A_CHUNK), 768:1024],
                logits_v, ng_v, st_ref[ci])
            daq, daf, dai, dag, dlg, dng, dst = vjp((do_ref[pl.ds(r, A_CHUNK), :], dst_scr[...]))
            dp_ref[pl.ds(r, A_CHUNK), 0:256] = daq.astype(dp_ref.dtype)
            dp_ref[pl.ds(r, A_CHUNK), 256:512] = daf.astype(dp_ref.dtype)
            dp_ref[pl.ds(r, A_CHUNK), 512:768] = dai.astype(dp_ref.dtype)
            dp_ref[pl.ds(r, A_CHUNK), 768:1024] = dag.astype(dp_ref.dtype)
            dlg_ref[...] += dlg
            dng_ref[...] += dng
            dst_scr[...] = dst
            return carry

        if n_ch <= 2:
            for c_static in range(n_ch):
                chunk(c_static, 0)
        else:
            lax.fori_loop(0, n_ch, chunk, 0, unroll=2)

    rev = lambda b, s: (b, n_s - 1 - s, 0)
    return pl.pallas_call(
        body, name=name, grid=(bsz, n_s),
        in_specs=[pl.BlockSpec((None, ts, GROUP_WIDTH), rev),
                  pl.BlockSpec((None, ts, 1024), rev),
                  pl.BlockSpec((None, n_ch, GROUP_WIDTH, GROUP_WIDTH), lambda b, s: (b, n_s - 1 - s, 0, 0)),
                  pl.BlockSpec((8, GROUP_WIDTH), lambda b, s: (0, 0)),
                  pl.BlockSpec((1, GROUP_WIDTH), lambda b, s: (0, 0))],
        out_specs=[pl.BlockSpec((None, ts, 1024), rev),
                   pl.BlockSpec((8, GROUP_WIDTH), lambda b, s: (0, 0)),
                   pl.BlockSpec((1, GROUP_WIDTH), lambda b, s: (0, 0))],
        out_shape=[jax.ShapeDtypeStruct((bsz, seq, PACK_W), BF16),
                   jax.ShapeDtypeStruct((8, GROUP_WIDTH), F32), jax.ShapeDtypeStruct((1, GROUP_WIDTH), F32)],
        scratch_shapes=[pltpu.VMEM((GROUP_WIDTH, GROUP_WIDTH), F32)],
        compiler_params=_cparams(("arbitrary", "arbitrary")),
    )(dmo, proj, states, logits8, norm_g)


def _rms_fn(x, g):
    return x * lax.rsqrt(jnp.mean(x * x, axis=-1, keepdims=True) + RMS_EPS) * g


def _tile4(t):
    return jnp.concatenate([t, t, t, t], axis=1)


def _rope(x, c, s1, s2):
    w = x.shape[-1]
    return x * c + pltpu.roll(x, 32, axis=1) * s2 + pltpu.roll(x, w - 32, axis=1) * s1


def _rope_t(dy, c, s1, s2):
    w = dy.shape[-1]
    return dy * c + pltpu.roll(dy * s2, w - 32, axis=1) + pltpu.roll(dy * s1, 32, axis=1)


def _mla_pre(proj, qg, kvg, wq, wkv, tabs, name, ts=256):
    bsz, seq, _ = proj.shape

    def body(p_ref, qg_ref, kvg_ref, wq_ref, wkv_ref, c_ref, s1_ref, s2_ref, q_ref, kv_ref):
        nq = _rms_fn(p_ref[:, 0:256], qg_ref[...])
        nkv = _rms_fn(p_ref[:, 256:384], kvg_ref[...])
        c, s1, s2 = c_ref[...], s1_ref[...], s2_ref[...]
        qp = jnp.dot(nq.astype(BF16), wq_ref[...], preferred_element_type=F32)
        q_ref[...] = _rope(qp, _tile4(c), _tile4(s1), _tile4(s2)).astype(q_ref.dtype)
        kv = jnp.dot(nkv.astype(BF16), wkv_ref[...], preferred_element_type=F32)
        krr = _rope(p_ref[:, 384:512], c, s1, s2)
        zero = jnp.zeros_like(krr)
        kv_ref[...] = (kv + jnp.concatenate([krr, zero] * N_HEADS, axis=1)).astype(kv_ref.dtype)

    tab_spec = pl.BlockSpec((ts, LANES), lambda b, s: (s, 0))
    return pl.pallas_call(
        body, name=name, grid=(bsz, seq // ts),
        in_specs=[pl.BlockSpec((None, ts, 512), lambda b, s: (b, s, P_B // 512)),
                  _vec_spec(256), _vec_spec(128),
                  pl.BlockSpec((256, 512), lambda b, s: (0, 0)), pl.BlockSpec((128, 1024), lambda b, s: (0, 0)),
                  tab_spec, tab_spec, tab_spec],
        out_specs=[_row_spec(ts, 512), _row_spec(ts, 1024)],
        out_shape=[jax.ShapeDtypeStruct((bsz, seq, 512), BF16), jax.ShapeDtypeStruct((bsz, seq, 1024), BF16)],
        compiler_params=_cparams(("parallel", "parallel")),
    )(proj, qg, kvg, wq, wkv, *tabs)


def _mla_pre_bwd(dq, dkv, dproj, proj, qg, kvg, wq, wkv, tabs, name, ts=256):
    bsz, seq, _ = proj.shape

    def body(dq_ref, dkv_ref, dp_any, p_ref, qg_ref, kvg_ref, wq_ref, wkv_ref, c_ref, s1_ref, s2_ref,
             dp_ref, dqg_ref, dkvg_ref, dwq_ref, dwkv_ref):
        del dp_any
        first = jnp.logical_and(pl.program_id(0) == 0, pl.program_id(1) == 0)

        @pl.when(first)
        def _():
            dqg_ref[...] = jnp.zeros_like(dqg_ref)
            dkvg_ref[...] = jnp.zeros_like(dkvg_ref)
            dwq_ref[...] = jnp.zeros_like(dwq_ref)
            dwkv_ref[...] = jnp.zeros_like(dwkv_ref)

        c, s1, s2 = c_ref[...], s1_ref[...], s2_ref[...]
        nq, vjp_q = jax.vjp(_rms_fn, p_ref[:, 0:256], qg_ref[...])
        nkv, vjp_kv = jax.vjp(_rms_fn, p_ref[:, 256:384], kvg_ref[...])
        dqp = _rope_t(dq_ref[...], _tile4(c), _tile4(s1), _tile4(s2)).astype(BF16)
        dkv_v = dkv_ref[...]
        dkv_b = dkv_v.astype(BF16)
        tn = (((0,), (0,)), ((), ()))
        nt = (((1,), (1,)), ((), ()))
        dwq_ref[...] += lax.dot_general(nq.astype(BF16), dqp, tn, preferred_element_type=F32)
        dwkv_ref[...] += lax.dot_general(nkv.astype(BF16), dkv_b, tn, preferred_element_type=F32)
        dcq, dqg = vjp_q(lax.dot_general(dqp, wq_ref[...], nt, preferred_element_type=F32))
        dckv, dkvg = vjp_kv(lax.dot_general(dkv_b, wkv_ref[...], nt, preferred_element_type=F32))
        dqg_ref[...] += dqg
        dkvg_ref[...] += dkvg
        dk_sum = dkv_v[:, 0:128] + dkv_v[:, 256:384] + dkv_v[:, 512:640] + dkv_v[:, 768:896]
        lane = lax.broadcasted_iota(jnp.int32, dk_sum.shape, 1)
        dkr = jnp.where(lane >= 64, _rope_t(dk_sum, c, s1, s2), 0.0)
        dp_ref[:, 0:256] = dcq.astype(dp_ref.dtype)
        dp_ref[:, 256:384] = dckv.astype(dp_ref.dtype)
        dp_ref[:, 384:512] = dkr.astype(dp_ref.dtype)

    tab_spec = pl.BlockSpec((ts, LANES), lambda b, s: (s, 0))
    const = lambda shape: pl.BlockSpec(shape, lambda b, s: (0, 0))
    return pl.pallas_call(
        body, name=name, grid=(bsz, seq // ts),
        in_specs=[_row_spec(ts, 512), _row_spec(ts, 1024), pl.BlockSpec(memory_space=pl.ANY),
                  pl.BlockSpec((None, ts, 512), lambda b, s: (b, s, P_B // 512)),
                  _vec_spec(256), _vec_spec(128), const((256, 512)), const((128, 1024)),
                  tab_spec, tab_spec, tab_spec],
        out_specs=[pl.BlockSpec((None, ts, 512), lambda b, s: (b, s, P_B // 512)),
                   _vec_spec(256), _vec_spec(128), const((256, 512)), const((128, 1024))],
        out_shape=[jax.ShapeDtypeStruct(dproj.shape, dproj.dtype), jax.ShapeDtypeStruct((1, 256), F32),
                   jax.ShapeDtypeStruct((1, 128), F32), jax.ShapeDtypeStruct((256, 512), F32),
                   jax.ShapeDtypeStruct((128, 1024), F32)],
        input_output_aliases={2: 0},
        compiler_params=_cparams(("arbitrary", "arbitrary")),
    )(dq, dkv, dproj, proj, qg, kvg, wq, wkv, *tabs)


def _fox_gate(proj, bf, name):
    bsz, seq, _ = proj.shape
    n_blk = seq // LANES

    def body(x_ref, bf_ref, f_ref):
        r_i = lax.broadcasted_iota(jnp.int32, (LANES, LANES), 0)
        c_i = lax.broadcasted_iota(jnp.int32, (LANES, LANES), 1)
        tril = (r_i >= c_i).astype(F32)
        bias = bf_ref[...]

        def blk(i, carry):
            r = pl.multiple_of(i * LANES, LANES)
            lf = _log_sigmoid(x_ref[pl.ds(r, LANES), :] + bias)
            f_ref[pl.ds(r, LANES), :] = jnp.dot(tril, lf, precision=HI, preferred_element_type=F32) + carry
            return carry + jnp.sum(lf, axis=0, keepdims=True)

        lax.fori_loop(0, n_blk, blk, jnp.zeros((1, LANES), F32))

    return pl.pallas_call(
        body, name=name, grid=(bsz,),
        in_specs=[pl.BlockSpec((None, seq, LANES), lambda b: (b, 0, P_CF // LANES)),
                  pl.BlockSpec((1, LANES), lambda b: (0, 0))],
        out_specs=pl.BlockSpec((None, seq, LANES), lambda b: (b, 0, 0)),
        out_shape=jax.ShapeDtypeStruct((bsz, seq, LANES), F32),
        compiler_params=_cparams(("parallel",)),
    )(proj, bf)


def _fox_gate_bwd(dfq, dfk_cols, dproj, proj, bf, name):
    bsz, seq, _ = proj.shape
    n_blk = seq // LANES

    def body(dfq_ref, dfk_ref, dp_any, x_ref, bf_ref, dp_ref, dbf_ref):
        del dp_any

        @pl.when(pl.program_id(0) == 0)
        def _():
            dbf_ref[...] = jnp.zeros_like(dbf_ref)

        r_i = lax.broadcasted_iota(jnp.int32, (LANES, LANES), 0)
        c_i = lax.broadcasted_iota(jnp.int32, (LANES, LANES), 1)
        triu = (r_i <= c_i).astype(F32)
        bias = bf_ref[...]

        def blk(t, carry):
            tail, dbf = carry
            r = pl.multiple_of((n_blk - 1 - t) * LANES, LANES)
            dc = dfk_ref[pl.ds(r, LANES), :]
            for hd in range(N_HEADS):
                dc = dc + jnp.where(c_i == hd, dfq_ref[hd, pl.ds(r, LANES), :], 0.0)
            dlf = jnp.dot(triu, dc, precision=HI, preferred_element_type=F32) + tail
            dx = dlf * (1.0 - jax.nn.sigmoid(x_ref[pl.ds(r, LANES), :] + bias))
            dp_ref[pl.ds(r, LANES), :] = dx.astype(dp_ref.dtype)
            return tail + jnp.sum(dc, axis=0, keepdims=True), dbf + jnp.sum(dx, axis=0, keepdims=True)

        z = jnp.zeros((1, LANES), F32)
        _, dbf = lax.fori_loop(0, n_blk, blk, (z, z))
        dbf_ref[...] += dbf

    return pl.pallas_call(
        body, name=name, grid=(bsz,),
        in_specs=[pl.BlockSpec((None, N_HEADS, seq, LANES), lambda b: (b, 0, 0, 0)),
                  pl.BlockSpec((None, seq, LANES), lambda b: (b, 0, 0)), pl.BlockSpec(memory_space=pl.ANY),
                  pl.BlockSpec((None, seq, LANES), lambda b: (b, 0, P_CF // LANES)),
                  pl.BlockSpec((1, LANES), lambda b: (0, 0))],
        out_specs=[pl.BlockSpec((None, seq, LANES), lambda b: (b, 0, P_CF // LANES)),
                   pl.BlockSpec((1, LANES), lambda b: (0, 0))],
        out_shape=[jax.ShapeDtypeStruct(dproj.shape, dproj.dtype), jax.ShapeDtypeStruct((1, LANES), F32)],
        input_output_aliases={2: 0},
        compiler_params=_cparams(("arbitrary",)),
    )(dfq, dfk_cols, dproj, proj, bf)


def _gate_terms(fc_ref, fr_ref, h, tq, tk):
    lane = lax.broadcasted_iota(jnp.int32, (tq, LANES), 1)
    fcol = jnp.sum(jnp.where(lane == h, fc_ref[...], 0.0), axis=1, keepdims=True)
    sub = lax.broadcasted_iota(jnp.int32, (8, tk), 0)
    frow = jnp.sum(jnp.where(sub == h, fr_ref[...], 0.0), axis=0, keepdims=True)
    return fcol - frow


def _scores(q_ref, k_ref, gate_refs, scale, h, masked, tq, tk):
    q = (q_ref[...].astype(F32) * scale).astype(BF16)
    s = lax.dot_general(q, k_ref[...].astype(BF16), _DN["nt"], preferred_element_type=F32)
    if gate_refs is not None:
        s = s + _gate_terms(gate_refs[0], gate_refs[1], h, tq, tk)
    if masked is not False:
        r_i = lax.broadcasted_iota(jnp.int32, (tq, tk), 0)
        c_i = lax.broadcasted_iota(jnp.int32, (tq, tk), 1)
        keep = c_i <= r_i
        s = jnp.where(keep if masked is True else jnp.logical_or(jnp.logical_not(masked), keep), s, NEG)
    return s, q


def _lanes(col):
    return jnp.broadcast_to(col, (col.shape[0], LANES))


def _attn_fwd(qa, q0, kva, kv0, mo, o0, gates, scale, name, tq=None):
    bsz, seq, _ = qa.shape
    tq = ATTN_TILE if tq is None else tq
    n_q = seq // tq
    gated = gates is not None

    def body(*refs):
        q_ref, k_ref, v_ref = refs[:3]
        gate_refs = refs[3:5] if gated else None
        o_ref, lse_ref, m_s, l_s, acc_s = refs[-5:]
        h, i, j = pl.program_id(1), pl.program_id(2), pl.program_id(3)

        @pl.when(j == 0)
        def _():
            m_s[...] = jnp.full_like(m_s, NEG)
            l_s[...] = jnp.zeros_like(l_s)
            acc_s[...] = jnp.zeros_like(acc_s)

        def step(masked):
            s, _ = _scores(q_ref, k_ref, gate_refs, scale, h, masked, tq, tq)
            m_prev = m_s[...]
            m_new = jnp.maximum(m_prev, jnp.max(s, axis=1, keepdims=True))
            alpha = jnp.exp(m_prev - m_new)
            p = jnp.exp(s - m_new)
            l_s[...] = alpha * l_s[...] + jnp.sum(p, axis=1, keepdims=True)
            acc_s[...] = alpha * acc_s[...] + jnp.dot(p.astype(BF16), v_ref[...].astype(BF16),
                                                      preferred_element_type=F32)
            m_s[...] = m_new

        @pl.when(j <= i)
        def _():
            step(j == i)

        @pl.when(j == i)
        def _():
            o_ref[...] = (acc_s[...] / l_s[...]).astype(o_ref.dtype)
            lse_ref[...] = _lanes(m_s[...] + jnp.log(l_s[...]))

    blk = (None, tq, LANES)
    in_specs = [pl.BlockSpec(blk, lambda b, h, i, j: (b, i, q0 + h)),
                pl.BlockSpec(blk, lambda b, h, i, j: (b, jnp.minimum(j, i), kv0 + 2 * h)),
                pl.BlockSpec(blk, lambda b, h, i, j: (b, jnp.minimum(j, i), kv0 + 2 * h + 1))]
    args = [qa, kva, kva]
    if gated:
        in_specs += [pl.BlockSpec(blk, lambda b, h, i, j: (b, i, 0)),
                     pl.BlockSpec((None, 8, tq), lambda b, h, i, j: (b, 0, jnp.minimum(j, i)))]
        args += list(gates)
    in_specs.append(pl.BlockSpec(memory_space=pl.ANY))
    args.append(mo)
    return pl.pallas_call(
        body, name=name, grid=(bsz, N_HEADS, n_q, n_q), in_specs=in_specs,
        out_specs=[pl.BlockSpec(blk, lambda b, h, i, j: (b, i, o0 + h)),
                   pl.BlockSpec((None, None, tq, LANES), lambda b, h, i, j: (b, h, i, 0))],
        out_shape=[jax.ShapeDtypeStruct(mo.shape, mo.dtype),
                   jax.ShapeDtypeStruct((bsz, N_HEADS, seq, LANES), F32)],
        scratch_shapes=[pltpu.VMEM((tq, 1), F32), pltpu.VMEM((tq, 1), F32), pltpu.VMEM((tq, LANES), F32)],
        input_output_aliases={len(args) - 1: 0},
        compiler_params=_cparams(("parallel", "parallel", "parallel", "arbitrary")),
    )(*args)


def _attn_bwd_q(qa, q0, kva, kv0, mo, dmo, o0, lse, gates, scale, out, out0, name, tq=None):
    bsz, seq, _ = qa.shape
    tq = ATTN_TILE if tq is None else tq
    n_q = seq // tq
    gated = gates is not None
    aliased = not isinstance(out, jax.ShapeDtypeStruct)

    def body(*refs):
        q_ref, k_ref, v_ref, o_ref, do_ref, lse_ref = refs[:6]
        gate_refs = refs[6:8] if gated else None
        dq_ref, delta_ref, dfq_ref, acc_s, dl_s, df_s = refs[-6:]
        h, i, j = pl.program_id(1), pl.program_id(2), pl.program_id(3)

        @pl.when(j == 0)
        def _():
            acc_s[...] = jnp.zeros_like(acc_s)
            df_s[...] = jnp.zeros_like(df_s)
            dl_s[...] = jnp.sum(do_ref[...] * o_ref[...].astype(F32), axis=1, keepdims=True)

        def step(masked):
            s, _ = _scores(q_ref, k_ref, gate_refs, scale, h, masked, tq, tq)
            p = jnp.exp(s - lse_ref[:, 0:1])
            dp = lax.dot_general(do_ref[...].astype(BF16), v_ref[...].astype(BF16), _DN["nt"],
                                 preferred_element_type=F32)
            ds = p * (dp - dl_s[...])
            acc_s[...] += jnp.dot(ds.astype(BF16), k_ref[...].astype(BF16), preferred_element_type=F32)
            df_s[...] += jnp.sum(ds, axis=1, keepdims=True)

        @pl.when(j <= i)
        def _():
            step(j == i)

        @pl.when(j == i)
        def _():
            dq_ref[...] = (acc_s[...] * scale).astype(dq_ref.dtype)
            delta_ref[...] = _lanes(dl_s[...])
            dfq_ref[...] = _lanes(df_s[...])

    blk = (None, tq, LANES)
    col = pl.BlockSpec((None, None, tq, LANES), lambda b, h, i, j: (b, h, i, 0))
    in_specs = [pl.BlockSpec(blk, lambda b, h, i, j: (b, i, q0 + h)),
                pl.BlockSpec(blk, lambda b, h, i, j: (b, jnp.minimum(j, i), kv0 + 2 * h)),
                pl.BlockSpec(blk, lambda b, h, i, j: (b, jnp.minimum(j, i), kv0 + 2 * h + 1)),
                pl.BlockSpec(blk, lambda b, h, i, j: (b, i, o0 + h)),
                pl.BlockSpec(blk, lambda b, h, i, j: (b, i, o0 + h)), col]
    args = [qa, kva, kva, mo, dmo, lse]
    if gated:
        in_specs += [pl.BlockSpec(blk, lambda b, h, i, j: (b, i, 0)),
                     pl.BlockSpec((None, 8, tq), lambda b, h, i, j: (b, 0, jnp.minimum(j, i)))]
        args += list(gates)
    aliases = {}
    if aliased:
        in_specs.append(pl.BlockSpec(memory_space=pl.ANY))
        args.append(out)
        aliases = {len(args) - 1: 0}
    vec = jax.ShapeDtypeStruct((bsz, N_HEADS, seq, LANES), F32)
    return pl.pallas_call(
        body, name=name, grid=(bsz, N_HEADS, n_q, n_q), in_specs=in_specs,
        out_specs=[pl.BlockSpec(blk, lambda b, h, i, j: (b, i, out0 + h)), col, col],
        out_shape=[jax.ShapeDtypeStruct(out.shape, out.dtype), vec, vec],
        scratch_shapes=[pltpu.VMEM((tq, LANES), F32), pltpu.VMEM((tq, 1), F32), pltpu.VMEM((tq, 1), F32)],
        input_output_aliases=aliases,
        compiler_params=_cparams(("parallel", "parallel", "parallel", "arbitrary")),
    )(*args)


def _attn_bwd_kv(qa, q0, kva, kv0, dmo, o0, lse, delta, gates, scale, out, out0, name, tq=None):
    bsz, seq, _ = qa.shape
    tq = ATTN_TILE if tq is None else tq
    n_q = seq // tq
    gated = gates is not None
    aliased = not isinstance(out, jax.ShapeDtypeStruct)

    def body(*refs):
        q_ref, k_ref, v_ref, do_ref, lse_ref, dl_ref = refs[:6]
        gate_refs = refs[6:8] if gated else None
        dkv_ref, dfk_ref, dk_s, dv_s, df_s = refs[-5:]
        h, j, i = pl.program_id(1), pl.program_id(2), pl.program_id(3)

        @pl.when(i == 0)
        def _():
            dk_s[...] = jnp.zeros_like(dk_s)
            dv_s[...] = jnp.zeros_like(dv_s)
            df_s[...] = jnp.zeros_like(df_s)

        def step(masked):
            s, q = _scores(q_ref, k_ref, gate_refs, scale, h, masked, tq, tq)
            p = jnp.exp(s - lse_ref[:, 0:1])
            do_b = do_ref[...].astype(BF16)
            dp = lax.dot_general(do_b, v_ref[...].astype(BF16), _DN["nt"], preferred_element_type=F32)
            ds = p * (dp - dl_ref[:, 0:1])
            dv_s[...] += lax.dot_general(p.astype(BF16), do_b, _DN["tn"], preferred_element_type=F32)
            dk_s[...] += lax.dot_general(ds.astype(BF16), q, _DN["tn"], preferred_element_type=F32)
            df_s[...] -= jnp.sum(ds, axis=0, keepdims=True)

        @pl.when(i > j)
        def _():
            step(False)

        @pl.when(i == j)
        def _():
            step(True)

        @pl.when(i == n_q - 1)
        def _():
            dkv_ref[:, 0:LANES] = dk_s[...].astype(dkv_ref.dtype)
            dkv_ref[:, LANES:2 * LANES] = dv_s[...].astype(dkv_ref.dtype)
            dfk_ref[...] = df_s[...]

    blk = (None, tq, LANES)
    col = pl.BlockSpec((None, None, tq, LANES), lambda b, h, j, i: (b, h, jnp.maximum(i, j), 0))
    in_specs = [pl.BlockSpec(blk, lambda b, h, j, i: (b, jnp.maximum(i, j), q0 + h)),
                pl.BlockSpec(blk, lambda b, h, j, i: (b, j, kv0 + 2 * h)),
                pl.BlockSpec(blk, lambda b, h, j, i: (b, j, kv0 + 2 * h + 1)),
                pl.BlockSpec(blk, lambda b, h, j, i: (b, jnp.maximum(i, j), o0 + h)), col, col]
    args = [qa, kva, kva, dmo, lse, delta]
    if gated:
        in_specs += [pl.BlockSpec(blk, lambda b, h, j, i: (b, jnp.maximum(i, j), 0)),
                     pl.BlockSpec((None, 8, tq), lambda b, h, j, i: (b, 0, j))]
        args += list(gates)
    aliases = {}
    if aliased:
        in_specs.append(pl.BlockSpec(memory_space=pl.ANY))
        args.append(out)
        aliases = {len(args) - 1: 0}
    return pl.pallas_call(
        body, name=name, grid=(bsz, N_HEADS, n_q, n_q), in_specs=in_specs,
        out_specs=[pl.BlockSpec((None, tq, 2 * LANES), lambda b, h, j, i: (b, j, out0 + h)),
                   pl.BlockSpec((None, None, 1, tq), lambda b, h, j, i: (b, h, 0, j))],
        out_shape=[jax.ShapeDtypeStruct(out.shape, out.dtype), jax.ShapeDtypeStruct((bsz, N_HEADS, 1, seq), F32)],
        scratch_shapes=[pltpu.VMEM((tq, LANES), F32), pltpu.VMEM((tq, LANES), F32), pltpu.VMEM((1, tq), F32)],
        input_output_aliases=aliases,
        compiler_params=_cparams(("parallel", "parallel", "parallel", "arbitrary")),
    )(*args)


def _block_logits(q, k_ref, gate, j, scale_unused, h, masked, tq):
    del scale_unused
    r = pl.multiple_of(j * tq, tq)
    s = lax.dot_general(q, k_ref[pl.ds(r, tq), :].astype(BF16), _DN["nt"], preferred_element_type=F32)
    if gate is not None:
        fcol, fr_ref = gate
        sub = lax.broadcasted_iota(jnp.int32, (8, tq), 0)
        frow = jnp.sum(jnp.where(sub == h, fr_ref[:, pl.ds(r, tq)], 0.0), axis=0, keepdims=True)
        s = s + (fcol - frow)
    if masked:
        r_i = lax.broadcasted_iota(jnp.int32, (tq, tq), 0)
        c_i = lax.broadcasted_iota(jnp.int32, (tq, tq), 1)
        s = jnp.where(c_i <= r_i, s, NEG)
    return s, r


def _gate_col(fc_ref, h, tq):
    lane = lax.broadcasted_iota(jnp.int32, (tq, LANES), 1)
    return jnp.sum(jnp.where(lane == h, fc_ref[...], 0.0), axis=1, keepdims=True)


def _attn_fwd_loop(qa, q0, kva, kv0, mo, o0, gates, scale, name, tq=None):
    bsz, seq, _ = qa.shape
    tq = ATTN_TILE if tq is None else tq
    n_q = seq // tq
    gated = gates is not None

    def body(*refs):
        q_ref, k_ref, v_ref = refs[:3]
        o_ref, lse_ref = refs[-2:]
        h, i = pl.program_id(1), pl.program_id(2)
        q = (q_ref[...].astype(F32) * scale).astype(BF16)
        gate = (_gate_col(refs[3], h, tq), refs[4]) if gated else None

        def step(j, carry, masked):
            m_prev, l_prev, acc = carry
            s, r = _block_logits(q, k_ref, gate, j, None, h, masked, tq)
            m_new = jnp.maximum(m_prev, jnp.max(s, axis=1, keepdims=True))
            alpha = jnp.exp(m_prev - m_new)
            p = jnp.exp(s - m_new)
            l_new = alpha * l_prev + jnp.sum(p, axis=1, keepdims=True)
            acc = alpha * acc + jnp.dot(p.astype(BF16), v_ref[pl.ds(r, tq), :].astype(BF16),
                                        preferred_element_type=F32)
            return m_new, l_new, acc

        init = (jnp.full((tq, 1), NEG, F32), jnp.zeros((tq, 1), F32), jnp.zeros((tq, LANES), F32))
        carry = lax.fori_loop(0, i, lambda j, c: step(j, c, False), init)
        m_f, l_f, acc = step(i, carry, True)
        o_ref[...] = (acc / l_f).astype(o_ref.dtype)
        lse_ref[...] = _lanes(m_f + jnp.log(l_f))

    blk = (None, tq, LANES)
    full = (None, seq, LANES)
    in_specs = [pl.BlockSpec(blk, lambda b, h, i: (b, i, q0 + h)),
                pl.BlockSpec(full, lambda b, h, i: (b, 0, kv0 + 2 * h)),
                pl.BlockSpec(full, lambda b, h, i: (b, 0, kv0 + 2 * h + 1))]
    args = [qa, kva, kva]
    if gated:
        in_specs += [pl.BlockSpec(blk, lambda b, h, i: (b, i, 0)),
                     pl.BlockSpec((None, 8, seq), lambda b, h, i: (b, 0, 0))]
        args += list(gates)
    in_specs.append(pl.BlockSpec(memory_space=pl.ANY))
    args.append(mo)
    return pl.pallas_call(
        body, name=name, grid=(bsz, N_HEADS, n_q), in_specs=in_specs,
        out_specs=[pl.BlockSpec(blk, lambda b, h, i: (b, i, o0 + h)),
                   pl.BlockSpec((None, None, tq, LANES), lambda b, h, i: (b, h, i, 0))],
        out_shape=[jax.ShapeDtypeStruct(mo.shape, mo.dtype),
                   jax.ShapeDtypeStruct((bsz, N_HEADS, seq, LANES), F32)],
        input_output_aliases={len(args) - 1: 0},
        compiler_params=_cparams(("parallel", "parallel", "parallel")),
    )(*args)


def _attn_bwd_q_loop(qa, q0, kva, kv0, mo, dmo, o0, lse, gates, scale, out, out0, name, tq=None):
    bsz, seq, _ = qa.shape
    tq = ATTN_TILE if tq is None else tq
    n_q = seq // tq
    gated = gates is not None
    aliased = not isinstance(out, jax.ShapeDtypeStruct)

    def body(*refs):
        q_ref, k_ref, v_ref, o_ref, do_ref, lse_ref = refs[:6]
        dq_ref, delta_ref, dfq_ref = refs[-3:]
        h, i = pl.program_id(1), pl.program_id(2)
        q = (q_ref[...].astype(F32) * scale).astype(BF16)
        gate = (_gate_col(refs[6], h, tq), refs[7]) if gated else None
        do_v = do_ref[...]
        do_b = do_v.astype(BF16)
        delta = jnp.sum(do_v * o_ref[...].astype(F32), axis=1, keepdims=True)
        lse_v = lse_ref[:, 0:1]

        def step(j, carry, masked):
            acc, dfq = carry
            s, r = _block_logits(q, k_ref, gate, j, None, h, masked, tq)
            p = jnp.exp(s - lse_v)
            dp = lax.dot_general(do_b, v_ref[pl.ds(r, tq), :].astype(BF16), _DN["nt"], preferred_element_type=F32)
            ds = p * (dp - delta)
            acc = acc + jnp.dot(ds.astype(BF16), k_ref[pl.ds(r, tq), :].astype(BF16), preferred_element_type=F32)
            return acc, dfq + jnp.sum(ds, axis=1, keepdims=True)

        init = (jnp.zeros((tq, LANES), F32), jnp.zeros((tq, 1), F32))
        carry = lax.fori_loop(0, i, lambda j, c: step(j, c, False), init)
        acc, dfq = step(i, carry, True)
        dq_ref[...] = (acc * scale).astype(dq_ref.dtype)
        delta_ref[...] = _lanes(delta)
        dfq_ref[...] = _lanes(dfq)

    blk = (None, tq, LANES)
    full = (None, seq, LANES)
    stat = pl.BlockSpec((None, None, tq, LANES), lambda b, h, i: (b, h, i, 0))
    in_specs = [pl.BlockSpec(blk, lambda b, h, i: (b, i, q0 + h)),
                pl.BlockSpec(full, lambda b, h, i: (b, 0, kv0 + 2 * h)),
                pl.BlockSpec(full, lambda b, h, i: (b, 0, kv0 + 2 * h + 1)),
                pl.BlockSpec(blk, lambda b, h, i: (b, i, o0 + h)),
                pl.BlockSpec(blk, lambda b, h, i: (b, i, o0 + h)), stat]
    args = [qa, kva, kva, mo, dmo, lse]
    if gated:
        in_specs += [pl.BlockSpec(blk, lambda b, h, i: (b, i, 0)),
                     pl.BlockSpec((None, 8, seq), lambda b, h, i: (b, 0, 0))]
        args += list(gates)
    aliases = {}
    if aliased:
        in_specs.append(pl.BlockSpec(memory_space=pl.ANY))
        args.append(out)
        aliases = {len(args) - 1: 0}
    vec = jax.ShapeDtypeStruct((bsz, N_HEADS, seq, LANES), F32)
    return pl.pallas_call(
        body, name=name, grid=(bsz, N_HEADS, n_q), in_specs=in_specs,
        out_specs=[pl.BlockSpec(blk, lambda b, h, i: (b, i, out0 + h)), stat, stat],
        out_shape=[jax.ShapeDtypeStruct(out.shape, out.dtype), vec, vec],
        input_output_aliases=aliases,
        compiler_params=_cparams(("parallel", "parallel", "parallel")),
    )(*args)


def _attn_bwd_kv_loop(qa, q0, kva, kv0, dmo, o0, lse, delta, gates, scale, out, out0, name, tq=None):
    bsz, seq, _ = qa.shape
    tq = ATTN_TILE if tq is None else tq
    n_q = seq // tq
    gated = gates is not None
    aliased = not isinstance(out, jax.ShapeDtypeStruct)

    def body(*refs):
        q_ref, k_ref, v_ref, do_ref, lse_ref, dl_ref = refs[:6]
        dkv_ref, dfk_ref = refs[-2:]
        h, j = pl.program_id(1), pl.program_id(2)
        k_b = k_ref[...].astype(BF16)
        v_b = v_ref[...].astype(BF16)
        if gated:
            fc_ref, fr_ref = refs[6], refs[7]
            sub = lax.broadcasted_iota(jnp.int32, (8, tq), 0)
            frow = jnp.sum(jnp.where(sub == h, fr_ref[...], 0.0), axis=0, keepdims=True)
            lane = lax.broadcasted_iota(jnp.int32, (tq, LANES), 1)

        def step(i, carry, masked):
            dk, dv, dfk = carry
            r = pl.multiple_of(i * tq, tq)
            q = (q_ref[pl.ds(r, tq), :].astype(F32) * scale).astype(BF16)
            s = lax.dot_general(q, k_b, _DN["nt"], preferred_element_type=F32)
            if gated:
                fcol = jnp.sum(jnp.where(lane == h, fc_ref[pl.ds(r, tq), :], 0.0), axis=1, keepdims=True)
                s = s + (fcol - frow)
            if masked:
                r_i = lax.broadcasted_iota(jnp.int32, (tq, tq), 0)
                c_i = lax.broadcasted_iota(jnp.int32, (tq, tq), 1)
                s = jnp.where(c_i <= r_i, s, NEG)
            p = jnp.exp(s - lse_ref[pl.ds(r, tq), 0:1])
            do_b = do_ref[pl.ds(r, tq), :].astype(BF16)
            dp = lax.dot_general(do_b, v_b, _DN["nt"], preferred_element_type=F32)
            ds = p * (dp - dl_ref[pl.ds(r, tq), 0:1])
            dv = dv + lax.dot_general(p.astype(BF16), do_b, _DN["tn"], preferred_element_type=F32)
            dk = dk + lax.dot_general(ds.astype(BF16), q, _DN["tn"], preferred_element_type=F32)
            return dk, dv, dfk - jnp.sum(ds, axis=0, keepdims=True)

        init = (jnp.zeros((tq, LANES), F32), jnp.zeros((tq, LANES), F32), jnp.zeros((1, tq), F32))
        carry = step(j, init, True)
        dk, dv, dfk = lax.fori_loop(j + 1, n_q, lambda i, c: step(i, c, False), carry)
        dkv_ref[:, 0:LANES] = dk.astype(dkv_ref.dtype)
        dkv_ref[:, LANES:2 * LANES] = dv.astype(dkv_ref.dtype)
        dfk_ref[...] = dfk

    blk = (None, tq, LANES)
    full = (None, seq, LANES)
    stat = pl.BlockSpec((None, None, seq, LANES), lambda b, h, j: (b, h, 0, 0))
    in_specs = [pl.BlockSpec(full, lambda b, h, j: (b, 0, q0 + h)),
                pl.BlockSpec(blk, lambda b, h, j: (b, j, kv0 + 2 * h)),
                pl.BlockSpec(blk, lambda b, h, j: (b, j, kv0 + 2 * h + 1)),
                pl.BlockSpec(full, lambda b, h, j: (b, 0, o0 + h)), stat, stat]
    args = [qa, kva, kva, dmo, lse, delta]
    if gated:
        in_specs += [pl.BlockSpec(full, lambda b, h, j: (b, 0, 0)),
                     pl.BlockSpec((None, 8, tq), lambda b, h, j: (b, 0, j))]
        args += list(gates)
    aliases = {}
    if aliased:
        in_specs.append(pl.BlockSpec(memory_space=pl.ANY))
        args.append(out)
        aliases = {len(args) - 1: 0}
    return pl.pallas_call(
        body, name=name, grid=(bsz, N_HEADS, n_q), in_specs=in_specs,
        out_specs=[pl.BlockSpec((None, tq, 2 * LANES), lambda b, h, j: (b, j, out0 + h)),
                   pl.BlockSpec((None, None, 1, tq), lambda b, h, j: (b, h, 0, j))],
        out_shape=[jax.ShapeDtypeStruct(out.shape, out.dtype), jax.ShapeDtypeStruct((bsz, N_HEADS, 1, seq), F32)],
        input_output_aliases=aliases,
        compiler_params=_cparams(("parallel", "parallel", "parallel")),
    )(*args)


def _gmlp_fn(uv, lng, lnb, ws, bst):
    u = jax.nn.gelu(uv[:, 0:GROUP_WIDTH])
    gv = jax.nn.gelu(uv[:, GROUP_WIDTH:2 * GROUP_WIDTH])
    mu = jnp.mean(gv, axis=-1, keepdims=True)
    vc = gv - mu
    var = jnp.mean(vc * vc, axis=-1, keepdims=True)
    vln = vc * lax.rsqrt(var + LN_EPS) * lng + lnb
    r_i = lax.broadcasted_iota(jnp.int32, (D_CHUNK, D_CHUNK), 0)
    c_i = lax.broadcasted_iota(jnp.int32, (D_CHUNK, D_CHUNK), 1)
    lane_g = lax.broadcasted_iota(jnp.int32, (D_CHUNK, GROUP_WIDTH), 1) // HEAD_DIM
    e_r = lax.broadcasted_iota(jnp.int32, (LANES, GROUP_WIDTH), 0)
    e_c = lax.broadcasted_iota(jnp.int32, (LANES, GROUP_WIDTH), 1)
    expand = (e_r == e_c // HEAD_DIM).astype(F32)
    mixed = jnp.dot(bst, expand, precision=HI, preferred_element_type=F32)
    for g in range(4):
        w = jnp.where(r_i >= c_i, ws[g], 0.0)
        mixed = mixed + jnp.where(lane_g == g, _bdot(w, vln, "nn"), 0.0)
    return u * mixed


def _gmlp_fwd(proj, mo, lng, lnb, ws, bst, name):
    bsz, seq, _ = proj.shape

    def body(p_ref, mo_any, lng_ref, lnb_ref, ws_ref, bst_ref, o_ref):
        del mo_any
        o_ref[...] = _gmlp_fn(p_ref[...], lng_ref[...], lnb_ref[...], ws_ref[...], bst_ref[...]).astype(o_ref.dtype)

    return pl.pallas_call(
        body, name=name, grid=(bsz, seq // D_CHUNK),
        in_specs=[pl.BlockSpec((None, D_CHUNK, 512), lambda b, s: (b, s, P_D // 512)),
                  pl.BlockSpec(memory_space=pl.ANY), _vec_spec(256), _vec_spec(256),
                  pl.BlockSpec((4, D_CHUNK, D_CHUNK), lambda b, s: (0, 0, 0)),
                  pl.BlockSpec((D_CHUNK, LANES), lambda b, s: (0, 0))],
        out_specs=pl.BlockSpec((None, D_CHUNK, GROUP_WIDTH), lambda b, s: (b, s, 1280 // GROUP_WIDTH)),
        out_shape=jax.ShapeDtypeStruct(mo.shape, mo.dtype),
        input_output_aliases={1: 0},
        compiler_params=_cparams(("parallel", "parallel")),
    )(proj, mo, lng, lnb, ws, bst)


def _gmlp_bwd(dmo, dproj, proj, lng, lnb, ws, bst, name):
    bsz, seq, _ = proj.shape

    def body(do_ref, dp_any, p_ref, lng_ref, lnb_ref, ws_ref, bst_ref, dp_ref, dlg_ref, dlb_ref, dws_ref, dbst_ref):
        del dp_any
        first = jnp.logical_and(pl.program_id(0) == 0, pl.program_id(1) == 0)

        @pl.when(first)
        def _():
            dlg_ref[...] = jnp.zeros_like(dlg_ref)
            dlb_ref[...] = jnp.zeros_like(dlb_ref)
            dws_ref[...] = jnp.zeros_like(dws_ref)
            dbst_ref[...] = jnp.zeros_like(dbst_ref)

        _, vjp = jax.vjp(_gmlp_fn, p_ref[...], lng_ref[...], lnb_ref[...], ws_ref[...], bst_ref[...])
        duv, dlg, dlb, dws, dbst = vjp(do_ref[...])
        dp_ref[...] = duv.astype(dp_ref.dtype)
        dlg_ref[...] += dlg
        dlb_ref[...] += dlb
        dws_ref[...] += dws
        dbst_ref[...] += dbst

    const2 = lambda shape: pl.BlockSpec(shape, lambda b, s: (0,) * len(shape))
    return pl.pallas_call(
        body, name=name, grid=(bsz, seq // D_CHUNK),
        in_specs=[pl.BlockSpec((None, D_CHUNK, GROUP_WIDTH), lambda b, s: (b, s, 1280 // GROUP_WIDTH)),
                  pl.BlockSpec(memory_space=pl.ANY),
                  pl.BlockSpec((None, D_CHUNK, 512), lambda b, s: (b, s, P_D // 512)),
                  _vec_spec(256), _vec_spec(256), const2((4, D_CHUNK, D_CHUNK)), const2((D_CHUNK, LANES))],
        out_specs=[pl.BlockSpec((None, D_CHUNK, 512), lambda b, s: (b, s, P_D // 512)),
                   _vec_spec(256), _vec_spec(256), const2((4, D_CHUNK, D_CHUNK)), const2((D_CHUNK, LANES))],
        out_shape=[jax.ShapeDtypeStruct(dproj.shape, dproj.dtype), jax.ShapeDtypeStruct((1, 256), F32),
                   jax.ShapeDtypeStruct((1, 256), F32), jax.ShapeDtypeStruct((4, D_CHUNK, D_CHUNK), F32),
                   jax.ShapeDtypeStruct((D_CHUNK, LANES), F32)],
        input_output_aliases={1: 0},
        compiler_params=_cparams(("arbitrary", "arbitrary")),
    )(dmo, dproj, proj, lng, lnb, ws, bst)


def _ada_fwd(c_all, ada_w, name):
    n_b = c_all.shape[0]
    depth, d, cols = ada_w.shape

    def body(c_ref, w_ref, o_ref):
        cv = c_ref[...]
        act = (cv * jax.nn.sigmoid(cv)).astype(BF16)
        o_ref[...] = jnp.dot(act, w_ref[...].astype(BF16), preferred_element_type=F32)

    return pl.pallas_call(
        body, name=name, grid=(depth,),
        in_specs=[pl.BlockSpec((n_b, d), lambda l: (0, 0)), pl.BlockSpec((None, d, cols), lambda l: (l, 0, 0))],
        out_specs=pl.BlockSpec((None, n_b, cols), lambda l: (l, 0, 0)),
        out_shape=jax.ShapeDtypeStruct((depth, n_b, cols), F32),
        compiler_params=_cparams(("parallel",)),
    )(c_all, ada_w)


def _ada_bwd(c_all, dmod_cols, dmod_full, name):
    n_b, d = c_all.shape
    depth, _, cols = dmod_cols.shape
    full = dmod_full.shape[-1]

    def body(c_ref, dm_ref, df_ref, gw_ref, gb_ref):
        cv = c_ref[...]
        act = (cv * jax.nn.sigmoid(cv)).astype(BF16)
        gw_ref[...] = lax.dot_general(act, dm_ref[...].astype(BF16), (((0,), (0,)), ((), ())),
                                      preferred_element_type=F32)
        gb_ref[...] = jnp.sum(df_ref[...], axis=0, keepdims=True)

    return pl.pallas_call(
        body, name=name, grid=(depth,),
        in_specs=[pl.BlockSpec((n_b, d), lambda l: (0, 0)), pl.BlockSpec((None, n_b, cols), lambda l: (l, 0, 0)),
                  pl.BlockSpec((None, n_b, full), lambda l: (l, 0, 0))],
        out_specs=[pl.BlockSpec((None, d, cols), lambda l: (l, 0, 0)),
                   pl.BlockSpec((None, 1, full), lambda l: (l, 0, 0))],
        out_shape=[jax.ShapeDtypeStruct((depth, d, cols), F32), jax.ShapeDtypeStruct((depth, 1, full), F32)],
        compiler_params=_cparams(("parallel",)),
    )(c_all, dmod_cols, dmod_full)


def _adamw(gparts, own, w, m, v, name, layer=0, prev=None):
    n_p, rows, cols = gparts.shape
    assert w.shape[1:] == (rows, cols)
    tr = rows
    if rows > 512:
        tr = next(c for c in range(512, 7, -8) if rows % c == 0)
    has_own = own is not None
    n_prev = 0 if prev is None else 4

    def body(*refs):
        if has_own:
            slot_ref, refs = refs[0], refs[1:]
        g_ref = refs[0]
        own_ref = refs[1] if has_own else None
        w_ref, m_ref, v_ref = refs[1 + has_own:4 + has_own]
        go_ref, do_ref, mo_ref, vo_ref = refs[4 + has_own + n_prev:]
        g = None
        for p in range(n_p):
            term = g_ref[p].astype(F32)
            if has_own:
                term = jnp.where(slot_ref[0] == p, own_ref[...].astype(F32), term)
            g = term if g is None else g + term
        m_new = ADAM_B1 * m_ref[...] + (1.0 - ADAM_B1) * g
        v_new = ADAM_B2 * v_ref[...] + (1.0 - ADAM_B2) * (g * g)
        m_hat = m_new / (1.0 - ADAM_B1 ** ADAM_STEP)
        v_hat = v_new / (1.0 - ADAM_B2 ** ADAM_STEP)
        go_ref[...] = g
        do_ref[...] = -ADAM_LR * (m_hat / (jnp.sqrt(v_hat) + ADAM_EPS) + ADAM_WD * w_ref[...])
        mo_ref[...] = m_new
        vo_ref[...] = v_new

    spec = pl.BlockSpec((None, tr, cols), lambda i, *_: (layer, i, 0))
    in_specs = [pl.BlockSpec((n_p, tr, cols), lambda i, *_: (0, i, 0))]
    args = [gparts]
    if has_own:
        in_specs.append(pl.BlockSpec((None, tr, cols), lambda i, slot: (slot[0], i, 0)))
        args.append(own[0])
    in_specs += [spec, spec, spec]
    args += [w, m, v]
    aliases = {}
    if prev is not None:
        aliases = {has_own + len(args) + k: k for k in range(4)}
        in_specs += [pl.BlockSpec(memory_space=pl.ANY)] * 4
        args += list(prev)
    shp = jax.ShapeDtypeStruct(w.shape, F32)
    out_specs, out_shape = [spec, spec, spec, spec], [shp, shp, shp, shp]
    if not has_own:
        return pl.pallas_call(
            body, name=name, grid=(rows // tr,), in_specs=in_specs, out_specs=out_specs, out_shape=out_shape,
            input_output_aliases=aliases, compiler_params=_cparams(("parallel",)),
        )(*args)
    return pl.pallas_call(
        body, name=name, out_shape=out_shape, input_output_aliases=aliases,
        grid_spec=pltpu.PrefetchScalarGridSpec(num_scalar_prefetch=1, grid=(rows // tr,), in_specs=in_specs,
                                               out_specs=out_specs),
        compiler_params=_cparams(("parallel",)),
    )(jnp.reshape(own[1], (1,)).astype(jnp.int32), *args)


def _sum_parts(parts, name):
    n_p, rows, cols = parts.shape
    tr = 256 if rows % 256 == 0 else rows

    def body(p_ref, o_ref):
        acc = p_ref[0]
        for p in range(1, n_p):
            acc = acc + p_ref[p]
        o_ref[...] = acc

    return pl.pallas_call(
        body, name=name, grid=(rows // tr,),
        in_specs=[pl.BlockSpec((n_p, tr, cols), lambda i: (0, i, 0))],
        out_specs=pl.BlockSpec((tr, cols), lambda i: (i, 0)),
        out_shape=jax.ShapeDtypeStruct((rows, cols), F32),
        compiler_params=_cparams(("parallel",)),
    )(parts)


def _all_gather(arrs, name):
    n = len(arrs)

    def body(*refs):
        in_refs, out_refs = refs[:n], refs[n:2 * n]
        send_sems, recv_sems, loc_sems = refs[2 * n:]
        x, y, c = lax.axis_index("x"), lax.axis_index("y"), lax.axis_index("c")
        me, sibling = (x, y, c), (x, y, 1 - c)
        chips = [(1 - x, y), (x, 1 - y), (1 - x, 1 - y)]

        def copy(a, k, block, to, src=None):
            slot = out_refs[a].at[4 * block[0] + 2 * block[1] + block[2]]
            return pltpu.make_async_remote_copy(
                src_ref=slot if src is None else src, dst_ref=slot, send_sem=send_sems.at[a, k],
                recv_sem=recv_sems.at[a, k], device_id=to, device_id_type=pl.DeviceIdType.MESH)

        mine = [pltpu.make_async_copy(in_refs[a], out_refs[a].at[4 * x + 2 * y + c], loc_sems.at[a])
                for a in range(n)]
        for cp in mine:
            cp.start()
        first = []
        for a in range(n):
            first.append(copy(a, 0, me, sibling, src=in_refs[a]))
            first += [copy(a, 1 + j, me, (*chip, c), src=in_refs[a]) for j, chip in enumerate(chips)]
        for cp in first:
            cp.start()
        passed = []
        for j, chip in enumerate(chips):
            for a in range(n):
                copy(a, 1 + j, (*chip, c), me).wait_recv()
                cp = copy(a, 4 + j, (*chip, c), sibling)
                cp.start()
                passed.append(cp)
        for a in range(n):
            copy(a, 0, sibling, me).wait_recv()
        for j, chip in enumerate(chips):
            for a in range(n):
                copy(a, 4 + j, (*chip, 1 - c), me).wait_recv()
        for cp in first + passed:
            cp.wait_send()
        for cp in mine:
            cp.wait()

    any_spec = pl.BlockSpec(memory_space=pl.ANY)
    return pl.pallas_call(
        body, name=name, in_specs=[any_spec] * n, out_specs=[any_spec] * n,
        out_shape=[jax.ShapeDtypeStruct((N_DEV,) + a.shape, a.dtype) for a in arrs],
        scratch_shapes=[pltpu.SemaphoreType.DMA((n, N_DEV - 1)), pltpu.SemaphoreType.DMA((n, N_DEV - 1)),
                        pltpu.SemaphoreType.DMA((n,))],
    )(*arrs)


def _flip_peers():
    x, y, c = lax.axis_index("x"), lax.axis_index("y"), lax.axis_index("c")
    peers = []
    for fx, fy, fc in [(fx, fy, fc) for fx in (0, 1) for fy in (0, 1) for fc in (0, 1)][1:]:
        px, py, pc = (1 - x if fx else x), (1 - y if fy else y), (1 - c if fc else c)
        peers.append(((px, py, pc), 4 * px + 2 * py + pc))
    return 4 * x + 2 * y + c, peers


def _push_start(srcs, name, whole=False):
    n, n_peer = len(srcs), N_DEV - 1
    if whole:
        me_w = 4 * lax.axis_index("x") + 2 * lax.axis_index("y") + lax.axis_index("c")
        lands = [lax.dynamic_update_slice_in_dim(lax.empty((N_DEV,) + a.shape, a.dtype), a[None], me_w, axis=0)
                 for a in srcs]
    else:
        lands = [lax.empty(a.shape, a.dtype) for a in srcs]

    def body(*refs):
        src_refs, land_refs = refs[:n], refs[n:2 * n]
        send_sems, recv_sems = refs[2 * n], refs[2 * n + 1]
        token = refs[-1]
        me, peers = _flip_peers()
        for k, (dev, idx) in enumerate(peers):
            for a in range(n):
                pltpu.make_async_remote_copy(
                    src_ref=src_refs[a] if whole else src_refs[a].at[idx], dst_ref=land_refs[a].at[me],
                    send_sem=send_sems.at[a * n_peer + k], recv_sem=recv_sems.at[a * n_peer + k], device_id=dev,
                    device_id_type=pl.DeviceIdType.MESH).start()
        token[...] = jnp.zeros_like(token)

    hbm = pl.BlockSpec(memory_space=pltpu.HBM)
    sem = pl.BlockSpec(memory_space=pltpu.SEMAPHORE)
    arrs = list(srcs) + lands
    res = pl.pallas_call(
        body, name=name, in_specs=[hbm] * (2 * n),
        out_specs=(sem, sem, *[hbm] * (2 * n), pl.BlockSpec(memory_space=pltpu.VMEM)),
        out_shape=(pltpu.SemaphoreType.DMA((n * n_peer,)), pltpu.SemaphoreType.DMA((n * n_peer,)),
                   *[pltpu.HBM(a.shape, a.dtype) for a in arrs], jax.ShapeDtypeStruct((8, LANES), F32)),
        input_output_aliases={i: 2 + i for i in range(2 * n)},
        compiler_params=pltpu.CompilerParams(has_side_effects=pltpu.SideEffectType.DATAFLOW_SIDE_EFFECTING),
    )(*[pltpu.with_memory_space_constraint(a, pltpu.HBM) for a in arrs])
    return res[0], res[1], list(res[2:2 + n]), list(res[2 + n:2 + 2 * n]), res[-1]


def _push_wait(send_sems, recv_sems, srcs, lands, after, name, whole=False):
    n, n_peer = len(srcs), N_DEV - 1

    def body(*refs):
        src_refs, land_refs = refs[:n], refs[n:2 * n]
        send_s, recv_s = refs[2 * n], refs[2 * n + 1]
        _, peers = _flip_peers()
        for k, (dev, idx) in enumerate(peers):
            for a in range(n):
                cp = pltpu.make_async_remote_copy(
                    src_ref=src_refs[a] if whole else src_refs[a].at[idx], dst_ref=land_refs[a].at[idx],
                    send_sem=send_s.at[a * n_peer + k],
                    recv_sem=recv_s.at[a * n_peer + k], device_id=dev, device_id_type=pl.DeviceIdType.MESH)
                cp.wait_send()
                cp.wait_recv()

    hbm = pl.BlockSpec(memory_space=pltpu.HBM)
    sem = pl.BlockSpec(memory_space=pltpu.SEMAPHORE)
    arrs = list(srcs) + list(lands)
    res = pl.pallas_call(
        body, name=name, in_specs=[hbm] * (2 * n) + [sem, sem, pl.BlockSpec(memory_space=pl.ANY)],
        out_specs=tuple([hbm] * (2 * n)), out_shape=tuple(pltpu.HBM(a.shape, a.dtype) for a in arrs),
        input_output_aliases={i: i for i in range(2 * n)},
        compiler_params=pltpu.CompilerParams(has_side_effects=pltpu.SideEffectType.DATAFLOW_SIDE_EFFECTING),
    )(*arrs, send_sems, recv_sems, after)
    return list(res[:n]), list(res[n:])


def _ffn_fwd(x, h, mod, w_in, w_out_after, lng, lnb, rows, tag, nxt):
    bsz, seq, d = x.shape
    t = bsz * seq
    if h is None:
        h = _modulate(x, mod, rows[0], rows[1], f"modulate_{tag}")
    z, a = _ffn_in_swiglu(h.reshape(t, d), w_in, f"ffn_in_{tag}")
    f = _matmul_groupsum(a, w_out_after(a), out_dtype=F32, tm=512, name=f"ffn_out_{tag}").reshape(bsz, seq, d)
    y, h_next = _res_ln(x, f, mod, lng, lnb, rows[2], 0.5, f"res_ln_{tag}", nxt)
    return y, h_next, (x, h, z, a, f)


def _tied(mod, tie):
    return mod if tie is None else mod + tie


def _open_tail(tail):
    dh, x, mod, dx_res, sc_row = tail
    return dx_res, (dh, x, mod, sc_row)


def _ffn_bwd(dy, pre, saved, mod, w_in, w_out, lng, lnb, rows, tag, ready):
    x, h, z, a, f = saved
    bsz, seq, d = x.shape
    t = bsz * seq
    (dx_res, df, dgate, dlg, dlb), closed = _res_ln_bwd(dy, x, f, mod, lng, lnb, rows[2], 0.5,
                                                       f"res_ln_bwd_{tag}", pre)
    df2 = df.reshape(1, t, d)
    dw_out = _matmul(a, df2, mode="tn", group_out=True, out_dtype=BF16, tm=a.shape[2], tk=min(t, 2048),
                     name=f"ffn_out_dw_{tag}")
    tie_out = ready(f"{tag}_out", dw_out)
    dz = _ffn_out_dx_swiglu(df.reshape(t, d), w_out, z, f"ffn_out_dx_{tag}").reshape(N_DEV, t, -1)
    dw_in = _matmul(dz, h.reshape(1, t, d), mode="tn", group_out=True, out_dtype=BF16, tm=dz.shape[2],
                    tk=min(t, 2048), name=f"ffn_in_dw_{tag}")
    tie_in = ready(f"{tag}_in", dw_in)
    dh = _matmul_groupsum(dz, w_in, out_dtype=F32, tm=512, name=f"ffn_in_dx_{tag}").reshape(bsz, seq, d)
    tail = (dh, x, _tied(_tied(mod, tie_out), tie_in), dx_res, rows[1])
    return tail, closed, dgate, dw_in, dw_out, dlg, dlb


def _mixer_fwd(x, h, mod, wts, small, lng, lnb, layer, tabs):
    bsz, seq, d = x.shape
    t = bsz * seq
    proj = _matmul(h.reshape(1, t, d), wts["mix_in"][None], mode="nn", group_out=True, out_dtype=F32, tm=512, tk=d,
                   name="mix_in").reshape(bsz, seq, PACK_W)
    mo, states = _hgrn_fwd(proj, small["lb_logits8"], small["hgrn_norm_g"], layer, f"hgrn_fwd_l{layer}")
    q, kv = _mla_pre(proj, small["q_norm_g"], small["kv_norm_g"], wts["uq"], wts["ukv"], tabs, "mla_pre")
    mla_scale = float((B_NOPE + B_ROPE) ** -0.5)
    mo, lse_b = _attn_fwd_loop(q, 0, kv, 0, mo, 2, None, mla_scale, "mla_attn_fwd")
    fg = _fox_gate(proj, small["fox_b_f"], "fox_gate")
    gates = (fg, jnp.swapaxes(fg[:, :, 0:8], 1, 2))
    fox_scale = float(HEAD_DIM ** -0.5)
    mo, lse_c = _attn_fwd_loop(proj, P_CQ // LANES, proj, P_CKV // LANES, mo, 6, gates, fox_scale, "fox_attn_fwd")
    mo = _gmlp_fwd(proj, mo, small["gmlp_ln_g"], small["gmlp_ln_b"], small["gmlp_w_s"], small["gmlp_bst"],
                   "gmlp_fwd")
    mixed = _matmul(mo.reshape(1, t, MO_W), wts["mix_out"][None], mode="nn", group_out=True, out_dtype=F32,
                    tm=1024, tk=MO_W, name="mix_out").reshape(bsz, seq, d)
    y, h_next = _res_ln(x, mixed, mod, lng, lnb, 5, 1.0, "res_ln_mix", (mod, 6, 7))
    return y, h_next, (x, h, proj, mo, states, q, kv, lse_b, gates, lse_c, mixed)


def _mixer_bwd(dy, pre, saved, mod, wts, small, lng, lnb, layer, tabs, ready):
    x, h, proj, mo, states, q, kv, lse_b, gates, lse_c, mixed = saved
    bsz, seq, d = x.shape
    t = bsz * seq
    (dx_res, dmixed, dgate, dlg, dlb), closed = _res_ln_bwd(dy, x, mixed, mod, lng, lnb, 5, 1.0, "res_ln_bwd_mix",
                                                           pre)
    dm2 = dmixed.reshape(1, t, d)
    dmo = _matmul(dm2, wts["mix_out"][None], mode="nt", group_out=True, out_dtype=F32, tm=1024, tk=d,
                  name="mix_out_dx").reshape(bsz, seq, MO_W)
    dw_out = _matmul(mo.reshape(1, t, MO_W), dm2, mode="tn", group_out=True, out_dtype=F32, tm=512, tk=min(t, 2048),
                     name="mix_out_dw")[0]
    tie_out = ready("mix_out", dw_out)
    g = {}
    dproj, g["lb_logits8"], g["hgrn_norm_g"] = _hgrn_bwd(dmo, proj, states, small["lb_logits8"],
                                                         small["hgrn_norm_g"], layer, f"hgrn_bwd_l{layer}")
    mla_scale = float((B_NOPE + B_ROPE) ** -0.5)
    dq, delta_b, _ = _attn_bwd_q_loop(q, 0, kv, 0, mo, dmo, 2, lse_b, None, mla_scale,
                                 jax.ShapeDtypeStruct((bsz, seq, 512), F32), 0, "mla_attn_bwd_q")
    dkv, _ = _attn_bwd_kv_loop(q, 0, kv, 0, dmo, 2, lse_b, delta_b, None, mla_scale,
                          jax.ShapeDtypeStruct((bsz, seq, 1024), F32), 0, "mla_attn_bwd_kv")
    dproj, g["q_norm_g"], g["kv_norm_g"], g["uq"], g["ukv"] = _mla_pre_bwd(
        dq, dkv, dproj, proj, small["q_norm_g"], small["kv_norm_g"], wts["uq"], wts["ukv"], tabs, "mla_pre_bwd")
    ready("mla_uq", g.pop("uq"))
    ready("mla_ukv", g.pop("ukv"))
    fox_scale = float(HEAD_DIM ** -0.5)
    dproj, delta_c, dfq = _attn_bwd_q_loop(proj, P_CQ // LANES, proj, P_CKV // LANES, mo, dmo, 6, lse_c, gates,
                                      fox_scale, dproj, P_CQ // LANES, "fox_attn_bwd_q")
    dproj, dfk = _attn_bwd_kv_loop(proj, P_CQ // LANES, proj, P_CKV // LANES, dmo, 6, lse_c, delta_c, gates, fox_scale,
                              dproj, P_CKV // (2 * LANES), "fox_attn_bwd_kv")
    dfk_cols = jnp.pad(jnp.swapaxes(dfk[:, :, 0, :], 1, 2), ((0, 0), (0, 0), (0, LANES - N_HEADS)))
    dproj, g["fox_b_f"] = _fox_gate_bwd(dfq, dfk_cols, dproj, proj, small["fox_b_f"], "fox_gate_bwd")
    dproj, g["gmlp_ln_g"], g["gmlp_ln_b"], g["gmlp_w_s"], g["gmlp_bst"] = _gmlp_bwd(
        dmo, dproj, proj, small["gmlp_ln_g"], small["gmlp_ln_b"], small["gmlp_w_s"], small["gmlp_bst"], "gmlp_bwd")
    dp2 = dproj.reshape(1, t, PACK_W)
    dw_in = _matmul(h.reshape(1, t, d), dp2, mode="tn", group_out=True, out_dtype=BF16, tm=512, tk=1024,
                    name="mix_in_dw")[0]
    tie_in = ready("mix_in", dw_in)
    dh = _matmul(dp2, wts["mix_in"][None], mode="nt", group_out=True, out_dtype=F32, tm=512, tk=PACK_W,
                 name="mix_in_dx").reshape(bsz, seq, d)
    tail = (dh, x, _tied(_tied(mod, tie_out), tie_in), dx_res, 4)
    return tail, closed, dgate, dw_in, dw_out, g, dlg, dlb


def _small_views(p, layer):
    return {
        "lb_logits8": jnp.pad(p["hgrn_lb_logits"], ((0, 8 - DEPTH), (0, 0))),
        "hgrn_norm_g": p["hgrn_norm_g"][layer][None],
        "q_norm_g": p["mla_q_norm_g"][layer][None],
        "kv_norm_g": p["mla_kv_norm_g"][layer][None],
        "fox_b_f": jnp.pad(p["fox_b_f"][layer][None], ((0, 0), (0, LANES - N_HEADS))),
        "gmlp_ln_g": p["gmlp_ln_g"][layer][None],
        "gmlp_ln_b": p["gmlp_ln_b"][layer][None],
        "gmlp_w_s": p["gmlp_w_s"][layer],
        "gmlp_bst": jnp.pad(p["gmlp_b_s"][layer].T, ((0, 0), (0, LANES - N_HEADS))),
    }


def _local_step(x, mod, target, weights, p, grads_ready=None):
    bsz, seq, d = x.shape
    tabs = _rope_tables(seq)
    saved = []
    h = None
    for l in range(DEPTH):
        sm = _small_views(p, l)
        lng, lnb = p["ln_g"][l], p["ln_b"][l]
        x, h, s1 = _ffn_fwd(x, h, mod[l], weights(l, "ffn1_in", x)["ffn1_in"],
                            lambda a, l=l: weights(l, "ffn1_out", a)["ffn1_out"], lng[0:1], lnb[0:1], (0, 1, 2),
                            "ffn1", (mod[l], 3, 4))
        x, h, s2 = _mixer_fwd(x, h, mod[l], weights(l, "mix", x), sm, lng[1:2], lnb[1:2], l, tabs)
        x, h, s3 = _ffn_fwd(x, h, mod[l], weights(l, "ffn2_in", x)["ffn2_in"],
                            lambda a, l=l: weights(l, "ffn2_out", a)["ffn2_out"], lng[2:3], lnb[2:3], (6, 7, 8),
                            "ffn2", (mod[l + 1], 0, 1) if l + 1 < DEPTH else None)
        saved.append((s1, s2, s3))
    dx, loss = _loss_head(x, target, "loss_head")
    big, small, dmods = [None] * DEPTH, [None] * DEPTH, [None] * DEPTH
    ties = []
    tail, rows_of = None, {}

    def tied(a):
        for t in ties:
            a = a + t
        return a

    for l in reversed(range(DEPTH)):
        w = {}
        for part in ("ffn1_in", "ffn1_out", "mix", "ffn2_in", "ffn2_out"):
            w.update(weights(l, part, None))
        sm = _small_views(p, l)
        lng, lnb = p["ln_g"][l], p["ln_b"][l]
        s1, s2, s3 = saved[l]

        def ready(name, grad, l=l):
            tie = None if grads_ready is None else grads_ready(l, name, grad)
            if tie is not None:
                ties.append(tie)
            return tie

        dy, pre = (dx, None) if tail is None else _open_tail(tail)
        tail, closed, dgate3, dwi2, dwo2, dlg2, dlb2 = _ffn_bwd(dy, pre, s3, tied(mod[l]), w["ffn2_in"],
                                                                w["ffn2_out"], lng[2:3], lnb[2:3], (6, 7, 8), "ffn2",
                                                                ready)
        if closed is not None:
            rows_of[(l + 1, 0)], rows_of[(l + 1, 1)] = closed
        dy, pre = _open_tail(tail)
        tail, closed, dgate2, dwmi, dwmo, g, dlg1, dlb1 = _mixer_bwd(dy, pre, s2, tied(mod[l]), w, sm, lng[1:2],
                                                                     lnb[1:2], l, tabs, ready)
        rows_of[(l, 6)], rows_of[(l, 7)] = closed
        dy, pre = _open_tail(tail)
        tail, closed, dgate1, dwi1, dwo1, dlg0, dlb0 = _ffn_bwd(dy, pre, s1, tied(mod[l]), w["ffn1_in"],
                                                                w["ffn1_out"], lng[0:1], lnb[0:1], (0, 1, 2), "ffn1",
                                                                ready)
        rows_of[(l, 3)], rows_of[(l, 4)] = closed
        rows_of[(l, 2)], rows_of[(l, 5)], rows_of[(l, 8)] = dgate1, dgate2, dgate3
        big[l] = {"ffn1_in": dwi1, "ffn1_out": dwo1, "ffn2_in": dwi2, "ffn2_out": dwo2, "mix_in": dwmi,
                  "mix_out": dwmo}
        g["ln_g"] = jnp.concatenate([dlg0, dlg1, dlg2], axis=0)
        g["ln_b"] = jnp.concatenate([dlb0, dlb1, dlb2], axis=0)
        small[l] = g
    dh, x0, mod0, dx_res, sc_row = tail
    dx, rows_of[(0, 0)], rows_of[(0, 1)] = _modulate_bwd(dh, x0, mod0, dx_res, sc_row, "modulate_bwd_ffn1")
    dmods = [jnp.concatenate([rows_of[(l, r)] for r in range(N_MOD)], axis=1) for l in range(DEPTH)]
    return loss, dx, jnp.stack(dmods), big, small


_BIG = ("ffn1_in", "ffn1_out", "ffn2_in", "ffn2_out", "mix_in", "mix_out")


def _small_grad_list(small, loss):
    def both(fn):
        return jnp.stack([fn(small[l]) for l in range(DEPTH)])

    return [
        ("loss", loss.reshape(1)),
        ("ln_g", both(lambda g: g["ln_g"])), ("ln_b", both(lambda g: g["ln_b"])),
        ("hgrn_lb_logits", small[0]["lb_logits8"][:DEPTH] + small[1]["lb_logits8"][:DEPTH]),
        ("hgrn_norm_g", both(lambda g: g["hgrn_norm_g"][0])),
        ("mla_q_norm_g", both(lambda g: g["q_norm_g"][0])),
        ("mla_kv_norm_g", both(lambda g: g["kv_norm_g"][0])),
        ("fox_b_f", both(lambda g: g["fox_b_f"][0, :N_HEADS])),
        ("gmlp_ln_g", both(lambda g: g["gmlp_ln_g"][0])), ("gmlp_ln_b", both(lambda g: g["gmlp_ln_b"][0])),
        ("gmlp_w_s", both(lambda g: g["gmlp_w_s"])),
        ("gmlp_b_s", both(lambda g: g["gmlp_bst"][:, :N_HEADS].T)),
    ]


_PACK_COLS = 512


def _pack_small(items):
    flat = jnp.concatenate([a.reshape(-1).astype(F32) for _, a in items])
    n = flat.shape[0]
    tile = 8 * _PACK_COLS
    flat = jnp.pad(flat, (0, (-n) % tile))
    return flat.reshape(-1, _PACK_COLS)


def _unpack_small(buf, items):
    flat = buf.reshape(-1)
    out, off = {}, 0
    for name, a in items:
        out[name] = flat[off:off + a.size].reshape(a.shape)
        off += a.size
    return out


def _as2d(a):
    return a.reshape(-1, a.shape[-1])


def kernel(x, c, ada_w, ada_b, ln_g, ln_b, ffn1_w_in, ffn1_w_out, ffn2_w_in, ffn2_w_out, mix_w_in, mix_w_out, hgrn_lb_logits, hgrn_norm_g, mla_q_norm_g, mla_kv_norm_g, mla_w_uq, mla_w_ukv, fox_b_f, gmlp_ln_g, gmlp_ln_b, gmlp_w_s, gmlp_b_s, loss_target, m_ada_w, m_ada_b, m_ln_g, m_ln_b, m_ffn1_w_in, m_ffn1_w_out, m_ffn2_w_in, m_ffn2_w_out, m_mix_w_in, m_mix_w_out, m_hgrn_lb_logits, m_hgrn_norm_g, m_mla_q_norm_g, m_mla_kv_norm_g, m_mla_w_uq, m_mla_w_ukv, m_fox_b_f, m_gmlp_ln_g, m_gmlp_ln_b, m_gmlp_w_s, m_gmlp_b_s, v_ada_w, v_ada_b, v_ln_g, v_ln_b, v_ffn1_w_in, v_ffn1_w_out, v_ffn2_w_in, v_ffn2_w_out, v_mix_w_in, v_mix_w_out, v_hgrn_lb_logits, v_hgrn_norm_g, v_mla_q_norm_g, v_mla_kv_norm_g, v_mla_w_uq, v_mla_w_ukv, v_fox_b_f, v_gmlp_ln_g, v_gmlp_ln_b, v_gmlp_w_s, v_gmlp_b_s):
    names = ["ada_w", "ada_b", "ln_g", "ln_b", "ffn1_w_in", "ffn1_w_out", "ffn2_w_in", "ffn2_w_out", "mix_w_in",
             "mix_w_out", "hgrn_lb_logits", "hgrn_norm_g", "mla_q_norm_g", "mla_kv_norm_g", "mla_w_uq", "mla_w_ukv",
             "fox_b_f", "gmlp_ln_g", "gmlp_ln_b", "gmlp_w_s", "gmlp_b_s"]
    w = dict(zip(names, [ada_w, ada_b, ln_g, ln_b, ffn1_w_in, ffn1_w_out, ffn2_w_in, ffn2_w_out, mix_w_in, mix_w_out,
                         hgrn_lb_logits, hgrn_norm_g, mla_q_norm_g, mla_kv_norm_g, mla_w_uq, mla_w_ukv, fox_b_f,
                         gmlp_ln_g, gmlp_ln_b, gmlp_w_s, gmlp_b_s]))
    m = dict(zip(names, [m_ada_w, m_ada_b, m_ln_g, m_ln_b, m_ffn1_w_in, m_ffn1_w_out, m_ffn2_w_in, m_ffn2_w_out,
                         m_mix_w_in, m_mix_w_out, m_hgrn_lb_logits, m_hgrn_norm_g, m_mla_q_norm_g, m_mla_kv_norm_g,
                         m_mla_w_uq, m_mla_w_ukv, m_fox_b_f, m_gmlp_ln_g, m_gmlp_ln_b, m_gmlp_w_s, m_gmlp_b_s]))
    v = dict(zip(names, [v_ada_w, v_ada_b, v_ln_g, v_ln_b, v_ffn1_w_in, v_ffn1_w_out, v_ffn2_w_in, v_ffn2_w_out,
                         v_mix_w_in, v_mix_w_out, v_hgrn_lb_logits, v_hgrn_norm_g, v_mla_q_norm_g, v_mla_kv_norm_g,
                         v_mla_w_uq, v_mla_w_ukv, v_fox_b_f, v_gmlp_ln_g, v_gmlp_ln_b, v_gmlp_w_s, v_gmlp_b_s]))
    bsz, seq, d = x.shape
    me = 4 * lax.axis_index("x") + 2 * lax.axis_index("y") + lax.axis_index("c")
    mix_src, uq_src, ukv_src, mo_src = _mix_in_src(), _uq_src(), _ukv_src(), _mo_src()

    part_names = {"ffn1_in": ["ffn1_w_in"], "ffn1_out": ["ffn1_w_out"],
                  "mix": ["mix_w_in", "mix_w_out", "mla_w_uq", "mla_w_ukv"],
                  "ffn2_in": ["ffn2_w_in"], "ffn2_out": ["ffn2_w_out"]}
    group_of = {(l, part): (l, part) for l in range(DEPTH) for part in part_names}
    in_flight = {}
    transposed = ("ffn1_w_in", "ffn2_w_in")

    def start_group(key, behind=None):
        members = [(l, part) for (l, part), g in group_of.items() if g == key]
        labels = [(l, n) for l, part in members for n in part_names[part]]
        shards = []
        for l, n in labels:
            a = w[n][l]
            if n == "mix_w_in":
                a = _pack_cols(a, mix_src)
            if n in transposed:
                a = jnp.swapaxes(w[n], 1, 2)[l]
            shards.append(a.astype(BF16))
        if behind is not None:
            shards, _ = lax.optimization_barrier((shards, behind))
        in_flight[key] = (labels, _push_start(shards, f"gather_start_{key[0]}_{key[1]}", whole=True))

    keys_in_order = list(dict.fromkeys(group_of.values()))
    start_group(keys_in_order[0])

    gathered = _all_gather([c, ln_g, ln_b], "gather_inputs")
    c_all = gathered[0].reshape(N_DEV * bsz, d)
    ln_g_full = jnp.moveaxis(gathered[1], 0, 2).reshape(DEPTH, 3, d)
    ln_b_full = jnp.moveaxis(gathered[2], 0, 2).reshape(DEPTH, 3, d)

    mod_cols = _ada_fwd(c_all, ada_w, "ada_fwd")
    mod_all, = _all_gather([mod_cols], "gather_mod")
    mod_mine = lax.dynamic_slice_in_dim(mod_all, me * bsz, bsz, axis=2)
    mod = jnp.moveaxis(mod_mine, 0, 2).reshape(DEPTH, bsz, N_MOD * d) + ada_b[:, None, :]
    for key in keys_in_order[1:]:
        start_group(key, behind=mod)
    tie = sum(h[-1][0, 0] for _, h in in_flight.values())
    mod = mod.reshape(DEPTH, bsz, N_MOD, d) + tie

    arrived, laid_out = {}, {}

    def weights(l, part, after):
        if (l, part) not in laid_out:
            laid_out[(l, part)] = lay_out(l, part, after)
        return laid_out[(l, part)]

    def lay_out(l, part, after):
        key = group_of[(l, part)]
        if key not in arrived:
            labels, (send_sems, recv_sems, srcs, lands, _) = in_flight[key]
            _, lands = _push_wait(send_sems, recv_sems, srcs, lands, after, f"gather_wait_{key[0]}_{key[1]}",
                                  whole=True)
            arrived[key] = dict(zip(labels, lands))
        gw = {n: arrived[key][(l, n)] for n in part_names[part]}
        if part.endswith("_in"):
            return {part: gw[part_names[part][0]]}
        if part.endswith("_out"):
            return {part: gw[part_names[part][0]].reshape(4, 704, d)}
        uq = jnp.moveaxis(gw["mla_w_uq"], 0, 1).reshape(256, 384)
        ukv = jnp.moveaxis(gw["mla_w_ukv"], 0, 1).reshape(128, 512)
        return {"mix_in": gw["mix_w_in"].reshape(d, PACK_W),
                "mix_out": _pack_cols(gw["mix_w_out"].reshape(d, d).T, mo_src).T,
                "uq": _pack_cols(uq, uq_src), "ukv": _pack_cols(ukv, ukv_src)}

    p = dict(w)
    p["ln_g"], p["ln_b"] = ln_g_full, ln_b_full
    def chunks(name, arr):
        if name in ("ffn1_in", "ffn2_in"):
            return arr
        if name in ("ffn1_out", "ffn2_out"):
            return arr.reshape(N_DEV, arr.shape[1] // 2, d)
        if name == "mix_in":
            return _unpack_cols(arr, mix_src, MIX_ORIG_W).reshape(N_DEV, d // N_DEV, MIX_ORIG_W)
        if name in ("mla_uq", "mla_ukv"):
            full_w = _unpack_cols(arr, uq_src, 384) if name == "mla_uq" else _unpack_cols(arr, ukv_src, 512)
            rows = full_w.shape[0]
            return jnp.moveaxis(full_w.reshape(rows, N_DEV, -1), 1, 0).astype(BF16)
        return _unpack_cols(arr.T, mo_src, d).T.astype(BF16).reshape(N_DEV, d // N_DEV, d)

    pending, started = {}, []

    def grads_ready(l, name, grad):
        pending[(name, l)] = chunks(name, grad)
        flush = name == "ffn1_in" if l > 0 else name in ("ffn2_in", "mix_out", "mix_in", "ffn1_out", "ffn1_in")
        if not flush:
            return None
        keys = sorted(pending)
        handles = _push_start([pending[k] for k in keys], f"push_start_{len(started)}")
        pending.clear()
        started.append((keys, handles, l == 0 and name.startswith("ffn1")))
        return handles[-1][0, 0]

    loss, grad_x, dmod, big, small = _local_step(x, mod, loss_target, weights, p, grads_ready)
    del big

    recv, out = {}, {}

    def arrive(n, after):
        keys, (send_sems, recv_sems, srcs, lands, _), _ = started[n]
        srcs, lands = _push_wait(send_sems, recv_sems, srcs, lands, after, f"push_wait_{n}")
        for k, src, land in zip(keys, srcs, lands):
            recv[k] = (land, src)

    big_of = {"ffn1_w_in": "ffn1_in", "ffn1_w_out": "ffn1_out", "ffn2_w_in": "ffn2_in", "ffn2_w_out": "ffn2_out",
              "mix_w_in": "mix_in", "mix_w_out": "mix_out", "mla_w_uq": "mla_uq", "mla_w_ukv": "mla_ukv"}
    chain = {name: None for name in big_of}

    def big_update(key, l):
        name = next(nm for nm, k in big_of.items() if k == key)
        parts, src = recv[(key, l)]
        view =(lambda a: jnp.swapaxes(a, 1, 2)) if name in transposed else (lambda a: a)
        chain[name] = _adamw(parts, (src, me), view(w[name]), view(m[name]), view(v[name]), f"adamw_{name}_l{l}",
                             layer=l, prev=chain[name])

    def update(name, grad):
        shape = w[name].shape
        as3 = lambda a: a.reshape(1, -1, shape[-1])
        res = _adamw(as3(grad), None, as3(w[name]), as3(m[name]), as3(v[name]), f"adamw_{name}")
        out[name] = tuple(r.reshape(shape) for r in res)

    for n, (keys, _, last) in enumerate(started):
        if not last:
            arrive(n, grad_x)
            for key, l in keys:
                big_update(key, l)

    dmod_flat = dmod.reshape(DEPTH, bsz, N_MOD * d)
    done = [r[0] for r in chain.values() if r is not None]
    if done:
        dmod_flat, _ = lax.optimization_barrier((dmod_flat, done))
    dmod_all, = _all_gather([dmod_flat], "gather_dmod")
    dmod_full = jnp.moveaxis(dmod_all, 0, 1).reshape(DEPTH, N_DEV * bsz, N_MOD * d)
    cols = ada_w.shape[2]
    dmod_cols = lax.dynamic_slice_in_dim(dmod_full, me * cols, cols, axis=2)
    g_ada_w, g_ada_b = _ada_bwd(c_all, dmod_cols, dmod_full, "ada_bwd")
    res = None
    for l in range(DEPTH):
        res = _adamw(g_ada_w[l][None], None, ada_w, m_ada_w, v_ada_w, f"adamw_ada_w_l{l}", layer=l, prev=res)
    out["ada_w"] = tuple(res)
    update("ada_b", g_ada_b.reshape(DEPTH, N_MOD * d))

    items = _small_grad_list(small, loss)
    packed, _ = lax.optimization_barrier((_pack_small(items), (grad_x, g_ada_b)))
    parts, = _all_gather([packed], "gather_small")
    sg = _unpack_small(_sum_parts(parts, "sum_small"), items)
    for name in ("ln_g", "ln_b"):
        update(name, lax.dynamic_slice_in_dim(sg[name], me * (d // N_DEV), d // N_DEV, axis=2))
    for name in ("hgrn_lb_logits", "hgrn_norm_g", "mla_q_norm_g", "mla_kv_norm_g", "fox_b_f", "gmlp_ln_g",
                 "gmlp_ln_b", "gmlp_w_s", "gmlp_b_s"):
        update(name, sg[name])

    for n, (keys, _, last) in enumerate(started):
        if last:
            arrive(n, out["gmlp_w_s"][0])
            for key, l in keys:
                big_update(key, l)
    for name in big_of:
        out[name] = tuple(jnp.swapaxes(r, 1, 2) if name in transposed else r for r in chain[name])

    return (sg["loss"][0], grad_x, *[out[n][0] for n in names], *[out[n][1] for n in names],
            *[out[n][2] for n in names], *[out[n][3] for n in names])
```

```python
import functools

import numpy as np
import jax
import jax.numpy as jnp
from jax import lax
from jax.experimental import pallas as pl
from jax.experimental.pallas import tpu as pltpu

F32 = jnp.float32
BF16 = jnp.bfloat16
HI = lax.Precision.HIGHEST

D_MODEL = 1024
DEPTH = 2
GROUP_WIDTH = 256
N_HEADS = 4
HEAD_DIM = 64
A_CHUNK = 16
LB_FLOOR = 1e-30
B_NOPE = 64
B_ROPE = 32
ROPE_THETA = 10000.0
D_CHUNK = 128
D_FF = 2816
N_MOD = 9
ALPHA = (2 * DEPTH) ** 0.25
LN_EPS = 1e-5
RMS_EPS = 1e-6
ADAM_LR = 0.001
ADAM_B1 = 0.9
ADAM_B2 = 0.999
ADAM_EPS = 1e-08
ADAM_WD = 0.01
ADAM_STEP = 10

N_DEV = 8
LANES = 128
PACK_W = 3712
MO_W = 1536
VMEM_LIMIT = 56 * 1024 * 1024
NEG = -1e30
ATTN_TILE = 1024

MIX_ORIG_W = 2724
O_BCQ, O_BCKV, O_BKR, O_CQ, O_CK, O_CV, O_CF, O_DU, O_DV = 1024, 1280, 1408, 1440, 1696, 1952, 2208, 2212, 2468
P_B, P_KR, P_CQ, P_CKV, P_D, P_CF = 1024, 1408, 1536, 2048, 3072, 3584


_DN = {"nn": (((1,), (0,)), ((), ())), "nt": (((1,), (1,)), ((), ())), "tn": (((0,), (0,)), ((), ()))}


def _raw_bdot(a, b, mode):
    return lax.dot_general(a.astype(BF16), b.astype(BF16), _DN[mode], preferred_element_type=F32)


@functools.partial(jax.custom_vjp, nondiff_argnums=(2,))
def _bdot(a, b, mode):
    return _raw_bdot(a, b, mode)


def _bdot_fwd(a, b, mode):
    return _raw_bdot(a, b, mode), (a, b)


def _bdot_bwd(mode, res, g):
    a, b = res
    if mode == "nn":
        return _raw_bdot(g, b, "nt"), _raw_bdot(a, g, "tn")
    if mode == "nt":
        return _raw_bdot(g, b, "nn"), _raw_bdot(g, a, "tn")
    return _raw_bdot(b, g, "nt"), _raw_bdot(a, g, "nn")


_bdot.defvjp(_bdot_fwd, _bdot_bwd)


def _cparams(sem):
    return pltpu.CompilerParams(dimension_semantics=sem, vmem_limit_bytes=VMEM_LIMIT)


def _mix_in_src():
    src = -np.ones(PACK_W, np.int64)
    src[0:P_KR] = np.arange(0, O_BKR)
    src[P_KR + 64:P_KR + 80] = O_BKR + np.arange(16)
    src[P_KR + 96:P_KR + 112] = O_BKR + 16 + np.arange(16)
    for h in range(N_HEADS):
        src[P_CQ + 128 * h:P_CQ + 128 * h + 64] = O_CQ + 64 * h + np.arange(64)
        src[P_CKV + 256 * h:P_CKV + 256 * h + 64] = O_CK + 64 * h + np.arange(64)
        src[P_CKV + 256 * h + 128:P_CKV + 256 * h + 192] = O_CV + 64 * h + np.arange(64)
    src[P_D:P_D + 512] = O_DU + np.arange(512)
    src[P_CF:P_CF + 4] = O_CF + np.arange(4)
    return src


def _uq_src():
    src = -np.ones(512, np.int64)
    for h in range(N_HEADS):
        src[128 * h:128 * h + 64] = 96 * h + np.arange(64)
        src[128 * h + 64:128 * h + 80] = 96 * h + 64 + np.arange(16)
        src[128 * h + 96:128 * h + 112] = 96 * h + 80 + np.arange(16)
    return src


def _ukv_src():
    src = -np.ones(1024, np.int64)
    for h in range(N_HEADS):
        src[256 * h:256 * h + 64] = 128 * h + np.arange(64)
        src[256 * h + 128:256 * h + 192] = 128 * h + 64 + np.arange(64)
    return src


def _mo_src():
    src = -np.ones(MO_W, np.int64)
    src[0:256] = np.arange(256)
    for g in range(2):
        for h in range(N_HEADS):
            src[256 + 512 * g + 128 * h:256 + 512 * g + 128 * h + 64] = 256 + 256 * g + 64 * h + np.arange(64)
    src[1280:1536] = 768 + np.arange(256)
    return src


def _runs(idx):
    runs, i = [], 0
    while i < len(idx):
        j = i + 1
        while j < len(idx) and ((idx[i] < 0 and idx[j] < 0) or (idx[i] >= 0 and idx[j] == idx[i] + j - i)):
            j += 1
        runs.append((int(idx[i]), j - i))
        i = j
    return runs


def _take_runs(w, idx):
    parts = [jnp.zeros(w.shape[:-1] + (n,), w.dtype) if s < 0 else lax.slice_in_dim(w, s, s + n, axis=w.ndim - 1)
             for s, n in _runs(idx)]
    return jnp.concatenate(parts, axis=-1)


def _pack_cols(w, src):
    return _take_runs(w, src)


def _unpack_cols(wp, src, n):
    dst = np.zeros(n, np.int64)
    dst[src[src >= 0]] = np.nonzero(src >= 0)[0]
    return _take_runs(wp, dst)


def _rope_tables(seq):
    half = B_ROPE // 2
    inv_freq = ROPE_THETA ** (-jnp.arange(half, dtype=F32) / half)
    ang = jnp.arange(seq).astype(F32)[:, None] * inv_freq[None, :]
    cos, sin = jnp.cos(ang), jnp.sin(ang)
    z16 = jnp.zeros((seq, 16), F32)
    c = jnp.concatenate([jnp.ones((seq, 64), F32), cos, z16, cos, z16], axis=1)
    s1 = jnp.concatenate([jnp.zeros((seq, 64), F32), -sin, z16, z16, z16], axis=1)
    s2 = jnp.concatenate([jnp.zeros((seq, 64), F32), z16, z16, sin, z16], axis=1)
    return c, s1, s2


def _matmul(a, b, *, mode, group_out, out_dtype, tm, tk, name):
    ga, gb = a.shape[0], b.shape[0]
    g_n = max(ga, gb)
    if mode == "tn":
        k_dim, m_dim = a.shape[1:]
    else:
        m_dim, k_dim = a.shape[1:]
    n_dim = b.shape[1] if mode == "nt" else b.shape[2]
    assert m_dim % tm == 0 and k_dim % tk == 0
    kt = k_dim // tk
    n_red = kt if group_out else g_n * kt
    g_out = g_n if group_out else 1

    def split(g, r):
        return (g, r) if group_out else (r // kt, r % kt)

    def a_map(g, i, r):
        gg, kk = split(g, r)
        gg = gg if ga > 1 else 0
        return (gg, kk, i) if mode == "tn" else (gg, i, kk)

    def b_map(g, i, r):
        gg, kk = split(g, r)
        gg = gg if gb > 1 else 0
        return (gg, 0, kk) if mode == "nt" else (gg, kk, 0)

    a_blk = (None, tk, tm) if mode == "tn" else (None, tm, tk)
    b_blk = (None, n_dim, tk) if mode == "nt" else (None, tk, n_dim)
    dn = _DN[mode]

    def body(a_ref, b_ref, o_ref, *scratch):
        part = lax.dot_general(a_ref[...].astype(BF16), b_ref[...].astype(BF16), dn, preferred_element_type=F32)
        if n_red == 1:
            o_ref[...] = part.astype(o_ref.dtype)
            return
        acc_ref, = scratch
        r = pl.program_id(2)

        @pl.when(r == 0)
        def _():
            acc_ref[...] = part

        @pl.when(r > 0)
        def _():
            acc_ref[...] += part

        @pl.when(r == n_red - 1)
        def _():
            o_ref[...] = acc_ref[...].astype(o_ref.dtype)

    return pl.pallas_call(
        body, name=name, grid=(g_out, m_dim // tm, n_red),
        in_specs=[pl.BlockSpec(a_blk, a_map), pl.BlockSpec(b_blk, b_map)],
        out_specs=pl.BlockSpec((None, tm, n_dim), lambda g, i, r: (g, i, 0)),
        out_shape=jax.ShapeDtypeStruct((g_out, m_dim, n_dim), out_dtype),
        scratch_shapes=[] if n_red == 1 else [pltpu.VMEM((tm, n_dim), F32)],
        compiler_params=_cparams(("parallel", "parallel", "arbitrary")),
    )(a, b)


def _matmul_groupsum(a, b, *, out_dtype, tm, name):
    g_n, m_dim, k_dim = a.shape
    n_dim = b.shape[2]
    assert m_dim % tm == 0 and b.shape[:2] == (g_n, k_dim)

    def body(a_ref, b_ref, o_ref):
        acc = jnp.dot(a_ref[0], b_ref[0], preferred_element_type=F32)
        for g in range(1, g_n):
            acc = acc + jnp.dot(a_ref[g], b_ref[g], preferred_element_type=F32)
        o_ref[...] = acc.astype(o_ref.dtype)

    return pl.pallas_call(
        body, name=name, grid=(m_dim // tm,),
        in_specs=[pl.BlockSpec((g_n, tm, k_dim), lambda i: (0, i, 0)),
                  pl.BlockSpec((g_n, k_dim, n_dim), lambda i: (0, 0, 0))],
        out_specs=pl.BlockSpec((tm, n_dim), lambda i: (i, 0)),
        out_shape=jax.ShapeDtypeStruct((m_dim, n_dim), out_dtype),
        compiler_params=_cparams(("parallel",)),
    )(a, b)


def _row_spec(ts, d):
    return pl.BlockSpec((None, ts, d), lambda b, s: (b, s, 0))


def _mod_spec(d):
    return pl.BlockSpec((None, N_MOD, d), lambda b, s: (b, 0, 0))


def _vec_spec(d):
    return pl.BlockSpec((1, d), lambda b, s: (0, 0))


def _bvec_spec(d):
    return pl.BlockSpec((None, 1, d), lambda b, s: (b, 0, 0))


def _modulate(x, mod, sh_row, sc_row, name, ts=512):
    bsz, seq, d = x.shape

    def body(x_ref, mod_ref, o_ref):
        sh = mod_ref[sh_row:sh_row + 1, :]
        sc = mod_ref[sc_row:sc_row + 1, :]
        o_ref[...] = (x_ref[...] * (1.0 + sc) + sh).astype(o_ref.dtype)

    return pl.pallas_call(
        body, name=name, grid=(bsz, seq // ts),
        in_specs=[_row_spec(ts, d), _mod_spec(d)], out_specs=_row_spec(ts, d),
        out_shape=jax.ShapeDtypeStruct((bsz, seq, d), BF16),
        compiler_params=_cparams(("parallel", "parallel")),
    )(x, mod)


def _modulate_bwd(dh, x, mod, dx_res, sc_row, name, ts=512):
    bsz, seq, d = x.shape

    def body(dh_ref, x_ref, mod_ref, dxr_ref, dx_ref, dsh_ref, dsc_ref):
        s = pl.program_id(1)
        sc = mod_ref[sc_row:sc_row + 1, :]
        dh_v = dh_ref[...]
        dx_ref[...] = dxr_ref[...] + dh_v * (1.0 + sc)
        psh = jnp.sum(dh_v, axis=0, keepdims=True)
        psc = jnp.sum(dh_v * x_ref[...], axis=0, keepdims=True)

        @pl.when(s == 0)
        def _():
            dsh_ref[...] = psh
            dsc_ref[...] = psc

        @pl.when(s > 0)
        def _():
            dsh_ref[...] += psh
            dsc_ref[...] += psc

    return pl.pallas_call(
        body, name=name, grid=(bsz, seq // ts),
        in_specs=[_row_spec(ts, d), _row_spec(ts, d), _mod_spec(d), _row_spec(ts, d)],
        out_specs=[_row_spec(ts, d), _bvec_spec(d), _bvec_spec(d)],
        out_shape=[jax.ShapeDtypeStruct((bsz, seq, d), F32), jax.ShapeDtypeStruct((bsz, 1, d), F32),
                   jax.ShapeDtypeStruct((bsz, 1, d), F32)],
        compiler_params=_cparams(("parallel", "arbitrary")),
    )(dh, x, mod, dx_res)


def _res_ln_fn(x, f, g, lng, lnb, cmul):
    r = ALPHA * x + (cmul * (1.0 + g)) * f
    mu = jnp.mean(r, axis=-1, keepdims=True)
    rc = r - mu
    var = jnp.mean(rc * rc, axis=-1, keepdims=True)
    return rc * lax.rsqrt(var + LN_EPS) * lng + lnb


def _res_ln(x, f, mod, lng, lnb, g_row, cmul, name, nxt=None, ts=512):
    bsz, seq, d = x.shape

    def body(*refs):
        x_ref, f_ref, mod_ref, lng_ref, lnb_ref = refs[:5]
        g = mod_ref[g_row:g_row + 1, :]
        y = _res_ln_fn(x_ref[...], f_ref[...], g, lng_ref[...], lnb_ref[...], cmul)
        if nxt is None:
            refs[5][...] = y
            return
        nmod_ref, o_ref, h_ref = refs[5:]
        o_ref[...] = y
        sh = nmod_ref[nxt[1]:nxt[1] + 1, :]
        sc = nmod_ref[nxt[2]:nxt[2] + 1, :]
        h_ref[...] = (y * (1.0 + sc) + sh).astype(h_ref.dtype)

    in_specs = [_row_spec(ts, d), _row_spec(ts, d), _mod_spec(d), _vec_spec(d), _vec_spec(d)]
    args = [x, f, mod, lng, lnb]
    out_specs, out_shape = [_row_spec(ts, d)], [jax.ShapeDtypeStruct((bsz, seq, d), F32)]
    if nxt is not None:
        in_specs.append(_mod_spec(d))
        args.append(nxt[0])
        out_specs.append(_row_spec(ts, d))
        out_shape.append(jax.ShapeDtypeStruct((bsz, seq, d), BF16))
    res = pl.pallas_call(
        body, name=name, grid=(bsz, seq // ts), in_specs=in_specs, out_specs=out_specs, out_shape=out_shape,
        compiler_params=_cparams(("parallel", "parallel")),
    )(*args)
    return (res[0], res[1]) if nxt is not None else (res[0], None)


def _res_ln_bwd(dy, x, f, mod, lng, lnb, g_row, cmul, name, pre=None, ts=256):
    bsz, seq, d = x.shape
    fused = pre is not None

    def body(*refs):
        dy_ref, x_ref, f_ref, mod_ref, lng_ref, lnb_ref = refs[:6]
        n_in = 8 if fused else 6
        dx_ref, df_ref, dg_ref, dlg_ref, dlb_ref = refs[n_in:n_in + 5]
        b, s = pl.program_id(0), pl.program_id(1)
        g = mod_ref[g_row:g_row + 1, :]
        y, vjp = jax.vjp(functools.partial(_res_ln_fn, cmul=cmul), x_ref[...], f_ref[...], g, lng_ref[...],
                         lnb_ref[...])
        ct = dy_ref[...]
        if fused:
            dh_ref, nmod_ref = refs[6:8]
            dsh_ref, dsc_ref = refs[n_in + 5:]
            dh_v = dh_ref[...]
            ct = ct + dh_v * (1.0 + nmod_ref[pre[3]:pre[3] + 1, :])
            psh = jnp.sum(dh_v, axis=0, keepdims=True)
            psc = jnp.sum(dh_v * y, axis=0, keepdims=True)
        dx, df, dg, dlg, dlb = vjp(ct)
        dx_ref[...] = dx
        df_ref[...] = df.astype(df_ref.dtype)

        @pl.when(s == 0)
        def _():
            dg_ref[...] = dg
            if fused:
                dsh_ref[...] = psh
                dsc_ref[...] = psc

        @pl.when(s > 0)
        def _():
            dg_ref[...] += dg
            if fused:
                dsh_ref[...] += psh
                dsc_ref[...] += psc

        first = jnp.logical_and(b == 0, s == 0)

        @pl.when(first)
        def _():
            dlg_ref[...] = dlg
            dlb_ref[...] = dlb

        @pl.when(jnp.logical_not(first))
        def _():
            dlg_ref[...] += dlg
            dlb_ref[...] += dlb

    in_specs = [_row_spec(ts, d), _row_spec(ts, d), _row_spec(ts, d), _mod_spec(d), _vec_spec(d), _vec_spec(d)]
    args = [dy, x, f, mod, lng, lnb]
    out_specs = [_row_spec(ts, d), _row_spec(ts, d), _bvec_spec(d), _vec_spec(d), _vec_spec(d)]
    bvec = jax.ShapeDtypeStruct((bsz, 1, d), F32)
    out_shape = [jax.ShapeDtypeStruct((bsz, seq, d), F32), jax.ShapeDtypeStruct((bsz, seq, d), BF16), bvec,
                 jax.ShapeDtypeStruct((1, d), F32), jax.ShapeDtypeStruct((1, d), F32)]
    if fused:
        in_specs += [_row_spec(ts, d), _mod_spec(d)]
        args += [pre[0], pre[2]]
        out_specs += [_bvec_spec(d), _bvec_spec(d)]
        out_shape += [bvec, bvec]
    res = pl.pallas_call(
        body, name=name, grid=(bsz, seq // ts), in_specs=in_specs, out_specs=out_specs, out_shape=out_shape,
        compiler_params=_cparams(("arbitrary", "arbitrary")),
    )(*args)
    return tuple(res[:5]), (tuple(res[5:]) if fused else None)


def _loss_head(y, target, name, ts=512):
    bsz, seq, d = y.shape
    n_s = seq // ts

    def body(y_ref, t_ref, dy_ref, loss_ref, acc_ref):
        b, s = pl.program_id(0), pl.program_id(1)
        err = y_ref[...] - t_ref[...]
        dy_ref[...] = err * (1.0 / d)
        part = jnp.sum(err * err, axis=0, keepdims=True)
        first = jnp.logical_and(b == 0, s == 0)

        @pl.when(first)
        def _():
            acc_ref[...] = part

        @pl.when(jnp.logical_not(first))
        def _():
            acc_ref[...] += part

        @pl.when(jnp.logical_and(b == bsz - 1, s == n_s - 1))
        def _():
            loss_ref[...] = jnp.sum(acc_ref[...], axis=1, keepdims=True) * (0.5 / d)

    return pl.pallas_call(
        body, name=name, grid=(bsz, n_s),
        in_specs=[_row_spec(ts, d), _row_spec(ts, d)],
        out_specs=[_row_spec(ts, d), pl.BlockSpec((1, 1), lambda b, s: (0, 0))],
        out_shape=[jax.ShapeDtypeStruct((bsz, seq, d), F32), jax.ShapeDtypeStruct((1, 1), F32)],
        scratch_shapes=[pltpu.VMEM((1, d), F32)],
        compiler_params=_cparams(("arbitrary", "arbitrary")),
    )(y, target)


def _ffn_in_swiglu(h, w_in_t, name, tm=1024):
    t, d = h.shape
    n_sh, w, _ = w_in_t.shape
    half = n_sh // 2

    def body(h_ref, w_ref, z_ref, a_ref):
        hv = h_ref[...]
        g = lax.dot_general(hv, w_ref[0], _DN["nt"], preferred_element_type=F32)
        u = lax.dot_general(hv, w_ref[1], _DN["nt"], preferred_element_type=F32)
        z_ref[0] = g.astype(z_ref.dtype)
        z_ref[1] = u.astype(z_ref.dtype)
        a_ref[...] = (g * jax.nn.sigmoid(g) * u).astype(a_ref.dtype)

    return pl.pallas_call(
        body, name=name, grid=(half, t // tm),
        in_specs=[pl.BlockSpec((tm, d), lambda g, i: (i, 0)),
                  pl.BlockSpec((2, None, w, d), lambda g, i: (0, g, 0, 0))],
        out_specs=[pl.BlockSpec((2, None, tm, w), lambda g, i: (0, g, i, 0)),
                   pl.BlockSpec((None, tm, w), lambda g, i: (g, i, 0))],
        out_shape=[jax.ShapeDtypeStruct((2, half, t, w), BF16), jax.ShapeDtypeStruct((half, t, w), BF16)],
        compiler_params=_cparams(("parallel", "parallel")),
    )(h, w_in_t.reshape(2, half, w, d))


def _ffn_out_dx_swiglu(df, w_out, z, name, tm=1024):
    t, d = df.shape
    half, w, _ = w_out.shape

    def body(df_ref, w_ref, z_ref, dz_ref):
        da = lax.dot_general(df_ref[...], w_ref[...], _DN["nt"], preferred_element_type=F32)
        g = z_ref[0].astype(F32)
        u = z_ref[1].astype(F32)
        sig = jax.nn.sigmoid(g)
        dz_ref[0] = (da * u * (sig * (1.0 + g * (1.0 - sig)))).astype(dz_ref.dtype)
        dz_ref[1] = (da * (g * sig)).astype(dz_ref.dtype)

    zspec = pl.BlockSpec((2, None, tm, w), lambda g, i: (0, g, i, 0))
    return pl.pallas_call(
        body, name=name, grid=(half, t // tm),
        in_specs=[pl.BlockSpec((tm, d), lambda g, i: (i, 0)), pl.BlockSpec((None, w, d), lambda g, i: (g, 0, 0)),
                  zspec],
        out_specs=zspec, out_shape=jax.ShapeDtypeStruct(z.shape, BF16),
        compiler_params=_cparams(("parallel", "parallel")),
    )(df, w_out, z)


def _log_sigmoid(x):
    return jnp.minimum(x, 0.0) - jnp.log(1.0 + jnp.exp(-jnp.abs(x)))


def _hgrn_consts():
    r = lax.broadcasted_iota(jnp.int32, (GROUP_WIDTH, GROUP_WIDTH), 0)
    c = lax.broadcasted_iota(jnp.int32, (GROUP_WIDTH, GROUP_WIDTH), 1)
    bd = (r // HEAD_DIM == c // HEAD_DIM).astype(F32)
    r16 = lax.broadcasted_iota(jnp.int32, (A_CHUNK, A_CHUNK), 0)
    c16 = lax.broadcasted_iota(jnp.int32, (A_CHUNK, A_CHUNK), 1)
    tril = (r16 >= c16).astype(F32)
    rows = lax.broadcasted_iota(jnp.int32, (A_CHUNK, GROUP_WIDTH), 0)
    return bd, tril, rows


def _hgrn_lb(logits8, layer):
    rows = lax.broadcasted_iota(jnp.int32, logits8.shape, 0)
    valid = rows < DEPTH
    mx = jnp.max(jnp.where(valid, logits8, NEG), axis=0, keepdims=True)
    e = jnp.where(valid, jnp.exp(logits8 - mx), 0.0)
    sm = e / jnp.sum(e, axis=0, keepdims=True)
    pick = jnp.logical_and(rows >= 1, rows <= layer)
    return jnp.sum(jnp.where(pick, sm, 0.0), axis=0, keepdims=True)


def _hgrn_chunk(aq, af, ai, ag, logits8, norm_g, st, *, layer, consts):
    bd, tril, rows = consts
    lb = _hgrn_lb(logits8, layer)
    la = jnp.log(jnp.maximum(lb, LB_FLOOR))
    b2 = jnp.log(1.0 - lb) + _log_sigmoid(af)
    log_f = jnp.maximum(la, b2) + jnp.log(1.0 + jnp.exp(-jnp.abs(la - b2)))
    k = 1.0 - jnp.exp(log_f)
    qf = aq * jax.nn.sigmoid(aq)
    g_cum = jnp.dot(tril, log_f, precision=HI, preferred_element_type=F32)

    c, w = A_CHUNK, GROUP_WIDTH

    def by_key(v):
        return jnp.broadcast_to(v[:, None, :], (c, c, w))

    def by_query(v):
        return jnp.broadcast_to(v[None, :, :], (c, c, w))

    s_i = lax.broadcasted_iota(jnp.int32, (c, c, w), 0)
    t_i = lax.broadcasted_iota(jnp.int32, (c, c, w), 1)
    rel = jnp.where(t_i >= s_i, by_query(g_cum) - by_key(g_cum), NEG)
    pairs = by_query(qf) * by_key(k) * jnp.exp(rel)
    a_all = _bdot(pairs.reshape(c * c, w), bd, "nn").reshape(c, c, w)
    o = jnp.sum(a_all * by_key(ai), axis=0)
    q_dec = qf * jnp.exp(g_cum)
    o = o + _bdot(q_dec, st, "nt")
    g_last = jnp.sum(jnp.where(rows == c - 1, g_cum, 0.0), axis=0, keepdims=True)
    k_end = k * jnp.exp(g_last - g_cum)
    kv = _bdot(ai, k_end, "tn")
    st_new = st * jnp.exp(g_last) + kv * bd
    ms = _bdot(o * o, bd, "nn") * (1.0 / HEAD_DIM)
    o = o * lax.rsqrt(ms + RMS_EPS) * norm_g
    return o * (ag * jax.nn.sigmoid(ag)), st_new


def _hgrn_fwd(proj, logits8, norm_g, layer, name, ts=256):
    bsz, seq, _ = proj.shape
    n_ch = ts // A_CHUNK

    def body(p_ref, lg_ref, ng_ref, o_ref, st_ref, st_scr):
        @pl.when(pl.program_id(1) == 0)
        def _():
            st_scr[...] = jnp.zeros_like(st_scr)

        consts = _hgrn_consts()
        logits_v, ng_v = lg_ref[...], ng_ref[...]

        def chunk(ci, carry):
            r = ci * A_CHUNK if isinstance(ci, int) else pl.multiple_of(ci * A_CHUNK, A_CHUNK)
            st = st_scr[...]
            st_ref[ci] = st
            o, st_new = _hgrn_chunk(
                p_ref[pl.ds(r, A_CHUNK), 0:256], p_ref[pl.ds(r, A_CHUNK), 256:512],
                p_ref[pl.ds(r, A_CHUNK), 512:768], p_ref[pl.ds(r, A_CHUNK), 768:1024],
                logits_v, ng_v, st, layer=layer, consts=consts)
            o_ref[pl.ds(r, A_CHUNK), :] = o.astype(o_ref.dtype)
            st_scr[...] = st_new
            return carry

        if n_ch <= 2:
            for c_static in range(n_ch):
                chunk(c_static, 0)
        else:
            lax.fori_loop(0, n_ch, chunk, 0, unroll=8)

    return pl.pallas_call(
        body, name=name, grid=(bsz, seq // ts),
        in_specs=[pl.BlockSpec((None, ts, 1024), lambda b, s: (b, s, 0)),
                  pl.BlockSpec((8, GROUP_WIDTH), lambda b, s: (0, 0)),
                  pl.BlockSpec((1, GROUP_WIDTH), lambda b, s: (0, 0))],
        out_specs=[pl.BlockSpec((None, ts, GROUP_WIDTH), lambda b, s: (b, s, 0)),
                   pl.BlockSpec((None, n_ch, GROUP_WIDTH, GROUP_WIDTH), lambda b, s: (b, s, 0, 0))],
        out_shape=[jax.ShapeDtypeStruct((bsz, seq, MO_W), BF16),
                   jax.ShapeDtypeStruct((bsz, seq // A_CHUNK, GROUP_WIDTH, GROUP_WIDTH), F32)],
        scratch_shapes=[pltpu.VMEM((GROUP_WIDTH, GROUP_WIDTH), F32)],
        compiler_params=_cparams(("parallel", "arbitrary")),
    )(proj, logits8, norm_g)


def _hgrn_bwd(dmo, proj, states, logits8, norm_g, layer, name, ts=256):
    bsz, seq, _ = proj.shape
    n_ch = ts // A_CHUNK
    n_s = seq // ts

    def body(do_ref, p_ref, st_ref, lg_ref, ng_ref, dp_ref, dlg_ref, dng_ref, dst_scr):
        b, s = pl.program_id(0), pl.program_id(1)

        @pl.when(s == 0)
        def _():
            dst_scr[...] = jnp.zeros_like(dst_scr)

        @pl.when(jnp.logical_and(b == 0, s == 0))
        def _():
            dlg_ref[...] = jnp.zeros_like(dlg_ref)
            dng_ref[...] = jnp.zeros_like(dng_ref)

        consts = _hgrn_consts()
        logits_v, ng_v = lg_ref[...], ng_ref[...]
        fn = functools.partial(_hgrn_chunk, layer=layer, consts=consts)

        def chunk(t, carry):
            ci = n_ch - 1 - t
            r = ci * A_CHUNK if isinstance(ci, int) else pl.multiple_of(ci * A_CHUNK, A_CHUNK)
            _, vjp = jax.vjp(
                fn, p_ref[pl.ds(r, A_CHUNK), 0:256], p_ref[pl.ds(r, A_CHUNK), 256:512],
                p_ref[pl.ds(r, A_CHUNK), 512:768], p_ref[pl.ds(r, A_CHUNK), 768:1024],
                logits_v, ng_v, st_ref[ci])
            daq, daf, dai, dag, dlg, dng, dst = vjp((do_ref[pl.ds(r, A_CHUNK), :], dst_scr[...]))
            dp_ref[pl.ds(r, A_CHUNK), 0:256] = daq.astype(dp_ref.dtype)
            dp_ref[pl.ds(r, A_CHUNK), 256:512] = daf.astype(dp_ref.dtype)
            dp_ref[pl.ds(r, A_CHUNK), 512:768] = dai.astype(dp_ref.dtype)
            dp_ref[pl.ds(r, A_CHUNK), 768:1024] = dag.astype(dp_ref.dtype)
            dlg_ref[...] += dlg
            dng_ref[...] += dng
            dst_scr[...] = dst
            return carry

        if n_ch <= 2:
            for c_static in range(n_ch):
                chunk(c_static, 0)
        else:
            lax.fori_loop(0, n_ch, chunk, 0, unroll=8)

    rev = lambda b, s: (b, n_s - 1 - s, 0)
    return pl.pallas_call(
        body, name=name, grid=(bsz, n_s),
        in_specs=[pl.BlockSpec((None, ts, GROUP_WIDTH), rev),
                  pl.BlockSpec((None, ts, 1024), rev),
                  pl.BlockSpec((None, n_ch, GROUP_WIDTH, GROUP_WIDTH), lambda b, s: (b, n_s - 1 - s, 0, 0)),
                  pl.BlockSpec((8, GROUP_WIDTH), lambda b, s: (0, 0)),
                  pl.BlockSpec((1, GROUP_WIDTH), lambda b, s: (0, 0))],
        out_specs=[pl.BlockSpec((None, ts, 1024), rev),
                   pl.BlockSpec((8, GROUP_WIDTH), lambda b, s: (0, 0)),
                   pl.BlockSpec((1, GROUP_WIDTH), lambda b, s: (0, 0))],
        out_shape=[jax.ShapeDtypeStruct((bsz, seq, PACK_W), BF16),
                   jax.ShapeDtypeStruct((8, GROUP_WIDTH), F32), jax.ShapeDtypeStruct((1, GROUP_WIDTH), F32)],
        scratch_shapes=[pltpu.VMEM((GROUP_WIDTH, GROUP_WIDTH), F32)],
        compiler_params=_cparams(("arbitrary", "arbitrary")),
    )(dmo, proj, states, logits8, norm_g)


def _rms_fn(x, g):
    return x * lax.rsqrt(jnp.mean(x * x, axis=-1, keepdims=True) + RMS_EPS) * g


def _tile4(t):
    return jnp.concatenate([t, t, t, t], axis=1)


def _rope(x, c, s1, s2):
    w = x.shape[-1]
    return x * c + pltpu.roll(x, 32, axis=1) * s2 + pltpu.roll(x, w - 32, axis=1) * s1


def _rope_t(dy, c, s1, s2):
    w = dy.shape[-1]
    return dy * c + pltpu.roll(dy * s2, w - 32, axis=1) + pltpu.roll(dy * s1, 32, axis=1)


def _mla_pre(proj, qg, kvg, wq, wkv, tabs, name, ts=256):
    bsz, seq, _ = proj.shape

    def body(p_ref, qg_ref, kvg_ref, wq_ref, wkv_ref, c_ref, s1_ref, s2_ref, q_ref, kv_ref):
        nq = _rms_fn(p_ref[:, 0:256], qg_ref[...])
        nkv = _rms_fn(p_ref[:, 256:384], kvg_ref[...])
        c, s1, s2 = c_ref[...], s1_ref[...], s2_ref[...]
        qp = jnp.dot(nq.astype(BF16), wq_ref[...], preferred_element_type=F32)
        q_ref[...] = _rope(qp, _tile4(c), _tile4(s1), _tile4(s2)).astype(q_ref.dtype)
        kv = jnp.dot(nkv.astype(BF16), wkv_ref[...], preferred_element_type=F32)
        krr = _rope(p_ref[:, 384:512], c, s1, s2)
        zero = jnp.zeros_like(krr)
        kv_ref[...] = (kv + jnp.concatenate([krr, zero] * N_HEADS, axis=1)).astype(kv_ref.dtype)

    tab_spec = pl.BlockSpec((ts, LANES), lambda b, s: (s, 0))
    return pl.pallas_call(
        body, name=name, grid=(bsz, seq // ts),
        in_specs=[pl.BlockSpec((None, ts, 512), lambda b, s: (b, s, P_B // 512)),
                  _vec_spec(256), _vec_spec(128),
                  pl.BlockSpec((256, 512), lambda b, s: (0, 0)), pl.BlockSpec((128, 1024), lambda b, s: (0, 0)),
                  tab_spec, tab_spec, tab_spec],
        out_specs=[_row_spec(ts, 512), _row_spec(ts, 1024)],
        out_shape=[jax.ShapeDtypeStruct((bsz, seq, 512), BF16), jax.ShapeDtypeStruct((bsz, seq, 1024), BF16)],
        compiler_params=_cparams(("parallel", "parallel")),
    )(proj, qg, kvg, wq, wkv, *tabs)


def _mla_pre_bwd(dq, dkv, dproj, proj, qg, kvg, wq, wkv, tabs, name, ts=256):
    bsz, seq, _ = proj.shape

    def body(dq_ref, dkv_ref, dp_any, p_ref, qg_ref, kvg_ref, wq_ref, wkv_ref, c_ref, s1_ref, s2_ref,
             dp_ref, dqg_ref, dkvg_ref, dwq_ref, dwkv_ref):
        del dp_any
        first = jnp.logical_and(pl.program_id(0) == 0, pl.program_id(1) == 0)

        @pl.when(first)
        def _():
            dqg_ref[...] = jnp.zeros_like(dqg_ref)
            dkvg_ref[...] = jnp.zeros_like(dkvg_ref)
            dwq_ref[...] = jnp.zeros_like(dwq_ref)
            dwkv_ref[...] = jnp.zeros_like(dwkv_ref)

        c, s1, s2 = c_ref[...], s1_ref[...], s2_ref[...]
        nq, vjp_q = jax.vjp(_rms_fn, p_ref[:, 0:256], qg_ref[...])
        nkv, vjp_kv = jax.vjp(_rms_fn, p_ref[:, 256:384], kvg_ref[...])
        dqp = _rope_t(dq_ref[...], _tile4(c), _tile4(s1), _tile4(s2)).astype(BF16)
        dkv_v = dkv_ref[...]
        dkv_b = dkv_v.astype(BF16)
        tn = (((0,), (0,)), ((), ()))
        nt = (((1,), (1,)), ((), ()))
        dwq_ref[...] += lax.dot_general(nq.astype(BF16), dqp, tn, preferred_element_type=F32)
        dwkv_ref[...] += lax.dot_general(nkv.astype(BF16), dkv_b, tn, preferred_element_type=F32)
        dcq, dqg = vjp_q(lax.dot_general(dqp, wq_ref[...], nt, preferred_element_type=F32))
        dckv, dkvg = vjp_kv(lax.dot_general(dkv_b, wkv_ref[...], nt, preferred_element_type=F32))
        dqg_ref[...] += dqg
        dkvg_ref[...] += dkvg
        dk_sum = dkv_v[:, 0:128] + dkv_v[:, 256:384] + dkv_v[:, 512:640] + dkv_v[:, 768:896]
        lane = lax.broadcasted_iota(jnp.int32, dk_sum.shape, 1)
        dkr = jnp.where(lane >= 64, _rope_t(dk_sum, c, s1, s2), 0.0)
        dp_ref[:, 0:256] = dcq.astype(dp_ref.dtype)
        dp_ref[:, 256:384] = dckv.astype(dp_ref.dtype)
        dp_ref[:, 384:512] = dkr.astype(dp_ref.dtype)

    tab_spec = pl.BlockSpec((ts, LANES), lambda b, s: (s, 0))
    const = lambda shape: pl.BlockSpec(shape, lambda b, s: (0, 0))
    return pl.pallas_call(
        body, name=name, grid=(bsz, seq // ts),
        in_specs=[_row_spec(ts, 512), _row_spec(ts, 1024), pl.BlockSpec(memory_space=pl.ANY),
                  pl.BlockSpec((None, ts, 512), lambda b, s: (b, s, P_B // 512)),
                  _vec_spec(256), _vec_spec(128), const((256, 512)), const((128, 1024)),
                  tab_spec, tab_spec, tab_spec],
        out_specs=[pl.BlockSpec((None, ts, 512), lambda b, s: (b, s, P_B // 512)),
                   _vec_spec(256), _vec_spec(128), const((256, 512)), const((128, 1024))],
        out_shape=[jax.ShapeDtypeStruct(dproj.shape, dproj.dtype), jax.ShapeDtypeStruct((1, 256), F32),
                   jax.ShapeDtypeStruct((1, 128), F32), jax.ShapeDtypeStruct((256, 512), F32),
                   jax.ShapeDtypeStruct((128, 1024), F32)],
        input_output_aliases={2: 0},
        compiler_params=_cparams(("arbitrary", "arbitrary")),
    )(dq, dkv, dproj, proj, qg, kvg, wq, wkv, *tabs)


def _fox_gate(proj, bf, name):
    bsz, seq, _ = proj.shape
    n_blk = seq // LANES

    def body(x_ref, bf_ref, f_ref):
        r_i = lax.broadcasted_iota(jnp.int32, (LANES, LANES), 0)
        c_i = lax.broadcasted_iota(jnp.int32, (LANES, LANES), 1)
        tril = (r_i >= c_i).astype(F32)
        bias = bf_ref[...]

        def blk(i, carry):
            r = pl.multiple_of(i * LANES, LANES)
            lf = _log_sigmoid(x_ref[pl.ds(r, LANES), :] + bias)
            f_ref[pl.ds(r, LANES), :] = jnp.dot(tril, lf, precision=HI, preferred_element_type=F32) + carry
            return carry + jnp.sum(lf, axis=0, keepdims=True)

        lax.fori_loop(0, n_blk, blk, jnp.zeros((1, LANES), F32))

    return pl.pallas_call(
        body, name=name, grid=(bsz,),
        in_specs=[pl.BlockSpec((None, seq, LANES), lambda b: (b, 0, P_CF // LANES)),
                  pl.BlockSpec((1, LANES), lambda b: (0, 0))],
        out_specs=pl.BlockSpec((None, seq, LANES), lambda b: (b, 0, 0)),
        out_shape=jax.ShapeDtypeStruct((bsz, seq, LANES), F32),
        compiler_params=_cparams(("parallel",)),
    )(proj, bf)


def _fox_gate_bwd(dfq, dfk_cols, dproj, proj, bf, name):
    bsz, seq, _ = proj.shape
    n_blk = seq // LANES

    def body(dfq_ref, dfk_ref, dp_any, x_ref, bf_ref, dp_ref, dbf_ref):
        del dp_any

        @pl.when(pl.program_id(0) == 0)
        def _():
            dbf_ref[...] = jnp.zeros_like(dbf_ref)

        r_i = lax.broadcasted_iota(jnp.int32, (LANES, LANES), 0)
        c_i = lax.broadcasted_iota(jnp.int32, (LANES, LANES), 1)
        triu = (r_i <= c_i).astype(F32)
        bias = bf_ref[...]

        def blk(t, carry):
            tail, dbf = carry
            r = pl.multiple_of((n_blk - 1 - t) * LANES, LANES)
            dc = dfk_ref[pl.ds(r, LANES), :]
            for hd in range(N_HEADS):
                dc = dc + jnp.where(c_i == hd, dfq_ref[hd, pl.ds(r, LANES), :], 0.0)
            dlf = jnp.dot(triu, dc, precision=HI, preferred_element_type=F32) + tail
            dx = dlf * (1.0 - jax.nn.sigmoid(x_ref[pl.ds(r, LANES), :] + bias))
            dp_ref[pl.ds(r, LANES), :] = dx.astype(dp_ref.dtype)
            return tail + jnp.sum(dc, axis=0, keepdims=True), dbf + jnp.sum(dx, axis=0, keepdims=True)

        z = jnp.zeros((1, LANES), F32)
        _, dbf = lax.fori_loop(0, n_blk, blk, (z, z))
        dbf_ref[...] += dbf

    return pl.pallas_call(
        body, name=name, grid=(bsz,),
        in_specs=[pl.BlockSpec((None, N_HEADS, seq, LANES), lambda b: (b, 0, 0, 0)),
                  pl.BlockSpec((None, seq, LANES), lambda b: (b, 0, 0)), pl.BlockSpec(memory_space=pl.ANY),
                  pl.BlockSpec((None, seq, LANES), lambda b: (b, 0, P_CF // LANES)),
                  pl.BlockSpec((1, LANES), lambda b: (0, 0))],
        out_specs=[pl.BlockSpec((None, seq, LANES), lambda b: (b, 0, P_CF // LANES)),
                   pl.BlockSpec((1, LANES), lambda b: (0, 0))],
        out_shape=[jax.ShapeDtypeStruct(dproj.shape, dproj.dtype), jax.ShapeDtypeStruct((1, LANES), F32)],
        input_output_aliases={2: 0},
        compiler_params=_cparams(("arbitrary",)),
    )(dfq, dfk_cols, dproj, proj, bf)


def _gate_terms(fc_ref, fr_ref, h, tq, tk):
    lane = lax.broadcasted_iota(jnp.int32, (tq, LANES), 1)
    fcol = jnp.sum(jnp.where(lane == h, fc_ref[...], 0.0), axis=1, keepdims=True)
    sub = lax.broadcasted_iota(jnp.int32, (8, tk), 0)
    frow = jnp.sum(jnp.where(sub == h, fr_ref[...], 0.0), axis=0, keepdims=True)
    return fcol - frow


def _scores(q_ref, k_ref, gate_refs, scale, h, masked, tq, tk):
    q = (q_ref[...].astype(F32) * scale).astype(BF16)
    s = lax.dot_general(q, k_ref[...].astype(BF16), _DN["nt"], preferred_element_type=F32)
    if gate_refs is not None:
        s = s + _gate_terms(gate_refs[0], gate_refs[1], h, tq, tk)
    if masked is not False:
        r_i = lax.broadcasted_iota(jnp.int32, (tq, tk), 0)
        c_i = lax.broadcasted_iota(jnp.int32, (tq, tk), 1)
        keep = c_i <= r_i
        s = jnp.where(keep if masked is True else jnp.logical_or(jnp.logical_not(masked), keep), s, NEG)
    return s, q


def _lanes(col):
    return jnp.broadcast_to(col, (col.shape[0], LANES))


def _attn_fwd(qa, q0, kva, kv0, mo, o0, gates, scale, name, tq=None):
    bsz, seq, _ = qa.shape
    tq = ATTN_TILE if tq is None else tq
    n_q = seq // tq
    gated = gates is not None

    def body(*refs):
        q_ref, k_ref, v_ref = refs[:3]
        gate_refs = refs[3:5] if gated else None
        o_ref, lse_ref, m_s, l_s, acc_s = refs[-5:]
        h, i, j = pl.program_id(1), pl.program_id(2), pl.program_id(3)

        @pl.when(j == 0)
        def _():
            m_s[...] = jnp.full_like(m_s, NEG)
            l_s[...] = jnp.zeros_like(l_s)
            acc_s[...] = jnp.zeros_like(acc_s)

        def step(masked):
            s, _ = _scores(q_ref, k_ref, gate_refs, scale, h, masked, tq, tq)
            m_prev = m_s[...]
            m_new = jnp.maximum(m_prev, jnp.max(s, axis=1, keepdims=True))
            alpha = jnp.exp(m_prev - m_new)
            p = jnp.exp(s - m_new)
            l_s[...] = alpha * l_s[...] + jnp.sum(p, axis=1, keepdims=True)
            acc_s[...] = alpha * acc_s[...] + jnp.dot(p.astype(BF16), v_ref[...].astype(BF16),
                                                      preferred_element_type=F32)
            m_s[...] = m_new

        @pl.when(j <= i)
        def _():
            step(j == i)

        @pl.when(j == i)
        def _():
            o_ref[...] = (acc_s[...] / l_s[...]).astype(o_ref.dtype)
            lse_ref[...] = _lanes(m_s[...] + jnp.log(l_s[...]))

    blk = (None, tq, LANES)
    in_specs = [pl.BlockSpec(blk, lambda b, h, i, j: (b, i, q0 + h)),
                pl.BlockSpec(blk, lambda b, h, i, j: (b, jnp.minimum(j, i), kv0 + 2 * h)),
                pl.BlockSpec(blk, lambda b, h, i, j: (b, jnp.minimum(j, i), kv0 + 2 * h + 1))]
    args = [qa, kva, kva]
    if gated:
        in_specs += [pl.BlockSpec(blk, lambda b, h, i, j: (b, i, 0)),
                     pl.BlockSpec((None, 8, tq), lambda b, h, i, j: (b, 0, jnp.minimum(j, i)))]
        args += list(gates)
    in_specs.append(pl.BlockSpec(memory_space=pl.ANY))
    args.append(mo)
    return pl.pallas_call(
        body, name=name, grid=(bsz, N_HEADS, n_q, n_q), in_specs=in_specs,
        out_specs=[pl.BlockSpec(blk, lambda b, h, i, j: (b, i, o0 + h)),
                   pl.BlockSpec((None, None, tq, LANES), lambda b, h, i, j: (b, h, i, 0))],
        out_shape=[jax.ShapeDtypeStruct(mo.shape, mo.dtype),
                   jax.ShapeDtypeStruct((bsz, N_HEADS, seq, LANES), F32)],
        scratch_shapes=[pltpu.VMEM((tq, 1), F32), pltpu.VMEM((tq, 1), F32), pltpu.VMEM((tq, LANES), F32)],
        input_output_aliases={len(args) - 1: 0},
        compiler_params=_cparams(("parallel", "parallel", "parallel", "arbitrary")),
    )(*args)


def _attn_bwd_q(qa, q0, kva, kv0, mo, dmo, o0, lse, gates, scale, out, out0, name, tq=None):
    bsz, seq, _ = qa.shape
    tq = ATTN_TILE if tq is None else tq
    n_q = seq // tq
    gated = gates is not None
    aliased = not isinstance(out, jax.ShapeDtypeStruct)

    def body(*refs):
        q_ref, k_ref, v_ref, o_ref, do_ref, lse_ref = refs[:6]
        gate_refs = refs[6:8] if gated else None
        dq_ref, delta_ref, dfq_ref, acc_s, dl_s, df_s = refs[-6:]
        h, i, j = pl.program_id(1), pl.program_id(2), pl.program_id(3)

        @pl.when(j == 0)
        def _():
            acc_s[...] = jnp.zeros_like(acc_s)
            df_s[...] = jnp.zeros_like(df_s)
            dl_s[...] = jnp.sum(do_ref[...] * o_ref[...].astype(F32), axis=1, keepdims=True)

        def step(masked):
            s, _ = _scores(q_ref, k_ref, gate_refs, scale, h, masked, tq, tq)
            p = jnp.exp(s - lse_ref[:, 0:1])
            dp = lax.dot_general(do_ref[...].astype(BF16), v_ref[...].astype(BF16), _DN["nt"],
                                 preferred_element_type=F32)
            ds = p * (dp - dl_s[...])
            acc_s[...] += jnp.dot(ds.astype(BF16), k_ref[...].astype(BF16), preferred_element_type=F32)
            df_s[...] += jnp.sum(ds, axis=1, keepdims=True)

        @pl.when(j <= i)
        def _():
            step(j == i)

        @pl.when(j == i)
        def _():
            dq_ref[...] = (acc_s[...] * scale).astype(dq_ref.dtype)
            delta_ref[...] = _lanes(dl_s[...])
            dfq_ref[...] = _lanes(df_s[...])

    blk = (None, tq, LANES)
    col = pl.BlockSpec((None, None, tq, LANES), lambda b, h, i, j: (b, h, i, 0))
    in_specs = [pl.BlockSpec(blk, lambda b, h, i, j: (b, i, q0 + h)),
                pl.BlockSpec(blk, lambda b, h, i, j: (b, jnp.minimum(j, i), kv0 + 2 * h)),
                pl.BlockSpec(blk, lambda b, h, i, j: (b, jnp.minimum(j, i), kv0 + 2 * h + 1)),
                pl.BlockSpec(blk, lambda b, h, i, j: (b, i, o0 + h)),
                pl.BlockSpec(blk, lambda b, h, i, j: (b, i, o0 + h)), col]
    args = [qa, kva, kva, mo, dmo, lse]
    if gated:
        in_specs += [pl.BlockSpec(blk, lambda b, h, i, j: (b, i, 0)),
                     pl.BlockSpec((None, 8, tq), lambda b, h, i, j: (b, 0, jnp.minimum(j, i)))]
        args += list(gates)
    aliases = {}
    if aliased:
        in_specs.append(pl.BlockSpec(memory_space=pl.ANY))
        args.append(out)
        aliases = {len(args) - 1: 0}
    vec = jax.ShapeDtypeStruct((bsz, N_HEADS, seq, LANES), F32)
    return pl.pallas_call(
        body, name=name, grid=(bsz, N_HEADS, n_q, n_q), in_specs=in_specs,
        out_specs=[pl.BlockSpec(blk, lambda b, h, i, j: (b, i, out0 + h)), col, col],
        out_shape=[jax.ShapeDtypeStruct(out.shape, out.dtype), vec, vec],
        scratch_shapes=[pltpu.VMEM((tq, LANES), F32), pltpu.VMEM((tq, 1), F32), pltpu.VMEM((tq, 1), F32)],
        input_output_aliases=aliases,
        compiler_params=_cparams(("parallel", "parallel", "parallel", "arbitrary")),
    )(*args)


def _attn_bwd_kv(qa, q0, kva, kv0, dmo, o0, lse, delta, gates, scale, out, out0, name, tq=None):
    bsz, seq, _ = qa.shape
    tq = ATTN_TILE if tq is None else tq
    n_q = seq // tq
    gated = gates is not None
    aliased = not isinstance(out, jax.ShapeDtypeStruct)

    def body(*refs):
        q_ref, k_ref, v_ref, do_ref, lse_ref, dl_ref = refs[:6]
        gate_refs = refs[6:8] if gated else None
        dkv_ref, dfk_ref, dk_s, dv_s, df_s = refs[-5:]
        h, j, i = pl.program_id(1), pl.program_id(2), pl.program_id(3)

        @pl.when(i == 0)
        def _():
            dk_s[...] = jnp.zeros_like(dk_s)
            dv_s[...] = jnp.zeros_like(dv_s)
            df_s[...] = jnp.zeros_like(df_s)

        def step(masked):
            s, q = _scores(q_ref, k_ref, gate_refs, scale, h, masked, tq, tq)
            p = jnp.exp(s - lse_ref[:, 0:1])
            do_b = do_ref[...].astype(BF16)
            dp = lax.dot_general(do_b, v_ref[...].astype(BF16), _DN["nt"], preferred_element_type=F32)
            ds = p * (dp - dl_ref[:, 0:1])
            dv_s[...] += lax.dot_general(p.astype(BF16), do_b, _DN["tn"], preferred_element_type=F32)
            dk_s[...] += lax.dot_general(ds.astype(BF16), q, _DN["tn"], preferred_element_type=F32)
            df_s[...] -= jnp.sum(ds, axis=0, keepdims=True)

        @pl.when(i > j)
        def _():
            step(False)

        @pl.when(i == j)
        def _():
            step(True)

        @pl.when(i == n_q - 1)
        def _():
            dkv_ref[:, 0:LANES] = dk_s[...].astype(dkv_ref.dtype)
            dkv_ref[:, LANES:2 * LANES] = dv_s[...].astype(dkv_ref.dtype)
            dfk_ref[...] = df_s[...]

    blk = (None, tq, LANES)
    col = pl.BlockSpec((None, None, tq, LANES), lambda b, h, j, i: (b, h, jnp.maximum(i, j), 0))
    in_specs = [pl.BlockSpec(blk, lambda b, h, j, i: (b, jnp.maximum(i, j), q0 + h)),
                pl.BlockSpec(blk, lambda b, h, j, i: (b, j, kv0 + 2 * h)),
                pl.BlockSpec(blk, lambda b, h, j, i: (b, j, kv0 + 2 * h + 1)),
                pl.BlockSpec(blk, lambda b, h, j, i: (b, jnp.maximum(i, j), o0 + h)), col, col]
    args = [qa, kva, kva, dmo, lse, delta]
    if gated:
        in_specs += [pl.BlockSpec(blk, lambda b, h, j, i: (b, jnp.maximum(i, j), 0)),
                     pl.BlockSpec((None, 8, tq), lambda b, h, j, i: (b, 0, j))]
        args += list(gates)
    aliases = {}
    if aliased:
        in_specs.append(pl.BlockSpec(memory_space=pl.ANY))
        args.append(out)
        aliases = {len(args) - 1: 0}
    return pl.pallas_call(
        body, name=name, grid=(bsz, N_HEADS, n_q, n_q), in_specs=in_specs,
        out_specs=[pl.BlockSpec((None, tq, 2 * LANES), lambda b, h, j, i: (b, j, out0 + h)),
                   pl.BlockSpec((None, None, 1, tq), lambda b, h, j, i: (b, h, 0, j))],
        out_shape=[jax.ShapeDtypeStruct(out.shape, out.dtype), jax.ShapeDtypeStruct((bsz, N_HEADS, 1, seq), F32)],
        scratch_shapes=[pltpu.VMEM((tq, LANES), F32), pltpu.VMEM((tq, LANES), F32), pltpu.VMEM((1, tq), F32)],
        input_output_aliases=aliases,
        compiler_params=_cparams(("parallel", "parallel", "parallel", "arbitrary")),
    )(*args)


def _block_logits(q, k_ref, gate, j, scale_unused, h, masked, tq):
    del scale_unused
    r = pl.multiple_of(j * tq, tq)
    s = lax.dot_general(q, k_ref[pl.ds(r, tq), :].astype(BF16), _DN["nt"], preferred_element_type=F32)
    if gate is not None:
        fcol, fr_ref = gate
        sub = lax.broadcasted_iota(jnp.int32, (8, tq), 0)
        frow = jnp.sum(jnp.where(sub == h, fr_ref[:, pl.ds(r, tq)], 0.0), axis=0, keepdims=True)
        s = s + (fcol - frow)
    if masked:
        r_i = lax.broadcasted_iota(jnp.int32, (tq, tq), 0)
        c_i = lax.broadcasted_iota(jnp.int32, (tq, tq), 1)
        s = jnp.where(c_i <= r_i, s, NEG)
    return s, r


def _gate_col(fc_ref, h, tq):
    lane = lax.broadcasted_iota(jnp.int32, (tq, LANES), 1)
    return jnp.sum(jnp.where(lane == h, fc_ref[...], 0.0), axis=1, keepdims=True)


def _attn_fwd_loop(qa, q0, kva, kv0, mo, o0, gates, scale, name, tq=None):
    bsz, seq, _ = qa.shape
    tq = ATTN_TILE if tq is None else tq
    n_q = seq // tq
    gated = gates is not None

    def body(*refs):
        q_ref, k_ref, v_ref = refs[:3]
        o_ref, lse_ref = refs[-2:]
        h, i = pl.program_id(1), pl.program_id(2)
        q = (q_ref[...].astype(F32) * scale).astype(BF16)
        gate = (_gate_col(refs[3], h, tq), refs[4]) if gated else None

        def step(j, carry, masked):
            m_prev, l_prev, acc = carry
            s, r = _block_logits(q, k_ref, gate, j, None, h, masked, tq)
            m_new = jnp.maximum(m_prev, jnp.max(s, axis=1, keepdims=True))
            alpha = jnp.exp(m_prev - m_new)
            p = jnp.exp(s - m_new)
            l_new = alpha * l_prev + jnp.sum(p, axis=1, keepdims=True)
            acc = alpha * acc + jnp.dot(p.astype(BF16), v_ref[pl.ds(r, tq), :].astype(BF16),
                                        preferred_element_type=F32)
            return m_new, l_new, acc

        init = (jnp.full((tq, 1), NEG, F32), jnp.zeros((tq, 1), F32), jnp.zeros((tq, LANES), F32))
        carry = lax.fori_loop(0, i, lambda j, c: step(j, c, False), init)
        m_f, l_f, acc = step(i, carry, True)
        o_ref[...] = (acc / l_f).astype(o_ref.dtype)
        lse_ref[...] = _lanes(m_f + jnp.log(l_f))

    blk = (None, tq, LANES)
    full = (None, seq, LANES)
    in_specs = [pl.BlockSpec(blk, lambda b, h, i: (b, i, q0 + h)),
                pl.BlockSpec(full, lambda b, h, i: (b, 0, kv0 + 2 * h)),
                pl.BlockSpec(full, lambda b, h, i: (b, 0, kv0 + 2 * h + 1))]
    args = [qa, kva, kva]
    if gated:
        in_specs += [pl.BlockSpec(blk, lambda b, h, i: (b, i, 0)),
                     pl.BlockSpec((None, 8, seq), lambda b, h, i: (b, 0, 0))]
        args += list(gates)
    in_specs.append(pl.BlockSpec(memory_space=pl.ANY))
    args.append(mo)
    return pl.pallas_call(
        body, name=name, grid=(bsz, N_HEADS, n_q), in_specs=in_specs,
        out_specs=[pl.BlockSpec(blk, lambda b, h, i: (b, i, o0 + h)),
                   pl.BlockSpec((None, None, tq, LANES), lambda b, h, i: (b, h, i, 0))],
        out_shape=[jax.ShapeDtypeStruct(mo.shape, mo.dtype),
                   jax.ShapeDtypeStruct((bsz, N_HEADS, seq, LANES), F32)],
        input_output_aliases={len(args) - 1: 0},
        compiler_params=_cparams(("parallel", "parallel", "parallel")),
    )(*args)


def _attn_bwd_q_loop(qa, q0, kva, kv0, mo, dmo, o0, lse, gates, scale, out, out0, name, tq=None):
    bsz, seq, _ = qa.shape
    tq = ATTN_TILE if tq is None else tq
    n_q = seq // tq
    gated = gates is not None
    aliased = not isinstance(out, jax.ShapeDtypeStruct)

    def body(*refs):
        q_ref, k_ref, v_ref, o_ref, do_ref, lse_ref = refs[:6]
        dq_ref, delta_ref, dfq_ref = refs[-3:]
        h, i = pl.program_id(1), pl.program_id(2)
        q = (q_ref[...].astype(F32) * scale).astype(BF16)
        gate = (_gate_col(refs[6], h, tq), refs[7]) if gated else None
        do_v = do_ref[...]
        do_b = do_v.astype(BF16)
        delta = jnp.sum(do_v * o_ref[...].astype(F32), axis=1, keepdims=True)
        lse_v = lse_ref[:, 0:1]

        def step(j, carry, masked):
            acc, dfq = carry
            s, r = _block_logits(q, k_ref, gate, j, None, h, masked, tq)
            p = jnp.exp(s - lse_v)
            dp = lax.dot_general(do_b, v_ref[pl.ds(r, tq), :].astype(BF16), _DN["nt"], preferred_element_type=F32)
            ds = p * (dp - delta)
            acc = acc + jnp.dot(ds.astype(BF16), k_ref[pl.ds(r, tq), :].astype(BF16), preferred_element_type=F32)
            return acc, dfq + jnp.sum(ds, axis=1, keepdims=True)

        init = (jnp.zeros((tq, LANES), F32), jnp.zeros((tq, 1), F32))
        carry = lax.fori_loop(0, i, lambda j, c: step(j, c, False), init)
        acc, dfq = step(i, carry, True)
        dq_ref[...] = (acc * scale).astype(dq_ref.dtype)
        delta_ref[...] = _lanes(delta)
        dfq_ref[...] = _lanes(dfq)

    blk = (None, tq, LANES)
    full = (None, seq, LANES)
    stat = pl.BlockSpec((None, None, tq, LANES), lambda b, h, i: (b, h, i, 0))
    in_specs = [pl.BlockSpec(blk, lambda b, h, i: (b, i, q0 + h)),
                pl.BlockSpec(full, lambda b, h, i: (b, 0, kv0 + 2 * h)),
                pl.BlockSpec(full, lambda b, h, i: (b, 0, kv0 + 2 * h + 1)),
                pl.BlockSpec(blk, lambda b, h, i: (b, i, o0 + h)),
                pl.BlockSpec(blk, lambda b, h, i: (b, i, o0 + h)), stat]
    args = [qa, kva, kva, mo, dmo, lse]
    if gated:
        in_specs += [pl.BlockSpec(blk, lambda b, h, i: (b, i, 0)),
                     pl.BlockSpec((None, 8, seq), lambda b, h, i: (b, 0, 0))]
        args += list(gates)
    aliases = {}
    if aliased:
        in_specs.append(pl.BlockSpec(memory_space=pl.ANY))
        args.append(out)
        aliases = {len(args) - 1: 0}
    vec = jax.ShapeDtypeStruct((bsz, N_HEADS, seq, LANES), F32)
    return pl.pallas_call(
        body, name=name, grid=(bsz, N_HEADS, n_q), in_specs=in_specs,
        out_specs=[pl.BlockSpec(blk, lambda b, h, i: (b, i, out0 + h)), stat, stat],
        out_shape=[jax.ShapeDtypeStruct(out.shape, out.dtype), vec, vec],
        input_output_aliases=aliases,
        compiler_params=_cparams(("parallel", "parallel", "parallel")),
    )(*args)


def _attn_bwd_kv_loop(qa, q0, kva, kv0, dmo, o0, lse, delta, gates, scale, out, out0, name, tq=None):
    bsz, seq, _ = qa.shape
    tq = ATTN_TILE if tq is None else tq
    n_q = seq // tq
    gated = gates is not None
    aliased = not isinstance(out, jax.ShapeDtypeStruct)

    def body(*refs):
        q_ref, k_ref, v_ref, do_ref, lse_ref, dl_ref = refs[:6]
        dkv_ref, dfk_ref = refs[-2:]
        h, j = pl.program_id(1), pl.program_id(2)
        k_b = k_ref[...].astype(BF16)
        v_b = v_ref[...].astype(BF16)
        if gated:
            fc_ref, fr_ref = refs[6], refs[7]
            sub = lax.broadcasted_iota(jnp.int32, (8, tq), 0)
            frow = jnp.sum(jnp.where(sub == h, fr_ref[...], 0.0), axis=0, keepdims=True)
            lane = lax.broadcasted_iota(jnp.int32, (tq, LANES), 1)

        def step(i, carry, masked):
            dk, dv, dfk = carry
            r = pl.multiple_of(i * tq, tq)
            q = (q_ref[pl.ds(r, tq), :].astype(F32) * scale).astype(BF16)
            s = lax.dot_general(q, k_b, _DN["nt"], preferred_element_type=F32)
            if gated:
                fcol = jnp.sum(jnp.where(lane == h, fc_ref[pl.ds(r, tq), :], 0.0), axis=1, keepdims=True)
                s = s + (fcol - frow)
            if masked:
                r_i = lax.broadcasted_iota(jnp.int32, (tq, tq), 0)
                c_i = lax.broadcasted_iota(jnp.int32, (tq, tq), 1)
                s = jnp.where(c_i <= r_i, s, NEG)
            p = jnp.exp(s - lse_ref[pl.ds(r, tq), 0:1])
            do_b = do_ref[pl.ds(r, tq), :].astype(BF16)
            dp = lax.dot_general(do_b, v_b, _DN["nt"], preferred_element_type=F32)
            ds = p * (dp - dl_ref[pl.ds(r, tq), 0:1])
            dv = dv + lax.dot_general(p.astype(BF16), do_b, _DN["tn"], preferred_element_type=F32)
            dk = dk + lax.dot_general(ds.astype(BF16), q, _DN["tn"], preferred_element_type=F32)
            return dk, dv, dfk - jnp.sum(ds, axis=0, keepdims=True)

        init = (jnp.zeros((tq, LANES), F32), jnp.zeros((tq, LANES), F32), jnp.zeros((1, tq), F32))
        carry = step(j, init, True)
        dk, dv, dfk = lax.fori_loop(j + 1, n_q, lambda i, c: step(i, c, False), carry)
        dkv_ref[:, 0:LANES] = dk.astype(dkv_ref.dtype)
        dkv_ref[:, LANES:2 * LANES] = dv.astype(dkv_ref.dtype)
        dfk_ref[...] = dfk

    blk = (None, tq, LANES)
    full = (None, seq, LANES)
    stat = pl.BlockSpec((None, None, seq, LANES), lambda b, h, j: (b, h, 0, 0))
    in_specs = [pl.BlockSpec(full, lambda b, h, j: (b, 0, q0 + h)),
                pl.BlockSpec(blk, lambda b, h, j: (b, j, kv0 + 2 * h)),
                pl.BlockSpec(blk, lambda b, h, j: (b, j, kv0 + 2 * h + 1)),
                pl.BlockSpec(full, lambda b, h, j: (b, 0, o0 + h)), stat, stat]
    args = [qa, kva, kva, dmo, lse, delta]
    if gated:
        in_specs += [pl.BlockSpec(full, lambda b, h, j: (b, 0, 0)),
                     pl.BlockSpec((None, 8, tq), lambda b, h, j: (b, 0, j))]
        args += list(gates)
    aliases = {}
    if aliased:
        in_specs.append(pl.BlockSpec(memory_space=pl.ANY))
        args.append(out)
        aliases = {len(args) - 1: 0}
    return pl.pallas_call(
        body, name=name, grid=(bsz, N_HEADS, n_q), in_specs=in_specs,
        out_specs=[pl.BlockSpec((None, tq, 2 * LANES), lambda b, h, j: (b, j, out0 + h)),
                   pl.BlockSpec((None, None, 1, tq), lambda b, h, j: (b, h, 0, j))],
        out_shape=[jax.ShapeDtypeStruct(out.shape, out.dtype), jax.ShapeDtypeStruct((bsz, N_HEADS, 1, seq), F32)],
        input_output_aliases=aliases,
        compiler_params=_cparams(("parallel", "parallel", "parallel")),
    )(*args)


def _gmlp_fn(uv, lng, lnb, ws, bst):
    u = jax.nn.gelu(uv[:, 0:GROUP_WIDTH])
    gv = jax.nn.gelu(uv[:, GROUP_WIDTH:2 * GROUP_WIDTH])
    mu = jnp.mean(gv, axis=-1, keepdims=True)
    vc = gv - mu
    var = jnp.mean(vc * vc, axis=-1, keepdims=True)
    vln = vc * lax.rsqrt(var + LN_EPS) * lng + lnb
    r_i = lax.broadcasted_iota(jnp.int32, (D_CHUNK, D_CHUNK), 0)
    c_i = lax.broadcasted_iota(jnp.int32, (D_CHUNK, D_CHUNK), 1)
    lane_g = lax.broadcasted_iota(jnp.int32, (D_CHUNK, GROUP_WIDTH), 1) // HEAD_DIM
    e_r = lax.broadcasted_iota(jnp.int32, (LANES, GROUP_WIDTH), 0)
    e_c = lax.broadcasted_iota(jnp.int32, (LANES, GROUP_WIDTH), 1)
    expand = (e_r == e_c // HEAD_DIM).astype(F32)
    mixed = jnp.dot(bst, expand, precision=HI, preferred_element_type=F32)
    for g in range(4):
        w = jnp.where(r_i >= c_i, ws[g], 0.0)
        mixed = mixed + jnp.where(lane_g == g, _bdot(w, vln, "nn"), 0.0)
    return u * mixed


def _gmlp_fwd(proj, mo, lng, lnb, ws, bst, name):
    bsz, seq, _ = proj.shape

    def body(p_ref, mo_any, lng_ref, lnb_ref, ws_ref, bst_ref, o_ref):
        del mo_any
        o_ref[...] = _gmlp_fn(p_ref[...], lng_ref[...], lnb_ref[...], ws_ref[...], bst_ref[...]).astype(o_ref.dtype)

    return pl.pallas_call(
        body, name=name, grid=(bsz, seq // D_CHUNK),
        in_specs=[pl.BlockSpec((None, D_CHUNK, 512), lambda b, s: (b, s, P_D // 512)),
                  pl.BlockSpec(memory_space=pl.ANY), _vec_spec(256), _vec_spec(256),
                  pl.BlockSpec((4, D_CHUNK, D_CHUNK), lambda b, s: (0, 0, 0)),
                  pl.BlockSpec((D_CHUNK, LANES), lambda b, s: (0, 0))],
        out_specs=pl.BlockSpec((None, D_CHUNK, GROUP_WIDTH), lambda b, s: (b, s, 1280 // GROUP_WIDTH)),
        out_shape=jax.ShapeDtypeStruct(mo.shape, mo.dtype),
        input_output_aliases={1: 0},
        compiler_params=_cparams(("parallel", "parallel")),
    )(proj, mo, lng, lnb, ws, bst)


def _gmlp_bwd(dmo, dproj, proj, lng, lnb, ws, bst, name):
    bsz, seq, _ = proj.shape

    def body(do_ref, dp_any, p_ref, lng_ref, lnb_ref, ws_ref, bst_ref, dp_ref, dlg_ref, dlb_ref, dws_ref, dbst_ref):
        del dp_any
        first = jnp.logical_and(pl.program_id(0) == 0, pl.program_id(1) == 0)

        @pl.when(first)
        def _():
            dlg_ref[...] = jnp.zeros_like(dlg_ref)
            dlb_ref[...] = jnp.zeros_like(dlb_ref)
            dws_ref[...] = jnp.zeros_like(dws_ref)
            dbst_ref[...] = jnp.zeros_like(dbst_ref)

        _, vjp = jax.vjp(_gmlp_fn, p_ref[...], lng_ref[...], lnb_ref[...], ws_ref[...], bst_ref[...])
        duv, dlg, dlb, dws, dbst = vjp(do_ref[...])
        dp_ref[...] = duv.astype(dp_ref.dtype)
        dlg_ref[...] += dlg
        dlb_ref[...] += dlb
        dws_ref[...] += dws
        dbst_ref[...] += dbst

    const2 = lambda shape: pl.BlockSpec(shape, lambda b, s: (0,) * len(shape))
    return pl.pallas_call(
        body, name=name, grid=(bsz, seq // D_CHUNK),
        in_specs=[pl.BlockSpec((None, D_CHUNK, GROUP_WIDTH), lambda b, s: (b, s, 1280 // GROUP_WIDTH)),
                  pl.BlockSpec(memory_space=pl.ANY),
                  pl.BlockSpec((None, D_CHUNK, 512), lambda b, s: (b, s, P_D // 512)),
                  _vec_spec(256), _vec_spec(256), const2((4, D_CHUNK, D_CHUNK)), const2((D_CHUNK, LANES))],
        out_specs=[pl.BlockSpec((None, D_CHUNK, 512), lambda b, s: (b, s, P_D // 512)),
                   _vec_spec(256), _vec_spec(256), const2((4, D_CHUNK, D_CHUNK)), const2((D_CHUNK, LANES))],
        out_shape=[jax.ShapeDtypeStruct(dproj.shape, dproj.dtype), jax.ShapeDtypeStruct((1, 256), F32),
                   jax.ShapeDtypeStruct((1, 256), F32), jax.ShapeDtypeStruct((4, D_CHUNK, D_CHUNK), F32),
                   jax.ShapeDtypeStruct((D_CHUNK, LANES), F32)],
        input_output_aliases={1: 0},
        compiler_params=_cparams(("arbitrary", "arbitrary")),
    )(dmo, dproj, proj, lng, lnb, ws, bst)


def _ada_fwd(c_all, ada_w, name):
    n_b = c_all.shape[0]
    depth, d, cols = ada_w.shape

    def body(c_ref, w_ref, o_ref):
        cv = c_ref[...]
        act = (cv * jax.nn.sigmoid(cv)).astype(BF16)
        o_ref[...] = jnp.dot(act, w_ref[...].astype(BF16), preferred_element_type=F32)

    return pl.pallas_call(
        body, name=name, grid=(depth,),
        in_specs=[pl.BlockSpec((n_b, d), lambda l: (0, 0)), pl.BlockSpec((None, d, cols), lambda l: (l, 0, 0))],
        out_specs=pl.BlockSpec((None, n_b, cols), lambda l: (l, 0, 0)),
        out_shape=jax.ShapeDtypeStruct((depth, n_b, cols), F32),
        compiler_params=_cparams(("parallel",)),
    )(c_all, ada_w)


def _ada_bwd(c_all, dmod_cols, dmod_full, name):
    n_b, d = c_all.shape
    depth, _, cols = dmod_cols.shape
    full = dmod_full.shape[-1]

    def body(c_ref, dm_ref, df_ref, gw_ref, gb_ref):
        cv = c_ref[...]
        act = (cv * jax.nn.sigmoid(cv)).astype(BF16)
        gw_ref[...] = lax.dot_general(act, dm_ref[...].astype(BF16), (((0,), (0,)), ((), ())),
                                      preferred_element_type=F32)
        gb_ref[...] = jnp.sum(df_ref[...], axis=0, keepdims=True)

    return pl.pallas_call(
        body, name=name, grid=(depth,),
        in_specs=[pl.BlockSpec((n_b, d), lambda l: (0, 0)), pl.BlockSpec((None, n_b, cols), lambda l: (l, 0, 0)),
                  pl.BlockSpec((None, n_b, full), lambda l: (l, 0, 0))],
        out_specs=[pl.BlockSpec((None, d, cols), lambda l: (l, 0, 0)),
                   pl.BlockSpec((None, 1, full), lambda l: (l, 0, 0))],
        out_shape=[jax.ShapeDtypeStruct((depth, d, cols), F32), jax.ShapeDtypeStruct((depth, 1, full), F32)],
        compiler_params=_cparams(("parallel",)),
    )(c_all, dmod_cols, dmod_full)


def _adamw(gparts, own, w, m, v, name, layer=0, prev=None):
    n_p, rows, cols = gparts.shape
    assert w.shape[1:] == (rows, cols)
    tr = rows
    if rows > 512:
        tr = next(c for c in range(512, 7, -8) if rows % c == 0)
    has_own = own is not None
    n_prev = 0 if prev is None else 4

    def body(*refs):
        if has_own:
            slot_ref, refs = refs[0], refs[1:]
        g_ref = refs[0]
        own_ref = refs[1] if has_own else None
        w_ref, m_ref, v_ref = refs[1 + has_own:4 + has_own]
        go_ref, do_ref, mo_ref, vo_ref = refs[4 + has_own + n_prev:]
        g = None
        for p in range(n_p):
            term = g_ref[p].astype(F32)
            if has_own:
                term = jnp.where(slot_ref[0] == p, own_ref[...].astype(F32), term)
            g = term if g is None else g + term
        m_new = ADAM_B1 * m_ref[...] + (1.0 - ADAM_B1) * g
        v_new = ADAM_B2 * v_ref[...] + (1.0 - ADAM_B2) * (g * g)
        m_hat = m_new / (1.0 - ADAM_B1 ** ADAM_STEP)
        v_hat = v_new / (1.0 - ADAM_B2 ** ADAM_STEP)
        go_ref[...] = g
        do_ref[...] = -ADAM_LR * (m_hat / (jnp.sqrt(v_hat) + ADAM_EPS) + ADAM_WD * w_ref[...])
        mo_ref[...] = m_new
        vo_ref[...] = v_new

    spec = pl.BlockSpec((None, tr, cols), lambda i, *_: (layer, i, 0))
    in_specs = [pl.BlockSpec((n_p, tr, cols), lambda i, *_: (0, i, 0))]
    args = [gparts]
    if has_own:
        in_specs.append(pl.BlockSpec((None, tr, cols), lambda i, slot: (slot[0], i, 0)))
        args.append(own[0])
    in_specs += [spec, spec, spec]
    args += [w, m, v]
    aliases = {}
    if prev is not None:
        aliases = {has_own + len(args) + k: k for k in range(4)}
        in_specs += [pl.BlockSpec(memory_space=pl.ANY)] * 4
        args += list(prev)
    shp = jax.ShapeDtypeStruct(w.shape, F32)
    out_specs, out_shape = [spec, spec, spec, spec], [shp, shp, shp, shp]
    if not has_own:
        return pl.pallas_call(
            body, name=name, grid=(rows // tr,), in_specs=in_specs, out_specs=out_specs, out_shape=out_shape,
            input_output_aliases=aliases, compiler_params=_cparams(("parallel",)),
        )(*args)
    return pl.pallas_call(
        body, name=name, out_shape=out_shape, input_output_aliases=aliases,
        grid_spec=pltpu.PrefetchScalarGridSpec(num_scalar_prefetch=1, grid=(rows // tr,), in_specs=in_specs,
                                               out_specs=out_specs),
        compiler_params=_cparams(("parallel",)),
    )(jnp.reshape(own[1], (1,)).astype(jnp.int32), *args)


def _sum_parts(parts, name):
    n_p, rows, cols = parts.shape
    tr = 256 if rows % 256 == 0 else rows

    def body(p_ref, o_ref):
        acc = p_ref[0]
        for p in range(1, n_p):
            acc = acc + p_ref[p]
        o_ref[...] = acc

    return pl.pallas_call(
        body, name=name, grid=(rows // tr,),
        in_specs=[pl.BlockSpec((n_p, tr, cols), lambda i: (0, i, 0))],
        out_specs=pl.BlockSpec((tr, cols), lambda i: (i, 0)),
        out_shape=jax.ShapeDtypeStruct((rows, cols), F32),
        compiler_params=_cparams(("parallel",)),
    )(parts)


def _all_gather(arrs, name):
    n = len(arrs)

    def body(*refs):
        in_refs, out_refs = refs[:n], refs[n:2 * n]
        send_sems, recv_sems, loc_sems = refs[2 * n:]
        x, y, c = lax.axis_index("x"), lax.axis_index("y"), lax.axis_index("c")
        me, sibling = (x, y, c), (x, y, 1 - c)
        chips = [(1 - x, y), (x, 1 - y), (1 - x, 1 - y)]

        def copy(a, k, block, to, src=None):
            slot = out_refs[a].at[4 * block[0] + 2 * block[1] + block[2]]
            return pltpu.make_async_remote_copy(
                src_ref=slot if src is None else src, dst_ref=slot, send_sem=send_sems.at[a, k],
                recv_sem=recv_sems.at[a, k], device_id=to, device_id_type=pl.DeviceIdType.MESH)

        mine = [pltpu.make_async_copy(in_refs[a], out_refs[a].at[4 * x + 2 * y + c], loc_sems.at[a])
                for a in range(n)]
        for cp in mine:
            cp.start()
        first = []
        for a in range(n):
            first.append(copy(a, 0, me, sibling, src=in_refs[a]))
            first += [copy(a, 1 + j, me, (*chip, c), src=in_refs[a]) for j, chip in enumerate(chips)]
        for cp in first:
            cp.start()
        passed = []
        for j, chip in enumerate(chips):
            for a in range(n):
                copy(a, 1 + j, (*chip, c), me).wait_recv()
                cp = copy(a, 4 + j, (*chip, c), sibling)
                cp.start()
                passed.append(cp)
        for a in range(n):
            copy(a, 0, sibling, me).wait_recv()
        for j, chip in enumerate(chips):
            for a in range(n):
                copy(a, 4 + j, (*chip, 1 - c), me).wait_recv()
        for cp in first + passed:
            cp.wait_send()
        for cp in mine:
            cp.wait()

    any_spec = pl.BlockSpec(memory_space=pl.ANY)
    return pl.pallas_call(
        body, name=name, in_specs=[any_spec] * n, out_specs=[any_spec] * n,
        out_shape=[jax.ShapeDtypeStruct((N_DEV,) + a.shape, a.dtype) for a in arrs],
        scratch_shapes=[pltpu.SemaphoreType.DMA((n, N_DEV - 1)), pltpu.SemaphoreType.DMA((n, N_DEV - 1)),
                        pltpu.SemaphoreType.DMA((n,))],
    )(*arrs)


def _flip_peers():
    x, y, c = lax.axis_index("x"), lax.axis_index("y"), lax.axis_index("c")
    peers = []
    for fx, fy, fc in [(fx, fy, fc) for fx in (0, 1) for fy in (0, 1) for fc in (0, 1)][1:]:
        px, py, pc = (1 - x if fx else x), (1 - y if fy else y), (1 - c if fc else c)
        peers.append(((px, py, pc), 4 * px + 2 * py + pc))
    return 4 * x + 2 * y + c, peers


def _push_start(srcs, name, whole=False):
    n, n_peer = len(srcs), N_DEV - 1
    if whole:
        me_w = 4 * lax.axis_index("x") + 2 * lax.axis_index("y") + lax.axis_index("c")
        lands = [lax.dynamic_update_slice_in_dim(lax.empty((N_DEV,) + a.shape, a.dtype), a[None], me_w, axis=0)
                 for a in srcs]
    else:
        lands = [lax.empty(a.shape, a.dtype) for a in srcs]

    def body(*refs):
        src_refs, land_refs = refs[:n], refs[n:2 * n]
        send_sems, recv_sems = refs[2 * n], refs[2 * n + 1]
        token = refs[-1]
        me, peers = _flip_peers()
        for k, (dev, idx) in enumerate(peers):
            for a in range(n):
                pltpu.make_async_remote_copy(
                    src_ref=src_refs[a] if whole else src_refs[a].at[idx], dst_ref=land_refs[a].at[me],
                    send_sem=send_sems.at[a * n_peer + k], recv_sem=recv_sems.at[a * n_peer + k], device_id=dev,
                    device_id_type=pl.DeviceIdType.MESH).start()
        token[...] = jnp.zeros_like(token)

    hbm = pl.BlockSpec(memory_space=pltpu.HBM)
    sem = pl.BlockSpec(memory_space=pltpu.SEMAPHORE)
    arrs = list(srcs) + lands
    res = pl.pallas_call(
        body, name=name, in_specs=[hbm] * (2 * n),
        out_specs=(sem, sem, *[hbm] * (2 * n), pl.BlockSpec(memory_space=pltpu.VMEM)),
        out_shape=(pltpu.SemaphoreType.DMA((n * n_peer,)), pltpu.SemaphoreType.DMA((n * n_peer,)),
                   *[pltpu.HBM(a.shape, a.dtype) for a in arrs], jax.ShapeDtypeStruct((8, LANES), F32)),
        input_output_aliases={i: 2 + i for i in range(2 * n)},
        compiler_params=pltpu.CompilerParams(has_side_effects=pltpu.SideEffectType.DATAFLOW_SIDE_EFFECTING),
    )(*[pltpu.with_memory_space_constraint(a, pltpu.HBM) for a in arrs])
    return res[0], res[1], list(res[2:2 + n]), list(res[2 + n:2 + 2 * n]), res[-1]


def _push_wait(send_sems, recv_sems, srcs, lands, after, name, whole=False):
    n, n_peer = len(srcs), N_DEV - 1

    def body(*refs):
        src_refs, land_refs = refs[:n], refs[n:2 * n]
        send_s, recv_s = refs[2 * n], refs[2 * n + 1]
        _, peers = _flip_peers()
        for k, (dev, idx) in enumerate(peers):
            for a in range(n):
                cp = pltpu.make_async_remote_copy(
                    src_ref=src_refs[a] if whole else src_refs[a].at[idx], dst_ref=land_refs[a].at[idx],
                    send_sem=send_s.at[a * n_peer + k],
                    recv_sem=recv_s.at[a * n_peer + k], device_id=dev, device_id_type=pl.DeviceIdType.MESH)
                cp.wait_send()
                cp.wait_recv()

    hbm = pl.BlockSpec(memory_space=pltpu.HBM)
    sem = pl.BlockSpec(memory_space=pltpu.SEMAPHORE)
    arrs = list(srcs) + list(lands)
    res = pl.pallas_call(
        body, name=name, in_specs=[hbm] * (2 * n) + [sem, sem, pl.BlockSpec(memory_space=pl.ANY)],
        out_specs=tuple([hbm] * (2 * n)), out_shape=tuple(pltpu.HBM(a.shape, a.dtype) for a in arrs),
        input_output_aliases={i: i for i in range(2 * n)},
        compiler_params=pltpu.CompilerParams(has_side_effects=pltpu.SideEffectType.DATAFLOW_SIDE_EFFECTING),
    )(*arrs, send_sems, recv_sems, after)
    return list(res[:n]), list(res[n:])


def _ffn_fwd(x, h, mod, w_in, w_out_after, lng, lnb, rows, tag, nxt):
    bsz, seq, d = x.shape
    t = bsz * seq
    if h is None:
        h = _modulate(x, mod, rows[0], rows[1], f"modulate_{tag}")
    z, a = _ffn_in_swiglu(h.reshape(t, d), w_in, f"ffn_in_{tag}")
    f = _matmul_groupsum(a, w_out_after(a), out_dtype=F32, tm=512, name=f"ffn_out_{tag}").reshape(bsz, seq, d)
    y, h_next = _res_ln(x, f, mod, lng, lnb, rows[2], 0.5, f"res_ln_{tag}", nxt)
    return y, h_next, (x, h, z, a, f)


def _tied(mod, tie):
    return mod if tie is None else mod + tie


def _open_tail(tail):
    dh, x, mod, dx_res, sc_row = tail
    return dx_res, (dh, x, mod, sc_row)


def _ffn_bwd(dy, pre, saved, mod, w_in, w_out, lng, lnb, rows, tag, ready):
    x, h, z, a, f = saved
    bsz, seq, d = x.shape
    t = bsz * seq
    (dx_res, df, dgate, dlg, dlb), closed = _res_ln_bwd(dy, x, f, mod, lng, lnb, rows[2], 0.5,
                                                       f"res_ln_bwd_{tag}", pre)
    df2 = df.reshape(1, t, d)
    dw_out = _matmul(a, df2, mode="tn", group_out=True, out_dtype=BF16, tm=a.shape[2], tk=min(t, 2048),
                     name=f"ffn_out_dw_{tag}")
    tie_out = ready(f"{tag}_out", dw_out)
    dz = _ffn_out_dx_swiglu(df.reshape(t, d), w_out, z, f"ffn_out_dx_{tag}").reshape(N_DEV, t, -1)
    dw_in = _matmul(dz, h.reshape(1, t, d), mode="tn", group_out=True, out_dtype=BF16, tm=dz.shape[2],
                    tk=min(t, 2048), name=f"ffn_in_dw_{tag}")
    tie_in = ready(f"{tag}_in", dw_in)
    dh = _matmul_groupsum(dz, w_in, out_dtype=F32, tm=512, name=f"ffn_in_dx_{tag}").reshape(bsz, seq, d)
    tail = (dh, x, _tied(_tied(mod, tie_out), tie_in), dx_res, rows[1])
    return tail, closed, dgate, dw_in, dw_out, dlg, dlb


def _mixer_fwd(x, h, mod, wts, small, lng, lnb, layer, tabs):
    bsz, seq, d = x.shape
    t = bsz * seq
    proj = _matmul(h.reshape(1, t, d), wts["mix_in"][None], mode="nn", group_out=True, out_dtype=F32, tm=512, tk=d,
                   name="mix_in").reshape(bsz, seq, PACK_W)
    mo, states = _hgrn_fwd(proj, small["lb_logits8"], small["hgrn_norm_g"], layer, f"hgrn_fwd_l{layer}")
    q, kv = _mla_pre(proj, small["q_norm_g"], small["kv_norm_g"], wts["uq"], wts["ukv"], tabs, "mla_pre")
    mla_scale = float((B_NOPE + B_ROPE) ** -0.5)
    mo, lse_b = _attn_fwd_loop(q, 0, kv, 0, mo, 2, None, mla_scale, "mla_attn_fwd")
    fg = _fox_gate(proj, small["fox_b_f"], "fox_gate")
    gates = (fg, jnp.swapaxes(fg[:, :, 0:8], 1, 2))
    fox_scale = float(HEAD_DIM ** -0.5)
    mo, lse_c = _attn_fwd_loop(proj, P_CQ // LANES, proj, P_CKV // LANES, mo, 6, gates, fox_scale, "fox_attn_fwd")
    mo = _gmlp_fwd(proj, mo, small["gmlp_ln_g"], small["gmlp_ln_b"], small["gmlp_w_s"], small["gmlp_bst"],
                   "gmlp_fwd")
    mixed = _matmul(mo.reshape(1, t, MO_W), wts["mix_out"][None], mode="nn", group_out=True, out_dtype=F32,
                    tm=1024, tk=MO_W, name="mix_out").reshape(bsz, seq, d)
    y, h_next = _res_ln(x, mixed, mod, lng, lnb, 5, 1.0, "res_ln_mix", (mod, 6, 7))
    return y, h_next, (x, h, proj, mo, states, q, kv, lse_b, gates, lse_c, mixed)


def _mixer_bwd(dy, pre, saved, mod, wts, small, lng, lnb, layer, tabs, ready):
    x, h, proj, mo, states, q, kv, lse_b, gates, lse_c, mixed = saved
    bsz, seq, d = x.shape
    t = bsz * seq
    (dx_res, dmixed, dgate, dlg, dlb), closed = _res_ln_bwd(dy, x, mixed, mod, lng, lnb, 5, 1.0, "res_ln_bwd_mix",
                                                           pre)
    dm2 = dmixed.reshape(1, t, d)
    dmo = _matmul(dm2, wts["mix_out"][None], mode="nt", group_out=True, out_dtype=F32, tm=1024, tk=d,
                  name="mix_out_dx").reshape(bsz, seq, MO_W)
    dw_out = _matmul(mo.reshape(1, t, MO_W), dm2, mode="tn", group_out=True, out_dtype=F32, tm=512, tk=min(t, 2048),
                     name="mix_out_dw")[0]
    tie_out = ready("mix_out", dw_out)
    g = {}
    dproj, g["lb_logits8"], g["hgrn_norm_g"] = _hgrn_bwd(dmo, proj, states, small["lb_logits8"],
                                                         small["hgrn_norm_g"], layer, f"hgrn_bwd_l{layer}")
    mla_scale = float((B_NOPE + B_ROPE) ** -0.5)
    dq, delta_b, _ = _attn_bwd_q_loop(q, 0, kv, 0, mo, dmo, 2, lse_b, None, mla_scale,
                                 jax.ShapeDtypeStruct((bsz, seq, 512), F32), 0, "mla_attn_bwd_q")
    dkv, _ = _attn_bwd_kv_loop(q, 0, kv, 0, dmo, 2, lse_b, delta_b, None, mla_scale,
                          jax.ShapeDtypeStruct((bsz, seq, 1024), F32), 0, "mla_attn_bwd_kv")
    dproj, g["q_norm_g"], g["kv_norm_g"], g["uq"], g["ukv"] = _mla_pre_bwd(
        dq, dkv, dproj, proj, small["q_norm_g"], small["kv_norm_g"], wts["uq"], wts["ukv"], tabs, "mla_pre_bwd")
    ready("mla_uq", g.pop("uq"))
    ready("mla_ukv", g.pop("ukv"))
    fox_scale = float(HEAD_DIM ** -0.5)
    dproj, delta_c, dfq = _attn_bwd_q_loop(proj, P_CQ // LANES, proj, P_CKV // LANES, mo, dmo, 6, lse_c, gates,
                                      fox_scale, dproj, P_CQ // LANES, "fox_attn_bwd_q")
    dproj, dfk = _attn_bwd_kv_loop(proj, P_CQ // LANES, proj, P_CKV // LANES, dmo, 6, lse_c, delta_c, gates, fox_scale,
                              dproj, P_CKV // (2 * LANES), "fox_attn_bwd_kv")
    dfk_cols = jnp.pad(jnp.swapaxes(dfk[:, :, 0, :], 1, 2), ((0, 0), (0, 0), (0, LANES - N_HEADS)))
    dproj, g["fox_b_f"] = _fox_gate_bwd(dfq, dfk_cols, dproj, proj, small["fox_b_f"], "fox_gate_bwd")
    dproj, g["gmlp_ln_g"], g["gmlp_ln_b"], g["gmlp_w_s"], g["gmlp_bst"] = _gmlp_bwd(
        dmo, dproj, proj, small["gmlp_ln_g"], small["gmlp_ln_b"], small["gmlp_w_s"], small["gmlp_bst"], "gmlp_bwd")
    dp2 = dproj.reshape(1, t, PACK_W)
    dw_in = _matmul(h.reshape(1, t, d), dp2, mode="tn", group_out=True, out_dtype=BF16, tm=512, tk=1024,
                    name="mix_in_dw")[0]
    tie_in = ready("mix_in", dw_in)
    dh = _matmul(dp2, wts["mix_in"][None], mode="nt", group_out=True, out_dtype=F32, tm=512, tk=PACK_W,
                 name="mix_in_dx").reshape(bsz, seq, d)
    tail = (dh, x, _tied(_tied(mod, tie_out), tie_in), dx_res, 4)
    return tail, closed, dgate, dw_in, dw_out, g, dlg, dlb


def _small_views(p, layer):
    return {
        "lb_logits8": jnp.pad(p["hgrn_lb_logits"], ((0, 8 - DEPTH), (0, 0))),
        "hgrn_norm_g": p["hgrn_norm_g"][layer][None],
        "q_norm_g": p["mla_q_norm_g"][layer][None],
        "kv_norm_g": p["mla_kv_norm_g"][layer][None],
        "fox_b_f": jnp.pad(p["fox_b_f"][layer][None], ((0, 0), (0, LANES - N_HEADS))),
        "gmlp_ln_g": p["gmlp_ln_g"][layer][None],
        "gmlp_ln_b": p["gmlp_ln_b"][layer][None],
        "gmlp_w_s": p["gmlp_w_s"][layer],
        "gmlp_bst": jnp.pad(p["gmlp_b_s"][layer].T, ((0, 0), (0, LANES - N_HEADS))),
    }


def _local_step(x, mod, target, weights, p, grads_ready=None):
    bsz, seq, d = x.shape
    tabs = _rope_tables(seq)
    saved = []
    h = None
    for l in range(DEPTH):
        sm = _small_views(p, l)
        lng, lnb = p["ln_g"][l], p["ln_b"][l]
        x, h, s1 = _ffn_fwd(x, h, mod[l], weights(l, "ffn1_in", x)["ffn1_in"],
                            lambda a, l=l: weights(l, "ffn1_out", a)["ffn1_out"], lng[0:1], lnb[0:1], (0, 1, 2),
                            "ffn1", (mod[l], 3, 4))
        x, h, s2 = _mixer_fwd(x, h, mod[l], weights(l, "mix", x), sm, lng[1:2], lnb[1:2], l, tabs)
        x, h, s3 = _ffn_fwd(x, h, mod[l], weights(l, "ffn2_in", x)["ffn2_in"],
                            lambda a, l=l: weights(l, "ffn2_out", a)["ffn2_out"], lng[2:3], lnb[2:3], (6, 7, 8),
                            "ffn2", (mod[l + 1], 0, 1) if l + 1 < DEPTH else None)
        saved.append((s1, s2, s3))
    dx, loss = _loss_head(x, target, "loss_head")
    big, small, dmods = [None] * DEPTH, [None] * DEPTH, [None] * DEPTH
    ties = []
    tail, rows_of = None, {}

    def tied(a):
        for t in ties:
            a = a + t
        return a

    for l in reversed(range(DEPTH)):
        w = {}
        for part in ("ffn1_in", "ffn1_out", "mix", "ffn2_in", "ffn2_out"):
            w.update(weights(l, part, None))
        sm = _small_views(p, l)
        lng, lnb = p["ln_g"][l], p["ln_b"][l]
        s1, s2, s3 = saved[l]

        def ready(name, grad, l=l):
            tie = None if grads_ready is None else grads_ready(l, name, grad)
            if tie is not None:
                ties.append(tie)
            return tie

        dy, pre = (dx, None) if tail is None else _open_tail(tail)
        tail, closed, dgate3, dwi2, dwo2, dlg2, dlb2 = _ffn_bwd(dy, pre, s3, tied(mod[l]), w["ffn2_in"],
                                                                w["ffn2_out"], lng[2:3], lnb[2:3], (6, 7, 8), "ffn2",
                                                                ready)
        if closed is not None:
            rows_of[(l + 1, 0)], rows_of[(l + 1, 1)] = closed
        dy, pre = _open_tail(tail)
        tail, closed, dgate2, dwmi, dwmo, g, dlg1, dlb1 = _mixer_bwd(dy, pre, s2, tied(mod[l]), w, sm, lng[1:2],
                                                                     lnb[1:2], l, tabs, ready)
        rows_of[(l, 6)], rows_of[(l, 7)] = closed
        dy, pre = _open_tail(tail)
        tail, closed, dgate1, dwi1, dwo1, dlg0, dlb0 = _ffn_bwd(dy, pre, s1, tied(mod[l]), w["ffn1_in"],
                                                                w["ffn1_out"], lng[0:1], lnb[0:1], (0, 1, 2), "ffn1",
                                                                ready)
        rows_of[(l, 3)], rows_of[(l, 4)] = closed
        rows_of[(l, 2)], rows_of[(l, 5)], rows_of[(l, 8)] = dgate1, dgate2, dgate3
        big[l] = {"ffn1_in": dwi1, "ffn1_out": dwo1, "ffn2_in": dwi2, "ffn2_out": dwo2, "mix_in": dwmi,
                  "mix_out": dwmo}
        g["ln_g"] = jnp.concatenate([dlg0, dlg1, dlg2], axis=0)
        g["ln_b"] = jnp.concatenate([dlb0, dlb1, dlb2], axis=0)
        small[l] = g
    dh, x0, mod0, dx_res, sc_row = tail
    dx, rows_of[(0, 0)], rows_of[(0, 1)] = _modulate_bwd(dh, x0, mod0, dx_res, sc_row, "modulate_bwd_ffn1")
    dmods = [jnp.concatenate([rows_of[(l, r)] for r in range(N_MOD)], axis=1) for l in range(DEPTH)]
    return loss, dx, jnp.stack(dmods), big, small


_BIG = ("ffn1_in", "ffn1_out", "ffn2_in", "ffn2_out", "mix_in", "mix_out")


def _small_grad_list(small, loss):
    def both(fn):
        return jnp.stack([fn(small[l]) for l in range(DEPTH)])

    return [
        ("loss", loss.reshape(1)),
        ("ln_g", both(lambda g: g["ln_g"])), ("ln_b", both(lambda g: g["ln_b"])),
        ("hgrn_lb_logits", small[0]["lb_logits8"][:DEPTH] + small[1]["lb_logits8"][:DEPTH]),
        ("hgrn_norm_g", both(lambda g: g["hgrn_norm_g"][0])),
        ("mla_q_norm_g", both(lambda g: g["q_norm_g"][0])),
        ("mla_kv_norm_g", both(lambda g: g["kv_norm_g"][0])),
        ("fox_b_f", both(lambda g: g["fox_b_f"][0, :N_HEADS])),
        ("gmlp_ln_g", both(lambda g: g["gmlp_ln_g"][0])), ("gmlp_ln_b", both(lambda g: g["gmlp_ln_b"][0])),
        ("gmlp_w_s", both(lambda g: g["gmlp_w_s"])),
        ("gmlp_b_s", both(lambda g: g["gmlp_bst"][:, :N_HEADS].T)),
    ]


_PACK_COLS = 512


def _pack_small(items):
    flat = jnp.concatenate([a.reshape(-1).astype(F32) for _, a in items])
    n = flat.shape[0]
    tile = 8 * _PACK_COLS
    flat = jnp.pad(flat, (0, (-n) % tile))
    return flat.reshape(-1, _PACK_COLS)


def _unpack_small(buf, items):
    flat = buf.reshape(-1)
    out, off = {}, 0
    for name, a in items:
        out[name] = flat[off:off + a.size].reshape(a.shape)
        off += a.size
    return out


def _as2d(a):
    return a.reshape(-1, a.shape[-1])


def kernel(x, c, ada_w, ada_b, ln_g, ln_b, ffn1_w_in, ffn1_w_out, ffn2_w_in, ffn2_w_out, mix_w_in, mix_w_out, hgrn_lb_logits, hgrn_norm_g, mla_q_norm_g, mla_kv_norm_g, mla_w_uq, mla_w_ukv, fox_b_f, gmlp_ln_g, gmlp_ln_b, gmlp_w_s, gmlp_b_s, loss_target, m_ada_w, m_ada_b, m_ln_g, m_ln_b, m_ffn1_w_in, m_ffn1_w_out, m_ffn2_w_in, m_ffn2_w_out, m_mix_w_in, m_mix_w_out, m_hgrn_lb_logits, m_hgrn_norm_g, m_mla_q_norm_g, m_mla_kv_norm_g, m_mla_w_uq, m_mla_w_ukv, m_fox_b_f, m_gmlp_ln_g, m_gmlp_ln_b, m_gmlp_w_s, m_gmlp_b_s, v_ada_w, v_ada_b, v_ln_g, v_ln_b, v_ffn1_w_in, v_ffn1_w_out, v_ffn2_w_in, v_ffn2_w_out, v_mix_w_in, v_mix_w_out, v_hgrn_lb_logits, v_hgrn_norm_g, v_mla_q_norm_g, v_mla_kv_norm_g, v_mla_w_uq, v_mla_w_ukv, v_fox_b_f, v_gmlp_ln_g, v_gmlp_ln_b, v_gmlp_w_s, v_gmlp_b_s):
    names = ["ada_w", "ada_b", "ln_g", "ln_b", "ffn1_w_in", "ffn1_w_out", "ffn2_w_in", "ffn2_w_out", "mix_w_in",
             "mix_w_out", "hgrn_lb_logits", "hgrn_norm_g", "mla_q_norm_g", "mla_kv_norm_g", "mla_w_uq", "mla_w_ukv",
             "fox_b_f", "gmlp_ln_g", "gmlp_ln_b", "gmlp_w_s", "gmlp_b_s"]
    w = dict(zip(names, [ada_w, ada_b, ln_g, ln_b, ffn1_w_in, ffn1_w_out, ffn2_w_in, ffn2_w_out, mix_w_in, mix_w_out,
                         hgrn_lb_logits, hgrn_norm_g, mla_q_norm_g, mla_kv_norm_g, mla_w_uq, mla_w_ukv, fox_b_f,
                         gmlp_ln_g, gmlp_ln_b, gmlp_w_s, gmlp_b_s]))
    m = dict(zip(names, [m_ada_w, m_ada_b, m_ln_g, m_ln_b, m_ffn1_w_in, m_ffn1_w_out, m_ffn2_w_in, m_ffn2_w_out,
                         m_mix_w_in, m_mix_w_out, m_hgrn_lb_logits, m_hgrn_norm_g, m_mla_q_norm_g, m_mla_kv_norm_g,
                         m_mla_w_uq, m_mla_w_ukv, m_fox_b_f, m_gmlp_ln_g, m_gmlp_ln_b, m_gmlp_w_s, m_gmlp_b_s]))
    v = dict(zip(names, [v_ada_w, v_ada_b, v_ln_g, v_ln_b, v_ffn1_w_in, v_ffn1_w_out, v_ffn2_w_in, v_ffn2_w_out,
                         v_mix_w_in, v_mix_w_out, v_hgrn_lb_logits, v_hgrn_norm_g, v_mla_q_norm_g, v_mla_kv_norm_g,
                         v_mla_w_uq, v_mla_w_ukv, v_fox_b_f, v_gmlp_ln_g, v_gmlp_ln_b, v_gmlp_w_s, v_gmlp_b_s]))
    bsz, seq, d = x.shape
    me = 4 * lax.axis_index("x") + 2 * lax.axis_index("y") + lax.axis_index("c")
    mix_src, uq_src, ukv_src, mo_src = _mix_in_src(), _uq_src(), _ukv_src(), _mo_src()

    part_names = {"ffn1_in": ["ffn1_w_in"], "ffn1_out": ["ffn1_w_out"],
                  "mix": ["mix_w_in", "mix_w_out", "mla_w_uq", "mla_w_ukv"],
                  "ffn2_in": ["ffn2_w_in"], "ffn2_out": ["ffn2_w_out"]}
    group_of = {(l, part): (l, part) for l in range(DEPTH) for part in part_names}
    in_flight = {}
    transposed = ("ffn1_w_in", "ffn2_w_in")

    def start_group(key, behind=None):
        members = [(l, part) for (l, part), g in group_of.items() if g == key]
        labels = [(l, n) for l, part in members for n in part_names[part]]
        shards = []
        for l, n in labels:
            a = w[n][l]
            if n == "mix_w_in":
                a = _pack_cols(a, mix_src)
            if n in transposed:
                a = jnp.swapaxes(w[n], 1, 2)[l]
            shards.append(a.astype(BF16))
        if behind is not None:
            shards, _ = lax.optimization_barrier((shards, behind))
        in_flight[key] = (labels, _push_start(shards, f"gather_start_{key[0]}_{key[1]}", whole=True))

    keys_in_order = list(dict.fromkeys(group_of.values()))
    start_group(keys_in_order[0])

    gathered = _all_gather([c, ln_g, ln_b], "gather_inputs")
    c_all = gathered[0].reshape(N_DEV * bsz, d)
    ln_g_full = jnp.moveaxis(gathered[1], 0, 2).reshape(DEPTH, 3, d)
    ln_b_full = jnp.moveaxis(gathered[2], 0, 2).reshape(DEPTH, 3, d)

    mod_cols = _ada_fwd(c_all, ada_w, "ada_fwd")
    mod_all, = _all_gather([mod_cols], "gather_mod")
    mod_mine = lax.dynamic_slice_in_dim(mod_all, me * bsz, bsz, axis=2)
    mod = jnp.moveaxis(mod_mine, 0, 2).reshape(DEPTH, bsz, N_MOD * d) + ada_b[:, None, :]
    for key in keys_in_order[1:]:
        start_group(key, behind=mod)
    tie = sum(h[-1][0, 0] for _, h in in_flight.values())
    mod = mod.reshape(DEPTH, bsz, N_MOD, d) + tie

    arrived, laid_out = {}, {}

    def weights(l, part, after):
        if (l, part) not in laid_out:
            laid_out[(l, part)] = lay_out(l, part, after)
        return laid_out[(l, part)]

    def lay_out(l, part, after):
        key = group_of[(l, part)]
        if key not in arrived:
            labels, (send_sems, recv_sems, srcs, lands, _) = in_flight[key]
            _, lands = _push_wait(send_sems, recv_sems, srcs, lands, after, f"gather_wait_{key[0]}_{key[1]}",
                                  whole=True)
            arrived[key] = dict(zip(labels, lands))
        gw = {n: arrived[key][(l, n)] for n in part_names[part]}
        if part.endswith("_in"):
            return {part: gw[part_names[part][0]]}
        if part.endswith("_out"):
            return {part: gw[part_names[part][0]].reshape(4, 704, d)}
        uq = jnp.moveaxis(gw["mla_w_uq"], 0, 1).reshape(256, 384)
        ukv = jnp.moveaxis(gw["mla_w_ukv"], 0, 1).reshape(128, 512)
        return {"mix_in": gw["mix_w_in"].reshape(d, PACK_W),
                "mix_out": _pack_cols(gw["mix_w_out"].reshape(d, d).T, mo_src).T,
                "uq": _pack_cols(uq, uq_src), "ukv": _pack_cols(ukv, ukv_src)}

    p = dict(w)
    p["ln_g"], p["ln_b"] = ln_g_full, ln_b_full
    def chunks(name, arr):
        if name in ("ffn1_in", "ffn2_in"):
            return arr
        if name in ("ffn1_out", "ffn2_out"):
            return arr.reshape(N_DEV, arr.shape[1] // 2, d)
        if name == "mix_in":
            return _unpack_cols(arr, mix_src, MIX_ORIG_W).reshape(N_DEV, d // N_DEV, MIX_ORIG_W)
        if name in ("mla_uq", "mla_ukv"):
            full_w = _unpack_cols(arr, uq_src, 384) if name == "mla_uq" else _unpack_cols(arr, ukv_src, 512)
            rows = full_w.shape[0]
            return jnp.moveaxis(full_w.reshape(rows, N_DEV, -1), 1, 0).astype(BF16)
        return _unpack_cols(arr.T, mo_src, d).T.astype(BF16).reshape(N_DEV, d // N_DEV, d)

    pending, started = {}, []

    def grads_ready(l, name, grad):
        pending[(name, l)] = chunks(name, grad)
        flush = name == "ffn1_in" if l > 0 else name in ("ffn2_in", "mix_out", "mix_in", "ffn1_out", "ffn1_in")
        if not flush:
            return None
        keys = sorted(pending)
        handles = _push_start([pending[k] for k in keys], f"push_start_{len(started)}")
        pending.clear()
        started.append((keys, handles, l == 0 and name.startswith("ffn1")))
        return handles[-1][0, 0]

    loss, grad_x, dmod, big, small = _local_step(x, mod, loss_target, weights, p, grads_ready)
    del big

    recv, out = {}, {}

    def arrive(n, after):
        keys, (send_sems, recv_sems, srcs, lands, _), _ = started[n]
        srcs, lands = _push_wait(send_sems, recv_sems, srcs, lands, after, f"push_wait_{n}")
        for k, src, land in zip(keys, srcs, lands):
            recv[k] = (land, src)

    big_of = {"ffn1_w_in": "ffn1_in", "ffn1_w_out": "ffn1_out", "ffn2_w_in": "ffn2_in", "ffn2_w_out": "ffn2_out",
              "mix_w_in": "mix_in", "mix_w_out": "mix_out", "mla_w_uq": "mla_uq", "mla_w_ukv": "mla_ukv"}
    chain = {name: None for name in big_of}

    def big_update(key, l):
        name = next(nm for nm, k in big_of.items() if k == key)
        parts, src = recv[(key, l)]
        view =(lambda a: jnp.swapaxes(a, 1, 2)) if name in transposed else (lambda a: a)
        chain[name] = _adamw(parts, (src, me), view(w[name]), view(m[name]), view(v[name]), f"adamw_{name}_l{l}",
                             layer=l, prev=chain[name])

    def update(name, grad):
        shape = w[name].shape
        as3 = lambda a: a.reshape(1, -1, shape[-1])
        res = _adamw(as3(grad), None, as3(w[name]), as3(m[name]), as3(v[name]), f"adamw_{name}")
        out[name] = tuple(r.reshape(shape) for r in res)

    for n, (keys, _, last) in enumerate(started):
        if not last:
            arrive(n, grad_x)
            for key, l in keys:
                big_update(key, l)

    dmod_flat = dmod.reshape(DEPTH, bsz, N_MOD * d)
    done = [r[0] for r in chain.values() if r is not None]
    if done:
        dmod_flat, _ = lax.optimization_barrier((dmod_flat, done))
    dmod_all, = _all_gather([dmod_flat], "gather_dmod")
    dmod_full = jnp.moveaxis(dmod_all, 0, 1).reshape(DEPTH, N_DEV * bsz, N_MOD * d)
    cols = ada_w.shape[2]
    dmod_cols = lax.dynamic_slice_in_dim(dmod_full, me * cols, cols, axis=2)
    g_ada_w, g_ada_b = _ada_bwd(c_all, dmod_cols, dmod_full, "ada_bwd")
    res = None
    for l in range(DEPTH):
        res = _adamw(g_ada_w[l][None], None, ada_w, m_ada_w, v_ada_w, f"adamw_ada_w_l{l}", layer=l, prev=res)
    out["ada_w"] = tuple(res)
    update("ada_b", g_ada_b.reshape(DEPTH, N_MOD * d))

    items = _small_grad_list(small, loss)
    packed, _ = lax.optimization_barrier((_pack_small(items), (grad_x, g_ada_b)))
    parts, = _all_gather([packed], "gather_small")
    sg = _unpack_small(_sum_parts(parts, "sum_small"), items)
    for name in ("ln_g", "ln_b"):
        update(name, lax.dynamic_slice_in_dim(sg[name], me * (d // N_DEV), d // N_DEV, axis=2))
    for name in ("hgrn_lb_logits", "hgrn_norm_g", "mla_q_norm_g", "mla_kv_norm_g", "fox_b_f", "gmlp_ln_g",
                 "gmlp_ln_b", "gmlp_w_s", "gmlp_b_s"):
        update(name, sg[name])

    for n, (keys, _, last) in enumerate(started):
        if last:
            arrive(n, out["gmlp_w_s"][0])
            for key, l in keys:
                big_update(key, l)
    for name in big_of:
        out[name] = tuple(jnp.swapaxes(r, 1, 2) if name in transposed else r for r in chain[name])

    return (sg["loss"][0], grad_x, *[out[n][0] for n in names], *[out[n][1] for n in names],
            *[out[n][2] for n in names], *[out[n][3] for n in names])
```

```python
import functools

import numpy as np
import jax
import jax.numpy as jnp
from jax import lax
from jax.experimental import pallas as pl
from jax.experimental.pallas import tpu as pltpu

F32 = jnp.float32
BF16 = jnp.bfloat16
HI = lax.Precision.HIGHEST

D_MODEL = 1024
DEPTH = 2
GROUP_WIDTH = 256
N_HEADS = 4
HEAD_DIM = 64
A_CHUNK = 16
LB_FLOOR = 1e-30
B_NOPE = 64
B_ROPE = 32
ROPE_THETA = 10000.0
D_CHUNK = 128
D_FF = 2816
N_MOD = 9
ALPHA = (2 * DEPTH) ** 0.25
LN_EPS = 1e-5
RMS_EPS = 1e-6
ADAM_LR = 0.001
ADAM_B1 = 0.9
ADAM_B2 = 0.999
ADAM_EPS = 1e-08
ADAM_WD = 0.01
ADAM_STEP = 10

N_DEV = 8
LANES = 128
PACK_W = 3712
MO_W = 1536
VMEM_LIMIT = 56 * 1024 * 1024
NEG = -1e30
ATTN_TILE = 1024
GMLP_STEP = 4

MIX_ORIG_W = 2724
O_BCQ, O_BCKV, O_BKR, O_CQ, O_CK, O_CV, O_CF, O_DU, O_DV = 1024, 1280, 1408, 1440, 1696, 1952, 2208, 2212, 2468
P_B, P_KR, P_CQ, P_CKV, P_D, P_CF = 1024, 1408, 1536, 2048, 3072, 3584


_DN = {"nn": (((1,), (0,)), ((), ())), "nt": (((1,), (1,)), ((), ())), "tn": (((0,), (0,)), ((), ()))}


def _raw_bdot(a, b, mode):
    return lax.dot_general(a.astype(BF16), b.astype(BF16), _DN[mode], preferred_element_type=F32)


@functools.partial(jax.custom_vjp, nondiff_argnums=(2,))
def _bdot(a, b, mode):
    return _raw_bdot(a, b, mode)


def _bdot_fwd(a, b, mode):
    return _raw_bdot(a, b, mode), (a, b)


def _bdot_bwd(mode, res, g):
    a, b = res
    if mode == "nn":
        return _raw_bdot(g, b, "nt"), _raw_bdot(a, g, "tn")
    if mode == "nt":
        return _raw_bdot(g, b, "nn"), _raw_bdot(g, a, "tn")
    return _raw_bdot(b, g, "nt"), _raw_bdot(a, g, "nn")


_bdot.defvjp(_bdot_fwd, _bdot_bwd)


def _cparams(sem):
    return pltpu.CompilerParams(dimension_semantics=sem, vmem_limit_bytes=VMEM_LIMIT)


def _mix_in_src():
    src = -np.ones(PACK_W, np.int64)
    src[0:P_KR] = np.arange(0, O_BKR)
    src[P_KR + 64:P_KR + 80] = O_BKR + np.arange(16)
    src[P_KR + 96:P_KR + 112] = O_BKR + 16 + np.arange(16)
    for h in range(N_HEADS):
        src[P_CQ + 128 * h:P_CQ + 128 * h + 64] = O_CQ + 64 * h + np.arange(64)
        src[P_CKV + 256 * h:P_CKV + 256 * h + 64] = O_CK + 64 * h + np.arange(64)
        src[P_CKV + 256 * h + 128:P_CKV + 256 * h + 192] = O_CV + 64 * h + np.arange(64)
    src[P_D:P_D + 512] = O_DU + np.arange(512)
    src[P_CF:P_CF + 4] = O_CF + np.arange(4)
    return src


def _uq_src():
    src = -np.ones(512, np.int64)
    for h in range(N_HEADS):
        src[128 * h:128 * h + 64] = 96 * h + np.arange(64)
        src[128 * h + 64:128 * h + 80] = 96 * h + 64 + np.arange(16)
        src[128 * h + 96:128 * h + 112] = 96 * h + 80 + np.arange(16)
    return src


def _ukv_src():
    src = -np.ones(1024, np.int64)
    for h in range(N_HEADS):
        src[256 * h:256 * h + 64] = 128 * h + np.arange(64)
        src[256 * h + 128:256 * h + 192] = 128 * h + 64 + np.arange(64)
    return src


def _mo_src():
    src = -np.ones(MO_W, np.int64)
    src[0:256] = np.arange(256)
    for g in range(2):
        for h in range(N_HEADS):
            src[256 + 512 * g + 128 * h:256 + 512 * g + 128 * h + 64] = 256 + 256 * g + 64 * h + np.arange(64)
    src[1280:1536] = 768 + np.arange(256)
    return src


def _runs(idx):
    runs, i = [], 0
    while i < len(idx):
        j = i + 1
        while j < len(idx) and ((idx[i] < 0 and idx[j] < 0) or (idx[i] >= 0 and idx[j] == idx[i] + j - i)):
            j += 1
        runs.append((int(idx[i]), j - i))
        i = j
    return runs


def _take_runs(w, idx):
    parts = [jnp.zeros(w.shape[:-1] + (n,), w.dtype) if s < 0 else lax.slice_in_dim(w, s, s + n, axis=w.ndim - 1)
             for s, n in _runs(idx)]
    return jnp.concatenate(parts, axis=-1)


def _pack_cols(w, src):
    return _take_runs(w, src)


def _unpack_cols(wp, src, n):
    dst = np.zeros(n, np.int64)
    dst[src[src >= 0]] = np.nonzero(src >= 0)[0]
    return _take_runs(wp, dst)


def _rope_tables(seq):
    half = B_ROPE // 2
    inv_freq = ROPE_THETA ** (-jnp.arange(half, dtype=F32) / half)
    ang = jnp.arange(seq).astype(F32)[:, None] * inv_freq[None, :]
    cos, sin = jnp.cos(ang), jnp.sin(ang)
    z16 = jnp.zeros((seq, 16), F32)
    c = jnp.concatenate([jnp.ones((seq, 64), F32), cos, z16, cos, z16], axis=1)
    s1 = jnp.concatenate([jnp.zeros((seq, 64), F32), -sin, z16, z16, z16], axis=1)
    s2 = jnp.concatenate([jnp.zeros((seq, 64), F32), z16, z16, sin, z16], axis=1)
    return c, s1, s2


def _matmul(a, b, *, mode, group_out, out_dtype, tm, tk, name):
    ga, gb = a.shape[0], b.shape[0]
    g_n = max(ga, gb)
    if mode == "tn":
        k_dim, m_dim = a.shape[1:]
    else:
        m_dim, k_dim = a.shape[1:]
    n_dim = b.shape[1] if mode == "nt" else b.shape[2]
    assert m_dim % tm == 0 and k_dim % tk == 0
    kt = k_dim // tk
    n_red = kt if group_out else g_n * kt
    g_out = g_n if group_out else 1

    def split(g, r):
        return (g, r) if group_out else (r // kt, r % kt)

    def a_map(g, i, r):
        gg, kk = split(g, r)
        gg = gg if ga > 1 else 0
        return (gg, kk, i) if mode == "tn" else (gg, i, kk)

    def b_map(g, i, r):
        gg, kk = split(g, r)
        gg = gg if gb > 1 else 0
        return (gg, 0, kk) if mode == "nt" else (gg, kk, 0)

    a_blk = (None, tk, tm) if mode == "tn" else (None, tm, tk)
    b_blk = (None, n_dim, tk) if mode == "nt" else (None, tk, n_dim)
    dn = _DN[mode]

    def body(a_ref, b_ref, o_ref, *scratch):
        part = lax.dot_general(a_ref[...].astype(BF16), b_ref[...].astype(BF16), dn, preferred_element_type=F32)
        if n_red == 1:
            o_ref[...] = part.astype(o_ref.dtype)
            return
        acc_ref, = scratch
        r = pl.program_id(2)

        @pl.when(r == 0)
        def _():
            acc_ref[...] = part

        @pl.when(r > 0)
        def _():
            acc_ref[...] += part

        @pl.when(r == n_red - 1)
        def _():
            o_ref[...] = acc_ref[...].astype(o_ref.dtype)

    return pl.pallas_call(
        body, name=name, grid=(g_out, m_dim // tm, n_red),
        in_specs=[pl.BlockSpec(a_blk, a_map), pl.BlockSpec(b_blk, b_map)],
        out_specs=pl.BlockSpec((None, tm, n_dim), lambda g, i, r: (g, i, 0)),
        out_shape=jax.ShapeDtypeStruct((g_out, m_dim, n_dim), out_dtype),
        scratch_shapes=[] if n_red == 1 else [pltpu.VMEM((tm, n_dim), F32)],
        compiler_params=_cparams(("parallel", "parallel", "arbitrary")),
    )(a, b)


def _matmul_groupsum(a, b, *, out_dtype, tm, name):
    g_n, m_dim, k_dim = a.shape
    n_dim = b.shape[2]
    assert m_dim % tm == 0 and b.shape[:2] == (g_n, k_dim)

    def body(a_ref, b_ref, o_ref):
        acc = jnp.dot(a_ref[0], b_ref[0], preferred_element_type=F32)
        for g in range(1, g_n):
            acc = acc + jnp.dot(a_ref[g], b_ref[g], preferred_element_type=F32)
        o_ref[...] = acc.astype(o_ref.dtype)

    return pl.pallas_call(
        body, name=name, grid=(m_dim // tm,),
        in_specs=[pl.BlockSpec((g_n, tm, k_dim), lambda i: (0, i, 0)),
                  pl.BlockSpec((g_n, k_dim, n_dim), lambda i: (0, 0, 0))],
        out_specs=pl.BlockSpec((tm, n_dim), lambda i: (i, 0)),
        out_shape=jax.ShapeDtypeStruct((m_dim, n_dim), out_dtype),
        compiler_params=_cparams(("parallel",)),
    )(a, b)


def _row_spec(ts, d):
    return pl.BlockSpec((None, ts, d), lambda b, s: (b, s, 0))


def _mod_spec(d):
    return pl.BlockSpec((None, N_MOD, d), lambda b, s: (b, 0, 0))


def _vec_spec(d):
    return pl.BlockSpec((1, d), lambda b, s: (0, 0))


def _bvec_spec(d):
    return pl.BlockSpec((None, 1, d), lambda b, s: (b, 0, 0))


def _modulate(x, mod, sh_row, sc_row, name, ts=512):
    bsz, seq, d = x.shape

    def body(x_ref, mod_ref, o_ref):
        sh = mod_ref[sh_row:sh_row + 1, :]
        sc = mod_ref[sc_row:sc_row + 1, :]
        o_ref[...] = (x_ref[...] * (1.0 + sc) + sh).astype(o_ref.dtype)

    return pl.pallas_call(
        body, name=name, grid=(bsz, seq // ts),
        in_specs=[_row_spec(ts, d), _mod_spec(d)], out_specs=_row_spec(ts, d),
        out_shape=jax.ShapeDtypeStruct((bsz, seq, d), BF16),
        compiler_params=_cparams(("parallel", "parallel")),
    )(x, mod)


def _modulate_bwd(dh, x, mod, dx_res, sc_row, name, ts=512):
    bsz, seq, d = x.shape

    def body(dh_ref, x_ref, mod_ref, dxr_ref, dx_ref, dsh_ref, dsc_ref):
        s = pl.program_id(1)
        sc = mod_ref[sc_row:sc_row + 1, :]
        dh_v = dh_ref[...]
        dx_ref[...] = dxr_ref[...] + dh_v * (1.0 + sc)
        psh = jnp.sum(dh_v, axis=0, keepdims=True)
        psc = jnp.sum(dh_v * x_ref[...], axis=0, keepdims=True)

        @pl.when(s == 0)
        def _():
            dsh_ref[...] = psh
            dsc_ref[...] = psc

        @pl.when(s > 0)
        def _():
            dsh_ref[...] += psh
            dsc_ref[...] += psc

    return pl.pallas_call(
        body, name=name, grid=(bsz, seq // ts),
        in_specs=[_row_spec(ts, d), _row_spec(ts, d), _mod_spec(d), _row_spec(ts, d)],
        out_specs=[_row_spec(ts, d), _bvec_spec(d), _bvec_spec(d)],
        out_shape=[jax.ShapeDtypeStruct((bsz, seq, d), F32), jax.ShapeDtypeStruct((bsz, 1, d), F32),
                   jax.ShapeDtypeStruct((bsz, 1, d), F32)],
        compiler_params=_cparams(("parallel", "arbitrary")),
    )(dh, x, mod, dx_res)


def _res_ln_fn(x, f, g, lng, lnb, cmul):
    r = ALPHA * x + (cmul * (1.0 + g)) * f
    mu = jnp.mean(r, axis=-1, keepdims=True)
    rc = r - mu
    var = jnp.mean(rc * rc, axis=-1, keepdims=True)
    return rc * lax.rsqrt(var + LN_EPS) * lng + lnb


def _res_ln(x, f, mod, lng, lnb, g_row, cmul, name, nxt=None, ts=512):
    bsz, seq, d = x.shape

    def body(*refs):
        x_ref, f_ref, mod_ref, lng_ref, lnb_ref = refs[:5]
        g = mod_ref[g_row:g_row + 1, :]
        y = _res_ln_fn(x_ref[...], f_ref[...], g, lng_ref[...], lnb_ref[...], cmul)
        if nxt is None:
            refs[5][...] = y
            return
        nmod_ref, o_ref, h_ref = refs[5:]
        o_ref[...] = y
        sh = nmod_ref[nxt[1]:nxt[1] + 1, :]
        sc = nmod_ref[nxt[2]:nxt[2] + 1, :]
        h_ref[...] = (y * (1.0 + sc) + sh).astype(h_ref.dtype)

    in_specs = [_row_spec(ts, d), _row_spec(ts, d), _mod_spec(d), _vec_spec(d), _vec_spec(d)]
    args = [x, f, mod, lng, lnb]
    out_specs, out_shape = [_row_spec(ts, d)], [jax.ShapeDtypeStruct((bsz, seq, d), F32)]
    if nxt is not None:
        in_specs.append(_mod_spec(d))
        args.append(nxt[0])
        out_specs.append(_row_spec(ts, d))
        out_shape.append(jax.ShapeDtypeStruct((bsz, seq, d), BF16))
    res = pl.pallas_call(
        body, name=name, grid=(bsz, seq // ts), in_specs=in_specs, out_specs=out_specs, out_shape=out_shape,
        compiler_params=_cparams(("parallel", "parallel")),
    )(*args)
    return (res[0], res[1]) if nxt is not None else (res[0], None)


def _res_ln_bwd(dy, x, f, mod, lng, lnb, g_row, cmul, name, pre=None, ts=256):
    bsz, seq, d = x.shape
    fused = pre is not None

    def body(*refs):
        dy_ref, x_ref, f_ref, mod_ref, lng_ref, lnb_ref = refs[:6]
        n_in = 8 if fused else 6
        dx_ref, df_ref, dg_ref, dlg_ref, dlb_ref = refs[n_in:n_in + 5]
        b, s = pl.program_id(0), pl.program_id(1)
        g = mod_ref[g_row:g_row + 1, :]
        y, vjp = jax.vjp(functools.partial(_res_ln_fn, cmul=cmul), x_ref[...], f_ref[...], g, lng_ref[...],
                         lnb_ref[...])
        ct = dy_ref[...]
        if fused:
            dh_ref, nmod_ref = refs[6:8]
            dsh_ref, dsc_ref = refs[n_in + 5:]
            dh_v = dh_ref[...]
            ct = ct + dh_v * (1.0 + nmod_ref[pre[3]:pre[3] + 1, :])
            psh = jnp.sum(dh_v, axis=0, keepdims=True)
            psc = jnp.sum(dh_v * y, axis=0, keepdims=True)
        dx, df, dg, dlg, dlb = vjp(ct)
        dx_ref[...] = dx
        df_ref[...] = df.astype(df_ref.dtype)

        @pl.when(s == 0)
        def _():
            dg_ref[...] = dg
            if fused:
                dsh_ref[...] = psh
                dsc_ref[...] = psc

        @pl.when(s > 0)
        def _():
            dg_ref[...] += dg
            if fused:
                dsh_ref[...] += psh
                dsc_ref[...] += psc

        first = jnp.logical_and(b == 0, s == 0)

        @pl.when(first)
        def _():
            dlg_ref[...] = dlg
            dlb_ref[...] = dlb

        @pl.when(jnp.logical_not(first))
        def _():
            dlg_ref[...] += dlg
            dlb_ref[...] += dlb

    in_specs = [_row_spec(ts, d), _row_spec(ts, d), _row_spec(ts, d), _mod_spec(d), _vec_spec(d), _vec_spec(d)]
    args = [dy, x, f, mod, lng, lnb]
    out_specs = [_row_spec(ts, d), _row_spec(ts, d), _bvec_spec(d), _vec_spec(d), _vec_spec(d)]
    bvec = jax.ShapeDtypeStruct((bsz, 1, d), F32)
    out_shape = [jax.ShapeDtypeStruct((bsz, seq, d), F32), jax.ShapeDtypeStruct((bsz, seq, d), BF16), bvec,
                 jax.ShapeDtypeStruct((1, d), F32), jax.ShapeDtypeStruct((1, d), F32)]
    if fused:
        in_specs += [_row_spec(ts, d), _mod_spec(d)]
        args += [pre[0], pre[2]]
        out_specs += [_bvec_spec(d), _bvec_spec(d)]
        out_shape += [bvec, bvec]
    res = pl.pallas_call(
        body, name=name, grid=(bsz, seq // ts), in_specs=in_specs, out_specs=out_specs, out_shape=out_shape,
        compiler_params=_cparams(("arbitrary", "arbitrary")),
    )(*args)
    return tuple(res[:5]), (tuple(res[5:]) if fused else None)


def _loss_head(y, target, name, ts=512):
    bsz, seq, d = y.shape
    n_s = seq // ts

    def body(y_ref, t_ref, dy_ref, loss_ref, acc_ref):
        b, s = pl.program_id(0), pl.program_id(1)
        err = y_ref[...] - t_ref[...]
        dy_ref[...] = err * (1.0 / d)
        part = jnp.sum(err * err, axis=0, keepdims=True)
        first = jnp.logical_and(b == 0, s == 0)

        @pl.when(first)
        def _():
            acc_ref[...] = part

        @pl.when(jnp.logical_not(first))
        def _():
            acc_ref[...] += part

        @pl.when(jnp.logical_and(b == bsz - 1, s == n_s - 1))
        def _():
            loss_ref[...] = jnp.sum(acc_ref[...], axis=1, keepdims=True) * (0.5 / d)

    return pl.pallas_call(
        body, name=name, grid=(bsz, n_s),
        in_specs=[_row_spec(ts, d), _row_spec(ts, d)],
        out_specs=[_row_spec(ts, d), pl.BlockSpec((1, 1), lambda b, s: (0, 0))],
        out_shape=[jax.ShapeDtypeStruct((bsz, seq, d), F32), jax.ShapeDtypeStruct((1, 1), F32)],
        scratch_shapes=[pltpu.VMEM((1, d), F32)],
        compiler_params=_cparams(("arbitrary", "arbitrary")),
    )(y, target)


def _ffn_in_swiglu(h, w_in_t, name, tm=1024):
    t, d = h.shape
    n_sh, w, _ = w_in_t.shape
    half = n_sh // 2

    def body(h_ref, w_ref, z_ref, a_ref):
        hv = h_ref[...]
        g = lax.dot_general(hv, w_ref[0], _DN["nt"], preferred_element_type=F32)
        u = lax.dot_general(hv, w_ref[1], _DN["nt"], preferred_element_type=F32)
        z_ref[0] = g.astype(z_ref.dtype)
        z_ref[1] = u.astype(z_ref.dtype)
        a_ref[...] = (g * jax.nn.sigmoid(g) * u).astype(a_ref.dtype)

    return pl.pallas_call(
        body, name=name, grid=(half, t // tm),
        in_specs=[pl.BlockSpec((tm, d), lambda g, i: (i, 0)),
                  pl.BlockSpec((2, None, w, d), lambda g, i: (0, g, 0, 0))],
        out_specs=[pl.BlockSpec((2, None, tm, w), lambda g, i: (0, g, i, 0)),
                   pl.BlockSpec((None, tm, w), lambda g, i: (g, i, 0))],
        out_shape=[jax.ShapeDtypeStruct((2, half, t, w), BF16), jax.ShapeDtypeStruct((half, t, w), BF16)],
        compiler_params=_cparams(("parallel", "parallel")),
    )(h, w_in_t.reshape(2, half, w, d))


def _ffn_out_dx_swiglu(df, w_out, z, name, tm=1024):
    t, d = df.shape
    half, w, _ = w_out.shape

    def body(df_ref, w_ref, z_ref, dz_ref):
        da = lax.dot_general(df_ref[...], w_ref[...], _DN["nt"], preferred_element_type=F32)
        g = z_ref[0].astype(F32)
        u = z_ref[1].astype(F32)
        sig = jax.nn.sigmoid(g)
        dz_ref[0] = (da * u * (sig * (1.0 + g * (1.0 - sig)))).astype(dz_ref.dtype)
        dz_ref[1] = (da * (g * sig)).astype(dz_ref.dtype)

    zspec = pl.BlockSpec((2, None, tm, w), lambda g, i: (0, g, i, 0))
    return pl.pallas_call(
        body, name=name, grid=(half, t // tm),
        in_specs=[pl.BlockSpec((tm, d), lambda g, i: (i, 0)), pl.BlockSpec((None, w, d), lambda g, i: (g, 0, 0)),
                  zspec],
        out_specs=zspec, out_shape=jax.ShapeDtypeStruct(z.shape, BF16),
        compiler_params=_cparams(("parallel", "parallel")),
    )(df, w_out, z)


def _log_sigmoid(x):
    return jnp.minimum(x, 0.0) - jnp.log(1.0 + jnp.exp(-jnp.abs(x)))


def _hgrn_consts():
    r = lax.broadcasted_iota(jnp.int32, (GROUP_WIDTH, GROUP_WIDTH), 0)
    c = lax.broadcasted_iota(jnp.int32, (GROUP_WIDTH, GROUP_WIDTH), 1)
    bd = (r // HEAD_DIM == c // HEAD_DIM).astype(F32)
    r16 = lax.broadcasted_iota(jnp.int32, (A_CHUNK, A_CHUNK), 0)
    c16 = lax.broadcasted_iota(jnp.int32, (A_CHUNK, A_CHUNK), 1)
    tril = (r16 >= c16).astype(F32)
    rows = lax.broadcasted_iota(jnp.int32, (A_CHUNK, GROUP_WIDTH), 0)
    return bd, tril, rows


def _hgrn_lb(logits8, layer):
    rows = lax.broadcasted_iota(jnp.int32, logits8.shape, 0)
    valid = rows < DEPTH
    mx = jnp.max(jnp.where(valid, logits8, NEG), axis=0, keepdims=True)
    e = jnp.where(valid, jnp.exp(logits8 - mx), 0.0)
    sm = e / jnp.sum(e, axis=0, keepdims=True)
    pick = jnp.logical_and(rows >= 1, rows <= layer)
    return jnp.sum(jnp.where(pick, sm, 0.0), axis=0, keepdims=True)


def _hgrn_chunk(aq, af, ai, ag, logits8, norm_g, st, *, layer, consts):
    bd, tril, rows = consts
    lb = _hgrn_lb(logits8, layer)
    la = jnp.log(jnp.maximum(lb, LB_FLOOR))
    b2 = jnp.log(1.0 - lb) + _log_sigmoid(af)
    log_f = jnp.maximum(la, b2) + jnp.log(1.0 + jnp.exp(-jnp.abs(la - b2)))
    k = 1.0 - jnp.exp(log_f)
    qf = aq * jax.nn.sigmoid(aq)
    g_cum = jnp.dot(tril, log_f, precision=HI, preferred_element_type=F32)

    c, w = A_CHUNK, GROUP_WIDTH

    def by_key(v):
        return jnp.broadcast_to(v[:, None, :], (c, c, w))

    def by_query(v):
        return jnp.broadcast_to(v[None, :, :], (c, c, w))

    s_i = lax.broadcasted_iota(jnp.int32, (c, c, w), 0)
    t_i = lax.broadcasted_iota(jnp.int32, (c, c, w), 1)
    rel = jnp.where(t_i >= s_i, by_query(g_cum) - by_key(g_cum), NEG)
    pairs = by_query(qf) * by_key(k) * jnp.exp(rel)
    a_all = _bdot(pairs.reshape(c * c, w), bd, "nn").reshape(c, c, w)
    o = jnp.sum(a_all * by_key(ai), axis=0)
    q_dec = qf * jnp.exp(g_cum)
    o = o + _bdot(q_dec, st, "nt")
    g_last = jnp.sum(jnp.where(rows == c - 1, g_cum, 0.0), axis=0, keepdims=True)
    k_end = k * jnp.exp(g_last - g_cum)
    kv = _bdot(ai, k_end, "tn")
    st_new = st * jnp.exp(g_last) + kv * bd
    ms = _bdot(o * o, bd, "nn") * (1.0 / HEAD_DIM)
    o = o * lax.rsqrt(ms + RMS_EPS) * norm_g
    return o * (ag * jax.nn.sigmoid(ag)), st_new


def _hgrn_fwd(proj, logits8, norm_g, layer, name, ts=256):
    bsz, seq, _ = proj.shape
    n_ch = ts // A_CHUNK

    def body(p_ref, lg_ref, ng_ref, o_ref, st_ref, st_scr):
        @pl.when(pl.program_id(1) == 0)
        def _():
            st_scr[...] = jnp.zeros_like(st_scr)

        consts = _hgrn_consts()
        logits_v, ng_v = lg_ref[...], ng_ref[...]

        def chunk(ci, carry):
            r = ci * A_CHUNK if isinstance(ci, int) else pl.multiple_of(ci * A_CHUNK, A_CHUNK)
            st = st_scr[...]
            st_ref[ci] = st
            o, st_new = _hgrn_chunk(
                p_ref[pl.ds(r, A_CHUNK), 0:256], p_ref[pl.ds(r, A_CHUNK), 256:512],
                p_ref[pl.ds(r, A_CHUNK), 512:768], p_ref[pl.ds(r, A_CHUNK), 768:1024],
                logits_v, ng_v, st, layer=layer, consts=consts)
            o_ref[pl.ds(r, A_CHUNK), :] = o.astype(o_ref.dtype)
            st_scr[...] = st_new
            return carry

        if n_ch <= 2:
            for c_static in range(n_ch):
                chunk(c_static, 0)
        else:
            lax.fori_loop(0, n_ch, chunk, 0, unroll=8)

    return pl.pallas_call(
        body, name=name, grid=(bsz, seq // ts),
        in_specs=[pl.BlockSpec((None, ts, 1024), lambda b, s: (b, s, 0)),
                  pl.BlockSpec((8, GROUP_WIDTH), lambda b, s: (0, 0)),
                  pl.BlockSpec((1, GROUP_WIDTH), lambda b, s: (0, 0))],
        out_specs=[pl.BlockSpec((None, ts, GROUP_WIDTH), lambda b, s: (b, s, 0)),
                   pl.BlockSpec((None, n_ch, GROUP_WIDTH, GROUP_WIDTH), lambda b, s: (b, s, 0, 0))],
        out_shape=[jax.ShapeDtypeStruct((bsz, seq, MO_W), BF16),
                   jax.ShapeDtypeStruct((bsz, seq // A_CHUNK, GROUP_WIDTH, GROUP_WIDTH), F32)],
        scratch_shapes=[pltpu.VMEM((GROUP_WIDTH, GROUP_WIDTH), F32)],
        compiler_params=_cparams(("parallel", "arbitrary")),
    )(proj, logits8, norm_g)


def _hgrn_bwd(dmo, proj, states, logits8, norm_g, layer, name, ts=256):
    bsz, seq, _ = proj.shape
    n_ch = ts // A_CHUNK
    n_s = seq // ts

    def body(do_ref, p_ref, st_ref, lg_ref, ng_ref, dp_ref, dlg_ref, dng_ref, dst_scr):
        b, s = pl.program_id(0), pl.program_id(1)

        @pl.when(s == 0)
        def _():
            dst_scr[...] = jnp.zeros_like(dst_scr)

        @pl.when(jnp.logical_and(b == 0, s == 0))
        def _():
            dlg_ref[...] = jnp.zeros_like(dlg_ref)
            dng_ref[...] = jnp.zeros_like(dng_ref)

        consts = _hgrn_consts()
        logits_v, ng_v = lg_ref[...], ng_ref[...]
        fn = functools.partial(_hgrn_chunk, layer=layer, consts=consts)

        def chunk(t, carry):
            ci = n_ch - 1 - t
            r = ci * A_CHUNK if isinstance(ci, int) else pl.multiple_of(ci * A_CHUNK, A_CHUNK)
            _, vjp = jax.vjp(
                fn, p_ref[pl.ds(r, A_CHUNK), 0:256], p_ref[pl.ds(r, A_CHUNK), 256:512],
                p_ref[pl.ds(r, A_CHUNK), 512:768], p_ref[pl.ds(r, A_CHUNK), 768:1024],
                logits_v, ng_v, st_ref[ci])
            daq, daf, dai, dag, dlg, dng, dst = vjp((do_ref[pl.ds(r, A_CHUNK), :], dst_scr[...]))
            dp_ref[pl.ds(r, A_CHUNK), 0:256] = daq.astype(dp_ref.dtype)
            dp_ref[pl.ds(r, A_CHUNK), 256:512] = daf.astype(dp_ref.dtype)
            dp_ref[pl.ds(r, A_CHUNK), 512:768] = dai.astype(dp_ref.dtype)
            dp_ref[pl.ds(r, A_CHUNK), 768:1024] = dag.astype(dp_ref.dtype)
            dlg_ref[...] += dlg
            dng_ref[...] += dng
            dst_scr[...] = dst
            return carry

        if n_ch <= 2:
            for c_static in range(n_ch):
                chunk(c_static, 0)
        else:
            lax.fori_loop(0, n_ch, chunk, 0, unroll=8)

    rev = lambda b, s: (b, n_s - 1 - s, 0)
    return pl.pallas_call(
        body, name=name, grid=(bsz, n_s),
        in_specs=[pl.BlockSpec((None, ts, GROUP_WIDTH), rev),
                  pl.BlockSpec((None, ts, 1024), rev),
                  pl.BlockSpec((None, n_ch, GROUP_WIDTH, GROUP_WIDTH), lambda b, s: (b, n_s - 1 - s, 0, 0)),
                  pl.BlockSpec((8, GROUP_WIDTH), lambda b, s: (0, 0)),
                  pl.BlockSpec((1, GROUP_WIDTH), lambda b, s: (0, 0))],
        out_specs=[pl.BlockSpec((None, ts, 1024), rev),
                   pl.BlockSpec((8, GROUP_WIDTH), lambda b, s: (0, 0)),
                   pl.BlockSpec((1, GROUP_WIDTH), lambda b, s: (0, 0))],
        out_shape=[jax.ShapeDtypeStruct((bsz, seq, PACK_W), BF16),
                   jax.ShapeDtypeStruct((8, GROUP_WIDTH), F32), jax.ShapeDtypeStruct((1, GROUP_WIDTH), F32)],
        scratch_shapes=[pltpu.VMEM((GROUP_WIDTH, GROUP_WIDTH), F32)],
        compiler_params=_cparams(("arbitrary", "arbitrary")),
    )(dmo, proj, states, logits8, norm_g)


def _rms_fn(x, g):
    return x * lax.rsqrt(jnp.mean(x * x, axis=-1, keepdims=True) + RMS_EPS) * g


def _tile4(t):
    return jnp.concatenate([t, t, t, t], axis=1)


def _rope(x, c, s1, s2):
    w = x.shape[-1]
    return x * c + pltpu.roll(x, 32, axis=1) * s2 + pltpu.roll(x, w - 32, axis=1) * s1


def _rope_t(dy, c, s1, s2):
    w = dy.shape[-1]
    return dy * c + pltpu.roll(dy * s2, w - 32, axis=1) + pltpu.roll(dy * s1, 32, axis=1)


def _mla_pre(proj, qg, kvg, wq, wkv, tabs, name, ts=256):
    bsz, seq, _ = proj.shape

    def body(p_ref, qg_ref, kvg_ref, wq_ref, wkv_ref, c_ref, s1_ref, s2_ref, q_ref, kv_ref):
        nq = _rms_fn(p_ref[:, 0:256], qg_ref[...])
        nkv = _rms_fn(p_ref[:, 256:384], kvg_ref[...])
        c, s1, s2 = c_ref[...], s1_ref[...], s2_ref[...]
        qp = jnp.dot(nq.astype(BF16), wq_ref[...], preferred_element_type=F32)
        q_ref[...] = _rope(qp, _tile4(c), _tile4(s1), _tile4(s2)).astype(q_ref.dtype)
        kv = jnp.dot(nkv.astype(BF16), wkv_ref[...], preferred_element_type=F32)
        krr = _rope(p_ref[:, 384:512], c, s1, s2)
        zero = jnp.zeros_like(krr)
        kv_ref[...] = (kv + jnp.concatenate([krr, zero] * N_HEADS, axis=1)).astype(kv_ref.dtype)

    tab_spec = pl.BlockSpec((ts, LANES), lambda b, s: (s, 0))
    return pl.pallas_call(
        body, name=name, grid=(bsz, seq // ts),
        in_specs=[pl.BlockSpec((None, ts, 512), lambda b, s: (b, s, P_B // 512)),
                  _vec_spec(256), _vec_spec(128),
                  pl.BlockSpec((256, 512), lambda b, s: (0, 0)), pl.BlockSpec((128, 1024), lambda b, s: (0, 0)),
                  tab_spec, tab_spec, tab_spec],
        out_specs=[_row_spec(ts, 512), _row_spec(ts, 1024)],
        out_shape=[jax.ShapeDtypeStruct((bsz, seq, 512), BF16), jax.ShapeDtypeStruct((bsz, seq, 1024), BF16)],
        compiler_params=_cparams(("parallel", "parallel")),
    )(proj, qg, kvg, wq, wkv, *tabs)


def _mla_pre_bwd(dq, dkv, dproj, proj, qg, kvg, wq, wkv, tabs, name, ts=256):
    bsz, seq, _ = proj.shape

    def body(dq_ref, dkv_ref, dp_any, p_ref, qg_ref, kvg_ref, wq_ref, wkv_ref, c_ref, s1_ref, s2_ref,
             dp_ref, dqg_ref, dkvg_ref, dwq_ref, dwkv_ref):
        del dp_any
        first = jnp.logical_and(pl.program_id(0) == 0, pl.program_id(1) == 0)

        @pl.when(first)
        def _():
            dqg_ref[...] = jnp.zeros_like(dqg_ref)
            dkvg_ref[...] = jnp.zeros_like(dkvg_ref)
            dwq_ref[...] = jnp.zeros_like(dwq_ref)
            dwkv_ref[...] = jnp.zeros_like(dwkv_ref)

        c, s1, s2 = c_ref[...], s1_ref[...], s2_ref[...]
        nq, vjp_q = jax.vjp(_rms_fn, p_ref[:, 0:256], qg_ref[...])
        nkv, vjp_kv = jax.vjp(_rms_fn, p_ref[:, 256:384], kvg_ref[...])
        dqp = _rope_t(dq_ref[...], _tile4(c), _tile4(s1), _tile4(s2)).astype(BF16)
        dkv_v = dkv_ref[...]
        dkv_b = dkv_v.astype(BF16)
        tn = (((0,), (0,)), ((), ()))
        nt = (((1,), (1,)), ((), ()))
        dwq_ref[...] += lax.dot_general(nq.astype(BF16), dqp, tn, preferred_element_type=F32)
        dwkv_ref[...] += lax.dot_general(nkv.astype(BF16), dkv_b, tn, preferred_element_type=F32)
        dcq, dqg = vjp_q(lax.dot_general(dqp, wq_ref[...], nt, preferred_element_type=F32))
        dckv, dkvg = vjp_kv(lax.dot_general(dkv_b, wkv_ref[...], nt, preferred_element_type=F32))
        dqg_ref[...] += dqg
        dkvg_ref[...] += dkvg
        dk_sum = dkv_v[:, 0:128] + dkv_v[:, 256:384] + dkv_v[:, 512:640] + dkv_v[:, 768:896]
        lane = lax.broadcasted_iota(jnp.int32, dk_sum.shape, 1)
        dkr = jnp.where(lane >= 64, _rope_t(dk_sum, c, s1, s2), 0.0)
        dp_ref[:, 0:256] = dcq.astype(dp_ref.dtype)
        dp_ref[:, 256:384] = dckv.astype(dp_ref.dtype)
        dp_ref[:, 384:512] = dkr.astype(dp_ref.dtype)

    tab_spec = pl.BlockSpec((ts, LANES), lambda b, s: (s, 0))
    const = lambda shape: pl.BlockSpec(shape, lambda b, s: (0, 0))
    return pl.pallas_call(
        body, name=name, grid=(bsz, seq // ts),
        in_specs=[_row_spec(ts, 512), _row_spec(ts, 1024), pl.BlockSpec(memory_space=pl.ANY),
                  pl.BlockSpec((None, ts, 512), lambda b, s: (b, s, P_B // 512)),
                  _vec_spec(256), _vec_spec(128), const((256, 512)), const((128, 1024)),
                  tab_spec, tab_spec, tab_spec],
        out_specs=[pl.BlockSpec((None, ts, 512), lambda b, s: (b, s, P_B // 512)),
                   _vec_spec(256), _vec_spec(128), const((256, 512)), const((128, 1024))],
        out_shape=[jax.ShapeDtypeStruct(dproj.shape, dproj.dtype), jax.ShapeDtypeStruct((1, 256), F32),
                   jax.ShapeDtypeStruct((1, 128), F32), jax.ShapeDtypeStruct((256, 512), F32),
                   jax.ShapeDtypeStruct((128, 1024), F32)],
        input_output_aliases={2: 0},
        compiler_params=_cparams(("arbitrary", "arbitrary")),
    )(dq, dkv, dproj, proj, qg, kvg, wq, wkv, *tabs)


def _fox_gate(proj, bf, name):
    bsz, seq, _ = proj.shape
    n_blk = seq // LANES

    def body(x_ref, bf_ref, f_ref):
        r_i = lax.broadcasted_iota(jnp.int32, (LANES, LANES), 0)
        c_i = lax.broadcasted_iota(jnp.int32, (LANES, LANES), 1)
        tril = (r_i >= c_i).astype(F32)
        bias = bf_ref[...]

        def blk(i, carry):
            r = pl.multiple_of(i * LANES, LANES)
            lf = _log_sigmoid(x_ref[pl.ds(r, LANES), :] + bias)
            f_ref[pl.ds(r, LANES), :] = jnp.dot(tril, lf, precision=HI, preferred_element_type=F32) + carry
            return carry + jnp.sum(lf, axis=0, keepdims=True)

        lax.fori_loop(0, n_blk, blk, jnp.zeros((1, LANES), F32))

    return pl.pallas_call(
        body, name=name, grid=(bsz,),
        in_specs=[pl.BlockSpec((None, seq, LANES), lambda b: (b, 0, P_CF // LANES)),
                  pl.BlockSpec((1, LANES), lambda b: (0, 0))],
        out_specs=pl.BlockSpec((None, seq, LANES), lambda b: (b, 0, 0)),
        out_shape=jax.ShapeDtypeStruct((bsz, seq, LANES), F32),
        compiler_params=_cparams(("parallel",)),
    )(proj, bf)


def _fox_gate_bwd(dfq, dfk_cols, dproj, proj, bf, name):
    bsz, seq, _ = proj.shape
    n_blk = seq // LANES

    def body(dfq_ref, dfk_ref, dp_any, x_ref, bf_ref, dp_ref, dbf_ref):
        del dp_any

        @pl.when(pl.program_id(0) == 0)
        def _():
            dbf_ref[...] = jnp.zeros_like(dbf_ref)

        r_i = lax.broadcasted_iota(jnp.int32, (LANES, LANES), 0)
        c_i = lax.broadcasted_iota(jnp.int32, (LANES, LANES), 1)
        triu = (r_i <= c_i).astype(F32)
        bias = bf_ref[...]

        def blk(t, carry):
            tail, dbf = carry
            r = pl.multiple_of((n_blk - 1 - t) * LANES, LANES)
            dc = dfk_ref[pl.ds(r, LANES), :]
            for hd in range(N_HEADS):
                dc = dc + jnp.where(c_i == hd, dfq_ref[hd, pl.ds(r, LANES), :], 0.0)
            dlf = jnp.dot(triu, dc, precision=HI, preferred_element_type=F32) + tail
            dx = dlf * (1.0 - jax.nn.sigmoid(x_ref[pl.ds(r, LANES), :] + bias))
            dp_ref[pl.ds(r, LANES), :] = dx.astype(dp_ref.dtype)
            return tail + jnp.sum(dc, axis=0, keepdims=True), dbf + jnp.sum(dx, axis=0, keepdims=True)

        z = jnp.zeros((1, LANES), F32)
        _, dbf = lax.fori_loop(0, n_blk, blk, (z, z))
        dbf_ref[...] += dbf

    return pl.pallas_call(
        body, name=name, grid=(bsz,),
        in_specs=[pl.BlockSpec((None, N_HEADS, seq, LANES), lambda b: (b, 0, 0, 0)),
                  pl.BlockSpec((None, seq, LANES), lambda b: (b, 0, 0)), pl.BlockSpec(memory_space=pl.ANY),
                  pl.BlockSpec((None, seq, LANES), lambda b: (b, 0, P_CF // LANES)),
                  pl.BlockSpec((1, LANES), lambda b: (0, 0))],
        out_specs=[pl.BlockSpec((None, seq, LANES), lambda b: (b, 0, P_CF // LANES)),
                   pl.BlockSpec((1, LANES), lambda b: (0, 0))],
        out_shape=[jax.ShapeDtypeStruct(dproj.shape, dproj.dtype), jax.ShapeDtypeStruct((1, LANES), F32)],
        input_output_aliases={2: 0},
        compiler_params=_cparams(("arbitrary",)),
    )(dfq, dfk_cols, dproj, proj, bf)


def _gate_terms(fc_ref, fr_ref, h, tq, tk):
    lane = lax.broadcasted_iota(jnp.int32, (tq, LANES), 1)
    fcol = jnp.sum(jnp.where(lane == h, fc_ref[...], 0.0), axis=1, keepdims=True)
    sub = lax.broadcasted_iota(jnp.int32, (8, tk), 0)
    frow = jnp.sum(jnp.where(sub == h, fr_ref[...], 0.0), axis=0, keepdims=True)
    return fcol - frow


def _scores(q_ref, k_ref, gate_refs, scale, h, masked, tq, tk):
    q = (q_ref[...].astype(F32) * scale).astype(BF16)
    s = lax.dot_general(q, k_ref[...].astype(BF16), _DN["nt"], preferred_element_type=F32)
    if gate_refs is not None:
        s = s + _gate_terms(gate_refs[0], gate_refs[1], h, tq, tk)
    if masked is not False:
        r_i = lax.broadcasted_iota(jnp.int32, (tq, tk), 0)
        c_i = lax.broadcasted_iota(jnp.int32, (tq, tk), 1)
        keep = c_i <= r_i
        s = jnp.where(keep if masked is True else jnp.logical_or(jnp.logical_not(masked), keep), s, NEG)
    return s, q


def _lanes(col):
    return jnp.broadcast_to(col, (col.shape[0], LANES))


def _attn_fwd(qa, q0, kva, kv0, mo, o0, gates, scale, name, tq=None):
    bsz, seq, _ = qa.shape
    tq = ATTN_TILE if tq is None else tq
    n_q = seq // tq
    gated = gates is not None

    def body(*refs):
        q_ref, k_ref, v_ref = refs[:3]
        gate_refs = refs[3:5] if gated else None
        o_ref, lse_ref, m_s, l_s, acc_s = refs[-5:]
        h, i, j = pl.program_id(1), pl.program_id(2), pl.program_id(3)

        @pl.when(j == 0)
        def _():
            m_s[...] = jnp.full_like(m_s, NEG)
            l_s[...] = jnp.zeros_like(l_s)
            acc_s[...] = jnp.zeros_like(acc_s)

        def step(masked):
            s, _ = _scores(q_ref, k_ref, gate_refs, scale, h, masked, tq, tq)
            m_prev = m_s[...]
            m_new = jnp.maximum(m_prev, jnp.max(s, axis=1, keepdims=True))
            alpha = jnp.exp(m_prev - m_new)
            p = jnp.exp(s - m_new)
            l_s[...] = alpha * l_s[...] + jnp.sum(p, axis=1, keepdims=True)
            acc_s[...] = alpha * acc_s[...] + jnp.dot(p.astype(BF16), v_ref[...].astype(BF16),
                                                      preferred_element_type=F32)
            m_s[...] = m_new

        @pl.when(j <= i)
        def _():
            step(j == i)

        @pl.when(j == i)
        def _():
            o_ref[...] = (acc_s[...] / l_s[...]).astype(o_ref.dtype)
            lse_ref[...] = _lanes(m_s[...] + jnp.log(l_s[...]))

    blk = (None, tq, LANES)
    in_specs = [pl.BlockSpec(blk, lambda b, h, i, j: (b, i, q0 + h)),
                pl.BlockSpec(blk, lambda b, h, i, j: (b, jnp.minimum(j, i), kv0 + 2 * h)),
                pl.BlockSpec(blk, lambda b, h, i, j: (b, jnp.minimum(j, i), kv0 + 2 * h + 1))]
    args = [qa, kva, kva]
    if gated:
        in_specs += [pl.BlockSpec(blk, lambda b, h, i, j: (b, i, 0)),
                     pl.BlockSpec((None, 8, tq), lambda b, h, i, j: (b, 0, jnp.minimum(j, i)))]
        args += list(gates)
    in_specs.append(pl.BlockSpec(memory_space=pl.ANY))
    args.append(mo)
    return pl.pallas_call(
        body, name=name, grid=(bsz, N_HEADS, n_q, n_q), in_specs=in_specs,
        out_specs=[pl.BlockSpec(blk, lambda b, h, i, j: (b, i, o0 + h)),
                   pl.BlockSpec((None, None, tq, LANES), lambda b, h, i, j: (b, h, i, 0))],
        out_shape=[jax.ShapeDtypeStruct(mo.shape, mo.dtype),
                   jax.ShapeDtypeStruct((bsz, N_HEADS, seq, LANES), F32)],
        scratch_shapes=[pltpu.VMEM((tq, 1), F32), pltpu.VMEM((tq, 1), F32), pltpu.VMEM((tq, LANES), F32)],
        input_output_aliases={len(args) - 1: 0},
        compiler_params=_cparams(("parallel", "parallel", "parallel", "arbitrary")),
    )(*args)


def _attn_bwd_q(qa, q0, kva, kv0, mo, dmo, o0, lse, gates, scale, out, out0, name, tq=None):
    bsz, seq, _ = qa.shape
    tq = ATTN_TILE if tq is None else tq
    n_q = seq // tq
    gated = gates is not None
    aliased = not isinstance(out, jax.ShapeDtypeStruct)

    def body(*refs):
        q_ref, k_ref, v_ref, o_ref, do_ref, lse_ref = refs[:6]
        gate_refs = refs[6:8] if gated else None
        dq_ref, delta_ref, dfq_ref, acc_s, dl_s, df_s = refs[-6:]
        h, i, j = pl.program_id(1), pl.program_id(2), pl.program_id(3)

        @pl.when(j == 0)
        def _():
            acc_s[...] = jnp.zeros_like(acc_s)
            df_s[...] = jnp.zeros_like(df_s)
            dl_s[...] = jnp.sum(do_ref[...] * o_ref[...].astype(F32), axis=1, keepdims=True)

        def step(masked):
            s, _ = _scores(q_ref, k_ref, gate_refs, scale, h, masked, tq, tq)
            p = jnp.exp(s - lse_ref[:, 0:1])
            dp = lax.dot_general(do_ref[...].astype(BF16), v_ref[...].astype(BF16), _DN["nt"],
                                 preferred_element_type=F32)
            ds = p * (dp - dl_s[...])
            acc_s[...] += jnp.dot(ds.astype(BF16), k_ref[...].astype(BF16), preferred_element_type=F32)
            df_s[...] += jnp.sum(ds, axis=1, keepdims=True)

        @pl.when(j <= i)
        def _():
            step(j == i)

        @pl.when(j == i)
        def _():
            dq_ref[...] = (acc_s[...] * scale).astype(dq_ref.dtype)
            delta_ref[...] = _lanes(dl_s[...])
            dfq_ref[...] = _lanes(df_s[...])

    blk = (None, tq, LANES)
    col = pl.BlockSpec((None, None, tq, LANES), lambda b, h, i, j: (b, h, i, 0))
    in_specs = [pl.BlockSpec(blk, lambda b, h, i, j: (b, i, q0 + h)),
                pl.BlockSpec(blk, lambda b, h, i, j: (b, jnp.minimum(j, i), kv0 + 2 * h)),
                pl.BlockSpec(blk, lambda b, h, i, j: (b, jnp.minimum(j, i), kv0 + 2 * h + 1)),
                pl.BlockSpec(blk, lambda b, h, i, j: (b, i, o0 + h)),
                pl.BlockSpec(blk, lambda b, h, i, j: (b, i, o0 + h)), col]
    args = [qa, kva, kva, mo, dmo, lse]
    if gated:
        in_specs += [pl.BlockSpec(blk, lambda b, h, i, j: (b, i, 0)),
                     pl.BlockSpec((None, 8, tq), lambda b, h, i, j: (b, 0, jnp.minimum(j, i)))]
        args += list(gates)
    aliases = {}
    if aliased:
        in_specs.append(pl.BlockSpec(memory_space=pl.ANY))
        args.append(out)
        aliases = {len(args) - 1: 0}
    vec = jax.ShapeDtypeStruct((bsz, N_HEADS, seq, LANES), F32)
    return pl.pallas_call(
        body, name=name, grid=(bsz, N_HEADS, n_q, n_q), in_specs=in_specs,
        out_specs=[pl.BlockSpec(blk, lambda b, h, i, j: (b, i, out0 + h)), col, col],
        out_shape=[jax.ShapeDtypeStruct(out.shape, out.dtype), vec, vec],
        scratch_shapes=[pltpu.VMEM((tq, LANES), F32), pltpu.VMEM((tq, 1), F32), pltpu.VMEM((tq, 1), F32)],
        input_output_aliases=aliases,
        compiler_params=_cparams(("parallel", "parallel", "parallel", "arbitrary")),
    )(*args)


def _attn_bwd_kv(qa, q0, kva, kv0, dmo, o0, lse, delta, gates, scale, out, out0, name, tq=None):
    bsz, seq, _ = qa.shape
    tq = ATTN_TILE if tq is None else tq
    n_q = seq // tq
    gated = gates is not None
    aliased = not isinstance(out, jax.ShapeDtypeStruct)

    def body(*refs):
        q_ref, k_ref, v_ref, do_ref, lse_ref, dl_ref = refs[:6]
        gate_refs = refs[6:8] if gated else None
        dkv_ref, dfk_ref, dk_s, dv_s, df_s = refs[-5:]
        h, j, i = pl.program_id(1), pl.program_id(2), pl.program_id(3)

        @pl.when(i == 0)
        def _():
            dk_s[...] = jnp.zeros_like(dk_s)
            dv_s[...] = jnp.zeros_like(dv_s)
            df_s[...] = jnp.zeros_like(df_s)

        def step(masked):
            s, q = _scores(q_ref, k_ref, gate_refs, scale, h, masked, tq, tq)
            p = jnp.exp(s - lse_ref[:, 0:1])
            do_b = do_ref[...].astype(BF16)
            dp = lax.dot_general(do_b, v_ref[...].astype(BF16), _DN["nt"], preferred_element_type=F32)
            ds = p * (dp - dl_ref[:, 0:1])
            dv_s[...] += lax.dot_general(p.astype(BF16), do_b, _DN["tn"], preferred_element_type=F32)
            dk_s[...] += lax.dot_general(ds.astype(BF16), q, _DN["tn"], preferred_element_type=F32)
            df_s[...] -= jnp.sum(ds, axis=0, keepdims=True)

        @pl.when(i > j)
        def _():
            step(False)

        @pl.when(i == j)
        def _():
            step(True)

        @pl.when(i == n_q - 1)
        def _():
            dkv_ref[:, 0:LANES] = dk_s[...].astype(dkv_ref.dtype)
            dkv_ref[:, LANES:2 * LANES] = dv_s[...].astype(dkv_ref.dtype)
            dfk_ref[...] = df_s[...]

    blk = (None, tq, LANES)
    col = pl.BlockSpec((None, None, tq, LANES), lambda b, h, j, i: (b, h, jnp.maximum(i, j), 0))
    in_specs = [pl.BlockSpec(blk, lambda b, h, j, i: (b, jnp.maximum(i, j), q0 + h)),
                pl.BlockSpec(blk, lambda b, h, j, i: (b, j, kv0 + 2 * h)),
                pl.BlockSpec(blk, lambda b, h, j, i: (b, j, kv0 + 2 * h + 1)),
                pl.BlockSpec(blk, lambda b, h, j, i: (b, jnp.maximum(i, j), o0 + h)), col, col]
    args = [qa, kva, kva, dmo, lse, delta]
    if gated:
        in_specs += [pl.BlockSpec(blk, lambda b, h, j, i: (b, jnp.maximum(i, j), 0)),
                     pl.BlockSpec((None, 8, tq), lambda b, h, j, i: (b, 0, j))]
        args += list(gates)
    aliases = {}
    if aliased:
        in_specs.append(pl.BlockSpec(memory_space=pl.ANY))
        args.append(out)
        aliases = {len(args) - 1: 0}
    return pl.pallas_call(
        body, name=name, grid=(bsz, N_HEADS, n_q, n_q), in_specs=in_specs,
        out_specs=[pl.BlockSpec((None, tq, 2 * LANES), lambda b, h, j, i: (b, j, out0 + h)),
                   pl.BlockSpec((None, None, 1, tq), lambda b, h, j, i: (b, h, 0, j))],
        out_shape=[jax.ShapeDtypeStruct(out.shape, out.dtype), jax.ShapeDtypeStruct((bsz, N_HEADS, 1, seq), F32)],
        scratch_shapes=[pltpu.VMEM((tq, LANES), F32), pltpu.VMEM((tq, LANES), F32), pltpu.VMEM((1, tq), F32)],
        input_output_aliases=aliases,
        compiler_params=_cparams(("parallel", "parallel", "parallel", "arbitrary")),
    )(*args)


def _block_logits(q, k_ref, gate, j, scale_unused, h, masked, tq):
    del scale_unused
    r = pl.multiple_of(j * tq, tq)
    s = lax.dot_general(q, k_ref[pl.ds(r, tq), :].astype(BF16), _DN["nt"], preferred_element_type=F32)
    if gate is not None:
        fcol, fr_ref = gate
        sub = lax.broadcasted_iota(jnp.int32, (8, tq), 0)
        frow = jnp.sum(jnp.where(sub == h, fr_ref[:, pl.ds(r, tq)], 0.0), axis=0, keepdims=True)
        s = s + (fcol - frow)
    if masked:
        r_i = lax.broadcasted_iota(jnp.int32, (tq, tq), 0)
        c_i = lax.broadcasted_iota(jnp.int32, (tq, tq), 1)
        s = jnp.where(c_i <= r_i, s, NEG)
    return s, r


def _gate_col(fc_ref, h, tq):
    lane = lax.broadcasted_iota(jnp.int32, (tq, LANES), 1)
    return jnp.sum(jnp.where(lane == h, fc_ref[...], 0.0), axis=1, keepdims=True)


def _attn_fwd_loop(qa, q0, kva, kv0, mo, o0, gates, scale, name, tq=None):
    bsz, seq, _ = qa.shape
    tq = ATTN_TILE if tq is None else tq
    n_q = seq // tq
    gated = gates is not None

    def body(*refs):
        q_ref, k_ref, v_ref = refs[:3]
        o_ref, lse_ref = refs[-2:]
        h, i = pl.program_id(1), pl.program_id(2)
        q = (q_ref[...].astype(F32) * scale).astype(BF16)
        gate = (_gate_col(refs[3], h, tq), refs[4]) if gated else None

        def step(j, carry, masked):
            m_prev, l_prev, acc = carry
            s, r = _block_logits(q, k_ref, gate, j, None, h, masked, tq)
            m_new = jnp.maximum(m_prev, jnp.max(s, axis=1, keepdims=True))
            alpha = jnp.exp(m_prev - m_new)
            p = jnp.exp(s - m_new)
            l_new = alpha * l_prev + jnp.sum(p, axis=1, keepdims=True)
            acc = alpha * acc + jnp.dot(p.astype(BF16), v_ref[pl.ds(r, tq), :].astype(BF16),
                                        preferred_element_type=F32)
            return m_new, l_new, acc

        init = (jnp.full((tq, 1), NEG, F32), jnp.zeros((tq, 1), F32), jnp.zeros((tq, LANES), F32))
        carry = lax.fori_loop(0, i, lambda j, c: step(j, c, False), init)
        m_f, l_f, acc = step(i, carry, True)
        o_ref[...] = (acc / l_f).astype(o_ref.dtype)
        lse_ref[...] = _lanes(m_f + jnp.log(l_f))

    blk = (None, tq, LANES)
    full = (None, seq, LANES)
    in_specs = [pl.BlockSpec(blk, lambda b, h, i: (b, i, q0 + h)),
                pl.BlockSpec(full, lambda b, h, i: (b, 0, kv0 + 2 * h)),
                pl.BlockSpec(full, lambda b, h, i: (b, 0, kv0 + 2 * h + 1))]
    args = [qa, kva, kva]
    if gated:
        in_specs += [pl.BlockSpec(blk, lambda b, h, i: (b, i, 0)),
                     pl.BlockSpec((None, 8, seq), lambda b, h, i: (b, 0, 0))]
        args += list(gates)
    in_specs.append(pl.BlockSpec(memory_space=pl.ANY))
    args.append(mo)
    return pl.pallas_call(
        body, name=name, grid=(bsz, N_HEADS, n_q), in_specs=in_specs,
        out_specs=[pl.BlockSpec(blk, lambda b, h, i: (b, i, o0 + h)),
                   pl.BlockSpec((None, None, tq, LANES), lambda b, h, i: (b, h, i, 0))],
        out_shape=[jax.ShapeDtypeStruct(mo.shape, mo.dtype),
                   jax.ShapeDtypeStruct((bsz, N_HEADS, seq, LANES), F32)],
        input_output_aliases={len(args) - 1: 0},
        compiler_params=_cparams(("parallel", "parallel", "parallel")),
    )(*args)


def _attn_bwd_q_loop(qa, q0, kva, kv0, mo, dmo, o0, lse, gates, scale, out, out0, name, tq=None):
    bsz, seq, _ = qa.shape
    tq = ATTN_TILE if tq is None else tq
    n_q = seq // tq
    gated = gates is not None
    aliased = not isinstance(out, jax.ShapeDtypeStruct)

    def body(*refs):
        q_ref, k_ref, v_ref, o_ref, do_ref, lse_ref = refs[:6]
        dq_ref, delta_ref, dfq_ref = refs[-3:]
        h, i = pl.program_id(1), pl.program_id(2)
        q = (q_ref[...].astype(F32) * scale).astype(BF16)
        gate = (_gate_col(refs[6], h, tq), refs[7]) if gated else None
        do_v = do_ref[...]
        do_b = do_v.astype(BF16)
        delta = jnp.sum(do_v * o_ref[...].astype(F32), axis=1, keepdims=True)
        lse_v = lse_ref[:, 0:1]

        def step(j, carry, masked):
            acc, dfq = carry
            s, r = _block_logits(q, k_ref, gate, j, None, h, masked, tq)
            p = jnp.exp(s - lse_v)
            dp = lax.dot_general(do_b, v_ref[pl.ds(r, tq), :].astype(BF16), _DN["nt"], preferred_element_type=F32)
            ds = p * (dp - delta)
            acc = acc + jnp.dot(ds.astype(BF16), k_ref[pl.ds(r, tq), :].astype(BF16), preferred_element_type=F32)
            return acc, dfq + jnp.sum(ds, axis=1, keepdims=True)

        init = (jnp.zeros((tq, LANES), F32), jnp.zeros((tq, 1), F32))
        carry = lax.fori_loop(0, i, lambda j, c: step(j, c, False), init)
        acc, dfq = step(i, carry, True)
        dq_ref[...] = (acc * scale).astype(dq_ref.dtype)
        delta_ref[...] = _lanes(delta)
        dfq_ref[...] = _lanes(dfq)

    blk = (None, tq, LANES)
    full = (None, seq, LANES)
    stat = pl.BlockSpec((None, None, tq, LANES), lambda b, h, i: (b, h, i, 0))
    in_specs = [pl.BlockSpec(blk, lambda b, h, i: (b, i, q0 + h)),
                pl.BlockSpec(full, lambda b, h, i: (b, 0, kv0 + 2 * h)),
                pl.BlockSpec(full, lambda b, h, i: (b, 0, kv0 + 2 * h + 1)),
                pl.BlockSpec(blk, lambda b, h, i: (b, i, o0 + h)),
                pl.BlockSpec(blk, lambda b, h, i: (b, i, o0 + h)), stat]
    args = [qa, kva, kva, mo, dmo, lse]
    if gated:
        in_specs += [pl.BlockSpec(blk, lambda b, h, i: (b, i, 0)),
                     pl.BlockSpec((None, 8, seq), lambda b, h, i: (b, 0, 0))]
        args += list(gates)
    aliases = {}
    if aliased:
        in_specs.append(pl.BlockSpec(memory_space=pl.ANY))
        args.append(out)
        aliases = {len(args) - 1: 0}
    vec = jax.ShapeDtypeStruct((bsz, N_HEADS, seq, LANES), F32)
    return pl.pallas_call(
        body, name=name, grid=(bsz, N_HEADS, n_q), in_specs=in_specs,
        out_specs=[pl.BlockSpec(blk, lambda b, h, i: (b, i, out0 + h)), stat, stat],
        out_shape=[jax.ShapeDtypeStruct(out.shape, out.dtype), vec, vec],
        input_output_aliases=aliases,
        compiler_params=_cparams(("parallel", "parallel", "parallel")),
    )(*args)


def _attn_bwd_kv_loop(qa, q0, kva, kv0, dmo, o0, lse, delta, gates, scale, out, out0, name, tq=None):
    bsz, seq, _ = qa.shape
    tq = ATTN_TILE if tq is None else tq
    n_q = seq // tq
    gated = gates is not None
    aliased = not isinstance(out, jax.ShapeDtypeStruct)

    def body(*refs):
        q_ref, k_ref, v_ref, do_ref, lse_ref, dl_ref = refs[:6]
        dkv_ref, dfk_ref = refs[-2:]
        h, j = pl.program_id(1), pl.program_id(2)
        k_b = k_ref[...].astype(BF16)
        v_b = v_ref[...].astype(BF16)
        if gated:
            fc_ref, fr_ref = refs[6], refs[7]
            sub = lax.broadcasted_iota(jnp.int32, (8, tq), 0)
            frow = jnp.sum(jnp.where(sub == h, fr_ref[...], 0.0), axis=0, keepdims=True)
            lane = lax.broadcasted_iota(jnp.int32, (tq, LANES), 1)

        def step(i, carry, masked):
            dk, dv, dfk = carry
            r = pl.multiple_of(i * tq, tq)
            q = (q_ref[pl.ds(r, tq), :].astype(F32) * scale).astype(BF16)
            s = lax.dot_general(q, k_b, _DN["nt"], preferred_element_type=F32)
            if gated:
                fcol = jnp.sum(jnp.where(lane == h, fc_ref[pl.ds(r, tq), :], 0.0), axis=1, keepdims=True)
                s = s + (fcol - frow)
            if masked:
                r_i = lax.broadcasted_iota(jnp.int32, (tq, tq), 0)
                c_i = lax.broadcasted_iota(jnp.int32, (tq, tq), 1)
                s = jnp.where(c_i <= r_i, s, NEG)
            p = jnp.exp(s - lse_ref[pl.ds(r, tq), 0:1])
            do_b = do_ref[pl.ds(r, tq), :].astype(BF16)
            dp = lax.dot_general(do_b, v_b, _DN["nt"], preferred_element_type=F32)
            ds = p * (dp - dl_ref[pl.ds(r, tq), 0:1])
            dv = dv + lax.dot_general(p.astype(BF16), do_b, _DN["tn"], preferred_element_type=F32)
            dk = dk + lax.dot_general(ds.astype(BF16), q, _DN["tn"], preferred_element_type=F32)
            return dk, dv, dfk - jnp.sum(ds, axis=0, keepdims=True)

        init = (jnp.zeros((tq, LANES), F32), jnp.zeros((tq, LANES), F32), jnp.zeros((1, tq), F32))
        carry = step(j, init, True)
        dk, dv, dfk = lax.fori_loop(j + 1, n_q, lambda i, c: step(i, c, False), carry)
        dkv_ref[:, 0:LANES] = dk.astype(dkv_ref.dtype)
        dkv_ref[:, LANES:2 * LANES] = dv.astype(dkv_ref.dtype)
        dfk_ref[...] = dfk

    blk = (None, tq, LANES)
    full = (None, seq, LANES)
    stat = pl.BlockSpec((None, None, seq, LANES), lambda b, h, j: (b, h, 0, 0))
    in_specs = [pl.BlockSpec(full, lambda b, h, j: (b, 0, q0 + h)),
                pl.BlockSpec(blk, lambda b, h, j: (b, j, kv0 + 2 * h)),
                pl.BlockSpec(blk, lambda b, h, j: (b, j, kv0 + 2 * h + 1)),
                pl.BlockSpec(full, lambda b, h, j: (b, 0, o0 + h)), stat, stat]
    args = [qa, kva, kva, dmo, lse, delta]
    if gated:
        in_specs += [pl.BlockSpec(full, lambda b, h, j: (b, 0, 0)),
                     pl.BlockSpec((None, 8, tq), lambda b, h, j: (b, 0, j))]
        args += list(gates)
    aliases = {}
    if aliased:
        in_specs.append(pl.BlockSpec(memory_space=pl.ANY))
        args.append(out)
        aliases = {len(args) - 1: 0}
    return pl.pallas_call(
        body, name=name, grid=(bsz, N_HEADS, n_q), in_specs=in_specs,
        out_specs=[pl.BlockSpec((None, tq, 2 * LANES), lambda b, h, j: (b, j, out0 + h)),
                   pl.BlockSpec((None, None, 1, tq), lambda b, h, j: (b, h, 0, j))],
        out_shape=[jax.ShapeDtypeStruct(out.shape, out.dtype), jax.ShapeDtypeStruct((bsz, N_HEADS, 1, seq), F32)],
        input_output_aliases=aliases,
        compiler_params=_cparams(("parallel", "parallel", "parallel")),
    )(*args)


def _gmlp_fn(uv, lng, lnb, ws, bst):
    u = jax.nn.gelu(uv[:, 0:GROUP_WIDTH])
    gv = jax.nn.gelu(uv[:, GROUP_WIDTH:2 * GROUP_WIDTH])
    mu = jnp.mean(gv, axis=-1, keepdims=True)
    vc = gv - mu
    var = jnp.mean(vc * vc, axis=-1, keepdims=True)
    vln = vc * lax.rsqrt(var + LN_EPS) * lng + lnb
    r_i = lax.broadcasted_iota(jnp.int32, (D_CHUNK, D_CHUNK), 0)
    c_i = lax.broadcasted_iota(jnp.int32, (D_CHUNK, D_CHUNK), 1)
    lane_g = lax.broadcasted_iota(jnp.int32, (D_CHUNK, GROUP_WIDTH), 1) // HEAD_DIM
    e_r = lax.broadcasted_iota(jnp.int32, (LANES, GROUP_WIDTH), 0)
    e_c = lax.broadcasted_iota(jnp.int32, (LANES, GROUP_WIDTH), 1)
    expand = (e_r == e_c // HEAD_DIM).astype(F32)
    mixed = jnp.dot(bst, expand, precision=HI, preferred_element_type=F32)
    for g in range(4):
        w = jnp.where(r_i >= c_i, ws[g], 0.0)
        mixed = mixed + jnp.where(lane_g == g, _bdot(w, vln, "nn"), 0.0)
    return u * mixed


def _gmlp_fwd(proj, mo, lng, lnb, ws, bst, name):
    bsz, seq, _ = proj.shape

    def body(p_ref, mo_any, lng_ref, lnb_ref, ws_ref, bst_ref, o_ref):
        del mo_any
        for c in range(GMLP_STEP):
            rows = slice(c * D_CHUNK, (c + 1) * D_CHUNK)
            o_ref[rows, :] = _gmlp_fn(p_ref[rows, :], lng_ref[...], lnb_ref[...], ws_ref[...],
                                      bst_ref[...]).astype(o_ref.dtype)

    blk = GMLP_STEP * D_CHUNK
    return pl.pallas_call(
        body, name=name, grid=(bsz, seq // blk),
        in_specs=[pl.BlockSpec((None, blk, 512), lambda b, s: (b, s, P_D // 512)),
                  pl.BlockSpec(memory_space=pl.ANY), _vec_spec(256), _vec_spec(256),
                  pl.BlockSpec((4, D_CHUNK, D_CHUNK), lambda b, s: (0, 0, 0)),
                  pl.BlockSpec((D_CHUNK, LANES), lambda b, s: (0, 0))],
        out_specs=pl.BlockSpec((None, blk, GROUP_WIDTH), lambda b, s: (b, s, 1280 // GROUP_WIDTH)),
        out_shape=jax.ShapeDtypeStruct(mo.shape, mo.dtype),
        input_output_aliases={1: 0},
        compiler_params=_cparams(("parallel", "parallel")),
    )(proj, mo, lng, lnb, ws, bst)


def _gmlp_bwd(dmo, dproj, proj, lng, lnb, ws, bst, name):
    bsz, seq, _ = proj.shape

    def body(do_ref, dp_any, p_ref, lng_ref, lnb_ref, ws_ref, bst_ref, dp_ref, dlg_ref, dlb_ref, dws_ref, dbst_ref):
        del dp_any
        first = jnp.logical_and(pl.program_id(0) == 0, pl.program_id(1) == 0)

        @pl.when(first)
        def _():
            dlg_ref[...] = jnp.zeros_like(dlg_ref)
            dlb_ref[...] = jnp.zeros_like(dlb_ref)
            dws_ref[...] = jnp.zeros_like(dws_ref)
            dbst_ref[...] = jnp.zeros_like(dbst_ref)

        for c in range(GMLP_STEP):
            rows = slice(c * D_CHUNK, (c + 1) * D_CHUNK)
            _, vjp = jax.vjp(_gmlp_fn, p_ref[rows, :], lng_ref[...], lnb_ref[...], ws_ref[...], bst_ref[...])
            duv, dlg, dlb, dws, dbst = vjp(do_ref[rows, :])
            dp_ref[rows, :] = duv.astype(dp_ref.dtype)
            dlg_ref[...] += dlg
            dlb_ref[...] += dlb
            dws_ref[...] += dws
            dbst_ref[...] += dbst

    const2 = lambda shape: pl.BlockSpec(shape, lambda b, s: (0,) * len(shape))
    blk = GMLP_STEP * D_CHUNK
    return pl.pallas_call(
        body, name=name, grid=(bsz, seq // blk),
        in_specs=[pl.BlockSpec((None, blk, GROUP_WIDTH), lambda b, s: (b, s, 1280 // GROUP_WIDTH)),
                  pl.BlockSpec(memory_space=pl.ANY),
                  pl.BlockSpec((None, blk, 512), lambda b, s: (b, s, P_D // 512)),
                  _vec_spec(256), _vec_spec(256), const2((4, D_CHUNK, D_CHUNK)), const2((D_CHUNK, LANES))],
        out_specs=[pl.BlockSpec((None, blk, 512), lambda b, s: (b, s, P_D // 512)),
                   _vec_spec(256), _vec_spec(256), const2((4, D_CHUNK, D_CHUNK)), const2((D_CHUNK, LANES))],
        out_shape=[jax.ShapeDtypeStruct(dproj.shape, dproj.dtype), jax.ShapeDtypeStruct((1, 256), F32),
                   jax.ShapeDtypeStruct((1, 256), F32), jax.ShapeDtypeStruct((4, D_CHUNK, D_CHUNK), F32),
                   jax.ShapeDtypeStruct((D_CHUNK, LANES), F32)],
        input_output_aliases={1: 0},
        compiler_params=_cparams(("arbitrary", "arbitrary")),
    )(dmo, dproj, proj, lng, lnb, ws, bst)


def _ada_fwd(c_all, ada_w, name):
    n_b = c_all.shape[0]
    depth, d, cols = ada_w.shape

    def body(c_ref, w_ref, o_ref):
        cv = c_ref[...]
        act = (cv * jax.nn.sigmoid(cv)).astype(BF16)
        o_ref[...] = jnp.dot(act, w_ref[...].astype(BF16), preferred_element_type=F32)

    return pl.pallas_call(
        body, name=name, grid=(depth,),
        in_specs=[pl.BlockSpec((n_b, d), lambda l: (0, 0)), pl.BlockSpec((None, d, cols), lambda l: (l, 0, 0))],
        out_specs=pl.BlockSpec((None, n_b, cols), lambda l: (l, 0, 0)),
        out_shape=jax.ShapeDtypeStruct((depth, n_b, cols), F32),
        compiler_params=_cparams(("parallel",)),
    )(c_all, ada_w)


def _ada_bwd(c_all, dmod_cols, dmod_full, name):
    n_b, d = c_all.shape
    depth, _, cols = dmod_cols.shape
    full = dmod_full.shape[-1]

    def body(c_ref, dm_ref, df_ref, gw_ref, gb_ref):
        cv = c_ref[...]
        act = (cv * jax.nn.sigmoid(cv)).astype(BF16)
        gw_ref[...] = lax.dot_general(act, dm_ref[...].astype(BF16), (((0,), (0,)), ((), ())),
                                      preferred_element_type=F32)
        gb_ref[...] = jnp.sum(df_ref[...], axis=0, keepdims=True)

    return pl.pallas_call(
        body, name=name, grid=(depth,),
        in_specs=[pl.BlockSpec((n_b, d), lambda l: (0, 0)), pl.BlockSpec((None, n_b, cols), lambda l: (l, 0, 0)),
                  pl.BlockSpec((None, n_b, full), lambda l: (l, 0, 0))],
        out_specs=[pl.BlockSpec((None, d, cols), lambda l: (l, 0, 0)),
                   pl.BlockSpec((None, 1, full), lambda l: (l, 0, 0))],
        out_shape=[jax.ShapeDtypeStruct((depth, d, cols), F32), jax.ShapeDtypeStruct((depth, 1, full), F32)],
        compiler_params=_cparams(("parallel",)),
    )(c_all, dmod_cols, dmod_full)


def _adamw(gparts, own, w, m, v, name, layer=0, prev=None):
    n_p, rows, cols = gparts.shape
    assert w.shape[1:] == (rows, cols)
    tr = rows
    if rows > 512:
        tr = next(c for c in range(512, 7, -8) if rows % c == 0)
    has_own = own is not None
    n_prev = 0 if prev is None else 4

    def body(*refs):
        if has_own:
            slot_ref, refs = refs[0], refs[1:]
        g_ref = refs[0]
        own_ref = refs[1] if has_own else None
        w_ref, m_ref, v_ref = refs[1 + has_own:4 + has_own]
        go_ref, do_ref, mo_ref, vo_ref = refs[4 + has_own + n_prev:]
        g = None
        for p in range(n_p):
            term = g_ref[p].astype(F32)
            if has_own:
                term = jnp.where(slot_ref[0] == p, own_ref[...].astype(F32), term)
            g = term if g is None else g + term
        m_new = ADAM_B1 * m_ref[...] + (1.0 - ADAM_B1) * g
        v_new = ADAM_B2 * v_ref[...] + (1.0 - ADAM_B2) * (g * g)
        m_hat = m_new / (1.0 - ADAM_B1 ** ADAM_STEP)
        v_hat = v_new / (1.0 - ADAM_B2 ** ADAM_STEP)
        go_ref[...] = g
        do_ref[...] = -ADAM_LR * (m_hat / (jnp.sqrt(v_hat) + ADAM_EPS) + ADAM_WD * w_ref[...])
        mo_ref[...] = m_new
        vo_ref[...] = v_new

    spec = pl.BlockSpec((None, tr, cols), lambda i, *_: (layer, i, 0))
    in_specs = [pl.BlockSpec((n_p, tr, cols), lambda i, *_: (0, i, 0))]
    args = [gparts]
    if has_own:
        in_specs.append(pl.BlockSpec((None, tr, cols), lambda i, slot: (slot[0], i, 0)))
        args.append(own[0])
    in_specs += [spec, spec, spec]
    args += [w, m, v]
    aliases = {}
    if prev is not None:
        aliases = {has_own + len(args) + k: k for k in range(4)}
        in_specs += [pl.BlockSpec(memory_space=pl.ANY)] * 4
        args += list(prev)
    shp = jax.ShapeDtypeStruct(w.shape, F32)
    out_specs, out_shape = [spec, spec, spec, spec], [shp, shp, shp, shp]
    if not has_own:
        return pl.pallas_call(
            body, name=name, grid=(rows // tr,), in_specs=in_specs, out_specs=out_specs, out_shape=out_shape,
            input_output_aliases=aliases, compiler_params=_cparams(("parallel",)),
        )(*args)
    return pl.pallas_call(
        body, name=name, out_shape=out_shape, input_output_aliases=aliases,
        grid_spec=pltpu.PrefetchScalarGridSpec(num_scalar_prefetch=1, grid=(rows // tr,), in_specs=in_specs,
                                               out_specs=out_specs),
        compiler_params=_cparams(("parallel",)),
    )(jnp.reshape(own[1], (1,)).astype(jnp.int32), *args)


def _sum_parts(parts, name):
    n_p, rows, cols = parts.shape
    tr = 256 if rows % 256 == 0 else rows

    def body(p_ref, o_ref):
        acc = p_ref[0]
        for p in range(1, n_p):
            acc = acc + p_ref[p]
        o_ref[...] = acc

    return pl.pallas_call(
        body, name=name, grid=(rows // tr,),
        in_specs=[pl.BlockSpec((n_p, tr, cols), lambda i: (0, i, 0))],
        out_specs=pl.BlockSpec((tr, cols), lambda i: (i, 0)),
        out_shape=jax.ShapeDtypeStruct((rows, cols), F32),
        compiler_params=_cparams(("parallel",)),
    )(parts)


def _all_gather(arrs, name):
    n = len(arrs)

    def body(*refs):
        in_refs, out_refs = refs[:n], refs[n:2 * n]
        send_sems, recv_sems, loc_sems = refs[2 * n:]
        x, y, c = lax.axis_index("x"), lax.axis_index("y"), lax.axis_index("c")
        me, sibling = (x, y, c), (x, y, 1 - c)
        chips = [(1 - x, y), (x, 1 - y), (1 - x, 1 - y)]

        def copy(a, k, block, to, src=None):
            slot = out_refs[a].at[4 * block[0] + 2 * block[1] + block[2]]
            return pltpu.make_async_remote_copy(
                src_ref=slot if src is None else src, dst_ref=slot, send_sem=send_sems.at[a, k],
                recv_sem=recv_sems.at[a, k], device_id=to, device_id_type=pl.DeviceIdType.MESH)

        mine = [pltpu.make_async_copy(in_refs[a], out_refs[a].at[4 * x + 2 * y + c], loc_sems.at[a])
                for a in range(n)]
        for cp in mine:
            cp.start()
        first = []
        for a in range(n):
            first.append(copy(a, 0, me, sibling, src=in_refs[a]))
            first += [copy(a, 1 + j, me, (*chip, c), src=in_refs[a]) for j, chip in enumerate(chips)]
        for cp in first:
            cp.start()
        passed = []
        for j, chip in enumerate(chips):
            for a in range(n):
                copy(a, 1 + j, (*chip, c), me).wait_recv()
                cp = copy(a, 4 + j, (*chip, c), sibling)
                cp.start()
                passed.append(cp)
        for a in range(n):
            copy(a, 0, sibling, me).wait_recv()
        for j, chip in enumerate(chips):
            for a in range(n):
                copy(a, 4 + j, (*chip, 1 - c), me).wait_recv()
        for cp in first + passed:
            cp.wait_send()
        for cp in mine:
            cp.wait()

    any_spec = pl.BlockSpec(memory_space=pl.ANY)
    return pl.pallas_call(
        body, name=name, in_specs=[any_spec] * n, out_specs=[any_spec] * n,
        out_shape=[jax.ShapeDtypeStruct((N_DEV,) + a.shape, a.dtype) for a in arrs],
        scratch_shapes=[pltpu.SemaphoreType.DMA((n, N_DEV - 1)), pltpu.SemaphoreType.DMA((n, N_DEV - 1)),
                        pltpu.SemaphoreType.DMA((n,))],
    )(*arrs)


def _flip_peers():
    x, y, c = lax.axis_index("x"), lax.axis_index("y"), lax.axis_index("c")
    peers = []
    for fx, fy, fc in [(fx, fy, fc) for fx in (0, 1) for fy in (0, 1) for fc in (0, 1)][1:]:
        px, py, pc = (1 - x if fx else x), (1 - y if fy else y), (1 - c if fc else c)
        peers.append(((px, py, pc), 4 * px + 2 * py + pc))
    return 4 * x + 2 * y + c, peers


def _push_start(srcs, name, whole=False):
    n, n_peer = len(srcs), N_DEV - 1
    if whole:
        me_w = 4 * lax.axis_index("x") + 2 * lax.axis_index("y") + lax.axis_index("c")
        lands = [lax.dynamic_update_slice_in_dim(lax.empty((N_DEV,) + a.shape, a.dtype), a[None], me_w, axis=0)
                 for a in srcs]
    else:
        lands = [lax.empty(a.shape, a.dtype) for a in srcs]

    def body(*refs):
        src_refs, land_refs = refs[:n], refs[n:2 * n]
        send_sems, recv_sems = refs[2 * n], refs[2 * n + 1]
        token = refs[-1]
        me, peers = _flip_peers()
        for k, (dev, idx) in enumerate(peers):
            for a in range(n):
                pltpu.make_async_remote_copy(
                    src_ref=src_refs[a] if whole else src_refs[a].at[idx], dst_ref=land_refs[a].at[me],
                    send_sem=send_sems.at[a * n_peer + k], recv_sem=recv_sems.at[a * n_peer + k], device_id=dev,
                    device_id_type=pl.DeviceIdType.MESH).start()
        token[...] = jnp.zeros_like(token)

    hbm = pl.BlockSpec(memory_space=pltpu.HBM)
    sem = pl.BlockSpec(memory_space=pltpu.SEMAPHORE)
    arrs = list(srcs) + lands
    res = pl.pallas_call(
        body, name=name, in_specs=[hbm] * (2 * n),
        out_specs=(sem, sem, *[hbm] * (2 * n), pl.BlockSpec(memory_space=pltpu.VMEM)),
        out_shape=(pltpu.SemaphoreType.DMA((n * n_peer,)), pltpu.SemaphoreType.DMA((n * n_peer,)),
                   *[pltpu.HBM(a.shape, a.dtype) for a in arrs], jax.ShapeDtypeStruct((8, LANES), F32)),
        input_output_aliases={i: 2 + i for i in range(2 * n)},
        compiler_params=pltpu.CompilerParams(has_side_effects=pltpu.SideEffectType.DATAFLOW_SIDE_EFFECTING),
    )(*[pltpu.with_memory_space_constraint(a, pltpu.HBM) for a in arrs])
    return res[0], res[1], list(res[2:2 + n]), list(res[2 + n:2 + 2 * n]), res[-1]


def _push_wait(send_sems, recv_sems, srcs, lands, after, name, whole=False):
    n, n_peer = len(srcs), N_DEV - 1

    def body(*refs):
        src_refs, land_refs = refs[:n], refs[n:2 * n]
        send_s, recv_s = refs[2 * n], refs[2 * n + 1]
        _, peers = _flip_peers()
        for k, (dev, idx) in enumerate(peers):
            for a in range(n):
                cp = pltpu.make_async_remote_copy(
                    src_ref=src_refs[a] if whole else src_refs[a].at[idx], dst_ref=land_refs[a].at[idx],
                    send_sem=send_s.at[a * n_peer + k],
                    recv_sem=recv_s.at[a * n_peer + k], device_id=dev, device_id_type=pl.DeviceIdType.MESH)
                cp.wait_send()
                cp.wait_recv()

    hbm = pl.BlockSpec(memory_space=pltpu.HBM)
    sem = pl.BlockSpec(memory_space=pltpu.SEMAPHORE)
    arrs = list(srcs) + list(lands)
    res = pl.pallas_call(
        body, name=name, in_specs=[hbm] * (2 * n) + [sem, sem, pl.BlockSpec(memory_space=pl.ANY)],
        out_specs=tuple([hbm] * (2 * n)), out_shape=tuple(pltpu.HBM(a.shape, a.dtype) for a in arrs),
        input_output_aliases={i: i for i in range(2 * n)},
        compiler_params=pltpu.CompilerParams(has_side_effects=pltpu.SideEffectType.DATAFLOW_SIDE_EFFECTING),
    )(*arrs, send_sems, recv_sems, after)
    return list(res[:n]), list(res[n:])


def _ffn_fwd(x, h, mod, w_in, w_out_after, lng, lnb, rows, tag, nxt):
    bsz, seq, d = x.shape
    t = bsz * seq
    if h is None:
        h = _modulate(x, mod, rows[0], rows[1], f"modulate_{tag}")
    z, a = _ffn_in_swiglu(h.reshape(t, d), w_in, f"ffn_in_{tag}")
    f = _matmul_groupsum(a, w_out_after(a), out_dtype=F32, tm=512, name=f"ffn_out_{tag}").reshape(bsz, seq, d)
    y, h_next = _res_ln(x, f, mod, lng, lnb, rows[2], 0.5, f"res_ln_{tag}", nxt)
    return y, h_next, (x, h, z, a, f)


def _tied(mod, tie):
    return mod if tie is None else mod + tie


def _open_tail(tail):
    dh, x, mod, dx_res, sc_row = tail
    return dx_res, (dh, x, mod, sc_row)


def _ffn_bwd(dy, pre, saved, mod, w_in, w_out, lng, lnb, rows, tag, ready):
    x, h, z, a, f = saved
    bsz, seq, d = x.shape
    t = bsz * seq
    (dx_res, df, dgate, dlg, dlb), closed = _res_ln_bwd(dy, x, f, mod, lng, lnb, rows[2], 0.5,
                                                       f"res_ln_bwd_{tag}", pre)
    df2 = df.reshape(1, t, d)
    dw_out = _matmul(a, df2, mode="tn", group_out=True, out_dtype=BF16, tm=a.shape[2], tk=min(t, 2048),
                     name=f"ffn_out_dw_{tag}")
    tie_out = ready(f"{tag}_out", dw_out)
    dz = _ffn_out_dx_swiglu(df.reshape(t, d), w_out, z, f"ffn_out_dx_{tag}").reshape(N_DEV, t, -1)
    dw_in = _matmul(dz, h.reshape(1, t, d), mode="tn", group_out=True, out_dtype=BF16, tm=dz.shape[2],
                    tk=min(t, 2048), name=f"ffn_in_dw_{tag}")
    tie_in = ready(f"{tag}_in", dw_in)
    dh = _matmul_groupsum(dz, w_in, out_dtype=F32, tm=512, name=f"ffn_in_dx_{tag}").reshape(bsz, seq, d)
    tail = (dh, x, _tied(_tied(mod, tie_out), tie_in), dx_res, rows[1])
    return tail, closed, dgate, dw_in, dw_out, dlg, dlb


def _mixer_fwd(x, h, mod, wts, small, lng, lnb, layer, tabs):
    bsz, seq, d = x.shape
    t = bsz * seq
    proj = _matmul(h.reshape(1, t, d), wts["mix_in"][None], mode="nn", group_out=True, out_dtype=F32, tm=512, tk=d,
                   name="mix_in").reshape(bsz, seq, PACK_W)
    mo, states = _hgrn_fwd(proj, small["lb_logits8"], small["hgrn_norm_g"], layer, f"hgrn_fwd_l{layer}")
    q, kv = _mla_pre(proj, small["q_norm_g"], small["kv_norm_g"], wts["uq"], wts["ukv"], tabs, "mla_pre")
    mla_scale = float((B_NOPE + B_ROPE) ** -0.5)
    mo, lse_b = _attn_fwd_loop(q, 0, kv, 0, mo, 2, None, mla_scale, "mla_attn_fwd")
    fg = _fox_gate(proj, small["fox_b_f"], "fox_gate")
    gates = (fg, jnp.swapaxes(fg[:, :, 0:8], 1, 2))
    fox_scale = float(HEAD_DIM ** -0.5)
    mo, lse_c = _attn_fwd_loop(proj, P_CQ // LANES, proj, P_CKV // LANES, mo, 6, gates, fox_scale, "fox_attn_fwd")
    mo = _gmlp_fwd(proj, mo, small["gmlp_ln_g"], small["gmlp_ln_b"], small["gmlp_w_s"], small["gmlp_bst"],
                   "gmlp_fwd")
    mixed = _matmul(mo.reshape(1, t, MO_W), wts["mix_out"][None], mode="nn", group_out=True, out_dtype=F32,
                    tm=1024, tk=MO_W, name="mix_out").reshape(bsz, seq, d)
    y, h_next = _res_ln(x, mixed, mod, lng, lnb, 5, 1.0, "res_ln_mix", (mod, 6, 7))
    return y, h_next, (x, h, proj, mo, states, q, kv, lse_b, gates, lse_c, mixed)


def _mixer_bwd(dy, pre, saved, mod, wts, small, lng, lnb, layer, tabs, ready):
    x, h, proj, mo, states, q, kv, lse_b, gates, lse_c, mixed = saved
    bsz, seq, d = x.shape
    t = bsz * seq
    (dx_res, dmixed, dgate, dlg, dlb), closed = _res_ln_bwd(dy, x, mixed, mod, lng, lnb, 5, 1.0, "res_ln_bwd_mix",
                                                           pre)
    dm2 = dmixed.reshape(1, t, d)
    dmo = _matmul(dm2, wts["mix_out"][None], mode="nt", group_out=True, out_dtype=F32, tm=1024, tk=d,
                  name="mix_out_dx").reshape(bsz, seq, MO_W)
    dw_out = _matmul(mo.reshape(1, t, MO_W), dm2, mode="tn", group_out=True, out_dtype=F32, tm=512, tk=min(t, 2048),
                     name="mix_out_dw")[0]
    tie_out = ready("mix_out", dw_out)
    g = {}
    dproj, g["lb_logits8"], g["hgrn_norm_g"] = _hgrn_bwd(dmo, proj, states, small["lb_logits8"],
                                                         small["hgrn_norm_g"], layer, f"hgrn_bwd_l{layer}")
    mla_scale = float((B_NOPE + B_ROPE) ** -0.5)
    dq, delta_b, _ = _attn_bwd_q_loop(q, 0, kv, 0, mo, dmo, 2, lse_b, None, mla_scale,
                                 jax.ShapeDtypeStruct((bsz, seq, 512), F32), 0, "mla_attn_bwd_q")
    dkv, _ = _attn_bwd_kv_loop(q, 0, kv, 0, dmo, 2, lse_b, delta_b, None, mla_scale,
                          jax.ShapeDtypeStruct((bsz, seq, 1024), F32), 0, "mla_attn_bwd_kv")
    dproj, g["q_norm_g"], g["kv_norm_g"], g["uq"], g["ukv"] = _mla_pre_bwd(
        dq, dkv, dproj, proj, small["q_norm_g"], small["kv_norm_g"], wts["uq"], wts["ukv"], tabs, "mla_pre_bwd")
    ready("mla_uq", g.pop("uq"))
    ready("mla_ukv", g.pop("ukv"))
    fox_scale = float(HEAD_DIM ** -0.5)
    dproj, delta_c, dfq = _attn_bwd_q_loop(proj, P_CQ // LANES, proj, P_CKV // LANES, mo, dmo, 6, lse_c, gates,
                                      fox_scale, dproj, P_CQ // LANES, "fox_attn_bwd_q")
    dproj, dfk = _attn_bwd_kv_loop(proj, P_CQ // LANES, proj, P_CKV // LANES, dmo, 6, lse_c, delta_c, gates, fox_scale,
                              dproj, P_CKV // (2 * LANES), "fox_attn_bwd_kv")
    dfk_cols = jnp.pad(jnp.swapaxes(dfk[:, :, 0, :], 1, 2), ((0, 0), (0, 0), (0, LANES - N_HEADS)))
    dproj, g["fox_b_f"] = _fox_gate_bwd(dfq, dfk_cols, dproj, proj, small["fox_b_f"], "fox_gate_bwd")
    dproj, g["gmlp_ln_g"], g["gmlp_ln_b"], g["gmlp_w_s"], g["gmlp_bst"] = _gmlp_bwd(
        dmo, dproj, proj, small["gmlp_ln_g"], small["gmlp_ln_b"], small["gmlp_w_s"], small["gmlp_bst"], "gmlp_bwd")
    dp2 = dproj.reshape(1, t, PACK_W)
    dw_in = _matmul(h.reshape(1, t, d), dp2, mode="tn", group_out=True, out_dtype=BF16, tm=512, tk=1024,
                    name="mix_in_dw")[0]
    tie_in = ready("mix_in", dw_in)
    dh = _matmul(dp2, wts["mix_in"][None], mode="nt", group_out=True, out_dtype=F32, tm=512, tk=PACK_W,
                 name="mix_in_dx").reshape(bsz, seq, d)
    tail = (dh, x, _tied(_tied(mod, tie_out), tie_in), dx_res, 4)
    return tail, closed, dgate, dw_in, dw_out, g, dlg, dlb


def _small_views(p, layer):
    return {
        "lb_logits8": jnp.pad(p["hgrn_lb_logits"], ((0, 8 - DEPTH), (0, 0))),
        "hgrn_norm_g": p["hgrn_norm_g"][layer][None],
        "q_norm_g": p["mla_q_norm_g"][layer][None],
        "kv_norm_g": p["mla_kv_norm_g"][layer][None],
        "fox_b_f": jnp.pad(p["fox_b_f"][layer][None], ((0, 0), (0, LANES - N_HEADS))),
        "gmlp_ln_g": p["gmlp_ln_g"][layer][None],
        "gmlp_ln_b": p["gmlp_ln_b"][layer][None],
        "gmlp_w_s": p["gmlp_w_s"][layer],
        "gmlp_bst": jnp.pad(p["gmlp_b_s"][layer].T, ((0, 0), (0, LANES - N_HEADS))),
    }


def _local_step(x, mod, target, weights, p, grads_ready=None):
    bsz, seq, d = x.shape
    tabs = _rope_tables(seq)
    saved = []
    h = None
    for l in range(DEPTH):
        sm = _small_views(p, l)
        lng, lnb = p["ln_g"][l], p["ln_b"][l]
        x, h, s1 = _ffn_fwd(x, h, mod[l], weights(l, "ffn1_in", x)["ffn1_in"],
                            lambda a, l=l: weights(l, "ffn1_out", a)["ffn1_out"], lng[0:1], lnb[0:1], (0, 1, 2),
                            "ffn1", (mod[l], 3, 4))
        x, h, s2 = _mixer_fwd(x, h, mod[l], weights(l, "mix", x), sm, lng[1:2], lnb[1:2], l, tabs)
        x, h, s3 = _ffn_fwd(x, h, mod[l], weights(l, "ffn2_in", x)["ffn2_in"],
                            lambda a, l=l: weights(l, "ffn2_out", a)["ffn2_out"], lng[2:3], lnb[2:3], (6, 7, 8),
                            "ffn2", (mod[l + 1], 0, 1) if l + 1 < DEPTH else None)
        saved.append((s1, s2, s3))
    dx, loss = _loss_head(x, target, "loss_head")
    big, small, dmods = [None] * DEPTH, [None] * DEPTH, [None] * DEPTH
    ties = []
    tail, rows_of = None, {}

    def tied(a):
        for t in ties:
            a = a + t
        return a

    for l in reversed(range(DEPTH)):
        w = {}
        for part in ("ffn1_in", "ffn1_out", "mix", "ffn2_in", "ffn2_out"):
            w.update(weights(l, part, None))
        sm = _small_views(p, l)
        lng, lnb = p["ln_g"][l], p["ln_b"][l]
        s1, s2, s3 = saved[l]

        def ready(name, grad, l=l):
            tie = None if grads_ready is None else grads_ready(l, name, grad)
            if tie is not None:
                ties.append(tie)
            return tie

        dy, pre = (dx, None) if tail is None else _open_tail(tail)
        tail, closed, dgate3, dwi2, dwo2, dlg2, dlb2 = _ffn_bwd(dy, pre, s3, tied(mod[l]), w["ffn2_in"],
                                                                w["ffn2_out"], lng[2:3], lnb[2:3], (6, 7, 8), "ffn2",
                                                                ready)
        if closed is not None:
            rows_of[(l + 1, 0)], rows_of[(l + 1, 1)] = closed
        dy, pre = _open_tail(tail)
        tail, closed, dgate2, dwmi, dwmo, g, dlg1, dlb1 = _mixer_bwd(dy, pre, s2, tied(mod[l]), w, sm, lng[1:2],
                                                                     lnb[1:2], l, tabs, ready)
        rows_of[(l, 6)], rows_of[(l, 7)] = closed
        dy, pre = _open_tail(tail)
        tail, closed, dgate1, dwi1, dwo1, dlg0, dlb0 = _ffn_bwd(dy, pre, s1, tied(mod[l]), w["ffn1_in"],
                                                                w["ffn1_out"], lng[0:1], lnb[0:1], (0, 1, 2), "ffn1",
                                                                ready)
        rows_of[(l, 3)], rows_of[(l, 4)] = closed
        rows_of[(l, 2)], rows_of[(l, 5)], rows_of[(l, 8)] = dgate1, dgate2, dgate3
        big[l] = {"ffn1_in": dwi1, "ffn1_out": dwo1, "ffn2_in": dwi2, "ffn2_out": dwo2, "mix_in": dwmi,
                  "mix_out": dwmo}
        g["ln_g"] = jnp.concatenate([dlg0, dlg1, dlg2], axis=0)
        g["ln_b"] = jnp.concatenate([dlb0, dlb1, dlb2], axis=0)
        small[l] = g
    dh, x0, mod0, dx_res, sc_row = tail
    dx, rows_of[(0, 0)], rows_of[(0, 1)] = _modulate_bwd(dh, x0, mod0, dx_res, sc_row, "modulate_bwd_ffn1")
    dmods = [jnp.concatenate([rows_of[(l, r)] for r in range(N_MOD)], axis=1) for l in range(DEPTH)]
    return loss, dx, jnp.stack(dmods), big, small


_BIG = ("ffn1_in", "ffn1_out", "ffn2_in", "ffn2_out", "mix_in", "mix_out")


def _small_grad_list(small, loss):
    def both(fn):
        return jnp.stack([fn(small[l]) for l in range(DEPTH)])

    return [
        ("loss", loss.reshape(1)),
        ("ln_g", both(lambda g: g["ln_g"])), ("ln_b", both(lambda g: g["ln_b"])),
        ("hgrn_lb_logits", small[0]["lb_logits8"][:DEPTH] + small[1]["lb_logits8"][:DEPTH]),
        ("hgrn_norm_g", both(lambda g: g["hgrn_norm_g"][0])),
        ("mla_q_norm_g", both(lambda g: g["q_norm_g"][0])),
        ("mla_kv_norm_g", both(lambda g: g["kv_norm_g"][0])),
        ("fox_b_f", both(lambda g: g["fox_b_f"][0, :N_HEADS])),
        ("gmlp_ln_g", both(lambda g: g["gmlp_ln_g"][0])), ("gmlp_ln_b", both(lambda g: g["gmlp_ln_b"][0])),
        ("gmlp_w_s", both(lambda g: g["gmlp_w_s"])),
        ("gmlp_b_s", both(lambda g: g["gmlp_bst"][:, :N_HEADS].T)),
    ]


_PACK_COLS = 512


def _pack_small(items):
    flat = jnp.concatenate([a.reshape(-1).astype(F32) for _, a in items])
    n = flat.shape[0]
    tile = 8 * _PACK_COLS
    flat = jnp.pad(flat, (0, (-n) % tile))
    return flat.reshape(-1, _PACK_COLS)


def _unpack_small(buf, items):
    flat = buf.reshape(-1)
    out, off = {}, 0
    for name, a in items:
        out[name] = flat[off:off + a.size].reshape(a.shape)
        off += a.size
    return out


def _as2d(a):
    return a.reshape(-1, a.shape[-1])


def kernel(x, c, ada_w, ada_b, ln_g, ln_b, ffn1_w_in, ffn1_w_out, ffn2_w_in, ffn2_w_out, mix_w_in, mix_w_out, hgrn_lb_logits, hgrn_norm_g, mla_q_norm_g, mla_kv_norm_g, mla_w_uq, mla_w_ukv, fox_b_f, gmlp_ln_g, gmlp_ln_b, gmlp_w_s, gmlp_b_s, loss_target, m_ada_w, m_ada_b, m_ln_g, m_ln_b, m_ffn1_w_in, m_ffn1_w_out, m_ffn2_w_in, m_ffn2_w_out, m_mix_w_in, m_mix_w_out, m_hgrn_lb_logits, m_hgrn_norm_g, m_mla_q_norm_g, m_mla_kv_norm_g, m_mla_w_uq, m_mla_w_ukv, m_fox_b_f, m_gmlp_ln_g, m_gmlp_ln_b, m_gmlp_w_s, m_gmlp_b_s, v_ada_w, v_ada_b, v_ln_g, v_ln_b, v_ffn1_w_in, v_ffn1_w_out, v_ffn2_w_in, v_ffn2_w_out, v_mix_w_in, v_mix_w_out, v_hgrn_lb_logits, v_hgrn_norm_g, v_mla_q_norm_g, v_mla_kv_norm_g, v_mla_w_uq, v_mla_w_ukv, v_fox_b_f, v_gmlp_ln_g, v_gmlp_ln_b, v_gmlp_w_s, v_gmlp_b_s):
    names = ["ada_w", "ada_b", "ln_g", "ln_b", "ffn1_w_in", "ffn1_w_out", "ffn2_w_in", "ffn2_w_out", "mix_w_in",
             "mix_w_out", "hgrn_lb_logits", "hgrn_norm_g", "mla_q_norm_g", "mla_kv_norm_g", "mla_w_uq", "mla_w_ukv",
             "fox_b_f", "gmlp_ln_g", "gmlp_ln_b", "gmlp_w_s", "gmlp_b_s"]
    w = dict(zip(names, [ada_w, ada_b, ln_g, ln_b, ffn1_w_in, ffn1_w_out, ffn2_w_in, ffn2_w_out, mix_w_in, mix_w_out,
                         hgrn_lb_logits, hgrn_norm_g, mla_q_norm_g, mla_kv_norm_g, mla_w_uq, mla_w_ukv, fox_b_f,
                         gmlp_ln_g, gmlp_ln_b, gmlp_w_s, gmlp_b_s]))
    m = dict(zip(names, [m_ada_w, m_ada_b, m_ln_g, m_ln_b, m_ffn1_w_in, m_ffn1_w_out, m_ffn2_w_in, m_ffn2_w_out,
                         m_mix_w_in, m_mix_w_out, m_hgrn_lb_logits, m_hgrn_norm_g, m_mla_q_norm_g, m_mla_kv_norm_g,
                         m_mla_w_uq, m_mla_w_ukv, m_fox_b_f, m_gmlp_ln_g, m_gmlp_ln_b, m_gmlp_w_s, m_gmlp_b_s]))
    v = dict(zip(names, [v_ada_w, v_ada_b, v_ln_g, v_ln_b, v_ffn1_w_in, v_ffn1_w_out, v_ffn2_w_in, v_ffn2_w_out,
                         v_mix_w_in, v_mix_w_out, v_hgrn_lb_logits, v_hgrn_norm_g, v_mla_q_norm_g, v_mla_kv_norm_g,
                         v_mla_w_uq, v_mla_w_ukv, v_fox_b_f, v_gmlp_ln_g, v_gmlp_ln_b, v_gmlp_w_s, v_gmlp_b_s]))
    bsz, seq, d = x.shape
    me = 4 * lax.axis_index("x") + 2 * lax.axis_index("y") + lax.axis_index("c")
    mix_src, uq_src, ukv_src, mo_src = _mix_in_src(), _uq_src(), _ukv_src(), _mo_src()

    part_names = {"ffn1_in": ["ffn1_w_in"], "ffn1_out": ["ffn1_w_out"],
                  "mix": ["mix_w_in", "mix_w_out", "mla_w_uq", "mla_w_ukv"],
                  "ffn2_in": ["ffn2_w_in"], "ffn2_out": ["ffn2_w_out"]}
    group_of = {(l, part): (l, part) for l in range(DEPTH) for part in part_names}
    in_flight = {}
    transposed = ("ffn1_w_in", "ffn2_w_in")

    def start_group(key, behind=None):
        members = [(l, part) for (l, part), g in group_of.items() if g == key]
        labels = [(l, n) for l, part in members for n in part_names[part]]
        shards = []
        for l, n in labels:
            a = w[n][l]
            if n == "mix_w_in":
                a = _pack_cols(a, mix_src)
            if n in transposed:
                a = jnp.swapaxes(w[n], 1, 2)[l]
            shards.append(a.astype(BF16))
        if behind is not None:
            shards, _ = lax.optimization_barrier((shards, behind))
        in_flight[key] = (labels, _push_start(shards, f"gather_start_{key[0]}_{key[1]}", whole=True))

    keys_in_order = list(dict.fromkeys(group_of.values()))
    start_group(keys_in_order[0])

    gathered = _all_gather([c, ln_g, ln_b], "gather_inputs")
    c_all = gathered[0].reshape(N_DEV * bsz, d)
    ln_g_full = jnp.moveaxis(gathered[1], 0, 2).reshape(DEPTH, 3, d)
    ln_b_full = jnp.moveaxis(gathered[2], 0, 2).reshape(DEPTH, 3, d)

    mod_cols = _ada_fwd(c_all, ada_w, "ada_fwd")
    mod_all, = _all_gather([mod_cols], "gather_mod")
    mod_mine = lax.dynamic_slice_in_dim(mod_all, me * bsz, bsz, axis=2)
    mod = jnp.moveaxis(mod_mine, 0, 2).reshape(DEPTH, bsz, N_MOD * d) + ada_b[:, None, :]
    for key in keys_in_order[1:]:
        start_group(key, behind=mod)
    tie = sum(h[-1][0, 0] for _, h in in_flight.values())
    mod = mod.reshape(DEPTH, bsz, N_MOD, d) + tie

    arrived, laid_out = {}, {}

    def weights(l, part, after):
        if (l, part) not in laid_out:
            laid_out[(l, part)] = lay_out(l, part, after)
        return laid_out[(l, part)]

    def lay_out(l, part, after):
        key = group_of[(l, part)]
        if key not in arrived:
            labels, (send_sems, recv_sems, srcs, lands, _) = in_flight[key]
            _, lands = _push_wait(send_sems, recv_sems, srcs, lands, after, f"gather_wait_{key[0]}_{key[1]}",
                                  whole=True)
            arrived[key] = dict(zip(labels, lands))
        gw = {n: arrived[key][(l, n)] for n in part_names[part]}
        if part.endswith("_in"):
            return {part: gw[part_names[part][0]]}
        if part.endswith("_out"):
            return {part: gw[part_names[part][0]].reshape(4, 704, d)}
        uq = jnp.moveaxis(gw["mla_w_uq"], 0, 1).reshape(256, 384)
        ukv = jnp.moveaxis(gw["mla_w_ukv"], 0, 1).reshape(128, 512)
        return {"mix_in": gw["mix_w_in"].reshape(d, PACK_W),
                "mix_out": _pack_cols(gw["mix_w_out"].reshape(d, d).T, mo_src).T,
                "uq": _pack_cols(uq, uq_src), "ukv": _pack_cols(ukv, ukv_src)}

    p = dict(w)
    p["ln_g"], p["ln_b"] = ln_g_full, ln_b_full
    def chunks(name, arr):
        if name in ("ffn1_in", "ffn2_in"):
            return arr
        if name in ("ffn1_out", "ffn2_out"):
            return arr.reshape(N_DEV, arr.shape[1] // 2, d)
        if name == "mix_in":
            return _unpack_cols(arr, mix_src, MIX_ORIG_W).reshape(N_DEV, d // N_DEV, MIX_ORIG_W)
        if name in ("mla_uq", "mla_ukv"):
            full_w = _unpack_cols(arr, uq_src, 384) if name == "mla_uq" else _unpack_cols(arr, ukv_src, 512)
            rows = full_w.shape[0]
            return jnp.moveaxis(full_w.reshape(rows, N_DEV, -1), 1, 0).astype(BF16)
        return _unpack_cols(arr.T, mo_src, d).T.astype(BF16).reshape(N_DEV, d // N_DEV, d)

    pending, started = {}, []

    def grads_ready(l, name, grad):
        pending[(name, l)] = chunks(name, grad)
        flush = name == "ffn1_in" if l > 0 else name in ("ffn2_in", "mix_out", "mix_in", "ffn1_out", "ffn1_in")
        if not flush:
            return None
        keys = sorted(pending)
        handles = _push_start([pending[k] for k in keys], f"push_start_{len(started)}")
        pending.clear()
        started.append((keys, handles, l == 0 and name.startswith("ffn1")))
        return handles[-1][0, 0]

    loss, grad_x, dmod, big, small = _local_step(x, mod, loss_target, weights, p, grads_ready)
    del big

    recv, out = {}, {}

    def arrive(n, after):
        keys, (send_sems, recv_sems, srcs, lands, _), _ = started[n]
        srcs, lands = _push_wait(send_sems, recv_sems, srcs, lands, after, f"push_wait_{n}")
        for k, src, land in zip(keys, srcs, lands):
            recv[k] = (land, src)

    big_of = {"ffn1_w_in": "ffn1_in", "ffn1_w_out": "ffn1_out", "ffn2_w_in": "ffn2_in", "ffn2_w_out": "ffn2_out",
              "mix_w_in": "mix_in", "mix_w_out": "mix_out", "mla_w_uq": "mla_uq", "mla_w_ukv": "mla_ukv"}
    chain = {name: None for name in big_of}

    def big_update(key, l):
        name = next(nm for nm, k in big_of.items() if k == key)
        parts, src = recv[(key, l)]
        view =(lambda a: jnp.swapaxes(a, 1, 2)) if name in transposed else (lambda a: a)
        chain[name] = _adamw(parts, (src, me), view(w[name]), view(m[name]), view(v[name]), f"adamw_{name}_l{l}",
                             layer=l, prev=chain[name])

    def update(name, grad):
        shape = w[name].shape
        as3 = lambda a: a.reshape(1, -1, shape[-1])
        res = _adamw(as3(grad), None, as3(w[name]), as3(m[name]), as3(v[name]), f"adamw_{name}")
        out[name] = tuple(r.reshape(shape) for r in res)

    for n, (keys, _, last) in enumerate(started):
        if not last:
            arrive(n, grad_x)
            for key, l in keys:
                big_update(key, l)

    dmod_flat = dmod.reshape(DEPTH, bsz, N_MOD * d)
    done = [r[0] for r in chain.values() if r is not None]
    if done:
        dmod_flat, _ = lax.optimization_barrier((dmod_flat, done))
    dmod_all, = _all_gather([dmod_flat], "gather_dmod")
    dmod_full = jnp.moveaxis(dmod_all, 0, 1).reshape(DEPTH, N_DEV * bsz, N_MOD * d)
    cols = ada_w.shape[2]
    dmod_cols = lax.dynamic_slice_in_dim(dmod_full, me * cols, cols, axis=2)
    g_ada_w, g_ada_b = _ada_bwd(c_all, dmod_cols, dmod_full, "ada_bwd")
    res = None
    for l in range(DEPTH):
        res = _adamw(g_ada_w[l][None], None, ada_w, m_ada_w, v_ada_w, f"adamw_ada_w_l{l}", layer=l, prev=res)
    out["ada_w"] = tuple(res)
    update("ada_b", g_ada_b.reshape(DEPTH, N_MOD * d))

    items = _small_grad_list(small, loss)
    packed, _ = lax.optimization_barrier((_pack_small(items), (grad_x, g_ada_b)))
    parts, = _all_gather([packed], "gather_small")
    sg = _unpack_small(_sum_parts(parts, "sum_small"), items)
    for name in ("ln_g", "ln_b"):
        update(name, lax.dynamic_slice_in_dim(sg[name], me * (d // N_DEV), d // N_DEV, axis=2))
    for name in ("hgrn_lb_logits", "hgrn_norm_g", "mla_q_norm_g", "mla_kv_norm_g", "fox_b_f", "gmlp_ln_g",
                 "gmlp_ln_b", "gmlp_w_s", "gmlp_b_s"):
        update(name, sg[name])

    for n, (keys, _, last) in enumerate(started):
        if last:
            arrive(n, out["gmlp_w_s"][0])
            for key, l in keys:
                big_update(key, l)
    for name in big_of:
        out[name] = tuple(jnp.swapaxes(r, 1, 2) if name in transposed else r for r in chain[name])

    return (sg["loss"][0], grad_x, *[out[n][0] for n in names], *[out[n][1] for n in names],
            *[out[n][2] for n in names], *[out[n][3] for n in names])
```

```python
import functools

import numpy as np
import jax
import jax.numpy as jnp
from jax import lax
from jax.experimental import pallas as pl
from jax.experimental.pallas import tpu as pltpu

F32 = jnp.float32
BF16 = jnp.bfloat16
HI = lax.Precision.HIGHEST

D_MODEL = 1024
DEPTH = 2
GROUP_WIDTH = 256
N_HEADS = 4
HEAD_DIM = 64
A_CHUNK = 16
LB_FLOOR = 1e-30
B_NOPE = 64
B_ROPE = 32
ROPE_THETA = 10000.0
D_CHUNK = 128
D_FF = 2816
N_MOD = 9
ALPHA = (2 * DEPTH) ** 0.25
LN_EPS = 1e-5
RMS_EPS = 1e-6
ADAM_LR = 0.001
ADAM_B1 = 0.9
ADAM_B2 = 0.999
ADAM_EPS = 1e-08
ADAM_WD = 0.01
ADAM_STEP = 10

N_DEV = 8
LANES = 128
PACK_W = 3712
MO_W = 1536
VMEM_LIMIT = 56 * 1024 * 1024
NEG = -1e30
ATTN_TILE = 1024
GMLP_STEP = 4

MIX_ORIG_W = 2724
O_BCQ, O_BCKV, O_BKR, O_CQ, O_CK, O_CV, O_CF, O_DU, O_DV = 1024, 1280, 1408, 1440, 1696, 1952, 2208, 2212, 2468
P_B, P_KR, P_CQ, P_CKV, P_D, P_CF = 1024, 1408, 1536, 2048, 3072, 3584


_DN = {"nn": (((1,), (0,)), ((), ())), "nt": (((1,), (1,)), ((), ())), "tn": (((0,), (0,)), ((), ()))}


def _raw_bdot(a, b, mode):
    return lax.dot_general(a.astype(BF16), b.astype(BF16), _DN[mode], preferred_element_type=F32)


@functools.partial(jax.custom_vjp, nondiff_argnums=(2,))
def _bdot(a, b, mode):
    return _raw_bdot(a, b, mode)


def _bdot_fwd(a, b, mode):
    return _raw_bdot(a, b, mode), (a, b)


def _bdot_bwd(mode, res, g):
    a, b = res
    if mode == "nn":
        return _raw_bdot(g, b, "nt"), _raw_bdot(a, g, "tn")
    if mode == "nt":
        return _raw_bdot(g, b, "nn"), _raw_bdot(g, a, "tn")
    return _raw_bdot(b, g, "nt"), _raw_bdot(a, g, "nn")


_bdot.defvjp(_bdot_fwd, _bdot_bwd)


def _cparams(sem):
    return pltpu.CompilerParams(dimension_semantics=sem, vmem_limit_bytes=VMEM_LIMIT)


def _mix_in_src():
    src = -np.ones(PACK_W, np.int64)
    src[0:P_KR] = np.arange(0, O_BKR)
    src[P_KR + 64:P_KR + 80] = O_BKR + np.arange(16)
    src[P_KR + 96:P_KR + 112] = O_BKR + 16 + np.arange(16)
    for h in range(N_HEADS):
        src[P_CQ + 128 * h:P_CQ + 128 * h + 64] = O_CQ + 64 * h + np.arange(64)
        src[P_CKV + 256 * h:P_CKV + 256 * h + 64] = O_CK + 64 * h + np.arange(64)
        src[P_CKV + 256 * h + 128:P_CKV + 256 * h + 192] = O_CV + 64 * h + np.arange(64)
    src[P_D:P_D + 512] = O_DU + np.arange(512)
    src[P_CF:P_CF + 4] = O_CF + np.arange(4)
    return src


def _uq_src():
    src = -np.ones(512, np.int64)
    for h in range(N_HEADS):
        src[128 * h:128 * h + 64] = 96 * h + np.arange(64)
        src[128 * h + 64:128 * h + 80] = 96 * h + 64 + np.arange(16)
        src[128 * h + 96:128 * h + 112] = 96 * h + 80 + np.arange(16)
    return src


def _ukv_src():
    src = -np.ones(1024, np.int64)
    for h in range(N_HEADS):
        src[256 * h:256 * h + 64] = 128 * h + np.arange(64)
        src[256 * h + 128:256 * h + 192] = 128 * h + 64 + np.arange(64)
    return src


def _mo_src():
    src = -np.ones(MO_W, np.int64)
    src[0:256] = np.arange(256)
    for g in range(2):
        for h in range(N_HEADS):
            src[256 + 512 * g + 128 * h:256 + 512 * g + 128 * h + 64] = 256 + 256 * g + 64 * h + np.arange(64)
    src[1280:1536] = 768 + np.arange(256)
    return src


def _runs(idx):
    runs, i = [], 0
    while i < len(idx):
        j = i + 1
        while j < len(idx) and ((idx[i] < 0 and idx[j] < 0) or (idx[i] >= 0 and idx[j] == idx[i] + j - i)):
            j += 1
        runs.append((int(idx[i]), j - i))
        i = j
    return runs


def _take_runs(w, idx):
    parts = [jnp.zeros(w.shape[:-1] + (n,), w.dtype) if s < 0 else lax.slice_in_dim(w, s, s + n, axis=w.ndim - 1)
             for s, n in _runs(idx)]
    return jnp.concatenate(parts, axis=-1)


def _pack_cols(w, src):
    return _take_runs(w, src)


def _unpack_cols(wp, src, n):
    dst = np.zeros(n, np.int64)
    dst[src[src >= 0]] = np.nonzero(src >= 0)[0]
    return _take_runs(wp, dst)


def _rope_tables(seq):
    half = B_ROPE // 2
    inv_freq = ROPE_THETA ** (-jnp.arange(half, dtype=F32) / half)
    ang = jnp.arange(seq).astype(F32)[:, None] * inv_freq[None, :]
    cos, sin = jnp.cos(ang), jnp.sin(ang)
    z16 = jnp.zeros((seq, 16), F32)
    c = jnp.concatenate([jnp.ones((seq, 64), F32), cos, z16, cos, z16], axis=1)
    s1 = jnp.concatenate([jnp.zeros((seq, 64), F32), -sin, z16, z16, z16], axis=1)
    s2 = jnp.concatenate([jnp.zeros((seq, 64), F32), z16, z16, sin, z16], axis=1)
    return c, s1, s2


def _matmul(a, b, *, mode, group_out, out_dtype, tm, tk, name):
    ga, gb = a.shape[0], b.shape[0]
    g_n = max(ga, gb)
    if mode == "tn":
        k_dim, m_dim = a.shape[1:]
    else:
        m_dim, k_dim = a.shape[1:]
    n_dim = b.shape[1] if mode == "nt" else b.shape[2]
    assert m_dim % tm == 0 and k_dim % tk == 0
    kt = k_dim // tk
    n_red = kt if group_out else g_n * kt
    g_out = g_n if group_out else 1

    def split(g, r):
        return (g, r) if group_out else (r // kt, r % kt)

    def a_map(g, i, r):
        gg, kk = split(g, r)
        gg = gg if ga > 1 else 0
        return (gg, kk, i) if mode == "tn" else (gg, i, kk)

    def b_map(g, i, r):
        gg, kk = split(g, r)
        gg = gg if gb > 1 else 0
        return (gg, 0, kk) if mode == "nt" else (gg, kk, 0)

    a_blk = (None, tk, tm) if mode == "tn" else (None, tm, tk)
    b_blk = (None, n_dim, tk) if mode == "nt" else (None, tk, n_dim)
    dn = _DN[mode]

    def body(a_ref, b_ref, o_ref, *scratch):
        part = lax.dot_general(a_ref[...].astype(BF16), b_ref[...].astype(BF16), dn, preferred_element_type=F32)
        if n_red == 1:
            o_ref[...] = part.astype(o_ref.dtype)
            return
        acc_ref, = scratch
        r = pl.program_id(2)

        @pl.when(r == 0)
        def _():
            acc_ref[...] = part

        @pl.when(r > 0)
        def _():
            acc_ref[...] += part

        @pl.when(r == n_red - 1)
        def _():
            o_ref[...] = acc_ref[...].astype(o_ref.dtype)

    return pl.pallas_call(
        body, name=name, grid=(g_out, m_dim // tm, n_red),
        in_specs=[pl.BlockSpec(a_blk, a_map), pl.BlockSpec(b_blk, b_map)],
        out_specs=pl.BlockSpec((None, tm, n_dim), lambda g, i, r: (g, i, 0)),
        out_shape=jax.ShapeDtypeStruct((g_out, m_dim, n_dim), out_dtype),
        scratch_shapes=[] if n_red == 1 else [pltpu.VMEM((tm, n_dim), F32)],
        compiler_params=_cparams(("parallel", "parallel", "arbitrary")),
    )(a, b)


def _matmul_groupsum(a, b, *, out_dtype, tm, name):
    g_n, m_dim, k_dim = a.shape
    n_dim = b.shape[2]
    assert m_dim % tm == 0 and b.shape[:2] == (g_n, k_dim)

    def body(a_ref, b_ref, o_ref):
        acc = jnp.dot(a_ref[0], b_ref[0], preferred_element_type=F32)
        for g in range(1, g_n):
            acc = acc + jnp.dot(a_ref[g], b_ref[g], preferred_element_type=F32)
        o_ref[...] = acc.astype(o_ref.dtype)

    return pl.pallas_call(
        body, name=name, grid=(m_dim // tm,),
        in_specs=[pl.BlockSpec((g_n, tm, k_dim), lambda i: (0, i, 0)),
                  pl.BlockSpec((g_n, k_dim, n_dim), lambda i: (0, 0, 0))],
        out_specs=pl.BlockSpec((tm, n_dim), lambda i: (i, 0)),
        out_shape=jax.ShapeDtypeStruct((m_dim, n_dim), out_dtype),
        compiler_params=_cparams(("parallel",)),
    )(a, b)


def _row_spec(ts, d):
    return pl.BlockSpec((None, ts, d), lambda b, s: (b, s, 0))


def _mod_spec(d):
    return pl.BlockSpec((None, N_MOD, d), lambda b, s: (b, 0, 0))


def _vec_spec(d):
    return pl.BlockSpec((1, d), lambda b, s: (0, 0))


def _bvec_spec(d):
    return pl.BlockSpec((None, 1, d), lambda b, s: (b, 0, 0))


def _modulate(x, mod, sh_row, sc_row, name, ts=512):
    bsz, seq, d = x.shape

    def body(x_ref, mod_ref, o_ref):
        sh = mod_ref[sh_row:sh_row + 1, :]
        sc = mod_ref[sc_row:sc_row + 1, :]
        o_ref[...] = (x_ref[...] * (1.0 + sc) + sh).astype(o_ref.dtype)

    return pl.pallas_call(
        body, name=name, grid=(bsz, seq // ts),
        in_specs=[_row_spec(ts, d), _mod_spec(d)], out_specs=_row_spec(ts, d),
        out_shape=jax.ShapeDtypeStruct((bsz, seq, d), BF16),
        compiler_params=_cparams(("parallel", "parallel")),
    )(x, mod)


def _modulate_bwd(dh, x, mod, dx_res, sc_row, name, ts=512):
    bsz, seq, d = x.shape

    def body(dh_ref, x_ref, mod_ref, dxr_ref, dx_ref, dsh_ref, dsc_ref):
        s = pl.program_id(1)
        sc = mod_ref[sc_row:sc_row + 1, :]
        dh_v = dh_ref[...]
        dx_ref[...] = dxr_ref[...] + dh_v * (1.0 + sc)
        psh = jnp.sum(dh_v, axis=0, keepdims=True)
        psc = jnp.sum(dh_v * x_ref[...], axis=0, keepdims=True)

        @pl.when(s == 0)
        def _():
            dsh_ref[...] = psh
            dsc_ref[...] = psc

        @pl.when(s > 0)
        def _():
            dsh_ref[...] += psh
            dsc_ref[...] += psc

    return pl.pallas_call(
        body, name=name, grid=(bsz, seq // ts),
        in_specs=[_row_spec(ts, d), _row_spec(ts, d), _mod_spec(d), _row_spec(ts, d)],
        out_specs=[_row_spec(ts, d), _bvec_spec(d), _bvec_spec(d)],
        out_shape=[jax.ShapeDtypeStruct((bsz, seq, d), F32), jax.ShapeDtypeStruct((bsz, 1, d), F32),
                   jax.ShapeDtypeStruct((bsz, 1, d), F32)],
        compiler_params=_cparams(("parallel", "arbitrary")),
    )(dh, x, mod, dx_res)


def _res_ln_fn(x, f, g, lng, lnb, cmul):
    r = ALPHA * x + (cmul * (1.0 + g)) * f
    mu = jnp.mean(r, axis=-1, keepdims=True)
    rc = r - mu
    var = jnp.mean(rc * rc, axis=-1, keepdims=True)
    return rc * lax.rsqrt(var + LN_EPS) * lng + lnb


def _res_ln(x, f, mod, lng, lnb, g_row, cmul, name, nxt=None, ts=512):
    bsz, seq, d = x.shape

    def body(*refs):
        x_ref, f_ref, mod_ref, lng_ref, lnb_ref = refs[:5]
        g = mod_ref[g_row:g_row + 1, :]
        y = _res_ln_fn(x_ref[...], f_ref[...], g, lng_ref[...], lnb_ref[...], cmul)
        if nxt is None:
            refs[5][...] = y
            return
        nmod_ref, o_ref, h_ref = refs[5:]
        o_ref[...] = y
        sh = nmod_ref[nxt[1]:nxt[1] + 1, :]
        sc = nmod_ref[nxt[2]:nxt[2] + 1, :]
        h_ref[...] = (y * (1.0 + sc) + sh).astype(h_ref.dtype)

    in_specs = [_row_spec(ts, d), _row_spec(ts, d), _mod_spec(d), _vec_spec(d), _vec_spec(d)]
    args = [x, f, mod, lng, lnb]
    out_specs, out_shape = [_row_spec(ts, d)], [jax.ShapeDtypeStruct((bsz, seq, d), F32)]
    if nxt is not None:
        in_specs.append(_mod_spec(d))
        args.append(nxt[0])
        out_specs.append(_row_spec(ts, d))
        out_shape.append(jax.ShapeDtypeStruct((bsz, seq, d), BF16))
    res = pl.pallas_call(
        body, name=name, grid=(bsz, seq // ts), in_specs=in_specs, out_specs=out_specs, out_shape=out_shape,
        compiler_params=_cparams(("parallel", "parallel")),
    )(*args)
    return (res[0], res[1]) if nxt is not None else (res[0], None)


def _res_ln_bwd(dy, x, f, mod, lng, lnb, g_row, cmul, name, pre=None, ts=512):
    bsz, seq, d = x.shape
    fused = pre is not None

    def body(*refs):
        dy_ref, x_ref, f_ref, mod_ref, lng_ref, lnb_ref = refs[:6]
        n_in = 8 if fused else 6
        dx_ref, df_ref, dg_ref, dlg_ref, dlb_ref = refs[n_in:n_in + 5]
        b, s = pl.program_id(0), pl.program_id(1)
        g = mod_ref[g_row:g_row + 1, :]
        y, vjp = jax.vjp(functools.partial(_res_ln_fn, cmul=cmul), x_ref[...], f_ref[...], g, lng_ref[...],
                         lnb_ref[...])
        ct = dy_ref[...]
        if fused:
            dh_ref, nmod_ref = refs[6:8]
            dsh_ref, dsc_ref = refs[n_in + 5:]
            dh_v = dh_ref[...]
            ct = ct + dh_v * (1.0 + nmod_ref[pre[3]:pre[3] + 1, :])
            psh = jnp.sum(dh_v, axis=0, keepdims=True)
            psc = jnp.sum(dh_v * y, axis=0, keepdims=True)
        dx, df, dg, dlg, dlb = vjp(ct)
        dx_ref[...] = dx
        df_ref[...] = df.astype(df_ref.dtype)

        @pl.when(s == 0)
        def _():
            dg_ref[...] = dg
            if fused:
                dsh_ref[...] = psh
                dsc_ref[...] = psc

        @pl.when(s > 0)
        def _():
            dg_ref[...] += dg
            if fused:
                dsh_ref[...] += psh
                dsc_ref[...] += psc

        first = jnp.logical_and(b == 0, s == 0)

        @pl.when(first)
        def _():
            dlg_ref[...] = dlg
            dlb_ref[...] = dlb

        @pl.when(jnp.logical_not(first))
        def _():
            dlg_ref[...] += dlg
            dlb_ref[...] += dlb

    in_specs = [_row_spec(ts, d), _row_spec(ts, d), _row_spec(ts, d), _mod_spec(d), _vec_spec(d), _vec_spec(d)]
    args = [dy, x, f, mod, lng, lnb]
    out_specs = [_row_spec(ts, d), _row_spec(ts, d), _bvec_spec(d), _vec_spec(d), _vec_spec(d)]
    bvec = jax.ShapeDtypeStruct((bsz, 1, d), F32)
    out_shape = [jax.ShapeDtypeStruct((bsz, seq, d), F32), jax.ShapeDtypeStruct((bsz, seq, d), BF16), bvec,
                 jax.ShapeDtypeStruct((1, d), F32), jax.ShapeDtypeStruct((1, d), F32)]
    if fused:
        in_specs += [_row_spec(ts, d), _mod_spec(d)]
        args += [pre[0], pre[2]]
        out_specs += [_bvec_spec(d), _bvec_spec(d)]
        out_shape += [bvec, bvec]
    res = pl.pallas_call(
        body, name=name, grid=(bsz, seq // ts), in_specs=in_specs, out_specs=out_specs, out_shape=out_shape,
        compiler_params=_cparams(("arbitrary", "arbitrary")),
    )(*args)
    return tuple(res[:5]), (tuple(res[5:]) if fused else None)


def _loss_head(y, target, name, ts=512):
    bsz, seq, d = y.shape
    n_s = seq // ts

    def body(y_ref, t_ref, dy_ref, loss_ref, acc_ref):
        b, s = pl.program_id(0), pl.program_id(1)
        err = y_ref[...] - t_ref[...]
        dy_ref[...] = err * (1.0 / d)
        part = jnp.sum(err * err, axis=0, keepdims=True)
        first = jnp.logical_and(b == 0, s == 0)

        @pl.when(first)
        def _():
            acc_ref[...] = part

        @pl.when(jnp.logical_not(first))
        def _():
            acc_ref[...] += part

        @pl.when(jnp.logical_and(b == bsz - 1, s == n_s - 1))
        def _():
            loss_ref[...] = jnp.sum(acc_ref[...], axis=1, keepdims=True) * (0.5 / d)

    return pl.pallas_call(
        body, name=name, grid=(bsz, n_s),
        in_specs=[_row_spec(ts, d), _row_spec(ts, d)],
        out_specs=[_row_spec(ts, d), pl.BlockSpec((1, 1), lambda b, s: (0, 0))],
        out_shape=[jax.ShapeDtypeStruct((bsz, seq, d), F32), jax.ShapeDtypeStruct((1, 1), F32)],
        scratch_shapes=[pltpu.VMEM((1, d), F32)],
        compiler_params=_cparams(("arbitrary", "arbitrary")),
    )(y, target)


def _ffn_in_swiglu(h, w_in_t, name, tm=1024):
    t, d = h.shape
    n_sh, w, _ = w_in_t.shape
    half = n_sh // 2

    def body(h_ref, w_ref, z_ref, a_ref):
        hv = h_ref[...]
        g = lax.dot_general(hv, w_ref[0], _DN["nt"], preferred_element_type=F32)
        u = lax.dot_general(hv, w_ref[1], _DN["nt"], preferred_element_type=F32)
        z_ref[0] = g.astype(z_ref.dtype)
        z_ref[1] = u.astype(z_ref.dtype)
        a_ref[...] = (g * jax.nn.sigmoid(g) * u).astype(a_ref.dtype)

    return pl.pallas_call(
        body, name=name, grid=(half, t // tm),
        in_specs=[pl.BlockSpec((tm, d), lambda g, i: (i, 0)),
                  pl.BlockSpec((2, None, w, d), lambda g, i: (0, g, 0, 0))],
        out_specs=[pl.BlockSpec((2, None, tm, w), lambda g, i: (0, g, i, 0)),
                   pl.BlockSpec((None, tm, w), lambda g, i: (g, i, 0))],
        out_shape=[jax.ShapeDtypeStruct((2, half, t, w), BF16), jax.ShapeDtypeStruct((half, t, w), BF16)],
        compiler_params=_cparams(("parallel", "parallel")),
    )(h, w_in_t.reshape(2, half, w, d))


def _ffn_out_dx_swiglu(df, w_out, z, name, tm=1024):
    t, d = df.shape
    half, w, _ = w_out.shape

    def body(df_ref, w_ref, z_ref, dz_ref):
        da = lax.dot_general(df_ref[...], w_ref[...], _DN["nt"], preferred_element_type=F32)
        g = z_ref[0].astype(F32)
        u = z_ref[1].astype(F32)
        sig = jax.nn.sigmoid(g)
        dz_ref[0] = (da * u * (sig * (1.0 + g * (1.0 - sig)))).astype(dz_ref.dtype)
        dz_ref[1] = (da * (g * sig)).astype(dz_ref.dtype)

    zspec = pl.BlockSpec((2, None, tm, w), lambda g, i: (0, g, i, 0))
    return pl.pallas_call(
        body, name=name, grid=(half, t // tm),
        in_specs=[pl.BlockSpec((tm, d), lambda g, i: (i, 0)), pl.BlockSpec((None, w, d), lambda g, i: (g, 0, 0)),
                  zspec],
        out_specs=zspec, out_shape=jax.ShapeDtypeStruct(z.shape, BF16),
        compiler_params=_cparams(("parallel", "parallel")),
    )(df, w_out, z)


def _log_sigmoid(x):
    return jnp.minimum(x, 0.0) - jnp.log(1.0 + jnp.exp(-jnp.abs(x)))


def _hgrn_consts():
    r = lax.broadcasted_iota(jnp.int32, (GROUP_WIDTH, GROUP_WIDTH), 0)
    c = lax.broadcasted_iota(jnp.int32, (GROUP_WIDTH, GROUP_WIDTH), 1)
    bd = (r // HEAD_DIM == c // HEAD_DIM).astype(F32)
    r16 = lax.broadcasted_iota(jnp.int32, (A_CHUNK, A_CHUNK), 0)
    c16 = lax.broadcasted_iota(jnp.int32, (A_CHUNK, A_CHUNK), 1)
    tril = (r16 >= c16).astype(F32)
    rows = lax.broadcasted_iota(jnp.int32, (A_CHUNK, GROUP_WIDTH), 0)
    return bd, tril, rows


def _hgrn_lb(logits8, layer):
    rows = lax.broadcasted_iota(jnp.int32, logits8.shape, 0)
    valid = rows < DEPTH
    mx = jnp.max(jnp.where(valid, logits8, NEG), axis=0, keepdims=True)
    e = jnp.where(valid, jnp.exp(logits8 - mx), 0.0)
    sm = e / jnp.sum(e, axis=0, keepdims=True)
    pick = jnp.logical_and(rows >= 1, rows <= layer)
    return jnp.sum(jnp.where(pick, sm, 0.0), axis=0, keepdims=True)


def _hgrn_chunk(aq, af, ai, ag, logits8, norm_g, st, *, layer, consts):
    bd, tril, rows = consts
    lb = _hgrn_lb(logits8, layer)
    la = jnp.log(jnp.maximum(lb, LB_FLOOR))
    b2 = jnp.log(1.0 - lb) + _log_sigmoid(af)
    log_f = jnp.maximum(la, b2) + jnp.log(1.0 + jnp.exp(-jnp.abs(la - b2)))
    k = 1.0 - jnp.exp(log_f)
    qf = aq * jax.nn.sigmoid(aq)
    g_cum = jnp.dot(tril, log_f, precision=HI, preferred_element_type=F32)

    c, w = A_CHUNK, GROUP_WIDTH

    def by_key(v):
        return jnp.broadcast_to(v[:, None, :], (c, c, w))

    def by_query(v):
        return jnp.broadcast_to(v[None, :, :], (c, c, w))

    s_i = lax.broadcasted_iota(jnp.int32, (c, c, w), 0)
    t_i = lax.broadcasted_iota(jnp.int32, (c, c, w), 1)
    rel = jnp.where(t_i >= s_i, by_query(g_cum) - by_key(g_cum), NEG)
    pairs = by_query(qf) * by_key(k) * jnp.exp(rel)
    a_all = _bdot(pairs.reshape(c * c, w), bd, "nn").reshape(c, c, w)
    o = jnp.sum(a_all * by_key(ai), axis=0)
    q_dec = qf * jnp.exp(g_cum)
    o = o + _bdot(q_dec, st, "nt")
    g_last = jnp.sum(jnp.where(rows == c - 1, g_cum, 0.0), axis=0, keepdims=True)
    k_end = k * jnp.exp(g_last - g_cum)
    kv = _bdot(ai, k_end, "tn")
    st_new = st * jnp.exp(g_last) + kv * bd
    ms = _bdot(o * o, bd, "nn") * (1.0 / HEAD_DIM)
    o = o * lax.rsqrt(ms + RMS_EPS) * norm_g
    return o * (ag * jax.nn.sigmoid(ag)), st_new


def _hgrn_fwd(proj, logits8, norm_g, layer, name, ts=256):
    bsz, seq, _ = proj.shape
    n_ch = ts // A_CHUNK

    def body(p_ref, lg_ref, ng_ref, o_ref, st_ref, st_scr):
        @pl.when(pl.program_id(1) == 0)
        def _():
            st_scr[...] = jnp.zeros_like(st_scr)

        consts = _hgrn_consts()
        logits_v, ng_v = lg_ref[...], ng_ref[...]

        def chunk(ci, carry):
            r = ci * A_CHUNK if isinstance(ci, int) else pl.multiple_of(ci * A_CHUNK, A_CHUNK)
            st = st_scr[...]
            st_ref[ci] = st
            o, st_new = _hgrn_chunk(
                p_ref[pl.ds(r, A_CHUNK), 0:256], p_ref[pl.ds(r, A_CHUNK), 256:512],
                p_ref[pl.ds(r, A_CHUNK), 512:768], p_ref[pl.ds(r, A_CHUNK), 768:1024],
                logits_v, ng_v, st, layer=layer, consts=consts)
            o_ref[pl.ds(r, A_CHUNK), :] = o.astype(o_ref.dtype)
            st_scr[...] = st_new
            return carry

        if n_ch <= 2:
            for c_static in range(n_ch):
                chunk(c_static, 0)
        else:
            lax.fori_loop(0, n_ch, chunk, 0, unroll=8)

    return pl.pallas_call(
        body, name=name, grid=(bsz, seq // ts),
        in_specs=[pl.BlockSpec((None, ts, 1024), lambda b, s: (b, s, 0)),
                  pl.BlockSpec((8, GROUP_WIDTH), lambda b, s: (0, 0)),
                  pl.BlockSpec((1, GROUP_WIDTH), lambda b, s: (0, 0))],
        out_specs=[pl.BlockSpec((None, ts, GROUP_WIDTH), lambda b, s: (b, s, 0)),
                   pl.BlockSpec((None, n_ch, GROUP_WIDTH, GROUP_WIDTH), lambda b, s: (b, s, 0, 0))],
        out_shape=[jax.ShapeDtypeStruct((bsz, seq, MO_W), BF16),
                   jax.ShapeDtypeStruct((bsz, seq // A_CHUNK, GROUP_WIDTH, GROUP_WIDTH), F32)],
        scratch_shapes=[pltpu.VMEM((GROUP_WIDTH, GROUP_WIDTH), F32)],
        compiler_params=_cparams(("parallel", "arbitrary")),
    )(proj, logits8, norm_g)


def _hgrn_bwd(dmo, proj, states, logits8, norm_g, layer, name, ts=256):
    bsz, seq, _ = proj.shape
    n_ch = ts // A_CHUNK
    n_s = seq // ts

    def body(do_ref, p_ref, st_ref, lg_ref, ng_ref, dp_ref, dlg_ref, dng_ref, dst_scr):
        b, s = pl.program_id(0), pl.program_id(1)

        @pl.when(s == 0)
        def _():
            dst_scr[...] = jnp.zeros_like(dst_scr)

        @pl.when(jnp.logical_and(b == 0, s == 0))
        def _():
            dlg_ref[...] = jnp.zeros_like(dlg_ref)
            dng_ref[...] = jnp.zeros_like(dng_ref)

        consts = _hgrn_consts()
        logits_v, ng_v = lg_ref[...], ng_ref[...]
        fn = functools.partial(_hgrn_chunk, layer=layer, consts=consts)

        def chunk(t, carry):
            ci = n_ch - 1 - t
            r = ci * A_CHUNK if isinstance(ci, int) else pl.multiple_of(ci * A_CHUNK, A_CHUNK)
            _, vjp = jax.vjp(
                fn, p_ref[pl.ds(r, A_CHUNK), 0:256], p_ref[pl.ds(r, A_CHUNK), 256:512],
                p_ref[pl.ds(r, A_CHUNK), 512:768], p_ref[pl.ds(r, A_CHUNK), 768:1024],
                logits_v, ng_v, st_ref[ci])
            daq, daf, dai, dag, dlg, dng, dst = vjp((do_ref[pl.ds(r, A_CHUNK), :], dst_scr[...]))
            dp_ref[pl.ds(r, A_CHUNK), 0:256] = daq.astype(dp_ref.dtype)
            dp_ref[pl.ds(r, A_CHUNK), 256:512] = daf.astype(dp_ref.dtype)
            dp_ref[pl.ds(r, A_CHUNK), 512:768] = dai.astype(dp_ref.dtype)
            dp_ref[pl.ds(r, A_CHUNK), 768:1024] = dag.astype(dp_ref.dtype)
            dlg_ref[...] += dlg
            dng_ref[...] += dng
            dst_scr[...] = dst
            return carry

        if n_ch <= 2:
            for c_static in range(n_ch):
                chunk(c_static, 0)
        else:
            lax.fori_loop(0, n_ch, chunk, 0, unroll=8)

    rev = lambda b, s: (b, n_s - 1 - s, 0)
    return pl.pallas_call(
        body, name=name, grid=(bsz, n_s),
        in_specs=[pl.BlockSpec((None, ts, GROUP_WIDTH), rev),
                  pl.BlockSpec((None, ts, 1024), rev),
                  pl.BlockSpec((None, n_ch, GROUP_WIDTH, GROUP_WIDTH), lambda b, s: (b, n_s - 1 - s, 0, 0)),
                  pl.BlockSpec((8, GROUP_WIDTH), lambda b, s: (0, 0)),
                  pl.BlockSpec((1, GROUP_WIDTH), lambda b, s: (0, 0))],
        out_specs=[pl.BlockSpec((None, ts, 1024), rev),
                   pl.BlockSpec((8, GROUP_WIDTH), lambda b, s: (0, 0)),
                   pl.BlockSpec((1, GROUP_WIDTH), lambda b, s: (0, 0))],
        out_shape=[jax.ShapeDtypeStruct((bsz, seq, PACK_W), BF16),
                   jax.ShapeDtypeStruct((8, GROUP_WIDTH), F32), jax.ShapeDtypeStruct((1, GROUP_WIDTH), F32)],
        scratch_shapes=[pltpu.VMEM((GROUP_WIDTH, GROUP_WIDTH), F32)],
        compiler_params=_cparams(("arbitrary", "arbitrary")),
    )(dmo, proj, states, logits8, norm_g)


def _rms_fn(x, g):
    return x * lax.rsqrt(jnp.mean(x * x, axis=-1, keepdims=True) + RMS_EPS) * g


def _tile4(t):
    return jnp.concatenate([t, t, t, t], axis=1)


def _rope(x, c, s1, s2):
    w = x.shape[-1]
    return x * c + pltpu.roll(x, 32, axis=1) * s2 + pltpu.roll(x, w - 32, axis=1) * s1


def _rope_t(dy, c, s1, s2):
    w = dy.shape[-1]
    return dy * c + pltpu.roll(dy * s2, w - 32, axis=1) + pltpu.roll(dy * s1, 32, axis=1)


def _mla_pre(proj, qg, kvg, wq, wkv, tabs, name, ts=512):
    bsz, seq, _ = proj.shape

    def body(p_ref, qg_ref, kvg_ref, wq_ref, wkv_ref, c_ref, s1_ref, s2_ref, q_ref, kv_ref):
        nq = _rms_fn(p_ref[:, 0:256], qg_ref[...])
        nkv = _rms_fn(p_ref[:, 256:384], kvg_ref[...])
        c, s1, s2 = c_ref[...], s1_ref[...], s2_ref[...]
        qp = jnp.dot(nq.astype(BF16), wq_ref[...], preferred_element_type=F32)
        q_ref[...] = _rope(qp, _tile4(c), _tile4(s1), _tile4(s2)).astype(q_ref.dtype)
        kv = jnp.dot(nkv.astype(BF16), wkv_ref[...], preferred_element_type=F32)
        krr = _rope(p_ref[:, 384:512], c, s1, s2)
        zero = jnp.zeros_like(krr)
        kv_ref[...] = (kv + jnp.concatenate([krr, zero] * N_HEADS, axis=1)).astype(kv_ref.dtype)

    tab_spec = pl.BlockSpec((ts, LANES), lambda b, s: (s, 0))
    return pl.pallas_call(
        body, name=name, grid=(bsz, seq // ts),
        in_specs=[pl.BlockSpec((None, ts, 512), lambda b, s: (b, s, P_B // 512)),
                  _vec_spec(256), _vec_spec(128),
                  pl.BlockSpec((256, 512), lambda b, s: (0, 0)), pl.BlockSpec((128, 1024), lambda b, s: (0, 0)),
                  tab_spec, tab_spec, tab_spec],
        out_specs=[_row_spec(ts, 512), _row_spec(ts, 1024)],
        out_shape=[jax.ShapeDtypeStruct((bsz, seq, 512), BF16), jax.ShapeDtypeStruct((bsz, seq, 1024), BF16)],
        compiler_params=_cparams(("parallel", "parallel")),
    )(proj, qg, kvg, wq, wkv, *tabs)


def _mla_pre_bwd(dq, dkv, dproj, proj, qg, kvg, wq, wkv, tabs, name, ts=512):
    bsz, seq, _ = proj.shape

    def body(dq_ref, dkv_ref, dp_any, p_ref, qg_ref, kvg_ref, wq_ref, wkv_ref, c_ref, s1_ref, s2_ref,
             dp_ref, dqg_ref, dkvg_ref, dwq_ref, dwkv_ref):
        del dp_any
        first = jnp.logical_and(pl.program_id(0) == 0, pl.program_id(1) == 0)

        @pl.when(first)
        def _():
            dqg_ref[...] = jnp.zeros_like(dqg_ref)
            dkvg_ref[...] = jnp.zeros_like(dkvg_ref)
            dwq_ref[...] = jnp.zeros_like(dwq_ref)
            dwkv_ref[...] = jnp.zeros_like(dwkv_ref)

        c, s1, s2 = c_ref[...], s1_ref[...], s2_ref[...]
        nq, vjp_q = jax.vjp(_rms_fn, p_ref[:, 0:256], qg_ref[...])
        nkv, vjp_kv = jax.vjp(_rms_fn, p_ref[:, 256:384], kvg_ref[...])
        dqp = _rope_t(dq_ref[...], _tile4(c), _tile4(s1), _tile4(s2)).astype(BF16)
        dkv_v = dkv_ref[...]
        dkv_b = dkv_v.astype(BF16)
        tn = (((0,), (0,)), ((), ()))
        nt = (((1,), (1,)), ((), ()))
        dwq_ref[...] += lax.dot_general(nq.astype(BF16), dqp, tn, preferred_element_type=F32)
        dwkv_ref[...] += lax.dot_general(nkv.astype(BF16), dkv_b, tn, preferred_element_type=F32)
        dcq, dqg = vjp_q(lax.dot_general(dqp, wq_ref[...], nt, preferred_element_type=F32))
        dckv, dkvg = vjp_kv(lax.dot_general(dkv_b, wkv_ref[...], nt, preferred_element_type=F32))
        dqg_ref[...] += dqg
        dkvg_ref[...] += dkvg
        dk_sum = dkv_v[:, 0:128] + dkv_v[:, 256:384] + dkv_v[:, 512:640] + dkv_v[:, 768:896]
        lane = lax.broadcasted_iota(jnp.int32, dk_sum.shape, 1)
        dkr = jnp.where(lane >= 64, _rope_t(dk_sum, c, s1, s2), 0.0)
        dp_ref[:, 0:256] = dcq.astype(dp_ref.dtype)
        dp_ref[:, 256:384] = dckv.astype(dp_ref.dtype)
        dp_ref[:, 384:512] = dkr.astype(dp_ref.dtype)

    tab_spec = pl.BlockSpec((ts, LANES), lambda b, s: (s, 0))
    const = lambda shape: pl.BlockSpec(shape, lambda b, s: (0, 0))
    return pl.pallas_call(
        body, name=name, grid=(bsz, seq // ts),
        in_specs=[_row_spec(ts, 512), _row_spec(ts, 1024), pl.BlockSpec(memory_space=pl.ANY),
                  pl.BlockSpec((None, ts, 512), lambda b, s: (b, s, P_B // 512)),
                  _vec_spec(256), _vec_spec(128), const((256, 512)), const((128, 1024)),
                  tab_spec, tab_spec, tab_spec],
        out_specs=[pl.BlockSpec((None, ts, 512), lambda b, s: (b, s, P_B // 512)),
                   _vec_spec(256), _vec_spec(128), const((256, 512)), const((128, 1024))],
        out_shape=[jax.ShapeDtypeStruct(dproj.shape, dproj.dtype), jax.ShapeDtypeStruct((1, 256), F32),
                   jax.ShapeDtypeStruct((1, 128), F32), jax.ShapeDtypeStruct((256, 512), F32),
                   jax.ShapeDtypeStruct((128, 1024), F32)],
        input_output_aliases={2: 0},
        compiler_params=_cparams(("arbitrary", "arbitrary")),
    )(dq, dkv, dproj, proj, qg, kvg, wq, wkv, *tabs)


def _fox_gate(proj, bf, name):
    bsz, seq, _ = proj.shape
    n_blk = seq // LANES

    def body(x_ref, bf_ref, f_ref):
        r_i = lax.broadcasted_iota(jnp.int32, (LANES, LANES), 0)
        c_i = lax.broadcasted_iota(jnp.int32, (LANES, LANES), 1)
        tril = (r_i >= c_i).astype(F32)
        bias = bf_ref[...]

        def blk(i, carry):
            r = pl.multiple_of(i * LANES, LANES)
            lf = _log_sigmoid(x_ref[pl.ds(r, LANES), :] + bias)
            f_ref[pl.ds(r, LANES), :] = jnp.dot(tril, lf, precision=HI, preferred_element_type=F32) + carry
            return carry + jnp.sum(lf, axis=0, keepdims=True)

        lax.fori_loop(0, n_blk, blk, jnp.zeros((1, LANES), F32))

    return pl.pallas_call(
        body, name=name, grid=(bsz,),
        in_specs=[pl.BlockSpec((None, seq, LANES), lambda b: (b, 0, P_CF // LANES)),
                  pl.BlockSpec((1, LANES), lambda b: (0, 0))],
        out_specs=pl.BlockSpec((None, seq, LANES), lambda b: (b, 0, 0)),
        out_shape=jax.ShapeDtypeStruct((bsz, seq, LANES), F32),
        compiler_params=_cparams(("parallel",)),
    )(proj, bf)


def _fox_gate_bwd(dfq, dfk_cols, dproj, proj, bf, name):
    bsz, seq, _ = proj.shape
    n_blk = seq // LANES

    def body(dfq_ref, dfk_ref, dp_any, x_ref, bf_ref, dp_ref, dbf_ref):
        del dp_any

        @pl.when(pl.program_id(0) == 0)
        def _():
            dbf_ref[...] = jnp.zeros_like(dbf_ref)

        r_i = lax.broadcasted_iota(jnp.int32, (LANES, LANES), 0)
        c_i = lax.broadcasted_iota(jnp.int32, (LANES, LANES), 1)
        triu = (r_i <= c_i).astype(F32)
        bias = bf_ref[...]

        def blk(t, carry):
            tail, dbf = carry
            r = pl.multiple_of((n_blk - 1 - t) * LANES, LANES)
            dc = dfk_ref[pl.ds(r, LANES), :]
            for hd in range(N_HEADS):
                dc = dc + jnp.where(c_i == hd, dfq_ref[hd, pl.ds(r, LANES), :], 0.0)
            dlf = jnp.dot(triu, dc, precision=HI, preferred_element_type=F32) + tail
            dx = dlf * (1.0 - jax.nn.sigmoid(x_ref[pl.ds(r, LANES), :] + bias))
            dp_ref[pl.ds(r, LANES), :] = dx.astype(dp_ref.dtype)
            return tail + jnp.sum(dc, axis=0, keepdims=True), dbf + jnp.sum(dx, axis=0, keepdims=True)

        z = jnp.zeros((1, LANES), F32)
        _, dbf = lax.fori_loop(0, n_blk, blk, (z, z))
        dbf_ref[...] += dbf

    return pl.pallas_call(
        body, name=name, grid=(bsz,),
        in_specs=[pl.BlockSpec((None, N_HEADS, seq, LANES), lambda b: (b, 0, 0, 0)),
                  pl.BlockSpec((None, seq, LANES), lambda b: (b, 0, 0)), pl.BlockSpec(memory_space=pl.ANY),
                  pl.BlockSpec((None, seq, LANES), lambda b: (b, 0, P_CF // LANES)),
                  pl.BlockSpec((1, LANES), lambda b: (0, 0))],
        out_specs=[pl.BlockSpec((None, seq, LANES), lambda b: (b, 0, P_CF // LANES)),
                   pl.BlockSpec((1, LANES), lambda b: (0, 0))],
        out_shape=[jax.ShapeDtypeStruct(dproj.shape, dproj.dtype), jax.ShapeDtypeStruct((1, LANES), F32)],
        input_output_aliases={2: 0},
        compiler_params=_cparams(("arbitrary",)),
    )(dfq, dfk_cols, dproj, proj, bf)


def _gate_terms(fc_ref, fr_ref, h, tq, tk):
    lane = lax.broadcasted_iota(jnp.int32, (tq, LANES), 1)
    fcol = jnp.sum(jnp.where(lane == h, fc_ref[...], 0.0), axis=1, keepdims=True)
    sub = lax.broadcasted_iota(jnp.int32, (8, tk), 0)
    frow = jnp.sum(jnp.where(sub == h, fr_ref[...], 0.0), axis=0, keepdims=True)
    return fcol - frow


def _scores(q_ref, k_ref, gate_refs, scale, h, masked, tq, tk):
    q = (q_ref[...].astype(F32) * scale).astype(BF16)
    s = lax.dot_general(q, k_ref[...].astype(BF16), _DN["nt"], preferred_element_type=F32)
    if gate_refs is not None:
        s = s + _gate_terms(gate_refs[0], gate_refs[1], h, tq, tk)
    if masked is not False:
        r_i = lax.broadcasted_iota(jnp.int32, (tq, tk), 0)
        c_i = lax.broadcasted_iota(jnp.int32, (tq, tk), 1)
        keep = c_i <= r_i
        s = jnp.where(keep if masked is True else jnp.logical_or(jnp.logical_not(masked), keep), s, NEG)
    return s, q


def _lanes(col):
    return jnp.broadcast_to(col, (col.shape[0], LANES))


def _attn_fwd(qa, q0, kva, kv0, mo, o0, gates, scale, name, tq=None):
    bsz, seq, _ = qa.shape
    tq = ATTN_TILE if tq is None else tq
    n_q = seq // tq
    gated = gates is not None

    def body(*refs):
        q_ref, k_ref, v_ref = refs[:3]
        gate_refs = refs[3:5] if gated else None
        o_ref, lse_ref, m_s, l_s, acc_s = refs[-5:]
        h, i, j = pl.program_id(1), pl.program_id(2), pl.program_id(3)

        @pl.when(j == 0)
        def _():
            m_s[...] = jnp.full_like(m_s, NEG)
            l_s[...] = jnp.zeros_like(l_s)
            acc_s[...] = jnp.zeros_like(acc_s)

        def step(masked):
            s, _ = _scores(q_ref, k_ref, gate_refs, scale, h, masked, tq, tq)
            m_prev = m_s[...]
            m_new = jnp.maximum(m_prev, jnp.max(s, axis=1, keepdims=True))
            alpha = jnp.exp(m_prev - m_new)
            p = jnp.exp(s - m_new)
            l_s[...] = alpha * l_s[...] + jnp.sum(p, axis=1, keepdims=True)
            acc_s[...] = alpha * acc_s[...] + jnp.dot(p.astype(BF16), v_ref[...].astype(BF16),
                                                      preferred_element_type=F32)
            m_s[...] = m_new

        @pl.when(j <= i)
        def _():
            step(j == i)

        @pl.when(j == i)
        def _():
            o_ref[...] = (acc_s[...] / l_s[...]).astype(o_ref.dtype)
            lse_ref[...] = _lanes(m_s[...] + jnp.log(l_s[...]))

    blk = (None, tq, LANES)
    in_specs = [pl.BlockSpec(blk, lambda b, h, i, j: (b, i, q0 + h)),
                pl.BlockSpec(blk, lambda b, h, i, j: (b, jnp.minimum(j, i), kv0 + 2 * h)),
                pl.BlockSpec(blk, lambda b, h, i, j: (b, jnp.minimum(j, i), kv0 + 2 * h + 1))]
    args = [qa, kva, kva]
    if gated:
        in_specs += [pl.BlockSpec(blk, lambda b, h, i, j: (b, i, 0)),
                     pl.BlockSpec((None, 8, tq), lambda b, h, i, j: (b, 0, jnp.minimum(j, i)))]
        args += list(gates)
    in_specs.append(pl.BlockSpec(memory_space=pl.ANY))
    args.append(mo)
    return pl.pallas_call(
        body, name=name, grid=(bsz, N_HEADS, n_q, n_q), in_specs=in_specs,
        out_specs=[pl.BlockSpec(blk, lambda b, h, i, j: (b, i, o0 + h)),
                   pl.BlockSpec((None, None, tq, LANES), lambda b, h, i, j: (b, h, i, 0))],
        out_shape=[jax.ShapeDtypeStruct(mo.shape, mo.dtype),
                   jax.ShapeDtypeStruct((bsz, N_HEADS, seq, LANES), F32)],
        scratch_shapes=[pltpu.VMEM((tq, 1), F32), pltpu.VMEM((tq, 1), F32), pltpu.VMEM((tq, LANES), F32)],
        input_output_aliases={len(args) - 1: 0},
        compiler_params=_cparams(("parallel", "parallel", "parallel", "arbitrary")),
    )(*args)


def _attn_bwd_q(qa, q0, kva, kv0, mo, dmo, o0, lse, gates, scale, out, out0, name, tq=None):
    bsz, seq, _ = qa.shape
    tq = ATTN_TILE if tq is None else tq
    n_q = seq // tq
    gated = gates is not None
    aliased = not isinstance(out, jax.ShapeDtypeStruct)

    def body(*refs):
        q_ref, k_ref, v_ref, o_ref, do_ref, lse_ref = refs[:6]
        gate_refs = refs[6:8] if gated else None
        dq_ref, delta_ref, dfq_ref, acc_s, dl_s, df_s = refs[-6:]
        h, i, j = pl.program_id(1), pl.program_id(2), pl.program_id(3)

        @pl.when(j == 0)
        def _():
            acc_s[...] = jnp.zeros_like(acc_s)
            df_s[...] = jnp.zeros_like(df_s)
            dl_s[...] = jnp.sum(do_ref[...] * o_ref[...].astype(F32), axis=1, keepdims=True)

        def step(masked):
            s, _ = _scores(q_ref, k_ref, gate_refs, scale, h, masked, tq, tq)
            p = jnp.exp(s - lse_ref[:, 0:1])
            dp = lax.dot_general(do_ref[...].astype(BF16), v_ref[...].astype(BF16), _DN["nt"],
                                 preferred_element_type=F32)
            ds = p * (dp - dl_s[...])
            acc_s[...] += jnp.dot(ds.astype(BF16), k_ref[...].astype(BF16), preferred_element_type=F32)
            df_s[...] += jnp.sum(ds, axis=1, keepdims=True)

        @pl.when(j <= i)
        def _():
            step(j == i)

        @pl.when(j == i)
        def _():
            dq_ref[...] = (acc_s[...] * scale).astype(dq_ref.dtype)
            delta_ref[...] = _lanes(dl_s[...])
            dfq_ref[...] = _lanes(df_s[...])

    blk = (None, tq, LANES)
    col = pl.BlockSpec((None, None, tq, LANES), lambda b, h, i, j: (b, h, i, 0))
    in_specs = [pl.BlockSpec(blk, lambda b, h, i, j: (b, i, q0 + h)),
                pl.BlockSpec(blk, lambda b, h, i, j: (b, jnp.minimum(j, i), kv0 + 2 * h)),
                pl.BlockSpec(blk, lambda b, h, i, j: (b, jnp.minimum(j, i), kv0 + 2 * h + 1)),
                pl.BlockSpec(blk, lambda b, h, i, j: (b, i, o0 + h)),
                pl.BlockSpec(blk, lambda b, h, i, j: (b, i, o0 + h)), col]
    args = [qa, kva, kva, mo, dmo, lse]
    if gated:
        in_specs += [pl.BlockSpec(blk, lambda b, h, i, j: (b, i, 0)),
                     pl.BlockSpec((None, 8, tq), lambda b, h, i, j: (b, 0, jnp.minimum(j, i)))]
        args += list(gates)
    aliases = {}
    if aliased:
        in_specs.append(pl.BlockSpec(memory_space=pl.ANY))
        args.append(out)
        aliases = {len(args) - 1: 0}
    vec = jax.ShapeDtypeStruct((bsz, N_HEADS, seq, LANES), F32)
    return pl.pallas_call(
        body, name=name, grid=(bsz, N_HEADS, n_q, n_q), in_specs=in_specs,
        out_specs=[pl.BlockSpec(blk, lambda b, h, i, j: (b, i, out0 + h)), col, col],
        out_shape=[jax.ShapeDtypeStruct(out.shape, out.dtype), vec, vec],
        scratch_shapes=[pltpu.VMEM((tq, LANES), F32), pltpu.VMEM((tq, 1), F32), pltpu.VMEM((tq, 1), F32)],
        input_output_aliases=aliases,
        compiler_params=_cparams(("parallel", "parallel", "parallel", "arbitrary")),
    )(*args)


def _attn_bwd_kv(qa, q0, kva, kv0, dmo, o0, lse, delta, gates, scale, out, out0, name, tq=None):
    bsz, seq, _ = qa.shape
    tq = ATTN_TILE if tq is None else tq
    n_q = seq // tq
    gated = gates is not None
    aliased = not isinstance(out, jax.ShapeDtypeStruct)

    def body(*refs):
        q_ref, k_ref, v_ref, do_ref, lse_ref, dl_ref = refs[:6]
        gate_refs = refs[6:8] if gated else None
        dkv_ref, dfk_ref, dk_s, dv_s, df_s = refs[-5:]
        h, j, i = pl.program_id(1), pl.program_id(2), pl.program_id(3)

        @pl.when(i == 0)
        def _():
            dk_s[...] = jnp.zeros_like(dk_s)
            dv_s[...] = jnp.zeros_like(dv_s)
            df_s[...] = jnp.zeros_like(df_s)

        def step(masked):
            s, q = _scores(q_ref, k_ref, gate_refs, scale, h, masked, tq, tq)
            p = jnp.exp(s - lse_ref[:, 0:1])
            do_b = do_ref[...].astype(BF16)
            dp = lax.dot_general(do_b, v_ref[...].astype(BF16), _DN["nt"], preferred_element_type=F32)
            ds = p * (dp - dl_ref[:, 0:1])
            dv_s[...] += lax.dot_general(p.astype(BF16), do_b, _DN["tn"], preferred_element_type=F32)
            dk_s[...] += lax.dot_general(ds.astype(BF16), q, _DN["tn"], preferred_element_type=F32)
            df_s[...] -= jnp.sum(ds, axis=0, keepdims=True)

        @pl.when(i > j)
        def _():
            step(False)

        @pl.when(i == j)
        def _():
            step(True)

        @pl.when(i == n_q - 1)
        def _():
            dkv_ref[:, 0:LANES] = dk_s[...].astype(dkv_ref.dtype)
            dkv_ref[:, LANES:2 * LANES] = dv_s[...].astype(dkv_ref.dtype)
            dfk_ref[...] = df_s[...]

    blk = (None, tq, LANES)
    col = pl.BlockSpec((None, None, tq, LANES), lambda b, h, j, i: (b, h, jnp.maximum(i, j), 0))
    in_specs = [pl.BlockSpec(blk, lambda b, h, j, i: (b, jnp.maximum(i, j), q0 + h)),
                pl.BlockSpec(blk, lambda b, h, j, i: (b, j, kv0 + 2 * h)),
                pl.BlockSpec(blk, lambda b, h, j, i: (b, j, kv0 + 2 * h + 1)),
                pl.BlockSpec(blk, lambda b, h, j, i: (b, jnp.maximum(i, j), o0 + h)), col, col]
    args = [qa, kva, kva, dmo, lse, delta]
    if gated:
        in_specs += [pl.BlockSpec(blk, lambda b, h, j, i: (b, jnp.maximum(i, j), 0)),
                     pl.BlockSpec((None, 8, tq), lambda b, h, j, i: (b, 0, j))]
        args += list(gates)
    aliases = {}
    if aliased:
        in_specs.append(pl.BlockSpec(memory_space=pl.ANY))
        args.append(out)
        aliases = {len(args) - 1: 0}
    return pl.pallas_call(
        body, name=name, grid=(bsz, N_HEADS, n_q, n_q), in_specs=in_specs,
        out_specs=[pl.BlockSpec((None, tq, 2 * LANES), lambda b, h, j, i: (b, j, out0 + h)),
                   pl.BlockSpec((None, None, 1, tq), lambda b, h, j, i: (b, h, 0, j))],
        out_shape=[jax.ShapeDtypeStruct(out.shape, out.dtype), jax.ShapeDtypeStruct((bsz, N_HEADS, 1, seq), F32)],
        scratch_shapes=[pltpu.VMEM((tq, LANES), F32), pltpu.VMEM((tq, LANES), F32), pltpu.VMEM((1, tq), F32)],
        input_output_aliases=aliases,
        compiler_params=_cparams(("parallel", "parallel", "parallel", "arbitrary")),
    )(*args)


def _block_logits(q, k_ref, gate, j, scale_unused, h, masked, tq):
    del scale_unused
    r = pl.multiple_of(j * tq, tq)
    s = lax.dot_general(q, k_ref[pl.ds(r, tq), :].astype(BF16), _DN["nt"], preferred_element_type=F32)
    if gate is not None:
        fcol, fr_ref = gate
        sub = lax.broadcasted_iota(jnp.int32, (8, tq), 0)
        frow = jnp.sum(jnp.where(sub == h, fr_ref[:, pl.ds(r, tq)], 0.0), axis=0, keepdims=True)
        s = s + (fcol - frow)
    if masked:
        r_i = lax.broadcasted_iota(jnp.int32, (tq, tq), 0)
        c_i = lax.broadcasted_iota(jnp.int32, (tq, tq), 1)
        s = jnp.where(c_i <= r_i, s, NEG)
    return s, r


def _gate_col(fc_ref, h, tq):
    lane = lax.broadcasted_iota(jnp.int32, (tq, LANES), 1)
    return jnp.sum(jnp.where(lane == h, fc_ref[...], 0.0), axis=1, keepdims=True)


def _attn_fwd_loop(qa, q0, kva, kv0, mo, o0, gates, scale, name, tq=None):
    bsz, seq, _ = qa.shape
    tq = ATTN_TILE if tq is None else tq
    n_q = seq // tq
    gated = gates is not None

    def body(*refs):
        q_ref, k_ref, v_ref = refs[:3]
        o_ref, lse_ref = refs[-2:]
        h, i = pl.program_id(1), pl.program_id(2)
        q = (q_ref[...].astype(F32) * scale).astype(BF16)
        gate = (_gate_col(refs[3], h, tq), refs[4]) if gated else None

        def step(j, carry, masked):
            m_prev, l_prev, acc = carry
            s, r = _block_logits(q, k_ref, gate, j, None, h, masked, tq)
            m_new = jnp.maximum(m_prev, jnp.max(s, axis=1, keepdims=True))
            alpha = jnp.exp(m_prev - m_new)
            p = jnp.exp(s - m_new)
            l_new = alpha * l_prev + jnp.sum(p, axis=1, keepdims=True)
            acc = alpha * acc + jnp.dot(p.astype(BF16), v_ref[pl.ds(r, tq), :].astype(BF16),
                                        preferred_element_type=F32)
            return m_new, l_new, acc

        init = (jnp.full((tq, 1), NEG, F32), jnp.zeros((tq, 1), F32), jnp.zeros((tq, LANES), F32))
        carry = lax.fori_loop(0, i, lambda j, c: step(j, c, False), init)
        m_f, l_f, acc = step(i, carry, True)
        o_ref[...] = (acc / l_f).astype(o_ref.dtype)
        lse_ref[...] = _lanes(m_f + jnp.log(l_f))

    blk = (None, tq, LANES)
    full = (None, seq, LANES)
    in_specs = [pl.BlockSpec(blk, lambda b, h, i: (b, i, q0 + h)),
                pl.BlockSpec(full, lambda b, h, i: (b, 0, kv0 + 2 * h)),
                pl.BlockSpec(full, lambda b, h, i: (b, 0, kv0 + 2 * h + 1))]
    args = [qa, kva, kva]
    if gated:
        in_specs += [pl.BlockSpec(blk, lambda b, h, i: (b, i, 0)),
                     pl.BlockSpec((None, 8, seq), lambda b, h, i: (b, 0, 0))]
        args += list(gates)
    in_specs.append(pl.BlockSpec(memory_space=pl.ANY))
    args.append(mo)
    return pl.pallas_call(
        body, name=name, grid=(bsz, N_HEADS, n_q), in_specs=in_specs,
        out_specs=[pl.BlockSpec(blk, lambda b, h, i: (b, i, o0 + h)),
                   pl.BlockSpec((None, None, tq, LANES), lambda b, h, i: (b, h, i, 0))],
        out_shape=[jax.ShapeDtypeStruct(mo.shape, mo.dtype),
                   jax.ShapeDtypeStruct((bsz, N_HEADS, seq, LANES), F32)],
        input_output_aliases={len(args) - 1: 0},
        compiler_params=_cparams(("parallel", "parallel", "parallel")),
    )(*args)


def _attn_bwd_q_loop(qa, q0, kva, kv0, mo, dmo, o0, lse, gates, scale, out, out0, name, tq=None):
    bsz, seq, _ = qa.shape
    tq = ATTN_TILE if tq is None else tq
    n_q = seq // tq
    gated = gates is not None
    aliased = not isinstance(out, jax.ShapeDtypeStruct)

    def body(*refs):
        q_ref, k_ref, v_ref, o_ref, do_ref, lse_ref = refs[:6]
        dq_ref, delta_ref, dfq_ref = refs[-3:]
        h, i = pl.program_id(1), pl.program_id(2)
        q = (q_ref[...].astype(F32) * scale).astype(BF16)
        gate = (_gate_col(refs[6], h, tq), refs[7]) if gated else None
        do_v = do_ref[...]
        do_b = do_v.astype(BF16)
        delta = jnp.sum(do_v * o_ref[...].astype(F32), axis=1, keepdims=True)
        lse_v = lse_ref[:, 0:1]

        def step(j, carry, masked):
            acc, dfq = carry
            s, r = _block_logits(q, k_ref, gate, j, None, h, masked, tq)
            p = jnp.exp(s - lse_v)
            dp = lax.dot_general(do_b, v_ref[pl.ds(r, tq), :].astype(BF16), _DN["nt"], preferred_element_type=F32)
            ds = p * (dp - delta)
            acc = acc + jnp.dot(ds.astype(BF16), k_ref[pl.ds(r, tq), :].astype(BF16), preferred_element_type=F32)
            return acc, dfq + jnp.sum(ds, axis=1, keepdims=True)

        init = (jnp.zeros((tq, LANES), F32), jnp.zeros((tq, 1), F32))
        carry = lax.fori_loop(0, i, lambda j, c: step(j, c, False), init)
        acc, dfq = step(i, carry, True)
        dq_ref[...] = (acc * scale).astype(dq_ref.dtype)
        delta_ref[...] = _lanes(delta)
        dfq_ref[...] = _lanes(dfq)

    blk = (None, tq, LANES)
    full = (None, seq, LANES)
    stat = pl.BlockSpec((None, None, tq, LANES), lambda b, h, i: (b, h, i, 0))
    in_specs = [pl.BlockSpec(blk, lambda b, h, i: (b, i, q0 + h)),
                pl.BlockSpec(full, lambda b, h, i: (b, 0, kv0 + 2 * h)),
                pl.BlockSpec(full, lambda b, h, i: (b, 0, kv0 + 2 * h + 1)),
                pl.BlockSpec(blk, lambda b, h, i: (b, i, o0 + h)),
                pl.BlockSpec(blk, lambda b, h, i: (b, i, o0 + h)), stat]
    args = [qa, kva, kva, mo, dmo, lse]
    if gated:
        in_specs += [pl.BlockSpec(blk, lambda b, h, i: (b, i, 0)),
                     pl.BlockSpec((None, 8, seq), lambda b, h, i: (b, 0, 0))]
        args += list(gates)
    aliases = {}
    if aliased:
        in_specs.append(pl.BlockSpec(memory_space=pl.ANY))
        args.append(out)
        aliases = {len(args) - 1: 0}
    vec = jax.ShapeDtypeStruct((bsz, N_HEADS, seq, LANES), F32)
    return pl.pallas_call(
        body, name=name, grid=(bsz, N_HEADS, n_q), in_specs=in_specs,
        out_specs=[pl.BlockSpec(blk, lambda b, h, i: (b, i, out0 + h)), stat, stat],
        out_shape=[jax.ShapeDtypeStruct(out.shape, out.dtype), vec, vec],
        input_output_aliases=aliases,
        compiler_params=_cparams(("parallel", "parallel", "parallel")),
    )(*args)


def _attn_bwd_kv_loop(qa, q0, kva, kv0, dmo, o0, lse, delta, gates, scale, out, out0, name, tq=None):
    bsz, seq, _ = qa.shape
    tq = ATTN_TILE if tq is None else tq
    n_q = seq // tq
    gated = gates is not None
    aliased = not isinstance(out, jax.ShapeDtypeStruct)

    def body(*refs):
        q_ref, k_ref, v_ref, do_ref, lse_ref, dl_ref = refs[:6]
        dkv_ref, dfk_ref = refs[-2:]
        h, j = pl.program_id(1), pl.program_id(2)
        k_b = k_ref[...].astype(BF16)
        v_b = v_ref[...].astype(BF16)
        if gated:
            fc_ref, fr_ref = refs[6], refs[7]
            sub = lax.broadcasted_iota(jnp.int32, (8, tq), 0)
            frow = jnp.sum(jnp.where(sub == h, fr_ref[...], 0.0), axis=0, keepdims=True)
            lane = lax.broadcasted_iota(jnp.int32, (tq, LANES), 1)

        def step(i, carry, masked):
            dk, dv, dfk = carry
            r = pl.multiple_of(i * tq, tq)
            q = (q_ref[pl.ds(r, tq), :].astype(F32) * scale).astype(BF16)
            s = lax.dot_general(q, k_b, _DN["nt"], preferred_element_type=F32)
            if gated:
                fcol = jnp.sum(jnp.where(lane == h, fc_ref[pl.ds(r, tq), :], 0.0), axis=1, keepdims=True)
                s = s + (fcol - frow)
            if masked:
                r_i = lax.broadcasted_iota(jnp.int32, (tq, tq), 0)
                c_i = lax.broadcasted_iota(jnp.int32, (tq, tq), 1)
                s = jnp.where(c_i <= r_i, s, NEG)
            p = jnp.exp(s - lse_ref[pl.ds(r, tq), 0:1])
            do_b = do_ref[pl.ds(r, tq), :].astype(BF16)
            dp = lax.dot_general(do_b, v_b, _DN["nt"], preferred_element_type=F32)
            ds = p * (dp - dl_ref[pl.ds(r, tq), 0:1])
            dv = dv + lax.dot_general(p.astype(BF16), do_b, _DN["tn"], preferred_element_type=F32)
            dk = dk + lax.dot_general(ds.astype(BF16), q, _DN["tn"], preferred_element_type=F32)
            return dk, dv, dfk - jnp.sum(ds, axis=0, keepdims=True)

        init = (jnp.zeros((tq, LANES), F32), jnp.zeros((tq, LANES), F32), jnp.zeros((1, tq), F32))
        carry = step(j, init, True)
        dk, dv, dfk = lax.fori_loop(j + 1, n_q, lambda i, c: step(i, c, False), carry)
        dkv_ref[:, 0:LANES] = dk.astype(dkv_ref.dtype)
        dkv_ref[:, LANES:2 * LANES] = dv.astype(dkv_ref.dtype)
        dfk_ref[...] = dfk

    blk = (None, tq, LANES)
    full = (None, seq, LANES)
    stat = pl.BlockSpec((None, None, seq, LANES), lambda b, h, j: (b, h, 0, 0))
    in_specs = [pl.BlockSpec(full, lambda b, h, j: (b, 0, q0 + h)),
                pl.BlockSpec(blk, lambda b, h, j: (b, j, kv0 + 2 * h)),
                pl.BlockSpec(blk, lambda b, h, j: (b, j, kv0 + 2 * h + 1)),
                pl.BlockSpec(full, lambda b, h, j: (b, 0, o0 + h)), stat, stat]
    args = [qa, kva, kva, dmo, lse, delta]
    if gated:
        in_specs += [pl.BlockSpec(full, lambda b, h, j: (b, 0, 0)),
                     pl.BlockSpec((None, 8, tq), lambda b, h, j: (b, 0, j))]
        args += list(gates)
    aliases = {}
    if aliased:
        in_specs.append(pl.BlockSpec(memory_space=pl.ANY))
        args.append(out)
        aliases = {len(args) - 1: 0}
    return pl.pallas_call(
        body, name=name, grid=(bsz, N_HEADS, n_q), in_specs=in_specs,
        out_specs=[pl.BlockSpec((None, tq, 2 * LANES), lambda b, h, j: (b, j, out0 + h)),
                   pl.BlockSpec((None, None, 1, tq), lambda b, h, j: (b, h, 0, j))],
        out_shape=[jax.ShapeDtypeStruct(out.shape, out.dtype), jax.ShapeDtypeStruct((bsz, N_HEADS, 1, seq), F32)],
        input_output_aliases=aliases,
        compiler_params=_cparams(("parallel", "parallel", "parallel")),
    )(*args)


def _gmlp_fn(uv, lng, lnb, ws, bst):
    u = jax.nn.gelu(uv[:, 0:GROUP_WIDTH])
    gv = jax.nn.gelu(uv[:, GROUP_WIDTH:2 * GROUP_WIDTH])
    mu = jnp.mean(gv, axis=-1, keepdims=True)
    vc = gv - mu
    var = jnp.mean(vc * vc, axis=-1, keepdims=True)
    vln = vc * lax.rsqrt(var + LN_EPS) * lng + lnb
    r_i = lax.broadcasted_iota(jnp.int32, (D_CHUNK, D_CHUNK), 0)
    c_i = lax.broadcasted_iota(jnp.int32, (D_CHUNK, D_CHUNK), 1)
    lane_g = lax.broadcasted_iota(jnp.int32, (D_CHUNK, GROUP_WIDTH), 1) // HEAD_DIM
    e_r = lax.broadcasted_iota(jnp.int32, (LANES, GROUP_WIDTH), 0)
    e_c = lax.broadcasted_iota(jnp.int32, (LANES, GROUP_WIDTH), 1)
    expand = (e_r == e_c // HEAD_DIM).astype(F32)
    mixed = jnp.dot(bst, expand, precision=HI, preferred_element_type=F32)
    for g in range(4):
        w = jnp.where(r_i >= c_i, ws[g], 0.0)
        mixed = mixed + jnp.where(lane_g == g, _bdot(w, vln, "nn"), 0.0)
    return u * mixed


def _gmlp_fwd(proj, mo, lng, lnb, ws, bst, name):
    bsz, seq, _ = proj.shape

    def body(p_ref, mo_any, lng_ref, lnb_ref, ws_ref, bst_ref, o_ref):
        del mo_any
        for c in range(GMLP_STEP):
            rows = slice(c * D_CHUNK, (c + 1) * D_CHUNK)
            o_ref[rows, :] = _gmlp_fn(p_ref[rows, :], lng_ref[...], lnb_ref[...], ws_ref[...],
                                      bst_ref[...]).astype(o_ref.dtype)

    blk = GMLP_STEP * D_CHUNK
    return pl.pallas_call(
        body, name=name, grid=(bsz, seq // blk),
        in_specs=[pl.BlockSpec((None, blk, 512), lambda b, s: (b, s, P_D // 512)),
                  pl.BlockSpec(memory_space=pl.ANY), _vec_spec(256), _vec_spec(256),
                  pl.BlockSpec((4, D_CHUNK, D_CHUNK), lambda b, s: (0, 0, 0)),
                  pl.BlockSpec((D_CHUNK, LANES), lambda b, s: (0, 0))],
        out_specs=pl.BlockSpec((None, blk, GROUP_WIDTH), lambda b, s: (b, s, 1280 // GROUP_WIDTH)),
        out_shape=jax.ShapeDtypeStruct(mo.shape, mo.dtype),
        input_output_aliases={1: 0},
        compiler_params=_cparams(("parallel", "parallel")),
    )(proj, mo, lng, lnb, ws, bst)


def _gmlp_bwd(dmo, dproj, proj, lng, lnb, ws, bst, name):
    bsz, seq, _ = proj.shape

    def body(do_ref, dp_any, p_ref, lng_ref, lnb_ref, ws_ref, bst_ref, dp_ref, dlg_ref, dlb_ref, dws_ref, dbst_ref):
        del dp_any
        first = jnp.logical_and(pl.program_id(0) == 0, pl.program_id(1) == 0)

        @pl.when(first)
        def _():
            dlg_ref[...] = jnp.zeros_like(dlg_ref)
            dlb_ref[...] = jnp.zeros_like(dlb_ref)
            dws_ref[...] = jnp.zeros_like(dws_ref)
            dbst_ref[...] = jnp.zeros_like(dbst_ref)

        for c in range(GMLP_STEP):
            rows = slice(c * D_CHUNK, (c + 1) * D_CHUNK)
            _, vjp = jax.vjp(_gmlp_fn, p_ref[rows, :], lng_ref[...], lnb_ref[...], ws_ref[...], bst_ref[...])
            duv, dlg, dlb, dws, dbst = vjp(do_ref[rows, :])
            dp_ref[rows, :] = duv.astype(dp_ref.dtype)
            dlg_ref[...] += dlg
            dlb_ref[...] += dlb
            dws_ref[...] += dws
            dbst_ref[...] += dbst

    const2 = lambda shape: pl.BlockSpec(shape, lambda b, s: (0,) * len(shape))
    blk = GMLP_STEP * D_CHUNK
    return pl.pallas_call(
        body, name=name, grid=(bsz, seq // blk),
        in_specs=[pl.BlockSpec((None, blk, GROUP_WIDTH), lambda b, s: (b, s, 1280 // GROUP_WIDTH)),
                  pl.BlockSpec(memory_space=pl.ANY),
                  pl.BlockSpec((None, blk, 512), lambda b, s: (b, s, P_D // 512)),
                  _vec_spec(256), _vec_spec(256), const2((4, D_CHUNK, D_CHUNK)), const2((D_CHUNK, LANES))],
        out_specs=[pl.BlockSpec((None, blk, 512), lambda b, s: (b, s, P_D // 512)),
                   _vec_spec(256), _vec_spec(256), const2((4, D_CHUNK, D_CHUNK)), const2((D_CHUNK, LANES))],
        out_shape=[jax.ShapeDtypeStruct(dproj.shape, dproj.dtype), jax.ShapeDtypeStruct((1, 256), F32),
                   jax.ShapeDtypeStruct((1, 256), F32), jax.ShapeDtypeStruct((4, D_CHUNK, D_CHUNK), F32),
                   jax.ShapeDtypeStruct((D_CHUNK, LANES), F32)],
        input_output_aliases={1: 0},
        compiler_params=_cparams(("arbitrary", "arbitrary")),
    )(dmo, dproj, proj, lng, lnb, ws, bst)


def _ada_fwd(c_all, ada_w, name):
    n_b = c_all.shape[0]
    depth, d, cols = ada_w.shape

    def body(c_ref, w_ref, o_ref):
        cv = c_ref[...]
        act = (cv * jax.nn.sigmoid(cv)).astype(BF16)
        o_ref[...] = jnp.dot(act, w_ref[...].astype(BF16), preferred_element_type=F32)

    return pl.pallas_call(
        body, name=name, grid=(depth,),
        in_specs=[pl.BlockSpec((n_b, d), lambda l: (0, 0)), pl.BlockSpec((None, d, cols), lambda l: (l, 0, 0))],
        out_specs=pl.BlockSpec((None, n_b, cols), lambda l: (l, 0, 0)),
        out_shape=jax.ShapeDtypeStruct((depth, n_b, cols), F32),
        compiler_params=_cparams(("parallel",)),
    )(c_all, ada_w)


def _ada_bwd(c_all, dmod_cols, dmod_full, name):
    n_b, d = c_all.shape
    depth, _, cols = dmod_cols.shape
    full = dmod_full.shape[-1]

    def body(c_ref, dm_ref, df_ref, gw_ref, gb_ref):
        cv = c_ref[...]
        act = (cv * jax.nn.sigmoid(cv)).astype(BF16)
        gw_ref[...] = lax.dot_general(act, dm_ref[...].astype(BF16), (((0,), (0,)), ((), ())),
                                      preferred_element_type=F32)
        gb_ref[...] = jnp.sum(df_ref[...], axis=0, keepdims=True)

    return pl.pallas_call(
        body, name=name, grid=(depth,),
        in_specs=[pl.BlockSpec((n_b, d), lambda l: (0, 0)), pl.BlockSpec((None, n_b, cols), lambda l: (l, 0, 0)),
                  pl.BlockSpec((None, n_b, full), lambda l: (l, 0, 0))],
        out_specs=[pl.BlockSpec((None, d, cols), lambda l: (l, 0, 0)),
                   pl.BlockSpec((None, 1, full), lambda l: (l, 0, 0))],
        out_shape=[jax.ShapeDtypeStruct((depth, d, cols), F32), jax.ShapeDtypeStruct((depth, 1, full), F32)],
        compiler_params=_cparams(("parallel",)),
    )(c_all, dmod_cols, dmod_full)


def _adamw(gparts, own, w, m, v, name, layer=0, prev=None):
    n_p, rows, cols = gparts.shape
    assert w.shape[1:] == (rows, cols)
    tr = rows
    if rows > 512:
        tr = next(c for c in range(512, 7, -8) if rows % c == 0)
    has_own = own is not None
    n_prev = 0 if prev is None else 4

    def body(*refs):
        if has_own:
            slot_ref, refs = refs[0], refs[1:]
        g_ref = refs[0]
        own_ref = refs[1] if has_own else None
        w_ref, m_ref, v_ref = refs[1 + has_own:4 + has_own]
        go_ref, do_ref, mo_ref, vo_ref = refs[4 + has_own + n_prev:]
        g = None
        for p in range(n_p):
            term = g_ref[p].astype(F32)
            if has_own:
                term = jnp.where(slot_ref[0] == p, own_ref[...].astype(F32), term)
            g = term if g is None else g + term
        m_new = ADAM_B1 * m_ref[...] + (1.0 - ADAM_B1) * g
        v_new = ADAM_B2 * v_ref[...] + (1.0 - ADAM_B2) * (g * g)
        m_hat = m_new / (1.0 - ADAM_B1 ** ADAM_STEP)
        v_hat = v_new / (1.0 - ADAM_B2 ** ADAM_STEP)
        go_ref[...] = g
        do_ref[...] = -ADAM_LR * (m_hat / (jnp.sqrt(v_hat) + ADAM_EPS) + ADAM_WD * w_ref[...])
        mo_ref[...] = m_new
        vo_ref[...] = v_new

    spec = pl.BlockSpec((None, tr, cols), lambda i, *_: (layer, i, 0))
    in_specs = [pl.BlockSpec((n_p, tr, cols), lambda i, *_: (0, i, 0))]
    args = [gparts]
    if has_own:
        in_specs.append(pl.BlockSpec((None, tr, cols), lambda i, slot: (slot[0], i, 0)))
        args.append(own[0])
    in_specs += [spec, spec, spec]
    args += [w, m, v]
    aliases = {}
    if prev is not None:
        aliases = {has_own + len(args) + k: k for k in range(4)}
        in_specs += [pl.BlockSpec(memory_space=pl.ANY)] * 4
        args += list(prev)
    shp = jax.ShapeDtypeStruct(w.shape, F32)
    out_specs, out_shape = [spec, spec, spec, spec], [shp, shp, shp, shp]
    if not has_own:
        return pl.pallas_call(
            body, name=name, grid=(rows // tr,), in_specs=in_specs, out_specs=out_specs, out_shape=out_shape,
            input_output_aliases=aliases, compiler_params=_cparams(("parallel",)),
        )(*args)
    return pl.pallas_call(
        body, name=name, out_shape=out_shape, input_output_aliases=aliases,
        grid_spec=pltpu.PrefetchScalarGridSpec(num_scalar_prefetch=1, grid=(rows // tr,), in_specs=in_specs,
                                               out_specs=out_specs),
        compiler_params=_cparams(("parallel",)),
    )(jnp.reshape(own[1], (1,)).astype(jnp.int32), *args)


def _sum_parts(parts, name):
    n_p, rows, cols = parts.shape
    tr = 256 if rows % 256 == 0 else rows

    def body(p_ref, o_ref):
        acc = p_ref[0]
        for p in range(1, n_p):
            acc = acc + p_ref[p]
        o_ref[...] = acc

    return pl.pallas_call(
        body, name=name, grid=(rows // tr,),
        in_specs=[pl.BlockSpec((n_p, tr, cols), lambda i: (0, i, 0))],
        out_specs=pl.BlockSpec((tr, cols), lambda i: (i, 0)),
        out_shape=jax.ShapeDtypeStruct((rows, cols), F32),
        compiler_params=_cparams(("parallel",)),
    )(parts)


def _all_gather(arrs, name):
    n = len(arrs)

    def body(*refs):
        in_refs, out_refs = refs[:n], refs[n:2 * n]
        send_sems, recv_sems, loc_sems = refs[2 * n:]
        x, y, c = lax.axis_index("x"), lax.axis_index("y"), lax.axis_index("c")
        me, sibling = (x, y, c), (x, y, 1 - c)
        chips = [(1 - x, y), (x, 1 - y), (1 - x, 1 - y)]

        def copy(a, k, block, to, src=None):
            slot = out_refs[a].at[4 * block[0] + 2 * block[1] + block[2]]
            return pltpu.make_async_remote_copy(
                src_ref=slot if src is None else src, dst_ref=slot, send_sem=send_sems.at[a, k],
                recv_sem=recv_sems.at[a, k], device_id=to, device_id_type=pl.DeviceIdType.MESH)

        mine = [pltpu.make_async_copy(in_refs[a], out_refs[a].at[4 * x + 2 * y + c], loc_sems.at[a])
                for a in range(n)]
        for cp in mine:
            cp.start()
        first = []
        for a in range(n):
            first.append(copy(a, 0, me, sibling, src=in_refs[a]))
            first += [copy(a, 1 + j, me, (*chip, c), src=in_refs[a]) for j, chip in enumerate(chips)]
        for cp in first:
            cp.start()
        passed = []
        for j, chip in enumerate(chips):
            for a in range(n):
                copy(a, 1 + j, (*chip, c), me).wait_recv()
                cp = copy(a, 4 + j, (*chip, c), sibling)
                cp.start()
                passed.append(cp)
        for a in range(n):
            copy(a, 0, sibling, me).wait_recv()
        for j, chip in enumerate(chips):
            for a in range(n):
                copy(a, 4 + j, (*chip, 1 - c), me).wait_recv()
        for cp in first + passed:
            cp.wait_send()
        for cp in mine:
            cp.wait()

    any_spec = pl.BlockSpec(memory_space=pl.ANY)
    return pl.pallas_call(
        body, name=name, in_specs=[any_spec] * n, out_specs=[any_spec] * n,
        out_shape=[jax.ShapeDtypeStruct((N_DEV,) + a.shape, a.dtype) for a in arrs],
        scratch_shapes=[pltpu.SemaphoreType.DMA((n, N_DEV - 1)), pltpu.SemaphoreType.DMA((n, N_DEV - 1)),
                        pltpu.SemaphoreType.DMA((n,))],
    )(*arrs)


def _flip_peers():
    x, y, c = lax.axis_index("x"), lax.axis_index("y"), lax.axis_index("c")
    peers = []
    for fx, fy, fc in [(fx, fy, fc) for fx in (0, 1) for fy in (0, 1) for fc in (0, 1)][1:]:
        px, py, pc = (1 - x if fx else x), (1 - y if fy else y), (1 - c if fc else c)
        peers.append(((px, py, pc), 4 * px + 2 * py + pc))
    return 4 * x + 2 * y + c, peers


def _push_start(srcs, name, whole=False):
    n, n_peer = len(srcs), N_DEV - 1
    if whole:
        me_w = 4 * lax.axis_index("x") + 2 * lax.axis_index("y") + lax.axis_index("c")
        lands = [lax.dynamic_update_slice_in_dim(lax.empty((N_DEV,) + a.shape, a.dtype), a[None], me_w, axis=0)
                 for a in srcs]
    else:
        lands = [lax.empty(a.shape, a.dtype) for a in srcs]

    def body(*refs):
        src_refs, land_refs = refs[:n], refs[n:2 * n]
        send_sems, recv_sems = refs[2 * n], refs[2 * n + 1]
        token = refs[-1]
        me, peers = _flip_peers()
        for k, (dev, idx) in enumerate(peers):
            for a in range(n):
                pltpu.make_async_remote_copy(
                    src_ref=src_refs[a] if whole else src_refs[a].at[idx], dst_ref=land_refs[a].at[me],
                    send_sem=send_sems.at[a * n_peer + k], recv_sem=recv_sems.at[a * n_peer + k], device_id=dev,
                    device_id_type=pl.DeviceIdType.MESH).start()
        token[...] = jnp.zeros_like(token)

    hbm = pl.BlockSpec(memory_space=pltpu.HBM)
    sem = pl.BlockSpec(memory_space=pltpu.SEMAPHORE)
    arrs = list(srcs) + lands
    res = pl.pallas_call(
        body, name=name, in_specs=[hbm] * (2 * n),
        out_specs=(sem, sem, *[hbm] * (2 * n), pl.BlockSpec(memory_space=pltpu.VMEM)),
        out_shape=(pltpu.SemaphoreType.DMA((n * n_peer,)), pltpu.SemaphoreType.DMA((n * n_peer,)),
                   *[pltpu.HBM(a.shape, a.dtype) for a in arrs], jax.ShapeDtypeStruct((8, LANES), F32)),
        input_output_aliases={i: 2 + i for i in range(2 * n)},
        compiler_params=pltpu.CompilerParams(has_side_effects=pltpu.SideEffectType.DATAFLOW_SIDE_EFFECTING),
    )(*[pltpu.with_memory_space_constraint(a, pltpu.HBM) for a in arrs])
    return res[0], res[1], list(res[2:2 + n]), list(res[2 + n:2 + 2 * n]), res[-1]


def _push_wait(send_sems, recv_sems, srcs, lands, after, name, whole=False):
    n, n_peer = len(srcs), N_DEV - 1

    def body(*refs):
        src_refs, land_refs = refs[:n], refs[n:2 * n]
        send_s, recv_s = refs[2 * n], refs[2 * n + 1]
        _, peers = _flip_peers()
        for k, (dev, idx) in enumerate(peers):
            for a in range(n):
                cp = pltpu.make_async_remote_copy(
                    src_ref=src_refs[a] if whole else src_refs[a].at[idx], dst_ref=land_refs[a].at[idx],
                    send_sem=send_s.at[a * n_peer + k],
                    recv_sem=recv_s.at[a * n_peer + k], device_id=dev, device_id_type=pl.DeviceIdType.MESH)
                cp.wait_send()
                cp.wait_recv()

    hbm = pl.BlockSpec(memory_space=pltpu.HBM)
    sem = pl.BlockSpec(memory_space=pltpu.SEMAPHORE)
    arrs = list(srcs) + list(lands)
    res = pl.pallas_call(
        body, name=name, in_specs=[hbm] * (2 * n) + [sem, sem, pl.BlockSpec(memory_space=pl.ANY)],
        out_specs=tuple([hbm] * (2 * n)), out_shape=tuple(pltpu.HBM(a.shape, a.dtype) for a in arrs),
        input_output_aliases={i: i for i in range(2 * n)},
        compiler_params=pltpu.CompilerParams(has_side_effects=pltpu.SideEffectType.DATAFLOW_SIDE_EFFECTING),
    )(*arrs, send_sems, recv_sems, after)
    return list(res[:n]), list(res[n:])


def _ffn_fwd(x, h, mod, w_in, w_out_after, lng, lnb, rows, tag, nxt):
    bsz, seq, d = x.shape
    t = bsz * seq
    if h is None:
        h = _modulate(x, mod, rows[0], rows[1], f"modulate_{tag}")
    z, a = _ffn_in_swiglu(h.reshape(t, d), w_in, f"ffn_in_{tag}")
    f = _matmul_groupsum(a, w_out_after(a), out_dtype=F32, tm=512, name=f"ffn_out_{tag}").reshape(bsz, seq, d)
    y, h_next = _res_ln(x, f, mod, lng, lnb, rows[2], 0.5, f"res_ln_{tag}", nxt)
    return y, h_next, (x, h, z, a, f)


def _tied(mod, tie):
    return mod if tie is None else mod + tie


def _open_tail(tail):
    dh, x, mod, dx_res, sc_row = tail
    return dx_res, (dh, x, mod, sc_row)


def _ffn_bwd(dy, pre, saved, mod, w_in, w_out, lng, lnb, rows, tag, ready):
    x, h, z, a, f = saved
    bsz, seq, d = x.shape
    t = bsz * seq
    (dx_res, df, dgate, dlg, dlb), closed = _res_ln_bwd(dy, x, f, mod, lng, lnb, rows[2], 0.5,
                                                       f"res_ln_bwd_{tag}", pre)
    df2 = df.reshape(1, t, d)
    dw_out = _matmul(a, df2, mode="tn", group_out=True, out_dtype=BF16, tm=a.shape[2], tk=min(t, 2048),
                     name=f"ffn_out_dw_{tag}")
    tie_out = ready(f"{tag}_out", dw_out)
    dz = _ffn_out_dx_swiglu(df.reshape(t, d), w_out, z, f"ffn_out_dx_{tag}").reshape(N_DEV, t, -1)
    dw_in = _matmul(dz, h.reshape(1, t, d), mode="tn", group_out=True, out_dtype=BF16, tm=dz.shape[2],
                    tk=min(t, 2048), name=f"ffn_in_dw_{tag}")
    tie_in = ready(f"{tag}_in", dw_in)
    dh = _matmul_groupsum(dz, w_in, out_dtype=F32, tm=512, name=f"ffn_in_dx_{tag}").reshape(bsz, seq, d)
    tail = (dh, x, _tied(_tied(mod, tie_out), tie_in), dx_res, rows[1])
    return tail, closed, dgate, dw_in, dw_out, dlg, dlb


def _mixer_fwd(x, h, mod, wts, small, lng, lnb, layer, tabs):
    bsz, seq, d = x.shape
    t = bsz * seq
    proj = _matmul(h.reshape(1, t, d), wts["mix_in"][None], mode="nn", group_out=True, out_dtype=F32, tm=512, tk=d,
                   name="mix_in").reshape(bsz, seq, PACK_W)
    mo, states = _hgrn_fwd(proj, small["lb_logits8"], small["hgrn_norm_g"], layer, f"hgrn_fwd_l{layer}")
    q, kv = _mla_pre(proj, small["q_norm_g"], small["kv_norm_g"], wts["uq"], wts["ukv"], tabs, "mla_pre")
    mla_scale = float((B_NOPE + B_ROPE) ** -0.5)
    mo, lse_b = _attn_fwd_loop(q, 0, kv, 0, mo, 2, None, mla_scale, "mla_attn_fwd")
    fg = _fox_gate(proj, small["fox_b_f"], "fox_gate")
    gates = (fg, jnp.swapaxes(fg[:, :, 0:8], 1, 2))
    fox_scale = float(HEAD_DIM ** -0.5)
    mo, lse_c = _attn_fwd_loop(proj, P_CQ // LANES, proj, P_CKV // LANES, mo, 6, gates, fox_scale, "fox_attn_fwd")
    mo = _gmlp_fwd(proj, mo, small["gmlp_ln_g"], small["gmlp_ln_b"], small["gmlp_w_s"], small["gmlp_bst"],
                   "gmlp_fwd")
    mixed = _matmul(mo.reshape(1, t, MO_W), wts["mix_out"][None], mode="nn", group_out=True, out_dtype=F32,
                    tm=1024, tk=MO_W, name="mix_out").reshape(bsz, seq, d)
    y, h_next = _res_ln(x, mixed, mod, lng, lnb, 5, 1.0, "res_ln_mix", (mod, 6, 7))
    return y, h_next, (x, h, proj, mo, states, q, kv, lse_b, gates, lse_c, mixed)


def _mixer_bwd(dy, pre, saved, mod, wts, small, lng, lnb, layer, tabs, ready):
    x, h, proj, mo, states, q, kv, lse_b, gates, lse_c, mixed = saved
    bsz, seq, d = x.shape
    t = bsz * seq
    (dx_res, dmixed, dgate, dlg, dlb), closed = _res_ln_bwd(dy, x, mixed, mod, lng, lnb, 5, 1.0, "res_ln_bwd_mix",
                                                           pre)
    dm2 = dmixed.reshape(1, t, d)
    dmo = _matmul(dm2, wts["mix_out"][None], mode="nt", group_out=True, out_dtype=F32, tm=1024, tk=d,
                  name="mix_out_dx").reshape(bsz, seq, MO_W)
    dw_out = _matmul(mo.reshape(1, t, MO_W), dm2, mode="tn", group_out=True, out_dtype=F32, tm=512, tk=min(t, 2048),
                     name="mix_out_dw")[0]
    tie_out = ready("mix_out", dw_out)
    g = {}
    dproj, g["lb_logits8"], g["hgrn_norm_g"] = _hgrn_bwd(dmo, proj, states, small["lb_logits8"],
                                                         small["hgrn_norm_g"], layer, f"hgrn_bwd_l{layer}")
    mla_scale = float((B_NOPE + B_ROPE) ** -0.5)
    dq, delta_b, _ = _attn_bwd_q_loop(q, 0, kv, 0, mo, dmo, 2, lse_b, None, mla_scale,
                                 jax.ShapeDtypeStruct((bsz, seq, 512), F32), 0, "mla_attn_bwd_q")
    dkv, _ = _attn_bwd_kv_loop(q, 0, kv, 0, dmo, 2, lse_b, delta_b, None, mla_scale,
                          jax.ShapeDtypeStruct((bsz, seq, 1024), F32), 0, "mla_attn_bwd_kv")
    dproj, g["q_norm_g"], g["kv_norm_g"], g["uq"], g["ukv"] = _mla_pre_bwd(
        dq, dkv, dproj, proj, small["q_norm_g"], small["kv_norm_g"], wts["uq"], wts["ukv"], tabs, "mla_pre_bwd")
    ready("mla_uq", g.pop("uq"))
    ready("mla_ukv", g.pop("ukv"))
    fox_scale = float(HEAD_DIM ** -0.5)
    dproj, delta_c, dfq = _attn_bwd_q_loop(proj, P_CQ // LANES, proj, P_CKV // LANES, mo, dmo, 6, lse_c, gates,
                                      fox_scale, dproj, P_CQ // LANES, "fox_attn_bwd_q")
    dproj, dfk = _attn_bwd_kv_loop(proj, P_CQ // LANES, proj, P_CKV // LANES, dmo, 6, lse_c, delta_c, gates, fox_scale,
                              dproj, P_CKV // (2 * LANES), "fox_attn_bwd_kv")
    dfk_cols = jnp.pad(jnp.swapaxes(dfk[:, :, 0, :], 1, 2), ((0, 0), (0, 0), (0, LANES - N_HEADS)))
    dproj, g["fox_b_f"] = _fox_gate_bwd(dfq, dfk_cols, dproj, proj, small["fox_b_f"], "fox_gate_bwd")
    dproj, g["gmlp_ln_g"], g["gmlp_ln_b"], g["gmlp_w_s"], g["gmlp_bst"] = _gmlp_bwd(
        dmo, dproj, proj, small["gmlp_ln_g"], small["gmlp_ln_b"], small["gmlp_w_s"], small["gmlp_bst"], "gmlp_bwd")
    dp2 = dproj.reshape(1, t, PACK_W)
    dw_in = _matmul(h.reshape(1, t, d), dp2, mode="tn", group_out=True, out_dtype=BF16, tm=512, tk=1024,
                    name="mix_in_dw")[0]
    tie_in = ready("mix_in", dw_in)
    dh = _matmul(dp2, wts["mix_in"][None], mode="nt", group_out=True, out_dtype=F32, tm=512, tk=PACK_W,
                 name="mix_in_dx").reshape(bsz, seq, d)
    tail = (dh, x, _tied(_tied(mod, tie_out), tie_in), dx_res, 4)
    return tail, closed, dgate, dw_in, dw_out, g, dlg, dlb


def _small_views(p, layer):
    return {
        "lb_logits8": jnp.pad(p["hgrn_lb_logits"], ((0, 8 - DEPTH), (0, 0))),
        "hgrn_norm_g": p["hgrn_norm_g"][layer][None],
        "q_norm_g": p["mla_q_norm_g"][layer][None],
        "kv_norm_g": p["mla_kv_norm_g"][layer][None],
        "fox_b_f": jnp.pad(p["fox_b_f"][layer][None], ((0, 0), (0, LANES - N_HEADS))),
        "gmlp_ln_g": p["gmlp_ln_g"][layer][None],
        "gmlp_ln_b": p["gmlp_ln_b"][layer][None],
        "gmlp_w_s": p["gmlp_w_s"][layer],
        "gmlp_bst": jnp.pad(p["gmlp_b_s"][layer].T, ((0, 0), (0, LANES - N_HEADS))),
    }


def _local_step(x, mod, target, weights, p, grads_ready=None):
    bsz, seq, d = x.shape
    tabs = _rope_tables(seq)
    saved = []
    h = None
    for l in range(DEPTH):
        sm = _small_views(p, l)
        lng, lnb = p["ln_g"][l], p["ln_b"][l]
        x, h, s1 = _ffn_fwd(x, h, mod[l], weights(l, "ffn1_in", x)["ffn1_in"],
                            lambda a, l=l: weights(l, "ffn1_out", a)["ffn1_out"], lng[0:1], lnb[0:1], (0, 1, 2),
                            "ffn1", (mod[l], 3, 4))
        x, h, s2 = _mixer_fwd(x, h, mod[l], weights(l, "mix", x), sm, lng[1:2], lnb[1:2], l, tabs)
        x, h, s3 = _ffn_fwd(x, h, mod[l], weights(l, "ffn2_in", x)["ffn2_in"],
                            lambda a, l=l: weights(l, "ffn2_out", a)["ffn2_out"], lng[2:3], lnb[2:3], (6, 7, 8),
                            "ffn2", (mod[l + 1], 0, 1) if l + 1 < DEPTH else None)
        saved.append((s1, s2, s3))
    dx, loss = _loss_head(x, target, "loss_head")
    big, small, dmods = [None] * DEPTH, [None] * DEPTH, [None] * DEPTH
    ties = []
    tail, rows_of = None, {}

    def tied(a):
        for t in ties:
            a = a + t
        return a

    for l in reversed(range(DEPTH)):
        w = {}
        for part in ("ffn1_in", "ffn1_out", "mix", "ffn2_in", "ffn2_out"):
            w.update(weights(l, part, None))
        sm = _small_views(p, l)
        lng, lnb = p["ln_g"][l], p["ln_b"][l]
        s1, s2, s3 = saved[l]

        def ready(name, grad, l=l):
            tie = None if grads_ready is None else grads_ready(l, name, grad)
            if tie is not None:
                ties.append(tie)
            return tie

        dy, pre = (dx, None) if tail is None else _open_tail(tail)
        tail, closed, dgate3, dwi2, dwo2, dlg2, dlb2 = _ffn_bwd(dy, pre, s3, tied(mod[l]), w["ffn2_in"],
                                                                w["ffn2_out"], lng[2:3], lnb[2:3], (6, 7, 8), "ffn2",
                                                                ready)
        if closed is not None:
            rows_of[(l + 1, 0)], rows_of[(l + 1, 1)] = closed
        dy, pre = _open_tail(tail)
        tail, closed, dgate2, dwmi, dwmo, g, dlg1, dlb1 = _mixer_bwd(dy, pre, s2, tied(mod[l]), w, sm, lng[1:2],
                                                                     lnb[1:2], l, tabs, ready)
        rows_of[(l, 6)], rows_of[(l, 7)] = closed
        dy, pre = _open_tail(tail)
        tail, closed, dgate1, dwi1, dwo1, dlg0, dlb0 = _ffn_bwd(dy, pre, s1, tied(mod[l]), w["ffn1_in"],
                                                                w["ffn1_out"], lng[0:1], lnb[0:1], (0, 1, 2), "ffn1",
                                                                ready)
        rows_of[(l, 3)], rows_of[(l, 4)] = closed
        rows_of[(l, 2)], rows_of[(l, 5)], rows_of[(l, 8)] = dgate1, dgate2, dgate3
        big[l] = {"ffn1_in": dwi1, "ffn1_out": dwo1, "ffn2_in": dwi2, "ffn2_out": dwo2, "mix_in": dwmi,
                  "mix_out": dwmo}
        g["ln_g"] = jnp.concatenate([dlg0, dlg1, dlg2], axis=0)
        g["ln_b"] = jnp.concatenate([dlb0, dlb1, dlb2], axis=0)
        small[l] = g
    dh, x0, mod0, dx_res, sc_row = tail
    dx, rows_of[(0, 0)], rows_of[(0, 1)] = _modulate_bwd(dh, x0, mod0, dx_res, sc_row, "modulate_bwd_ffn1")
    dmods = [jnp.concatenate([rows_of[(l, r)] for r in range(N_MOD)], axis=1) for l in range(DEPTH)]
    return loss, dx, jnp.stack(dmods), big, small


_BIG = ("ffn1_in", "ffn1_out", "ffn2_in", "ffn2_out", "mix_in", "mix_out")


def _small_grad_list(small, loss):
    def both(fn):
        return jnp.stack([fn(small[l]) for l in range(DEPTH)])

    return [
        ("loss", loss.reshape(1)),
        ("ln_g", both(lambda g: g["ln_g"])), ("ln_b", both(lambda g: g["ln_b"])),
        ("hgrn_lb_logits", small[0]["lb_logits8"][:DEPTH] + small[1]["lb_logits8"][:DEPTH]),
        ("hgrn_norm_g", both(lambda g: g["hgrn_norm_g"][0])),
        ("mla_q_norm_g", both(lambda g: g["q_norm_g"][0])),
        ("mla_kv_norm_g", both(lambda g: g["kv_norm_g"][0])),
        ("fox_b_f", both(lambda g: g["fox_b_f"][0, :N_HEADS])),
        ("gmlp_ln_g", both(lambda g: g["gmlp_ln_g"][0])), ("gmlp_ln_b", both(lambda g: g["gmlp_ln_b"][0])),
        ("gmlp_w_s", both(lambda g: g["gmlp_w_s"])),
        ("gmlp_b_s", both(lambda g: g["gmlp_bst"][:, :N_HEADS].T)),
    ]


_PACK_COLS = 512


def _pack_small(items):
    flat = jnp.concatenate([a.reshape(-1).astype(F32) for _, a in items])
    n = flat.shape[0]
    tile = 8 * _PACK_COLS
    flat = jnp.pad(flat, (0, (-n) % tile))
    return flat.reshape(-1, _PACK_COLS)


def _unpack_small(buf, items):
    flat = buf.reshape(-1)
    out, off = {}, 0
    for name, a in items:
        out[name] = flat[off:off + a.size].reshape(a.shape)
        off += a.size
    return out


def _as2d(a):
    return a.reshape(-1, a.shape[-1])


def kernel(x, c, ada_w, ada_b, ln_g, ln_b, ffn1_w_in, ffn1_w_out, ffn2_w_in, ffn2_w_out, mix_w_in, mix_w_out, hgrn_lb_logits, hgrn_norm_g, mla_q_norm_g, mla_kv_norm_g, mla_w_uq, mla_w_ukv, fox_b_f, gmlp_ln_g, gmlp_ln_b, gmlp_w_s, gmlp_b_s, loss_target, m_ada_w, m_ada_b, m_ln_g, m_ln_b, m_ffn1_w_in, m_ffn1_w_out, m_ffn2_w_in, m_ffn2_w_out, m_mix_w_in, m_mix_w_out, m_hgrn_lb_logits, m_hgrn_norm_g, m_mla_q_norm_g, m_mla_kv_norm_g, m_mla_w_uq, m_mla_w_ukv, m_fox_b_f, m_gmlp_ln_g, m_gmlp_ln_b, m_gmlp_w_s, m_gmlp_b_s, v_ada_w, v_ada_b, v_ln_g, v_ln_b, v_ffn1_w_in, v_ffn1_w_out, v_ffn2_w_in, v_ffn2_w_out, v_mix_w_in, v_mix_w_out, v_hgrn_lb_logits, v_hgrn_norm_g, v_mla_q_norm_g, v_mla_kv_norm_g, v_mla_w_uq, v_mla_w_ukv, v_fox_b_f, v_gmlp_ln_g, v_gmlp_ln_b, v_gmlp_w_s, v_gmlp_b_s):
    names = ["ada_w", "ada_b", "ln_g", "ln_b", "ffn1_w_in", "ffn1_w_out", "ffn2_w_in", "ffn2_w_out", "mix_w_in",
             "mix_w_out", "hgrn_lb_logits", "hgrn_norm_g", "mla_q_norm_g", "mla_kv_norm_g", "mla_w_uq", "mla_w_ukv",
             "fox_b_f", "gmlp_ln_g", "gmlp_ln_b", "gmlp_w_s", "gmlp_b_s"]
    w = dict(zip(names, [ada_w, ada_b, ln_g, ln_b, ffn1_w_in, ffn1_w_out, ffn2_w_in, ffn2_w_out, mix_w_in, mix_w_out,
                         hgrn_lb_logits, hgrn_norm_g, mla_q_norm_g, mla_kv_norm_g, mla_w_uq, mla_w_ukv, fox_b_f,
                         gmlp_ln_g, gmlp_ln_b, gmlp_w_s, gmlp_b_s]))
    m = dict(zip(names, [m_ada_w, m_ada_b, m_ln_g, m_ln_b, m_ffn1_w_in, m_ffn1_w_out, m_ffn2_w_in, m_ffn2_w_out,
                         m_mix_w_in, m_mix_w_out, m_hgrn_lb_logits, m_hgrn_norm_g, m_mla_q_norm_g, m_mla_kv_norm_g,
                         m_mla_w_uq, m_mla_w_ukv, m_fox_b_f, m_gmlp_ln_g, m_gmlp_ln_b, m_gmlp_w_s, m_gmlp_b_s]))
    v = dict(zip(names, [v_ada_w, v_ada_b, v_ln_g, v_ln_b, v_ffn1_w_in, v_ffn1_w_out, v_ffn2_w_in, v_ffn2_w_out,
                         v_mix_w_in, v_mix_w_out, v_hgrn_lb_logits, v_hgrn_norm_g, v_mla_q_norm_g, v_mla_kv_norm_g,
                         v_mla_w_uq, v_mla_w_ukv, v_fox_b_f, v_gmlp_ln_g, v_gmlp_ln_b, v_gmlp_w_s, v_gmlp_b_s]))
    bsz, seq, d = x.shape
    me = 4 * lax.axis_index("x") + 2 * lax.axis_index("y") + lax.axis_index("c")
    mix_src, uq_src, ukv_src, mo_src = _mix_in_src(), _uq_src(), _ukv_src(), _mo_src()

    part_names = {"ffn1_in": ["ffn1_w_in"], "ffn1_out": ["ffn1_w_out"],
                  "mix": ["mix_w_in", "mix_w_out", "mla_w_uq", "mla_w_ukv"],
                  "ffn2_in": ["ffn2_w_in"], "ffn2_out": ["ffn2_w_out"]}
    group_of = {(l, part): (l, part) for l in range(DEPTH) for part in part_names}
    in_flight = {}
    transposed = ("ffn1_w_in", "ffn2_w_in")

    def start_group(key, behind=None):
        members = [(l, part) for (l, part), g in group_of.items() if g == key]
        labels = [(l, n) for l, part in members for n in part_names[part]]
        shards = []
        for l, n in labels:
            a = w[n][l]
            if n == "mix_w_in":
                a = _pack_cols(a, mix_src)
            if n in transposed:
                a = jnp.swapaxes(w[n], 1, 2)[l]
            shards.append(a.astype(BF16))
        if behind is not None:
            shards, _ = lax.optimization_barrier((shards, behind))
        in_flight[key] = (labels, _push_start(shards, f"gather_start_{key[0]}_{key[1]}", whole=True))

    keys_in_order = list(dict.fromkeys(group_of.values()))
    start_group(keys_in_order[0])

    gathered = _all_gather([c, ln_g, ln_b], "gather_inputs")
    c_all = gathered[0].reshape(N_DEV * bsz, d)
    ln_g_full = jnp.moveaxis(gathered[1], 0, 2).reshape(DEPTH, 3, d)
    ln_b_full = jnp.moveaxis(gathered[2], 0, 2).reshape(DEPTH, 3, d)

    mod_cols = _ada_fwd(c_all, ada_w, "ada_fwd")
    mod_all, = _all_gather([mod_cols], "gather_mod")
    mod_mine = lax.dynamic_slice_in_dim(mod_all, me * bsz, bsz, axis=2)
    mod = jnp.moveaxis(mod_mine, 0, 2).reshape(DEPTH, bsz, N_MOD * d) + ada_b[:, None, :]
    for key in keys_in_order[1:]:
        start_group(key, behind=mod)
    tie = sum(h[-1][0, 0] for _, h in in_flight.values())
    mod = mod.reshape(DEPTH, bsz, N_MOD, d) + tie

    arrived, laid_out = {}, {}

    def weights(l, part, after):
        if (l, part) not in laid_out:
            laid_out[(l, part)] = lay_out(l, part, after)
        return laid_out[(l, part)]

    def lay_out(l, part, after):
        key = group_of[(l, part)]
        if key not in arrived:
            labels, (send_sems, recv_sems, srcs, lands, _) = in_flight[key]
            _, lands = _push_wait(send_sems, recv_sems, srcs, lands, after, f"gather_wait_{key[0]}_{key[1]}",
                                  whole=True)
            arrived[key] = dict(zip(labels, lands))
        gw = {n: arrived[key][(l, n)] for n in part_names[part]}
        if part.endswith("_in"):
            return {part: gw[part_names[part][0]]}
        if part.endswith("_out"):
            return {part: gw[part_names[part][0]].reshape(4, 704, d)}
        uq = jnp.moveaxis(gw["mla_w_uq"], 0, 1).reshape(256, 384)
        ukv = jnp.moveaxis(gw["mla_w_ukv"], 0, 1).reshape(128, 512)
        return {"mix_in": gw["mix_w_in"].reshape(d, PACK_W),
                "mix_out": _pack_cols(gw["mix_w_out"].reshape(d, d).T, mo_src).T,
                "uq": _pack_cols(uq, uq_src), "ukv": _pack_cols(ukv, ukv_src)}

    p = dict(w)
    p["ln_g"], p["ln_b"] = ln_g_full, ln_b_full
    def chunks(name, arr):
        if name in ("ffn1_in", "ffn2_in"):
            return arr
        if name in ("ffn1_out", "ffn2_out"):
            return arr.reshape(N_DEV, arr.shape[1] // 2, d)
        if name == "mix_in":
            return _unpack_cols(arr, mix_src, MIX_ORIG_W).reshape(N_DEV, d // N_DEV, MIX_ORIG_W)
        if name in ("mla_uq", "mla_ukv"):
            full_w = _unpack_cols(arr, uq_src, 384) if name == "mla_uq" else _unpack_cols(arr, ukv_src, 512)
            rows = full_w.shape[0]
            return jnp.moveaxis(full_w.reshape(rows, N_DEV, -1), 1, 0).astype(BF16)
        return _unpack_cols(arr.T, mo_src, d).T.astype(BF16).reshape(N_DEV, d // N_DEV, d)

    pending, started = {}, []

    def grads_ready(l, name, grad):
        pending[(name, l)] = chunks(name, grad)
        flush = name == "ffn1_in" if l > 0 else name in ("ffn2_in", "mix_out", "mix_in", "ffn1_out", "ffn1_in")
        if not flush:
            return None
        keys = sorted(pending)
        handles = _push_start([pending[k] for k in keys], f"push_start_{len(started)}")
        pending.clear()
        started.append((keys, handles, l == 0 and name.startswith("ffn1")))
        return handles[-1][0, 0]

    loss, grad_x, dmod, big, small = _local_step(x, mod, loss_target, weights, p, grads_ready)
    del big

    recv, out = {}, {}

    def arrive(n, after):
        keys, (send_sems, recv_sems, srcs, lands, _), _ = started[n]
        srcs, lands = _push_wait(send_sems, recv_sems, srcs, lands, after, f"push_wait_{n}")
        for k, src, land in zip(keys, srcs, lands):
            recv[k] = (land, src)

    big_of = {"ffn1_w_in": "ffn1_in", "ffn1_w_out": "ffn1_out", "ffn2_w_in": "ffn2_in", "ffn2_w_out": "ffn2_out",
              "mix_w_in": "mix_in", "mix_w_out": "mix_out", "mla_w_uq": "mla_uq", "mla_w_ukv": "mla_ukv"}
    chain = {name: None for name in big_of}

    def big_update(key, l):
        name = next(nm for nm, k in big_of.items() if k == key)
        parts, src = recv[(key, l)]
        view =(lambda a: jnp.swapaxes(a, 1, 2)) if name in transposed else (lambda a: a)
        chain[name] = _adamw(parts, (src, me), view(w[name]), view(m[name]), view(v[name]), f"adamw_{name}_l{l}",
                             layer=l, prev=chain[name])

    def update(name, grad):
        shape = w[name].shape
        as3 = lambda a: a.reshape(1, -1, shape[-1])
        res = _adamw(as3(grad), None, as3(w[name]), as3(m[name]), as3(v[name]), f"adamw_{name}")
        out[name] = tuple(r.reshape(shape) for r in res)

    for n, (keys, _, last) in enumerate(started):
        if not last:
            arrive(n, grad_x)
            for key, l in keys:
                big_update(key, l)

    dmod_flat = dmod.reshape(DEPTH, bsz, N_MOD * d)
    done = [r[0] for r in chain.values() if r is not None]
    if done:
        dmod_flat, _ = lax.optimization_barrier((dmod_flat, done))
    dmod_all, = _all_gather([dmod_flat], "gather_dmod")
    dmod_full = jnp.moveaxis(dmod_all, 0, 1).reshape(DEPTH, N_DEV * bsz, N_MOD * d)
    cols = ada_w.shape[2]
    dmod_cols = lax.dynamic_slice_in_dim(dmod_full, me * cols, cols, axis=2)
    g_ada_w, g_ada_b = _ada_bwd(c_all, dmod_cols, dmod_full, "ada_bwd")
    res = None
    for l in range(DEPTH):
        res = _adamw(g_ada_w[l][None], None, ada_w, m_ada_w, v_ada_w, f"adamw_ada_w_l{l}", layer=l, prev=res)
    out["ada_w"] = tuple(res)
    update("ada_b", g_ada_b.reshape(DEPTH, N_MOD * d))

    items = _small_grad_list(small, loss)
    packed, _ = lax.optimization_barrier((_pack_small(items), (grad_x, g_ada_b)))
    parts, = _all_gather([packed], "gather_small")
    sg = _unpack_small(_sum_parts(parts, "sum_small"), items)
    for name in ("ln_g", "ln_b"):
        update(name, lax.dynamic_slice_in_dim(sg[name], me * (d // N_DEV), d // N_DEV, axis=2))
    for name in ("hgrn_lb_logits", "hgrn_norm_g", "mla_q_norm_g", "mla_kv_norm_g", "fox_b_f", "gmlp_ln_g",
                 "gmlp_ln_b", "gmlp_w_s", "gmlp_b_s"):
        update(name, sg[name])

    for n, (keys, _, last) in enumerate(started):
        if last:
            arrive(n, out["gmlp_w_s"][0])
            for key, l in keys:
                big_update(key, l)
    for name in big_of:
        out[name] = tuple(jnp.swapaxes(r, 1, 2) if name in transposed else r for r in chain[name])

    return (sg["loss"][0], grad_x, *[out[n][0] for n in names], *[out[n][1] for n in names],
            *[out[n][2] for n in names], *[out[n][3] for n in names])
```
